```python
import jax, jax.numpy as jnp
from jax import lax
import numpy as np

D_MODEL = 1024
BATCH = 8
SEQ = 8192
DEPTH = 1

HG_HEADS = 8
HG_KEY_DIM = 128
HG_VAL_DIM = D_MODEL // HG_HEADS
HG_CHUNK = 64
FOX_HEADS = 16
FOX_HEAD_DIM = 64
FOX_BLOCK = 128
FOX_F_BIAS_INIT = 2.0
D_FF = 2816
CONV_WIDTH = 3
EPS = 1e-6

HG_QK = HG_HEADS * HG_KEY_DIM
HG_V = HG_HEADS * HG_VAL_DIM
FOX_W = FOX_HEADS * FOX_HEAD_DIM
SPLITS = (HG_QK, HG_QK, HG_V, HG_V, FOX_W, FOX_W, FOX_W, FOX_HEADS, D_MODEL, D_MODEL)
D_IN = sum(SPLITS)

kernel_name = 'hgrn2_fox_gated_hybrid_block'


def rms_norm(x, g):
    xf = x.astype(jnp.float32)
    y = xf * lax.rsqrt(jnp.mean(xf * xf, axis=-1, keepdims=True) + EPS)
    return (y * g.astype(jnp.float32)).astype(x.dtype)


def hgrn2_mixer(q, f_logit, i, g, lb, g_norm):
    B, S, _ = q.shape
    n_chunks = S // HG_CHUNK
    f32 = jnp.float32
    q = jax.nn.silu(q.astype(f32))
    f = lb + (1.0 - lb) * jax.nn.sigmoid(f_logit.astype(f32))
    k = 1.0 - f
    log_f = jnp.log(f)

    def chunks(t, d):
        return t.astype(f32).reshape(B, n_chunks, HG_CHUNK, HG_HEADS, d).transpose(1, 0, 3, 2, 4)

    xs = (chunks(q, HG_KEY_DIM), chunks(k, HG_KEY_DIM), chunks(log_f, HG_KEY_DIM), chunks(i, HG_VAL_DIM))
    causal = jnp.tril(jnp.ones((HG_CHUNK, HG_CHUNK), dtype=bool))[:, :, None]

    def step(state, inp):
        qc, kc, gc, vc = inp
        b = jnp.cumsum(gc, axis=2)
        o_inter = jnp.einsum('bhtk,bhkv->bhtv', qc * jnp.exp(b), state)
        rel = b[:, :, :, None, :] - b[:, :, None, :, :]
        decay = jnp.exp(jnp.where(causal, rel, -jnp.inf))
        scores = jnp.einsum('bhtsk,bhsk->bhts', qc[:, :, :, None, :] * decay, kc)
        o_intra = jnp.einsum('bhts,bhsv->bhtv', scores, vc)
        b_end = b[:, :, -1:, :]
        state = jnp.exp(b_end[:, :, 0, :])[..., None] * state + jnp.einsum('bhsk,bhsv->bhkv', kc * jnp.exp(b_end - b), vc)
        return state, o_inter + o_intra

    state0 = jnp.zeros((B, HG_HEADS, HG_KEY_DIM, HG_VAL_DIM), f32)
    _, o = lax.scan(step, state0, xs)
    o = o.transpose(1, 0, 3, 2, 4).reshape(B, S, HG_HEADS, HG_VAL_DIM)
    o = rms_norm(o, g_norm) * jax.nn.silu(g.astype(f32)).reshape(B, S, HG_HEADS, HG_VAL_DIM)
    return o.reshape(B, S, HG_V).astype(i.dtype)


def fox_mixer(q, k, v, f_logit, f_bias):
    B, S, _ = q.shape
    n_blocks = S // FOX_BLOCK
    f32 = jnp.float32
    q = q.reshape(B, S, FOX_HEADS, FOX_HEAD_DIM) * FOX_HEAD_DIM ** -0.5
    k = k.reshape(B, S, FOX_HEADS, FOX_HEAD_DIM)
    v = v.reshape(B, S, FOX_HEADS, FOX_HEAD_DIM)
    log_f = jax.nn.log_sigmoid(f_logit.astype(f32) + f_bias.astype(f32))
    c = jnp.cumsum(log_f, axis=1).transpose(0, 2, 1)
    q_blocks = q.reshape(B, n_blocks, FOX_BLOCK, FOX_HEADS, FOX_HEAD_DIM).transpose(1, 0, 2, 3, 4)
    c_blocks = c.reshape(B, FOX_HEADS, n_blocks, FOX_BLOCK).transpose(2, 0, 1, 3)
    key_pos = jnp.arange(S)

    def attend(args):
        blk, qb, cb = args
        logits = jnp.einsum('bqhd,bshd->bhqs', qb, k).astype(f32)
        logits = logits + (cb[..., None] - c[:, :, None, :])
        q_pos = blk * FOX_BLOCK + jnp.arange(FOX_BLOCK)
        mask = key_pos[None, :] <= q_pos[:, None]
        p = jax.nn.softmax(jnp.where(mask, logits, -jnp.inf), axis=-1)
        return jnp.einsum('bhqs,bshd->bqhd', p.astype(v.dtype), v)

    o = lax.map(attend, (jnp.arange(n_blocks), q_blocks, c_blocks))
    return o.transpose(1, 0, 2, 3, 4).reshape(B, S, FOX_W)


def conv_glu_ffn(x, w_up, conv_w, conv_b, w_down):
    S = x.shape[1]
    u = x @ w_up
    u_pad = jnp.pad(u, ((0, 0), (CONV_WIDTH - 1, 0), (0, 0)))
    acc = conv_b
    for j in range(CONV_WIDTH):
        acc = acc + conv_w[j] * u_pad[:, j:j + S]
    gate, val = jnp.split(acc, 2, axis=-1)
    return (jax.nn.gelu(gate, approximate=False) * val) @ w_down


def _fwd_setup_inputs(seed: int = 0) -> dict:
    key = jax.random.key(seed)
    ks = jax.random.split(key, 16)
    f32 = jnp.float32

    def dense(k, shape, fan_in):
        return jax.random.normal(k, shape, f32) * fan_in ** -0.5

    def gain(k, shape):
        return 1.0 + 0.02 * jax.random.normal(k, shape, f32)

    return {
        'x': jax.random.normal(ks[0], (BATCH, SEQ, D_MODEL), f32),
        'norm_mix': gain(ks[1], (DEPTH, D_MODEL)),
        'w_in': dense(ks[2], (DEPTH, D_MODEL, D_IN), D_MODEL),
        'fox_f_bias': FOX_F_BIAS_INIT + 0.5 * jax.random.normal(ks[3], (DEPTH, FOX_HEADS), f32),
        'hg_lb_logits': 0.1 * jax.random.normal(ks[4], (DEPTH + 1, HG_QK), f32),
        'hg_norm': gain(ks[5], (DEPTH, HG_VAL_DIM)),
        'w_branch_a': dense(ks[6], (DEPTH, HG_V, D_MODEL), HG_V),
        'w_branch_b': dense(ks[7], (DEPTH, FOX_W, D_MODEL), FOX_W),
        'w_out': dense(ks[8], (DEPTH, D_MODEL, D_MODEL), D_MODEL),
        'norm_ffn': gain(ks[9], (DEPTH, D_MODEL)),
        'w_up': dense(ks[10], (DEPTH, D_MODEL, 2 * D_FF), D_MODEL),
        'conv_w': dense(ks[11], (DEPTH, CONV_WIDTH, 2 * D_FF), CONV_WIDTH),
        'conv_b': 0.02 * jax.random.normal(ks[12], (DEPTH, 2 * D_FF), f32),
        'w_down': dense(ks[13], (DEPTH, D_FF, D_MODEL), D_FF),
        'norm_final': gain(ks[14], (D_MODEL,)),
    }


def _fwd_reference(x, norm_mix, w_in, fox_f_bias, hg_lb_logits, hg_norm, w_branch_a, w_branch_b, w_out, norm_ffn, w_up, conv_w, conv_b, w_down, norm_final):
    h = x
    lb_all = jnp.cumsum(jax.nn.softmax(hg_lb_logits.astype(jnp.float32), axis=0), axis=0)
    cuts = np.cumsum(SPLITS)[:-1].tolist()
    for layer in range(DEPTH):
        n = rms_norm(h, norm_mix[layer])
        proj = n @ w_in[layer]
        hq, hf, hi, hg, fq, fk, fv, ff, ga, gb = jnp.split(proj, cuts, axis=-1)
        o_a = hgrn2_mixer(hq, hf, hi, hg, lb_all[layer], hg_norm[layer])
        o_b = fox_mixer(fq, fk, fv, ff, fox_f_bias[layer])
        merged = jax.nn.sigmoid(ga) * (o_a @ w_branch_a[layer]) + jax.nn.sigmoid(gb) * (o_b @ w_branch_b[layer])
        h = h + merged @ w_out[layer]
        h = h + conv_glu_ffn(rms_norm(h, norm_ffn[layer]), w_up[layer], conv_w[layer], conv_b[layer], w_down[layer])
    return rms_norm(h, norm_final)


import jax as _jax
import jax.numpy as _jnp

TWIN_FORMAT = 'train_step'
FWD_PARAMS = ['x', 'norm_mix', 'w_in', 'fox_f_bias', 'hg_lb_logits', 'hg_norm', 'w_branch_a', 'w_branch_b', 'w_out', 'norm_ffn', 'w_up', 'conv_w', 'conv_b', 'w_down', 'norm_final']
TWIN_WEIGHTS = ['norm_mix', 'w_in', 'fox_f_bias', 'hg_lb_logits', 'hg_norm', 'w_branch_a', 'w_branch_b', 'w_out', 'norm_ffn', 'w_up', 'conv_w', 'conv_b', 'w_down', 'norm_final']
TWIN_DIFF_INPUT = 'x'
TWIN_INPUTS = ['x', 'norm_mix', 'w_in', 'fox_f_bias', 'hg_lb_logits', 'hg_norm', 'w_branch_a', 'w_branch_b', 'w_out', 'norm_ffn', 'w_up', 'conv_w', 'conv_b', 'w_down', 'norm_final', 'loss_target', 'm_norm_mix', 'm_w_in', 'm_fox_f_bias', 'm_hg_lb_logits', 'm_hg_norm', 'm_w_branch_a', 'm_w_branch_b', 'm_w_out', 'm_norm_ffn', 'm_w_up', 'm_conv_w', 'm_conv_b', 'm_w_down', 'm_norm_final', 'v_norm_mix', 'v_w_in', 'v_fox_f_bias', 'v_hg_lb_logits', 'v_hg_norm', 'v_w_branch_a', 'v_w_branch_b', 'v_w_out', 'v_norm_ffn', 'v_w_up', 'v_conv_w', 'v_conv_b', 'v_w_down', 'v_norm_final']
TWIN_OUTPUTS = ['loss', 'grad_x', 'grad_norm_mix', 'grad_w_in', 'grad_fox_f_bias', 'grad_hg_lb_logits', 'grad_hg_norm', 'grad_w_branch_a', 'grad_w_branch_b', 'grad_w_out', 'grad_norm_ffn', 'grad_w_up', 'grad_conv_w', 'grad_conv_b', 'grad_w_down', 'grad_norm_final', 'delta_norm_mix', 'delta_w_in', 'delta_fox_f_bias', 'delta_hg_lb_logits', 'delta_hg_norm', 'delta_w_branch_a', 'delta_w_branch_b', 'delta_w_out', 'delta_norm_ffn', 'delta_w_up', 'delta_conv_w', 'delta_conv_b', 'delta_w_down', 'delta_norm_final', 'new_m_norm_mix', 'new_m_w_in', 'new_m_fox_f_bias', 'new_m_hg_lb_logits', 'new_m_hg_norm', 'new_m_w_branch_a', 'new_m_w_branch_b', 'new_m_w_out', 'new_m_norm_ffn', 'new_m_w_up', 'new_m_conv_w', 'new_m_conv_b', 'new_m_w_down', 'new_m_norm_final', 'new_v_norm_mix', 'new_v_w_in', 'new_v_fox_f_bias', 'new_v_hg_lb_logits', 'new_v_hg_norm', 'new_v_w_branch_a', 'new_v_w_branch_b', 'new_v_w_out', 'new_v_norm_ffn', 'new_v_w_up', 'new_v_conv_w', 'new_v_conv_b', 'new_v_w_down', 'new_v_norm_final']
TWIN_LEAF_KINDS = {'loss': 'loss', 'grad_x': 'grad_x', 'grad_norm_mix': 'grad_w', 'grad_w_in': 'grad_w', 'grad_fox_f_bias': 'grad_w', 'grad_hg_lb_logits': 'grad_w', 'grad_hg_norm': 'grad_w', 'grad_w_branch_a': 'grad_w', 'grad_w_branch_b': 'grad_w', 'grad_w_out': 'grad_w', 'grad_norm_ffn': 'grad_w', 'grad_w_up': 'grad_w', 'grad_conv_w': 'grad_w', 'grad_conv_b': 'grad_w', 'grad_w_down': 'grad_w', 'grad_norm_final': 'grad_w', 'delta_norm_mix': 'delta_w', 'delta_w_in': 'delta_w', 'delta_fox_f_bias': 'delta_w', 'delta_hg_lb_logits': 'delta_w', 'delta_hg_norm': 'delta_w', 'delta_w_branch_a': 'delta_w', 'delta_w_branch_b': 'delta_w', 'delta_w_out': 'delta_w', 'delta_norm_ffn': 'delta_w', 'delta_w_up': 'delta_w', 'delta_conv_w': 'delta_w', 'delta_conv_b': 'delta_w', 'delta_w_down': 'delta_w', 'delta_norm_final': 'delta_w', 'new_m_norm_mix': 'new_m', 'new_m_w_in': 'new_m', 'new_m_fox_f_bias': 'new_m', 'new_m_hg_lb_logits': 'new_m', 'new_m_hg_norm': 'new_m', 'new_m_w_branch_a': 'new_m', 'new_m_w_branch_b': 'new_m', 'new_m_w_out': 'new_m', 'new_m_norm_ffn': 'new_m', 'new_m_w_up': 'new_m', 'new_m_conv_w': 'new_m', 'new_m_conv_b': 'new_m', 'new_m_w_down': 'new_m', 'new_m_norm_final': 'new_m', 'new_v_norm_mix': 'new_v', 'new_v_w_in': 'new_v', 'new_v_fox_f_bias': 'new_v', 'new_v_hg_lb_logits': 'new_v', 'new_v_hg_norm': 'new_v', 'new_v_w_branch_a': 'new_v', 'new_v_w_branch_b': 'new_v', 'new_v_w_out': 'new_v', 'new_v_norm_ffn': 'new_v', 'new_v_w_up': 'new_v', 'new_v_conv_w': 'new_v', 'new_v_conv_b': 'new_v', 'new_v_w_down': 'new_v', 'new_v_norm_final': 'new_v'}


def _forward(args):
    return _fwd_reference(*[args[k] for k in FWD_PARAMS])


def _output_shape():
    def fwd():
        inp = _fwd_setup_inputs(0)
        return _fwd_reference(*[inp[k] for k in FWD_PARAMS])
    out = _jax.eval_shape(fwd)
    return out.shape, out.dtype

N_MICROBATCH = 1
ADAM_LR = 0.001
ADAM_B1 = 0.9
ADAM_B2 = 0.999
ADAM_EPS = 1e-08
ADAM_WD = 0.01
ADAM_STEP = 10
PER_EXAMPLE_BATCH_AXIS = {'x': 0, 'loss_target': 0}
SHARED_INPUTS = []
_WEIGHT_DTYPES = {'norm_mix': _jnp.float32, 'w_in': _jnp.float32, 'fox_f_bias': _jnp.float32, 'hg_lb_logits': _jnp.float32, 'hg_norm': _jnp.float32, 'w_branch_a': _jnp.float32, 'w_branch_b': _jnp.float32, 'w_out': _jnp.float32, 'norm_ffn': _jnp.float32, 'w_up': _jnp.float32, 'conv_w': _jnp.float32, 'conv_b': _jnp.float32, 'w_down': _jnp.float32, 'norm_final': _jnp.float32}
MOMENT_SCALE = {'norm_mix': 1.580357e-01, 'w_in': 5.258714e-02, 'fox_f_bias': 2.008604e-01, 'hg_lb_logits': 8.221180e-03, 'hg_norm': 2.381864e-01, 'w_branch_a': 8.669063e-02, 'w_branch_b': 6.021212e-02, 'w_out': 1.049395e-01, 'norm_ffn': 1.951928e-01, 'w_up': 8.218196e-02, 'conv_w': 8.160251e-02, 'conv_b': 8.044154e-02, 'w_down': 1.344729e-01, 'norm_final': 6.398498e+01}


def _to_microbatches(a, axis):
    t = _jnp.moveaxis(a, axis, 0)
    t = t.reshape((N_MICROBATCH, t.shape[0] // N_MICROBATCH) + t.shape[1:])
    return _jnp.moveaxis(t, 1, axis + 1)


def setup_inputs(seed: int = 0) -> dict:
    inp = _fwd_setup_inputs(seed)
    key = _jax.random.fold_in(_jax.random.key(seed), 7919)
    shape, _ = _output_shape()
    out = dict(inp)
    out["loss_target"] = _jax.random.normal(_jax.random.fold_in(key, 0), shape, _jnp.float32)
    for i, name in enumerate(TWIN_WEIGHTS):
        w = inp[name].astype(_jnp.float32)
        if MOMENT_SCALE is None:
            s = _jnp.sqrt(_jnp.mean(_jnp.square(w)) + 1e-30)
        else:
            s = MOMENT_SCALE[name]
        km, kv = _jax.random.split(_jax.random.fold_in(key, i + 1))
        out[name] = w
        out["m_" + name] = s * _jax.random.normal(km, w.shape, _jnp.float32)
        out["v_" + name] = (s * s) * _jax.random.uniform(kv, w.shape, _jnp.float32, 0.5, 1.5)
    if N_MICROBATCH > 1:
        for name, axis in PER_EXAMPLE_BATCH_AXIS.items():
            out[name] = _to_microbatches(out[name], axis)
    return {'x': out['x'], 'norm_mix': out['norm_mix'], 'w_in': out['w_in'], 'fox_f_bias': out['fox_f_bias'], 'hg_lb_logits': out['hg_lb_logits'], 'hg_norm': out['hg_norm'], 'w_branch_a': out['w_branch_a'], 'w_branch_b': out['w_branch_b'], 'w_out': out['w_out'], 'norm_ffn': out['norm_ffn'], 'w_up': out['w_up'], 'conv_w': out['conv_w'], 'conv_b': out['conv_b'], 'w_down': out['w_down'], 'norm_final': out['norm_final'], 'loss_target': out['loss_target'], 'm_norm_mix': out['m_norm_mix'], 'm_w_in': out['m_w_in'], 'm_fox_f_bias': out['m_fox_f_bias'], 'm_hg_lb_logits': out['m_hg_lb_logits'], 'm_hg_norm': out['m_hg_norm'], 'm_w_branch_a': out['m_w_branch_a'], 'm_w_branch_b': out['m_w_branch_b'], 'm_w_out': out['m_w_out'], 'm_norm_ffn': out['m_norm_ffn'], 'm_w_up': out['m_w_up'], 'm_conv_w': out['m_conv_w'], 'm_conv_b': out['m_conv_b'], 'm_w_down': out['m_w_down'], 'm_norm_final': out['m_norm_final'], 'v_norm_mix': out['v_norm_mix'], 'v_w_in': out['v_w_in'], 'v_fox_f_bias': out['v_fox_f_bias'], 'v_hg_lb_logits': out['v_hg_lb_logits'], 'v_hg_norm': out['v_hg_norm'], 'v_w_branch_a': out['v_w_branch_a'], 'v_w_branch_b': out['v_w_branch_b'], 'v_w_out': out['v_w_out'], 'v_norm_ffn': out['v_norm_ffn'], 'v_w_up': out['v_w_up'], 'v_conv_w': out['v_conv_w'], 'v_conv_b': out['v_conv_b'], 'v_w_down': out['v_w_down'], 'v_norm_final': out['v_norm_final']}


def _loss(weights, diff, rest, loss_target):
    with _jax.named_scope("forward"):
        args = {**rest, TWIN_DIFF_INPUT: diff, **{k: w.astype(_WEIGHT_DTYPES[k]) for k, w in weights.items()}}
        y = _forward(args)
    with _jax.named_scope("loss_head"):
        err = _jnp.square(y.astype(_jnp.float32) - loss_target)
        return 0.5 * _jnp.sum(_jnp.mean(err, axis=-1)) if err.ndim else 0.5 * err


def _adamw(w, g, m, v):
    m = ADAM_B1 * m + (1.0 - ADAM_B1) * g
    v = ADAM_B2 * v + (1.0 - ADAM_B2) * _jnp.square(g)
    m_hat = m / (1.0 - ADAM_B1 ** ADAM_STEP)
    v_hat = v / (1.0 - ADAM_B2 ** ADAM_STEP)
    delta = -ADAM_LR * (m_hat / (_jnp.sqrt(v_hat) + ADAM_EPS) + ADAM_WD * w)
    return delta, m, v


def reference(x, norm_mix, w_in, fox_f_bias, hg_lb_logits, hg_norm, w_branch_a, w_branch_b, w_out, norm_ffn, w_up, conv_w, conv_b, w_down, norm_final, loss_target, m_norm_mix, m_w_in, m_fox_f_bias, m_hg_lb_logits, m_hg_norm, m_w_branch_a, m_w_branch_b, m_w_out, m_norm_ffn, m_w_up, m_conv_w, m_conv_b, m_w_down, m_norm_final, v_norm_mix, v_w_in, v_fox_f_bias, v_hg_lb_logits, v_hg_norm, v_w_branch_a, v_w_branch_b, v_w_out, v_norm_ffn, v_w_up, v_conv_w, v_conv_b, v_w_down, v_norm_final):
    given = dict(x=x, norm_mix=norm_mix, w_in=w_in, fox_f_bias=fox_f_bias, hg_lb_logits=hg_lb_logits, hg_norm=hg_norm, w_branch_a=w_branch_a, w_branch_b=w_branch_b, w_out=w_out, norm_ffn=norm_ffn, w_up=w_up, conv_w=conv_w, conv_b=conv_b, w_down=w_down, norm_final=norm_final, loss_target=loss_target, m_norm_mix=m_norm_mix, m_w_in=m_w_in, m_fox_f_bias=m_fox_f_bias, m_hg_lb_logits=m_hg_lb_logits, m_hg_norm=m_hg_norm, m_w_branch_a=m_w_branch_a, m_w_branch_b=m_w_branch_b, m_w_out=m_w_out, m_norm_ffn=m_norm_ffn, m_w_up=m_w_up, m_conv_w=m_conv_w, m_conv_b=m_conv_b, m_w_down=m_w_down, m_norm_final=m_norm_final, v_norm_mix=v_norm_mix, v_w_in=v_w_in, v_fox_f_bias=v_fox_f_bias, v_hg_lb_logits=v_hg_lb_logits, v_hg_norm=v_hg_norm, v_w_branch_a=v_w_branch_a, v_w_branch_b=v_w_branch_b, v_w_out=v_w_out, v_norm_ffn=v_norm_ffn, v_w_up=v_w_up, v_conv_w=v_conv_w, v_conv_b=v_conv_b, v_w_down=v_w_down, v_norm_final=v_norm_final)
    weights = {n: given[n] for n in TWIN_WEIGHTS}
    shared = {n: given[n] for n in SHARED_INPUTS}
    per_example = {n: given[n] for n in ['x']}
    grad_fn = _jax.value_and_grad(_loss, argnums=(0, 1))

    def one_microbatch(ex, loss_target):
        ex = dict(ex)
        diff = ex.pop(TWIN_DIFF_INPUT)
        return grad_fn(weights, diff, {**shared, **ex}, loss_target)

    if N_MICROBATCH == 1:
        loss, (grad_w, grad_x) = one_microbatch(per_example, given["loss_target"])
    else:
        def body(carry, xs):
            loss_sum, grad_sum = carry
            l_k, (gw_k, gx_k) = one_microbatch(xs[0], xs[1])
            with _jax.named_scope("update"):
                return (loss_sum + l_k, _jax.tree.map(_jnp.add, grad_sum, gw_k)), gx_k

        init = (_jnp.zeros((), _jnp.float32), _jax.tree.map(_jnp.zeros_like, weights))
        (loss, grad_w), grad_x = _jax.lax.scan(body, init, (per_example, given["loss_target"]))
    with _jax.named_scope("update"):
        delta_w, new_m, new_v = {}, {}, {}
        for n in TWIN_WEIGHTS:
            delta_w[n], new_m[n], new_v[n] = _adamw(weights[n], grad_w[n], given["m_" + n], given["v_" + n])
    return (loss, grad_x, *[grad_w[n] for n in TWIN_WEIGHTS], *[delta_w[n] for n in TWIN_WEIGHTS],
            *[new_m[n] for n in TWIN_WEIGHTS], *[new_v[n] for n in TWIN_WEIGHTS])
```

```python
import functools

import jax
import jax.numpy as jnp
from jax import lax
from jax.experimental import pallas as pl
from jax.experimental.pallas import tpu as pltpu

F32 = jnp.float32
BF16 = jnp.bfloat16

D_MODEL = 1024
HG_HEADS = 8
HG_DK = 128
HG_DV = 128
HG_CHUNK = 64
FOX_HEADS = 16
FOX_DH = 64
D_FF = 2816
EPS = 1e-6
N_DEV = 8

ADAM_LR = 0.001
ADAM_B1 = 0.9
ADAM_B2 = 0.999
ADAM_EPS = 1e-08
ADAM_WD = 0.01
ADAM_STEP = 10

VMEM_LIMIT = 56 * 1024 * 1024


def _cparams(sem):
    return pltpu.CompilerParams(dimension_semantics=sem, vmem_limit_bytes=VMEM_LIMIT)


_DIMS = {
    "nn": (((1,), (0,)), ((), ())),
    "nt": (((1,), (1,)), ((), ())),
    "tn": (((0,), (0,)), ((), ())),
}


def _pick(n, prefs):
    for p in prefs:
        if n % p == 0:
            return p
    return n


def _matmul(a, b, form, *, out_dtype=F32, addend=None, tm=None, tn=None, tk=None, name):
    if form == "nn":
        (M, K), (K2, N) = a.shape, b.shape
    elif form == "nt":
        (M, K), (N, K2) = a.shape, b.shape
    else:
        (K, M), (K2, N) = a.shape, b.shape
    assert K == K2, (a.shape, b.shape, form)
    tm = tm or _pick(M, (512, 256, 128))
    tn = tn or _pick(N, (512, 256, 128))
    tk = tk or (K if K <= 2816 else _pick(K, (1024, 512, 256, 128)))
    assert M % tm == 0 and N % tn == 0 and K % tk == 0, (M, N, K, tm, tn, tk)
    nk = K // tk
    dims = _DIMS[form]

    def body(*refs):
        if addend is None:
            a_ref, b_ref, o_ref, acc_ref = refs
            add_ref = None
        else:
            a_ref, b_ref, add_ref, o_ref, acc_ref = refs
        k = pl.program_id(2)

        @pl.when(k == 0)
        def _():
            acc_ref[...] = jnp.zeros_like(acc_ref)

        acc_ref[...] += lax.dot_general(a_ref[...].astype(BF16), b_ref[...].astype(BF16), dims,
                                        preferred_element_type=F32)

        @pl.when(k == nk - 1)
        def _():
            r = acc_ref[...]
            if add_ref is not None:
                r = r + add_ref[...].astype(F32)
            o_ref[...] = r.astype(o_ref.dtype)

    if form == "nn":
        a_spec = pl.BlockSpec((tm, tk), lambda i, j, k: (i, k))
        b_spec = pl.BlockSpec((tk, tn), lambda i, j, k: (k, j))
    elif form == "nt":
        a_spec = pl.BlockSpec((tm, tk), lambda i, j, k: (i, k))
        b_spec = pl.BlockSpec((tn, tk), lambda i, j, k: (j, k))
    else:
        a_spec = pl.BlockSpec((tk, tm), lambda i, j, k: (k, i))
        b_spec = pl.BlockSpec((tk, tn), lambda i, j, k: (k, j))
    o_spec = pl.BlockSpec((tm, tn), lambda i, j, k: (i, j))
    in_specs = [a_spec, b_spec] + ([o_spec] if addend is not None else [])
    args = (a, b) + ((addend,) if addend is not None else ())
    return pl.pallas_call(
        body, name=name, grid=(M // tm, N // tn, nk),
        in_specs=in_specs, out_specs=o_spec,
        out_shape=jax.ShapeDtypeStruct((M, N), out_dtype),
        scratch_shapes=[pltpu.VMEM((tm, tn), F32)],
        compiler_params=_cparams(("parallel", "parallel", "arbitrary")),
    )(*args)


def _rms_fwd(x, g, *, name, tm=512):
    M, D = x.shape
    tm = min(tm, M)

    def body(x_ref, g_ref, n_ref):
        xf = x_ref[...]
        r = lax.rsqrt(jnp.mean(xf * xf, axis=-1, keepdims=True) + EPS)
        n_ref[...] = (xf * r * g_ref[...]).astype(n_ref.dtype)

    return pl.pallas_call(
        body, name=name, grid=(M // tm,),
        in_specs=[pl.BlockSpec((tm, D), lambda i: (i, 0)), pl.BlockSpec((1, D), lambda i: (0, 0))],
        out_specs=pl.BlockSpec((tm, D), lambda i: (i, 0)),
        out_shape=jax.ShapeDtypeStruct((M, D), BF16),
        compiler_params=_cparams(("parallel",)),
    )(x, g.reshape(1, D))


def _rms_bwd(x, g, dn, dres, *, name, tm=512):
    M, D = x.shape
    tm = min(tm, M)

    def body(x_ref, g_ref, dn_ref, dres_ref, dx_ref, dg_ref):
        @pl.when(pl.program_id(0) == 0)
        def _():
            dg_ref[...] = jnp.zeros_like(dg_ref)

        xf = x_ref[...]
        r = lax.rsqrt(jnp.mean(xf * xf, axis=-1, keepdims=True) + EPS)
        xh = xf * r
        dn_ = dn_ref[...].astype(F32)
        dg_ref[...] += jnp.sum(dn_ * xh, axis=0, keepdims=True)
        dxh = dn_ * g_ref[...]
        dx = r * (dxh - xh * jnp.mean(dxh * xh, axis=-1, keepdims=True))
        dx_ref[...] = dres_ref[...] + dx

    row = pl.BlockSpec((tm, D), lambda i: (i, 0))
    vec = pl.BlockSpec((1, D), lambda i: (0, 0))
    return pl.pallas_call(
        body, name=name, grid=(M // tm,),
        in_specs=[row, vec, row, row], out_specs=[row, vec],
        out_shape=[jax.ShapeDtypeStruct((M, D), F32), jax.ShapeDtypeStruct((1, D), F32)],
        compiler_params=_cparams(("arbitrary",)),
    )(x, g.reshape(1, D), dn, dres)


def _loss_head(h, g, tgt, *, name, tm=512):
    M, D = h.shape
    tm = min(tm, M)

    def body(h_ref, g_ref, t_ref, loss_ref, dh_ref, dg_ref):
        @pl.when(pl.program_id(0) == 0)
        def _():
            dg_ref[...] = jnp.zeros_like(dg_ref)
            loss_ref[...] = jnp.zeros_like(loss_ref)

        xf = h_ref[...]
        r = lax.rsqrt(jnp.mean(xf * xf, axis=-1, keepdims=True) + EPS)
        xh = xf * r
        err = xh * g_ref[...] - t_ref[...]
        part = jnp.sum(jnp.mean(err * err, axis=-1, keepdims=True), axis=0, keepdims=True)
        loss_ref[...] += 0.5 * part
        dy = err * (1.0 / D)
        dg_ref[...] += jnp.sum(dy * xh, axis=0, keepdims=True)
        dxh = dy * g_ref[...]
        dh_ref[...] = r * (dxh - xh * jnp.mean(dxh * xh, axis=-1, keepdims=True))

    row = pl.BlockSpec((tm, D), lambda i: (i, 0))
    vec = pl.BlockSpec((1, D), lambda i: (0, 0))
    one = pl.BlockSpec((1, 1), lambda i: (0, 0))
    return pl.pallas_call(
        body, name=name, grid=(M // tm,),
        in_specs=[row, vec, row], out_specs=[one, row, vec],
        out_shape=[jax.ShapeDtypeStruct((1, 1), F32), jax.ShapeDtypeStruct((M, D), F32),
                   jax.ShapeDtypeStruct((1, D), F32)],
        compiler_params=_cparams(("arbitrary",)),
    )(h, g.reshape(1, D), tgt)


HG_MID = HG_CHUNK // 2 - 1
EXP_CAP = 80.0


def _sigmoid(x):
    return 1.0 / (1.0 + jnp.exp(-x))


def _dot(a, b, dims, precision=None):
    return lax.dot_general(a, b, dims, preferred_element_type=F32, precision=precision)


def _bdot(a, b, form):
    return _dot(a.astype(BF16), b.astype(BF16), _DIMS[form])


def _hdot(a, b, form):
    return _dot(a, b, _DIMS[form], precision=lax.Precision.HIGHEST)


def _hgrn_chunk_common(hq, hf, lbv, tril, rid):
    sq = _sigmoid(hq)
    q = hq * sq
    sg = _sigmoid(hf)
    f = lbv + (1.0 - lbv) * sg
    k = (1.0 - lbv) * (1.0 - sg)
    g = jnp.log(f)
    b = _dot(tril, g, _DIMS["nn"], precision=lax.Precision.HIGHEST)
    bref = jnp.sum(jnp.where(rid == HG_MID, b, 0.0), axis=0, keepdims=True)
    bend = jnp.sum(jnp.where(rid == HG_CHUNK - 1, b, 0.0), axis=0, keepdims=True)
    eb = jnp.exp(b)
    e1 = jnp.exp(jnp.minimum(b - bref, EXP_CAP))
    e2 = jnp.exp(jnp.minimum(bref - b, EXP_CAP))
    e3 = jnp.exp(bend - b)
    return sq, q, sg, f, k, bend, eb, e1, e2, e3


def _hgrn_fwd(proj, lb, gnorm, *, name, T=512):
    S = proj.shape[0]
    T = min(T, S)
    nch = T // HG_CHUNK
    C = HG_CHUNK

    def body(hq_ref, hf_ref, hi_ref, hg_ref, lb_ref, gn_ref, o_ref, oa_ref, st_ref, state):
        @pl.when(pl.program_id(1) == 0)
        def _():
            state[...] = jnp.zeros_like(state)

        lbv = lb_ref[...]
        gn = gn_ref[...]
        row = lax.broadcasted_iota(jnp.int32, (C, C), 0)
        col = lax.broadcasted_iota(jnp.int32, (C, C), 1)
        causal = row >= col
        tril = causal.astype(F32)
        rid = lax.broadcasted_iota(jnp.int32, (C, HG_DK), 0)
        for c in range(nch):
            sl = pl.ds(c * C, C)
            hq, hf, v, hg = hq_ref[sl, :], hf_ref[sl, :], hi_ref[sl, :], hg_ref[sl, :]
            _, q, _, _, k, bend, eb, e1, e2, e3 = _hgrn_chunk_common(hq, hf, lbv, tril, rid)
            st = state[...]
            st_ref[0, c] = st
            o = _hdot(q * eb, st, "nt")
            a = jnp.where(causal, _hdot(q * e1, k * e2, "nt"), 0.0)
            o = o + _hdot(a, v, "nn")
            state[...] = st * jnp.exp(bend) + _hdot(v, k * e3, "tn")
            o_ref[sl, :] = o
            r = lax.rsqrt(jnp.mean(o * o, axis=-1, keepdims=True) + EPS)
            oa_ref[sl, :] = (o * r * gn * (hg * _sigmoid(hg))).astype(oa_ref.dtype)

    def grp(gidx):
        return pl.BlockSpec((T, 128), lambda h, t: (t, gidx * 8 + h))

    return pl.pallas_call(
        body, name=name, grid=(HG_HEADS, S // T),
        in_specs=[grp(0), grp(1), grp(2), grp(3),
                  pl.BlockSpec((1, 128), lambda h, t: (0, h)), pl.BlockSpec((1, 128), lambda h, t: (0, 0))],
        out_specs=[pl.BlockSpec((T, 128), lambda h, t: (t, h)), pl.BlockSpec((T, 128), lambda h, t: (t, h)),
                   pl.BlockSpec((1, nch, HG_DV, HG_DK), lambda h, t: (h, t, 0, 0))],
        out_shape=[jax.ShapeDtypeStruct((S, HG_HEADS * HG_DV), F32), jax.ShapeDtypeStruct((S, HG_HEADS * HG_DV), BF16),
                   jax.ShapeDtypeStruct((HG_HEADS, S // C, HG_DV, HG_DK), F32)],
        scratch_shapes=[pltpu.VMEM((HG_DV, HG_DK), F32)],
        compiler_params=_cparams(("parallel", "arbitrary")),
    )(proj, proj, proj, proj, lb, gnorm)


def _hgrn_bwd(proj, lb, gnorm, o, states, doa, *, name, T=512):
    S = proj.shape[0]
    T = min(T, S)
    nch = T // HG_CHUNK
    C = HG_CHUNK
    nT = S // T

    def body(hq_ref, hf_ref, hi_ref, hg_ref, lb_ref, gn_ref, o_ref, st_ref, doa_ref,
             dhq_ref, dhf_ref, dhi_ref, dhg_ref, dlb_ref, dgn_ref, dstate):
        @pl.when(pl.program_id(1) == 0)
        def _():
            dstate[...] = jnp.zeros_like(dstate)
            dlb_ref[...] = jnp.zeros_like(dlb_ref)
            dgn_ref[...] = jnp.zeros_like(dgn_ref)

        lbv = lb_ref[...]
        gn = gn_ref[...]
        row = lax.broadcasted_iota(jnp.int32, (C, C), 0)
        col = lax.broadcasted_iota(jnp.int32, (C, C), 1)
        causal = row >= col
        tril = causal.astype(F32)
        triu = (row <= col).astype(F32)
        rid = lax.broadcasted_iota(jnp.int32, (C, HG_DK), 0)
        for c in reversed(range(nch)):
            sl = pl.ds(c * C, C)
            hq, hf, v, hg = hq_ref[sl, :], hf_ref[sl, :], hi_ref[sl, :], hg_ref[sl, :]
            sq, q, sg, f, k, bend, eb, e1, e2, e3 = _hgrn_chunk_common(hq, hf, lbv, tril, rid)
            qi, qp, kp, kend = q * eb, q * e1, k * e2, k * e3
            st0 = st_ref[0, c]
            ov = o_ref[sl, :]
            r = lax.rsqrt(jnp.mean(ov * ov, axis=-1, keepdims=True) + EPS)
            xh = ov * r
            sgg = _sigmoid(hg)
            d_oa = doa_ref[sl, :].astype(F32)
            dz = d_oa * (hg * sgg)
            dhg_ref[sl, :] = (d_oa * (xh * gn) * (sgg * (1.0 + hg * (1.0 - sgg)))).astype(dhg_ref.dtype)
            dgn_ref[0] += jnp.sum(dz * xh, axis=0, keepdims=True)
            dxh = dz * gn
            do = r * (dxh - xh * jnp.mean(dxh * xh, axis=-1, keepdims=True))
            ds1 = dstate[...]
            dqi = _hdot(do, st0, "nn")
            a = jnp.where(causal, _hdot(qp, kp, "nt"), 0.0)
            da = jnp.where(causal, _hdot(do, v, "nt"), 0.0)
            dv = _hdot(a, do, "tn") + _hdot(kend, ds1, "nt")
            dqp = _hdot(da, kp, "nn")
            dkp = _hdot(da, qp, "tn")
            dkend = _hdot(v, ds1, "nn")
            dq = dqi * eb + dqp * e1
            dk = dkp * e2 + dkend * e3
            db = dqi * qi + dqp * qp - dkp * kp - dkend * kend
            dbend = (jnp.sum(dkend * kend, axis=0, keepdims=True)
                     + jnp.exp(bend) * jnp.sum(ds1 * st0, axis=0, keepdims=True))
            db = db + jnp.where(rid == C - 1, dbend, 0.0)
            dg = _dot(triu, db, _DIMS["nn"], precision=lax.Precision.HIGHEST)
            df = dg / f - dk
            dlb_ref[...] += jnp.sum(df * (1.0 - sg), axis=0, keepdims=True)
            dhf_ref[sl, :] = (df * (1.0 - lbv) * sg * (1.0 - sg)).astype(dhf_ref.dtype)
            dhq_ref[sl, :] = (dq * (sq * (1.0 + hq * (1.0 - sq)))).astype(dhq_ref.dtype)
            dhi_ref[sl, :] = dv.astype(dhi_ref.dtype)
            dstate[...] = ds1 * jnp.exp(bend) + _hdot(do, qi, "tn")

    def grp(gidx):
        return pl.BlockSpec((T, 128), lambda h, t: (nT - 1 - t, gidx * 8 + h))

    tok = pl.BlockSpec((T, 128), lambda h, t: (nT - 1 - t, h))
    big = jax.ShapeDtypeStruct((S, HG_HEADS * HG_DV), BF16)
    return pl.pallas_call(
        body, name=name, grid=(HG_HEADS, nT),
        in_specs=[grp(0), grp(1), grp(2), grp(3),
                  pl.BlockSpec((1, 128), lambda h, t: (0, h)), pl.BlockSpec((1, 128), lambda h, t: (0, 0)),
                  tok, pl.BlockSpec((1, nch, HG_DV, HG_DK), lambda h, t: (h, nT - 1 - t, 0, 0)), tok],
        out_specs=[tok, tok, tok, tok, pl.BlockSpec((1, 128), lambda h, t: (0, h)),
                   pl.BlockSpec((1, 1, 128), lambda h, t: (h, 0, 0))],
        out_shape=[big, big, big, big, jax.ShapeDtypeStruct((1, HG_HEADS * HG_DK), F32),
                   jax.ShapeDtypeStruct((HG_HEADS, 1, HG_DV), F32)],
        scratch_shapes=[pltpu.VMEM((HG_DV, HG_DK), F32)],
        compiler_params=_cparams(("parallel", "arbitrary")),
    )(proj, proj, proj, proj, lb, gnorm, o, states, doa)


NEG = -1e30
FOX_SCALE = FOX_DH ** -0.5
FOX_PAIRS = FOX_HEADS // 2


def _fox_gate_fwd(ff, bias, *, name, T=512):
    S = ff.shape[0]
    T = min(T, S)

    def body(ff_ref, b_ref, c_ref, carry):
        @pl.when(pl.program_id(0) == 0)
        def _():
            carry[...] = jnp.zeros_like(carry)

        z = ff_ref[...] + b_ref[...]
        logf = jnp.minimum(z, 0.0) - jnp.log(1.0 + jnp.exp(-jnp.abs(z)))
        row = lax.broadcasted_iota(jnp.int32, (T, T), 0)
        col = lax.broadcasted_iota(jnp.int32, (T, T), 1)
        c = _dot((row >= col).astype(F32), logf, _DIMS["nn"], precision=lax.Precision.HIGHEST) + carry[...]
        c_ref[...] = c
        carry[...] = c[T - 1:T, :]

    return pl.pallas_call(
        body, name=name, grid=(S // T,),
        in_specs=[pl.BlockSpec((T, 128), lambda i: (i, 0)), pl.BlockSpec((1, 128), lambda i: (0, 0))],
        out_specs=pl.BlockSpec((T, 128), lambda i: (i, 0)),
        out_shape=jax.ShapeDtypeStruct((S, 128), F32),
        scratch_shapes=[pltpu.VMEM((1, 128), F32)],
        compiler_params=_cparams(("arbitrary",)),
    )(ff, bias)


def _fox_gate_bwd(ff, bias, dcs, *, name, T=512):
    S = ff.shape[0]
    T = min(T, S)
    nT = S // T

    def body(ff_ref, b_ref, d_ref, dff_ref, db_ref, carry):
        @pl.when(pl.program_id(0) == 0)
        def _():
            carry[...] = jnp.zeros_like(carry)
            db_ref[...] = jnp.zeros_like(db_ref)

        row = lax.broadcasted_iota(jnp.int32, (T, T), 0)
        col = lax.broadcasted_iota(jnp.int32, (T, T), 1)
        dlogf = carry[...] - _dot((row <= col).astype(F32), d_ref[...], _DIMS["nn"], precision=lax.Precision.HIGHEST)
        carry[...] = dlogf[0:1, :]
        dff = dlogf * (1.0 - _sigmoid(ff_ref[...] + b_ref[...]))
        dff_ref[...] = dff.astype(dff_ref.dtype)
        db_ref[...] += jnp.sum(dff, axis=0, keepdims=True)

    rev = pl.BlockSpec((T, 128), lambda i: (nT - 1 - i, 0))
    vec = pl.BlockSpec((1, 128), lambda i: (0, 0))
    return pl.pallas_call(
        body, name=name, grid=(nT,),
        in_specs=[rev, vec, rev], out_specs=[rev, vec],
        out_shape=[jax.ShapeDtypeStruct((S, 128), BF16), jax.ShapeDtypeStruct((1, 128), F32)],
        scratch_shapes=[pltpu.VMEM((1, 128), F32)],
        compiler_params=_cparams(("arbitrary",)),
    )(ff, bias, dcs)


def _fox_logits(q, k, cc, cr, qi, ki, tq, tk):
    s = _bdot(q, k, "nt") * FOX_SCALE + cc - cr
    qpos = qi * tq + lax.broadcasted_iota(jnp.int32, (tq, tk), 0)
    kpos = ki * tk + lax.broadcasted_iota(jnp.int32, (tq, tk), 1)
    return jnp.where(kpos <= qpos, s, NEG)


def _fox_fwd(proj, ccol, crow, *, name, tq=512, tk=512):
    S = proj.shape[0]
    tq, tk = min(tq, S), min(tk, S)

    def body(q_ref, k_ref, v_ref, cc_ref, cr_ref, o_ref, lse_ref, m_s, l_s, acc_s):
        qi, ki = pl.program_id(1), pl.program_id(2)

        @pl.when(ki == 0)
        def _():
            m_s[...] = jnp.full_like(m_s, NEG)
            l_s[...] = jnp.zeros_like(l_s)
            acc_s[...] = jnp.zeros_like(acc_s)

        @pl.when(ki <= qi)
        def _():
            for hh in range(2):
                ls = slice(hh * FOX_DH, (hh + 1) * FOX_DH)
                s = _fox_logits(q_ref[:, ls], k_ref[:, ls], cc_ref[0, :, hh:hh + 1], cr_ref[0, hh:hh + 1, :], qi, ki, tq, tk)
                m_old = m_s[hh]
                m_new = jnp.maximum(m_old, jnp.max(s, axis=-1, keepdims=True))
                p = jnp.exp(s - m_new)
                alpha = jnp.exp(m_old - m_new)
                l_s[hh] = alpha * l_s[hh] + jnp.sum(p, axis=-1, keepdims=True)
                p_hi = p.astype(BF16)
                p_lo = (p - p_hi.astype(F32)).astype(BF16)
                vv = v_ref[:, ls].astype(BF16)
                acc_s[hh] = alpha * acc_s[hh] + _bdot(p_hi, vv, "nn") + _bdot(p_lo, vv, "nn")
                m_s[hh] = m_new

        @pl.when(ki == qi)
        def _():
            for hh in range(2):
                o_ref[:, hh * FOX_DH:(hh + 1) * FOX_DH] = acc_s[hh] / l_s[hh]
                lse_ref[0, :, hh:hh + 1] = m_s[hh] + jnp.log(l_s[hh])

    qspec = pl.BlockSpec((tq, 128), lambda p, i, j: (i, 32 + p))
    kspec = pl.BlockSpec((tk, 128), lambda p, i, j: (jnp.minimum(j, i), 40 + p))
    vspec = pl.BlockSpec((tk, 128), lambda p, i, j: (jnp.minimum(j, i), 48 + p))
    ccs = pl.BlockSpec((1, tq, 2), lambda p, i, j: (p, i, 0))
    crs = pl.BlockSpec((1, 2, tk), lambda p, i, j: (p, 0, jnp.minimum(j, i)))
    return pl.pallas_call(
        body, name=name, grid=(FOX_PAIRS, S // tq, S // tk),
        in_specs=[qspec, kspec, vspec, ccs, crs],
        out_specs=[pl.BlockSpec((tq, 128), lambda p, i, j: (i, p)), ccs],
        out_shape=[jax.ShapeDtypeStruct((S, FOX_HEADS * FOX_DH), F32), jax.ShapeDtypeStruct((FOX_PAIRS, S, 2), F32)],
        scratch_shapes=[pltpu.VMEM((2, tq, 1), F32), pltpu.VMEM((2, tq, 1), F32), pltpu.VMEM((2, tq, FOX_DH), F32)],
        compiler_params=_cparams(("parallel", "parallel", "arbitrary")),
    )(proj, proj, proj, ccol, crow)


def _fox_bwd_dq(proj, ccol, crow, o, lse, do, *, name, tq=512, tk=512):
    S = proj.shape[0]
    tq, tk = min(tq, S), min(tk, S)

    def body(q_ref, k_ref, v_ref, cc_ref, cr_ref, o_ref, lse_ref, do_ref, dq_ref, dl_ref, acc_s):
        qi, ki = pl.program_id(1), pl.program_id(2)

        @pl.when(ki == 0)
        def _():
            acc_s[...] = jnp.zeros_like(acc_s)
            for hh in range(2):
                ls = slice(hh * FOX_DH, (hh + 1) * FOX_DH)
                dl_ref[0, :, hh:hh + 1] = jnp.sum(do_ref[:, ls].astype(F32) * o_ref[:, ls], axis=-1, keepdims=True)

        @pl.when(ki <= qi)
        def _():
            for hh in range(2):
                ls = slice(hh * FOX_DH, (hh + 1) * FOX_DH)
                s = _fox_logits(q_ref[:, ls], k_ref[:, ls], cc_ref[0, :, hh:hh + 1], cr_ref[0, hh:hh + 1, :], qi, ki, tq, tk)
                p = jnp.exp(s - lse_ref[0, :, hh:hh + 1])
                dp = _bdot(do_ref[:, ls], v_ref[:, ls], "nt")
                ds = p * (dp - dl_ref[0, :, hh:hh + 1])
                acc_s[hh] += _bdot(ds, k_ref[:, ls], "nn")

        @pl.when(ki == qi)
        def _():
            for hh in range(2):
                dq_ref[:, hh * FOX_DH:(hh + 1) * FOX_DH] = (acc_s[hh] * FOX_SCALE).astype(dq_ref.dtype)

    qspec = pl.BlockSpec((tq, 128), lambda p, i, j: (i, 32 + p))
    kspec = pl.BlockSpec((tk, 128), lambda p, i, j: (jnp.minimum(j, i), 40 + p))
    vspec = pl.BlockSpec((tk, 128), lambda p, i, j: (jnp.minimum(j, i), 48 + p))
    ccs = pl.BlockSpec((1, tq, 2), lambda p, i, j: (p, i, 0))
    crs = pl.BlockSpec((1, 2, tk), lambda p, i, j: (p, 0, jnp.minimum(j, i)))
    tok = pl.BlockSpec((tq, 128), lambda p, i, j: (i, p))
    return pl.pallas_call(
        body, name=name, grid=(FOX_PAIRS, S // tq, S // tk),
        in_specs=[qspec, kspec, vspec, ccs, crs, tok, ccs, tok],
        out_specs=[tok, ccs],
        out_shape=[jax.ShapeDtypeStruct((S, FOX_HEADS * FOX_DH), BF16), jax.ShapeDtypeStruct((FOX_PAIRS, S, 2), F32)],
        scratch_shapes=[pltpu.VMEM((2, tq, FOX_DH), F32)],
        compiler_params=_cparams(("parallel", "parallel", "arbitrary")),
    )(proj, proj, proj, ccol, crow, o, lse, do)


def _fox_bwd_dkv(proj, ccol, crow, lse, delta, do, *, name, tq=512, tk=512):
    S = proj.shape[0]
    tq, tk = min(tq, S), min(tk, S)
    nq = S // tq

    def body(q_ref, k_ref, v_ref, cc_ref, cr_ref, lse_ref, dl_ref, do_ref, dk_ref, dv_ref, dcs_ref, dk_s, dv_s):
        ki, qi = pl.program_id(1), pl.program_id(2)

        @pl.when(qi == 0)
        def _():
            dk_s[...] = jnp.zeros_like(dk_s)
            dv_s[...] = jnp.zeros_like(dv_s)
            dcs_ref[...] = jnp.zeros_like(dcs_ref)

        @pl.when(qi >= ki)
        def _():
            for hh in range(2):
                ls = slice(hh * FOX_DH, (hh + 1) * FOX_DH)
                s = _fox_logits(q_ref[:, ls], k_ref[:, ls], cc_ref[0, :, hh:hh + 1], cr_ref[0, hh:hh + 1, :], qi, ki, tq, tk)
                p = jnp.exp(s - lse_ref[0, :, hh:hh + 1])
                dp = _bdot(do_ref[:, ls], v_ref[:, ls], "nt")
                ds = p * (dp - dl_ref[0, :, hh:hh + 1])
                dv_s[hh] += _bdot(p, do_ref[:, ls], "tn")
                dk_s[hh] += _bdot(ds, q_ref[:, ls], "tn")
                dcs_ref[0, hh:hh + 1, :] += jnp.sum(ds, axis=0, keepdims=True)

        @pl.when(qi == nq - 1)
        def _():
            for hh in range(2):
                dk_ref[:, hh * FOX_DH:(hh + 1) * FOX_DH] = (dk_s[hh] * FOX_SCALE).astype(dk_ref.dtype)
                dv_ref[:, hh * FOX_DH:(hh + 1) * FOX_DH] = dv_s[hh].astype(dv_ref.dtype)

    qspec = pl.BlockSpec((tq, 128), lambda p, j, i: (jnp.maximum(i, j), 32 + p))
    kspec = pl.BlockSpec((tk, 128), lambda p, j, i: (j, 40 + p))
    vspec = pl.BlockSpec((tk, 128), lambda p, j, i: (j, 48 + p))
    ccs = pl.BlockSpec((1, tq, 2), lambda p, j, i: (p, jnp.maximum(i, j), 0))
    crs = pl.BlockSpec((1, 2, tk), lambda p, j, i: (p, 0, j))
    dos = pl.BlockSpec((tq, 128), lambda p, j, i: (jnp.maximum(i, j), p))
    ktok = pl.BlockSpec((tk, 128), lambda p, j, i: (j, p))
    big = jax.ShapeDtypeStruct((S, FOX_HEADS * FOX_DH), BF16)
    return pl.pallas_call(
        body, name=name, grid=(FOX_PAIRS, S // tk, nq),
        in_specs=[qspec, kspec, vspec, ccs, crs, ccs, ccs, dos],
        out_specs=[ktok, ktok, crs],
        out_shape=[big, big, jax.ShapeDtypeStruct((FOX_PAIRS, 2, S), F32)],
        scratch_shapes=[pltpu.VMEM((2, tk, FOX_DH), F32), pltpu.VMEM((2, tk, FOX_DH), F32)],
        compiler_params=_cparams(("parallel", "parallel", "arbitrary")),
    )(proj, proj, proj, ccol, crow, lse, delta, do)


def _merge_fwd(proj, pa, pb, *, name, T=512):
    S, D = pa.shape
    T = min(T, S)

    def body(ga_ref, gb_ref, pa_ref, pb_ref, m_ref):
        m_ref[...] = (_sigmoid(ga_ref[...]) * pa_ref[...] + _sigmoid(gb_ref[...]) * pb_ref[...]).astype(m_ref.dtype)

    tok = pl.BlockSpec((T, D), lambda i: (i, 0))
    return pl.pallas_call(
        body, name=name, grid=(S // T,),
        in_specs=[pl.BlockSpec((T, D), lambda i: (i, 7)), pl.BlockSpec((T, D), lambda i: (i, 8)), tok, tok],
        out_specs=tok, out_shape=jax.ShapeDtypeStruct((S, D), BF16),
        compiler_params=_cparams(("parallel",)),
    )(proj, proj, pa, pb)


def _merge_bwd(proj, pa, pb, dm, *, name, T=512):
    S, D = pa.shape
    T = min(T, S)

    def body(ga_ref, gb_ref, pa_ref, pb_ref, dm_ref, dpa_ref, dpb_ref, dga_ref, dgb_ref):
        dm_ = dm_ref[...]
        sa, sb = _sigmoid(ga_ref[...]), _sigmoid(gb_ref[...])
        dpa_ref[...] = (dm_ * sa).astype(BF16)
        dpb_ref[...] = (dm_ * sb).astype(BF16)
        dga_ref[...] = (dm_ * pa_ref[...] * sa * (1.0 - sa)).astype(BF16)
        dgb_ref[...] = (dm_ * pb_ref[...] * sb * (1.0 - sb)).astype(BF16)

    tok = pl.BlockSpec((T, D), lambda i: (i, 0))
    big = jax.ShapeDtypeStruct((S, D), BF16)
    return pl.pallas_call(
        body, name=name, grid=(S // T,),
        in_specs=[pl.BlockSpec((T, D), lambda i: (i, 7)), pl.BlockSpec((T, D), lambda i: (i, 8)), tok, tok, tok],
        out_specs=[tok, tok, tok, tok], out_shape=[big, big, big, big],
        compiler_params=_cparams(("parallel",)),
    )(proj, proj, pa, pb, dm)


INV_SQRT2 = 0.7071067811865476
INV_SQRT2PI = 0.3989422804014327


def _shifted(u, prev, rid):
    m1 = jnp.where(rid == 0, prev[7:8, :], pltpu.roll(u, 1, 0))
    m2 = jnp.where(rid == 0, prev[6:7, :], jnp.where(rid == 1, prev[7:8, :], pltpu.roll(u, 2, 0)))
    return m1, m2


def _conv_acc(u, prev, w_ref, b_ref, rid):
    m1, m2 = _shifted(u, prev, rid)
    return b_ref[...] + w_ref[0:1, :] * m2 + w_ref[1:2, :] * m1 + w_ref[2:3, :] * u, m1, m2


def _convglu_fwd(ug, uv, wg, wv, bg, bv, *, name, T=512, tc=256):
    S, F = ug.shape
    T = min(T, S)

    def body(ug_ref, uv_ref, wg_ref, wv_ref, bg_ref, bv_ref, a_ref, pg, pv):
        @pl.when(pl.program_id(1) == 0)
        def _():
            pg[...] = jnp.zeros_like(pg)
            pv[...] = jnp.zeros_like(pv)

        rid = lax.broadcasted_iota(jnp.int32, (T, tc), 0)
        g_, v_ = ug_ref[...], uv_ref[...]
        accg, _, _ = _conv_acc(g_, pg[...], wg_ref, bg_ref, rid)
        accv, _, _ = _conv_acc(v_, pv[...], wv_ref, bv_ref, rid)
        gel = 0.5 * accg * (1.0 + lax.erf(accg * INV_SQRT2))
        a_ref[...] = (gel * accv).astype(a_ref.dtype)
        pg[...] = g_[T - 8:T, :]
        pv[...] = v_[T - 8:T, :]

    tok = pl.BlockSpec((T, tc), lambda j, t: (t, j))
    w3 = pl.BlockSpec((3, tc), lambda j, t: (0, j))
    b1 = pl.BlockSpec((1, tc), lambda j, t: (0, j))
    return pl.pallas_call(
        body, name=name, grid=(F // tc, S // T),
        in_specs=[tok, tok, w3, w3, b1, b1], out_specs=tok,
        out_shape=jax.ShapeDtypeStruct((S, F), BF16),
        scratch_shapes=[pltpu.VMEM((8, tc), F32), pltpu.VMEM((8, tc), F32)],
        compiler_params=_cparams(("parallel", "arbitrary")),
    )(ug, uv, wg, wv, bg, bv)


def _convglu_bwd_acc(ug, uv, wg, wv, bg, bv, da, *, name, T=512, tc=256):
    S, F = ug.shape
    T = min(T, S)

    def body(ug_ref, uv_ref, wg_ref, wv_ref, bg_ref, bv_ref, da_ref,
             dg_ref, dv_ref, dwg_ref, dwv_ref, dbg_ref, dbv_ref, pg, pv):
        @pl.when(pl.program_id(1) == 0)
        def _():
            pg[...] = jnp.zeros_like(pg)
            pv[...] = jnp.zeros_like(pv)
            for r in (dwg_ref, dwv_ref, dbg_ref, dbv_ref):
                r[...] = jnp.zeros_like(r)

        rid = lax.broadcasted_iota(jnp.int32, (T, tc), 0)
        g_, v_ = ug_ref[...], uv_ref[...]
        accg, g1, g2 = _conv_acc(g_, pg[...], wg_ref, bg_ref, rid)
        accv, v1, v2 = _conv_acc(v_, pv[...], wv_ref, bv_ref, rid)
        cdf = 0.5 * (1.0 + lax.erf(accg * INV_SQRT2))
        pdf = INV_SQRT2PI * jnp.exp(-0.5 * accg * accg)
        da_ = da_ref[...].astype(F32)
        dgate = da_ * accv * (cdf + accg * pdf)
        dval = da_ * (accg * cdf)
        dg_ref[...] = dgate.astype(dg_ref.dtype)
        dv_ref[...] = dval.astype(dv_ref.dtype)
        dbg_ref[...] += jnp.sum(dgate, axis=0, keepdims=True)
        dbv_ref[...] += jnp.sum(dval, axis=0, keepdims=True)
        for j, (sg_, sv_) in enumerate(((g2, v2), (g1, v1), (g_, v_))):
            dwg_ref[j:j + 1, :] += jnp.sum(dgate * sg_, axis=0, keepdims=True)
            dwv_ref[j:j + 1, :] += jnp.sum(dval * sv_, axis=0, keepdims=True)
        pg[...] = g_[T - 8:T, :]
        pv[...] = v_[T - 8:T, :]

    tok = pl.BlockSpec((T, tc), lambda j, t: (t, j))
    w3 = pl.BlockSpec((3, tc), lambda j, t: (0, j))
    b1 = pl.BlockSpec((1, tc), lambda j, t: (0, j))
    big = jax.ShapeDtypeStruct((S, F), BF16)
    return pl.pallas_call(
        body, name=name, grid=(F // tc, S // T),
        in_specs=[tok, tok, w3, w3, b1, b1, tok], out_specs=[tok, tok, w3, w3, b1, b1],
        out_shape=[big, big, jax.ShapeDtypeStruct((3, F), F32), jax.ShapeDtypeStruct((3, F), F32),
                   jax.ShapeDtypeStruct((1, F), F32), jax.ShapeDtypeStruct((1, F), F32)],
        scratch_shapes=[pltpu.VMEM((8, tc), F32), pltpu.VMEM((8, tc), F32)],
        compiler_params=_cparams(("parallel", "arbitrary")),
    )(ug, uv, wg, wv, bg, bv, da)


def _conv_bwd_u(dacc, w, *, name, T=512, tc=256):
    S, F = dacc.shape
    T = min(T, S)
    nT = S // T

    def body(d_ref, w_ref, du_ref, nxt):
        @pl.when(pl.program_id(1) == 0)
        def _():
            nxt[...] = jnp.zeros_like(nxt)

        rid = lax.broadcasted_iota(jnp.int32, (T, tc), 0)
        d = d_ref[...].astype(F32)
        nx = nxt[...]
        p1 = jnp.where(rid == T - 1, nx[0:1, :], pltpu.roll(d, T - 1, 0))
        p2 = jnp.where(rid == T - 1, nx[1:2, :], jnp.where(rid == T - 2, nx[0:1, :], pltpu.roll(d, T - 2, 0)))
        du_ref[...] = (w_ref[2:3, :] * d + w_ref[1:2, :] * p1 + w_ref[0:1, :] * p2).astype(du_ref.dtype)
        nxt[...] = d[0:8, :]

    tok = pl.BlockSpec((T, tc), lambda j, t: (nT - 1 - t, j))
    return pl.pallas_call(
        body, name=name, grid=(F // tc, nT),
        in_specs=[tok, pl.BlockSpec((3, tc), lambda j, t: (0, j))], out_specs=tok,
        out_shape=jax.ShapeDtypeStruct((S, F), BF16),
        scratch_shapes=[pltpu.VMEM((8, tc), F32)],
        compiler_params=_cparams(("parallel", "arbitrary")),
    )(dacc, w)


def _local_step(x, tgt, w, p):
    S = x.shape[0]
    mm = _matmul
    n1 = _rms_fwd(x, p["norm_mix"], name="rms1_fwd")
    proj = mm(n1, w["wm"], "nn", name="proj_main")
    ff = mm(n1, w["wff"], "nn", name="proj_ff")
    lb = _lb_fwd(p["hg_lb_logits"], name="lb_fwd")
    gnorm = p["hg_norm"].reshape(1, HG_DV)
    o_hg, oa, states = _hgrn_fwd(proj, lb, gnorm, name="hgrn_fwd")
    bias = jnp.pad(p["fox_f_bias"].reshape(1, FOX_HEADS), ((0, 0), (0, 128 - FOX_HEADS)))
    c = _fox_gate_fwd(ff, bias, name="fox_gate_fwd")
    c16 = c[:, :FOX_HEADS]
    ccol = c16.reshape(S, FOX_PAIRS, 2).transpose(1, 0, 2)
    crow = c16.T.reshape(FOX_PAIRS, 2, S)
    ob, lse = _fox_fwd(proj, ccol, crow, name="fox_fwd")
    pa = mm(oa, w["wa"], "nn", name="branch_a")
    pb = mm(ob, w["wb"], "nn", name="branch_b")
    merged = _merge_fwd(proj, pa, pb, name="merge_fwd")
    h1 = mm(merged, w["wo"], "nn", addend=x, name="mix_out")
    n2 = _rms_fwd(h1, p["norm_ffn"], name="rms2_fwd")
    ug = mm(n2, w["wug"], "nn", name="up_gate")
    uv = mm(n2, w["wuv"], "nn", name="up_val")
    a = _convglu_fwd(ug, uv, w["cwg"], w["cwv"], p["cbg"], p["cbv"], name="convglu_fwd")
    h2 = mm(a, w["wd"], "nn", addend=h1, name="ffn_down")
    loss, dh2, d_norm_final = _loss_head(h2, p["norm_final"], tgt, name="loss_head")
    da = mm(dh2, w["wd"], "nt", out_dtype=BF16, name="d_act")
    d_wd = mm(a, dh2, "tn", name="dw_down")
    daccg, daccv, d_cwg, d_cwv, d_cbg, d_cbv = _convglu_bwd_acc(
        ug, uv, w["cwg"], w["cwv"], p["cbg"], p["cbv"], da, name="convglu_bwd")
    dug = _conv_bwd_u(daccg, w["cwg"], name="conv_bwd_gate")
    duv = _conv_bwd_u(daccv, w["cwv"], name="conv_bwd_val")
    dn2 = mm(dug, w["wug"], "nt", name="dn2_gate")
    dn2 = mm(duv, w["wuv"], "nt", addend=dn2, name="dn2_val")
    d_wug = mm(n2, dug, "tn", name="dw_up_gate")
    d_wuv = mm(n2, duv, "tn", name="dw_up_val")
    dh1, d_norm_ffn = _rms_bwd(h1, p["norm_ffn"], dn2, dh2, name="rms2_bwd")
    dmerged = mm(dh1, w["wo"], "nt", name="d_merged")
    d_wo = mm(merged, dh1, "tn", name="dw_out")
    dpa, dpb, dga, dgb = _merge_bwd(proj, pa, pb, dmerged, name="merge_bwd")
    doa = mm(dpa, w["wa"], "nt", name="d_oa")
    dob = mm(dpb, w["wb"], "nt", out_dtype=BF16, name="d_ob")
    d_wa = mm(oa, dpa, "tn", name="dw_branch_a")
    d_wb = mm(ob, dpb, "tn", name="dw_branch_b")
    dhq, dhf, dhi, dhg, dlb, dgn8 = _hgrn_bwd(proj, lb, gnorm, o_hg, states, doa, name="hgrn_bwd")
    d_logits = _lb_bwd(p["hg_lb_logits"], dlb, name="lb_bwd")
    dq, delta = _fox_bwd_dq(proj, ccol, crow, ob, lse, dob, name="fox_bwd_dq")
    dk, dv, dcs = _fox_bwd_dkv(proj, ccol, crow, lse, delta, dob, name="fox_bwd_dkv")
    dcs_tok = jnp.pad(dcs.reshape(FOX_HEADS, S).T, ((0, 0), (0, 128 - FOX_HEADS)))
    dff, dbias = _fox_gate_bwd(ff, bias, dcs_tok, name="fox_gate_bwd")
    dproj = jnp.concatenate([dhq, dhf, dhi, dhg, dq, dk, dv, dga, dgb], axis=1)
    dn1 = mm(dff, w["wff"], "nt", name="dn1_ff")
    dn1 = mm(dproj, w["wm"], "nt", addend=dn1, name="dn1_main")
    d_wm = mm(n1, dproj, "tn", name="dw_in_main")
    d_wff = mm(n1, dff, "tn", name="dw_in_ff")
    dx, d_norm_mix = _rms_bwd(x, p["norm_mix"], dn1, dh1, name="rms1_bwd")
    grads = dict(
        wm=d_wm, wff=d_wff, wa=d_wa, wb=d_wb, wo=d_wo, wug=d_wug, wuv=d_wuv, cwg=d_cwg, cwv=d_cwv, wd=d_wd,
        norm_mix=d_norm_mix.reshape(-1), fox_f_bias=dbias[0, :FOX_HEADS], hg_lb_logits=d_logits,
        hg_norm=jnp.sum(dgn8, axis=0).reshape(-1), norm_ffn=d_norm_ffn.reshape(-1), cbg=d_cbg, cbv=d_cbv,
        norm_final=d_norm_final.reshape(-1))
    return loss, dx, grads


MESH = pl.DeviceIdType.MESH
ANY = pl.BlockSpec(memory_space=pl.ANY)


def _all_gather(xs, *, name):
    def body(x_ref, out_ref, send_sems, recv_sems, local_sem):
        x, y, c = lax.axis_index("x"), lax.axis_index("y"), lax.axis_index("c")
        me, sibling = (x, y, c), (x, y, 1 - c)
        chips = [(1 - x, y), (x, 1 - y), (1 - x, 1 - y)]

        def rows(px, py, pc):
            return out_ref.at[4 * px + 2 * py + pc]

        def copy(k, block, to, src=None):
            return pltpu.make_async_remote_copy(
                src_ref=rows(*block) if src is None else src, dst_ref=rows(*block),
                send_sem=send_sems.at[k], recv_sem=recv_sems.at[k], device_id=to, device_id_type=MESH)

        mine = pltpu.make_async_copy(x_ref, rows(*me), local_sem)
        mine.start()
        first = [copy(0, me, sibling, src=x_ref)]
        first += [copy(1 + j, me, (*chip, c), src=x_ref) for j, chip in enumerate(chips)]
        for cp in first:
            cp.start()
        passed = [copy(4 + j, (*chip, c), sibling) for j, chip in enumerate(chips)]
        for j, chip in enumerate(chips):
            copy(1 + j, (*chip, c), me).wait_recv()
            passed[j].start()
        copy(0, sibling, me).wait_recv()
        for j, chip in enumerate(chips):
            copy(4 + j, (*chip, 1 - c), me).wait_recv()
        for cp in first + passed:
            cp.wait_send()
        mine.wait()

    return pl.pallas_call(
        body, name=name, in_specs=[ANY], out_specs=ANY,
        out_shape=jax.ShapeDtypeStruct((N_DEV,) + xs.shape, xs.dtype),
        scratch_shapes=[pltpu.SemaphoreType.DMA((7,)), pltpu.SemaphoreType.DMA((7,)), pltpu.SemaphoreType.DMA],
    )(xs)


def _exchange_blocks(g, *, name):
    def body(g_ref, out_ref, send_sems, recv_sems, local_sem):
        x, y, c = lax.axis_index("x"), lax.axis_index("y"), lax.axis_index("c")
        me = 4 * x + 2 * y + c
        mine = pltpu.make_async_copy(g_ref.at[me], out_ref.at[me], local_sem)
        mine.start()
        sends, recvs = [], []
        for k in range(1, N_DEV):
            px = 1 - x if k & 4 else x
            py = 1 - y if k & 2 else y
            pc = 1 - c if k & 1 else c
            p = 4 * px + 2 * py + pc
            sends.append(pltpu.make_async_remote_copy(
                src_ref=g_ref.at[p], dst_ref=out_ref.at[me], send_sem=send_sems.at[k - 1], recv_sem=recv_sems.at[k - 1],
                device_id=(px, py, pc), device_id_type=MESH))
            recvs.append(pltpu.make_async_remote_copy(
                src_ref=g_ref.at[p], dst_ref=out_ref.at[p], send_sem=send_sems.at[k - 1], recv_sem=recv_sems.at[k - 1],
                device_id=(px, py, pc), device_id_type=MESH))
        for cp in sends:
            cp.start()
        for cp in recvs:
            cp.wait_recv()
        for cp in sends:
            cp.wait_send()
        mine.wait()

    return pl.pallas_call(
        body, name=name, in_specs=[ANY], out_specs=ANY,
        out_shape=jax.ShapeDtypeStruct(g.shape, g.dtype),
        scratch_shapes=[pltpu.SemaphoreType.DMA((7,)), pltpu.SemaphoreType.DMA((7,)), pltpu.SemaphoreType.DMA],
    )(g)


def _adamw(parts, w, m, v, *, name, T=512):
    R, L = w.shape
    c1 = 1.0 / (1.0 - ADAM_B1 ** ADAM_STEP)
    c2 = 1.0 / (1.0 - ADAM_B2 ** ADAM_STEP)

    def body(p_ref, w_ref, m_ref, v_ref, g_ref, d_ref, nm_ref, nv_ref):
        g = p_ref[0]
        for s in range(1, N_DEV):
            g = g + p_ref[s]
        g_ref[...] = g
        nm = ADAM_B1 * m_ref[...] + (1.0 - ADAM_B1) * g
        nv = ADAM_B2 * v_ref[...] + (1.0 - ADAM_B2) * (g * g)
        nm_ref[...] = nm
        nv_ref[...] = nv
        d_ref[...] = -ADAM_LR * ((nm * c1) / (jnp.sqrt(nv * c2) + ADAM_EPS) + ADAM_WD * w_ref[...])

    blk = pl.BlockSpec((T, L), lambda i: (i, 0))
    out = jax.ShapeDtypeStruct((R, L), F32)
    return pl.pallas_call(
        body, name=name, grid=(R // T,),
        in_specs=[pl.BlockSpec((N_DEV, T, L), lambda i: (0, i, 0)), blk, blk, blk],
        out_specs=[blk, blk, blk, blk], out_shape=[out, out, out, out],
        compiler_params=_cparams(("parallel",)),
    )(parts, w, m, v)


D_IN = 9232
FF_LO, FF_HI = 7168, 7184
IN_SH, UP_SH, DOWN_SH = D_IN // N_DEV, 2 * D_FF // N_DEV, D_FF // N_DEV
SQ_SH = D_MODEL // N_DEV

BIG = [("w_in", (1, D_MODEL, IN_SH)), ("w_branch_a", (1, SQ_SH, D_MODEL)), ("w_branch_b", (1, SQ_SH, D_MODEL)),
       ("w_out", (1, SQ_SH, D_MODEL)), ("w_up", (1, D_MODEL, UP_SH)), ("conv_w", (1, 3, UP_SH)),
       ("w_down", (1, DOWN_SH, D_MODEL))]
SMALL = [("norm_mix", (1, D_MODEL)), ("fox_f_bias", (1, FOX_HEADS)), ("hg_lb_logits", (2, HG_HEADS * HG_DK)),
         ("hg_norm", (1, HG_DV)), ("norm_ffn", (1, D_MODEL)), ("conv_b", (1, 2 * D_FF)), ("norm_final", (D_MODEL,))]
NAMES = ["norm_mix", "w_in", "fox_f_bias", "hg_lb_logits", "hg_norm", "w_branch_a", "w_branch_b", "w_out",
         "norm_ffn", "w_up", "conv_w", "conv_b", "w_down", "norm_final"]


def _size(shape):
    n = 1
    for s in shape:
        n *= s
    return n


PACK_ROWS = 20992
GATHER_ROWS = 20800
assert sum(_size(s) for _, s in BIG + SMALL) <= PACK_ROWS * 128


def _pack_rows(flat_parts, rows):
    flat = jnp.concatenate(flat_parts, axis=-1)
    pad = rows * 128 - flat.shape[-1]
    flat = jnp.pad(flat, [(0, 0)] * (flat.ndim - 1) + [(0, pad)])
    return flat.reshape(flat.shape[:-1] + (rows, 128))


def _pack_shard(vals):
    return _pack_rows([vals[n].reshape(1, -1).astype(F32) for n, _ in BIG + SMALL], PACK_ROWS)[0]


def _unpack_shard(buf):
    flat = buf.reshape(-1)
    out, off = {}, 0
    for n, shape in BIG + SMALL:
        out[n] = flat[off:off + _size(shape)].reshape(shape)
        off += _size(shape)
    return out


def _cols_by_device(a, width):
    rows = a.shape[0]
    return a.reshape(rows, N_DEV, width).transpose(1, 0, 2).reshape(N_DEV, rows * width)


def _cols_from_devices(a, rows, width):
    return a.reshape(N_DEV, rows, width).transpose(1, 0, 2).reshape(rows, N_DEV * width)


def _pack_grads(g):
    w_in = jnp.concatenate([g["wm"][:, :FF_LO], g["wff"][:, :FOX_HEADS], g["wm"][:, FF_LO:]], axis=1)
    w_up = jnp.concatenate([g["wug"], g["wuv"]], axis=1)
    conv_w = jnp.concatenate([g["cwg"], g["cwv"]], axis=1)
    conv_b = jnp.concatenate([g["cbg"], g["cbv"]], axis=1)
    big = [_cols_by_device(w_in, IN_SH), g["wa"].reshape(N_DEV, -1), g["wb"].reshape(N_DEV, -1),
           g["wo"].reshape(N_DEV, -1), _cols_by_device(w_up, UP_SH), _cols_by_device(conv_w, UP_SH),
           g["wd"].reshape(N_DEV, -1)]
    small = [g["norm_mix"], g["fox_f_bias"], g["hg_lb_logits"], g["hg_norm"], g["norm_ffn"], conv_b, g["norm_final"]]
    small = [jnp.broadcast_to(s.reshape(1, -1), (N_DEV, s.size)) for s in small]
    return _pack_rows(big + small, PACK_ROWS)


def _gather_weights(w_in, w_a, w_b, w_o, w_up, conv_w, w_down):
    taps = lax.bitcast_convert_type(conv_w.reshape(3, UP_SH), BF16).reshape(1, -1)
    mats = [w_in, w_a, w_b, w_o, w_up, w_down]
    packed = _pack_rows([t.reshape(1, -1).astype(BF16) for t in mats] + [taps], GATHER_ROWS)[0]
    full = _all_gather(packed, name="gather_weights").reshape(N_DEV, -1)
    off = 0

    def take(n):
        nonlocal off
        piece = full[:, off:off + n]
        off += n
        return piece

    win = _cols_from_devices(take(D_MODEL * IN_SH), D_MODEL, IN_SH)
    wa = take(SQ_SH * D_MODEL).reshape(D_MODEL, D_MODEL)
    wb = take(SQ_SH * D_MODEL).reshape(D_MODEL, D_MODEL)
    wo = take(SQ_SH * D_MODEL).reshape(D_MODEL, D_MODEL)
    wup = _cols_from_devices(take(D_MODEL * UP_SH), D_MODEL, UP_SH)
    wd = take(DOWN_SH * D_MODEL).reshape(D_FF, D_MODEL)
    cw = lax.bitcast_convert_type(take(3 * UP_SH * 2).reshape(N_DEV, 3, UP_SH, 2), F32)
    cw = cw.transpose(1, 0, 2).reshape(3, 2 * D_FF)
    return dict(
        wm=jnp.concatenate([win[:, :FF_LO], win[:, FF_HI:]], axis=1),
        wff=jnp.pad(win[:, FF_LO:FF_HI], ((0, 0), (0, 128 - FOX_HEADS))),
        wa=wa, wb=wb, wo=wo, wug=wup[:, :D_FF], wuv=wup[:, D_FF:], cwg=cw[:, :D_FF], cwv=cw[:, D_FF:], wd=wd)


def kernel(x, norm_mix, w_in, fox_f_bias, hg_lb_logits, hg_norm, w_branch_a, w_branch_b, w_out, norm_ffn, w_up, conv_w, conv_b, w_down, norm_final, loss_target, m_norm_mix, m_w_in, m_fox_f_bias, m_hg_lb_logits, m_hg_norm, m_w_branch_a, m_w_branch_b, m_w_out, m_norm_ffn, m_w_up, m_conv_w, m_conv_b, m_w_down, m_norm_final, v_norm_mix, v_w_in, v_fox_f_bias, v_hg_lb_logits, v_hg_norm, v_w_branch_a, v_w_branch_b, v_w_out, v_norm_ffn, v_w_up, v_conv_w, v_conv_b, v_w_down, v_norm_final):
    wv = dict(norm_mix=norm_mix, w_in=w_in, fox_f_bias=fox_f_bias, hg_lb_logits=hg_lb_logits, hg_norm=hg_norm,
              w_branch_a=w_branch_a, w_branch_b=w_branch_b, w_out=w_out, norm_ffn=norm_ffn, w_up=w_up, conv_w=conv_w,
              conv_b=conv_b, w_down=w_down, norm_final=norm_final)
    mv = dict(norm_mix=m_norm_mix, w_in=m_w_in, fox_f_bias=m_fox_f_bias, hg_lb_logits=m_hg_lb_logits, hg_norm=m_hg_norm,
              w_branch_a=m_w_branch_a, w_branch_b=m_w_branch_b, w_out=m_w_out, norm_ffn=m_norm_ffn, w_up=m_w_up,
              conv_w=m_conv_w, conv_b=m_conv_b, w_down=m_w_down, norm_final=m_norm_final)
    vv = dict(norm_mix=v_norm_mix, w_in=v_w_in, fox_f_bias=v_fox_f_bias, hg_lb_logits=v_hg_lb_logits, hg_norm=v_hg_norm,
              w_branch_a=v_w_branch_a, w_branch_b=v_w_branch_b, w_out=v_w_out, norm_ffn=v_norm_ffn, w_up=v_w_up,
              conv_w=v_conv_w, conv_b=v_conv_b, w_down=v_w_down, norm_final=v_norm_final)

    w = _gather_weights(w_in, w_branch_a, w_branch_b, w_out, w_up, conv_w, w_down)
    p = dict(norm_mix=norm_mix[0], fox_f_bias=fox_f_bias[0], hg_lb_logits=hg_lb_logits, hg_norm=hg_norm[0],
             norm_ffn=norm_ffn[0], cbg=conv_b[:, :D_FF], cbv=conv_b[:, D_FF:], norm_final=norm_final)
    loss, dx, grads = _local_step(x[0], loss_target[0], w, p)
    loss = lax.psum(loss[0, 0], ("x", "y", "c"))

    parts = _exchange_blocks(_pack_grads(grads), name="exchange_grads")
    outs = _adamw(parts, _pack_shard(wv), _pack_shard(mv), _pack_shard(vv), name="adamw")
    g_out, d_out, m_out, v_out = (_unpack_shard(o) for o in outs)
    return (loss, dx[None], *[g_out[n] for n in NAMES], *[d_out[n] for n in NAMES],
            *[m_out[n] for n in NAMES], *[v_out[n] for n in NAMES])


def _lb_fwd(logits, *, name):
    def body(l_ref, lb_ref):
        lb_ref[...] = _sigmoid(l_ref[0:1, :] - l_ref[1:2, :])

    return pl.pallas_call(body, name=name, out_shape=jax.ShapeDtypeStruct((1, logits.shape[1]), F32))(logits)


def _lb_bwd(logits, dlb, *, name):
    def body(l_ref, d_ref, o_ref):
        lbv = _sigmoid(l_ref[0:1, :] - l_ref[1:2, :])
        t = d_ref[...] * lbv * (1.0 - lbv)
        o_ref[0:1, :] = t
        o_ref[1:2, :] = -t

    return pl.pallas_call(body, name=name, out_shape=jax.ShapeDtypeStruct(logits.shape, F32))(logits, dlb)
```

```python
import functools

import jax
import jax.numpy as jnp
from jax import lax
from jax.experimental import pallas as pl
from jax.experimental.pallas import tpu as pltpu

F32 = jnp.float32
BF16 = jnp.bfloat16

D_MODEL = 1024
HG_HEADS = 8
HG_DK = 128
HG_DV = 128
HG_CHUNK = 64
FOX_HEADS = 16
FOX_DH = 64
D_FF = 2816
EPS = 1e-6
N_DEV = 8

ADAM_LR = 0.001
ADAM_B1 = 0.9
ADAM_B2 = 0.999
ADAM_EPS = 1e-08
ADAM_WD = 0.01
ADAM_STEP = 10

VMEM_LIMIT = 56 * 1024 * 1024


def _cparams(sem):
    return pltpu.CompilerParams(dimension_semantics=sem, vmem_limit_bytes=VMEM_LIMIT)


_DIMS = {
    "nn": (((1,), (0,)), ((), ())),
    "nt": (((1,), (1,)), ((), ())),
    "tn": (((0,), (0,)), ((), ())),
}


def _pick(n, prefs):
    for p in prefs:
        if n % p == 0:
            return p
    return n


def _matmul(a, b, form, *, out_dtype=F32, addend=None, tm=None, tn=None, tk=None, name):
    if form == "nn":
        (M, K), (K2, N) = a.shape, b.shape
    elif form == "nt":
        (M, K), (N, K2) = a.shape, b.shape
    else:
        (K, M), (K2, N) = a.shape, b.shape
    assert K == K2, (a.shape, b.shape, form)
    tm = tm or _pick(M, (512, 256, 128))
    tn = tn or _pick(N, (512, 256, 128))
    tk = tk or (K if K <= 2816 else _pick(K, (1024, 512, 256, 128)))
    assert M % tm == 0 and N % tn == 0 and K % tk == 0, (M, N, K, tm, tn, tk)
    nk = K // tk
    dims = _DIMS[form]

    def body(*refs):
        if addend is None:
            a_ref, b_ref, o_ref, acc_ref = refs
            add_ref = None
        else:
            a_ref, b_ref, add_ref, o_ref, acc_ref = refs
        k = pl.program_id(2)

        @pl.when(k == 0)
        def _():
            acc_ref[...] = jnp.zeros_like(acc_ref)

        acc_ref[...] += lax.dot_general(a_ref[...].astype(BF16), b_ref[...].astype(BF16), dims,
                                        preferred_element_type=F32)

        @pl.when(k == nk - 1)
        def _():
            r = acc_ref[...]
            if add_ref is not None:
                r = r + add_ref[...].astype(F32)
            o_ref[...] = r.astype(o_ref.dtype)

    if form == "nn":
        a_spec = pl.BlockSpec((tm, tk), lambda i, j, k: (i, k))
        b_spec = pl.BlockSpec((tk, tn), lambda i, j, k: (k, j))
    elif form == "nt":
        a_spec = pl.BlockSpec((tm, tk), lambda i, j, k: (i, k))
        b_spec = pl.BlockSpec((tn, tk), lambda i, j, k: (j, k))
    else:
        a_spec = pl.BlockSpec((tk, tm), lambda i, j, k: (k, i))
        b_spec = pl.BlockSpec((tk, tn), lambda i, j, k: (k, j))
    o_spec = pl.BlockSpec((tm, tn), lambda i, j, k: (i, j))
    in_specs = [a_spec, b_spec] + ([o_spec] if addend is not None else [])
    args = (a, b) + ((addend,) if addend is not None else ())
    return pl.pallas_call(
        body, name=name, grid=(M // tm, N // tn, nk),
        in_specs=in_specs, out_specs=o_spec,
        out_shape=jax.ShapeDtypeStruct((M, N), out_dtype),
        scratch_shapes=[pltpu.VMEM((tm, tn), F32)],
        compiler_params=_cparams(("parallel", "parallel", "arbitrary")),
    )(*args)


def _rms_fwd(x, g, *, name, tm=512):
    M, D = x.shape
    tm = min(tm, M)

    def body(x_ref, g_ref, n_ref):
        xf = x_ref[...]
        r = lax.rsqrt(jnp.mean(xf * xf, axis=-1, keepdims=True) + EPS)
        n_ref[...] = (xf * r * g_ref[...]).astype(n_ref.dtype)

    return pl.pallas_call(
        body, name=name, grid=(M // tm,),
        in_specs=[pl.BlockSpec((tm, D), lambda i: (i, 0)), pl.BlockSpec((1, D), lambda i: (0, 0))],
        out_specs=pl.BlockSpec((tm, D), lambda i: (i, 0)),
        out_shape=jax.ShapeDtypeStruct((M, D), BF16),
        compiler_params=_cparams(("parallel",)),
    )(x, g.reshape(1, D))


def _rms_bwd(x, g, dn, dres, *, name, tm=512):
    M, D = x.shape
    tm = min(tm, M)

    def body(x_ref, g_ref, dn_ref, dres_ref, dx_ref, dg_ref):
        @pl.when(pl.program_id(0) == 0)
        def _():
            dg_ref[...] = jnp.zeros_like(dg_ref)

        xf = x_ref[...]
        r = lax.rsqrt(jnp.mean(xf * xf, axis=-1, keepdims=True) + EPS)
        xh = xf * r
        dn_ = dn_ref[...].astype(F32)
        dg_ref[...] += jnp.sum(dn_ * xh, axis=0, keepdims=True)
        dxh = dn_ * g_ref[...]
        dx = r * (dxh - xh * jnp.mean(dxh * xh, axis=-1, keepdims=True))
        dx_ref[...] = dres_ref[...] + dx

    row = pl.BlockSpec((tm, D), lambda i: (i, 0))
    vec = pl.BlockSpec((1, D), lambda i: (0, 0))
    return pl.pallas_call(
        body, name=name, grid=(M // tm,),
        in_specs=[row, vec, row, row], out_specs=[row, vec],
        out_shape=[jax.ShapeDtypeStruct((M, D), F32), jax.ShapeDtypeStruct((1, D), F32)],
        compiler_params=_cparams(("arbitrary",)),
    )(x, g.reshape(1, D), dn, dres)


def _loss_head(h, g, tgt, *, name, tm=512):
    M, D = h.shape
    tm = min(tm, M)

    def body(h_ref, g_ref, t_ref, loss_ref, dh_ref, dg_ref):
        @pl.when(pl.program_id(0) == 0)
        def _():
            dg_ref[...] = jnp.zeros_like(dg_ref)
            loss_ref[...] = jnp.zeros_like(loss_ref)

        xf = h_ref[...]
        r = lax.rsqrt(jnp.mean(xf * xf, axis=-1, keepdims=True) + EPS)
        xh = xf * r
        err = xh * g_ref[...] - t_ref[...]
        part = jnp.sum(jnp.mean(err * err, axis=-1, keepdims=True), axis=0, keepdims=True)
        loss_ref[...] += 0.5 * part
        dy = err * (1.0 / D)
        dg_ref[...] += jnp.sum(dy * xh, axis=0, keepdims=True)
        dxh = dy * g_ref[...]
        dh_ref[...] = r * (dxh - xh * jnp.mean(dxh * xh, axis=-1, keepdims=True))

    row = pl.BlockSpec((tm, D), lambda i: (i, 0))
    vec = pl.BlockSpec((1, D), lambda i: (0, 0))
    one = pl.BlockSpec((1, 1), lambda i: (0, 0))
    return pl.pallas_call(
        body, name=name, grid=(M // tm,),
        in_specs=[row, vec, row], out_specs=[one, row, vec],
        out_shape=[jax.ShapeDtypeStruct((1, 1), F32), jax.ShapeDtypeStruct((M, D), F32),
                   jax.ShapeDtypeStruct((1, D), F32)],
        compiler_params=_cparams(("arbitrary",)),
    )(h, g.reshape(1, D), tgt)


HG_MID = HG_CHUNK // 2 - 1
EXP_CAP = 80.0


def _sigmoid(x):
    return 1.0 / (1.0 + jnp.exp(-x))


def _dot(a, b, dims, precision=None):
    return lax.dot_general(a, b, dims, preferred_element_type=F32, precision=precision)


def _bdot(a, b, form):
    return _dot(a.astype(BF16), b.astype(BF16), _DIMS[form])


def _hdot(a, b, form):
    return _dot(a, b, _DIMS[form], precision=lax.Precision.HIGHEST)


def _hgrn_chunk_common(hq, hf, lbv, tril, rid):
    sq = _sigmoid(hq)
    q = hq * sq
    sg = _sigmoid(hf)
    f = lbv + (1.0 - lbv) * sg
    k = (1.0 - lbv) * (1.0 - sg)
    g = jnp.log(f)
    b = _dot(tril, g, _DIMS["nn"], precision=lax.Precision.HIGHEST)
    bref = jnp.sum(jnp.where(rid == HG_MID, b, 0.0), axis=0, keepdims=True)
    bend = jnp.sum(jnp.where(rid == HG_CHUNK - 1, b, 0.0), axis=0, keepdims=True)
    eb = jnp.exp(b)
    e1 = jnp.exp(jnp.minimum(b - bref, EXP_CAP))
    e2 = jnp.exp(jnp.minimum(bref - b, EXP_CAP))
    e3 = jnp.exp(bend - b)
    return sq, q, sg, f, k, bend, eb, e1, e2, e3


def _hgrn_fwd(proj, lb, gnorm, *, name, T=512):
    S = proj.shape[0]
    T = min(T, S)
    nch = T // HG_CHUNK
    C = HG_CHUNK

    def body(hq_ref, hf_ref, hi_ref, hg_ref, lb_ref, gn_ref, o_ref, oa_ref, st_ref, state):
        @pl.when(pl.program_id(1) == 0)
        def _():
            state[...] = jnp.zeros_like(state)

        lbv = lb_ref[...]
        gn = gn_ref[...]
        row = lax.broadcasted_iota(jnp.int32, (C, C), 0)
        col = lax.broadcasted_iota(jnp.int32, (C, C), 1)
        causal = row >= col
        tril = causal.astype(F32)
        rid = lax.broadcasted_iota(jnp.int32, (C, HG_DK), 0)
        for c in range(nch):
            sl = pl.ds(c * C, C)
            hq, hf, v, hg = hq_ref[sl, :], hf_ref[sl, :], hi_ref[sl, :], hg_ref[sl, :]
            _, q, _, _, k, bend, eb, e1, e2, e3 = _hgrn_chunk_common(hq, hf, lbv, tril, rid)
            st = state[...]
            st_ref[0, c] = st
            o = _hdot(q * eb, st, "nt")
            a = jnp.where(causal, _hdot(q * e1, k * e2, "nt"), 0.0)
            o = o + _hdot(a, v, "nn")
            state[...] = st * jnp.exp(bend) + _hdot(v, k * e3, "tn")
            o_ref[sl, :] = o
            r = lax.rsqrt(jnp.mean(o * o, axis=-1, keepdims=True) + EPS)
            oa_ref[sl, :] = (o * r * gn * (hg * _sigmoid(hg))).astype(oa_ref.dtype)

    def grp(gidx):
        return pl.BlockSpec((T, 128), lambda h, t: (t, gidx * 8 + h))

    return pl.pallas_call(
        body, name=name, grid=(HG_HEADS, S // T),
        in_specs=[grp(0), grp(1), grp(2), grp(3),
                  pl.BlockSpec((1, 128), lambda h, t: (0, h)), pl.BlockSpec((1, 128), lambda h, t: (0, 0))],
        out_specs=[pl.BlockSpec((T, 128), lambda h, t: (t, h)), pl.BlockSpec((T, 128), lambda h, t: (t, h)),
                   pl.BlockSpec((1, nch, HG_DV, HG_DK), lambda h, t: (h, t, 0, 0))],
        out_shape=[jax.ShapeDtypeStruct((S, HG_HEADS * HG_DV), F32), jax.ShapeDtypeStruct((S, HG_HEADS * HG_DV), BF16),
                   jax.ShapeDtypeStruct((HG_HEADS, S // C, HG_DV, HG_DK), F32)],
        scratch_shapes=[pltpu.VMEM((HG_DV, HG_DK), F32)],
        compiler_params=_cparams(("parallel", "arbitrary")),
    )(proj, proj, proj, proj, lb, gnorm)


def _hgrn_bwd(proj, lb, gnorm, o, states, doa, *, name, T=512):
    S = proj.shape[0]
    T = min(T, S)
    nch = T // HG_CHUNK
    C = HG_CHUNK
    nT = S // T

    def body(hq_ref, hf_ref, hi_ref, hg_ref, lb_ref, gn_ref, o_ref, st_ref, doa_ref,
             dhq_ref, dhf_ref, dhi_ref, dhg_ref, dlb_ref, dgn_ref, dstate):
        @pl.when(pl.program_id(1) == 0)
        def _():
            dstate[...] = jnp.zeros_like(dstate)
            dlb_ref[...] = jnp.zeros_like(dlb_ref)
            dgn_ref[...] = jnp.zeros_like(dgn_ref)

        lbv = lb_ref[...]
        gn = gn_ref[...]
        row = lax.broadcasted_iota(jnp.int32, (C, C), 0)
        col = lax.broadcasted_iota(jnp.int32, (C, C), 1)
        causal = row >= col
        tril = causal.astype(F32)
        triu = (row <= col).astype(F32)
        rid = lax.broadcasted_iota(jnp.int32, (C, HG_DK), 0)
        for c in reversed(range(nch)):
            sl = pl.ds(c * C, C)
            hq, hf, v, hg = hq_ref[sl, :], hf_ref[sl, :], hi_ref[sl, :], hg_ref[sl, :]
            sq, q, sg, f, k, bend, eb, e1, e2, e3 = _hgrn_chunk_common(hq, hf, lbv, tril, rid)
            qi, qp, kp, kend = q * eb, q * e1, k * e2, k * e3
            st0 = st_ref[0, c]
            ov = o_ref[sl, :]
            r = lax.rsqrt(jnp.mean(ov * ov, axis=-1, keepdims=True) + EPS)
            xh = ov * r
            sgg = _sigmoid(hg)
            d_oa = doa_ref[sl, :].astype(F32)
            dz = d_oa * (hg * sgg)
            dhg_ref[sl, :] = (d_oa * (xh * gn) * (sgg * (1.0 + hg * (1.0 - sgg)))).astype(dhg_ref.dtype)
            dgn_ref[0] += jnp.sum(dz * xh, axis=0, keepdims=True)
            dxh = dz * gn
            do = r * (dxh - xh * jnp.mean(dxh * xh, axis=-1, keepdims=True))
            ds1 = dstate[...]
            dqi = _hdot(do, st0, "nn")
            a = jnp.where(causal, _hdot(qp, kp, "nt"), 0.0)
            da = jnp.where(causal, _hdot(do, v, "nt"), 0.0)
            dv = _hdot(a, do, "tn") + _hdot(kend, ds1, "nt")
            dqp = _hdot(da, kp, "nn")
            dkp = _hdot(da, qp, "tn")
            dkend = _hdot(v, ds1, "nn")
            dq = dqi * eb + dqp * e1
            dk = dkp * e2 + dkend * e3
            db = dqi * qi + dqp * qp - dkp * kp - dkend * kend
            dbend = (jnp.sum(dkend * kend, axis=0, keepdims=True)
                     + jnp.exp(bend) * jnp.sum(ds1 * st0, axis=0, keepdims=True))
            db = db + jnp.where(rid == C - 1, dbend, 0.0)
            dg = _dot(triu, db, _DIMS["nn"], precision=lax.Precision.HIGHEST)
            df = dg / f - dk
            dlb_ref[...] += jnp.sum(df * (1.0 - sg), axis=0, keepdims=True)
            dhf_ref[sl, :] = (df * (1.0 - lbv) * sg * (1.0 - sg)).astype(dhf_ref.dtype)
            dhq_ref[sl, :] = (dq * (sq * (1.0 + hq * (1.0 - sq)))).astype(dhq_ref.dtype)
            dhi_ref[sl, :] = dv.astype(dhi_ref.dtype)
            dstate[...] = ds1 * jnp.exp(bend) + _hdot(do, qi, "tn")

    def grp(gidx):
        return pl.BlockSpec((T, 128), lambda h, t: (nT - 1 - t, gidx * 8 + h))

    tok = pl.BlockSpec((T, 128), lambda h, t: (nT - 1 - t, h))
    big = jax.ShapeDtypeStruct((S, HG_HEADS * HG_DV), BF16)
    return pl.pallas_call(
        body, name=name, grid=(HG_HEADS, nT),
        in_specs=[grp(0), grp(1), grp(2), grp(3),
                  pl.BlockSpec((1, 128), lambda h, t: (0, h)), pl.BlockSpec((1, 128), lambda h, t: (0, 0)),
                  tok, pl.BlockSpec((1, nch, HG_DV, HG_DK), lambda h, t: (h, nT - 1 - t, 0, 0)), tok],
        out_specs=[tok, tok, tok, tok, pl.BlockSpec((1, 128), lambda h, t: (0, h)),
                   pl.BlockSpec((1, 1, 128), lambda h, t: (h, 0, 0))],
        out_shape=[big, big, big, big, jax.ShapeDtypeStruct((1, HG_HEADS * HG_DK), F32),
                   jax.ShapeDtypeStruct((HG_HEADS, 1, HG_DV), F32)],
        scratch_shapes=[pltpu.VMEM((HG_DV, HG_DK), F32)],
        compiler_params=_cparams(("parallel", "arbitrary")),
    )(proj, proj, proj, proj, lb, gnorm, o, states, doa)


NEG = -1e30
FOX_SCALE = FOX_DH ** -0.5
FOX_PAIRS = FOX_HEADS // 2


def _fox_gate_fwd(ff, bias, *, name, T=512):
    S = ff.shape[0]
    T = min(T, S)

    def body(ff_ref, b_ref, c_ref, carry):
        @pl.when(pl.program_id(0) == 0)
        def _():
            carry[...] = jnp.zeros_like(carry)

        z = ff_ref[...] + b_ref[...]
        logf = jnp.minimum(z, 0.0) - jnp.log(1.0 + jnp.exp(-jnp.abs(z)))
        row = lax.broadcasted_iota(jnp.int32, (T, T), 0)
        col = lax.broadcasted_iota(jnp.int32, (T, T), 1)
        c = _dot((row >= col).astype(F32), logf, _DIMS["nn"], precision=lax.Precision.HIGHEST) + carry[...]
        c_ref[...] = c
        carry[...] = c[T - 1:T, :]

    return pl.pallas_call(
        body, name=name, grid=(S // T,),
        in_specs=[pl.BlockSpec((T, 128), lambda i: (i, 0)), pl.BlockSpec((1, 128), lambda i: (0, 0))],
        out_specs=pl.BlockSpec((T, 128), lambda i: (i, 0)),
        out_shape=jax.ShapeDtypeStruct((S, 128), F32),
        scratch_shapes=[pltpu.VMEM((1, 128), F32)],
        compiler_params=_cparams(("arbitrary",)),
    )(ff, bias)


def _fox_gate_bwd(ff, bias, dcs, *, name, T=512):
    S = ff.shape[0]
    T = min(T, S)
    nT = S // T

    def body(ff_ref, b_ref, d_ref, dff_ref, db_ref, carry):
        @pl.when(pl.program_id(0) == 0)
        def _():
            carry[...] = jnp.zeros_like(carry)
            db_ref[...] = jnp.zeros_like(db_ref)

        row = lax.broadcasted_iota(jnp.int32, (T, T), 0)
        col = lax.broadcasted_iota(jnp.int32, (T, T), 1)
        dlogf = carry[...] - _dot((row <= col).astype(F32), d_ref[...], _DIMS["nn"], precision=lax.Precision.HIGHEST)
        carry[...] = dlogf[0:1, :]
        dff = dlogf * (1.0 - _sigmoid(ff_ref[...] + b_ref[...]))
        dff_ref[...] = dff.astype(dff_ref.dtype)
        db_ref[...] += jnp.sum(dff, axis=0, keepdims=True)

    rev = pl.BlockSpec((T, 128), lambda i: (nT - 1 - i, 0))
    vec = pl.BlockSpec((1, 128), lambda i: (0, 0))
    return pl.pallas_call(
        body, name=name, grid=(nT,),
        in_specs=[rev, vec, rev], out_specs=[rev, vec],
        out_shape=[jax.ShapeDtypeStruct((S, 128), BF16), jax.ShapeDtypeStruct((1, 128), F32)],
        scratch_shapes=[pltpu.VMEM((1, 128), F32)],
        compiler_params=_cparams(("arbitrary",)),
    )(ff, bias, dcs)


def _fox_logits(q, k, cc, cr, qi, ki, tq, tk):
    s = _bdot(q, k, "nt") * FOX_SCALE + cc - cr
    qpos = qi * tq + lax.broadcasted_iota(jnp.int32, (tq, tk), 0)
    kpos = ki * tk + lax.broadcasted_iota(jnp.int32, (tq, tk), 1)
    return jnp.where(kpos <= qpos, s, NEG)


def _fox_fwd(proj, ccol, crow, *, name, tq=512, tk=512):
    S = proj.shape[0]
    tq, tk = min(tq, S), min(tk, S)

    def body(q_ref, k_ref, v_ref, cc_ref, cr_ref, o_ref, lse_ref, m_s, l_s, acc_s):
        qi, ki = pl.program_id(1), pl.program_id(2)

        @pl.when(ki == 0)
        def _():
            m_s[...] = jnp.full_like(m_s, NEG)
            l_s[...] = jnp.zeros_like(l_s)
            acc_s[...] = jnp.zeros_like(acc_s)

        @pl.when(ki <= qi)
        def _():
            for hh in range(2):
                ls = slice(hh * FOX_DH, (hh + 1) * FOX_DH)
                s = _fox_logits(q_ref[:, ls], k_ref[:, ls], cc_ref[0, :, hh:hh + 1], cr_ref[0, hh:hh + 1, :], qi, ki, tq, tk)
                m_old = m_s[hh]
                m_new = jnp.maximum(m_old, jnp.max(s, axis=-1, keepdims=True))
                p = jnp.exp(s - m_new)
                alpha = jnp.exp(m_old - m_new)
                l_s[hh] = alpha * l_s[hh] + jnp.sum(p, axis=-1, keepdims=True)
                p_hi = p.astype(BF16)
                p_lo = (p - p_hi.astype(F32)).astype(BF16)
                vv = v_ref[:, ls].astype(BF16)
                acc_s[hh] = alpha * acc_s[hh] + _bdot(p_hi, vv, "nn") + _bdot(p_lo, vv, "nn")
                m_s[hh] = m_new

        @pl.when(ki == qi)
        def _():
            for hh in range(2):
                o_ref[:, hh * FOX_DH:(hh + 1) * FOX_DH] = acc_s[hh] / l_s[hh]
                lse_ref[0, :, hh:hh + 1] = m_s[hh] + jnp.log(l_s[hh])

    qspec = pl.BlockSpec((tq, 128), lambda p, i, j: (i, 32 + p))
    kspec = pl.BlockSpec((tk, 128), lambda p, i, j: (jnp.minimum(j, i), 40 + p))
    vspec = pl.BlockSpec((tk, 128), lambda p, i, j: (jnp.minimum(j, i), 48 + p))
    ccs = pl.BlockSpec((1, tq, 2), lambda p, i, j: (p, i, 0))
    crs = pl.BlockSpec((1, 2, tk), lambda p, i, j: (p, 0, jnp.minimum(j, i)))
    return pl.pallas_call(
        body, name=name, grid=(FOX_PAIRS, S // tq, S // tk),
        in_specs=[qspec, kspec, vspec, ccs, crs],
        out_specs=[pl.BlockSpec((tq, 128), lambda p, i, j: (i, p)), ccs],
        out_shape=[jax.ShapeDtypeStruct((S, FOX_HEADS * FOX_DH), F32), jax.ShapeDtypeStruct((FOX_PAIRS, S, 2), F32)],
        scratch_shapes=[pltpu.VMEM((2, tq, 1), F32), pltpu.VMEM((2, tq, 1), F32), pltpu.VMEM((2, tq, FOX_DH), F32)],
        compiler_params=_cparams(("parallel", "parallel", "arbitrary")),
    )(proj, proj, proj, ccol, crow)


def _fox_bwd_dq(proj, ccol, crow, o, lse, do, *, name, tq=512, tk=512):
    S = proj.shape[0]
    tq, tk = min(tq, S), min(tk, S)

    def body(q_ref, k_ref, v_ref, cc_ref, cr_ref, o_ref, lse_ref, do_ref, dq_ref, dl_ref, acc_s):
        qi, ki = pl.program_id(1), pl.program_id(2)

        @pl.when(ki == 0)
        def _():
            acc_s[...] = jnp.zeros_like(acc_s)
            for hh in range(2):
                ls = slice(hh * FOX_DH, (hh + 1) * FOX_DH)
                dl_ref[0, :, hh:hh + 1] = jnp.sum(do_ref[:, ls].astype(F32) * o_ref[:, ls], axis=-1, keepdims=True)

        @pl.when(ki <= qi)
        def _():
            for hh in range(2):
                ls = slice(hh * FOX_DH, (hh + 1) * FOX_DH)
                s = _fox_logits(q_ref[:, ls], k_ref[:, ls], cc_ref[0, :, hh:hh + 1], cr_ref[0, hh:hh + 1, :], qi, ki, tq, tk)
                p = jnp.exp(s - lse_ref[0, :, hh:hh + 1])
                dp = _bdot(do_ref[:, ls], v_ref[:, ls], "nt")
                ds = p * (dp - dl_ref[0, :, hh:hh + 1])
                acc_s[hh] += _bdot(ds, k_ref[:, ls], "nn")

        @pl.when(ki == qi)
        def _():
            for hh in range(2):
                dq_ref[:, hh * FOX_DH:(hh + 1) * FOX_DH] = (acc_s[hh] * FOX_SCALE).astype(dq_ref.dtype)

    qspec = pl.BlockSpec((tq, 128), lambda p, i, j: (i, 32 + p))
    kspec = pl.BlockSpec((tk, 128), lambda p, i, j: (jnp.minimum(j, i), 40 + p))
    vspec = pl.BlockSpec((tk, 128), lambda p, i, j: (jnp.minimum(j, i), 48 + p))
    ccs = pl.BlockSpec((1, tq, 2), lambda p, i, j: (p, i, 0))
    crs = pl.BlockSpec((1, 2, tk), lambda p, i, j: (p, 0, jnp.minimum(j, i)))
    tok = pl.BlockSpec((tq, 128), lambda p, i, j: (i, p))
    return pl.pallas_call(
        body, name=name, grid=(FOX_PAIRS, S // tq, S // tk),
        in_specs=[qspec, kspec, vspec, ccs, crs, tok, ccs, tok],
        out_specs=[tok, ccs],
        out_shape=[jax.ShapeDtypeStruct((S, FOX_HEADS * FOX_DH), BF16), jax.ShapeDtypeStruct((FOX_PAIRS, S, 2), F32)],
        scratch_shapes=[pltpu.VMEM((2, tq, FOX_DH), F32)],
        compiler_params=_cparams(("parallel", "parallel", "arbitrary")),
    )(proj, proj, proj, ccol, crow, o, lse, do)


def _fox_bwd_dkv(proj, ccol, crow, lse, delta, do, *, name, tq=512, tk=512):
    S = proj.shape[0]
    tq, tk = min(tq, S), min(tk, S)
    nq = S // tq

    def body(q_ref, k_ref, v_ref, cc_ref, cr_ref, lse_ref, dl_ref, do_ref, dk_ref, dv_ref, dcs_ref, dk_s, dv_s):
        ki, qi = pl.program_id(1), pl.program_id(2)

        @pl.when(qi == 0)
        def _():
            dk_s[...] = jnp.zeros_like(dk_s)
            dv_s[...] = jnp.zeros_like(dv_s)
            dcs_ref[...] = jnp.zeros_like(dcs_ref)

        @pl.when(qi >= ki)
        def _():
            for hh in range(2):
                ls = slice(hh * FOX_DH, (hh + 1) * FOX_DH)
                s = _fox_logits(q_ref[:, ls], k_ref[:, ls], cc_ref[0, :, hh:hh + 1], cr_ref[0, hh:hh + 1, :], qi, ki, tq, tk)
                p = jnp.exp(s - lse_ref[0, :, hh:hh + 1])
                dp = _bdot(do_ref[:, ls], v_ref[:, ls], "nt")
                ds = p * (dp - dl_ref[0, :, hh:hh + 1])
                dv_s[hh] += _bdot(p, do_ref[:, ls], "tn")
                dk_s[hh] += _bdot(ds, q_ref[:, ls], "tn")
                dcs_ref[0, hh:hh + 1, :] += jnp.sum(ds, axis=0, keepdims=True)

        @pl.when(qi == nq - 1)
        def _():
            for hh in range(2):
                dk_ref[:, hh * FOX_DH:(hh + 1) * FOX_DH] = (dk_s[hh] * FOX_SCALE).astype(dk_ref.dtype)
                dv_ref[:, hh * FOX_DH:(hh + 1) * FOX_DH] = dv_s[hh].astype(dv_ref.dtype)

    qspec = pl.BlockSpec((tq, 128), lambda p, j, i: (jnp.maximum(i, j), 32 + p))
    kspec = pl.BlockSpec((tk, 128), lambda p, j, i: (j, 40 + p))
    vspec = pl.BlockSpec((tk, 128), lambda p, j, i: (j, 48 + p))
    ccs = pl.BlockSpec((1, tq, 2), lambda p, j, i: (p, jnp.maximum(i, j), 0))
    crs = pl.BlockSpec((1, 2, tk), lambda p, j, i: (p, 0, j))
    dos = pl.BlockSpec((tq, 128), lambda p, j, i: (jnp.maximum(i, j), p))
    ktok = pl.BlockSpec((tk, 128), lambda p, j, i: (j, p))
    big = jax.ShapeDtypeStruct((S, FOX_HEADS * FOX_DH), BF16)
    return pl.pallas_call(
        body, name=name, grid=(FOX_PAIRS, S // tk, nq),
        in_specs=[qspec, kspec, vspec, ccs, crs, ccs, ccs, dos],
        out_specs=[ktok, ktok, crs],
        out_shape=[big, big, jax.ShapeDtypeStruct((FOX_PAIRS, 2, S), F32)],
        scratch_shapes=[pltpu.VMEM((2, tk, FOX_DH), F32), pltpu.VMEM((2, tk, FOX_DH), F32)],
        compiler_params=_cparams(("parallel", "parallel", "arbitrary")),
    )(proj, proj, proj, ccol, crow, lse, delta, do)


AUG = FOX_DH


def _split3(x):
    a = x.astype(BF16).astype(F32)
    r = x - a
    b = r.astype(BF16).astype(F32)
    return a, b, r - b


def _lane_fill(lane, base, pieces, start):
    for i, pc in enumerate(pieces):
        base = jnp.where(lane == start + i, pc, base)
    return base


def _fox_prep(proj, c_tok, *, name, T=512):
    S = proj.shape[0]
    T = min(T, S)

    def body(q_ref, k_ref, v_ref, c_ref, qa_ref, ka_ref, va_ref):
        pair = pl.program_id(0)
        lane = lax.broadcasted_iota(jnp.int32, (T, 128), 1)
        c = c_ref[...]
        ones3 = jnp.where((lane >= AUG) & (lane < AUG + 3), 1.0, 0.0)
        for hh in range(2):
            ch = jnp.sum(jnp.where(lane == 2 * pair + hh, c, 0.0), axis=-1, keepdims=True)
            c1, c2, c3 = _split3(ch)
            q, k, v = q_ref[...], k_ref[...], v_ref[...]
            if hh == 1:
                q, k, v = (pltpu.roll(t, 64, 1) for t in (q, k, v))
            aug_q = _lane_fill(lane, jnp.where((lane >= AUG + 3) & (lane < AUG + 6), 1.0, 0.0), (c1, c2, c3), AUG)
            aug_k = _lane_fill(lane, ones3, (-c1, -c2, -c3), AUG + 3)
            qa_ref[hh] = jnp.where(lane < AUG, q * FOX_SCALE, aug_q).astype(BF16)
            ka_ref[hh] = jnp.where(lane < AUG, k, aug_k).astype(BF16)
            va_ref[hh] = jnp.where(lane < AUG, v, ones3).astype(BF16)

    def grp(g):
        return pl.BlockSpec((T, 128), lambda p, t: (t, g * 8 + p))

    hm = pl.BlockSpec((2, T, 128), lambda p, t: (p, t, 0))
    out = jax.ShapeDtypeStruct((FOX_HEADS, S, 128), BF16)
    return pl.pallas_call(
        body, name=name, grid=(FOX_PAIRS, S // T),
        in_specs=[grp(4), grp(5), grp(6), pl.BlockSpec((T, 128), lambda p, t: (t, 0))],
        out_specs=[hm, hm, hm], out_shape=[out, out, out],
        compiler_params=_cparams(("parallel", "parallel")),
    )(proj, proj, proj, c_tok)


def _pair_lanes(lane, a0, a1):
    return jnp.where(lane < AUG, a0, pltpu.roll(a1, 64, 1))


def _fox_fwd2(qa, ka, va, *, name, tb=512):
    S = qa.shape[1]
    tb = min(tb, S)

    def body(qa_ref, ka_ref, va_ref, o_ref, qb_ref, m_s, acc_s):
        qi, ki = pl.program_id(1), pl.program_id(2)

        @pl.when(ki == 0)
        def _():
            m_s[...] = jnp.full_like(m_s, NEG)
            acc_s[...] = jnp.zeros_like(acc_s)

        def step(masked):
            for hh in range(2):
                s = _dot(qa_ref[hh], ka_ref[hh], _DIMS["nt"])
                if masked:
                    row = lax.broadcasted_iota(jnp.int32, (tb, tb), 0)
                    col = lax.broadcasted_iota(jnp.int32, (tb, tb), 1)
                    s = jnp.where(col <= row, s, NEG)
                m_old = m_s[hh]
                m_new = jnp.maximum(m_old, jnp.max(s, axis=-1, keepdims=True))
                p = jnp.exp(s - m_new)
                p_hi = p.astype(BF16)
                p_lo = (p - p_hi.astype(F32)).astype(BF16)
                vv = va_ref[hh]
                acc_s[hh] = (jnp.exp(m_old - m_new) * acc_s[hh]
                             + _dot(p_hi, vv, _DIMS["nn"]) + _dot(p_lo, vv, _DIMS["nn"]))
                m_s[hh] = m_new

        @pl.when(ki < qi)
        def _():
            step(False)

        @pl.when(ki == qi)
        def _():
            step(True)
            lane = lax.broadcasted_iota(jnp.int32, (tb, 128), 1)
            outs = []
            for hh in range(2):
                acc = acc_s[hh]
                l = acc[:, AUG:AUG + 1]
                outs.append(acc / l)
                qf = qa_ref[hh].astype(F32)
                cb = qf[:, AUG:AUG + 1] + qf[:, AUG + 1:AUG + 2] + qf[:, AUG + 2:AUG + 3] - (m_s[hh] + jnp.log(l))
                qb_ref[hh] = _lane_fill(lane, qf, _split3(cb), AUG).astype(BF16)
            o_ref[...] = _pair_lanes(lane, outs[0], outs[1])

    qs = pl.BlockSpec((2, tb, 128), lambda p, i, j: (p, i, 0))
    ks = pl.BlockSpec((2, tb, 128), lambda p, i, j: (p, jnp.minimum(j, i), 0))
    return pl.pallas_call(
        body, name=name, grid=(FOX_PAIRS, S // tb, S // tb),
        in_specs=[qs, ks, ks],
        out_specs=[pl.BlockSpec((tb, 128), lambda p, i, j: (i, p)), qs],
        out_shape=[jax.ShapeDtypeStruct((S, FOX_HEADS * FOX_DH), F32), jax.ShapeDtypeStruct((FOX_HEADS, S, 128), BF16)],
        scratch_shapes=[pltpu.VMEM((2, tb, 1), F32), pltpu.VMEM((2, tb, 128), F32)],
        compiler_params=_cparams(("parallel", "parallel", "arbitrary")),
    )(qa, ka, va)


def _fox_bwd_prep(o, do, *, name, T=512):
    S = o.shape[0]
    T = min(T, S)

    def body(o_ref, do_ref, dob_ref):
        lane = lax.broadcasted_iota(jnp.int32, (T, 128), 1)
        d = do_ref[...].astype(F32)
        prod = d * o_ref[...]
        for hh in range(2):
            mine = (lane < AUG) if hh == 0 else (lane >= AUG)
            delta = jnp.sum(jnp.where(mine, prod, 0.0), axis=-1, keepdims=True)
            dh = d if hh == 0 else pltpu.roll(d, 64, 1)
            dob_ref[hh] = _lane_fill(lane, jnp.where(lane < AUG, dh, 0.0), _split3(-delta), AUG).astype(BF16)

    tok = pl.BlockSpec((T, 128), lambda p, t: (t, p))
    return pl.pallas_call(
        body, name=name, grid=(FOX_PAIRS, S // T),
        in_specs=[tok, tok], out_specs=pl.BlockSpec((2, T, 128), lambda p, t: (p, t, 0)),
        out_shape=jax.ShapeDtypeStruct((FOX_HEADS, S, 128), BF16),
        compiler_params=_cparams(("parallel", "parallel")),
    )(o, do)


def _fox_bwd_dq2(qb, ka, va, dob, *, name, tb=512):
    S = qb.shape[1]
    tb = min(tb, S)
    nb = S // tb

    def body(qb_ref, ka_ref, va_ref, dob_ref, dq_ref, dcs_ref, acc_s):
        qi, ki = pl.program_id(1), pl.program_id(2)

        @pl.when(ki == 0)
        def _():
            acc_s[...] = jnp.zeros_like(acc_s)

        def step(masked):
            for hh in range(2):
                s = _dot(qb_ref[hh], ka_ref[hh], _DIMS["nt"])
                if masked:
                    row = lax.broadcasted_iota(jnp.int32, (tb, tb), 0)
                    col = lax.broadcasted_iota(jnp.int32, (tb, tb), 1)
                    s = jnp.where(col <= row, s, NEG)
                ds = jnp.exp(s) * _dot(dob_ref[hh], va_ref[hh], _DIMS["nt"])
                dcs_ref[0, 0, hh:hh + 1, :] = jnp.sum(ds, axis=0, keepdims=True)
                acc_s[hh] += _dot(ds.astype(BF16), ka_ref[hh], _DIMS["nn"])

        @pl.when(ki < qi)
        def _():
            step(False)

        @pl.when(ki == qi)
        def _():
            step(True)
            lane = lax.broadcasted_iota(jnp.int32, (tb, 128), 1)
            dq_ref[...] = (_pair_lanes(lane, acc_s[0], acc_s[1]) * FOX_SCALE).astype(dq_ref.dtype)

    qs = pl.BlockSpec((2, tb, 128), lambda p, i, j: (p, i, 0))
    ks = pl.BlockSpec((2, tb, 128), lambda p, i, j: (p, jnp.minimum(j, i), 0))
    return pl.pallas_call(
        body, name=name, grid=(FOX_PAIRS, nb, nb),
        in_specs=[qs, ks, ks, qs],
        out_specs=[pl.BlockSpec((tb, 128), lambda p, i, j: (i, p)),
                   pl.BlockSpec((1, 1, 2, tb), lambda p, i, j: (p, i, 0, jnp.minimum(j, i)))],
        out_shape=[jax.ShapeDtypeStruct((S, FOX_HEADS * FOX_DH), BF16),
                   jax.ShapeDtypeStruct((FOX_PAIRS, nb, 2, S), F32)],
        scratch_shapes=[pltpu.VMEM((2, tb, 128), F32)],
        compiler_params=_cparams(("parallel", "parallel", "arbitrary")),
    )(qb, ka, va, dob)


def _fox_bwd_dkv2(qb, ka, va, dob, *, name, tb=512):
    S = qb.shape[1]
    tb = min(tb, S)
    nb = S // tb

    def body(qb_ref, ka_ref, va_ref, dob_ref, dk_ref, dv_ref, dk_s, dv_s):
        ki, qi = pl.program_id(1), pl.program_id(2)

        @pl.when(qi == 0)
        def _():
            dk_s[...] = jnp.zeros_like(dk_s)
            dv_s[...] = jnp.zeros_like(dv_s)

        def step(masked):
            for hh in range(2):
                st = _dot(ka_ref[hh], qb_ref[hh], _DIMS["nt"])
                if masked:
                    row = lax.broadcasted_iota(jnp.int32, (tb, tb), 0)
                    col = lax.broadcasted_iota(jnp.int32, (tb, tb), 1)
                    st = jnp.where(row <= col, st, NEG)
                pt = jnp.exp(st)
                dst = pt * _dot(va_ref[hh], dob_ref[hh], _DIMS["nt"])
                dv_s[hh] += _dot(pt.astype(BF16), dob_ref[hh], _DIMS["nn"])
                dk_s[hh] += _dot(dst.astype(BF16), qb_ref[hh], _DIMS["nn"])

        @pl.when(qi > ki)
        def _():
            step(False)

        @pl.when(qi == ki)
        def _():
            step(True)

        @pl.when(qi == nb - 1)
        def _():
            lane = lax.broadcasted_iota(jnp.int32, (tb, 128), 1)
            dk_ref[...] = _pair_lanes(lane, dk_s[0], dk_s[1]).astype(dk_ref.dtype)
            dv_ref[...] = _pair_lanes(lane, dv_s[0], dv_s[1]).astype(dv_ref.dtype)

    ks = pl.BlockSpec((2, tb, 128), lambda p, j, i: (p, j, 0))
    qs = pl.BlockSpec((2, tb, 128), lambda p, j, i: (p, jnp.maximum(i, j), 0))
    tok = pl.BlockSpec((tb, 128), lambda p, j, i: (j, p))
    big = jax.ShapeDtypeStruct((S, FOX_HEADS * FOX_DH), BF16)
    return pl.pallas_call(
        body, name=name, grid=(FOX_PAIRS, nb, nb),
        in_specs=[qs, ks, ks, qs], out_specs=[tok, tok], out_shape=[big, big],
        scratch_shapes=[pltpu.VMEM((2, tb, 128), F32), pltpu.VMEM((2, tb, 128), F32)],
        compiler_params=_cparams(("parallel", "parallel", "arbitrary")),
    )(qb, ka, va, dob)


def _merge_fwd(proj, pa, pb, *, name, T=512):
    S, D = pa.shape
    T = min(T, S)

    def body(ga_ref, gb_ref, pa_ref, pb_ref, m_ref):
        m_ref[...] = (_sigmoid(ga_ref[...]) * pa_ref[...] + _sigmoid(gb_ref[...]) * pb_ref[...]).astype(m_ref.dtype)

    tok = pl.BlockSpec((T, D), lambda i: (i, 0))
    return pl.pallas_call(
        body, name=name, grid=(S // T,),
        in_specs=[pl.BlockSpec((T, D), lambda i: (i, 7)), pl.BlockSpec((T, D), lambda i: (i, 8)), tok, tok],
        out_specs=tok, out_shape=jax.ShapeDtypeStruct((S, D), BF16),
        compiler_params=_cparams(("parallel",)),
    )(proj, proj, pa, pb)


def _merge_bwd(proj, pa, pb, dm, *, name, T=512):
    S, D = pa.shape
    T = min(T, S)

    def body(ga_ref, gb_ref, pa_ref, pb_ref, dm_ref, dpa_ref, dpb_ref, dga_ref, dgb_ref):
        dm_ = dm_ref[...]
        sa, sb = _sigmoid(ga_ref[...]), _sigmoid(gb_ref[...])
        dpa_ref[...] = (dm_ * sa).astype(BF16)
        dpb_ref[...] = (dm_ * sb).astype(BF16)
        dga_ref[...] = (dm_ * pa_ref[...] * sa * (1.0 - sa)).astype(BF16)
        dgb_ref[...] = (dm_ * pb_ref[...] * sb * (1.0 - sb)).astype(BF16)

    tok = pl.BlockSpec((T, D), lambda i: (i, 0))
    big = jax.ShapeDtypeStruct((S, D), BF16)
    return pl.pallas_call(
        body, name=name, grid=(S // T,),
        in_specs=[pl.BlockSpec((T, D), lambda i: (i, 7)), pl.BlockSpec((T, D), lambda i: (i, 8)), tok, tok, tok],
        out_specs=[tok, tok, tok, tok], out_shape=[big, big, big, big],
        compiler_params=_cparams(("parallel",)),
    )(proj, proj, pa, pb, dm)


INV_SQRT2 = 0.7071067811865476
INV_SQRT2PI = 0.3989422804014327


def _shifted(u, prev, rid):
    m1 = jnp.where(rid == 0, prev[7:8, :], pltpu.roll(u, 1, 0))
    m2 = jnp.where(rid == 0, prev[6:7, :], jnp.where(rid == 1, prev[7:8, :], pltpu.roll(u, 2, 0)))
    return m1, m2


def _conv_acc(u, prev, w_ref, b_ref, rid):
    m1, m2 = _shifted(u, prev, rid)
    return b_ref[...] + w_ref[0:1, :] * m2 + w_ref[1:2, :] * m1 + w_ref[2:3, :] * u, m1, m2


def _convglu_fwd(ug, uv, wg, wv, bg, bv, *, name, T=512, tc=256):
    S, F = ug.shape
    T = min(T, S)

    def body(ug_ref, uv_ref, wg_ref, wv_ref, bg_ref, bv_ref, a_ref, pg, pv):
        @pl.when(pl.program_id(1) == 0)
        def _():
            pg[...] = jnp.zeros_like(pg)
            pv[...] = jnp.zeros_like(pv)

        rid = lax.broadcasted_iota(jnp.int32, (T, tc), 0)
        g_, v_ = ug_ref[...], uv_ref[...]
        accg, _, _ = _conv_acc(g_, pg[...], wg_ref, bg_ref, rid)
        accv, _, _ = _conv_acc(v_, pv[...], wv_ref, bv_ref, rid)
        gel = 0.5 * accg * (1.0 + lax.erf(accg * INV_SQRT2))
        a_ref[...] = (gel * accv).astype(a_ref.dtype)
        pg[...] = g_[T - 8:T, :]
        pv[...] = v_[T - 8:T, :]

    tok = pl.BlockSpec((T, tc), lambda j, t: (t, j))
    w3 = pl.BlockSpec((3, tc), lambda j, t: (0, j))
    b1 = pl.BlockSpec((1, tc), lambda j, t: (0, j))
    return pl.pallas_call(
        body, name=name, grid=(F // tc, S // T),
        in_specs=[tok, tok, w3, w3, b1, b1], out_specs=tok,
        out_shape=jax.ShapeDtypeStruct((S, F), BF16),
        scratch_shapes=[pltpu.VMEM((8, tc), F32), pltpu.VMEM((8, tc), F32)],
        compiler_params=_cparams(("parallel", "arbitrary")),
    )(ug, uv, wg, wv, bg, bv)


def _convglu_bwd_acc(ug, uv, wg, wv, bg, bv, da, *, name, T=512, tc=256):
    S, F = ug.shape
    T = min(T, S)

    def body(ug_ref, uv_ref, wg_ref, wv_ref, bg_ref, bv_ref, da_ref,
             dg_ref, dv_ref, dwg_ref, dwv_ref, dbg_ref, dbv_ref, pg, pv):
        @pl.when(pl.program_id(1) == 0)
        def _():
            pg[...] = jnp.zeros_like(pg)
            pv[...] = jnp.zeros_like(pv)
            for r in (dwg_ref, dwv_ref, dbg_ref, dbv_ref):
                r[...] = jnp.zeros_like(r)

        rid = lax.broadcasted_iota(jnp.int32, (T, tc), 0)
        g_, v_ = ug_ref[...], uv_ref[...]
        accg, g1, g2 = _conv_acc(g_, pg[...], wg_ref, bg_ref, rid)
        accv, v1, v2 = _conv_acc(v_, pv[...], wv_ref, bv_ref, rid)
        cdf = 0.5 * (1.0 + lax.erf(accg * INV_SQRT2))
        pdf = INV_SQRT2PI * jnp.exp(-0.5 * accg * accg)
        da_ = da_ref[...].astype(F32)
        dgate = da_ * accv * (cdf + accg * pdf)
        dval = da_ * (accg * cdf)
        dg_ref[...] = dgate.astype(dg_ref.dtype)
        dv_ref[...] = dval.astype(dv_ref.dtype)
        dbg_ref[...] += jnp.sum(dgate, axis=0, keepdims=True)
        dbv_ref[...] += jnp.sum(dval, axis=0, keepdims=True)
        for j, (sg_, sv_) in enumerate(((g2, v2), (g1, v1), (g_, v_))):
            dwg_ref[j:j + 1, :] += jnp.sum(dgate * sg_, axis=0, keepdims=True)
            dwv_ref[j:j + 1, :] += jnp.sum(dval * sv_, axis=0, keepdims=True)
        pg[...] = g_[T - 8:T, :]
        pv[...] = v_[T - 8:T, :]

    tok = pl.BlockSpec((T, tc), lambda j, t: (t, j))
    w3 = pl.BlockSpec((3, tc), lambda j, t: (0, j))
    b1 = pl.BlockSpec((1, tc), lambda j, t: (0, j))
    big = jax.ShapeDtypeStruct((S, F), BF16)
    return pl.pallas_call(
        body, name=name, grid=(F // tc, S // T),
        in_specs=[tok, tok, w3, w3, b1, b1, tok], out_specs=[tok, tok, w3, w3, b1, b1],
        out_shape=[big, big, jax.ShapeDtypeStruct((3, F), F32), jax.ShapeDtypeStruct((3, F), F32),
                   jax.ShapeDtypeStruct((1, F), F32), jax.ShapeDtypeStruct((1, F), F32)],
        scratch_shapes=[pltpu.VMEM((8, tc), F32), pltpu.VMEM((8, tc), F32)],
        compiler_params=_cparams(("parallel", "arbitrary")),
    )(ug, uv, wg, wv, bg, bv, da)


def _conv_bwd_u(dacc, w, *, name, T=512, tc=256):
    S, F = dacc.shape
    T = min(T, S)
    nT = S // T

    def body(d_ref, w_ref, du_ref, nxt):
        @pl.when(pl.program_id(1) == 0)
        def _():
            nxt[...] = jnp.zeros_like(nxt)

        rid = lax.broadcasted_iota(jnp.int32, (T, tc), 0)
        d = d_ref[...].astype(F32)
        nx = nxt[...]
        p1 = jnp.where(rid == T - 1, nx[0:1, :], pltpu.roll(d, T - 1, 0))
        p2 = jnp.where(rid == T - 1, nx[1:2, :], jnp.where(rid == T - 2, nx[0:1, :], pltpu.roll(d, T - 2, 0)))
        du_ref[...] = (w_ref[2:3, :] * d + w_ref[1:2, :] * p1 + w_ref[0:1, :] * p2).astype(du_ref.dtype)
        nxt[...] = d[0:8, :]

    tok = pl.BlockSpec((T, tc), lambda j, t: (nT - 1 - t, j))
    return pl.pallas_call(
        body, name=name, grid=(F // tc, nT),
        in_specs=[tok, pl.BlockSpec((3, tc), lambda j, t: (0, j))], out_specs=tok,
        out_shape=jax.ShapeDtypeStruct((S, F), BF16),
        scratch_shapes=[pltpu.VMEM((8, tc), F32)],
        compiler_params=_cparams(("parallel", "arbitrary")),
    )(dacc, w)


def _local_step(x, tgt, w, p):
    S = x.shape[0]
    mm = _matmul
    n1 = _rms_fwd(x, p["norm_mix"], name="rms1_fwd")
    proj = mm(n1, w["wm"], "nn", name="proj_main")
    ff = mm(n1, w["wff"], "nn", name="proj_ff")
    lb = _lb_fwd(p["hg_lb_logits"], name="lb_fwd")
    gnorm = p["hg_norm"].reshape(1, HG_DV)
    o_hg, oa, states = _hgrn_fwd(proj, lb, gnorm, name="hgrn_fwd")
    bias = jnp.pad(p["fox_f_bias"].reshape(1, FOX_HEADS), ((0, 0), (0, 128 - FOX_HEADS)))
    c = _fox_gate_fwd(ff, bias, name="fox_gate_fwd")
    qa, ka, va = _fox_prep(proj, c, name="fox_prep")
    ob, qb = _fox_fwd2(qa, ka, va, name="fox_fwd")
    pa = mm(oa, w["wa"], "nn", name="branch_a")
    pb = mm(ob, w["wb"], "nn", name="branch_b")
    merged = _merge_fwd(proj, pa, pb, name="merge_fwd")
    h1 = mm(merged, w["wo"], "nn", addend=x, name="mix_out")
    n2 = _rms_fwd(h1, p["norm_ffn"], name="rms2_fwd")
    ug = mm(n2, w["wug"], "nn", name="up_gate")
    uv = mm(n2, w["wuv"], "nn", name="up_val")
    a = _convglu_fwd(ug, uv, w["cwg"], w["cwv"], p["cbg"], p["cbv"], name="convglu_fwd")
    h2 = mm(a, w["wd"], "nn", addend=h1, name="ffn_down")
    loss, dh2, d_norm_final = _loss_head(h2, p["norm_final"], tgt, name="loss_head")
    da = mm(dh2, w["wd"], "nt", out_dtype=BF16, name="d_act")
    d_wd = mm(a, dh2, "tn", out_dtype=BF16, name="dw_down")
    daccg, daccv, d_cwg, d_cwv, d_cbg, d_cbv = _convglu_bwd_acc(
        ug, uv, w["cwg"], w["cwv"], p["cbg"], p["cbv"], da, name="convglu_bwd")
    dug = _conv_bwd_u(daccg, w["cwg"], name="conv_bwd_gate")
    duv = _conv_bwd_u(daccv, w["cwv"], name="conv_bwd_val")
    dn2 = mm(dug, w["wug"], "nt", name="dn2_gate")
    dn2 = mm(duv, w["wuv"], "nt", addend=dn2, name="dn2_val")
    d_wug = mm(n2, dug, "tn", out_dtype=BF16, name="dw_up_gate")
    d_wuv = mm(n2, duv, "tn", out_dtype=BF16, name="dw_up_val")
    dh1, d_norm_ffn = _rms_bwd(h1, p["norm_ffn"], dn2, dh2, name="rms2_bwd")
    dmerged = mm(dh1, w["wo"], "nt", name="d_merged")
    d_wo = mm(merged, dh1, "tn", out_dtype=BF16, name="dw_out")
    dpa, dpb, dga, dgb = _merge_bwd(proj, pa, pb, dmerged, name="merge_bwd")
    doa = mm(dpa, w["wa"], "nt", name="d_oa")
    dob = mm(dpb, w["wb"], "nt", out_dtype=BF16, name="d_ob")
    d_wa = mm(oa, dpa, "tn", out_dtype=BF16, name="dw_branch_a")
    d_wb = mm(ob, dpb, "tn", out_dtype=BF16, name="dw_branch_b")
    dhq, dhf, dhi, dhg, dlb, dgn8 = _hgrn_bwd(proj, lb, gnorm, o_hg, states, doa, name="hgrn_bwd")
    d_logits = _lb_bwd(p["hg_lb_logits"], dlb, name="lb_bwd")
    dob_hm = _fox_bwd_prep(ob, dob, name="fox_bwd_prep")
    dq, dcsp = _fox_bwd_dq2(qb, ka, va, dob_hm, name="fox_bwd_dq")
    dk, dv = _fox_bwd_dkv2(qb, ka, va, dob_hm, name="fox_bwd_dkv")
    nb = dcsp.shape[1]
    written = (jnp.arange(S) // (S // nb))[None, None, None, :] <= jnp.arange(nb)[None, :, None, None]
    dcs = jnp.sum(jnp.where(written, dcsp, 0.0), axis=1)
    dcs_tok = jnp.pad(dcs.reshape(FOX_HEADS, S).T, ((0, 0), (0, 128 - FOX_HEADS)))
    dff, dbias = _fox_gate_bwd(ff, bias, dcs_tok, name="fox_gate_bwd")
    dproj = jnp.concatenate([dhq, dhf, dhi, dhg, dq, dk, dv, dga, dgb], axis=1)
    dn1 = mm(dff, w["wff"], "nt", name="dn1_ff")
    dn1 = mm(dproj, w["wm"], "nt", addend=dn1, name="dn1_main")
    d_wm = mm(n1, dproj, "tn", out_dtype=BF16, name="dw_in_main")
    d_wff = mm(n1, dff, "tn", out_dtype=BF16, name="dw_in_ff")
    dx, d_norm_mix = _rms_bwd(x, p["norm_mix"], dn1, dh1, name="rms1_bwd")
    grads = dict(
        wm=d_wm, wff=d_wff, wa=d_wa, wb=d_wb, wo=d_wo, wug=d_wug, wuv=d_wuv, cwg=d_cwg, cwv=d_cwv, wd=d_wd,
        norm_mix=d_norm_mix.reshape(-1), fox_f_bias=dbias[0, :FOX_HEADS], hg_lb_logits=d_logits,
        hg_norm=jnp.sum(dgn8, axis=0).reshape(-1), norm_ffn=d_norm_ffn.reshape(-1), cbg=d_cbg, cbv=d_cbv,
        norm_final=d_norm_final.reshape(-1))
    return loss, dx, grads


MESH = pl.DeviceIdType.MESH
ANY = pl.BlockSpec(memory_space=pl.ANY)


def _all_gather(xs, *, name):
    def body(x_ref, out_ref, send_sems, recv_sems, local_sem):
        x, y, c = lax.axis_index("x"), lax.axis_index("y"), lax.axis_index("c")
        me, sibling = (x, y, c), (x, y, 1 - c)
        chips = [(1 - x, y), (x, 1 - y), (1 - x, 1 - y)]

        def rows(px, py, pc):
            return out_ref.at[4 * px + 2 * py + pc]

        def copy(k, block, to, src=None):
            return pltpu.make_async_remote_copy(
                src_ref=rows(*block) if src is None else src, dst_ref=rows(*block),
                send_sem=send_sems.at[k], recv_sem=recv_sems.at[k], device_id=to, device_id_type=MESH)

        mine = pltpu.make_async_copy(x_ref, rows(*me), local_sem)
        mine.start()
        first = [copy(0, me, sibling, src=x_ref)]
        first += [copy(1 + j, me, (*chip, c), src=x_ref) for j, chip in enumerate(chips)]
        for cp in first:
            cp.start()
        passed = [copy(4 + j, (*chip, c), sibling) for j, chip in enumerate(chips)]
        for j, chip in enumerate(chips):
            copy(1 + j, (*chip, c), me).wait_recv()
            passed[j].start()
        copy(0, sibling, me).wait_recv()
        for j, chip in enumerate(chips):
            copy(4 + j, (*chip, 1 - c), me).wait_recv()
        for cp in first + passed:
            cp.wait_send()
        mine.wait()

    return pl.pallas_call(
        body, name=name, in_specs=[ANY], out_specs=ANY,
        out_shape=jax.ShapeDtypeStruct((N_DEV,) + xs.shape, xs.dtype),
        scratch_shapes=[pltpu.SemaphoreType.DMA((7,)), pltpu.SemaphoreType.DMA((7,)), pltpu.SemaphoreType.DMA],
    )(xs)


def _exchange_blocks(g, *, name):
    def body(g_ref, out_ref, send_sems, recv_sems, local_sem):
        x, y, c = lax.axis_index("x"), lax.axis_index("y"), lax.axis_index("c")
        me = 4 * x + 2 * y + c
        mine = pltpu.make_async_copy(g_ref.at[me], out_ref.at[me], local_sem)
        mine.start()
        sends, recvs = [], []
        for k in range(1, N_DEV):
            px = 1 - x if k & 4 else x
            py = 1 - y if k & 2 else y
            pc = 1 - c if k & 1 else c
            p = 4 * px + 2 * py + pc
            sends.append(pltpu.make_async_remote_copy(
                src_ref=g_ref.at[p], dst_ref=out_ref.at[me], send_sem=send_sems.at[k - 1], recv_sem=recv_sems.at[k - 1],
                device_id=(px, py, pc), device_id_type=MESH))
            recvs.append(pltpu.make_async_remote_copy(
                src_ref=g_ref.at[p], dst_ref=out_ref.at[p], send_sem=send_sems.at[k - 1], recv_sem=recv_sems.at[k - 1],
                device_id=(px, py, pc), device_id_type=MESH))
        for cp in sends:
            cp.start()
        for cp in recvs:
            cp.wait_recv()
        for cp in sends:
            cp.wait_send()
        mine.wait()

    return pl.pallas_call(
        body, name=name, in_specs=[ANY], out_specs=ANY,
        out_shape=jax.ShapeDtypeStruct(g.shape, g.dtype),
        scratch_shapes=[pltpu.SemaphoreType.DMA((7,)), pltpu.SemaphoreType.DMA((7,)), pltpu.SemaphoreType.DMA],
    )(g)


def _adamw(parts, w, m, v, *, name, T=512):
    R, L = w.shape
    c1 = 1.0 / (1.0 - ADAM_B1 ** ADAM_STEP)
    c2 = 1.0 / (1.0 - ADAM_B2 ** ADAM_STEP)

    def body(p_ref, w_ref, m_ref, v_ref, g_ref, d_ref, nm_ref, nv_ref):
        g = p_ref[0]
        for s in range(1, N_DEV):
            g = g + p_ref[s]
        g_ref[...] = g
        nm = ADAM_B1 * m_ref[...] + (1.0 - ADAM_B1) * g
        nv = ADAM_B2 * v_ref[...] + (1.0 - ADAM_B2) * (g * g)
        nm_ref[...] = nm
        nv_ref[...] = nv
        d_ref[...] = -ADAM_LR * ((nm * c1) / (jnp.sqrt(nv * c2) + ADAM_EPS) + ADAM_WD * w_ref[...])

    blk = pl.BlockSpec((T, L), lambda i: (i, 0))
    out = jax.ShapeDtypeStruct((R, L), F32)
    return pl.pallas_call(
        body, name=name, grid=(R // T,),
        in_specs=[pl.BlockSpec((N_DEV, T, L), lambda i: (0, i, 0)), blk, blk, blk],
        out_specs=[blk, blk, blk, blk], out_shape=[out, out, out, out],
        compiler_params=_cparams(("parallel",)),
    )(parts, w, m, v)


D_IN = 9232
FF_LO, FF_HI = 7168, 7184
IN_SH, UP_SH, DOWN_SH = D_IN // N_DEV, 2 * D_FF // N_DEV, D_FF // N_DEV
SQ_SH = D_MODEL // N_DEV

BIG = [("w_in", (1, D_MODEL, IN_SH)), ("w_branch_a", (1, SQ_SH, D_MODEL)), ("w_branch_b", (1, SQ_SH, D_MODEL)),
       ("w_out", (1, SQ_SH, D_MODEL)), ("w_up", (1, D_MODEL, UP_SH)), ("conv_w", (1, 3, UP_SH)),
       ("w_down", (1, DOWN_SH, D_MODEL))]
SMALL = [("norm_mix", (1, D_MODEL)), ("fox_f_bias", (1, FOX_HEADS)), ("hg_lb_logits", (2, HG_HEADS * HG_DK)),
         ("hg_norm", (1, HG_DV)), ("norm_ffn", (1, D_MODEL)), ("conv_b", (1, 2 * D_FF)), ("norm_final", (D_MODEL,))]
NAMES = ["norm_mix", "w_in", "fox_f_bias", "hg_lb_logits", "hg_norm", "w_branch_a", "w_branch_b", "w_out",
         "norm_ffn", "w_up", "conv_w", "conv_b", "w_down", "norm_final"]


def _size(shape):
    n = 1
    for s in shape:
        n *= s
    return n


PACK_ROWS = 20992
GATHER_ROWS = 20800
assert sum(_size(s) for _, s in BIG + SMALL) <= PACK_ROWS * 128


def _pack_rows(flat_parts, rows):
    flat = jnp.concatenate(flat_parts, axis=-1)
    pad = rows * 128 - flat.shape[-1]
    flat = jnp.pad(flat, [(0, 0)] * (flat.ndim - 1) + [(0, pad)])
    return flat.reshape(flat.shape[:-1] + (rows, 128))


def _pack_shard(vals):
    return _pack_rows([vals[n].reshape(1, -1).astype(F32) for n, _ in BIG + SMALL], PACK_ROWS)[0]


def _unpack_shard(buf):
    flat = buf.reshape(-1)
    out, off = {}, 0
    for n, shape in BIG + SMALL:
        out[n] = flat[off:off + _size(shape)].reshape(shape)
        off += _size(shape)
    return out


def _cols_by_device(a, width):
    rows = a.shape[0]
    return a.reshape(rows, N_DEV, width).transpose(1, 0, 2).reshape(N_DEV, rows * width)


def _cols_from_devices(a, rows, width):
    return a.reshape(N_DEV, rows, width).transpose(1, 0, 2).reshape(rows, N_DEV * width)


def _pack_grads(g):
    w_in = jnp.concatenate([g["wm"][:, :FF_LO], g["wff"][:, :FOX_HEADS], g["wm"][:, FF_LO:]], axis=1)
    w_up = jnp.concatenate([g["wug"], g["wuv"]], axis=1)
    conv_w = jnp.concatenate([g["cwg"], g["cwv"]], axis=1)
    conv_b = jnp.concatenate([g["cbg"], g["cbv"]], axis=1)
    big = [_cols_by_device(w_in, IN_SH), g["wa"].reshape(N_DEV, -1), g["wb"].reshape(N_DEV, -1),
           g["wo"].reshape(N_DEV, -1), _cols_by_device(w_up, UP_SH), _cols_by_device(conv_w, UP_SH),
           g["wd"].reshape(N_DEV, -1)]
    small = [g["norm_mix"], g["fox_f_bias"], g["hg_lb_logits"], g["hg_norm"], g["norm_ffn"], conv_b, g["norm_final"]]
    small = [jnp.broadcast_to(s.reshape(1, -1), (N_DEV, s.size)) for s in small]
    return _pack_rows(big + small, PACK_ROWS)


def _gather_weights(w_in, w_a, w_b, w_o, w_up, conv_w, w_down):
    taps = lax.bitcast_convert_type(conv_w.reshape(3, UP_SH), BF16).reshape(1, -1)
    mats = [w_in, w_a, w_b, w_o, w_up, w_down]
    packed = _pack_rows([t.reshape(1, -1).astype(BF16) for t in mats] + [taps], GATHER_ROWS)[0]
    full = _all_gather(packed, name="gather_weights").reshape(N_DEV, -1)
    off = 0

    def take(n):
        nonlocal off
        piece = full[:, off:off + n]
        off += n
        return piece

    win = _cols_from_devices(take(D_MODEL * IN_SH), D_MODEL, IN_SH)
    wa = take(SQ_SH * D_MODEL).reshape(D_MODEL, D_MODEL)
    wb = take(SQ_SH * D_MODEL).reshape(D_MODEL, D_MODEL)
    wo = take(SQ_SH * D_MODEL).reshape(D_MODEL, D_MODEL)
    wup = _cols_from_devices(take(D_MODEL * UP_SH), D_MODEL, UP_SH)
    wd = take(DOWN_SH * D_MODEL).reshape(D_FF, D_MODEL)
    cw = lax.bitcast_convert_type(take(3 * UP_SH * 2).reshape(N_DEV, 3, UP_SH, 2), F32)
    cw = cw.transpose(1, 0, 2).reshape(3, 2 * D_FF)
    return dict(
        wm=jnp.concatenate([win[:, :FF_LO], win[:, FF_HI:]], axis=1),
        wff=jnp.pad(win[:, FF_LO:FF_HI], ((0, 0), (0, 128 - FOX_HEADS))),
        wa=wa, wb=wb, wo=wo, wug=wup[:, :D_FF], wuv=wup[:, D_FF:], cwg=cw[:, :D_FF], cwv=cw[:, D_FF:], wd=wd)


def _peer(k, x, y, c):
    return (1 - x if k & 4 else x, 1 - y if k & 2 else y, 1 - c if k & 1 else c)


def _gather_multi(shards, *, name):
    n = len(shards)

    def body(*refs):
        x_refs, out_refs = refs[:n], refs[n:2 * n]
        send_sems, recv_sems, local_sems = refs[2 * n:]
        x, y, c = lax.axis_index("x"), lax.axis_index("y"), lax.axis_index("c")
        me, sibling = (x, y, c), (x, y, 1 - c)
        chips = [(1 - x, y), (x, 1 - y), (1 - x, 1 - y)]

        def copy(t, k, block, to, src=None):
            slot = out_refs[t].at[4 * block[0] + 2 * block[1] + block[2]]
            return pltpu.make_async_remote_copy(
                src_ref=slot if src is None else src, dst_ref=slot,
                send_sem=send_sems.at[t, k], recv_sem=recv_sems.at[t, k], device_id=to, device_id_type=MESH)

        mine = [pltpu.make_async_copy(x_refs[t], out_refs[t].at[4 * x + 2 * y + c], local_sems.at[t]) for t in range(n)]
        for cp in mine:
            cp.start()
        first = []
        for t in range(n):
            first.append(copy(t, 0, me, sibling, src=x_refs[t]))
            first += [copy(t, 1 + j, me, (*chip, c), src=x_refs[t]) for j, chip in enumerate(chips)]
        for cp in first:
            cp.start()
        passed = []
        for j, chip in enumerate(chips):
            for t in range(n):
                copy(t, 1 + j, (*chip, c), me).wait_recv()
                passed.append(copy(t, 4 + j, (*chip, c), sibling))
                passed[-1].start()
        for t in range(n):
            copy(t, 0, sibling, me).wait_recv()
            for j, chip in enumerate(chips):
                copy(t, 4 + j, (*chip, 1 - c), me).wait_recv()
        for cp in first + passed:
            cp.wait_send()
        for cp in mine:
            cp.wait()

    return pl.pallas_call(
        body, name=name, in_specs=[ANY] * n, out_specs=[ANY] * n,
        out_shape=[jax.ShapeDtypeStruct((N_DEV,) + s.shape, s.dtype) for s in shards],
        scratch_shapes=[pltpu.SemaphoreType.DMA((n, 7)), pltpu.SemaphoreType.DMA((n, 7)), pltpu.SemaphoreType.DMA((n,))],
    )(*shards)


def _exchange_multi(blocks, *, name):
    n = len(blocks)

    def body(*refs):
        g_refs, out_refs = refs[:n], refs[n:2 * n]
        send_sems, recv_sems, local_sems = refs[2 * n:]
        x, y, c = lax.axis_index("x"), lax.axis_index("y"), lax.axis_index("c")
        me = 4 * x + 2 * y + c
        mine = [pltpu.make_async_copy(g_refs[t].at[me], out_refs[t].at[me], local_sems.at[t]) for t in range(n)]
        for cp in mine:
            cp.start()
        sends, recvs = [], []
        for k in range(1, N_DEV):
            px, py, pc = _peer(k, x, y, c)
            p = 4 * px + 2 * py + pc
            for t in range(n):
                sends.append(pltpu.make_async_remote_copy(
                    src_ref=g_refs[t].at[p], dst_ref=out_refs[t].at[me], send_sem=send_sems.at[t, k - 1],
                    recv_sem=recv_sems.at[t, k - 1], device_id=(px, py, pc), device_id_type=MESH))
                recvs.append(pltpu.make_async_remote_copy(
                    src_ref=g_refs[t].at[p], dst_ref=out_refs[t].at[p], send_sem=send_sems.at[t, k - 1],
                    recv_sem=recv_sems.at[t, k - 1], device_id=(px, py, pc), device_id_type=MESH))
        for cp in sends:
            cp.start()
        for cp in recvs:
            cp.wait_recv()
        for cp in sends:
            cp.wait_send()
        for cp in mine:
            cp.wait()

    return pl.pallas_call(
        body, name=name, in_specs=[ANY] * n, out_specs=[ANY] * n,
        out_shape=[jax.ShapeDtypeStruct(b.shape, b.dtype) for b in blocks],
        scratch_shapes=[pltpu.SemaphoreType.DMA((n, 7)), pltpu.SemaphoreType.DMA((n, 7)), pltpu.SemaphoreType.DMA((n,))],
    )(*blocks)


def _adamw2(parts, w, m, v, *, name, T):
    R, C = w.shape
    c1 = 1.0 / (1.0 - ADAM_B1 ** ADAM_STEP)
    c2 = 1.0 / (1.0 - ADAM_B2 ** ADAM_STEP)

    def body(p_ref, w_ref, m_ref, v_ref, g_ref, d_ref, nm_ref, nv_ref):
        g = p_ref[0].astype(F32)
        for s in range(1, N_DEV):
            g = g + p_ref[s].astype(F32)
        g_ref[...] = g
        nm = ADAM_B1 * m_ref[...] + (1.0 - ADAM_B1) * g
        nv = ADAM_B2 * v_ref[...] + (1.0 - ADAM_B2) * (g * g)
        nm_ref[...] = nm
        nv_ref[...] = nv
        d_ref[...] = -ADAM_LR * ((nm * c1) / (jnp.sqrt(nv * c2) + ADAM_EPS) + ADAM_WD * w_ref[...])

    blk = pl.BlockSpec((T, C), lambda i: (i, 0))
    out = jax.ShapeDtypeStruct((R, C), F32)
    return pl.pallas_call(
        body, name=name, grid=(R // T,),
        in_specs=[pl.BlockSpec((N_DEV, T, C), lambda i: (0, i, 0)), blk, blk, blk],
        out_specs=[blk, blk, blk, blk], out_shape=[out, out, out, out],
        compiler_params=_cparams(("parallel",)),
    )(parts, w, m, v)


SMALL_ROWS = 88
SHARDED = [("w_in", (D_MODEL, 1154), 256), ("w_branch_a", (128, D_MODEL), 128), ("w_branch_b", (128, D_MODEL), 128),
           ("w_out", (128, D_MODEL), 128), ("w_up", (D_MODEL, 704), 256), ("conv_w", (3, 704), 3),
           ("w_down", (352, D_MODEL), 352)]


def _col_blocks(a, width):
    return jnp.stack([a[:, d * width:(d + 1) * width] for d in range(N_DEV)])


def _pack_small(vals):
    flat = jnp.concatenate([vals[n].reshape(-1).astype(F32) for n, _ in SMALL])
    return jnp.pad(flat, (0, SMALL_ROWS * 128 - flat.shape[0])).reshape(SMALL_ROWS, 128)


def _unpack_small(buf):
    flat, out, off = buf.reshape(-1), {}, 0
    for n, shape in SMALL:
        out[n] = flat[off:off + _size(shape)].reshape(shape)
        off += _size(shape)
    return out


def _gather_weights2(w_in, w_a, w_b, w_o, w_up, conv_w, w_down):
    shards = [w_in[0].astype(BF16), w_a[0].astype(BF16), w_b[0].astype(BF16), w_o[0].astype(BF16),
              w_up[0].astype(BF16), conv_w[0], w_down[0].astype(BF16)]
    g_in, g_a, g_b, g_o, g_up, g_cw, g_d = _gather_multi(shards, name="gather_weights")
    win = jnp.concatenate([g_in[d] for d in range(N_DEV)], axis=1)
    wup = jnp.concatenate([g_up[d] for d in range(N_DEV)], axis=1)
    cw = jnp.concatenate([g_cw[d] for d in range(N_DEV)], axis=1)
    return dict(
        wm=jnp.concatenate([win[:, :FF_LO], win[:, FF_HI:]], axis=1),
        wff=jnp.pad(win[:, FF_LO:FF_HI], ((0, 0), (0, 128 - FOX_HEADS))),
        wa=g_a.reshape(D_MODEL, D_MODEL), wb=g_b.reshape(D_MODEL, D_MODEL), wo=g_o.reshape(D_MODEL, D_MODEL),
        wug=wup[:, :D_FF], wuv=wup[:, D_FF:], cwg=cw[:, :D_FF], cwv=cw[:, D_FF:], wd=g_d.reshape(D_FF, D_MODEL))


def _grad_blocks(g):
    w_in = jnp.concatenate([g["wm"][:, :FF_LO], g["wff"][:, :FOX_HEADS], g["wm"][:, FF_LO:]], axis=1)
    conv_w = jnp.concatenate([g["cwg"], g["cwv"]], axis=1).astype(F32)
    conv_b = jnp.concatenate([g["cbg"], g["cbv"]], axis=1)
    small = _pack_small(dict(norm_mix=g["norm_mix"], fox_f_bias=g["fox_f_bias"], hg_lb_logits=g["hg_lb_logits"],
                             hg_norm=g["hg_norm"], norm_ffn=g["norm_ffn"], conv_b=conv_b, norm_final=g["norm_final"]))
    up = jnp.stack([g["wug"][:, d * 704:(d + 1) * 704] for d in range(4)]
                   + [g["wuv"][:, d * 704:(d + 1) * 704] for d in range(4)])
    return [_col_blocks(w_in, 1154), g["wa"].reshape(N_DEV, 128, D_MODEL), g["wb"].reshape(N_DEV, 128, D_MODEL),
            g["wo"].reshape(N_DEV, 128, D_MODEL), up, _col_blocks(conv_w, 704), g["wd"].reshape(N_DEV, 352, D_MODEL),
            jnp.broadcast_to(small[None], (N_DEV, SMALL_ROWS, 128))]


def kernel(x, norm_mix, w_in,fox_f_bias, hg_lb_logits, hg_norm, w_branch_a, w_branch_b, w_out, norm_ffn, w_up, conv_w, conv_b, w_down, norm_final, loss_target, m_norm_mix, m_w_in, m_fox_f_bias, m_hg_lb_logits, m_hg_norm, m_w_branch_a, m_w_branch_b, m_w_out, m_norm_ffn, m_w_up, m_conv_w, m_conv_b, m_w_down, m_norm_final, v_norm_mix, v_w_in, v_fox_f_bias, v_hg_lb_logits, v_hg_norm, v_w_branch_a, v_w_branch_b, v_w_out, v_norm_ffn, v_w_up, v_conv_w, v_conv_b, v_w_down, v_norm_final):
    wv = dict(norm_mix=norm_mix, w_in=w_in, fox_f_bias=fox_f_bias, hg_lb_logits=hg_lb_logits, hg_norm=hg_norm,
              w_branch_a=w_branch_a, w_branch_b=w_branch_b, w_out=w_out, norm_ffn=norm_ffn, w_up=w_up, conv_w=conv_w,
              conv_b=conv_b, w_down=w_down, norm_final=norm_final)
    mv = dict(norm_mix=m_norm_mix, w_in=m_w_in, fox_f_bias=m_fox_f_bias, hg_lb_logits=m_hg_lb_logits, hg_norm=m_hg_norm,
              w_branch_a=m_w_branch_a, w_branch_b=m_w_branch_b, w_out=m_w_out, norm_ffn=m_norm_ffn, w_up=m_w_up,
              conv_w=m_conv_w, conv_b=m_conv_b, w_down=m_w_down, norm_final=m_norm_final)
    vv = dict(norm_mix=v_norm_mix, w_in=v_w_in, fox_f_bias=v_fox_f_bias, hg_lb_logits=v_hg_lb_logits, hg_norm=v_hg_norm,
              w_branch_a=v_w_branch_a, w_branch_b=v_w_branch_b, w_out=v_w_out, norm_ffn=v_norm_ffn, w_up=v_w_up,
              conv_w=v_conv_w, conv_b=v_conv_b, w_down=v_w_down, norm_final=v_norm_final)

    w = _gather_weights2(w_in, w_branch_a, w_branch_b, w_out, w_up, conv_w, w_down)
    p = dict(norm_mix=norm_mix[0], fox_f_bias=fox_f_bias[0], hg_lb_logits=hg_lb_logits, hg_norm=hg_norm[0],
             norm_ffn=norm_ffn[0], cbg=conv_b[:, :D_FF], cbv=conv_b[:, D_FF:], norm_final=norm_final)
    loss, dx, grads = _local_step(x[0], loss_target[0], w, p)
    loss = lax.psum(loss[0, 0], ("x", "y", "c"))

    parts = _exchange_multi(_grad_blocks(grads), name="exchange_grads")
    res = {}
    for (n, shape, tile), part in zip(SHARDED, parts):
        outs = _adamw2(part, wv[n].reshape(shape), mv[n].reshape(shape), vv[n].reshape(shape), name="adamw_" + n, T=tile)
        res[n] = [o.reshape(wv[n].shape) for o in outs]
    outs = _adamw2(parts[-1], _pack_small(wv), _pack_small(mv), _pack_small(vv), name="adamw_small", T=SMALL_ROWS)
    small = [_unpack_small(o) for o in outs]
    for n, _ in SMALL:
        res[n] = [s[n] for s in small]
    return (loss, dx[None], *[res[n][0] for n in NAMES], *[res[n][1] for n in NAMES],
            *[res[n][2] for n in NAMES], *[res[n][3] for n in NAMES])


def _lb_fwd(logits, *, name):
    def body(l_ref, lb_ref):
        lb_ref[...] = _sigmoid(l_ref[0:1, :] - l_ref[1:2, :])

    return pl.pallas_call(body, name=name, out_shape=jax.ShapeDtypeStruct((1, logits.shape[1]), F32))(logits)


def _lb_bwd(logits, dlb, *, name):
    def body(l_ref, d_ref, o_ref):
        lbv = _sigmoid(l_ref[0:1, :] - l_ref[1:2, :])
        t = d_ref[...] * lbv * (1.0 - lbv)
        o_ref[0:1, :] = t
        o_ref[1:2, :] = -t

    return pl.pallas_call(body, name=name, out_shape=jax.ShapeDtypeStruct(logits.shape, F32))(logits, dlb)
```

```python
import functools

import numpy as np
import jax
import jax.numpy as jnp
from jax import lax
from jax.experimental import pallas as pl
from jax.experimental.pallas import tpu as pltpu

F32 = jnp.float32
BF16 = jnp.bfloat16

D_MODEL = 1024
HG_HEADS = 8
HG_DK = 128
HG_DV = 128
HG_CHUNK = 64
FOX_HEADS = 16
FOX_DH = 64
D_FF = 2816
EPS = 1e-6
N_DEV = 8

ADAM_LR = 0.001
ADAM_B1 = 0.9
ADAM_B2 = 0.999
ADAM_EPS = 1e-08
ADAM_WD = 0.01
ADAM_STEP = 10

VMEM_LIMIT = 56 * 1024 * 1024


def _cparams(sem):
    return pltpu.CompilerParams(dimension_semantics=sem, vmem_limit_bytes=VMEM_LIMIT)


_DIMS = {
    "nn": (((1,), (0,)), ((), ())),
    "nt": (((1,), (1,)), ((), ())),
    "tn": (((0,), (0,)), ((), ())),
}


def _pick(n, prefs):
    for p in prefs:
        if n % p == 0:
            return p
    return n


MATMUL_VMEM_BUDGET = 36 * 1024 * 1024
MAX_TILE = 1536


def _tile_options(n):
    return [d for d in range(128, min(n, MAX_TILE) + 1, 128) if n % d == 0] or [n]


def _pick_tiles(M, N, tk, nk, sa, sb, so, has_addend, tm, tn):
    best = None
    for cm in ([tm] if tm else _tile_options(M)):
        for cn in ([tn] if tn else _tile_options(N)):
            need = 2 * (cm * tk * sa + tk * cn * sb + cm * cn * so + (cm * cn * 4 if has_addend else 0))
            need += cm * cn * 4 if nk > 1 else 0
            if need <= MATMUL_VMEM_BUDGET and (best is None or cm * cn > best[0] * best[1]
                                               or (cm * cn == best[0] * best[1] and cn > best[1])):
                best = (cm, cn)
    assert best is not None, (M, N, tk)
    return best


def _matmul(a, b, form, *, out_dtype=F32, addend=None, tm=None, tn=None, tk=None, name):
    if form == "nn":
        (M, K), (K2, N) = a.shape, b.shape
    elif form == "nt":
        (M, K), (N, K2) = a.shape, b.shape
    else:
        (K, M), (K2, N) = a.shape, b.shape
    assert K == K2, (a.shape, b.shape, form)
    tk = tk or (K if K <= 2816 else _pick(K, (1024, 512, 256, 128)))
    nk = K // tk
    if tm is None or tn is None:
        tm, tn = _pick_tiles(M, N, tk, nk, a.dtype.itemsize, b.dtype.itemsize, jnp.dtype(out_dtype).itemsize,
                             addend is not None, tm, tn)
    assert M % tm == 0 and N % tn == 0 and K % tk == 0, (M, N, K, tm, tn, tk)
    dims = _DIMS[form]

    def body(*refs):
        a_ref, b_ref = refs[:2]
        add_ref = refs[2] if addend is not None else None
        o_ref = refs[3] if addend is not None else refs[2]

        def finish(r):
            if add_ref is not None:
                r = r + add_ref[...].astype(F32)
            o_ref[...] = r.astype(o_ref.dtype)

        part = lax.dot_general(a_ref[...].astype(BF16), b_ref[...].astype(BF16), dims, preferred_element_type=F32)
        if nk == 1:
            finish(part)
            return
        acc_ref = refs[-1]
        k = pl.program_id(2)

        @pl.when(k == 0)
        def _():
            acc_ref[...] = part

        @pl.when(k > 0)
        def _():
            acc_ref[...] += part

        @pl.when(k == nk - 1)
        def _():
            finish(acc_ref[...])

    if form == "nn":
        a_spec = pl.BlockSpec((tm, tk), lambda i, j, k: (i, k))
        b_spec = pl.BlockSpec((tk, tn), lambda i, j, k: (k, j))
    elif form == "nt":
        a_spec = pl.BlockSpec((tm, tk), lambda i, j, k: (i, k))
        b_spec = pl.BlockSpec((tn, tk), lambda i, j, k: (j, k))
    else:
        a_spec = pl.BlockSpec((tk, tm), lambda i, j, k: (k, i))
        b_spec = pl.BlockSpec((tk, tn), lambda i, j, k: (k, j))
    o_spec = pl.BlockSpec((tm, tn), lambda i, j, k: (i, j))
    in_specs = [a_spec, b_spec] + ([o_spec] if addend is not None else [])
    args = (a, b) + ((addend,) if addend is not None else ())
    return pl.pallas_call(
        body, name=name, grid=(M // tm, N // tn, nk),
        in_specs=in_specs, out_specs=o_spec,
        out_shape=jax.ShapeDtypeStruct((M, N), out_dtype),
        scratch_shapes=[pltpu.VMEM((tm, tn), F32)] if nk > 1 else [],
        compiler_params=_cparams(("parallel", "parallel", "arbitrary")),
    )(*args)


def _rms_fwd(x, g, *, name, tm=512):
    M, D = x.shape
    tm = min(tm, M)

    def body(x_ref, g_ref, n_ref):
        xf = x_ref[...]
        r = lax.rsqrt(jnp.mean(xf * xf, axis=-1, keepdims=True) + EPS)
        n_ref[...] = (xf * r * g_ref[...]).astype(n_ref.dtype)

    return pl.pallas_call(
        body, name=name, grid=(M // tm,),
        in_specs=[pl.BlockSpec((tm, D), lambda i: (i, 0)), pl.BlockSpec((1, D), lambda i: (0, 0))],
        out_specs=pl.BlockSpec((tm, D), lambda i: (i, 0)),
        out_shape=jax.ShapeDtypeStruct((M, D), BF16),
        compiler_params=_cparams(("parallel",)),
    )(x, g.reshape(1, D))


def _rms_bwd(x, g, dn, dres, *, name, tm=512):
    M, D = x.shape
    tm = min(tm, M)

    def body(x_ref, g_ref, dn_ref, dres_ref, dx_ref, dg_ref):
        @pl.when(pl.program_id(0) == 0)
        def _():
            dg_ref[...] = jnp.zeros_like(dg_ref)

        xf = x_ref[...]
        r = lax.rsqrt(jnp.mean(xf * xf, axis=-1, keepdims=True) + EPS)
        xh = xf * r
        dn_ = dn_ref[...].astype(F32)
        dg_ref[...] += jnp.sum(dn_ * xh, axis=0, keepdims=True)
        dxh = dn_ * g_ref[...]
        dx = r * (dxh - xh * jnp.mean(dxh * xh, axis=-1, keepdims=True))
        dx_ref[...] = dres_ref[...] + dx

    row = pl.BlockSpec((tm, D), lambda i: (i, 0))
    vec = pl.BlockSpec((1, D), lambda i: (0, 0))
    return pl.pallas_call(
        body, name=name, grid=(M // tm,),
        in_specs=[row, vec, row, row], out_specs=[row, vec],
        out_shape=[jax.ShapeDtypeStruct((M, D), F32), jax.ShapeDtypeStruct((1, D), F32)],
        compiler_params=_cparams(("arbitrary",)),
    )(x, g.reshape(1, D), dn, dres)


def _loss_head(h, g, tgt, *, name, tm=512):
    M, D = h.shape
    tm = min(tm, M)

    def body(h_ref, g_ref, t_ref, loss_ref, dh_ref, dg_ref):
        @pl.when(pl.program_id(0) == 0)
        def _():
            dg_ref[...] = jnp.zeros_like(dg_ref)
            loss_ref[...] = jnp.zeros_like(loss_ref)

        xf = h_ref[...]
        r = lax.rsqrt(jnp.mean(xf * xf, axis=-1, keepdims=True) + EPS)
        xh = xf * r
        err = xh * g_ref[...] - t_ref[...]
        part = jnp.sum(jnp.mean(err * err, axis=-1, keepdims=True), axis=0, keepdims=True)
        loss_ref[...] += 0.5 * part
        dy = err * (1.0 / D)
        dg_ref[...] += jnp.sum(dy * xh, axis=0, keepdims=True)
        dxh = dy * g_ref[...]
        dh_ref[...] = r * (dxh - xh * jnp.mean(dxh * xh, axis=-1, keepdims=True))

    row = pl.BlockSpec((tm, D), lambda i: (i, 0))
    vec = pl.BlockSpec((1, D), lambda i: (0, 0))
    one = pl.BlockSpec((1, 1), lambda i: (0, 0))
    return pl.pallas_call(
        body, name=name, grid=(M // tm,),
        in_specs=[row, vec, row], out_specs=[one, row, vec],
        out_shape=[jax.ShapeDtypeStruct((1, 1), F32), jax.ShapeDtypeStruct((M, D), F32),
                   jax.ShapeDtypeStruct((1, D), F32)],
        compiler_params=_cparams(("arbitrary",)),
    )(h, g.reshape(1, D), tgt)


HG_MID = HG_CHUNK // 2 - 1
EXP_CAP = 80.0


def _sigmoid(x):
    return 1.0 / (1.0 + jnp.exp(-x))


def _dot(a, b, dims, precision=None):
    return lax.dot_general(a, b, dims, preferred_element_type=F32, precision=precision)


def _bdot(a, b, form):
    return _dot(a.astype(BF16), b.astype(BF16), _DIMS[form])


def _hdot(a, b, form):
    return _dot(a, b, _DIMS[form], precision=lax.Precision.HIGHEST)


def _hgrn_chunk_common(hq, hf, lbv, tril, rid):
    sq = _sigmoid(hq)
    q = hq * sq
    sg = _sigmoid(hf)
    f = lbv + (1.0 - lbv) * sg
    k = (1.0 - lbv) * (1.0 - sg)
    g = jnp.log(f)
    b = _dot(tril, g, _DIMS["nn"], precision=lax.Precision.HIGHEST)
    bref = jnp.sum(jnp.where(rid == HG_MID, b, 0.0), axis=0, keepdims=True)
    bend = jnp.sum(jnp.where(rid == HG_CHUNK - 1, b, 0.0), axis=0, keepdims=True)
    eb = jnp.exp(b)
    e1 = jnp.exp(jnp.minimum(b - bref, EXP_CAP))
    e2 = jnp.exp(jnp.minimum(bref - b, EXP_CAP))
    e3 = jnp.exp(bend - b)
    return sq, q, sg, f, k, bend, eb, e1, e2, e3


def _hgrn_fwd(proj, lb, gnorm, *, name, T=512):
    S = proj.shape[0]
    T = min(T, S)
    nch = T // HG_CHUNK
    C = HG_CHUNK

    def body(hq_ref, hf_ref, hi_ref, hg_ref, lb_ref, gn_ref, o_ref, oa_ref, st_ref, state):
        @pl.when(pl.program_id(1) == 0)
        def _():
            state[...] = jnp.zeros_like(state)

        lbv = lb_ref[...]
        gn = gn_ref[...]
        row = lax.broadcasted_iota(jnp.int32, (C, C), 0)
        col = lax.broadcasted_iota(jnp.int32, (C, C), 1)
        causal = row >= col
        tril = causal.astype(F32)
        rid = lax.broadcasted_iota(jnp.int32, (C, HG_DK), 0)
        for c in range(nch):
            sl = pl.ds(c * C, C)
            hq, hf, v, hg = hq_ref[sl, :], hf_ref[sl, :], hi_ref[sl, :], hg_ref[sl, :]
            _, q, _, _, k, bend, eb, e1, e2, e3 = _hgrn_chunk_common(hq, hf, lbv, tril, rid)
            st = state[...]
            st_ref[0, c] = st
            o = _hdot(q * eb, st, "nt")
            a = jnp.where(causal, _hdot(q * e1, k * e2, "nt"), 0.0)
            o = o + _hdot(a, v, "nn")
            state[...] = st * jnp.exp(bend) + _hdot(v, k * e3, "tn")
            o_ref[sl, :] = o
            r = lax.rsqrt(jnp.mean(o * o, axis=-1, keepdims=True) + EPS)
            oa_ref[sl, :] = (o * r * gn * (hg * _sigmoid(hg))).astype(oa_ref.dtype)

    def grp(gidx):
        return pl.BlockSpec((T, 128), lambda h, t: (t, gidx * 8 + h))

    return pl.pallas_call(
        body, name=name, grid=(HG_HEADS, S // T),
        in_specs=[grp(0), grp(1), grp(2), grp(3),
                  pl.BlockSpec((1, 128), lambda h, t: (0, h)), pl.BlockSpec((1, 128), lambda h, t: (0, 0))],
        out_specs=[pl.BlockSpec((T, 128), lambda h, t: (t, h)), pl.BlockSpec((T, 128), lambda h, t: (t, h)),
                   pl.BlockSpec((1, nch, HG_DV, HG_DK), lambda h, t: (h, t, 0, 0))],
        out_shape=[jax.ShapeDtypeStruct((S, HG_HEADS * HG_DV), F32), jax.ShapeDtypeStruct((S, HG_HEADS * HG_DV), BF16),
                   jax.ShapeDtypeStruct((HG_HEADS, S // C, HG_DV, HG_DK), F32)],
        scratch_shapes=[pltpu.VMEM((HG_DV, HG_DK), F32)],
        compiler_params=_cparams(("parallel", "arbitrary")),
    )(proj, proj, proj, proj, lb, gnorm)


def _hgrn_bwd(proj, lb, gnorm, o, states, doa, *, name, T=512):
    S = proj.shape[0]
    T = min(T, S)
    nch = T // HG_CHUNK
    C = HG_CHUNK
    nT = S // T

    def body(hq_ref, hf_ref, hi_ref, hg_ref, lb_ref, gn_ref, o_ref, st_ref, doa_ref,
             dhq_ref, dhf_ref, dhi_ref, dhg_ref, dlb_ref, dgn_ref, dstate):
        @pl.when(pl.program_id(1) == 0)
        def _():
            dstate[...] = jnp.zeros_like(dstate)
            dlb_ref[...] = jnp.zeros_like(dlb_ref)
            dgn_ref[...] = jnp.zeros_like(dgn_ref)

        lbv = lb_ref[...]
        gn = gn_ref[...]
        row = lax.broadcasted_iota(jnp.int32, (C, C), 0)
        col = lax.broadcasted_iota(jnp.int32, (C, C), 1)
        causal = row >= col
        tril = causal.astype(F32)
        triu = (row <= col).astype(F32)
        rid = lax.broadcasted_iota(jnp.int32, (C, HG_DK), 0)
        for c in reversed(range(nch)):
            sl = pl.ds(c * C, C)
            hq, hf, v, hg = hq_ref[sl, :], hf_ref[sl, :], hi_ref[sl, :], hg_ref[sl, :]
            sq, q, sg, f, k, bend, eb, e1, e2, e3 = _hgrn_chunk_common(hq, hf, lbv, tril, rid)
            qi, qp, kp, kend = q * eb, q * e1, k * e2, k * e3
            st0 = st_ref[0, c]
            ov = o_ref[sl, :]
            r = lax.rsqrt(jnp.mean(ov * ov, axis=-1, keepdims=True) + EPS)
            xh = ov * r
            sgg = _sigmoid(hg)
            d_oa = doa_ref[sl, :].astype(F32)
            dz = d_oa * (hg * sgg)
            dhg_ref[sl, :] = (d_oa * (xh * gn) * (sgg * (1.0 + hg * (1.0 - sgg)))).astype(dhg_ref.dtype)
            dgn_ref[0] += jnp.sum(dz * xh, axis=0, keepdims=True)
            dxh = dz * gn
            do = r * (dxh - xh * jnp.mean(dxh * xh, axis=-1, keepdims=True))
            ds1 = dstate[...]
            dqi = _hdot(do, st0, "nn")
            a = jnp.where(causal, _hdot(qp, kp, "nt"), 0.0)
            da = jnp.where(causal, _hdot(do, v, "nt"), 0.0)
            dv = _hdot(a, do, "tn") + _hdot(kend, ds1, "nt")
            dqp = _hdot(da, kp, "nn")
            dkp = _hdot(da, qp, "tn")
            dkend = _hdot(v, ds1, "nn")
            dq = dqi * eb + dqp * e1
            dk = dkp * e2 + dkend * e3
            db = dqi * qi + dqp * qp - dkp * kp - dkend * kend
            dbend = (jnp.sum(dkend * kend, axis=0, keepdims=True)
                     + jnp.exp(bend) * jnp.sum(ds1 * st0, axis=0, keepdims=True))
            db = db + jnp.where(rid == C - 1, dbend, 0.0)
            dg = _dot(triu, db, _DIMS["nn"], precision=lax.Precision.HIGHEST)
            df = dg / f - dk
            dlb_ref[...] += jnp.sum(df * (1.0 - sg), axis=0, keepdims=True)
            dhf_ref[sl, :] = (df * (1.0 - lbv) * sg * (1.0 - sg)).astype(dhf_ref.dtype)
            dhq_ref[sl, :] = (dq * (sq * (1.0 + hq * (1.0 - sq)))).astype(dhq_ref.dtype)
            dhi_ref[sl, :] = dv.astype(dhi_ref.dtype)
            dstate[...] = ds1 * jnp.exp(bend) + _hdot(do, qi, "tn")

    def grp(gidx):
        return pl.BlockSpec((T, 128), lambda h, t: (nT - 1 - t, gidx * 8 + h))

    tok = pl.BlockSpec((T, 128), lambda h, t: (nT - 1 - t, h))
    big = jax.ShapeDtypeStruct((S, HG_HEADS * HG_DV), BF16)
    return pl.pallas_call(
        body, name=name, grid=(HG_HEADS, nT),
        in_specs=[grp(0), grp(1), grp(2), grp(3),
                  pl.BlockSpec((1, 128), lambda h, t: (0, h)), pl.BlockSpec((1, 128), lambda h, t: (0, 0)),
                  tok, pl.BlockSpec((1, nch, HG_DV, HG_DK), lambda h, t: (h, nT - 1 - t, 0, 0)), tok],
        out_specs=[tok, tok, tok, tok, pl.BlockSpec((1, 128), lambda h, t: (0, h)),
                   pl.BlockSpec((1, 1, 128), lambda h, t: (h, 0, 0))],
        out_shape=[big, big, big, big, jax.ShapeDtypeStruct((1, HG_HEADS * HG_DK), F32),
                   jax.ShapeDtypeStruct((HG_HEADS, 1, HG_DV), F32)],
        scratch_shapes=[pltpu.VMEM((HG_DV, HG_DK), F32)],
        compiler_params=_cparams(("parallel", "arbitrary")),
    )(proj, proj, proj, proj, lb, gnorm, o, states, doa)


NEG = -1e30
FOX_SCALE = FOX_DH ** -0.5
FOX_PAIRS = FOX_HEADS // 2


def _fox_gate_fwd(ff, bias, *, name, T=512):
    S = ff.shape[0]
    T = min(T, S)

    def body(ff_ref, b_ref, c_ref, carry):
        @pl.when(pl.program_id(0) == 0)
        def _():
            carry[...] = jnp.zeros_like(carry)

        z = ff_ref[...] + b_ref[...]
        logf = jnp.minimum(z, 0.0) - jnp.log(1.0 + jnp.exp(-jnp.abs(z)))
        row = lax.broadcasted_iota(jnp.int32, (T, T), 0)
        col = lax.broadcasted_iota(jnp.int32, (T, T), 1)
        c = _dot((row >= col).astype(F32), logf, _DIMS["nn"], precision=lax.Precision.HIGHEST) + carry[...]
        c_ref[...] = c
        carry[...] = c[T - 1:T, :]

    return pl.pallas_call(
        body, name=name, grid=(S // T,),
        in_specs=[pl.BlockSpec((T, 128), lambda i: (i, 0)), pl.BlockSpec((1, 128), lambda i: (0, 0))],
        out_specs=pl.BlockSpec((T, 128), lambda i: (i, 0)),
        out_shape=jax.ShapeDtypeStruct((S, 128), F32),
        scratch_shapes=[pltpu.VMEM((1, 128), F32)],
        compiler_params=_cparams(("arbitrary",)),
    )(ff, bias)


def _fox_gate_bwd(ff, bias, dcs, *, name, T=512):
    S = ff.shape[0]
    T = min(T, S)
    nT = S // T

    def body(ff_ref, b_ref, d_ref, dff_ref, db_ref, carry):
        @pl.when(pl.program_id(0) == 0)
        def _():
            carry[...] = jnp.zeros_like(carry)
            db_ref[...] = jnp.zeros_like(db_ref)

        row = lax.broadcasted_iota(jnp.int32, (T, T), 0)
        col = lax.broadcasted_iota(jnp.int32, (T, T), 1)
        dlogf = carry[...] - _dot((row <= col).astype(F32), d_ref[...], _DIMS["nn"], precision=lax.Precision.HIGHEST)
        carry[...] = dlogf[0:1, :]
        dff = dlogf * (1.0 - _sigmoid(ff_ref[...] + b_ref[...]))
        dff_ref[...] = dff.astype(dff_ref.dtype)
        db_ref[...] += jnp.sum(dff, axis=0, keepdims=True)

    rev = pl.BlockSpec((T, 128), lambda i: (nT - 1 - i, 0))
    vec = pl.BlockSpec((1, 128), lambda i: (0, 0))
    return pl.pallas_call(
        body, name=name, grid=(nT,),
        in_specs=[rev, vec, rev], out_specs=[rev, vec],
        out_shape=[jax.ShapeDtypeStruct((S, 128), BF16), jax.ShapeDtypeStruct((1, 128), F32)],
        scratch_shapes=[pltpu.VMEM((1, 128), F32)],
        compiler_params=_cparams(("arbitrary",)),
    )(ff, bias, dcs)


def _fox_logits(q, k, cc, cr, qi, ki, tq, tk):
    s = _bdot(q, k, "nt") * FOX_SCALE + cc - cr
    qpos = qi * tq + lax.broadcasted_iota(jnp.int32, (tq, tk), 0)
    kpos = ki * tk + lax.broadcasted_iota(jnp.int32, (tq, tk), 1)
    return jnp.where(kpos <= qpos, s, NEG)


def _fox_fwd(proj, ccol, crow, *, name, tq=512, tk=512):
    S = proj.shape[0]
    tq, tk = min(tq, S), min(tk, S)

    def body(q_ref, k_ref, v_ref, cc_ref, cr_ref, o_ref, lse_ref, m_s, l_s, acc_s):
        qi, ki = pl.program_id(1), pl.program_id(2)

        @pl.when(ki == 0)
        def _():
            m_s[...] = jnp.full_like(m_s, NEG)
            l_s[...] = jnp.zeros_like(l_s)
            acc_s[...] = jnp.zeros_like(acc_s)

        @pl.when(ki <= qi)
        def _():
            for hh in range(2):
                ls = slice(hh * FOX_DH, (hh + 1) * FOX_DH)
                s = _fox_logits(q_ref[:, ls], k_ref[:, ls], cc_ref[0, :, hh:hh + 1], cr_ref[0, hh:hh + 1, :], qi, ki, tq, tk)
                m_old = m_s[hh]
                m_new = jnp.maximum(m_old, jnp.max(s, axis=-1, keepdims=True))
                p = jnp.exp(s - m_new)
                alpha = jnp.exp(m_old - m_new)
                l_s[hh] = alpha * l_s[hh] + jnp.sum(p, axis=-1, keepdims=True)
                p_hi = p.astype(BF16)
                p_lo = (p - p_hi.astype(F32)).astype(BF16)
                vv = v_ref[:, ls].astype(BF16)
                acc_s[hh] = alpha * acc_s[hh] + _bdot(p_hi, vv, "nn") + _bdot(p_lo, vv, "nn")
                m_s[hh] = m_new

        @pl.when(ki == qi)
        def _():
            for hh in range(2):
                o_ref[:, hh * FOX_DH:(hh + 1) * FOX_DH] = acc_s[hh] / l_s[hh]
                lse_ref[0, :, hh:hh + 1] = m_s[hh] + jnp.log(l_s[hh])

    qspec = pl.BlockSpec((tq, 128), lambda p, i, j: (i, 32 + p))
    kspec = pl.BlockSpec((tk, 128), lambda p, i, j: (jnp.minimum(j, i), 40 + p))
    vspec = pl.BlockSpec((tk, 128), lambda p, i, j: (jnp.minimum(j, i), 48 + p))
    ccs = pl.BlockSpec((1, tq, 2), lambda p, i, j: (p, i, 0))
    crs = pl.BlockSpec((1, 2, tk), lambda p, i, j: (p, 0, jnp.minimum(j, i)))
    return pl.pallas_call(
        body, name=name, grid=(FOX_PAIRS, S // tq, S // tk),
        in_specs=[qspec, kspec, vspec, ccs, crs],
        out_specs=[pl.BlockSpec((tq, 128), lambda p, i, j: (i, p)), ccs],
        out_shape=[jax.ShapeDtypeStruct((S, FOX_HEADS * FOX_DH), F32), jax.ShapeDtypeStruct((FOX_PAIRS, S, 2), F32)],
        scratch_shapes=[pltpu.VMEM((2, tq, 1), F32), pltpu.VMEM((2, tq, 1), F32), pltpu.VMEM((2, tq, FOX_DH), F32)],
        compiler_params=_cparams(("parallel", "parallel", "arbitrary")),
    )(proj, proj, proj, ccol, crow)


def _fox_bwd_dq(proj, ccol, crow, o, lse, do, *, name, tq=512, tk=512):
    S = proj.shape[0]
    tq, tk = min(tq, S), min(tk, S)

    def body(q_ref, k_ref, v_ref, cc_ref, cr_ref, o_ref, lse_ref, do_ref, dq_ref, dl_ref, acc_s):
        qi, ki = pl.program_id(1), pl.program_id(2)

        @pl.when(ki == 0)
        def _():
            acc_s[...] = jnp.zeros_like(acc_s)
            for hh in range(2):
                ls = slice(hh * FOX_DH, (hh + 1) * FOX_DH)
                dl_ref[0, :, hh:hh + 1] = jnp.sum(do_ref[:, ls].astype(F32) * o_ref[:, ls], axis=-1, keepdims=True)

        @pl.when(ki <= qi)
        def _():
            for hh in range(2):
                ls = slice(hh * FOX_DH, (hh + 1) * FOX_DH)
                s = _fox_logits(q_ref[:, ls], k_ref[:, ls], cc_ref[0, :, hh:hh + 1], cr_ref[0, hh:hh + 1, :], qi, ki, tq, tk)
                p = jnp.exp(s - lse_ref[0, :, hh:hh + 1])
                dp = _bdot(do_ref[:, ls], v_ref[:, ls], "nt")
                ds = p * (dp - dl_ref[0, :, hh:hh + 1])
                acc_s[hh] += _bdot(ds, k_ref[:, ls], "nn")

        @pl.when(ki == qi)
        def _():
            for hh in range(2):
                dq_ref[:, hh * FOX_DH:(hh + 1) * FOX_DH] = (acc_s[hh] * FOX_SCALE).astype(dq_ref.dtype)

    qspec = pl.BlockSpec((tq, 128), lambda p, i, j: (i, 32 + p))
    kspec = pl.BlockSpec((tk, 128), lambda p, i, j: (jnp.minimum(j, i), 40 + p))
    vspec = pl.BlockSpec((tk, 128), lambda p, i, j: (jnp.minimum(j, i), 48 + p))
    ccs = pl.BlockSpec((1, tq, 2), lambda p, i, j: (p, i, 0))
    crs = pl.BlockSpec((1, 2, tk), lambda p, i, j: (p, 0, jnp.minimum(j, i)))
    tok = pl.BlockSpec((tq, 128), lambda p, i, j: (i, p))
    return pl.pallas_call(
        body, name=name, grid=(FOX_PAIRS, S // tq, S // tk),
        in_specs=[qspec, kspec, vspec, ccs, crs, tok, ccs, tok],
        out_specs=[tok, ccs],
        out_shape=[jax.ShapeDtypeStruct((S, FOX_HEADS * FOX_DH), BF16), jax.ShapeDtypeStruct((FOX_PAIRS, S, 2), F32)],
        scratch_shapes=[pltpu.VMEM((2, tq, FOX_DH), F32)],
        compiler_params=_cparams(("parallel", "parallel", "arbitrary")),
    )(proj, proj, proj, ccol, crow, o, lse, do)


def _fox_bwd_dkv(proj, ccol, crow, lse, delta, do, *, name, tq=512, tk=512):
    S = proj.shape[0]
    tq, tk = min(tq, S), min(tk, S)
    nq = S // tq

    def body(q_ref, k_ref, v_ref, cc_ref, cr_ref, lse_ref, dl_ref, do_ref, dk_ref, dv_ref, dcs_ref, dk_s, dv_s):
        ki, qi = pl.program_id(1), pl.program_id(2)

        @pl.when(qi == 0)
        def _():
            dk_s[...] = jnp.zeros_like(dk_s)
            dv_s[...] = jnp.zeros_like(dv_s)
            dcs_ref[...] = jnp.zeros_like(dcs_ref)

        @pl.when(qi >= ki)
        def _():
            for hh in range(2):
                ls = slice(hh * FOX_DH, (hh + 1) * FOX_DH)
                s = _fox_logits(q_ref[:, ls], k_ref[:, ls], cc_ref[0, :, hh:hh + 1], cr_ref[0, hh:hh + 1, :], qi, ki, tq, tk)
                p = jnp.exp(s - lse_ref[0, :, hh:hh + 1])
                dp = _bdot(do_ref[:, ls], v_ref[:, ls], "nt")
                ds = p * (dp - dl_ref[0, :, hh:hh + 1])
                dv_s[hh] += _bdot(p, do_ref[:, ls], "tn")
                dk_s[hh] += _bdot(ds, q_ref[:, ls], "tn")
                dcs_ref[0, hh:hh + 1, :] += jnp.sum(ds, axis=0, keepdims=True)

        @pl.when(qi == nq - 1)
        def _():
            for hh in range(2):
                dk_ref[:, hh * FOX_DH:(hh + 1) * FOX_DH] = (dk_s[hh] * FOX_SCALE).astype(dk_ref.dtype)
                dv_ref[:, hh * FOX_DH:(hh + 1) * FOX_DH] = dv_s[hh].astype(dv_ref.dtype)

    qspec = pl.BlockSpec((tq, 128), lambda p, j, i: (jnp.maximum(i, j), 32 + p))
    kspec = pl.BlockSpec((tk, 128), lambda p, j, i: (j, 40 + p))
    vspec = pl.BlockSpec((tk, 128), lambda p, j, i: (j, 48 + p))
    ccs = pl.BlockSpec((1, tq, 2), lambda p, j, i: (p, jnp.maximum(i, j), 0))
    crs = pl.BlockSpec((1, 2, tk), lambda p, j, i: (p, 0, j))
    dos = pl.BlockSpec((tq, 128), lambda p, j, i: (jnp.maximum(i, j), p))
    ktok = pl.BlockSpec((tk, 128), lambda p, j, i: (j, p))
    big = jax.ShapeDtypeStruct((S, FOX_HEADS * FOX_DH), BF16)
    return pl.pallas_call(
        body, name=name, grid=(FOX_PAIRS, S // tk, nq),
        in_specs=[qspec, kspec, vspec, ccs, crs, ccs, ccs, dos],
        out_specs=[ktok, ktok, crs],
        out_shape=[big, big, jax.ShapeDtypeStruct((FOX_PAIRS, 2, S), F32)],
        scratch_shapes=[pltpu.VMEM((2, tk, FOX_DH), F32), pltpu.VMEM((2, tk, FOX_DH), F32)],
        compiler_params=_cparams(("parallel", "parallel", "arbitrary")),
    )(proj, proj, proj, ccol, crow, lse, delta, do)


AUG = FOX_DH


def _split3(x):
    a = x.astype(BF16).astype(F32)
    r = x - a
    b = r.astype(BF16).astype(F32)
    return a, b, r - b


def _lane_fill(lane, base, pieces, start):
    for i, pc in enumerate(pieces):
        base = jnp.where(lane == start + i, pc, base)
    return base


def _fox_prep(proj, c_tok, *, name, T=512):
    S = proj.shape[0]
    T = min(T, S)

    def body(q_ref, k_ref, v_ref, c_ref, qa_ref, ka_ref, va_ref):
        pair = pl.program_id(0)
        lane = lax.broadcasted_iota(jnp.int32, (T, 128), 1)
        c = c_ref[...]
        ones3 = jnp.where((lane >= AUG) & (lane < AUG + 3), 1.0, 0.0)
        for hh in range(2):
            ch = jnp.sum(jnp.where(lane == 2 * pair + hh, c, 0.0), axis=-1, keepdims=True)
            c1, c2, c3 = _split3(ch)
            q, k, v = q_ref[...], k_ref[...], v_ref[...]
            if hh == 1:
                q, k, v = (pltpu.roll(t, 64, 1) for t in (q, k, v))
            aug_q = _lane_fill(lane, jnp.where((lane >= AUG + 3) & (lane < AUG + 6), 1.0, 0.0), (c1, c2, c3), AUG)
            aug_k = _lane_fill(lane, ones3, (-c1, -c2, -c3), AUG + 3)
            qa_ref[hh] = jnp.where(lane < AUG, q * FOX_SCALE, aug_q).astype(BF16)
            ka_ref[hh] = jnp.where(lane < AUG, k, aug_k).astype(BF16)
            va_ref[hh] = jnp.where(lane < AUG, v, ones3).astype(BF16)

    def grp(g):
        return pl.BlockSpec((T, 128), lambda p, t: (t, g * 8 + p))

    hm = pl.BlockSpec((2, T, 128), lambda p, t: (p, t, 0))
    out = jax.ShapeDtypeStruct((FOX_HEADS, S, 128), BF16)
    return pl.pallas_call(
        body, name=name, grid=(FOX_PAIRS, S // T),
        in_specs=[grp(4), grp(5), grp(6), pl.BlockSpec((T, 128), lambda p, t: (t, 0))],
        out_specs=[hm, hm, hm], out_shape=[out, out, out],
        compiler_params=_cparams(("parallel", "parallel")),
    )(proj, proj, proj, c_tok)


def _pair_lanes(lane, a0, a1):
    return jnp.where(lane < AUG, a0, pltpu.roll(a1, 64, 1))


def _tri_tables(nb, by_query):
    if by_query:
        pairs = [(i, j) for i in range(nb) for j in range(i + 1)]
    else:
        pairs = [(i, j) for j in range(nb) for i in range(j, nb)]
    return (jnp.asarray(np.array([p[0] for p in pairs], np.int32)),
            jnp.asarray(np.array([p[1] for p in pairs], np.int32)))


def _fox_fwd2(qa, ka, va, *, name, tb=512):
    S = qa.shape[1]
    tb = min(tb, S)
    qtab, ktab = _tri_tables(S // tb, True)

    def body(qt_ref, kt_ref, qa_ref, ka_ref, va_ref, o_ref, qb_ref, m_s, acc_s):
        qi, ki = qt_ref[pl.program_id(1)], kt_ref[pl.program_id(1)]

        @pl.when(ki == 0)
        def _():
            m_s[...] = jnp.full_like(m_s, NEG)
            acc_s[...] = jnp.zeros_like(acc_s)

        def step(masked):
            for hh in range(2):
                s = _dot(qa_ref[hh], ka_ref[hh], _DIMS["nt"])
                if masked:
                    row = lax.broadcasted_iota(jnp.int32, (tb, tb), 0)
                    col = lax.broadcasted_iota(jnp.int32, (tb, tb), 1)
                    s = jnp.where(col <= row, s, NEG)
                m_old = m_s[hh]
                m_new = jnp.maximum(m_old, jnp.max(s, axis=-1, keepdims=True))
                p = jnp.exp(s - m_new)
                p_hi = p.astype(BF16)
                p_lo = (p - p_hi.astype(F32)).astype(BF16)
                vv = va_ref[hh]
                acc_s[hh] = (jnp.exp(m_old - m_new) * acc_s[hh]
                             + _dot(p_hi, vv, _DIMS["nn"]) + _dot(p_lo, vv, _DIMS["nn"]))
                m_s[hh] = m_new

        @pl.when(ki < qi)
        def _():
            step(False)

        @pl.when(ki == qi)
        def _():
            step(True)
            lane = lax.broadcasted_iota(jnp.int32, (tb, 128), 1)
            outs = []
            for hh in range(2):
                acc = acc_s[hh]
                l = acc[:, AUG:AUG + 1]
                outs.append(acc / l)
                qf = qa_ref[hh].astype(F32)
                cb = qf[:, AUG:AUG + 1] + qf[:, AUG + 1:AUG + 2] + qf[:, AUG + 2:AUG + 3] - (m_s[hh] + jnp.log(l))
                qb_ref[hh] = _lane_fill(lane, qf, _split3(cb), AUG).astype(BF16)
            o_ref[...] = _pair_lanes(lane, outs[0], outs[1])

    qs = pl.BlockSpec((2, tb, 128), lambda p, t, qt, kt: (p, qt[t], 0))
    ks = pl.BlockSpec((2, tb, 128), lambda p, t, qt, kt: (p, kt[t], 0))
    return pl.pallas_call(
        body, name=name,
        grid_spec=pltpu.PrefetchScalarGridSpec(
            num_scalar_prefetch=2, grid=(FOX_PAIRS, qtab.shape[0]), in_specs=[qs, ks, ks],
            out_specs=[pl.BlockSpec((tb, 128), lambda p, t, qt, kt: (qt[t], p)), qs],
            scratch_shapes=[pltpu.VMEM((2, tb, 1), F32), pltpu.VMEM((2, tb, 128), F32)]),
        out_shape=[jax.ShapeDtypeStruct((S, FOX_HEADS * FOX_DH), F32), jax.ShapeDtypeStruct((FOX_HEADS, S, 128), BF16)],
        compiler_params=_cparams(("parallel", "arbitrary")),
    )(qtab, ktab, qa, ka, va)


def _fox_bwd_prep(o, do, *, name, T=512):
    S = o.shape[0]
    T = min(T, S)

    def body(o_ref, do_ref, dob_ref):
        lane = lax.broadcasted_iota(jnp.int32, (T, 128), 1)
        d = do_ref[...].astype(F32)
        prod = d * o_ref[...]
        for hh in range(2):
            mine = (lane < AUG) if hh == 0 else (lane >= AUG)
            delta = jnp.sum(jnp.where(mine, prod, 0.0), axis=-1, keepdims=True)
            dh = d if hh == 0 else pltpu.roll(d, 64, 1)
            dob_ref[hh] = _lane_fill(lane, jnp.where(lane < AUG, dh, 0.0), _split3(-delta), AUG).astype(BF16)

    tok = pl.BlockSpec((T, 128), lambda p, t: (t, p))
    return pl.pallas_call(
        body, name=name, grid=(FOX_PAIRS, S // T),
        in_specs=[tok, tok], out_specs=pl.BlockSpec((2, T, 128), lambda p, t: (p, t, 0)),
        out_shape=jax.ShapeDtypeStruct((FOX_HEADS, S, 128), BF16),
        compiler_params=_cparams(("parallel", "parallel")),
    )(o, do)


def _fox_bwd_dq2(qb, ka, va, dob, *, name, tb=512):
    S = qb.shape[1]
    tb = min(tb, S)
    nb = S // tb
    qtab, ktab = _tri_tables(nb, True)

    def body(qt_ref, kt_ref, qb_ref, ka_ref, va_ref, dob_ref, dq_ref, dcs_ref, acc_s):
        qi, ki = qt_ref[pl.program_id(1)], kt_ref[pl.program_id(1)]

        @pl.when(ki == 0)
        def _():
            acc_s[...] = jnp.zeros_like(acc_s)

        def step(masked):
            for hh in range(2):
                s = _dot(qb_ref[hh], ka_ref[hh], _DIMS["nt"])
                if masked:
                    row = lax.broadcasted_iota(jnp.int32, (tb, tb), 0)
                    col = lax.broadcasted_iota(jnp.int32, (tb, tb), 1)
                    s = jnp.where(col <= row, s, NEG)
                ds = jnp.exp(s) * _dot(dob_ref[hh], va_ref[hh], _DIMS["nt"])
                dcs_ref[0, 0, hh:hh + 1, :] = jnp.sum(ds, axis=0, keepdims=True)
                acc_s[hh] += _dot(ds.astype(BF16), ka_ref[hh], _DIMS["nn"])

        @pl.when(ki < qi)
        def _():
            step(False)

        @pl.when(ki == qi)
        def _():
            step(True)
            lane = lax.broadcasted_iota(jnp.int32, (tb, 128), 1)
            dq_ref[...] = (_pair_lanes(lane, acc_s[0], acc_s[1]) * FOX_SCALE).astype(dq_ref.dtype)

    qs = pl.BlockSpec((2, tb, 128), lambda p, t, qt, kt: (p, qt[t], 0))
    ks = pl.BlockSpec((2, tb, 128), lambda p, t, qt, kt: (p, kt[t], 0))
    return pl.pallas_call(
        body, name=name,
        grid_spec=pltpu.PrefetchScalarGridSpec(
            num_scalar_prefetch=2, grid=(FOX_PAIRS, qtab.shape[0]), in_specs=[qs, ks, ks, qs],
            out_specs=[pl.BlockSpec((tb, 128), lambda p, t, qt, kt: (qt[t], p)),
                       pl.BlockSpec((1, 1, 2, tb), lambda p, t, qt, kt: (p, qt[t], 0, kt[t]))],
            scratch_shapes=[pltpu.VMEM((2, tb, 128), F32)]),
        out_shape=[jax.ShapeDtypeStruct((S, FOX_HEADS * FOX_DH), BF16),
                   jax.ShapeDtypeStruct((FOX_PAIRS, nb, 2, S), F32)],
        compiler_params=_cparams(("parallel", "arbitrary")),
    )(qtab, ktab, qb, ka, va, dob)


def _fox_bwd_dkv2(qb, ka, va, dob, *, name, tb=512):
    S = qb.shape[1]
    tb = min(tb, S)
    nb = S // tb
    qtab, ktab = _tri_tables(nb, False)

    def body(qt_ref, kt_ref, qb_ref, ka_ref, va_ref, dob_ref, dk_ref, dv_ref, dk_s, dv_s):
        qi, ki = qt_ref[pl.program_id(1)], kt_ref[pl.program_id(1)]

        @pl.when(qi == ki)
        def _():
            dk_s[...] = jnp.zeros_like(dk_s)
            dv_s[...] = jnp.zeros_like(dv_s)

        def step(masked):
            for hh in range(2):
                st = _dot(ka_ref[hh], qb_ref[hh], _DIMS["nt"])
                if masked:
                    row = lax.broadcasted_iota(jnp.int32, (tb, tb), 0)
                    col = lax.broadcasted_iota(jnp.int32, (tb, tb), 1)
                    st = jnp.where(row <= col, st, NEG)
                pt = jnp.exp(st)
                dst = pt * _dot(va_ref[hh], dob_ref[hh], _DIMS["nt"])
                dv_s[hh] += _dot(pt.astype(BF16), dob_ref[hh], _DIMS["nn"])
                dk_s[hh] += _dot(dst.astype(BF16), qb_ref[hh], _DIMS["nn"])

        @pl.when(qi > ki)
        def _():
            step(False)

        @pl.when(qi == ki)
        def _():
            step(True)

        @pl.when(qi == nb - 1)
        def _():
            lane = lax.broadcasted_iota(jnp.int32, (tb, 128), 1)
            dk_ref[...] = _pair_lanes(lane, dk_s[0], dk_s[1]).astype(dk_ref.dtype)
            dv_ref[...] = _pair_lanes(lane, dv_s[0], dv_s[1]).astype(dv_ref.dtype)

    ks = pl.BlockSpec((2, tb, 128), lambda p, t, qt, kt: (p, kt[t], 0))
    qs = pl.BlockSpec((2, tb, 128), lambda p, t, qt, kt: (p, qt[t], 0))
    tok = pl.BlockSpec((tb, 128), lambda p, t, qt, kt: (kt[t], p))
    big = jax.ShapeDtypeStruct((S, FOX_HEADS * FOX_DH), BF16)
    return pl.pallas_call(
        body, name=name,
        grid_spec=pltpu.PrefetchScalarGridSpec(
            num_scalar_prefetch=2, grid=(FOX_PAIRS, qtab.shape[0]), in_specs=[qs, ks, ks, qs], out_specs=[tok, tok],
            scratch_shapes=[pltpu.VMEM((2, tb, 128), F32), pltpu.VMEM((2, tb, 128), F32)]),
        out_shape=[big, big],
        compiler_params=_cparams(("parallel", "arbitrary")),
    )(qtab, ktab, qb, ka, va, dob)


def _merge_fwd(proj, pa, pb, *, name, T=512):
    S, D = pa.shape
    T = min(T, S)

    def body(ga_ref, gb_ref, pa_ref, pb_ref, m_ref):
        m_ref[...] = (_sigmoid(ga_ref[...]) * pa_ref[...] + _sigmoid(gb_ref[...]) * pb_ref[...]).astype(m_ref.dtype)

    tok = pl.BlockSpec((T, D), lambda i: (i, 0))
    return pl.pallas_call(
        body, name=name, grid=(S // T,),
        in_specs=[pl.BlockSpec((T, D), lambda i: (i, 7)), pl.BlockSpec((T, D), lambda i: (i, 8)), tok, tok],
        out_specs=tok, out_shape=jax.ShapeDtypeStruct((S, D), BF16),
        compiler_params=_cparams(("parallel",)),
    )(proj, proj, pa, pb)


def _merge_bwd(proj, pa, pb, dm, *, name, T=512):
    S, D = pa.shape
    T = min(T, S)

    def body(ga_ref, gb_ref, pa_ref, pb_ref, dm_ref, dpa_ref, dpb_ref, dga_ref, dgb_ref):
        dm_ = dm_ref[...]
        sa, sb = _sigmoid(ga_ref[...]), _sigmoid(gb_ref[...])
        dpa_ref[...] = (dm_ * sa).astype(BF16)
        dpb_ref[...] = (dm_ * sb).astype(BF16)
        dga_ref[...] = (dm_ * pa_ref[...] * sa * (1.0 - sa)).astype(BF16)
        dgb_ref[...] = (dm_ * pb_ref[...] * sb * (1.0 - sb)).astype(BF16)

    tok = pl.BlockSpec((T, D), lambda i: (i, 0))
    big = jax.ShapeDtypeStruct((S, D), BF16)
    return pl.pallas_call(
        body, name=name, grid=(S // T,),
        in_specs=[pl.BlockSpec((T, D), lambda i: (i, 7)), pl.BlockSpec((T, D), lambda i: (i, 8)), tok, tok, tok],
        out_specs=[tok, tok, tok, tok], out_shape=[big, big, big, big],
        compiler_params=_cparams(("parallel",)),
    )(proj, proj, pa, pb, dm)


INV_SQRT2 = 0.7071067811865476
INV_SQRT2PI = 0.3989422804014327


def _shifted(u, prev, rid):
    m1 = jnp.where(rid == 0, prev[7:8, :], pltpu.roll(u, 1, 0))
    m2 = jnp.where(rid == 0, prev[6:7, :], jnp.where(rid == 1, prev[7:8, :], pltpu.roll(u, 2, 0)))
    return m1, m2


def _conv_acc(u, prev, w_ref, b_ref, rid):
    m1, m2 = _shifted(u, prev, rid)
    return b_ref[...] + w_ref[0:1, :] * m2 + w_ref[1:2, :] * m1 + w_ref[2:3, :] * u, m1, m2


def _convglu_fwd(ug, uv, wg, wv, bg, bv, *, name, T=512, tc=256):
    S, F = ug.shape
    T = min(T, S)

    def body(ug_ref, uv_ref, wg_ref, wv_ref, bg_ref, bv_ref, a_ref, pg, pv):
        @pl.when(pl.program_id(1) == 0)
        def _():
            pg[...] = jnp.zeros_like(pg)
            pv[...] = jnp.zeros_like(pv)

        rid = lax.broadcasted_iota(jnp.int32, (T, tc), 0)
        g_, v_ = ug_ref[...], uv_ref[...]
        accg, _, _ = _conv_acc(g_, pg[...], wg_ref, bg_ref, rid)
        accv, _, _ = _conv_acc(v_, pv[...], wv_ref, bv_ref, rid)
        gel = 0.5 * accg * (1.0 + lax.erf(accg * INV_SQRT2))
        a_ref[...] = (gel * accv).astype(a_ref.dtype)
        pg[...] = g_[T - 8:T, :]
        pv[...] = v_[T - 8:T, :]

    tok = pl.BlockSpec((T, tc), lambda j, t: (t, j))
    w3 = pl.BlockSpec((3, tc), lambda j, t: (0, j))
    b1 = pl.BlockSpec((1, tc), lambda j, t: (0, j))
    return pl.pallas_call(
        body, name=name, grid=(F // tc, S // T),
        in_specs=[tok, tok, w3, w3, b1, b1], out_specs=tok,
        out_shape=jax.ShapeDtypeStruct((S, F), BF16),
        scratch_shapes=[pltpu.VMEM((8, tc), F32), pltpu.VMEM((8, tc), F32)],
        compiler_params=_cparams(("parallel", "arbitrary")),
    )(ug, uv, wg, wv, bg, bv)


def _convglu_bwd_acc(ug, uv, wg, wv, bg, bv, da, *, name, T=512, tc=256):
    S, F = ug.shape
    T = min(T, S)

    def body(ug_ref, uv_ref, wg_ref, wv_ref, bg_ref, bv_ref, da_ref,
             dg_ref, dv_ref, dwg_ref, dwv_ref, dbg_ref, dbv_ref, pg, pv):
        @pl.when(pl.program_id(1) == 0)
        def _():
            pg[...] = jnp.zeros_like(pg)
            pv[...] = jnp.zeros_like(pv)
            for r in (dwg_ref, dwv_ref, dbg_ref, dbv_ref):
                r[...] = jnp.zeros_like(r)

        rid = lax.broadcasted_iota(jnp.int32, (T, tc), 0)
        g_, v_ = ug_ref[...], uv_ref[...]
        accg, g1, g2 = _conv_acc(g_, pg[...], wg_ref, bg_ref, rid)
        accv, v1, v2 = _conv_acc(v_, pv[...], wv_ref, bv_ref, rid)
        cdf = 0.5 * (1.0 + lax.erf(accg * INV_SQRT2))
        pdf = INV_SQRT2PI * jnp.exp(-0.5 * accg * accg)
        da_ = da_ref[...].astype(F32)
        dgate = da_ * accv * (cdf + accg * pdf)
        dval = da_ * (accg * cdf)
        dg_ref[...] = dgate.astype(dg_ref.dtype)
        dv_ref[...] = dval.astype(dv_ref.dtype)
        dbg_ref[...] += jnp.sum(dgate, axis=0, keepdims=True)
        dbv_ref[...] += jnp.sum(dval, axis=0, keepdims=True)
        for j, (sg_, sv_) in enumerate(((g2, v2), (g1, v1), (g_, v_))):
            dwg_ref[j:j + 1, :] += jnp.sum(dgate * sg_, axis=0, keepdims=True)
            dwv_ref[j:j + 1, :] += jnp.sum(dval * sv_, axis=0, keepdims=True)
        pg[...] = g_[T - 8:T, :]
        pv[...] = v_[T - 8:T, :]

    tok = pl.BlockSpec((T, tc), lambda j, t: (t, j))
    w3 = pl.BlockSpec((3, tc), lambda j, t: (0, j))
    b1 = pl.BlockSpec((1, tc), lambda j, t: (0, j))
    big = jax.ShapeDtypeStruct((S, F), BF16)
    return pl.pallas_call(
        body, name=name, grid=(F // tc, S // T),
        in_specs=[tok, tok, w3, w3, b1, b1, tok], out_specs=[tok, tok, w3, w3, b1, b1],
        out_shape=[big, big, jax.ShapeDtypeStruct((3, F), F32), jax.ShapeDtypeStruct((3, F), F32),
                   jax.ShapeDtypeStruct((1, F), F32), jax.ShapeDtypeStruct((1, F), F32)],
        scratch_shapes=[pltpu.VMEM((8, tc), F32), pltpu.VMEM((8, tc), F32)],
        compiler_params=_cparams(("parallel", "arbitrary")),
    )(ug, uv, wg, wv, bg, bv, da)


def _conv_bwd_u(dacc, w, *, name, T=512, tc=256):
    S, F = dacc.shape
    T = min(T, S)
    nT = S // T

    def body(d_ref, w_ref, du_ref, nxt):
        @pl.when(pl.program_id(1) == 0)
        def _():
            nxt[...] = jnp.zeros_like(nxt)

        rid = lax.broadcasted_iota(jnp.int32, (T, tc), 0)
        d = d_ref[...].astype(F32)
        nx = nxt[...]
        p1 = jnp.where(rid == T - 1, nx[0:1, :], pltpu.roll(d, T - 1, 0))
        p2 = jnp.where(rid == T - 1, nx[1:2, :], jnp.where(rid == T - 2, nx[0:1, :], pltpu.roll(d, T - 2, 0)))
        du_ref[...] = (w_ref[2:3, :] * d + w_ref[1:2, :] * p1 + w_ref[0:1, :] * p2).astype(du_ref.dtype)
        nxt[...] = d[0:8, :]

    tok = pl.BlockSpec((T, tc), lambda j, t: (nT - 1 - t, j))
    return pl.pallas_call(
        body, name=name, grid=(F // tc, nT),
        in_specs=[tok, pl.BlockSpec((3, tc), lambda j, t: (0, j))], out_specs=tok,
        out_shape=jax.ShapeDtypeStruct((S, F), BF16),
        scratch_shapes=[pltpu.VMEM((8, tc), F32)],
        compiler_params=_cparams(("parallel", "arbitrary")),
    )(dacc, w)


def _local_step(x, tgt, w, p):
    S = x.shape[0]
    mm = _matmul
    n1 = _rms_fwd(x, p["norm_mix"], name="rms1_fwd")
    proj = mm(n1, w["wm"], "nn", name="proj_main")
    ff = mm(n1, w["wff"], "nn", name="proj_ff")
    lb = _lb_fwd(p["hg_lb_logits"], name="lb_fwd")
    gnorm = p["hg_norm"].reshape(1, HG_DV)
    o_hg, oa, states = _hgrn_fwd(proj, lb, gnorm, name="hgrn_fwd")
    bias = jnp.pad(p["fox_f_bias"].reshape(1, FOX_HEADS), ((0, 0), (0, 128 - FOX_HEADS)))
    c = _fox_gate_fwd(ff, bias, name="fox_gate_fwd")
    qa, ka, va = _fox_prep(proj, c, name="fox_prep")
    ob, qb = _fox_fwd2(qa, ka, va, name="fox_fwd")
    pa = mm(oa, w["wa"], "nn", name="branch_a")
    pb = mm(ob, w["wb"], "nn", name="branch_b")
    merged = _merge_fwd(proj, pa, pb, name="merge_fwd")
    h1 = mm(merged, w["wo"], "nn", addend=x, name="mix_out")
    n2 = _rms_fwd(h1, p["norm_ffn"], name="rms2_fwd")
    ug = mm(n2, w["wug"], "nn", name="up_gate")
    uv = mm(n2, w["wuv"], "nn", name="up_val")
    a = _convglu_fwd(ug, uv, w["cwg"], w["cwv"], p["cbg"], p["cbv"], name="convglu_fwd")
    h2 = mm(a, w["wd"], "nn", addend=h1, name="ffn_down")
    loss, dh2, d_norm_final = _loss_head(h2, p["norm_final"], tgt, name="loss_head")
    da = mm(dh2, w["wd"], "nt", out_dtype=BF16, name="d_act")
    d_wd = mm(a, dh2, "tn", out_dtype=BF16, name="dw_down")
    daccg, daccv, d_cwg, d_cwv, d_cbg, d_cbv = _convglu_bwd_acc(
        ug, uv, w["cwg"], w["cwv"], p["cbg"], p["cbv"], da, name="convglu_bwd")
    dug = _conv_bwd_u(daccg, w["cwg"], name="conv_bwd_gate")
    duv = _conv_bwd_u(daccv, w["cwv"], name="conv_bwd_val")
    dn2 = mm(dug, w["wug"], "nt", name="dn2_gate")
    dn2 = mm(duv, w["wuv"], "nt", addend=dn2, name="dn2_val")
    d_wug = mm(n2, dug, "tn", out_dtype=BF16, name="dw_up_gate")
    d_wuv = mm(n2, duv, "tn", out_dtype=BF16, name="dw_up_val")
    dh1, d_norm_ffn = _rms_bwd(h1, p["norm_ffn"], dn2, dh2, name="rms2_bwd")
    dmerged = mm(dh1, w["wo"], "nt", name="d_merged")
    d_wo = mm(merged, dh1, "tn", out_dtype=BF16, name="dw_out")
    dpa, dpb, dga, dgb = _merge_bwd(proj, pa, pb, dmerged, name="merge_bwd")
    doa = mm(dpa, w["wa"], "nt", name="d_oa")
    dob = mm(dpb, w["wb"], "nt", out_dtype=BF16, name="d_ob")
    d_wa = mm(oa, dpa, "tn", out_dtype=BF16, name="dw_branch_a")
    d_wb = mm(ob, dpb, "tn", out_dtype=BF16, name="dw_branch_b")
    dhq, dhf, dhi, dhg, dlb, dgn8 = _hgrn_bwd(proj, lb, gnorm, o_hg, states, doa, name="hgrn_bwd")
    d_logits = _lb_bwd(p["hg_lb_logits"], dlb, name="lb_bwd")
    dob_hm = _fox_bwd_prep(ob, dob, name="fox_bwd_prep")
    dq, dcsp = _fox_bwd_dq2(qb, ka, va, dob_hm, name="fox_bwd_dq")
    dk, dv = _fox_bwd_dkv2(qb, ka, va, dob_hm, name="fox_bwd_dkv")
    nb = dcsp.shape[1]
    written = (jnp.arange(S) // (S // nb))[None, None, None, :] <= jnp.arange(nb)[None, :, None, None]
    dcs = jnp.sum(jnp.where(written, dcsp, 0.0), axis=1)
    dcs_tok = jnp.pad(dcs.reshape(FOX_HEADS, S).T, ((0, 0), (0, 128 - FOX_HEADS)))
    dff, dbias = _fox_gate_bwd(ff, bias, dcs_tok, name="fox_gate_bwd")
    dproj = jnp.concatenate([dhq, dhf, dhi, dhg, dq, dk, dv, dga, dgb], axis=1)
    dn1 = mm(dff, w["wff"], "nt", name="dn1_ff")
    dn1 = mm(dproj, w["wm"], "nt", addend=dn1, name="dn1_main")
    d_wm = mm(n1, dproj, "tn", out_dtype=BF16, name="dw_in_main")
    d_wff = mm(n1, dff, "tn", out_dtype=BF16, name="dw_in_ff")
    dx, d_norm_mix = _rms_bwd(x, p["norm_mix"], dn1, dh1, name="rms1_bwd")
    grads = dict(
        wm=d_wm, wff=d_wff, wa=d_wa, wb=d_wb, wo=d_wo, wug=d_wug, wuv=d_wuv, cwg=d_cwg, cwv=d_cwv, wd=d_wd,
        norm_mix=d_norm_mix.reshape(-1), fox_f_bias=dbias[0, :FOX_HEADS], hg_lb_logits=d_logits,
        hg_norm=jnp.sum(dgn8, axis=0).reshape(-1), norm_ffn=d_norm_ffn.reshape(-1), cbg=d_cbg, cbv=d_cbv,
        norm_final=d_norm_final.reshape(-1))
    return loss, dx, grads


MESH = pl.DeviceIdType.MESH
ANY = pl.BlockSpec(memory_space=pl.ANY)


def _all_gather(xs, *, name):
    def body(x_ref, out_ref, send_sems, recv_sems, local_sem):
        x, y, c = lax.axis_index("x"), lax.axis_index("y"), lax.axis_index("c")
        me, sibling = (x, y, c), (x, y, 1 - c)
        chips = [(1 - x, y), (x, 1 - y), (1 - x, 1 - y)]

        def rows(px, py, pc):
            return out_ref.at[4 * px + 2 * py + pc]

        def copy(k, block, to, src=None):
            return pltpu.make_async_remote_copy(
                src_ref=rows(*block) if src is None else src, dst_ref=rows(*block),
                send_sem=send_sems.at[k], recv_sem=recv_sems.at[k], device_id=to, device_id_type=MESH)

        mine = pltpu.make_async_copy(x_ref, rows(*me), local_sem)
        mine.start()
        first = [copy(0, me, sibling, src=x_ref)]
        first += [copy(1 + j, me, (*chip, c), src=x_ref) for j, chip in enumerate(chips)]
        for cp in first:
            cp.start()
        passed = [copy(4 + j, (*chip, c), sibling) for j, chip in enumerate(chips)]
        for j, chip in enumerate(chips):
            copy(1 + j, (*chip, c), me).wait_recv()
            passed[j].start()
        copy(0, sibling, me).wait_recv()
        for j, chip in enumerate(chips):
            copy(4 + j, (*chip, 1 - c), me).wait_recv()
        for cp in first + passed:
            cp.wait_send()
        mine.wait()

    return pl.pallas_call(
        body, name=name, in_specs=[ANY], out_specs=ANY,
        out_shape=jax.ShapeDtypeStruct((N_DEV,) + xs.shape, xs.dtype),
        scratch_shapes=[pltpu.SemaphoreType.DMA((7,)), pltpu.SemaphoreType.DMA((7,)), pltpu.SemaphoreType.DMA],
    )(xs)


def _exchange_blocks(g, *, name):
    def body(g_ref, out_ref, send_sems, recv_sems, local_sem):
        x, y, c = lax.axis_index("x"), lax.axis_index("y"), lax.axis_index("c")
        me = 4 * x + 2 * y + c
        mine = pltpu.make_async_copy(g_ref.at[me], out_ref.at[me], local_sem)
        mine.start()
        sends, recvs = [], []
        for k in range(1, N_DEV):
            px = 1 - x if k & 4 else x
            py = 1 - y if k & 2 else y
            pc = 1 - c if k & 1 else c
            p = 4 * px + 2 * py + pc
            sends.append(pltpu.make_async_remote_copy(
                src_ref=g_ref.at[p], dst_ref=out_ref.at[me], send_sem=send_sems.at[k - 1], recv_sem=recv_sems.at[k - 1],
                device_id=(px, py, pc), device_id_type=MESH))
            recvs.append(pltpu.make_async_remote_copy(
                src_ref=g_ref.at[p], dst_ref=out_ref.at[p], send_sem=send_sems.at[k - 1], recv_sem=recv_sems.at[k - 1],
                device_id=(px, py, pc), device_id_type=MESH))
        for cp in sends:
            cp.start()
        for cp in recvs:
            cp.wait_recv()
        for cp in sends:
            cp.wait_send()
        mine.wait()

    return pl.pallas_call(
        body, name=name, in_specs=[ANY], out_specs=ANY,
        out_shape=jax.ShapeDtypeStruct(g.shape, g.dtype),
        scratch_shapes=[pltpu.SemaphoreType.DMA((7,)), pltpu.SemaphoreType.DMA((7,)), pltpu.SemaphoreType.DMA],
    )(g)


def _adamw(parts, w, m, v, *, name, T=512):
    R, L = w.shape
    c1 = 1.0 / (1.0 - ADAM_B1 ** ADAM_STEP)
    c2 = 1.0 / (1.0 - ADAM_B2 ** ADAM_STEP)

    def body(p_ref, w_ref, m_ref, v_ref, g_ref, d_ref, nm_ref, nv_ref):
        g = p_ref[0]
        for s in range(1, N_DEV):
            g = g + p_ref[s]
        g_ref[...] = g
        nm = ADAM_B1 * m_ref[...] + (1.0 - ADAM_B1) * g
        nv = ADAM_B2 * v_ref[...] + (1.0 - ADAM_B2) * (g * g)
        nm_ref[...] = nm
        nv_ref[...] = nv
        d_ref[...] = -ADAM_LR * ((nm * c1) / (jnp.sqrt(nv * c2) + ADAM_EPS) + ADAM_WD * w_ref[...])

    blk = pl.BlockSpec((T, L), lambda i: (i, 0))
    out = jax.ShapeDtypeStruct((R, L), F32)
    return pl.pallas_call(
        body, name=name, grid=(R // T,),
        in_specs=[pl.BlockSpec((N_DEV, T, L), lambda i: (0, i, 0)), blk, blk, blk],
        out_specs=[blk, blk, blk, blk], out_shape=[out, out, out, out],
        compiler_params=_cparams(("parallel",)),
    )(parts, w, m, v)


D_IN = 9232
FF_LO, FF_HI = 7168, 7184
IN_SH, UP_SH, DOWN_SH = D_IN // N_DEV, 2 * D_FF // N_DEV, D_FF // N_DEV
SQ_SH = D_MODEL // N_DEV

BIG = [("w_in", (1, D_MODEL, IN_SH)), ("w_branch_a", (1, SQ_SH, D_MODEL)), ("w_branch_b", (1, SQ_SH, D_MODEL)),
       ("w_out", (1, SQ_SH, D_MODEL)), ("w_up", (1, D_MODEL, UP_SH)), ("conv_w", (1, 3, UP_SH)),
       ("w_down", (1, DOWN_SH, D_MODEL))]
SMALL = [("norm_mix", (1, D_MODEL)), ("fox_f_bias", (1, FOX_HEADS)), ("hg_lb_logits", (2, HG_HEADS * HG_DK)),
         ("hg_norm", (1, HG_DV)), ("norm_ffn", (1, D_MODEL)), ("conv_b", (1, 2 * D_FF)), ("norm_final", (D_MODEL,))]
NAMES = ["norm_mix", "w_in", "fox_f_bias", "hg_lb_logits", "hg_norm", "w_branch_a", "w_branch_b", "w_out",
         "norm_ffn", "w_up", "conv_w", "conv_b", "w_down", "norm_final"]


def _size(shape):
    n = 1
    for s in shape:
        n *= s
    return n


PACK_ROWS = 20992
GATHER_ROWS = 20800
assert sum(_size(s) for _, s in BIG + SMALL) <= PACK_ROWS * 128


def _pack_rows(flat_parts, rows):
    flat = jnp.concatenate(flat_parts, axis=-1)
    pad = rows * 128 - flat.shape[-1]
    flat = jnp.pad(flat, [(0, 0)] * (flat.ndim - 1) + [(0, pad)])
    return flat.reshape(flat.shape[:-1] + (rows, 128))


def _pack_shard(vals):
    return _pack_rows([vals[n].reshape(1, -1).astype(F32) for n, _ in BIG + SMALL], PACK_ROWS)[0]


def _unpack_shard(buf):
    flat = buf.reshape(-1)
    out, off = {}, 0
    for n, shape in BIG + SMALL:
        out[n] = flat[off:off + _size(shape)].reshape(shape)
        off += _size(shape)
    return out


def _cols_by_device(a, width):
    rows = a.shape[0]
    return a.reshape(rows, N_DEV, width).transpose(1, 0, 2).reshape(N_DEV, rows * width)


def _cols_from_devices(a, rows, width):
    return a.reshape(N_DEV, rows, width).transpose(1, 0, 2).reshape(rows, N_DEV * width)


def _pack_grads(g):
    w_in = jnp.concatenate([g["wm"][:, :FF_LO], g["wff"][:, :FOX_HEADS], g["wm"][:, FF_LO:]], axis=1)
    w_up = jnp.concatenate([g["wug"], g["wuv"]], axis=1)
    conv_w = jnp.concatenate([g["cwg"], g["cwv"]], axis=1)
    conv_b = jnp.concatenate([g["cbg"], g["cbv"]], axis=1)
    big = [_cols_by_device(w_in, IN_SH), g["wa"].reshape(N_DEV, -1), g["wb"].reshape(N_DEV, -1),
           g["wo"].reshape(N_DEV, -1), _cols_by_device(w_up, UP_SH), _cols_by_device(conv_w, UP_SH),
           g["wd"].reshape(N_DEV, -1)]
    small = [g["norm_mix"], g["fox_f_bias"], g["hg_lb_logits"], g["hg_norm"], g["norm_ffn"], conv_b, g["norm_final"]]
    small = [jnp.broadcast_to(s.reshape(1, -1), (N_DEV, s.size)) for s in small]
    return _pack_rows(big + small, PACK_ROWS)


def _gather_weights(w_in, w_a, w_b, w_o, w_up, conv_w, w_down):
    taps = lax.bitcast_convert_type(conv_w.reshape(3, UP_SH), BF16).reshape(1, -1)
    mats = [w_in, w_a, w_b, w_o, w_up, w_down]
    packed = _pack_rows([t.reshape(1, -1).astype(BF16) for t in mats] + [taps], GATHER_ROWS)[0]
    full = _all_gather(packed, name="gather_weights").reshape(N_DEV, -1)
    off = 0

    def take(n):
        nonlocal off
        piece = full[:, off:off + n]
        off += n
        return piece

    win = _cols_from_devices(take(D_MODEL * IN_SH), D_MODEL, IN_SH)
    wa = take(SQ_SH * D_MODEL).reshape(D_MODEL, D_MODEL)
    wb = take(SQ_SH * D_MODEL).reshape(D_MODEL, D_MODEL)
    wo = take(SQ_SH * D_MODEL).reshape(D_MODEL, D_MODEL)
    wup = _cols_from_devices(take(D_MODEL * UP_SH), D_MODEL, UP_SH)
    wd = take(DOWN_SH * D_MODEL).reshape(D_FF, D_MODEL)
    cw = lax.bitcast_convert_type(take(3 * UP_SH * 2).reshape(N_DEV, 3, UP_SH, 2), F32)
    cw = cw.transpose(1, 0, 2).reshape(3, 2 * D_FF)
    return dict(
        wm=jnp.concatenate([win[:, :FF_LO], win[:, FF_HI:]], axis=1),
        wff=jnp.pad(win[:, FF_LO:FF_HI], ((0, 0), (0, 128 - FOX_HEADS))),
        wa=wa, wb=wb, wo=wo, wug=wup[:, :D_FF], wuv=wup[:, D_FF:], cwg=cw[:, :D_FF], cwv=cw[:, D_FF:], wd=wd)


def _peer(k, x, y, c):
    return (1 - x if k & 4 else x, 1 - y if k & 2 else y, 1 - c if k & 1 else c)


def _gather_multi(shards, *, name):
    n = len(shards)

    def body(*refs):
        x_refs, out_refs = refs[:n], refs[n:2 * n]
        send_sems, recv_sems, local_sems = refs[2 * n:]
        x, y, c = lax.axis_index("x"), lax.axis_index("y"), lax.axis_index("c")
        me, sibling = (x, y, c), (x, y, 1 - c)
        chips = [(1 - x, y), (x, 1 - y), (1 - x, 1 - y)]

        def copy(t, k, block, to, src=None):
            slot = out_refs[t].at[4 * block[0] + 2 * block[1] + block[2]]
            return pltpu.make_async_remote_copy(
                src_ref=slot if src is None else src, dst_ref=slot,
                send_sem=send_sems.at[t, k], recv_sem=recv_sems.at[t, k], device_id=to, device_id_type=MESH)

        mine = [pltpu.make_async_copy(x_refs[t], out_refs[t].at[4 * x + 2 * y + c], local_sems.at[t]) for t in range(n)]
        for cp in mine:
            cp.start()
        first = []
        for t in range(n):
            first.append(copy(t, 0, me, sibling, src=x_refs[t]))
            first += [copy(t, 1 + j, me, (*chip, c), src=x_refs[t]) for j, chip in enumerate(chips)]
        for cp in first:
            cp.start()
        passed = []
        for j, chip in enumerate(chips):
            for t in range(n):
                copy(t, 1 + j, (*chip, c), me).wait_recv()
                passed.append(copy(t, 4 + j, (*chip, c), sibling))
                passed[-1].start()
        for t in range(n):
            copy(t, 0, sibling, me).wait_recv()
            for j, chip in enumerate(chips):
                copy(t, 4 + j, (*chip, 1 - c), me).wait_recv()
        for cp in first + passed:
            cp.wait_send()
        for cp in mine:
            cp.wait()

    return pl.pallas_call(
        body, name=name, in_specs=[ANY] * n, out_specs=[ANY] * n,
        out_shape=[jax.ShapeDtypeStruct((N_DEV,) + s.shape, s.dtype) for s in shards],
        scratch_shapes=[pltpu.SemaphoreType.DMA((n, 7)), pltpu.SemaphoreType.DMA((n, 7)), pltpu.SemaphoreType.DMA((n,))],
    )(*shards)


def _exchange_multi(blocks, *, name):
    n = len(blocks)

    def body(*refs):
        g_refs, out_refs = refs[:n], refs[n:2 * n]
        send_sems, recv_sems, local_sems = refs[2 * n:]
        x, y, c = lax.axis_index("x"), lax.axis_index("y"), lax.axis_index("c")
        me = 4 * x + 2 * y + c
        mine = [pltpu.make_async_copy(g_refs[t].at[me], out_refs[t].at[me], local_sems.at[t]) for t in range(n)]
        for cp in mine:
            cp.start()
        sends, recvs = [], []
        for k in range(1, N_DEV):
            px, py, pc = _peer(k, x, y, c)
            p = 4 * px + 2 * py + pc
            for t in range(n):
                sends.append(pltpu.make_async_remote_copy(
                    src_ref=g_refs[t].at[p], dst_ref=out_refs[t].at[me], send_sem=send_sems.at[t, k - 1],
                    recv_sem=recv_sems.at[t, k - 1], device_id=(px, py, pc), device_id_type=MESH))
                recvs.append(pltpu.make_async_remote_copy(
                    src_ref=g_refs[t].at[p], dst_ref=out_refs[t].at[p], send_sem=send_sems.at[t, k - 1],
                    recv_sem=recv_sems.at[t, k - 1], device_id=(px, py, pc), device_id_type=MESH))
        for cp in sends:
            cp.start()
        for cp in recvs:
            cp.wait_recv()
        for cp in sends:
            cp.wait_send()
        for cp in mine:
            cp.wait()

    return pl.pallas_call(
        body, name=name, in_specs=[ANY] * n, out_specs=[ANY] * n,
        out_shape=[jax.ShapeDtypeStruct(b.shape, b.dtype) for b in blocks],
        scratch_shapes=[pltpu.SemaphoreType.DMA((n, 7)), pltpu.SemaphoreType.DMA((n, 7)), pltpu.SemaphoreType.DMA((n,))],
    )(*blocks)


def _adamw2(parts, w, m, v, *, name, T):
    R, C = w.shape
    c1 = 1.0 / (1.0 - ADAM_B1 ** ADAM_STEP)
    c2 = 1.0 / (1.0 - ADAM_B2 ** ADAM_STEP)

    def body(p_ref, w_ref, m_ref, v_ref, g_ref, d_ref, nm_ref, nv_ref):
        g = p_ref[0].astype(F32)
        for s in range(1, N_DEV):
            g = g + p_ref[s].astype(F32)
        g_ref[...] = g
        nm = ADAM_B1 * m_ref[...] + (1.0 - ADAM_B1) * g
        nv = ADAM_B2 * v_ref[...] + (1.0 - ADAM_B2) * (g * g)
        nm_ref[...] = nm
        nv_ref[...] = nv
        d_ref[...] = -ADAM_LR * ((nm * c1) / (jnp.sqrt(nv * c2) + ADAM_EPS) + ADAM_WD * w_ref[...])

    blk = pl.BlockSpec((T, C), lambda i: (i, 0))
    out = jax.ShapeDtypeStruct((R, C), F32)
    return pl.pallas_call(
        body, name=name, grid=(R // T,),
        in_specs=[pl.BlockSpec((N_DEV, T, C), lambda i: (0, i, 0)), blk, blk, blk],
        out_specs=[blk, blk, blk, blk], out_shape=[out, out, out, out],
        compiler_params=_cparams(("parallel",)),
    )(parts, w, m, v)


SMALL_ROWS = 88
SHARDED = [("w_in", (D_MODEL, 1154), 256), ("w_branch_a", (128, D_MODEL), 128), ("w_branch_b", (128, D_MODEL), 128),
           ("w_out", (128, D_MODEL), 128), ("w_up", (D_MODEL, 704), 256), ("conv_w", (3, 704), 3),
           ("w_down", (352, D_MODEL), 352)]


def _col_blocks(a, width):
    return jnp.stack([a[:, d * width:(d + 1) * width] for d in range(N_DEV)])


def _pack_small(vals):
    flat = jnp.concatenate([vals[n].reshape(-1).astype(F32) for n, _ in SMALL])
    return jnp.pad(flat, (0, SMALL_ROWS * 128 - flat.shape[0])).reshape(SMALL_ROWS, 128)


def _unpack_small(buf):
    flat, out, off = buf.reshape(-1), {}, 0
    for n, shape in SMALL:
        out[n] = flat[off:off + _size(shape)].reshape(shape)
        off += _size(shape)
    return out


def _gather_weights2(w_in, w_a, w_b, w_o, w_up, conv_w, w_down):
    shards = [w_in[0].astype(BF16), w_a[0].astype(BF16), w_b[0].astype(BF16), w_o[0].astype(BF16),
              w_up[0].astype(BF16), conv_w[0], w_down[0].astype(BF16)]
    g_in, g_a, g_b, g_o, g_up, g_cw, g_d = _gather_multi(shards, name="gather_weights")
    win = jnp.concatenate([g_in[d] for d in range(N_DEV)], axis=1)
    wup = jnp.concatenate([g_up[d] for d in range(N_DEV)], axis=1)
    cw = jnp.concatenate([g_cw[d] for d in range(N_DEV)], axis=1)
    return dict(
        wm=jnp.concatenate([win[:, :FF_LO], win[:, FF_HI:]], axis=1),
        wff=jnp.pad(win[:, FF_LO:FF_HI], ((0, 0), (0, 128 - FOX_HEADS))),
        wa=g_a.reshape(D_MODEL, D_MODEL), wb=g_b.reshape(D_MODEL, D_MODEL), wo=g_o.reshape(D_MODEL, D_MODEL),
        wug=wup[:, :D_FF], wuv=wup[:, D_FF:], cwg=cw[:, :D_FF], cwv=cw[:, D_FF:], wd=g_d.reshape(D_FF, D_MODEL))


def _grad_blocks(g):
    w_in = jnp.concatenate([g["wm"][:, :FF_LO], g["wff"][:, :FOX_HEADS], g["wm"][:, FF_LO:]], axis=1)
    conv_w = jnp.concatenate([g["cwg"], g["cwv"]], axis=1).astype(F32)
    conv_b = jnp.concatenate([g["cbg"], g["cbv"]], axis=1)
    small = _pack_small(dict(norm_mix=g["norm_mix"], fox_f_bias=g["fox_f_bias"], hg_lb_logits=g["hg_lb_logits"],
                             hg_norm=g["hg_norm"], norm_ffn=g["norm_ffn"], conv_b=conv_b, norm_final=g["norm_final"]))
    up = jnp.stack([g["wug"][:, d * 704:(d + 1) * 704] for d in range(4)]
                   + [g["wuv"][:, d * 704:(d + 1) * 704] for d in range(4)])
    return [_col_blocks(w_in, 1154), g["wa"].reshape(N_DEV, 128, D_MODEL), g["wb"].reshape(N_DEV, 128, D_MODEL),
            g["wo"].reshape(N_DEV, 128, D_MODEL), up, _col_blocks(conv_w, 704), g["wd"].reshape(N_DEV, 352, D_MODEL),
            jnp.broadcast_to(small[None], (N_DEV, SMALL_ROWS, 128))]


def kernel(x, norm_mix, w_in,fox_f_bias, hg_lb_logits, hg_norm, w_branch_a, w_branch_b, w_out, norm_ffn, w_up, conv_w, conv_b, w_down, norm_final, loss_target, m_norm_mix, m_w_in, m_fox_f_bias, m_hg_lb_logits, m_hg_norm, m_w_branch_a, m_w_branch_b, m_w_out, m_norm_ffn, m_w_up, m_conv_w, m_conv_b, m_w_down, m_norm_final, v_norm_mix, v_w_in, v_fox_f_bias, v_hg_lb_logits, v_hg_norm, v_w_branch_a, v_w_branch_b, v_w_out, v_norm_ffn, v_w_up, v_conv_w, v_conv_b, v_w_down, v_norm_final):
    wv = dict(norm_mix=norm_mix, w_in=w_in, fox_f_bias=fox_f_bias, hg_lb_logits=hg_lb_logits, hg_norm=hg_norm,
              w_branch_a=w_branch_a, w_branch_b=w_branch_b, w_out=w_out, norm_ffn=norm_ffn, w_up=w_up, conv_w=conv_w,
              conv_b=conv_b, w_down=w_down, norm_final=norm_final)
    mv = dict(norm_mix=m_norm_mix, w_in=m_w_in, fox_f_bias=m_fox_f_bias, hg_lb_logits=m_hg_lb_logits, hg_norm=m_hg_norm,
              w_branch_a=m_w_branch_a, w_branch_b=m_w_branch_b, w_out=m_w_out, norm_ffn=m_norm_ffn, w_up=m_w_up,
              conv_w=m_conv_w, conv_b=m_conv_b, w_down=m_w_down, norm_final=m_norm_final)
    vv = dict(norm_mix=v_norm_mix, w_in=v_w_in, fox_f_bias=v_fox_f_bias, hg_lb_logits=v_hg_lb_logits, hg_norm=v_hg_norm,
              w_branch_a=v_w_branch_a, w_branch_b=v_w_branch_b, w_out=v_w_out, norm_ffn=v_norm_ffn, w_up=v_w_up,
              conv_w=v_conv_w, conv_b=v_conv_b, w_down=v_w_down, norm_final=v_norm_final)

    w = _gather_weights2(w_in, w_branch_a, w_branch_b, w_out, w_up, conv_w, w_down)
    p = dict(norm_mix=norm_mix[0], fox_f_bias=fox_f_bias[0], hg_lb_logits=hg_lb_logits, hg_norm=hg_norm[0],
             norm_ffn=norm_ffn[0], cbg=conv_b[:, :D_FF], cbv=conv_b[:, D_FF:], norm_final=norm_final)
    loss, dx, grads = _local_step(x[0], loss_target[0], w, p)
    loss = lax.psum(loss[0, 0], ("x", "y", "c"))

    parts = _exchange_multi(_grad_blocks(grads), name="exchange_grads")
    res = {}
    for (n, shape, tile), part in zip(SHARDED, parts):
        outs = _adamw2(part, wv[n].reshape(shape), mv[n].reshape(shape), vv[n].reshape(shape), name="adamw_" + n, T=tile)
        res[n] = [o.reshape(wv[n].shape) for o in outs]
    outs = _adamw2(parts[-1], _pack_small(wv), _pack_small(mv), _pack_small(vv), name="adamw_small", T=SMALL_ROWS)
    small = [_unpack_small(o) for o in outs]
    for n, _ in SMALL:
        res[n] = [s[n] for s in small]
    return (loss, dx[None], *[res[n][0] for n in NAMES], *[res[n][1] for n in NAMES],
            *[res[n][2] for n in NAMES], *[res[n][3] for n in NAMES])


def _lb_fwd(logits, *, name):
    def body(l_ref, lb_ref):
        lb_ref[...] = _sigmoid(l_ref[0:1, :] - l_ref[1:2, :])

    return pl.pallas_call(body, name=name, out_shape=jax.ShapeDtypeStruct((1, logits.shape[1]), F32))(logits)


def _lb_bwd(logits, dlb, *, name):
    def body(l_ref, d_ref, o_ref):
        lbv = _sigmoid(l_ref[0:1, :] - l_ref[1:2, :])
        t = d_ref[...] * lbv * (1.0 - lbv)
        o_ref[0:1, :] = t
        o_ref[1:2, :] = -t

    return pl.pallas_call(body, name=name, out_shape=jax.ShapeDtypeStruct(logits.shape, F32))(logits, dlb)
```

```python
import functools

import numpy as np
import jax
import jax.numpy as jnp
from jax import lax
from jax.experimental import pallas as pl
from jax.experimental.pallas import tpu as pltpu

F32 = jnp.float32
BF16 = jnp.bfloat16

D_MODEL = 1024
HG_HEADS = 8
HG_DK = 128
HG_DV = 128
HG_CHUNK = 64
FOX_HEADS = 16
FOX_DH = 64
D_FF = 2816
EPS = 1e-6
N_DEV = 8

ADAM_LR = 0.001
ADAM_B1 = 0.9
ADAM_B2 = 0.999
ADAM_EPS = 1e-08
ADAM_WD = 0.01
ADAM_STEP = 10

VMEM_LIMIT = 56 * 1024 * 1024


def _cparams(sem):
    return pltpu.CompilerParams(dimension_semantics=sem, vmem_limit_bytes=VMEM_LIMIT)


_DIMS = {
    "nn": (((1,), (0,)), ((), ())),
    "nt": (((1,), (1,)), ((), ())),
    "tn": (((0,), (0,)), ((), ())),
}


def _pick(n, prefs):
    for p in prefs:
        if n % p == 0:
            return p
    return n


MATMUL_VMEM_BUDGET = 36 * 1024 * 1024
MAX_TILE = 1536


def _tile_options(n):
    return [d for d in range(128, min(n, MAX_TILE) + 1, 128) if n % d == 0] or [n]


def _pick_tiles(M, N, tk, nk, sa, sb, so, has_addend, tm, tn):
    best = None
    for cm in ([tm] if tm else _tile_options(M)):
        for cn in ([tn] if tn else _tile_options(N)):
            need = 2 * (cm * tk * sa + tk * cn * sb + cm * cn * so + (cm * cn * 4 if has_addend else 0))
            need += cm * cn * 4 if nk > 1 else 0
            if need <= MATMUL_VMEM_BUDGET and (best is None or cm * cn > best[0] * best[1]
                                               or (cm * cn == best[0] * best[1] and cn > best[1])):
                best = (cm, cn)
    assert best is not None, (M, N, tk)
    return best


def _matmul(a, b, form, *, out_dtype=F32, addend=None, tm=None, tn=None, tk=None, name):
    if form == "nn":
        (M, K), (K2, N) = a.shape, b.shape
    elif form == "nt":
        (M, K), (N, K2) = a.shape, b.shape
    else:
        (K, M), (K2, N) = a.shape, b.shape
    assert K == K2, (a.shape, b.shape, form)
    tk = tk or (K if K <= 2816 else _pick(K, (1024, 512, 256, 128)))
    nk = K // tk
    if tm is None or tn is None:
        tm, tn = _pick_tiles(M, N, tk, nk, a.dtype.itemsize, b.dtype.itemsize, jnp.dtype(out_dtype).itemsize,
                             addend is not None, tm, tn)
    assert M % tm == 0 and N % tn == 0 and K % tk == 0, (M, N, K, tm, tn, tk)
    dims = _DIMS[form]

    def body(*refs):
        a_ref, b_ref = refs[:2]
        add_ref = refs[2] if addend is not None else None
        o_ref = refs[3] if addend is not None else refs[2]

        def finish(r):
            if add_ref is not None:
                r = r + add_ref[...].astype(F32)
            o_ref[...] = r.astype(o_ref.dtype)

        part = lax.dot_general(a_ref[...].astype(BF16), b_ref[...].astype(BF16), dims, preferred_element_type=F32)
        if nk == 1:
            finish(part)
            return
        acc_ref = refs[-1]
        k = pl.program_id(2)

        @pl.when(k == 0)
        def _():
            acc_ref[...] = part

        @pl.when(k > 0)
        def _():
            acc_ref[...] += part

        @pl.when(k == nk - 1)
        def _():
            finish(acc_ref[...])

    if form == "nn":
        a_spec = pl.BlockSpec((tm, tk), lambda i, j, k: (i, k))
        b_spec = pl.BlockSpec((tk, tn), lambda i, j, k: (k, j))
    elif form == "nt":
        a_spec = pl.BlockSpec((tm, tk), lambda i, j, k: (i, k))
        b_spec = pl.BlockSpec((tn, tk), lambda i, j, k: (j, k))
    else:
        a_spec = pl.BlockSpec((tk, tm), lambda i, j, k: (k, i))
        b_spec = pl.BlockSpec((tk, tn), lambda i, j, k: (k, j))
    o_spec = pl.BlockSpec((tm, tn), lambda i, j, k: (i, j))
    in_specs = [a_spec, b_spec] + ([o_spec] if addend is not None else [])
    args = (a, b) + ((addend,) if addend is not None else ())
    return pl.pallas_call(
        body, name=name, grid=(M // tm, N // tn, nk),
        in_specs=in_specs, out_specs=o_spec,
        out_shape=jax.ShapeDtypeStruct((M, N), out_dtype),
        scratch_shapes=[pltpu.VMEM((tm, tn), F32)] if nk > 1 else [],
        compiler_params=_cparams(("parallel", "parallel", "arbitrary")),
    )(*args)


def _rms_fwd(x, g, *, name, tm=512):
    M, D = x.shape
    tm = min(tm, M)

    def body(x_ref, g_ref, n_ref):
        xf = x_ref[...]
        r = lax.rsqrt(jnp.mean(xf * xf, axis=-1, keepdims=True) + EPS)
        n_ref[...] = (xf * r * g_ref[...]).astype(n_ref.dtype)

    return pl.pallas_call(
        body, name=name, grid=(M // tm,),
        in_specs=[pl.BlockSpec((tm, D), lambda i: (i, 0)), pl.BlockSpec((1, D), lambda i: (0, 0))],
        out_specs=pl.BlockSpec((tm, D), lambda i: (i, 0)),
        out_shape=jax.ShapeDtypeStruct((M, D), BF16),
        compiler_params=_cparams(("parallel",)),
    )(x, g.reshape(1, D))


def _rms_bwd(x, g, dn, dres, *, name, tm=512):
    M, D = x.shape
    tm = min(tm, M)

    def body(x_ref, g_ref, dn_ref, dres_ref, dx_ref, dg_ref):
        @pl.when(pl.program_id(0) == 0)
        def _():
            dg_ref[...] = jnp.zeros_like(dg_ref)

        xf = x_ref[...]
        r = lax.rsqrt(jnp.mean(xf * xf, axis=-1, keepdims=True) + EPS)
        xh = xf * r
        dn_ = dn_ref[...].astype(F32)
        dg_ref[...] += jnp.sum(dn_ * xh, axis=0, keepdims=True)
        dxh = dn_ * g_ref[...]
        dx = r * (dxh - xh * jnp.mean(dxh * xh, axis=-1, keepdims=True))
        dx_ref[...] = dres_ref[...] + dx

    row = pl.BlockSpec((tm, D), lambda i: (i, 0))
    vec = pl.BlockSpec((1, D), lambda i: (0, 0))
    return pl.pallas_call(
        body, name=name, grid=(M // tm,),
        in_specs=[row, vec, row, row], out_specs=[row, vec],
        out_shape=[jax.ShapeDtypeStruct((M, D), F32), jax.ShapeDtypeStruct((1, D), F32)],
        compiler_params=_cparams(("arbitrary",)),
    )(x, g.reshape(1, D), dn, dres)


def _loss_head(h, g, tgt, *, name, tm=512):
    M, D = h.shape
    tm = min(tm, M)

    def body(h_ref, g_ref, t_ref, loss_ref, dh_ref, dg_ref):
        @pl.when(pl.program_id(0) == 0)
        def _():
            dg_ref[...] = jnp.zeros_like(dg_ref)
            loss_ref[...] = jnp.zeros_like(loss_ref)

        xf = h_ref[...]
        r = lax.rsqrt(jnp.mean(xf * xf, axis=-1, keepdims=True) + EPS)
        xh = xf * r
        err = xh * g_ref[...] - t_ref[...]
        part = jnp.sum(jnp.mean(err * err, axis=-1, keepdims=True), axis=0, keepdims=True)
        loss_ref[...] += 0.5 * part
        dy = err * (1.0 / D)
        dg_ref[...] += jnp.sum(dy * xh, axis=0, keepdims=True)
        dxh = dy * g_ref[...]
        dh_ref[...] = r * (dxh - xh * jnp.mean(dxh * xh, axis=-1, keepdims=True))

    row = pl.BlockSpec((tm, D), lambda i: (i, 0))
    vec = pl.BlockSpec((1, D), lambda i: (0, 0))
    one = pl.BlockSpec((1, 1), lambda i: (0, 0))
    return pl.pallas_call(
        body, name=name, grid=(M // tm,),
        in_specs=[row, vec, row], out_specs=[one, row, vec],
        out_shape=[jax.ShapeDtypeStruct((1, 1), F32), jax.ShapeDtypeStruct((M, D), F32),
                   jax.ShapeDtypeStruct((1, D), F32)],
        compiler_params=_cparams(("arbitrary",)),
    )(h, g.reshape(1, D), tgt)


HG_MID = HG_CHUNK // 2 - 1
EXP_CAP = 80.0


def _sigmoid(x):
    return 1.0 / (1.0 + jnp.exp(-x))


def _dot(a, b, dims, precision=None):
    return lax.dot_general(a, b, dims, preferred_element_type=F32, precision=precision)


def _bdot(a, b, form):
    return _dot(a.astype(BF16), b.astype(BF16), _DIMS[form])


HG_PREC = "highest"


def _hdot(a, b, form):
    if HG_PREC == "x1":
        return _bdot(a, b, form)
    if HG_PREC == "x3":
        ah, bh = a.astype(BF16), b.astype(BF16)
        al, bl = (a - ah.astype(F32)).astype(BF16), (b - bh.astype(F32)).astype(BF16)
        d = _DIMS[form]
        return _dot(ah, bh, d) + (_dot(ah, bl, d) + _dot(al, bh, d))
    return _dot(a, b, _DIMS[form], precision=lax.Precision.HIGHEST)


def _hgrn_chunk_common(hq, hf, lbv, tril, rid):
    sq = _sigmoid(hq)
    q = hq * sq
    sg = _sigmoid(hf)
    f = lbv + (1.0 - lbv) * sg
    k = (1.0 - lbv) * (1.0 - sg)
    g = jnp.log(f)
    b = _dot(tril, g, _DIMS["nn"], precision=lax.Precision.HIGHEST)
    bref = jnp.sum(jnp.where(rid == HG_MID, b, 0.0), axis=0, keepdims=True)
    bend = jnp.sum(jnp.where(rid == HG_CHUNK - 1, b, 0.0), axis=0, keepdims=True)
    eb = jnp.exp(b)
    e1 = jnp.exp(jnp.minimum(b - bref, EXP_CAP))
    e2 = jnp.exp(jnp.minimum(bref - b, EXP_CAP))
    e3 = jnp.exp(bend - b)
    return sq, q, sg, f, k, bend, eb, e1, e2, e3


def _split2(x):
    hi = x.astype(BF16)
    return hi, (x - hi.astype(F32)).astype(BF16)


def _dot3(a, b, form):
    d = _DIMS[form]
    return _dot(a[0], b[0], d) + (_dot(a[0], b[1], d) + _dot(a[1], b[0], d))


def _hgrn_fwd_phased(proj, lb, gnorm, *, name, T=512):
    S = proj.shape[0]
    T = min(T, S)
    nch = T // HG_CHUNK
    C = HG_CHUNK

    def body(hq_ref, hf_ref, hi_ref, hg_ref, lb_ref, gn_ref, o_ref, oa_ref, st_ref, state):
        @pl.when(pl.program_id(1) == 0)
        def _():
            state[...] = jnp.zeros_like(state)

        lbv = lb_ref[...]
        gn = gn_ref[...]
        row = lax.broadcasted_iota(jnp.int32, (C, C), 0)
        col = lax.broadcasted_iota(jnp.int32, (C, C), 1)
        causal = row >= col
        tril = causal.astype(F32)
        rid = lax.broadcasted_iota(jnp.int32, (C, HG_DK), 0)
        sls = [pl.ds(c * C, C) for c in range(nch)]
        pre = [_hgrn_chunk_common(hq_ref[sl, :], hf_ref[sl, :], lbv, tril, rid) for sl in sls]
        v2 = [_split2(hi_ref[sl, :]) for sl in sls]
        a_l, u_l = [], []
        for c in range(nch):
            _, q, _, _, k, _, _, e1, e2, e3 = pre[c]
            a_l.append(jnp.where(causal, _dot3(_split2(q * e1), _split2(k * e2), "nt"), 0.0))
            u_l.append(_dot3(v2[c], _split2(k * e3), "tn"))
        o_l = [_dot3(_split2(a_l[c]), v2[c], "nn") for c in range(nch)]
        st = state[...]
        st_l = []
        for c in range(nch):
            st_l.append(st)
            st = st * jnp.exp(pre[c][5]) + u_l[c]
        state[...] = st
        for c in range(nch):
            st_ref[0, c] = st_l[c]
            o_l[c] = o_l[c] + _dot3(_split2(pre[c][1] * pre[c][6]), _split2(st_l[c]), "nt")
        for c in range(nch):
            o, hg = o_l[c], hg_ref[sls[c], :]
            o_ref[sls[c], :] = o
            r = lax.rsqrt(jnp.mean(o * o, axis=-1, keepdims=True) + EPS)
            oa_ref[sls[c], :] = (o * r * gn * (hg * _sigmoid(hg))).astype(oa_ref.dtype)

    def grp(gidx):
        return pl.BlockSpec((T, 128), lambda h, t: (t, gidx * 8 + h))

    return pl.pallas_call(
        body, name=name, grid=(HG_HEADS, S // T),
        in_specs=[grp(0), grp(1), grp(2), grp(3),
                  pl.BlockSpec((1, 128), lambda h, t: (0, h)), pl.BlockSpec((1, 128), lambda h, t: (0, 0))],
        out_specs=[pl.BlockSpec((T, 128), lambda h, t: (t, h)), pl.BlockSpec((T, 128), lambda h, t: (t, h)),
                   pl.BlockSpec((1, nch, HG_DV, HG_DK), lambda h, t: (h, t, 0, 0))],
        out_shape=[jax.ShapeDtypeStruct((S, HG_HEADS * HG_DV), F32), jax.ShapeDtypeStruct((S, HG_HEADS * HG_DV), BF16),
                   jax.ShapeDtypeStruct((HG_HEADS, S // C, HG_DV, HG_DK), F32)],
        scratch_shapes=[pltpu.VMEM((HG_DV, HG_DK), F32)],
        compiler_params=_cparams(("parallel", "arbitrary")),
    )(proj, proj, proj, proj, lb, gnorm)


def _hgrn_bwd_phased(proj, lb, gnorm, o, states, doa, *, name, T=512):
    S = proj.shape[0]
    T = min(T, S)
    nch = T // HG_CHUNK
    C = HG_CHUNK
    nT = S // T

    def body(hq_ref, hf_ref, hi_ref, hg_ref, lb_ref, gn_ref, o_ref, st_ref, doa_ref,
             dhq_ref, dhf_ref, dhi_ref, dhg_ref, dlb_ref, dgn_ref, dstate):
        @pl.when(pl.program_id(1) == 0)
        def _():
            dstate[...] = jnp.zeros_like(dstate)
            dlb_ref[...] = jnp.zeros_like(dlb_ref)
            dgn_ref[...] = jnp.zeros_like(dgn_ref)

        lbv = lb_ref[...]
        gn = gn_ref[...]
        row = lax.broadcasted_iota(jnp.int32, (C, C), 0)
        col = lax.broadcasted_iota(jnp.int32, (C, C), 1)
        causal = row >= col
        tril = causal.astype(F32)
        triu = (row <= col).astype(F32)
        rid = lax.broadcasted_iota(jnp.int32, (C, HG_DK), 0)
        rng = range(nch)
        sls = [pl.ds(c * C, C) for c in rng]
        pre = [_hgrn_chunk_common(hq_ref[sl, :], hf_ref[sl, :], lbv, tril, rid) for sl in sls]
        do2, dgn_acc = [], jnp.zeros((1, HG_DV), F32)
        for c in rng:
            hg, ov = hg_ref[sls[c], :], o_ref[sls[c], :]
            r = lax.rsqrt(jnp.mean(ov * ov, axis=-1, keepdims=True) + EPS)
            xh = ov * r
            sgg = _sigmoid(hg)
            d_oa = doa_ref[sls[c], :].astype(F32)
            dz = d_oa * (hg * sgg)
            dhg_ref[sls[c], :] = (d_oa * (xh * gn) * (sgg * (1.0 + hg * (1.0 - sgg)))).astype(dhg_ref.dtype)
            dgn_acc = dgn_acc + jnp.sum(dz * xh, axis=0, keepdims=True)
            dxh = dz * gn
            do2.append(_split2(r * (dxh - xh * jnp.mean(dxh * xh, axis=-1, keepdims=True))))
        dgn_ref[0] += dgn_acc
        qi = [pre[c][1] * pre[c][6] for c in rng]
        qp = [pre[c][1] * pre[c][7] for c in rng]
        kp = [pre[c][4] * pre[c][8] for c in rng]
        kend = [pre[c][4] * pre[c][9] for c in rng]
        qi2, qp2, kp2, kend2 = ([_split2(t) for t in lst] for lst in (qi, qp, kp, kend))
        v2 = [_split2(hi_ref[sl, :]) for sl in sls]
        st0 = [st_ref[0, c] for c in rng]
        a2 = [_split2(jnp.where(causal, _dot3(qp2[c], kp2[c], "nt"), 0.0)) for c in rng]
        da2 = [_split2(jnp.where(causal, _dot3(do2[c], v2[c], "nt"), 0.0)) for c in rng]
        dqi = [_dot3(do2[c], _split2(st0[c]), "nn") for c in rng]
        w_l = [_dot3(do2[c], qi2[c], "tn") for c in rng]
        ds = dstate[...]
        ds1 = [None] * nch
        for c in reversed(rng):
            ds1[c] = ds
            ds = ds * jnp.exp(pre[c][5]) + w_l[c]
        dstate[...] = ds
        ds12 = [_split2(t) for t in ds1]
        dqp = [_dot3(da2[c], kp2[c], "nn") for c in rng]
        dkp = [_dot3(da2[c], qp2[c], "tn") for c in rng]
        dv = [_dot3(a2[c], do2[c], "tn") + _dot3(kend2[c], ds12[c], "nt") for c in rng]
        dkend = [_dot3(v2[c], ds12[c], "nn") for c in rng]
        dq_l, dk_l, db_l = [], [], []
        for c in rng:
            _, _, _, _, _, bend, eb, e1, e2, e3 = pre[c]
            dq_l.append(dqi[c] * eb + dqp[c] * e1)
            dk_l.append(dkp[c] * e2 + dkend[c] * e3)
            db = dqi[c] * qi[c] + dqp[c] * qp[c] - dkp[c] * kp[c] - dkend[c] * kend[c]
            dbend = (jnp.sum(dkend[c] * kend[c], axis=0, keepdims=True)
                     + jnp.exp(bend) * jnp.sum(ds1[c] * st0[c], axis=0, keepdims=True))
            db_l.append(db + jnp.where(rid == C - 1, dbend, 0.0))
        dg = [_dot(triu, db_l[c], _DIMS["nn"], precision=lax.Precision.HIGHEST) for c in rng]
        dlb_acc = jnp.zeros((1, HG_DK), F32)
        for c in rng:
            sq, _, sg, f, _, _, _, _, _, _ = pre[c]
            hq = hq_ref[sls[c], :]
            df = dg[c] / f - dk_l[c]
            dlb_acc = dlb_acc + jnp.sum(df * (1.0 - sg), axis=0, keepdims=True)
            dhf_ref[sls[c], :] = (df * (1.0 - lbv) * sg * (1.0 - sg)).astype(dhf_ref.dtype)
            dhq_ref[sls[c], :] = (dq_l[c] * (sq * (1.0 + hq * (1.0 - sq)))).astype(dhq_ref.dtype)
            dhi_ref[sls[c], :] = dv[c].astype(dhi_ref.dtype)
        dlb_ref[...] += dlb_acc

    def grp(gidx):
        return pl.BlockSpec((T, 128), lambda h, t: (nT - 1 - t, gidx * 8 + h))

    tok = pl.BlockSpec((T, 128), lambda h, t: (nT - 1 - t, h))
    big = jax.ShapeDtypeStruct((S, HG_HEADS * HG_DV), BF16)
    return pl.pallas_call(
        body, name=name, grid=(HG_HEADS, nT),
        in_specs=[grp(0), grp(1), grp(2), grp(3),
                  pl.BlockSpec((1, 128), lambda h, t: (0, h)), pl.BlockSpec((1, 128), lambda h, t: (0, 0)),
                  tok, pl.BlockSpec((1, nch, HG_DV, HG_DK), lambda h, t: (h, nT - 1 - t, 0, 0)), tok],
        out_specs=[tok, tok, tok, tok, pl.BlockSpec((1, 128), lambda h, t: (0, h)),
                   pl.BlockSpec((1, 1, 128), lambda h, t: (h, 0, 0))],
        out_shape=[big, big, big, big, jax.ShapeDtypeStruct((1, HG_HEADS * HG_DK), F32),
                   jax.ShapeDtypeStruct((HG_HEADS, 1, HG_DV), F32)],
        scratch_shapes=[pltpu.VMEM((HG_DV, HG_DK), F32)],
        compiler_params=_cparams(("parallel", "arbitrary")),
    )(proj, proj, proj, proj, lb, gnorm, o, states, doa)


def _hgrn_fwd(proj, lb, gnorm, *, name, T=512):
    S = proj.shape[0]
    T = min(T, S)
    nch = T // HG_CHUNK
    C = HG_CHUNK

    def body(hq_ref, hf_ref, hi_ref, hg_ref, lb_ref, gn_ref, o_ref, oa_ref, st_ref, state):
        @pl.when(pl.program_id(1) == 0)
        def _():
            state[...] = jnp.zeros_like(state)

        lbv = lb_ref[...]
        gn = gn_ref[...]
        row = lax.broadcasted_iota(jnp.int32, (C, C), 0)
        col = lax.broadcasted_iota(jnp.int32, (C, C), 1)
        causal = row >= col
        tril = causal.astype(F32)
        rid = lax.broadcasted_iota(jnp.int32, (C, HG_DK), 0)
        st = state[...]
        for c in range(nch):
            sl = pl.ds(c * C, C)
            hq, hf, v, hg = hq_ref[sl, :], hf_ref[sl, :], hi_ref[sl, :], hg_ref[sl, :]
            _, q, _, _, k, bend, eb, e1, e2, e3 = _hgrn_chunk_common(hq, hf, lbv, tril, rid)
            st_ref[0, c] = st
            o = _hdot(q * eb, st, "nt")
            a = jnp.where(causal, _hdot(q * e1, k * e2, "nt"), 0.0)
            o = o + _hdot(a, v, "nn")
            st = st * jnp.exp(bend) + _hdot(v, k * e3, "tn")
            o_ref[sl, :] = o
            r = lax.rsqrt(jnp.mean(o * o, axis=-1, keepdims=True) + EPS)
            oa_ref[sl, :] = (o * r * gn * (hg * _sigmoid(hg))).astype(oa_ref.dtype)
        state[...] = st

    def grp(gidx):
        return pl.BlockSpec((T, 128), lambda h, t: (t, gidx * 8 + h))

    return pl.pallas_call(
        body, name=name, grid=(HG_HEADS, S // T),
        in_specs=[grp(0), grp(1), grp(2), grp(3),
                  pl.BlockSpec((1, 128), lambda h, t: (0, h)), pl.BlockSpec((1, 128), lambda h, t: (0, 0))],
        out_specs=[pl.BlockSpec((T, 128), lambda h, t: (t, h)), pl.BlockSpec((T, 128), lambda h, t: (t, h)),
                   pl.BlockSpec((1, nch, HG_DV, HG_DK), lambda h, t: (h, t, 0, 0))],
        out_shape=[jax.ShapeDtypeStruct((S, HG_HEADS * HG_DV), F32), jax.ShapeDtypeStruct((S, HG_HEADS * HG_DV), BF16),
                   jax.ShapeDtypeStruct((HG_HEADS, S // C, HG_DV, HG_DK), F32)],
        scratch_shapes=[pltpu.VMEM((HG_DV, HG_DK), F32)],
        compiler_params=_cparams(("parallel", "arbitrary")),
    )(proj, proj, proj, proj, lb, gnorm)


def _hgrn_bwd(proj, lb, gnorm, o, states, doa, *, name, T=512):
    S = proj.shape[0]
    T = min(T, S)
    nch = T // HG_CHUNK
    C = HG_CHUNK
    nT = S // T

    def body(hq_ref, hf_ref, hi_ref, hg_ref, lb_ref, gn_ref, o_ref, st_ref, doa_ref,
             dhq_ref, dhf_ref, dhi_ref, dhg_ref, dlb_ref, dgn_ref, dstate):
        @pl.when(pl.program_id(1) == 0)
        def _():
            dstate[...] = jnp.zeros_like(dstate)
            dlb_ref[...] = jnp.zeros_like(dlb_ref)
            dgn_ref[...] = jnp.zeros_like(dgn_ref)

        lbv = lb_ref[...]
        gn = gn_ref[...]
        row = lax.broadcasted_iota(jnp.int32, (C, C), 0)
        col = lax.broadcasted_iota(jnp.int32, (C, C), 1)
        causal = row >= col
        tril = causal.astype(F32)
        triu = (row <= col).astype(F32)
        rid = lax.broadcasted_iota(jnp.int32, (C, HG_DK), 0)
        for c in reversed(range(nch)):
            sl = pl.ds(c * C, C)
            hq, hf, v, hg = hq_ref[sl, :], hf_ref[sl, :], hi_ref[sl, :], hg_ref[sl, :]
            sq, q, sg, f, k, bend, eb, e1, e2, e3 = _hgrn_chunk_common(hq, hf, lbv, tril, rid)
            qi, qp, kp, kend = q * eb, q * e1, k * e2, k * e3
            st0 = st_ref[0, c]
            ov = o_ref[sl, :]
            r = lax.rsqrt(jnp.mean(ov * ov, axis=-1, keepdims=True) + EPS)
            xh = ov * r
            sgg = _sigmoid(hg)
            d_oa = doa_ref[sl, :].astype(F32)
            dz = d_oa * (hg * sgg)
            dhg_ref[sl, :] = (d_oa * (xh * gn) * (sgg * (1.0 + hg * (1.0 - sgg)))).astype(dhg_ref.dtype)
            dgn_ref[0] += jnp.sum(dz * xh, axis=0, keepdims=True)
            dxh = dz * gn
            do = r * (dxh - xh * jnp.mean(dxh * xh, axis=-1, keepdims=True))
            ds1 = dstate[...]
            dqi = _hdot(do, st0, "nn")
            a = jnp.where(causal, _hdot(qp, kp, "nt"), 0.0)
            da = jnp.where(causal, _hdot(do, v, "nt"), 0.0)
            dv = _hdot(a, do, "tn") + _hdot(kend, ds1, "nt")
            dqp = _hdot(da, kp, "nn")
            dkp = _hdot(da, qp, "tn")
            dkend = _hdot(v, ds1, "nn")
            dq = dqi * eb + dqp * e1
            dk = dkp * e2 + dkend * e3
            db = dqi * qi + dqp * qp - dkp * kp - dkend * kend
            dbend = (jnp.sum(dkend * kend, axis=0, keepdims=True)
                     + jnp.exp(bend) * jnp.sum(ds1 * st0, axis=0, keepdims=True))
            db = db + jnp.where(rid == C - 1, dbend, 0.0)
            dg = _dot(triu, db, _DIMS["nn"], precision=lax.Precision.HIGHEST)
            df = dg / f - dk
            dlb_ref[...] += jnp.sum(df * (1.0 - sg), axis=0, keepdims=True)
            dhf_ref[sl, :] = (df * (1.0 - lbv) * sg * (1.0 - sg)).astype(dhf_ref.dtype)
            dhq_ref[sl, :] = (dq * (sq * (1.0 + hq * (1.0 - sq)))).astype(dhq_ref.dtype)
            dhi_ref[sl, :] = dv.astype(dhi_ref.dtype)
            dstate[...] = ds1 * jnp.exp(bend) + _hdot(do, qi, "tn")

    def grp(gidx):
        return pl.BlockSpec((T, 128), lambda h, t: (nT - 1 - t, gidx * 8 + h))

    tok = pl.BlockSpec((T, 128), lambda h, t: (nT - 1 - t, h))
    big = jax.ShapeDtypeStruct((S, HG_HEADS * HG_DV), BF16)
    return pl.pallas_call(
        body, name=name, grid=(HG_HEADS, nT),
        in_specs=[grp(0), grp(1), grp(2), grp(3),
                  pl.BlockSpec((1, 128), lambda h, t: (0, h)), pl.BlockSpec((1, 128), lambda h, t: (0, 0)),
                  tok, pl.BlockSpec((1, nch, HG_DV, HG_DK), lambda h, t: (h, nT - 1 - t, 0, 0)), tok],
        out_specs=[tok, tok, tok, tok, pl.BlockSpec((1, 128), lambda h, t: (0, h)),
                   pl.BlockSpec((1, 1, 128), lambda h, t: (h, 0, 0))],
        out_shape=[big, big, big, big, jax.ShapeDtypeStruct((1, HG_HEADS * HG_DK), F32),
                   jax.ShapeDtypeStruct((HG_HEADS, 1, HG_DV), F32)],
        scratch_shapes=[pltpu.VMEM((HG_DV, HG_DK), F32)],
        compiler_params=_cparams(("parallel", "arbitrary")),
    )(proj, proj, proj, proj, lb, gnorm, o, states, doa)


NEG = -1e30
FOX_SCALE = FOX_DH ** -0.5
FOX_PAIRS = FOX_HEADS // 2


def _fox_gate_fwd(ff, bias, *, name, T=512):
    S = ff.shape[0]
    T = min(T, S)

    def body(ff_ref, b_ref, c_ref, carry):
        @pl.when(pl.program_id(0) == 0)
        def _():
            carry[...] = jnp.zeros_like(carry)

        z = ff_ref[...] + b_ref[...]
        logf = jnp.minimum(z, 0.0) - jnp.log(1.0 + jnp.exp(-jnp.abs(z)))
        row = lax.broadcasted_iota(jnp.int32, (T, T), 0)
        col = lax.broadcasted_iota(jnp.int32, (T, T), 1)
        c = _dot((row >= col).astype(F32), logf, _DIMS["nn"], precision=lax.Precision.HIGHEST) + carry[...]
        c_ref[...] = c
        carry[...] = c[T - 1:T, :]

    return pl.pallas_call(
        body, name=name, grid=(S // T,),
        in_specs=[pl.BlockSpec((T, 128), lambda i: (i, 0)), pl.BlockSpec((1, 128), lambda i: (0, 0))],
        out_specs=pl.BlockSpec((T, 128), lambda i: (i, 0)),
        out_shape=jax.ShapeDtypeStruct((S, 128), F32),
        scratch_shapes=[pltpu.VMEM((1, 128), F32)],
        compiler_params=_cparams(("arbitrary",)),
    )(ff, bias)


def _fox_gate_bwd(ff, bias, dcs, *, name, T=512):
    S = ff.shape[0]
    T = min(T, S)
    nT = S // T

    def body(ff_ref, b_ref, d_ref, dff_ref, db_ref, carry):
        @pl.when(pl.program_id(0) == 0)
        def _():
            carry[...] = jnp.zeros_like(carry)
            db_ref[...] = jnp.zeros_like(db_ref)

        row = lax.broadcasted_iota(jnp.int32, (T, T), 0)
        col = lax.broadcasted_iota(jnp.int32, (T, T), 1)
        dlogf = carry[...] - _dot((row <= col).astype(F32), d_ref[...], _DIMS["nn"], precision=lax.Precision.HIGHEST)
        carry[...] = dlogf[0:1, :]
        dff = dlogf * (1.0 - _sigmoid(ff_ref[...] + b_ref[...]))
        dff_ref[...] = dff.astype(dff_ref.dtype)
        db_ref[...] += jnp.sum(dff, axis=0, keepdims=True)

    rev = pl.BlockSpec((T, 128), lambda i: (nT - 1 - i, 0))
    vec = pl.BlockSpec((1, 128), lambda i: (0, 0))
    return pl.pallas_call(
        body, name=name, grid=(nT,),
        in_specs=[rev, vec, rev], out_specs=[rev, vec],
        out_shape=[jax.ShapeDtypeStruct((S, 128), BF16), jax.ShapeDtypeStruct((1, 128), F32)],
        scratch_shapes=[pltpu.VMEM((1, 128), F32)],
        compiler_params=_cparams(("arbitrary",)),
    )(ff, bias, dcs)


def _fox_logits(q, k, cc, cr, qi, ki, tq, tk):
    s = _bdot(q, k, "nt") * FOX_SCALE + cc - cr
    qpos = qi * tq + lax.broadcasted_iota(jnp.int32, (tq, tk), 0)
    kpos = ki * tk + lax.broadcasted_iota(jnp.int32, (tq, tk), 1)
    return jnp.where(kpos <= qpos, s, NEG)


def _fox_fwd(proj, ccol, crow, *, name, tq=512, tk=512):
    S = proj.shape[0]
    tq, tk = min(tq, S), min(tk, S)

    def body(q_ref, k_ref, v_ref, cc_ref, cr_ref, o_ref, lse_ref, m_s, l_s, acc_s):
        qi, ki = pl.program_id(1), pl.program_id(2)

        @pl.when(ki == 0)
        def _():
            m_s[...] = jnp.full_like(m_s, NEG)
            l_s[...] = jnp.zeros_like(l_s)
            acc_s[...] = jnp.zeros_like(acc_s)

        @pl.when(ki <= qi)
        def _():
            for hh in range(2):
                ls = slice(hh * FOX_DH, (hh + 1) * FOX_DH)
                s = _fox_logits(q_ref[:, ls], k_ref[:, ls], cc_ref[0, :, hh:hh + 1], cr_ref[0, hh:hh + 1, :], qi, ki, tq, tk)
                m_old = m_s[hh]
                m_new = jnp.maximum(m_old, jnp.max(s, axis=-1, keepdims=True))
                p = jnp.exp(s - m_new)
                alpha = jnp.exp(m_old - m_new)
                l_s[hh] = alpha * l_s[hh] + jnp.sum(p, axis=-1, keepdims=True)
                p_hi = p.astype(BF16)
                p_lo = (p - p_hi.astype(F32)).astype(BF16)
                vv = v_ref[:, ls].astype(BF16)
                acc_s[hh] = alpha * acc_s[hh] + _bdot(p_hi, vv, "nn") + _bdot(p_lo, vv, "nn")
                m_s[hh] = m_new

        @pl.when(ki == qi)
        def _():
            for hh in range(2):
                o_ref[:, hh * FOX_DH:(hh + 1) * FOX_DH] = acc_s[hh] / l_s[hh]
                lse_ref[0, :, hh:hh + 1] = m_s[hh] + jnp.log(l_s[hh])

    qspec = pl.BlockSpec((tq, 128), lambda p, i, j: (i, 32 + p))
    kspec = pl.BlockSpec((tk, 128), lambda p, i, j: (jnp.minimum(j, i), 40 + p))
    vspec = pl.BlockSpec((tk, 128), lambda p, i, j: (jnp.minimum(j, i), 48 + p))
    ccs = pl.BlockSpec((1, tq, 2), lambda p, i, j: (p, i, 0))
    crs = pl.BlockSpec((1, 2, tk), lambda p, i, j: (p, 0, jnp.minimum(j, i)))
    return pl.pallas_call(
        body, name=name, grid=(FOX_PAIRS, S // tq, S // tk),
        in_specs=[qspec, kspec, vspec, ccs, crs],
        out_specs=[pl.BlockSpec((tq, 128), lambda p, i, j: (i, p)), ccs],
        out_shape=[jax.ShapeDtypeStruct((S, FOX_HEADS * FOX_DH), F32), jax.ShapeDtypeStruct((FOX_PAIRS, S, 2), F32)],
        scratch_shapes=[pltpu.VMEM((2, tq, 1), F32), pltpu.VMEM((2, tq, 1), F32), pltpu.VMEM((2, tq, FOX_DH), F32)],
        compiler_params=_cparams(("parallel", "parallel", "arbitrary")),
    )(proj, proj, proj, ccol, crow)


def _fox_bwd_dq(proj, ccol, crow, o, lse, do, *, name, tq=512, tk=512):
    S = proj.shape[0]
    tq, tk = min(tq, S), min(tk, S)

    def body(q_ref, k_ref, v_ref, cc_ref, cr_ref, o_ref, lse_ref, do_ref, dq_ref, dl_ref, acc_s):
        qi, ki = pl.program_id(1), pl.program_id(2)

        @pl.when(ki == 0)
        def _():
            acc_s[...] = jnp.zeros_like(acc_s)
            for hh in range(2):
                ls = slice(hh * FOX_DH, (hh + 1) * FOX_DH)
                dl_ref[0, :, hh:hh + 1] = jnp.sum(do_ref[:, ls].astype(F32) * o_ref[:, ls], axis=-1, keepdims=True)

        @pl.when(ki <= qi)
        def _():
            for hh in range(2):
                ls = slice(hh * FOX_DH, (hh + 1) * FOX_DH)
                s = _fox_logits(q_ref[:, ls], k_ref[:, ls], cc_ref[0, :, hh:hh + 1], cr_ref[0, hh:hh + 1, :], qi, ki, tq, tk)
                p = jnp.exp(s - lse_ref[0, :, hh:hh + 1])
                dp = _bdot(do_ref[:, ls], v_ref[:, ls], "nt")
                ds = p * (dp - dl_ref[0, :, hh:hh + 1])
                acc_s[hh] += _bdot(ds, k_ref[:, ls], "nn")

        @pl.when(ki == qi)
        def _():
            for hh in range(2):
                dq_ref[:, hh * FOX_DH:(hh + 1) * FOX_DH] = (acc_s[hh] * FOX_SCALE).astype(dq_ref.dtype)

    qspec = pl.BlockSpec((tq, 128), lambda p, i, j: (i, 32 + p))
    kspec = pl.BlockSpec((tk, 128), lambda p, i, j: (jnp.minimum(j, i), 40 + p))
    vspec = pl.BlockSpec((tk, 128), lambda p, i, j: (jnp.minimum(j, i), 48 + p))
    ccs = pl.BlockSpec((1, tq, 2), lambda p, i, j: (p, i, 0))
    crs = pl.BlockSpec((1, 2, tk), lambda p, i, j: (p, 0, jnp.minimum(j, i)))
    tok = pl.BlockSpec((tq, 128), lambda p, i, j: (i, p))
    return pl.pallas_call(
        body, name=name, grid=(FOX_PAIRS, S // tq, S // tk),
        in_specs=[qspec, kspec, vspec, ccs, crs, tok, ccs, tok],
        out_specs=[tok, ccs],
        out_shape=[jax.ShapeDtypeStruct((S, FOX_HEADS * FOX_DH), BF16), jax.ShapeDtypeStruct((FOX_PAIRS, S, 2), F32)],
        scratch_shapes=[pltpu.VMEM((2, tq, FOX_DH), F32)],
        compiler_params=_cparams(("parallel", "parallel", "arbitrary")),
    )(proj, proj, proj, ccol, crow, o, lse, do)


def _fox_bwd_dkv(proj, ccol, crow, lse, delta, do, *, name, tq=512, tk=512):
    S = proj.shape[0]
    tq, tk = min(tq, S), min(tk, S)
    nq = S // tq

    def body(q_ref, k_ref, v_ref, cc_ref, cr_ref, lse_ref, dl_ref, do_ref, dk_ref, dv_ref, dcs_ref, dk_s, dv_s):
        ki, qi = pl.program_id(1), pl.program_id(2)

        @pl.when(qi == 0)
        def _():
            dk_s[...] = jnp.zeros_like(dk_s)
            dv_s[...] = jnp.zeros_like(dv_s)
            dcs_ref[...] = jnp.zeros_like(dcs_ref)

        @pl.when(qi >= ki)
        def _():
            for hh in range(2):
                ls = slice(hh * FOX_DH, (hh + 1) * FOX_DH)
                s = _fox_logits(q_ref[:, ls], k_ref[:, ls], cc_ref[0, :, hh:hh + 1], cr_ref[0, hh:hh + 1, :], qi, ki, tq, tk)
                p = jnp.exp(s - lse_ref[0, :, hh:hh + 1])
                dp = _bdot(do_ref[:, ls], v_ref[:, ls], "nt")
                ds = p * (dp - dl_ref[0, :, hh:hh + 1])
                dv_s[hh] += _bdot(p, do_ref[:, ls], "tn")
                dk_s[hh] += _bdot(ds, q_ref[:, ls], "tn")
                dcs_ref[0, hh:hh + 1, :] += jnp.sum(ds, axis=0, keepdims=True)

        @pl.when(qi == nq - 1)
        def _():
            for hh in range(2):
                dk_ref[:, hh * FOX_DH:(hh + 1) * FOX_DH] = (dk_s[hh] * FOX_SCALE).astype(dk_ref.dtype)
                dv_ref[:, hh * FOX_DH:(hh + 1) * FOX_DH] = dv_s[hh].astype(dv_ref.dtype)

    qspec = pl.BlockSpec((tq, 128), lambda p, j, i: (jnp.maximum(i, j), 32 + p))
    kspec = pl.BlockSpec((tk, 128), lambda p, j, i: (j, 40 + p))
    vspec = pl.BlockSpec((tk, 128), lambda p, j, i: (j, 48 + p))
    ccs = pl.BlockSpec((1, tq, 2), lambda p, j, i: (p, jnp.maximum(i, j), 0))
    crs = pl.BlockSpec((1, 2, tk), lambda p, j, i: (p, 0, j))
    dos = pl.BlockSpec((tq, 128), lambda p, j, i: (jnp.maximum(i, j), p))
    ktok = pl.BlockSpec((tk, 128), lambda p, j, i: (j, p))
    big = jax.ShapeDtypeStruct((S, FOX_HEADS * FOX_DH), BF16)
    return pl.pallas_call(
        body, name=name, grid=(FOX_PAIRS, S // tk, nq),
        in_specs=[qspec, kspec, vspec, ccs, crs, ccs, ccs, dos],
        out_specs=[ktok, ktok, crs],
        out_shape=[big, big, jax.ShapeDtypeStruct((FOX_PAIRS, 2, S), F32)],
        scratch_shapes=[pltpu.VMEM((2, tk, FOX_DH), F32), pltpu.VMEM((2, tk, FOX_DH), F32)],
        compiler_params=_cparams(("parallel", "parallel", "arbitrary")),
    )(proj, proj, proj, ccol, crow, lse, delta, do)


AUG = FOX_DH
FOX_SUB = 2


def _split3(x):
    a = x.astype(BF16).astype(F32)
    r = x - a
    b = r.astype(BF16).astype(F32)
    return a, b, r - b


def _lane_fill(lane, base, pieces, start):
    for i, pc in enumerate(pieces):
        base = jnp.where(lane == start + i, pc, base)
    return base


def _fox_prep(proj, c_tok, *, name, T=512):
    S = proj.shape[0]
    T = min(T, S)

    def body(q_ref, k_ref, v_ref, c_ref, qa_ref, ka_ref, va_ref):
        pair = pl.program_id(0)
        lane = lax.broadcasted_iota(jnp.int32, (T, 128), 1)
        c = c_ref[...]
        ones3 = jnp.where((lane >= AUG) & (lane < AUG + 3), 1.0, 0.0)
        for hh in range(2):
            ch = jnp.sum(jnp.where(lane == 2 * pair + hh, c, 0.0), axis=-1, keepdims=True)
            c1, c2, c3 = _split3(ch)
            q, k, v = q_ref[...], k_ref[...], v_ref[...]
            if hh == 1:
                q, k, v = (pltpu.roll(t, 64, 1) for t in (q, k, v))
            aug_q = _lane_fill(lane, jnp.where((lane >= AUG + 3) & (lane < AUG + 6), 1.0, 0.0), (c1, c2, c3), AUG)
            aug_k = _lane_fill(lane, ones3, (-c1, -c2, -c3), AUG + 3)
            qa_ref[hh] = jnp.where(lane < AUG, q * FOX_SCALE, aug_q).astype(BF16)
            ka_ref[hh] = jnp.where(lane < AUG, k, aug_k).astype(BF16)
            va_ref[hh] = jnp.where(lane < AUG, v, ones3).astype(BF16)

    def grp(g):
        return pl.BlockSpec((T, 128), lambda p, t: (t, g * 8 + p))

    hm = pl.BlockSpec((2, T, 128), lambda p, t: (p, t, 0))
    out = jax.ShapeDtypeStruct((FOX_HEADS, S, 128), BF16)
    return pl.pallas_call(
        body, name=name, grid=(FOX_PAIRS, S // T),
        in_specs=[grp(4), grp(5), grp(6), pl.BlockSpec((T, 128), lambda p, t: (t, 0))],
        out_specs=[hm, hm, hm], out_shape=[out, out, out],
        compiler_params=_cparams(("parallel", "parallel")),
    )(proj, proj, proj, c_tok)


def _pair_lanes(lane, a0, a1):
    return jnp.where(lane < AUG, a0, pltpu.roll(a1, 64, 1))


def _tri_tables(nb, by_query):
    if by_query:
        pairs = [(i, j) for i in range(nb) for j in range(i + 1)]
    else:
        pairs = [(i, j) for j in range(nb) for i in range(j, nb)]
    return (jnp.asarray(np.array([p[0] for p in pairs], np.int32)),
            jnp.asarray(np.array([p[1] for p in pairs], np.int32)))


def _fox_fwd2(qa, ka, va, *, name, tb=512):
    S = qa.shape[1]
    tb = min(tb, S)
    rs = tb // FOX_SUB
    qtab, ktab = _tri_tables(S // tb, True)

    def body(qt_ref, kt_ref, qa_ref, ka_ref, va_ref, o_ref, qb_ref, m_s, acc_s):
        qi, ki = qt_ref[pl.program_id(1)], kt_ref[pl.program_id(1)]

        @pl.when(ki == 0)
        def _():
            m_s[...] = jnp.full_like(m_s, NEG)
            acc_s[...] = jnp.zeros_like(acc_s)

        def step(masked):
            for hh in range(2):
                s = _dot(qa_ref[hh], ka_ref[hh], _DIMS["nt"])
                if masked:
                    row = lax.broadcasted_iota(jnp.int32, (tb, tb), 0)
                    col = lax.broadcasted_iota(jnp.int32, (tb, tb), 1)
                    s = jnp.where(col <= row, s, NEG)
                m_old = m_s[hh]
                m_new = jnp.maximum(m_old, jnp.max(s, axis=-1, keepdims=True))
                p = jnp.exp(s - m_new)
                p_hi = p.astype(BF16)
                p_lo = (p - p_hi.astype(F32)).astype(BF16)
                vv = va_ref[hh]
                acc_s[hh] = (jnp.exp(m_old - m_new) * acc_s[hh]
                             + _dot(p_hi, vv, _DIMS["nn"]) + _dot(p_lo, vv, _DIMS["nn"]))
                m_s[hh] = m_new

        @pl.when(ki < qi)
        def _():
            step(False)

        @pl.when(ki == qi)
        def _():
            step(True)
            lane = lax.broadcasted_iota(jnp.int32, (tb, 128), 1)
            outs = []
            for hh in range(2):
                acc = acc_s[hh]
                l = acc[:, AUG:AUG + 1]
                outs.append(acc / l)
                qf = qa_ref[hh].astype(F32)
                cb = qf[:, AUG:AUG + 1] + qf[:, AUG + 1:AUG + 2] + qf[:, AUG + 2:AUG + 3] - (m_s[hh] + jnp.log(l))
                qb_ref[hh] = _lane_fill(lane, qf, _split3(cb), AUG).astype(BF16)
            o_ref[...] = _pair_lanes(lane, outs[0], outs[1])

    qs = pl.BlockSpec((2, tb, 128), lambda p, t, qt, kt: (p, qt[t], 0))
    ks = pl.BlockSpec((2, tb, 128), lambda p, t, qt, kt: (p, kt[t], 0))
    return pl.pallas_call(
        body, name=name,
        grid_spec=pltpu.PrefetchScalarGridSpec(
            num_scalar_prefetch=2, grid=(FOX_PAIRS, qtab.shape[0]), in_specs=[qs, ks, ks],
            out_specs=[pl.BlockSpec((tb, 128), lambda p, t, qt, kt: (qt[t], p)), qs],
            scratch_shapes=[pltpu.VMEM((2, tb, 1), F32), pltpu.VMEM((2, tb, 128), F32)]),
        out_shape=[jax.ShapeDtypeStruct((S, FOX_HEADS * FOX_DH), F32), jax.ShapeDtypeStruct((FOX_HEADS, S, 128), BF16)],
        compiler_params=_cparams(("parallel", "arbitrary")),
    )(qtab, ktab, qa, ka, va)


def _fox_bwd_prep(o, do, *, name, T=512):
    S = o.shape[0]
    T = min(T, S)

    def body(o_ref, do_ref, dob_ref):
        lane = lax.broadcasted_iota(jnp.int32, (T, 128), 1)
        d = do_ref[...].astype(F32)
        prod = d * o_ref[...]
        for hh in range(2):
            mine = (lane < AUG) if hh == 0 else (lane >= AUG)
            delta = jnp.sum(jnp.where(mine, prod, 0.0), axis=-1, keepdims=True)
            dh = d if hh == 0 else pltpu.roll(d, 64, 1)
            dob_ref[hh] = _lane_fill(lane, jnp.where(lane < AUG, dh, 0.0), _split3(-delta), AUG).astype(BF16)

    tok = pl.BlockSpec((T, 128), lambda p, t: (t, p))
    return pl.pallas_call(
        body, name=name, grid=(FOX_PAIRS, S // T),
        in_specs=[tok, tok], out_specs=pl.BlockSpec((2, T, 128), lambda p, t: (p, t, 0)),
        out_shape=jax.ShapeDtypeStruct((FOX_HEADS, S, 128), BF16),
        compiler_params=_cparams(("parallel", "parallel")),
    )(o, do)


def _fox_bwd_dq2(qb, ka, va, dob, *, name, tb=512):
    S = qb.shape[1]
    tb = min(tb, S)
    rs = tb // FOX_SUB
    nb = S // tb
    qtab, ktab = _tri_tables(nb, True)

    def body(qt_ref, kt_ref, qb_ref, ka_ref, va_ref, dob_ref, dq_ref, dcs_ref, acc_s):
        qi, ki = qt_ref[pl.program_id(1)], kt_ref[pl.program_id(1)]

        @pl.when(ki == 0)
        def _():
            acc_s[...] = jnp.zeros_like(acc_s)

        def step(masked):
            for hh in range(2):
                s = _dot(qb_ref[hh], ka_ref[hh], _DIMS["nt"])
                if masked:
                    row = lax.broadcasted_iota(jnp.int32, (tb, tb), 0)
                    col = lax.broadcasted_iota(jnp.int32, (tb, tb), 1)
                    s = jnp.where(col <= row, s, NEG)
                ds = jnp.exp(s) * _dot(dob_ref[hh], va_ref[hh], _DIMS["nt"])
                dcs_ref[0, 0, hh:hh + 1, :] = jnp.sum(ds, axis=0, keepdims=True)
                acc_s[hh] += _dot(ds.astype(BF16), ka_ref[hh], _DIMS["nn"])

        @pl.when(ki < qi)
        def _():
            step(False)

        @pl.when(ki == qi)
        def _():
            step(True)
            lane = lax.broadcasted_iota(jnp.int32, (tb, 128), 1)
            dq_ref[...] = (_pair_lanes(lane, acc_s[0], acc_s[1]) * FOX_SCALE).astype(dq_ref.dtype)

    qs = pl.BlockSpec((2, tb, 128), lambda p, t, qt, kt: (p, qt[t], 0))
    ks = pl.BlockSpec((2, tb, 128), lambda p, t, qt, kt: (p, kt[t], 0))
    return pl.pallas_call(
        body, name=name,
        grid_spec=pltpu.PrefetchScalarGridSpec(
            num_scalar_prefetch=2, grid=(FOX_PAIRS, qtab.shape[0]), in_specs=[qs, ks, ks, qs],
            out_specs=[pl.BlockSpec((tb, 128), lambda p, t, qt, kt: (qt[t], p)),
                       pl.BlockSpec((1, 1, 2, tb), lambda p, t, qt, kt: (p, qt[t], 0, kt[t]))],
            scratch_shapes=[pltpu.VMEM((2, tb, 128), F32)]),
        out_shape=[jax.ShapeDtypeStruct((S, FOX_HEADS * FOX_DH), BF16),
                   jax.ShapeDtypeStruct((FOX_PAIRS, nb, 2, S), F32)],
        compiler_params=_cparams(("parallel", "arbitrary")),
    )(qtab, ktab, qb, ka, va, dob)


def _fox_bwd_dkv2(qb, ka, va, dob, *, name, tb=512):
    S = qb.shape[1]
    tb = min(tb, S)
    rs = tb // FOX_SUB
    nb = S // tb
    qtab, ktab = _tri_tables(nb, False)

    def body(qt_ref, kt_ref, qb_ref, ka_ref, va_ref, dob_ref, dk_ref, dv_ref, dk_s, dv_s):
        qi, ki = qt_ref[pl.program_id(1)], kt_ref[pl.program_id(1)]

        @pl.when(qi == ki)
        def _():
            dk_s[...] = jnp.zeros_like(dk_s)
            dv_s[...] = jnp.zeros_like(dv_s)

        def step(masked):
            for hh in range(2):
                st = _dot(ka_ref[hh], qb_ref[hh], _DIMS["nt"])
                if masked:
                    row = lax.broadcasted_iota(jnp.int32, (tb, tb), 0)
                    col = lax.broadcasted_iota(jnp.int32, (tb, tb), 1)
                    st = jnp.where(row <= col, st, NEG)
                pt = jnp.exp(st)
                dst = pt * _dot(va_ref[hh], dob_ref[hh], _DIMS["nt"])
                dv_s[hh] += _dot(pt.astype(BF16), dob_ref[hh], _DIMS["nn"])
                dk_s[hh] += _dot(dst.astype(BF16), qb_ref[hh], _DIMS["nn"])

        @pl.when(qi > ki)
        def _():
            step(False)

        @pl.when(qi == ki)
        def _():
            step(True)

        @pl.when(qi == nb - 1)
        def _():
            lane = lax.broadcasted_iota(jnp.int32, (tb, 128), 1)
            dk_ref[...] = _pair_lanes(lane, dk_s[0], dk_s[1]).astype(dk_ref.dtype)
            dv_ref[...] = _pair_lanes(lane, dv_s[0], dv_s[1]).astype(dv_ref.dtype)

    ks = pl.BlockSpec((2, tb, 128), lambda p, t, qt, kt: (p, kt[t], 0))
    qs = pl.BlockSpec((2, tb, 128), lambda p, t, qt, kt: (p, qt[t], 0))
    tok = pl.BlockSpec((tb, 128), lambda p, t, qt, kt: (kt[t], p))
    big = jax.ShapeDtypeStruct((S, FOX_HEADS * FOX_DH), BF16)
    return pl.pallas_call(
        body, name=name,
        grid_spec=pltpu.PrefetchScalarGridSpec(
            num_scalar_prefetch=2, grid=(FOX_PAIRS, qtab.shape[0]), in_specs=[qs, ks, ks, qs], out_specs=[tok, tok],
            scratch_shapes=[pltpu.VMEM((2, tb, 128), F32), pltpu.VMEM((2, tb, 128), F32)]),
        out_shape=[big, big],
        compiler_params=_cparams(("parallel", "arbitrary")),
    )(qtab, ktab, qb, ka, va, dob)


def _merge_fwd(proj, pa, pb, *, name, T=512):
    S, D = pa.shape
    T = min(T, S)

    def body(ga_ref, gb_ref, pa_ref, pb_ref, m_ref):
        m_ref[...] = (_sigmoid(ga_ref[...]) * pa_ref[...] + _sigmoid(gb_ref[...]) * pb_ref[...]).astype(m_ref.dtype)

    tok = pl.BlockSpec((T, D), lambda i: (i, 0))
    return pl.pallas_call(
        body, name=name, grid=(S // T,),
        in_specs=[pl.BlockSpec((T, D), lambda i: (i, 7)), pl.BlockSpec((T, D), lambda i: (i, 8)), tok, tok],
        out_specs=tok, out_shape=jax.ShapeDtypeStruct((S, D), BF16),
        compiler_params=_cparams(("parallel",)),
    )(proj, proj, pa, pb)


def _merge_bwd(proj, pa, pb, dm, *, name, T=512):
    S, D = pa.shape
    T = min(T, S)

    def body(ga_ref, gb_ref, pa_ref, pb_ref, dm_ref, dpa_ref, dpb_ref, dga_ref, dgb_ref):
        dm_ = dm_ref[...]
        sa, sb = _sigmoid(ga_ref[...]), _sigmoid(gb_ref[...])
        dpa_ref[...] = (dm_ * sa).astype(BF16)
        dpb_ref[...] = (dm_ * sb).astype(BF16)
        dga_ref[...] = (dm_ * pa_ref[...] * sa * (1.0 - sa)).astype(BF16)
        dgb_ref[...] = (dm_ * pb_ref[...] * sb * (1.0 - sb)).astype(BF16)

    tok = pl.BlockSpec((T, D), lambda i: (i, 0))
    big = jax.ShapeDtypeStruct((S, D), BF16)
    return pl.pallas_call(
        body, name=name, grid=(S // T,),
        in_specs=[pl.BlockSpec((T, D), lambda i: (i, 7)), pl.BlockSpec((T, D), lambda i: (i, 8)), tok, tok, tok],
        out_specs=[tok, tok, tok, tok], out_shape=[big, big, big, big],
        compiler_params=_cparams(("parallel",)),
    )(proj, proj, pa, pb, dm)


INV_SQRT2 = 0.7071067811865476
INV_SQRT2PI = 0.3989422804014327


def _shifted(u, prev, rid):
    m1 = jnp.where(rid == 0, prev[7:8, :], pltpu.roll(u, 1, 0))
    m2 = jnp.where(rid == 0, prev[6:7, :], jnp.where(rid == 1, prev[7:8, :], pltpu.roll(u, 2, 0)))
    return m1, m2


def _conv_acc(u, prev, w_ref, b_ref, rid):
    m1, m2 = _shifted(u, prev, rid)
    return b_ref[...] + w_ref[0:1, :] * m2 + w_ref[1:2, :] * m1 + w_ref[2:3, :] * u, m1, m2


def _convglu_fwd(ug, uv, wg, wv, bg, bv, *, name, T=512, tc=256):
    S, F = ug.shape
    T = min(T, S)

    def body(ug_ref, uv_ref, wg_ref, wv_ref, bg_ref, bv_ref, a_ref, pg, pv):
        @pl.when(pl.program_id(1) == 0)
        def _():
            pg[...] = jnp.zeros_like(pg)
            pv[...] = jnp.zeros_like(pv)

        rid = lax.broadcasted_iota(jnp.int32, (T, tc), 0)
        g_, v_ = ug_ref[...], uv_ref[...]
        accg, _, _ = _conv_acc(g_, pg[...], wg_ref, bg_ref, rid)
        accv, _, _ = _conv_acc(v_, pv[...], wv_ref, bv_ref, rid)
        gel = 0.5 * accg * (1.0 + lax.erf(accg * INV_SQRT2))
        a_ref[...] = (gel * accv).astype(a_ref.dtype)
        pg[...] = g_[T - 8:T, :]
        pv[...] = v_[T - 8:T, :]

    tok = pl.BlockSpec((T, tc), lambda j, t: (t, j))
    w3 = pl.BlockSpec((3, tc), lambda j, t: (0, j))
    b1 = pl.BlockSpec((1, tc), lambda j, t: (0, j))
    return pl.pallas_call(
        body, name=name, grid=(F // tc, S // T),
        in_specs=[tok, tok, w3, w3, b1, b1], out_specs=tok,
        out_shape=jax.ShapeDtypeStruct((S, F), BF16),
        scratch_shapes=[pltpu.VMEM((8, tc), F32), pltpu.VMEM((8, tc), F32)],
        compiler_params=_cparams(("parallel", "arbitrary")),
    )(ug, uv, wg, wv, bg, bv)


def _convglu_bwd_acc(ug, uv, wg, wv, bg, bv, da, *, name, T=512, tc=256):
    S, F = ug.shape
    T = min(T, S)

    def body(ug_ref, uv_ref, wg_ref, wv_ref, bg_ref, bv_ref, da_ref,
             dg_ref, dv_ref, dwg_ref, dwv_ref, dbg_ref, dbv_ref, pg, pv):
        @pl.when(pl.program_id(1) == 0)
        def _():
            pg[...] = jnp.zeros_like(pg)
            pv[...] = jnp.zeros_like(pv)
            for r in (dwg_ref, dwv_ref, dbg_ref, dbv_ref):
                r[...] = jnp.zeros_like(r)

        rid = lax.broadcasted_iota(jnp.int32, (T, tc), 0)
        g_, v_ = ug_ref[...], uv_ref[...]
        accg, g1, g2 = _conv_acc(g_, pg[...], wg_ref, bg_ref, rid)
        accv, v1, v2 = _conv_acc(v_, pv[...], wv_ref, bv_ref, rid)
        cdf = 0.5 * (1.0 + lax.erf(accg * INV_SQRT2))
        pdf = INV_SQRT2PI * jnp.exp(-0.5 * accg * accg)
        da_ = da_ref[...].astype(F32)
        dgate = da_ * accv * (cdf + accg * pdf)
        dval = da_ * (accg * cdf)
        dg_ref[...] = dgate.astype(dg_ref.dtype)
        dv_ref[...] = dval.astype(dv_ref.dtype)
        dbg_ref[...] += jnp.sum(dgate, axis=0, keepdims=True)
        dbv_ref[...] += jnp.sum(dval, axis=0, keepdims=True)
        for j, (sg_, sv_) in enumerate(((g2, v2), (g1, v1), (g_, v_))):
            dwg_ref[j:j + 1, :] += jnp.sum(dgate * sg_, axis=0, keepdims=True)
            dwv_ref[j:j + 1, :] += jnp.sum(dval * sv_, axis=0, keepdims=True)
        pg[...] = g_[T - 8:T, :]
        pv[...] = v_[T - 8:T, :]

    tok = pl.BlockSpec((T, tc), lambda j, t: (t, j))
    w3 = pl.BlockSpec((3, tc), lambda j, t: (0, j))
    b1 = pl.BlockSpec((1, tc), lambda j, t: (0, j))
    big = jax.ShapeDtypeStruct((S, F), BF16)
    return pl.pallas_call(
        body, name=name, grid=(F // tc, S // T),
        in_specs=[tok, tok, w3, w3, b1, b1, tok], out_specs=[tok, tok, w3, w3, b1, b1],
        out_shape=[big, big, jax.ShapeDtypeStruct((3, F), F32), jax.ShapeDtypeStruct((3, F), F32),
                   jax.ShapeDtypeStruct((1, F), F32), jax.ShapeDtypeStruct((1, F), F32)],
        scratch_shapes=[pltpu.VMEM((8, tc), F32), pltpu.VMEM((8, tc), F32)],
        compiler_params=_cparams(("parallel", "arbitrary")),
    )(ug, uv, wg, wv, bg, bv, da)


def _conv_bwd_u(dacc, w, *, name, T=512, tc=256):
    S, F = dacc.shape
    T = min(T, S)
    nT = S // T

    def body(d_ref, w_ref, du_ref, nxt):
        @pl.when(pl.program_id(1) == 0)
        def _():
            nxt[...] = jnp.zeros_like(nxt)

        rid = lax.broadcasted_iota(jnp.int32, (T, tc), 0)
        d = d_ref[...].astype(F32)
        nx = nxt[...]
        p1 = jnp.where(rid == T - 1, nx[0:1, :], pltpu.roll(d, T - 1, 0))
        p2 = jnp.where(rid == T - 1, nx[1:2, :], jnp.where(rid == T - 2, nx[0:1, :], pltpu.roll(d, T - 2, 0)))
        du_ref[...] = (w_ref[2:3, :] * d + w_ref[1:2, :] * p1 + w_ref[0:1, :] * p2).astype(du_ref.dtype)
        nxt[...] = d[0:8, :]

    tok = pl.BlockSpec((T, tc), lambda j, t: (nT - 1 - t, j))
    return pl.pallas_call(
        body, name=name, grid=(F // tc, nT),
        in_specs=[tok, pl.BlockSpec((3, tc), lambda j, t: (0, j))], out_specs=tok,
        out_shape=jax.ShapeDtypeStruct((S, F), BF16),
        scratch_shapes=[pltpu.VMEM((8, tc), F32)],
        compiler_params=_cparams(("parallel", "arbitrary")),
    )(dacc, w)


def _local_step(x, tgt, w, p):
    S = x.shape[0]
    mm = _matmul
    n1 = _rms_fwd(x, p["norm_mix"], name="rms1_fwd")
    proj = mm(n1, w["wm"], "nn", name="proj_main")
    ff = mm(n1, w["wff"], "nn", name="proj_ff")
    lb = _lb_fwd(p["hg_lb_logits"], name="lb_fwd")
    gnorm = p["hg_norm"].reshape(1, HG_DV)
    o_hg, oa, states = _hgrn_fwd_phased(proj, lb, gnorm, name="hgrn_fwd")
    bias = jnp.pad(p["fox_f_bias"].reshape(1, FOX_HEADS), ((0, 0), (0, 128 - FOX_HEADS)))
    c = _fox_gate_fwd(ff, bias, name="fox_gate_fwd")
    qa, ka, va = _fox_prep(proj, c, name="fox_prep")
    ob, qb = _fox_fwd2(qa, ka, va, name="fox_fwd")
    pa = mm(oa, w["wa"], "nn", name="branch_a")
    pb = mm(ob, w["wb"], "nn", name="branch_b")
    merged = _merge_fwd(proj, pa, pb, name="merge_fwd")
    h1 = mm(merged, w["wo"], "nn", addend=x, name="mix_out")
    n2 = _rms_fwd(h1, p["norm_ffn"], name="rms2_fwd")
    ug = mm(n2, w["wug"], "nn", name="up_gate")
    uv = mm(n2, w["wuv"], "nn", name="up_val")
    a = _convglu_fwd(ug, uv, w["cwg"], w["cwv"], p["cbg"], p["cbv"], name="convglu_fwd")
    h2 = mm(a, w["wd"], "nn", addend=h1, name="ffn_down")
    loss, dh2, d_norm_final = _loss_head(h2, p["norm_final"], tgt, name="loss_head")
    da = mm(dh2, w["wd"], "nt", out_dtype=BF16, name="d_act")
    d_wd = mm(a, dh2, "tn", out_dtype=BF16, name="dw_down")
    daccg, daccv, d_cwg, d_cwv, d_cbg, d_cbv = _convglu_bwd_acc(
        ug, uv, w["cwg"], w["cwv"], p["cbg"], p["cbv"], da, name="convglu_bwd")
    dug = _conv_bwd_u(daccg, w["cwg"], name="conv_bwd_gate")
    duv = _conv_bwd_u(daccv, w["cwv"], name="conv_bwd_val")
    dn2 = mm(dug, w["wug"], "nt", name="dn2_gate")
    dn2 = mm(duv, w["wuv"], "nt", addend=dn2, name="dn2_val")
    d_wug = mm(n2, dug, "tn", out_dtype=BF16, name="dw_up_gate")
    d_wuv = mm(n2, duv, "tn", out_dtype=BF16, name="dw_up_val")
    dh1, d_norm_ffn = _rms_bwd(h1, p["norm_ffn"], dn2, dh2, name="rms2_bwd")
    dmerged = mm(dh1, w["wo"], "nt", name="d_merged")
    d_wo = mm(merged, dh1, "tn", out_dtype=BF16, name="dw_out")
    dpa, dpb, dga, dgb = _merge_bwd(proj, pa, pb, dmerged, name="merge_bwd")
    doa = mm(dpa, w["wa"], "nt", name="d_oa")
    dob = mm(dpb, w["wb"], "nt", out_dtype=BF16, name="d_ob")
    d_wa = mm(oa, dpa, "tn", out_dtype=BF16, name="dw_branch_a")
    d_wb = mm(ob, dpb, "tn", out_dtype=BF16, name="dw_branch_b")
    dhq, dhf, dhi, dhg, dlb, dgn8 = _hgrn_bwd_phased(proj, lb, gnorm, o_hg, states, doa, name="hgrn_bwd")
    d_logits = _lb_bwd(p["hg_lb_logits"], dlb, name="lb_bwd")
    dob_hm = _fox_bwd_prep(ob, dob, name="fox_bwd_prep")
    dq, dcsp = _fox_bwd_dq2(qb, ka, va, dob_hm, name="fox_bwd_dq")
    dk, dv = _fox_bwd_dkv2(qb, ka, va, dob_hm, name="fox_bwd_dkv")
    nb = dcsp.shape[1]
    written = (jnp.arange(S) // (S // nb))[None, None, None, :] <= jnp.arange(nb)[None, :, None, None]
    dcs = jnp.sum(jnp.where(written, dcsp, 0.0), axis=1)
    dcs_tok = jnp.pad(dcs.reshape(FOX_HEADS, S).T, ((0, 0), (0, 128 - FOX_HEADS)))
    dff, dbias = _fox_gate_bwd(ff, bias, dcs_tok, name="fox_gate_bwd")
    dproj = jnp.concatenate([dhq, dhf, dhi, dhg, dq, dk, dv, dga, dgb], axis=1)
    dn1 = mm(dff, w["wff"], "nt", name="dn1_ff")
    dn1 = mm(dproj, w["wm"], "nt", addend=dn1, name="dn1_main")
    d_wm = mm(n1, dproj, "tn", out_dtype=BF16, name="dw_in_main")
    d_wff = mm(n1, dff, "tn", out_dtype=BF16, name="dw_in_ff")
    dx, d_norm_mix = _rms_bwd(x, p["norm_mix"], dn1, dh1, name="rms1_bwd")
    grads = dict(
        wm=d_wm, wff=d_wff, wa=d_wa, wb=d_wb, wo=d_wo, wug=d_wug, wuv=d_wuv, cwg=d_cwg, cwv=d_cwv, wd=d_wd,
        norm_mix=d_norm_mix.reshape(-1), fox_f_bias=dbias[0, :FOX_HEADS], hg_lb_logits=d_logits,
        hg_norm=jnp.sum(dgn8, axis=0).reshape(-1), norm_ffn=d_norm_ffn.reshape(-1), cbg=d_cbg, cbv=d_cbv,
        norm_final=d_norm_final.reshape(-1))
    return loss, dx, grads


MESH = pl.DeviceIdType.MESH
ANY = pl.BlockSpec(memory_space=pl.ANY)


def _all_gather(xs, *, name):
    def body(x_ref, out_ref, send_sems, recv_sems, local_sem):
        x, y, c = lax.axis_index("x"), lax.axis_index("y"), lax.axis_index("c")
        me, sibling = (x, y, c), (x, y, 1 - c)
        chips = [(1 - x, y), (x, 1 - y), (1 - x, 1 - y)]

        def rows(px, py, pc):
            return out_ref.at[4 * px + 2 * py + pc]

        def copy(k, block, to, src=None):
            return pltpu.make_async_remote_copy(
                src_ref=rows(*block) if src is None else src, dst_ref=rows(*block),
                send_sem=send_sems.at[k], recv_sem=recv_sems.at[k], device_id=to, device_id_type=MESH)

        mine = pltpu.make_async_copy(x_ref, rows(*me), local_sem)
        mine.start()
        first = [copy(0, me, sibling, src=x_ref)]
        first += [copy(1 + j, me, (*chip, c), src=x_ref) for j, chip in enumerate(chips)]
        for cp in first:
            cp.start()
        passed = [copy(4 + j, (*chip, c), sibling) for j, chip in enumerate(chips)]
        for j, chip in enumerate(chips):
            copy(1 + j, (*chip, c), me).wait_recv()
            passed[j].start()
        copy(0, sibling, me).wait_recv()
        for j, chip in enumerate(chips):
            copy(4 + j, (*chip, 1 - c), me).wait_recv()
        for cp in first + passed:
            cp.wait_send()
        mine.wait()

    return pl.pallas_call(
        body, name=name, in_specs=[ANY], out_specs=ANY,
        out_shape=jax.ShapeDtypeStruct((N_DEV,) + xs.shape, xs.dtype),
        scratch_shapes=[pltpu.SemaphoreType.DMA((7,)), pltpu.SemaphoreType.DMA((7,)), pltpu.SemaphoreType.DMA],
    )(xs)


def _exchange_blocks(g, *, name):
    def body(g_ref, out_ref, send_sems, recv_sems, local_sem):
        x, y, c = lax.axis_index("x"), lax.axis_index("y"), lax.axis_index("c")
        me = 4 * x + 2 * y + c
        mine = pltpu.make_async_copy(g_ref.at[me], out_ref.at[me], local_sem)
        mine.start()
        sends, recvs = [], []
        for k in range(1, N_DEV):
            px = 1 - x if k & 4 else x
            py = 1 - y if k & 2 else y
            pc = 1 - c if k & 1 else c
            p = 4 * px + 2 * py + pc
            sends.append(pltpu.make_async_remote_copy(
                src_ref=g_ref.at[p], dst_ref=out_ref.at[me], send_sem=send_sems.at[k - 1], recv_sem=recv_sems.at[k - 1],
                device_id=(px, py, pc), device_id_type=MESH))
            recvs.append(pltpu.make_async_remote_copy(
                src_ref=g_ref.at[p], dst_ref=out_ref.at[p], send_sem=send_sems.at[k - 1], recv_sem=recv_sems.at[k - 1],
                device_id=(px, py, pc), device_id_type=MESH))
        for cp in sends:
            cp.start()
        for cp in recvs:
            cp.wait_recv()
        for cp in sends:
            cp.wait_send()
        mine.wait()

    return pl.pallas_call(
        body, name=name, in_specs=[ANY], out_specs=ANY,
        out_shape=jax.ShapeDtypeStruct(g.shape, g.dtype),
        scratch_shapes=[pltpu.SemaphoreType.DMA((7,)), pltpu.SemaphoreType.DMA((7,)), pltpu.SemaphoreType.DMA],
    )(g)


def _adamw(parts, w, m, v, *, name, T=512):
    R, L = w.shape
    c1 = 1.0 / (1.0 - ADAM_B1 ** ADAM_STEP)
    c2 = 1.0 / (1.0 - ADAM_B2 ** ADAM_STEP)

    def body(p_ref, w_ref, m_ref, v_ref, g_ref, d_ref, nm_ref, nv_ref):
        g = p_ref[0]
        for s in range(1, N_DEV):
            g = g + p_ref[s]
        g_ref[...] = g
        nm = ADAM_B1 * m_ref[...] + (1.0 - ADAM_B1) * g
        nv = ADAM_B2 * v_ref[...] + (1.0 - ADAM_B2) * (g * g)
        nm_ref[...] = nm
        nv_ref[...] = nv
        d_ref[...] = -ADAM_LR * ((nm * c1) / (jnp.sqrt(nv * c2) + ADAM_EPS) + ADAM_WD * w_ref[...])

    blk = pl.BlockSpec((T, L), lambda i: (i, 0))
    out = jax.ShapeDtypeStruct((R, L), F32)
    return pl.pallas_call(
        body, name=name, grid=(R // T,),
        in_specs=[pl.BlockSpec((N_DEV, T, L), lambda i: (0, i, 0)), blk, blk, blk],
        out_specs=[blk, blk, blk, blk], out_shape=[out, out, out, out],
        compiler_params=_cparams(("parallel",)),
    )(parts, w, m, v)


D_IN = 9232
FF_LO, FF_HI = 7168, 7184
IN_SH, UP_SH, DOWN_SH = D_IN // N_DEV, 2 * D_FF // N_DEV, D_FF // N_DEV
SQ_SH = D_MODEL // N_DEV

BIG = [("w_in", (1, D_MODEL, IN_SH)), ("w_branch_a", (1, SQ_SH, D_MODEL)), ("w_branch_b", (1, SQ_SH, D_MODEL)),
       ("w_out", (1, SQ_SH, D_MODEL)), ("w_up", (1, D_MODEL, UP_SH)), ("conv_w", (1, 3, UP_SH)),
       ("w_down", (1, DOWN_SH, D_MODEL))]
SMALL = [("norm_mix", (1, D_MODEL)), ("fox_f_bias", (1, FOX_HEADS)), ("hg_lb_logits", (2, HG_HEADS * HG_DK)),
         ("hg_norm", (1, HG_DV)), ("norm_ffn", (1, D_MODEL)), ("conv_b", (1, 2 * D_FF)), ("norm_final", (D_MODEL,))]
NAMES = ["norm_mix", "w_in", "fox_f_bias", "hg_lb_logits", "hg_norm", "w_branch_a", "w_branch_b", "w_out",
         "norm_ffn", "w_up", "conv_w", "conv_b", "w_down", "norm_final"]


def _size(shape):
    n = 1
    for s in shape:
        n *= s
    return n


PACK_ROWS = 20992
GATHER_ROWS = 20800
assert sum(_size(s) for _, s in BIG + SMALL) <= PACK_ROWS * 128


def _pack_rows(flat_parts, rows):
    flat = jnp.concatenate(flat_parts, axis=-1)
    pad = rows * 128 - flat.shape[-1]
    flat = jnp.pad(flat, [(0, 0)] * (flat.ndim - 1) + [(0, pad)])
    return flat.reshape(flat.shape[:-1] + (rows, 128))


def _pack_shard(vals):
    return _pack_rows([vals[n].reshape(1, -1).astype(F32) for n, _ in BIG + SMALL], PACK_ROWS)[0]


def _unpack_shard(buf):
    flat = buf.reshape(-1)
    out, off = {}, 0
    for n, shape in BIG + SMALL:
        out[n] = flat[off:off + _size(shape)].reshape(shape)
        off += _size(shape)
    return out


def _cols_by_device(a, width):
    rows = a.shape[0]
    return a.reshape(rows, N_DEV, width).transpose(1, 0, 2).reshape(N_DEV, rows * width)


def _cols_from_devices(a, rows, width):
    return a.reshape(N_DEV, rows, width).transpose(1, 0, 2).reshape(rows, N_DEV * width)


def _pack_grads(g):
    w_in = jnp.concatenate([g["wm"][:, :FF_LO], g["wff"][:, :FOX_HEADS], g["wm"][:, FF_LO:]], axis=1)
    w_up = jnp.concatenate([g["wug"], g["wuv"]], axis=1)
    conv_w = jnp.concatenate([g["cwg"], g["cwv"]], axis=1)
    conv_b = jnp.concatenate([g["cbg"], g["cbv"]], axis=1)
    big = [_cols_by_device(w_in, IN_SH), g["wa"].reshape(N_DEV, -1), g["wb"].reshape(N_DEV, -1),
           g["wo"].reshape(N_DEV, -1), _cols_by_device(w_up, UP_SH), _cols_by_device(conv_w, UP_SH),
           g["wd"].reshape(N_DEV, -1)]
    small = [g["norm_mix"], g["fox_f_bias"], g["hg_lb_logits"], g["hg_norm"], g["norm_ffn"], conv_b, g["norm_final"]]
    small = [jnp.broadcast_to(s.reshape(1, -1), (N_DEV, s.size)) for s in small]
    return _pack_rows(big + small, PACK_ROWS)


def _gather_weights(w_in, w_a, w_b, w_o, w_up, conv_w, w_down):
    taps = lax.bitcast_convert_type(conv_w.reshape(3, UP_SH), BF16).reshape(1, -1)
    mats = [w_in, w_a, w_b, w_o, w_up, w_down]
    packed = _pack_rows([t.reshape(1, -1).astype(BF16) for t in mats] + [taps], GATHER_ROWS)[0]
    full = _all_gather(packed, name="gather_weights").reshape(N_DEV, -1)
    off = 0

    def take(n):
        nonlocal off
        piece = full[:, off:off + n]
        off += n
        return piece

    win = _cols_from_devices(take(D_MODEL * IN_SH), D_MODEL, IN_SH)
    wa = take(SQ_SH * D_MODEL).reshape(D_MODEL, D_MODEL)
    wb = take(SQ_SH * D_MODEL).reshape(D_MODEL, D_MODEL)
    wo = take(SQ_SH * D_MODEL).reshape(D_MODEL, D_MODEL)
    wup = _cols_from_devices(take(D_MODEL * UP_SH), D_MODEL, UP_SH)
    wd = take(DOWN_SH * D_MODEL).reshape(D_FF, D_MODEL)
    cw = lax.bitcast_convert_type(take(3 * UP_SH * 2).reshape(N_DEV, 3, UP_SH, 2), F32)
    cw = cw.transpose(1, 0, 2).reshape(3, 2 * D_FF)
    return dict(
        wm=jnp.concatenate([win[:, :FF_LO], win[:, FF_HI:]], axis=1),
        wff=jnp.pad(win[:, FF_LO:FF_HI], ((0, 0), (0, 128 - FOX_HEADS))),
        wa=wa, wb=wb, wo=wo, wug=wup[:, :D_FF], wuv=wup[:, D_FF:], cwg=cw[:, :D_FF], cwv=cw[:, D_FF:], wd=wd)


def _peer(k, x, y, c):
    return (1 - x if k & 4 else x, 1 - y if k & 2 else y, 1 - c if k & 1 else c)


def _gather_multi(shards, *, name):
    n = len(shards)

    def body(*refs):
        x_refs, out_refs = refs[:n], refs[n:2 * n]
        send_sems, recv_sems, local_sems = refs[2 * n:]
        x, y, c = lax.axis_index("x"), lax.axis_index("y"), lax.axis_index("c")
        me, sibling = (x, y, c), (x, y, 1 - c)
        chips = [(1 - x, y), (x, 1 - y), (1 - x, 1 - y)]

        def copy(t, k, block, to, src=None):
            slot = out_refs[t].at[4 * block[0] + 2 * block[1] + block[2]]
            return pltpu.make_async_remote_copy(
                src_ref=slot if src is None else src, dst_ref=slot,
                send_sem=send_sems.at[t, k], recv_sem=recv_sems.at[t, k], device_id=to, device_id_type=MESH)

        mine = [pltpu.make_async_copy(x_refs[t], out_refs[t].at[4 * x + 2 * y + c], local_sems.at[t]) for t in range(n)]
        for cp in mine:
            cp.start()
        first = []
        for t in range(n):
            first.append(copy(t, 0, me, sibling, src=x_refs[t]))
            first += [copy(t, 1 + j, me, (*chip, c), src=x_refs[t]) for j, chip in enumerate(chips)]
        for cp in first:
            cp.start()
        passed = []
        for j, chip in enumerate(chips):
            for t in range(n):
                copy(t, 1 + j, (*chip, c), me).wait_recv()
                passed.append(copy(t, 4 + j, (*chip, c), sibling))
                passed[-1].start()
        for t in range(n):
            copy(t, 0, sibling, me).wait_recv()
            for j, chip in enumerate(chips):
                copy(t, 4 + j, (*chip, 1 - c), me).wait_recv()
        for cp in first + passed:
            cp.wait_send()
        for cp in mine:
            cp.wait()

    return pl.pallas_call(
        body, name=name, in_specs=[ANY] * n, out_specs=[ANY] * n,
        out_shape=[jax.ShapeDtypeStruct((N_DEV,) + s.shape, s.dtype) for s in shards],
        scratch_shapes=[pltpu.SemaphoreType.DMA((n, 7)), pltpu.SemaphoreType.DMA((n, 7)), pltpu.SemaphoreType.DMA((n,))],
    )(*shards)


def _exchange_multi(blocks, *, name):
    n = len(blocks)

    def body(*refs):
        g_refs, out_refs = refs[:n], refs[n:2 * n]
        send_sems, recv_sems, local_sems = refs[2 * n:]
        x, y, c = lax.axis_index("x"), lax.axis_index("y"), lax.axis_index("c")
        me = 4 * x + 2 * y + c
        mine = [pltpu.make_async_copy(g_refs[t].at[me], out_refs[t].at[me], local_sems.at[t]) for t in range(n)]
        for cp in mine:
            cp.start()
        sends, recvs = [], []
        for k in range(1, N_DEV):
            px, py, pc = _peer(k, x, y, c)
            p = 4 * px + 2 * py + pc
            for t in range(n):
                sends.append(pltpu.make_async_remote_copy(
                    src_ref=g_refs[t].at[p], dst_ref=out_refs[t].at[me], send_sem=send_sems.at[t, k - 1],
                    recv_sem=recv_sems.at[t, k - 1], device_id=(px, py, pc), device_id_type=MESH))
                recvs.append(pltpu.make_async_remote_copy(
                    src_ref=g_refs[t].at[p], dst_ref=out_refs[t].at[p], send_sem=send_sems.at[t, k - 1],
                    recv_sem=recv_sems.at[t, k - 1], device_id=(px, py, pc), device_id_type=MESH))
        for cp in sends:
            cp.start()
        for cp in recvs:
            cp.wait_recv()
        for cp in sends:
            cp.wait_send()
        for cp in mine:
            cp.wait()

    return pl.pallas_call(
        body, name=name, in_specs=[ANY] * n, out_specs=[ANY] * n,
        out_shape=[jax.ShapeDtypeStruct(b.shape, b.dtype) for b in blocks],
        scratch_shapes=[pltpu.SemaphoreType.DMA((n, 7)), pltpu.SemaphoreType.DMA((n, 7)), pltpu.SemaphoreType.DMA((n,))],
    )(*blocks)


def _adamw2(parts, w, m, v, *, name, T):
    R, C = w.shape
    c1 = 1.0 / (1.0 - ADAM_B1 ** ADAM_STEP)
    c2 = 1.0 / (1.0 - ADAM_B2 ** ADAM_STEP)

    def body(p_ref, w_ref, m_ref, v_ref, g_ref, d_ref, nm_ref, nv_ref):
        g = p_ref[0].astype(F32)
        for s in range(1, N_DEV):
            g = g + p_ref[s].astype(F32)
        g_ref[...] = g
        nm = ADAM_B1 * m_ref[...] + (1.0 - ADAM_B1) * g
        nv = ADAM_B2 * v_ref[...] + (1.0 - ADAM_B2) * (g * g)
        nm_ref[...] = nm
        nv_ref[...] = nv
        d_ref[...] = -ADAM_LR * ((nm * c1) / (jnp.sqrt(nv * c2) + ADAM_EPS) + ADAM_WD * w_ref[...])

    blk = pl.BlockSpec((T, C), lambda i: (i, 0))
    out = jax.ShapeDtypeStruct((R, C), F32)
    return pl.pallas_call(
        body, name=name, grid=(R // T,),
        in_specs=[pl.BlockSpec((N_DEV, T, C), lambda i: (0, i, 0)), blk, blk, blk],
        out_specs=[blk, blk, blk, blk], out_shape=[out, out, out, out],
        compiler_params=_cparams(("parallel",)),
    )(parts, w, m, v)


SMALL_ROWS = 88
SHARDED = [("w_in", (D_MODEL, 1154), 256), ("w_branch_a", (128, D_MODEL), 128), ("w_branch_b", (128, D_MODEL), 128),
           ("w_out", (128, D_MODEL), 128), ("w_up", (D_MODEL, 704), 256), ("conv_w", (3, 704), 3),
           ("w_down", (352, D_MODEL), 352)]


def _col_blocks(a, width):
    return jnp.stack([a[:, d * width:(d + 1) * width] for d in range(N_DEV)])


def _pack_small(vals):
    flat = jnp.concatenate([vals[n].reshape(-1).astype(F32) for n, _ in SMALL])
    return jnp.pad(flat, (0, SMALL_ROWS * 128 - flat.shape[0])).reshape(SMALL_ROWS, 128)


def _unpack_small(buf):
    flat, out, off = buf.reshape(-1), {}, 0
    for n, shape in SMALL:
        out[n] = flat[off:off + _size(shape)].reshape(shape)
        off += _size(shape)
    return out


def _gather_weights2(w_in, w_a, w_b, w_o, w_up, conv_w, w_down):
    shards = [w_in[0].astype(BF16), w_a[0].astype(BF16), w_b[0].astype(BF16), w_o[0].astype(BF16),
              w_up[0].astype(BF16), conv_w[0], w_down[0].astype(BF16)]
    g_in, g_a, g_b, g_o, g_up, g_cw, g_d = _gather_multi(shards, name="gather_weights")
    win = jnp.concatenate([g_in[d] for d in range(N_DEV)], axis=1)
    wup = jnp.concatenate([g_up[d] for d in range(N_DEV)], axis=1)
    cw = jnp.concatenate([g_cw[d] for d in range(N_DEV)], axis=1)
    return dict(
        wm=jnp.concatenate([win[:, :FF_LO], win[:, FF_HI:]], axis=1),
        wff=jnp.pad(win[:, FF_LO:FF_HI], ((0, 0), (0, 128 - FOX_HEADS))),
        wa=g_a.reshape(D_MODEL, D_MODEL), wb=g_b.reshape(D_MODEL, D_MODEL), wo=g_o.reshape(D_MODEL, D_MODEL),
        wug=wup[:, :D_FF], wuv=wup[:, D_FF:], cwg=cw[:, :D_FF], cwv=cw[:, D_FF:], wd=g_d.reshape(D_FF, D_MODEL))


def _grad_blocks(g):
    w_in = jnp.concatenate([g["wm"][:, :FF_LO], g["wff"][:, :FOX_HEADS], g["wm"][:, FF_LO:]], axis=1)
    conv_w = jnp.concatenate([g["cwg"], g["cwv"]], axis=1).astype(F32)
    conv_b = jnp.concatenate([g["cbg"], g["cbv"]], axis=1)
    small = _pack_small(dict(norm_mix=g["norm_mix"], fox_f_bias=g["fox_f_bias"], hg_lb_logits=g["hg_lb_logits"],
                             hg_norm=g["hg_norm"], norm_ffn=g["norm_ffn"], conv_b=conv_b, norm_final=g["norm_final"]))
    up = jnp.stack([g["wug"][:, d * 704:(d + 1) * 704] for d in range(4)]
                   + [g["wuv"][:, d * 704:(d + 1) * 704] for d in range(4)])
    return [_col_blocks(w_in, 1154), g["wa"].reshape(N_DEV, 128, D_MODEL), g["wb"].reshape(N_DEV, 128, D_MODEL),
            g["wo"].reshape(N_DEV, 128, D_MODEL), up, _col_blocks(conv_w, 704), g["wd"].reshape(N_DEV, 352, D_MODEL),
            jnp.broadcast_to(small[None], (N_DEV, SMALL_ROWS, 128))]


def kernel(x, norm_mix, w_in,fox_f_bias, hg_lb_logits, hg_norm, w_branch_a, w_branch_b, w_out, norm_ffn, w_up, conv_w, conv_b, w_down, norm_final, loss_target, m_norm_mix, m_w_in, m_fox_f_bias, m_hg_lb_logits, m_hg_norm, m_w_branch_a, m_w_branch_b, m_w_out, m_norm_ffn, m_w_up, m_conv_w, m_conv_b, m_w_down, m_norm_final, v_norm_mix, v_w_in, v_fox_f_bias, v_hg_lb_logits, v_hg_norm, v_w_branch_a, v_w_branch_b, v_w_out, v_norm_ffn, v_w_up, v_conv_w, v_conv_b, v_w_down, v_norm_final):
    wv = dict(norm_mix=norm_mix, w_in=w_in, fox_f_bias=fox_f_bias, hg_lb_logits=hg_lb_logits, hg_norm=hg_norm,
              w_branch_a=w_branch_a, w_branch_b=w_branch_b, w_out=w_out, norm_ffn=norm_ffn, w_up=w_up, conv_w=conv_w,
              conv_b=conv_b, w_down=w_down, norm_final=norm_final)
    mv = dict(norm_mix=m_norm_mix, w_in=m_w_in, fox_f_bias=m_fox_f_bias, hg_lb_logits=m_hg_lb_logits, hg_norm=m_hg_norm,
              w_branch_a=m_w_branch_a, w_branch_b=m_w_branch_b, w_out=m_w_out, norm_ffn=m_norm_ffn, w_up=m_w_up,
              conv_w=m_conv_w, conv_b=m_conv_b, w_down=m_w_down, norm_final=m_norm_final)
    vv = dict(norm_mix=v_norm_mix, w_in=v_w_in, fox_f_bias=v_fox_f_bias, hg_lb_logits=v_hg_lb_logits, hg_norm=v_hg_norm,
              w_branch_a=v_w_branch_a, w_branch_b=v_w_branch_b, w_out=v_w_out, norm_ffn=v_norm_ffn, w_up=v_w_up,
              conv_w=v_conv_w, conv_b=v_conv_b, w_down=v_w_down, norm_final=v_norm_final)

    w = _gather_weights2(w_in, w_branch_a, w_branch_b, w_out, w_up, conv_w, w_down)
    p = dict(norm_mix=norm_mix[0], fox_f_bias=fox_f_bias[0], hg_lb_logits=hg_lb_logits, hg_norm=hg_norm[0],
             norm_ffn=norm_ffn[0], cbg=conv_b[:, :D_FF], cbv=conv_b[:, D_FF:], norm_final=norm_final)
    loss, dx, grads = _local_step(x[0], loss_target[0], w, p)
    loss = lax.psum(loss[0, 0], ("x", "y", "c"))

    parts = _exchange_multi(_grad_blocks(grads), name="exchange_grads")
    res = {}
    for (n, shape, tile), part in zip(SHARDED, parts):
        outs = _adamw2(part, wv[n].reshape(shape), mv[n].reshape(shape), vv[n].reshape(shape), name="adamw_" + n, T=tile)
        res[n] = [o.reshape(wv[n].shape) for o in outs]
    outs = _adamw2(parts[-1], _pack_small(wv), _pack_small(mv), _pack_small(vv), name="adamw_small", T=SMALL_ROWS)
    small = [_unpack_small(o) for o in outs]
    for n, _ in SMALL:
        res[n] = [s[n] for s in small]
    return (loss, dx[None], *[res[n][0] for n in NAMES], *[res[n][1] for n in NAMES],
            *[res[n][2] for n in NAMES], *[res[n][3] for n in NAMES])


def _lb_fwd(logits, *, name):
    def body(l_ref, lb_ref):
        lb_ref[...] = _sigmoid(l_ref[0:1, :] - l_ref[1:2, :])

    return pl.pallas_call(body, name=name, out_shape=jax.ShapeDtypeStruct((1, logits.shape[1]), F32))(logits)


def _lb_bwd(logits, dlb, *, name):
    def body(l_ref, d_ref, o_ref):
        lbv = _sigmoid(l_ref[0:1, :] - l_ref[1:2, :])
        t = d_ref[...] * lbv * (1.0 - lbv)
        o_ref[0:1, :] = t
        o_ref[1:2, :] = -t

    return pl.pallas_call(body, name=name, out_shape=jax.ShapeDtypeStruct(logits.shape, F32))(logits, dlb)
```

```python
import functools

import numpy as np
import jax
import jax.numpy as jnp
from jax import lax
from jax.experimental import pallas as pl
from jax.experimental.pallas import tpu as pltpu

F32 = jnp.float32
BF16 = jnp.bfloat16

D_MODEL = 1024
HG_HEADS = 8
HG_DK = 128
HG_DV = 128
HG_CHUNK = 64
FOX_HEADS = 16
FOX_DH = 64
D_FF = 2816
EPS = 1e-6
N_DEV = 8

ADAM_LR = 0.001
ADAM_B1 = 0.9
ADAM_B2 = 0.999
ADAM_EPS = 1e-08
ADAM_WD = 0.01
ADAM_STEP = 10

VMEM_LIMIT = 56 * 1024 * 1024


def _cparams(sem):
    return pltpu.CompilerParams(dimension_semantics=sem, vmem_limit_bytes=VMEM_LIMIT)


_DIMS = {
    "nn": (((1,), (0,)), ((), ())),
    "nt": (((1,), (1,)), ((), ())),
    "tn": (((0,), (0,)), ((), ())),
}


def _pick(n, prefs):
    for p in prefs:
        if n % p == 0:
            return p
    return n


MESH = pl.DeviceIdType.MESH
ANY = pl.BlockSpec(memory_space=pl.ANY)


class _GatherComm:
    def __init__(self, shards):
        self.inputs = list(shards)
        n = self.n = len(shards)
        self.out_shapes = [jax.ShapeDtypeStruct((N_DEV,) + s.shape, s.dtype) for s in shards]
        self.scratch = [pltpu.SemaphoreType.DMA((n, 7)), pltpu.SemaphoreType.DMA((n, 7)), pltpu.SemaphoreType.DMA((n,))]

    def _parts(self, x_refs, out_refs, sems):
        send_sems, recv_sems, local_sems = sems
        x, y, c = lax.axis_index("x"), lax.axis_index("y"), lax.axis_index("c")
        me, sibling = (x, y, c), (x, y, 1 - c)
        chips = [(1 - x, y), (x, 1 - y), (1 - x, 1 - y)]

        def copy(t, k, block, to, src=None):
            slot = out_refs[t].at[4 * block[0] + 2 * block[1] + block[2]]
            return pltpu.make_async_remote_copy(
                src_ref=slot if src is None else src, dst_ref=slot,
                send_sem=send_sems.at[t, k], recv_sem=recv_sems.at[t, k], device_id=to, device_id_type=MESH)

        mine = [pltpu.make_async_copy(x_refs[t], out_refs[t].at[4 * x + 2 * y + c], local_sems.at[t])
                for t in range(self.n)]
        first = []
        for t in range(self.n):
            first.append(copy(t, 0, me, sibling, src=x_refs[t]))
            first += [copy(t, 1 + j, me, (*chip, c), src=x_refs[t]) for j, chip in enumerate(chips)]
        return c, me, sibling, chips, copy, mine, first

    def start(self, x_refs, out_refs, sems):
        _, _, _, _, _, mine, first = self._parts(x_refs, out_refs, sems)
        for cp in mine + first:
            cp.start()

    def finish(self, x_refs, out_refs, sems):
        c, me, sibling, chips, copy, mine, first = self._parts(x_refs, out_refs, sems)
        passed = []
        for j, chip in enumerate(chips):
            for t in range(self.n):
                copy(t, 1 + j, (*chip, c), me).wait_recv()
                passed.append(copy(t, 4 + j, (*chip, c), sibling))
                passed[-1].start()
        for t in range(self.n):
            copy(t, 0, sibling, me).wait_recv()
            for j, chip in enumerate(chips):
                copy(t, 4 + j, (*chip, 1 - c), me).wait_recv()
        for cp in first + passed:
            cp.wait_send()
        for cp in mine:
            cp.wait()


class _ExchangeComm:
    def __init__(self, blocks):
        self.inputs = list(blocks)
        n = self.n = len(blocks)
        self.out_shapes = [jax.ShapeDtypeStruct(b.shape, b.dtype) for b in blocks]
        self.scratch = [pltpu.SemaphoreType.DMA((n, 7)), pltpu.SemaphoreType.DMA((n, 7)), pltpu.SemaphoreType.DMA((n,))]

    def _parts(self, g_refs, out_refs, sems):
        send_sems, recv_sems, local_sems = sems
        x, y, c = lax.axis_index("x"), lax.axis_index("y"), lax.axis_index("c")
        me = 4 * x + 2 * y + c
        mine = [pltpu.make_async_copy(g_refs[t].at[me], out_refs[t].at[me], local_sems.at[t]) for t in range(self.n)]
        sends, recvs = [], []
        for k in range(1, N_DEV):
            px = 1 - x if k & 4 else x
            py = 1 - y if k & 2 else y
            pc = 1 - c if k & 1 else c
            p = 4 * px + 2 * py + pc
            for t in range(self.n):
                sends.append(pltpu.make_async_remote_copy(
                    src_ref=g_refs[t].at[p], dst_ref=out_refs[t].at[me], send_sem=send_sems.at[t, k - 1],
                    recv_sem=recv_sems.at[t, k - 1], device_id=(px, py, pc), device_id_type=MESH))
                recvs.append(pltpu.make_async_remote_copy(
                    src_ref=g_refs[t].at[p], dst_ref=out_refs[t].at[p], send_sem=send_sems.at[t, k - 1],
                    recv_sem=recv_sems.at[t, k - 1], device_id=(px, py, pc), device_id_type=MESH))
        return mine, sends, recvs

    def start(self, g_refs, out_refs, sems):
        mine, sends, _ = self._parts(g_refs, out_refs, sems)
        for cp in mine + sends:
            cp.start()

    def finish(self, g_refs, out_refs, sems):
        mine, sends, recvs = self._parts(g_refs, out_refs, sems)
        for cp in recvs:
            cp.wait_recv()
        for cp in sends:
            cp.wait_send()
        for cp in mine:
            cp.wait()


def _comm_call(comm, *, name):
    n = comm.n

    def body(*refs):
        comm.start(refs[:n], refs[n:2 * n], refs[2 * n:])
        comm.finish(refs[:n], refs[n:2 * n], refs[2 * n:])

    return pl.pallas_call(body, name=name, in_specs=[ANY] * n, out_specs=[ANY] * n, out_shape=comm.out_shapes,
                          scratch_shapes=comm.scratch)(*comm.inputs)


MATMUL_VMEM_BUDGET = 36 * 1024 * 1024
MAX_TILE = 1536


def _tile_options(n):
    return [d for d in range(128, min(n, MAX_TILE) + 1, 128) if n % d == 0] or [n]


def _pick_tiles(M, N, tk, nk, sa, sb, so, has_addend, tm, tn):
    best = None
    for cm in ([tm] if tm else _tile_options(M)):
        for cn in ([tn] if tn else _tile_options(N)):
            need = 2 * (cm * tk * sa + tk * cn * sb + cm * cn * so + (cm * cn * 4 if has_addend else 0))
            need += cm * cn * 4 if nk > 1 else 0
            if need <= MATMUL_VMEM_BUDGET and (best is None or cm * cn > best[0] * best[1]
                                               or (cm * cn == best[0] * best[1] and cn > best[1])):
                best = (cm, cn)
    assert best is not None, (M, N, tk)
    return best


def _matmul(a, b, form, *, out_dtype=F32, addend=None, tm=None, tn=None, tk=None, comm=None, name):
    if form == "nn":
        (M, K), (K2, N) = a.shape, b.shape
    elif form == "nt":
        (M, K), (N, K2) = a.shape, b.shape
    else:
        (K, M), (K2, N) = a.shape, b.shape
    assert K == K2, (a.shape, b.shape, form)
    tk = tk or (K if K <= 2816 else _pick(K, (1024, 512, 256, 128)))
    nk = K // tk
    if tm is None or tn is None:
        tm, tn = _pick_tiles(M, N, tk, nk, a.dtype.itemsize, b.dtype.itemsize, jnp.dtype(out_dtype).itemsize,
                             addend is not None, tm, tn)
    assert M % tm == 0 and N % tn == 0 and K % tk == 0, (M, N, K, tm, tn, tk)
    dims = _DIMS[form]

    nc = comm.n if comm is not None else 0
    grid = (M // tm, N // tn, nk)

    def body(*refs):
        a_ref, b_ref = refs[:2]
        pos = 2
        add_ref = refs[pos] if addend is not None else None
        pos += addend is not None
        c_in, o_ref, c_out = refs[pos:pos + nc], refs[pos + nc], refs[pos + nc + 1:pos + 2 * nc + 1]
        pos += 2 * nc + 1
        acc_ref = refs[pos] if nk > 1 else None
        c_sems = refs[pos + (nk > 1):]
        if comm is not None:
            ids = [pl.program_id(d) for d in range(3)]

            @pl.when((ids[0] == 0) & (ids[1] == 0) & (ids[2] == 0))
            def _():
                comm.start(c_in, c_out, c_sems)

        def finish(r):
            if add_ref is not None:
                r = r + add_ref[...].astype(F32)
            o_ref[...] = r.astype(o_ref.dtype)

        part = lax.dot_general(a_ref[...].astype(BF16), b_ref[...].astype(BF16), dims, preferred_element_type=F32)
        if nk == 1:
            finish(part)
        else:
            k = pl.program_id(2)

            @pl.when(k == 0)
            def _():
                acc_ref[...] = part

            @pl.when(k > 0)
            def _():
                acc_ref[...] += part

            @pl.when(k == nk - 1)
            def _():
                finish(acc_ref[...])

        if comm is not None:
            @pl.when((ids[0] == grid[0] - 1) & (ids[1] == grid[1] - 1) & (ids[2] == grid[2] - 1))
            def _():
                comm.finish(c_in, c_out, c_sems)

    if form == "nn":
        a_spec = pl.BlockSpec((tm, tk), lambda i, j, k: (i, k))
        b_spec = pl.BlockSpec((tk, tn), lambda i, j, k: (k, j))
    elif form == "nt":
        a_spec = pl.BlockSpec((tm, tk), lambda i, j, k: (i, k))
        b_spec = pl.BlockSpec((tn, tk), lambda i, j, k: (j, k))
    else:
        a_spec = pl.BlockSpec((tk, tm), lambda i, j, k: (k, i))
        b_spec = pl.BlockSpec((tk, tn), lambda i, j, k: (k, j))
    o_spec = pl.BlockSpec((tm, tn), lambda i, j, k: (i, j))
    in_specs = [a_spec, b_spec] + ([o_spec] if addend is not None else [])
    args = (a, b) + ((addend,) if addend is not None else ())
    out_shape = jax.ShapeDtypeStruct((M, N), out_dtype)
    scratch = [pltpu.VMEM((tm, tn), F32)] if nk > 1 else []
    if comm is None:
        return pl.pallas_call(
            body, name=name, grid=grid, in_specs=in_specs, out_specs=o_spec, out_shape=out_shape,
            scratch_shapes=scratch, compiler_params=_cparams(("parallel", "parallel", "arbitrary")),
        )(*args)
    outs = pl.pallas_call(
        body, name=name, grid=grid, in_specs=in_specs + [ANY] * nc, out_specs=[o_spec] + [ANY] * nc,
        out_shape=[out_shape] + comm.out_shapes, scratch_shapes=scratch + comm.scratch,
        compiler_params=_cparams(("arbitrary", "arbitrary", "arbitrary")),
    )(*args, *comm.inputs)
    return outs[0], outs[1:]


def _rms_fwd(x, g, *, name, tm=512):
    M, D = x.shape
    tm = min(tm, M)

    def body(x_ref, g_ref, n_ref):
        xf = x_ref[...]
        r = lax.rsqrt(jnp.mean(xf * xf, axis=-1, keepdims=True) + EPS)
        n_ref[...] = (xf * r * g_ref[...]).astype(n_ref.dtype)

    return pl.pallas_call(
        body, name=name, grid=(M // tm,),
        in_specs=[pl.BlockSpec((tm, D), lambda i: (i, 0)), pl.BlockSpec((1, D), lambda i: (0, 0))],
        out_specs=pl.BlockSpec((tm, D), lambda i: (i, 0)),
        out_shape=jax.ShapeDtypeStruct((M, D), BF16),
        compiler_params=_cparams(("parallel",)),
    )(x, g.reshape(1, D))


def _rms_bwd(x, g, dn, dres, *, name, tm=512):
    M, D = x.shape
    tm = min(tm, M)

    def body(x_ref, g_ref, dn_ref, dres_ref, dx_ref, dg_ref):
        @pl.when(pl.program_id(0) == 0)
        def _():
            dg_ref[...] = jnp.zeros_like(dg_ref)

        xf = x_ref[...]
        r = lax.rsqrt(jnp.mean(xf * xf, axis=-1, keepdims=True) + EPS)
        xh = xf * r
        dn_ = dn_ref[...].astype(F32)
        dg_ref[...] += jnp.sum(dn_ * xh, axis=0, keepdims=True)
        dxh = dn_ * g_ref[...]
        dx = r * (dxh - xh * jnp.mean(dxh * xh, axis=-1, keepdims=True))
        dx_ref[...] = dres_ref[...] + dx

    row = pl.BlockSpec((tm, D), lambda i: (i, 0))
    vec = pl.BlockSpec((1, D), lambda i: (0, 0))
    return pl.pallas_call(
        body, name=name, grid=(M // tm,),
        in_specs=[row, vec, row, row], out_specs=[row, vec],
        out_shape=[jax.ShapeDtypeStruct((M, D), F32), jax.ShapeDtypeStruct((1, D), F32)],
        compiler_params=_cparams(("arbitrary",)),
    )(x, g.reshape(1, D), dn, dres)


def _loss_head(h, g, tgt, *, name, tm=512):
    M, D = h.shape
    tm = min(tm, M)

    def body(h_ref, g_ref, t_ref, loss_ref, dh_ref, dg_ref):
        @pl.when(pl.program_id(0) == 0)
        def _():
            dg_ref[...] = jnp.zeros_like(dg_ref)
            loss_ref[...] = jnp.zeros_like(loss_ref)

        xf = h_ref[...]
        r = lax.rsqrt(jnp.mean(xf * xf, axis=-1, keepdims=True) + EPS)
        xh = xf * r
        err = xh * g_ref[...] - t_ref[...]
        part = jnp.sum(jnp.mean(err * err, axis=-1, keepdims=True), axis=0, keepdims=True)
        loss_ref[...] += 0.5 * part
        dy = err * (1.0 / D)
        dg_ref[...] += jnp.sum(dy * xh, axis=0, keepdims=True)
        dxh = dy * g_ref[...]
        dh_ref[...] = r * (dxh - xh * jnp.mean(dxh * xh, axis=-1, keepdims=True))

    row = pl.BlockSpec((tm, D), lambda i: (i, 0))
    vec = pl.BlockSpec((1, D), lambda i: (0, 0))
    one = pl.BlockSpec((1, 1), lambda i: (0, 0))
    return pl.pallas_call(
        body, name=name, grid=(M // tm,),
        in_specs=[row, vec, row], out_specs=[one, row, vec],
        out_shape=[jax.ShapeDtypeStruct((1, 1), F32), jax.ShapeDtypeStruct((M, D), F32),
                   jax.ShapeDtypeStruct((1, D), F32)],
        compiler_params=_cparams(("arbitrary",)),
    )(h, g.reshape(1, D), tgt)


HG_MID = HG_CHUNK // 2 - 1
EXP_CAP = 80.0


def _sigmoid(x):
    return 1.0 / (1.0 + jnp.exp(-x))


def _dot(a, b, dims, precision=None):
    return lax.dot_general(a, b, dims, preferred_element_type=F32, precision=precision)


def _bdot(a, b, form):
    return _dot(a.astype(BF16), b.astype(BF16), _DIMS[form])


HG_PREC = "highest"


def _hdot(a, b, form):
    if HG_PREC == "x1":
        return _bdot(a, b, form)
    if HG_PREC == "x3":
        ah, bh = a.astype(BF16), b.astype(BF16)
        al, bl = (a - ah.astype(F32)).astype(BF16), (b - bh.astype(F32)).astype(BF16)
        d = _DIMS[form]
        return _dot(ah, bh, d) + (_dot(ah, bl, d) + _dot(al, bh, d))
    return _dot(a, b, _DIMS[form], precision=lax.Precision.HIGHEST)


def _hgrn_chunk_common(hq, hf, lbv, tril, rid):
    sq = _sigmoid(hq)
    q = hq * sq
    sg = _sigmoid(hf)
    f = lbv + (1.0 - lbv) * sg
    k = (1.0 - lbv) * (1.0 - sg)
    g = jnp.log(f)
    b = _dot(tril, g, _DIMS["nn"], precision=lax.Precision.HIGHEST)
    bref = jnp.sum(jnp.where(rid == HG_MID, b, 0.0), axis=0, keepdims=True)
    bend = jnp.sum(jnp.where(rid == HG_CHUNK - 1, b, 0.0), axis=0, keepdims=True)
    eb = jnp.exp(b)
    e1 = jnp.exp(jnp.minimum(b - bref, EXP_CAP))
    e2 = jnp.exp(jnp.minimum(bref - b, EXP_CAP))
    e3 = jnp.exp(bend - b)
    return sq, q, sg, f, k, bend, eb, e1, e2, e3


def _split2(x):
    hi = x.astype(BF16)
    return hi, (x - hi.astype(F32)).astype(BF16)


def _dot3(a, b, form):
    d = _DIMS[form]
    return _dot(a[0], b[0], d) + (_dot(a[0], b[1], d) + _dot(a[1], b[0], d))


def _hgrn_fwd_phased(proj, lb, gnorm, *, name, T=512):
    S = proj.shape[0]
    T = min(T, S)
    nch = T // HG_CHUNK
    C = HG_CHUNK

    def body(hq_ref, hf_ref, hi_ref, hg_ref, lb_ref, gn_ref, o_ref, oa_ref, st_ref, state):
        @pl.when(pl.program_id(1) == 0)
        def _():
            state[...] = jnp.zeros_like(state)

        lbv = lb_ref[...]
        gn = gn_ref[...]
        row = lax.broadcasted_iota(jnp.int32, (C, C), 0)
        col = lax.broadcasted_iota(jnp.int32, (C, C), 1)
        causal = row >= col
        tril = causal.astype(F32)
        rid = lax.broadcasted_iota(jnp.int32, (C, HG_DK), 0)
        sls = [pl.ds(c * C, C) for c in range(nch)]
        pre = [_hgrn_chunk_common(hq_ref[sl, :], hf_ref[sl, :], lbv, tril, rid) for sl in sls]
        v2 = [_split2(hi_ref[sl, :]) for sl in sls]
        a_l, u_l = [], []
        for c in range(nch):
            _, q, _, _, k, _, _, e1, e2, e3 = pre[c]
            a_l.append(jnp.where(causal, _dot3(_split2(q * e1), _split2(k * e2), "nt"), 0.0))
            u_l.append(_dot3(v2[c], _split2(k * e3), "tn"))
        o_l = [_dot3(_split2(a_l[c]), v2[c], "nn") for c in range(nch)]
        st = state[...]
        st_l = []
        for c in range(nch):
            st_l.append(st)
            st = st * jnp.exp(pre[c][5]) + u_l[c]
        state[...] = st
        for c in range(nch):
            st_ref[0, c] = st_l[c]
            o_l[c] = o_l[c] + _dot3(_split2(pre[c][1] * pre[c][6]), _split2(st_l[c]), "nt")
        for c in range(nch):
            o, hg = o_l[c], hg_ref[sls[c], :]
            o_ref[sls[c], :] = o
            r = lax.rsqrt(jnp.mean(o * o, axis=-1, keepdims=True) + EPS)
            oa_ref[sls[c], :] = (o * r * gn * (hg * _sigmoid(hg))).astype(oa_ref.dtype)

    def grp(gidx):
        return pl.BlockSpec((T, 128), lambda h, t: (t, gidx * 8 + h))

    return pl.pallas_call(
        body, name=name, grid=(HG_HEADS, S // T),
        in_specs=[grp(0), grp(1), grp(2), grp(3),
                  pl.BlockSpec((1, 128), lambda h, t: (0, h)), pl.BlockSpec((1, 128), lambda h, t: (0, 0))],
        out_specs=[pl.BlockSpec((T, 128), lambda h, t: (t, h)), pl.BlockSpec((T, 128), lambda h, t: (t, h)),
                   pl.BlockSpec((1, nch, HG_DV, HG_DK), lambda h, t: (h, t, 0, 0))],
        out_shape=[jax.ShapeDtypeStruct((S, HG_HEADS * HG_DV), F32), jax.ShapeDtypeStruct((S, HG_HEADS * HG_DV), BF16),
                   jax.ShapeDtypeStruct((HG_HEADS, S // C, HG_DV, HG_DK), F32)],
        scratch_shapes=[pltpu.VMEM((HG_DV, HG_DK), F32)],
        compiler_params=_cparams(("parallel", "arbitrary")),
    )(proj, proj, proj, proj, lb, gnorm)


def _hgrn_bwd_phased(proj, lb, gnorm, o, states, doa, *, name, T=512):
    S = proj.shape[0]
    T = min(T, S)
    nch = T // HG_CHUNK
    C = HG_CHUNK
    nT = S // T

    def body(hq_ref, hf_ref, hi_ref, hg_ref, lb_ref, gn_ref, o_ref, st_ref, doa_ref,
             dhq_ref, dhf_ref, dhi_ref, dhg_ref, dlb_ref, dgn_ref, dstate):
        @pl.when(pl.program_id(1) == 0)
        def _():
            dstate[...] = jnp.zeros_like(dstate)
            dlb_ref[...] = jnp.zeros_like(dlb_ref)
            dgn_ref[...] = jnp.zeros_like(dgn_ref)

        lbv = lb_ref[...]
        gn = gn_ref[...]
        row = lax.broadcasted_iota(jnp.int32, (C, C), 0)
        col = lax.broadcasted_iota(jnp.int32, (C, C), 1)
        causal = row >= col
        tril = causal.astype(F32)
        triu = (row <= col).astype(F32)
        rid = lax.broadcasted_iota(jnp.int32, (C, HG_DK), 0)
        rng = range(nch)
        sls = [pl.ds(c * C, C) for c in rng]
        pre = [_hgrn_chunk_common(hq_ref[sl, :], hf_ref[sl, :], lbv, tril, rid) for sl in sls]
        do2, dgn_acc = [], jnp.zeros((1, HG_DV), F32)
        for c in rng:
            hg, ov = hg_ref[sls[c], :], o_ref[sls[c], :]
            r = lax.rsqrt(jnp.mean(ov * ov, axis=-1, keepdims=True) + EPS)
            xh = ov * r
            sgg = _sigmoid(hg)
            d_oa = doa_ref[sls[c], :].astype(F32)
            dz = d_oa * (hg * sgg)
            dhg_ref[sls[c], :] = (d_oa * (xh * gn) * (sgg * (1.0 + hg * (1.0 - sgg)))).astype(dhg_ref.dtype)
            dgn_acc = dgn_acc + jnp.sum(dz * xh, axis=0, keepdims=True)
            dxh = dz * gn
            do2.append(_split2(r * (dxh - xh * jnp.mean(dxh * xh, axis=-1, keepdims=True))))
        dgn_ref[0] += dgn_acc
        qi = [pre[c][1] * pre[c][6] for c in rng]
        qp = [pre[c][1] * pre[c][7] for c in rng]
        kp = [pre[c][4] * pre[c][8] for c in rng]
        kend = [pre[c][4] * pre[c][9] for c in rng]
        qi2, qp2, kp2, kend2 = ([_split2(t) for t in lst] for lst in (qi, qp, kp, kend))
        v2 = [_split2(hi_ref[sl, :]) for sl in sls]
        st0 = [st_ref[0, c] for c in rng]
        a2 = [_split2(jnp.where(causal, _dot3(qp2[c], kp2[c], "nt"), 0.0)) for c in rng]
        da2 = [_split2(jnp.where(causal, _dot3(do2[c], v2[c], "nt"), 0.0)) for c in rng]
        dqi = [_dot3(do2[c], _split2(st0[c]), "nn") for c in rng]
        w_l = [_dot3(do2[c], qi2[c], "tn") for c in rng]
        ds = dstate[...]
        ds1 = [None] * nch
        for c in reversed(rng):
            ds1[c] = ds
            ds = ds * jnp.exp(pre[c][5]) + w_l[c]
        dstate[...] = ds
        ds12 = [_split2(t) for t in ds1]
        dqp = [_dot3(da2[c], kp2[c], "nn") for c in rng]
        dkp = [_dot3(da2[c], qp2[c], "tn") for c in rng]
        dv = [_dot3(a2[c], do2[c], "tn") + _dot3(kend2[c], ds12[c], "nt") for c in rng]
        dkend = [_dot3(v2[c], ds12[c], "nn") for c in rng]
        dq_l, dk_l, db_l = [], [], []
        for c in rng:
            _, _, _, _, _, bend, eb, e1, e2, e3 = pre[c]
            dq_l.append(dqi[c] * eb + dqp[c] * e1)
            dk_l.append(dkp[c] * e2 + dkend[c] * e3)
            db = dqi[c] * qi[c] + dqp[c] * qp[c] - dkp[c] * kp[c] - dkend[c] * kend[c]
            dbend = (jnp.sum(dkend[c] * kend[c], axis=0, keepdims=True)
                     + jnp.exp(bend) * jnp.sum(ds1[c] * st0[c], axis=0, keepdims=True))
            db_l.append(db + jnp.where(rid == C - 1, dbend, 0.0))
        dg = [_dot(triu, db_l[c], _DIMS["nn"], precision=lax.Precision.HIGHEST) for c in rng]
        dlb_acc = jnp.zeros((1, HG_DK), F32)
        for c in rng:
            sq, _, sg, f, _, _, _, _, _, _ = pre[c]
            hq = hq_ref[sls[c], :]
            df = dg[c] / f - dk_l[c]
            dlb_acc = dlb_acc + jnp.sum(df * (1.0 - sg), axis=0, keepdims=True)
            dhf_ref[sls[c], :] = (df * (1.0 - lbv) * sg * (1.0 - sg)).astype(dhf_ref.dtype)
            dhq_ref[sls[c], :] = (dq_l[c] * (sq * (1.0 + hq * (1.0 - sq)))).astype(dhq_ref.dtype)
            dhi_ref[sls[c], :] = dv[c].astype(dhi_ref.dtype)
        dlb_ref[...] += dlb_acc

    def grp(gidx):
        return pl.BlockSpec((T, 128), lambda h, t: (nT - 1 - t, gidx * 8 + h))

    tok = pl.BlockSpec((T, 128), lambda h, t: (nT - 1 - t, h))
    big = jax.ShapeDtypeStruct((S, HG_HEADS * HG_DV), BF16)
    return pl.pallas_call(
        body, name=name, grid=(HG_HEADS, nT),
        in_specs=[grp(0), grp(1), grp(2), grp(3),
                  pl.BlockSpec((1, 128), lambda h, t: (0, h)), pl.BlockSpec((1, 128), lambda h, t: (0, 0)),
                  tok, pl.BlockSpec((1, nch, HG_DV, HG_DK), lambda h, t: (h, nT - 1 - t, 0, 0)), tok],
        out_specs=[tok, tok, tok, tok, pl.BlockSpec((1, 128), lambda h, t: (0, h)),
                   pl.BlockSpec((1, 1, 128), lambda h, t: (h, 0, 0))],
        out_shape=[big, big, big, big, jax.ShapeDtypeStruct((1, HG_HEADS * HG_DK), F32),
                   jax.ShapeDtypeStruct((HG_HEADS, 1, HG_DV), F32)],
        scratch_shapes=[pltpu.VMEM((HG_DV, HG_DK), F32)],
        compiler_params=_cparams(("parallel", "arbitrary")),
    )(proj, proj, proj, proj, lb, gnorm, o, states, doa)


def _hgrn_fwd(proj, lb, gnorm, *, name, T=512):
    S = proj.shape[0]
    T = min(T, S)
    nch = T // HG_CHUNK
    C = HG_CHUNK

    def body(hq_ref, hf_ref, hi_ref, hg_ref, lb_ref, gn_ref, o_ref, oa_ref, st_ref, state):
        @pl.when(pl.program_id(1) == 0)
        def _():
            state[...] = jnp.zeros_like(state)

        lbv = lb_ref[...]
        gn = gn_ref[...]
        row = lax.broadcasted_iota(jnp.int32, (C, C), 0)
        col = lax.broadcasted_iota(jnp.int32, (C, C), 1)
        causal = row >= col
        tril = causal.astype(F32)
        rid = lax.broadcasted_iota(jnp.int32, (C, HG_DK), 0)
        st = state[...]
        for c in range(nch):
            sl = pl.ds(c * C, C)
            hq, hf, v, hg = hq_ref[sl, :], hf_ref[sl, :], hi_ref[sl, :], hg_ref[sl, :]
            _, q, _, _, k, bend, eb, e1, e2, e3 = _hgrn_chunk_common(hq, hf, lbv, tril, rid)
            st_ref[0, c] = st
            o = _hdot(q * eb, st, "nt")
            a = jnp.where(causal, _hdot(q * e1, k * e2, "nt"), 0.0)
            o = o + _hdot(a, v, "nn")
            st = st * jnp.exp(bend) + _hdot(v, k * e3, "tn")
            o_ref[sl, :] = o
            r = lax.rsqrt(jnp.mean(o * o, axis=-1, keepdims=True) + EPS)
            oa_ref[sl, :] = (o * r * gn * (hg * _sigmoid(hg))).astype(oa_ref.dtype)
        state[...] = st

    def grp(gidx):
        return pl.BlockSpec((T, 128), lambda h, t: (t, gidx * 8 + h))

    return pl.pallas_call(
        body, name=name, grid=(HG_HEADS, S // T),
        in_specs=[grp(0), grp(1), grp(2), grp(3),
                  pl.BlockSpec((1, 128), lambda h, t: (0, h)), pl.BlockSpec((1, 128), lambda h, t: (0, 0))],
        out_specs=[pl.BlockSpec((T, 128), lambda h, t: (t, h)), pl.BlockSpec((T, 128), lambda h, t: (t, h)),
                   pl.BlockSpec((1, nch, HG_DV, HG_DK), lambda h, t: (h, t, 0, 0))],
        out_shape=[jax.ShapeDtypeStruct((S, HG_HEADS * HG_DV), F32), jax.ShapeDtypeStruct((S, HG_HEADS * HG_DV), BF16),
                   jax.ShapeDtypeStruct((HG_HEADS, S // C, HG_DV, HG_DK), F32)],
        scratch_shapes=[pltpu.VMEM((HG_DV, HG_DK), F32)],
        compiler_params=_cparams(("parallel", "arbitrary")),
    )(proj, proj, proj, proj, lb, gnorm)


def _hgrn_bwd(proj, lb, gnorm, o, states, doa, *, name, T=512):
    S = proj.shape[0]
    T = min(T, S)
    nch = T // HG_CHUNK
    C = HG_CHUNK
    nT = S // T

    def body(hq_ref, hf_ref, hi_ref, hg_ref, lb_ref, gn_ref, o_ref, st_ref, doa_ref,
             dhq_ref, dhf_ref, dhi_ref, dhg_ref, dlb_ref, dgn_ref, dstate):
        @pl.when(pl.program_id(1) == 0)
        def _():
            dstate[...] = jnp.zeros_like(dstate)
            dlb_ref[...] = jnp.zeros_like(dlb_ref)
            dgn_ref[...] = jnp.zeros_like(dgn_ref)

        lbv = lb_ref[...]
        gn = gn_ref[...]
        row = lax.broadcasted_iota(jnp.int32, (C, C), 0)
        col = lax.broadcasted_iota(jnp.int32, (C, C), 1)
        causal = row >= col
        tril = causal.astype(F32)
        triu = (row <= col).astype(F32)
        rid = lax.broadcasted_iota(jnp.int32, (C, HG_DK), 0)
        for c in reversed(range(nch)):
            sl = pl.ds(c * C, C)
            hq, hf, v, hg = hq_ref[sl, :], hf_ref[sl, :], hi_ref[sl, :], hg_ref[sl, :]
            sq, q, sg, f, k, bend, eb, e1, e2, e3 = _hgrn_chunk_common(hq, hf, lbv, tril, rid)
            qi, qp, kp, kend = q * eb, q * e1, k * e2, k * e3
            st0 = st_ref[0, c]
            ov = o_ref[sl, :]
            r = lax.rsqrt(jnp.mean(ov * ov, axis=-1, keepdims=True) + EPS)
            xh = ov * r
            sgg = _sigmoid(hg)
            d_oa = doa_ref[sl, :].astype(F32)
            dz = d_oa * (hg * sgg)
            dhg_ref[sl, :] = (d_oa * (xh * gn) * (sgg * (1.0 + hg * (1.0 - sgg)))).astype(dhg_ref.dtype)
            dgn_ref[0] += jnp.sum(dz * xh, axis=0, keepdims=True)
            dxh = dz * gn
            do = r * (dxh - xh * jnp.mean(dxh * xh, axis=-1, keepdims=True))
            ds1 = dstate[...]
            dqi = _hdot(do, st0, "nn")
            a = jnp.where(causal, _hdot(qp, kp, "nt"), 0.0)
            da = jnp.where(causal, _hdot(do, v, "nt"), 0.0)
            dv = _hdot(a, do, "tn") + _hdot(kend, ds1, "nt")
            dqp = _hdot(da, kp, "nn")
            dkp = _hdot(da, qp, "tn")
            dkend = _hdot(v, ds1, "nn")
            dq = dqi * eb + dqp * e1
            dk = dkp * e2 + dkend * e3
            db = dqi * qi + dqp * qp - dkp * kp - dkend * kend
            dbend = (jnp.sum(dkend * kend, axis=0, keepdims=True)
                     + jnp.exp(bend) * jnp.sum(ds1 * st0, axis=0, keepdims=True))
            db = db + jnp.where(rid == C - 1, dbend, 0.0)
            dg = _dot(triu, db, _DIMS["nn"], precision=lax.Precision.HIGHEST)
            df = dg / f - dk
            dlb_ref[...] += jnp.sum(df * (1.0 - sg), axis=0, keepdims=True)
            dhf_ref[sl, :] = (df * (1.0 - lbv) * sg * (1.0 - sg)).astype(dhf_ref.dtype)
            dhq_ref[sl, :] = (dq * (sq * (1.0 + hq * (1.0 - sq)))).astype(dhq_ref.dtype)
            dhi_ref[sl, :] = dv.astype(dhi_ref.dtype)
            dstate[...] = ds1 * jnp.exp(bend) + _hdot(do, qi, "tn")

    def grp(gidx):
        return pl.BlockSpec((T, 128), lambda h, t: (nT - 1 - t, gidx * 8 + h))

    tok = pl.BlockSpec((T, 128), lambda h, t: (nT - 1 - t, h))
    big = jax.ShapeDtypeStruct((S, HG_HEADS * HG_DV), BF16)
    return pl.pallas_call(
        body, name=name, grid=(HG_HEADS, nT),
        in_specs=[grp(0), grp(1), grp(2), grp(3),
                  pl.BlockSpec((1, 128), lambda h, t: (0, h)), pl.BlockSpec((1, 128), lambda h, t: (0, 0)),
                  tok, pl.BlockSpec((1, nch, HG_DV, HG_DK), lambda h, t: (h, nT - 1 - t, 0, 0)), tok],
        out_specs=[tok, tok, tok, tok, pl.BlockSpec((1, 128), lambda h, t: (0, h)),
                   pl.BlockSpec((1, 1, 128), lambda h, t: (h, 0, 0))],
        out_shape=[big, big, big, big, jax.ShapeDtypeStruct((1, HG_HEADS * HG_DK), F32),
                   jax.ShapeDtypeStruct((HG_HEADS, 1, HG_DV), F32)],
        scratch_shapes=[pltpu.VMEM((HG_DV, HG_DK), F32)],
        compiler_params=_cparams(("parallel", "arbitrary")),
    )(proj, proj, proj, proj, lb, gnorm, o, states, doa)


NEG = -1e30
FOX_SCALE = FOX_DH ** -0.5
FOX_PAIRS = FOX_HEADS // 2


def _fox_gate_fwd(ff, bias, *, name, T=512):
    S = ff.shape[0]
    T = min(T, S)

    def body(ff_ref, b_ref, c_ref, carry):
        @pl.when(pl.program_id(0) == 0)
        def _():
            carry[...] = jnp.zeros_like(carry)

        z = ff_ref[...] + b_ref[...]
        logf = jnp.minimum(z, 0.0) - jnp.log(1.0 + jnp.exp(-jnp.abs(z)))
        row = lax.broadcasted_iota(jnp.int32, (T, T), 0)
        col = lax.broadcasted_iota(jnp.int32, (T, T), 1)
        c = _dot((row >= col).astype(F32), logf, _DIMS["nn"], precision=lax.Precision.HIGHEST) + carry[...]
        c_ref[...] = c
        carry[...] = c[T - 1:T, :]

    return pl.pallas_call(
        body, name=name, grid=(S // T,),
        in_specs=[pl.BlockSpec((T, 128), lambda i: (i, 0)), pl.BlockSpec((1, 128), lambda i: (0, 0))],
        out_specs=pl.BlockSpec((T, 128), lambda i: (i, 0)),
        out_shape=jax.ShapeDtypeStruct((S, 128), F32),
        scratch_shapes=[pltpu.VMEM((1, 128), F32)],
        compiler_params=_cparams(("arbitrary",)),
    )(ff, bias)


def _fox_gate_bwd(ff, bias, dcs, *, name, T=512):
    S = ff.shape[0]
    T = min(T, S)
    nT = S // T

    def body(ff_ref, b_ref, d_ref, dff_ref, db_ref, carry):
        @pl.when(pl.program_id(0) == 0)
        def _():
            carry[...] = jnp.zeros_like(carry)
            db_ref[...] = jnp.zeros_like(db_ref)

        row = lax.broadcasted_iota(jnp.int32, (T, T), 0)
        col = lax.broadcasted_iota(jnp.int32, (T, T), 1)
        dlogf = carry[...] - _dot((row <= col).astype(F32), d_ref[...], _DIMS["nn"], precision=lax.Precision.HIGHEST)
        carry[...] = dlogf[0:1, :]
        dff = dlogf * (1.0 - _sigmoid(ff_ref[...] + b_ref[...]))
        dff_ref[...] = dff.astype(dff_ref.dtype)
        db_ref[...] += jnp.sum(dff, axis=0, keepdims=True)

    rev = pl.BlockSpec((T, 128), lambda i: (nT - 1 - i, 0))
    vec = pl.BlockSpec((1, 128), lambda i: (0, 0))
    return pl.pallas_call(
        body, name=name, grid=(nT,),
        in_specs=[rev, vec, rev], out_specs=[rev, vec],
        out_shape=[jax.ShapeDtypeStruct((S, 128), BF16), jax.ShapeDtypeStruct((1, 128), F32)],
        scratch_shapes=[pltpu.VMEM((1, 128), F32)],
        compiler_params=_cparams(("arbitrary",)),
    )(ff, bias, dcs)


def _fox_logits(q, k, cc, cr, qi, ki, tq, tk):
    s = _bdot(q, k, "nt") * FOX_SCALE + cc - cr
    qpos = qi * tq + lax.broadcasted_iota(jnp.int32, (tq, tk), 0)
    kpos = ki * tk + lax.broadcasted_iota(jnp.int32, (tq, tk), 1)
    return jnp.where(kpos <= qpos, s, NEG)


def _fox_fwd(proj, ccol, crow, *, name, tq=512, tk=512):
    S = proj.shape[0]
    tq, tk = min(tq, S), min(tk, S)

    def body(q_ref, k_ref, v_ref, cc_ref, cr_ref, o_ref, lse_ref, m_s, l_s, acc_s):
        qi, ki = pl.program_id(1), pl.program_id(2)

        @pl.when(ki == 0)
        def _():
            m_s[...] = jnp.full_like(m_s, NEG)
            l_s[...] = jnp.zeros_like(l_s)
            acc_s[...] = jnp.zeros_like(acc_s)

        @pl.when(ki <= qi)
        def _():
            for hh in range(2):
                ls = slice(hh * FOX_DH, (hh + 1) * FOX_DH)
                s = _fox_logits(q_ref[:, ls], k_ref[:, ls], cc_ref[0, :, hh:hh + 1], cr_ref[0, hh:hh + 1, :], qi, ki, tq, tk)
                m_old = m_s[hh]
                m_new = jnp.maximum(m_old, jnp.max(s, axis=-1, keepdims=True))
                p = jnp.exp(s - m_new)
                alpha = jnp.exp(m_old - m_new)
                l_s[hh] = alpha * l_s[hh] + jnp.sum(p, axis=-1, keepdims=True)
                p_hi = p.astype(BF16)
                p_lo = (p - p_hi.astype(F32)).astype(BF16)
                vv = v_ref[:, ls].astype(BF16)
                acc_s[hh] = alpha * acc_s[hh] + _bdot(p_hi, vv, "nn") + _bdot(p_lo, vv, "nn")
                m_s[hh] = m_new

        @pl.when(ki == qi)
        def _():
            for hh in range(2):
                o_ref[:, hh * FOX_DH:(hh + 1) * FOX_DH] = acc_s[hh] / l_s[hh]
                lse_ref[0, :, hh:hh + 1] = m_s[hh] + jnp.log(l_s[hh])

    qspec = pl.BlockSpec((tq, 128), lambda p, i, j: (i, 32 + p))
    kspec = pl.BlockSpec((tk, 128), lambda p, i, j: (jnp.minimum(j, i), 40 + p))
    vspec = pl.BlockSpec((tk, 128), lambda p, i, j: (jnp.minimum(j, i), 48 + p))
    ccs = pl.BlockSpec((1, tq, 2), lambda p, i, j: (p, i, 0))
    crs = pl.BlockSpec((1, 2, tk), lambda p, i, j: (p, 0, jnp.minimum(j, i)))
    return pl.pallas_call(
        body, name=name, grid=(FOX_PAIRS, S // tq, S // tk),
        in_specs=[qspec, kspec, vspec, ccs, crs],
        out_specs=[pl.BlockSpec((tq, 128), lambda p, i, j: (i, p)), ccs],
        out_shape=[jax.ShapeDtypeStruct((S, FOX_HEADS * FOX_DH), F32), jax.ShapeDtypeStruct((FOX_PAIRS, S, 2), F32)],
        scratch_shapes=[pltpu.VMEM((2, tq, 1), F32), pltpu.VMEM((2, tq, 1), F32), pltpu.VMEM((2, tq, FOX_DH), F32)],
        compiler_params=_cparams(("parallel", "parallel", "arbitrary")),
    )(proj, proj, proj, ccol, crow)


def _fox_bwd_dq(proj, ccol, crow, o, lse, do, *, name, tq=512, tk=512):
    S = proj.shape[0]
    tq, tk = min(tq, S), min(tk, S)

    def body(q_ref, k_ref, v_ref, cc_ref, cr_ref, o_ref, lse_ref, do_ref, dq_ref, dl_ref, acc_s):
        qi, ki = pl.program_id(1), pl.program_id(2)

        @pl.when(ki == 0)
        def _():
            acc_s[...] = jnp.zeros_like(acc_s)
            for hh in range(2):
                ls = slice(hh * FOX_DH, (hh + 1) * FOX_DH)
                dl_ref[0, :, hh:hh + 1] = jnp.sum(do_ref[:, ls].astype(F32) * o_ref[:, ls], axis=-1, keepdims=True)

        @pl.when(ki <= qi)
        def _():
            for hh in range(2):
                ls = slice(hh * FOX_DH, (hh + 1) * FOX_DH)
                s = _fox_logits(q_ref[:, ls], k_ref[:, ls], cc_ref[0, :, hh:hh + 1], cr_ref[0, hh:hh + 1, :], qi, ki, tq, tk)
                p = jnp.exp(s - lse_ref[0, :, hh:hh + 1])
                dp = _bdot(do_ref[:, ls], v_ref[:, ls], "nt")
                ds = p * (dp - dl_ref[0, :, hh:hh + 1])
                acc_s[hh] += _bdot(ds, k_ref[:, ls], "nn")

        @pl.when(ki == qi)
        def _():
            for hh in range(2):
                dq_ref[:, hh * FOX_DH:(hh + 1) * FOX_DH] = (acc_s[hh] * FOX_SCALE).astype(dq_ref.dtype)

    qspec = pl.BlockSpec((tq, 128), lambda p, i, j: (i, 32 + p))
    kspec = pl.BlockSpec((tk, 128), lambda p, i, j: (jnp.minimum(j, i), 40 + p))
    vspec = pl.BlockSpec((tk, 128), lambda p, i, j: (jnp.minimum(j, i), 48 + p))
    ccs = pl.BlockSpec((1, tq, 2), lambda p, i, j: (p, i, 0))
    crs = pl.BlockSpec((1, 2, tk), lambda p, i, j: (p, 0, jnp.minimum(j, i)))
    tok = pl.BlockSpec((tq, 128), lambda p, i, j: (i, p))
    return pl.pallas_call(
        body, name=name, grid=(FOX_PAIRS, S // tq, S // tk),
        in_specs=[qspec, kspec, vspec, ccs, crs, tok, ccs, tok],
        out_specs=[tok, ccs],
        out_shape=[jax.ShapeDtypeStruct((S, FOX_HEADS * FOX_DH), BF16), jax.ShapeDtypeStruct((FOX_PAIRS, S, 2), F32)],
        scratch_shapes=[pltpu.VMEM((2, tq, FOX_DH), F32)],
        compiler_params=_cparams(("parallel", "parallel", "arbitrary")),
    )(proj, proj, proj, ccol, crow, o, lse, do)


def _fox_bwd_dkv(proj, ccol, crow, lse, delta, do, *, name, tq=512, tk=512):
    S = proj.shape[0]
    tq, tk = min(tq, S), min(tk, S)
    nq = S // tq

    def body(q_ref, k_ref, v_ref, cc_ref, cr_ref, lse_ref, dl_ref, do_ref, dk_ref, dv_ref, dcs_ref, dk_s, dv_s):
        ki, qi = pl.program_id(1), pl.program_id(2)

        @pl.when(qi == 0)
        def _():
            dk_s[...] = jnp.zeros_like(dk_s)
            dv_s[...] = jnp.zeros_like(dv_s)
            dcs_ref[...] = jnp.zeros_like(dcs_ref)

        @pl.when(qi >= ki)
        def _():
            for hh in range(2):
                ls = slice(hh * FOX_DH, (hh + 1) * FOX_DH)
                s = _fox_logits(q_ref[:, ls], k_ref[:, ls], cc_ref[0, :, hh:hh + 1], cr_ref[0, hh:hh + 1, :], qi, ki, tq, tk)
                p = jnp.exp(s - lse_ref[0, :, hh:hh + 1])
                dp = _bdot(do_ref[:, ls], v_ref[:, ls], "nt")
                ds = p * (dp - dl_ref[0, :, hh:hh + 1])
                dv_s[hh] += _bdot(p, do_ref[:, ls], "tn")
                dk_s[hh] += _bdot(ds, q_ref[:, ls], "tn")
                dcs_ref[0, hh:hh + 1, :] += jnp.sum(ds, axis=0, keepdims=True)

        @pl.when(qi == nq - 1)
        def _():
            for hh in range(2):
                dk_ref[:, hh * FOX_DH:(hh + 1) * FOX_DH] = (dk_s[hh] * FOX_SCALE).astype(dk_ref.dtype)
                dv_ref[:, hh * FOX_DH:(hh + 1) * FOX_DH] = dv_s[hh].astype(dv_ref.dtype)

    qspec = pl.BlockSpec((tq, 128), lambda p, j, i: (jnp.maximum(i, j), 32 + p))
    kspec = pl.BlockSpec((tk, 128), lambda p, j, i: (j, 40 + p))
    vspec = pl.BlockSpec((tk, 128), lambda p, j, i: (j, 48 + p))
    ccs = pl.BlockSpec((1, tq, 2), lambda p, j, i: (p, jnp.maximum(i, j), 0))
    crs = pl.BlockSpec((1, 2, tk), lambda p, j, i: (p, 0, j))
    dos = pl.BlockSpec((tq, 128), lambda p, j, i: (jnp.maximum(i, j), p))
    ktok = pl.BlockSpec((tk, 128), lambda p, j, i: (j, p))
    big = jax.ShapeDtypeStruct((S, FOX_HEADS * FOX_DH), BF16)
    return pl.pallas_call(
        body, name=name, grid=(FOX_PAIRS, S // tk, nq),
        in_specs=[qspec, kspec, vspec, ccs, crs, ccs, ccs, dos],
        out_specs=[ktok, ktok, crs],
        out_shape=[big, big, jax.ShapeDtypeStruct((FOX_PAIRS, 2, S), F32)],
        scratch_shapes=[pltpu.VMEM((2, tk, FOX_DH), F32), pltpu.VMEM((2, tk, FOX_DH), F32)],
        compiler_params=_cparams(("parallel", "parallel", "arbitrary")),
    )(proj, proj, proj, ccol, crow, lse, delta, do)


AUG = FOX_DH
FOX_SUB = 2


def _split3(x):
    a = x.astype(BF16).astype(F32)
    r = x - a
    b = r.astype(BF16).astype(F32)
    return a, b, r - b


def _lane_fill(lane, base, pieces, start):
    for i, pc in enumerate(pieces):
        base = jnp.where(lane == start + i, pc, base)
    return base


def _fox_prep(proj, c_tok, *, name, T=512):
    S = proj.shape[0]
    T = min(T, S)

    def body(q_ref, k_ref, v_ref, c_ref, qa_ref, ka_ref, va_ref):
        pair = pl.program_id(0)
        lane = lax.broadcasted_iota(jnp.int32, (T, 128), 1)
        c = c_ref[...]
        ones3 = jnp.where((lane >= AUG) & (lane < AUG + 3), 1.0, 0.0)
        for hh in range(2):
            ch = jnp.sum(jnp.where(lane == 2 * pair + hh, c, 0.0), axis=-1, keepdims=True)
            c1, c2, c3 = _split3(ch)
            q, k, v = q_ref[...], k_ref[...], v_ref[...]
            if hh == 1:
                q, k, v = (pltpu.roll(t, 64, 1) for t in (q, k, v))
            aug_q = _lane_fill(lane, jnp.where((lane >= AUG + 3) & (lane < AUG + 6), 1.0, 0.0), (c1, c2, c3), AUG)
            aug_k = _lane_fill(lane, ones3, (-c1, -c2, -c3), AUG + 3)
            qa_ref[hh] = jnp.where(lane < AUG, q * FOX_SCALE, aug_q).astype(BF16)
            ka_ref[hh] = jnp.where(lane < AUG, k, aug_k).astype(BF16)
            va_ref[hh] = jnp.where(lane < AUG, v, ones3).astype(BF16)

    def grp(g):
        return pl.BlockSpec((T, 128), lambda p, t: (t, g * 8 + p))

    hm = pl.BlockSpec((2, T, 128), lambda p, t: (p, t, 0))
    out = jax.ShapeDtypeStruct((FOX_HEADS, S, 128), BF16)
    return pl.pallas_call(
        body, name=name, grid=(FOX_PAIRS, S // T),
        in_specs=[grp(4), grp(5), grp(6), pl.BlockSpec((T, 128), lambda p, t: (t, 0))],
        out_specs=[hm, hm, hm], out_shape=[out, out, out],
        compiler_params=_cparams(("parallel", "parallel")),
    )(proj, proj, proj, c_tok)


def _pair_lanes(lane, a0, a1):
    return jnp.where(lane < AUG, a0, pltpu.roll(a1, 64, 1))


def _tri_tables(nb, by_query):
    if by_query:
        pairs = [(i, j) for i in range(nb) for j in range(i + 1)]
    else:
        pairs = [(i, j) for j in range(nb) for i in range(j, nb)]
    return (jnp.asarray(np.array([p[0] for p in pairs], np.int32)),
            jnp.asarray(np.array([p[1] for p in pairs], np.int32)))


def _fox_fwd2(qa, ka, va, *, name, tb=512):
    S = qa.shape[1]
    tb = min(tb, S)
    rs = tb // FOX_SUB
    qtab, ktab = _tri_tables(S // tb, True)

    def body(qt_ref, kt_ref, qa_ref, ka_ref, va_ref, o_ref, qb_ref, m_s, acc_s):
        qi, ki = qt_ref[pl.program_id(1)], kt_ref[pl.program_id(1)]

        @pl.when(ki == 0)
        def _():
            m_s[...] = jnp.full_like(m_s, NEG)
            acc_s[...] = jnp.zeros_like(acc_s)

        def step(masked):
            for hh in range(2):
                s = _dot(qa_ref[hh], ka_ref[hh], _DIMS["nt"])
                if masked:
                    row = lax.broadcasted_iota(jnp.int32, (tb, tb), 0)
                    col = lax.broadcasted_iota(jnp.int32, (tb, tb), 1)
                    s = jnp.where(col <= row, s, NEG)
                m_old = m_s[hh]
                m_new = jnp.maximum(m_old, jnp.max(s, axis=-1, keepdims=True))
                p = jnp.exp(s - m_new)
                p_hi = p.astype(BF16)
                p_lo = (p - p_hi.astype(F32)).astype(BF16)
                vv = va_ref[hh]
                acc_s[hh] = (jnp.exp(m_old - m_new) * acc_s[hh]
                             + _dot(p_hi, vv, _DIMS["nn"]) + _dot(p_lo, vv, _DIMS["nn"]))
                m_s[hh] = m_new

        @pl.when(ki < qi)
        def _():
            step(False)

        @pl.when(ki == qi)
        def _():
            step(True)
            lane = lax.broadcasted_iota(jnp.int32, (tb, 128), 1)
            outs = []
            for hh in range(2):
                acc = acc_s[hh]
                l = acc[:, AUG:AUG + 1]
                outs.append(acc / l)
                qf = qa_ref[hh].astype(F32)
                cb = qf[:, AUG:AUG + 1] + qf[:, AUG + 1:AUG + 2] + qf[:, AUG + 2:AUG + 3] - (m_s[hh] + jnp.log(l))
                qb_ref[hh] = _lane_fill(lane, qf, _split3(cb), AUG).astype(BF16)
            o_ref[...] = _pair_lanes(lane, outs[0], outs[1])

    qs = pl.BlockSpec((2, tb, 128), lambda p, t, qt, kt: (p, qt[t], 0))
    ks = pl.BlockSpec((2, tb, 128), lambda p, t, qt, kt: (p, kt[t], 0))
    return pl.pallas_call(
        body, name=name,
        grid_spec=pltpu.PrefetchScalarGridSpec(
            num_scalar_prefetch=2, grid=(FOX_PAIRS, qtab.shape[0]), in_specs=[qs, ks, ks],
            out_specs=[pl.BlockSpec((tb, 128), lambda p, t, qt, kt: (qt[t], p)), qs],
            scratch_shapes=[pltpu.VMEM((2, tb, 1), F32), pltpu.VMEM((2, tb, 128), F32)]),
        out_shape=[jax.ShapeDtypeStruct((S, FOX_HEADS * FOX_DH), F32), jax.ShapeDtypeStruct((FOX_HEADS, S, 128), BF16)],
        compiler_params=_cparams(("parallel", "arbitrary")),
    )(qtab, ktab, qa, ka, va)


def _fox_bwd_prep(o, do, *, name, T=512):
    S = o.shape[0]
    T = min(T, S)

    def body(o_ref, do_ref, dob_ref):
        lane = lax.broadcasted_iota(jnp.int32, (T, 128), 1)
        d = do_ref[...].astype(F32)
        prod = d * o_ref[...]
        for hh in range(2):
            mine = (lane < AUG) if hh == 0 else (lane >= AUG)
            delta = jnp.sum(jnp.where(mine, prod, 0.0), axis=-1, keepdims=True)
            dh = d if hh == 0 else pltpu.roll(d, 64, 1)
            dob_ref[hh] = _lane_fill(lane, jnp.where(lane < AUG, dh, 0.0), _split3(-delta), AUG).astype(BF16)

    tok = pl.BlockSpec((T, 128), lambda p, t: (t, p))
    return pl.pallas_call(
        body, name=name, grid=(FOX_PAIRS, S // T),
        in_specs=[tok, tok], out_specs=pl.BlockSpec((2, T, 128), lambda p, t: (p, t, 0)),
        out_shape=jax.ShapeDtypeStruct((FOX_HEADS, S, 128), BF16),
        compiler_params=_cparams(("parallel", "parallel")),
    )(o, do)


def _fox_bwd_dq2(qb, ka, va, dob, *, name, tb=512, comm=None):
    S = qb.shape[1]
    tb = min(tb, S)
    rs = tb // FOX_SUB
    nb = S // tb
    qtab, ktab = _tri_tables(nb, True)
    nc = comm.n if comm is not None else 0
    ntri = qtab.shape[0]

    def body(qt_ref, kt_ref, qb_ref, ka_ref, va_ref, dob_ref, *rest):
        c_in, (dq_ref, dcs_ref), c_out = rest[:nc], rest[nc:nc + 2], rest[nc + 2:2 * nc + 2]
        acc_s, c_sems = rest[2 * nc + 2], rest[2 * nc + 3:]
        qi, ki = qt_ref[pl.program_id(1)], kt_ref[pl.program_id(1)]
        if comm is not None:
            @pl.when((pl.program_id(0) == 0) & (pl.program_id(1) == 0))
            def _():
                comm.start(c_in, c_out, c_sems)

        @pl.when(ki == 0)
        def _():
            acc_s[...] = jnp.zeros_like(acc_s)

        def step(masked):
            for hh in range(2):
                s = _dot(qb_ref[hh], ka_ref[hh], _DIMS["nt"])
                if masked:
                    row = lax.broadcasted_iota(jnp.int32, (tb, tb), 0)
                    col = lax.broadcasted_iota(jnp.int32, (tb, tb), 1)
                    s = jnp.where(col <= row, s, NEG)
                ds = jnp.exp(s) * _dot(dob_ref[hh], va_ref[hh], _DIMS["nt"])
                dcs_ref[0, 0, hh:hh + 1, :] = jnp.sum(ds, axis=0, keepdims=True)
                acc_s[hh] += _dot(ds.astype(BF16), ka_ref[hh], _DIMS["nn"])

        @pl.when(ki < qi)
        def _():
            step(False)

        @pl.when(ki == qi)
        def _():
            step(True)
            lane = lax.broadcasted_iota(jnp.int32, (tb, 128), 1)
            dq_ref[...] = (_pair_lanes(lane, acc_s[0], acc_s[1]) * FOX_SCALE).astype(dq_ref.dtype)

        if comm is not None:
            @pl.when((pl.program_id(0) == FOX_PAIRS - 1) & (pl.program_id(1) == ntri - 1))
            def _():
                comm.finish(c_in, c_out, c_sems)

    qs = pl.BlockSpec((2, tb, 128), lambda p, t, qt, kt: (p, qt[t], 0))
    ks = pl.BlockSpec((2, tb, 128), lambda p, t, qt, kt: (p, kt[t], 0))
    outs = pl.pallas_call(
        body, name=name,
        grid_spec=pltpu.PrefetchScalarGridSpec(
            num_scalar_prefetch=2, grid=(FOX_PAIRS, ntri), in_specs=[qs, ks, ks, qs] + [ANY] * nc,
            out_specs=[pl.BlockSpec((tb, 128), lambda p, t, qt, kt: (qt[t], p)),
                       pl.BlockSpec((1, 1, 2, tb), lambda p, t, qt, kt: (p, qt[t], 0, kt[t]))] + [ANY] * nc,
            scratch_shapes=[pltpu.VMEM((2, tb, 128), F32)] + (comm.scratch if comm is not None else [])),
        out_shape=[jax.ShapeDtypeStruct((S, FOX_HEADS * FOX_DH), BF16),
                   jax.ShapeDtypeStruct((FOX_PAIRS, nb, 2, S), F32)] + (comm.out_shapes if comm is not None else []),
        compiler_params=_cparams(("parallel", "arbitrary") if comm is None else ("arbitrary", "arbitrary")),
    )(qtab, ktab, qb, ka, va, dob, *(comm.inputs if comm is not None else []))
    return (outs[0], outs[1]) if comm is None else (outs[0], outs[1], outs[2:])


def _fox_bwd_dkv2(qb, ka, va, dob, *, name, tb=512):
    S = qb.shape[1]
    tb = min(tb, S)
    rs = tb // FOX_SUB
    nb = S // tb
    qtab, ktab = _tri_tables(nb, False)

    def body(qt_ref, kt_ref, qb_ref, ka_ref, va_ref, dob_ref, dk_ref, dv_ref, dk_s, dv_s):
        qi, ki = qt_ref[pl.program_id(1)], kt_ref[pl.program_id(1)]

        @pl.when(qi == ki)
        def _():
            dk_s[...] = jnp.zeros_like(dk_s)
            dv_s[...] = jnp.zeros_like(dv_s)

        def step(masked):
            for hh in range(2):
                st = _dot(ka_ref[hh], qb_ref[hh], _DIMS["nt"])
                if masked:
                    row = lax.broadcasted_iota(jnp.int32, (tb, tb), 0)
                    col = lax.broadcasted_iota(jnp.int32, (tb, tb), 1)
                    st = jnp.where(row <= col, st, NEG)
                pt = jnp.exp(st)
                dst = pt * _dot(va_ref[hh], dob_ref[hh], _DIMS["nt"])
                dv_s[hh] += _dot(pt.astype(BF16), dob_ref[hh], _DIMS["nn"])
                dk_s[hh] += _dot(dst.astype(BF16), qb_ref[hh], _DIMS["nn"])

        @pl.when(qi > ki)
        def _():
            step(False)

        @pl.when(qi == ki)
        def _():
            step(True)

        @pl.when(qi == nb - 1)
        def _():
            lane = lax.broadcasted_iota(jnp.int32, (tb, 128), 1)
            dk_ref[...] = _pair_lanes(lane, dk_s[0], dk_s[1]).astype(dk_ref.dtype)
            dv_ref[...] = _pair_lanes(lane, dv_s[0], dv_s[1]).astype(dv_ref.dtype)

    ks = pl.BlockSpec((2, tb, 128), lambda p, t, qt, kt: (p, kt[t], 0))
    qs = pl.BlockSpec((2, tb, 128), lambda p, t, qt, kt: (p, qt[t], 0))
    tok = pl.BlockSpec((tb, 128), lambda p, t, qt, kt: (kt[t], p))
    big = jax.ShapeDtypeStruct((S, FOX_HEADS * FOX_DH), BF16)
    return pl.pallas_call(
        body, name=name,
        grid_spec=pltpu.PrefetchScalarGridSpec(
            num_scalar_prefetch=2, grid=(FOX_PAIRS, qtab.shape[0]), in_specs=[qs, ks, ks, qs], out_specs=[tok, tok],
            scratch_shapes=[pltpu.VMEM((2, tb, 128), F32), pltpu.VMEM((2, tb, 128), F32)]),
        out_shape=[big, big],
        compiler_params=_cparams(("parallel", "arbitrary")),
    )(qtab, ktab, qb, ka, va, dob)


def _merge_fwd(proj, pa, pb, *, name, T=512):
    S, D = pa.shape
    T = min(T, S)

    def body(ga_ref, gb_ref, pa_ref, pb_ref, m_ref):
        m_ref[...] = (_sigmoid(ga_ref[...]) * pa_ref[...] + _sigmoid(gb_ref[...]) * pb_ref[...]).astype(m_ref.dtype)

    tok = pl.BlockSpec((T, D), lambda i: (i, 0))
    return pl.pallas_call(
        body, name=name, grid=(S // T,),
        in_specs=[pl.BlockSpec((T, D), lambda i: (i, 7)), pl.BlockSpec((T, D), lambda i: (i, 8)), tok, tok],
        out_specs=tok, out_shape=jax.ShapeDtypeStruct((S, D), BF16),
        compiler_params=_cparams(("parallel",)),
    )(proj, proj, pa, pb)


def _merge_bwd(proj, pa, pb, dm, *, name, T=512):
    S, D = pa.shape
    T = min(T, S)

    def body(ga_ref, gb_ref, pa_ref, pb_ref, dm_ref, dpa_ref, dpb_ref, dga_ref, dgb_ref):
        dm_ = dm_ref[...]
        sa, sb = _sigmoid(ga_ref[...]), _sigmoid(gb_ref[...])
        dpa_ref[...] = (dm_ * sa).astype(BF16)
        dpb_ref[...] = (dm_ * sb).astype(BF16)
        dga_ref[...] = (dm_ * pa_ref[...] * sa * (1.0 - sa)).astype(BF16)
        dgb_ref[...] = (dm_ * pb_ref[...] * sb * (1.0 - sb)).astype(BF16)

    tok = pl.BlockSpec((T, D), lambda i: (i, 0))
    big = jax.ShapeDtypeStruct((S, D), BF16)
    return pl.pallas_call(
        body, name=name, grid=(S // T,),
        in_specs=[pl.BlockSpec((T, D), lambda i: (i, 7)), pl.BlockSpec((T, D), lambda i: (i, 8)), tok, tok, tok],
        out_specs=[tok, tok, tok, tok], out_shape=[big, big, big, big],
        compiler_params=_cparams(("parallel",)),
    )(proj, proj, pa, pb, dm)


INV_SQRT2 = 0.7071067811865476
INV_SQRT2PI = 0.3989422804014327


def _shifted(u, prev, rid):
    m1 = jnp.where(rid == 0, prev[7:8, :], pltpu.roll(u, 1, 0))
    m2 = jnp.where(rid == 0, prev[6:7, :], jnp.where(rid == 1, prev[7:8, :], pltpu.roll(u, 2, 0)))
    return m1, m2


def _conv_acc(u, prev, w_ref, b_ref, rid):
    m1, m2 = _shifted(u, prev, rid)
    return b_ref[...] + w_ref[0:1, :] * m2 + w_ref[1:2, :] * m1 + w_ref[2:3, :] * u, m1, m2


def _convglu_fwd(ug, uv, wg, wv, bg, bv, *, name, T=512, tc=256):
    S, F = ug.shape
    T = min(T, S)

    def body(ug_ref, uv_ref, wg_ref, wv_ref, bg_ref, bv_ref, a_ref, pg, pv):
        @pl.when(pl.program_id(1) == 0)
        def _():
            pg[...] = jnp.zeros_like(pg)
            pv[...] = jnp.zeros_like(pv)

        rid = lax.broadcasted_iota(jnp.int32, (T, tc), 0)
        g_, v_ = ug_ref[...], uv_ref[...]
        accg, _, _ = _conv_acc(g_, pg[...], wg_ref, bg_ref, rid)
        accv, _, _ = _conv_acc(v_, pv[...], wv_ref, bv_ref, rid)
        gel = 0.5 * accg * (1.0 + lax.erf(accg * INV_SQRT2))
        a_ref[...] = (gel * accv).astype(a_ref.dtype)
        pg[...] = g_[T - 8:T, :]
        pv[...] = v_[T - 8:T, :]

    tok = pl.BlockSpec((T, tc), lambda j, t: (t, j))
    w3 = pl.BlockSpec((3, tc), lambda j, t: (0, j))
    b1 = pl.BlockSpec((1, tc), lambda j, t: (0, j))
    return pl.pallas_call(
        body, name=name, grid=(F // tc, S // T),
        in_specs=[tok, tok, w3, w3, b1, b1], out_specs=tok,
        out_shape=jax.ShapeDtypeStruct((S, F), BF16),
        scratch_shapes=[pltpu.VMEM((8, tc), F32), pltpu.VMEM((8, tc), F32)],
        compiler_params=_cparams(("parallel", "arbitrary")),
    )(ug, uv, wg, wv, bg, bv)


def _convglu_bwd_acc(ug, uv, wg, wv, bg, bv, da, *, name, T=512, tc=256):
    S, F = ug.shape
    T = min(T, S)

    def body(ug_ref, uv_ref, wg_ref, wv_ref, bg_ref, bv_ref, da_ref,
             dg_ref, dv_ref, dwg_ref, dwv_ref, dbg_ref, dbv_ref, pg, pv):
        @pl.when(pl.program_id(1) == 0)
        def _():
            pg[...] = jnp.zeros_like(pg)
            pv[...] = jnp.zeros_like(pv)
            for r in (dwg_ref, dwv_ref, dbg_ref, dbv_ref):
                r[...] = jnp.zeros_like(r)

        rid = lax.broadcasted_iota(jnp.int32, (T, tc), 0)
        g_, v_ = ug_ref[...], uv_ref[...]
        accg, g1, g2 = _conv_acc(g_, pg[...], wg_ref, bg_ref, rid)
        accv, v1, v2 = _conv_acc(v_, pv[...], wv_ref, bv_ref, rid)
        cdf = 0.5 * (1.0 + lax.erf(accg * INV_SQRT2))
        pdf = INV_SQRT2PI * jnp.exp(-0.5 * accg * accg)
        da_ = da_ref[...].astype(F32)
        dgate = da_ * accv * (cdf + accg * pdf)
        dval = da_ * (accg * cdf)
        dg_ref[...] = dgate.astype(dg_ref.dtype)
        dv_ref[...] = dval.astype(dv_ref.dtype)
        dbg_ref[...] += jnp.sum(dgate, axis=0, keepdims=True)
        dbv_ref[...] += jnp.sum(dval, axis=0, keepdims=True)
        for j, (sg_, sv_) in enumerate(((g2, v2), (g1, v1), (g_, v_))):
            dwg_ref[j:j + 1, :] += jnp.sum(dgate * sg_, axis=0, keepdims=True)
            dwv_ref[j:j + 1, :] += jnp.sum(dval * sv_, axis=0, keepdims=True)
        pg[...] = g_[T - 8:T, :]
        pv[...] = v_[T - 8:T, :]

    tok = pl.BlockSpec((T, tc), lambda j, t: (t, j))
    w3 = pl.BlockSpec((3, tc), lambda j, t: (0, j))
    b1 = pl.BlockSpec((1, tc), lambda j, t: (0, j))
    big = jax.ShapeDtypeStruct((S, F), BF16)
    return pl.pallas_call(
        body, name=name, grid=(F // tc, S // T),
        in_specs=[tok, tok, w3, w3, b1, b1, tok], out_specs=[tok, tok, w3, w3, b1, b1],
        out_shape=[big, big, jax.ShapeDtypeStruct((3, F), F32), jax.ShapeDtypeStruct((3, F), F32),
                   jax.ShapeDtypeStruct((1, F), F32), jax.ShapeDtypeStruct((1, F), F32)],
        scratch_shapes=[pltpu.VMEM((8, tc), F32), pltpu.VMEM((8, tc), F32)],
        compiler_params=_cparams(("parallel", "arbitrary")),
    )(ug, uv, wg, wv, bg, bv, da)


def _conv_bwd_u(dacc, w, *, name, T=512, tc=256):
    S, F = dacc.shape
    T = min(T, S)
    nT = S // T

    def body(d_ref, w_ref, du_ref, nxt):
        @pl.when(pl.program_id(1) == 0)
        def _():
            nxt[...] = jnp.zeros_like(nxt)

        rid = lax.broadcasted_iota(jnp.int32, (T, tc), 0)
        d = d_ref[...].astype(F32)
        nx = nxt[...]
        p1 = jnp.where(rid == T - 1, nx[0:1, :], pltpu.roll(d, T - 1, 0))
        p2 = jnp.where(rid == T - 1, nx[1:2, :], jnp.where(rid == T - 2, nx[0:1, :], pltpu.roll(d, T - 2, 0)))
        du_ref[...] = (w_ref[2:3, :] * d + w_ref[1:2, :] * p1 + w_ref[0:1, :] * p2).astype(du_ref.dtype)
        nxt[...] = d[0:8, :]

    tok = pl.BlockSpec((T, tc), lambda j, t: (nT - 1 - t, j))
    return pl.pallas_call(
        body, name=name, grid=(F // tc, nT),
        in_specs=[tok, pl.BlockSpec((3, tc), lambda j, t: (0, j))], out_specs=tok,
        out_shape=jax.ShapeDtypeStruct((S, F), BF16),
        scratch_shapes=[pltpu.VMEM((8, tc), F32)],
        compiler_params=_cparams(("parallel", "arbitrary")),
    )(dacc, w)


def _late_weights(g_a, g_b, g_o, g_up, g_cw, g_d):
    wup = jnp.concatenate([g_up[d] for d in range(N_DEV)], axis=1)
    cw = jnp.concatenate([g_cw[d] for d in range(N_DEV)], axis=1)
    return dict(wa=g_a.reshape(D_MODEL, D_MODEL), wb=g_b.reshape(D_MODEL, D_MODEL), wo=g_o.reshape(D_MODEL, D_MODEL),
                wug=wup[:, :D_FF], wuv=wup[:, D_FF:], cwg=cw[:, :D_FF], cwv=cw[:, D_FF:], wd=g_d.reshape(D_FF, D_MODEL))


def _early_grad_blocks(d_wa, d_wb, d_wo, d_wug, d_wuv, d_wd):
    up = jnp.stack([d_wug[:, d * 704:(d + 1) * 704] for d in range(4)]
                   + [d_wuv[:, d * 704:(d + 1) * 704] for d in range(4)])
    return [d_wa.reshape(N_DEV, 128, D_MODEL), d_wb.reshape(N_DEV, 128, D_MODEL), d_wo.reshape(N_DEV, 128, D_MODEL),
            up, d_wd.reshape(N_DEV, 352, D_MODEL)]


def _local_step(x, tgt, w, p, late=None, exchange_early=False):
    S = x.shape[0]
    mm = _matmul
    n1 = _rms_fwd(x, p["norm_mix"], name="rms1_fwd")
    if late is None:
        proj = mm(n1, w["wm"], "nn", name="proj_main")
    else:
        proj, gathered = mm(n1, w["wm"], "nn", comm=late, name="proj_main")
        w = {**w, **_late_weights(*gathered)}
    ff = mm(n1, w["wff"], "nn", name="proj_ff")
    lb = _lb_fwd(p["hg_lb_logits"], name="lb_fwd")
    gnorm = p["hg_norm"].reshape(1, HG_DV)
    o_hg, oa, states = _hgrn_fwd_phased(proj, lb, gnorm, name="hgrn_fwd")
    bias = jnp.pad(p["fox_f_bias"].reshape(1, FOX_HEADS), ((0, 0), (0, 128 - FOX_HEADS)))
    c = _fox_gate_fwd(ff, bias, name="fox_gate_fwd")
    qa, ka, va = _fox_prep(proj, c, name="fox_prep")
    ob, qb = _fox_fwd2(qa, ka, va, name="fox_fwd")
    pa = mm(oa, w["wa"], "nn", name="branch_a")
    pb = mm(ob, w["wb"], "nn", name="branch_b")
    merged = _merge_fwd(proj, pa, pb, name="merge_fwd")
    h1 = mm(merged, w["wo"], "nn", addend=x, name="mix_out")
    n2 = _rms_fwd(h1, p["norm_ffn"], name="rms2_fwd")
    ug = mm(n2, w["wug"], "nn", name="up_gate")
    uv = mm(n2, w["wuv"], "nn", name="up_val")
    a = _convglu_fwd(ug, uv, w["cwg"], w["cwv"], p["cbg"], p["cbv"], name="convglu_fwd")
    h2 = mm(a, w["wd"], "nn", addend=h1, name="ffn_down")
    loss, dh2, d_norm_final = _loss_head(h2, p["norm_final"], tgt, name="loss_head")
    da = mm(dh2, w["wd"], "nt", out_dtype=BF16, name="d_act")
    d_wd = mm(a, dh2, "tn", out_dtype=BF16, name="dw_down")
    daccg, daccv, d_cwg, d_cwv, d_cbg, d_cbv = _convglu_bwd_acc(
        ug, uv, w["cwg"], w["cwv"], p["cbg"], p["cbv"], da, name="convglu_bwd")
    dug = _conv_bwd_u(daccg, w["cwg"], name="conv_bwd_gate")
    duv = _conv_bwd_u(daccv, w["cwv"], name="conv_bwd_val")
    dn2 = mm(dug, w["wug"], "nt", name="dn2_gate")
    dn2 = mm(duv, w["wuv"], "nt", addend=dn2, name="dn2_val")
    d_wug = mm(n2, dug, "tn", out_dtype=BF16, name="dw_up_gate")
    d_wuv = mm(n2, duv, "tn", out_dtype=BF16, name="dw_up_val")
    dh1, d_norm_ffn = _rms_bwd(h1, p["norm_ffn"], dn2, dh2, name="rms2_bwd")
    dmerged = mm(dh1, w["wo"], "nt", name="d_merged")
    d_wo = mm(merged, dh1, "tn", out_dtype=BF16, name="dw_out")
    dpa, dpb, dga, dgb = _merge_bwd(proj, pa, pb, dmerged, name="merge_bwd")
    doa = mm(dpa, w["wa"], "nt", name="d_oa")
    dob = mm(dpb, w["wb"], "nt", out_dtype=BF16, name="d_ob")
    d_wa = mm(oa, dpa, "tn", out_dtype=BF16, name="dw_branch_a")
    d_wb = mm(ob, dpb, "tn", out_dtype=BF16, name="dw_branch_b")
    dhq, dhf, dhi, dhg, dlb, dgn8 = _hgrn_bwd_phased(proj, lb, gnorm, o_hg, states, doa, name="hgrn_bwd")
    d_logits = _lb_bwd(p["hg_lb_logits"], dlb, name="lb_bwd")
    dob_hm = _fox_bwd_prep(ob, dob, name="fox_bwd_prep")
    early_parts = None
    if exchange_early:
        comm = _ExchangeComm(_early_grad_blocks(d_wa, d_wb, d_wo, d_wug, d_wuv, d_wd))
        dq, dcsp, early_parts = _fox_bwd_dq2(qb, ka, va, dob_hm, comm=comm, name="fox_bwd_dq")
    else:
        dq, dcsp = _fox_bwd_dq2(qb, ka, va, dob_hm, name="fox_bwd_dq")
    dk, dv = _fox_bwd_dkv2(qb, ka, va, dob_hm, name="fox_bwd_dkv")
    nb = dcsp.shape[1]
    written = (jnp.arange(S) // (S // nb))[None, None, None, :] <= jnp.arange(nb)[None, :, None, None]
    dcs = jnp.sum(jnp.where(written, dcsp, 0.0), axis=1)
    dcs_tok = jnp.pad(dcs.reshape(FOX_HEADS, S).T, ((0, 0), (0, 128 - FOX_HEADS)))
    dff, dbias = _fox_gate_bwd(ff, bias, dcs_tok, name="fox_gate_bwd")
    dproj = jnp.concatenate([dhq, dhf, dhi, dhg, dq, dk, dv, dga, dgb], axis=1)
    dn1 = mm(dff, w["wff"], "nt", name="dn1_ff")
    dn1 = mm(dproj, w["wm"], "nt", addend=dn1, name="dn1_main")
    d_wm = mm(n1, dproj, "tn", out_dtype=BF16, name="dw_in_main")
    d_wff = mm(n1, dff, "tn", out_dtype=BF16, name="dw_in_ff")
    dx, d_norm_mix = _rms_bwd(x, p["norm_mix"], dn1, dh1, name="rms1_bwd")
    grads = dict(
        wm=d_wm, wff=d_wff, wa=d_wa, wb=d_wb, wo=d_wo, wug=d_wug, wuv=d_wuv, cwg=d_cwg, cwv=d_cwv, wd=d_wd,
        norm_mix=d_norm_mix.reshape(-1), fox_f_bias=dbias[0, :FOX_HEADS], hg_lb_logits=d_logits,
        hg_norm=jnp.sum(dgn8, axis=0).reshape(-1), norm_ffn=d_norm_ffn.reshape(-1), cbg=d_cbg, cbv=d_cbv,
        norm_final=d_norm_final.reshape(-1), early_parts=early_parts)
    return loss, dx, grads


MESH = pl.DeviceIdType.MESH
ANY = pl.BlockSpec(memory_space=pl.ANY)


def _all_gather(xs, *, name):
    def body(x_ref, out_ref, send_sems, recv_sems, local_sem):
        x, y, c = lax.axis_index("x"), lax.axis_index("y"), lax.axis_index("c")
        me, sibling = (x, y, c), (x, y, 1 - c)
        chips = [(1 - x, y), (x, 1 - y), (1 - x, 1 - y)]

        def rows(px, py, pc):
            return out_ref.at[4 * px + 2 * py + pc]

        def copy(k, block, to, src=None):
            return pltpu.make_async_remote_copy(
                src_ref=rows(*block) if src is None else src, dst_ref=rows(*block),
                send_sem=send_sems.at[k], recv_sem=recv_sems.at[k], device_id=to, device_id_type=MESH)

        mine = pltpu.make_async_copy(x_ref, rows(*me), local_sem)
        mine.start()
        first = [copy(0, me, sibling, src=x_ref)]
        first += [copy(1 + j, me, (*chip, c), src=x_ref) for j, chip in enumerate(chips)]
        for cp in first:
            cp.start()
        passed = [copy(4 + j, (*chip, c), sibling) for j, chip in enumerate(chips)]
        for j, chip in enumerate(chips):
            copy(1 + j, (*chip, c), me).wait_recv()
            passed[j].start()
        copy(0, sibling, me).wait_recv()
        for j, chip in enumerate(chips):
            copy(4 + j, (*chip, 1 - c), me).wait_recv()
        for cp in first + passed:
            cp.wait_send()
        mine.wait()

    return pl.pallas_call(
        body, name=name, in_specs=[ANY], out_specs=ANY,
        out_shape=jax.ShapeDtypeStruct((N_DEV,) + xs.shape, xs.dtype),
        scratch_shapes=[pltpu.SemaphoreType.DMA((7,)), pltpu.SemaphoreType.DMA((7,)), pltpu.SemaphoreType.DMA],
    )(xs)


def _exchange_blocks(g, *, name):
    def body(g_ref, out_ref, send_sems, recv_sems, local_sem):
        x, y, c = lax.axis_index("x"), lax.axis_index("y"), lax.axis_index("c")
        me = 4 * x + 2 * y + c
        mine = pltpu.make_async_copy(g_ref.at[me], out_ref.at[me], local_sem)
        mine.start()
        sends, recvs = [], []
        for k in range(1, N_DEV):
            px = 1 - x if k & 4 else x
            py = 1 - y if k & 2 else y
            pc = 1 - c if k & 1 else c
            p = 4 * px + 2 * py + pc
            sends.append(pltpu.make_async_remote_copy(
                src_ref=g_ref.at[p], dst_ref=out_ref.at[me], send_sem=send_sems.at[k - 1], recv_sem=recv_sems.at[k - 1],
                device_id=(px, py, pc), device_id_type=MESH))
            recvs.append(pltpu.make_async_remote_copy(
                src_ref=g_ref.at[p], dst_ref=out_ref.at[p], send_sem=send_sems.at[k - 1], recv_sem=recv_sems.at[k - 1],
                device_id=(px, py, pc), device_id_type=MESH))
        for cp in sends:
            cp.start()
        for cp in recvs:
            cp.wait_recv()
        for cp in sends:
            cp.wait_send()
        mine.wait()

    return pl.pallas_call(
        body, name=name, in_specs=[ANY], out_specs=ANY,
        out_shape=jax.ShapeDtypeStruct(g.shape, g.dtype),
        scratch_shapes=[pltpu.SemaphoreType.DMA((7,)), pltpu.SemaphoreType.DMA((7,)), pltpu.SemaphoreType.DMA],
    )(g)


def _adamw(parts, w, m, v, *, name, T=512):
    R, L = w.shape
    c1 = 1.0 / (1.0 - ADAM_B1 ** ADAM_STEP)
    c2 = 1.0 / (1.0 - ADAM_B2 ** ADAM_STEP)

    def body(p_ref, w_ref, m_ref, v_ref, g_ref, d_ref, nm_ref, nv_ref):
        g = p_ref[0]
        for s in range(1, N_DEV):
            g = g + p_ref[s]
        g_ref[...] = g
        nm = ADAM_B1 * m_ref[...] + (1.0 - ADAM_B1) * g
        nv = ADAM_B2 * v_ref[...] + (1.0 - ADAM_B2) * (g * g)
        nm_ref[...] = nm
        nv_ref[...] = nv
        d_ref[...] = -ADAM_LR * ((nm * c1) / (jnp.sqrt(nv * c2) + ADAM_EPS) + ADAM_WD * w_ref[...])

    blk = pl.BlockSpec((T, L), lambda i: (i, 0))
    out = jax.ShapeDtypeStruct((R, L), F32)
    return pl.pallas_call(
        body, name=name, grid=(R // T,),
        in_specs=[pl.BlockSpec((N_DEV, T, L), lambda i: (0, i, 0)), blk, blk, blk],
        out_specs=[blk, blk, blk, blk], out_shape=[out, out, out, out],
        compiler_params=_cparams(("parallel",)),
    )(parts, w, m, v)


D_IN = 9232
FF_LO, FF_HI = 7168, 7184
IN_SH, UP_SH, DOWN_SH = D_IN // N_DEV, 2 * D_FF // N_DEV, D_FF // N_DEV
SQ_SH = D_MODEL // N_DEV

BIG = [("w_in", (1, D_MODEL, IN_SH)), ("w_branch_a", (1, SQ_SH, D_MODEL)), ("w_branch_b", (1, SQ_SH, D_MODEL)),
       ("w_out", (1, SQ_SH, D_MODEL)), ("w_up", (1, D_MODEL, UP_SH)), ("conv_w", (1, 3, UP_SH)),
       ("w_down", (1, DOWN_SH, D_MODEL))]
SMALL = [("norm_mix", (1, D_MODEL)), ("fox_f_bias", (1, FOX_HEADS)), ("hg_lb_logits", (2, HG_HEADS * HG_DK)),
         ("hg_norm", (1, HG_DV)), ("norm_ffn", (1, D_MODEL)), ("conv_b", (1, 2 * D_FF)), ("norm_final", (D_MODEL,))]
NAMES = ["norm_mix", "w_in", "fox_f_bias", "hg_lb_logits", "hg_norm", "w_branch_a", "w_branch_b", "w_out",
         "norm_ffn", "w_up", "conv_w", "conv_b", "w_down", "norm_final"]


def _size(shape):
    n = 1
    for s in shape:
        n *= s
    return n


PACK_ROWS = 20992
GATHER_ROWS = 20800
assert sum(_size(s) for _, s in BIG + SMALL) <= PACK_ROWS * 128


def _pack_rows(flat_parts, rows):
    flat = jnp.concatenate(flat_parts, axis=-1)
    pad = rows * 128 - flat.shape[-1]
    flat = jnp.pad(flat, [(0, 0)] * (flat.ndim - 1) + [(0, pad)])
    return flat.reshape(flat.shape[:-1] + (rows, 128))


def _pack_shard(vals):
    return _pack_rows([vals[n].reshape(1, -1).astype(F32) for n, _ in BIG + SMALL], PACK_ROWS)[0]


def _unpack_shard(buf):
    flat = buf.reshape(-1)
    out, off = {}, 0
    for n, shape in BIG + SMALL:
        out[n] = flat[off:off + _size(shape)].reshape(shape)
        off += _size(shape)
    return out


def _cols_by_device(a, width):
    rows = a.shape[0]
    return a.reshape(rows, N_DEV, width).transpose(1, 0, 2).reshape(N_DEV, rows * width)


def _cols_from_devices(a, rows, width):
    return a.reshape(N_DEV, rows, width).transpose(1, 0, 2).reshape(rows, N_DEV * width)


def _pack_grads(g):
    w_in = jnp.concatenate([g["wm"][:, :FF_LO], g["wff"][:, :FOX_HEADS], g["wm"][:, FF_LO:]], axis=1)
    w_up = jnp.concatenate([g["wug"], g["wuv"]], axis=1)
    conv_w = jnp.concatenate([g["cwg"], g["cwv"]], axis=1)
    conv_b = jnp.concatenate([g["cbg"], g["cbv"]], axis=1)
    big = [_cols_by_device(w_in, IN_SH), g["wa"].reshape(N_DEV, -1), g["wb"].reshape(N_DEV, -1),
           g["wo"].reshape(N_DEV, -1), _cols_by_device(w_up, UP_SH), _cols_by_device(conv_w, UP_SH),
           g["wd"].reshape(N_DEV, -1)]
    small = [g["norm_mix"], g["fox_f_bias"], g["hg_lb_logits"], g["hg_norm"], g["norm_ffn"], conv_b, g["norm_final"]]
    small = [jnp.broadcast_to(s.reshape(1, -1), (N_DEV, s.size)) for s in small]
    return _pack_rows(big + small, PACK_ROWS)


def _gather_weights(w_in, w_a, w_b, w_o, w_up, conv_w, w_down):
    taps = lax.bitcast_convert_type(conv_w.reshape(3, UP_SH), BF16).reshape(1, -1)
    mats = [w_in, w_a, w_b, w_o, w_up, w_down]
    packed = _pack_rows([t.reshape(1, -1).astype(BF16) for t in mats] + [taps], GATHER_ROWS)[0]
    full = _all_gather(packed, name="gather_weights").reshape(N_DEV, -1)
    off = 0

    def take(n):
        nonlocal off
        piece = full[:, off:off + n]
        off += n
        return piece

    win = _cols_from_devices(take(D_MODEL * IN_SH), D_MODEL, IN_SH)
    wa = take(SQ_SH * D_MODEL).reshape(D_MODEL, D_MODEL)
    wb = take(SQ_SH * D_MODEL).reshape(D_MODEL, D_MODEL)
    wo = take(SQ_SH * D_MODEL).reshape(D_MODEL, D_MODEL)
    wup = _cols_from_devices(take(D_MODEL * UP_SH), D_MODEL, UP_SH)
    wd = take(DOWN_SH * D_MODEL).reshape(D_FF, D_MODEL)
    cw = lax.bitcast_convert_type(take(3 * UP_SH * 2).reshape(N_DEV, 3, UP_SH, 2), F32)
    cw = cw.transpose(1, 0, 2).reshape(3, 2 * D_FF)
    return dict(
        wm=jnp.concatenate([win[:, :FF_LO], win[:, FF_HI:]], axis=1),
        wff=jnp.pad(win[:, FF_LO:FF_HI], ((0, 0), (0, 128 - FOX_HEADS))),
        wa=wa, wb=wb, wo=wo, wug=wup[:, :D_FF], wuv=wup[:, D_FF:], cwg=cw[:, :D_FF], cwv=cw[:, D_FF:], wd=wd)


def _peer(k, x, y, c):
    return (1 - x if k & 4 else x, 1 - y if k & 2 else y, 1 - c if k & 1 else c)


def _gather_multi(shards, *, name):
    n = len(shards)

    def body(*refs):
        x_refs, out_refs = refs[:n], refs[n:2 * n]
        send_sems, recv_sems, local_sems = refs[2 * n:]
        x, y, c = lax.axis_index("x"), lax.axis_index("y"), lax.axis_index("c")
        me, sibling = (x, y, c), (x, y, 1 - c)
        chips = [(1 - x, y), (x, 1 - y), (1 - x, 1 - y)]

        def copy(t, k, block, to, src=None):
            slot = out_refs[t].at[4 * block[0] + 2 * block[1] + block[2]]
            return pltpu.make_async_remote_copy(
                src_ref=slot if src is None else src, dst_ref=slot,
                send_sem=send_sems.at[t, k], recv_sem=recv_sems.at[t, k], device_id=to, device_id_type=MESH)

        mine = [pltpu.make_async_copy(x_refs[t], out_refs[t].at[4 * x + 2 * y + c], local_sems.at[t]) for t in range(n)]
        for cp in mine:
            cp.start()
        first = []
        for t in range(n):
            first.append(copy(t, 0, me, sibling, src=x_refs[t]))
            first += [copy(t, 1 + j, me, (*chip, c), src=x_refs[t]) for j, chip in enumerate(chips)]
        for cp in first:
            cp.start()
        passed = []
        for j, chip in enumerate(chips):
            for t in range(n):
                copy(t, 1 + j, (*chip, c), me).wait_recv()
                passed.append(copy(t, 4 + j, (*chip, c), sibling))
                passed[-1].start()
        for t in range(n):
            copy(t, 0, sibling, me).wait_recv()
            for j, chip in enumerate(chips):
                copy(t, 4 + j, (*chip, 1 - c), me).wait_recv()
        for cp in first + passed:
            cp.wait_send()
        for cp in mine:
            cp.wait()

    return pl.pallas_call(
        body, name=name, in_specs=[ANY] * n, out_specs=[ANY] * n,
        out_shape=[jax.ShapeDtypeStruct((N_DEV,) + s.shape, s.dtype) for s in shards],
        scratch_shapes=[pltpu.SemaphoreType.DMA((n, 7)), pltpu.SemaphoreType.DMA((n, 7)), pltpu.SemaphoreType.DMA((n,))],
    )(*shards)


def _exchange_multi(blocks, *, name):
    n = len(blocks)

    def body(*refs):
        g_refs, out_refs = refs[:n], refs[n:2 * n]
        send_sems, recv_sems, local_sems = refs[2 * n:]
        x, y, c = lax.axis_index("x"), lax.axis_index("y"), lax.axis_index("c")
        me = 4 * x + 2 * y + c
        mine = [pltpu.make_async_copy(g_refs[t].at[me], out_refs[t].at[me], local_sems.at[t]) for t in range(n)]
        for cp in mine:
            cp.start()
        sends, recvs = [], []
        for k in range(1, N_DEV):
            px, py, pc = _peer(k, x, y, c)
            p = 4 * px + 2 * py + pc
            for t in range(n):
                sends.append(pltpu.make_async_remote_copy(
                    src_ref=g_refs[t].at[p], dst_ref=out_refs[t].at[me], send_sem=send_sems.at[t, k - 1],
                    recv_sem=recv_sems.at[t, k - 1], device_id=(px, py, pc), device_id_type=MESH))
                recvs.append(pltpu.make_async_remote_copy(
                    src_ref=g_refs[t].at[p], dst_ref=out_refs[t].at[p], send_sem=send_sems.at[t, k - 1],
                    recv_sem=recv_sems.at[t, k - 1], device_id=(px, py, pc), device_id_type=MESH))
        for cp in sends:
            cp.start()
        for cp in recvs:
            cp.wait_recv()
        for cp in sends:
            cp.wait_send()
        for cp in mine:
            cp.wait()

    return pl.pallas_call(
        body, name=name, in_specs=[ANY] * n, out_specs=[ANY] * n,
        out_shape=[jax.ShapeDtypeStruct(b.shape, b.dtype) for b in blocks],
        scratch_shapes=[pltpu.SemaphoreType.DMA((n, 7)), pltpu.SemaphoreType.DMA((n, 7)), pltpu.SemaphoreType.DMA((n,))],
    )(*blocks)


def _adamw2(parts, w, m, v, *, name, T):
    R, C = w.shape
    c1 = 1.0 / (1.0 - ADAM_B1 ** ADAM_STEP)
    c2 = 1.0 / (1.0 - ADAM_B2 ** ADAM_STEP)

    def body(p_ref, w_ref, m_ref, v_ref, g_ref, d_ref, nm_ref, nv_ref):
        g = p_ref[0].astype(F32)
        for s in range(1, N_DEV):
            g = g + p_ref[s].astype(F32)
        g_ref[...] = g
        nm = ADAM_B1 * m_ref[...] + (1.0 - ADAM_B1) * g
        nv = ADAM_B2 * v_ref[...] + (1.0 - ADAM_B2) * (g * g)
        nm_ref[...] = nm
        nv_ref[...] = nv
        d_ref[...] = -ADAM_LR * ((nm * c1) / (jnp.sqrt(nv * c2) + ADAM_EPS) + ADAM_WD * w_ref[...])

    blk = pl.BlockSpec((T, C), lambda i: (i, 0))
    out = jax.ShapeDtypeStruct((R, C), F32)
    return pl.pallas_call(
        body, name=name, grid=(R // T,),
        in_specs=[pl.BlockSpec((N_DEV, T, C), lambda i: (0, i, 0)), blk, blk, blk],
        out_specs=[blk, blk, blk, blk], out_shape=[out, out, out, out],
        compiler_params=_cparams(("parallel",)),
    )(parts, w, m, v)


SMALL_ROWS = 88
SHARDED = [("w_in", (D_MODEL, 1154), 256), ("w_branch_a", (128, D_MODEL), 128), ("w_branch_b", (128, D_MODEL), 128),
           ("w_out", (128, D_MODEL), 128), ("w_up", (D_MODEL, 704), 256), ("conv_w", (3, 704), 3),
           ("w_down", (352, D_MODEL), 352)]


def _col_blocks(a, width):
    return jnp.stack([a[:, d * width:(d + 1) * width] for d in range(N_DEV)])


def _pack_small(vals):
    flat = jnp.concatenate([vals[n].reshape(-1).astype(F32) for n, _ in SMALL])
    return jnp.pad(flat, (0, SMALL_ROWS * 128 - flat.shape[0])).reshape(SMALL_ROWS, 128)


def _unpack_small(buf):
    flat, out, off = buf.reshape(-1), {}, 0
    for n, shape in SMALL:
        out[n] = flat[off:off + _size(shape)].reshape(shape)
        off += _size(shape)
    return out


def _gather_weights2(w_in, w_a, w_b, w_o, w_up, conv_w, w_down):
    shards = [w_in[0].astype(BF16), w_a[0].astype(BF16), w_b[0].astype(BF16), w_o[0].astype(BF16),
              w_up[0].astype(BF16), conv_w[0], w_down[0].astype(BF16)]
    g_in, g_a, g_b, g_o, g_up, g_cw, g_d = _gather_multi(shards, name="gather_weights")
    win = jnp.concatenate([g_in[d] for d in range(N_DEV)], axis=1)
    wup = jnp.concatenate([g_up[d] for d in range(N_DEV)], axis=1)
    cw = jnp.concatenate([g_cw[d] for d in range(N_DEV)], axis=1)
    return dict(
        wm=jnp.concatenate([win[:, :FF_LO], win[:, FF_HI:]], axis=1),
        wff=jnp.pad(win[:, FF_LO:FF_HI], ((0, 0), (0, 128 - FOX_HEADS))),
        wa=g_a.reshape(D_MODEL, D_MODEL), wb=g_b.reshape(D_MODEL, D_MODEL), wo=g_o.reshape(D_MODEL, D_MODEL),
        wug=wup[:, :D_FF], wuv=wup[:, D_FF:], cwg=cw[:, :D_FF], cwv=cw[:, D_FF:], wd=g_d.reshape(D_FF, D_MODEL))


def _grad_blocks(g):
    w_in = jnp.concatenate([g["wm"][:, :FF_LO], g["wff"][:, :FOX_HEADS], g["wm"][:, FF_LO:]], axis=1)
    conv_w = jnp.concatenate([g["cwg"], g["cwv"]], axis=1).astype(F32)
    conv_b = jnp.concatenate([g["cbg"], g["cbv"]], axis=1)
    small = _pack_small(dict(norm_mix=g["norm_mix"], fox_f_bias=g["fox_f_bias"], hg_lb_logits=g["hg_lb_logits"],
                             hg_norm=g["hg_norm"], norm_ffn=g["norm_ffn"], conv_b=conv_b, norm_final=g["norm_final"]))
    up = jnp.stack([g["wug"][:, d * 704:(d + 1) * 704] for d in range(4)]
                   + [g["wuv"][:, d * 704:(d + 1) * 704] for d in range(4)])
    return [_col_blocks(w_in, 1154), g["wa"].reshape(N_DEV, 128, D_MODEL), g["wb"].reshape(N_DEV, 128, D_MODEL),
            g["wo"].reshape(N_DEV, 128, D_MODEL), up, _col_blocks(conv_w, 704), g["wd"].reshape(N_DEV, 352, D_MODEL),
            jnp.broadcast_to(small[None], (N_DEV, SMALL_ROWS, 128))]


def kernel(x, norm_mix, w_in,fox_f_bias, hg_lb_logits, hg_norm, w_branch_a, w_branch_b, w_out, norm_ffn, w_up, conv_w, conv_b, w_down, norm_final, loss_target, m_norm_mix, m_w_in, m_fox_f_bias, m_hg_lb_logits, m_hg_norm, m_w_branch_a, m_w_branch_b, m_w_out, m_norm_ffn, m_w_up, m_conv_w, m_conv_b, m_w_down, m_norm_final, v_norm_mix, v_w_in, v_fox_f_bias, v_hg_lb_logits, v_hg_norm, v_w_branch_a, v_w_branch_b, v_w_out, v_norm_ffn, v_w_up, v_conv_w, v_conv_b, v_w_down, v_norm_final):
    wv = dict(norm_mix=norm_mix, w_in=w_in, fox_f_bias=fox_f_bias, hg_lb_logits=hg_lb_logits, hg_norm=hg_norm,
              w_branch_a=w_branch_a, w_branch_b=w_branch_b, w_out=w_out, norm_ffn=norm_ffn, w_up=w_up, conv_w=conv_w,
              conv_b=conv_b, w_down=w_down, norm_final=norm_final)
    mv = dict(norm_mix=m_norm_mix, w_in=m_w_in, fox_f_bias=m_fox_f_bias, hg_lb_logits=m_hg_lb_logits, hg_norm=m_hg_norm,
              w_branch_a=m_w_branch_a, w_branch_b=m_w_branch_b, w_out=m_w_out, norm_ffn=m_norm_ffn, w_up=m_w_up,
              conv_w=m_conv_w, conv_b=m_conv_b, w_down=m_w_down, norm_final=m_norm_final)
    vv = dict(norm_mix=v_norm_mix, w_in=v_w_in, fox_f_bias=v_fox_f_bias, hg_lb_logits=v_hg_lb_logits, hg_norm=v_hg_norm,
              w_branch_a=v_w_branch_a, w_branch_b=v_w_branch_b, w_out=v_w_out, norm_ffn=v_norm_ffn, w_up=v_w_up,
              conv_w=v_conv_w, conv_b=v_conv_b, w_down=v_w_down, norm_final=v_norm_final)

    (g_in,) = _comm_call(_GatherComm([w_in[0].astype(BF16)]), name="gather_w_in")
    win = jnp.concatenate([g_in[d] for d in range(N_DEV)], axis=1)
    w = dict(wm=jnp.concatenate([win[:, :FF_LO], win[:, FF_HI:]], axis=1),
             wff=jnp.pad(win[:, FF_LO:FF_HI], ((0, 0), (0, 128 - FOX_HEADS))))
    late = _GatherComm([w_branch_a[0].astype(BF16), w_branch_b[0].astype(BF16), w_out[0].astype(BF16),
                        w_up[0].astype(BF16), conv_w[0], w_down[0].astype(BF16)])
    p = dict(norm_mix=norm_mix[0], fox_f_bias=fox_f_bias[0], hg_lb_logits=hg_lb_logits, hg_norm=hg_norm[0],
             norm_ffn=norm_ffn[0], cbg=conv_b[:, :D_FF], cbv=conv_b[:, D_FF:], norm_final=norm_final)
    loss, dx, grads = _local_step(x[0], loss_target[0], w, p, late=late, exchange_early=True)
    loss = lax.psum(loss[0, 0], ("x", "y", "c"))

    blocks = _grad_blocks(grads)
    last_parts = _comm_call(_ExchangeComm([blocks[0], blocks[5], blocks[7]]), name="exchange_grads")
    ea, eb, eo, eup, ed = grads["early_parts"]
    parts = [last_parts[0], ea, eb, eo, eup, last_parts[1], ed, last_parts[2]]
    res = {}
    for (n, shape, tile), part in zip(SHARDED, parts):
        outs = _adamw2(part, wv[n].reshape(shape), mv[n].reshape(shape), vv[n].reshape(shape), name="adamw_" + n, T=tile)
        res[n] = [o.reshape(wv[n].shape) for o in outs]
    outs = _adamw2(parts[-1], _pack_small(wv), _pack_small(mv), _pack_small(vv), name="adamw_small", T=SMALL_ROWS)
    small = [_unpack_small(o) for o in outs]
    for n, _ in SMALL:
        res[n] = [s[n] for s in small]
    return (loss, dx[None], *[res[n][0] for n in NAMES], *[res[n][1] for n in NAMES],
            *[res[n][2] for n in NAMES], *[res[n][3] for n in NAMES])


def _lb_fwd(logits, *, name):
    def body(l_ref, lb_ref):
        lb_ref[...] = _sigmoid(l_ref[0:1, :] - l_ref[1:2, :])

    return pl.pallas_call(body, name=name, out_shape=jax.ShapeDtypeStruct((1, logits.shape[1]), F32))(logits)


def _lb_bwd(logits, dlb, *, name):
    def body(l_ref, d_ref, o_ref):
        lbv = _sigmoid(l_ref[0:1, :] - l_ref[1:2, :])
        t = d_ref[...] * lbv * (1.0 - lbv)
        o_ref[0:1, :] = t
        o_ref[1:2, :] = -t

    return pl.pallas_call(body, name=name, out_shape=jax.ShapeDtypeStruct(logits.shape, F32))(logits, dlb)
```

```python
import functools

import numpy as np
import jax
import jax.numpy as jnp
from jax import lax
from jax.experimental import pallas as pl
from jax.experimental.pallas import tpu as pltpu

F32 = jnp.float32
BF16 = jnp.bfloat16

D_MODEL = 1024
HG_HEADS = 8
HG_DK = 128
HG_DV = 128
HG_CHUNK = 64
FOX_HEADS = 16
FOX_DH = 64
D_FF = 2816
EPS = 1e-6
N_DEV = 8

ADAM_LR = 0.001
ADAM_B1 = 0.9
ADAM_B2 = 0.999
ADAM_EPS = 1e-08
ADAM_WD = 0.01
ADAM_STEP = 10

VMEM_LIMIT = 56 * 1024 * 1024


def _cparams(sem):
    return pltpu.CompilerParams(dimension_semantics=sem, vmem_limit_bytes=VMEM_LIMIT)


_DIMS = {
    "nn": (((1,), (0,)), ((), ())),
    "nt": (((1,), (1,)), ((), ())),
    "tn": (((0,), (0,)), ((), ())),
}


def _pick(n, prefs):
    for p in prefs:
        if n % p == 0:
            return p
    return n


MESH = pl.DeviceIdType.MESH
ANY = pl.BlockSpec(memory_space=pl.ANY)


class _GatherComm:
    def __init__(self, shards):
        self.inputs = list(shards)
        n = self.n = len(shards)
        self.out_shapes = [jax.ShapeDtypeStruct((N_DEV,) + s.shape, s.dtype) for s in shards]
        self.scratch = [pltpu.SemaphoreType.DMA((n, 7)), pltpu.SemaphoreType.DMA((n, 7)), pltpu.SemaphoreType.DMA((n,))]

    def _parts(self, x_refs, out_refs, sems):
        send_sems, recv_sems, local_sems = sems
        x, y, c = lax.axis_index("x"), lax.axis_index("y"), lax.axis_index("c")
        me, sibling = (x, y, c), (x, y, 1 - c)
        chips = [(1 - x, y), (x, 1 - y), (1 - x, 1 - y)]

        def copy(t, k, block, to, src=None):
            slot = out_refs[t].at[4 * block[0] + 2 * block[1] + block[2]]
            return pltpu.make_async_remote_copy(
                src_ref=slot if src is None else src, dst_ref=slot,
                send_sem=send_sems.at[t, k], recv_sem=recv_sems.at[t, k], device_id=to, device_id_type=MESH)

        mine = [pltpu.make_async_copy(x_refs[t], out_refs[t].at[4 * x + 2 * y + c], local_sems.at[t])
                for t in range(self.n)]
        first = []
        for t in range(self.n):
            first.append(copy(t, 0, me, sibling, src=x_refs[t]))
            first += [copy(t, 1 + j, me, (*chip, c), src=x_refs[t]) for j, chip in enumerate(chips)]
        return c, me, sibling, chips, copy, mine, first

    def start(self, x_refs, out_refs, sems):
        _, _, _, _, _, mine, first = self._parts(x_refs, out_refs, sems)
        for cp in mine + first:
            cp.start()

    def finish(self, x_refs, out_refs, sems):
        c, me, sibling, chips, copy, mine, first = self._parts(x_refs, out_refs, sems)
        passed = []
        for j, chip in enumerate(chips):
            for t in range(self.n):
                copy(t, 1 + j, (*chip, c), me).wait_recv()
                passed.append(copy(t, 4 + j, (*chip, c), sibling))
                passed[-1].start()
        for t in range(self.n):
            copy(t, 0, sibling, me).wait_recv()
            for j, chip in enumerate(chips):
                copy(t, 4 + j, (*chip, 1 - c), me).wait_recv()
        for cp in first + passed:
            cp.wait_send()
        for cp in mine:
            cp.wait()


class _ExchangeComm:
    def __init__(self, blocks):
        self.inputs = list(blocks)
        n = self.n = len(blocks)
        self.out_shapes = [jax.ShapeDtypeStruct(b.shape, b.dtype) for b in blocks]
        self.scratch = [pltpu.SemaphoreType.DMA((n, 7)), pltpu.SemaphoreType.DMA((n, 7)), pltpu.SemaphoreType.DMA((n,))]

    def _parts(self, g_refs, out_refs, sems):
        send_sems, recv_sems, local_sems = sems
        x, y, c = lax.axis_index("x"), lax.axis_index("y"), lax.axis_index("c")
        me = 4 * x + 2 * y + c
        mine = [pltpu.make_async_copy(g_refs[t].at[me], out_refs[t].at[me], local_sems.at[t]) for t in range(self.n)]
        sends, recvs = [], []
        for k in range(1, N_DEV):
            px = 1 - x if k & 4 else x
            py = 1 - y if k & 2 else y
            pc = 1 - c if k & 1 else c
            p = 4 * px + 2 * py + pc
            for t in range(self.n):
                sends.append(pltpu.make_async_remote_copy(
                    src_ref=g_refs[t].at[p], dst_ref=out_refs[t].at[me], send_sem=send_sems.at[t, k - 1],
                    recv_sem=recv_sems.at[t, k - 1], device_id=(px, py, pc), device_id_type=MESH))
                recvs.append(pltpu.make_async_remote_copy(
                    src_ref=g_refs[t].at[p], dst_ref=out_refs[t].at[p], send_sem=send_sems.at[t, k - 1],
                    recv_sem=recv_sems.at[t, k - 1], device_id=(px, py, pc), device_id_type=MESH))
        return mine, sends, recvs

    def start(self, g_refs, out_refs, sems):
        mine, sends, _ = self._parts(g_refs, out_refs, sems)
        for cp in mine + sends:
            cp.start()

    def finish(self, g_refs, out_refs, sems):
        mine, sends, recvs = self._parts(g_refs, out_refs, sems)
        for cp in recvs:
            cp.wait_recv()
        for cp in sends:
            cp.wait_send()
        for cp in mine:
            cp.wait()


def _comm_call(comm, *, name):
    n = comm.n

    def body(*refs):
        comm.start(refs[:n], refs[n:2 * n], refs[2 * n:])
        comm.finish(refs[:n], refs[n:2 * n], refs[2 * n:])

    return pl.pallas_call(body, name=name, in_specs=[ANY] * n, out_specs=[ANY] * n, out_shape=comm.out_shapes,
                          scratch_shapes=comm.scratch)(*comm.inputs)


MATMUL_VMEM_BUDGET = 36 * 1024 * 1024
MAX_TILE = 1536


def _tile_options(n):
    return [d for d in range(128, min(n, MAX_TILE) + 1, 128) if n % d == 0] or [n]


def _pick_tiles(M, N, tk, nk, sa, sb, so, has_addend, tm, tn):
    best = None
    for cm in ([tm] if tm else _tile_options(M)):
        for cn in ([tn] if tn else _tile_options(N)):
            need = 2 * (cm * tk * sa + tk * cn * sb + cm * cn * so + (cm * cn * 4 if has_addend else 0))
            need += cm * cn * 4 if nk > 1 else 0
            if need <= MATMUL_VMEM_BUDGET and (best is None or cm * cn > best[0] * best[1]
                                               or (cm * cn == best[0] * best[1] and cn > best[1])):
                best = (cm, cn)
    assert best is not None, (M, N, tk)
    return best


def _matmul(a, b, form, *, out_dtype=F32, addend=None, tm=None, tn=None, tk=None, comm=None, name):
    if form == "nn":
        (M, K), (K2, N) = a.shape, b.shape
    elif form == "nt":
        (M, K), (N, K2) = a.shape, b.shape
    else:
        (K, M), (K2, N) = a.shape, b.shape
    assert K == K2, (a.shape, b.shape, form)
    tk = tk or (K if K <= 2816 else _pick(K, (1024, 512, 256, 128)))
    nk = K // tk
    if tm is None or tn is None:
        tm, tn = _pick_tiles(M, N, tk, nk, a.dtype.itemsize, b.dtype.itemsize, jnp.dtype(out_dtype).itemsize,
                             addend is not None, tm, tn)
    assert M % tm == 0 and N % tn == 0 and K % tk == 0, (M, N, K, tm, tn, tk)
    dims = _DIMS[form]

    nc = comm.n if comm is not None else 0
    grid = (M // tm, N // tn, nk)

    def body(*refs):
        a_ref, b_ref = refs[:2]
        pos = 2
        add_ref = refs[pos] if addend is not None else None
        pos += addend is not None
        c_in, o_ref, c_out = refs[pos:pos + nc], refs[pos + nc], refs[pos + nc + 1:pos + 2 * nc + 1]
        pos += 2 * nc + 1
        acc_ref = refs[pos] if nk > 1 else None
        c_sems = refs[pos + (nk > 1):]
        if comm is not None:
            ids = [pl.program_id(d) for d in range(3)]

            @pl.when((ids[0] == 0) & (ids[1] == 0) & (ids[2] == 0))
            def _():
                comm.start(c_in, c_out, c_sems)

        def finish(r):
            if add_ref is not None:
                r = r + add_ref[...].astype(F32)
            o_ref[...] = r.astype(o_ref.dtype)

        part = lax.dot_general(a_ref[...].astype(BF16), b_ref[...].astype(BF16), dims, preferred_element_type=F32)
        if nk == 1:
            finish(part)
        else:
            k = pl.program_id(2)

            @pl.when(k == 0)
            def _():
                acc_ref[...] = part

            @pl.when(k > 0)
            def _():
                acc_ref[...] += part

            @pl.when(k == nk - 1)
            def _():
                finish(acc_ref[...])

        if comm is not None:
            @pl.when((ids[0] == grid[0] - 1) & (ids[1] == grid[1] - 1) & (ids[2] == grid[2] - 1))
            def _():
                comm.finish(c_in, c_out, c_sems)

    if form == "nn":
        a_spec = pl.BlockSpec((tm, tk), lambda i, j, k: (i, k))
        b_spec = pl.BlockSpec((tk, tn), lambda i, j, k: (k, j))
    elif form == "nt":
        a_spec = pl.BlockSpec((tm, tk), lambda i, j, k: (i, k))
        b_spec = pl.BlockSpec((tn, tk), lambda i, j, k: (j, k))
    else:
        a_spec = pl.BlockSpec((tk, tm), lambda i, j, k: (k, i))
        b_spec = pl.BlockSpec((tk, tn), lambda i, j, k: (k, j))
    o_spec = pl.BlockSpec((tm, tn), lambda i, j, k: (i, j))
    in_specs = [a_spec, b_spec] + ([o_spec] if addend is not None else [])
    args = (a, b) + ((addend,) if addend is not None else ())
    out_shape = jax.ShapeDtypeStruct((M, N), out_dtype)
    scratch = [pltpu.VMEM((tm, tn), F32)] if nk > 1 else []
    if comm is None:
        return pl.pallas_call(
            body, name=name, grid=grid, in_specs=in_specs, out_specs=o_spec, out_shape=out_shape,
            scratch_shapes=scratch, compiler_params=_cparams(("parallel", "parallel", "arbitrary")),
        )(*args)
    outs = pl.pallas_call(
        body, name=name, grid=grid, in_specs=in_specs + [ANY] * nc, out_specs=[o_spec] + [ANY] * nc,
        out_shape=[out_shape] + comm.out_shapes, scratch_shapes=scratch + comm.scratch,
        compiler_params=_cparams(("arbitrary", "arbitrary", "arbitrary")),
    )(*args, *comm.inputs)
    return outs[0], outs[1:]


def _rms_fwd(x, g, *, name, tm=512):
    M, D = x.shape
    tm = min(tm, M)

    def body(x_ref, g_ref, n_ref):
        xf = x_ref[...]
        r = lax.rsqrt(jnp.mean(xf * xf, axis=-1, keepdims=True) + EPS)
        n_ref[...] = (xf * r * g_ref[...]).astype(n_ref.dtype)

    return pl.pallas_call(
        body, name=name, grid=(M // tm,),
        in_specs=[pl.BlockSpec((tm, D), lambda i: (i, 0)), pl.BlockSpec((1, D), lambda i: (0, 0))],
        out_specs=pl.BlockSpec((tm, D), lambda i: (i, 0)),
        out_shape=jax.ShapeDtypeStruct((M, D), BF16),
        compiler_params=_cparams(("parallel",)),
    )(x, g.reshape(1, D))


def _rms_bwd(x, g, dn, dres, *, name, tm=512):
    M, D = x.shape
    tm = min(tm, M)

    def body(x_ref, g_ref, dn_ref, dres_ref, dx_ref, dg_ref):
        @pl.when(pl.program_id(0) == 0)
        def _():
            dg_ref[...] = jnp.zeros_like(dg_ref)

        xf = x_ref[...]
        r = lax.rsqrt(jnp.mean(xf * xf, axis=-1, keepdims=True) + EPS)
        xh = xf * r
        dn_ = dn_ref[...].astype(F32)
        dg_ref[...] += jnp.sum(dn_ * xh, axis=0, keepdims=True)
        dxh = dn_ * g_ref[...]
        dx = r * (dxh - xh * jnp.mean(dxh * xh, axis=-1, keepdims=True))
        dx_ref[...] = dres_ref[...] + dx

    row = pl.BlockSpec((tm, D), lambda i: (i, 0))
    vec = pl.BlockSpec((1, D), lambda i: (0, 0))
    return pl.pallas_call(
        body, name=name, grid=(M // tm,),
        in_specs=[row, vec, row, row], out_specs=[row, vec],
        out_shape=[jax.ShapeDtypeStruct((M, D), F32), jax.ShapeDtypeStruct((1, D), F32)],
        compiler_params=_cparams(("arbitrary",)),
    )(x, g.reshape(1, D), dn, dres)


def _loss_head(h, g, tgt, *, name, tm=512):
    M, D = h.shape
    tm = min(tm, M)

    def body(h_ref, g_ref, t_ref, loss_ref, dh_ref, dg_ref):
        @pl.when(pl.program_id(0) == 0)
        def _():
            dg_ref[...] = jnp.zeros_like(dg_ref)
            loss_ref[...] = jnp.zeros_like(loss_ref)

        xf = h_ref[...]
        r = lax.rsqrt(jnp.mean(xf * xf, axis=-1, keepdims=True) + EPS)
        xh = xf * r
        err = xh * g_ref[...] - t_ref[...]
        part = jnp.sum(jnp.mean(err * err, axis=-1, keepdims=True), axis=0, keepdims=True)
        loss_ref[...] += 0.5 * part
        dy = err * (1.0 / D)
        dg_ref[...] += jnp.sum(dy * xh, axis=0, keepdims=True)
        dxh = dy * g_ref[...]
        dh_ref[...] = r * (dxh - xh * jnp.mean(dxh * xh, axis=-1, keepdims=True))

    row = pl.BlockSpec((tm, D), lambda i: (i, 0))
    vec = pl.BlockSpec((1, D), lambda i: (0, 0))
    one = pl.BlockSpec((1, 1), lambda i: (0, 0))
    return pl.pallas_call(
        body, name=name, grid=(M // tm,),
        in_specs=[row, vec, row], out_specs=[one, row, vec],
        out_shape=[jax.ShapeDtypeStruct((1, 1), F32), jax.ShapeDtypeStruct((M, D), F32),
                   jax.ShapeDtypeStruct((1, D), F32)],
        compiler_params=_cparams(("arbitrary",)),
    )(h, g.reshape(1, D), tgt)


HG_MID = HG_CHUNK // 2 - 1
EXP_CAP = 80.0


def _sigmoid(x):
    return 1.0 / (1.0 + jnp.exp(-x))


def _dot(a, b, dims, precision=None):
    return lax.dot_general(a, b, dims, preferred_element_type=F32, precision=precision)


def _bdot(a, b, form):
    return _dot(a.astype(BF16), b.astype(BF16), _DIMS[form])


HG_PREC = "highest"


def _hdot(a, b, form):
    if HG_PREC == "x1":
        return _bdot(a, b, form)
    if HG_PREC == "x3":
        ah, bh = a.astype(BF16), b.astype(BF16)
        al, bl = (a - ah.astype(F32)).astype(BF16), (b - bh.astype(F32)).astype(BF16)
        d = _DIMS[form]
        return _dot(ah, bh, d) + (_dot(ah, bl, d) + _dot(al, bh, d))
    return _dot(a, b, _DIMS[form], precision=lax.Precision.HIGHEST)


def _hgrn_chunk_common(hq, hf, lbv, tril, rid):
    sq = _sigmoid(hq)
    q = hq * sq
    sg = _sigmoid(hf)
    f = lbv + (1.0 - lbv) * sg
    k = (1.0 - lbv) * (1.0 - sg)
    g = jnp.log(f)
    b = _dot(tril, g, _DIMS["nn"], precision=lax.Precision.HIGHEST)
    bref = jnp.sum(jnp.where(rid == HG_MID, b, 0.0), axis=0, keepdims=True)
    bend = jnp.sum(jnp.where(rid == HG_CHUNK - 1, b, 0.0), axis=0, keepdims=True)
    eb = jnp.exp(b)
    e1 = jnp.exp(jnp.minimum(b - bref, EXP_CAP))
    e2 = jnp.exp(jnp.minimum(bref - b, EXP_CAP))
    e3 = jnp.exp(bend - b)
    return sq, q, sg, f, k, bend, eb, e1, e2, e3


def _split2(x):
    hi = x.astype(BF16)
    return hi, (x - hi.astype(F32)).astype(BF16)


def _dot3(a, b, form):
    d = _DIMS[form]
    return _dot(a[0], b[0], d) + (_dot(a[0], b[1], d) + _dot(a[1], b[0], d))


def _hgrn_fwd_phased(proj, lb, gnorm, *, name, T=512):
    S = proj.shape[0]
    T = min(T, S)
    nch = T // HG_CHUNK
    C = HG_CHUNK

    def body(hq_ref, hf_ref, hi_ref, hg_ref, lb_ref, gn_ref, o_ref, oa_ref, st_ref, state):
        @pl.when(pl.program_id(1) == 0)
        def _():
            state[...] = jnp.zeros_like(state)

        lbv = lb_ref[...]
        gn = gn_ref[...]
        row = lax.broadcasted_iota(jnp.int32, (C, C), 0)
        col = lax.broadcasted_iota(jnp.int32, (C, C), 1)
        causal = row >= col
        tril = causal.astype(F32)
        rid = lax.broadcasted_iota(jnp.int32, (C, HG_DK), 0)
        sls = [pl.ds(c * C, C) for c in range(nch)]
        pre = [_hgrn_chunk_common(hq_ref[sl, :], hf_ref[sl, :], lbv, tril, rid) for sl in sls]
        v2 = [_split2(hi_ref[sl, :]) for sl in sls]
        a_l, u_l = [], []
        for c in range(nch):
            _, q, _, _, k, _, _, e1, e2, e3 = pre[c]
            a_l.append(jnp.where(causal, _dot3(_split2(q * e1), _split2(k * e2), "nt"), 0.0))
            u_l.append(_dot3(v2[c], _split2(k * e3), "tn"))
        o_l = [_dot3(_split2(a_l[c]), v2[c], "nn") for c in range(nch)]
        st = state[...]
        st_l = []
        for c in range(nch):
            st_l.append(st)
            st = st * jnp.exp(pre[c][5]) + u_l[c]
        state[...] = st
        for c in range(nch):
            st_ref[0, c] = st_l[c]
            o_l[c] = o_l[c] + _dot3(_split2(pre[c][1] * pre[c][6]), _split2(st_l[c]), "nt")
        for c in range(nch):
            o, hg = o_l[c], hg_ref[sls[c], :]
            o_ref[sls[c], :] = o
            r = lax.rsqrt(jnp.mean(o * o, axis=-1, keepdims=True) + EPS)
            oa_ref[sls[c], :] = (o * r * gn * (hg * _sigmoid(hg))).astype(oa_ref.dtype)

    def grp(gidx):
        return pl.BlockSpec((T, 128), lambda h, t: (t, gidx * 8 + h))

    return pl.pallas_call(
        body, name=name, grid=(HG_HEADS, S // T),
        in_specs=[grp(0), grp(1), grp(2), grp(3),
                  pl.BlockSpec((1, 128), lambda h, t: (0, h)), pl.BlockSpec((1, 128), lambda h, t: (0, 0))],
        out_specs=[pl.BlockSpec((T, 128), lambda h, t: (t, h)), pl.BlockSpec((T, 128), lambda h, t: (t, h)),
                   pl.BlockSpec((1, nch, HG_DV, HG_DK), lambda h, t: (h, t, 0, 0))],
        out_shape=[jax.ShapeDtypeStruct((S, HG_HEADS * HG_DV), F32), jax.ShapeDtypeStruct((S, HG_HEADS * HG_DV), BF16),
                   jax.ShapeDtypeStruct((HG_HEADS, S // C, HG_DV, HG_DK), F32)],
        scratch_shapes=[pltpu.VMEM((HG_DV, HG_DK), F32)],
        compiler_params=_cparams(("parallel", "arbitrary")),
    )(proj, proj, proj, proj, lb, gnorm)


def _hgrn_bwd_phased(proj, lb, gnorm, o, states, doa, *, name, T=512):
    S = proj.shape[0]
    T = min(T, S)
    nch = T // HG_CHUNK
    C = HG_CHUNK
    nT = S // T

    def body(hq_ref, hf_ref, hi_ref, hg_ref, lb_ref, gn_ref, o_ref, st_ref, doa_ref,
             dhq_ref, dhf_ref, dhi_ref, dhg_ref, dlb_ref, dgn_ref, dstate):
        @pl.when(pl.program_id(1) == 0)
        def _():
            dstate[...] = jnp.zeros_like(dstate)
            dlb_ref[...] = jnp.zeros_like(dlb_ref)
            dgn_ref[...] = jnp.zeros_like(dgn_ref)

        lbv = lb_ref[...]
        gn = gn_ref[...]
        row = lax.broadcasted_iota(jnp.int32, (C, C), 0)
        col = lax.broadcasted_iota(jnp.int32, (C, C), 1)
        causal = row >= col
        tril = causal.astype(F32)
        triu = (row <= col).astype(F32)
        rid = lax.broadcasted_iota(jnp.int32, (C, HG_DK), 0)
        rng = range(nch)
        sls = [pl.ds(c * C, C) for c in rng]
        pre = [_hgrn_chunk_common(hq_ref[sl, :], hf_ref[sl, :], lbv, tril, rid) for sl in sls]
        do2, dgn_acc = [], jnp.zeros((1, HG_DV), F32)
        for c in rng:
            hg, ov = hg_ref[sls[c], :], o_ref[sls[c], :]
            r = lax.rsqrt(jnp.mean(ov * ov, axis=-1, keepdims=True) + EPS)
            xh = ov * r
            sgg = _sigmoid(hg)
            d_oa = doa_ref[sls[c], :].astype(F32)
            dz = d_oa * (hg * sgg)
            dhg_ref[sls[c], :] = (d_oa * (xh * gn) * (sgg * (1.0 + hg * (1.0 - sgg)))).astype(dhg_ref.dtype)
            dgn_acc = dgn_acc + jnp.sum(dz * xh, axis=0, keepdims=True)
            dxh = dz * gn
            do2.append(_split2(r * (dxh - xh * jnp.mean(dxh * xh, axis=-1, keepdims=True))))
        dgn_ref[0] += dgn_acc
        qi = [pre[c][1] * pre[c][6] for c in rng]
        qp = [pre[c][1] * pre[c][7] for c in rng]
        kp = [pre[c][4] * pre[c][8] for c in rng]
        kend = [pre[c][4] * pre[c][9] for c in rng]
        qi2, qp2, kp2, kend2 = ([_split2(t) for t in lst] for lst in (qi, qp, kp, kend))
        v2 = [_split2(hi_ref[sl, :]) for sl in sls]
        st0 = [st_ref[0, c] for c in rng]
        a2 = [_split2(jnp.where(causal, _dot3(qp2[c], kp2[c], "nt"), 0.0)) for c in rng]
        da2 = [_split2(jnp.where(causal, _dot3(do2[c], v2[c], "nt"), 0.0)) for c in rng]
        dqi = [_dot3(do2[c], _split2(st0[c]), "nn") for c in rng]
        w_l = [_dot3(do2[c], qi2[c], "tn") for c in rng]
        ds = dstate[...]
        ds1 = [None] * nch
        for c in reversed(rng):
            ds1[c] = ds
            ds = ds * jnp.exp(pre[c][5]) + w_l[c]
        dstate[...] = ds
        ds12 = [_split2(t) for t in ds1]
        dqp = [_dot3(da2[c], kp2[c], "nn") for c in rng]
        dkp = [_dot3(da2[c], qp2[c], "tn") for c in rng]
        dv = [_dot3(a2[c], do2[c], "tn") + _dot3(kend2[c], ds12[c], "nt") for c in rng]
        dkend = [_dot3(v2[c], ds12[c], "nn") for c in rng]
        dq_l, dk_l, db_l = [], [], []
        for c in rng:
            _, _, _, _, _, bend, eb, e1, e2, e3 = pre[c]
            dq_l.append(dqi[c] * eb + dqp[c] * e1)
            dk_l.append(dkp[c] * e2 + dkend[c] * e3)
            db = dqi[c] * qi[c] + dqp[c] * qp[c] - dkp[c] * kp[c] - dkend[c] * kend[c]
            dbend = (jnp.sum(dkend[c] * kend[c], axis=0, keepdims=True)
                     + jnp.exp(bend) * jnp.sum(ds1[c] * st0[c], axis=0, keepdims=True))
            db_l.append(db + jnp.where(rid == C - 1, dbend, 0.0))
        dg = [_dot(triu, db_l[c], _DIMS["nn"], precision=lax.Precision.HIGHEST) for c in rng]
        dlb_acc = jnp.zeros((1, HG_DK), F32)
        for c in rng:
            sq, _, sg, f, _, _, _, _, _, _ = pre[c]
            hq = hq_ref[sls[c], :]
            df = dg[c] / f - dk_l[c]
            dlb_acc = dlb_acc + jnp.sum(df * (1.0 - sg), axis=0, keepdims=True)
            dhf_ref[sls[c], :] = (df * (1.0 - lbv) * sg * (1.0 - sg)).astype(dhf_ref.dtype)
            dhq_ref[sls[c], :] = (dq_l[c] * (sq * (1.0 + hq * (1.0 - sq)))).astype(dhq_ref.dtype)
            dhi_ref[sls[c], :] = dv[c].astype(dhi_ref.dtype)
        dlb_ref[...] += dlb_acc

    def grp(gidx):
        return pl.BlockSpec((T, 128), lambda h, t: (nT - 1 - t, gidx * 8 + h))

    tok = pl.BlockSpec((T, 128), lambda h, t: (nT - 1 - t, h))
    big = jax.ShapeDtypeStruct((S, HG_HEADS * HG_DV), BF16)
    return pl.pallas_call(
        body, name=name, grid=(HG_HEADS, nT),
        in_specs=[grp(0), grp(1), grp(2), grp(3),
                  pl.BlockSpec((1, 128), lambda h, t: (0, h)), pl.BlockSpec((1, 128), lambda h, t: (0, 0)),
                  tok, pl.BlockSpec((1, nch, HG_DV, HG_DK), lambda h, t: (h, nT - 1 - t, 0, 0)), tok],
        out_specs=[tok, tok, tok, tok, pl.BlockSpec((1, 128), lambda h, t: (0, h)),
                   pl.BlockSpec((1, 1, 128), lambda h, t: (h, 0, 0))],
        out_shape=[big, big, big, big, jax.ShapeDtypeStruct((1, HG_HEADS * HG_DK), F32),
                   jax.ShapeDtypeStruct((HG_HEADS, 1, HG_DV), F32)],
        scratch_shapes=[pltpu.VMEM((HG_DV, HG_DK), F32)],
        compiler_params=_cparams(("parallel", "arbitrary")),
    )(proj, proj, proj, proj, lb, gnorm, o, states, doa)


def _hgrn_fwd(proj, lb, gnorm, *, name, T=512):
    S = proj.shape[0]
    T = min(T, S)
    nch = T // HG_CHUNK
    C = HG_CHUNK

    def body(hq_ref, hf_ref, hi_ref, hg_ref, lb_ref, gn_ref, o_ref, oa_ref, st_ref, state):
        @pl.when(pl.program_id(1) == 0)
        def _():
            state[...] = jnp.zeros_like(state)

        lbv = lb_ref[...]
        gn = gn_ref[...]
        row = lax.broadcasted_iota(jnp.int32, (C, C), 0)
        col = lax.broadcasted_iota(jnp.int32, (C, C), 1)
        causal = row >= col
        tril = causal.astype(F32)
        rid = lax.broadcasted_iota(jnp.int32, (C, HG_DK), 0)
        st = state[...]
        for c in range(nch):
            sl = pl.ds(c * C, C)
            hq, hf, v, hg = hq_ref[sl, :], hf_ref[sl, :], hi_ref[sl, :], hg_ref[sl, :]
            _, q, _, _, k, bend, eb, e1, e2, e3 = _hgrn_chunk_common(hq, hf, lbv, tril, rid)
            st_ref[0, c] = st
            o = _hdot(q * eb, st, "nt")
            a = jnp.where(causal, _hdot(q * e1, k * e2, "nt"), 0.0)
            o = o + _hdot(a, v, "nn")
            st = st * jnp.exp(bend) + _hdot(v, k * e3, "tn")
            o_ref[sl, :] = o
            r = lax.rsqrt(jnp.mean(o * o, axis=-1, keepdims=True) + EPS)
            oa_ref[sl, :] = (o * r * gn * (hg * _sigmoid(hg))).astype(oa_ref.dtype)
        state[...] = st

    def grp(gidx):
        return pl.BlockSpec((T, 128), lambda h, t: (t, gidx * 8 + h))

    return pl.pallas_call(
        body, name=name, grid=(HG_HEADS, S // T),
        in_specs=[grp(0), grp(1), grp(2), grp(3),
                  pl.BlockSpec((1, 128), lambda h, t: (0, h)), pl.BlockSpec((1, 128), lambda h, t: (0, 0))],
        out_specs=[pl.BlockSpec((T, 128), lambda h, t: (t, h)), pl.BlockSpec((T, 128), lambda h, t: (t, h)),
                   pl.BlockSpec((1, nch, HG_DV, HG_DK), lambda h, t: (h, t, 0, 0))],
        out_shape=[jax.ShapeDtypeStruct((S, HG_HEADS * HG_DV), F32), jax.ShapeDtypeStruct((S, HG_HEADS * HG_DV), BF16),
                   jax.ShapeDtypeStruct((HG_HEADS, S // C, HG_DV, HG_DK), F32)],
        scratch_shapes=[pltpu.VMEM((HG_DV, HG_DK), F32)],
        compiler_params=_cparams(("parallel", "arbitrary")),
    )(proj, proj, proj, proj, lb, gnorm)


def _hgrn_bwd(proj, lb, gnorm, o, states, doa, *, name, T=512):
    S = proj.shape[0]
    T = min(T, S)
    nch = T // HG_CHUNK
    C = HG_CHUNK
    nT = S // T

    def body(hq_ref, hf_ref, hi_ref, hg_ref, lb_ref, gn_ref, o_ref, st_ref, doa_ref,
             dhq_ref, dhf_ref, dhi_ref, dhg_ref, dlb_ref, dgn_ref, dstate):
        @pl.when(pl.program_id(1) == 0)
        def _():
            dstate[...] = jnp.zeros_like(dstate)
            dlb_ref[...] = jnp.zeros_like(dlb_ref)
            dgn_ref[...] = jnp.zeros_like(dgn_ref)

        lbv = lb_ref[...]
        gn = gn_ref[...]
        row = lax.broadcasted_iota(jnp.int32, (C, C), 0)
        col = lax.broadcasted_iota(jnp.int32, (C, C), 1)
        causal = row >= col
        tril = causal.astype(F32)
        triu = (row <= col).astype(F32)
        rid = lax.broadcasted_iota(jnp.int32, (C, HG_DK), 0)
        for c in reversed(range(nch)):
            sl = pl.ds(c * C, C)
            hq, hf, v, hg = hq_ref[sl, :], hf_ref[sl, :], hi_ref[sl, :], hg_ref[sl, :]
            sq, q, sg, f, k, bend, eb, e1, e2, e3 = _hgrn_chunk_common(hq, hf, lbv, tril, rid)
            qi, qp, kp, kend = q * eb, q * e1, k * e2, k * e3
            st0 = st_ref[0, c]
            ov = o_ref[sl, :]
            r = lax.rsqrt(jnp.mean(ov * ov, axis=-1, keepdims=True) + EPS)
            xh = ov * r
            sgg = _sigmoid(hg)
            d_oa = doa_ref[sl, :].astype(F32)
            dz = d_oa * (hg * sgg)
            dhg_ref[sl, :] = (d_oa * (xh * gn) * (sgg * (1.0 + hg * (1.0 - sgg)))).astype(dhg_ref.dtype)
            dgn_ref[0] += jnp.sum(dz * xh, axis=0, keepdims=True)
            dxh = dz * gn
            do = r * (dxh - xh * jnp.mean(dxh * xh, axis=-1, keepdims=True))
            ds1 = dstate[...]
            dqi = _hdot(do, st0, "nn")
            a = jnp.where(causal, _hdot(qp, kp, "nt"), 0.0)
            da = jnp.where(causal, _hdot(do, v, "nt"), 0.0)
            dv = _hdot(a, do, "tn") + _hdot(kend, ds1, "nt")
            dqp = _hdot(da, kp, "nn")
            dkp = _hdot(da, qp, "tn")
            dkend = _hdot(v, ds1, "nn")
            dq = dqi * eb + dqp * e1
            dk = dkp * e2 + dkend * e3
            db = dqi * qi + dqp * qp - dkp * kp - dkend * kend
            dbend = (jnp.sum(dkend * kend, axis=0, keepdims=True)
                     + jnp.exp(bend) * jnp.sum(ds1 * st0, axis=0, keepdims=True))
            db = db + jnp.where(rid == C - 1, dbend, 0.0)
            dg = _dot(triu, db, _DIMS["nn"], precision=lax.Precision.HIGHEST)
            df = dg / f - dk
            dlb_ref[...] += jnp.sum(df * (1.0 - sg), axis=0, keepdims=True)
            dhf_ref[sl, :] = (df * (1.0 - lbv) * sg * (1.0 - sg)).astype(dhf_ref.dtype)
            dhq_ref[sl, :] = (dq * (sq * (1.0 + hq * (1.0 - sq)))).astype(dhq_ref.dtype)
            dhi_ref[sl, :] = dv.astype(dhi_ref.dtype)
            dstate[...] = ds1 * jnp.exp(bend) + _hdot(do, qi, "tn")

    def grp(gidx):
        return pl.BlockSpec((T, 128), lambda h, t: (nT - 1 - t, gidx * 8 + h))

    tok = pl.BlockSpec((T, 128), lambda h, t: (nT - 1 - t, h))
    big = jax.ShapeDtypeStruct((S, HG_HEADS * HG_DV), BF16)
    return pl.pallas_call(
        body, name=name, grid=(HG_HEADS, nT),
        in_specs=[grp(0), grp(1), grp(2), grp(3),
                  pl.BlockSpec((1, 128), lambda h, t: (0, h)), pl.BlockSpec((1, 128), lambda h, t: (0, 0)),
                  tok, pl.BlockSpec((1, nch, HG_DV, HG_DK), lambda h, t: (h, nT - 1 - t, 0, 0)), tok],
        out_specs=[tok, tok, tok, tok, pl.BlockSpec((1, 128), lambda h, t: (0, h)),
                   pl.BlockSpec((1, 1, 128), lambda h, t: (h, 0, 0))],
        out_shape=[big, big, big, big, jax.ShapeDtypeStruct((1, HG_HEADS * HG_DK), F32),
                   jax.ShapeDtypeStruct((HG_HEADS, 1, HG_DV), F32)],
        scratch_shapes=[pltpu.VMEM((HG_DV, HG_DK), F32)],
        compiler_params=_cparams(("parallel", "arbitrary")),
    )(proj, proj, proj, proj, lb, gnorm, o, states, doa)


NEG = -1e30
FOX_SCALE = FOX_DH ** -0.5
FOX_PAIRS = FOX_HEADS // 2


def _fox_gate_fwd(ff, bias, *, name, T=512):
    S = ff.shape[0]
    T = min(T, S)

    def body(ff_ref, b_ref, c_ref, carry):
        @pl.when(pl.program_id(0) == 0)
        def _():
            carry[...] = jnp.zeros_like(carry)

        z = ff_ref[...] + b_ref[...]
        logf = jnp.minimum(z, 0.0) - jnp.log(1.0 + jnp.exp(-jnp.abs(z)))
        row = lax.broadcasted_iota(jnp.int32, (T, T), 0)
        col = lax.broadcasted_iota(jnp.int32, (T, T), 1)
        c = _dot((row >= col).astype(F32), logf, _DIMS["nn"], precision=lax.Precision.HIGHEST) + carry[...]
        c_ref[...] = c
        carry[...] = c[T - 1:T, :]

    return pl.pallas_call(
        body, name=name, grid=(S // T,),
        in_specs=[pl.BlockSpec((T, 128), lambda i: (i, 0)), pl.BlockSpec((1, 128), lambda i: (0, 0))],
        out_specs=pl.BlockSpec((T, 128), lambda i: (i, 0)),
        out_shape=jax.ShapeDtypeStruct((S, 128), F32),
        scratch_shapes=[pltpu.VMEM((1, 128), F32)],
        compiler_params=_cparams(("arbitrary",)),
    )(ff, bias)


def _fox_gate_bwd(ff, bias, dcs, *, name, T=512):
    S = ff.shape[0]
    T = min(T, S)
    nT = S // T

    def body(ff_ref, b_ref, d_ref, dff_ref, db_ref, carry):
        @pl.when(pl.program_id(0) == 0)
        def _():
            carry[...] = jnp.zeros_like(carry)
            db_ref[...] = jnp.zeros_like(db_ref)

        row = lax.broadcasted_iota(jnp.int32, (T, T), 0)
        col = lax.broadcasted_iota(jnp.int32, (T, T), 1)
        dlogf = carry[...] - _dot((row <= col).astype(F32), d_ref[...], _DIMS["nn"], precision=lax.Precision.HIGHEST)
        carry[...] = dlogf[0:1, :]
        dff = dlogf * (1.0 - _sigmoid(ff_ref[...] + b_ref[...]))
        dff_ref[...] = dff.astype(dff_ref.dtype)
        db_ref[...] += jnp.sum(dff, axis=0, keepdims=True)

    rev = pl.BlockSpec((T, 128), lambda i: (nT - 1 - i, 0))
    vec = pl.BlockSpec((1, 128), lambda i: (0, 0))
    return pl.pallas_call(
        body, name=name, grid=(nT,),
        in_specs=[rev, vec, rev], out_specs=[rev, vec],
        out_shape=[jax.ShapeDtypeStruct((S, 128), BF16), jax.ShapeDtypeStruct((1, 128), F32)],
        scratch_shapes=[pltpu.VMEM((1, 128), F32)],
        compiler_params=_cparams(("arbitrary",)),
    )(ff, bias, dcs)


def _fox_logits(q, k, cc, cr, qi, ki, tq, tk):
    s = _bdot(q, k, "nt") * FOX_SCALE + cc - cr
    qpos = qi * tq + lax.broadcasted_iota(jnp.int32, (tq, tk), 0)
    kpos = ki * tk + lax.broadcasted_iota(jnp.int32, (tq, tk), 1)
    return jnp.where(kpos <= qpos, s, NEG)


def _fox_fwd(proj, ccol, crow, *, name, tq=512, tk=512):
    S = proj.shape[0]
    tq, tk = min(tq, S), min(tk, S)

    def body(q_ref, k_ref, v_ref, cc_ref, cr_ref, o_ref, lse_ref, m_s, l_s, acc_s):
        qi, ki = pl.program_id(1), pl.program_id(2)

        @pl.when(ki == 0)
        def _():
            m_s[...] = jnp.full_like(m_s, NEG)
            l_s[...] = jnp.zeros_like(l_s)
            acc_s[...] = jnp.zeros_like(acc_s)

        @pl.when(ki <= qi)
        def _():
            for hh in range(2):
                ls = slice(hh * FOX_DH, (hh + 1) * FOX_DH)
                s = _fox_logits(q_ref[:, ls], k_ref[:, ls], cc_ref[0, :, hh:hh + 1], cr_ref[0, hh:hh + 1, :], qi, ki, tq, tk)
                m_old = m_s[hh]
                m_new = jnp.maximum(m_old, jnp.max(s, axis=-1, keepdims=True))
                p = jnp.exp(s - m_new)
                alpha = jnp.exp(m_old - m_new)
                l_s[hh] = alpha * l_s[hh] + jnp.sum(p, axis=-1, keepdims=True)
                p_hi = p.astype(BF16)
                p_lo = (p - p_hi.astype(F32)).astype(BF16)
                vv = v_ref[:, ls].astype(BF16)
                acc_s[hh] = alpha * acc_s[hh] + _bdot(p_hi, vv, "nn") + _bdot(p_lo, vv, "nn")
                m_s[hh] = m_new

        @pl.when(ki == qi)
        def _():
            for hh in range(2):
                o_ref[:, hh * FOX_DH:(hh + 1) * FOX_DH] = acc_s[hh] / l_s[hh]
                lse_ref[0, :, hh:hh + 1] = m_s[hh] + jnp.log(l_s[hh])

    qspec = pl.BlockSpec((tq, 128), lambda p, i, j: (i, 32 + p))
    kspec = pl.BlockSpec((tk, 128), lambda p, i, j: (jnp.minimum(j, i), 40 + p))
    vspec = pl.BlockSpec((tk, 128), lambda p, i, j: (jnp.minimum(j, i), 48 + p))
    ccs = pl.BlockSpec((1, tq, 2), lambda p, i, j: (p, i, 0))
    crs = pl.BlockSpec((1, 2, tk), lambda p, i, j: (p, 0, jnp.minimum(j, i)))
    return pl.pallas_call(
        body, name=name, grid=(FOX_PAIRS, S // tq, S // tk),
        in_specs=[qspec, kspec, vspec, ccs, crs],
        out_specs=[pl.BlockSpec((tq, 128), lambda p, i, j: (i, p)), ccs],
        out_shape=[jax.ShapeDtypeStruct((S, FOX_HEADS * FOX_DH), F32), jax.ShapeDtypeStruct((FOX_PAIRS, S, 2), F32)],
        scratch_shapes=[pltpu.VMEM((2, tq, 1), F32), pltpu.VMEM((2, tq, 1), F32), pltpu.VMEM((2, tq, FOX_DH), F32)],
        compiler_params=_cparams(("parallel", "parallel", "arbitrary")),
    )(proj, proj, proj, ccol, crow)


def _fox_bwd_dq(proj, ccol, crow, o, lse, do, *, name, tq=512, tk=512):
    S = proj.shape[0]
    tq, tk = min(tq, S), min(tk, S)

    def body(q_ref, k_ref, v_ref, cc_ref, cr_ref, o_ref, lse_ref, do_ref, dq_ref, dl_ref, acc_s):
        qi, ki = pl.program_id(1), pl.program_id(2)

        @pl.when(ki == 0)
        def _():
            acc_s[...] = jnp.zeros_like(acc_s)
            for hh in range(2):
                ls = slice(hh * FOX_DH, (hh + 1) * FOX_DH)
                dl_ref[0, :, hh:hh + 1] = jnp.sum(do_ref[:, ls].astype(F32) * o_ref[:, ls], axis=-1, keepdims=True)

        @pl.when(ki <= qi)
        def _():
            for hh in range(2):
                ls = slice(hh * FOX_DH, (hh + 1) * FOX_DH)
                s = _fox_logits(q_ref[:, ls], k_ref[:, ls], cc_ref[0, :, hh:hh + 1], cr_ref[0, hh:hh + 1, :], qi, ki, tq, tk)
                p = jnp.exp(s - lse_ref[0, :, hh:hh + 1])
                dp = _bdot(do_ref[:, ls], v_ref[:, ls], "nt")
                ds = p * (dp - dl_ref[0, :, hh:hh + 1])
                acc_s[hh] += _bdot(ds, k_ref[:, ls], "nn")

        @pl.when(ki == qi)
        def _():
            for hh in range(2):
                dq_ref[:, hh * FOX_DH:(hh + 1) * FOX_DH] = (acc_s[hh] * FOX_SCALE).astype(dq_ref.dtype)

    qspec = pl.BlockSpec((tq, 128), lambda p, i, j: (i, 32 + p))
    kspec = pl.BlockSpec((tk, 128), lambda p, i, j: (jnp.minimum(j, i), 40 + p))
    vspec = pl.BlockSpec((tk, 128), lambda p, i, j: (jnp.minimum(j, i), 48 + p))
    ccs = pl.BlockSpec((1, tq, 2), lambda p, i, j: (p, i, 0))
    crs = pl.BlockSpec((1, 2, tk), lambda p, i, j: (p, 0, jnp.minimum(j, i)))
    tok = pl.BlockSpec((tq, 128), lambda p, i, j: (i, p))
    return pl.pallas_call(
        body, name=name, grid=(FOX_PAIRS, S // tq, S // tk),
        in_specs=[qspec, kspec, vspec, ccs, crs, tok, ccs, tok],
        out_specs=[tok, ccs],
        out_shape=[jax.ShapeDtypeStruct((S, FOX_HEADS * FOX_DH), BF16), jax.ShapeDtypeStruct((FOX_PAIRS, S, 2), F32)],
        scratch_shapes=[pltpu.VMEM((2, tq, FOX_DH), F32)],
        compiler_params=_cparams(("parallel", "parallel", "arbitrary")),
    )(proj, proj, proj, ccol, crow, o, lse, do)


def _fox_bwd_dkv(proj, ccol, crow, lse, delta, do, *, name, tq=512, tk=512):
    S = proj.shape[0]
    tq, tk = min(tq, S), min(tk, S)
    nq = S // tq

    def body(q_ref, k_ref, v_ref, cc_ref, cr_ref, lse_ref, dl_ref, do_ref, dk_ref, dv_ref, dcs_ref, dk_s, dv_s):
        ki, qi = pl.program_id(1), pl.program_id(2)

        @pl.when(qi == 0)
        def _():
            dk_s[...] = jnp.zeros_like(dk_s)
            dv_s[...] = jnp.zeros_like(dv_s)
            dcs_ref[...] = jnp.zeros_like(dcs_ref)

        @pl.when(qi >= ki)
        def _():
            for hh in range(2):
                ls = slice(hh * FOX_DH, (hh + 1) * FOX_DH)
                s = _fox_logits(q_ref[:, ls], k_ref[:, ls], cc_ref[0, :, hh:hh + 1], cr_ref[0, hh:hh + 1, :], qi, ki, tq, tk)
                p = jnp.exp(s - lse_ref[0, :, hh:hh + 1])
                dp = _bdot(do_ref[:, ls], v_ref[:, ls], "nt")
                ds = p * (dp - dl_ref[0, :, hh:hh + 1])
                dv_s[hh] += _bdot(p, do_ref[:, ls], "tn")
                dk_s[hh] += _bdot(ds, q_ref[:, ls], "tn")
                dcs_ref[0, hh:hh + 1, :] += jnp.sum(ds, axis=0, keepdims=True)

        @pl.when(qi == nq - 1)
        def _():
            for hh in range(2):
                dk_ref[:, hh * FOX_DH:(hh + 1) * FOX_DH] = (dk_s[hh] * FOX_SCALE).astype(dk_ref.dtype)
                dv_ref[:, hh * FOX_DH:(hh + 1) * FOX_DH] = dv_s[hh].astype(dv_ref.dtype)

    qspec = pl.BlockSpec((tq, 128), lambda p, j, i: (jnp.maximum(i, j), 32 + p))
    kspec = pl.BlockSpec((tk, 128), lambda p, j, i: (j, 40 + p))
    vspec = pl.BlockSpec((tk, 128), lambda p, j, i: (j, 48 + p))
    ccs = pl.BlockSpec((1, tq, 2), lambda p, j, i: (p, jnp.maximum(i, j), 0))
    crs = pl.BlockSpec((1, 2, tk), lambda p, j, i: (p, 0, j))
    dos = pl.BlockSpec((tq, 128), lambda p, j, i: (jnp.maximum(i, j), p))
    ktok = pl.BlockSpec((tk, 128), lambda p, j, i: (j, p))
    big = jax.ShapeDtypeStruct((S, FOX_HEADS * FOX_DH), BF16)
    return pl.pallas_call(
        body, name=name, grid=(FOX_PAIRS, S // tk, nq),
        in_specs=[qspec, kspec, vspec, ccs, crs, ccs, ccs, dos],
        out_specs=[ktok, ktok, crs],
        out_shape=[big, big, jax.ShapeDtypeStruct((FOX_PAIRS, 2, S), F32)],
        scratch_shapes=[pltpu.VMEM((2, tk, FOX_DH), F32), pltpu.VMEM((2, tk, FOX_DH), F32)],
        compiler_params=_cparams(("parallel", "parallel", "arbitrary")),
    )(proj, proj, proj, ccol, crow, lse, delta, do)


AUG = FOX_DH
FOX_SUB = 2


def _split3(x):
    a = x.astype(BF16).astype(F32)
    r = x - a
    b = r.astype(BF16).astype(F32)
    return a, b, r - b


def _lane_fill(lane, base, pieces, start):
    for i, pc in enumerate(pieces):
        base = jnp.where(lane == start + i, pc, base)
    return base


def _fox_prep(proj, c_tok, *, name, T=512):
    S = proj.shape[0]
    T = min(T, S)

    def body(q_ref, k_ref, v_ref, c_ref, qa_ref, ka_ref, va_ref):
        pair = pl.program_id(0)
        lane = lax.broadcasted_iota(jnp.int32, (T, 128), 1)
        c = c_ref[...]
        ones3 = jnp.where((lane >= AUG) & (lane < AUG + 3), 1.0, 0.0)
        for hh in range(2):
            ch = jnp.sum(jnp.where(lane == 2 * pair + hh, c, 0.0), axis=-1, keepdims=True)
            c1, c2, c3 = _split3(ch)
            q, k, v = q_ref[...], k_ref[...], v_ref[...]
            if hh == 1:
                q, k, v = (pltpu.roll(t, 64, 1) for t in (q, k, v))
            aug_q = _lane_fill(lane, jnp.where((lane >= AUG + 3) & (lane < AUG + 6), 1.0, 0.0), (c1, c2, c3), AUG)
            aug_k = _lane_fill(lane, ones3, (-c1, -c2, -c3), AUG + 3)
            qa_ref[hh] = jnp.where(lane < AUG, q * FOX_SCALE, aug_q).astype(BF16)
            ka_ref[hh] = jnp.where(lane < AUG, k, aug_k).astype(BF16)
            va_ref[hh] = jnp.where(lane < AUG, v, ones3).astype(BF16)

    def grp(g):
        return pl.BlockSpec((T, 128), lambda p, t: (t, g * 8 + p))

    hm = pl.BlockSpec((2, T, 128), lambda p, t: (p, t, 0))
    out = jax.ShapeDtypeStruct((FOX_HEADS, S, 128), BF16)
    return pl.pallas_call(
        body, name=name, grid=(FOX_PAIRS, S // T),
        in_specs=[grp(4), grp(5), grp(6), pl.BlockSpec((T, 128), lambda p, t: (t, 0))],
        out_specs=[hm, hm, hm], out_shape=[out, out, out],
        compiler_params=_cparams(("parallel", "parallel")),
    )(proj, proj, proj, c_tok)


def _pair_lanes(lane, a0, a1):
    return jnp.where(lane < AUG, a0, pltpu.roll(a1, 64, 1))


def _tri_tables(nb, by_query):
    if by_query:
        pairs = [(i, j) for i in range(nb) for j in range(i + 1)]
    else:
        pairs = [(i, j) for j in range(nb) for i in range(j, nb)]
    return (jnp.asarray(np.array([p[0] for p in pairs], np.int32)),
            jnp.asarray(np.array([p[1] for p in pairs], np.int32)))


def _fox_fwd2(qa, ka, va, *, name, tb=512):
    S = qa.shape[1]
    tb = min(tb, S)
    rs = tb // FOX_SUB
    qtab, ktab = _tri_tables(S // tb, True)

    def body(qt_ref, kt_ref, qa_ref, ka_ref, va_ref, o_ref, qb_ref, m_s, acc_s):
        qi, ki = qt_ref[pl.program_id(1)], kt_ref[pl.program_id(1)]

        @pl.when(ki == 0)
        def _():
            m_s[...] = jnp.full_like(m_s, NEG)
            acc_s[...] = jnp.zeros_like(acc_s)

        def step(masked):
            for hh in range(2):
                s = _dot(qa_ref[hh], ka_ref[hh], _DIMS["nt"])
                if masked:
                    row = lax.broadcasted_iota(jnp.int32, (tb, tb), 0)
                    col = lax.broadcasted_iota(jnp.int32, (tb, tb), 1)
                    s = jnp.where(col <= row, s, NEG)
                m_old = m_s[hh]
                m_new = jnp.maximum(m_old, jnp.max(s, axis=-1, keepdims=True))
                p = jnp.exp(s - m_new)
                p_hi = p.astype(BF16)
                p_lo = (p - p_hi.astype(F32)).astype(BF16)
                vv = va_ref[hh]
                acc_s[hh] = (jnp.exp(m_old - m_new) * acc_s[hh]
                             + _dot(p_hi, vv, _DIMS["nn"]) + _dot(p_lo, vv, _DIMS["nn"]))
                m_s[hh] = m_new

        @pl.when(ki < qi)
        def _():
            step(False)

        @pl.when(ki == qi)
        def _():
            step(True)
            lane = lax.broadcasted_iota(jnp.int32, (tb, 128), 1)
            outs = []
            for hh in range(2):
                acc = acc_s[hh]
                l = acc[:, AUG:AUG + 1]
                outs.append(acc / l)
                qf = qa_ref[hh].astype(F32)
                cb = qf[:, AUG:AUG + 1] + qf[:, AUG + 1:AUG + 2] + qf[:, AUG + 2:AUG + 3] - (m_s[hh] + jnp.log(l))
                qb_ref[hh] = _lane_fill(lane, qf, _split3(cb), AUG).astype(BF16)
            o_ref[...] = _pair_lanes(lane, outs[0], outs[1])

    qs = pl.BlockSpec((2, tb, 128), lambda p, t, qt, kt: (p, qt[t], 0))
    ks = pl.BlockSpec((2, tb, 128), lambda p, t, qt, kt: (p, kt[t], 0))
    return pl.pallas_call(
        body, name=name,
        grid_spec=pltpu.PrefetchScalarGridSpec(
            num_scalar_prefetch=2, grid=(FOX_PAIRS, qtab.shape[0]), in_specs=[qs, ks, ks],
            out_specs=[pl.BlockSpec((tb, 128), lambda p, t, qt, kt: (qt[t], p)), qs],
            scratch_shapes=[pltpu.VMEM((2, tb, 1), F32), pltpu.VMEM((2, tb, 128), F32)]),
        out_shape=[jax.ShapeDtypeStruct((S, FOX_HEADS * FOX_DH), F32), jax.ShapeDtypeStruct((FOX_HEADS, S, 128), BF16)],
        compiler_params=_cparams(("parallel", "arbitrary")),
    )(qtab, ktab, qa, ka, va)


def _fox_bwd_prep(o, do, *, name, T=512):
    S = o.shape[0]
    T = min(T, S)

    def body(o_ref, do_ref, dob_ref):
        lane = lax.broadcasted_iota(jnp.int32, (T, 128), 1)
        d = do_ref[...].astype(F32)
        prod = d * o_ref[...]
        for hh in range(2):
            mine = (lane < AUG) if hh == 0 else (lane >= AUG)
            delta = jnp.sum(jnp.where(mine, prod, 0.0), axis=-1, keepdims=True)
            dh = d if hh == 0 else pltpu.roll(d, 64, 1)
            dob_ref[hh] = _lane_fill(lane, jnp.where(lane < AUG, dh, 0.0), _split3(-delta), AUG).astype(BF16)

    tok = pl.BlockSpec((T, 128), lambda p, t: (t, p))
    return pl.pallas_call(
        body, name=name, grid=(FOX_PAIRS, S // T),
        in_specs=[tok, tok], out_specs=pl.BlockSpec((2, T, 128), lambda p, t: (p, t, 0)),
        out_shape=jax.ShapeDtypeStruct((FOX_HEADS, S, 128), BF16),
        compiler_params=_cparams(("parallel", "parallel")),
    )(o, do)


def _fox_bwd_dq2(qb, ka, va, dob, *, name, tb=512, comm=None):
    S = qb.shape[1]
    tb = min(tb, S)
    rs = tb // FOX_SUB
    nb = S // tb
    qtab, ktab = _tri_tables(nb, True)
    nc = comm.n if comm is not None else 0
    ntri = qtab.shape[0]

    def body(qt_ref, kt_ref, qb_ref, ka_ref, va_ref, dob_ref, *rest):
        c_in, (dq_ref, dcs_ref), c_out = rest[:nc], rest[nc:nc + 2], rest[nc + 2:2 * nc + 2]
        acc_s, c_sems = rest[2 * nc + 2], rest[2 * nc + 3:]
        qi, ki = qt_ref[pl.program_id(1)], kt_ref[pl.program_id(1)]
        if comm is not None:
            @pl.when((pl.program_id(0) == 0) & (pl.program_id(1) == 0))
            def _():
                comm.start(c_in, c_out, c_sems)

        @pl.when(ki == 0)
        def _():
            acc_s[...] = jnp.zeros_like(acc_s)

        def step(masked):
            for hh in range(2):
                s = _dot(qb_ref[hh], ka_ref[hh], _DIMS["nt"])
                if masked:
                    row = lax.broadcasted_iota(jnp.int32, (tb, tb), 0)
                    col = lax.broadcasted_iota(jnp.int32, (tb, tb), 1)
                    s = jnp.where(col <= row, s, NEG)
                ds = jnp.exp(s) * _dot(dob_ref[hh], va_ref[hh], _DIMS["nt"])
                dcs_ref[0, 0, hh:hh + 1, :] = jnp.sum(ds, axis=0, keepdims=True)
                acc_s[hh] += _dot(ds.astype(BF16), ka_ref[hh], _DIMS["nn"])

        @pl.when(ki < qi)
        def _():
            step(False)

        @pl.when(ki == qi)
        def _():
            step(True)
            lane = lax.broadcasted_iota(jnp.int32, (tb, 128), 1)
            dq_ref[...] = (_pair_lanes(lane, acc_s[0], acc_s[1]) * FOX_SCALE).astype(dq_ref.dtype)

        if comm is not None:
            @pl.when((pl.program_id(0) == FOX_PAIRS - 1) & (pl.program_id(1) == ntri - 1))
            def _():
                comm.finish(c_in, c_out, c_sems)

    qs = pl.BlockSpec((2, tb, 128), lambda p, t, qt, kt: (p, qt[t], 0))
    ks = pl.BlockSpec((2, tb, 128), lambda p, t, qt, kt: (p, kt[t], 0))
    outs = pl.pallas_call(
        body, name=name,
        grid_spec=pltpu.PrefetchScalarGridSpec(
            num_scalar_prefetch=2, grid=(FOX_PAIRS, ntri), in_specs=[qs, ks, ks, qs] + [ANY] * nc,
            out_specs=[pl.BlockSpec((tb, 128), lambda p, t, qt, kt: (qt[t], p)),
                       pl.BlockSpec((1, 1, 2, tb), lambda p, t, qt, kt: (p, qt[t], 0, kt[t]))] + [ANY] * nc,
            scratch_shapes=[pltpu.VMEM((2, tb, 128), F32)] + (comm.scratch if comm is not None else [])),
        out_shape=[jax.ShapeDtypeStruct((S, FOX_HEADS * FOX_DH), BF16),
                   jax.ShapeDtypeStruct((FOX_PAIRS, nb, 2, S), F32)] + (comm.out_shapes if comm is not None else []),
        compiler_params=_cparams(("parallel", "arbitrary") if comm is None else ("arbitrary", "arbitrary")),
    )(qtab, ktab, qb, ka, va, dob, *(comm.inputs if comm is not None else []))
    return (outs[0], outs[1]) if comm is None else (outs[0], outs[1], outs[2:])


def _fox_bwd_dkv2(qb, ka, va, dob, *, name, tb=512):
    S = qb.shape[1]
    tb = min(tb, S)
    rs = tb // FOX_SUB
    nb = S // tb
    qtab, ktab = _tri_tables(nb, False)

    def body(qt_ref, kt_ref, qb_ref, ka_ref, va_ref, dob_ref, dk_ref, dv_ref, dk_s, dv_s):
        qi, ki = qt_ref[pl.program_id(1)], kt_ref[pl.program_id(1)]

        @pl.when(qi == ki)
        def _():
            dk_s[...] = jnp.zeros_like(dk_s)
            dv_s[...] = jnp.zeros_like(dv_s)

        def step(masked):
            for hh in range(2):
                st = _dot(ka_ref[hh], qb_ref[hh], _DIMS["nt"])
                if masked:
                    row = lax.broadcasted_iota(jnp.int32, (tb, tb), 0)
                    col = lax.broadcasted_iota(jnp.int32, (tb, tb), 1)
                    st = jnp.where(row <= col, st, NEG)
                pt = jnp.exp(st)
                dst = pt * _dot(va_ref[hh], dob_ref[hh], _DIMS["nt"])
                dv_s[hh] += _dot(pt.astype(BF16), dob_ref[hh], _DIMS["nn"])
                dk_s[hh] += _dot(dst.astype(BF16), qb_ref[hh], _DIMS["nn"])

        @pl.when(qi > ki)
        def _():
            step(False)

        @pl.when(qi == ki)
        def _():
            step(True)

        @pl.when(qi == nb - 1)
        def _():
            lane = lax.broadcasted_iota(jnp.int32, (tb, 128), 1)
            dk_ref[...] = _pair_lanes(lane, dk_s[0], dk_s[1]).astype(dk_ref.dtype)
            dv_ref[...] = _pair_lanes(lane, dv_s[0], dv_s[1]).astype(dv_ref.dtype)

    ks = pl.BlockSpec((2, tb, 128), lambda p, t, qt, kt: (p, kt[t], 0))
    qs = pl.BlockSpec((2, tb, 128), lambda p, t, qt, kt: (p, qt[t], 0))
    tok = pl.BlockSpec((tb, 128), lambda p, t, qt, kt: (kt[t], p))
    big = jax.ShapeDtypeStruct((S, FOX_HEADS * FOX_DH), BF16)
    return pl.pallas_call(
        body, name=name,
        grid_spec=pltpu.PrefetchScalarGridSpec(
            num_scalar_prefetch=2, grid=(FOX_PAIRS, qtab.shape[0]), in_specs=[qs, ks, ks, qs], out_specs=[tok, tok],
            scratch_shapes=[pltpu.VMEM((2, tb, 128), F32), pltpu.VMEM((2, tb, 128), F32)]),
        out_shape=[big, big],
        compiler_params=_cparams(("parallel", "arbitrary")),
    )(qtab, ktab, qb, ka, va, dob)


def _merge_fwd(proj, pa, pb, *, name, T=512):
    S, D = pa.shape
    T = min(T, S)

    def body(ga_ref, gb_ref, pa_ref, pb_ref, m_ref):
        m_ref[...] = (_sigmoid(ga_ref[...]) * pa_ref[...] + _sigmoid(gb_ref[...]) * pb_ref[...]).astype(m_ref.dtype)

    tok = pl.BlockSpec((T, D), lambda i: (i, 0))
    return pl.pallas_call(
        body, name=name, grid=(S // T,),
        in_specs=[pl.BlockSpec((T, D), lambda i: (i, 7)), pl.BlockSpec((T, D), lambda i: (i, 8)), tok, tok],
        out_specs=tok, out_shape=jax.ShapeDtypeStruct((S, D), BF16),
        compiler_params=_cparams(("parallel",)),
    )(proj, proj, pa, pb)


def _merge_bwd(proj, pa, pb, dm, *, name, T=512):
    S, D = pa.shape
    T = min(T, S)

    def body(ga_ref, gb_ref, pa_ref, pb_ref, dm_ref, dpa_ref, dpb_ref, dga_ref, dgb_ref):
        dm_ = dm_ref[...]
        sa, sb = _sigmoid(ga_ref[...]), _sigmoid(gb_ref[...])
        dpa_ref[...] = (dm_ * sa).astype(BF16)
        dpb_ref[...] = (dm_ * sb).astype(BF16)
        dga_ref[...] = (dm_ * pa_ref[...] * sa * (1.0 - sa)).astype(BF16)
        dgb_ref[...] = (dm_ * pb_ref[...] * sb * (1.0 - sb)).astype(BF16)

    tok = pl.BlockSpec((T, D), lambda i: (i, 0))
    big = jax.ShapeDtypeStruct((S, D), BF16)
    return pl.pallas_call(
        body, name=name, grid=(S // T,),
        in_specs=[pl.BlockSpec((T, D), lambda i: (i, 7)), pl.BlockSpec((T, D), lambda i: (i, 8)), tok, tok, tok],
        out_specs=[tok, tok, tok, tok], out_shape=[big, big, big, big],
        compiler_params=_cparams(("parallel",)),
    )(proj, proj, pa, pb, dm)


INV_SQRT2 = 0.7071067811865476
INV_SQRT2PI = 0.3989422804014327


def _shifted(u, prev, rid):
    m1 = jnp.where(rid == 0, prev[7:8, :], pltpu.roll(u, 1, 0))
    m2 = jnp.where(rid == 0, prev[6:7, :], jnp.where(rid == 1, prev[7:8, :], pltpu.roll(u, 2, 0)))
    return m1, m2


def _conv_acc(u, prev, w_ref, b_ref, rid):
    m1, m2 = _shifted(u, prev, rid)
    return b_ref[...] + w_ref[0:1, :] * m2 + w_ref[1:2, :] * m1 + w_ref[2:3, :] * u, m1, m2


def _convglu_fwd(ug, uv, wg, wv, bg, bv, *, name, T=512, tc=256):
    S, F = ug.shape
    T = min(T, S)

    def body(ug_ref, uv_ref, wg_ref, wv_ref, bg_ref, bv_ref, a_ref, pg, pv):
        @pl.when(pl.program_id(1) == 0)
        def _():
            pg[...] = jnp.zeros_like(pg)
            pv[...] = jnp.zeros_like(pv)

        rid = lax.broadcasted_iota(jnp.int32, (T, tc), 0)
        g_, v_ = ug_ref[...], uv_ref[...]
        accg, _, _ = _conv_acc(g_, pg[...], wg_ref, bg_ref, rid)
        accv, _, _ = _conv_acc(v_, pv[...], wv_ref, bv_ref, rid)
        gel = 0.5 * accg * (1.0 + lax.erf(accg * INV_SQRT2))
        a_ref[...] = (gel * accv).astype(a_ref.dtype)
        pg[...] = g_[T - 8:T, :]
        pv[...] = v_[T - 8:T, :]

    tok = pl.BlockSpec((T, tc), lambda j, t: (t, j))
    w3 = pl.BlockSpec((3, tc), lambda j, t: (0, j))
    b1 = pl.BlockSpec((1, tc), lambda j, t: (0, j))
    return pl.pallas_call(
        body, name=name, grid=(F // tc, S // T),
        in_specs=[tok, tok, w3, w3, b1, b1], out_specs=tok,
        out_shape=jax.ShapeDtypeStruct((S, F), BF16),
        scratch_shapes=[pltpu.VMEM((8, tc), F32), pltpu.VMEM((8, tc), F32)],
        compiler_params=_cparams(("parallel", "arbitrary")),
    )(ug, uv, wg, wv, bg, bv)


def _convglu_bwd_acc(ug, uv, wg, wv, bg, bv, da, *, name, T=512, tc=256):
    S, F = ug.shape
    T = min(T, S)

    def body(ug_ref, uv_ref, wg_ref, wv_ref, bg_ref, bv_ref, da_ref,
             dg_ref, dv_ref, dwg_ref, dwv_ref, dbg_ref, dbv_ref, pg, pv):
        @pl.when(pl.program_id(1) == 0)
        def _():
            pg[...] = jnp.zeros_like(pg)
            pv[...] = jnp.zeros_like(pv)
            for r in (dwg_ref, dwv_ref, dbg_ref, dbv_ref):
                r[...] = jnp.zeros_like(r)

        rid = lax.broadcasted_iota(jnp.int32, (T, tc), 0)
        g_, v_ = ug_ref[...], uv_ref[...]
        accg, g1, g2 = _conv_acc(g_, pg[...], wg_ref, bg_ref, rid)
        accv, v1, v2 = _conv_acc(v_, pv[...], wv_ref, bv_ref, rid)
        cdf = 0.5 * (1.0 + lax.erf(accg * INV_SQRT2))
        pdf = INV_SQRT2PI * jnp.exp(-0.5 * accg * accg)
        da_ = da_ref[...].astype(F32)
        dgate = da_ * accv * (cdf + accg * pdf)
        dval = da_ * (accg * cdf)
        dg_ref[...] = dgate.astype(dg_ref.dtype)
        dv_ref[...] = dval.astype(dv_ref.dtype)
        dbg_ref[...] += jnp.sum(dgate, axis=0, keepdims=True)
        dbv_ref[...] += jnp.sum(dval, axis=0, keepdims=True)
        for j, (sg_, sv_) in enumerate(((g2, v2), (g1, v1), (g_, v_))):
            dwg_ref[j:j + 1, :] += jnp.sum(dgate * sg_, axis=0, keepdims=True)
            dwv_ref[j:j + 1, :] += jnp.sum(dval * sv_, axis=0, keepdims=True)
        pg[...] = g_[T - 8:T, :]
        pv[...] = v_[T - 8:T, :]

    tok = pl.BlockSpec((T, tc), lambda j, t: (t, j))
    w3 = pl.BlockSpec((3, tc), lambda j, t: (0, j))
    b1 = pl.BlockSpec((1, tc), lambda j, t: (0, j))
    big = jax.ShapeDtypeStruct((S, F), BF16)
    return pl.pallas_call(
        body, name=name, grid=(F // tc, S // T),
        in_specs=[tok, tok, w3, w3, b1, b1, tok], out_specs=[tok, tok, w3, w3, b1, b1],
        out_shape=[big, big, jax.ShapeDtypeStruct((3, F), F32), jax.ShapeDtypeStruct((3, F), F32),
                   jax.ShapeDtypeStruct((1, F), F32), jax.ShapeDtypeStruct((1, F), F32)],
        scratch_shapes=[pltpu.VMEM((8, tc), F32), pltpu.VMEM((8, tc), F32)],
        compiler_params=_cparams(("parallel", "arbitrary")),
    )(ug, uv, wg, wv, bg, bv, da)


def _convglu_bwd_fused(ug, uv, wg, wv, bg, bv, da, *, name, T=512, tc=256):
    S, F = ug.shape
    T = min(T, S)
    nT = S // T
    halo_blocks = T // 8

    def up_shift(d, nx, rid):
        p1 = jnp.where(rid == T - 1, nx[0:1, :], pltpu.roll(d, T - 1, 0))
        p2 = jnp.where(rid == T - 1, nx[1:2, :], jnp.where(rid == T - 2, nx[0:1, :], pltpu.roll(d, T - 2, 0)))
        return p1, p2

    def body(ug_ref, uv_ref, hg_ref, hv_ref, wg_ref, wv_ref, bg_ref, bv_ref, da_ref,
             dug_ref, duv_ref, dwg_ref, dwv_ref, dbg_ref, dbv_ref, ng, nv):
        @pl.when(pl.program_id(1) == 0)
        def _():
            ng[...] = jnp.zeros_like(ng)
            nv[...] = jnp.zeros_like(nv)
            for r in (dwg_ref, dwv_ref, dbg_ref, dbv_ref):
                r[...] = jnp.zeros_like(r)

        first_block = pl.program_id(1) == nT - 1
        rid = lax.broadcasted_iota(jnp.int32, (T, tc), 0)
        g_, v_ = ug_ref[...], uv_ref[...]
        pg = jnp.where(first_block, 0.0, hg_ref[...])
        pv = jnp.where(first_block, 0.0, hv_ref[...])
        accg, g1, g2 = _conv_acc(g_, pg, wg_ref, bg_ref, rid)
        accv, v1, v2 = _conv_acc(v_, pv, wv_ref, bv_ref, rid)
        cdf = 0.5 * (1.0 + lax.erf(accg * INV_SQRT2))
        pdf = INV_SQRT2PI * jnp.exp(-0.5 * accg * accg)
        da_ = da_ref[...].astype(F32)
        dgate = da_ * accv * (cdf + accg * pdf)
        dval = da_ * (accg * cdf)
        dbg_ref[...] += jnp.sum(dgate, axis=0, keepdims=True)
        dbv_ref[...] += jnp.sum(dval, axis=0, keepdims=True)
        for j, (sg_, sv_) in enumerate(((g2, v2), (g1, v1), (g_, v_))):
            dwg_ref[j:j + 1, :] += jnp.sum(dgate * sg_, axis=0, keepdims=True)
            dwv_ref[j:j + 1, :] += jnp.sum(dval * sv_, axis=0, keepdims=True)
        for d, w_ref, nx, out_ref in ((dgate, wg_ref, ng, dug_ref), (dval, wv_ref, nv, duv_ref)):
            p1, p2 = up_shift(d, nx[...], rid)
            out_ref[...] = (w_ref[2:3, :] * d + w_ref[1:2, :] * p1 + w_ref[0:1, :] * p2).astype(out_ref.dtype)
            nx[...] = d[0:8, :]

    tok = pl.BlockSpec((T, tc), lambda j, t: (nT - 1 - t, j))
    halo = pl.BlockSpec((8, tc), lambda j, t: (jnp.maximum((nT - 1 - t) * halo_blocks - 1, 0), j))
    w3 = pl.BlockSpec((3, tc), lambda j, t: (0, j))
    b1 = pl.BlockSpec((1, tc), lambda j, t: (0, j))
    big = jax.ShapeDtypeStruct((S, F), BF16)
    return pl.pallas_call(
        body, name=name, grid=(F // tc, nT),
        in_specs=[tok, tok, halo, halo, w3, w3, b1, b1, tok], out_specs=[tok, tok, w3, w3, b1, b1],
        out_shape=[big, big, jax.ShapeDtypeStruct((3, F), F32), jax.ShapeDtypeStruct((3, F), F32),
                   jax.ShapeDtypeStruct((1, F), F32), jax.ShapeDtypeStruct((1, F), F32)],
        scratch_shapes=[pltpu.VMEM((8, tc), F32), pltpu.VMEM((8, tc), F32)],
        compiler_params=_cparams(("parallel", "arbitrary")),
    )(ug, uv, ug, uv, wg, wv, bg, bv, da)


def _conv_bwd_u(dacc, w, *, name, T=512, tc=256):
    S, F = dacc.shape
    T = min(T, S)
    nT = S // T

    def body(d_ref, w_ref, du_ref, nxt):
        @pl.when(pl.program_id(1) == 0)
        def _():
            nxt[...] = jnp.zeros_like(nxt)

        rid = lax.broadcasted_iota(jnp.int32, (T, tc), 0)
        d = d_ref[...].astype(F32)
        nx = nxt[...]
        p1 = jnp.where(rid == T - 1, nx[0:1, :], pltpu.roll(d, T - 1, 0))
        p2 = jnp.where(rid == T - 1, nx[1:2, :], jnp.where(rid == T - 2, nx[0:1, :], pltpu.roll(d, T - 2, 0)))
        du_ref[...] = (w_ref[2:3, :] * d + w_ref[1:2, :] * p1 + w_ref[0:1, :] * p2).astype(du_ref.dtype)
        nxt[...] = d[0:8, :]

    tok = pl.BlockSpec((T, tc), lambda j, t: (nT - 1 - t, j))
    return pl.pallas_call(
        body, name=name, grid=(F // tc, nT),
        in_specs=[tok, pl.BlockSpec((3, tc), lambda j, t: (0, j))], out_specs=tok,
        out_shape=jax.ShapeDtypeStruct((S, F), BF16),
        scratch_shapes=[pltpu.VMEM((8, tc), F32)],
        compiler_params=_cparams(("parallel", "arbitrary")),
    )(dacc, w)


def _late_weights(g_a, g_b, g_o, g_up, g_cw, g_d):
    wup = jnp.concatenate([g_up[d] for d in range(N_DEV)], axis=1)
    cw = jnp.concatenate([g_cw[d] for d in range(N_DEV)], axis=1)
    return dict(wa=g_a.reshape(D_MODEL, D_MODEL), wb=g_b.reshape(D_MODEL, D_MODEL), wo=g_o.reshape(D_MODEL, D_MODEL),
                wug=wup[:, :D_FF], wuv=wup[:, D_FF:], cwg=cw[:, :D_FF], cwv=cw[:, D_FF:], wd=g_d.reshape(D_FF, D_MODEL))


def _early_grad_blocks(d_wa, d_wb, d_wo, d_wug, d_wuv, d_wd):
    up = jnp.stack([d_wug[:, d * 704:(d + 1) * 704] for d in range(4)]
                   + [d_wuv[:, d * 704:(d + 1) * 704] for d in range(4)])
    return [d_wa.reshape(N_DEV, 128, D_MODEL), d_wb.reshape(N_DEV, 128, D_MODEL), d_wo.reshape(N_DEV, 128, D_MODEL),
            up, d_wd.reshape(N_DEV, 352, D_MODEL)]


def _local_step(x, tgt, w, p, late=None, exchange_early=False):
    S = x.shape[0]
    mm = _matmul
    n1 = _rms_fwd(x, p["norm_mix"], name="rms1_fwd")
    if late is None:
        proj = mm(n1, w["wm"], "nn", name="proj_main")
    else:
        proj, gathered = mm(n1, w["wm"], "nn", comm=late, name="proj_main")
        w = {**w, **_late_weights(*gathered)}
    ff = mm(n1, w["wff"], "nn", name="proj_ff")
    lb = _lb_fwd(p["hg_lb_logits"], name="lb_fwd")
    gnorm = p["hg_norm"].reshape(1, HG_DV)
    o_hg, oa, states = _hgrn_fwd_phased(proj, lb, gnorm, name="hgrn_fwd")
    bias = jnp.pad(p["fox_f_bias"].reshape(1, FOX_HEADS), ((0, 0), (0, 128 - FOX_HEADS)))
    c = _fox_gate_fwd(ff, bias, name="fox_gate_fwd")
    qa, ka, va = _fox_prep(proj, c, name="fox_prep")
    ob, qb = _fox_fwd2(qa, ka, va, name="fox_fwd")
    pa = mm(oa, w["wa"], "nn", name="branch_a")
    pb = mm(ob, w["wb"], "nn", name="branch_b")
    merged = _merge_fwd(proj, pa, pb, name="merge_fwd")
    h1 = mm(merged, w["wo"], "nn", addend=x, name="mix_out")
    n2 = _rms_fwd(h1, p["norm_ffn"], name="rms2_fwd")
    ug = mm(n2, w["wug"], "nn", name="up_gate")
    uv = mm(n2, w["wuv"], "nn", name="up_val")
    a = _convglu_fwd(ug, uv, w["cwg"], w["cwv"], p["cbg"], p["cbv"], name="convglu_fwd")
    h2 = mm(a, w["wd"], "nn", addend=h1, name="ffn_down")
    loss, dh2, d_norm_final = _loss_head(h2, p["norm_final"], tgt, name="loss_head")
    da = mm(dh2, w["wd"], "nt", out_dtype=BF16, name="d_act")
    d_wd = mm(a, dh2, "tn", out_dtype=BF16, name="dw_down")
    dug, duv, d_cwg, d_cwv, d_cbg, d_cbv = _convglu_bwd_fused(
        ug, uv, w["cwg"], w["cwv"], p["cbg"], p["cbv"], da, name="convglu_bwd")
    dn2 = mm(dug, w["wug"], "nt", name="dn2_gate")
    dn2 = mm(duv, w["wuv"], "nt", addend=dn2, name="dn2_val")
    d_wug = mm(n2, dug, "tn", out_dtype=BF16, name="dw_up_gate")
    d_wuv = mm(n2, duv, "tn", out_dtype=BF16, name="dw_up_val")
    dh1, d_norm_ffn = _rms_bwd(h1, p["norm_ffn"], dn2, dh2, name="rms2_bwd")
    dmerged = mm(dh1, w["wo"], "nt", name="d_merged")
    d_wo = mm(merged, dh1, "tn", out_dtype=BF16, name="dw_out")
    dpa, dpb, dga, dgb = _merge_bwd(proj, pa, pb, dmerged, name="merge_bwd")
    doa = mm(dpa, w["wa"], "nt", name="d_oa")
    dob = mm(dpb, w["wb"], "nt", out_dtype=BF16, name="d_ob")
    d_wa = mm(oa, dpa, "tn", out_dtype=BF16, name="dw_branch_a")
    d_wb = mm(ob, dpb, "tn", out_dtype=BF16, name="dw_branch_b")
    dhq, dhf, dhi, dhg, dlb, dgn8 = _hgrn_bwd_phased(proj, lb, gnorm, o_hg, states, doa, name="hgrn_bwd")
    d_logits = _lb_bwd(p["hg_lb_logits"], dlb, name="lb_bwd")
    dob_hm = _fox_bwd_prep(ob, dob, name="fox_bwd_prep")
    early_parts = None
    if exchange_early:
        comm = _ExchangeComm(_early_grad_blocks(d_wa, d_wb, d_wo, d_wug, d_wuv, d_wd))
        dq, dcsp, early_parts = _fox_bwd_dq2(qb, ka, va, dob_hm, comm=comm, name="fox_bwd_dq")
    else:
        dq, dcsp = _fox_bwd_dq2(qb, ka, va, dob_hm, name="fox_bwd_dq")
    dk, dv = _fox_bwd_dkv2(qb, ka, va, dob_hm, name="fox_bwd_dkv")
    nb = dcsp.shape[1]
    written = (jnp.arange(S) // (S // nb))[None, None, None, :] <= jnp.arange(nb)[None, :, None, None]
    dcs = jnp.sum(jnp.where(written, dcsp, 0.0), axis=1)
    dcs_tok = jnp.pad(dcs.reshape(FOX_HEADS, S).T, ((0, 0), (0, 128 - FOX_HEADS)))
    dff, dbias = _fox_gate_bwd(ff, bias, dcs_tok, name="fox_gate_bwd")
    dproj = jnp.concatenate([dhq, dhf, dhi, dhg, dq, dk, dv, dga, dgb], axis=1)
    d_wm = mm(n1, dproj, "tn", out_dtype=BF16, name="dw_in_main")
    d_wff = mm(n1, dff, "tn", out_dtype=BF16, name="dw_in_ff")
    dn1 = mm(dff, w["wff"], "nt", name="dn1_ff")
    late_parts = None
    if exchange_early:
        d_win = jnp.concatenate([d_wm[:, :FF_LO], d_wff[:, :FOX_HEADS], d_wm[:, FF_LO:]], axis=1)
        d_cw = jnp.concatenate([d_cwg, d_cwv], axis=1)
        comm = _ExchangeComm([_col_blocks(d_win, 1154), _col_blocks(d_cw, 704)])
        dn1, late_parts = mm(dproj, w["wm"], "nt", addend=dn1, comm=comm, name="dn1_main")
    else:
        dn1 = mm(dproj, w["wm"], "nt", addend=dn1, name="dn1_main")
    dx, d_norm_mix = _rms_bwd(x, p["norm_mix"], dn1, dh1, name="rms1_bwd")
    grads = dict(
        wm=d_wm, wff=d_wff, wa=d_wa, wb=d_wb, wo=d_wo, wug=d_wug, wuv=d_wuv, cwg=d_cwg, cwv=d_cwv, wd=d_wd,
        norm_mix=d_norm_mix.reshape(-1), fox_f_bias=dbias[0, :FOX_HEADS], hg_lb_logits=d_logits,
        hg_norm=jnp.sum(dgn8, axis=0).reshape(-1), norm_ffn=d_norm_ffn.reshape(-1), cbg=d_cbg, cbv=d_cbv,
        norm_final=d_norm_final.reshape(-1), early_parts=early_parts, late_parts=late_parts)
    return loss, dx, grads


MESH = pl.DeviceIdType.MESH
ANY = pl.BlockSpec(memory_space=pl.ANY)


def _all_gather(xs, *, name):
    def body(x_ref, out_ref, send_sems, recv_sems, local_sem):
        x, y, c = lax.axis_index("x"), lax.axis_index("y"), lax.axis_index("c")
        me, sibling = (x, y, c), (x, y, 1 - c)
        chips = [(1 - x, y), (x, 1 - y), (1 - x, 1 - y)]

        def rows(px, py, pc):
            return out_ref.at[4 * px + 2 * py + pc]

        def copy(k, block, to, src=None):
            return pltpu.make_async_remote_copy(
                src_ref=rows(*block) if src is None else src, dst_ref=rows(*block),
                send_sem=send_sems.at[k], recv_sem=recv_sems.at[k], device_id=to, device_id_type=MESH)

        mine = pltpu.make_async_copy(x_ref, rows(*me), local_sem)
        mine.start()
        first = [copy(0, me, sibling, src=x_ref)]
        first += [copy(1 + j, me, (*chip, c), src=x_ref) for j, chip in enumerate(chips)]
        for cp in first:
            cp.start()
        passed = [copy(4 + j, (*chip, c), sibling) for j, chip in enumerate(chips)]
        for j, chip in enumerate(chips):
            copy(1 + j, (*chip, c), me).wait_recv()
            passed[j].start()
        copy(0, sibling, me).wait_recv()
        for j, chip in enumerate(chips):
            copy(4 + j, (*chip, 1 - c), me).wait_recv()
        for cp in first + passed:
            cp.wait_send()
        mine.wait()

    return pl.pallas_call(
        body, name=name, in_specs=[ANY], out_specs=ANY,
        out_shape=jax.ShapeDtypeStruct((N_DEV,) + xs.shape, xs.dtype),
        scratch_shapes=[pltpu.SemaphoreType.DMA((7,)), pltpu.SemaphoreType.DMA((7,)), pltpu.SemaphoreType.DMA],
    )(xs)


def _exchange_blocks(g, *, name):
    def body(g_ref, out_ref, send_sems, recv_sems, local_sem):
        x, y, c = lax.axis_index("x"), lax.axis_index("y"), lax.axis_index("c")
        me = 4 * x + 2 * y + c
        mine = pltpu.make_async_copy(g_ref.at[me], out_ref.at[me], local_sem)
        mine.start()
        sends, recvs = [], []
        for k in range(1, N_DEV):
            px = 1 - x if k & 4 else x
            py = 1 - y if k & 2 else y
            pc = 1 - c if k & 1 else c
            p = 4 * px + 2 * py + pc
            sends.append(pltpu.make_async_remote_copy(
                src_ref=g_ref.at[p], dst_ref=out_ref.at[me], send_sem=send_sems.at[k - 1], recv_sem=recv_sems.at[k - 1],
                device_id=(px, py, pc), device_id_type=MESH))
            recvs.append(pltpu.make_async_remote_copy(
                src_ref=g_ref.at[p], dst_ref=out_ref.at[p], send_sem=send_sems.at[k - 1], recv_sem=recv_sems.at[k - 1],
                device_id=(px, py, pc), device_id_type=MESH))
        for cp in sends:
            cp.start()
        for cp in recvs:
            cp.wait_recv()
        for cp in sends:
            cp.wait_send()
        mine.wait()

    return pl.pallas_call(
        body, name=name, in_specs=[ANY], out_specs=ANY,
        out_shape=jax.ShapeDtypeStruct(g.shape, g.dtype),
        scratch_shapes=[pltpu.SemaphoreType.DMA((7,)), pltpu.SemaphoreType.DMA((7,)), pltpu.SemaphoreType.DMA],
    )(g)


def _adamw(parts, w, m, v, *, name, T=512):
    R, L = w.shape
    c1 = 1.0 / (1.0 - ADAM_B1 ** ADAM_STEP)
    c2 = 1.0 / (1.0 - ADAM_B2 ** ADAM_STEP)

    def body(p_ref, w_ref, m_ref, v_ref, g_ref, d_ref, nm_ref, nv_ref):
        g = p_ref[0]
        for s in range(1, N_DEV):
            g = g + p_ref[s]
        g_ref[...] = g
        nm = ADAM_B1 * m_ref[...] + (1.0 - ADAM_B1) * g
        nv = ADAM_B2 * v_ref[...] + (1.0 - ADAM_B2) * (g * g)
        nm_ref[...] = nm
        nv_ref[...] = nv
        d_ref[...] = -ADAM_LR * ((nm * c1) / (jnp.sqrt(nv * c2) + ADAM_EPS) + ADAM_WD * w_ref[...])

    blk = pl.BlockSpec((T, L), lambda i: (i, 0))
    out = jax.ShapeDtypeStruct((R, L), F32)
    return pl.pallas_call(
        body, name=name, grid=(R // T,),
        in_specs=[pl.BlockSpec((N_DEV, T, L), lambda i: (0, i, 0)), blk, blk, blk],
        out_specs=[blk, blk, blk, blk], out_shape=[out, out, out, out],
        compiler_params=_cparams(("parallel",)),
    )(parts, w, m, v)


D_IN = 9232
FF_LO, FF_HI = 7168, 7184
IN_SH, UP_SH, DOWN_SH = D_IN // N_DEV, 2 * D_FF // N_DEV, D_FF // N_DEV
SQ_SH = D_MODEL // N_DEV

BIG = [("w_in", (1, D_MODEL, IN_SH)), ("w_branch_a", (1, SQ_SH, D_MODEL)), ("w_branch_b", (1, SQ_SH, D_MODEL)),
       ("w_out", (1, SQ_SH, D_MODEL)), ("w_up", (1, D_MODEL, UP_SH)), ("conv_w", (1, 3, UP_SH)),
       ("w_down", (1, DOWN_SH, D_MODEL))]
SMALL = [("norm_mix", (1, D_MODEL)), ("fox_f_bias", (1, FOX_HEADS)), ("hg_lb_logits", (2, HG_HEADS * HG_DK)),
         ("hg_norm", (1, HG_DV)), ("norm_ffn", (1, D_MODEL)), ("conv_b", (1, 2 * D_FF)), ("norm_final", (D_MODEL,))]
NAMES = ["norm_mix", "w_in", "fox_f_bias", "hg_lb_logits", "hg_norm", "w_branch_a", "w_branch_b", "w_out",
         "norm_ffn", "w_up", "conv_w", "conv_b", "w_down", "norm_final"]


def _size(shape):
    n = 1
    for s in shape:
        n *= s
    return n


PACK_ROWS = 20992
GATHER_ROWS = 20800
assert sum(_size(s) for _, s in BIG + SMALL) <= PACK_ROWS * 128


def _pack_rows(flat_parts, rows):
    flat = jnp.concatenate(flat_parts, axis=-1)
    pad = rows * 128 - flat.shape[-1]
    flat = jnp.pad(flat, [(0, 0)] * (flat.ndim - 1) + [(0, pad)])
    return flat.reshape(flat.shape[:-1] + (rows, 128))


def _pack_shard(vals):
    return _pack_rows([vals[n].reshape(1, -1).astype(F32) for n, _ in BIG + SMALL], PACK_ROWS)[0]


def _unpack_shard(buf):
    flat = buf.reshape(-1)
    out, off = {}, 0
    for n, shape in BIG + SMALL:
        out[n] = flat[off:off + _size(shape)].reshape(shape)
        off += _size(shape)
    return out


def _cols_by_device(a, width):
    rows = a.shape[0]
    return a.reshape(rows, N_DEV, width).transpose(1, 0, 2).reshape(N_DEV, rows * width)


def _cols_from_devices(a, rows, width):
    return a.reshape(N_DEV, rows, width).transpose(1, 0, 2).reshape(rows, N_DEV * width)


def _pack_grads(g):
    w_in = jnp.concatenate([g["wm"][:, :FF_LO], g["wff"][:, :FOX_HEADS], g["wm"][:, FF_LO:]], axis=1)
    w_up = jnp.concatenate([g["wug"], g["wuv"]], axis=1)
    conv_w = jnp.concatenate([g["cwg"], g["cwv"]], axis=1)
    conv_b = jnp.concatenate([g["cbg"], g["cbv"]], axis=1)
    big = [_cols_by_device(w_in, IN_SH), g["wa"].reshape(N_DEV, -1), g["wb"].reshape(N_DEV, -1),
           g["wo"].reshape(N_DEV, -1), _cols_by_device(w_up, UP_SH), _cols_by_device(conv_w, UP_SH),
           g["wd"].reshape(N_DEV, -1)]
    small = [g["norm_mix"], g["fox_f_bias"], g["hg_lb_logits"], g["hg_norm"], g["norm_ffn"], conv_b, g["norm_final"]]
    small = [jnp.broadcast_to(s.reshape(1, -1), (N_DEV, s.size)) for s in small]
    return _pack_rows(big + small, PACK_ROWS)


def _gather_weights(w_in, w_a, w_b, w_o, w_up, conv_w, w_down):
    taps = lax.bitcast_convert_type(conv_w.reshape(3, UP_SH), BF16).reshape(1, -1)
    mats = [w_in, w_a, w_b, w_o, w_up, w_down]
    packed = _pack_rows([t.reshape(1, -1).astype(BF16) for t in mats] + [taps], GATHER_ROWS)[0]
    full = _all_gather(packed, name="gather_weights").reshape(N_DEV, -1)
    off = 0

    def take(n):
        nonlocal off
        piece = full[:, off:off + n]
        off += n
        return piece

    win = _cols_from_devices(take(D_MODEL * IN_SH), D_MODEL, IN_SH)
    wa = take(SQ_SH * D_MODEL).reshape(D_MODEL, D_MODEL)
    wb = take(SQ_SH * D_MODEL).reshape(D_MODEL, D_MODEL)
    wo = take(SQ_SH * D_MODEL).reshape(D_MODEL, D_MODEL)
    wup = _cols_from_devices(take(D_MODEL * UP_SH), D_MODEL, UP_SH)
    wd = take(DOWN_SH * D_MODEL).reshape(D_FF, D_MODEL)
    cw = lax.bitcast_convert_type(take(3 * UP_SH * 2).reshape(N_DEV, 3, UP_SH, 2), F32)
    cw = cw.transpose(1, 0, 2).reshape(3, 2 * D_FF)
    return dict(
        wm=jnp.concatenate([win[:, :FF_LO], win[:, FF_HI:]], axis=1),
        wff=jnp.pad(win[:, FF_LO:FF_HI], ((0, 0), (0, 128 - FOX_HEADS))),
        wa=wa, wb=wb, wo=wo, wug=wup[:, :D_FF], wuv=wup[:, D_FF:], cwg=cw[:, :D_FF], cwv=cw[:, D_FF:], wd=wd)


def _peer(k, x, y, c):
    return (1 - x if k & 4 else x, 1 - y if k & 2 else y, 1 - c if k & 1 else c)


def _gather_multi(shards, *, name):
    n = len(shards)

    def body(*refs):
        x_refs, out_refs = refs[:n], refs[n:2 * n]
        send_sems, recv_sems, local_sems = refs[2 * n:]
        x, y, c = lax.axis_index("x"), lax.axis_index("y"), lax.axis_index("c")
        me, sibling = (x, y, c), (x, y, 1 - c)
        chips = [(1 - x, y), (x, 1 - y), (1 - x, 1 - y)]

        def copy(t, k, block, to, src=None):
            slot = out_refs[t].at[4 * block[0] + 2 * block[1] + block[2]]
            return pltpu.make_async_remote_copy(
                src_ref=slot if src is None else src, dst_ref=slot,
                send_sem=send_sems.at[t, k], recv_sem=recv_sems.at[t, k], device_id=to, device_id_type=MESH)

        mine = [pltpu.make_async_copy(x_refs[t], out_refs[t].at[4 * x + 2 * y + c], local_sems.at[t]) for t in range(n)]
        for cp in mine:
            cp.start()
        first = []
        for t in range(n):
            first.append(copy(t, 0, me, sibling, src=x_refs[t]))
            first += [copy(t, 1 + j, me, (*chip, c), src=x_refs[t]) for j, chip in enumerate(chips)]
        for cp in first:
            cp.start()
        passed = []
        for j, chip in enumerate(chips):
            for t in range(n):
                copy(t, 1 + j, (*chip, c), me).wait_recv()
                passed.append(copy(t, 4 + j, (*chip, c), sibling))
                passed[-1].start()
        for t in range(n):
            copy(t, 0, sibling, me).wait_recv()
            for j, chip in enumerate(chips):
                copy(t, 4 + j, (*chip, 1 - c), me).wait_recv()
        for cp in first + passed:
            cp.wait_send()
        for cp in mine:
            cp.wait()

    return pl.pallas_call(
        body, name=name, in_specs=[ANY] * n, out_specs=[ANY] * n,
        out_shape=[jax.ShapeDtypeStruct((N_DEV,) + s.shape, s.dtype) for s in shards],
        scratch_shapes=[pltpu.SemaphoreType.DMA((n, 7)), pltpu.SemaphoreType.DMA((n, 7)), pltpu.SemaphoreType.DMA((n,))],
    )(*shards)


def _exchange_multi(blocks, *, name):
    n = len(blocks)

    def body(*refs):
        g_refs, out_refs = refs[:n], refs[n:2 * n]
        send_sems, recv_sems, local_sems = refs[2 * n:]
        x, y, c = lax.axis_index("x"), lax.axis_index("y"), lax.axis_index("c")
        me = 4 * x + 2 * y + c
        mine = [pltpu.make_async_copy(g_refs[t].at[me], out_refs[t].at[me], local_sems.at[t]) for t in range(n)]
        for cp in mine:
            cp.start()
        sends, recvs = [], []
        for k in range(1, N_DEV):
            px, py, pc = _peer(k, x, y, c)
            p = 4 * px + 2 * py + pc
            for t in range(n):
                sends.append(pltpu.make_async_remote_copy(
                    src_ref=g_refs[t].at[p], dst_ref=out_refs[t].at[me], send_sem=send_sems.at[t, k - 1],
                    recv_sem=recv_sems.at[t, k - 1], device_id=(px, py, pc), device_id_type=MESH))
                recvs.append(pltpu.make_async_remote_copy(
                    src_ref=g_refs[t].at[p], dst_ref=out_refs[t].at[p], send_sem=send_sems.at[t, k - 1],
                    recv_sem=recv_sems.at[t, k - 1], device_id=(px, py, pc), device_id_type=MESH))
        for cp in sends:
            cp.start()
        for cp in recvs:
            cp.wait_recv()
        for cp in sends:
            cp.wait_send()
        for cp in mine:
            cp.wait()

    return pl.pallas_call(
        body, name=name, in_specs=[ANY] * n, out_specs=[ANY] * n,
        out_shape=[jax.ShapeDtypeStruct(b.shape, b.dtype) for b in blocks],
        scratch_shapes=[pltpu.SemaphoreType.DMA((n, 7)), pltpu.SemaphoreType.DMA((n, 7)), pltpu.SemaphoreType.DMA((n,))],
    )(*blocks)


def _adamw2(parts, w, m, v, *, name, T):
    R, C = w.shape
    c1 = 1.0 / (1.0 - ADAM_B1 ** ADAM_STEP)
    c2 = 1.0 / (1.0 - ADAM_B2 ** ADAM_STEP)

    def body(p_ref, w_ref, m_ref, v_ref, g_ref, d_ref, nm_ref, nv_ref):
        g = p_ref[0].astype(F32)
        for s in range(1, N_DEV):
            g = g + p_ref[s].astype(F32)
        g_ref[...] = g
        nm = ADAM_B1 * m_ref[...] + (1.0 - ADAM_B1) * g
        nv = ADAM_B2 * v_ref[...] + (1.0 - ADAM_B2) * (g * g)
        nm_ref[...] = nm
        nv_ref[...] = nv
        d_ref[...] = -ADAM_LR * ((nm * c1) / (jnp.sqrt(nv * c2) + ADAM_EPS) + ADAM_WD * w_ref[...])

    blk = pl.BlockSpec((T, C), lambda i: (i, 0))
    out = jax.ShapeDtypeStruct((R, C), F32)
    return pl.pallas_call(
        body, name=name, grid=(R // T,),
        in_specs=[pl.BlockSpec((N_DEV, T, C), lambda i: (0, i, 0)), blk, blk, blk],
        out_specs=[blk, blk, blk, blk], out_shape=[out, out, out, out],
        compiler_params=_cparams(("parallel",)),
    )(parts, w, m, v)


SMALL_ROWS = 88
SHARDED = [("w_in", (D_MODEL, 1154), 256), ("w_branch_a", (128, D_MODEL), 128), ("w_branch_b", (128, D_MODEL), 128),
           ("w_out", (128, D_MODEL), 128), ("w_up", (D_MODEL, 704), 256), ("conv_w", (3, 704), 3),
           ("w_down", (352, D_MODEL), 352)]


def _col_blocks(a, width):
    return jnp.stack([a[:, d * width:(d + 1) * width] for d in range(N_DEV)])


def _pack_small(vals):
    flat = jnp.concatenate([vals[n].reshape(-1).astype(F32) for n, _ in SMALL])
    return jnp.pad(flat, (0, SMALL_ROWS * 128 - flat.shape[0])).reshape(SMALL_ROWS, 128)


def _unpack_small(buf):
    flat, out, off = buf.reshape(-1), {}, 0
    for n, shape in SMALL:
        out[n] = flat[off:off + _size(shape)].reshape(shape)
        off += _size(shape)
    return out


def _gather_weights2(w_in, w_a, w_b, w_o, w_up, conv_w, w_down):
    shards = [w_in[0].astype(BF16), w_a[0].astype(BF16), w_b[0].astype(BF16), w_o[0].astype(BF16),
              w_up[0].astype(BF16), conv_w[0], w_down[0].astype(BF16)]
    g_in, g_a, g_b, g_o, g_up, g_cw, g_d = _gather_multi(shards, name="gather_weights")
    win = jnp.concatenate([g_in[d] for d in range(N_DEV)], axis=1)
    wup = jnp.concatenate([g_up[d] for d in range(N_DEV)], axis=1)
    cw = jnp.concatenate([g_cw[d] for d in range(N_DEV)], axis=1)
    return dict(
        wm=jnp.concatenate([win[:, :FF_LO], win[:, FF_HI:]], axis=1),
        wff=jnp.pad(win[:, FF_LO:FF_HI], ((0, 0), (0, 128 - FOX_HEADS))),
        wa=g_a.reshape(D_MODEL, D_MODEL), wb=g_b.reshape(D_MODEL, D_MODEL), wo=g_o.reshape(D_MODEL, D_MODEL),
        wug=wup[:, :D_FF], wuv=wup[:, D_FF:], cwg=cw[:, :D_FF], cwv=cw[:, D_FF:], wd=g_d.reshape(D_FF, D_MODEL))


def _grad_blocks(g):
    w_in = jnp.concatenate([g["wm"][:, :FF_LO], g["wff"][:, :FOX_HEADS], g["wm"][:, FF_LO:]], axis=1)
    conv_w = jnp.concatenate([g["cwg"], g["cwv"]], axis=1).astype(F32)
    conv_b = jnp.concatenate([g["cbg"], g["cbv"]], axis=1)
    small = _pack_small(dict(norm_mix=g["norm_mix"], fox_f_bias=g["fox_f_bias"], hg_lb_logits=g["hg_lb_logits"],
                             hg_norm=g["hg_norm"], norm_ffn=g["norm_ffn"], conv_b=conv_b, norm_final=g["norm_final"]))
    up = jnp.stack([g["wug"][:, d * 704:(d + 1) * 704] for d in range(4)]
                   + [g["wuv"][:, d * 704:(d + 1) * 704] for d in range(4)])
    return [_col_blocks(w_in, 1154), g["wa"].reshape(N_DEV, 128, D_MODEL), g["wb"].reshape(N_DEV, 128, D_MODEL),
            g["wo"].reshape(N_DEV, 128, D_MODEL), up, _col_blocks(conv_w, 704), g["wd"].reshape(N_DEV, 352, D_MODEL),
            jnp.broadcast_to(small[None], (N_DEV, SMALL_ROWS, 128))]


def kernel(x, norm_mix, w_in,fox_f_bias, hg_lb_logits, hg_norm, w_branch_a, w_branch_b, w_out, norm_ffn, w_up, conv_w, conv_b, w_down, norm_final, loss_target, m_norm_mix, m_w_in, m_fox_f_bias, m_hg_lb_logits, m_hg_norm, m_w_branch_a, m_w_branch_b, m_w_out, m_norm_ffn, m_w_up, m_conv_w, m_conv_b, m_w_down, m_norm_final, v_norm_mix, v_w_in, v_fox_f_bias, v_hg_lb_logits, v_hg_norm, v_w_branch_a, v_w_branch_b, v_w_out, v_norm_ffn, v_w_up, v_conv_w, v_conv_b, v_w_down, v_norm_final):
    wv = dict(norm_mix=norm_mix, w_in=w_in, fox_f_bias=fox_f_bias, hg_lb_logits=hg_lb_logits, hg_norm=hg_norm,
              w_branch_a=w_branch_a, w_branch_b=w_branch_b, w_out=w_out, norm_ffn=norm_ffn, w_up=w_up, conv_w=conv_w,
              conv_b=conv_b, w_down=w_down, norm_final=norm_final)
    mv = dict(norm_mix=m_norm_mix, w_in=m_w_in, fox_f_bias=m_fox_f_bias, hg_lb_logits=m_hg_lb_logits, hg_norm=m_hg_norm,
              w_branch_a=m_w_branch_a, w_branch_b=m_w_branch_b, w_out=m_w_out, norm_ffn=m_norm_ffn, w_up=m_w_up,
              conv_w=m_conv_w, conv_b=m_conv_b, w_down=m_w_down, norm_final=m_norm_final)
    vv = dict(norm_mix=v_norm_mix, w_in=v_w_in, fox_f_bias=v_fox_f_bias, hg_lb_logits=v_hg_lb_logits, hg_norm=v_hg_norm,
              w_branch_a=v_w_branch_a, w_branch_b=v_w_branch_b, w_out=v_w_out, norm_ffn=v_norm_ffn, w_up=v_w_up,
              conv_w=v_conv_w, conv_b=v_conv_b, w_down=v_w_down, norm_final=v_norm_final)

    (g_in,) = _comm_call(_GatherComm([w_in[0].astype(BF16)]), name="gather_w_in")
    win = jnp.concatenate([g_in[d] for d in range(N_DEV)], axis=1)
    w = dict(wm=jnp.concatenate([win[:, :FF_LO], win[:, FF_HI:]], axis=1),
             wff=jnp.pad(win[:, FF_LO:FF_HI], ((0, 0), (0, 128 - FOX_HEADS))))
    late = _GatherComm([w_branch_a[0].astype(BF16), w_branch_b[0].astype(BF16), w_out[0].astype(BF16),
                        w_up[0].astype(BF16), conv_w[0], w_down[0].astype(BF16)])
    p = dict(norm_mix=norm_mix[0], fox_f_bias=fox_f_bias[0], hg_lb_logits=hg_lb_logits, hg_norm=hg_norm[0],
             norm_ffn=norm_ffn[0], cbg=conv_b[:, :D_FF], cbv=conv_b[:, D_FF:], norm_final=norm_final)
    loss, dx, grads = _local_step(x[0], loss_target[0], w, p, late=late, exchange_early=True)
    loss = lax.psum(loss[0, 0], ("x", "y", "c"))

    small = _pack_small(dict(
        norm_mix=grads["norm_mix"], fox_f_bias=grads["fox_f_bias"], hg_lb_logits=grads["hg_lb_logits"],
        hg_norm=grads["hg_norm"], norm_ffn=grads["norm_ffn"], conv_b=jnp.concatenate([grads["cbg"], grads["cbv"]], axis=1),
        norm_final=grads["norm_final"]))
    (small_parts,) = _comm_call(_ExchangeComm([jnp.broadcast_to(small[None], (N_DEV, SMALL_ROWS, 128))]),
                                name="exchange_small")
    ea, eb, eo, eup, ed = grads["early_parts"]
    p_in, p_cw = grads["late_parts"]
    parts = [p_in, ea, eb, eo, eup, p_cw, ed, small_parts]
    res = {}
    for (n, shape, tile), part in zip(SHARDED, parts):
        outs = _adamw2(part, wv[n].reshape(shape), mv[n].reshape(shape), vv[n].reshape(shape), name="adamw_" + n, T=tile)
        res[n] = [o.reshape(wv[n].shape) for o in outs]
    outs = _adamw2(parts[-1], _pack_small(wv), _pack_small(mv), _pack_small(vv), name="adamw_small", T=SMALL_ROWS)
    small = [_unpack_small(o) for o in outs]
    for n, _ in SMALL:
        res[n] = [s[n] for s in small]
    return (loss, dx[None], *[res[n][0] for n in NAMES], *[res[n][1] for n in NAMES],
            *[res[n][2] for n in NAMES], *[res[n][3] for n in NAMES])


def _lb_fwd(logits, *, name):
    def body(l_ref, lb_ref):
        lb_ref[...] = _sigmoid(l_ref[0:1, :] - l_ref[1:2, :])

    return pl.pallas_call(body, name=name, out_shape=jax.ShapeDtypeStruct((1, logits.shape[1]), F32))(logits)


def _lb_bwd(logits, dlb, *, name):
    def body(l_ref, d_ref, o_ref):
        lbv = _sigmoid(l_ref[0:1, :] - l_ref[1:2, :])
        t = d_ref[...] * lbv * (1.0 - lbv)
        o_ref[0:1, :] = t
        o_ref[1:2, :] = -t

    return pl.pallas_call(body, name=name, out_shape=jax.ShapeDtypeStruct(logits.shape, F32))(logits, dlb)
```

```python
import numpy as np
import jax
import jax.numpy as jnp
from jax import lax
from jax.experimental import pallas as pl
from jax.experimental.pallas import tpu as pltpu

F32 = jnp.float32
BF16 = jnp.bfloat16

D_MODEL = 1024
HG_HEADS = 8
HG_DK = 128
HG_DV = 128
HG_CHUNK = 64
FOX_HEADS = 16
FOX_DH = 64
D_FF = 2816
EPS = 1e-6
N_DEV = 8

ADAM_LR = 0.001
ADAM_B1 = 0.9
ADAM_B2 = 0.999
ADAM_EPS = 1e-08
ADAM_WD = 0.01
ADAM_STEP = 10

VMEM_LIMIT = 56 * 1024 * 1024


def _cparams(sem):
    return pltpu.CompilerParams(dimension_semantics=sem, vmem_limit_bytes=VMEM_LIMIT)


MESH = pl.DeviceIdType.MESH
ANY = pl.BlockSpec(memory_space=pl.ANY)


class _GatherComm:
    def __init__(self, shards):
        self.inputs = list(shards)
        n = self.n = len(shards)
        self.out_shapes = [jax.ShapeDtypeStruct((N_DEV,) + s.shape, s.dtype) for s in shards]
        self.scratch = [pltpu.SemaphoreType.DMA((n, 7)), pltpu.SemaphoreType.DMA((n, 7)), pltpu.SemaphoreType.DMA((n,))]

    def _parts(self, x_refs, out_refs, sems):
        send_sems, recv_sems, local_sems = sems
        x, y, c = lax.axis_index("x"), lax.axis_index("y"), lax.axis_index("c")
        me, sibling = (x, y, c), (x, y, 1 - c)
        chips = [(1 - x, y), (x, 1 - y), (1 - x, 1 - y)]

        def copy(t, k, block, to, src=None):
            slot = out_refs[t].at[4 * block[0] + 2 * block[1] + block[2]]
            return pltpu.make_async_remote_copy(
                src_ref=slot if src is None else src, dst_ref=slot,
                send_sem=send_sems.at[t, k], recv_sem=recv_sems.at[t, k], device_id=to, device_id_type=MESH)

        mine = [pltpu.make_async_copy(x_refs[t], out_refs[t].at[4 * x + 2 * y + c], local_sems.at[t])
                for t in range(self.n)]
        first = []
        for t in range(self.n):
            first.append(copy(t, 0, me, sibling, src=x_refs[t]))
            first += [copy(t, 1 + j, me, (*chip, c), src=x_refs[t]) for j, chip in enumerate(chips)]
        return c, me, sibling, chips, copy, mine, first

    def start(self, x_refs, out_refs, sems):
        _, _, _, _, _, mine, first = self._parts(x_refs, out_refs, sems)
        for cp in mine + first:
            cp.start()

    def finish(self, x_refs, out_refs, sems):
        c, me, sibling, chips, copy, mine, first = self._parts(x_refs, out_refs, sems)
        passed = []
        for j, chip in enumerate(chips):
            for t in range(self.n):
                copy(t, 1 + j, (*chip, c), me).wait_recv()
                passed.append(copy(t, 4 + j, (*chip, c), sibling))
                passed[-1].start()
        for t in range(self.n):
            copy(t, 0, sibling, me).wait_recv()
            for j, chip in enumerate(chips):
                copy(t, 4 + j, (*chip, 1 - c), me).wait_recv()
        for cp in first + passed:
            cp.wait_send()
        for cp in mine:
            cp.wait()


class _ExchangeComm:
    def __init__(self, blocks):
        self.inputs = list(blocks)
        n = self.n = len(blocks)
        self.out_shapes = [jax.ShapeDtypeStruct(b.shape, b.dtype) for b in blocks]
        self.scratch = [pltpu.SemaphoreType.DMA((n, 7)), pltpu.SemaphoreType.DMA((n, 7)), pltpu.SemaphoreType.DMA((n,))]

    def _parts(self, g_refs, out_refs, sems):
        send_sems, recv_sems, local_sems = sems
        x, y, c = lax.axis_index("x"), lax.axis_index("y"), lax.axis_index("c")
        me = 4 * x + 2 * y + c
        mine = [pltpu.make_async_copy(g_refs[t].at[me], out_refs[t].at[me], local_sems.at[t]) for t in range(self.n)]
        sends, recvs = [], []
        for k in range(1, N_DEV):
            px = 1 - x if k & 4 else x
            py = 1 - y if k & 2 else y
            pc = 1 - c if k & 1 else c
            p = 4 * px + 2 * py + pc
            for t in range(self.n):
                sends.append(pltpu.make_async_remote_copy(
                    src_ref=g_refs[t].at[p], dst_ref=out_refs[t].at[me], send_sem=send_sems.at[t, k - 1],
                    recv_sem=recv_sems.at[t, k - 1], device_id=(px, py, pc), device_id_type=MESH))
                recvs.append(pltpu.make_async_remote_copy(
                    src_ref=g_refs[t].at[p], dst_ref=out_refs[t].at[p], send_sem=send_sems.at[t, k - 1],
                    recv_sem=recv_sems.at[t, k - 1], device_id=(px, py, pc), device_id_type=MESH))
        return mine, sends, recvs

    def start(self, g_refs, out_refs, sems):
        mine, sends, _ = self._parts(g_refs, out_refs, sems)
        for cp in mine + sends:
            cp.start()

    def finish(self, g_refs, out_refs, sems):
        mine, sends, recvs = self._parts(g_refs, out_refs, sems)
        for cp in recvs:
            cp.wait_recv()
        for cp in sends:
            cp.wait_send()
        for cp in mine:
            cp.wait()


def _comm_call(comm, *, name):
    n = comm.n

    def body(*refs):
        comm.start(refs[:n], refs[n:2 * n], refs[2 * n:])
        comm.finish(refs[:n], refs[n:2 * n], refs[2 * n:])

    return pl.pallas_call(body, name=name, in_specs=[ANY] * n, out_specs=[ANY] * n, out_shape=comm.out_shapes,
                          scratch_shapes=comm.scratch)(*comm.inputs)


_DIMS = {
    "nn": (((1,), (0,)), ((), ())),
    "nt": (((1,), (1,)), ((), ())),
    "tn": (((0,), (0,)), ((), ())),
}

MATMUL_VMEM_BUDGET = 36 * 1024 * 1024
MAX_TILE = 1536


def _pick(n, prefs):
    for p in prefs:
        if n % p == 0:
            return p
    return n


def _tile_options(n):
    return [d for d in range(128, min(n, MAX_TILE) + 1, 128) if n % d == 0] or [n]


def _pick_tiles(M, N, tk, nk, sa, sb, so, has_addend, tm, tn):
    best = None
    for cm in ([tm] if tm else _tile_options(M)):
        for cn in ([tn] if tn else _tile_options(N)):
            need = 2 * (cm * tk * sa + tk * cn * sb + cm * cn * so + (cm * cn * 4 if has_addend else 0))
            need += cm * cn * 4 if nk > 1 else 0
            if need <= MATMUL_VMEM_BUDGET and (best is None or cm * cn > best[0] * best[1]
                                               or (cm * cn == best[0] * best[1] and cn > best[1])):
                best = (cm, cn)
    assert best is not None, (M, N, tk)
    return best


def _matmul(a, b, form, *, out_dtype=F32, addend=None, tm=None, tn=None, tk=None, comm=None, name):
    if form == "nn":
        (M, K), (K2, N) = a.shape, b.shape
    elif form == "nt":
        (M, K), (N, K2) = a.shape, b.shape
    else:
        (K, M), (K2, N) = a.shape, b.shape
    assert K == K2, (a.shape, b.shape, form)
    tk = tk or (K if K <= 2816 else _pick(K, (1024, 512, 256, 128)))
    nk = K // tk
    if tm is None or tn is None:
        tm, tn = _pick_tiles(M, N, tk, nk, a.dtype.itemsize, b.dtype.itemsize, jnp.dtype(out_dtype).itemsize,
                             addend is not None, tm, tn)
    assert M % tm == 0 and N % tn == 0 and K % tk == 0, (M, N, K, tm, tn, tk)
    dims = _DIMS[form]
    nc = comm.n if comm is not None else 0
    grid = (M // tm, N // tn, nk)

    def body(*refs):
        a_ref, b_ref = refs[:2]
        pos = 2
        add_ref = refs[pos] if addend is not None else None
        pos += addend is not None
        c_in, o_ref, c_out = refs[pos:pos + nc], refs[pos + nc], refs[pos + nc + 1:pos + 2 * nc + 1]
        pos += 2 * nc + 1
        acc_ref = refs[pos] if nk > 1 else None
        c_sems = refs[pos + (nk > 1):]
        if comm is not None:
            ids = [pl.program_id(d) for d in range(3)]

            @pl.when((ids[0] == 0) & (ids[1] == 0) & (ids[2] == 0))
            def _():
                comm.start(c_in, c_out, c_sems)

        def finish(r):
            if add_ref is not None:
                r = r + add_ref[...].astype(F32)
            o_ref[...] = r.astype(o_ref.dtype)

        part = lax.dot_general(a_ref[...].astype(BF16), b_ref[...].astype(BF16), dims, preferred_element_type=F32)
        if nk == 1:
            finish(part)
        else:
            k = pl.program_id(2)

            @pl.when(k == 0)
            def _():
                acc_ref[...] = part

            @pl.when(k > 0)
            def _():
                acc_ref[...] += part

            @pl.when(k == nk - 1)
            def _():
                finish(acc_ref[...])

        if comm is not None:
            @pl.when((ids[0] == grid[0] - 1) & (ids[1] == grid[1] - 1) & (ids[2] == grid[2] - 1))
            def _():
                comm.finish(c_in, c_out, c_sems)

    if form == "nn":
        a_spec = pl.BlockSpec((tm, tk), lambda i, j, k: (i, k))
        b_spec = pl.BlockSpec((tk, tn), lambda i, j, k: (k, j))
    elif form == "nt":
        a_spec = pl.BlockSpec((tm, tk), lambda i, j, k: (i, k))
        b_spec = pl.BlockSpec((tn, tk), lambda i, j, k: (j, k))
    else:
        a_spec = pl.BlockSpec((tk, tm), lambda i, j, k: (k, i))
        b_spec = pl.BlockSpec((tk, tn), lambda i, j, k: (k, j))
    o_spec = pl.BlockSpec((tm, tn), lambda i, j, k: (i, j))
    in_specs = [a_spec, b_spec] + ([o_spec] if addend is not None else [])
    args = (a, b) + ((addend,) if addend is not None else ())
    out_shape = jax.ShapeDtypeStruct((M, N), out_dtype)
    scratch = [pltpu.VMEM((tm, tn), F32)] if nk > 1 else []
    if comm is None:
        return pl.pallas_call(
            body, name=name, grid=grid, in_specs=in_specs, out_specs=o_spec, out_shape=out_shape,
            scratch_shapes=scratch, compiler_params=_cparams(("parallel", "parallel", "arbitrary")),
        )(*args)
    outs = pl.pallas_call(
        body, name=name, grid=grid, in_specs=in_specs + [ANY] * nc, out_specs=[o_spec] + [ANY] * nc,
        out_shape=[out_shape] + comm.out_shapes, scratch_shapes=scratch + comm.scratch,
        compiler_params=_cparams(("arbitrary", "arbitrary", "arbitrary")),
    )(*args, *comm.inputs)
    return outs[0], outs[1:]


def _rms_fwd(x, g, *, name, tm=512):
    M, D = x.shape
    tm = min(tm, M)

    def body(x_ref, g_ref, n_ref):
        xf = x_ref[...]
        r = lax.rsqrt(jnp.mean(xf * xf, axis=-1, keepdims=True) + EPS)
        n_ref[...] = (xf * r * g_ref[...]).astype(n_ref.dtype)

    return pl.pallas_call(
        body, name=name, grid=(M // tm,),
        in_specs=[pl.BlockSpec((tm, D), lambda i: (i, 0)), pl.BlockSpec((1, D), lambda i: (0, 0))],
        out_specs=pl.BlockSpec((tm, D), lambda i: (i, 0)),
        out_shape=jax.ShapeDtypeStruct((M, D), BF16),
        compiler_params=_cparams(("parallel",)),
    )(x, g.reshape(1, D))


def _rms_bwd(x, g, dn, dres, *, name, tm=512):
    M, D = x.shape
    tm = min(tm, M)

    def body(x_ref, g_ref, dn_ref, dres_ref, dx_ref, dg_ref):
        @pl.when(pl.program_id(0) == 0)
        def _():
            dg_ref[...] = jnp.zeros_like(dg_ref)

        xf = x_ref[...]
        r = lax.rsqrt(jnp.mean(xf * xf, axis=-1, keepdims=True) + EPS)
        xh = xf * r
        dn_ = dn_ref[...].astype(F32)
        dg_ref[...] += jnp.sum(dn_ * xh, axis=0, keepdims=True)
        dxh = dn_ * g_ref[...]
        dx = r * (dxh - xh * jnp.mean(dxh * xh, axis=-1, keepdims=True))
        dx_ref[...] = dres_ref[...] + dx

    row = pl.BlockSpec((tm, D), lambda i: (i, 0))
    vec = pl.BlockSpec((1, D), lambda i: (0, 0))
    return pl.pallas_call(
        body, name=name, grid=(M // tm,),
        in_specs=[row, vec, row, row], out_specs=[row, vec],
        out_shape=[jax.ShapeDtypeStruct((M, D), F32), jax.ShapeDtypeStruct((1, D), F32)],
        compiler_params=_cparams(("arbitrary",)),
    )(x, g.reshape(1, D), dn, dres)


def _loss_head(h, g, tgt, *, name, tm=512):
    M, D = h.shape
    tm = min(tm, M)

    def body(h_ref, g_ref, t_ref, loss_ref, dh_ref, dg_ref):
        @pl.when(pl.program_id(0) == 0)
        def _():
            dg_ref[...] = jnp.zeros_like(dg_ref)
            loss_ref[...] = jnp.zeros_like(loss_ref)

        xf = h_ref[...]
        r = lax.rsqrt(jnp.mean(xf * xf, axis=-1, keepdims=True) + EPS)
        xh = xf * r
        err = xh * g_ref[...] - t_ref[...]
        part = jnp.sum(jnp.mean(err * err, axis=-1, keepdims=True), axis=0, keepdims=True)
        loss_ref[...] += 0.5 * part
        dy = err * (1.0 / D)
        dg_ref[...] += jnp.sum(dy * xh, axis=0, keepdims=True)
        dxh = dy * g_ref[...]
        dh_ref[...] = r * (dxh - xh * jnp.mean(dxh * xh, axis=-1, keepdims=True))

    row = pl.BlockSpec((tm, D), lambda i: (i, 0))
    vec = pl.BlockSpec((1, D), lambda i: (0, 0))
    one = pl.BlockSpec((1, 1), lambda i: (0, 0))
    return pl.pallas_call(
        body, name=name, grid=(M // tm,),
        in_specs=[row, vec, row], out_specs=[one, row, vec],
        out_shape=[jax.ShapeDtypeStruct((1, 1), F32), jax.ShapeDtypeStruct((M, D), F32),
                   jax.ShapeDtypeStruct((1, D), F32)],
        compiler_params=_cparams(("arbitrary",)),
    )(h, g.reshape(1, D), tgt)


HG_MID = HG_CHUNK // 2 - 1
EXP_CAP = 80.0


def _sigmoid(x):
    return 1.0 / (1.0 + jnp.exp(-x))


def _dot(a, b, dims, precision=None):
    return lax.dot_general(a, b, dims, preferred_element_type=F32, precision=precision)


def _bdot(a, b, form):
    return _dot(a.astype(BF16), b.astype(BF16), _DIMS[form])


def _split2(x):
    hi = x.astype(BF16)
    return hi, (x - hi.astype(F32)).astype(BF16)


def _dot3(a, b, form):
    d = _DIMS[form]
    return _dot(a[0], b[0], d) + (_dot(a[0], b[1], d) + _dot(a[1], b[0], d))


def _hgrn_chunk_common(hq, hf, lbv, tril, rid):
    sq = _sigmoid(hq)
    q = hq * sq
    sg = _sigmoid(hf)
    f = lbv + (1.0 - lbv) * sg
    k = (1.0 - lbv) * (1.0 - sg)
    g = jnp.log(f)
    b = _dot(tril, g, _DIMS["nn"], precision=lax.Precision.HIGHEST)
    bref = jnp.sum(jnp.where(rid == HG_MID, b, 0.0), axis=0, keepdims=True)
    bend = jnp.sum(jnp.where(rid == HG_CHUNK - 1, b, 0.0), axis=0, keepdims=True)
    eb = jnp.exp(b)
    e1 = jnp.exp(jnp.minimum(b - bref, EXP_CAP))
    e2 = jnp.exp(jnp.minimum(bref - b, EXP_CAP))
    e3 = jnp.exp(bend - b)
    return sq, q, sg, f, k, bend, eb, e1, e2, e3


def _hgrn_fwd(proj, lb, gnorm, *, name, T=512):
    S = proj.shape[0]
    T = min(T, S)
    nch = T // HG_CHUNK
    C = HG_CHUNK

    def body(hq_ref, hf_ref, hi_ref, hg_ref, lb_ref, gn_ref, o_ref, oa_ref, st_ref, state):
        @pl.when(pl.program_id(1) == 0)
        def _():
            state[...] = jnp.zeros_like(state)

        lbv = lb_ref[...]
        gn = gn_ref[...]
        row = lax.broadcasted_iota(jnp.int32, (C, C), 0)
        col = lax.broadcasted_iota(jnp.int32, (C, C), 1)
        causal = row >= col
        tril = causal.astype(F32)
        rid = lax.broadcasted_iota(jnp.int32, (C, HG_DK), 0)
        sls = [pl.ds(c * C, C) for c in range(nch)]
        pre = [_hgrn_chunk_common(hq_ref[sl, :], hf_ref[sl, :], lbv, tril, rid) for sl in sls]
        v_l = [hi_ref[sl, :].astype(BF16) for sl in sls]
        a_l, u_l = [], []
        for c in range(nch):
            _, q, _, _, k, _, _, e1, e2, e3 = pre[c]
            a_l.append(jnp.where(causal, _bdot(q * e1, k * e2, "nt"), 0.0))
            u_l.append(_bdot(v_l[c], k * e3, "tn"))
        o_l = [_bdot(a_l[c], v_l[c], "nn") for c in range(nch)]
        st = state[...]
        st_l = []
        for c in range(nch):
            st_l.append(st)
            st = st * jnp.exp(pre[c][5]) + u_l[c]
        state[...] = st
        for c in range(nch):
            st_ref[0, c] = st_l[c]
            o_l[c] = o_l[c] + _bdot(pre[c][1] * pre[c][6], st_l[c], "nt")
        for c in range(nch):
            o, hg = o_l[c], hg_ref[sls[c], :]
            o_ref[sls[c], :] = o
            r = lax.rsqrt(jnp.mean(o * o, axis=-1, keepdims=True) + EPS)
            oa_ref[sls[c], :] = (o * r * gn * (hg * _sigmoid(hg))).astype(oa_ref.dtype)

    def grp(gidx):
        return pl.BlockSpec((T, 128), lambda h, t: (t, gidx * 8 + h))

    return pl.pallas_call(
        body, name=name, grid=(HG_HEADS, S // T),
        in_specs=[grp(0), grp(1), grp(2), grp(3),
                  pl.BlockSpec((1, 128), lambda h, t: (0, h)), pl.BlockSpec((1, 128), lambda h, t: (0, 0))],
        out_specs=[pl.BlockSpec((T, 128), lambda h, t: (t, h)), pl.BlockSpec((T, 128), lambda h, t: (t, h)),
                   pl.BlockSpec((1, nch, HG_DV, HG_DK), lambda h, t: (h, t, 0, 0))],
        out_shape=[jax.ShapeDtypeStruct((S, HG_HEADS * HG_DV), F32), jax.ShapeDtypeStruct((S, HG_HEADS * HG_DV), BF16),
                   jax.ShapeDtypeStruct((HG_HEADS, S // C, HG_DV, HG_DK), F32)],
        scratch_shapes=[pltpu.VMEM((HG_DV, HG_DK), F32)],
        compiler_params=_cparams(("parallel", "arbitrary")),
    )(proj, proj, proj, proj, lb, gnorm)


def _hgrn_bwd(proj, lb, gnorm, o, states, doa, *, name, T=512):
    S = proj.shape[0]
    T = min(T, S)
    nch = T // HG_CHUNK
    C = HG_CHUNK
    nT = S // T

    def body(hq_ref, hf_ref, hi_ref, hg_ref, lb_ref, gn_ref, o_ref, st_ref, doa_ref,
             dhq_ref, dhf_ref, dhi_ref, dhg_ref, dlb_ref, dgn_ref, dstate):
        @pl.when(pl.program_id(1) == 0)
        def _():
            dstate[...] = jnp.zeros_like(dstate)
            dlb_ref[...] = jnp.zeros_like(dlb_ref)
            dgn_ref[...] = jnp.zeros_like(dgn_ref)

        lbv = lb_ref[...]
        gn = gn_ref[...]
        row = lax.broadcasted_iota(jnp.int32, (C, C), 0)
        col = lax.broadcasted_iota(jnp.int32, (C, C), 1)
        causal = row >= col
        tril = causal.astype(F32)
        triu = (row <= col).astype(F32)
        rid = lax.broadcasted_iota(jnp.int32, (C, HG_DK), 0)
        rng = range(nch)
        sls = [pl.ds(c * C, C) for c in rng]
        pre = [_hgrn_chunk_common(hq_ref[sl, :], hf_ref[sl, :], lbv, tril, rid) for sl in sls]
        do2, dgn_acc = [], jnp.zeros((1, HG_DV), F32)
        for c in rng:
            hg, ov = hg_ref[sls[c], :], o_ref[sls[c], :]
            r = lax.rsqrt(jnp.mean(ov * ov, axis=-1, keepdims=True) + EPS)
            xh = ov * r
            sgg = _sigmoid(hg)
            d_oa = doa_ref[sls[c], :].astype(F32)
            dz = d_oa * (hg * sgg)
            dhg_ref[sls[c], :] = (d_oa * (xh * gn) * (sgg * (1.0 + hg * (1.0 - sgg)))).astype(dhg_ref.dtype)
            dgn_acc = dgn_acc + jnp.sum(dz * xh, axis=0, keepdims=True)
            dxh = dz * gn
            do2.append(_split2(r * (dxh - xh * jnp.mean(dxh * xh, axis=-1, keepdims=True))))
        dgn_ref[0] += dgn_acc
        qi = [pre[c][1] * pre[c][6] for c in rng]
        qp = [pre[c][1] * pre[c][7] for c in rng]
        kp = [pre[c][4] * pre[c][8] for c in rng]
        kend = [pre[c][4] * pre[c][9] for c in rng]
        qi2, qp2, kp2, kend2 = ([_split2(t) for t in lst] for lst in (qi, qp, kp, kend))
        v2 = [_split2(hi_ref[sl, :]) for sl in sls]
        st0 = [st_ref[0, c] for c in rng]
        a_l = [jnp.where(causal, _dot(qp2[c][0], kp2[c][0], _DIMS["nt"]), 0.0).astype(BF16) for c in rng]
        da2 = [_split2(jnp.where(causal, _dot3(do2[c], v2[c], "nt"), 0.0)) for c in rng]
        dqi = [_dot3(do2[c], _split2(st0[c]), "nn") for c in rng]
        w_l = [_dot3(do2[c], qi2[c], "tn") for c in rng]
        ds = dstate[...]
        ds1 = [None] * nch
        for c in reversed(rng):
            ds1[c] = ds
            ds = ds * jnp.exp(pre[c][5]) + w_l[c]
        dstate[...] = ds
        ds12 = [_split2(t) for t in ds1]
        dqp = [_dot3(da2[c], kp2[c], "nn") for c in rng]
        dkp = [_dot3(da2[c], qp2[c], "tn") for c in rng]
        dv = [_dot(a_l[c], do2[c][0], _DIMS["tn"]) + _dot(kend2[c][0], ds12[c][0], _DIMS["nt"]) for c in rng]
        dkend = [_dot3(v2[c], ds12[c], "nn") for c in rng]
        dq_l, dk_l, db_l = [], [], []
        for c in rng:
            _, _, _, _, _, bend, eb, e1, e2, e3 = pre[c]
            dq_l.append(dqi[c] * eb + dqp[c] * e1)
            dk_l.append(dkp[c] * e2 + dkend[c] * e3)
            db = dqi[c] * qi[c] + dqp[c] * qp[c] - dkp[c] * kp[c] - dkend[c] * kend[c]
            dbend = (jnp.sum(dkend[c] * kend[c], axis=0, keepdims=True)
                     + jnp.exp(bend) * jnp.sum(ds1[c] * st0[c], axis=0, keepdims=True))
            db_l.append(db + jnp.where(rid == C - 1, dbend, 0.0))
        dg = [_dot(triu, db_l[c], _DIMS["nn"], precision=lax.Precision.HIGHEST) for c in rng]
        dlb_acc = jnp.zeros((1, HG_DK), F32)
        for c in rng:
            sq, _, sg, f, _, _, _, _, _, _ = pre[c]
            hq = hq_ref[sls[c], :]
            df = dg[c] / f - dk_l[c]
            dlb_acc = dlb_acc + jnp.sum(df * (1.0 - sg), axis=0, keepdims=True)
            dhf_ref[sls[c], :] = (df * (1.0 - lbv) * sg * (1.0 - sg)).astype(dhf_ref.dtype)
            dhq_ref[sls[c], :] = (dq_l[c] * (sq * (1.0 + hq * (1.0 - sq)))).astype(dhq_ref.dtype)
            dhi_ref[sls[c], :] = dv[c].astype(dhi_ref.dtype)
        dlb_ref[...] += dlb_acc

    def grp(gidx):
        return pl.BlockSpec((T, 128), lambda h, t: (nT - 1 - t, gidx * 8 + h))

    tok = pl.BlockSpec((T, 128), lambda h, t: (nT - 1 - t, h))
    big = jax.ShapeDtypeStruct((S, HG_HEADS * HG_DV), BF16)
    return pl.pallas_call(
        body, name=name, grid=(HG_HEADS, nT),
        in_specs=[grp(0), grp(1), grp(2), grp(3),
                  pl.BlockSpec((1, 128), lambda h, t: (0, h)), pl.BlockSpec((1, 128), lambda h, t: (0, 0)),
                  tok, pl.BlockSpec((1, nch, HG_DV, HG_DK), lambda h, t: (h, nT - 1 - t, 0, 0)), tok],
        out_specs=[tok, tok, tok, tok, pl.BlockSpec((1, 128), lambda h, t: (0, h)),
                   pl.BlockSpec((1, 1, 128), lambda h, t: (h, 0, 0))],
        out_shape=[big, big, big, big, jax.ShapeDtypeStruct((1, HG_HEADS * HG_DK), F32),
                   jax.ShapeDtypeStruct((HG_HEADS, 1, HG_DV), F32)],
        scratch_shapes=[pltpu.VMEM((HG_DV, HG_DK), F32)],
        compiler_params=_cparams(("parallel", "arbitrary")),
    )(proj, proj, proj, proj, lb, gnorm, o, states, doa)


def _lb_fwd(logits, *, name):
    def body(l_ref, lb_ref):
        lb_ref[...] = _sigmoid(l_ref[0:1, :] - l_ref[1:2, :])

    return pl.pallas_call(body, name=name, out_shape=jax.ShapeDtypeStruct((1, logits.shape[1]), F32))(logits)


def _lb_bwd(logits, dlb, *, name):
    def body(l_ref, d_ref, o_ref):
        lbv = _sigmoid(l_ref[0:1, :] - l_ref[1:2, :])
        t = d_ref[...] * lbv * (1.0 - lbv)
        o_ref[0:1, :] = t
        o_ref[1:2, :] = -t

    return pl.pallas_call(body, name=name, out_shape=jax.ShapeDtypeStruct(logits.shape, F32))(logits, dlb)


NEG = -1e30
FOX_SCALE = FOX_DH ** -0.5
FOX_PAIRS = FOX_HEADS // 2


def _fox_gate_fwd(ff, bias, *, name, T=512):
    S = ff.shape[0]
    T = min(T, S)

    def body(ff_ref, b_ref, c_ref, carry):
        @pl.when(pl.program_id(0) == 0)
        def _():
            carry[...] = jnp.zeros_like(carry)

        z = ff_ref[...] + b_ref[...]
        logf = jnp.minimum(z, 0.0) - jnp.log(1.0 + jnp.exp(-jnp.abs(z)))
        row = lax.broadcasted_iota(jnp.int32, (T, T), 0)
        col = lax.broadcasted_iota(jnp.int32, (T, T), 1)
        c = _dot((row >= col).astype(F32), logf, _DIMS["nn"], precision=lax.Precision.HIGHEST) + carry[...]
        c_ref[...] = c
        carry[...] = c[T - 1:T, :]

    return pl.pallas_call(
        body, name=name, grid=(S // T,),
        in_specs=[pl.BlockSpec((T, 128), lambda i: (i, 0)), pl.BlockSpec((1, 128), lambda i: (0, 0))],
        out_specs=pl.BlockSpec((T, 128), lambda i: (i, 0)),
        out_shape=jax.ShapeDtypeStruct((S, 128), F32),
        scratch_shapes=[pltpu.VMEM((1, 128), F32)],
        compiler_params=_cparams(("arbitrary",)),
    )(ff, bias)


def _fox_gate_bwd(ff, bias, dcs, *, name, T=512):
    S = ff.shape[0]
    T = min(T, S)
    nT = S // T

    def body(ff_ref, b_ref, d_ref, dff_ref, db_ref, carry):
        @pl.when(pl.program_id(0) == 0)
        def _():
            carry[...] = jnp.zeros_like(carry)
            db_ref[...] = jnp.zeros_like(db_ref)

        row = lax.broadcasted_iota(jnp.int32, (T, T), 0)
        col = lax.broadcasted_iota(jnp.int32, (T, T), 1)
        dlogf = carry[...] - _dot((row <= col).astype(F32), d_ref[...], _DIMS["nn"], precision=lax.Precision.HIGHEST)
        carry[...] = dlogf[0:1, :]
        dff = dlogf * (1.0 - _sigmoid(ff_ref[...] + b_ref[...]))
        dff_ref[...] = dff.astype(dff_ref.dtype)
        db_ref[...] += jnp.sum(dff, axis=0, keepdims=True)

    rev = pl.BlockSpec((T, 128), lambda i: (nT - 1 - i, 0))
    vec = pl.BlockSpec((1, 128), lambda i: (0, 0))
    return pl.pallas_call(
        body, name=name, grid=(nT,),
        in_specs=[rev, vec, rev], out_specs=[rev, vec],
        out_shape=[jax.ShapeDtypeStruct((S, 128), BF16), jax.ShapeDtypeStruct((1, 128), F32)],
        scratch_shapes=[pltpu.VMEM((1, 128), F32)],
        compiler_params=_cparams(("arbitrary",)),
    )(ff, bias, dcs)


AUG = FOX_DH


def _split3(x):
    a = x.astype(BF16).astype(F32)
    r = x - a
    b = r.astype(BF16).astype(F32)
    return a, b, r - b


def _lane_fill(lane, base, pieces, start):
    for i, pc in enumerate(pieces):
        base = jnp.where(lane == start + i, pc, base)
    return base


def _fox_prep(proj, c_tok, *, name, T=512):
    S = proj.shape[0]
    T = min(T, S)

    def body(q_ref, k_ref, v_ref, c_ref, qa_ref, ka_ref, va_ref):
        pair = pl.program_id(0)
        lane = lax.broadcasted_iota(jnp.int32, (T, 128), 1)
        c = c_ref[...]
        ones3 = jnp.where((lane >= AUG) & (lane < AUG + 3), 1.0, 0.0)
        for hh in range(2):
            ch = jnp.sum(jnp.where(lane == 2 * pair + hh, c, 0.0), axis=-1, keepdims=True)
            c1, c2, c3 = _split3(ch)
            q, k, v = q_ref[...], k_ref[...], v_ref[...]
            if hh == 1:
                q, k, v = (pltpu.roll(t, 64, 1) for t in (q, k, v))
            aug_q = _lane_fill(lane, jnp.where((lane >= AUG + 3) & (lane < AUG + 6), 1.0, 0.0), (c1, c2, c3), AUG)
            aug_k = _lane_fill(lane, ones3, (-c1, -c2, -c3), AUG + 3)
            qa_ref[hh] = jnp.where(lane < AUG, q * FOX_SCALE, aug_q).astype(BF16)
            ka_ref[hh] = jnp.where(lane < AUG, k, aug_k).astype(BF16)
            va_ref[hh] = jnp.where(lane < AUG, v, ones3).astype(BF16)

    def grp(g):
        return pl.BlockSpec((T, 128), lambda p, t: (t, g * 8 + p))

    hm = pl.BlockSpec((2, T, 128), lambda p, t: (p, t, 0))
    out = jax.ShapeDtypeStruct((FOX_HEADS, S, 128), BF16)
    return pl.pallas_call(
        body, name=name, grid=(FOX_PAIRS, S // T),
        in_specs=[grp(4), grp(5), grp(6), pl.BlockSpec((T, 128), lambda p, t: (t, 0))],
        out_specs=[hm, hm, hm], out_shape=[out, out, out],
        compiler_params=_cparams(("parallel", "parallel")),
    )(proj, proj, proj, c_tok)


def _pair_lanes(lane, a0, a1):
    return jnp.where(lane < AUG, a0, pltpu.roll(a1, 64, 1))


def _tri_tables(nb, by_query):
    if by_query:
        pairs = [(i, j) for i in range(nb) for j in range(i + 1)]
    else:
        pairs = [(i, j) for j in range(nb) for i in range(j, nb)]
    return (jnp.asarray(np.array([p[0] for p in pairs], np.int32)),
            jnp.asarray(np.array([p[1] for p in pairs], np.int32)))


def _fox_fwd(qa, ka, va, *, name, tb=512):
    S = qa.shape[1]
    tb = min(tb, S)
    qtab, ktab = _tri_tables(S // tb, True)

    def body(qt_ref, kt_ref, qa_ref, ka_ref, va_ref, o_ref, qb_ref, m_s, acc_s):
        qi, ki = qt_ref[pl.program_id(1)], kt_ref[pl.program_id(1)]

        @pl.when(ki == 0)
        def _():
            m_s[...] = jnp.full_like(m_s, NEG)
            acc_s[...] = jnp.zeros_like(acc_s)

        def step(masked):
            for hh in range(2):
                s = _dot(qa_ref[hh], ka_ref[hh], _DIMS["nt"])
                if masked:
                    row = lax.broadcasted_iota(jnp.int32, (tb, tb), 0)
                    col = lax.broadcasted_iota(jnp.int32, (tb, tb), 1)
                    s = jnp.where(col <= row, s, NEG)
                m_old = m_s[hh]
                m_new = jnp.maximum(m_old, jnp.max(s, axis=-1, keepdims=True))
                p = jnp.exp(s - m_new)
                p_hi = p.astype(BF16)
                p_lo = (p - p_hi.astype(F32)).astype(BF16)
                vv = va_ref[hh]
                acc_s[hh] = (jnp.exp(m_old - m_new) * acc_s[hh]
                             + _dot(p_hi, vv, _DIMS["nn"]) + _dot(p_lo, vv, _DIMS["nn"]))
                m_s[hh] = m_new

        @pl.when(ki < qi)
        def _():
            step(False)

        @pl.when(ki == qi)
        def _():
            step(True)
            lane = lax.broadcasted_iota(jnp.int32, (tb, 128), 1)
            outs = []
            for hh in range(2):
                acc = acc_s[hh]
                l = acc[:, AUG:AUG + 1]
                outs.append(acc / l)
                qf = qa_ref[hh].astype(F32)
                cb = qf[:, AUG:AUG + 1] + qf[:, AUG + 1:AUG + 2] + qf[:, AUG + 2:AUG + 3] - (m_s[hh] + jnp.log(l))
                qb_ref[hh] = _lane_fill(lane, qf, _split3(cb), AUG).astype(BF16)
            o_ref[...] = _pair_lanes(lane, outs[0], outs[1])

    qs = pl.BlockSpec((2, tb, 128), lambda p, t, qt, kt: (p, qt[t], 0))
    ks = pl.BlockSpec((2, tb, 128), lambda p, t, qt, kt: (p, kt[t], 0))
    return pl.pallas_call(
        body, name=name,
        grid_spec=pltpu.PrefetchScalarGridSpec(
            num_scalar_prefetch=2, grid=(FOX_PAIRS, qtab.shape[0]), in_specs=[qs, ks, ks],
            out_specs=[pl.BlockSpec((tb, 128), lambda p, t, qt, kt: (qt[t], p)), qs],
            scratch_shapes=[pltpu.VMEM((2, tb, 1), F32), pltpu.VMEM((2, tb, 128), F32)]),
        out_shape=[jax.ShapeDtypeStruct((S, FOX_HEADS * FOX_DH), F32), jax.ShapeDtypeStruct((FOX_HEADS, S, 128), BF16)],
        compiler_params=_cparams(("parallel", "arbitrary")),
    )(qtab, ktab, qa, ka, va)


def _fox_bwd_prep(o, do, *, name, T=512):
    S = o.shape[0]
    T = min(T, S)

    def body(o_ref, do_ref, dob_ref):
        lane = lax.broadcasted_iota(jnp.int32, (T, 128), 1)
        d = do_ref[...].astype(F32)
        prod = d * o_ref[...]
        for hh in range(2):
            mine = (lane < AUG) if hh == 0 else (lane >= AUG)
            delta = jnp.sum(jnp.where(mine, prod, 0.0), axis=-1, keepdims=True)
            dh = d if hh == 0 else pltpu.roll(d, 64, 1)
            dob_ref[hh] = _lane_fill(lane, jnp.where(lane < AUG, dh, 0.0), _split3(-delta), AUG).astype(BF16)

    tok = pl.BlockSpec((T, 128), lambda p, t: (t, p))
    return pl.pallas_call(
        body, name=name, grid=(FOX_PAIRS, S // T),
        in_specs=[tok, tok], out_specs=pl.BlockSpec((2, T, 128), lambda p, t: (p, t, 0)),
        out_shape=jax.ShapeDtypeStruct((FOX_HEADS, S, 128), BF16),
        compiler_params=_cparams(("parallel", "parallel")),
    )(o, do)


def _fox_bwd_dq(qb, ka, va, dob, *, name, tb=512, comm=None):
    S = qb.shape[1]
    tb = min(tb, S)
    nb = S // tb
    qtab, ktab = _tri_tables(nb, True)
    nc = comm.n if comm is not None else 0
    ntri = qtab.shape[0]

    def body(qt_ref, kt_ref, qb_ref, ka_ref, va_ref, dob_ref, *rest):
        c_in, (dq_ref, dcs_ref), c_out = rest[:nc], rest[nc:nc + 2], rest[nc + 2:2 * nc + 2]
        acc_s, c_sems = rest[2 * nc + 2], rest[2 * nc + 3:]
        qi, ki = qt_ref[pl.program_id(1)], kt_ref[pl.program_id(1)]
        if comm is not None:
            @pl.when((pl.program_id(0) == 0) & (pl.program_id(1) == 0))
            def _():
                comm.start(c_in, c_out, c_sems)

        @pl.when(ki == 0)
        def _():
            acc_s[...] = jnp.zeros_like(acc_s)

        def step(masked):
            for hh in range(2):
                s = _dot(qb_ref[hh], ka_ref[hh], _DIMS["nt"])
                if masked:
                    row = lax.broadcasted_iota(jnp.int32, (tb, tb), 0)
                    col = lax.broadcasted_iota(jnp.int32, (tb, tb), 1)
                    s = jnp.where(col <= row, s, NEG)
                ds = jnp.exp(s) * _dot(dob_ref[hh], va_ref[hh], _DIMS["nt"])
                dcs_ref[0, 0, hh:hh + 1, :] = jnp.sum(ds, axis=0, keepdims=True)
                acc_s[hh] += _dot(ds.astype(BF16), ka_ref[hh], _DIMS["nn"])

        @pl.when(ki < qi)
        def _():
            step(False)

        @pl.when(ki == qi)
        def _():
            step(True)
            lane = lax.broadcasted_iota(jnp.int32, (tb, 128), 1)
            dq_ref[...] = (_pair_lanes(lane, acc_s[0], acc_s[1]) * FOX_SCALE).astype(dq_ref.dtype)

        if comm is not None:
            @pl.when((pl.program_id(0) == FOX_PAIRS - 1) & (pl.program_id(1) == ntri - 1))
            def _():
                comm.finish(c_in, c_out, c_sems)

    qs = pl.BlockSpec((2, tb, 128), lambda p, t, qt, kt: (p, qt[t], 0))
    ks = pl.BlockSpec((2, tb, 128), lambda p, t, qt, kt: (p, kt[t], 0))
    outs = pl.pallas_call(
        body, name=name,
        grid_spec=pltpu.PrefetchScalarGridSpec(
            num_scalar_prefetch=2, grid=(FOX_PAIRS, ntri), in_specs=[qs, ks, ks, qs] + [ANY] * nc,
            out_specs=[pl.BlockSpec((tb, 128), lambda p, t, qt, kt: (qt[t], p)),
                       pl.BlockSpec((1, 1, 2, tb), lambda p, t, qt, kt: (p, qt[t], 0, kt[t]))] + [ANY] * nc,
            scratch_shapes=[pltpu.VMEM((2, tb, 128), F32)] + (comm.scratch if comm is not None else [])),
        out_shape=[jax.ShapeDtypeStruct((S, FOX_HEADS * FOX_DH), BF16),
                   jax.ShapeDtypeStruct((FOX_PAIRS, nb, 2, S), F32)] + (comm.out_shapes if comm is not None else []),
        compiler_params=_cparams(("parallel", "arbitrary") if comm is None else ("arbitrary", "arbitrary")),
    )(qtab, ktab, qb, ka, va, dob, *(comm.inputs if comm is not None else []))
    return (outs[0], outs[1]) if comm is None else (outs[0], outs[1], outs[2:])


def _fox_bwd_dkv(qb, ka, va, dob, *, name, tb=512):
    S = qb.shape[1]
    tb = min(tb, S)
    nb = S // tb
    qtab, ktab = _tri_tables(nb, False)

    def body(qt_ref, kt_ref, qb_ref, ka_ref, va_ref, dob_ref, dk_ref, dv_ref, dk_s, dv_s):
        qi, ki = qt_ref[pl.program_id(1)], kt_ref[pl.program_id(1)]

        @pl.when(qi == ki)
        def _():
            dk_s[...] = jnp.zeros_like(dk_s)
            dv_s[...] = jnp.zeros_like(dv_s)

        def step(masked):
            for hh in range(2):
                st = _dot(ka_ref[hh], qb_ref[hh], _DIMS["nt"])
                if masked:
                    row = lax.broadcasted_iota(jnp.int32, (tb, tb), 0)
                    col = lax.broadcasted_iota(jnp.int32, (tb, tb), 1)
                    st = jnp.where(row <= col, st, NEG)
                pt = jnp.exp(st)
                dst = pt * _dot(va_ref[hh], dob_ref[hh], _DIMS["nt"])
                dv_s[hh] += _dot(pt.astype(BF16), dob_ref[hh], _DIMS["nn"])
                dk_s[hh] += _dot(dst.astype(BF16), qb_ref[hh], _DIMS["nn"])

        @pl.when(qi > ki)
        def _():
            step(False)

        @pl.when(qi == ki)
        def _():
            step(True)

        @pl.when(qi == nb - 1)
        def _():
            lane = lax.broadcasted_iota(jnp.int32, (tb, 128), 1)
            dk_ref[...] = _pair_lanes(lane, dk_s[0], dk_s[1]).astype(dk_ref.dtype)
            dv_ref[...] = _pair_lanes(lane, dv_s[0], dv_s[1]).astype(dv_ref.dtype)

    ks = pl.BlockSpec((2, tb, 128), lambda p, t, qt, kt: (p, kt[t], 0))
    qs = pl.BlockSpec((2, tb, 128), lambda p, t, qt, kt: (p, qt[t], 0))
    tok = pl.BlockSpec((tb, 128), lambda p, t, qt, kt: (kt[t], p))
    big = jax.ShapeDtypeStruct((S, FOX_HEADS * FOX_DH), BF16)
    return pl.pallas_call(
        body, name=name,
        grid_spec=pltpu.PrefetchScalarGridSpec(
            num_scalar_prefetch=2, grid=(FOX_PAIRS, qtab.shape[0]), in_specs=[qs, ks, ks, qs], out_specs=[tok, tok],
            scratch_shapes=[pltpu.VMEM((2, tb, 128), F32), pltpu.VMEM((2, tb, 128), F32)]),
        out_shape=[big, big],
        compiler_params=_cparams(("parallel", "arbitrary")),
    )(qtab, ktab, qb, ka, va, dob)


def _merge_fwd(proj, pa, pb, *, name, T=512):
    S, D = pa.shape
    T = min(T, S)

    def body(ga_ref, gb_ref, pa_ref, pb_ref, m_ref):
        m_ref[...] = (_sigmoid(ga_ref[...]) * pa_ref[...] + _sigmoid(gb_ref[...]) * pb_ref[...]).astype(m_ref.dtype)

    tok = pl.BlockSpec((T, D), lambda i: (i, 0))
    return pl.pallas_call(
        body, name=name, grid=(S // T,),
        in_specs=[pl.BlockSpec((T, D), lambda i: (i, 7)), pl.BlockSpec((T, D), lambda i: (i, 8)), tok, tok],
        out_specs=tok, out_shape=jax.ShapeDtypeStruct((S, D), BF16),
        compiler_params=_cparams(("parallel",)),
    )(proj, proj, pa, pb)


def _merge_bwd(proj, pa, pb, dm, *, name, T=512):
    S, D = pa.shape
    T = min(T, S)

    def body(ga_ref, gb_ref, pa_ref, pb_ref, dm_ref, dpa_ref, dpb_ref, dga_ref, dgb_ref):
        dm_ = dm_ref[...]
        sa, sb = _sigmoid(ga_ref[...]), _sigmoid(gb_ref[...])
        dpa_ref[...] = (dm_ * sa).astype(BF16)
        dpb_ref[...] = (dm_ * sb).astype(BF16)
        dga_ref[...] = (dm_ * pa_ref[...] * sa * (1.0 - sa)).astype(BF16)
        dgb_ref[...] = (dm_ * pb_ref[...] * sb * (1.0 - sb)).astype(BF16)

    tok = pl.BlockSpec((T, D), lambda i: (i, 0))
    big = jax.ShapeDtypeStruct((S, D), BF16)
    return pl.pallas_call(
        body, name=name, grid=(S // T,),
        in_specs=[pl.BlockSpec((T, D), lambda i: (i, 7)), pl.BlockSpec((T, D), lambda i: (i, 8)), tok, tok, tok],
        out_specs=[tok, tok, tok, tok], out_shape=[big, big, big, big],
        compiler_params=_cparams(("parallel",)),
    )(proj, proj, pa, pb, dm)


INV_SQRT2 = 0.7071067811865476
INV_SQRT2PI = 0.3989422804014327


def _shifted(u, prev, rid):
    m1 = jnp.where(rid == 0, prev[7:8, :], pltpu.roll(u, 1, 0))
    m2 = jnp.where(rid == 0, prev[6:7, :], jnp.where(rid == 1, prev[7:8, :], pltpu.roll(u, 2, 0)))
    return m1, m2


def _conv_acc(u, prev, w_ref, b_ref, rid):
    m1, m2 = _shifted(u, prev, rid)
    return b_ref[...] + w_ref[0:1, :] * m2 + w_ref[1:2, :] * m1 + w_ref[2:3, :] * u, m1, m2


def _convglu_fwd(ug, uv, wg, wv, bg, bv, *, name, T=512, tc=256):
    S, F = ug.shape
    T = min(T, S)

    def body(ug_ref, uv_ref, wg_ref, wv_ref, bg_ref, bv_ref, a_ref, pg, pv):
        @pl.when(pl.program_id(1) == 0)
        def _():
            pg[...] = jnp.zeros_like(pg)
            pv[...] = jnp.zeros_like(pv)

        rid = lax.broadcasted_iota(jnp.int32, (T, tc), 0)
        g_, v_ = ug_ref[...], uv_ref[...]
        accg, _, _ = _conv_acc(g_, pg[...], wg_ref, bg_ref, rid)
        accv, _, _ = _conv_acc(v_, pv[...], wv_ref, bv_ref, rid)
        gel = 0.5 * accg * (1.0 + lax.erf(accg * INV_SQRT2))
        a_ref[...] = (gel * accv).astype(a_ref.dtype)
        pg[...] = g_[T - 8:T, :]
        pv[...] = v_[T - 8:T, :]

    tok = pl.BlockSpec((T, tc), lambda j, t: (t, j))
    w3 = pl.BlockSpec((3, tc), lambda j, t: (0, j))
    b1 = pl.BlockSpec((1, tc), lambda j, t: (0, j))
    return pl.pallas_call(
        body, name=name, grid=(F // tc, S // T),
        in_specs=[tok, tok, w3, w3, b1, b1], out_specs=tok,
        out_shape=jax.ShapeDtypeStruct((S, F), BF16),
        scratch_shapes=[pltpu.VMEM((8, tc), F32), pltpu.VMEM((8, tc), F32)],
        compiler_params=_cparams(("parallel", "arbitrary")),
    )(ug, uv, wg, wv, bg, bv)


def _convglu_bwd(ug, uv, wg, wv, bg, bv, da, *, name, T=512, tc=256):
    S, F = ug.shape
    T = min(T, S)
    nT = S // T
    halo_blocks = T // 8

    def up_shift(d, nx, rid):
        p1 = jnp.where(rid == T - 1, nx[0:1, :], pltpu.roll(d, T - 1, 0))
        p2 = jnp.where(rid == T - 1, nx[1:2, :], jnp.where(rid == T - 2, nx[0:1, :], pltpu.roll(d, T - 2, 0)))
        return p1, p2

    def body(ug_ref, uv_ref, hg_ref, hv_ref, wg_ref, wv_ref, bg_ref, bv_ref, da_ref,
             dug_ref, duv_ref, dwg_ref, dwv_ref, dbg_ref, dbv_ref, ng, nv):
        @pl.when(pl.program_id(1) == 0)
        def _():
            ng[...] = jnp.zeros_like(ng)
            nv[...] = jnp.zeros_like(nv)
            for r in (dwg_ref, dwv_ref, dbg_ref, dbv_ref):
                r[...] = jnp.zeros_like(r)

        first_block = pl.program_id(1) == nT - 1
        rid = lax.broadcasted_iota(jnp.int32, (T, tc), 0)
        g_, v_ = ug_ref[...], uv_ref[...]
        pg = jnp.where(first_block, 0.0, hg_ref[...])
        pv = jnp.where(first_block, 0.0, hv_ref[...])
        accg, g1, g2 = _conv_acc(g_, pg, wg_ref, bg_ref, rid)
        accv, v1, v2 = _conv_acc(v_, pv, wv_ref, bv_ref, rid)
        cdf = 0.5 * (1.0 + lax.erf(accg * INV_SQRT2))
        pdf = INV_SQRT2PI * jnp.exp(-0.5 * accg * accg)
        da_ = da_ref[...].astype(F32)
        dgate = da_ * accv * (cdf + accg * pdf)
        dval = da_ * (accg * cdf)
        dbg_ref[...] += jnp.sum(dgate, axis=0, keepdims=True)
        dbv_ref[...] += jnp.sum(dval, axis=0, keepdims=True)
        for j, (sg_, sv_) in enumerate(((g2, v2), (g1, v1), (g_, v_))):
            dwg_ref[j:j + 1, :] += jnp.sum(dgate * sg_, axis=0, keepdims=True)
            dwv_ref[j:j + 1, :] += jnp.sum(dval * sv_, axis=0, keepdims=True)
        for d, w_ref, nx, out_ref in ((dgate, wg_ref, ng, dug_ref), (dval, wv_ref, nv, duv_ref)):
            p1, p2 = up_shift(d, nx[...], rid)
            out_ref[...] = (w_ref[2:3, :] * d + w_ref[1:2, :] * p1 + w_ref[0:1, :] * p2).astype(out_ref.dtype)
            nx[...] = d[0:8, :]

    tok = pl.BlockSpec((T, tc), lambda j, t: (nT - 1 - t, j))
    halo = pl.BlockSpec((8, tc), lambda j, t: (jnp.maximum((nT - 1 - t) * halo_blocks - 1, 0), j))
    w3 = pl.BlockSpec((3, tc), lambda j, t: (0, j))
    b1 = pl.BlockSpec((1, tc), lambda j, t: (0, j))
    big = jax.ShapeDtypeStruct((S, F), BF16)
    return pl.pallas_call(
        body, name=name, grid=(F // tc, nT),
        in_specs=[tok, tok, halo, halo, w3, w3, b1, b1, tok], out_specs=[tok, tok, w3, w3, b1, b1],
        out_shape=[big, big, jax.ShapeDtypeStruct((3, F), F32), jax.ShapeDtypeStruct((3, F), F32),
                   jax.ShapeDtypeStruct((1, F), F32), jax.ShapeDtypeStruct((1, F), F32)],
        scratch_shapes=[pltpu.VMEM((8, tc), F32), pltpu.VMEM((8, tc), F32)],
        compiler_params=_cparams(("parallel", "arbitrary")),
    )(ug, uv, ug, uv, wg, wv, bg, bv, da)


FF_LO, FF_HI = 7168, 7184


def _col_blocks(a, width):
    return jnp.stack([a[:, d * width:(d + 1) * width] for d in range(N_DEV)])


def _late_weights(g_a, g_b, g_o, g_up, g_cw, g_d):
    wup = jnp.concatenate([g_up[d] for d in range(N_DEV)], axis=1)
    cw = jnp.concatenate([g_cw[d] for d in range(N_DEV)], axis=1)
    return dict(wa=g_a.reshape(D_MODEL, D_MODEL), wb=g_b.reshape(D_MODEL, D_MODEL), wo=g_o.reshape(D_MODEL, D_MODEL),
                wug=wup[:, :D_FF], wuv=wup[:, D_FF:], cwg=cw[:, :D_FF], cwv=cw[:, D_FF:], wd=g_d.reshape(D_FF, D_MODEL))


def _early_grad_blocks(d_wa, d_wb, d_wo, d_wug, d_wuv, d_wd):
    up = jnp.stack([d_wug[:, d * 704:(d + 1) * 704] for d in range(4)]
                   + [d_wuv[:, d * 704:(d + 1) * 704] for d in range(4)])
    return [d_wa.reshape(N_DEV, 128, D_MODEL), d_wb.reshape(N_DEV, 128, D_MODEL), d_wo.reshape(N_DEV, 128, D_MODEL),
            up, d_wd.reshape(N_DEV, 352, D_MODEL)]


def _local_step(x, tgt, w, p, late=None, exchange=False):
    S = x.shape[0]
    mm = _matmul
    n1 = _rms_fwd(x, p["norm_mix"], name="rms1_fwd")
    if late is None:
        proj = mm(n1, w["wm"], "nn", name="proj_main")
    else:
        proj, gathered = mm(n1, w["wm"], "nn", comm=late, name="proj_main")
        w = {**w, **_late_weights(*gathered)}
    ff = mm(n1, w["wff"], "nn", name="proj_ff")
    lb = _lb_fwd(p["hg_lb_logits"], name="lb_fwd")
    gnorm = p["hg_norm"].reshape(1, HG_DV)
    o_hg, oa, states = _hgrn_fwd(proj, lb, gnorm, name="hgrn_fwd")
    bias = jnp.pad(p["fox_f_bias"].reshape(1, FOX_HEADS), ((0, 0), (0, 128 - FOX_HEADS)))
    c = _fox_gate_fwd(ff, bias, name="fox_gate_fwd")
    qa, ka, va = _fox_prep(proj, c, name="fox_prep")
    ob, qb = _fox_fwd(qa, ka, va, name="fox_fwd")
    pa = mm(oa, w["wa"], "nn", name="branch_a")
    pb = mm(ob, w["wb"], "nn", name="branch_b")
    merged = _merge_fwd(proj, pa, pb, name="merge_fwd")
    h1 = mm(merged, w["wo"], "nn", addend=x, name="mix_out")
    n2 = _rms_fwd(h1, p["norm_ffn"], name="rms2_fwd")
    ug = mm(n2, w["wug"], "nn", name="up_gate")
    uv = mm(n2, w["wuv"], "nn", name="up_val")
    a = _convglu_fwd(ug, uv, w["cwg"], w["cwv"], p["cbg"], p["cbv"], name="convglu_fwd")
    h2 = mm(a, w["wd"], "nn", addend=h1, name="ffn_down")
    loss, dh2, d_norm_final = _loss_head(h2, p["norm_final"], tgt, name="loss_head")
    da = mm(dh2, w["wd"], "nt", out_dtype=BF16, name="d_act")
    d_wd = mm(a, dh2, "tn", out_dtype=BF16, name="dw_down")
    dug, duv, d_cwg, d_cwv, d_cbg, d_cbv = _convglu_bwd(
        ug, uv, w["cwg"], w["cwv"], p["cbg"], p["cbv"], da, name="convglu_bwd")
    dn2 = mm(dug, w["wug"], "nt", name="dn2_gate")
    dn2 = mm(duv, w["wuv"], "nt", addend=dn2, name="dn2_val")
    d_wug = mm(n2, dug, "tn", out_dtype=BF16, name="dw_up_gate")
    d_wuv = mm(n2, duv, "tn", out_dtype=BF16, name="dw_up_val")
    dh1, d_norm_ffn = _rms_bwd(h1, p["norm_ffn"], dn2, dh2, name="rms2_bwd")
    dmerged = mm(dh1, w["wo"], "nt", name="d_merged")
    d_wo = mm(merged, dh1, "tn", out_dtype=BF16, name="dw_out")
    dpa, dpb, dga, dgb = _merge_bwd(proj, pa, pb, dmerged, name="merge_bwd")
    doa = mm(dpa, w["wa"], "nt", name="d_oa")
    dob = mm(dpb, w["wb"], "nt", out_dtype=BF16, name="d_ob")
    d_wa = mm(oa, dpa, "tn", out_dtype=BF16, name="dw_branch_a")
    d_wb = mm(ob, dpb, "tn", out_dtype=BF16, name="dw_branch_b")
    dhq, dhf, dhi, dhg, dlb, dgn8 = _hgrn_bwd(proj, lb, gnorm, o_hg, states, doa, name="hgrn_bwd")
    d_logits = _lb_bwd(p["hg_lb_logits"], dlb, name="lb_bwd")
    dob_hm = _fox_bwd_prep(ob, dob, name="fox_bwd_prep")
    early_parts = None
    if exchange:
        comm = _ExchangeComm(_early_grad_blocks(d_wa, d_wb, d_wo, d_wug, d_wuv, d_wd))
        dq, dcsp, early_parts = _fox_bwd_dq(qb, ka, va, dob_hm, comm=comm, name="fox_bwd_dq")
    else:
        dq, dcsp = _fox_bwd_dq(qb, ka, va, dob_hm, name="fox_bwd_dq")
    dk, dv = _fox_bwd_dkv(qb, ka, va, dob_hm, name="fox_bwd_dkv")
    nb = dcsp.shape[1]
    written = (jnp.arange(S) // (S // nb))[None, None, None, :] <= jnp.arange(nb)[None, :, None, None]
    dcs = jnp.sum(jnp.where(written, dcsp, 0.0), axis=1)
    dcs_tok = jnp.pad(dcs.reshape(FOX_HEADS, S).T, ((0, 0), (0, 128 - FOX_HEADS)))
    dff, dbias = _fox_gate_bwd(ff, bias, dcs_tok, name="fox_gate_bwd")
    dproj = jnp.concatenate([dhq, dhf, dhi, dhg, dq, dk, dv, dga, dgb], axis=1)
    d_wm = mm(n1, dproj, "tn", out_dtype=BF16, name="dw_in_main")
    d_wff = mm(n1, dff, "tn", out_dtype=BF16, name="dw_in_ff")
    dn1 = mm(dff, w["wff"], "nt", name="dn1_ff")
    late_parts = None
    if exchange:
        d_win = jnp.concatenate([d_wm[:, :FF_LO], d_wff[:, :FOX_HEADS], d_wm[:, FF_LO:]], axis=1)
        d_cw = jnp.concatenate([d_cwg, d_cwv], axis=1)
        comm = _ExchangeComm([_col_blocks(d_win, 1154), _col_blocks(d_cw, 704)])
        dn1, late_parts = mm(dproj, w["wm"], "nt", addend=dn1, comm=comm, name="dn1_main")
    else:
        dn1 = mm(dproj, w["wm"], "nt", addend=dn1, name="dn1_main")
    dx, d_norm_mix = _rms_bwd(x, p["norm_mix"], dn1, dh1, name="rms1_bwd")
    grads = dict(
        wm=d_wm, wff=d_wff, wa=d_wa, wb=d_wb, wo=d_wo, wug=d_wug, wuv=d_wuv, cwg=d_cwg, cwv=d_cwv, wd=d_wd,
        norm_mix=d_norm_mix.reshape(-1), fox_f_bias=dbias[0, :FOX_HEADS], hg_lb_logits=d_logits,
        hg_norm=jnp.sum(dgn8, axis=0).reshape(-1), norm_ffn=d_norm_ffn.reshape(-1), cbg=d_cbg, cbv=d_cbv,
        norm_final=d_norm_final.reshape(-1), early_parts=early_parts, late_parts=late_parts)
    return loss, dx, grads


SMALL = [("norm_mix", (1, D_MODEL)), ("fox_f_bias", (1, FOX_HEADS)), ("hg_lb_logits", (2, HG_HEADS * HG_DK)),
         ("hg_norm", (1, HG_DV)), ("norm_ffn", (1, D_MODEL)), ("conv_b", (1, 2 * D_FF)), ("norm_final", (D_MODEL,))]
SMALL_ROWS = 88
SHARDED = [("w_in", (D_MODEL, 1154), 256), ("w_branch_a", (128, D_MODEL), 128), ("w_branch_b", (128, D_MODEL), 128),
           ("w_out", (128, D_MODEL), 128), ("w_up", (D_MODEL, 704), 256), ("conv_w", (3, 704), 3),
           ("w_down", (352, D_MODEL), 352)]
NAMES = ["norm_mix", "w_in", "fox_f_bias", "hg_lb_logits", "hg_norm", "w_branch_a", "w_branch_b", "w_out",
         "norm_ffn", "w_up", "conv_w", "conv_b", "w_down", "norm_final"]


def _size(shape):
    n = 1
    for s in shape:
        n *= s
    return n


def _adamw(parts, w, m, v, *, name, T):
    R, C = w.shape
    c1 = 1.0 / (1.0 - ADAM_B1 ** ADAM_STEP)
    c2 = 1.0 / (1.0 - ADAM_B2 ** ADAM_STEP)

    def body(p_ref, w_ref, m_ref, v_ref, g_ref, d_ref, nm_ref, nv_ref):
        g = p_ref[0].astype(F32)
        for s in range(1, N_DEV):
            g = g + p_ref[s].astype(F32)
        g_ref[...] = g
        nm = ADAM_B1 * m_ref[...] + (1.0 - ADAM_B1) * g
        nv = ADAM_B2 * v_ref[...] + (1.0 - ADAM_B2) * (g * g)
        nm_ref[...] = nm
        nv_ref[...] = nv
        d_ref[...] = -ADAM_LR * ((nm * c1) / (jnp.sqrt(nv * c2) + ADAM_EPS) + ADAM_WD * w_ref[...])

    blk = pl.BlockSpec((T, C), lambda i: (i, 0))
    out = jax.ShapeDtypeStruct((R, C), F32)
    return pl.pallas_call(
        body, name=name, grid=(R // T,),
        in_specs=[pl.BlockSpec((N_DEV, T, C), lambda i: (0, i, 0)), blk, blk, blk],
        out_specs=[blk, blk, blk, blk], out_shape=[out, out, out, out],
        compiler_params=_cparams(("parallel",)),
    )(parts, w, m, v)


def _pack_small(vals):
    flat = jnp.concatenate([vals[n].reshape(-1).astype(F32) for n, _ in SMALL])
    return jnp.pad(flat, (0, SMALL_ROWS * 128 - flat.shape[0])).reshape(SMALL_ROWS, 128)


def _unpack_small(buf):
    flat, out, off = buf.reshape(-1), {}, 0
    for n, shape in SMALL:
        out[n] = flat[off:off + _size(shape)].reshape(shape)
        off += _size(shape)
    return out


def kernel(x, norm_mix, w_in, fox_f_bias, hg_lb_logits, hg_norm, w_branch_a, w_branch_b, w_out, norm_ffn, w_up, conv_w, conv_b, w_down, norm_final, loss_target, m_norm_mix, m_w_in, m_fox_f_bias, m_hg_lb_logits, m_hg_norm, m_w_branch_a, m_w_branch_b, m_w_out, m_norm_ffn, m_w_up, m_conv_w, m_conv_b, m_w_down, m_norm_final, v_norm_mix, v_w_in, v_fox_f_bias, v_hg_lb_logits, v_hg_norm, v_w_branch_a, v_w_branch_b, v_w_out, v_norm_ffn, v_w_up, v_conv_w, v_conv_b, v_w_down, v_norm_final):
    wv = dict(norm_mix=norm_mix, w_in=w_in, fox_f_bias=fox_f_bias, hg_lb_logits=hg_lb_logits, hg_norm=hg_norm,
              w_branch_a=w_branch_a, w_branch_b=w_branch_b, w_out=w_out, norm_ffn=norm_ffn, w_up=w_up, conv_w=conv_w,
              conv_b=conv_b, w_down=w_down, norm_final=norm_final)
    mv = dict(norm_mix=m_norm_mix, w_in=m_w_in, fox_f_bias=m_fox_f_bias, hg_lb_logits=m_hg_lb_logits, hg_norm=m_hg_norm,
              w_branch_a=m_w_branch_a, w_branch_b=m_w_branch_b, w_out=m_w_out, norm_ffn=m_norm_ffn, w_up=m_w_up,
              conv_w=m_conv_w, conv_b=m_conv_b, w_down=m_w_down, norm_final=m_norm_final)
    vv = dict(norm_mix=v_norm_mix, w_in=v_w_in, fox_f_bias=v_fox_f_bias, hg_lb_logits=v_hg_lb_logits, hg_norm=v_hg_norm,
              w_branch_a=v_w_branch_a, w_branch_b=v_w_branch_b, w_out=v_w_out, norm_ffn=v_norm_ffn, w_up=v_w_up,
              conv_w=v_conv_w, conv_b=v_conv_b, w_down=v_w_down, norm_final=v_norm_final)

    (g_in,) = _comm_call(_GatherComm([w_in[0].astype(BF16)]), name="gather_w_in")
    win = jnp.concatenate([g_in[d] for d in range(N_DEV)], axis=1)
    w = dict(wm=jnp.concatenate([win[:, :FF_LO], win[:, FF_HI:]], axis=1),
             wff=jnp.pad(win[:, FF_LO:FF_HI], ((0, 0), (0, 128 - FOX_HEADS))))
    late = _GatherComm([w_branch_a[0].astype(BF16), w_branch_b[0].astype(BF16), w_out[0].astype(BF16),
                        w_up[0].astype(BF16), conv_w[0], w_down[0].astype(BF16)])
    p = dict(norm_mix=norm_mix[0], fox_f_bias=fox_f_bias[0], hg_lb_logits=hg_lb_logits, hg_norm=hg_norm[0],
             norm_ffn=norm_ffn[0], cbg=conv_b[:, :D_FF], cbv=conv_b[:, D_FF:], norm_final=norm_final)
    loss, dx, grads = _local_step(x[0], loss_target[0], w, p, late=late, exchange=True)
    loss = lax.psum(loss[0, 0], ("x", "y", "c"))

    small = _pack_small(dict(
        norm_mix=grads["norm_mix"], fox_f_bias=grads["fox_f_bias"], hg_lb_logits=grads["hg_lb_logits"],
        hg_norm=grads["hg_norm"], norm_ffn=grads["norm_ffn"], conv_b=jnp.concatenate([grads["cbg"], grads["cbv"]], axis=1),
        norm_final=grads["norm_final"]))
    (small_parts,) = _comm_call(_ExchangeComm([jnp.broadcast_to(small[None], (N_DEV, SMALL_ROWS, 128))]),
                                name="exchange_small")
    ea, eb, eo, eup, ed = grads["early_parts"]
    p_in, p_cw = grads["late_parts"]
    parts = [p_in, ea, eb, eo, eup, p_cw, ed, small_parts]
    res = {}
    for (n, shape, tile), part in zip(SHARDED, parts):
        outs = _adamw(part, wv[n].reshape(shape), mv[n].reshape(shape), vv[n].reshape(shape), name="adamw_" + n, T=tile)
        res[n] = [o.reshape(wv[n].shape) for o in outs]
    outs = _adamw(parts[-1], _pack_small(wv), _pack_small(mv), _pack_small(vv), name="adamw_small", T=SMALL_ROWS)
    small = [_unpack_small(o) for o in outs]
    for n, _ in SMALL:
        res[n] = [s[n] for s in small]
    return (loss, dx[None], *[res[n][0] for n in NAMES], *[res[n][1] for n in NAMES],
            *[res[n][2] for n in NAMES], *[res[n][3] for n in NAMES])
```

```python
import numpy as np
import jax
import jax.numpy as jnp
from jax import lax
from jax.experimental import pallas as pl
from jax.experimental.pallas import tpu as pltpu

F32 = jnp.float32
BF16 = jnp.bfloat16

D_MODEL = 1024
HG_HEADS = 8
HG_DK = 128
HG_DV = 128
HG_CHUNK = 64
FOX_HEADS = 16
FOX_DH = 64
D_FF = 2816
EPS = 1e-6
N_DEV = 8

ADAM_LR = 0.001
ADAM_B1 = 0.9
ADAM_B2 = 0.999
ADAM_EPS = 1e-08
ADAM_WD = 0.01
ADAM_STEP = 10

VMEM_LIMIT = 56 * 1024 * 1024


def _cparams(sem):
    return pltpu.CompilerParams(dimension_semantics=sem, vmem_limit_bytes=VMEM_LIMIT)


MESH = pl.DeviceIdType.MESH
ANY = pl.BlockSpec(memory_space=pl.ANY)
SMEM = pl.BlockSpec(memory_space=pltpu.SMEM)


class _GatherComm:
    def __init__(self, shards):
        self.inputs = list(shards)
        n = self.n = len(shards)
        self.out_shapes = [jax.ShapeDtypeStruct((N_DEV,) + s.shape, s.dtype) for s in shards]
        self.scratch = [pltpu.SemaphoreType.DMA((n, 7)), pltpu.SemaphoreType.DMA((n, 7)), pltpu.SemaphoreType.DMA((n,))]

    def _parts(self, x_refs, out_refs, sems):
        send_sems, recv_sems, local_sems = sems
        x, y, c = lax.axis_index("x"), lax.axis_index("y"), lax.axis_index("c")
        me, sibling = (x, y, c), (x, y, 1 - c)
        chips = [(1 - x, y), (x, 1 - y), (1 - x, 1 - y)]

        def copy(t, k, block, to, src=None):
            slot = out_refs[t].at[4 * block[0] + 2 * block[1] + block[2]]
            return pltpu.make_async_remote_copy(
                src_ref=slot if src is None else src, dst_ref=slot,
                send_sem=send_sems.at[t, k], recv_sem=recv_sems.at[t, k], device_id=to, device_id_type=MESH)

        mine = [pltpu.make_async_copy(x_refs[t], out_refs[t].at[4 * x + 2 * y + c], local_sems.at[t])
                for t in range(self.n)]
        first = []
        for t in range(self.n):
            first.append(copy(t, 0, me, sibling, src=x_refs[t]))
            first += [copy(t, 1 + j, me, (*chip, c), src=x_refs[t]) for j, chip in enumerate(chips)]
        return c, me, sibling, chips, copy, mine, first

    def start(self, x_refs, out_refs, sems):
        _, _, _, _, _, mine, first = self._parts(x_refs, out_refs, sems)
        for cp in mine + first:
            cp.start()

    def finish(self, x_refs, out_refs, sems):
        c, me, sibling, chips, copy, mine, first = self._parts(x_refs, out_refs, sems)
        passed = []
        for j, chip in enumerate(chips):
            for t in range(self.n):
                copy(t, 1 + j, (*chip, c), me).wait_recv()
                passed.append(copy(t, 4 + j, (*chip, c), sibling))
                passed[-1].start()
        for t in range(self.n):
            copy(t, 0, sibling, me).wait_recv()
            for j, chip in enumerate(chips):
                copy(t, 4 + j, (*chip, 1 - c), me).wait_recv()
        for cp in first + passed:
            cp.wait_send()
        for cp in mine:
            cp.wait()


class _ExchangeComm:
    def __init__(self, blocks):
        self.inputs = list(blocks)
        n = self.n = len(blocks)
        self.out_shapes = [jax.ShapeDtypeStruct(b.shape, b.dtype) for b in blocks]
        self.scratch = [pltpu.SemaphoreType.DMA((n, 7)), pltpu.SemaphoreType.DMA((n, 7)), pltpu.SemaphoreType.DMA((n,))]

    def _parts(self, g_refs, out_refs, sems):
        send_sems, recv_sems, local_sems = sems
        x, y, c = lax.axis_index("x"), lax.axis_index("y"), lax.axis_index("c")
        me = 4 * x + 2 * y + c
        mine = [pltpu.make_async_copy(g_refs[t].at[me], out_refs[t].at[me], local_sems.at[t]) for t in range(self.n)]
        sends, recvs = [], []
        for k in range(1, N_DEV):
            px = 1 - x if k & 4 else x
            py = 1 - y if k & 2 else y
            pc = 1 - c if k & 1 else c
            p = 4 * px + 2 * py + pc
            for t in range(self.n):
                sends.append(pltpu.make_async_remote_copy(
                    src_ref=g_refs[t].at[p], dst_ref=out_refs[t].at[me], send_sem=send_sems.at[t, k - 1],
                    recv_sem=recv_sems.at[t, k - 1], device_id=(px, py, pc), device_id_type=MESH))
                recvs.append(pltpu.make_async_remote_copy(
                    src_ref=g_refs[t].at[p], dst_ref=out_refs[t].at[p], send_sem=send_sems.at[t, k - 1],
                    recv_sem=recv_sems.at[t, k - 1], device_id=(px, py, pc), device_id_type=MESH))
        return mine, sends, recvs

    def start(self, g_refs, out_refs, sems):
        mine, sends, _ = self._parts(g_refs, out_refs, sems)
        for cp in mine + sends:
            cp.start()

    def finish(self, g_refs, out_refs, sems):
        mine, sends, recvs = self._parts(g_refs, out_refs, sems)
        for cp in recvs:
            cp.wait_recv()
        for cp in sends:
            cp.wait_send()
        for cp in mine:
            cp.wait()


def _comm_call(comm, *, name):
    n = comm.n

    def body(*refs):
        comm.start(refs[:n], refs[n:2 * n], refs[2 * n:])
        comm.finish(refs[:n], refs[n:2 * n], refs[2 * n:])

    return pl.pallas_call(body, name=name, in_specs=[ANY] * n, out_specs=[ANY] * n, out_shape=comm.out_shapes,
                          scratch_shapes=comm.scratch)(*comm.inputs)


_DIMS = {
    "nn": (((1,), (0,)), ((), ())),
    "nt": (((1,), (1,)), ((), ())),
    "tn": (((0,), (0,)), ((), ())),
}

MATMUL_VMEM_BUDGET = 36 * 1024 * 1024
MAX_TILE = 1536


def _pick(n, prefs):
    for p in prefs:
        if n % p == 0:
            return p
    return n


def _tile_options(n):
    return [d for d in range(128, min(n, MAX_TILE) + 1, 128) if n % d == 0] or [n]


def _pick_tiles(M, N, tk, nk, sa, sb, so, has_addend, tm, tn):
    best = None
    for cm in ([tm] if tm else _tile_options(M)):
        for cn in ([tn] if tn else _tile_options(N)):
            need = 2 * (cm * tk * sa + tk * cn * sb + cm * cn * so + (cm * cn * 4 if has_addend else 0))
            need += cm * cn * 4 if nk > 1 else 0
            if need <= MATMUL_VMEM_BUDGET and (best is None or cm * cn > best[0] * best[1]
                                               or (cm * cn == best[0] * best[1] and cn > best[1])):
                best = (cm, cn)
    assert best is not None, (M, N, tk)
    return best


def _matmul(a, b, form, *, out_dtype=F32, addend=None, tm=None, tn=None, tk=None, comm=None, name):
    if form == "nn":
        (M, K), (K2, N) = a.shape, b.shape
    elif form == "nt":
        (M, K), (N, K2) = a.shape, b.shape
    else:
        (K, M), (K2, N) = a.shape, b.shape
    assert K == K2, (a.shape, b.shape, form)
    tk = tk or (K if K <= 2816 else _pick(K, (1024, 512, 256, 128)))
    nk = K // tk
    if tm is None or tn is None:
        tm, tn = _pick_tiles(M, N, tk, nk, a.dtype.itemsize, b.dtype.itemsize, jnp.dtype(out_dtype).itemsize,
                             addend is not None, tm, tn)
    assert M % tm == 0 and N % tn == 0 and K % tk == 0, (M, N, K, tm, tn, tk)
    dims = _DIMS[form]
    nc = comm.n if comm is not None else 0
    grid = (M // tm, N // tn, nk)

    def body(*refs):
        a_ref, b_ref = refs[:2]
        pos = 2
        add_ref = refs[pos] if addend is not None else None
        pos += addend is not None
        c_in, o_ref, c_out = refs[pos:pos + nc], refs[pos + nc], refs[pos + nc + 1:pos + 2 * nc + 1]
        pos += 2 * nc + 1
        acc_ref = refs[pos] if nk > 1 else None
        c_sems = refs[pos + (nk > 1):]
        if comm is not None:
            ids = [pl.program_id(d) for d in range(3)]

            @pl.when((ids[0] == 0) & (ids[1] == 0) & (ids[2] == 0))
            def _():
                comm.start(c_in, c_out, c_sems)

        def finish(r):
            if add_ref is not None:
                r = r + add_ref[...].astype(F32)
            o_ref[...] = r.astype(o_ref.dtype)

        part = lax.dot_general(a_ref[...].astype(BF16), b_ref[...].astype(BF16), dims, preferred_element_type=F32)
        if nk == 1:
            finish(part)
        else:
            k = pl.program_id(2)

            @pl.when(k == 0)
            def _():
                acc_ref[...] = part

            @pl.when(k > 0)
            def _():
                acc_ref[...] += part

            @pl.when(k == nk - 1)
            def _():
                finish(acc_ref[...])

        if comm is not None:
            @pl.when((ids[0] == grid[0] - 1) & (ids[1] == grid[1] - 1) & (ids[2] == grid[2] - 1))
            def _():
                comm.finish(c_in, c_out, c_sems)

    if form == "nn":
        a_spec = pl.BlockSpec((tm, tk), lambda i, j, k: (i, k))
        b_spec = pl.BlockSpec((tk, tn), lambda i, j, k: (k, j))
    elif form == "nt":
        a_spec = pl.BlockSpec((tm, tk), lambda i, j, k: (i, k))
        b_spec = pl.BlockSpec((tn, tk), lambda i, j, k: (j, k))
    else:
        a_spec = pl.BlockSpec((tk, tm), lambda i, j, k: (k, i))
        b_spec = pl.BlockSpec((tk, tn), lambda i, j, k: (k, j))
    o_spec = pl.BlockSpec((tm, tn), lambda i, j, k: (i, j))
    in_specs = [a_spec, b_spec] + ([o_spec] if addend is not None else [])
    args = (a, b) + ((addend,) if addend is not None else ())
    out_shape = jax.ShapeDtypeStruct((M, N), out_dtype)
    scratch = [pltpu.VMEM((tm, tn), F32)] if nk > 1 else []
    if comm is None:
        return pl.pallas_call(
            body, name=name, grid=grid, in_specs=in_specs, out_specs=o_spec, out_shape=out_shape,
            scratch_shapes=scratch, compiler_params=_cparams(("parallel", "parallel", "arbitrary")),
        )(*args)
    outs = pl.pallas_call(
        body, name=name, grid=grid, in_specs=in_specs + [ANY] * nc, out_specs=[o_spec] + [ANY] * nc,
        out_shape=[out_shape] + comm.out_shapes, scratch_shapes=scratch + comm.scratch,
        compiler_params=_cparams(("arbitrary", "arbitrary", "arbitrary")),
    )(*args, *comm.inputs)
    return outs[0], outs[1:]


def _rms_fwd(x, g, *, name, tm=512):
    M, D = x.shape
    tm = min(tm, M)

    def body(x_ref, g_ref, n_ref):
        xf = x_ref[...]
        r = lax.rsqrt(jnp.mean(xf * xf, axis=-1, keepdims=True) + EPS)
        n_ref[...] = (xf * r * g_ref[...]).astype(n_ref.dtype)

    return pl.pallas_call(
        body, name=name, grid=(M // tm,),
        in_specs=[pl.BlockSpec((tm, D), lambda i: (i, 0)), pl.BlockSpec((1, D), lambda i: (0, 0))],
        out_specs=pl.BlockSpec((tm, D), lambda i: (i, 0)),
        out_shape=jax.ShapeDtypeStruct((M, D), BF16),
        compiler_params=_cparams(("parallel",)),
    )(x, g.reshape(1, D))


def _rms_bwd(x, g, dn, dres, *, name, tm=512):
    M, D = x.shape
    tm = min(tm, M)

    def body(x_ref, g_ref, dn_ref, dres_ref, dx_ref, dg_ref):
        @pl.when(pl.program_id(0) == 0)
        def _():
            dg_ref[...] = jnp.zeros_like(dg_ref)

        xf = x_ref[...]
        r = lax.rsqrt(jnp.mean(xf * xf, axis=-1, keepdims=True) + EPS)
        xh = xf * r
        dn_ = dn_ref[...].astype(F32)
        dg_ref[...] += jnp.sum(dn_ * xh, axis=0, keepdims=True)
        dxh = dn_ * g_ref[...]
        dx = r * (dxh - xh * jnp.mean(dxh * xh, axis=-1, keepdims=True))
        dx_ref[...] = dres_ref[...] + dx

    row = pl.BlockSpec((tm, D), lambda i: (i, 0))
    vec = pl.BlockSpec((1, D), lambda i: (0, 0))
    return pl.pallas_call(
        body, name=name, grid=(M // tm,),
        in_specs=[row, vec, row, row], out_specs=[row, vec],
        out_shape=[jax.ShapeDtypeStruct((M, D), F32), jax.ShapeDtypeStruct((1, D), F32)],
        compiler_params=_cparams(("arbitrary",)),
    )(x, g.reshape(1, D), dn, dres)


def _loss_head(h, g, tgt, *, name, tm=512):
    M, D = h.shape
    tm = min(tm, M)

    def body(h_ref, g_ref, t_ref, loss_ref, dh_ref, dg_ref):
        @pl.when(pl.program_id(0) == 0)
        def _():
            dg_ref[...] = jnp.zeros_like(dg_ref)
            loss_ref[...] = jnp.zeros_like(loss_ref)

        xf = h_ref[...]
        r = lax.rsqrt(jnp.mean(xf * xf, axis=-1, keepdims=True) + EPS)
        xh = xf * r
        err = xh * g_ref[...] - t_ref[...]
        part = jnp.sum(jnp.mean(err * err, axis=-1, keepdims=True), axis=0, keepdims=True)
        loss_ref[...] += 0.5 * part
        dy = err * (1.0 / D)
        dg_ref[...] += jnp.sum(dy * xh, axis=0, keepdims=True)
        dxh = dy * g_ref[...]
        dh_ref[...] = r * (dxh - xh * jnp.mean(dxh * xh, axis=-1, keepdims=True))

    row = pl.BlockSpec((tm, D), lambda i: (i, 0))
    vec = pl.BlockSpec((1, D), lambda i: (0, 0))
    one = pl.BlockSpec((1, 1), lambda i: (0, 0))
    return pl.pallas_call(
        body, name=name, grid=(M // tm,),
        in_specs=[row, vec, row], out_specs=[one, row, vec],
        out_shape=[jax.ShapeDtypeStruct((1, 1), F32), jax.ShapeDtypeStruct((M, D), F32),
                   jax.ShapeDtypeStruct((1, D), F32)],
        compiler_params=_cparams(("arbitrary",)),
    )(h, g.reshape(1, D), tgt)


HG_MID = HG_CHUNK // 2 - 1
EXP_CAP = 80.0


def _sigmoid(x):
    return 1.0 / (1.0 + jnp.exp(-x))


def _dot(a, b, dims, precision=None):
    return lax.dot_general(a, b, dims, preferred_element_type=F32, precision=precision)


def _bdot(a, b, form):
    return _dot(a.astype(BF16), b.astype(BF16), _DIMS[form])


def _split2(x):
    hi = x.astype(BF16)
    return hi, (x - hi.astype(F32)).astype(BF16)


def _dot3(a, b, form):
    d = _DIMS[form]
    return _dot(a[0], b[0], d) + (_dot(a[0], b[1], d) + _dot(a[1], b[0], d))


def _hgrn_chunk_common(hq, hf, lbv, tril, rid):
    sq = _sigmoid(hq)
    q = hq * sq
    sg = _sigmoid(hf)
    f = lbv + (1.0 - lbv) * sg
    k = (1.0 - lbv) * (1.0 - sg)
    g = jnp.log(f)
    b = _dot(tril, g, _DIMS["nn"], precision=lax.Precision.HIGHEST)
    bref = jnp.sum(jnp.where(rid == HG_MID, b, 0.0), axis=0, keepdims=True)
    bend = jnp.sum(jnp.where(rid == HG_CHUNK - 1, b, 0.0), axis=0, keepdims=True)
    eb = jnp.exp(b)
    e1 = jnp.exp(jnp.minimum(b - bref, EXP_CAP))
    e2 = jnp.exp(jnp.minimum(bref - b, EXP_CAP))
    e3 = jnp.exp(bend - b)
    return sq, q, sg, f, k, bend, eb, e1, e2, e3


def _hgrn_fwd(proj, lb, gnorm, *, name, T=512):
    S = proj.shape[0]
    T = min(T, S)
    nch = T // HG_CHUNK
    C = HG_CHUNK

    def body(hq_ref, hf_ref, hi_ref, hg_ref, lb_ref, gn_ref, o_ref, oa_ref, st_ref, state):
        @pl.when(pl.program_id(1) == 0)
        def _():
            state[...] = jnp.zeros_like(state)

        lbv = lb_ref[...]
        gn = gn_ref[...]
        row = lax.broadcasted_iota(jnp.int32, (C, C), 0)
        col = lax.broadcasted_iota(jnp.int32, (C, C), 1)
        causal = row >= col
        tril = causal.astype(F32)
        rid = lax.broadcasted_iota(jnp.int32, (C, HG_DK), 0)
        sls = [pl.ds(c * C, C) for c in range(nch)]
        pre = [_hgrn_chunk_common(hq_ref[sl, :], hf_ref[sl, :], lbv, tril, rid) for sl in sls]
        v_l = [hi_ref[sl, :].astype(BF16) for sl in sls]
        a_l, u_l = [], []
        for c in range(nch):
            _, q, _, _, k, _, _, e1, e2, e3 = pre[c]
            a_l.append(jnp.where(causal, _bdot(q * e1, k * e2, "nt"), 0.0))
            u_l.append(_bdot(v_l[c], k * e3, "tn"))
        o_l = [_bdot(a_l[c], v_l[c], "nn") for c in range(nch)]
        st = state[...]
        st_l = []
        for c in range(nch):
            st_l.append(st)
            st = st * jnp.exp(pre[c][5]) + u_l[c]
        state[...] = st
        for c in range(nch):
            st_ref[0, c] = st_l[c]
            o_l[c] = o_l[c] + _bdot(pre[c][1] * pre[c][6], st_l[c], "nt")
        for c in range(nch):
            o, hg = o_l[c], hg_ref[sls[c], :]
            o_ref[sls[c], :] = o
            r = lax.rsqrt(jnp.mean(o * o, axis=-1, keepdims=True) + EPS)
            oa_ref[sls[c], :] = (o * r * gn * (hg * _sigmoid(hg))).astype(oa_ref.dtype)

    def grp(gidx):
        return pl.BlockSpec((T, 128), lambda h, t: (t, gidx * 8 + h))

    return pl.pallas_call(
        body, name=name, grid=(HG_HEADS, S // T),
        in_specs=[grp(0), grp(1), grp(2), grp(3),
                  pl.BlockSpec((1, 128), lambda h, t: (0, h)), pl.BlockSpec((1, 128), lambda h, t: (0, 0))],
        out_specs=[pl.BlockSpec((T, 128), lambda h, t: (t, h)), pl.BlockSpec((T, 128), lambda h, t: (t, h)),
                   pl.BlockSpec((1, nch, HG_DV, HG_DK), lambda h, t: (h, t, 0, 0))],
        out_shape=[jax.ShapeDtypeStruct((S, HG_HEADS * HG_DV), F32), jax.ShapeDtypeStruct((S, HG_HEADS * HG_DV), BF16),
                   jax.ShapeDtypeStruct((HG_HEADS, S // C, HG_DV, HG_DK), F32)],
        scratch_shapes=[pltpu.VMEM((HG_DV, HG_DK), F32)],
        compiler_params=_cparams(("parallel", "arbitrary")),
    )(proj, proj, proj, proj, lb, gnorm)


def _hgrn_bwd(proj, lb, gnorm, o, states, doa, *, name, T=512):
    S = proj.shape[0]
    T = min(T, S)
    nch = T // HG_CHUNK
    C = HG_CHUNK
    nT = S // T

    def body(hq_ref, hf_ref, hi_ref, hg_ref, lb_ref, gn_ref, o_ref, st_ref, doa_ref,
             dhq_ref, dhf_ref, dhi_ref, dhg_ref, dlb_ref, dgn_ref, dstate):
        @pl.when(pl.program_id(1) == 0)
        def _():
            dstate[...] = jnp.zeros_like(dstate)
            dlb_ref[...] = jnp.zeros_like(dlb_ref)
            dgn_ref[...] = jnp.zeros_like(dgn_ref)

        lbv = lb_ref[...]
        gn = gn_ref[...]
        row = lax.broadcasted_iota(jnp.int32, (C, C), 0)
        col = lax.broadcasted_iota(jnp.int32, (C, C), 1)
        causal = row >= col
        tril = causal.astype(F32)
        triu = (row <= col).astype(F32)
        rid = lax.broadcasted_iota(jnp.int32, (C, HG_DK), 0)
        rng = range(nch)
        sls = [pl.ds(c * C, C) for c in rng]
        pre = [_hgrn_chunk_common(hq_ref[sl, :], hf_ref[sl, :], lbv, tril, rid) for sl in sls]
        do2, dgn_acc = [], jnp.zeros((1, HG_DV), F32)
        for c in rng:
            hg, ov = hg_ref[sls[c], :], o_ref[sls[c], :]
            r = lax.rsqrt(jnp.mean(ov * ov, axis=-1, keepdims=True) + EPS)
            xh = ov * r
            sgg = _sigmoid(hg)
            d_oa = doa_ref[sls[c], :].astype(F32)
            dz = d_oa * (hg * sgg)
            dhg_ref[sls[c], :] = (d_oa * (xh * gn) * (sgg * (1.0 + hg * (1.0 - sgg)))).astype(dhg_ref.dtype)
            dgn_acc = dgn_acc + jnp.sum(dz * xh, axis=0, keepdims=True)
            dxh = dz * gn
            do2.append(_split2(r * (dxh - xh * jnp.mean(dxh * xh, axis=-1, keepdims=True))))
        dgn_ref[0] += dgn_acc
        qi = [pre[c][1] * pre[c][6] for c in rng]
        qp = [pre[c][1] * pre[c][7] for c in rng]
        kp = [pre[c][4] * pre[c][8] for c in rng]
        kend = [pre[c][4] * pre[c][9] for c in rng]
        qi2, qp2, kp2, kend2 = ([_split2(t) for t in lst] for lst in (qi, qp, kp, kend))
        v2 = [_split2(hi_ref[sl, :]) for sl in sls]
        st0 = [st_ref[0, c] for c in rng]
        a_l = [jnp.where(causal, _dot(qp2[c][0], kp2[c][0], _DIMS["nt"]), 0.0).astype(BF16) for c in rng]
        da2 = [_split2(jnp.where(causal, _dot3(do2[c], v2[c], "nt"), 0.0)) for c in rng]
        dqi = [_dot3(do2[c], _split2(st0[c]), "nn") for c in rng]
        w_l = [_dot3(do2[c], qi2[c], "tn") for c in rng]
        ds = dstate[...]
        ds1 = [None] * nch
        for c in reversed(rng):
            ds1[c] = ds
            ds = ds * jnp.exp(pre[c][5]) + w_l[c]
        dstate[...] = ds
        ds12 = [_split2(t) for t in ds1]
        dqp = [_dot3(da2[c], kp2[c], "nn") for c in rng]
        dkp = [_dot3(da2[c], qp2[c], "tn") for c in rng]
        dv = [_dot(a_l[c], do2[c][0], _DIMS["tn"]) + _dot(kend2[c][0], ds12[c][0], _DIMS["nt"]) for c in rng]
        dkend = [_dot3(v2[c], ds12[c], "nn") for c in rng]
        dq_l, dk_l, db_l = [], [], []
        for c in rng:
            _, _, _, _, _, bend, eb, e1, e2, e3 = pre[c]
            dq_l.append(dqi[c] * eb + dqp[c] * e1)
            dk_l.append(dkp[c] * e2 + dkend[c] * e3)
            db = dqi[c] * qi[c] + dqp[c] * qp[c] - dkp[c] * kp[c] - dkend[c] * kend[c]
            dbend = (jnp.sum(dkend[c] * kend[c], axis=0, keepdims=True)
                     + jnp.exp(bend) * jnp.sum(ds1[c] * st0[c], axis=0, keepdims=True))
            db_l.append(db + jnp.where(rid == C - 1, dbend, 0.0))
        dg = [_dot(triu, db_l[c], _DIMS["nn"], precision=lax.Precision.HIGHEST) for c in rng]
        dlb_acc = jnp.zeros((1, HG_DK), F32)
        for c in rng:
            sq, _, sg, f, _, _, _, _, _, _ = pre[c]
            hq = hq_ref[sls[c], :]
            df = dg[c] / f - dk_l[c]
            dlb_acc = dlb_acc + jnp.sum(df * (1.0 - sg), axis=0, keepdims=True)
            dhf_ref[sls[c], :] = (df * (1.0 - lbv) * sg * (1.0 - sg)).astype(dhf_ref.dtype)
            dhq_ref[sls[c], :] = (dq_l[c] * (sq * (1.0 + hq * (1.0 - sq)))).astype(dhq_ref.dtype)
            dhi_ref[sls[c], :] = dv[c].astype(dhi_ref.dtype)
        dlb_ref[...] += dlb_acc

    def grp(gidx):
        return pl.BlockSpec((T, 128), lambda h, t: (nT - 1 - t, gidx * 8 + h))

    tok = pl.BlockSpec((T, 128), lambda h, t: (nT - 1 - t, h))
    big = jax.ShapeDtypeStruct((S, HG_HEADS * HG_DV), BF16)
    return pl.pallas_call(
        body, name=name, grid=(HG_HEADS, nT),
        in_specs=[grp(0), grp(1), grp(2), grp(3),
                  pl.BlockSpec((1, 128), lambda h, t: (0, h)), pl.BlockSpec((1, 128), lambda h, t: (0, 0)),
                  tok, pl.BlockSpec((1, nch, HG_DV, HG_DK), lambda h, t: (h, nT - 1 - t, 0, 0)), tok],
        out_specs=[tok, tok, tok, tok, pl.BlockSpec((1, 128), lambda h, t: (0, h)),
                   pl.BlockSpec((1, 1, 128), lambda h, t: (h, 0, 0))],
        out_shape=[big, big, big, big, jax.ShapeDtypeStruct((1, HG_HEADS * HG_DK), F32),
                   jax.ShapeDtypeStruct((HG_HEADS, 1, HG_DV), F32)],
        scratch_shapes=[pltpu.VMEM((HG_DV, HG_DK), F32)],
        compiler_params=_cparams(("parallel", "arbitrary")),
    )(proj, proj, proj, proj, lb, gnorm, o, states, doa)


def _lb_fwd(logits, *, name):
    def body(l_ref, lb_ref):
        lb_ref[...] = _sigmoid(l_ref[0:1, :] - l_ref[1:2, :])

    return pl.pallas_call(body, name=name, out_shape=jax.ShapeDtypeStruct((1, logits.shape[1]), F32))(logits)


def _lb_bwd(logits, dlb, *, name):
    def body(l_ref, d_ref, o_ref):
        lbv = _sigmoid(l_ref[0:1, :] - l_ref[1:2, :])
        t = d_ref[...] * lbv * (1.0 - lbv)
        o_ref[0:1, :] = t
        o_ref[1:2, :] = -t

    return pl.pallas_call(body, name=name, out_shape=jax.ShapeDtypeStruct(logits.shape, F32))(logits, dlb)


NEG = -1e30
FOX_SCALE = FOX_DH ** -0.5
FOX_PAIRS = FOX_HEADS // 2


def _fox_gate_fwd(ff, bias, *, name, T=512):
    S = ff.shape[0]
    T = min(T, S)

    def body(ff_ref, b_ref, c_ref, carry):
        @pl.when(pl.program_id(0) == 0)
        def _():
            carry[...] = jnp.zeros_like(carry)

        z = ff_ref[...] + b_ref[...]
        logf = jnp.minimum(z, 0.0) - jnp.log(1.0 + jnp.exp(-jnp.abs(z)))
        row = lax.broadcasted_iota(jnp.int32, (T, T), 0)
        col = lax.broadcasted_iota(jnp.int32, (T, T), 1)
        c = _dot((row >= col).astype(F32), logf, _DIMS["nn"], precision=lax.Precision.HIGHEST) + carry[...]
        c_ref[...] = c
        carry[...] = c[T - 1:T, :]

    return pl.pallas_call(
        body, name=name, grid=(S // T,),
        in_specs=[pl.BlockSpec((T, 128), lambda i: (i, 0)), pl.BlockSpec((1, 128), lambda i: (0, 0))],
        out_specs=pl.BlockSpec((T, 128), lambda i: (i, 0)),
        out_shape=jax.ShapeDtypeStruct((S, 128), F32),
        scratch_shapes=[pltpu.VMEM((1, 128), F32)],
        compiler_params=_cparams(("arbitrary",)),
    )(ff, bias)


def _fox_gate_bwd(ff, bias, dcs, *, name, T=512):
    S = ff.shape[0]
    T = min(T, S)
    nT = S // T

    def body(ff_ref, b_ref, d_ref, dff_ref, db_ref, carry):
        @pl.when(pl.program_id(0) == 0)
        def _():
            carry[...] = jnp.zeros_like(carry)
            db_ref[...] = jnp.zeros_like(db_ref)

        row = lax.broadcasted_iota(jnp.int32, (T, T), 0)
        col = lax.broadcasted_iota(jnp.int32, (T, T), 1)
        dlogf = carry[...] - _dot((row <= col).astype(F32), d_ref[...], _DIMS["nn"], precision=lax.Precision.HIGHEST)
        carry[...] = dlogf[0:1, :]
        dff = dlogf * (1.0 - _sigmoid(ff_ref[...] + b_ref[...]))
        dff_ref[...] = dff.astype(dff_ref.dtype)
        db_ref[...] += jnp.sum(dff, axis=0, keepdims=True)

    rev = pl.BlockSpec((T, 128), lambda i: (nT - 1 - i, 0))
    vec = pl.BlockSpec((1, 128), lambda i: (0, 0))
    return pl.pallas_call(
        body, name=name, grid=(nT,),
        in_specs=[rev, vec, rev], out_specs=[rev, vec],
        out_shape=[jax.ShapeDtypeStruct((S, 128), BF16), jax.ShapeDtypeStruct((1, 128), F32)],
        scratch_shapes=[pltpu.VMEM((1, 128), F32)],
        compiler_params=_cparams(("arbitrary",)),
    )(ff, bias, dcs)


AUG = FOX_DH


def _split3(x):
    a = x.astype(BF16).astype(F32)
    r = x - a
    b = r.astype(BF16).astype(F32)
    return a, b, r - b


def _lane_fill(lane, base, pieces, start):
    for i, pc in enumerate(pieces):
        base = jnp.where(lane == start + i, pc, base)
    return base


FOX_TB = 512
FOX_SKIP = 60.0
N_STAT = 4


def _fox_prep(proj, c_tok, *, name):
    S = proj.shape[0]
    T = min(FOX_TB, S)

    def body(q_ref, k_ref, v_ref, c_ref, qa_ref, ka_ref, va_ref, st_ref):
        pair = pl.program_id(0)
        lane = lax.broadcasted_iota(jnp.int32, (T, 128), 1)
        lane1 = lax.broadcasted_iota(jnp.int32, (1, 128), 1)
        c = c_ref[...]
        ones3 = jnp.where((lane >= AUG) & (lane < AUG + 3), 1.0, 0.0)

        def max_norm(t):
            tr = jnp.where(lane < AUG, t.astype(BF16).astype(F32), 0.0)
            return jnp.sqrt(jnp.max(jnp.sum(tr * tr, axis=-1, keepdims=True), axis=0, keepdims=True))

        for hh in range(2):
            ch = jnp.sum(jnp.where(lane == 2 * pair + hh, c, 0.0), axis=-1, keepdims=True)
            c1, c2, c3 = _split3(ch)
            q, k, v = q_ref[...], k_ref[...], v_ref[...]
            if hh == 1:
                q, k, v = (pltpu.roll(t, 64, 1) for t in (q, k, v))
            aug_q = _lane_fill(lane, jnp.where((lane >= AUG + 3) & (lane < AUG + 6), 1.0, 0.0), (c1, c2, c3), AUG)
            aug_k = _lane_fill(lane, ones3, (-c1, -c2, -c3), AUG + 3)
            qa_ref[hh] = jnp.where(lane < AUG, q * FOX_SCALE, aug_q).astype(BF16)
            ka_ref[hh] = jnp.where(lane < AUG, k, aug_k).astype(BF16)
            va_ref[hh] = jnp.where(lane < AUG, v, ones3).astype(BF16)
            stats = (max_norm(q * FOX_SCALE), jnp.max(ch, axis=0, keepdims=True), max_norm(k),
                     jnp.min(ch, axis=0, keepdims=True))
            st_ref[hh, 0] = _lane_fill(lane1, jnp.zeros((1, 128), F32), stats, 0)

    def grp(g):
        return pl.BlockSpec((T, 128), lambda p, t: (t, g * 8 + p))

    hm = pl.BlockSpec((2, T, 128), lambda p, t: (p, t, 0))
    out = jax.ShapeDtypeStruct((FOX_HEADS, S, 128), BF16)
    return pl.pallas_call(
        body, name=name, grid=(FOX_PAIRS, S // T),
        in_specs=[grp(4), grp(5), grp(6), pl.BlockSpec((T, 128), lambda p, t: (t, 0))],
        out_specs=[hm, hm, hm, pl.BlockSpec((2, 1, 1, 128), lambda p, t: (p, t, 0, 0))],
        out_shape=[out, out, out, jax.ShapeDtypeStruct((FOX_HEADS, S // T, 1, 128), F32)],
        compiler_params=_cparams(("parallel", "parallel")),
    )(proj, proj, proj, c_tok)


def _fox_bound(st_ref, head, nb, qi, ki):
    qb_, kb_ = (head * nb + qi) * N_STAT, (head * nb + ki) * N_STAT
    return st_ref[qb_] * st_ref[kb_ + 2] + st_ref[qb_ + 1] - st_ref[kb_ + 3] + 0.01


def _pair_lanes(lane, a0, a1):
    return jnp.where(lane < AUG, a0, pltpu.roll(a1, 64, 1))


def _tri_tables(nb, by_query):
    if by_query:
        pairs = [(i, j) for i in range(nb) for j in range(i, -1, -1)]
    else:
        pairs = [(i, j) for j in range(nb) for i in range(j, nb)]
    return (jnp.asarray(np.array([p[0] for p in pairs], np.int32)),
            jnp.asarray(np.array([p[1] for p in pairs], np.int32)))


def _fox_fwd(qa, ka, va, bounds, *, name):
    S = qa.shape[1]
    tb = min(FOX_TB, S)
    nb = S // tb
    qtab, ktab = _tri_tables(nb, True)

    def body(qt_ref, kt_ref, qa_ref, ka_ref, va_ref, st_ref, o_ref, qb_ref, lse_ref, m_s, acc_s, m_min):
        qi, ki = qt_ref[pl.program_id(1)], kt_ref[pl.program_id(1)]

        def head_step(hh, masked):
            s = _dot(qa_ref[hh], ka_ref[hh], _DIMS["nt"])
            if masked:
                row = lax.broadcasted_iota(jnp.int32, (tb, tb), 0)
                col = lax.broadcasted_iota(jnp.int32, (tb, tb), 1)
                s = jnp.where(col <= row, s, NEG)
            m_old = m_s[hh]
            m_new = jnp.maximum(m_old, jnp.max(s, axis=-1, keepdims=True))
            p = jnp.exp(s - m_new)
            p_hi = p.astype(BF16)
            p_lo = (p - p_hi.astype(F32)).astype(BF16)
            vv = va_ref[hh]
            acc_s[hh] = (jnp.exp(m_old - m_new) * acc_s[hh]
                         + _dot(p_hi, vv, _DIMS["nn"]) + _dot(p_lo, vv, _DIMS["nn"]))
            m_s[hh] = m_new
            m_min[hh] = jnp.min(m_new)

        @pl.when(ki == qi)
        def _():
            m_s[...] = jnp.full_like(m_s, NEG)
            acc_s[...] = jnp.zeros_like(acc_s)
            for hh in range(2):
                head_step(hh, True)

        for hh in range(2):
            bound = _fox_bound(st_ref, 2 * pl.program_id(0) + hh, nb, qi, ki)

            @pl.when((ki < qi) & (bound > m_min[hh] - FOX_SKIP))
            def _():
                head_step(hh, False)

        @pl.when(ki == 0)
        def _():
            lane = lax.broadcasted_iota(jnp.int32, (tb, 128), 1)
            outs = []
            for hh in range(2):
                acc = acc_s[hh]
                l = acc[:, AUG:AUG + 1]
                outs.append(acc / l)
                lse = m_s[hh] + jnp.log(l)
                lse_ref[hh, 0] = jnp.broadcast_to(jnp.min(lse, axis=0, keepdims=True), (1, 128))
                qf = qa_ref[hh].astype(F32)
                cb = qf[:, AUG:AUG + 1] + qf[:, AUG + 1:AUG + 2] + qf[:, AUG + 2:AUG + 3] - lse
                qb_ref[hh] = _lane_fill(lane, qf, _split3(cb), AUG).astype(BF16)
            o_ref[...] = _pair_lanes(lane, outs[0], outs[1])

    qs = pl.BlockSpec((2, tb, 128), lambda p, t, qt, kt: (p, qt[t], 0))
    ks = pl.BlockSpec((2, tb, 128), lambda p, t, qt, kt: (p, kt[t], 0))
    return pl.pallas_call(
        body, name=name,
        grid_spec=pltpu.PrefetchScalarGridSpec(
            num_scalar_prefetch=2, grid=(FOX_PAIRS, qtab.shape[0]),
            in_specs=[qs, ks, ks, SMEM],
            out_specs=[pl.BlockSpec((tb, 128), lambda p, t, qt, kt: (qt[t], p)), qs,
                       pl.BlockSpec((2, 1, 1, 128), lambda p, t, qt, kt: (p, qt[t], 0, 0))],
            scratch_shapes=[pltpu.VMEM((2, tb, 1), F32), pltpu.VMEM((2, tb, 128), F32), pltpu.SMEM((2,), F32)]),
        out_shape=[jax.ShapeDtypeStruct((S, FOX_HEADS * FOX_DH), F32), jax.ShapeDtypeStruct((FOX_HEADS, S, 128), BF16),
                   jax.ShapeDtypeStruct((FOX_HEADS, nb, 1, 128), F32)],
        compiler_params=_cparams(("parallel", "arbitrary")),
    )(qtab, ktab, qa, ka, va, bounds)


def _fox_bwd_prep(o, do, *, name, T=512):
    S = o.shape[0]
    T = min(T, S)

    def body(o_ref, do_ref, dob_ref):
        lane = lax.broadcasted_iota(jnp.int32, (T, 128), 1)
        d = do_ref[...].astype(F32)
        prod = d * o_ref[...]
        for hh in range(2):
            mine = (lane < AUG) if hh == 0 else (lane >= AUG)
            delta = jnp.sum(jnp.where(mine, prod, 0.0), axis=-1, keepdims=True)
            dh = d if hh == 0 else pltpu.roll(d, 64, 1)
            dob_ref[hh] = _lane_fill(lane, jnp.where(lane < AUG, dh, 0.0), _split3(-delta), AUG).astype(BF16)

    tok = pl.BlockSpec((T, 128), lambda p, t: (t, p))
    return pl.pallas_call(
        body, name=name, grid=(FOX_PAIRS, S // T),
        in_specs=[tok, tok], out_specs=pl.BlockSpec((2, T, 128), lambda p, t: (p, t, 0)),
        out_shape=jax.ShapeDtypeStruct((FOX_HEADS, S, 128), BF16),
        compiler_params=_cparams(("parallel", "parallel")),
    )(o, do)


def _fox_bwd_dq(qb, ka, va, dob, bounds, lse_min, *, name, comm=None):
    S = qb.shape[1]
    tb = min(FOX_TB, S)
    nb = S // tb
    qtab, ktab = _tri_tables(nb, True)
    nc = comm.n if comm is not None else 0
    ntri = qtab.shape[0]

    def body(qt_ref, kt_ref, qb_ref, ka_ref, va_ref, dob_ref, st_ref, lm_ref, *rest):
        c_in, (dq_ref, dcs_ref), c_out = rest[:nc], rest[nc:nc + 2], rest[nc + 2:2 * nc + 2]
        acc_s, c_sems = rest[2 * nc + 2], rest[2 * nc + 3:]
        qi, ki = qt_ref[pl.program_id(1)], kt_ref[pl.program_id(1)]
        if comm is not None:
            @pl.when((pl.program_id(0) == 0) & (pl.program_id(1) == 0))
            def _():
                comm.start(c_in, c_out, c_sems)

        def head_step(hh, masked):
            s = _dot(qb_ref[hh], ka_ref[hh], _DIMS["nt"])
            if masked:
                row = lax.broadcasted_iota(jnp.int32, (tb, tb), 0)
                col = lax.broadcasted_iota(jnp.int32, (tb, tb), 1)
                s = jnp.where(col <= row, s, NEG)
            ds = jnp.exp(s) * _dot(dob_ref[hh], va_ref[hh], _DIMS["nt"])
            dcs_ref[0, 0, hh:hh + 1, :] = jnp.sum(ds, axis=0, keepdims=True)
            acc_s[hh] += _dot(ds.astype(BF16), ka_ref[hh], _DIMS["nn"])

        @pl.when(ki == qi)
        def _():
            acc_s[...] = jnp.zeros_like(acc_s)
            for hh in range(2):
                head_step(hh, True)

        for hh in range(2):
            head = 2 * pl.program_id(0) + hh
            live = _fox_bound(st_ref, head, nb, qi, ki) > lm_ref[head * nb + qi] - FOX_SKIP

            @pl.when((ki < qi) & live)
            def _():
                head_step(hh, False)

            @pl.when((ki < qi) & jnp.logical_not(live))
            def _():
                dcs_ref[0, 0, hh:hh + 1, :] = jnp.zeros((1, tb), F32)

        @pl.when(ki == 0)
        def _():
            lane = lax.broadcasted_iota(jnp.int32, (tb, 128), 1)
            dq_ref[...] = (_pair_lanes(lane, acc_s[0], acc_s[1]) * FOX_SCALE).astype(dq_ref.dtype)

        if comm is not None:
            @pl.when((pl.program_id(0) == FOX_PAIRS - 1) & (pl.program_id(1) == ntri - 1))
            def _():
                comm.finish(c_in, c_out, c_sems)

    qs = pl.BlockSpec((2, tb, 128), lambda p, t, qt, kt: (p, qt[t], 0))
    ks = pl.BlockSpec((2, tb, 128), lambda p, t, qt, kt: (p, kt[t], 0))
    outs = pl.pallas_call(
        body, name=name,
        grid_spec=pltpu.PrefetchScalarGridSpec(
            num_scalar_prefetch=2, grid=(FOX_PAIRS, ntri), in_specs=[qs, ks, ks, qs, SMEM, SMEM] + [ANY] * nc,
            out_specs=[pl.BlockSpec((tb, 128), lambda p, t, qt, kt: (qt[t], p)),
                       pl.BlockSpec((1, 1, 2, tb), lambda p, t, qt, kt: (p, qt[t], 0, kt[t]))] + [ANY] * nc,
            scratch_shapes=[pltpu.VMEM((2, tb, 128), F32)] + (comm.scratch if comm is not None else [])),
        out_shape=[jax.ShapeDtypeStruct((S, FOX_HEADS * FOX_DH), BF16),
                   jax.ShapeDtypeStruct((FOX_PAIRS, nb, 2, S), F32)] + (comm.out_shapes if comm is not None else []),
        compiler_params=_cparams(("parallel", "arbitrary") if comm is None else ("arbitrary", "arbitrary")),
    )(qtab, ktab, qb, ka, va, dob, bounds, lse_min, *(comm.inputs if comm is not None else []))
    return (outs[0], outs[1]) if comm is None else (outs[0], outs[1], outs[2:])


def _fox_bwd_dkv(qb, ka, va, dob, bounds, lse_min, *, name):
    S = qb.shape[1]
    tb = min(FOX_TB, S)
    nb = S // tb
    qtab, ktab = _tri_tables(nb, False)

    def body(qt_ref, kt_ref, qb_ref, ka_ref, va_ref, dob_ref, st_ref, lm_ref, dk_ref, dv_ref, dk_s, dv_s):
        qi, ki = qt_ref[pl.program_id(1)], kt_ref[pl.program_id(1)]

        def head_step(hh, masked):
            st = _dot(ka_ref[hh], qb_ref[hh], _DIMS["nt"])
            if masked:
                row = lax.broadcasted_iota(jnp.int32, (tb, tb), 0)
                col = lax.broadcasted_iota(jnp.int32, (tb, tb), 1)
                st = jnp.where(row <= col, st, NEG)
            pt = jnp.exp(st)
            dst = pt * _dot(va_ref[hh], dob_ref[hh], _DIMS["nt"])
            dv_s[hh] += _dot(pt.astype(BF16), dob_ref[hh], _DIMS["nn"])
            dk_s[hh] += _dot(dst.astype(BF16), qb_ref[hh], _DIMS["nn"])

        @pl.when(qi == ki)
        def _():
            dk_s[...] = jnp.zeros_like(dk_s)
            dv_s[...] = jnp.zeros_like(dv_s)
            for hh in range(2):
                head_step(hh, True)

        for hh in range(2):
            head = 2 * pl.program_id(0) + hh
            live = _fox_bound(st_ref, head, nb, qi, ki) > lm_ref[head * nb + qi] - FOX_SKIP

            @pl.when((qi > ki) & live)
            def _():
                head_step(hh, False)

        @pl.when(qi == nb - 1)
        def _():
            lane = lax.broadcasted_iota(jnp.int32, (tb, 128), 1)
            dk_ref[...] = _pair_lanes(lane, dk_s[0], dk_s[1]).astype(dk_ref.dtype)
            dv_ref[...] = _pair_lanes(lane, dv_s[0], dv_s[1]).astype(dv_ref.dtype)

    ks = pl.BlockSpec((2, tb, 128), lambda p, t, qt, kt: (p, kt[t], 0))
    qs = pl.BlockSpec((2, tb, 128), lambda p, t, qt, kt: (p, qt[t], 0))
    tok = pl.BlockSpec((tb, 128), lambda p, t, qt, kt: (kt[t], p))
    big = jax.ShapeDtypeStruct((S, FOX_HEADS * FOX_DH), BF16)
    return pl.pallas_call(
        body, name=name,
        grid_spec=pltpu.PrefetchScalarGridSpec(
            num_scalar_prefetch=2, grid=(FOX_PAIRS, qtab.shape[0]), in_specs=[qs, ks, ks, qs, SMEM, SMEM],
            out_specs=[tok, tok],
            scratch_shapes=[pltpu.VMEM((2, tb, 128), F32), pltpu.VMEM((2, tb, 128), F32)]),
        out_shape=[big, big],
        compiler_params=_cparams(("parallel", "arbitrary")),
    )(qtab, ktab, qb, ka, va, dob, bounds, lse_min)


def _merge_fwd(proj, pa, pb, *, name, T=512):
    S, D = pa.shape
    T = min(T, S)

    def body(ga_ref, gb_ref, pa_ref, pb_ref, m_ref):
        m_ref[...] = (_sigmoid(ga_ref[...]) * pa_ref[...] + _sigmoid(gb_ref[...]) * pb_ref[...]).astype(m_ref.dtype)

    tok = pl.BlockSpec((T, D), lambda i: (i, 0))
    return pl.pallas_call(
        body, name=name, grid=(S // T,),
        in_specs=[pl.BlockSpec((T, D), lambda i: (i, 7)), pl.BlockSpec((T, D), lambda i: (i, 8)), tok, tok],
        out_specs=tok, out_shape=jax.ShapeDtypeStruct((S, D), BF16),
        compiler_params=_cparams(("parallel",)),
    )(proj, proj, pa, pb)


def _merge_bwd(proj, pa, pb, dm, *, name, T=512):
    S, D = pa.shape
    T = min(T, S)

    def body(ga_ref, gb_ref, pa_ref, pb_ref, dm_ref, dpa_ref, dpb_ref, dga_ref, dgb_ref):
        dm_ = dm_ref[...]
        sa, sb = _sigmoid(ga_ref[...]), _sigmoid(gb_ref[...])
        dpa_ref[...] = (dm_ * sa).astype(BF16)
        dpb_ref[...] = (dm_ * sb).astype(BF16)
        dga_ref[...] = (dm_ * pa_ref[...] * sa * (1.0 - sa)).astype(BF16)
        dgb_ref[...] = (dm_ * pb_ref[...] * sb * (1.0 - sb)).astype(BF16)

    tok = pl.BlockSpec((T, D), lambda i: (i, 0))
    big = jax.ShapeDtypeStruct((S, D), BF16)
    return pl.pallas_call(
        body, name=name, grid=(S // T,),
        in_specs=[pl.BlockSpec((T, D), lambda i: (i, 7)), pl.BlockSpec((T, D), lambda i: (i, 8)), tok, tok, tok],
        out_specs=[tok, tok, tok, tok], out_shape=[big, big, big, big],
        compiler_params=_cparams(("parallel",)),
    )(proj, proj, pa, pb, dm)


INV_SQRT2 = 0.7071067811865476
INV_SQRT2PI = 0.3989422804014327


def _shifted(u, prev, rid):
    m1 = jnp.where(rid == 0, prev[7:8, :], pltpu.roll(u, 1, 0))
    m2 = jnp.where(rid == 0, prev[6:7, :], jnp.where(rid == 1, prev[7:8, :], pltpu.roll(u, 2, 0)))
    return m1, m2


def _conv_acc(u, prev, w_ref, b_ref, rid):
    m1, m2 = _shifted(u, prev, rid)
    return b_ref[...] + w_ref[0:1, :] * m2 + w_ref[1:2, :] * m1 + w_ref[2:3, :] * u, m1, m2


def _convglu_fwd(ug, uv, wg, wv, bg, bv, *, name, T=512, tc=256):
    S, F = ug.shape
    T = min(T, S)

    def body(ug_ref, uv_ref, wg_ref, wv_ref, bg_ref, bv_ref, a_ref, pg, pv):
        @pl.when(pl.program_id(1) == 0)
        def _():
            pg[...] = jnp.zeros_like(pg)
            pv[...] = jnp.zeros_like(pv)

        rid = lax.broadcasted_iota(jnp.int32, (T, tc), 0)
        g_, v_ = ug_ref[...], uv_ref[...]
        accg, _, _ = _conv_acc(g_, pg[...], wg_ref, bg_ref, rid)
        accv, _, _ = _conv_acc(v_, pv[...], wv_ref, bv_ref, rid)
        gel = 0.5 * accg * (1.0 + lax.erf(accg * INV_SQRT2))
        a_ref[...] = (gel * accv).astype(a_ref.dtype)
        pg[...] = g_[T - 8:T, :]
        pv[...] = v_[T - 8:T, :]

    tok = pl.BlockSpec((T, tc), lambda j, t: (t, j))
    w3 = pl.BlockSpec((3, tc), lambda j, t: (0, j))
    b1 = pl.BlockSpec((1, tc), lambda j, t: (0, j))
    return pl.pallas_call(
        body, name=name, grid=(F // tc, S // T),
        in_specs=[tok, tok, w3, w3, b1, b1], out_specs=tok,
        out_shape=jax.ShapeDtypeStruct((S, F), BF16),
        scratch_shapes=[pltpu.VMEM((8, tc), F32), pltpu.VMEM((8, tc), F32)],
        compiler_params=_cparams(("parallel", "arbitrary")),
    )(ug, uv, wg, wv, bg, bv)


def _convglu_bwd(ug, uv, wg, wv, bg, bv, da, *, name, T=512, tc=256):
    S, F = ug.shape
    T = min(T, S)
    nT = S // T
    halo_blocks = T // 8

    def up_shift(d, nx, rid):
        p1 = jnp.where(rid == T - 1, nx[0:1, :], pltpu.roll(d, T - 1, 0))
        p2 = jnp.where(rid == T - 1, nx[1:2, :], jnp.where(rid == T - 2, nx[0:1, :], pltpu.roll(d, T - 2, 0)))
        return p1, p2

    def body(ug_ref, uv_ref, hg_ref, hv_ref, wg_ref, wv_ref, bg_ref, bv_ref, da_ref,
             dug_ref, duv_ref, dwg_ref, dwv_ref, dbg_ref, dbv_ref, ng, nv):
        @pl.when(pl.program_id(1) == 0)
        def _():
            ng[...] = jnp.zeros_like(ng)
            nv[...] = jnp.zeros_like(nv)
            for r in (dwg_ref, dwv_ref, dbg_ref, dbv_ref):
                r[...] = jnp.zeros_like(r)

        first_block = pl.program_id(1) == nT - 1
        rid = lax.broadcasted_iota(jnp.int32, (T, tc), 0)
        g_, v_ = ug_ref[...], uv_ref[...]
        pg = jnp.where(first_block, 0.0, hg_ref[...])
        pv = jnp.where(first_block, 0.0, hv_ref[...])
        accg, g1, g2 = _conv_acc(g_, pg, wg_ref, bg_ref, rid)
        accv, v1, v2 = _conv_acc(v_, pv, wv_ref, bv_ref, rid)
        cdf = 0.5 * (1.0 + lax.erf(accg * INV_SQRT2))
        pdf = INV_SQRT2PI * jnp.exp(-0.5 * accg * accg)
        da_ = da_ref[...].astype(F32)
        dgate = da_ * accv * (cdf + accg * pdf)
        dval = da_ * (accg * cdf)
        dbg_ref[...] += jnp.sum(dgate, axis=0, keepdims=True)
        dbv_ref[...] += jnp.sum(dval, axis=0, keepdims=True)
        for j, (sg_, sv_) in enumerate(((g2, v2), (g1, v1), (g_, v_))):
            dwg_ref[j:j + 1, :] += jnp.sum(dgate * sg_, axis=0, keepdims=True)
            dwv_ref[j:j + 1, :] += jnp.sum(dval * sv_, axis=0, keepdims=True)
        for d, w_ref, nx, out_ref in ((dgate, wg_ref, ng, dug_ref), (dval, wv_ref, nv, duv_ref)):
            p1, p2 = up_shift(d, nx[...], rid)
            out_ref[...] = (w_ref[2:3, :] * d + w_ref[1:2, :] * p1 + w_ref[0:1, :] * p2).astype(out_ref.dtype)
            nx[...] = d[0:8, :]

    tok = pl.BlockSpec((T, tc), lambda j, t: (nT - 1 - t, j))
    halo = pl.BlockSpec((8, tc), lambda j, t: (jnp.maximum((nT - 1 - t) * halo_blocks - 1, 0), j))
    w3 = pl.BlockSpec((3, tc), lambda j, t: (0, j))
    b1 = pl.BlockSpec((1, tc), lambda j, t: (0, j))
    big = jax.ShapeDtypeStruct((S, F), BF16)
    return pl.pallas_call(
        body, name=name, grid=(F // tc, nT),
        in_specs=[tok, tok, halo, halo, w3, w3, b1, b1, tok], out_specs=[tok, tok, w3, w3, b1, b1],
        out_shape=[big, big, jax.ShapeDtypeStruct((3, F), F32), jax.ShapeDtypeStruct((3, F), F32),
                   jax.ShapeDtypeStruct((1, F), F32), jax.ShapeDtypeStruct((1, F), F32)],
        scratch_shapes=[pltpu.VMEM((8, tc), F32), pltpu.VMEM((8, tc), F32)],
        compiler_params=_cparams(("parallel", "arbitrary")),
    )(ug, uv, ug, uv, wg, wv, bg, bv, da)


FF_LO, FF_HI = 7168, 7184


def _col_blocks(a, width):
    return jnp.stack([a[:, d * width:(d + 1) * width] for d in range(N_DEV)])


def _late_weights(g_a, g_b, g_o, g_up, g_cw, g_d):
    wup = jnp.concatenate([g_up[d] for d in range(N_DEV)], axis=1)
    cw = jnp.concatenate([g_cw[d] for d in range(N_DEV)], axis=1)
    return dict(wa=g_a.reshape(D_MODEL, D_MODEL), wb=g_b.reshape(D_MODEL, D_MODEL), wo=g_o.reshape(D_MODEL, D_MODEL),
                wug=wup[:, :D_FF], wuv=wup[:, D_FF:], cwg=cw[:, :D_FF], cwv=cw[:, D_FF:], wd=g_d.reshape(D_FF, D_MODEL))


def _early_grad_blocks(d_wa, d_wb, d_wo, d_wug, d_wuv, d_wd):
    up = jnp.stack([d_wug[:, d * 704:(d + 1) * 704] for d in range(4)]
                   + [d_wuv[:, d * 704:(d + 1) * 704] for d in range(4)])
    return [d_wa.reshape(N_DEV, 128, D_MODEL), d_wb.reshape(N_DEV, 128, D_MODEL), d_wo.reshape(N_DEV, 128, D_MODEL),
            up, d_wd.reshape(N_DEV, 352, D_MODEL)]


def _local_step(x, tgt, w, p, late=None, exchange=False):
    S = x.shape[0]
    mm = _matmul
    n1 = _rms_fwd(x, p["norm_mix"], name="rms1_fwd")
    if late is None:
        proj = mm(n1, w["wm"], "nn", name="proj_main")
    else:
        proj, gathered = mm(n1, w["wm"], "nn", comm=late, name="proj_main")
        w = {**w, **_late_weights(*gathered)}
    ff = mm(n1, w["wff"], "nn", name="proj_ff")
    lb = _lb_fwd(p["hg_lb_logits"], name="lb_fwd")
    gnorm = p["hg_norm"].reshape(1, HG_DV)
    o_hg, oa, states = _hgrn_fwd(proj, lb, gnorm, name="hgrn_fwd")
    bias = jnp.pad(p["fox_f_bias"].reshape(1, FOX_HEADS), ((0, 0), (0, 128 - FOX_HEADS)))
    c = _fox_gate_fwd(ff, bias, name="fox_gate_fwd")
    qa, ka, va, fox_stats = _fox_prep(proj, c, name="fox_prep")
    bounds = fox_stats[:, :, 0, :N_STAT].reshape(-1)
    ob, qb, lse_stats = _fox_fwd(qa, ka, va, bounds, name="fox_fwd")
    lse_min = lse_stats[:, :, 0, 0].reshape(-1)
    pa = mm(oa, w["wa"], "nn", name="branch_a")
    pb = mm(ob, w["wb"], "nn", name="branch_b")
    merged = _merge_fwd(proj, pa, pb, name="merge_fwd")
    h1 = mm(merged, w["wo"], "nn", addend=x, name="mix_out")
    n2 = _rms_fwd(h1, p["norm_ffn"], name="rms2_fwd")
    ug = mm(n2, w["wug"], "nn", name="up_gate")
    uv = mm(n2, w["wuv"], "nn", name="up_val")
    a = _convglu_fwd(ug, uv, w["cwg"], w["cwv"], p["cbg"], p["cbv"], name="convglu_fwd")
    h2 = mm(a, w["wd"], "nn", addend=h1, name="ffn_down")
    loss, dh2, d_norm_final = _loss_head(h2, p["norm_final"], tgt, name="loss_head")
    da = mm(dh2, w["wd"], "nt", out_dtype=BF16, name="d_act")
    d_wd = mm(a, dh2, "tn", out_dtype=BF16, name="dw_down")
    dug, duv, d_cwg, d_cwv, d_cbg, d_cbv = _convglu_bwd(
        ug, uv, w["cwg"], w["cwv"], p["cbg"], p["cbv"], da, name="convglu_bwd")
    dn2 = mm(dug, w["wug"], "nt", name="dn2_gate")
    dn2 = mm(duv, w["wuv"], "nt", addend=dn2, name="dn2_val")
    d_wug = mm(n2, dug, "tn", out_dtype=BF16, name="dw_up_gate")
    d_wuv = mm(n2, duv, "tn", out_dtype=BF16, name="dw_up_val")
    dh1, d_norm_ffn = _rms_bwd(h1, p["norm_ffn"], dn2, dh2, name="rms2_bwd")
    dmerged = mm(dh1, w["wo"], "nt", name="d_merged")
    d_wo = mm(merged, dh1, "tn", out_dtype=BF16, name="dw_out")
    dpa, dpb, dga, dgb = _merge_bwd(proj, pa, pb, dmerged, name="merge_bwd")
    doa = mm(dpa, w["wa"], "nt", name="d_oa")
    dob = mm(dpb, w["wb"], "nt", out_dtype=BF16, name="d_ob")
    d_wa = mm(oa, dpa, "tn", out_dtype=BF16, name="dw_branch_a")
    d_wb = mm(ob, dpb, "tn", out_dtype=BF16, name="dw_branch_b")
    dhq, dhf, dhi, dhg, dlb, dgn8 = _hgrn_bwd(proj, lb, gnorm, o_hg, states, doa, name="hgrn_bwd")
    d_logits = _lb_bwd(p["hg_lb_logits"], dlb, name="lb_bwd")
    dob_hm = _fox_bwd_prep(ob, dob, name="fox_bwd_prep")
    early_parts = None
    if exchange:
        comm = _ExchangeComm(_early_grad_blocks(d_wa, d_wb, d_wo, d_wug, d_wuv, d_wd))
        dq, dcsp, early_parts = _fox_bwd_dq(qb, ka, va, dob_hm, bounds, lse_min, comm=comm, name="fox_bwd_dq")
    else:
        dq, dcsp = _fox_bwd_dq(qb, ka, va, dob_hm, bounds, lse_min, name="fox_bwd_dq")
    dk, dv = _fox_bwd_dkv(qb, ka, va, dob_hm, bounds, lse_min, name="fox_bwd_dkv")
    nb = dcsp.shape[1]
    written = (jnp.arange(S) // (S // nb))[None, None, None, :] <= jnp.arange(nb)[None, :, None, None]
    dcs = jnp.sum(jnp.where(written, dcsp, 0.0), axis=1)
    dcs_tok = jnp.pad(dcs.reshape(FOX_HEADS, S).T, ((0, 0), (0, 128 - FOX_HEADS)))
    dff, dbias = _fox_gate_bwd(ff, bias, dcs_tok, name="fox_gate_bwd")
    dproj = jnp.concatenate([dhq, dhf, dhi, dhg, dq, dk, dv, dga, dgb], axis=1)
    d_wm = mm(n1, dproj, "tn", out_dtype=BF16, name="dw_in_main")
    d_wff = mm(n1, dff, "tn", out_dtype=BF16, name="dw_in_ff")
    dn1 = mm(dff, w["wff"], "nt", name="dn1_ff")
    late_parts = None
    if exchange:
        d_win = jnp.concatenate([d_wm[:, :FF_LO], d_wff[:, :FOX_HEADS], d_wm[:, FF_LO:]], axis=1)
        d_cw = jnp.concatenate([d_cwg, d_cwv], axis=1)
        comm = _ExchangeComm([_col_blocks(d_win, 1154), _col_blocks(d_cw, 704)])
        dn1, late_parts = mm(dproj, w["wm"], "nt", addend=dn1, comm=comm, name="dn1_main")
    else:
        dn1 = mm(dproj, w["wm"], "nt", addend=dn1, name="dn1_main")
    dx, d_norm_mix = _rms_bwd(x, p["norm_mix"], dn1, dh1, name="rms1_bwd")
    grads = dict(
        wm=d_wm, wff=d_wff, wa=d_wa, wb=d_wb, wo=d_wo, wug=d_wug, wuv=d_wuv, cwg=d_cwg, cwv=d_cwv, wd=d_wd,
        norm_mix=d_norm_mix.reshape(-1), fox_f_bias=dbias[0, :FOX_HEADS], hg_lb_logits=d_logits,
        hg_norm=jnp.sum(dgn8, axis=0).reshape(-1), norm_ffn=d_norm_ffn.reshape(-1), cbg=d_cbg, cbv=d_cbv,
        norm_final=d_norm_final.reshape(-1), early_parts=early_parts, late_parts=late_parts)
    return loss, dx, grads


SMALL = [("norm_mix", (1, D_MODEL)), ("fox_f_bias", (1, FOX_HEADS)), ("hg_lb_logits", (2, HG_HEADS * HG_DK)),
         ("hg_norm", (1, HG_DV)), ("norm_ffn", (1, D_MODEL)), ("conv_b", (1, 2 * D_FF)), ("norm_final", (D_MODEL,))]
SMALL_ROWS = 88
SHARDED = [("w_in", (D_MODEL, 1154), 256), ("w_branch_a", (128, D_MODEL), 128), ("w_branch_b", (128, D_MODEL), 128),
           ("w_out", (128, D_MODEL), 128), ("w_up", (D_MODEL, 704), 256), ("conv_w", (3, 704), 3),
           ("w_down", (352, D_MODEL), 352)]
NAMES = ["norm_mix", "w_in", "fox_f_bias", "hg_lb_logits", "hg_norm", "w_branch_a", "w_branch_b", "w_out",
         "norm_ffn", "w_up", "conv_w", "conv_b", "w_down", "norm_final"]


def _size(shape):
    n = 1
    for s in shape:
        n *= s
    return n


def _adamw(parts, w, m, v, *, name, T):
    R, C = w.shape
    c1 = 1.0 / (1.0 - ADAM_B1 ** ADAM_STEP)
    c2 = 1.0 / (1.0 - ADAM_B2 ** ADAM_STEP)

    def body(p_ref, w_ref, m_ref, v_ref, g_ref, d_ref, nm_ref, nv_ref):
        g = p_ref[0].astype(F32)
        for s in range(1, N_DEV):
            g = g + p_ref[s].astype(F32)
        g_ref[...] = g
        nm = ADAM_B1 * m_ref[...] + (1.0 - ADAM_B1) * g
        nv = ADAM_B2 * v_ref[...] + (1.0 - ADAM_B2) * (g * g)
        nm_ref[...] = nm
        nv_ref[...] = nv
        d_ref[...] = -ADAM_LR * ((nm * c1) / (jnp.sqrt(nv * c2) + ADAM_EPS) + ADAM_WD * w_ref[...])

    blk = pl.BlockSpec((T, C), lambda i: (i, 0))
    out = jax.ShapeDtypeStruct((R, C), F32)
    return pl.pallas_call(
        body, name=name, grid=(R // T,),
        in_specs=[pl.BlockSpec((N_DEV, T, C), lambda i: (0, i, 0)), blk, blk, blk],
        out_specs=[blk, blk, blk, blk], out_shape=[out, out, out, out],
        compiler_params=_cparams(("parallel",)),
    )(parts, w, m, v)


def _pack_small(vals):
    flat = jnp.concatenate([vals[n].reshape(-1).astype(F32) for n, _ in SMALL])
    return jnp.pad(flat, (0, SMALL_ROWS * 128 - flat.shape[0])).reshape(SMALL_ROWS, 128)


def _unpack_small(buf):
    flat, out, off = buf.reshape(-1), {}, 0
    for n, shape in SMALL:
        out[n] = flat[off:off + _size(shape)].reshape(shape)
        off += _size(shape)
    return out


def kernel(x, norm_mix, w_in, fox_f_bias, hg_lb_logits, hg_norm, w_branch_a, w_branch_b, w_out, norm_ffn, w_up, conv_w, conv_b, w_down, norm_final, loss_target, m_norm_mix, m_w_in, m_fox_f_bias, m_hg_lb_logits, m_hg_norm, m_w_branch_a, m_w_branch_b, m_w_out, m_norm_ffn, m_w_up, m_conv_w, m_conv_b, m_w_down, m_norm_final, v_norm_mix, v_w_in, v_fox_f_bias, v_hg_lb_logits, v_hg_norm, v_w_branch_a, v_w_branch_b, v_w_out, v_norm_ffn, v_w_up, v_conv_w, v_conv_b, v_w_down, v_norm_final):
    wv = dict(norm_mix=norm_mix, w_in=w_in, fox_f_bias=fox_f_bias, hg_lb_logits=hg_lb_logits, hg_norm=hg_norm,
              w_branch_a=w_branch_a, w_branch_b=w_branch_b, w_out=w_out, norm_ffn=norm_ffn, w_up=w_up, conv_w=conv_w,
              conv_b=conv_b, w_down=w_down, norm_final=norm_final)
    mv = dict(norm_mix=m_norm_mix, w_in=m_w_in, fox_f_bias=m_fox_f_bias, hg_lb_logits=m_hg_lb_logits, hg_norm=m_hg_norm,
              w_branch_a=m_w_branch_a, w_branch_b=m_w_branch_b, w_out=m_w_out, norm_ffn=m_norm_ffn, w_up=m_w_up,
              conv_w=m_conv_w, conv_b=m_conv_b, w_down=m_w_down, norm_final=m_norm_final)
    vv = dict(norm_mix=v_norm_mix, w_in=v_w_in, fox_f_bias=v_fox_f_bias, hg_lb_logits=v_hg_lb_logits, hg_norm=v_hg_norm,
              w_branch_a=v_w_branch_a, w_branch_b=v_w_branch_b, w_out=v_w_out, norm_ffn=v_norm_ffn, w_up=v_w_up,
              conv_w=v_conv_w, conv_b=v_conv_b, w_down=v_w_down, norm_final=v_norm_final)

    (g_in,) = _comm_call(_GatherComm([w_in[0].astype(BF16)]), name="gather_w_in")
    win = jnp.concatenate([g_in[d] for d in range(N_DEV)], axis=1)
    w = dict(wm=jnp.concatenate([win[:, :FF_LO], win[:, FF_HI:]], axis=1),
             wff=jnp.pad(win[:, FF_LO:FF_HI], ((0, 0), (0, 128 - FOX_HEADS))))
    late = _GatherComm([w_branch_a[0].astype(BF16), w_branch_b[0].astype(BF16), w_out[0].astype(BF16),
                        w_up[0].astype(BF16), conv_w[0], w_down[0].astype(BF16)])
    p = dict(norm_mix=norm_mix[0], fox_f_bias=fox_f_bias[0], hg_lb_logits=hg_lb_logits, hg_norm=hg_norm[0],
             norm_ffn=norm_ffn[0], cbg=conv_b[:, :D_FF], cbv=conv_b[:, D_FF:], norm_final=norm_final)
    loss, dx, grads = _local_step(x[0], loss_target[0], w, p, late=late, exchange=True)
    loss = lax.psum(loss[0, 0], ("x", "y", "c"))

    small = _pack_small(dict(
        norm_mix=grads["norm_mix"], fox_f_bias=grads["fox_f_bias"], hg_lb_logits=grads["hg_lb_logits"],
        hg_norm=grads["hg_norm"], norm_ffn=grads["norm_ffn"], conv_b=jnp.concatenate([grads["cbg"], grads["cbv"]], axis=1),
        norm_final=grads["norm_final"]))
    (small_parts,) = _comm_call(_ExchangeComm([jnp.broadcast_to(small[None], (N_DEV, SMALL_ROWS, 128))]),
                                name="exchange_small")
    ea, eb, eo, eup, ed = grads["early_parts"]
    p_in, p_cw = grads["late_parts"]
    parts = [p_in, ea, eb, eo, eup, p_cw, ed, small_parts]
    res = {}
    for (n, shape, tile), part in zip(SHARDED, parts):
        outs = _adamw(part, wv[n].reshape(shape), mv[n].reshape(shape), vv[n].reshape(shape), name="adamw_" + n, T=tile)
        res[n] = [o.reshape(wv[n].shape) for o in outs]
    outs = _adamw(parts[-1], _pack_small(wv), _pack_small(mv), _pack_small(vv), name="adamw_small", T=SMALL_ROWS)
    small = [_unpack_small(o) for o in outs]
    for n, _ in SMALL:
        res[n] = [s[n] for s in small]
    return (loss, dx[None], *[res[n][0] for n in NAMES], *[res[n][1] for n in NAMES],
            *[res[n][2] for n in NAMES], *[res[n][3] for n in NAMES])
```

```python
import jax
import jax.numpy as jnp
from jax import lax
from jax.experimental import pallas as pl
from jax.experimental.pallas import tpu as pltpu

F32 = jnp.float32
BF16 = jnp.bfloat16

D_MODEL = 1024
HG_HEADS = 8
HG_DK = 128
HG_DV = 128
HG_CHUNK = 64
FOX_HEADS = 16
FOX_DH = 64
D_FF = 2816
EPS = 1e-6
N_DEV = 8

ADAM_LR = 0.001
ADAM_B1 = 0.9
ADAM_B2 = 0.999
ADAM_EPS = 1e-08
ADAM_WD = 0.01
ADAM_STEP = 10

VMEM_LIMIT = 56 * 1024 * 1024


def _cparams(sem):
    return pltpu.CompilerParams(dimension_semantics=sem, vmem_limit_bytes=VMEM_LIMIT)


MESH = pl.DeviceIdType.MESH
ANY = pl.BlockSpec(memory_space=pl.ANY)
SMEM = pl.BlockSpec(memory_space=pltpu.SMEM)


class _GatherComm:
    def __init__(self, shards):
        self.inputs = list(shards)
        n = self.n = len(shards)
        self.out_shapes = [jax.ShapeDtypeStruct((N_DEV,) + s.shape, s.dtype) for s in shards]
        self.scratch = [pltpu.SemaphoreType.DMA((n, 7)), pltpu.SemaphoreType.DMA((n, 7)), pltpu.SemaphoreType.DMA((n,))]

    def _parts(self, x_refs, out_refs, sems):
        send_sems, recv_sems, local_sems = sems
        x, y, c = lax.axis_index("x"), lax.axis_index("y"), lax.axis_index("c")
        me, sibling = (x, y, c), (x, y, 1 - c)
        chips = [(1 - x, y), (x, 1 - y), (1 - x, 1 - y)]

        def copy(t, k, block, to, src=None):
            slot = out_refs[t].at[4 * block[0] + 2 * block[1] + block[2]]
            return pltpu.make_async_remote_copy(
                src_ref=slot if src is None else src, dst_ref=slot,
                send_sem=send_sems.at[t, k], recv_sem=recv_sems.at[t, k], device_id=to, device_id_type=MESH)

        mine = [pltpu.make_async_copy(x_refs[t], out_refs[t].at[4 * x + 2 * y + c], local_sems.at[t])
                for t in range(self.n)]
        first = []
        for t in range(self.n):
            first.append(copy(t, 0, me, sibling, src=x_refs[t]))
            first += [copy(t, 1 + j, me, (*chip, c), src=x_refs[t]) for j, chip in enumerate(chips)]
        return c, me, sibling, chips, copy, mine, first

    def start(self, x_refs, out_refs, sems):
        _, _, _, _, _, mine, first = self._parts(x_refs, out_refs, sems)
        for cp in mine + first:
            cp.start()

    def finish(self, x_refs, out_refs, sems):
        c, me, sibling, chips, copy, mine, first = self._parts(x_refs, out_refs, sems)
        passed = []
        for j, chip in enumerate(chips):
            for t in range(self.n):
                copy(t, 1 + j, (*chip, c), me).wait_recv()
                passed.append(copy(t, 4 + j, (*chip, c), sibling))
                passed[-1].start()
        for t in range(self.n):
            copy(t, 0, sibling, me).wait_recv()
            for j, chip in enumerate(chips):
                copy(t, 4 + j, (*chip, 1 - c), me).wait_recv()
        for cp in first + passed:
            cp.wait_send()
        for cp in mine:
            cp.wait()


class _ExchangeComm:
    def __init__(self, blocks):
        self.inputs = list(blocks)
        n = self.n = len(blocks)
        self.out_shapes = [jax.ShapeDtypeStruct(b.shape, b.dtype) for b in blocks]
        self.scratch = [pltpu.SemaphoreType.DMA((n, 7)), pltpu.SemaphoreType.DMA((n, 7)), pltpu.SemaphoreType.DMA((n,))]

    def _parts(self, g_refs, out_refs, sems):
        send_sems, recv_sems, local_sems = sems
        x, y, c = lax.axis_index("x"), lax.axis_index("y"), lax.axis_index("c")
        me = 4 * x + 2 * y + c
        mine = [pltpu.make_async_copy(g_refs[t].at[me], out_refs[t].at[me], local_sems.at[t]) for t in range(self.n)]
        sends, recvs = [], []
        for k in range(1, N_DEV):
            px = 1 - x if k & 4 else x
            py = 1 - y if k & 2 else y
            pc = 1 - c if k & 1 else c
            p = 4 * px + 2 * py + pc
            for t in range(self.n):
                sends.append(pltpu.make_async_remote_copy(
                    src_ref=g_refs[t].at[p], dst_ref=out_refs[t].at[me], send_sem=send_sems.at[t, k - 1],
                    recv_sem=recv_sems.at[t, k - 1], device_id=(px, py, pc), device_id_type=MESH))
                recvs.append(pltpu.make_async_remote_copy(
                    src_ref=g_refs[t].at[p], dst_ref=out_refs[t].at[p], send_sem=send_sems.at[t, k - 1],
                    recv_sem=recv_sems.at[t, k - 1], device_id=(px, py, pc), device_id_type=MESH))
        return mine, sends, recvs

    def start(self, g_refs, out_refs, sems):
        mine, sends, _ = self._parts(g_refs, out_refs, sems)
        for cp in mine + sends:
            cp.start()

    def finish(self, g_refs, out_refs, sems):
        mine, sends, recvs = self._parts(g_refs, out_refs, sems)
        for cp in recvs:
            cp.wait_recv()
        for cp in sends:
            cp.wait_send()
        for cp in mine:
            cp.wait()


def _comm_call(comm, *, name):
    n = comm.n

    def body(*refs):
        comm.start(refs[:n], refs[n:2 * n], refs[2 * n:])
        comm.finish(refs[:n], refs[n:2 * n], refs[2 * n:])

    return pl.pallas_call(body, name=name, in_specs=[ANY] * n, out_specs=[ANY] * n, out_shape=comm.out_shapes,
                          scratch_shapes=comm.scratch)(*comm.inputs)


_DIMS = {
    "nn": (((1,), (0,)), ((), ())),
    "nt": (((1,), (1,)), ((), ())),
    "tn": (((0,), (0,)), ((), ())),
}

MATMUL_VMEM_BUDGET = 36 * 1024 * 1024
MAX_TILE = 1536


def _pick(n, prefs):
    for p in prefs:
        if n % p == 0:
            return p
    return n


def _tile_options(n):
    return [d for d in range(128, min(n, MAX_TILE) + 1, 128) if n % d == 0] or [n]


def _pick_tiles(M, N, tk, nk, sa, sb, so, has_addend, tm, tn):
    best = None
    for cm in ([tm] if tm else _tile_options(M)):
        for cn in ([tn] if tn else _tile_options(N)):
            need = 2 * (cm * tk * sa + tk * cn * sb + cm * cn * so + (cm * cn * 4 if has_addend else 0))
            need += cm * cn * 4 if nk > 1 else 0
            if need <= MATMUL_VMEM_BUDGET and (best is None or cm * cn > best[0] * best[1]
                                               or (cm * cn == best[0] * best[1] and cn > best[1])):
                best = (cm, cn)
    assert best is not None, (M, N, tk)
    return best


def _matmul(a, b, form, *, out_dtype=F32, addend=None, tm=None, tn=None, tk=None, comm=None, name):
    if form == "nn":
        (M, K), (K2, N) = a.shape, b.shape
    elif form == "nt":
        (M, K), (N, K2) = a.shape, b.shape
    else:
        (K, M), (K2, N) = a.shape, b.shape
    assert K == K2, (a.shape, b.shape, form)
    tk = tk or (K if K <= 2816 else _pick(K, (1024, 512, 256, 128)))
    nk = K // tk
    if tm is None or tn is None:
        tm, tn = _pick_tiles(M, N, tk, nk, a.dtype.itemsize, b.dtype.itemsize, jnp.dtype(out_dtype).itemsize,
                             addend is not None, tm, tn)
    assert M % tm == 0 and N % tn == 0 and K % tk == 0, (M, N, K, tm, tn, tk)
    dims = _DIMS[form]
    nc = comm.n if comm is not None else 0
    grid = (M // tm, N // tn, nk)

    def body(*refs):
        a_ref, b_ref = refs[:2]
        pos = 2
        add_ref = refs[pos] if addend is not None else None
        pos += addend is not None
        c_in, o_ref, c_out = refs[pos:pos + nc], refs[pos + nc], refs[pos + nc + 1:pos + 2 * nc + 1]
        pos += 2 * nc + 1
        acc_ref = refs[pos] if nk > 1 else None
        c_sems = refs[pos + (nk > 1):]
        if comm is not None:
            ids = [pl.program_id(d) for d in range(3)]

            @pl.when((ids[0] == 0) & (ids[1] == 0) & (ids[2] == 0))
            def _():
                comm.start(c_in, c_out, c_sems)

        def finish(r):
            if add_ref is not None:
                r = r + add_ref[...].astype(F32)
            o_ref[...] = r.astype(o_ref.dtype)

        part = lax.dot_general(a_ref[...].astype(BF16), b_ref[...].astype(BF16), dims, preferred_element_type=F32)
        if nk == 1:
            finish(part)
        else:
            k = pl.program_id(2)

            @pl.when(k == 0)
            def _():
                acc_ref[...] = part

            @pl.when(k > 0)
            def _():
                acc_ref[...] += part

            @pl.when(k == nk - 1)
            def _():
                finish(acc_ref[...])

        if comm is not None:
            @pl.when((ids[0] == grid[0] - 1) & (ids[1] == grid[1] - 1) & (ids[2] == grid[2] - 1))
            def _():
                comm.finish(c_in, c_out, c_sems)

    if form == "nn":
        a_spec = pl.BlockSpec((tm, tk), lambda i, j, k: (i, k))
        b_spec = pl.BlockSpec((tk, tn), lambda i, j, k: (k, j))
    elif form == "nt":
        a_spec = pl.BlockSpec((tm, tk), lambda i, j, k: (i, k))
        b_spec = pl.BlockSpec((tn, tk), lambda i, j, k: (j, k))
    else:
        a_spec = pl.BlockSpec((tk, tm), lambda i, j, k: (k, i))
        b_spec = pl.BlockSpec((tk, tn), lambda i, j, k: (k, j))
    o_spec = pl.BlockSpec((tm, tn), lambda i, j, k: (i, j))
    in_specs = [a_spec, b_spec] + ([o_spec] if addend is not None else [])
    args = (a, b) + ((addend,) if addend is not None else ())
    out_shape = jax.ShapeDtypeStruct((M, N), out_dtype)
    scratch = [pltpu.VMEM((tm, tn), F32)] if nk > 1 else []
    if comm is None:
        return pl.pallas_call(
            body, name=name, grid=grid, in_specs=in_specs, out_specs=o_spec, out_shape=out_shape,
            scratch_shapes=scratch, compiler_params=_cparams(("parallel", "parallel", "arbitrary")),
        )(*args)
    outs = pl.pallas_call(
        body, name=name, grid=grid, in_specs=in_specs + [ANY] * nc, out_specs=[o_spec] + [ANY] * nc,
        out_shape=[out_shape] + comm.out_shapes, scratch_shapes=scratch + comm.scratch,
        compiler_params=_cparams(("arbitrary", "arbitrary", "arbitrary")),
    )(*args, *comm.inputs)
    return outs[0], outs[1:]


def _rms_fwd(x, g, *, name, tm=512):
    M, D = x.shape
    tm = min(tm, M)

    def body(x_ref, g_ref, n_ref):
        xf = x_ref[...]
        r = lax.rsqrt(jnp.mean(xf * xf, axis=-1, keepdims=True) + EPS)
        n_ref[...] = (xf * r * g_ref[...]).astype(n_ref.dtype)

    return pl.pallas_call(
        body, name=name, grid=(M // tm,),
        in_specs=[pl.BlockSpec((tm, D), lambda i: (i, 0)), pl.BlockSpec((1, D), lambda i: (0, 0))],
        out_specs=pl.BlockSpec((tm, D), lambda i: (i, 0)),
        out_shape=jax.ShapeDtypeStruct((M, D), BF16),
        compiler_params=_cparams(("parallel",)),
    )(x, g.reshape(1, D))


def _rms_bwd(x, g, dn, dres, *, name, tm=512):
    M, D = x.shape
    tm = min(tm, M)

    def body(x_ref, g_ref, dn_ref, dres_ref, dx_ref, dg_ref):
        @pl.when(pl.program_id(0) == 0)
        def _():
            dg_ref[...] = jnp.zeros_like(dg_ref)

        xf = x_ref[...]
        r = lax.rsqrt(jnp.mean(xf * xf, axis=-1, keepdims=True) + EPS)
        xh = xf * r
        dn_ = dn_ref[...].astype(F32)
        dg_ref[...] += jnp.sum(dn_ * xh, axis=0, keepdims=True)
        dxh = dn_ * g_ref[...]
        dx = r * (dxh - xh * jnp.mean(dxh * xh, axis=-1, keepdims=True))
        dx_ref[...] = dres_ref[...] + dx

    row = pl.BlockSpec((tm, D), lambda i: (i, 0))
    vec = pl.BlockSpec((1, D), lambda i: (0, 0))
    return pl.pallas_call(
        body, name=name, grid=(M // tm,),
        in_specs=[row, vec, row, row], out_specs=[row, vec],
        out_shape=[jax.ShapeDtypeStruct((M, D), F32), jax.ShapeDtypeStruct((1, D), F32)],
        compiler_params=_cparams(("arbitrary",)),
    )(x, g.reshape(1, D), dn, dres)


def _loss_head(h, g, tgt, *, name, tm=512):
    M, D = h.shape
    tm = min(tm, M)

    def body(h_ref, g_ref, t_ref, loss_ref, dh_ref, dg_ref):
        @pl.when(pl.program_id(0) == 0)
        def _():
            dg_ref[...] = jnp.zeros_like(dg_ref)
            loss_ref[...] = jnp.zeros_like(loss_ref)

        xf = h_ref[...]
        r = lax.rsqrt(jnp.mean(xf * xf, axis=-1, keepdims=True) + EPS)
        xh = xf * r
        err = xh * g_ref[...] - t_ref[...]
        part = jnp.sum(jnp.mean(err * err, axis=-1, keepdims=True), axis=0, keepdims=True)
        loss_ref[...] += 0.5 * part
        dy = err * (1.0 / D)
        dg_ref[...] += jnp.sum(dy * xh, axis=0, keepdims=True)
        dxh = dy * g_ref[...]
        dh_ref[...] = r * (dxh - xh * jnp.mean(dxh * xh, axis=-1, keepdims=True))

    row = pl.BlockSpec((tm, D), lambda i: (i, 0))
    vec = pl.BlockSpec((1, D), lambda i: (0, 0))
    one = pl.BlockSpec((1, 1), lambda i: (0, 0))
    return pl.pallas_call(
        body, name=name, grid=(M // tm,),
        in_specs=[row, vec, row], out_specs=[one, row, vec],
        out_shape=[jax.ShapeDtypeStruct((1, 1), F32), jax.ShapeDtypeStruct((M, D), F32),
                   jax.ShapeDtypeStruct((1, D), F32)],
        compiler_params=_cparams(("arbitrary",)),
    )(h, g.reshape(1, D), tgt)


HG_MID = HG_CHUNK // 2 - 1
EXP_CAP = 80.0


def _sigmoid(x):
    return 1.0 / (1.0 + jnp.exp(-x))


def _dot(a, b, dims, precision=None):
    return lax.dot_general(a, b, dims, preferred_element_type=F32, precision=precision)


def _bdot(a, b, form):
    return _dot(a.astype(BF16), b.astype(BF16), _DIMS[form])


def _split2(x):
    hi = x.astype(BF16)
    return hi, (x - hi.astype(F32)).astype(BF16)


def _dot3(a, b, form):
    d = _DIMS[form]
    return _dot(a[0], b[0], d) + (_dot(a[0], b[1], d) + _dot(a[1], b[0], d))


def _hgrn_chunk_common(hq, hf, lbv, tril, rid):
    sq = _sigmoid(hq)
    q = hq * sq
    sg = _sigmoid(hf)
    f = lbv + (1.0 - lbv) * sg
    k = (1.0 - lbv) * (1.0 - sg)
    g = jnp.log(f)
    b = _dot(tril, g, _DIMS["nn"], precision=lax.Precision.HIGHEST)
    bref = jnp.sum(jnp.where(rid == HG_MID, b, 0.0), axis=0, keepdims=True)
    bend = jnp.sum(jnp.where(rid == HG_CHUNK - 1, b, 0.0), axis=0, keepdims=True)
    eb = jnp.exp(b)
    e1 = jnp.exp(jnp.minimum(b - bref, EXP_CAP))
    e2 = jnp.exp(jnp.minimum(bref - b, EXP_CAP))
    e3 = jnp.exp(bend - b)
    return sq, q, sg, f, k, bend, eb, e1, e2, e3


def _hgrn_fwd(proj, lb, gnorm, *, name, T=512):
    S = proj.shape[0]
    T = min(T, S)
    nch = T // HG_CHUNK
    C = HG_CHUNK

    def body(hq_ref, hf_ref, hi_ref, hg_ref, lb_ref, gn_ref, o_ref, oa_ref, st_ref, state):
        @pl.when(pl.program_id(1) == 0)
        def _():
            state[...] = jnp.zeros_like(state)

        lbv = lb_ref[...]
        gn = gn_ref[...]
        row = lax.broadcasted_iota(jnp.int32, (C, C), 0)
        col = lax.broadcasted_iota(jnp.int32, (C, C), 1)
        causal = row >= col
        tril = causal.astype(F32)
        rid = lax.broadcasted_iota(jnp.int32, (C, HG_DK), 0)
        sls = [pl.ds(c * C, C) for c in range(nch)]
        pre = [_hgrn_chunk_common(hq_ref[sl, :], hf_ref[sl, :], lbv, tril, rid) for sl in sls]
        v_l = [hi_ref[sl, :].astype(BF16) for sl in sls]
        a_l, u_l = [], []
        for c in range(nch):
            _, q, _, _, k, _, _, e1, e2, e3 = pre[c]
            a_l.append(jnp.where(causal, _bdot(q * e1, k * e2, "nt"), 0.0))
            u_l.append(_bdot(v_l[c], k * e3, "tn"))
        o_l = [_bdot(a_l[c], v_l[c], "nn") for c in range(nch)]
        st = state[...]
        st_l = []
        for c in range(nch):
            st_l.append(st)
            st = st * jnp.exp(pre[c][5]) + u_l[c]
        state[...] = st
        for c in range(nch):
            st_ref[0, c] = st_l[c]
            o_l[c] = o_l[c] + _bdot(pre[c][1] * pre[c][6], st_l[c], "nt")
        for c in range(nch):
            o, hg = o_l[c], hg_ref[sls[c], :]
            o_ref[sls[c], :] = o
            r = lax.rsqrt(jnp.mean(o * o, axis=-1, keepdims=True) + EPS)
            oa_ref[sls[c], :] = (o * r * gn * (hg * _sigmoid(hg))).astype(oa_ref.dtype)

    def grp(gidx):
        return pl.BlockSpec((T, 128), lambda h, t: (t, gidx * 8 + h))

    return pl.pallas_call(
        body, name=name, grid=(HG_HEADS, S // T),
        in_specs=[grp(0), grp(1), grp(2), grp(3),
                  pl.BlockSpec((1, 128), lambda h, t: (0, h)), pl.BlockSpec((1, 128), lambda h, t: (0, 0))],
        out_specs=[pl.BlockSpec((T, 128), lambda h, t: (t, h)), pl.BlockSpec((T, 128), lambda h, t: (t, h)),
                   pl.BlockSpec((1, nch, HG_DV, HG_DK), lambda h, t: (h, t, 0, 0))],
        out_shape=[jax.ShapeDtypeStruct((S, HG_HEADS * HG_DV), F32), jax.ShapeDtypeStruct((S, HG_HEADS * HG_DV), BF16),
                   jax.ShapeDtypeStruct((HG_HEADS, S // C, HG_DV, HG_DK), F32)],
        scratch_shapes=[pltpu.VMEM((HG_DV, HG_DK), F32)],
        compiler_params=_cparams(("parallel", "arbitrary")),
    )(proj, proj, proj, proj, lb, gnorm)


def _hgrn_bwd(proj, lb, gnorm, o, states, doa, *, name, T=512):
    S = proj.shape[0]
    T = min(T, S)
    nch = T // HG_CHUNK
    C = HG_CHUNK
    nT = S // T

    def body(hq_ref, hf_ref, hi_ref, hg_ref, lb_ref, gn_ref, o_ref, st_ref, doa_ref,
             dhq_ref, dhf_ref, dhi_ref, dhg_ref, dlb_ref, dgn_ref, dstate):
        @pl.when(pl.program_id(1) == 0)
        def _():
            dstate[...] = jnp.zeros_like(dstate)
            dlb_ref[...] = jnp.zeros_like(dlb_ref)
            dgn_ref[...] = jnp.zeros_like(dgn_ref)

        lbv = lb_ref[...]
        gn = gn_ref[...]
        row = lax.broadcasted_iota(jnp.int32, (C, C), 0)
        col = lax.broadcasted_iota(jnp.int32, (C, C), 1)
        causal = row >= col
        tril = causal.astype(F32)
        triu = (row <= col).astype(F32)
        rid = lax.broadcasted_iota(jnp.int32, (C, HG_DK), 0)
        rng = range(nch)
        sls = [pl.ds(c * C, C) for c in rng]
        pre = [_hgrn_chunk_common(hq_ref[sl, :], hf_ref[sl, :], lbv, tril, rid) for sl in sls]
        do2, dgn_acc = [], jnp.zeros((1, HG_DV), F32)
        for c in rng:
            hg, ov = hg_ref[sls[c], :], o_ref[sls[c], :]
            r = lax.rsqrt(jnp.mean(ov * ov, axis=-1, keepdims=True) + EPS)
            xh = ov * r
            sgg = _sigmoid(hg)
            d_oa = doa_ref[sls[c], :].astype(F32)
            dz = d_oa * (hg * sgg)
            dhg_ref[sls[c], :] = (d_oa * (xh * gn) * (sgg * (1.0 + hg * (1.0 - sgg)))).astype(dhg_ref.dtype)
            dgn_acc = dgn_acc + jnp.sum(dz * xh, axis=0, keepdims=True)
            dxh = dz * gn
            do2.append(_split2(r * (dxh - xh * jnp.mean(dxh * xh, axis=-1, keepdims=True))))
        dgn_ref[0] += dgn_acc
        qi = [pre[c][1] * pre[c][6] for c in rng]
        qp = [pre[c][1] * pre[c][7] for c in rng]
        kp = [pre[c][4] * pre[c][8] for c in rng]
        kend = [pre[c][4] * pre[c][9] for c in rng]
        qi2, qp2, kp2, kend2 = ([_split2(t) for t in lst] for lst in (qi, qp, kp, kend))
        v2 = [_split2(hi_ref[sl, :]) for sl in sls]
        st0 = [st_ref[0, c] for c in rng]
        a_l = [jnp.where(causal, _dot(qp2[c][0], kp2[c][0], _DIMS["nt"]), 0.0).astype(BF16) for c in rng]
        da2 = [_split2(jnp.where(causal, _dot3(do2[c], v2[c], "nt"), 0.0)) for c in rng]
        dqi = [_dot3(do2[c], _split2(st0[c]), "nn") for c in rng]
        w_l = [_dot3(do2[c], qi2[c], "tn") for c in rng]
        ds = dstate[...]
        ds1 = [None] * nch
        for c in reversed(rng):
            ds1[c] = ds
            ds = ds * jnp.exp(pre[c][5]) + w_l[c]
        dstate[...] = ds
        ds12 = [_split2(t) for t in ds1]
        dqp = [_dot3(da2[c], kp2[c], "nn") for c in rng]
        dkp = [_dot3(da2[c], qp2[c], "tn") for c in rng]
        dv = [_dot(a_l[c], do2[c][0], _DIMS["tn"]) + _dot(kend2[c][0], ds12[c][0], _DIMS["nt"]) for c in rng]
        dkend = [_dot3(v2[c], ds12[c], "nn") for c in rng]
        dq_l, dk_l, db_l = [], [], []
        for c in rng:
            _, _, _, _, _, bend, eb, e1, e2, e3 = pre[c]
            dq_l.append(dqi[c] * eb + dqp[c] * e1)
            dk_l.append(dkp[c] * e2 + dkend[c] * e3)
            db = dqi[c] * qi[c] + dqp[c] * qp[c] - dkp[c] * kp[c] - dkend[c] * kend[c]
            dbend = (jnp.sum(dkend[c] * kend[c], axis=0, keepdims=True)
                     + jnp.exp(bend) * jnp.sum(ds1[c] * st0[c], axis=0, keepdims=True))
            db_l.append(db + jnp.where(rid == C - 1, dbend, 0.0))
        dg = [_dot(triu, db_l[c], _DIMS["nn"], precision=lax.Precision.HIGHEST) for c in rng]
        dlb_acc = jnp.zeros((1, HG_DK), F32)
        for c in rng:
            sq, _, sg, f, _, _, _, _, _, _ = pre[c]
            hq = hq_ref[sls[c], :]
            df = dg[c] / f - dk_l[c]
            dlb_acc = dlb_acc + jnp.sum(df * (1.0 - sg), axis=0, keepdims=True)
            dhf_ref[sls[c], :] = (df * (1.0 - lbv) * sg * (1.0 - sg)).astype(dhf_ref.dtype)
            dhq_ref[sls[c], :] = (dq_l[c] * (sq * (1.0 + hq * (1.0 - sq)))).astype(dhq_ref.dtype)
            dhi_ref[sls[c], :] = dv[c].astype(dhi_ref.dtype)
        dlb_ref[...] += dlb_acc

    def grp(gidx):
        return pl.BlockSpec((T, 128), lambda h, t: (nT - 1 - t, gidx * 8 + h))

    tok = pl.BlockSpec((T, 128), lambda h, t: (nT - 1 - t, h))
    big = jax.ShapeDtypeStruct((S, HG_HEADS * HG_DV), BF16)
    return pl.pallas_call(
        body, name=name, grid=(HG_HEADS, nT),
        in_specs=[grp(0), grp(1), grp(2), grp(3),
                  pl.BlockSpec((1, 128), lambda h, t: (0, h)), pl.BlockSpec((1, 128), lambda h, t: (0, 0)),
                  tok, pl.BlockSpec((1, nch, HG_DV, HG_DK), lambda h, t: (h, nT - 1 - t, 0, 0)), tok],
        out_specs=[tok, tok, tok, tok, pl.BlockSpec((1, 128), lambda h, t: (0, h)),
                   pl.BlockSpec((1, 1, 128), lambda h, t: (h, 0, 0))],
        out_shape=[big, big, big, big, jax.ShapeDtypeStruct((1, HG_HEADS * HG_DK), F32),
                   jax.ShapeDtypeStruct((HG_HEADS, 1, HG_DV), F32)],
        scratch_shapes=[pltpu.VMEM((HG_DV, HG_DK), F32)],
        compiler_params=_cparams(("parallel", "arbitrary")),
    )(proj, proj, proj, proj, lb, gnorm, o, states, doa)


def _lb_fwd(logits, *, name):
    def body(l_ref, lb_ref):
        lb_ref[...] = _sigmoid(l_ref[0:1, :] - l_ref[1:2, :])

    return pl.pallas_call(body, name=name, out_shape=jax.ShapeDtypeStruct((1, logits.shape[1]), F32))(logits)


def _lb_bwd(logits, dlb, *, name):
    def body(l_ref, d_ref, o_ref):
        lbv = _sigmoid(l_ref[0:1, :] - l_ref[1:2, :])
        t = d_ref[...] * lbv * (1.0 - lbv)
        o_ref[0:1, :] = t
        o_ref[1:2, :] = -t

    return pl.pallas_call(body, name=name, out_shape=jax.ShapeDtypeStruct(logits.shape, F32))(logits, dlb)


NEG = -1e30
FOX_SCALE = FOX_DH ** -0.5
FOX_PAIRS = FOX_HEADS // 2


def _fox_gate_fwd(ff, bias, *, name, T=512):
    S = ff.shape[0]
    T = min(T, S)

    def body(ff_ref, b_ref, c_ref, carry):
        @pl.when(pl.program_id(0) == 0)
        def _():
            carry[...] = jnp.zeros_like(carry)

        z = ff_ref[...] + b_ref[...]
        logf = jnp.minimum(z, 0.0) - jnp.log(1.0 + jnp.exp(-jnp.abs(z)))
        row = lax.broadcasted_iota(jnp.int32, (T, T), 0)
        col = lax.broadcasted_iota(jnp.int32, (T, T), 1)
        c = _dot((row >= col).astype(F32), logf, _DIMS["nn"], precision=lax.Precision.HIGHEST) + carry[...]
        c_ref[...] = c
        carry[...] = c[T - 1:T, :]

    return pl.pallas_call(
        body, name=name, grid=(S // T,),
        in_specs=[pl.BlockSpec((T, 128), lambda i: (i, 0)), pl.BlockSpec((1, 128), lambda i: (0, 0))],
        out_specs=pl.BlockSpec((T, 128), lambda i: (i, 0)),
        out_shape=jax.ShapeDtypeStruct((S, 128), F32),
        scratch_shapes=[pltpu.VMEM((1, 128), F32)],
        compiler_params=_cparams(("arbitrary",)),
    )(ff, bias)


def _fox_gate_bwd(ff, bias, dcs, *, name, T=512):
    S = ff.shape[0]
    T = min(T, S)
    nT = S // T

    def body(ff_ref, b_ref, d_ref, dff_ref, db_ref, carry):
        @pl.when(pl.program_id(0) == 0)
        def _():
            carry[...] = jnp.zeros_like(carry)
            db_ref[...] = jnp.zeros_like(db_ref)

        row = lax.broadcasted_iota(jnp.int32, (T, T), 0)
        col = lax.broadcasted_iota(jnp.int32, (T, T), 1)
        dlogf = carry[...] - _dot((row <= col).astype(F32), d_ref[...], _DIMS["nn"], precision=lax.Precision.HIGHEST)
        carry[...] = dlogf[0:1, :]
        dff = dlogf * (1.0 - _sigmoid(ff_ref[...] + b_ref[...]))
        dff_ref[...] = dff.astype(dff_ref.dtype)
        db_ref[...] += jnp.sum(dff, axis=0, keepdims=True)

    rev = pl.BlockSpec((T, 128), lambda i: (nT - 1 - i, 0))
    vec = pl.BlockSpec((1, 128), lambda i: (0, 0))
    return pl.pallas_call(
        body, name=name, grid=(nT,),
        in_specs=[rev, vec, rev], out_specs=[rev, vec],
        out_shape=[jax.ShapeDtypeStruct((S, 128), BF16), jax.ShapeDtypeStruct((1, 128), F32)],
        scratch_shapes=[pltpu.VMEM((1, 128), F32)],
        compiler_params=_cparams(("arbitrary",)),
    )(ff, bias, dcs)


AUG = FOX_DH


def _split3(x):
    a = x.astype(BF16).astype(F32)
    r = x - a
    b = r.astype(BF16).astype(F32)
    return a, b, r - b


def _lane_fill(lane, base, pieces, start):
    for i, pc in enumerate(pieces):
        base = jnp.where(lane == start + i, pc, base)
    return base


FOX_TB = 512
FOX_SKIP = 60.0
N_STAT = 4


def _fox_prep(proj, c_tok, *, name):
    S = proj.shape[0]
    T = min(FOX_TB, S)

    def body(q_ref, k_ref, v_ref, c_ref, qa_ref, ka_ref, va_ref, st_ref):
        pair = pl.program_id(0)
        lane = lax.broadcasted_iota(jnp.int32, (T, 128), 1)
        lane1 = lax.broadcasted_iota(jnp.int32, (1, 128), 1)
        c = c_ref[...]
        ones3 = jnp.where((lane >= AUG) & (lane < AUG + 3), 1.0, 0.0)

        def max_norm(t):
            tr = jnp.where(lane < AUG, t.astype(BF16).astype(F32), 0.0)
            return jnp.sqrt(jnp.max(jnp.sum(tr * tr, axis=-1, keepdims=True), axis=0, keepdims=True))

        for hh in range(2):
            ch = jnp.sum(jnp.where(lane == 2 * pair + hh, c, 0.0), axis=-1, keepdims=True)
            c1, c2, c3 = _split3(ch)
            q, k, v = q_ref[...], k_ref[...], v_ref[...]
            if hh == 1:
                q, k, v = (pltpu.roll(t, 64, 1) for t in (q, k, v))
            aug_q = _lane_fill(lane, jnp.where((lane >= AUG + 3) & (lane < AUG + 6), 1.0, 0.0), (c1, c2, c3), AUG)
            aug_k = _lane_fill(lane, ones3, (-c1, -c2, -c3), AUG + 3)
            qa_ref[hh] = jnp.where(lane < AUG, q * FOX_SCALE, aug_q).astype(BF16)
            ka_ref[hh] = jnp.where(lane < AUG, k, aug_k).astype(BF16)
            va_ref[hh] = jnp.where(lane < AUG, v, ones3).astype(BF16)
            stats = (max_norm(q * FOX_SCALE), jnp.max(ch, axis=0, keepdims=True), max_norm(k),
                     jnp.min(ch, axis=0, keepdims=True))
            st_ref[hh, 0] = _lane_fill(lane1, jnp.zeros((1, 128), F32), stats, 0)

    def grp(g):
        return pl.BlockSpec((T, 128), lambda p, t: (t, g * 8 + p))

    hm = pl.BlockSpec((2, T, 128), lambda p, t: (p, t, 0))
    out = jax.ShapeDtypeStruct((FOX_HEADS, S, 128), BF16)
    return pl.pallas_call(
        body, name=name, grid=(FOX_PAIRS, S // T),
        in_specs=[grp(4), grp(5), grp(6), pl.BlockSpec((T, 128), lambda p, t: (t, 0))],
        out_specs=[hm, hm, hm, pl.BlockSpec((2, 1, 1, 128), lambda p, t: (p, t, 0, 0))],
        out_shape=[out, out, out, jax.ShapeDtypeStruct((FOX_HEADS, S // T, 1, 128), F32)],
        compiler_params=_cparams(("parallel", "parallel")),
    )(proj, proj, proj, c_tok)


def _fox_bound(st_ref, head, nb, qi, ki):
    qb_, kb_ = (head * nb + qi) * N_STAT, (head * nb + ki) * N_STAT
    return st_ref[qb_] * st_ref[kb_ + 2] + st_ref[qb_ + 1] - st_ref[kb_ + 3] + 0.01


def _pair_lanes(lane, a0, a1):
    return jnp.where(lane < AUG, a0, pltpu.roll(a1, 64, 1))


def _first_live_key(st_ref, head, nb, qi, thr):
    def body(t, k0):
        k = qi - 1 - t
        return jnp.where(_fox_bound(st_ref, head, nb, qi, k) > thr, k, k0)

    return lax.fori_loop(0, qi, body, qi)


def _last_live_query(st_ref, lm_ref, head, nb, ki):
    def body(t, i1):
        i = ki + 1 + t
        live = _fox_bound(st_ref, head, nb, i, ki) > lm_ref[head * nb + i] - FOX_SKIP
        return jnp.where(live, i, i1)

    return lax.fori_loop(0, nb - 1 - ki, body, ki)


class _BlockStream:
    def __init__(self, hbm_refs, bufs, sems, pair, tb):
        self.hbm, self.bufs, self.sems, self.pair, self.tb = hbm_refs, bufs, sems, pair, tb

    def _copies(self, blk, slot):
        rows = pl.ds(pl.multiple_of(blk * self.tb, self.tb), self.tb)
        return [pltpu.make_async_copy(h.at[pl.ds(2 * self.pair, 2), rows, :], b.at[slot], self.sems.at[n, slot])
                for n, (h, b) in enumerate(zip(self.hbm, self.bufs))]

    def start(self, blk, slot):
        for cp in self._copies(blk, slot):
            cp.start()

    def wait(self, blk, slot):
        for cp in self._copies(blk, slot):
            cp.wait()


def _fox_fwd(qa, ka, va, bounds, *, name):
    S = qa.shape[1]
    tb = min(FOX_TB, S)
    nb = S // tb

    def body(qa_ref, ka_hbm, va_hbm, st_ref, o_ref, qb_ref, lse_ref, kbuf, vbuf, sems, m_s, acc_s):
        pair, qi = pl.program_id(0), pl.program_id(1)
        stream = _BlockStream((ka_hbm, va_hbm), (kbuf, vbuf), sems, pair, tb)

        def head_step(hh, slot, masked):
            s = _dot(qa_ref[hh], kbuf[slot, hh], _DIMS["nt"])
            if masked:
                row = lax.broadcasted_iota(jnp.int32, (tb, tb), 0)
                col = lax.broadcasted_iota(jnp.int32, (tb, tb), 1)
                s = jnp.where(col <= row, s, NEG)
            m_old = m_s[hh]
            m_new = jnp.maximum(m_old, jnp.max(s, axis=-1, keepdims=True))
            p = jnp.exp(s - m_new)
            p_hi = p.astype(BF16)
            p_lo = (p - p_hi.astype(F32)).astype(BF16)
            vv = vbuf[slot, hh]
            acc_s[hh] = (jnp.exp(m_old - m_new) * acc_s[hh]
                         + _dot(p_hi, vv, _DIMS["nn"]) + _dot(p_lo, vv, _DIMS["nn"]))
            m_s[hh] = m_new
            return jnp.min(m_new)

        stream.start(qi, 0)

        @pl.when(qi > 0)
        def _():
            stream.start(qi - 1, 1)

        m_s[...] = jnp.full_like(m_s, NEG)
        acc_s[...] = jnp.zeros_like(acc_s)
        stream.wait(qi, 0)
        k0 = [_first_live_key(st_ref, 2 * pair + hh, nb, qi, head_step(hh, 0, True) - FOX_SKIP) for hh in range(2)]
        n = qi - jnp.minimum(k0[0], k0[1])

        @pl.when((qi > 0) & (n == 0))
        def _():
            stream.wait(qi - 1, 1)

        def loop(t, carry):
            k = qi - 1 - t
            slot = (t + 1) % 2
            stream.wait(k, slot)

            @pl.when(t + 1 < n)
            def _():
                stream.start(k - 1, 1 - slot)

            for hh in range(2):
                @pl.when(k >= k0[hh])
                def _():
                    head_step(hh, slot, False)
            return carry

        lax.fori_loop(0, n, loop, 0)
        lane = lax.broadcasted_iota(jnp.int32, (tb, 128), 1)
        outs = []
        for hh in range(2):
            acc = acc_s[hh]
            l = acc[:, AUG:AUG + 1]
            outs.append(acc / l)
            lse = m_s[hh] + jnp.log(l)
            lse_ref[hh, 0] = jnp.broadcast_to(jnp.min(lse, axis=0, keepdims=True), (1, 128))
            qf = qa_ref[hh].astype(F32)
            cb = qf[:, AUG:AUG + 1] + qf[:, AUG + 1:AUG + 2] + qf[:, AUG + 2:AUG + 3] - lse
            qb_ref[hh] = _lane_fill(lane, qf, _split3(cb), AUG).astype(BF16)
        o_ref[...] = _pair_lanes(lane, outs[0], outs[1])

    qs = pl.BlockSpec((2, tb, 128), lambda p, i: (p, i, 0))
    return pl.pallas_call(
        body, name=name, grid=(FOX_PAIRS, nb),
        in_specs=[qs, ANY, ANY, SMEM],
        out_specs=[pl.BlockSpec((tb, 128), lambda p, i: (i, p)), qs,
                   pl.BlockSpec((2, 1, 1, 128), lambda p, i: (p, i, 0, 0))],
        out_shape=[jax.ShapeDtypeStruct((S, FOX_HEADS * FOX_DH), F32), jax.ShapeDtypeStruct((FOX_HEADS, S, 128), BF16),
                   jax.ShapeDtypeStruct((FOX_HEADS, nb, 1, 128), F32)],
        scratch_shapes=[pltpu.VMEM((2, 2, tb, 128), BF16), pltpu.VMEM((2, 2, tb, 128), BF16),
                        pltpu.SemaphoreType.DMA((2, 2)), pltpu.VMEM((2, tb, 1), F32), pltpu.VMEM((2, tb, 128), F32)],
        compiler_params=_cparams(("parallel", "arbitrary")),
    )(qa, ka, va, bounds)


def _fox_bwd_prep(o, do, *, name, T=512):
    S = o.shape[0]
    T = min(T, S)

    def body(o_ref, do_ref, dob_ref):
        lane = lax.broadcasted_iota(jnp.int32, (T, 128), 1)
        d = do_ref[...].astype(F32)
        prod = d * o_ref[...]
        for hh in range(2):
            mine = (lane < AUG) if hh == 0 else (lane >= AUG)
            delta = jnp.sum(jnp.where(mine, prod, 0.0), axis=-1, keepdims=True)
            dh = d if hh == 0 else pltpu.roll(d, 64, 1)
            dob_ref[hh] = _lane_fill(lane, jnp.where(lane < AUG, dh, 0.0), _split3(-delta), AUG).astype(BF16)

    tok = pl.BlockSpec((T, 128), lambda p, t: (t, p))
    return pl.pallas_call(
        body, name=name, grid=(FOX_PAIRS, S // T),
        in_specs=[tok, tok], out_specs=pl.BlockSpec((2, T, 128), lambda p, t: (p, t, 0)),
        out_shape=jax.ShapeDtypeStruct((FOX_HEADS, S, 128), BF16),
        compiler_params=_cparams(("parallel", "parallel")),
    )(o, do)


def _fox_bwd_dq(qb, ka, va, dob, bounds, lse_min, *, name, comm=None):
    S = qb.shape[1]
    tb = min(FOX_TB, S)
    nb = S // tb
    nc = comm.n if comm is not None else 0

    def body(qb_ref, dob_ref, ka_hbm, va_hbm, st_ref, lm_ref, *rest):
        c_in, (dq_ref, dcs_ref), c_out = rest[:nc], rest[nc:nc + 2], rest[nc + 2:2 * nc + 2]
        kbuf, vbuf, sems, acc_s = rest[2 * nc + 2:2 * nc + 6]
        c_sems = rest[2 * nc + 6:]
        pair, qi = pl.program_id(0), pl.program_id(1)
        if comm is not None:
            @pl.when((pair == 0) & (qi == 0))
            def _():
                comm.start(c_in, c_out, c_sems)

        stream = _BlockStream((ka_hbm, va_hbm), (kbuf, vbuf), sems, pair, tb)
        k0 = [_first_live_key(st_ref, 2 * pair + hh, nb, qi, lm_ref[(2 * pair + hh) * nb + qi] - FOX_SKIP)
              for hh in range(2)]
        n = qi - jnp.minimum(k0[0], k0[1]) + 1
        stream.start(qi, 0)
        acc_s[...] = jnp.zeros_like(acc_s)
        dcs_ref[...] = jnp.zeros_like(dcs_ref)

        def head_step(hh, slot, k, masked):
            s = _dot(qb_ref[hh], kbuf[slot, hh], _DIMS["nt"])
            if masked:
                row = lax.broadcasted_iota(jnp.int32, (tb, tb), 0)
                col = lax.broadcasted_iota(jnp.int32, (tb, tb), 1)
                s = jnp.where(col <= row, s, NEG)
            ds = jnp.exp(s) * _dot(dob_ref[hh], vbuf[slot, hh], _DIMS["nt"])
            dcs_ref[0, 0, hh:hh + 1, pl.ds(pl.multiple_of(k * tb, tb), tb)] = jnp.sum(ds, axis=0, keepdims=True)
            acc_s[hh] += _dot(ds.astype(BF16), kbuf[slot, hh], _DIMS["nn"])

        def loop(t, carry):
            k = qi - t
            slot = t % 2
            stream.wait(k, slot)

            @pl.when(t + 1 < n)
            def _():
                stream.start(k - 1, 1 - slot)

            @pl.when(t == 0)
            def _():
                for hh in range(2):
                    head_step(hh, slot, k, True)

            for hh in range(2):
                @pl.when((t > 0) & (k >= k0[hh]))
                def _():
                    head_step(hh, slot, k, False)
            return carry

        lax.fori_loop(0, n, loop, 0)
        lane = lax.broadcasted_iota(jnp.int32, (tb, 128), 1)
        dq_ref[...] = (_pair_lanes(lane, acc_s[0], acc_s[1]) * FOX_SCALE).astype(dq_ref.dtype)
        if comm is not None:
            @pl.when((pair == FOX_PAIRS - 1) & (qi == nb - 1))
            def _():
                comm.finish(c_in, c_out, c_sems)

    qs = pl.BlockSpec((2, tb, 128), lambda p, i: (p, i, 0))
    outs = pl.pallas_call(
        body, name=name, grid=(FOX_PAIRS, nb),
        in_specs=[qs, qs, ANY, ANY, SMEM, SMEM] + [ANY] * nc,
        out_specs=[pl.BlockSpec((tb, 128), lambda p, i: (i, p)),
                   pl.BlockSpec((1, 1, 2, S), lambda p, i: (p, i, 0, 0))] + [ANY] * nc,
        out_shape=[jax.ShapeDtypeStruct((S, FOX_HEADS * FOX_DH), BF16),
                   jax.ShapeDtypeStruct((FOX_PAIRS, nb, 2, S), F32)] + (comm.out_shapes if comm is not None else []),
        scratch_shapes=[pltpu.VMEM((2, 2, tb, 128), BF16), pltpu.VMEM((2, 2, tb, 128), BF16),
                        pltpu.SemaphoreType.DMA((2, 2)), pltpu.VMEM((2, tb, 128), F32)]
        + (comm.scratch if comm is not None else []),
        compiler_params=_cparams(("parallel", "arbitrary") if comm is None else ("arbitrary", "arbitrary")),
    )(qb, dob, ka, va, bounds, lse_min, *(comm.inputs if comm is not None else []))
    return (outs[0], outs[1]) if comm is None else (outs[0], outs[1], outs[2:])


def _fox_bwd_dkv(qb, ka, va, dob, bounds, lse_min, *, name):
    S = qb.shape[1]
    tb = min(FOX_TB, S)
    nb = S // tb

    def body(ka_ref, va_ref, qb_hbm, dob_hbm, st_ref, lm_ref, dk_ref, dv_ref, qbuf, dbuf, sems, dk_s, dv_s):
        pair, ki = pl.program_id(0), pl.program_id(1)
        stream = _BlockStream((qb_hbm, dob_hbm), (qbuf, dbuf), sems, pair, tb)
        i1 = [_last_live_query(st_ref, lm_ref, 2 * pair + hh, nb, ki) for hh in range(2)]
        n = jnp.maximum(i1[0], i1[1]) - ki + 1
        stream.start(ki, 0)
        dk_s[...] = jnp.zeros_like(dk_s)
        dv_s[...] = jnp.zeros_like(dv_s)

        def head_step(hh, slot, masked):
            st = _dot(ka_ref[hh], qbuf[slot, hh], _DIMS["nt"])
            if masked:
                row = lax.broadcasted_iota(jnp.int32, (tb, tb), 0)
                col = lax.broadcasted_iota(jnp.int32, (tb, tb), 1)
                st = jnp.where(row <= col, st, NEG)
            pt = jnp.exp(st)
            dst = pt * _dot(va_ref[hh], dbuf[slot, hh], _DIMS["nt"])
            dv_s[hh] += _dot(pt.astype(BF16), dbuf[slot, hh], _DIMS["nn"])
            dk_s[hh] += _dot(dst.astype(BF16), qbuf[slot, hh], _DIMS["nn"])

        def loop(t, carry):
            i = ki + t
            slot = t % 2
            stream.wait(i, slot)

            @pl.when(t + 1 < n)
            def _():
                stream.start(i + 1, 1 - slot)

            @pl.when(t == 0)
            def _():
                for hh in range(2):
                    head_step(hh, slot, True)

            for hh in range(2):
                @pl.when((t > 0) & (i <= i1[hh]))
                def _():
                    head_step(hh, slot, False)
            return carry

        lax.fori_loop(0, n, loop, 0)
        lane = lax.broadcasted_iota(jnp.int32, (tb, 128), 1)
        dk_ref[...] = _pair_lanes(lane, dk_s[0], dk_s[1]).astype(dk_ref.dtype)
        dv_ref[...] = _pair_lanes(lane, dv_s[0], dv_s[1]).astype(dv_ref.dtype)

    ks = pl.BlockSpec((2, tb, 128), lambda p, j: (p, j, 0))
    tok = pl.BlockSpec((tb, 128), lambda p, j: (j, p))
    big = jax.ShapeDtypeStruct((S, FOX_HEADS * FOX_DH), BF16)
    return pl.pallas_call(
        body, name=name, grid=(FOX_PAIRS, nb),
        in_specs=[ks, ks, ANY, ANY, SMEM, SMEM], out_specs=[tok, tok], out_shape=[big, big],
        scratch_shapes=[pltpu.VMEM((2, 2, tb, 128), BF16), pltpu.VMEM((2, 2, tb, 128), BF16),
                        pltpu.SemaphoreType.DMA((2, 2)), pltpu.VMEM((2, tb, 128), F32), pltpu.VMEM((2, tb, 128), F32)],
        compiler_params=_cparams(("parallel", "arbitrary")),
    )(ka, va, qb, dob, bounds, lse_min)


def _merge_fwd(proj, pa, pb, *, name, T=512):
    S, D = pa.shape
    T = min(T, S)

    def body(ga_ref, gb_ref, pa_ref, pb_ref, m_ref):
        m_ref[...] = (_sigmoid(ga_ref[...]) * pa_ref[...] + _sigmoid(gb_ref[...]) * pb_ref[...]).astype(m_ref.dtype)

    tok = pl.BlockSpec((T, D), lambda i: (i, 0))
    return pl.pallas_call(
        body, name=name, grid=(S // T,),
        in_specs=[pl.BlockSpec((T, D), lambda i: (i, 7)), pl.BlockSpec((T, D), lambda i: (i, 8)), tok, tok],
        out_specs=tok, out_shape=jax.ShapeDtypeStruct((S, D), BF16),
        compiler_params=_cparams(("parallel",)),
    )(proj, proj, pa, pb)


def _merge_bwd(proj, pa, pb, dm, *, name, T=512):
    S, D = pa.shape
    T = min(T, S)

    def body(ga_ref, gb_ref, pa_ref, pb_ref, dm_ref, dpa_ref, dpb_ref, dga_ref, dgb_ref):
        dm_ = dm_ref[...]
        sa, sb = _sigmoid(ga_ref[...]), _sigmoid(gb_ref[...])
        dpa_ref[...] = (dm_ * sa).astype(BF16)
        dpb_ref[...] = (dm_ * sb).astype(BF16)
        dga_ref[...] = (dm_ * pa_ref[...] * sa * (1.0 - sa)).astype(BF16)
        dgb_ref[...] = (dm_ * pb_ref[...] * sb * (1.0 - sb)).astype(BF16)

    tok = pl.BlockSpec((T, D), lambda i: (i, 0))
    big = jax.ShapeDtypeStruct((S, D), BF16)
    return pl.pallas_call(
        body, name=name, grid=(S // T,),
        in_specs=[pl.BlockSpec((T, D), lambda i: (i, 7)), pl.BlockSpec((T, D), lambda i: (i, 8)), tok, tok, tok],
        out_specs=[tok, tok, tok, tok], out_shape=[big, big, big, big],
        compiler_params=_cparams(("parallel",)),
    )(proj, proj, pa, pb, dm)


INV_SQRT2 = 0.7071067811865476
INV_SQRT2PI = 0.3989422804014327


def _shifted(u, prev, rid):
    m1 = jnp.where(rid == 0, prev[7:8, :], pltpu.roll(u, 1, 0))
    m2 = jnp.where(rid == 0, prev[6:7, :], jnp.where(rid == 1, prev[7:8, :], pltpu.roll(u, 2, 0)))
    return m1, m2


def _conv_acc(u, prev, w_ref, b_ref, rid):
    m1, m2 = _shifted(u, prev, rid)
    return b_ref[...] + w_ref[0:1, :] * m2 + w_ref[1:2, :] * m1 + w_ref[2:3, :] * u, m1, m2


def _convglu_fwd(ug, uv, wg, wv, bg, bv, *, name, T=512, tc=256):
    S, F = ug.shape
    T = min(T, S)

    def body(ug_ref, uv_ref, wg_ref, wv_ref, bg_ref, bv_ref, a_ref, pg, pv):
        @pl.when(pl.program_id(1) == 0)
        def _():
            pg[...] = jnp.zeros_like(pg)
            pv[...] = jnp.zeros_like(pv)

        rid = lax.broadcasted_iota(jnp.int32, (T, tc), 0)
        g_, v_ = ug_ref[...], uv_ref[...]
        accg, _, _ = _conv_acc(g_, pg[...], wg_ref, bg_ref, rid)
        accv, _, _ = _conv_acc(v_, pv[...], wv_ref, bv_ref, rid)
        gel = 0.5 * accg * (1.0 + lax.erf(accg * INV_SQRT2))
        a_ref[...] = (gel * accv).astype(a_ref.dtype)
        pg[...] = g_[T - 8:T, :]
        pv[...] = v_[T - 8:T, :]

    tok = pl.BlockSpec((T, tc), lambda j, t: (t, j))
    w3 = pl.BlockSpec((3, tc), lambda j, t: (0, j))
    b1 = pl.BlockSpec((1, tc), lambda j, t: (0, j))
    return pl.pallas_call(
        body, name=name, grid=(F // tc, S // T),
        in_specs=[tok, tok, w3, w3, b1, b1], out_specs=tok,
        out_shape=jax.ShapeDtypeStruct((S, F), BF16),
        scratch_shapes=[pltpu.VMEM((8, tc), F32), pltpu.VMEM((8, tc), F32)],
        compiler_params=_cparams(("parallel", "arbitrary")),
    )(ug, uv, wg, wv, bg, bv)


def _convglu_bwd(ug, uv, wg, wv, bg, bv, da, *, name, T=512, tc=256):
    S, F = ug.shape
    T = min(T, S)
    nT = S // T
    halo_blocks = T // 8

    def up_shift(d, nx, rid):
        p1 = jnp.where(rid == T - 1, nx[0:1, :], pltpu.roll(d, T - 1, 0))
        p2 = jnp.where(rid == T - 1, nx[1:2, :], jnp.where(rid == T - 2, nx[0:1, :], pltpu.roll(d, T - 2, 0)))
        return p1, p2

    def body(ug_ref, uv_ref, hg_ref, hv_ref, wg_ref, wv_ref, bg_ref, bv_ref, da_ref,
             dug_ref, duv_ref, dwg_ref, dwv_ref, dbg_ref, dbv_ref, ng, nv):
        @pl.when(pl.program_id(1) == 0)
        def _():
            ng[...] = jnp.zeros_like(ng)
            nv[...] = jnp.zeros_like(nv)
            for r in (dwg_ref, dwv_ref, dbg_ref, dbv_ref):
                r[...] = jnp.zeros_like(r)

        first_block = pl.program_id(1) == nT - 1
        rid = lax.broadcasted_iota(jnp.int32, (T, tc), 0)
        g_, v_ = ug_ref[...], uv_ref[...]
        pg = jnp.where(first_block, 0.0, hg_ref[...])
        pv = jnp.where(first_block, 0.0, hv_ref[...])
        accg, g1, g2 = _conv_acc(g_, pg, wg_ref, bg_ref, rid)
        accv, v1, v2 = _conv_acc(v_, pv, wv_ref, bv_ref, rid)
        cdf = 0.5 * (1.0 + lax.erf(accg * INV_SQRT2))
        pdf = INV_SQRT2PI * jnp.exp(-0.5 * accg * accg)
        da_ = da_ref[...].astype(F32)
        dgate = da_ * accv * (cdf + accg * pdf)
        dval = da_ * (accg * cdf)
        dbg_ref[...] += jnp.sum(dgate, axis=0, keepdims=True)
        dbv_ref[...] += jnp.sum(dval, axis=0, keepdims=True)
        for j, (sg_, sv_) in enumerate(((g2, v2), (g1, v1), (g_, v_))):
            dwg_ref[j:j + 1, :] += jnp.sum(dgate * sg_, axis=0, keepdims=True)
            dwv_ref[j:j + 1, :] += jnp.sum(dval * sv_, axis=0, keepdims=True)
        for d, w_ref, nx, out_ref in ((dgate, wg_ref, ng, dug_ref), (dval, wv_ref, nv, duv_ref)):
            p1, p2 = up_shift(d, nx[...], rid)
            out_ref[...] = (w_ref[2:3, :] * d + w_ref[1:2, :] * p1 + w_ref[0:1, :] * p2).astype(out_ref.dtype)
            nx[...] = d[0:8, :]

    tok = pl.BlockSpec((T, tc), lambda j, t: (nT - 1 - t, j))
    halo = pl.BlockSpec((8, tc), lambda j, t: (jnp.maximum((nT - 1 - t) * halo_blocks - 1, 0), j))
    w3 = pl.BlockSpec((3, tc), lambda j, t: (0, j))
    b1 = pl.BlockSpec((1, tc), lambda j, t: (0, j))
    big = jax.ShapeDtypeStruct((S, F), BF16)
    return pl.pallas_call(
        body, name=name, grid=(F // tc, nT),
        in_specs=[tok, tok, halo, halo, w3, w3, b1, b1, tok], out_specs=[tok, tok, w3, w3, b1, b1],
        out_shape=[big, big, jax.ShapeDtypeStruct((3, F), F32), jax.ShapeDtypeStruct((3, F), F32),
                   jax.ShapeDtypeStruct((1, F), F32), jax.ShapeDtypeStruct((1, F), F32)],
        scratch_shapes=[pltpu.VMEM((8, tc), F32), pltpu.VMEM((8, tc), F32)],
        compiler_params=_cparams(("parallel", "arbitrary")),
    )(ug, uv, ug, uv, wg, wv, bg, bv, da)


FF_LO, FF_HI = 7168, 7184


def _col_blocks(a, width):
    return jnp.stack([a[:, d * width:(d + 1) * width] for d in range(N_DEV)])


def _late_weights(g_a, g_b, g_o, g_up, g_cw, g_d):
    wup = jnp.concatenate([g_up[d] for d in range(N_DEV)], axis=1)
    cw = jnp.concatenate([g_cw[d] for d in range(N_DEV)], axis=1)
    return dict(wa=g_a.reshape(D_MODEL, D_MODEL), wb=g_b.reshape(D_MODEL, D_MODEL), wo=g_o.reshape(D_MODEL, D_MODEL),
                wug=wup[:, :D_FF], wuv=wup[:, D_FF:], cwg=cw[:, :D_FF], cwv=cw[:, D_FF:], wd=g_d.reshape(D_FF, D_MODEL))


def _early_grad_blocks(d_wa, d_wb, d_wo, d_wug, d_wuv, d_wd):
    up = jnp.stack([d_wug[:, d * 704:(d + 1) * 704] for d in range(4)]
                   + [d_wuv[:, d * 704:(d + 1) * 704] for d in range(4)])
    return [d_wa.reshape(N_DEV, 128, D_MODEL), d_wb.reshape(N_DEV, 128, D_MODEL), d_wo.reshape(N_DEV, 128, D_MODEL),
            up, d_wd.reshape(N_DEV, 352, D_MODEL)]


def _local_step(x, tgt, w, p, late=None, exchange=False):
    S = x.shape[0]
    mm = _matmul
    n1 = _rms_fwd(x, p["norm_mix"], name="rms1_fwd")
    if late is None:
        proj = mm(n1, w["wm"], "nn", name="proj_main")
    else:
        proj, gathered = mm(n1, w["wm"], "nn", comm=late, name="proj_main")
        w = {**w, **_late_weights(*gathered)}
    ff = mm(n1, w["wff"], "nn", name="proj_ff")
    lb = _lb_fwd(p["hg_lb_logits"], name="lb_fwd")
    gnorm = p["hg_norm"].reshape(1, HG_DV)
    o_hg, oa, states = _hgrn_fwd(proj, lb, gnorm, name="hgrn_fwd")
    bias = jnp.pad(p["fox_f_bias"].reshape(1, FOX_HEADS), ((0, 0), (0, 128 - FOX_HEADS)))
    c = _fox_gate_fwd(ff, bias, name="fox_gate_fwd")
    qa, ka, va, fox_stats = _fox_prep(proj, c, name="fox_prep")
    bounds = fox_stats[:, :, 0, :N_STAT].reshape(-1)
    ob, qb, lse_stats = _fox_fwd(qa, ka, va, bounds, name="fox_fwd")
    lse_min = lse_stats[:, :, 0, 0].reshape(-1)
    pa = mm(oa, w["wa"], "nn", name="branch_a")
    pb = mm(ob, w["wb"], "nn", name="branch_b")
    merged = _merge_fwd(proj, pa, pb, name="merge_fwd")
    h1 = mm(merged, w["wo"], "nn", addend=x, name="mix_out")
    n2 = _rms_fwd(h1, p["norm_ffn"], name="rms2_fwd")
    ug = mm(n2, w["wug"], "nn", name="up_gate")
    uv = mm(n2, w["wuv"], "nn", name="up_val")
    a = _convglu_fwd(ug, uv, w["cwg"], w["cwv"], p["cbg"], p["cbv"], name="convglu_fwd")
    h2 = mm(a, w["wd"], "nn", addend=h1, name="ffn_down")
    loss, dh2, d_norm_final = _loss_head(h2, p["norm_final"], tgt, name="loss_head")
    da = mm(dh2, w["wd"], "nt", out_dtype=BF16, name="d_act")
    d_wd = mm(a, dh2, "tn", out_dtype=BF16, name="dw_down")
    dug, duv, d_cwg, d_cwv, d_cbg, d_cbv = _convglu_bwd(
        ug, uv, w["cwg"], w["cwv"], p["cbg"], p["cbv"], da, name="convglu_bwd")
    dn2 = mm(dug, w["wug"], "nt", name="dn2_gate")
    dn2 = mm(duv, w["wuv"], "nt", addend=dn2, name="dn2_val")
    d_wug = mm(n2, dug, "tn", out_dtype=BF16, name="dw_up_gate")
    d_wuv = mm(n2, duv, "tn", out_dtype=BF16, name="dw_up_val")
    dh1, d_norm_ffn = _rms_bwd(h1, p["norm_ffn"], dn2, dh2, name="rms2_bwd")
    dmerged = mm(dh1, w["wo"], "nt", name="d_merged")
    d_wo = mm(merged, dh1, "tn", out_dtype=BF16, name="dw_out")
    dpa, dpb, dga, dgb = _merge_bwd(proj, pa, pb, dmerged, name="merge_bwd")
    doa = mm(dpa, w["wa"], "nt", name="d_oa")
    dob = mm(dpb, w["wb"], "nt", out_dtype=BF16, name="d_ob")
    d_wa = mm(oa, dpa, "tn", out_dtype=BF16, name="dw_branch_a")
    d_wb = mm(ob, dpb, "tn", out_dtype=BF16, name="dw_branch_b")
    dhq, dhf, dhi, dhg, dlb, dgn8 = _hgrn_bwd(proj, lb, gnorm, o_hg, states, doa, name="hgrn_bwd")
    d_logits = _lb_bwd(p["hg_lb_logits"], dlb, name="lb_bwd")
    dob_hm = _fox_bwd_prep(ob, dob, name="fox_bwd_prep")
    early_parts = None
    if exchange:
        comm = _ExchangeComm(_early_grad_blocks(d_wa, d_wb, d_wo, d_wug, d_wuv, d_wd))
        dq, dcsp, early_parts = _fox_bwd_dq(qb, ka, va, dob_hm, bounds, lse_min, comm=comm, name="fox_bwd_dq")
    else:
        dq, dcsp = _fox_bwd_dq(qb, ka, va, dob_hm, bounds, lse_min, name="fox_bwd_dq")
    dk, dv = _fox_bwd_dkv(qb, ka, va, dob_hm, bounds, lse_min, name="fox_bwd_dkv")
    dcs = jnp.sum(dcsp, axis=1)
    dcs_tok = jnp.pad(dcs.reshape(FOX_HEADS, S).T, ((0, 0), (0, 128 - FOX_HEADS)))
    dff, dbias = _fox_gate_bwd(ff, bias, dcs_tok, name="fox_gate_bwd")
    dproj = jnp.concatenate([dhq, dhf, dhi, dhg, dq, dk, dv, dga, dgb], axis=1)
    d_wm = mm(n1, dproj, "tn", out_dtype=BF16, name="dw_in_main")
    d_wff = mm(n1, dff, "tn", out_dtype=BF16, name="dw_in_ff")
    dn1 = mm(dff, w["wff"], "nt", name="dn1_ff")
    late_parts = None
    if exchange:
        d_win = jnp.concatenate([d_wm[:, :FF_LO], d_wff[:, :FOX_HEADS], d_wm[:, FF_LO:]], axis=1)
        d_cw = jnp.concatenate([d_cwg, d_cwv], axis=1)
        comm = _ExchangeComm([_col_blocks(d_win, 1154), _col_blocks(d_cw, 704)])
        dn1, late_parts = mm(dproj, w["wm"], "nt", addend=dn1, comm=comm, name="dn1_main")
    else:
        dn1 = mm(dproj, w["wm"], "nt", addend=dn1, name="dn1_main")
    dx, d_norm_mix = _rms_bwd(x, p["norm_mix"], dn1, dh1, name="rms1_bwd")
    grads = dict(
        wm=d_wm, wff=d_wff, wa=d_wa, wb=d_wb, wo=d_wo, wug=d_wug, wuv=d_wuv, cwg=d_cwg, cwv=d_cwv, wd=d_wd,
        norm_mix=d_norm_mix.reshape(-1), fox_f_bias=dbias[0, :FOX_HEADS], hg_lb_logits=d_logits,
        hg_norm=jnp.sum(dgn8, axis=0).reshape(-1), norm_ffn=d_norm_ffn.reshape(-1), cbg=d_cbg, cbv=d_cbv,
        norm_final=d_norm_final.reshape(-1), early_parts=early_parts, late_parts=late_parts)
    return loss, dx, grads


SMALL = [("norm_mix", (1, D_MODEL)), ("fox_f_bias", (1, FOX_HEADS)), ("hg_lb_logits", (2, HG_HEADS * HG_DK)),
         ("hg_norm", (1, HG_DV)), ("norm_ffn", (1, D_MODEL)), ("conv_b", (1, 2 * D_FF)), ("norm_final", (D_MODEL,))]
SMALL_ROWS = 88
SHARDED = [("w_in", (D_MODEL, 1154), 256), ("w_branch_a", (128, D_MODEL), 128), ("w_branch_b", (128, D_MODEL), 128),
           ("w_out", (128, D_MODEL), 128), ("w_up", (D_MODEL, 704), 256), ("conv_w", (3, 704), 3),
           ("w_down", (352, D_MODEL), 352)]
NAMES = ["norm_mix", "w_in", "fox_f_bias", "hg_lb_logits", "hg_norm", "w_branch_a", "w_branch_b", "w_out",
         "norm_ffn", "w_up", "conv_w", "conv_b", "w_down", "norm_final"]


def _size(shape):
    n = 1
    for s in shape:
        n *= s
    return n


def _adamw(parts, w, m, v, *, name, T):
    R, C = w.shape
    c1 = 1.0 / (1.0 - ADAM_B1 ** ADAM_STEP)
    c2 = 1.0 / (1.0 - ADAM_B2 ** ADAM_STEP)

    def body(p_ref, w_ref, m_ref, v_ref, g_ref, d_ref, nm_ref, nv_ref):
        g = p_ref[0].astype(F32)
        for s in range(1, N_DEV):
            g = g + p_ref[s].astype(F32)
        g_ref[...] = g
        nm = ADAM_B1 * m_ref[...] + (1.0 - ADAM_B1) * g
        nv = ADAM_B2 * v_ref[...] + (1.0 - ADAM_B2) * (g * g)
        nm_ref[...] = nm
        nv_ref[...] = nv
        d_ref[...] = -ADAM_LR * ((nm * c1) / (jnp.sqrt(nv * c2) + ADAM_EPS) + ADAM_WD * w_ref[...])

    blk = pl.BlockSpec((T, C), lambda i: (i, 0))
    out = jax.ShapeDtypeStruct((R, C), F32)
    return pl.pallas_call(
        body, name=name, grid=(R // T,),
        in_specs=[pl.BlockSpec((N_DEV, T, C), lambda i: (0, i, 0)), blk, blk, blk],
        out_specs=[blk, blk, blk, blk], out_shape=[out, out, out, out],
        compiler_params=_cparams(("parallel",)),
    )(parts, w, m, v)


def _pack_small(vals):
    flat = jnp.concatenate([vals[n].reshape(-1).astype(F32) for n, _ in SMALL])
    return jnp.pad(flat, (0, SMALL_ROWS * 128 - flat.shape[0])).reshape(SMALL_ROWS, 128)


def _unpack_small(buf):
    flat, out, off = buf.reshape(-1), {}, 0
    for n, shape in SMALL:
        out[n] = flat[off:off + _size(shape)].reshape(shape)
        off += _size(shape)
    return out


def kernel(x, norm_mix, w_in, fox_f_bias, hg_lb_logits, hg_norm, w_branch_a, w_branch_b, w_out, norm_ffn, w_up, conv_w, conv_b, w_down, norm_final, loss_target, m_norm_mix, m_w_in, m_fox_f_bias, m_hg_lb_logits, m_hg_norm, m_w_branch_a, m_w_branch_b, m_w_out, m_norm_ffn, m_w_up, m_conv_w, m_conv_b, m_w_down, m_norm_final, v_norm_mix, v_w_in, v_fox_f_bias, v_hg_lb_logits, v_hg_norm, v_w_branch_a, v_w_branch_b, v_w_out, v_norm_ffn, v_w_up, v_conv_w, v_conv_b, v_w_down, v_norm_final):
    wv = dict(norm_mix=norm_mix, w_in=w_in, fox_f_bias=fox_f_bias, hg_lb_logits=hg_lb_logits, hg_norm=hg_norm,
              w_branch_a=w_branch_a, w_branch_b=w_branch_b, w_out=w_out, norm_ffn=norm_ffn, w_up=w_up, conv_w=conv_w,
              conv_b=conv_b, w_down=w_down, norm_final=norm_final)
    mv = dict(norm_mix=m_norm_mix, w_in=m_w_in, fox_f_bias=m_fox_f_bias, hg_lb_logits=m_hg_lb_logits, hg_norm=m_hg_norm,
              w_branch_a=m_w_branch_a, w_branch_b=m_w_branch_b, w_out=m_w_out, norm_ffn=m_norm_ffn, w_up=m_w_up,
              conv_w=m_conv_w, conv_b=m_conv_b, w_down=m_w_down, norm_final=m_norm_final)
    vv = dict(norm_mix=v_norm_mix, w_in=v_w_in, fox_f_bias=v_fox_f_bias, hg_lb_logits=v_hg_lb_logits, hg_norm=v_hg_norm,
              w_branch_a=v_w_branch_a, w_branch_b=v_w_branch_b, w_out=v_w_out, norm_ffn=v_norm_ffn, w_up=v_w_up,
              conv_w=v_conv_w, conv_b=v_conv_b, w_down=v_w_down, norm_final=v_norm_final)

    (g_in,) = _comm_call(_GatherComm([w_in[0].astype(BF16)]), name="gather_w_in")
    win = jnp.concatenate([g_in[d] for d in range(N_DEV)], axis=1)
    w = dict(wm=jnp.concatenate([win[:, :FF_LO], win[:, FF_HI:]], axis=1),
             wff=jnp.pad(win[:, FF_LO:FF_HI], ((0, 0), (0, 128 - FOX_HEADS))))
    late = _GatherComm([w_branch_a[0].astype(BF16), w_branch_b[0].astype(BF16), w_out[0].astype(BF16),
                        w_up[0].astype(BF16), conv_w[0], w_down[0].astype(BF16)])
    p = dict(norm_mix=norm_mix[0], fox_f_bias=fox_f_bias[0], hg_lb_logits=hg_lb_logits, hg_norm=hg_norm[0],
             norm_ffn=norm_ffn[0], cbg=conv_b[:, :D_FF], cbv=conv_b[:, D_FF:], norm_final=norm_final)
    loss, dx, grads = _local_step(x[0], loss_target[0], w, p, late=late, exchange=True)
    loss = lax.psum(loss[0, 0], ("x", "y", "c"))

    small = _pack_small(dict(
        norm_mix=grads["norm_mix"], fox_f_bias=grads["fox_f_bias"], hg_lb_logits=grads["hg_lb_logits"],
        hg_norm=grads["hg_norm"], norm_ffn=grads["norm_ffn"], conv_b=jnp.concatenate([grads["cbg"], grads["cbv"]], axis=1),
        norm_final=grads["norm_final"]))
    (small_parts,) = _comm_call(_ExchangeComm([jnp.broadcast_to(small[None], (N_DEV, SMALL_ROWS, 128))]),
                                name="exchange_small")
    ea, eb, eo, eup, ed = grads["early_parts"]
    p_in, p_cw = grads["late_parts"]
    parts = [p_in, ea, eb, eo, eup, p_cw, ed, small_parts]
    res = {}
    for (n, shape, tile), part in zip(SHARDED, parts):
        outs = _adamw(part, wv[n].reshape(shape), mv[n].reshape(shape), vv[n].reshape(shape), name="adamw_" + n, T=tile)
        res[n] = [o.reshape(wv[n].shape) for o in outs]
    outs = _adamw(parts[-1], _pack_small(wv), _pack_small(mv), _pack_small(vv), name="adamw_small", T=SMALL_ROWS)
    small = [_unpack_small(o) for o in outs]
    for n, _ in SMALL:
        res[n] = [s[n] for s in small]
    return (loss, dx[None], *[res[n][0] for n in NAMES], *[res[n][1] for n in NAMES],
            *[res[n][2] for n in NAMES], *[res[n][3] for n in NAMES])
```

```python
import jax
import jax.numpy as jnp
from jax import lax
from jax.experimental import pallas as pl
from jax.experimental.pallas import tpu as pltpu

F32 = jnp.float32
BF16 = jnp.bfloat16

D_MODEL = 1024
HG_HEADS = 8
HG_DK = 128
HG_DV = 128
HG_CHUNK = 64
FOX_HEADS = 16
FOX_DH = 64
D_FF = 2816
EPS = 1e-6
N_DEV = 8

ADAM_LR = 0.001
ADAM_B1 = 0.9
ADAM_B2 = 0.999
ADAM_EPS = 1e-08
ADAM_WD = 0.01
ADAM_STEP = 10

VMEM_LIMIT = 56 * 1024 * 1024


def _cparams(sem):
    return pltpu.CompilerParams(dimension_semantics=sem, vmem_limit_bytes=VMEM_LIMIT)


MESH = pl.DeviceIdType.MESH
ANY = pl.BlockSpec(memory_space=pl.ANY)
SMEM = pl.BlockSpec(memory_space=pltpu.SMEM)


class _GatherComm:
    def __init__(self, shards):
        self.inputs = list(shards)
        n = self.n = len(shards)
        self.out_shapes = [jax.ShapeDtypeStruct((N_DEV,) + s.shape, s.dtype) for s in shards]
        self.scratch = [pltpu.SemaphoreType.DMA((n, 7)), pltpu.SemaphoreType.DMA((n, 7)), pltpu.SemaphoreType.DMA((n,))]

    def _parts(self, x_refs, out_refs, sems):
        send_sems, recv_sems, local_sems = sems
        x, y, c = lax.axis_index("x"), lax.axis_index("y"), lax.axis_index("c")
        me, sibling = (x, y, c), (x, y, 1 - c)
        chips = [(1 - x, y), (x, 1 - y), (1 - x, 1 - y)]

        def copy(t, k, block, to, src=None):
            slot = out_refs[t].at[4 * block[0] + 2 * block[1] + block[2]]
            return pltpu.make_async_remote_copy(
                src_ref=slot if src is None else src, dst_ref=slot,
                send_sem=send_sems.at[t, k], recv_sem=recv_sems.at[t, k], device_id=to, device_id_type=MESH)

        mine = [pltpu.make_async_copy(x_refs[t], out_refs[t].at[4 * x + 2 * y + c], local_sems.at[t])
                for t in range(self.n)]
        first = []
        for t in range(self.n):
            first.append(copy(t, 0, me, sibling, src=x_refs[t]))
            first += [copy(t, 1 + j, me, (*chip, c), src=x_refs[t]) for j, chip in enumerate(chips)]
        return c, me, sibling, chips, copy, mine, first

    def start(self, x_refs, out_refs, sems):
        _, _, _, _, _, mine, first = self._parts(x_refs, out_refs, sems)
        for cp in mine + first:
            cp.start()

    def finish(self, x_refs, out_refs, sems):
        c, me, sibling, chips, copy, mine, first = self._parts(x_refs, out_refs, sems)
        passed = []
        for j, chip in enumerate(chips):
            for t in range(self.n):
                copy(t, 1 + j, (*chip, c), me).wait_recv()
                passed.append(copy(t, 4 + j, (*chip, c), sibling))
                passed[-1].start()
        for t in range(self.n):
            copy(t, 0, sibling, me).wait_recv()
            for j, chip in enumerate(chips):
                copy(t, 4 + j, (*chip, 1 - c), me).wait_recv()
        for cp in first + passed:
            cp.wait_send()
        for cp in mine:
            cp.wait()


class _ExchangeComm:
    def __init__(self, blocks):
        self.inputs = list(blocks)
        n = self.n = len(blocks)
        self.out_shapes = [jax.ShapeDtypeStruct(b.shape, b.dtype) for b in blocks]
        self.scratch = [pltpu.SemaphoreType.DMA((n, 7)), pltpu.SemaphoreType.DMA((n, 7)), pltpu.SemaphoreType.DMA((n,))]

    def _parts(self, g_refs, out_refs, sems):
        send_sems, recv_sems, local_sems = sems
        x, y, c = lax.axis_index("x"), lax.axis_index("y"), lax.axis_index("c")
        me = 4 * x + 2 * y + c
        mine = [pltpu.make_async_copy(g_refs[t].at[me], out_refs[t].at[me], local_sems.at[t]) for t in range(self.n)]
        sends, recvs = [], []
        for k in range(1, N_DEV):
            px = 1 - x if k & 4 else x
            py = 1 - y if k & 2 else y
            pc = 1 - c if k & 1 else c
            p = 4 * px + 2 * py + pc
            for t in range(self.n):
                sends.append(pltpu.make_async_remote_copy(
                    src_ref=g_refs[t].at[p], dst_ref=out_refs[t].at[me], send_sem=send_sems.at[t, k - 1],
                    recv_sem=recv_sems.at[t, k - 1], device_id=(px, py, pc), device_id_type=MESH))
                recvs.append(pltpu.make_async_remote_copy(
                    src_ref=g_refs[t].at[p], dst_ref=out_refs[t].at[p], send_sem=send_sems.at[t, k - 1],
                    recv_sem=recv_sems.at[t, k - 1], device_id=(px, py, pc), device_id_type=MESH))
        return mine, sends, recvs

    def start(self, g_refs, out_refs, sems):
        mine, sends, _ = self._parts(g_refs, out_refs, sems)
        for cp in mine + sends:
            cp.start()

    def finish(self, g_refs, out_refs, sems):
        mine, sends, recvs = self._parts(g_refs, out_refs, sems)
        for cp in recvs:
            cp.wait_recv()
        for cp in sends:
            cp.wait_send()
        for cp in mine:
            cp.wait()


def _comm_call(comm, *, name):
    n = comm.n

    def body(*refs):
        comm.start(refs[:n], refs[n:2 * n], refs[2 * n:])
        comm.finish(refs[:n], refs[n:2 * n], refs[2 * n:])

    return pl.pallas_call(body, name=name, in_specs=[ANY] * n, out_specs=[ANY] * n, out_shape=comm.out_shapes,
                          scratch_shapes=comm.scratch)(*comm.inputs)


_DIMS = {
    "nn": (((1,), (0,)), ((), ())),
    "nt": (((1,), (1,)), ((), ())),
    "tn": (((0,), (0,)), ((), ())),
}

MATMUL_VMEM_BUDGET = 36 * 1024 * 1024
MAX_TILE = 1536


def _pick(n, prefs):
    for p in prefs:
        if n % p == 0:
            return p
    return n


def _tile_options(n):
    return [d for d in range(128, min(n, MAX_TILE) + 1, 128) if n % d == 0] or [n]


def _pick_tiles(M, N, tk, nk, sa, sb, so, has_addend, tm, tn):
    best = None
    for cm in ([tm] if tm else _tile_options(M)):
        for cn in ([tn] if tn else _tile_options(N)):
            need = 2 * (cm * tk * sa + tk * cn * sb + cm * cn * so + (cm * cn * 4 if has_addend else 0))
            need += cm * cn * 4 if nk > 1 else 0
            if need <= MATMUL_VMEM_BUDGET and (best is None or cm * cn > best[0] * best[1]
                                               or (cm * cn == best[0] * best[1] and cn > best[1])):
                best = (cm, cn)
    assert best is not None, (M, N, tk)
    return best


def _matmul(a, b, form, *, out_dtype=F32, addend=None, tm=None, tn=None, tk=None, comm=None, name):
    if form == "nn":
        (M, K), (K2, N) = a.shape, b.shape
    elif form == "nt":
        (M, K), (N, K2) = a.shape, b.shape
    else:
        (K, M), (K2, N) = a.shape, b.shape
    assert K == K2, (a.shape, b.shape, form)
    tk = tk or (K if K <= 2816 else _pick(K, (1024, 512, 256, 128)))
    nk = K // tk
    if tm is None or tn is None:
        tm, tn = _pick_tiles(M, N, tk, nk, a.dtype.itemsize, b.dtype.itemsize, jnp.dtype(out_dtype).itemsize,
                             addend is not None, tm, tn)
    assert M % tm == 0 and N % tn == 0 and K % tk == 0, (M, N, K, tm, tn, tk)
    dims = _DIMS[form]
    nc = comm.n if comm is not None else 0
    grid = (M // tm, N // tn, nk)

    def body(*refs):
        a_ref, b_ref = refs[:2]
        pos = 2
        add_ref = refs[pos] if addend is not None else None
        pos += addend is not None
        c_in, o_ref, c_out = refs[pos:pos + nc], refs[pos + nc], refs[pos + nc + 1:pos + 2 * nc + 1]
        pos += 2 * nc + 1
        acc_ref = refs[pos] if nk > 1 else None
        c_sems = refs[pos + (nk > 1):]
        if comm is not None:
            ids = [pl.program_id(d) for d in range(3)]

            @pl.when((ids[0] == 0) & (ids[1] == 0) & (ids[2] == 0))
            def _():
                comm.start(c_in, c_out, c_sems)

        def finish(r):
            if add_ref is not None:
                r = r + add_ref[...].astype(F32)
            o_ref[...] = r.astype(o_ref.dtype)

        part = lax.dot_general(a_ref[...].astype(BF16), b_ref[...].astype(BF16), dims, preferred_element_type=F32)
        if nk == 1:
            finish(part)
        else:
            k = pl.program_id(2)

            @pl.when(k == 0)
            def _():
                acc_ref[...] = part

            @pl.when(k > 0)
            def _():
                acc_ref[...] += part

            @pl.when(k == nk - 1)
            def _():
                finish(acc_ref[...])

        if comm is not None:
            @pl.when((ids[0] == grid[0] - 1) & (ids[1] == grid[1] - 1) & (ids[2] == grid[2] - 1))
            def _():
                comm.finish(c_in, c_out, c_sems)

    if form == "nn":
        a_spec = pl.BlockSpec((tm, tk), lambda i, j, k: (i, k))
        b_spec = pl.BlockSpec((tk, tn), lambda i, j, k: (k, j))
    elif form == "nt":
        a_spec = pl.BlockSpec((tm, tk), lambda i, j, k: (i, k))
        b_spec = pl.BlockSpec((tn, tk), lambda i, j, k: (j, k))
    else:
        a_spec = pl.BlockSpec((tk, tm), lambda i, j, k: (k, i))
        b_spec = pl.BlockSpec((tk, tn), lambda i, j, k: (k, j))
    o_spec = pl.BlockSpec((tm, tn), lambda i, j, k: (i, j))
    in_specs = [a_spec, b_spec] + ([o_spec] if addend is not None else [])
    args = (a, b) + ((addend,) if addend is not None else ())
    out_shape = jax.ShapeDtypeStruct((M, N), out_dtype)
    scratch = [pltpu.VMEM((tm, tn), F32)] if nk > 1 else []
    if comm is None:
        return pl.pallas_call(
            body, name=name, grid=grid, in_specs=in_specs, out_specs=o_spec, out_shape=out_shape,
            scratch_shapes=scratch, compiler_params=_cparams(("parallel", "parallel", "arbitrary")),
        )(*args)
    outs = pl.pallas_call(
        body, name=name, grid=grid, in_specs=in_specs + [ANY] * nc, out_specs=[o_spec] + [ANY] * nc,
        out_shape=[out_shape] + comm.out_shapes, scratch_shapes=scratch + comm.scratch,
        compiler_params=_cparams(("arbitrary", "arbitrary", "arbitrary")),
    )(*args, *comm.inputs)
    return outs[0], outs[1:]


def _rms_fwd(x, g, *, name, tm=512):
    M, D = x.shape
    tm = min(tm, M)

    def body(x_ref, g_ref, n_ref):
        xf = x_ref[...]
        r = lax.rsqrt(jnp.mean(xf * xf, axis=-1, keepdims=True) + EPS)
        n_ref[...] = (xf * r * g_ref[...]).astype(n_ref.dtype)

    return pl.pallas_call(
        body, name=name, grid=(M // tm,),
        in_specs=[pl.BlockSpec((tm, D), lambda i: (i, 0)), pl.BlockSpec((1, D), lambda i: (0, 0))],
        out_specs=pl.BlockSpec((tm, D), lambda i: (i, 0)),
        out_shape=jax.ShapeDtypeStruct((M, D), BF16),
        compiler_params=_cparams(("parallel",)),
    )(x, g.reshape(1, D))


def _rms_bwd(x, g, dn, dres, *, name, tm=512):
    M, D = x.shape
    tm = min(tm, M)

    def body(x_ref, g_ref, dn_ref, dres_ref, dx_ref, dg_ref):
        @pl.when(pl.program_id(0) == 0)
        def _():
            dg_ref[...] = jnp.zeros_like(dg_ref)

        xf = x_ref[...]
        r = lax.rsqrt(jnp.mean(xf * xf, axis=-1, keepdims=True) + EPS)
        xh = xf * r
        dn_ = dn_ref[...].astype(F32)
        dg_ref[...] += jnp.sum(dn_ * xh, axis=0, keepdims=True)
        dxh = dn_ * g_ref[...]
        dx = r * (dxh - xh * jnp.mean(dxh * xh, axis=-1, keepdims=True))
        dx_ref[...] = dres_ref[...] + dx

    row = pl.BlockSpec((tm, D), lambda i: (i, 0))
    vec = pl.BlockSpec((1, D), lambda i: (0, 0))
    return pl.pallas_call(
        body, name=name, grid=(M // tm,),
        in_specs=[row, vec, row, row], out_specs=[row, vec],
        out_shape=[jax.ShapeDtypeStruct((M, D), F32), jax.ShapeDtypeStruct((1, D), F32)],
        compiler_params=_cparams(("arbitrary",)),
    )(x, g.reshape(1, D), dn, dres)


def _loss_head(h, g, tgt, *, name, tm=512):
    M, D = h.shape
    tm = min(tm, M)

    def body(h_ref, g_ref, t_ref, loss_ref, dh_ref, dg_ref):
        @pl.when(pl.program_id(0) == 0)
        def _():
            dg_ref[...] = jnp.zeros_like(dg_ref)
            loss_ref[...] = jnp.zeros_like(loss_ref)

        xf = h_ref[...]
        r = lax.rsqrt(jnp.mean(xf * xf, axis=-1, keepdims=True) + EPS)
        xh = xf * r
        err = xh * g_ref[...] - t_ref[...]
        part = jnp.sum(jnp.mean(err * err, axis=-1, keepdims=True), axis=0, keepdims=True)
        loss_ref[...] += 0.5 * part
        dy = err * (1.0 / D)
        dg_ref[...] += jnp.sum(dy * xh, axis=0, keepdims=True)
        dxh = dy * g_ref[...]
        dh_ref[...] = r * (dxh - xh * jnp.mean(dxh * xh, axis=-1, keepdims=True))

    row = pl.BlockSpec((tm, D), lambda i: (i, 0))
    vec = pl.BlockSpec((1, D), lambda i: (0, 0))
    one = pl.BlockSpec((1, 1), lambda i: (0, 0))
    return pl.pallas_call(
        body, name=name, grid=(M // tm,),
        in_specs=[row, vec, row], out_specs=[one, row, vec],
        out_shape=[jax.ShapeDtypeStruct((1, 1), F32), jax.ShapeDtypeStruct((M, D), F32),
                   jax.ShapeDtypeStruct((1, D), F32)],
        compiler_params=_cparams(("arbitrary",)),
    )(h, g.reshape(1, D), tgt)


HG_MID = HG_CHUNK // 2 - 1
EXP_CAP = 80.0


def _sigmoid(x):
    return 1.0 / (1.0 + jnp.exp(-x))


def _dot(a, b, dims, precision=None):
    return lax.dot_general(a, b, dims, preferred_element_type=F32, precision=precision)


def _bdot(a, b, form):
    return _dot(a.astype(BF16), b.astype(BF16), _DIMS[form])


def _split2(x):
    hi = x.astype(BF16)
    return hi, (x - hi.astype(F32)).astype(BF16)


def _dot3(a, b, form):
    d = _DIMS[form]
    return _dot(a[0], b[0], d) + (_dot(a[0], b[1], d) + _dot(a[1], b[0], d))


def _hgrn_chunk_common(hq, hf, lbv, tril, rid):
    sq = _sigmoid(hq)
    q = hq * sq
    sg = _sigmoid(hf)
    f = lbv + (1.0 - lbv) * sg
    k = (1.0 - lbv) * (1.0 - sg)
    g = jnp.log(f)
    b = _dot(tril, g, _DIMS["nn"], precision=lax.Precision.HIGHEST)
    bref = jnp.sum(jnp.where(rid == HG_MID, b, 0.0), axis=0, keepdims=True)
    bend = jnp.sum(jnp.where(rid == HG_CHUNK - 1, b, 0.0), axis=0, keepdims=True)
    eb = jnp.exp(b)
    e1 = jnp.exp(jnp.minimum(b - bref, EXP_CAP))
    e2 = jnp.exp(jnp.minimum(bref - b, EXP_CAP))
    e3 = jnp.exp(bend - b)
    return sq, q, sg, f, k, bend, eb, e1, e2, e3


def _hgrn_fwd(proj, lb, gnorm, *, name, T=512):
    S = proj.shape[0]
    T = min(T, S)
    nch = T // HG_CHUNK
    C = HG_CHUNK

    def body(hq_ref, hf_ref, hi_ref, hg_ref, lb_ref, gn_ref, o_ref, oa_ref, st_ref, state):
        @pl.when(pl.program_id(1) == 0)
        def _():
            state[...] = jnp.zeros_like(state)

        lbv = lb_ref[...]
        gn = gn_ref[...]
        row = lax.broadcasted_iota(jnp.int32, (C, C), 0)
        col = lax.broadcasted_iota(jnp.int32, (C, C), 1)
        causal = row >= col
        tril = causal.astype(F32)
        rid = lax.broadcasted_iota(jnp.int32, (C, HG_DK), 0)
        sls = [pl.ds(c * C, C) for c in range(nch)]
        pre = [_hgrn_chunk_common(hq_ref[sl, :], hf_ref[sl, :], lbv, tril, rid) for sl in sls]
        v_l = [hi_ref[sl, :].astype(BF16) for sl in sls]
        a_l, u_l = [], []
        for c in range(nch):
            _, q, _, _, k, _, _, e1, e2, e3 = pre[c]
            a_l.append(jnp.where(causal, _bdot(q * e1, k * e2, "nt"), 0.0))
            u_l.append(_bdot(v_l[c], k * e3, "tn"))
        o_l = [_bdot(a_l[c], v_l[c], "nn") for c in range(nch)]
        st = state[...]
        st_l = []
        for c in range(nch):
            st_l.append(st)
            st = st * jnp.exp(pre[c][5]) + u_l[c]
        state[...] = st
        for c in range(nch):
            st_ref[0, c] = st_l[c]
            o_l[c] = o_l[c] + _bdot(pre[c][1] * pre[c][6], st_l[c], "nt")
        for c in range(nch):
            o, hg = o_l[c], hg_ref[sls[c], :]
            o_ref[sls[c], :] = o
            r = lax.rsqrt(jnp.mean(o * o, axis=-1, keepdims=True) + EPS)
            oa_ref[sls[c], :] = (o * r * gn * (hg * _sigmoid(hg))).astype(oa_ref.dtype)

    def grp(gidx):
        return pl.BlockSpec((T, 128), lambda h, t: (t, gidx * 8 + h))

    return pl.pallas_call(
        body, name=name, grid=(HG_HEADS, S // T),
        in_specs=[grp(0), grp(1), grp(2), grp(3),
                  pl.BlockSpec((1, 128), lambda h, t: (0, h)), pl.BlockSpec((1, 128), lambda h, t: (0, 0))],
        out_specs=[pl.BlockSpec((T, 128), lambda h, t: (t, h)), pl.BlockSpec((T, 128), lambda h, t: (t, h)),
                   pl.BlockSpec((1, nch, HG_DV, HG_DK), lambda h, t: (h, t, 0, 0))],
        out_shape=[jax.ShapeDtypeStruct((S, HG_HEADS * HG_DV), F32), jax.ShapeDtypeStruct((S, HG_HEADS * HG_DV), BF16),
                   jax.ShapeDtypeStruct((HG_HEADS, S // C, HG_DV, HG_DK), F32)],
        scratch_shapes=[pltpu.VMEM((HG_DV, HG_DK), F32)],
        compiler_params=_cparams(("parallel", "arbitrary")),
    )(proj, proj, proj, proj, lb, gnorm)


def _hgrn_bwd(proj, lb, gnorm, o, states, doa, *, name, T=512):
    S = proj.shape[0]
    T = min(T, S)
    nch = T // HG_CHUNK
    C = HG_CHUNK
    nT = S // T

    def body(hq_ref, hf_ref, hi_ref, hg_ref, lb_ref, gn_ref, o_ref, st_ref, doa_ref,
             dhq_ref, dhf_ref, dhi_ref, dhg_ref, dlb_ref, dgn_ref, dstate):
        @pl.when(pl.program_id(1) == 0)
        def _():
            dstate[...] = jnp.zeros_like(dstate)
            dlb_ref[...] = jnp.zeros_like(dlb_ref)
            dgn_ref[...] = jnp.zeros_like(dgn_ref)

        lbv = lb_ref[...]
        gn = gn_ref[...]
        row = lax.broadcasted_iota(jnp.int32, (C, C), 0)
        col = lax.broadcasted_iota(jnp.int32, (C, C), 1)
        causal = row >= col
        tril = causal.astype(F32)
        triu = (row <= col).astype(F32)
        rid = lax.broadcasted_iota(jnp.int32, (C, HG_DK), 0)
        rng = range(nch)
        sls = [pl.ds(c * C, C) for c in rng]
        pre = [_hgrn_chunk_common(hq_ref[sl, :], hf_ref[sl, :], lbv, tril, rid) for sl in sls]
        do2, dgn_acc = [], jnp.zeros((1, HG_DV), F32)
        for c in rng:
            hg, ov = hg_ref[sls[c], :], o_ref[sls[c], :]
            r = lax.rsqrt(jnp.mean(ov * ov, axis=-1, keepdims=True) + EPS)
            xh = ov * r
            sgg = _sigmoid(hg)
            d_oa = doa_ref[sls[c], :].astype(F32)
            dz = d_oa * (hg * sgg)
            dhg_ref[sls[c], :] = (d_oa * (xh * gn) * (sgg * (1.0 + hg * (1.0 - sgg)))).astype(dhg_ref.dtype)
            dgn_acc = dgn_acc + jnp.sum(dz * xh, axis=0, keepdims=True)
            dxh = dz * gn
            do2.append(_split2(r * (dxh - xh * jnp.mean(dxh * xh, axis=-1, keepdims=True))))
        dgn_ref[0] += dgn_acc
        qi = [pre[c][1] * pre[c][6] for c in rng]
        qp = [pre[c][1] * pre[c][7] for c in rng]
        kp = [pre[c][4] * pre[c][8] for c in rng]
        kend = [pre[c][4] * pre[c][9] for c in rng]
        qi2, qp2, kp2, kend2 = ([_split2(t) for t in lst] for lst in (qi, qp, kp, kend))
        v2 = [_split2(hi_ref[sl, :]) for sl in sls]
        st0 = [st_ref[0, c] for c in rng]
        a_l = [jnp.where(causal, _dot(qp2[c][0], kp2[c][0], _DIMS["nt"]), 0.0).astype(BF16) for c in rng]
        da2 = [_split2(jnp.where(causal, _dot3(do2[c], v2[c], "nt"), 0.0)) for c in rng]
        dqi = [_dot3(do2[c], _split2(st0[c]), "nn") for c in rng]
        w_l = [_dot3(do2[c], qi2[c], "tn") for c in rng]
        ds = dstate[...]
        ds1 = [None] * nch
        for c in reversed(rng):
            ds1[c] = ds
            ds = ds * jnp.exp(pre[c][5]) + w_l[c]
        dstate[...] = ds
        ds12 = [_split2(t) for t in ds1]
        dqp = [_dot3(da2[c], kp2[c], "nn") for c in rng]
        dkp = [_dot3(da2[c], qp2[c], "tn") for c in rng]
        dv = [_dot(a_l[c], do2[c][0], _DIMS["tn"]) + _dot(kend2[c][0], ds12[c][0], _DIMS["nt"]) for c in rng]
        dkend = [_dot3(v2[c], ds12[c], "nn") for c in rng]
        dq_l, dk_l, db_l = [], [], []
        for c in rng:
            _, _, _, _, _, bend, eb, e1, e2, e3 = pre[c]
            dq_l.append(dqi[c] * eb + dqp[c] * e1)
            dk_l.append(dkp[c] * e2 + dkend[c] * e3)
            db = dqi[c] * qi[c] + dqp[c] * qp[c] - dkp[c] * kp[c] - dkend[c] * kend[c]
            dbend = (jnp.sum(dkend[c] * kend[c], axis=0, keepdims=True)
                     + jnp.exp(bend) * jnp.sum(ds1[c] * st0[c], axis=0, keepdims=True))
            db_l.append(db + jnp.where(rid == C - 1, dbend, 0.0))
        dg = [_dot(triu, db_l[c], _DIMS["nn"], precision=lax.Precision.HIGHEST) for c in rng]
        dlb_acc = jnp.zeros((1, HG_DK), F32)
        for c in rng:
            sq, _, sg, f, _, _, _, _, _, _ = pre[c]
            hq = hq_ref[sls[c], :]
            df = dg[c] / f - dk_l[c]
            dlb_acc = dlb_acc + jnp.sum(df * (1.0 - sg), axis=0, keepdims=True)
            dhf_ref[sls[c], :] = (df * (1.0 - lbv) * sg * (1.0 - sg)).astype(dhf_ref.dtype)
            dhq_ref[sls[c], :] = (dq_l[c] * (sq * (1.0 + hq * (1.0 - sq)))).astype(dhq_ref.dtype)
            dhi_ref[sls[c], :] = dv[c].astype(dhi_ref.dtype)
        dlb_ref[...] += dlb_acc

    def grp(gidx):
        return pl.BlockSpec((T, 128), lambda h, t: (nT - 1 - t, gidx * 8 + h))

    tok = pl.BlockSpec((T, 128), lambda h, t: (nT - 1 - t, h))
    big = jax.ShapeDtypeStruct((S, HG_HEADS * HG_DV), BF16)
    return pl.pallas_call(
        body, name=name, grid=(HG_HEADS, nT),
        in_specs=[grp(0), grp(1), grp(2), grp(3),
                  pl.BlockSpec((1, 128), lambda h, t: (0, h)), pl.BlockSpec((1, 128), lambda h, t: (0, 0)),
                  tok, pl.BlockSpec((1, nch, HG_DV, HG_DK), lambda h, t: (h, nT - 1 - t, 0, 0)), tok],
        out_specs=[tok, tok, tok, tok, pl.BlockSpec((1, 128), lambda h, t: (0, h)),
                   pl.BlockSpec((1, 1, 128), lambda h, t: (h, 0, 0))],
        out_shape=[big, big, big, big, jax.ShapeDtypeStruct((1, HG_HEADS * HG_DK), F32),
                   jax.ShapeDtypeStruct((HG_HEADS, 1, HG_DV), F32)],
        scratch_shapes=[pltpu.VMEM((HG_DV, HG_DK), F32)],
        compiler_params=_cparams(("parallel", "arbitrary")),
    )(proj, proj, proj, proj, lb, gnorm, o, states, doa)


def _lb_fwd(logits, *, name):
    def body(l_ref, lb_ref):
        lb_ref[...] = _sigmoid(l_ref[0:1, :] - l_ref[1:2, :])

    return pl.pallas_call(body, name=name, out_shape=jax.ShapeDtypeStruct((1, logits.shape[1]), F32))(logits)


def _lb_bwd(logits, dlb, *, name):
    def body(l_ref, d_ref, o_ref):
        lbv = _sigmoid(l_ref[0:1, :] - l_ref[1:2, :])
        t = d_ref[...] * lbv * (1.0 - lbv)
        o_ref[0:1, :] = t
        o_ref[1:2, :] = -t

    return pl.pallas_call(body, name=name, out_shape=jax.ShapeDtypeStruct(logits.shape, F32))(logits, dlb)


NEG = -1e30
FOX_SCALE = FOX_DH ** -0.5
FOX_PAIRS = FOX_HEADS // 2


def _fox_gate_fwd(ff, bias, *, name, T=512):
    S = ff.shape[0]
    T = min(T, S)

    def body(ff_ref, b_ref, c_ref, carry):
        @pl.when(pl.program_id(0) == 0)
        def _():
            carry[...] = jnp.zeros_like(carry)

        z = ff_ref[...] + b_ref[...]
        logf = jnp.minimum(z, 0.0) - jnp.log(1.0 + jnp.exp(-jnp.abs(z)))
        row = lax.broadcasted_iota(jnp.int32, (T, T), 0)
        col = lax.broadcasted_iota(jnp.int32, (T, T), 1)
        c = _dot((row >= col).astype(F32), logf, _DIMS["nn"], precision=lax.Precision.HIGHEST) + carry[...]
        c_ref[...] = c
        carry[...] = c[T - 1:T, :]

    return pl.pallas_call(
        body, name=name, grid=(S // T,),
        in_specs=[pl.BlockSpec((T, 128), lambda i: (i, 0)), pl.BlockSpec((1, 128), lambda i: (0, 0))],
        out_specs=pl.BlockSpec((T, 128), lambda i: (i, 0)),
        out_shape=jax.ShapeDtypeStruct((S, 128), F32),
        scratch_shapes=[pltpu.VMEM((1, 128), F32)],
        compiler_params=_cparams(("arbitrary",)),
    )(ff, bias)


def _fox_gate_bwd(ff, bias, dcs, *, name, T=512):
    S = ff.shape[0]
    T = min(T, S)
    nT = S // T

    def body(ff_ref, b_ref, d_ref, dff_ref, db_ref, carry):
        @pl.when(pl.program_id(0) == 0)
        def _():
            carry[...] = jnp.zeros_like(carry)
            db_ref[...] = jnp.zeros_like(db_ref)

        row = lax.broadcasted_iota(jnp.int32, (T, T), 0)
        col = lax.broadcasted_iota(jnp.int32, (T, T), 1)
        dlogf = carry[...] - _dot((row <= col).astype(F32), d_ref[...], _DIMS["nn"], precision=lax.Precision.HIGHEST)
        carry[...] = dlogf[0:1, :]
        dff = dlogf * (1.0 - _sigmoid(ff_ref[...] + b_ref[...]))
        dff_ref[...] = dff.astype(dff_ref.dtype)
        db_ref[...] += jnp.sum(dff, axis=0, keepdims=True)

    rev = pl.BlockSpec((T, 128), lambda i: (nT - 1 - i, 0))
    vec = pl.BlockSpec((1, 128), lambda i: (0, 0))
    return pl.pallas_call(
        body, name=name, grid=(nT,),
        in_specs=[rev, vec, rev], out_specs=[rev, vec],
        out_shape=[jax.ShapeDtypeStruct((S, 128), BF16), jax.ShapeDtypeStruct((1, 128), F32)],
        scratch_shapes=[pltpu.VMEM((1, 128), F32)],
        compiler_params=_cparams(("arbitrary",)),
    )(ff, bias, dcs)


AUG = FOX_DH


def _split3(x):
    a = x.astype(BF16).astype(F32)
    r = x - a
    b = r.astype(BF16).astype(F32)
    return a, b, r - b


def _lane_fill(lane, base, pieces, start):
    for i, pc in enumerate(pieces):
        base = jnp.where(lane == start + i, pc, base)
    return base


FOX_TB = 512
FOX_SKIP = 40.0
N_STAT = 4


def _fox_prep(proj, c_tok, *, name):
    S = proj.shape[0]
    T = min(FOX_TB, S)

    def body(q_ref, k_ref, v_ref, c_ref, qa_ref, ka_ref, va_ref, st_ref):
        pair = pl.program_id(0)
        lane = lax.broadcasted_iota(jnp.int32, (T, 128), 1)
        lane1 = lax.broadcasted_iota(jnp.int32, (1, 128), 1)
        c = c_ref[...]
        ones3 = jnp.where((lane >= AUG) & (lane < AUG + 3), 1.0, 0.0)

        def max_norm(t):
            tr = jnp.where(lane < AUG, t.astype(BF16).astype(F32), 0.0)
            return jnp.sqrt(jnp.max(jnp.sum(tr * tr, axis=-1, keepdims=True), axis=0, keepdims=True))

        for hh in range(2):
            ch = jnp.sum(jnp.where(lane == 2 * pair + hh, c, 0.0), axis=-1, keepdims=True)
            c1, c2, c3 = _split3(ch)
            q, k, v = q_ref[...], k_ref[...], v_ref[...]
            if hh == 1:
                q, k, v = (pltpu.roll(t, 64, 1) for t in (q, k, v))
            aug_q = _lane_fill(lane, jnp.where((lane >= AUG + 3) & (lane < AUG + 6), 1.0, 0.0), (c1, c2, c3), AUG)
            aug_k = _lane_fill(lane, ones3, (-c1, -c2, -c3), AUG + 3)
            qa_ref[hh] = jnp.where(lane < AUG, q * FOX_SCALE, aug_q).astype(BF16)
            ka_ref[hh] = jnp.where(lane < AUG, k, aug_k).astype(BF16)
            va_ref[hh] = jnp.where(lane < AUG, v, ones3).astype(BF16)
            stats = (max_norm(q * FOX_SCALE), jnp.max(ch, axis=0, keepdims=True), max_norm(k),
                     jnp.min(ch, axis=0, keepdims=True))
            st_ref[hh, 0] = _lane_fill(lane1, jnp.zeros((1, 128), F32), stats, 0)

    def grp(g):
        return pl.BlockSpec((T, 128), lambda p, t: (t, g * 8 + p))

    hm = pl.BlockSpec((2, T, 128), lambda p, t: (p, t, 0))
    out = jax.ShapeDtypeStruct((FOX_HEADS, S, 128), BF16)
    return pl.pallas_call(
        body, name=name, grid=(FOX_PAIRS, S // T),
        in_specs=[grp(4), grp(5), grp(6), pl.BlockSpec((T, 128), lambda p, t: (t, 0))],
        out_specs=[hm, hm, hm, pl.BlockSpec((2, 1, 1, 128), lambda p, t: (p, t, 0, 0))],
        out_shape=[out, out, out, jax.ShapeDtypeStruct((FOX_HEADS, S // T, 1, 128), F32)],
        compiler_params=_cparams(("parallel", "parallel")),
    )(proj, proj, proj, c_tok)


def _fox_bound(st_ref, head, nb, qi, ki):
    qb_, kb_ = (head * nb + qi) * N_STAT, (head * nb + ki) * N_STAT
    return st_ref[qb_] * st_ref[kb_ + 2] + st_ref[qb_ + 1] - st_ref[kb_ + 3] + 0.01


def _pair_lanes(lane, a0, a1):
    return jnp.where(lane < AUG, a0, pltpu.roll(a1, 64, 1))


def _first_live_key(st_ref, head, nb, qi, newest, thr):
    def body(t, k0):
        k = newest - t
        return jnp.where(_fox_bound(st_ref, head, nb, qi, k) > thr, k, k0)

    return lax.fori_loop(0, newest + 1, body, newest + 1)


def _last_live_query(st_ref, lm_ref, head, nb, ki):
    def body(t, i1):
        i = ki + 1 + t
        live = _fox_bound(st_ref, head, nb, i, ki) > lm_ref[head * nb + i] - FOX_SKIP
        return jnp.where(live, i, i1)

    return lax.fori_loop(0, nb - 1 - ki, body, ki)


class _BlockStream:
    def __init__(self, hbm_refs, bufs, sems, pair, tb):
        self.hbm, self.bufs, self.sems, self.pair, self.tb = hbm_refs, bufs, sems, pair, tb

    def _copies(self, blk, slot):
        rows = pl.ds(pl.multiple_of(blk * self.tb, self.tb), self.tb)
        return [pltpu.make_async_copy(h.at[pl.ds(2 * self.pair, 2), rows, :], b.at[slot], self.sems.at[n, slot])
                for n, (h, b) in enumerate(zip(self.hbm, self.bufs))]

    def start(self, blk, slot):
        for cp in self._copies(blk, slot):
            cp.start()

    def wait(self, blk, slot):
        for cp in self._copies(blk, slot):
            cp.wait()


def _fox_fwd(qa, ka, va, bounds, *, name):
    S = qa.shape[1]
    tb = min(FOX_TB, S)
    nb = S // tb

    def body(qa_ref, ka_hbm, va_hbm, st_ref, o_ref, qb_ref, lse_ref, kbuf, vbuf, sems, m_s, acc_s, m_min):
        pair, qi = pl.program_id(0), pl.program_id(1)
        stream = _BlockStream((ka_hbm, va_hbm), (kbuf, vbuf), sems, pair, tb)

        def head_step(hh, slot, masked):
            s = _dot(qa_ref[hh], kbuf[slot, hh], _DIMS["nt"])
            if masked:
                row = lax.broadcasted_iota(jnp.int32, (tb, tb), 0)
                col = lax.broadcasted_iota(jnp.int32, (tb, tb), 1)
                s = jnp.where(col <= row, s, NEG)
            m_old = m_s[hh]
            m_new = jnp.maximum(m_old, jnp.max(s, axis=-1, keepdims=True))
            p = jnp.exp(s - m_new)
            p_hi = p.astype(BF16)
            p_lo = (p - p_hi.astype(F32)).astype(BF16)
            vv = vbuf[slot, hh]
            acc_s[hh] = (jnp.exp(m_old - m_new) * acc_s[hh]
                         + _dot(p_hi, vv, _DIMS["nn"]) + _dot(p_lo, vv, _DIMS["nn"]))
            m_s[hh] = m_new
            m_min[hh] = jnp.min(m_new)

        stream.start(qi, 0)

        @pl.when(qi > 0)
        def _():
            stream.start(qi - 1, 1)

        m_s[...] = jnp.full_like(m_s, NEG)
        acc_s[...] = jnp.zeros_like(acc_s)
        stream.wait(qi, 0)
        for hh in range(2):
            head_step(hh, 0, True)

        @pl.when(qi > 1)
        def _():
            stream.start(qi - 2, 0)

        @pl.when(qi > 0)
        def _():
            stream.wait(qi - 1, 1)
            for hh in range(2):
                head_step(hh, 1, False)

        k0 = [_first_live_key(st_ref, 2 * pair + hh, nb, qi, qi - 2, m_min[hh] - FOX_SKIP) for hh in range(2)]
        n = qi - 1 - jnp.minimum(k0[0], k0[1])

        @pl.when((qi > 1) & (n == 0))
        def _():
            stream.wait(qi - 2, 0)

        def loop(t, carry):
            k = qi - 2 - t
            slot = t % 2
            stream.wait(k, slot)

            @pl.when(t + 1 < n)
            def _():
                stream.start(k - 1, 1 - slot)

            for hh in range(2):
                @pl.when(k >= k0[hh])
                def _():
                    head_step(hh, slot, False)
            return carry

        lax.fori_loop(0, n, loop, 0)
        lane = lax.broadcasted_iota(jnp.int32, (tb, 128), 1)
        outs = []
        for hh in range(2):
            acc = acc_s[hh]
            l = acc[:, AUG:AUG + 1]
            outs.append(acc / l)
            lse = m_s[hh] + jnp.log(l)
            lse_ref[hh, 0] = jnp.broadcast_to(jnp.min(lse, axis=0, keepdims=True), (1, 128))
            qf = qa_ref[hh].astype(F32)
            cb = qf[:, AUG:AUG + 1] + qf[:, AUG + 1:AUG + 2] + qf[:, AUG + 2:AUG + 3] - lse
            qb_ref[hh] = _lane_fill(lane, qf, _split3(cb), AUG).astype(BF16)
        o_ref[...] = _pair_lanes(lane, outs[0], outs[1])

    qs = pl.BlockSpec((2, tb, 128), lambda p, i: (p, i, 0))
    return pl.pallas_call(
        body, name=name, grid=(FOX_PAIRS, nb),
        in_specs=[qs, ANY, ANY, SMEM],
        out_specs=[pl.BlockSpec((tb, 128), lambda p, i: (i, p)), qs,
                   pl.BlockSpec((2, 1, 1, 128), lambda p, i: (p, i, 0, 0))],
        out_shape=[jax.ShapeDtypeStruct((S, FOX_HEADS * FOX_DH), F32), jax.ShapeDtypeStruct((FOX_HEADS, S, 128), BF16),
                   jax.ShapeDtypeStruct((FOX_HEADS, nb, 1, 128), F32)],
        scratch_shapes=[pltpu.VMEM((2, 2, tb, 128), BF16), pltpu.VMEM((2, 2, tb, 128), BF16),
                        pltpu.SemaphoreType.DMA((2, 2)), pltpu.VMEM((2, tb, 1), F32), pltpu.VMEM((2, tb, 128), F32),
                        pltpu.SMEM((2,), F32)],
        compiler_params=_cparams(("parallel", "arbitrary")),
    )(qa, ka, va, bounds)


def _fox_bwd_prep(o, do, *, name, T=512):
    S = o.shape[0]
    T = min(T, S)

    def body(o_ref, do_ref, dob_ref):
        lane = lax.broadcasted_iota(jnp.int32, (T, 128), 1)
        d = do_ref[...].astype(F32)
        prod = d * o_ref[...]
        for hh in range(2):
            mine = (lane < AUG) if hh == 0 else (lane >= AUG)
            delta = jnp.sum(jnp.where(mine, prod, 0.0), axis=-1, keepdims=True)
            dh = d if hh == 0 else pltpu.roll(d, 64, 1)
            dob_ref[hh] = _lane_fill(lane, jnp.where(lane < AUG, dh, 0.0), _split3(-delta), AUG).astype(BF16)

    tok = pl.BlockSpec((T, 128), lambda p, t: (t, p))
    return pl.pallas_call(
        body, name=name, grid=(FOX_PAIRS, S // T),
        in_specs=[tok, tok], out_specs=pl.BlockSpec((2, T, 128), lambda p, t: (p, t, 0)),
        out_shape=jax.ShapeDtypeStruct((FOX_HEADS, S, 128), BF16),
        compiler_params=_cparams(("parallel", "parallel")),
    )(o, do)


def _fox_bwd_dq(qb, ka, va, dob, bounds, lse_min, *, name, comm=None):
    S = qb.shape[1]
    tb = min(FOX_TB, S)
    nb = S // tb
    nc = comm.n if comm is not None else 0

    def body(qb_ref, dob_ref, ka_hbm, va_hbm, st_ref, lm_ref, *rest):
        c_in, (dq_ref, dcs_ref), c_out = rest[:nc], rest[nc:nc + 2], rest[nc + 2:2 * nc + 2]
        kbuf, vbuf, sems, acc_s = rest[2 * nc + 2:2 * nc + 6]
        c_sems = rest[2 * nc + 6:]
        pair, qi = pl.program_id(0), pl.program_id(1)
        if comm is not None:
            @pl.when((pair == 0) & (qi == 0))
            def _():
                comm.start(c_in, c_out, c_sems)

        stream = _BlockStream((ka_hbm, va_hbm), (kbuf, vbuf), sems, pair, tb)
        k0 = [_first_live_key(st_ref, 2 * pair + hh, nb, qi, qi - 1, lm_ref[(2 * pair + hh) * nb + qi] - FOX_SKIP)
              for hh in range(2)]
        n = qi - jnp.minimum(k0[0], k0[1]) + 1
        stream.start(qi, 0)
        acc_s[...] = jnp.zeros_like(acc_s)
        dcs_ref[...] = jnp.zeros_like(dcs_ref)

        def head_step(hh, slot, k, masked):
            s = _dot(qb_ref[hh], kbuf[slot, hh], _DIMS["nt"])
            if masked:
                row = lax.broadcasted_iota(jnp.int32, (tb, tb), 0)
                col = lax.broadcasted_iota(jnp.int32, (tb, tb), 1)
                s = jnp.where(col <= row, s, NEG)
            ds = jnp.exp(s) * _dot(dob_ref[hh], vbuf[slot, hh], _DIMS["nt"])
            dcs_ref[0, 0, hh:hh + 1, pl.ds(pl.multiple_of(k * tb, tb), tb)] = jnp.sum(ds, axis=0, keepdims=True)
            acc_s[hh] += _dot(ds.astype(BF16), kbuf[slot, hh], _DIMS["nn"])

        def loop(t, carry):
            k = qi - t
            slot = t % 2
            stream.wait(k, slot)

            @pl.when(t + 1 < n)
            def _():
                stream.start(k - 1, 1 - slot)

            @pl.when(t == 0)
            def _():
                for hh in range(2):
                    head_step(hh, slot, k, True)

            for hh in range(2):
                @pl.when((t > 0) & (k >= k0[hh]))
                def _():
                    head_step(hh, slot, k, False)
            return carry

        lax.fori_loop(0, n, loop, 0)
        lane = lax.broadcasted_iota(jnp.int32, (tb, 128), 1)
        dq_ref[...] = (_pair_lanes(lane, acc_s[0], acc_s[1]) * FOX_SCALE).astype(dq_ref.dtype)
        if comm is not None:
            @pl.when((pair == FOX_PAIRS - 1) & (qi == nb - 1))
            def _():
                comm.finish(c_in, c_out, c_sems)

    qs = pl.BlockSpec((2, tb, 128), lambda p, i: (p, i, 0))
    outs = pl.pallas_call(
        body, name=name, grid=(FOX_PAIRS, nb),
        in_specs=[qs, qs, ANY, ANY, SMEM, SMEM] + [ANY] * nc,
        out_specs=[pl.BlockSpec((tb, 128), lambda p, i: (i, p)),
                   pl.BlockSpec((1, 1, 2, S), lambda p, i: (p, i, 0, 0))] + [ANY] * nc,
        out_shape=[jax.ShapeDtypeStruct((S, FOX_HEADS * FOX_DH), BF16),
                   jax.ShapeDtypeStruct((FOX_PAIRS, nb, 2, S), F32)] + (comm.out_shapes if comm is not None else []),
        scratch_shapes=[pltpu.VMEM((2, 2, tb, 128), BF16), pltpu.VMEM((2, 2, tb, 128), BF16),
                        pltpu.SemaphoreType.DMA((2, 2)), pltpu.VMEM((2, tb, 128), F32)]
        + (comm.scratch if comm is not None else []),
        compiler_params=_cparams(("parallel", "arbitrary") if comm is None else ("arbitrary", "arbitrary")),
    )(qb, dob, ka, va, bounds, lse_min, *(comm.inputs if comm is not None else []))
    return (outs[0], outs[1]) if comm is None else (outs[0], outs[1], outs[2:])


def _fox_bwd_dkv(qb, ka, va, dob, bounds, lse_min, *, name):
    S = qb.shape[1]
    tb = min(FOX_TB, S)
    nb = S // tb

    def body(ka_ref, va_ref, qb_hbm, dob_hbm, st_ref, lm_ref, dk_ref, dv_ref, qbuf, dbuf, sems, dk_s, dv_s):
        pair, ki = pl.program_id(0), pl.program_id(1)
        stream = _BlockStream((qb_hbm, dob_hbm), (qbuf, dbuf), sems, pair, tb)
        i1 = [_last_live_query(st_ref, lm_ref, 2 * pair + hh, nb, ki) for hh in range(2)]
        n = jnp.maximum(i1[0], i1[1]) - ki + 1
        stream.start(ki, 0)
        dk_s[...] = jnp.zeros_like(dk_s)
        dv_s[...] = jnp.zeros_like(dv_s)

        def head_step(hh, slot, masked):
            st = _dot(ka_ref[hh], qbuf[slot, hh], _DIMS["nt"])
            if masked:
                row = lax.broadcasted_iota(jnp.int32, (tb, tb), 0)
                col = lax.broadcasted_iota(jnp.int32, (tb, tb), 1)
                st = jnp.where(row <= col, st, NEG)
            pt = jnp.exp(st)
            dst = pt * _dot(va_ref[hh], dbuf[slot, hh], _DIMS["nt"])
            dv_s[hh] += _dot(pt.astype(BF16), dbuf[slot, hh], _DIMS["nn"])
            dk_s[hh] += _dot(dst.astype(BF16), qbuf[slot, hh], _DIMS["nn"])

        def loop(t, carry):
            i = ki + t
            slot = t % 2
            stream.wait(i, slot)

            @pl.when(t + 1 < n)
            def _():
                stream.start(i + 1, 1 - slot)

            @pl.when(t == 0)
            def _():
                for hh in range(2):
                    head_step(hh, slot, True)

            for hh in range(2):
                @pl.when((t > 0) & (i <= i1[hh]))
                def _():
                    head_step(hh, slot, False)
            return carry

        lax.fori_loop(0, n, loop, 0)
        lane = lax.broadcasted_iota(jnp.int32, (tb, 128), 1)
        dk_ref[...] = _pair_lanes(lane, dk_s[0], dk_s[1]).astype(dk_ref.dtype)
        dv_ref[...] = _pair_lanes(lane, dv_s[0], dv_s[1]).astype(dv_ref.dtype)

    ks = pl.BlockSpec((2, tb, 128), lambda p, j: (p, j, 0))
    tok = pl.BlockSpec((tb, 128), lambda p, j: (j, p))
    big = jax.ShapeDtypeStruct((S, FOX_HEADS * FOX_DH), BF16)
    return pl.pallas_call(
        body, name=name, grid=(FOX_PAIRS, nb),
        in_specs=[ks, ks, ANY, ANY, SMEM, SMEM], out_specs=[tok, tok], out_shape=[big, big],
        scratch_shapes=[pltpu.VMEM((2, 2, tb, 128), BF16), pltpu.VMEM((2, 2, tb, 128), BF16),
                        pltpu.SemaphoreType.DMA((2, 2)), pltpu.VMEM((2, tb, 128), F32), pltpu.VMEM((2, tb, 128), F32)],
        compiler_params=_cparams(("parallel", "arbitrary")),
    )(ka, va, qb, dob, bounds, lse_min)


def _merge_fwd(proj, pa, pb, *, name, T=512):
    S, D = pa.shape
    T = min(T, S)

    def body(ga_ref, gb_ref, pa_ref, pb_ref, m_ref):
        m_ref[...] = (_sigmoid(ga_ref[...]) * pa_ref[...] + _sigmoid(gb_ref[...]) * pb_ref[...]).astype(m_ref.dtype)

    tok = pl.BlockSpec((T, D), lambda i: (i, 0))
    return pl.pallas_call(
        body, name=name, grid=(S // T,),
        in_specs=[pl.BlockSpec((T, D), lambda i: (i, 7)), pl.BlockSpec((T, D), lambda i: (i, 8)), tok, tok],
        out_specs=tok, out_shape=jax.ShapeDtypeStruct((S, D), BF16),
        compiler_params=_cparams(("parallel",)),
    )(proj, proj, pa, pb)


def _merge_bwd(proj, pa, pb, dm, *, name, T=512):
    S, D = pa.shape
    T = min(T, S)

    def body(ga_ref, gb_ref, pa_ref, pb_ref, dm_ref, dpa_ref, dpb_ref, dga_ref, dgb_ref):
        dm_ = dm_ref[...]
        sa, sb = _sigmoid(ga_ref[...]), _sigmoid(gb_ref[...])
        dpa_ref[...] = (dm_ * sa).astype(BF16)
        dpb_ref[...] = (dm_ * sb).astype(BF16)
        dga_ref[...] = (dm_ * pa_ref[...] * sa * (1.0 - sa)).astype(BF16)
        dgb_ref[...] = (dm_ * pb_ref[...] * sb * (1.0 - sb)).astype(BF16)

    tok = pl.BlockSpec((T, D), lambda i: (i, 0))
    big = jax.ShapeDtypeStruct((S, D), BF16)
    return pl.pallas_call(
        body, name=name, grid=(S // T,),
        in_specs=[pl.BlockSpec((T, D), lambda i: (i, 7)), pl.BlockSpec((T, D), lambda i: (i, 8)), tok, tok, tok],
        out_specs=[tok, tok, tok, tok], out_shape=[big, big, big, big],
        compiler_params=_cparams(("parallel",)),
    )(proj, proj, pa, pb, dm)


INV_SQRT2 = 0.7071067811865476
INV_SQRT2PI = 0.3989422804014327


def _shifted(u, prev, rid):
    m1 = jnp.where(rid == 0, prev[7:8, :], pltpu.roll(u, 1, 0))
    m2 = jnp.where(rid == 0, prev[6:7, :], jnp.where(rid == 1, prev[7:8, :], pltpu.roll(u, 2, 0)))
    return m1, m2


def _conv_acc(u, prev, w_ref, b_ref, rid):
    m1, m2 = _shifted(u, prev, rid)
    return b_ref[...] + w_ref[0:1, :] * m2 + w_ref[1:2, :] * m1 + w_ref[2:3, :] * u, m1, m2


def _convglu_fwd(ug, uv, wg, wv, bg, bv, *, name, T=512, tc=256):
    S, F = ug.shape
    T = min(T, S)

    def body(ug_ref, uv_ref, wg_ref, wv_ref, bg_ref, bv_ref, a_ref, pg, pv):
        @pl.when(pl.program_id(1) == 0)
        def _():
            pg[...] = jnp.zeros_like(pg)
            pv[...] = jnp.zeros_like(pv)

        rid = lax.broadcasted_iota(jnp.int32, (T, tc), 0)
        g_, v_ = ug_ref[...], uv_ref[...]
        accg, _, _ = _conv_acc(g_, pg[...], wg_ref, bg_ref, rid)
        accv, _, _ = _conv_acc(v_, pv[...], wv_ref, bv_ref, rid)
        gel = 0.5 * accg * (1.0 + lax.erf(accg * INV_SQRT2))
        a_ref[...] = (gel * accv).astype(a_ref.dtype)
        pg[...] = g_[T - 8:T, :]
        pv[...] = v_[T - 8:T, :]

    tok = pl.BlockSpec((T, tc), lambda j, t: (t, j))
    w3 = pl.BlockSpec((3, tc), lambda j, t: (0, j))
    b1 = pl.BlockSpec((1, tc), lambda j, t: (0, j))
    return pl.pallas_call(
        body, name=name, grid=(F // tc, S // T),
        in_specs=[tok, tok, w3, w3, b1, b1], out_specs=tok,
        out_shape=jax.ShapeDtypeStruct((S, F), BF16),
        scratch_shapes=[pltpu.VMEM((8, tc), F32), pltpu.VMEM((8, tc), F32)],
        compiler_params=_cparams(("parallel", "arbitrary")),
    )(ug, uv, wg, wv, bg, bv)


def _convglu_bwd(ug, uv, wg, wv, bg, bv, da, *, name, T=512, tc=256):
    S, F = ug.shape
    T = min(T, S)
    nT = S // T
    halo_blocks = T // 8

    def up_shift(d, nx, rid):
        p1 = jnp.where(rid == T - 1, nx[0:1, :], pltpu.roll(d, T - 1, 0))
        p2 = jnp.where(rid == T - 1, nx[1:2, :], jnp.where(rid == T - 2, nx[0:1, :], pltpu.roll(d, T - 2, 0)))
        return p1, p2

    def body(ug_ref, uv_ref, hg_ref, hv_ref, wg_ref, wv_ref, bg_ref, bv_ref, da_ref,
             dug_ref, duv_ref, dwg_ref, dwv_ref, dbg_ref, dbv_ref, ng, nv):
        @pl.when(pl.program_id(1) == 0)
        def _():
            ng[...] = jnp.zeros_like(ng)
            nv[...] = jnp.zeros_like(nv)
            for r in (dwg_ref, dwv_ref, dbg_ref, dbv_ref):
                r[...] = jnp.zeros_like(r)

        first_block = pl.program_id(1) == nT - 1
        rid = lax.broadcasted_iota(jnp.int32, (T, tc), 0)
        g_, v_ = ug_ref[...], uv_ref[...]
        pg = jnp.where(first_block, 0.0, hg_ref[...])
        pv = jnp.where(first_block, 0.0, hv_ref[...])
        accg, g1, g2 = _conv_acc(g_, pg, wg_ref, bg_ref, rid)
        accv, v1, v2 = _conv_acc(v_, pv, wv_ref, bv_ref, rid)
        cdf = 0.5 * (1.0 + lax.erf(accg * INV_SQRT2))
        pdf = INV_SQRT2PI * jnp.exp(-0.5 * accg * accg)
        da_ = da_ref[...].astype(F32)
        dgate = da_ * accv * (cdf + accg * pdf)
        dval = da_ * (accg * cdf)
        dbg_ref[...] += jnp.sum(dgate, axis=0, keepdims=True)
        dbv_ref[...] += jnp.sum(dval, axis=0, keepdims=True)
        for j, (sg_, sv_) in enumerate(((g2, v2), (g1, v1), (g_, v_))):
            dwg_ref[j:j + 1, :] += jnp.sum(dgate * sg_, axis=0, keepdims=True)
            dwv_ref[j:j + 1, :] += jnp.sum(dval * sv_, axis=0, keepdims=True)
        for d, w_ref, nx, out_ref in ((dgate, wg_ref, ng, dug_ref), (dval, wv_ref, nv, duv_ref)):
            p1, p2 = up_shift(d, nx[...], rid)
            out_ref[...] = (w_ref[2:3, :] * d + w_ref[1:2, :] * p1 + w_ref[0:1, :] * p2).astype(out_ref.dtype)
            nx[...] = d[0:8, :]

    tok = pl.BlockSpec((T, tc), lambda j, t: (nT - 1 - t, j))
    halo = pl.BlockSpec((8, tc), lambda j, t: (jnp.maximum((nT - 1 - t) * halo_blocks - 1, 0), j))
    w3 = pl.BlockSpec((3, tc), lambda j, t: (0, j))
    b1 = pl.BlockSpec((1, tc), lambda j, t: (0, j))
    big = jax.ShapeDtypeStruct((S, F), BF16)
    return pl.pallas_call(
        body, name=name, grid=(F // tc, nT),
        in_specs=[tok, tok, halo, halo, w3, w3, b1, b1, tok], out_specs=[tok, tok, w3, w3, b1, b1],
        out_shape=[big, big, jax.ShapeDtypeStruct((3, F), F32), jax.ShapeDtypeStruct((3, F), F32),
                   jax.ShapeDtypeStruct((1, F), F32), jax.ShapeDtypeStruct((1, F), F32)],
        scratch_shapes=[pltpu.VMEM((8, tc), F32), pltpu.VMEM((8, tc), F32)],
        compiler_params=_cparams(("parallel", "arbitrary")),
    )(ug, uv, ug, uv, wg, wv, bg, bv, da)


FF_LO, FF_HI = 7168, 7184


def _col_blocks(a, width):
    return jnp.stack([a[:, d * width:(d + 1) * width] for d in range(N_DEV)])


def _late_weights(g_a, g_b, g_o, g_up, g_cw, g_d):
    wup = jnp.concatenate([g_up[d] for d in range(N_DEV)], axis=1)
    cw = jnp.concatenate([g_cw[d] for d in range(N_DEV)], axis=1)
    return dict(wa=g_a.reshape(D_MODEL, D_MODEL), wb=g_b.reshape(D_MODEL, D_MODEL), wo=g_o.reshape(D_MODEL, D_MODEL),
                wug=wup[:, :D_FF], wuv=wup[:, D_FF:], cwg=cw[:, :D_FF], cwv=cw[:, D_FF:], wd=g_d.reshape(D_FF, D_MODEL))


def _early_grad_blocks(d_wa, d_wb, d_wo, d_wug, d_wuv, d_wd):
    up = jnp.stack([d_wug[:, d * 704:(d + 1) * 704] for d in range(4)]
                   + [d_wuv[:, d * 704:(d + 1) * 704] for d in range(4)])
    return [d_wa.reshape(N_DEV, 128, D_MODEL), d_wb.reshape(N_DEV, 128, D_MODEL), d_wo.reshape(N_DEV, 128, D_MODEL),
            up, d_wd.reshape(N_DEV, 352, D_MODEL)]


def _local_step(x, tgt, w, p, late=None, exchange=False):
    S = x.shape[0]
    mm = _matmul
    n1 = _rms_fwd(x, p["norm_mix"], name="rms1_fwd")
    if late is None:
        proj = mm(n1, w["wm"], "nn", name="proj_main")
    else:
        proj, gathered = mm(n1, w["wm"], "nn", comm=late, name="proj_main")
        w = {**w, **_late_weights(*gathered)}
    ff = mm(n1, w["wff"], "nn", name="proj_ff")
    lb = _lb_fwd(p["hg_lb_logits"], name="lb_fwd")
    gnorm = p["hg_norm"].reshape(1, HG_DV)
    o_hg, oa, states = _hgrn_fwd(proj, lb, gnorm, name="hgrn_fwd")
    bias = jnp.pad(p["fox_f_bias"].reshape(1, FOX_HEADS), ((0, 0), (0, 128 - FOX_HEADS)))
    c = _fox_gate_fwd(ff, bias, name="fox_gate_fwd")
    qa, ka, va, fox_stats = _fox_prep(proj, c, name="fox_prep")
    bounds = fox_stats[:, :, 0, :N_STAT].reshape(-1)
    ob, qb, lse_stats = _fox_fwd(qa, ka, va, bounds, name="fox_fwd")
    lse_min = lse_stats[:, :, 0, 0].reshape(-1)
    pa = mm(oa, w["wa"], "nn", name="branch_a")
    pb = mm(ob, w["wb"], "nn", name="branch_b")
    merged = _merge_fwd(proj, pa, pb, name="merge_fwd")
    h1 = mm(merged, w["wo"], "nn", addend=x, name="mix_out")
    n2 = _rms_fwd(h1, p["norm_ffn"], name="rms2_fwd")
    ug = mm(n2, w["wug"], "nn", name="up_gate")
    uv = mm(n2, w["wuv"], "nn", name="up_val")
    a = _convglu_fwd(ug, uv, w["cwg"], w["cwv"], p["cbg"], p["cbv"], name="convglu_fwd")
    h2 = mm(a, w["wd"], "nn", addend=h1, name="ffn_down")
    loss, dh2, d_norm_final = _loss_head(h2, p["norm_final"], tgt, name="loss_head")
    da = mm(dh2, w["wd"], "nt", out_dtype=BF16, name="d_act")
    d_wd = mm(a, dh2, "tn", out_dtype=BF16, name="dw_down")
    dug, duv, d_cwg, d_cwv, d_cbg, d_cbv = _convglu_bwd(
        ug, uv, w["cwg"], w["cwv"], p["cbg"], p["cbv"], da, name="convglu_bwd")
    dn2 = mm(dug, w["wug"], "nt", name="dn2_gate")
    dn2 = mm(duv, w["wuv"], "nt", addend=dn2, name="dn2_val")
    d_wug = mm(n2, dug, "tn", out_dtype=BF16, name="dw_up_gate")
    d_wuv = mm(n2, duv, "tn", out_dtype=BF16, name="dw_up_val")
    dh1, d_norm_ffn = _rms_bwd(h1, p["norm_ffn"], dn2, dh2, name="rms2_bwd")
    dmerged = mm(dh1, w["wo"], "nt", name="d_merged")
    d_wo = mm(merged, dh1, "tn", out_dtype=BF16, name="dw_out")
    dpa, dpb, dga, dgb = _merge_bwd(proj, pa, pb, dmerged, name="merge_bwd")
    doa = mm(dpa, w["wa"], "nt", name="d_oa")
    dob = mm(dpb, w["wb"], "nt", out_dtype=BF16, name="d_ob")
    d_wa = mm(oa, dpa, "tn", out_dtype=BF16, name="dw_branch_a")
    d_wb = mm(ob, dpb, "tn", out_dtype=BF16, name="dw_branch_b")
    dhq, dhf, dhi, dhg, dlb, dgn8 = _hgrn_bwd(proj, lb, gnorm, o_hg, states, doa, name="hgrn_bwd")
    d_logits = _lb_bwd(p["hg_lb_logits"], dlb, name="lb_bwd")
    dob_hm = _fox_bwd_prep(ob, dob, name="fox_bwd_prep")
    early_parts = None
    if exchange:
        comm = _ExchangeComm(_early_grad_blocks(d_wa, d_wb, d_wo, d_wug, d_wuv, d_wd))
        dq, dcsp, early_parts = _fox_bwd_dq(qb, ka, va, dob_hm, bounds, lse_min, comm=comm, name="fox_bwd_dq")
    else:
        dq, dcsp = _fox_bwd_dq(qb, ka, va, dob_hm, bounds, lse_min, name="fox_bwd_dq")
    dk, dv = _fox_bwd_dkv(qb, ka, va, dob_hm, bounds, lse_min, name="fox_bwd_dkv")
    dcs = jnp.sum(dcsp, axis=1)
    dcs_tok = jnp.pad(dcs.reshape(FOX_HEADS, S).T, ((0, 0), (0, 128 - FOX_HEADS)))
    dff, dbias = _fox_gate_bwd(ff, bias, dcs_tok, name="fox_gate_bwd")
    dproj = jnp.concatenate([dhq, dhf, dhi, dhg, dq, dk, dv, dga, dgb], axis=1)
    d_wm = mm(n1, dproj, "tn", out_dtype=BF16, name="dw_in_main")
    d_wff = mm(n1, dff, "tn", out_dtype=BF16, name="dw_in_ff")
    dn1 = mm(dff, w["wff"], "nt", name="dn1_ff")
    late_parts = None
    if exchange:
        d_win = jnp.concatenate([d_wm[:, :FF_LO], d_wff[:, :FOX_HEADS], d_wm[:, FF_LO:]], axis=1)
        d_cw = jnp.concatenate([d_cwg, d_cwv], axis=1)
        comm = _ExchangeComm([_col_blocks(d_win, 1154), _col_blocks(d_cw, 704)])
        dn1, late_parts = mm(dproj, w["wm"], "nt", addend=dn1, comm=comm, name="dn1_main")
    else:
        dn1 = mm(dproj, w["wm"], "nt", addend=dn1, name="dn1_main")
    dx, d_norm_mix = _rms_bwd(x, p["norm_mix"], dn1, dh1, name="rms1_bwd")
    grads = dict(
        wm=d_wm, wff=d_wff, wa=d_wa, wb=d_wb, wo=d_wo, wug=d_wug, wuv=d_wuv, cwg=d_cwg, cwv=d_cwv, wd=d_wd,
        norm_mix=d_norm_mix.reshape(-1), fox_f_bias=dbias[0, :FOX_HEADS], hg_lb_logits=d_logits,
        hg_norm=jnp.sum(dgn8, axis=0).reshape(-1), norm_ffn=d_norm_ffn.reshape(-1), cbg=d_cbg, cbv=d_cbv,
        norm_final=d_norm_final.reshape(-1), early_parts=early_parts, late_parts=late_parts)
    return loss, dx, grads


SMALL = [("norm_mix", (1, D_MODEL)), ("fox_f_bias", (1, FOX_HEADS)), ("hg_lb_logits", (2, HG_HEADS * HG_DK)),
         ("hg_norm", (1, HG_DV)), ("norm_ffn", (1, D_MODEL)), ("conv_b", (1, 2 * D_FF)), ("norm_final", (D_MODEL,))]
SMALL_ROWS = 88
SHARDED = [("w_in", (D_MODEL, 1154), 256), ("w_branch_a", (128, D_MODEL), 128), ("w_branch_b", (128, D_MODEL), 128),
           ("w_out", (128, D_MODEL), 128), ("w_up", (D_MODEL, 704), 256), ("conv_w", (3, 704), 3),
           ("w_down", (352, D_MODEL), 352)]
NAMES = ["norm_mix", "w_in", "fox_f_bias", "hg_lb_logits", "hg_norm", "w_branch_a", "w_branch_b", "w_out",
         "norm_ffn", "w_up", "conv_w", "conv_b", "w_down", "norm_final"]


def _size(shape):
    n = 1
    for s in shape:
        n *= s
    return n


def _adamw(parts, w, m, v, *, name, T):
    R, C = w.shape
    c1 = 1.0 / (1.0 - ADAM_B1 ** ADAM_STEP)
    c2 = 1.0 / (1.0 - ADAM_B2 ** ADAM_STEP)

    def body(p_ref, w_ref, m_ref, v_ref, g_ref, d_ref, nm_ref, nv_ref):
        g = p_ref[0].astype(F32)
        for s in range(1, N_DEV):
            g = g + p_ref[s].astype(F32)
        g_ref[...] = g
        nm = ADAM_B1 * m_ref[...] + (1.0 - ADAM_B1) * g
        nv = ADAM_B2 * v_ref[...] + (1.0 - ADAM_B2) * (g * g)
        nm_ref[...] = nm
        nv_ref[...] = nv
        d_ref[...] = -ADAM_LR * ((nm * c1) / (jnp.sqrt(nv * c2) + ADAM_EPS) + ADAM_WD * w_ref[...])

    blk = pl.BlockSpec((T, C), lambda i: (i, 0))
    out = jax.ShapeDtypeStruct((R, C), F32)
    return pl.pallas_call(
        body, name=name, grid=(R // T,),
        in_specs=[pl.BlockSpec((N_DEV, T, C), lambda i: (0, i, 0)), blk, blk, blk],
        out_specs=[blk, blk, blk, blk], out_shape=[out, out, out, out],
        compiler_params=_cparams(("parallel",)),
    )(parts, w, m, v)


def _pack_small(vals):
    flat = jnp.concatenate([vals[n].reshape(-1).astype(F32) for n, _ in SMALL])
    return jnp.pad(flat, (0, SMALL_ROWS * 128 - flat.shape[0])).reshape(SMALL_ROWS, 128)


def _unpack_small(buf):
    flat, out, off = buf.reshape(-1), {}, 0
    for n, shape in SMALL:
        out[n] = flat[off:off + _size(shape)].reshape(shape)
        off += _size(shape)
    return out


def kernel(x, norm_mix, w_in, fox_f_bias, hg_lb_logits, hg_norm, w_branch_a, w_branch_b, w_out, norm_ffn, w_up, conv_w, conv_b, w_down, norm_final, loss_target, m_norm_mix, m_w_in, m_fox_f_bias, m_hg_lb_logits, m_hg_norm, m_w_branch_a, m_w_branch_b, m_w_out, m_norm_ffn, m_w_up, m_conv_w, m_conv_b, m_w_down, m_norm_final, v_norm_mix, v_w_in, v_fox_f_bias, v_hg_lb_logits, v_hg_norm, v_w_branch_a, v_w_branch_b, v_w_out, v_norm_ffn, v_w_up, v_conv_w, v_conv_b, v_w_down, v_norm_final):
    wv = dict(norm_mix=norm_mix, w_in=w_in, fox_f_bias=fox_f_bias, hg_lb_logits=hg_lb_logits, hg_norm=hg_norm,
              w_branch_a=w_branch_a, w_branch_b=w_branch_b, w_out=w_out, norm_ffn=norm_ffn, w_up=w_up, conv_w=conv_w,
              conv_b=conv_b, w_down=w_down, norm_final=norm_final)
    mv = dict(norm_mix=m_norm_mix, w_in=m_w_in, fox_f_bias=m_fox_f_bias, hg_lb_logits=m_hg_lb_logits, hg_norm=m_hg_norm,
              w_branch_a=m_w_branch_a, w_branch_b=m_w_branch_b, w_out=m_w_out, norm_ffn=m_norm_ffn, w_up=m_w_up,
              conv_w=m_conv_w, conv_b=m_conv_b, w_down=m_w_down, norm_final=m_norm_final)
    vv = dict(norm_mix=v_norm_mix, w_in=v_w_in, fox_f_bias=v_fox_f_bias, hg_lb_logits=v_hg_lb_logits, hg_norm=v_hg_norm,
              w_branch_a=v_w_branch_a, w_branch_b=v_w_branch_b, w_out=v_w_out, norm_ffn=v_norm_ffn, w_up=v_w_up,
              conv_w=v_conv_w, conv_b=v_conv_b, w_down=v_w_down, norm_final=v_norm_final)

    (g_in,) = _comm_call(_GatherComm([w_in[0].astype(BF16)]), name="gather_w_in")
    win = jnp.concatenate([g_in[d] for d in range(N_DEV)], axis=1)
    w = dict(wm=jnp.concatenate([win[:, :FF_LO], win[:, FF_HI:]], axis=1),
             wff=jnp.pad(win[:, FF_LO:FF_HI], ((0, 0), (0, 128 - FOX_HEADS))))
    late = _GatherComm([w_branch_a[0].astype(BF16), w_branch_b[0].astype(BF16), w_out[0].astype(BF16),
                        w_up[0].astype(BF16), conv_w[0], w_down[0].astype(BF16)])
    p = dict(norm_mix=norm_mix[0], fox_f_bias=fox_f_bias[0], hg_lb_logits=hg_lb_logits, hg_norm=hg_norm[0],
             norm_ffn=norm_ffn[0], cbg=conv_b[:, :D_FF], cbv=conv_b[:, D_FF:], norm_final=norm_final)
    loss, dx, grads = _local_step(x[0], loss_target[0], w, p, late=late, exchange=True)
    loss = lax.psum(loss[0, 0], ("x", "y", "c"))

    small = _pack_small(dict(
        norm_mix=grads["norm_mix"], fox_f_bias=grads["fox_f_bias"], hg_lb_logits=grads["hg_lb_logits"],
        hg_norm=grads["hg_norm"], norm_ffn=grads["norm_ffn"], conv_b=jnp.concatenate([grads["cbg"], grads["cbv"]], axis=1),
        norm_final=grads["norm_final"]))
    (small_parts,) = _comm_call(_ExchangeComm([jnp.broadcast_to(small[None], (N_DEV, SMALL_ROWS, 128))]),
                                name="exchange_small")
    ea, eb, eo, eup, ed = grads["early_parts"]
    p_in, p_cw = grads["late_parts"]
    parts = [p_in, ea, eb, eo, eup, p_cw, ed, small_parts]
    res = {}
    for (n, shape, tile), part in zip(SHARDED, parts):
        outs = _adamw(part, wv[n].reshape(shape), mv[n].reshape(shape), vv[n].reshape(shape), name="adamw_" + n, T=tile)
        res[n] = [o.reshape(wv[n].shape) for o in outs]
    outs = _adamw(parts[-1], _pack_small(wv), _pack_small(mv), _pack_small(vv), name="adamw_small", T=SMALL_ROWS)
    small = [_unpack_small(o) for o in outs]
    for n, _ in SMALL:
        res[n] = [s[n] for s in small]
    return (loss, dx[None], *[res[n][0] for n in NAMES], *[res[n][1] for n in NAMES],
            *[res[n][2] for n in NAMES], *[res[n][3] for n in NAMES])
```

```python
import jax
import jax.numpy as jnp
from jax import lax
from jax.experimental import pallas as pl
from jax.experimental.pallas import tpu as pltpu

F32 = jnp.float32
BF16 = jnp.bfloat16

D_MODEL = 1024
HG_HEADS = 8
HG_DK = 128
HG_DV = 128
HG_CHUNK = 64
FOX_HEADS = 16
FOX_DH = 64
D_FF = 2816
EPS = 1e-6
N_DEV = 8

ADAM_LR = 0.001
ADAM_B1 = 0.9
ADAM_B2 = 0.999
ADAM_EPS = 1e-08
ADAM_WD = 0.01
ADAM_STEP = 10

VMEM_LIMIT = 56 * 1024 * 1024


def _cparams(sem):
    return pltpu.CompilerParams(dimension_semantics=sem, vmem_limit_bytes=VMEM_LIMIT)


MESH = pl.DeviceIdType.MESH
ANY = pl.BlockSpec(memory_space=pl.ANY)
SMEM = pl.BlockSpec(memory_space=pltpu.SMEM)


class _GatherComm:
    def __init__(self, shards):
        self.inputs = list(shards)
        n = self.n = len(shards)
        self.out_shapes = [jax.ShapeDtypeStruct((N_DEV,) + s.shape, s.dtype) for s in shards]
        self.scratch = [pltpu.SemaphoreType.DMA((n, 7)), pltpu.SemaphoreType.DMA((n, 7)), pltpu.SemaphoreType.DMA((n,))]

    def _parts(self, x_refs, out_refs, sems):
        send_sems, recv_sems, local_sems = sems
        x, y, c = lax.axis_index("x"), lax.axis_index("y"), lax.axis_index("c")
        me, sibling = (x, y, c), (x, y, 1 - c)
        chips = [(1 - x, y), (x, 1 - y), (1 - x, 1 - y)]

        def copy(t, k, block, to, src=None):
            slot = out_refs[t].at[4 * block[0] + 2 * block[1] + block[2]]
            return pltpu.make_async_remote_copy(
                src_ref=slot if src is None else src, dst_ref=slot,
                send_sem=send_sems.at[t, k], recv_sem=recv_sems.at[t, k], device_id=to, device_id_type=MESH)

        mine = [pltpu.make_async_copy(x_refs[t], out_refs[t].at[4 * x + 2 * y + c], local_sems.at[t])
                for t in range(self.n)]
        first = []
        for t in range(self.n):
            first.append(copy(t, 0, me, sibling, src=x_refs[t]))
            first += [copy(t, 1 + j, me, (*chip, c), src=x_refs[t]) for j, chip in enumerate(chips)]
        return c, me, sibling, chips, copy, mine, first

    def start(self, x_refs, out_refs, sems):
        _, _, _, _, _, mine, first = self._parts(x_refs, out_refs, sems)
        for cp in mine + first:
            cp.start()

    def finish(self, x_refs, out_refs, sems):
        c, me, sibling, chips, copy, mine, first = self._parts(x_refs, out_refs, sems)
        passed = []
        for j, chip in enumerate(chips):
            for t in range(self.n):
                copy(t, 1 + j, (*chip, c), me).wait_recv()
                passed.append(copy(t, 4 + j, (*chip, c), sibling))
                passed[-1].start()
        for t in range(self.n):
            copy(t, 0, sibling, me).wait_recv()
            for j, chip in enumerate(chips):
                copy(t, 4 + j, (*chip, 1 - c), me).wait_recv()
        for cp in first + passed:
            cp.wait_send()
        for cp in mine:
            cp.wait()


class _ExchangeComm:
    def __init__(self, blocks):
        self.inputs = list(blocks)
        n = self.n = len(blocks)
        self.out_shapes = [jax.ShapeDtypeStruct(b.shape, b.dtype) for b in blocks]
        self.scratch = [pltpu.SemaphoreType.DMA((n, 7)), pltpu.SemaphoreType.DMA((n, 7)), pltpu.SemaphoreType.DMA((n,))]

    def _parts(self, g_refs, out_refs, sems):
        send_sems, recv_sems, local_sems = sems
        x, y, c = lax.axis_index("x"), lax.axis_index("y"), lax.axis_index("c")
        me = 4 * x + 2 * y + c
        mine = [pltpu.make_async_copy(g_refs[t].at[me], out_refs[t].at[me], local_sems.at[t]) for t in range(self.n)]
        sends, recvs = [], []
        for k in range(1, N_DEV):
            px = 1 - x if k & 4 else x
            py = 1 - y if k & 2 else y
            pc = 1 - c if k & 1 else c
            p = 4 * px + 2 * py + pc
            for t in range(self.n):
                sends.append(pltpu.make_async_remote_copy(
                    src_ref=g_refs[t].at[p], dst_ref=out_refs[t].at[me], send_sem=send_sems.at[t, k - 1],
                    recv_sem=recv_sems.at[t, k - 1], device_id=(px, py, pc), device_id_type=MESH))
                recvs.append(pltpu.make_async_remote_copy(
                    src_ref=g_refs[t].at[p], dst_ref=out_refs[t].at[p], send_sem=send_sems.at[t, k - 1],
                    recv_sem=recv_sems.at[t, k - 1], device_id=(px, py, pc), device_id_type=MESH))
        return mine, sends, recvs

    def start(self, g_refs, out_refs, sems):
        mine, sends, _ = self._parts(g_refs, out_refs, sems)
        for cp in mine + sends:
            cp.start()

    def finish(self, g_refs, out_refs, sems):
        mine, sends, recvs = self._parts(g_refs, out_refs, sems)
        for cp in recvs:
            cp.wait_recv()
        for cp in sends:
            cp.wait_send()
        for cp in mine:
            cp.wait()


def _comm_call(comm, *, name):
    n = comm.n

    def body(*refs):
        comm.start(refs[:n], refs[n:2 * n], refs[2 * n:])
        comm.finish(refs[:n], refs[n:2 * n], refs[2 * n:])

    return pl.pallas_call(body, name=name, in_specs=[ANY] * n, out_specs=[ANY] * n, out_shape=comm.out_shapes,
                          scratch_shapes=comm.scratch)(*comm.inputs)


_DIMS = {
    "nn": (((1,), (0,)), ((), ())),
    "nt": (((1,), (1,)), ((), ())),
    "tn": (((0,), (0,)), ((), ())),
}

MATMUL_VMEM_BUDGET = 36 * 1024 * 1024
MAX_TILE = 1536


def _pick(n, prefs):
    for p in prefs:
        if n % p == 0:
            return p
    return n


def _tile_options(n):
    return [d for d in range(128, min(n, MAX_TILE) + 1, 128) if n % d == 0] or [n]


def _pick_tiles(M, N, tk, nk, sa, sb, so, has_addend, tm, tn):
    best = None
    for cm in ([tm] if tm else _tile_options(M)):
        for cn in ([tn] if tn else _tile_options(N)):
            need = 2 * (cm * tk * sa + tk * cn * sb + cm * cn * so + (cm * cn * 4 if has_addend else 0))
            need += cm * cn * 4 if nk > 1 else 0
            if need <= MATMUL_VMEM_BUDGET and (best is None or cm * cn > best[0] * best[1]
                                               or (cm * cn == best[0] * best[1] and cn > best[1])):
                best = (cm, cn)
    assert best is not None, (M, N, tk)
    return best


def _matmul(a, b, form, *, out_dtype=F32, addend=None, tm=None, tn=None, tk=None, comm=None, name):
    if form == "nn":
        (M, K), (K2, N) = a.shape, b.shape
    elif form == "nt":
        (M, K), (N, K2) = a.shape, b.shape
    else:
        (K, M), (K2, N) = a.shape, b.shape
    assert K == K2, (a.shape, b.shape, form)
    tk = tk or (K if K <= 2816 else _pick(K, (1024, 512, 256, 128)))
    nk = K // tk
    if tm is None or tn is None:
        tm, tn = _pick_tiles(M, N, tk, nk, a.dtype.itemsize, b.dtype.itemsize, jnp.dtype(out_dtype).itemsize,
                             addend is not None, tm, tn)
    assert M % tm == 0 and N % tn == 0 and K % tk == 0, (M, N, K, tm, tn, tk)
    dims = _DIMS[form]
    nc = comm.n if comm is not None else 0
    grid = (M // tm, N // tn, nk)

    def body(*refs):
        a_ref, b_ref = refs[:2]
        pos = 2
        add_ref = refs[pos] if addend is not None else None
        pos += addend is not None
        c_in, o_ref, c_out = refs[pos:pos + nc], refs[pos + nc], refs[pos + nc + 1:pos + 2 * nc + 1]
        pos += 2 * nc + 1
        acc_ref = refs[pos] if nk > 1 else None
        c_sems = refs[pos + (nk > 1):]
        if comm is not None:
            ids = [pl.program_id(d) for d in range(3)]

            @pl.when((ids[0] == 0) & (ids[1] == 0) & (ids[2] == 0))
            def _():
                comm.start(c_in, c_out, c_sems)

        def finish(r):
            if add_ref is not None:
                r = r + add_ref[...].astype(F32)
            o_ref[...] = r.astype(o_ref.dtype)

        part = lax.dot_general(a_ref[...].astype(BF16), b_ref[...].astype(BF16), dims, preferred_element_type=F32)
        if nk == 1:
            finish(part)
        else:
            k = pl.program_id(2)

            @pl.when(k == 0)
            def _():
                acc_ref[...] = part

            @pl.when(k > 0)
            def _():
                acc_ref[...] += part

            @pl.when(k == nk - 1)
            def _():
                finish(acc_ref[...])

        if comm is not None:
            @pl.when((ids[0] == grid[0] - 1) & (ids[1] == grid[1] - 1) & (ids[2] == grid[2] - 1))
            def _():
                comm.finish(c_in, c_out, c_sems)

    if form == "nn":
        a_spec = pl.BlockSpec((tm, tk), lambda i, j, k: (i, k))
        b_spec = pl.BlockSpec((tk, tn), lambda i, j, k: (k, j))
    elif form == "nt":
        a_spec = pl.BlockSpec((tm, tk), lambda i, j, k: (i, k))
        b_spec = pl.BlockSpec((tn, tk), lambda i, j, k: (j, k))
    else:
        a_spec = pl.BlockSpec((tk, tm), lambda i, j, k: (k, i))
        b_spec = pl.BlockSpec((tk, tn), lambda i, j, k: (k, j))
    o_spec = pl.BlockSpec((tm, tn), lambda i, j, k: (i, j))
    in_specs = [a_spec, b_spec] + ([o_spec] if addend is not None else [])
    args = (a, b) + ((addend,) if addend is not None else ())
    out_shape = jax.ShapeDtypeStruct((M, N), out_dtype)
    scratch = [pltpu.VMEM((tm, tn), F32)] if nk > 1 else []
    if comm is None:
        return pl.pallas_call(
            body, name=name, grid=grid, in_specs=in_specs, out_specs=o_spec, out_shape=out_shape,
            scratch_shapes=scratch, compiler_params=_cparams(("parallel", "parallel", "arbitrary")),
        )(*args)
    outs = pl.pallas_call(
        body, name=name, grid=grid, in_specs=in_specs + [ANY] * nc, out_specs=[o_spec] + [ANY] * nc,
        out_shape=[out_shape] + comm.out_shapes, scratch_shapes=scratch + comm.scratch,
        compiler_params=_cparams(("arbitrary", "arbitrary", "arbitrary")),
    )(*args, *comm.inputs)
    return outs[0], outs[1:]


def _rms_fwd(x, g, *, name, tm=512):
    M, D = x.shape
    tm = min(tm, M)

    def body(x_ref, g_ref, n_ref):
        xf = x_ref[...]
        r = lax.rsqrt(jnp.mean(xf * xf, axis=-1, keepdims=True) + EPS)
        n_ref[...] = (xf * r * g_ref[...]).astype(n_ref.dtype)

    return pl.pallas_call(
        body, name=name, grid=(M // tm,),
        in_specs=[pl.BlockSpec((tm, D), lambda i: (i, 0)), pl.BlockSpec((1, D), lambda i: (0, 0))],
        out_specs=pl.BlockSpec((tm, D), lambda i: (i, 0)),
        out_shape=jax.ShapeDtypeStruct((M, D), BF16),
        compiler_params=_cparams(("parallel",)),
    )(x, g.reshape(1, D))


def _rms_bwd(x, g, dn, dres, *, name, tm=512):
    M, D = x.shape
    tm = min(tm, M)

    def body(x_ref, g_ref, dn_ref, dres_ref, dx_ref, dg_ref):
        @pl.when(pl.program_id(0) == 0)
        def _():
            dg_ref[...] = jnp.zeros_like(dg_ref)

        xf = x_ref[...]
        r = lax.rsqrt(jnp.mean(xf * xf, axis=-1, keepdims=True) + EPS)
        xh = xf * r
        dn_ = dn_ref[...].astype(F32)
        dg_ref[...] += jnp.sum(dn_ * xh, axis=0, keepdims=True)
        dxh = dn_ * g_ref[...]
        dx = r * (dxh - xh * jnp.mean(dxh * xh, axis=-1, keepdims=True))
        dx_ref[...] = dres_ref[...] + dx

    row = pl.BlockSpec((tm, D), lambda i: (i, 0))
    vec = pl.BlockSpec((1, D), lambda i: (0, 0))
    return pl.pallas_call(
        body, name=name, grid=(M // tm,),
        in_specs=[row, vec, row, row], out_specs=[row, vec],
        out_shape=[jax.ShapeDtypeStruct((M, D), F32), jax.ShapeDtypeStruct((1, D), F32)],
        compiler_params=_cparams(("arbitrary",)),
    )(x, g.reshape(1, D), dn, dres)


def _loss_head(h, g, tgt, *, name, tm=512):
    M, D = h.shape
    tm = min(tm, M)

    def body(h_ref, g_ref, t_ref, loss_ref, dh_ref, dg_ref):
        @pl.when(pl.program_id(0) == 0)
        def _():
            dg_ref[...] = jnp.zeros_like(dg_ref)
            loss_ref[...] = jnp.zeros_like(loss_ref)

        xf = h_ref[...]
        r = lax.rsqrt(jnp.mean(xf * xf, axis=-1, keepdims=True) + EPS)
        xh = xf * r
        err = xh * g_ref[...] - t_ref[...]
        part = jnp.sum(jnp.mean(err * err, axis=-1, keepdims=True), axis=0, keepdims=True)
        loss_ref[...] += 0.5 * part
        dy = err * (1.0 / D)
        dg_ref[...] += jnp.sum(dy * xh, axis=0, keepdims=True)
        dxh = dy * g_ref[...]
        dh_ref[...] = r * (dxh - xh * jnp.mean(dxh * xh, axis=-1, keepdims=True))

    row = pl.BlockSpec((tm, D), lambda i: (i, 0))
    vec = pl.BlockSpec((1, D), lambda i: (0, 0))
    one = pl.BlockSpec((1, 1), lambda i: (0, 0))
    return pl.pallas_call(
        body, name=name, grid=(M // tm,),
        in_specs=[row, vec, row], out_specs=[one, row, vec],
        out_shape=[jax.ShapeDtypeStruct((1, 1), F32), jax.ShapeDtypeStruct((M, D), F32),
                   jax.ShapeDtypeStruct((1, D), F32)],
        compiler_params=_cparams(("arbitrary",)),
    )(h, g.reshape(1, D), tgt)


HG_MID = HG_CHUNK // 2 - 1
EXP_CAP = 80.0


def _sigmoid(x):
    return 1.0 / (1.0 + jnp.exp(-x))


def _dot(a, b, dims, precision=None):
    return lax.dot_general(a, b, dims, preferred_element_type=F32, precision=precision)


def _bdot(a, b, form):
    return _dot(a.astype(BF16), b.astype(BF16), _DIMS[form])


def _split2(x):
    hi = x.astype(BF16)
    return hi, (x - hi.astype(F32)).astype(BF16)


def _dot3(a, b, form):
    d = _DIMS[form]
    return _dot(a[0], b[0], d) + (_dot(a[0], b[1], d) + _dot(a[1], b[0], d))


def _hgrn_chunk_common(hq, hf, lbv, tril, rid):
    sq = _sigmoid(hq)
    q = hq * sq
    sg = _sigmoid(hf)
    f = lbv + (1.0 - lbv) * sg
    k = (1.0 - lbv) * (1.0 - sg)
    g = jnp.log(f)
    b = _dot(tril, g, _DIMS["nn"], precision=lax.Precision.HIGHEST)
    bref = jnp.sum(jnp.where(rid == HG_MID, b, 0.0), axis=0, keepdims=True)
    bend = jnp.sum(jnp.where(rid == HG_CHUNK - 1, b, 0.0), axis=0, keepdims=True)
    eb = jnp.exp(b)
    e1 = jnp.exp(jnp.minimum(b - bref, EXP_CAP))
    e2 = jnp.exp(jnp.minimum(bref - b, EXP_CAP))
    e3 = jnp.exp(bend - b)
    return sq, q, sg, f, k, bend, eb, e1, e2, e3


def _hgrn_fwd(proj, lb, gnorm, *, name, T=1024):
    S = proj.shape[0]
    T = min(T, S)
    nch = T // HG_CHUNK
    C = HG_CHUNK

    def body(hq_ref, hf_ref, hi_ref, hg_ref, lb_ref, gn_ref, o_ref, oa_ref, st_ref, state):
        @pl.when(pl.program_id(1) == 0)
        def _():
            state[...] = jnp.zeros_like(state)

        lbv = lb_ref[...]
        gn = gn_ref[...]
        row = lax.broadcasted_iota(jnp.int32, (C, C), 0)
        col = lax.broadcasted_iota(jnp.int32, (C, C), 1)
        causal = row >= col
        tril = causal.astype(F32)
        rid = lax.broadcasted_iota(jnp.int32, (C, HG_DK), 0)
        sls = [pl.ds(c * C, C) for c in range(nch)]
        pre = [_hgrn_chunk_common(hq_ref[sl, :], hf_ref[sl, :], lbv, tril, rid) for sl in sls]
        v_l = [hi_ref[sl, :].astype(BF16) for sl in sls]
        a_l, u_l = [], []
        for c in range(nch):
            _, q, _, _, k, _, _, e1, e2, e3 = pre[c]
            a_l.append(jnp.where(causal, _bdot(q * e1, k * e2, "nt"), 0.0))
            u_l.append(_bdot(v_l[c], k * e3, "tn"))
        o_l = [_bdot(a_l[c], v_l[c], "nn") for c in range(nch)]
        st = state[...]
        st_l = []
        for c in range(nch):
            st_l.append(st)
            st = st * jnp.exp(pre[c][5]) + u_l[c]
        state[...] = st
        for c in range(nch):
            st_ref[0, c] = st_l[c]
            o_l[c] = o_l[c] + _bdot(pre[c][1] * pre[c][6], st_l[c], "nt")
        for c in range(nch):
            o, hg = o_l[c], hg_ref[sls[c], :]
            o_ref[sls[c], :] = o
            r = lax.rsqrt(jnp.mean(o * o, axis=-1, keepdims=True) + EPS)
            oa_ref[sls[c], :] = (o * r * gn * (hg * _sigmoid(hg))).astype(oa_ref.dtype)

    def grp(gidx):
        return pl.BlockSpec((T, 128), lambda h, t: (t, gidx * 8 + h))

    return pl.pallas_call(
        body, name=name, grid=(HG_HEADS, S // T),
        in_specs=[grp(0), grp(1), grp(2), grp(3),
                  pl.BlockSpec((1, 128), lambda h, t: (0, h)), pl.BlockSpec((1, 128), lambda h, t: (0, 0))],
        out_specs=[pl.BlockSpec((T, 128), lambda h, t: (t, h)), pl.BlockSpec((T, 128), lambda h, t: (t, h)),
                   pl.BlockSpec((1, nch, HG_DV, HG_DK), lambda h, t: (h, t, 0, 0))],
        out_shape=[jax.ShapeDtypeStruct((S, HG_HEADS * HG_DV), F32), jax.ShapeDtypeStruct((S, HG_HEADS * HG_DV), BF16),
                   jax.ShapeDtypeStruct((HG_HEADS, S // C, HG_DV, HG_DK), F32)],
        scratch_shapes=[pltpu.VMEM((HG_DV, HG_DK), F32)],
        compiler_params=_cparams(("parallel", "arbitrary")),
    )(proj, proj, proj, proj, lb, gnorm)


def _hgrn_bwd(proj, lb, gnorm, o, states, doa, *, name, T=1024):
    S = proj.shape[0]
    T = min(T, S)
    nch = T // HG_CHUNK
    C = HG_CHUNK
    nT = S // T

    def body(hq_ref, hf_ref, hi_ref, hg_ref, lb_ref, gn_ref, o_ref, st_ref, doa_ref,
             dhq_ref, dhf_ref, dhi_ref, dhg_ref, dlb_ref, dgn_ref, dstate):
        @pl.when(pl.program_id(1) == 0)
        def _():
            dstate[...] = jnp.zeros_like(dstate)
            dlb_ref[...] = jnp.zeros_like(dlb_ref)
            dgn_ref[...] = jnp.zeros_like(dgn_ref)

        lbv = lb_ref[...]
        gn = gn_ref[...]
        row = lax.broadcasted_iota(jnp.int32, (C, C), 0)
        col = lax.broadcasted_iota(jnp.int32, (C, C), 1)
        causal = row >= col
        tril = causal.astype(F32)
        triu = (row <= col).astype(F32)
        rid = lax.broadcasted_iota(jnp.int32, (C, HG_DK), 0)
        rng = range(nch)
        sls = [pl.ds(c * C, C) for c in rng]
        pre = [_hgrn_chunk_common(hq_ref[sl, :], hf_ref[sl, :], lbv, tril, rid) for sl in sls]
        do2, dgn_acc = [], jnp.zeros((1, HG_DV), F32)
        for c in rng:
            hg, ov = hg_ref[sls[c], :], o_ref[sls[c], :]
            r = lax.rsqrt(jnp.mean(ov * ov, axis=-1, keepdims=True) + EPS)
            xh = ov * r
            sgg = _sigmoid(hg)
            d_oa = doa_ref[sls[c], :].astype(F32)
            dz = d_oa * (hg * sgg)
            dhg_ref[sls[c], :] = (d_oa * (xh * gn) * (sgg * (1.0 + hg * (1.0 - sgg)))).astype(dhg_ref.dtype)
            dgn_acc = dgn_acc + jnp.sum(dz * xh, axis=0, keepdims=True)
            dxh = dz * gn
            do2.append(_split2(r * (dxh - xh * jnp.mean(dxh * xh, axis=-1, keepdims=True))))
        dgn_ref[0] += dgn_acc
        qi = [pre[c][1] * pre[c][6] for c in rng]
        qp = [pre[c][1] * pre[c][7] for c in rng]
        kp = [pre[c][4] * pre[c][8] for c in rng]
        kend = [pre[c][4] * pre[c][9] for c in rng]
        qi2, qp2, kp2, kend2 = ([_split2(t) for t in lst] for lst in (qi, qp, kp, kend))
        v2 = [_split2(hi_ref[sl, :]) for sl in sls]
        st0 = [st_ref[0, c] for c in rng]
        a_l = [jnp.where(causal, _dot(qp2[c][0], kp2[c][0], _DIMS["nt"]), 0.0).astype(BF16) for c in rng]
        da2 = [_split2(jnp.where(causal, _dot3(do2[c], v2[c], "nt"), 0.0)) for c in rng]
        dqi = [_dot3(do2[c], _split2(st0[c]), "nn") for c in rng]
        w_l = [_dot3(do2[c], qi2[c], "tn") for c in rng]
        ds = dstate[...]
        ds1 = [None] * nch
        for c in reversed(rng):
            ds1[c] = ds
            ds = ds * jnp.exp(pre[c][5]) + w_l[c]
        dstate[...] = ds
        ds12 = [_split2(t) for t in ds1]
        dqp = [_dot3(da2[c], kp2[c], "nn") for c in rng]
        dkp = [_dot3(da2[c], qp2[c], "tn") for c in rng]
        dv = [_dot(a_l[c], do2[c][0], _DIMS["tn"]) + _dot(kend2[c][0], ds12[c][0], _DIMS["nt"]) for c in rng]
        dkend = [_dot3(v2[c], ds12[c], "nn") for c in rng]
        dq_l, dk_l, db_l = [], [], []
        for c in rng:
            _, _, _, _, _, bend, eb, e1, e2, e3 = pre[c]
            dq_l.append(dqi[c] * eb + dqp[c] * e1)
            dk_l.append(dkp[c] * e2 + dkend[c] * e3)
            db = dqi[c] * qi[c] + dqp[c] * qp[c] - dkp[c] * kp[c] - dkend[c] * kend[c]
            dbend = (jnp.sum(dkend[c] * kend[c], axis=0, keepdims=True)
                     + jnp.exp(bend) * jnp.sum(ds1[c] * st0[c], axis=0, keepdims=True))
            db_l.append(db + jnp.where(rid == C - 1, dbend, 0.0))
        dg = [_dot(triu, db_l[c], _DIMS["nn"], precision=lax.Precision.HIGHEST) for c in rng]
        dlb_acc = jnp.zeros((1, HG_DK), F32)
        for c in rng:
            sq, _, sg, f, _, _, _, _, _, _ = pre[c]
            hq = hq_ref[sls[c], :]
            df = dg[c] / f - dk_l[c]
            dlb_acc = dlb_acc + jnp.sum(df * (1.0 - sg), axis=0, keepdims=True)
            dhf_ref[sls[c], :] = (df * (1.0 - lbv) * sg * (1.0 - sg)).astype(dhf_ref.dtype)
            dhq_ref[sls[c], :] = (dq_l[c] * (sq * (1.0 + hq * (1.0 - sq)))).astype(dhq_ref.dtype)
            dhi_ref[sls[c], :] = dv[c].astype(dhi_ref.dtype)
        dlb_ref[...] += dlb_acc

    def grp(gidx):
        return pl.BlockSpec((T, 128), lambda h, t: (nT - 1 - t, gidx * 8 + h))

    tok = pl.BlockSpec((T, 128), lambda h, t: (nT - 1 - t, h))
    big = jax.ShapeDtypeStruct((S, HG_HEADS * HG_DV), BF16)
    return pl.pallas_call(
        body, name=name, grid=(HG_HEADS, nT),
        in_specs=[grp(0), grp(1), grp(2), grp(3),
                  pl.BlockSpec((1, 128), lambda h, t: (0, h)), pl.BlockSpec((1, 128), lambda h, t: (0, 0)),
                  tok, pl.BlockSpec((1, nch, HG_DV, HG_DK), lambda h, t: (h, nT - 1 - t, 0, 0)), tok],
        out_specs=[tok, tok, tok, tok, pl.BlockSpec((1, 128), lambda h, t: (0, h)),
                   pl.BlockSpec((1, 1, 128), lambda h, t: (h, 0, 0))],
        out_shape=[big, big, big, big, jax.ShapeDtypeStruct((1, HG_HEADS * HG_DK), F32),
                   jax.ShapeDtypeStruct((HG_HEADS, 1, HG_DV), F32)],
        scratch_shapes=[pltpu.VMEM((HG_DV, HG_DK), F32)],
        compiler_params=_cparams(("parallel", "arbitrary")),
    )(proj, proj, proj, proj, lb, gnorm, o, states, doa)


def _lb_fwd(logits, *, name):
    def body(l_ref, lb_ref):
        lb_ref[...] = _sigmoid(l_ref[0:1, :] - l_ref[1:2, :])

    return pl.pallas_call(body, name=name, out_shape=jax.ShapeDtypeStruct((1, logits.shape[1]), F32))(logits)


def _lb_bwd(logits, dlb, *, name):
    def body(l_ref, d_ref, o_ref):
        lbv = _sigmoid(l_ref[0:1, :] - l_ref[1:2, :])
        t = d_ref[...] * lbv * (1.0 - lbv)
        o_ref[0:1, :] = t
        o_ref[1:2, :] = -t

    return pl.pallas_call(body, name=name, out_shape=jax.ShapeDtypeStruct(logits.shape, F32))(logits, dlb)


NEG = -1e30
FOX_SCALE = FOX_DH ** -0.5
FOX_PAIRS = FOX_HEADS // 2


def _fox_gate_fwd(ff, bias, *, name, T=512):
    S = ff.shape[0]
    T = min(T, S)

    def body(ff_ref, b_ref, c_ref, carry):
        @pl.when(pl.program_id(0) == 0)
        def _():
            carry[...] = jnp.zeros_like(carry)

        z = ff_ref[...] + b_ref[...]
        logf = jnp.minimum(z, 0.0) - jnp.log(1.0 + jnp.exp(-jnp.abs(z)))
        row = lax.broadcasted_iota(jnp.int32, (T, T), 0)
        col = lax.broadcasted_iota(jnp.int32, (T, T), 1)
        c = _dot((row >= col).astype(F32), logf, _DIMS["nn"], precision=lax.Precision.HIGHEST) + carry[...]
        c_ref[...] = c
        carry[...] = c[T - 1:T, :]

    return pl.pallas_call(
        body, name=name, grid=(S // T,),
        in_specs=[pl.BlockSpec((T, 128), lambda i: (i, 0)), pl.BlockSpec((1, 128), lambda i: (0, 0))],
        out_specs=pl.BlockSpec((T, 128), lambda i: (i, 0)),
        out_shape=jax.ShapeDtypeStruct((S, 128), F32),
        scratch_shapes=[pltpu.VMEM((1, 128), F32)],
        compiler_params=_cparams(("arbitrary",)),
    )(ff, bias)


def _fox_gate_bwd(ff, bias, dcs, *, name, T=512):
    S = ff.shape[0]
    T = min(T, S)
    nT = S // T

    def body(ff_ref, b_ref, d_ref, dff_ref, db_ref, carry):
        @pl.when(pl.program_id(0) == 0)
        def _():
            carry[...] = jnp.zeros_like(carry)
            db_ref[...] = jnp.zeros_like(db_ref)

        row = lax.broadcasted_iota(jnp.int32, (T, T), 0)
        col = lax.broadcasted_iota(jnp.int32, (T, T), 1)
        dlogf = carry[...] - _dot((row <= col).astype(F32), d_ref[...], _DIMS["nn"], precision=lax.Precision.HIGHEST)
        carry[...] = dlogf[0:1, :]
        dff = dlogf * (1.0 - _sigmoid(ff_ref[...] + b_ref[...]))
        dff_ref[...] = dff.astype(dff_ref.dtype)
        db_ref[...] += jnp.sum(dff, axis=0, keepdims=True)

    rev = pl.BlockSpec((T, 128), lambda i: (nT - 1 - i, 0))
    vec = pl.BlockSpec((1, 128), lambda i: (0, 0))
    return pl.pallas_call(
        body, name=name, grid=(nT,),
        in_specs=[rev, vec, rev], out_specs=[rev, vec],
        out_shape=[jax.ShapeDtypeStruct((S, 128), BF16), jax.ShapeDtypeStruct((1, 128), F32)],
        scratch_shapes=[pltpu.VMEM((1, 128), F32)],
        compiler_params=_cparams(("arbitrary",)),
    )(ff, bias, dcs)


AUG = FOX_DH


def _split3(x):
    a = x.astype(BF16).astype(F32)
    r = x - a
    b = r.astype(BF16).astype(F32)
    return a, b, r - b


def _lane_fill(lane, base, pieces, start):
    for i, pc in enumerate(pieces):
        base = jnp.where(lane == start + i, pc, base)
    return base


FOX_TB = 512
FOX_SKIP = 40.0
N_STAT = 4


def _fox_prep(proj, c_tok, *, name):
    S = proj.shape[0]
    T = min(FOX_TB, S)

    def body(q_ref, k_ref, v_ref, c_ref, qa_ref, ka_ref, va_ref, st_ref):
        pair = pl.program_id(0)
        lane = lax.broadcasted_iota(jnp.int32, (T, 128), 1)
        lane1 = lax.broadcasted_iota(jnp.int32, (1, 128), 1)
        c = c_ref[...]
        ones3 = jnp.where((lane >= AUG) & (lane < AUG + 3), 1.0, 0.0)

        def max_norm(t):
            tr = jnp.where(lane < AUG, t.astype(BF16).astype(F32), 0.0)
            return jnp.sqrt(jnp.max(jnp.sum(tr * tr, axis=-1, keepdims=True), axis=0, keepdims=True))

        for hh in range(2):
            ch = jnp.sum(jnp.where(lane == 2 * pair + hh, c, 0.0), axis=-1, keepdims=True)
            c1, c2, c3 = _split3(ch)
            q, k, v = q_ref[...], k_ref[...], v_ref[...]
            if hh == 1:
                q, k, v = (pltpu.roll(t, 64, 1) for t in (q, k, v))
            aug_q = _lane_fill(lane, jnp.where((lane >= AUG + 3) & (lane < AUG + 6), 1.0, 0.0), (c1, c2, c3), AUG)
            aug_k = _lane_fill(lane, ones3, (-c1, -c2, -c3), AUG + 3)
            qa_ref[hh] = jnp.where(lane < AUG, q * FOX_SCALE, aug_q).astype(BF16)
            ka_ref[hh] = jnp.where(lane < AUG, k, aug_k).astype(BF16)
            va_ref[hh] = jnp.where(lane < AUG, v, ones3).astype(BF16)
            stats = (max_norm(q * FOX_SCALE), jnp.max(ch, axis=0, keepdims=True), max_norm(k),
                     jnp.min(ch, axis=0, keepdims=True))
            st_ref[hh, 0] = _lane_fill(lane1, jnp.zeros((1, 128), F32), stats, 0)

    def grp(g):
        return pl.BlockSpec((T, 128), lambda p, t: (t, g * 8 + p))

    hm = pl.BlockSpec((2, T, 128), lambda p, t: (p, t, 0))
    out = jax.ShapeDtypeStruct((FOX_HEADS, S, 128), BF16)
    return pl.pallas_call(
        body, name=name, grid=(FOX_PAIRS, S // T),
        in_specs=[grp(4), grp(5), grp(6), pl.BlockSpec((T, 128), lambda p, t: (t, 0))],
        out_specs=[hm, hm, hm, pl.BlockSpec((2, 1, 1, 128), lambda p, t: (p, t, 0, 0))],
        out_shape=[out, out, out, jax.ShapeDtypeStruct((FOX_HEADS, S // T, 1, 128), F32)],
        compiler_params=_cparams(("parallel", "parallel")),
    )(proj, proj, proj, c_tok)


def _fox_bound(st_ref, head, nb, qi, ki):
    qb_, kb_ = (head * nb + qi) * N_STAT, (head * nb + ki) * N_STAT
    return st_ref[qb_] * st_ref[kb_ + 2] + st_ref[qb_ + 1] - st_ref[kb_ + 3] + 0.01


def _pair_lanes(lane, a0, a1):
    return jnp.where(lane < AUG, a0, pltpu.roll(a1, 64, 1))


def _first_live_key(st_ref, head, nb, qi, newest, thr):
    def body(t, k0):
        k = newest - t
        return jnp.where(_fox_bound(st_ref, head, nb, qi, k) > thr, k, k0)

    return lax.fori_loop(0, newest + 1, body, newest + 1)


def _last_live_query(st_ref, lm_ref, head, nb, ki):
    def body(t, i1):
        i = ki + 1 + t
        live = _fox_bound(st_ref, head, nb, i, ki) > lm_ref[head * nb + i] - FOX_SKIP
        return jnp.where(live, i, i1)

    return lax.fori_loop(0, nb - 1 - ki, body, ki)


class _BlockStream:
    def __init__(self, hbm_refs, bufs, sems, pair, tb):
        self.hbm, self.bufs, self.sems, self.pair, self.tb = hbm_refs, bufs, sems, pair, tb

    def _copies(self, blk, slot):
        rows = pl.ds(pl.multiple_of(blk * self.tb, self.tb), self.tb)
        return [pltpu.make_async_copy(h.at[pl.ds(2 * self.pair, 2), rows, :], b.at[slot], self.sems.at[n, slot])
                for n, (h, b) in enumerate(zip(self.hbm, self.bufs))]

    def start(self, blk, slot):
        for cp in self._copies(blk, slot):
            cp.start()

    def wait(self, blk, slot):
        for cp in self._copies(blk, slot):
            cp.wait()


def _fox_fwd(qa, ka, va, bounds, *, name):
    S = qa.shape[1]
    tb = min(FOX_TB, S)
    nb = S // tb

    def body(qa_ref, ka_hbm, va_hbm, st_ref, o_ref, qb_ref, lse_ref, kbuf, vbuf, sems, m_s, acc_s, m_min):
        pair, qi = pl.program_id(0), pl.program_id(1)
        stream = _BlockStream((ka_hbm, va_hbm), (kbuf, vbuf), sems, pair, tb)

        def head_step(hh, slot, masked):
            s = _dot(qa_ref[hh], kbuf[slot, hh], _DIMS["nt"])
            if masked:
                row = lax.broadcasted_iota(jnp.int32, (tb, tb), 0)
                col = lax.broadcasted_iota(jnp.int32, (tb, tb), 1)
                s = jnp.where(col <= row, s, NEG)
            m_old = m_s[hh]
            m_new = jnp.maximum(m_old, jnp.max(s, axis=-1, keepdims=True))
            p = jnp.exp(s - m_new)
            p_hi = p.astype(BF16)
            p_lo = (p - p_hi.astype(F32)).astype(BF16)
            vv = vbuf[slot, hh]
            acc_s[hh] = (jnp.exp(m_old - m_new) * acc_s[hh]
                         + _dot(p_hi, vv, _DIMS["nn"]) + _dot(p_lo, vv, _DIMS["nn"]))
            m_s[hh] = m_new
            m_min[hh] = jnp.min(m_new)

        @pl.when(qi == 0)
        def _():
            stream.start(qi, 0)

        @pl.when(qi > 0)
        def _():
            stream.start(qi - 1, 1)

        m_s[...] = jnp.full_like(m_s, NEG)
        acc_s[...] = jnp.zeros_like(acc_s)
        stream.wait(qi, 0)
        for hh in range(2):
            head_step(hh, 0, True)

        @pl.when(qi > 1)
        def _():
            stream.start(qi - 2, 0)

        @pl.when(qi > 0)
        def _():
            stream.wait(qi - 1, 1)
            for hh in range(2):
                head_step(hh, 1, False)

        k0 = [_first_live_key(st_ref, 2 * pair + hh, nb, qi, qi - 2, m_min[hh] - FOX_SKIP) for hh in range(2)]
        n = qi - 1 - jnp.minimum(k0[0], k0[1])

        @pl.when((qi > 1) & (n == 0))
        def _():
            stream.wait(qi - 2, 0)

        def loop(t, carry):
            k = qi - 2 - t
            slot = t % 2
            stream.wait(k, slot)

            @pl.when(t + 1 < n)
            def _():
                stream.start(k - 1, 1 - slot)

            for hh in range(2):
                @pl.when(k >= k0[hh])
                def _():
                    head_step(hh, slot, False)
            return carry

        lax.fori_loop(0, n, loop, 0)

        @pl.when(qi + 1 < nb)
        def _():
            stream.start(qi + 1, 0)

        lane = lax.broadcasted_iota(jnp.int32, (tb, 128), 1)
        outs = []
        for hh in range(2):
            acc = acc_s[hh]
            l = acc[:, AUG:AUG + 1]
            outs.append(acc / l)
            lse = m_s[hh] + jnp.log(l)
            lse_ref[hh, 0] = jnp.broadcast_to(jnp.min(lse, axis=0, keepdims=True), (1, 128))
            qf = qa_ref[hh].astype(F32)
            cb = qf[:, AUG:AUG + 1] + qf[:, AUG + 1:AUG + 2] + qf[:, AUG + 2:AUG + 3] - lse
            qb_ref[hh] = _lane_fill(lane, qf, _split3(cb), AUG).astype(BF16)
        o_ref[...] = _pair_lanes(lane, outs[0], outs[1])

    qs = pl.BlockSpec((2, tb, 128), lambda p, i: (p, i, 0))
    return pl.pallas_call(
        body, name=name, grid=(FOX_PAIRS, nb),
        in_specs=[qs, ANY, ANY, SMEM],
        out_specs=[pl.BlockSpec((tb, 128), lambda p, i: (i, p)), qs,
                   pl.BlockSpec((2, 1, 1, 128), lambda p, i: (p, i, 0, 0))],
        out_shape=[jax.ShapeDtypeStruct((S, FOX_HEADS * FOX_DH), F32), jax.ShapeDtypeStruct((FOX_HEADS, S, 128), BF16),
                   jax.ShapeDtypeStruct((FOX_HEADS, nb, 1, 128), F32)],
        scratch_shapes=[pltpu.VMEM((2, 2, tb, 128), BF16), pltpu.VMEM((2, 2, tb, 128), BF16),
                        pltpu.SemaphoreType.DMA((2, 2)), pltpu.VMEM((2, tb, 1), F32), pltpu.VMEM((2, tb, 128), F32),
                        pltpu.SMEM((2,), F32)],
        compiler_params=_cparams(("arbitrary", "arbitrary")),
    )(qa, ka, va, bounds)


def _fox_bwd_prep(o, do, *, name, T=512):
    S = o.shape[0]
    T = min(T, S)

    def body(o_ref, do_ref, dob_ref):
        lane = lax.broadcasted_iota(jnp.int32, (T, 128), 1)
        d = do_ref[...].astype(F32)
        prod = d * o_ref[...]
        for hh in range(2):
            mine = (lane < AUG) if hh == 0 else (lane >= AUG)
            delta = jnp.sum(jnp.where(mine, prod, 0.0), axis=-1, keepdims=True)
            dh = d if hh == 0 else pltpu.roll(d, 64, 1)
            dob_ref[hh] = _lane_fill(lane, jnp.where(lane < AUG, dh, 0.0), _split3(-delta), AUG).astype(BF16)

    tok = pl.BlockSpec((T, 128), lambda p, t: (t, p))
    return pl.pallas_call(
        body, name=name, grid=(FOX_PAIRS, S // T),
        in_specs=[tok, tok], out_specs=pl.BlockSpec((2, T, 128), lambda p, t: (p, t, 0)),
        out_shape=jax.ShapeDtypeStruct((FOX_HEADS, S, 128), BF16),
        compiler_params=_cparams(("parallel", "parallel")),
    )(o, do)


def _fox_bwd_dq(qb, ka, va, dob, bounds, lse_min, *, name, comm=None):
    S = qb.shape[1]
    tb = min(FOX_TB, S)
    nb = S // tb
    nc = comm.n if comm is not None else 0

    def body(qb_ref, dob_ref, ka_hbm, va_hbm, st_ref, lm_ref, *rest):
        c_in, (dq_ref, dcs_ref), c_out = rest[:nc], rest[nc:nc + 2], rest[nc + 2:2 * nc + 2]
        kbuf, vbuf, sems, acc_s = rest[2 * nc + 2:2 * nc + 6]
        c_sems = rest[2 * nc + 6:]
        pair, qi = pl.program_id(0), pl.program_id(1)
        if comm is not None:
            @pl.when((pair == 0) & (qi == 0))
            def _():
                comm.start(c_in, c_out, c_sems)

        stream = _BlockStream((ka_hbm, va_hbm), (kbuf, vbuf), sems, pair, tb)
        k0 = [_first_live_key(st_ref, 2 * pair + hh, nb, qi, qi - 1, lm_ref[(2 * pair + hh) * nb + qi] - FOX_SKIP)
              for hh in range(2)]
        n = qi - jnp.minimum(k0[0], k0[1]) + 1

        @pl.when(qi == 0)
        def _():
            stream.start(qi, 0)

        acc_s[...] = jnp.zeros_like(acc_s)
        dcs_ref[...] = jnp.zeros_like(dcs_ref)

        def head_step(hh, slot, k, masked):
            s = _dot(qb_ref[hh], kbuf[slot, hh], _DIMS["nt"])
            if masked:
                row = lax.broadcasted_iota(jnp.int32, (tb, tb), 0)
                col = lax.broadcasted_iota(jnp.int32, (tb, tb), 1)
                s = jnp.where(col <= row, s, NEG)
            ds = jnp.exp(s) * _dot(dob_ref[hh], vbuf[slot, hh], _DIMS["nt"])
            dcs_ref[0, 0, hh:hh + 1, pl.ds(pl.multiple_of(k * tb, tb), tb)] = jnp.sum(ds, axis=0, keepdims=True)
            acc_s[hh] += _dot(ds.astype(BF16), kbuf[slot, hh], _DIMS["nn"])

        def loop(t, carry):
            k = qi - t
            slot = t % 2
            stream.wait(k, slot)

            @pl.when(t + 1 < n)
            def _():
                stream.start(k - 1, 1 - slot)

            @pl.when(t == 0)
            def _():
                for hh in range(2):
                    head_step(hh, slot, k, True)

            for hh in range(2):
                @pl.when((t > 0) & (k >= k0[hh]))
                def _():
                    head_step(hh, slot, k, False)
            return carry

        lax.fori_loop(0, n, loop, 0)

        @pl.when(qi + 1 < nb)
        def _():
            stream.start(qi + 1, 0)

        lane = lax.broadcasted_iota(jnp.int32, (tb, 128), 1)
        dq_ref[...] = (_pair_lanes(lane, acc_s[0], acc_s[1]) * FOX_SCALE).astype(dq_ref.dtype)
        if comm is not None:
            @pl.when((pair == FOX_PAIRS - 1) & (qi == nb - 1))
            def _():
                comm.finish(c_in, c_out, c_sems)

    qs = pl.BlockSpec((2, tb, 128), lambda p, i: (p, i, 0))
    outs = pl.pallas_call(
        body, name=name, grid=(FOX_PAIRS, nb),
        in_specs=[qs, qs, ANY, ANY, SMEM, SMEM] + [ANY] * nc,
        out_specs=[pl.BlockSpec((tb, 128), lambda p, i: (i, p)),
                   pl.BlockSpec((1, 1, 2, S), lambda p, i: (p, i, 0, 0))] + [ANY] * nc,
        out_shape=[jax.ShapeDtypeStruct((S, FOX_HEADS * FOX_DH), BF16),
                   jax.ShapeDtypeStruct((FOX_PAIRS, nb, 2, S), F32)] + (comm.out_shapes if comm is not None else []),
        scratch_shapes=[pltpu.VMEM((2, 2, tb, 128), BF16), pltpu.VMEM((2, 2, tb, 128), BF16),
                        pltpu.SemaphoreType.DMA((2, 2)), pltpu.VMEM((2, tb, 128), F32)]
        + (comm.scratch if comm is not None else []),
        compiler_params=_cparams(("arbitrary", "arbitrary")),
    )(qb, dob, ka, va, bounds, lse_min, *(comm.inputs if comm is not None else []))
    return (outs[0], outs[1]) if comm is None else (outs[0], outs[1], outs[2:])


def _fox_bwd_dkv(qb, ka, va, dob, bounds, lse_min, *, name):
    S = qb.shape[1]
    tb = min(FOX_TB, S)
    nb = S // tb

    def body(ka_ref, va_ref, qb_hbm, dob_hbm, st_ref, lm_ref, dk_ref, dv_ref, qbuf, dbuf, sems, dk_s, dv_s):
        pair, ki = pl.program_id(0), pl.program_id(1)
        stream = _BlockStream((qb_hbm, dob_hbm), (qbuf, dbuf), sems, pair, tb)
        i1 = [_last_live_query(st_ref, lm_ref, 2 * pair + hh, nb, ki) for hh in range(2)]
        n = jnp.maximum(i1[0], i1[1]) - ki + 1

        @pl.when(ki == 0)
        def _():
            stream.start(ki, 0)

        dk_s[...] = jnp.zeros_like(dk_s)
        dv_s[...] = jnp.zeros_like(dv_s)

        def head_step(hh, slot, masked):
            st = _dot(ka_ref[hh], qbuf[slot, hh], _DIMS["nt"])
            if masked:
                row = lax.broadcasted_iota(jnp.int32, (tb, tb), 0)
                col = lax.broadcasted_iota(jnp.int32, (tb, tb), 1)
                st = jnp.where(row <= col, st, NEG)
            pt = jnp.exp(st)
            dst = pt * _dot(va_ref[hh], dbuf[slot, hh], _DIMS["nt"])
            dv_s[hh] += _dot(pt.astype(BF16), dbuf[slot, hh], _DIMS["nn"])
            dk_s[hh] += _dot(dst.astype(BF16), qbuf[slot, hh], _DIMS["nn"])

        def loop(t, carry):
            i = ki + t
            slot = t % 2
            stream.wait(i, slot)

            @pl.when(t + 1 < n)
            def _():
                stream.start(i + 1, 1 - slot)

            @pl.when(t == 0)
            def _():
                for hh in range(2):
                    head_step(hh, slot, True)

            for hh in range(2):
                @pl.when((t > 0) & (i <= i1[hh]))
                def _():
                    head_step(hh, slot, False)
            return carry

        lax.fori_loop(0, n, loop, 0)

        @pl.when(ki + 1 < nb)
        def _():
            stream.start(ki + 1, 0)

        lane = lax.broadcasted_iota(jnp.int32, (tb, 128), 1)
        dk_ref[...] = _pair_lanes(lane, dk_s[0], dk_s[1]).astype(dk_ref.dtype)
        dv_ref[...] = _pair_lanes(lane, dv_s[0], dv_s[1]).astype(dv_ref.dtype)

    ks = pl.BlockSpec((2, tb, 128), lambda p, j: (p, j, 0))
    tok = pl.BlockSpec((tb, 128), lambda p, j: (j, p))
    big = jax.ShapeDtypeStruct((S, FOX_HEADS * FOX_DH), BF16)
    return pl.pallas_call(
        body, name=name, grid=(FOX_PAIRS, nb),
        in_specs=[ks, ks, ANY, ANY, SMEM, SMEM], out_specs=[tok, tok], out_shape=[big, big],
        scratch_shapes=[pltpu.VMEM((2, 2, tb, 128), BF16), pltpu.VMEM((2, 2, tb, 128), BF16),
                        pltpu.SemaphoreType.DMA((2, 2)), pltpu.VMEM((2, tb, 128), F32), pltpu.VMEM((2, tb, 128), F32)],
        compiler_params=_cparams(("arbitrary", "arbitrary")),
    )(ka, va, qb, dob, bounds, lse_min)


def _merge_fwd(proj, pa, pb, *, name, T=512):
    S, D = pa.shape
    T = min(T, S)

    def body(ga_ref, gb_ref, pa_ref, pb_ref, m_ref):
        m_ref[...] = (_sigmoid(ga_ref[...]) * pa_ref[...] + _sigmoid(gb_ref[...]) * pb_ref[...]).astype(m_ref.dtype)

    tok = pl.BlockSpec((T, D), lambda i: (i, 0))
    return pl.pallas_call(
        body, name=name, grid=(S // T,),
        in_specs=[pl.BlockSpec((T, D), lambda i: (i, 7)), pl.BlockSpec((T, D), lambda i: (i, 8)), tok, tok],
        out_specs=tok, out_shape=jax.ShapeDtypeStruct((S, D), BF16),
        compiler_params=_cparams(("parallel",)),
    )(proj, proj, pa, pb)


def _merge_bwd(proj, pa, pb, dm, *, name, T=512):
    S, D = pa.shape
    T = min(T, S)

    def body(ga_ref, gb_ref, pa_ref, pb_ref, dm_ref, dpa_ref, dpb_ref, dga_ref, dgb_ref):
        dm_ = dm_ref[...]
        sa, sb = _sigmoid(ga_ref[...]), _sigmoid(gb_ref[...])
        dpa_ref[...] = (dm_ * sa).astype(BF16)
        dpb_ref[...] = (dm_ * sb).astype(BF16)
        dga_ref[...] = (dm_ * pa_ref[...] * sa * (1.0 - sa)).astype(BF16)
        dgb_ref[...] = (dm_ * pb_ref[...] * sb * (1.0 - sb)).astype(BF16)

    tok = pl.BlockSpec((T, D), lambda i: (i, 0))
    big = jax.ShapeDtypeStruct((S, D), BF16)
    return pl.pallas_call(
        body, name=name, grid=(S // T,),
        in_specs=[pl.BlockSpec((T, D), lambda i: (i, 7)), pl.BlockSpec((T, D), lambda i: (i, 8)), tok, tok, tok],
        out_specs=[tok, tok, tok, tok], out_shape=[big, big, big, big],
        compiler_params=_cparams(("parallel",)),
    )(proj, proj, pa, pb, dm)


INV_SQRT2 = 0.7071067811865476
INV_SQRT2PI = 0.3989422804014327


def _shifted(u, prev, rid):
    m1 = jnp.where(rid == 0, prev[7:8, :], pltpu.roll(u, 1, 0))
    m2 = jnp.where(rid == 0, prev[6:7, :], jnp.where(rid == 1, prev[7:8, :], pltpu.roll(u, 2, 0)))
    return m1, m2


def _conv_acc(u, prev, w_ref, b_ref, rid):
    m1, m2 = _shifted(u, prev, rid)
    return b_ref[...] + w_ref[0:1, :] * m2 + w_ref[1:2, :] * m1 + w_ref[2:3, :] * u, m1, m2


def _convglu_fwd(ug, uv, wg, wv, bg, bv, *, name, T=512, tc=256):
    S, F = ug.shape
    T = min(T, S)

    def body(ug_ref, uv_ref, wg_ref, wv_ref, bg_ref, bv_ref, a_ref, pg, pv):
        @pl.when(pl.program_id(1) == 0)
        def _():
            pg[...] = jnp.zeros_like(pg)
            pv[...] = jnp.zeros_like(pv)

        rid = lax.broadcasted_iota(jnp.int32, (T, tc), 0)
        g_, v_ = ug_ref[...], uv_ref[...]
        accg, _, _ = _conv_acc(g_, pg[...], wg_ref, bg_ref, rid)
        accv, _, _ = _conv_acc(v_, pv[...], wv_ref, bv_ref, rid)
        gel = 0.5 * accg * (1.0 + lax.erf(accg * INV_SQRT2))
        a_ref[...] = (gel * accv).astype(a_ref.dtype)
        pg[...] = g_[T - 8:T, :]
        pv[...] = v_[T - 8:T, :]

    tok = pl.BlockSpec((T, tc), lambda j, t: (t, j))
    w3 = pl.BlockSpec((3, tc), lambda j, t: (0, j))
    b1 = pl.BlockSpec((1, tc), lambda j, t: (0, j))
    return pl.pallas_call(
        body, name=name, grid=(F // tc, S // T),
        in_specs=[tok, tok, w3, w3, b1, b1], out_specs=tok,
        out_shape=jax.ShapeDtypeStruct((S, F), BF16),
        scratch_shapes=[pltpu.VMEM((8, tc), F32), pltpu.VMEM((8, tc), F32)],
        compiler_params=_cparams(("parallel", "arbitrary")),
    )(ug, uv, wg, wv, bg, bv)


def _convglu_bwd(ug, uv, wg, wv, bg, bv, da, *, name, T=512, tc=256):
    S, F = ug.shape
    T = min(T, S)
    nT = S // T
    halo_blocks = T // 8

    def up_shift(d, nx, rid):
        p1 = jnp.where(rid == T - 1, nx[0:1, :], pltpu.roll(d, T - 1, 0))
        p2 = jnp.where(rid == T - 1, nx[1:2, :], jnp.where(rid == T - 2, nx[0:1, :], pltpu.roll(d, T - 2, 0)))
        return p1, p2

    def body(ug_ref, uv_ref, hg_ref, hv_ref, wg_ref, wv_ref, bg_ref, bv_ref, da_ref,
             dug_ref, duv_ref, dwg_ref, dwv_ref, dbg_ref, dbv_ref, ng, nv):
        @pl.when(pl.program_id(1) == 0)
        def _():
            ng[...] = jnp.zeros_like(ng)
            nv[...] = jnp.zeros_like(nv)
            for r in (dwg_ref, dwv_ref, dbg_ref, dbv_ref):
                r[...] = jnp.zeros_like(r)

        first_block = pl.program_id(1) == nT - 1
        rid = lax.broadcasted_iota(jnp.int32, (T, tc), 0)
        g_, v_ = ug_ref[...], uv_ref[...]
        pg = jnp.where(first_block, 0.0, hg_ref[...])
        pv = jnp.where(first_block, 0.0, hv_ref[...])
        accg, g1, g2 = _conv_acc(g_, pg, wg_ref, bg_ref, rid)
        accv, v1, v2 = _conv_acc(v_, pv, wv_ref, bv_ref, rid)
        cdf = 0.5 * (1.0 + lax.erf(accg * INV_SQRT2))
        pdf = INV_SQRT2PI * jnp.exp(-0.5 * accg * accg)
        da_ = da_ref[...].astype(F32)
        dgate = da_ * accv * (cdf + accg * pdf)
        dval = da_ * (accg * cdf)
        dbg_ref[...] += jnp.sum(dgate, axis=0, keepdims=True)
        dbv_ref[...] += jnp.sum(dval, axis=0, keepdims=True)
        for j, (sg_, sv_) in enumerate(((g2, v2), (g1, v1), (g_, v_))):
            dwg_ref[j:j + 1, :] += jnp.sum(dgate * sg_, axis=0, keepdims=True)
            dwv_ref[j:j + 1, :] += jnp.sum(dval * sv_, axis=0, keepdims=True)
        for d, w_ref, nx, out_ref in ((dgate, wg_ref, ng, dug_ref), (dval, wv_ref, nv, duv_ref)):
            p1, p2 = up_shift(d, nx[...], rid)
            out_ref[...] = (w_ref[2:3, :] * d + w_ref[1:2, :] * p1 + w_ref[0:1, :] * p2).astype(out_ref.dtype)
            nx[...] = d[0:8, :]

    tok = pl.BlockSpec((T, tc), lambda j, t: (nT - 1 - t, j))
    halo = pl.BlockSpec((8, tc), lambda j, t: (jnp.maximum((nT - 1 - t) * halo_blocks - 1, 0), j))
    w3 = pl.BlockSpec((3, tc), lambda j, t: (0, j))
    b1 = pl.BlockSpec((1, tc), lambda j, t: (0, j))
    big = jax.ShapeDtypeStruct((S, F), BF16)
    return pl.pallas_call(
        body, name=name, grid=(F // tc, nT),
        in_specs=[tok, tok, halo, halo, w3, w3, b1, b1, tok], out_specs=[tok, tok, w3, w3, b1, b1],
        out_shape=[big, big, jax.ShapeDtypeStruct((3, F), F32), jax.ShapeDtypeStruct((3, F), F32),
                   jax.ShapeDtypeStruct((1, F), F32), jax.ShapeDtypeStruct((1, F), F32)],
        scratch_shapes=[pltpu.VMEM((8, tc), F32), pltpu.VMEM((8, tc), F32)],
        compiler_params=_cparams(("parallel", "arbitrary")),
    )(ug, uv, ug, uv, wg, wv, bg, bv, da)


FF_LO, FF_HI = 7168, 7184


def _col_blocks(a, width):
    return jnp.stack([a[:, d * width:(d + 1) * width] for d in range(N_DEV)])


def _late_weights(g_a, g_b, g_o, g_up, g_cw, g_d):
    wup = jnp.concatenate([g_up[d] for d in range(N_DEV)], axis=1)
    cw = jnp.concatenate([g_cw[d] for d in range(N_DEV)], axis=1)
    return dict(wa=g_a.reshape(D_MODEL, D_MODEL), wb=g_b.reshape(D_MODEL, D_MODEL), wo=g_o.reshape(D_MODEL, D_MODEL),
                wug=wup[:, :D_FF], wuv=wup[:, D_FF:], cwg=cw[:, :D_FF], cwv=cw[:, D_FF:], wd=g_d.reshape(D_FF, D_MODEL))


def _early_grad_blocks(d_wa, d_wb, d_wo, d_wug, d_wuv, d_wd):
    up = jnp.stack([d_wug[:, d * 704:(d + 1) * 704] for d in range(4)]
                   + [d_wuv[:, d * 704:(d + 1) * 704] for d in range(4)])
    return [d_wa.reshape(N_DEV, 128, D_MODEL), d_wb.reshape(N_DEV, 128, D_MODEL), d_wo.reshape(N_DEV, 128, D_MODEL),
            up, d_wd.reshape(N_DEV, 352, D_MODEL)]


def _local_step(x, tgt, w, p, late=None, exchange=False):
    S = x.shape[0]
    mm = _matmul
    n1 = _rms_fwd(x, p["norm_mix"], name="rms1_fwd")
    if late is None:
        proj = mm(n1, w["wm"], "nn", name="proj_main")
    else:
        proj, gathered = mm(n1, w["wm"], "nn", comm=late, name="proj_main")
        w = {**w, **_late_weights(*gathered)}
    ff = mm(n1, w["wff"], "nn", name="proj_ff")
    lb = _lb_fwd(p["hg_lb_logits"], name="lb_fwd")
    gnorm = p["hg_norm"].reshape(1, HG_DV)
    o_hg, oa, states = _hgrn_fwd(proj, lb, gnorm, name="hgrn_fwd")
    bias = jnp.pad(p["fox_f_bias"].reshape(1, FOX_HEADS), ((0, 0), (0, 128 - FOX_HEADS)))
    c = _fox_gate_fwd(ff, bias, name="fox_gate_fwd")
    qa, ka, va, fox_stats = _fox_prep(proj, c, name="fox_prep")
    bounds = fox_stats[:, :, 0, :N_STAT].reshape(-1)
    ob, qb, lse_stats = _fox_fwd(qa, ka, va, bounds, name="fox_fwd")
    lse_min = lse_stats[:, :, 0, 0].reshape(-1)
    pa = mm(oa, w["wa"], "nn", name="branch_a")
    pb = mm(ob, w["wb"], "nn", name="branch_b")
    merged = _merge_fwd(proj, pa, pb, name="merge_fwd")
    h1 = mm(merged, w["wo"], "nn", addend=x, name="mix_out")
    n2 = _rms_fwd(h1, p["norm_ffn"], name="rms2_fwd")
    ug = mm(n2, w["wug"], "nn", name="up_gate")
    uv = mm(n2, w["wuv"], "nn", name="up_val")
    a = _convglu_fwd(ug, uv, w["cwg"], w["cwv"], p["cbg"], p["cbv"], name="convglu_fwd")
    h2 = mm(a, w["wd"], "nn", addend=h1, name="ffn_down")
    loss, dh2, d_norm_final = _loss_head(h2, p["norm_final"], tgt, name="loss_head")
    da = mm(dh2, w["wd"], "nt", out_dtype=BF16, name="d_act")
    d_wd = mm(a, dh2, "tn", out_dtype=BF16, name="dw_down")
    dug, duv, d_cwg, d_cwv, d_cbg, d_cbv = _convglu_bwd(
        ug, uv, w["cwg"], w["cwv"], p["cbg"], p["cbv"], da, name="convglu_bwd")
    dn2 = mm(dug, w["wug"], "nt", name="dn2_gate")
    dn2 = mm(duv, w["wuv"], "nt", addend=dn2, name="dn2_val")
    d_wug = mm(n2, dug, "tn", out_dtype=BF16, name="dw_up_gate")
    d_wuv = mm(n2, duv, "tn", out_dtype=BF16, name="dw_up_val")
    dh1, d_norm_ffn = _rms_bwd(h1, p["norm_ffn"], dn2, dh2, name="rms2_bwd")
    dmerged = mm(dh1, w["wo"], "nt", name="d_merged")
    d_wo = mm(merged, dh1, "tn", out_dtype=BF16, name="dw_out")
    dpa, dpb, dga, dgb = _merge_bwd(proj, pa, pb, dmerged, name="merge_bwd")
    doa = mm(dpa, w["wa"], "nt", name="d_oa")
    dob = mm(dpb, w["wb"], "nt", out_dtype=BF16, name="d_ob")
    d_wa = mm(oa, dpa, "tn", out_dtype=BF16, name="dw_branch_a")
    d_wb = mm(ob, dpb, "tn", out_dtype=BF16, name="dw_branch_b")
    dhq, dhf, dhi, dhg, dlb, dgn8 = _hgrn_bwd(proj, lb, gnorm, o_hg, states, doa, name="hgrn_bwd")
    d_logits = _lb_bwd(p["hg_lb_logits"], dlb, name="lb_bwd")
    dob_hm = _fox_bwd_prep(ob, dob, name="fox_bwd_prep")
    early_parts = None
    if exchange:
        comm = _ExchangeComm(_early_grad_blocks(d_wa, d_wb, d_wo, d_wug, d_wuv, d_wd))
        dq, dcsp, early_parts = _fox_bwd_dq(qb, ka, va, dob_hm, bounds, lse_min, comm=comm, name="fox_bwd_dq")
    else:
        dq, dcsp = _fox_bwd_dq(qb, ka, va, dob_hm, bounds, lse_min, name="fox_bwd_dq")
    dk, dv = _fox_bwd_dkv(qb, ka, va, dob_hm, bounds, lse_min, name="fox_bwd_dkv")
    dcs = jnp.sum(dcsp, axis=1)
    dcs_tok = jnp.pad(dcs.reshape(FOX_HEADS, S).T, ((0, 0), (0, 128 - FOX_HEADS)))
    dff, dbias = _fox_gate_bwd(ff, bias, dcs_tok, name="fox_gate_bwd")
    dproj = jnp.concatenate([dhq, dhf, dhi, dhg, dq, dk, dv, dga, dgb], axis=1)
    d_wm = mm(n1, dproj, "tn", out_dtype=BF16, name="dw_in_main")
    d_wff = mm(n1, dff, "tn", out_dtype=BF16, name="dw_in_ff")
    dn1 = mm(dff, w["wff"], "nt", name="dn1_ff")
    late_parts = None
    if exchange:
        d_win = jnp.concatenate([d_wm[:, :FF_LO], d_wff[:, :FOX_HEADS], d_wm[:, FF_LO:]], axis=1)
        d_cw = jnp.concatenate([d_cwg, d_cwv], axis=1)
        comm = _ExchangeComm([_col_blocks(d_win, 1154), _col_blocks(d_cw, 704)])
        dn1, late_parts = mm(dproj, w["wm"], "nt", addend=dn1, comm=comm, name="dn1_main")
    else:
        dn1 = mm(dproj, w["wm"], "nt", addend=dn1, name="dn1_main")
    dx, d_norm_mix = _rms_bwd(x, p["norm_mix"], dn1, dh1, name="rms1_bwd")
    grads = dict(
        wm=d_wm, wff=d_wff, wa=d_wa, wb=d_wb, wo=d_wo, wug=d_wug, wuv=d_wuv, cwg=d_cwg, cwv=d_cwv, wd=d_wd,
        norm_mix=d_norm_mix.reshape(-1), fox_f_bias=dbias[0, :FOX_HEADS], hg_lb_logits=d_logits,
        hg_norm=jnp.sum(dgn8, axis=0).reshape(-1), norm_ffn=d_norm_ffn.reshape(-1), cbg=d_cbg, cbv=d_cbv,
        norm_final=d_norm_final.reshape(-1), early_parts=early_parts, late_parts=late_parts)
    return loss, dx, grads


SMALL = [("norm_mix", (1, D_MODEL)), ("fox_f_bias", (1, FOX_HEADS)), ("hg_lb_logits", (2, HG_HEADS * HG_DK)),
         ("hg_norm", (1, HG_DV)), ("norm_ffn", (1, D_MODEL)), ("conv_b", (1, 2 * D_FF)), ("norm_final", (D_MODEL,))]
SMALL_ROWS = 88
SHARDED = [("w_in", (D_MODEL, 1154), 256), ("w_branch_a", (128, D_MODEL), 128), ("w_branch_b", (128, D_MODEL), 128),
           ("w_out", (128, D_MODEL), 128), ("w_up", (D_MODEL, 704), 256), ("conv_w", (3, 704), 3),
           ("w_down", (352, D_MODEL), 352)]
NAMES = ["norm_mix", "w_in", "fox_f_bias", "hg_lb_logits", "hg_norm", "w_branch_a", "w_branch_b", "w_out",
         "norm_ffn", "w_up", "conv_w", "conv_b", "w_down", "norm_final"]


def _size(shape):
    n = 1
    for s in shape:
        n *= s
    return n


def _adamw(parts, w, m, v, *, name, T):
    R, C = w.shape
    c1 = 1.0 / (1.0 - ADAM_B1 ** ADAM_STEP)
    c2 = 1.0 / (1.0 - ADAM_B2 ** ADAM_STEP)

    def body(p_ref, w_ref, m_ref, v_ref, g_ref, d_ref, nm_ref, nv_ref):
        g = p_ref[0].astype(F32)
        for s in range(1, N_DEV):
            g = g + p_ref[s].astype(F32)
        g_ref[...] = g
        nm = ADAM_B1 * m_ref[...] + (1.0 - ADAM_B1) * g
        nv = ADAM_B2 * v_ref[...] + (1.0 - ADAM_B2) * (g * g)
        nm_ref[...] = nm
        nv_ref[...] = nv
        d_ref[...] = -ADAM_LR * ((nm * c1) / (jnp.sqrt(nv * c2) + ADAM_EPS) + ADAM_WD * w_ref[...])

    blk = pl.BlockSpec((T, C), lambda i: (i, 0))
    out = jax.ShapeDtypeStruct((R, C), F32)
    return pl.pallas_call(
        body, name=name, grid=(R // T,),
        in_specs=[pl.BlockSpec((N_DEV, T, C), lambda i: (0, i, 0)), blk, blk, blk],
        out_specs=[blk, blk, blk, blk], out_shape=[out, out, out, out],
        compiler_params=_cparams(("parallel",)),
    )(parts, w, m, v)


def _pack_small(vals):
    flat = jnp.concatenate([vals[n].reshape(-1).astype(F32) for n, _ in SMALL])
    return jnp.pad(flat, (0, SMALL_ROWS * 128 - flat.shape[0])).reshape(SMALL_ROWS, 128)


def _unpack_small(buf):
    flat, out, off = buf.reshape(-1), {}, 0
    for n, shape in SMALL:
        out[n] = flat[off:off + _size(shape)].reshape(shape)
        off += _size(shape)
    return out


def kernel(x, norm_mix, w_in, fox_f_bias, hg_lb_logits, hg_norm, w_branch_a, w_branch_b, w_out, norm_ffn, w_up, conv_w, conv_b, w_down, norm_final, loss_target, m_norm_mix, m_w_in, m_fox_f_bias, m_hg_lb_logits, m_hg_norm, m_w_branch_a, m_w_branch_b, m_w_out, m_norm_ffn, m_w_up, m_conv_w, m_conv_b, m_w_down, m_norm_final, v_norm_mix, v_w_in, v_fox_f_bias, v_hg_lb_logits, v_hg_norm, v_w_branch_a, v_w_branch_b, v_w_out, v_norm_ffn, v_w_up, v_conv_w, v_conv_b, v_w_down, v_norm_final):
    wv = dict(norm_mix=norm_mix, w_in=w_in, fox_f_bias=fox_f_bias, hg_lb_logits=hg_lb_logits, hg_norm=hg_norm,
              w_branch_a=w_branch_a, w_branch_b=w_branch_b, w_out=w_out, norm_ffn=norm_ffn, w_up=w_up, conv_w=conv_w,
              conv_b=conv_b, w_down=w_down, norm_final=norm_final)
    mv = dict(norm_mix=m_norm_mix, w_in=m_w_in, fox_f_bias=m_fox_f_bias, hg_lb_logits=m_hg_lb_logits, hg_norm=m_hg_norm,
              w_branch_a=m_w_branch_a, w_branch_b=m_w_branch_b, w_out=m_w_out, norm_ffn=m_norm_ffn, w_up=m_w_up,
              conv_w=m_conv_w, conv_b=m_conv_b, w_down=m_w_down, norm_final=m_norm_final)
    vv = dict(norm_mix=v_norm_mix, w_in=v_w_in, fox_f_bias=v_fox_f_bias, hg_lb_logits=v_hg_lb_logits, hg_norm=v_hg_norm,
              w_branch_a=v_w_branch_a, w_branch_b=v_w_branch_b, w_out=v_w_out, norm_ffn=v_norm_ffn, w_up=v_w_up,
              conv_w=v_conv_w, conv_b=v_conv_b, w_down=v_w_down, norm_final=v_norm_final)

    (g_in,) = _comm_call(_GatherComm([w_in[0].astype(BF16)]), name="gather_w_in")
    win = jnp.concatenate([g_in[d] for d in range(N_DEV)], axis=1)
    w = dict(wm=jnp.concatenate([win[:, :FF_LO], win[:, FF_HI:]], axis=1),
             wff=jnp.pad(win[:, FF_LO:FF_HI], ((0, 0), (0, 128 - FOX_HEADS))))
    late = _GatherComm([w_branch_a[0].astype(BF16), w_branch_b[0].astype(BF16), w_out[0].astype(BF16),
                        w_up[0].astype(BF16), conv_w[0], w_down[0].astype(BF16)])
    p = dict(norm_mix=norm_mix[0], fox_f_bias=fox_f_bias[0], hg_lb_logits=hg_lb_logits, hg_norm=hg_norm[0],
             norm_ffn=norm_ffn[0], cbg=conv_b[:, :D_FF], cbv=conv_b[:, D_FF:], norm_final=norm_final)
    loss, dx, grads = _local_step(x[0], loss_target[0], w, p, late=late, exchange=True)
    loss = lax.psum(loss[0, 0], ("x", "y", "c"))

    small = _pack_small(dict(
        norm_mix=grads["norm_mix"], fox_f_bias=grads["fox_f_bias"], hg_lb_logits=grads["hg_lb_logits"],
        hg_norm=grads["hg_norm"], norm_ffn=grads["norm_ffn"], conv_b=jnp.concatenate([grads["cbg"], grads["cbv"]], axis=1),
        norm_final=grads["norm_final"]))
    (small_parts,) = _comm_call(_ExchangeComm([jnp.broadcast_to(small[None], (N_DEV, SMALL_ROWS, 128))]),
                                name="exchange_small")
    ea, eb, eo, eup, ed = grads["early_parts"]
    p_in, p_cw = grads["late_parts"]
    parts = [p_in, ea, eb, eo, eup, p_cw, ed, small_parts]
    res = {}
    for (n, shape, tile), part in zip(SHARDED, parts):
        outs = _adamw(part, wv[n].reshape(shape), mv[n].reshape(shape), vv[n].reshape(shape), name="adamw_" + n, T=tile)
        res[n] = [o.reshape(wv[n].shape) for o in outs]
    outs = _adamw(parts[-1], _pack_small(wv), _pack_small(mv), _pack_small(vv), name="adamw_small", T=SMALL_ROWS)
    small = [_unpack_small(o) for o in outs]
    for n, _ in SMALL:
        res[n] = [s[n] for s in small]
    return (loss, dx[None], *[res[n][0] for n in NAMES], *[res[n][1] for n in NAMES],
            *[res[n][2] for n in NAMES], *[res[n][3] for n in NAMES])
```

```python
import jax
import jax.numpy as jnp
from jax import lax
from jax.experimental import pallas as pl
from jax.experimental.pallas import tpu as pltpu

F32 = jnp.float32
BF16 = jnp.bfloat16

D_MODEL = 1024
HG_HEADS = 8
HG_DK = 128
HG_DV = 128
HG_CHUNK = 64
FOX_HEADS = 16
FOX_DH = 64
D_FF = 2816
EPS = 1e-6
N_DEV = 8

ADAM_LR = 0.001
ADAM_B1 = 0.9
ADAM_B2 = 0.999
ADAM_EPS = 1e-08
ADAM_WD = 0.01
ADAM_STEP = 10

VMEM_LIMIT = 56 * 1024 * 1024


def _cparams(sem):
    return pltpu.CompilerParams(dimension_semantics=sem, vmem_limit_bytes=VMEM_LIMIT)


MESH = pl.DeviceIdType.MESH
ANY = pl.BlockSpec(memory_space=pl.ANY)
SMEM = pl.BlockSpec(memory_space=pltpu.SMEM)


class _GatherComm:
    def __init__(self, shards):
        self.inputs = list(shards)
        n = self.n = len(shards)
        self.out_shapes = [jax.ShapeDtypeStruct((N_DEV,) + s.shape, s.dtype) for s in shards]
        self.scratch = [pltpu.SemaphoreType.DMA((n, 7)), pltpu.SemaphoreType.DMA((n, 7)), pltpu.SemaphoreType.DMA((n,))]

    def _parts(self, x_refs, out_refs, sems):
        send_sems, recv_sems, local_sems = sems
        x, y, c = lax.axis_index("x"), lax.axis_index("y"), lax.axis_index("c")
        me, sibling = (x, y, c), (x, y, 1 - c)
        chips = [(1 - x, y), (x, 1 - y), (1 - x, 1 - y)]

        def copy(t, k, block, to, src=None):
            slot = out_refs[t].at[4 * block[0] + 2 * block[1] + block[2]]
            return pltpu.make_async_remote_copy(
                src_ref=slot if src is None else src, dst_ref=slot,
                send_sem=send_sems.at[t, k], recv_sem=recv_sems.at[t, k], device_id=to, device_id_type=MESH)

        mine = [pltpu.make_async_copy(x_refs[t], out_refs[t].at[4 * x + 2 * y + c], local_sems.at[t])
                for t in range(self.n)]
        first = []
        for t in range(self.n):
            first.append(copy(t, 0, me, sibling, src=x_refs[t]))
            first += [copy(t, 1 + j, me, (*chip, c), src=x_refs[t]) for j, chip in enumerate(chips)]
        return c, me, sibling, chips, copy, mine, first

    def start(self, x_refs, out_refs, sems):
        _, _, _, _, _, mine, first = self._parts(x_refs, out_refs, sems)
        for cp in mine + first:
            cp.start()

    def finish(self, x_refs, out_refs, sems):
        c, me, sibling, chips, copy, mine, first = self._parts(x_refs, out_refs, sems)
        passed = []
        for j, chip in enumerate(chips):
            for t in range(self.n):
                copy(t, 1 + j, (*chip, c), me).wait_recv()
                passed.append(copy(t, 4 + j, (*chip, c), sibling))
                passed[-1].start()
        for t in range(self.n):
            copy(t, 0, sibling, me).wait_recv()
            for j, chip in enumerate(chips):
                copy(t, 4 + j, (*chip, 1 - c), me).wait_recv()
        for cp in first + passed:
            cp.wait_send()
        for cp in mine:
            cp.wait()


class _ExchangeComm:
    def __init__(self, blocks):
        self.inputs = list(blocks)
        n = self.n = len(blocks)
        self.out_shapes = [jax.ShapeDtypeStruct(b.shape, b.dtype) for b in blocks]
        self.scratch = [pltpu.SemaphoreType.DMA((n, 7)), pltpu.SemaphoreType.DMA((n, 7)), pltpu.SemaphoreType.DMA((n,))]

    def _parts(self, g_refs, out_refs, sems):
        send_sems, recv_sems, local_sems = sems
        x, y, c = lax.axis_index("x"), lax.axis_index("y"), lax.axis_index("c")
        me = 4 * x + 2 * y + c
        mine = [pltpu.make_async_copy(g_refs[t].at[me], out_refs[t].at[me], local_sems.at[t]) for t in range(self.n)]
        sends, recvs = [], []
        for k in range(1, N_DEV):
            px = 1 - x if k & 4 else x
            py = 1 - y if k & 2 else y
            pc = 1 - c if k & 1 else c
            p = 4 * px + 2 * py + pc
            for t in range(self.n):
                sends.append(pltpu.make_async_remote_copy(
                    src_ref=g_refs[t].at[p], dst_ref=out_refs[t].at[me], send_sem=send_sems.at[t, k - 1],
                    recv_sem=recv_sems.at[t, k - 1], device_id=(px, py, pc), device_id_type=MESH))
                recvs.append(pltpu.make_async_remote_copy(
                    src_ref=g_refs[t].at[p], dst_ref=out_refs[t].at[p], send_sem=send_sems.at[t, k - 1],
                    recv_sem=recv_sems.at[t, k - 1], device_id=(px, py, pc), device_id_type=MESH))
        return mine, sends, recvs

    def start(self, g_refs, out_refs, sems):
        mine, sends, _ = self._parts(g_refs, out_refs, sems)
        for cp in mine + sends:
            cp.start()

    def finish(self, g_refs, out_refs, sems):
        mine, sends, recvs = self._parts(g_refs, out_refs, sems)
        for cp in recvs:
            cp.wait_recv()
        for cp in sends:
            cp.wait_send()
        for cp in mine:
            cp.wait()


def _comm_call(comm, *, name):
    n = comm.n

    def body(*refs):
        comm.start(refs[:n], refs[n:2 * n], refs[2 * n:])
        comm.finish(refs[:n], refs[n:2 * n], refs[2 * n:])

    return pl.pallas_call(body, name=name, in_specs=[ANY] * n, out_specs=[ANY] * n, out_shape=comm.out_shapes,
                          scratch_shapes=comm.scratch)(*comm.inputs)


_DIMS = {
    "nn": (((1,), (0,)), ((), ())),
    "nt": (((1,), (1,)), ((), ())),
    "tn": (((0,), (0,)), ((), ())),
}

MATMUL_VMEM_BUDGET = 36 * 1024 * 1024
MAX_TILE = 1536


def _pick(n, prefs):
    for p in prefs:
        if n % p == 0:
            return p
    return n


def _tile_options(n):
    return [d for d in range(128, min(n, MAX_TILE) + 1, 128) if n % d == 0] or [n]


def _pick_tiles(M, N, tk, nk, sa, sb, so, has_addend, tm, tn):
    best = None
    for cm in ([tm] if tm else _tile_options(M)):
        for cn in ([tn] if tn else _tile_options(N)):
            need = 2 * (cm * tk * sa + tk * cn * sb + cm * cn * so + (cm * cn * 4 if has_addend else 0))
            need += cm * cn * 4 if nk > 1 else 0
            if need <= MATMUL_VMEM_BUDGET and (best is None or cm * cn > best[0] * best[1]
                                               or (cm * cn == best[0] * best[1] and cn > best[1])):
                best = (cm, cn)
    assert best is not None, (M, N, tk)
    return best


def _matmul(a, b, form, *, out_dtype=F32, addend=None, tm=None, tn=None, tk=None, comm=None, name):
    if form == "nn":
        (M, K), (K2, N) = a.shape, b.shape
    elif form == "nt":
        (M, K), (N, K2) = a.shape, b.shape
    else:
        (K, M), (K2, N) = a.shape, b.shape
    assert K == K2, (a.shape, b.shape, form)
    tk = tk or (K if K <= 2816 else _pick(K, (1024, 512, 256, 128)))
    nk = K // tk
    if tm is None or tn is None:
        tm, tn = _pick_tiles(M, N, tk, nk, a.dtype.itemsize, b.dtype.itemsize, jnp.dtype(out_dtype).itemsize,
                             addend is not None, tm, tn)
    assert M % tm == 0 and N % tn == 0 and K % tk == 0, (M, N, K, tm, tn, tk)
    dims = _DIMS[form]
    nc = comm.n if comm is not None else 0
    grid = (M // tm, N // tn, nk)

    def body(*refs):
        a_ref, b_ref = refs[:2]
        pos = 2
        add_ref = refs[pos] if addend is not None else None
        pos += addend is not None
        c_in, o_ref, c_out = refs[pos:pos + nc], refs[pos + nc], refs[pos + nc + 1:pos + 2 * nc + 1]
        pos += 2 * nc + 1
        acc_ref = refs[pos] if nk > 1 else None
        c_sems = refs[pos + (nk > 1):]
        if comm is not None:
            ids = [pl.program_id(d) for d in range(3)]

            @pl.when((ids[0] == 0) & (ids[1] == 0) & (ids[2] == 0))
            def _():
                comm.start(c_in, c_out, c_sems)

        def finish(r):
            if add_ref is not None:
                r = r + add_ref[...].astype(F32)
            o_ref[...] = r.astype(o_ref.dtype)

        part = lax.dot_general(a_ref[...].astype(BF16), b_ref[...].astype(BF16), dims, preferred_element_type=F32)
        if nk == 1:
            finish(part)
        else:
            k = pl.program_id(2)

            @pl.when(k == 0)
            def _():
                acc_ref[...] = part

            @pl.when(k > 0)
            def _():
                acc_ref[...] += part

            @pl.when(k == nk - 1)
            def _():
                finish(acc_ref[...])

        if comm is not None:
            @pl.when((ids[0] == grid[0] - 1) & (ids[1] == grid[1] - 1) & (ids[2] == grid[2] - 1))
            def _():
                comm.finish(c_in, c_out, c_sems)

    if form == "nn":
        a_spec = pl.BlockSpec((tm, tk), lambda i, j, k: (i, k))
        b_spec = pl.BlockSpec((tk, tn), lambda i, j, k: (k, j))
    elif form == "nt":
        a_spec = pl.BlockSpec((tm, tk), lambda i, j, k: (i, k))
        b_spec = pl.BlockSpec((tn, tk), lambda i, j, k: (j, k))
    else:
        a_spec = pl.BlockSpec((tk, tm), lambda i, j, k: (k, i))
        b_spec = pl.BlockSpec((tk, tn), lambda i, j, k: (k, j))
    o_spec = pl.BlockSpec((tm, tn), lambda i, j, k: (i, j))
    in_specs = [a_spec, b_spec] + ([o_spec] if addend is not None else [])
    args = (a, b) + ((addend,) if addend is not None else ())
    out_shape = jax.ShapeDtypeStruct((M, N), out_dtype)
    scratch = [pltpu.VMEM((tm, tn), F32)] if nk > 1 else []
    if comm is None:
        return pl.pallas_call(
            body, name=name, grid=grid, in_specs=in_specs, out_specs=o_spec, out_shape=out_shape,
            scratch_shapes=scratch, compiler_params=_cparams(("parallel", "parallel", "arbitrary")),
        )(*args)
    outs = pl.pallas_call(
        body, name=name, grid=grid, in_specs=in_specs + [ANY] * nc, out_specs=[o_spec] + [ANY] * nc,
        out_shape=[out_shape] + comm.out_shapes, scratch_shapes=scratch + comm.scratch,
        compiler_params=_cparams(("arbitrary", "arbitrary", "arbitrary")),
    )(*args, *comm.inputs)
    return outs[0], outs[1:]


def _rms_fwd(x, g, *, name, tm=512):
    M, D = x.shape
    tm = min(tm, M)

    def body(x_ref, g_ref, n_ref):
        xf = x_ref[...]
        r = lax.rsqrt(jnp.mean(xf * xf, axis=-1, keepdims=True) + EPS)
        n_ref[...] = (xf * r * g_ref[...]).astype(n_ref.dtype)

    return pl.pallas_call(
        body, name=name, grid=(M // tm,),
        in_specs=[pl.BlockSpec((tm, D), lambda i: (i, 0)), pl.BlockSpec((1, D), lambda i: (0, 0))],
        out_specs=pl.BlockSpec((tm, D), lambda i: (i, 0)),
        out_shape=jax.ShapeDtypeStruct((M, D), BF16),
        compiler_params=_cparams(("parallel",)),
    )(x, g.reshape(1, D))


def _rms_bwd(x, g, dn, dres, *, name, tm=512):
    M, D = x.shape
    tm = min(tm, M)

    def body(x_ref, g_ref, dn_ref, dres_ref, dx_ref, dg_ref):
        @pl.when(pl.program_id(0) == 0)
        def _():
            dg_ref[...] = jnp.zeros_like(dg_ref)

        xf = x_ref[...]
        r = lax.rsqrt(jnp.mean(xf * xf, axis=-1, keepdims=True) + EPS)
        xh = xf * r
        dn_ = dn_ref[...].astype(F32)
        dg_ref[...] += jnp.sum(dn_ * xh, axis=0, keepdims=True)
        dxh = dn_ * g_ref[...]
        dx = r * (dxh - xh * jnp.mean(dxh * xh, axis=-1, keepdims=True))
        dx_ref[...] = dres_ref[...] + dx

    row = pl.BlockSpec((tm, D), lambda i: (i, 0))
    vec = pl.BlockSpec((1, D), lambda i: (0, 0))
    return pl.pallas_call(
        body, name=name, grid=(M // tm,),
        in_specs=[row, vec, row, row], out_specs=[row, vec],
        out_shape=[jax.ShapeDtypeStruct((M, D), F32), jax.ShapeDtypeStruct((1, D), F32)],
        compiler_params=_cparams(("arbitrary",)),
    )(x, g.reshape(1, D), dn, dres)


def _loss_head(h, g, tgt, *, name, tm=512):
    M, D = h.shape
    tm = min(tm, M)

    def body(h_ref, g_ref, t_ref, loss_ref, dh_ref, dg_ref):
        @pl.when(pl.program_id(0) == 0)
        def _():
            dg_ref[...] = jnp.zeros_like(dg_ref)
            loss_ref[...] = jnp.zeros_like(loss_ref)

        xf = h_ref[...]
        r = lax.rsqrt(jnp.mean(xf * xf, axis=-1, keepdims=True) + EPS)
        xh = xf * r
        err = xh * g_ref[...] - t_ref[...]
        part = jnp.sum(jnp.mean(err * err, axis=-1, keepdims=True), axis=0, keepdims=True)
        loss_ref[...] += 0.5 * part
        dy = err * (1.0 / D)
        dg_ref[...] += jnp.sum(dy * xh, axis=0, keepdims=True)
        dxh = dy * g_ref[...]
        dh_ref[...] = r * (dxh - xh * jnp.mean(dxh * xh, axis=-1, keepdims=True))

    row = pl.BlockSpec((tm, D), lambda i: (i, 0))
    vec = pl.BlockSpec((1, D), lambda i: (0, 0))
    one = pl.BlockSpec((1, 1), lambda i: (0, 0))
    return pl.pallas_call(
        body, name=name, grid=(M // tm,),
        in_specs=[row, vec, row], out_specs=[one, row, vec],
        out_shape=[jax.ShapeDtypeStruct((1, 1), F32), jax.ShapeDtypeStruct((M, D), F32),
                   jax.ShapeDtypeStruct((1, D), F32)],
        compiler_params=_cparams(("arbitrary",)),
    )(h, g.reshape(1, D), tgt)


HG_MID = HG_CHUNK // 2 - 1
EXP_CAP = 80.0


def _sigmoid(x):
    return 1.0 / (1.0 + jnp.exp(-x))


def _dot(a, b, dims, precision=None):
    return lax.dot_general(a, b, dims, preferred_element_type=F32, precision=precision)


def _bdot(a, b, form):
    return _dot(a.astype(BF16), b.astype(BF16), _DIMS[form])


def _split2(x):
    hi = x.astype(BF16)
    return hi, (x - hi.astype(F32)).astype(BF16)


def _dot3(a, b, form):
    d = _DIMS[form]
    return _dot(a[0], b[0], d) + (_dot(a[0], b[1], d) + _dot(a[1], b[0], d))


def _hgrn_chunk_common(hq, hf, lbv, tril, rid):
    sq = _sigmoid(hq)
    q = hq * sq
    sg = _sigmoid(hf)
    f = lbv + (1.0 - lbv) * sg
    k = (1.0 - lbv) * (1.0 - sg)
    g = jnp.log(f)
    b = _dot(tril, g, _DIMS["nn"], precision=lax.Precision.HIGHEST)
    bref = jnp.sum(jnp.where(rid == HG_MID, b, 0.0), axis=0, keepdims=True)
    bend = jnp.sum(jnp.where(rid == HG_CHUNK - 1, b, 0.0), axis=0, keepdims=True)
    eb = jnp.exp(b)
    e1 = jnp.exp(jnp.minimum(b - bref, EXP_CAP))
    e2 = jnp.exp(jnp.minimum(bref - b, EXP_CAP))
    e3 = jnp.exp(bend - b)
    return sq, q, sg, f, k, bend, eb, e1, e2, e3


def _hgrn_fwd(proj, lb, gnorm, *, name, T=1024):
    S = proj.shape[0]
    T = min(T, S)
    nch = T // HG_CHUNK
    C = HG_CHUNK

    def body(hq_ref, hf_ref, hi_ref, hg_ref, lb_ref, gn_ref, o_ref, oa_ref, st_ref, state):
        @pl.when(pl.program_id(1) == 0)
        def _():
            state[...] = jnp.zeros_like(state)

        lbv = lb_ref[...]
        gn = gn_ref[...]
        row = lax.broadcasted_iota(jnp.int32, (C, C), 0)
        col = lax.broadcasted_iota(jnp.int32, (C, C), 1)
        causal = row >= col
        tril = causal.astype(F32)
        rid = lax.broadcasted_iota(jnp.int32, (C, HG_DK), 0)
        sls = [pl.ds(c * C, C) for c in range(nch)]
        pre = [_hgrn_chunk_common(hq_ref[sl, :], hf_ref[sl, :], lbv, tril, rid) for sl in sls]
        v_l = [hi_ref[sl, :].astype(BF16) for sl in sls]
        a_l, u_l = [], []
        for c in range(nch):
            _, q, _, _, k, _, _, e1, e2, e3 = pre[c]
            a_l.append(jnp.where(causal, _bdot(q * e1, k * e2, "nt"), 0.0))
            u_l.append(_bdot(v_l[c], k * e3, "tn"))
        o_l = [_bdot(a_l[c], v_l[c], "nn") for c in range(nch)]
        st = state[...]
        st_l = []
        for c in range(nch):
            st_l.append(st)
            st = st * jnp.exp(pre[c][5]) + u_l[c]
        state[...] = st
        for c in range(nch):
            st_ref[0, c] = st_l[c]
            o_l[c] = o_l[c] + _bdot(pre[c][1] * pre[c][6], st_l[c], "nt")
        for c in range(nch):
            o, hg = o_l[c], hg_ref[sls[c], :]
            o_ref[sls[c], :] = o
            r = lax.rsqrt(jnp.mean(o * o, axis=-1, keepdims=True) + EPS)
            oa_ref[sls[c], :] = (o * r * gn * (hg * _sigmoid(hg))).astype(oa_ref.dtype)

    def grp(gidx):
        return pl.BlockSpec((T, 128), lambda h, t: (t, gidx * 8 + h))

    return pl.pallas_call(
        body, name=name, grid=(HG_HEADS, S // T),
        in_specs=[grp(0), grp(1), grp(2), grp(3),
                  pl.BlockSpec((1, 128), lambda h, t: (0, h)), pl.BlockSpec((1, 128), lambda h, t: (0, 0))],
        out_specs=[pl.BlockSpec((T, 128), lambda h, t: (t, h)), pl.BlockSpec((T, 128), lambda h, t: (t, h)),
                   pl.BlockSpec((1, nch, HG_DV, HG_DK), lambda h, t: (h, t, 0, 0))],
        out_shape=[jax.ShapeDtypeStruct((S, HG_HEADS * HG_DV), F32), jax.ShapeDtypeStruct((S, HG_HEADS * HG_DV), BF16),
                   jax.ShapeDtypeStruct((HG_HEADS, S // C, HG_DV, HG_DK), F32)],
        scratch_shapes=[pltpu.VMEM((HG_DV, HG_DK), F32)],
        compiler_params=_cparams(("parallel", "arbitrary")),
    )(proj, proj, proj, proj, lb, gnorm)


def _hgrn_bwd(proj, lb, gnorm, o, states, doa, *, name, T=1024):
    S = proj.shape[0]
    T = min(T, S)
    nch = T // HG_CHUNK
    C = HG_CHUNK
    nT = S // T

    def body(hq_ref, hf_ref, hi_ref, hg_ref, lb_ref, gn_ref, o_ref, st_ref, doa_ref,
             dhq_ref, dhf_ref, dhi_ref, dhg_ref, dlb_ref, dgn_ref, dstate):
        @pl.when(pl.program_id(1) == 0)
        def _():
            dstate[...] = jnp.zeros_like(dstate)
            dlb_ref[...] = jnp.zeros_like(dlb_ref)
            dgn_ref[...] = jnp.zeros_like(dgn_ref)

        lbv = lb_ref[...]
        gn = gn_ref[...]
        row = lax.broadcasted_iota(jnp.int32, (C, C), 0)
        col = lax.broadcasted_iota(jnp.int32, (C, C), 1)
        causal = row >= col
        tril = causal.astype(F32)
        triu = (row <= col).astype(F32)
        rid = lax.broadcasted_iota(jnp.int32, (C, HG_DK), 0)
        rng = range(nch)
        sls = [pl.ds(c * C, C) for c in rng]
        pre = [_hgrn_chunk_common(hq_ref[sl, :], hf_ref[sl, :], lbv, tril, rid) for sl in sls]
        do2, dgn_acc = [], jnp.zeros((1, HG_DV), F32)
        for c in rng:
            hg, ov = hg_ref[sls[c], :], o_ref[sls[c], :]
            r = lax.rsqrt(jnp.mean(ov * ov, axis=-1, keepdims=True) + EPS)
            xh = ov * r
            sgg = _sigmoid(hg)
            d_oa = doa_ref[sls[c], :].astype(F32)
            dz = d_oa * (hg * sgg)
            dhg_ref[sls[c], :] = (d_oa * (xh * gn) * (sgg * (1.0 + hg * (1.0 - sgg)))).astype(dhg_ref.dtype)
            dgn_acc = dgn_acc + jnp.sum(dz * xh, axis=0, keepdims=True)
            dxh = dz * gn
            do2.append(_split2(r * (dxh - xh * jnp.mean(dxh * xh, axis=-1, keepdims=True))))
        dgn_ref[0] += dgn_acc
        qi = [pre[c][1] * pre[c][6] for c in rng]
        qp = [pre[c][1] * pre[c][7] for c in rng]
        kp = [pre[c][4] * pre[c][8] for c in rng]
        kend = [pre[c][4] * pre[c][9] for c in rng]
        qi2, qp2, kp2, kend2 = ([_split2(t) for t in lst] for lst in (qi, qp, kp, kend))
        v2 = [_split2(hi_ref[sl, :]) for sl in sls]
        st0 = [st_ref[0, c] for c in rng]
        a_l = [jnp.where(causal, _dot(qp2[c][0], kp2[c][0], _DIMS["nt"]), 0.0).astype(BF16) for c in rng]
        da2 = [_split2(jnp.where(causal, _dot3(do2[c], v2[c], "nt"), 0.0)) for c in rng]
        dqi = [_dot3(do2[c], _split2(st0[c]), "nn") for c in rng]
        w_l = [_dot3(do2[c], qi2[c], "tn") for c in rng]
        ds = dstate[...]
        ds1 = [None] * nch
        for c in reversed(rng):
            ds1[c] = ds
            ds = ds * jnp.exp(pre[c][5]) + w_l[c]
        dstate[...] = ds
        ds12 = [_split2(t) for t in ds1]
        dqp = [_dot3(da2[c], kp2[c], "nn") for c in rng]
        dkp = [_dot3(da2[c], qp2[c], "tn") for c in rng]
        dv = [_dot(a_l[c], do2[c][0], _DIMS["tn"]) + _dot(kend2[c][0], ds12[c][0], _DIMS["nt"]) for c in rng]
        dkend = [_dot3(v2[c], ds12[c], "nn") for c in rng]
        dq_l, dk_l, db_l = [], [], []
        for c in rng:
            _, _, _, _, _, bend, eb, e1, e2, e3 = pre[c]
            dq_l.append(dqi[c] * eb + dqp[c] * e1)
            dk_l.append(dkp[c] * e2 + dkend[c] * e3)
            db = dqi[c] * qi[c] + dqp[c] * qp[c] - dkp[c] * kp[c] - dkend[c] * kend[c]
            dbend = (jnp.sum(dkend[c] * kend[c], axis=0, keepdims=True)
                     + jnp.exp(bend) * jnp.sum(ds1[c] * st0[c], axis=0, keepdims=True))
            db_l.append(db + jnp.where(rid == C - 1, dbend, 0.0))
        dg = [_dot(triu, db_l[c], _DIMS["nn"], precision=lax.Precision.HIGHEST) for c in rng]
        dlb_acc = jnp.zeros((1, HG_DK), F32)
        for c in rng:
            sq, _, sg, f, _, _, _, _, _, _ = pre[c]
            hq = hq_ref[sls[c], :]
            df = dg[c] / f - dk_l[c]
            dlb_acc = dlb_acc + jnp.sum(df * (1.0 - sg), axis=0, keepdims=True)
            dhf_ref[sls[c], :] = (df * (1.0 - lbv) * sg * (1.0 - sg)).astype(dhf_ref.dtype)
            dhq_ref[sls[c], :] = (dq_l[c] * (sq * (1.0 + hq * (1.0 - sq)))).astype(dhq_ref.dtype)
            dhi_ref[sls[c], :] = dv[c].astype(dhi_ref.dtype)
        dlb_ref[...] += dlb_acc

    def grp(gidx):
        return pl.BlockSpec((T, 128), lambda h, t: (nT - 1 - t, gidx * 8 + h))

    tok = pl.BlockSpec((T, 128), lambda h, t: (nT - 1 - t, h))
    big = jax.ShapeDtypeStruct((S, HG_HEADS * HG_DV), BF16)
    return pl.pallas_call(
        body, name=name, grid=(HG_HEADS, nT),
        in_specs=[grp(0), grp(1), grp(2), grp(3),
                  pl.BlockSpec((1, 128), lambda h, t: (0, h)), pl.BlockSpec((1, 128), lambda h, t: (0, 0)),
                  tok, pl.BlockSpec((1, nch, HG_DV, HG_DK), lambda h, t: (h, nT - 1 - t, 0, 0)), tok],
        out_specs=[tok, tok, tok, tok, pl.BlockSpec((1, 128), lambda h, t: (0, h)),
                   pl.BlockSpec((1, 1, 128), lambda h, t: (h, 0, 0))],
        out_shape=[big, big, big, big, jax.ShapeDtypeStruct((1, HG_HEADS * HG_DK), F32),
                   jax.ShapeDtypeStruct((HG_HEADS, 1, HG_DV), F32)],
        scratch_shapes=[pltpu.VMEM((HG_DV, HG_DK), F32)],
        compiler_params=_cparams(("parallel", "arbitrary")),
    )(proj, proj, proj, proj, lb, gnorm, o, states, doa)


def _lb_fwd(logits, *, name):
    def body(l_ref, lb_ref):
        lb_ref[...] = _sigmoid(l_ref[0:1, :] - l_ref[1:2, :])

    return pl.pallas_call(body, name=name, out_shape=jax.ShapeDtypeStruct((1, logits.shape[1]), F32))(logits)


def _lb_bwd(logits, dlb, *, name):
    def body(l_ref, d_ref, o_ref):
        lbv = _sigmoid(l_ref[0:1, :] - l_ref[1:2, :])
        t = d_ref[...] * lbv * (1.0 - lbv)
        o_ref[0:1, :] = t
        o_ref[1:2, :] = -t

    return pl.pallas_call(body, name=name, out_shape=jax.ShapeDtypeStruct(logits.shape, F32))(logits, dlb)


NEG = -1e30
FOX_SCALE = FOX_DH ** -0.5
FOX_PAIRS = FOX_HEADS // 2


def _fox_gate_fwd(ff, bias, *, name, T=512):
    S = ff.shape[0]
    T = min(T, S)

    def body(ff_ref, b_ref, c_ref, carry):
        @pl.when(pl.program_id(0) == 0)
        def _():
            carry[...] = jnp.zeros_like(carry)

        z = ff_ref[...] + b_ref[...]
        logf = jnp.minimum(z, 0.0) - jnp.log(1.0 + jnp.exp(-jnp.abs(z)))
        row = lax.broadcasted_iota(jnp.int32, (T, T), 0)
        col = lax.broadcasted_iota(jnp.int32, (T, T), 1)
        c = _dot((row >= col).astype(F32), logf, _DIMS["nn"], precision=lax.Precision.HIGHEST) + carry[...]
        c_ref[...] = c
        carry[...] = c[T - 1:T, :]

    return pl.pallas_call(
        body, name=name, grid=(S // T,),
        in_specs=[pl.BlockSpec((T, 128), lambda i: (i, 0)), pl.BlockSpec((1, 128), lambda i: (0, 0))],
        out_specs=pl.BlockSpec((T, 128), lambda i: (i, 0)),
        out_shape=jax.ShapeDtypeStruct((S, 128), F32),
        scratch_shapes=[pltpu.VMEM((1, 128), F32)],
        compiler_params=_cparams(("arbitrary",)),
    )(ff, bias)


def _fox_gate_bwd(ff, bias, dcs, *, name, T=512):
    S = ff.shape[0]
    T = min(T, S)
    nT = S // T

    def body(ff_ref, b_ref, d_ref, dff_ref, db_ref, carry):
        @pl.when(pl.program_id(0) == 0)
        def _():
            carry[...] = jnp.zeros_like(carry)
            db_ref[...] = jnp.zeros_like(db_ref)

        row = lax.broadcasted_iota(jnp.int32, (T, T), 0)
        col = lax.broadcasted_iota(jnp.int32, (T, T), 1)
        dlogf = carry[...] - _dot((row <= col).astype(F32), d_ref[...], _DIMS["nn"], precision=lax.Precision.HIGHEST)
        carry[...] = dlogf[0:1, :]
        dff = dlogf * (1.0 - _sigmoid(ff_ref[...] + b_ref[...]))
        dff_ref[...] = dff.astype(dff_ref.dtype)
        db_ref[...] += jnp.sum(dff, axis=0, keepdims=True)

    rev = pl.BlockSpec((T, 128), lambda i: (nT - 1 - i, 0))
    vec = pl.BlockSpec((1, 128), lambda i: (0, 0))
    return pl.pallas_call(
        body, name=name, grid=(nT,),
        in_specs=[rev, vec, rev], out_specs=[rev, vec],
        out_shape=[jax.ShapeDtypeStruct((S, 128), BF16), jax.ShapeDtypeStruct((1, 128), F32)],
        scratch_shapes=[pltpu.VMEM((1, 128), F32)],
        compiler_params=_cparams(("arbitrary",)),
    )(ff, bias, dcs)


AUG = FOX_DH


def _bias_lane(hh):
    return AUG * (1 - hh)


def _data_lanes(lane, hh):
    return (lane < AUG) if hh == 0 else (lane >= AUG)


def _split3(x):
    a = x.astype(BF16).astype(F32)
    r = x - a
    b = r.astype(BF16).astype(F32)
    return a, b, r - b


def _lane_fill(lane, base, pieces, start):
    for i, pc in enumerate(pieces):
        base = jnp.where(lane == start + i, pc, base)
    return base


FOX_TB = 512
FOX_SKIP = 40.0
N_STAT = 4


def _fox_prep(proj, c_tok, *, name):
    S = proj.shape[0]
    T = min(FOX_TB, S)

    def body(q_ref, k_ref, v_ref, c_ref, qa_ref, ka_ref, va_ref, st_ref):
        pair = pl.program_id(0)
        lane = lax.broadcasted_iota(jnp.int32, (T, 128), 1)
        lane1 = lax.broadcasted_iota(jnp.int32, (1, 128), 1)
        c = c_ref[...]
        q, k, v = q_ref[...], k_ref[...], v_ref[...]
        for hh in range(2):
            data, b0 = _data_lanes(lane, hh), _bias_lane(hh)
            ones3 = jnp.where((lane >= b0) & (lane < b0 + 3), 1.0, 0.0)

            def max_norm(t):
                tr = jnp.where(data, t.astype(BF16).astype(F32), 0.0)
                return jnp.sqrt(jnp.max(jnp.sum(tr * tr, axis=-1, keepdims=True), axis=0, keepdims=True))

            ch = jnp.sum(jnp.where(lane == 2 * pair + hh, c, 0.0), axis=-1, keepdims=True)
            c1, c2, c3 = _split3(ch)
            aug_q = _lane_fill(lane, jnp.where((lane >= b0 + 3) & (lane < b0 + 6), 1.0, 0.0), (c1, c2, c3), b0)
            aug_k = _lane_fill(lane, ones3, (-c1, -c2, -c3), b0 + 3)
            qa_ref[hh] = jnp.where(data, q * FOX_SCALE, aug_q).astype(BF16)
            ka_ref[hh] = jnp.where(data, k, aug_k).astype(BF16)
            va_ref[hh] = jnp.where(data, v, ones3).astype(BF16)
            stats = (max_norm(q * FOX_SCALE), jnp.max(ch, axis=0, keepdims=True), max_norm(k),
                     jnp.min(ch, axis=0, keepdims=True))
            st_ref[hh, 0] = _lane_fill(lane1, jnp.zeros((1, 128), F32), stats, 0)

    def grp(g):
        return pl.BlockSpec((T, 128), lambda p, t: (t, g * 8 + p))

    hm = pl.BlockSpec((2, T, 128), lambda p, t: (p, t, 0))
    out = jax.ShapeDtypeStruct((FOX_HEADS, S, 128), BF16)
    return pl.pallas_call(
        body, name=name, grid=(FOX_PAIRS, S // T),
        in_specs=[grp(4), grp(5), grp(6), pl.BlockSpec((T, 128), lambda p, t: (t, 0))],
        out_specs=[hm, hm, hm, pl.BlockSpec((2, 1, 1, 128), lambda p, t: (p, t, 0, 0))],
        out_shape=[out, out, out, jax.ShapeDtypeStruct((FOX_HEADS, S // T, 1, 128), F32)],
        compiler_params=_cparams(("parallel", "parallel")),
    )(proj, proj, proj, c_tok)


def _fox_bound(st_ref, head, nb, qi, ki):
    qb_, kb_ = (head * nb + qi) * N_STAT, (head * nb + ki) * N_STAT
    return st_ref[qb_] * st_ref[kb_ + 2] + st_ref[qb_ + 1] - st_ref[kb_ + 3] + 0.01


def _pair_lanes(lane, a0, a1):
    return jnp.where(lane < AUG, a0, a1)


def _first_live_key(st_ref, head, nb, qi, newest, thr):
    def body(t, k0):
        k = newest - t
        return jnp.where(_fox_bound(st_ref, head, nb, qi, k) > thr, k, k0)

    return lax.fori_loop(0, newest + 1, body, newest + 1)


def _last_live_query(st_ref, lm_ref, head, nb, ki):
    def body(t, i1):
        i = ki + 1 + t
        live = _fox_bound(st_ref, head, nb, i, ki) > lm_ref[head * nb + i] - FOX_SKIP
        return jnp.where(live, i, i1)

    return lax.fori_loop(0, nb - 1 - ki, body, ki)


class _BlockStream:
    def __init__(self, hbm_refs, bufs, sems, pair, tb):
        self.hbm, self.bufs, self.sems, self.pair, self.tb = hbm_refs, bufs, sems, pair, tb

    def _copies(self, blk, slot):
        rows = pl.ds(pl.multiple_of(blk * self.tb, self.tb), self.tb)
        return [pltpu.make_async_copy(h.at[pl.ds(2 * self.pair, 2), rows, :], b.at[slot], self.sems.at[n, slot])
                for n, (h, b) in enumerate(zip(self.hbm, self.bufs))]

    def start(self, blk, slot):
        for cp in self._copies(blk, slot):
            cp.start()

    def wait(self, blk, slot):
        for cp in self._copies(blk, slot):
            cp.wait()


def _fox_fwd(qa, ka, va, bounds, *, name):
    S = qa.shape[1]
    tb = min(FOX_TB, S)
    nb = S // tb

    def body(qa_ref, ka_hbm, va_hbm, st_ref, o_ref, qb_ref, lse_ref, kbuf, vbuf, sems, m_s, acc_s, m_min):
        pair, qi = pl.program_id(0), pl.program_id(1)
        stream = _BlockStream((ka_hbm, va_hbm), (kbuf, vbuf), sems, pair, tb)

        def head_step(hh, slot, masked):
            s = _dot(qa_ref[hh], kbuf[slot, hh], _DIMS["nt"])
            if masked:
                row = lax.broadcasted_iota(jnp.int32, (tb, tb), 0)
                col = lax.broadcasted_iota(jnp.int32, (tb, tb), 1)
                s = jnp.where(col <= row, s, NEG)
            m_old = m_s[hh]
            m_new = jnp.maximum(m_old, jnp.max(s, axis=-1, keepdims=True))
            p = jnp.exp(s - m_new)
            p_hi = p.astype(BF16)
            p_lo = (p - p_hi.astype(F32)).astype(BF16)
            vv = vbuf[slot, hh]
            acc_s[hh] = (jnp.exp(m_old - m_new) * acc_s[hh]
                         + _dot(p_hi, vv, _DIMS["nn"]) + _dot(p_lo, vv, _DIMS["nn"]))
            m_s[hh] = m_new
            m_min[hh] = jnp.min(m_new)

        @pl.when(qi == 0)
        def _():
            stream.start(qi, 0)

        @pl.when(qi > 0)
        def _():
            stream.start(qi - 1, 1)

        m_s[...] = jnp.full_like(m_s, NEG)
        acc_s[...] = jnp.zeros_like(acc_s)
        stream.wait(qi, 0)
        for hh in range(2):
            head_step(hh, 0, True)

        @pl.when(qi > 1)
        def _():
            stream.start(qi - 2, 0)

        @pl.when(qi > 0)
        def _():
            stream.wait(qi - 1, 1)
            for hh in range(2):
                head_step(hh, 1, False)

        k0 = [_first_live_key(st_ref, 2 * pair + hh, nb, qi, qi - 2, m_min[hh] - FOX_SKIP) for hh in range(2)]
        n = qi - 1 - jnp.minimum(k0[0], k0[1])

        @pl.when((qi > 1) & (n == 0))
        def _():
            stream.wait(qi - 2, 0)

        def loop(t, carry):
            k = qi - 2 - t
            slot = t % 2
            stream.wait(k, slot)

            @pl.when(t + 1 < n)
            def _():
                stream.start(k - 1, 1 - slot)

            for hh in range(2):
                @pl.when(k >= k0[hh])
                def _():
                    head_step(hh, slot, False)
            return carry

        lax.fori_loop(0, n, loop, 0)

        @pl.when(qi + 1 < nb)
        def _():
            stream.start(qi + 1, 0)

        lane = lax.broadcasted_iota(jnp.int32, (tb, 128), 1)
        outs = []
        for hh in range(2):
            acc = acc_s[hh]
            b0 = _bias_lane(hh)
            l = acc[:, b0:b0 + 1]
            outs.append(acc / l)
            lse = m_s[hh] + jnp.log(l)
            lse_ref[hh, 0] = jnp.broadcast_to(jnp.min(lse, axis=0, keepdims=True), (1, 128))
            qf = qa_ref[hh].astype(F32)
            cb = qf[:, b0:b0 + 1] + qf[:, b0 + 1:b0 + 2] + qf[:, b0 + 2:b0 + 3] - lse
            qb_ref[hh] = _lane_fill(lane, qf, _split3(cb), b0).astype(BF16)
        o_ref[...] = _pair_lanes(lane, outs[0], outs[1])

    qs = pl.BlockSpec((2, tb, 128), lambda p, i: (p, i, 0))
    return pl.pallas_call(
        body, name=name, grid=(FOX_PAIRS, nb),
        in_specs=[qs, ANY, ANY, SMEM],
        out_specs=[pl.BlockSpec((tb, 128), lambda p, i: (i, p)), qs,
                   pl.BlockSpec((2, 1, 1, 128), lambda p, i: (p, i, 0, 0))],
        out_shape=[jax.ShapeDtypeStruct((S, FOX_HEADS * FOX_DH), F32), jax.ShapeDtypeStruct((FOX_HEADS, S, 128), BF16),
                   jax.ShapeDtypeStruct((FOX_HEADS, nb, 1, 128), F32)],
        scratch_shapes=[pltpu.VMEM((2, 2, tb, 128), BF16), pltpu.VMEM((2, 2, tb, 128), BF16),
                        pltpu.SemaphoreType.DMA((2, 2)), pltpu.VMEM((2, tb, 1), F32), pltpu.VMEM((2, tb, 128), F32),
                        pltpu.SMEM((2,), F32)],
        compiler_params=_cparams(("arbitrary", "arbitrary")),
    )(qa, ka, va, bounds)


def _fox_bwd_prep(o, do, *, name, T=512):
    S = o.shape[0]
    T = min(T, S)

    def body(o_ref, do_ref, dob_ref):
        lane = lax.broadcasted_iota(jnp.int32, (T, 128), 1)
        d = do_ref[...].astype(F32)
        prod = d * o_ref[...]
        for hh in range(2):
            mine = _data_lanes(lane, hh)
            delta = jnp.sum(jnp.where(mine, prod, 0.0), axis=-1, keepdims=True)
            dob_ref[hh] = _lane_fill(lane, jnp.where(mine, d, 0.0), _split3(-delta), _bias_lane(hh)).astype(BF16)

    tok = pl.BlockSpec((T, 128), lambda p, t: (t, p))
    return pl.pallas_call(
        body, name=name, grid=(FOX_PAIRS, S // T),
        in_specs=[tok, tok], out_specs=pl.BlockSpec((2, T, 128), lambda p, t: (p, t, 0)),
        out_shape=jax.ShapeDtypeStruct((FOX_HEADS, S, 128), BF16),
        compiler_params=_cparams(("parallel", "parallel")),
    )(o, do)


def _fox_bwd_dq(qb, ka, va, dob, bounds, lse_min, *, name, comm=None):
    S = qb.shape[1]
    tb = min(FOX_TB, S)
    nb = S // tb
    nc = comm.n if comm is not None else 0

    def body(qb_ref, dob_ref, ka_hbm, va_hbm, st_ref, lm_ref, *rest):
        c_in, (dq_ref, dcs_ref), c_out = rest[:nc], rest[nc:nc + 2], rest[nc + 2:2 * nc + 2]
        kbuf, vbuf, sems, acc_s = rest[2 * nc + 2:2 * nc + 6]
        c_sems = rest[2 * nc + 6:]
        pair, qi = pl.program_id(0), pl.program_id(1)
        if comm is not None:
            @pl.when((pair == 0) & (qi == 0))
            def _():
                comm.start(c_in, c_out, c_sems)

        stream = _BlockStream((ka_hbm, va_hbm), (kbuf, vbuf), sems, pair, tb)
        k0 = [_first_live_key(st_ref, 2 * pair + hh, nb, qi, qi - 1, lm_ref[(2 * pair + hh) * nb + qi] - FOX_SKIP)
              for hh in range(2)]
        n = qi - jnp.minimum(k0[0], k0[1]) + 1

        @pl.when(qi == 0)
        def _():
            stream.start(qi, 0)

        acc_s[...] = jnp.zeros_like(acc_s)
        dcs_ref[...] = jnp.zeros_like(dcs_ref)

        def head_step(hh, slot, k, masked):
            s = _dot(qb_ref[hh], kbuf[slot, hh], _DIMS["nt"])
            if masked:
                row = lax.broadcasted_iota(jnp.int32, (tb, tb), 0)
                col = lax.broadcasted_iota(jnp.int32, (tb, tb), 1)
                s = jnp.where(col <= row, s, NEG)
            ds = jnp.exp(s) * _dot(dob_ref[hh], vbuf[slot, hh], _DIMS["nt"])
            dcs_ref[0, 0, hh:hh + 1, pl.ds(pl.multiple_of(k * tb, tb), tb)] = jnp.sum(ds, axis=0, keepdims=True)
            acc_s[hh] += _dot(ds.astype(BF16), kbuf[slot, hh], _DIMS["nn"])

        def loop(t, carry):
            k = qi - t
            slot = t % 2
            stream.wait(k, slot)

            @pl.when(t + 1 < n)
            def _():
                stream.start(k - 1, 1 - slot)

            @pl.when(t == 0)
            def _():
                for hh in range(2):
                    head_step(hh, slot, k, True)

            for hh in range(2):
                @pl.when((t > 0) & (k >= k0[hh]))
                def _():
                    head_step(hh, slot, k, False)
            return carry

        lax.fori_loop(0, n, loop, 0)

        @pl.when(qi + 1 < nb)
        def _():
            stream.start(qi + 1, 0)

        lane = lax.broadcasted_iota(jnp.int32, (tb, 128), 1)
        dq_ref[...] = (_pair_lanes(lane, acc_s[0], acc_s[1]) * FOX_SCALE).astype(dq_ref.dtype)
        if comm is not None:
            @pl.when((pair == FOX_PAIRS - 1) & (qi == nb - 1))
            def _():
                comm.finish(c_in, c_out, c_sems)

    qs = pl.BlockSpec((2, tb, 128), lambda p, i: (p, i, 0))
    outs = pl.pallas_call(
        body, name=name, grid=(FOX_PAIRS, nb),
        in_specs=[qs, qs, ANY, ANY, SMEM, SMEM] + [ANY] * nc,
        out_specs=[pl.BlockSpec((tb, 128), lambda p, i: (i, p)),
                   pl.BlockSpec((1, 1, 2, S), lambda p, i: (p, i, 0, 0))] + [ANY] * nc,
        out_shape=[jax.ShapeDtypeStruct((S, FOX_HEADS * FOX_DH), BF16),
                   jax.ShapeDtypeStruct((FOX_PAIRS, nb, 2, S), F32)] + (comm.out_shapes if comm is not None else []),
        scratch_shapes=[pltpu.VMEM((2, 2, tb, 128), BF16), pltpu.VMEM((2, 2, tb, 128), BF16),
                        pltpu.SemaphoreType.DMA((2, 2)), pltpu.VMEM((2, tb, 128), F32)]
        + (comm.scratch if comm is not None else []),
        compiler_params=_cparams(("arbitrary", "arbitrary")),
    )(qb, dob, ka, va, bounds, lse_min, *(comm.inputs if comm is not None else []))
    return (outs[0], outs[1]) if comm is None else (outs[0], outs[1], outs[2:])


def _fox_bwd_dkv(qb, ka, va, dob, bounds, lse_min, *, name):
    S = qb.shape[1]
    tb = min(FOX_TB, S)
    nb = S // tb

    def body(ka_ref, va_ref, qb_hbm, dob_hbm, st_ref, lm_ref, dk_ref, dv_ref, qbuf, dbuf, sems, dk_s, dv_s):
        pair, ki = pl.program_id(0), pl.program_id(1)
        stream = _BlockStream((qb_hbm, dob_hbm), (qbuf, dbuf), sems, pair, tb)
        i1 = [_last_live_query(st_ref, lm_ref, 2 * pair + hh, nb, ki) for hh in range(2)]
        n = jnp.maximum(i1[0], i1[1]) - ki + 1

        @pl.when(ki == 0)
        def _():
            stream.start(ki, 0)

        dk_s[...] = jnp.zeros_like(dk_s)
        dv_s[...] = jnp.zeros_like(dv_s)

        def head_step(hh, slot, masked):
            st = _dot(ka_ref[hh], qbuf[slot, hh], _DIMS["nt"])
            if masked:
                row = lax.broadcasted_iota(jnp.int32, (tb, tb), 0)
                col = lax.broadcasted_iota(jnp.int32, (tb, tb), 1)
                st = jnp.where(row <= col, st, NEG)
            pt = jnp.exp(st)
            dst = pt * _dot(va_ref[hh], dbuf[slot, hh], _DIMS["nt"])
            dv_s[hh] += _dot(pt.astype(BF16), dbuf[slot, hh], _DIMS["nn"])
            dk_s[hh] += _dot(dst.astype(BF16), qbuf[slot, hh], _DIMS["nn"])

        def loop(t, carry):
            i = ki + t
            slot = t % 2
            stream.wait(i, slot)

            @pl.when(t + 1 < n)
            def _():
                stream.start(i + 1, 1 - slot)

            @pl.when(t == 0)
            def _():
                for hh in range(2):
                    head_step(hh, slot, True)

            for hh in range(2):
                @pl.when((t > 0) & (i <= i1[hh]))
                def _():
                    head_step(hh, slot, False)
            return carry

        lax.fori_loop(0, n, loop, 0)

        @pl.when(ki + 1 < nb)
        def _():
            stream.start(ki + 1, 0)

        lane = lax.broadcasted_iota(jnp.int32, (tb, 128), 1)
        dk_ref[...] = _pair_lanes(lane, dk_s[0], dk_s[1]).astype(dk_ref.dtype)
        dv_ref[...] = _pair_lanes(lane, dv_s[0], dv_s[1]).astype(dv_ref.dtype)

    ks = pl.BlockSpec((2, tb, 128), lambda p, j: (p, j, 0))
    tok = pl.BlockSpec((tb, 128), lambda p, j: (j, p))
    big = jax.ShapeDtypeStruct((S, FOX_HEADS * FOX_DH), BF16)
    return pl.pallas_call(
        body, name=name, grid=(FOX_PAIRS, nb),
        in_specs=[ks, ks, ANY, ANY, SMEM, SMEM], out_specs=[tok, tok], out_shape=[big, big],
        scratch_shapes=[pltpu.VMEM((2, 2, tb, 128), BF16), pltpu.VMEM((2, 2, tb, 128), BF16),
                        pltpu.SemaphoreType.DMA((2, 2)), pltpu.VMEM((2, tb, 128), F32), pltpu.VMEM((2, tb, 128), F32)],
        compiler_params=_cparams(("arbitrary", "arbitrary")),
    )(ka, va, qb, dob, bounds, lse_min)


def _merge_fwd(proj, pa, pb, *, name, T=512):
    S, D = pa.shape
    T = min(T, S)

    def body(ga_ref, gb_ref, pa_ref, pb_ref, m_ref):
        m_ref[...] = (_sigmoid(ga_ref[...]) * pa_ref[...] + _sigmoid(gb_ref[...]) * pb_ref[...]).astype(m_ref.dtype)

    tok = pl.BlockSpec((T, D), lambda i: (i, 0))
    return pl.pallas_call(
        body, name=name, grid=(S // T,),
        in_specs=[pl.BlockSpec((T, D), lambda i: (i, 7)), pl.BlockSpec((T, D), lambda i: (i, 8)), tok, tok],
        out_specs=tok, out_shape=jax.ShapeDtypeStruct((S, D), BF16),
        compiler_params=_cparams(("parallel",)),
    )(proj, proj, pa, pb)


def _merge_bwd(proj, pa, pb, dm, *, name, T=512):
    S, D = pa.shape
    T = min(T, S)

    def body(ga_ref, gb_ref, pa_ref, pb_ref, dm_ref, dpa_ref, dpb_ref, dga_ref, dgb_ref):
        dm_ = dm_ref[...]
        sa, sb = _sigmoid(ga_ref[...]), _sigmoid(gb_ref[...])
        dpa_ref[...] = (dm_ * sa).astype(BF16)
        dpb_ref[...] = (dm_ * sb).astype(BF16)
        dga_ref[...] = (dm_ * pa_ref[...] * sa * (1.0 - sa)).astype(BF16)
        dgb_ref[...] = (dm_ * pb_ref[...] * sb * (1.0 - sb)).astype(BF16)

    tok = pl.BlockSpec((T, D), lambda i: (i, 0))
    big = jax.ShapeDtypeStruct((S, D), BF16)
    return pl.pallas_call(
        body, name=name, grid=(S // T,),
        in_specs=[pl.BlockSpec((T, D), lambda i: (i, 7)), pl.BlockSpec((T, D), lambda i: (i, 8)), tok, tok, tok],
        out_specs=[tok, tok, tok, tok], out_shape=[big, big, big, big],
        compiler_params=_cparams(("parallel",)),
    )(proj, proj, pa, pb, dm)


INV_SQRT2 = 0.7071067811865476
INV_SQRT2PI = 0.3989422804014327


def _shifted(u, prev, rid):
    m1 = jnp.where(rid == 0, prev[7:8, :], pltpu.roll(u, 1, 0))
    m2 = jnp.where(rid == 0, prev[6:7, :], jnp.where(rid == 1, prev[7:8, :], pltpu.roll(u, 2, 0)))
    return m1, m2


def _conv_acc(u, prev, w_ref, b_ref, rid):
    m1, m2 = _shifted(u, prev, rid)
    return b_ref[...] + w_ref[0:1, :] * m2 + w_ref[1:2, :] * m1 + w_ref[2:3, :] * u, m1, m2


def _convglu_fwd(ug, uv, wg, wv, bg, bv, *, name, T=512, tc=256):
    S, F = ug.shape
    T = min(T, S)

    def body(ug_ref, uv_ref, wg_ref, wv_ref, bg_ref, bv_ref, a_ref, pg, pv):
        @pl.when(pl.program_id(1) == 0)
        def _():
            pg[...] = jnp.zeros_like(pg)
            pv[...] = jnp.zeros_like(pv)

        rid = lax.broadcasted_iota(jnp.int32, (T, tc), 0)
        g_, v_ = ug_ref[...], uv_ref[...]
        accg, _, _ = _conv_acc(g_, pg[...], wg_ref, bg_ref, rid)
        accv, _, _ = _conv_acc(v_, pv[...], wv_ref, bv_ref, rid)
        gel = 0.5 * accg * (1.0 + lax.erf(accg * INV_SQRT2))
        a_ref[...] = (gel * accv).astype(a_ref.dtype)
        pg[...] = g_[T - 8:T, :]
        pv[...] = v_[T - 8:T, :]

    tok = pl.BlockSpec((T, tc), lambda j, t: (t, j))
    w3 = pl.BlockSpec((3, tc), lambda j, t: (0, j))
    b1 = pl.BlockSpec((1, tc), lambda j, t: (0, j))
    return pl.pallas_call(
        body, name=name, grid=(F // tc, S // T),
        in_specs=[tok, tok, w3, w3, b1, b1], out_specs=tok,
        out_shape=jax.ShapeDtypeStruct((S, F), BF16),
        scratch_shapes=[pltpu.VMEM((8, tc), F32), pltpu.VMEM((8, tc), F32)],
        compiler_params=_cparams(("parallel", "arbitrary")),
    )(ug, uv, wg, wv, bg, bv)


def _convglu_bwd(ug, uv, wg, wv, bg, bv, da, *, name, T=512, tc=256):
    S, F = ug.shape
    T = min(T, S)
    nT = S // T
    halo_blocks = T // 8

    def up_shift(d, nx, rid):
        p1 = jnp.where(rid == T - 1, nx[0:1, :], pltpu.roll(d, T - 1, 0))
        p2 = jnp.where(rid == T - 1, nx[1:2, :], jnp.where(rid == T - 2, nx[0:1, :], pltpu.roll(d, T - 2, 0)))
        return p1, p2

    def body(ug_ref, uv_ref, hg_ref, hv_ref, wg_ref, wv_ref, bg_ref, bv_ref, da_ref,
             dug_ref, duv_ref, dwg_ref, dwv_ref, dbg_ref, dbv_ref, ng, nv):
        @pl.when(pl.program_id(1) == 0)
        def _():
            ng[...] = jnp.zeros_like(ng)
            nv[...] = jnp.zeros_like(nv)
            for r in (dwg_ref, dwv_ref, dbg_ref, dbv_ref):
                r[...] = jnp.zeros_like(r)

        first_block = pl.program_id(1) == nT - 1
        rid = lax.broadcasted_iota(jnp.int32, (T, tc), 0)
        g_, v_ = ug_ref[...], uv_ref[...]
        pg = jnp.where(first_block, 0.0, hg_ref[...])
        pv = jnp.where(first_block, 0.0, hv_ref[...])
        accg, g1, g2 = _conv_acc(g_, pg, wg_ref, bg_ref, rid)
        accv, v1, v2 = _conv_acc(v_, pv, wv_ref, bv_ref, rid)
        cdf = 0.5 * (1.0 + lax.erf(accg * INV_SQRT2))
        pdf = INV_SQRT2PI * jnp.exp(-0.5 * accg * accg)
        da_ = da_ref[...].astype(F32)
        dgate = da_ * accv * (cdf + accg * pdf)
        dval = da_ * (accg * cdf)
        dbg_ref[...] += jnp.sum(dgate, axis=0, keepdims=True)
        dbv_ref[...] += jnp.sum(dval, axis=0, keepdims=True)
        for j, (sg_, sv_) in enumerate(((g2, v2), (g1, v1), (g_, v_))):
            dwg_ref[j:j + 1, :] += jnp.sum(dgate * sg_, axis=0, keepdims=True)
            dwv_ref[j:j + 1, :] += jnp.sum(dval * sv_, axis=0, keepdims=True)
        for d, w_ref, nx, out_ref in ((dgate, wg_ref, ng, dug_ref), (dval, wv_ref, nv, duv_ref)):
            p1, p2 = up_shift(d, nx[...], rid)
            out_ref[...] = (w_ref[2:3, :] * d + w_ref[1:2, :] * p1 + w_ref[0:1, :] * p2).astype(out_ref.dtype)
            nx[...] = d[0:8, :]

    tok = pl.BlockSpec((T, tc), lambda j, t: (nT - 1 - t, j))
    halo = pl.BlockSpec((8, tc), lambda j, t: (jnp.maximum((nT - 1 - t) * halo_blocks - 1, 0), j))
    w3 = pl.BlockSpec((3, tc), lambda j, t: (0, j))
    b1 = pl.BlockSpec((1, tc), lambda j, t: (0, j))
    big = jax.ShapeDtypeStruct((S, F), BF16)
    return pl.pallas_call(
        body, name=name, grid=(F // tc, nT),
        in_specs=[tok, tok, halo, halo, w3, w3, b1, b1, tok], out_specs=[tok, tok, w3, w3, b1, b1],
        out_shape=[big, big, jax.ShapeDtypeStruct((3, F), F32), jax.ShapeDtypeStruct((3, F), F32),
                   jax.ShapeDtypeStruct((1, F), F32), jax.ShapeDtypeStruct((1, F), F32)],
        scratch_shapes=[pltpu.VMEM((8, tc), F32), pltpu.VMEM((8, tc), F32)],
        compiler_params=_cparams(("parallel", "arbitrary")),
    )(ug, uv, ug, uv, wg, wv, bg, bv, da)


FF_LO = 7168
IN_SHARD = 1154
FF_DEV, FF_OFF = FF_LO // IN_SHARD, FF_LO % IN_SHARD


def _col_blocks(a, width):
    return jnp.stack([a[:, d * width:(d + 1) * width] for d in range(N_DEV)])


def _w_in_blocks(d_wm, d_wff):
    def block(d):
        lo = d * IN_SHARD
        if d < FF_DEV:
            return d_wm[:, lo:lo + IN_SHARD]
        if d > FF_DEV:
            return d_wm[:, lo - FOX_HEADS:lo - FOX_HEADS + IN_SHARD]
        return jnp.concatenate([d_wm[:, lo:FF_LO], d_wff[:, :FOX_HEADS], d_wm[:, FF_LO:lo + IN_SHARD - FOX_HEADS]], axis=1)

    return jnp.stack([block(d) for d in range(N_DEV)])


def _late_weights(g_a, g_b, g_o, g_up, g_cw, g_d):
    wup = jnp.concatenate([g_up[d] for d in range(N_DEV)], axis=1)
    cw = jnp.concatenate([g_cw[d] for d in range(N_DEV)], axis=1)
    return dict(wa=g_a.reshape(D_MODEL, D_MODEL), wb=g_b.reshape(D_MODEL, D_MODEL), wo=g_o.reshape(D_MODEL, D_MODEL),
                wug=wup[:, :D_FF], wuv=wup[:, D_FF:], cwg=cw[:, :D_FF], cwv=cw[:, D_FF:], wd=g_d.reshape(D_FF, D_MODEL))


def _early_grad_blocks(d_wa, d_wb, d_wo, d_wug, d_wuv, d_wd):
    up = jnp.stack([d_wug[:, d * 704:(d + 1) * 704] for d in range(4)]
                   + [d_wuv[:, d * 704:(d + 1) * 704] for d in range(4)])
    return [d_wa.reshape(N_DEV, 128, D_MODEL), d_wb.reshape(N_DEV, 128, D_MODEL), d_wo.reshape(N_DEV, 128, D_MODEL),
            up, d_wd.reshape(N_DEV, 352, D_MODEL)]


def _local_step(x, tgt, w, p, late=None, exchange=False):
    S = x.shape[0]
    mm = _matmul
    n1 = _rms_fwd(x, p["norm_mix"], name="rms1_fwd")
    if late is None:
        proj = mm(n1, w["wm"], "nn", name="proj_main")
    else:
        proj, gathered = mm(n1, w["wm"], "nn", comm=late, name="proj_main")
        w = {**w, **_late_weights(*gathered)}
    ff = mm(n1, w["wff"], "nn", name="proj_ff")
    lb = _lb_fwd(p["hg_lb_logits"], name="lb_fwd")
    gnorm = p["hg_norm"].reshape(1, HG_DV)
    o_hg, oa, states = _hgrn_fwd(proj, lb, gnorm, name="hgrn_fwd")
    bias = jnp.pad(p["fox_f_bias"].reshape(1, FOX_HEADS), ((0, 0), (0, 128 - FOX_HEADS)))
    c = _fox_gate_fwd(ff, bias, name="fox_gate_fwd")
    qa, ka, va, fox_stats = _fox_prep(proj, c, name="fox_prep")
    bounds = fox_stats[:, :, 0, :N_STAT].reshape(-1)
    ob, qb, lse_stats = _fox_fwd(qa, ka, va, bounds, name="fox_fwd")
    lse_min = lse_stats[:, :, 0, 0].reshape(-1)
    pa = mm(oa, w["wa"], "nn", name="branch_a")
    pb = mm(ob, w["wb"], "nn", name="branch_b")
    merged = _merge_fwd(proj, pa, pb, name="merge_fwd")
    h1 = mm(merged, w["wo"], "nn", addend=x, name="mix_out")
    n2 = _rms_fwd(h1, p["norm_ffn"], name="rms2_fwd")
    ug = mm(n2, w["wug"], "nn", name="up_gate")
    uv = mm(n2, w["wuv"], "nn", name="up_val")
    a = _convglu_fwd(ug, uv, w["cwg"], w["cwv"], p["cbg"], p["cbv"], name="convglu_fwd")
    h2 = mm(a, w["wd"], "nn", addend=h1, name="ffn_down")
    loss, dh2, d_norm_final = _loss_head(h2, p["norm_final"], tgt, name="loss_head")
    da = mm(dh2, w["wd"], "nt", out_dtype=BF16, name="d_act")
    d_wd = mm(a, dh2, "tn", out_dtype=BF16, name="dw_down")
    dug, duv, d_cwg, d_cwv, d_cbg, d_cbv = _convglu_bwd(
        ug, uv, w["cwg"], w["cwv"], p["cbg"], p["cbv"], da, name="convglu_bwd")
    dn2 = mm(dug, w["wug"], "nt", name="dn2_gate")
    dn2 = mm(duv, w["wuv"], "nt", addend=dn2, name="dn2_val")
    d_wug = mm(n2, dug, "tn", out_dtype=BF16, name="dw_up_gate")
    d_wuv = mm(n2, duv, "tn", out_dtype=BF16, name="dw_up_val")
    dh1, d_norm_ffn = _rms_bwd(h1, p["norm_ffn"], dn2, dh2, name="rms2_bwd")
    dmerged = mm(dh1, w["wo"], "nt", name="d_merged")
    d_wo = mm(merged, dh1, "tn", out_dtype=BF16, name="dw_out")
    dpa, dpb, dga, dgb = _merge_bwd(proj, pa, pb, dmerged, name="merge_bwd")
    doa = mm(dpa, w["wa"], "nt", name="d_oa")
    dob = mm(dpb, w["wb"], "nt", out_dtype=BF16, name="d_ob")
    d_wa = mm(oa, dpa, "tn", out_dtype=BF16, name="dw_branch_a")
    d_wb = mm(ob, dpb, "tn", out_dtype=BF16, name="dw_branch_b")
    dhq, dhf, dhi, dhg, dlb, dgn8 = _hgrn_bwd(proj, lb, gnorm, o_hg, states, doa, name="hgrn_bwd")
    d_logits = _lb_bwd(p["hg_lb_logits"], dlb, name="lb_bwd")
    dob_hm = _fox_bwd_prep(ob, dob, name="fox_bwd_prep")
    early_parts = None
    if exchange:
        comm = _ExchangeComm(_early_grad_blocks(d_wa, d_wb, d_wo, d_wug, d_wuv, d_wd))
        dq, dcsp, early_parts = _fox_bwd_dq(qb, ka, va, dob_hm, bounds, lse_min, comm=comm, name="fox_bwd_dq")
    else:
        dq, dcsp = _fox_bwd_dq(qb, ka, va, dob_hm, bounds, lse_min, name="fox_bwd_dq")
    dk, dv = _fox_bwd_dkv(qb, ka, va, dob_hm, bounds, lse_min, name="fox_bwd_dkv")
    dcs = jnp.sum(dcsp, axis=1)
    dcs_tok = jnp.pad(dcs.reshape(FOX_HEADS, S).T, ((0, 0), (0, 128 - FOX_HEADS)))
    dff, dbias = _fox_gate_bwd(ff, bias, dcs_tok, name="fox_gate_bwd")
    dproj = jnp.concatenate([dhq, dhf, dhi, dhg, dq, dk, dv, dga, dgb], axis=1)
    d_wm = mm(n1, dproj, "tn", out_dtype=BF16, name="dw_in_main")
    d_wff = mm(n1, dff, "tn", out_dtype=BF16, name="dw_in_ff")
    dn1 = mm(dff, w["wff"], "nt", name="dn1_ff")
    late_parts = None
    if exchange:
        d_cw = jnp.concatenate([d_cwg, d_cwv], axis=1)
        comm = _ExchangeComm([_w_in_blocks(d_wm, d_wff), _col_blocks(d_cw, 704)])
        dn1, late_parts = mm(dproj, w["wm"], "nt", addend=dn1, comm=comm, name="dn1_main")
    else:
        dn1 = mm(dproj, w["wm"], "nt", addend=dn1, name="dn1_main")
    dx, d_norm_mix = _rms_bwd(x, p["norm_mix"], dn1, dh1, name="rms1_bwd")
    grads = dict(
        wm=d_wm, wff=d_wff, wa=d_wa, wb=d_wb, wo=d_wo, wug=d_wug, wuv=d_wuv, cwg=d_cwg, cwv=d_cwv, wd=d_wd,
        norm_mix=d_norm_mix.reshape(-1), fox_f_bias=dbias[0, :FOX_HEADS], hg_lb_logits=d_logits,
        hg_norm=jnp.sum(dgn8, axis=0).reshape(-1), norm_ffn=d_norm_ffn.reshape(-1), cbg=d_cbg, cbv=d_cbv,
        norm_final=d_norm_final.reshape(-1), early_parts=early_parts, late_parts=late_parts)
    return loss, dx, grads


SMALL = [("norm_mix", (1, D_MODEL)), ("fox_f_bias", (1, FOX_HEADS)), ("hg_lb_logits", (2, HG_HEADS * HG_DK)),
         ("hg_norm", (1, HG_DV)), ("norm_ffn", (1, D_MODEL)), ("conv_b", (1, 2 * D_FF)), ("norm_final", (D_MODEL,))]
SMALL_ROWS = 88
SHARDED = [("w_in", (D_MODEL, 1154), 256), ("w_branch_a", (128, D_MODEL), 128), ("w_branch_b", (128, D_MODEL), 128),
           ("w_out", (128, D_MODEL), 128), ("w_up", (D_MODEL, 704), 256), ("conv_w", (3, 704), 3),
           ("w_down", (352, D_MODEL), 352)]
NAMES = ["norm_mix", "w_in", "fox_f_bias", "hg_lb_logits", "hg_norm", "w_branch_a", "w_branch_b", "w_out",
         "norm_ffn", "w_up", "conv_w", "conv_b", "w_down", "norm_final"]


def _size(shape):
    n = 1
    for s in shape:
        n *= s
    return n


def _adamw(parts, w, m, v, *, name, T):
    R, C = w.shape
    c1 = 1.0 / (1.0 - ADAM_B1 ** ADAM_STEP)
    c2 = 1.0 / (1.0 - ADAM_B2 ** ADAM_STEP)

    def body(p_ref, w_ref, m_ref, v_ref, g_ref, d_ref, nm_ref, nv_ref):
        g = p_ref[0].astype(F32)
        for s in range(1, N_DEV):
            g = g + p_ref[s].astype(F32)
        g_ref[...] = g
        nm = ADAM_B1 * m_ref[...] + (1.0 - ADAM_B1) * g
        nv = ADAM_B2 * v_ref[...] + (1.0 - ADAM_B2) * (g * g)
        nm_ref[...] = nm
        nv_ref[...] = nv
        d_ref[...] = -ADAM_LR * ((nm * c1) / (jnp.sqrt(nv * c2) + ADAM_EPS) + ADAM_WD * w_ref[...])

    blk = pl.BlockSpec((T, C), lambda i: (i, 0))
    out = jax.ShapeDtypeStruct((R, C), F32)
    return pl.pallas_call(
        body, name=name, grid=(R // T,),
        in_specs=[pl.BlockSpec((N_DEV, T, C), lambda i: (0, i, 0)), blk, blk, blk],
        out_specs=[blk, blk, blk, blk], out_shape=[out, out, out, out],
        compiler_params=_cparams(("parallel",)),
    )(parts, w, m, v)


def _pack_small(vals):
    flat = jnp.concatenate([vals[n].reshape(-1).astype(F32) for n, _ in SMALL])
    return jnp.pad(flat, (0, SMALL_ROWS * 128 - flat.shape[0])).reshape(SMALL_ROWS, 128)


def _unpack_small(buf):
    flat, out, off = buf.reshape(-1), {}, 0
    for n, shape in SMALL:
        out[n] = flat[off:off + _size(shape)].reshape(shape)
        off += _size(shape)
    return out


def kernel(x, norm_mix, w_in, fox_f_bias, hg_lb_logits, hg_norm, w_branch_a, w_branch_b, w_out, norm_ffn, w_up, conv_w, conv_b, w_down, norm_final, loss_target, m_norm_mix, m_w_in, m_fox_f_bias, m_hg_lb_logits, m_hg_norm, m_w_branch_a, m_w_branch_b, m_w_out, m_norm_ffn, m_w_up, m_conv_w, m_conv_b, m_w_down, m_norm_final, v_norm_mix, v_w_in, v_fox_f_bias, v_hg_lb_logits, v_hg_norm, v_w_branch_a, v_w_branch_b, v_w_out, v_norm_ffn, v_w_up, v_conv_w, v_conv_b, v_w_down, v_norm_final):
    wv = dict(norm_mix=norm_mix, w_in=w_in, fox_f_bias=fox_f_bias, hg_lb_logits=hg_lb_logits, hg_norm=hg_norm,
              w_branch_a=w_branch_a, w_branch_b=w_branch_b, w_out=w_out, norm_ffn=norm_ffn, w_up=w_up, conv_w=conv_w,
              conv_b=conv_b, w_down=w_down, norm_final=norm_final)
    mv = dict(norm_mix=m_norm_mix, w_in=m_w_in, fox_f_bias=m_fox_f_bias, hg_lb_logits=m_hg_lb_logits, hg_norm=m_hg_norm,
              w_branch_a=m_w_branch_a, w_branch_b=m_w_branch_b, w_out=m_w_out, norm_ffn=m_norm_ffn, w_up=m_w_up,
              conv_w=m_conv_w, conv_b=m_conv_b, w_down=m_w_down, norm_final=m_norm_final)
    vv = dict(norm_mix=v_norm_mix, w_in=v_w_in, fox_f_bias=v_fox_f_bias, hg_lb_logits=v_hg_lb_logits, hg_norm=v_hg_norm,
              w_branch_a=v_w_branch_a, w_branch_b=v_w_branch_b, w_out=v_w_out, norm_ffn=v_norm_ffn, w_up=v_w_up,
              conv_w=v_conv_w, conv_b=v_conv_b, w_down=v_w_down, norm_final=v_norm_final)

    (g_in,) = _comm_call(_GatherComm([w_in[0].astype(BF16)]), name="gather_w_in")
    w = dict(wm=jnp.concatenate([g_in[d] for d in range(FF_DEV)]
                                + [g_in[FF_DEV][:, :FF_OFF], g_in[FF_DEV][:, FF_OFF + FOX_HEADS:]]
                                + [g_in[d] for d in range(FF_DEV + 1, N_DEV)], axis=1),
             wff=jnp.pad(g_in[FF_DEV][:, FF_OFF:FF_OFF + FOX_HEADS], ((0, 0), (0, 128 - FOX_HEADS))))
    late = _GatherComm([w_branch_a[0].astype(BF16), w_branch_b[0].astype(BF16), w_out[0].astype(BF16),
                        w_up[0].astype(BF16), conv_w[0], w_down[0].astype(BF16)])
    p = dict(norm_mix=norm_mix[0], fox_f_bias=fox_f_bias[0], hg_lb_logits=hg_lb_logits, hg_norm=hg_norm[0],
             norm_ffn=norm_ffn[0], cbg=conv_b[:, :D_FF], cbv=conv_b[:, D_FF:], norm_final=norm_final)
    loss, dx, grads = _local_step(x[0], loss_target[0], w, p, late=late, exchange=True)
    loss = lax.psum(loss[0, 0], ("x", "y", "c"))

    small = _pack_small(dict(
        norm_mix=grads["norm_mix"], fox_f_bias=grads["fox_f_bias"], hg_lb_logits=grads["hg_lb_logits"],
        hg_norm=grads["hg_norm"], norm_ffn=grads["norm_ffn"], conv_b=jnp.concatenate([grads["cbg"], grads["cbv"]], axis=1),
        norm_final=grads["norm_final"]))
    (small_parts,) = _comm_call(_ExchangeComm([jnp.broadcast_to(small[None], (N_DEV, SMALL_ROWS, 128))]),
                                name="exchange_small")
    ea, eb, eo, eup, ed = grads["early_parts"]
    p_in, p_cw = grads["late_parts"]
    parts = [p_in, ea, eb, eo, eup, p_cw, ed, small_parts]
    res = {}
    for (n, shape, tile), part in zip(SHARDED, parts):
        outs = _adamw(part, wv[n].reshape(shape), mv[n].reshape(shape), vv[n].reshape(shape), name="adamw_" + n, T=tile)
        res[n] = [o.reshape(wv[n].shape) for o in outs]
    outs = _adamw(parts[-1], _pack_small(wv), _pack_small(mv), _pack_small(vv), name="adamw_small", T=SMALL_ROWS)
    small = [_unpack_small(o) for o in outs]
    for n, _ in SMALL:
        res[n] = [s[n] for s in small]
    return (loss, dx[None], *[res[n][0] for n in NAMES], *[res[n][1] for n in NAMES],
            *[res[n][2] for n in NAMES], *[res[n][3] for n in NAMES])
```

```python
import jax
import jax.numpy as jnp
from jax import lax
from jax.experimental import pallas as pl
from jax.experimental.pallas import tpu as pltpu

F32 = jnp.float32
BF16 = jnp.bfloat16

D_MODEL = 1024
HG_HEADS = 8
HG_DK = 128
HG_DV = 128
HG_CHUNK = 64
FOX_HEADS = 16
FOX_DH = 64
D_FF = 2816
EPS = 1e-6
N_DEV = 8

ADAM_LR = 0.001
ADAM_B1 = 0.9
ADAM_B2 = 0.999
ADAM_EPS = 1e-08
ADAM_WD = 0.01
ADAM_STEP = 10

VMEM_LIMIT = 56 * 1024 * 1024


def _cparams(sem):
    return pltpu.CompilerParams(dimension_semantics=sem, vmem_limit_bytes=VMEM_LIMIT)


MESH = pl.DeviceIdType.MESH
ANY = pl.BlockSpec(memory_space=pl.ANY)
SMEM = pl.BlockSpec(memory_space=pltpu.SMEM)


class _GatherComm:
    def __init__(self, shards):
        self.inputs = list(shards)
        n = self.n = len(shards)
        self.out_shapes = [jax.ShapeDtypeStruct((N_DEV,) + s.shape, s.dtype) for s in shards]
        self.scratch = [pltpu.SemaphoreType.DMA((n, 7)), pltpu.SemaphoreType.DMA((n, 7)), pltpu.SemaphoreType.DMA((n,))]

    def _parts(self, x_refs, out_refs, sems):
        send_sems, recv_sems, local_sems = sems
        x, y, c = lax.axis_index("x"), lax.axis_index("y"), lax.axis_index("c")
        me, sibling = (x, y, c), (x, y, 1 - c)
        chips = [(1 - x, y), (x, 1 - y), (1 - x, 1 - y)]

        def copy(t, k, block, to, src=None):
            slot = out_refs[t].at[4 * block[0] + 2 * block[1] + block[2]]
            return pltpu.make_async_remote_copy(
                src_ref=slot if src is None else src, dst_ref=slot,
                send_sem=send_sems.at[t, k], recv_sem=recv_sems.at[t, k], device_id=to, device_id_type=MESH)

        mine = [pltpu.make_async_copy(x_refs[t], out_refs[t].at[4 * x + 2 * y + c], local_sems.at[t])
                for t in range(self.n)]
        first = []
        for t in range(self.n):
            first.append(copy(t, 0, me, sibling, src=x_refs[t]))
            first += [copy(t, 1 + j, me, (*chip, c), src=x_refs[t]) for j, chip in enumerate(chips)]
        return c, me, sibling, chips, copy, mine, first

    def start(self, x_refs, out_refs, sems):
        _, _, _, _, _, mine, first = self._parts(x_refs, out_refs, sems)
        for cp in mine + first:
            cp.start()

    def finish(self, x_refs, out_refs, sems):
        c, me, sibling, chips, copy, mine, first = self._parts(x_refs, out_refs, sems)
        passed = []
        for j, chip in enumerate(chips):
            for t in range(self.n):
                copy(t, 1 + j, (*chip, c), me).wait_recv()
                passed.append(copy(t, 4 + j, (*chip, c), sibling))
                passed[-1].start()
        for t in range(self.n):
            copy(t, 0, sibling, me).wait_recv()
            for j, chip in enumerate(chips):
                copy(t, 4 + j, (*chip, 1 - c), me).wait_recv()
        for cp in first + passed:
            cp.wait_send()
        for cp in mine:
            cp.wait()


class _ExchangeComm:
    def __init__(self, blocks):
        self.inputs = list(blocks)
        n = self.n = len(blocks)
        self.out_shapes = [jax.ShapeDtypeStruct(b.shape, b.dtype) for b in blocks]
        self.scratch = [pltpu.SemaphoreType.DMA((n, 7)), pltpu.SemaphoreType.DMA((n, 7)), pltpu.SemaphoreType.DMA((n,))]

    def _parts(self, g_refs, out_refs, sems):
        send_sems, recv_sems, local_sems = sems
        x, y, c = lax.axis_index("x"), lax.axis_index("y"), lax.axis_index("c")
        me = 4 * x + 2 * y + c
        mine = [pltpu.make_async_copy(g_refs[t].at[me], out_refs[t].at[me], local_sems.at[t]) for t in range(self.n)]
        sends, recvs = [], []
        for k in range(1, N_DEV):
            px = 1 - x if k & 4 else x
            py = 1 - y if k & 2 else y
            pc = 1 - c if k & 1 else c
            p = 4 * px + 2 * py + pc
            for t in range(self.n):
                sends.append(pltpu.make_async_remote_copy(
                    src_ref=g_refs[t].at[p], dst_ref=out_refs[t].at[me], send_sem=send_sems.at[t, k - 1],
                    recv_sem=recv_sems.at[t, k - 1], device_id=(px, py, pc), device_id_type=MESH))
                recvs.append(pltpu.make_async_remote_copy(
                    src_ref=g_refs[t].at[p], dst_ref=out_refs[t].at[p], send_sem=send_sems.at[t, k - 1],
                    recv_sem=recv_sems.at[t, k - 1], device_id=(px, py, pc), device_id_type=MESH))
        return mine, sends, recvs

    def start(self, g_refs, out_refs, sems):
        mine, sends, _ = self._parts(g_refs, out_refs, sems)
        for cp in mine + sends:
            cp.start()

    def finish(self, g_refs, out_refs, sems):
        mine, sends, recvs = self._parts(g_refs, out_refs, sems)
        for cp in recvs:
            cp.wait_recv()
        for cp in sends:
            cp.wait_send()
        for cp in mine:
            cp.wait()


def _comm_call(comm, *, name):
    n = comm.n

    def body(*refs):
        comm.start(refs[:n], refs[n:2 * n], refs[2 * n:])
        comm.finish(refs[:n], refs[n:2 * n], refs[2 * n:])

    return pl.pallas_call(body, name=name, in_specs=[ANY] * n, out_specs=[ANY] * n, out_shape=comm.out_shapes,
                          scratch_shapes=comm.scratch)(*comm.inputs)


_DIMS = {
    "nn": (((1,), (0,)), ((), ())),
    "nt": (((1,), (1,)), ((), ())),
    "tn": (((0,), (0,)), ((), ())),
}

MATMUL_VMEM_BUDGET = 36 * 1024 * 1024
MAX_TILE = 1536


def _pick(n, prefs):
    for p in prefs:
        if n % p == 0:
            return p
    return n


def _tile_options(n):
    return [d for d in range(128, min(n, MAX_TILE) + 1, 128) if n % d == 0] or [n]


def _pick_tiles(M, N, tk, nk, sa, sb, so, has_addend, tm, tn):
    best = None
    for cm in ([tm] if tm else _tile_options(M)):
        for cn in ([tn] if tn else _tile_options(N)):
            need = 2 * (cm * tk * sa + tk * cn * sb + cm * cn * so + (cm * cn * 4 if has_addend else 0))
            need += cm * cn * 4 if nk > 1 else 0
            if need <= MATMUL_VMEM_BUDGET and (best is None or cm * cn > best[0] * best[1]
                                               or (cm * cn == best[0] * best[1] and cn > best[1])):
                best = (cm, cn)
    assert best is not None, (M, N, tk)
    return best


def _matmul(a, b, form, *, out_dtype=F32, addend=None, tm=None, tn=None, tk=None, comm=None, name):
    if form == "nn":
        (M, K), (K2, N) = a.shape, b.shape
    elif form == "nt":
        (M, K), (N, K2) = a.shape, b.shape
    else:
        (K, M), (K2, N) = a.shape, b.shape
    assert K == K2, (a.shape, b.shape, form)
    tk = tk or (K if K <= 2816 else _pick(K, (1024, 512, 256, 128)))
    nk = K // tk
    if tm is None or tn is None:
        tm, tn = _pick_tiles(M, N, tk, nk, a.dtype.itemsize, b.dtype.itemsize, jnp.dtype(out_dtype).itemsize,
                             addend is not None, tm, tn)
    assert M % tm == 0 and N % tn == 0 and K % tk == 0, (M, N, K, tm, tn, tk)
    dims = _DIMS[form]
    nc = comm.n if comm is not None else 0
    grid = (M // tm, N // tn, nk)

    def body(*refs):
        a_ref, b_ref = refs[:2]
        pos = 2
        add_ref = refs[pos] if addend is not None else None
        pos += addend is not None
        c_in, o_ref, c_out = refs[pos:pos + nc], refs[pos + nc], refs[pos + nc + 1:pos + 2 * nc + 1]
        pos += 2 * nc + 1
        acc_ref = refs[pos] if nk > 1 else None
        c_sems = refs[pos + (nk > 1):]
        if comm is not None:
            ids = [pl.program_id(d) for d in range(3)]

            @pl.when((ids[0] == 0) & (ids[1] == 0) & (ids[2] == 0))
            def _():
                comm.start(c_in, c_out, c_sems)

        def finish(r):
            if add_ref is not None:
                r = r + add_ref[...].astype(F32)
            o_ref[...] = r.astype(o_ref.dtype)

        part = lax.dot_general(a_ref[...].astype(BF16), b_ref[...].astype(BF16), dims, preferred_element_type=F32)
        if nk == 1:
            finish(part)
        else:
            k = pl.program_id(2)

            @pl.when(k == 0)
            def _():
                acc_ref[...] = part

            @pl.when(k > 0)
            def _():
                acc_ref[...] += part

            @pl.when(k == nk - 1)
            def _():
                finish(acc_ref[...])

        if comm is not None:
            @pl.when((ids[0] == grid[0] - 1) & (ids[1] == grid[1] - 1) & (ids[2] == grid[2] - 1))
            def _():
                comm.finish(c_in, c_out, c_sems)

    if form == "nn":
        a_spec = pl.BlockSpec((tm, tk), lambda i, j, k: (i, k))
        b_spec = pl.BlockSpec((tk, tn), lambda i, j, k: (k, j))
    elif form == "nt":
        a_spec = pl.BlockSpec((tm, tk), lambda i, j, k: (i, k))
        b_spec = pl.BlockSpec((tn, tk), lambda i, j, k: (j, k))
    else:
        a_spec = pl.BlockSpec((tk, tm), lambda i, j, k: (k, i))
        b_spec = pl.BlockSpec((tk, tn), lambda i, j, k: (k, j))
    o_spec = pl.BlockSpec((tm, tn), lambda i, j, k: (i, j))
    in_specs = [a_spec, b_spec] + ([o_spec] if addend is not None else [])
    args = (a, b) + ((addend,) if addend is not None else ())
    out_shape = jax.ShapeDtypeStruct((M, N), out_dtype)
    scratch = [pltpu.VMEM((tm, tn), F32)] if nk > 1 else []
    if comm is None:
        return pl.pallas_call(
            body, name=name, grid=grid, in_specs=in_specs, out_specs=o_spec, out_shape=out_shape,
            scratch_shapes=scratch, compiler_params=_cparams(("parallel", "parallel", "arbitrary")),
        )(*args)
    outs = pl.pallas_call(
        body, name=name, grid=grid, in_specs=in_specs + [ANY] * nc, out_specs=[o_spec] + [ANY] * nc,
        out_shape=[out_shape] + comm.out_shapes, scratch_shapes=scratch + comm.scratch,
        compiler_params=_cparams(("arbitrary", "arbitrary", "arbitrary")),
    )(*args, *comm.inputs)
    return outs[0], outs[1:]


def _rms_fwd(x, g, *, name, tm=512):
    M, D = x.shape
    tm = min(tm, M)

    def body(x_ref, g_ref, n_ref):
        xf = x_ref[...]
        r = lax.rsqrt(jnp.mean(xf * xf, axis=-1, keepdims=True) + EPS)
        n_ref[...] = (xf * r * g_ref[...]).astype(n_ref.dtype)

    return pl.pallas_call(
        body, name=name, grid=(M // tm,),
        in_specs=[pl.BlockSpec((tm, D), lambda i: (i, 0)), pl.BlockSpec((1, D), lambda i: (0, 0))],
        out_specs=pl.BlockSpec((tm, D), lambda i: (i, 0)),
        out_shape=jax.ShapeDtypeStruct((M, D), BF16),
        compiler_params=_cparams(("parallel",)),
    )(x, g.reshape(1, D))


def _rms_bwd(x, g, dn, dres, *, name, tm=512):
    M, D = x.shape
    tm = min(tm, M)

    def body(x_ref, g_ref, dn_ref, dres_ref, dx_ref, dg_ref):
        @pl.when(pl.program_id(0) == 0)
        def _():
            dg_ref[...] = jnp.zeros_like(dg_ref)

        xf = x_ref[...]
        r = lax.rsqrt(jnp.mean(xf * xf, axis=-1, keepdims=True) + EPS)
        xh = xf * r
        dn_ = dn_ref[...].astype(F32)
        dg_ref[...] += jnp.sum(dn_ * xh, axis=0, keepdims=True)
        dxh = dn_ * g_ref[...]
        dx = r * (dxh - xh * jnp.mean(dxh * xh, axis=-1, keepdims=True))
        dx_ref[...] = dres_ref[...] + dx

    row = pl.BlockSpec((tm, D), lambda i: (i, 0))
    vec = pl.BlockSpec((1, D), lambda i: (0, 0))
    return pl.pallas_call(
        body, name=name, grid=(M // tm,),
        in_specs=[row, vec, row, row], out_specs=[row, vec],
        out_shape=[jax.ShapeDtypeStruct((M, D), F32), jax.ShapeDtypeStruct((1, D), F32)],
        compiler_params=_cparams(("arbitrary",)),
    )(x, g.reshape(1, D), dn, dres)


def _loss_head(h, g, tgt, *, name, tm=512):
    M, D = h.shape
    tm = min(tm, M)

    def body(h_ref, g_ref, t_ref, loss_ref, dh_ref, dg_ref):
        @pl.when(pl.program_id(0) == 0)
        def _():
            dg_ref[...] = jnp.zeros_like(dg_ref)
            loss_ref[...] = jnp.zeros_like(loss_ref)

        xf = h_ref[...]
        r = lax.rsqrt(jnp.mean(xf * xf, axis=-1, keepdims=True) + EPS)
        xh = xf * r
        err = xh * g_ref[...] - t_ref[...]
        part = jnp.sum(jnp.mean(err * err, axis=-1, keepdims=True), axis=0, keepdims=True)
        loss_ref[...] += 0.5 * part
        dy = err * (1.0 / D)
        dg_ref[...] += jnp.sum(dy * xh, axis=0, keepdims=True)
        dxh = dy * g_ref[...]
        dh_ref[...] = r * (dxh - xh * jnp.mean(dxh * xh, axis=-1, keepdims=True))

    row = pl.BlockSpec((tm, D), lambda i: (i, 0))
    vec = pl.BlockSpec((1, D), lambda i: (0, 0))
    one = pl.BlockSpec((1, 1), lambda i: (0, 0))
    return pl.pallas_call(
        body, name=name, grid=(M // tm,),
        in_specs=[row, vec, row], out_specs=[one, row, vec],
        out_shape=[jax.ShapeDtypeStruct((1, 1), F32), jax.ShapeDtypeStruct((M, D), F32),
                   jax.ShapeDtypeStruct((1, D), F32)],
        compiler_params=_cparams(("arbitrary",)),
    )(h, g.reshape(1, D), tgt)


HG_MID = HG_CHUNK // 2 - 1
EXP_CAP = 80.0


def _sigmoid(x):
    return 1.0 / (1.0 + jnp.exp(-x))


def _dot(a, b, dims, precision=None):
    return lax.dot_general(a, b, dims, preferred_element_type=F32, precision=precision)


def _bdot(a, b, form):
    return _dot(a.astype(BF16), b.astype(BF16), _DIMS[form])


def _split2(x):
    hi = x.astype(BF16)
    return hi, (x - hi.astype(F32)).astype(BF16)


def _dot3(a, b, form):
    d = _DIMS[form]
    return _dot(a[0], b[0], d) + (_dot(a[0], b[1], d) + _dot(a[1], b[0], d))


def _hgrn_chunk_common(hq, hf, lbv, tril, rid):
    sq = _sigmoid(hq)
    q = hq * sq
    sg = _sigmoid(hf)
    f = lbv + (1.0 - lbv) * sg
    k = (1.0 - lbv) * (1.0 - sg)
    g = jnp.log(f)
    b = _dot(tril, g, _DIMS["nn"], precision=lax.Precision.HIGHEST)
    bref = jnp.sum(jnp.where(rid == HG_MID, b, 0.0), axis=0, keepdims=True)
    bend = jnp.sum(jnp.where(rid == HG_CHUNK - 1, b, 0.0), axis=0, keepdims=True)
    eb = jnp.exp(b)
    e1 = jnp.exp(jnp.minimum(b - bref, EXP_CAP))
    e2 = jnp.exp(jnp.minimum(bref - b, EXP_CAP))
    e3 = jnp.exp(bend - b)
    return sq, q, sg, f, k, bend, eb, e1, e2, e3


def _hgrn_fwd(proj, lb, gnorm, *, name, T=1024):
    S = proj.shape[0]
    T = min(T, S)
    nch = T // HG_CHUNK
    C = HG_CHUNK

    def body(hq_ref, hf_ref, hi_ref, hg_ref, lb_ref, gn_ref, o_ref, oa_ref, st_ref, state):
        @pl.when(pl.program_id(1) == 0)
        def _():
            state[...] = jnp.zeros_like(state)

        lbv = lb_ref[...]
        gn = gn_ref[...]
        row = lax.broadcasted_iota(jnp.int32, (C, C), 0)
        col = lax.broadcasted_iota(jnp.int32, (C, C), 1)
        causal = row >= col
        tril = causal.astype(F32)
        rid = lax.broadcasted_iota(jnp.int32, (C, HG_DK), 0)
        sls = [pl.ds(c * C, C) for c in range(nch)]
        pre = [_hgrn_chunk_common(hq_ref[sl, :], hf_ref[sl, :], lbv, tril, rid) for sl in sls]
        v_l = [hi_ref[sl, :].astype(BF16) for sl in sls]
        a_l, u_l = [], []
        for c in range(nch):
            _, q, _, _, k, _, _, e1, e2, e3 = pre[c]
            a_l.append(jnp.where(causal, _bdot(q * e1, k * e2, "nt"), 0.0))
            u_l.append(_bdot(v_l[c], k * e3, "tn"))
        o_l = [_bdot(a_l[c], v_l[c], "nn") for c in range(nch)]
        st = state[...]
        st_l = []
        for c in range(nch):
            st_l.append(st)
            st = st * jnp.exp(pre[c][5]) + u_l[c]
        state[...] = st
        for c in range(nch):
            st_ref[0, c] = st_l[c]
            o_l[c] = o_l[c] + _bdot(pre[c][1] * pre[c][6], st_l[c], "nt")
        for c in range(nch):
            o, hg = o_l[c], hg_ref[sls[c], :]
            o_ref[sls[c], :] = o
            r = lax.rsqrt(jnp.mean(o * o, axis=-1, keepdims=True) + EPS)
            oa_ref[sls[c], :] = (o * r * gn * (hg * _sigmoid(hg))).astype(oa_ref.dtype)

    def grp(gidx):
        return pl.BlockSpec((T, 128), lambda h, t: (t, gidx * 8 + h))

    return pl.pallas_call(
        body, name=name, grid=(HG_HEADS, S // T),
        in_specs=[grp(0), grp(1), grp(2), grp(3),
                  pl.BlockSpec((1, 128), lambda h, t: (0, h)), pl.BlockSpec((1, 128), lambda h, t: (0, 0))],
        out_specs=[pl.BlockSpec((T, 128), lambda h, t: (t, h)), pl.BlockSpec((T, 128), lambda h, t: (t, h)),
                   pl.BlockSpec((1, nch, HG_DV, HG_DK), lambda h, t: (h, t, 0, 0))],
        out_shape=[jax.ShapeDtypeStruct((S, HG_HEADS * HG_DV), F32), jax.ShapeDtypeStruct((S, HG_HEADS * HG_DV), BF16),
                   jax.ShapeDtypeStruct((HG_HEADS, S // C, HG_DV, HG_DK), F32)],
        scratch_shapes=[pltpu.VMEM((HG_DV, HG_DK), F32)],
        compiler_params=_cparams(("parallel", "arbitrary")),
    )(proj, proj, proj, proj, lb, gnorm)


def _hgrn_bwd(proj, lb, gnorm, o, states, doa, *, name, T=1024):
    S = proj.shape[0]
    T = min(T, S)
    nch = T // HG_CHUNK
    C = HG_CHUNK
    nT = S // T

    def body(hq_ref, hf_ref, hi_ref, hg_ref, lb_ref, gn_ref, o_ref, st_ref, doa_ref,
             dhq_ref, dhf_ref, dhi_ref, dhg_ref, dlb_ref, dgn_ref, dstate):
        @pl.when(pl.program_id(1) == 0)
        def _():
            dstate[...] = jnp.zeros_like(dstate)
            dlb_ref[...] = jnp.zeros_like(dlb_ref)
            dgn_ref[...] = jnp.zeros_like(dgn_ref)

        lbv = lb_ref[...]
        gn = gn_ref[...]
        row = lax.broadcasted_iota(jnp.int32, (C, C), 0)
        col = lax.broadcasted_iota(jnp.int32, (C, C), 1)
        causal = row >= col
        tril = causal.astype(F32)
        triu = (row <= col).astype(F32)
        rid = lax.broadcasted_iota(jnp.int32, (C, HG_DK), 0)
        rng = range(nch)
        sls = [pl.ds(c * C, C) for c in rng]
        pre = [_hgrn_chunk_common(hq_ref[sl, :], hf_ref[sl, :], lbv, tril, rid) for sl in sls]
        do2, dgn_acc = [], jnp.zeros((1, HG_DV), F32)
        for c in rng:
            hg, ov = hg_ref[sls[c], :], o_ref[sls[c], :]
            r = lax.rsqrt(jnp.mean(ov * ov, axis=-1, keepdims=True) + EPS)
            xh = ov * r
            sgg = _sigmoid(hg)
            d_oa = doa_ref[sls[c], :].astype(F32)
            dz = d_oa * (hg * sgg)
            dhg_ref[sls[c], :] = (d_oa * (xh * gn) * (sgg * (1.0 + hg * (1.0 - sgg)))).astype(dhg_ref.dtype)
            dgn_acc = dgn_acc + jnp.sum(dz * xh, axis=0, keepdims=True)
            dxh = dz * gn
            do2.append(_split2(r * (dxh - xh * jnp.mean(dxh * xh, axis=-1, keepdims=True))))
        dgn_ref[0] += dgn_acc
        qi = [pre[c][1] * pre[c][6] for c in rng]
        qp = [pre[c][1] * pre[c][7] for c in rng]
        kp = [pre[c][4] * pre[c][8] for c in rng]
        kend = [pre[c][4] * pre[c][9] for c in rng]
        qi2, qp2, kp2, kend2 = ([_split2(t) for t in lst] for lst in (qi, qp, kp, kend))
        v2 = [_split2(hi_ref[sl, :]) for sl in sls]
        st0 = [st_ref[0, c] for c in rng]
        a_l = [jnp.where(causal, _dot(qp2[c][0], kp2[c][0], _DIMS["nt"]), 0.0).astype(BF16) for c in rng]
        da2 = [_split2(jnp.where(causal, _dot3(do2[c], v2[c], "nt"), 0.0)) for c in rng]
        dqi = [_dot3(do2[c], _split2(st0[c]), "nn") for c in rng]
        w_l = [_dot3(do2[c], qi2[c], "tn") for c in rng]
        ds = dstate[...]
        ds1 = [None] * nch
        for c in reversed(rng):
            ds1[c] = ds
            ds = ds * jnp.exp(pre[c][5]) + w_l[c]
        dstate[...] = ds
        ds12 = [_split2(t) for t in ds1]
        dqp = [_dot3(da2[c], kp2[c], "nn") for c in rng]
        dkp = [_dot3(da2[c], qp2[c], "tn") for c in rng]
        dv = [_dot(a_l[c], do2[c][0], _DIMS["tn"]) + _dot(kend2[c][0], ds12[c][0], _DIMS["nt"]) for c in rng]
        dkend = [_dot3(v2[c], ds12[c], "nn") for c in rng]
        dq_l, dk_l, db_l = [], [], []
        for c in rng:
            _, _, _, _, _, bend, eb, e1, e2, e3 = pre[c]
            dq_l.append(dqi[c] * eb + dqp[c] * e1)
            dk_l.append(dkp[c] * e2 + dkend[c] * e3)
            db = dqi[c] * qi[c] + dqp[c] * qp[c] - dkp[c] * kp[c] - dkend[c] * kend[c]
            dbend = (jnp.sum(dkend[c] * kend[c], axis=0, keepdims=True)
                     + jnp.exp(bend) * jnp.sum(ds1[c] * st0[c], axis=0, keepdims=True))
            db_l.append(db + jnp.where(rid == C - 1, dbend, 0.0))
        dg = [_dot(triu, db_l[c], _DIMS["nn"], precision=lax.Precision.HIGHEST) for c in rng]
        dlb_acc = jnp.zeros((1, HG_DK), F32)
        for c in rng:
            sq, _, sg, f, _, _, _, _, _, _ = pre[c]
            hq = hq_ref[sls[c], :]
            df = dg[c] / f - dk_l[c]
            dlb_acc = dlb_acc + jnp.sum(df * (1.0 - sg), axis=0, keepdims=True)
            dhf_ref[sls[c], :] = (df * (1.0 - lbv) * sg * (1.0 - sg)).astype(dhf_ref.dtype)
            dhq_ref[sls[c], :] = (dq_l[c] * (sq * (1.0 + hq * (1.0 - sq)))).astype(dhq_ref.dtype)
            dhi_ref[sls[c], :] = dv[c].astype(dhi_ref.dtype)
        dlb_ref[...] += dlb_acc

    def grp(gidx):
        return pl.BlockSpec((T, 128), lambda h, t: (nT - 1 - t, gidx * 8 + h))

    tok = pl.BlockSpec((T, 128), lambda h, t: (nT - 1 - t, h))
    big = jax.ShapeDtypeStruct((S, HG_HEADS * HG_DV), BF16)
    return pl.pallas_call(
        body, name=name, grid=(HG_HEADS, nT),
        in_specs=[grp(0), grp(1), grp(2), grp(3),
                  pl.BlockSpec((1, 128), lambda h, t: (0, h)), pl.BlockSpec((1, 128), lambda h, t: (0, 0)),
                  tok, pl.BlockSpec((1, nch, HG_DV, HG_DK), lambda h, t: (h, nT - 1 - t, 0, 0)), tok],
        out_specs=[tok, tok, tok, tok, pl.BlockSpec((1, 128), lambda h, t: (0, h)),
                   pl.BlockSpec((1, 1, 128), lambda h, t: (h, 0, 0))],
        out_shape=[big, big, big, big, jax.ShapeDtypeStruct((1, HG_HEADS * HG_DK), F32),
                   jax.ShapeDtypeStruct((HG_HEADS, 1, HG_DV), F32)],
        scratch_shapes=[pltpu.VMEM((HG_DV, HG_DK), F32)],
        compiler_params=_cparams(("parallel", "arbitrary")),
    )(proj, proj, proj, proj, lb, gnorm, o, states, doa)


def _lb_fwd(logits, *, name):
    def body(l_ref, lb_ref):
        lb_ref[...] = _sigmoid(l_ref[0:1, :] - l_ref[1:2, :])

    return pl.pallas_call(body, name=name, out_shape=jax.ShapeDtypeStruct((1, logits.shape[1]), F32))(logits)


def _lb_bwd(logits, dlb, *, name):
    def body(l_ref, d_ref, o_ref):
        lbv = _sigmoid(l_ref[0:1, :] - l_ref[1:2, :])
        t = d_ref[...] * lbv * (1.0 - lbv)
        o_ref[0:1, :] = t
        o_ref[1:2, :] = -t

    return pl.pallas_call(body, name=name, out_shape=jax.ShapeDtypeStruct(logits.shape, F32))(logits, dlb)


NEG = -1e30
FOX_SCALE = FOX_DH ** -0.5
FOX_PAIRS = FOX_HEADS // 2


def _fox_gate_fwd(ff, bias, *, name, T=512):
    S = ff.shape[0]
    T = min(T, S)

    def body(ff_ref, b_ref, c_ref, carry):
        @pl.when(pl.program_id(0) == 0)
        def _():
            carry[...] = jnp.zeros_like(carry)

        z = ff_ref[...] + b_ref[...]
        logf = jnp.minimum(z, 0.0) - jnp.log(1.0 + jnp.exp(-jnp.abs(z)))
        row = lax.broadcasted_iota(jnp.int32, (T, T), 0)
        col = lax.broadcasted_iota(jnp.int32, (T, T), 1)
        c = _dot((row >= col).astype(F32), logf, _DIMS["nn"], precision=lax.Precision.HIGHEST) + carry[...]
        c_ref[...] = c
        carry[...] = c[T - 1:T, :]

    return pl.pallas_call(
        body, name=name, grid=(S // T,),
        in_specs=[pl.BlockSpec((T, 128), lambda i: (i, 0)), pl.BlockSpec((1, 128), lambda i: (0, 0))],
        out_specs=pl.BlockSpec((T, 128), lambda i: (i, 0)),
        out_shape=jax.ShapeDtypeStruct((S, 128), F32),
        scratch_shapes=[pltpu.VMEM((1, 128), F32)],
        compiler_params=_cparams(("arbitrary",)),
    )(ff, bias)


def _fox_gate_bwd(ff, bias, dcs, *, name, T=512):
    S = ff.shape[0]
    T = min(T, S)
    nT = S // T

    def body(ff_ref, b_ref, d_ref, dff_ref, db_ref, carry):
        @pl.when(pl.program_id(0) == 0)
        def _():
            carry[...] = jnp.zeros_like(carry)
            db_ref[...] = jnp.zeros_like(db_ref)

        row = lax.broadcasted_iota(jnp.int32, (T, T), 0)
        col = lax.broadcasted_iota(jnp.int32, (T, T), 1)
        dlogf = carry[...] - _dot((row <= col).astype(F32), d_ref[...], _DIMS["nn"], precision=lax.Precision.HIGHEST)
        carry[...] = dlogf[0:1, :]
        dff = dlogf * (1.0 - _sigmoid(ff_ref[...] + b_ref[...]))
        dff_ref[...] = dff.astype(dff_ref.dtype)
        db_ref[...] += jnp.sum(dff, axis=0, keepdims=True)

    rev = pl.BlockSpec((T, 128), lambda i: (nT - 1 - i, 0))
    vec = pl.BlockSpec((1, 128), lambda i: (0, 0))
    return pl.pallas_call(
        body, name=name, grid=(nT,),
        in_specs=[rev, vec, rev], out_specs=[rev, vec],
        out_shape=[jax.ShapeDtypeStruct((S, 128), BF16), jax.ShapeDtypeStruct((1, 128), F32)],
        scratch_shapes=[pltpu.VMEM((1, 128), F32)],
        compiler_params=_cparams(("arbitrary",)),
    )(ff, bias, dcs)


AUG = FOX_DH


def _bias_lane(hh):
    return AUG * (1 - hh)


def _data_lanes(lane, hh):
    return (lane < AUG) if hh == 0 else (lane >= AUG)


def _split3(x):
    a = x.astype(BF16).astype(F32)
    r = x - a
    b = r.astype(BF16).astype(F32)
    return a, b, r - b


def _lane_fill(lane, base, pieces, start):
    for i, pc in enumerate(pieces):
        base = jnp.where(lane == start + i, pc, base)
    return base


FOX_TB = 512
FOX_SKIP = 40.0
N_STAT = 4


def _fox_prep(proj, c_tok, *, name):
    S = proj.shape[0]
    T = min(FOX_TB, S)

    def body(q_ref, k_ref, v_ref, c_ref, qa_ref, ka_ref, va_ref, st_ref):
        pair = pl.program_id(0)
        lane = lax.broadcasted_iota(jnp.int32, (T, 128), 1)
        lane1 = lax.broadcasted_iota(jnp.int32, (1, 128), 1)
        c = c_ref[...]
        q, k, v = q_ref[...], k_ref[...], v_ref[...]
        for hh in range(2):
            data, b0 = _data_lanes(lane, hh), _bias_lane(hh)
            ones3 = jnp.where((lane >= b0) & (lane < b0 + 3), 1.0, 0.0)

            def max_norm(t):
                tr = jnp.where(data, t.astype(BF16).astype(F32), 0.0)
                return jnp.sqrt(jnp.max(jnp.sum(tr * tr, axis=-1, keepdims=True), axis=0, keepdims=True))

            ch = jnp.sum(jnp.where(lane == 2 * pair + hh, c, 0.0), axis=-1, keepdims=True)
            c1, c2, c3 = _split3(ch)
            aug_q = _lane_fill(lane, jnp.where((lane >= b0 + 3) & (lane < b0 + 6), 1.0, 0.0), (c1, c2, c3), b0)
            aug_k = _lane_fill(lane, ones3, (-c1, -c2, -c3), b0 + 3)
            qa_ref[hh] = jnp.where(data, q * FOX_SCALE, aug_q).astype(BF16)
            ka_ref[hh] = jnp.where(data, k, aug_k).astype(BF16)
            va_ref[hh] = jnp.where(data, v, ones3).astype(BF16)
            stats = (max_norm(q * FOX_SCALE), jnp.max(ch, axis=0, keepdims=True), max_norm(k),
                     jnp.min(ch, axis=0, keepdims=True))
            st_ref[hh, 0] = _lane_fill(lane1, jnp.zeros((1, 128), F32), stats, 0)

    def grp(g):
        return pl.BlockSpec((T, 128), lambda p, t: (t, g * 8 + p))

    hm = pl.BlockSpec((2, T, 128), lambda p, t: (p, t, 0))
    out = jax.ShapeDtypeStruct((FOX_HEADS, S, 128), BF16)
    return pl.pallas_call(
        body, name=name, grid=(FOX_PAIRS, S // T),
        in_specs=[grp(4), grp(5), grp(6), pl.BlockSpec((T, 128), lambda p, t: (t, 0))],
        out_specs=[hm, hm, hm, pl.BlockSpec((2, 1, 1, 128), lambda p, t: (p, t, 0, 0))],
        out_shape=[out, out, out, jax.ShapeDtypeStruct((FOX_HEADS, S // T, 1, 128), F32)],
        compiler_params=_cparams(("parallel", "parallel")),
    )(proj, proj, proj, c_tok)


def _fox_bound(st_ref, head, nb, qi, ki):
    qb_, kb_ = (head * nb + qi) * N_STAT, (head * nb + ki) * N_STAT
    return st_ref[qb_] * st_ref[kb_ + 2] + st_ref[qb_ + 1] - st_ref[kb_ + 3] + 0.01


def _pair_lanes(lane, a0, a1):
    return jnp.where(lane < AUG, a0, a1)


def _first_live_key(st_ref, head, nb, qi, newest, thr):
    def body(t, k0):
        k = newest - t
        return jnp.where(_fox_bound(st_ref, head, nb, qi, k) > thr, k, k0)

    return lax.fori_loop(0, newest + 1, body, newest + 1)


def _last_live_query(st_ref, lm_ref, head, nb, ki):
    def body(t, i1):
        i = ki + 1 + t
        live = _fox_bound(st_ref, head, nb, i, ki) > lm_ref[head * nb + i] - FOX_SKIP
        return jnp.where(live, i, i1)

    return lax.fori_loop(0, nb - 1 - ki, body, ki)


class _BlockStream:
    def __init__(self, hbm_refs, bufs, sems, pair, tb):
        self.hbm, self.bufs, self.sems, self.pair, self.tb = hbm_refs, bufs, sems, pair, tb

    def _copies(self, blk, slot):
        rows = pl.ds(pl.multiple_of(blk * self.tb, self.tb), self.tb)
        return [pltpu.make_async_copy(h.at[pl.ds(2 * self.pair, 2), rows, :], b.at[slot], self.sems.at[n, slot])
                for n, (h, b) in enumerate(zip(self.hbm, self.bufs))]

    def start(self, blk, slot):
        for cp in self._copies(blk, slot):
            cp.start()

    def wait(self, blk, slot):
        for cp in self._copies(blk, slot):
            cp.wait()


def _fox_fwd(qa, ka, va, bounds, *, name):
    S = qa.shape[1]
    tb = min(FOX_TB, S)
    half = tb // 2
    nb = S // tb

    def body(qa_ref, ka_hbm, va_hbm, st_ref, o_ref, qb_ref, lse_ref, kbuf, vbuf, sems, m_s, acc_s, m_min):
        pair, qi = pl.program_id(0), pl.program_id(1)
        stream = _BlockStream((ka_hbm, va_hbm), (kbuf, vbuf), sems, pair, tb)

        def rows_step(hh, slot, rows, nkeys, r0):
            s = _dot(qa_ref[hh, rows, :], kbuf[slot, hh, 0:nkeys, :], _DIMS["nt"])
            if r0 is not None:
                row = r0 + lax.broadcasted_iota(jnp.int32, s.shape, 0)
                col = lax.broadcasted_iota(jnp.int32, s.shape, 1)
                s = jnp.where(col <= row, s, NEG)
            m_old = m_s[hh, rows, :]
            m_new = jnp.maximum(m_old, jnp.max(s, axis=-1, keepdims=True))
            p = jnp.exp(s - m_new)
            p_hi = p.astype(BF16)
            p_lo = (p - p_hi.astype(F32)).astype(BF16)
            vv = vbuf[slot, hh, 0:nkeys, :]
            acc_s[hh, rows, :] = (jnp.exp(m_old - m_new) * acc_s[hh, rows, :]
                                  + _dot(p_hi, vv, _DIMS["nn"]) + _dot(p_lo, vv, _DIMS["nn"]))
            m_s[hh, rows, :] = m_new
            return jnp.min(m_new)

        def head_step(hh, slot, masked):
            if masked:
                m_min[hh] = jnp.minimum(rows_step(hh, slot, pl.ds(0, half), half, 0),
                                        rows_step(hh, slot, pl.ds(half, half), tb, half))
            else:
                m_min[hh] = rows_step(hh, slot, pl.ds(0, tb), tb, None)

        @pl.when(qi == 0)
        def _():
            stream.start(qi, 0)

        @pl.when(qi > 0)
        def _():
            stream.start(qi - 1, 1)

        m_s[...] = jnp.full_like(m_s, NEG)
        acc_s[...] = jnp.zeros_like(acc_s)
        stream.wait(qi, 0)
        for hh in range(2):
            head_step(hh, 0, True)

        @pl.when(qi > 1)
        def _():
            stream.start(qi - 2, 0)

        @pl.when(qi > 0)
        def _():
            stream.wait(qi - 1, 1)
            for hh in range(2):
                head_step(hh, 1, False)

        k0 = [_first_live_key(st_ref, 2 * pair + hh, nb, qi, qi - 2, m_min[hh] - FOX_SKIP) for hh in range(2)]
        n = qi - 1 - jnp.minimum(k0[0], k0[1])

        @pl.when((qi > 1) & (n == 0))
        def _():
            stream.wait(qi - 2, 0)

        def loop(t, carry):
            k = qi - 2 - t
            slot = t % 2
            stream.wait(k, slot)

            @pl.when(t + 1 < n)
            def _():
                stream.start(k - 1, 1 - slot)

            for hh in range(2):
                @pl.when(k >= k0[hh])
                def _():
                    head_step(hh, slot, False)
            return carry

        lax.fori_loop(0, n, loop, 0)

        @pl.when(qi + 1 < nb)
        def _():
            stream.start(qi + 1, 0)

        lane = lax.broadcasted_iota(jnp.int32, (tb, 128), 1)
        outs = []
        for hh in range(2):
            acc = acc_s[hh]
            b0 = _bias_lane(hh)
            l = acc[:, b0:b0 + 1]
            outs.append(acc / l)
            lse = m_s[hh] + jnp.log(l)
            lse_ref[hh, 0] = jnp.broadcast_to(jnp.min(lse, axis=0, keepdims=True), (1, 128))
            qf = qa_ref[hh].astype(F32)
            cb = qf[:, b0:b0 + 1] + qf[:, b0 + 1:b0 + 2] + qf[:, b0 + 2:b0 + 3] - lse
            qb_ref[hh] = _lane_fill(lane, qf, _split3(cb), b0).astype(BF16)
        o_ref[...] = _pair_lanes(lane, outs[0], outs[1])

    qs = pl.BlockSpec((2, tb, 128), lambda p, i: (p, i, 0))
    return pl.pallas_call(
        body, name=name, grid=(FOX_PAIRS, nb),
        in_specs=[qs, ANY, ANY, SMEM],
        out_specs=[pl.BlockSpec((tb, 128), lambda p, i: (i, p)), qs,
                   pl.BlockSpec((2, 1, 1, 128), lambda p, i: (p, i, 0, 0))],
        out_shape=[jax.ShapeDtypeStruct((S, FOX_HEADS * FOX_DH), F32), jax.ShapeDtypeStruct((FOX_HEADS, S, 128), BF16),
                   jax.ShapeDtypeStruct((FOX_HEADS, nb, 1, 128), F32)],
        scratch_shapes=[pltpu.VMEM((2, 2, tb, 128), BF16), pltpu.VMEM((2, 2, tb, 128), BF16),
                        pltpu.SemaphoreType.DMA((2, 2)), pltpu.VMEM((2, tb, 1), F32), pltpu.VMEM((2, tb, 128), F32),
                        pltpu.SMEM((2,), F32)],
        compiler_params=_cparams(("arbitrary", "arbitrary")),
    )(qa, ka, va, bounds)


def _fox_bwd_prep(o, do, *, name, T=512):
    S = o.shape[0]
    T = min(T, S)

    def body(o_ref, do_ref, dob_ref):
        lane = lax.broadcasted_iota(jnp.int32, (T, 128), 1)
        d = do_ref[...].astype(F32)
        prod = d * o_ref[...]
        for hh in range(2):
            mine = _data_lanes(lane, hh)
            delta = jnp.sum(jnp.where(mine, prod, 0.0), axis=-1, keepdims=True)
            dob_ref[hh] = _lane_fill(lane, jnp.where(mine, d, 0.0), _split3(-delta), _bias_lane(hh)).astype(BF16)

    tok = pl.BlockSpec((T, 128), lambda p, t: (t, p))
    return pl.pallas_call(
        body, name=name, grid=(FOX_PAIRS, S // T),
        in_specs=[tok, tok], out_specs=pl.BlockSpec((2, T, 128), lambda p, t: (p, t, 0)),
        out_shape=jax.ShapeDtypeStruct((FOX_HEADS, S, 128), BF16),
        compiler_params=_cparams(("parallel", "parallel")),
    )(o, do)


def _fox_bwd_dq(qb, ka, va, dob, bounds, lse_min, *, name, comm=None):
    S = qb.shape[1]
    tb = min(FOX_TB, S)
    half = tb // 2
    nb = S // tb
    nc = comm.n if comm is not None else 0

    def body(qb_ref, dob_ref, ka_hbm, va_hbm, st_ref, lm_ref, *rest):
        c_in, (dq_ref, dcs_ref), c_out = rest[:nc], rest[nc:nc + 2], rest[nc + 2:2 * nc + 2]
        kbuf, vbuf, sems, acc_s = rest[2 * nc + 2:2 * nc + 6]
        c_sems = rest[2 * nc + 6:]
        pair, qi = pl.program_id(0), pl.program_id(1)
        if comm is not None:
            @pl.when((pair == 0) & (qi == 0))
            def _():
                comm.start(c_in, c_out, c_sems)

        stream = _BlockStream((ka_hbm, va_hbm), (kbuf, vbuf), sems, pair, tb)
        k0 = [_first_live_key(st_ref, 2 * pair + hh, nb, qi, qi - 1, lm_ref[(2 * pair + hh) * nb + qi] - FOX_SKIP)
              for hh in range(2)]
        n = qi - jnp.minimum(k0[0], k0[1]) + 1

        @pl.when(qi == 0)
        def _():
            stream.start(qi, 0)

        acc_s[...] = jnp.zeros_like(acc_s)
        dcs_ref[...] = jnp.zeros_like(dcs_ref)

        def rows_step(hh, slot, rows, nkeys, r0):
            kk = kbuf[slot, hh, 0:nkeys, :]
            s = _dot(qb_ref[hh, rows, :], kk, _DIMS["nt"])
            if r0 is not None:
                row = r0 + lax.broadcasted_iota(jnp.int32, s.shape, 0)
                col = lax.broadcasted_iota(jnp.int32, s.shape, 1)
                s = jnp.where(col <= row, s, NEG)
            ds = jnp.exp(s) * _dot(dob_ref[hh, rows, :], vbuf[slot, hh, 0:nkeys, :], _DIMS["nt"])
            acc_s[hh, rows, :] += _dot(ds.astype(BF16), kk, _DIMS["nn"])
            return jnp.sum(ds, axis=0, keepdims=True)

        def head_step(hh, slot, k, masked):
            base = pl.multiple_of(k * tb, tb)
            if masked:
                top = rows_step(hh, slot, pl.ds(0, half), half, 0)
                bot = rows_step(hh, slot, pl.ds(half, half), tb, half)
                dcs_ref[0, 0, hh:hh + 1, pl.ds(base, half)] = top + bot[:, 0:half]
                dcs_ref[0, 0, hh:hh + 1, pl.ds(pl.multiple_of(base + half, half), half)] = bot[:, half:tb]
            else:
                dcs_ref[0, 0, hh:hh + 1, pl.ds(base, tb)] = rows_step(hh, slot, pl.ds(0, tb), tb, None)

        def loop(t, carry):
            k = qi - t
            slot = t % 2
            stream.wait(k, slot)

            @pl.when(t + 1 < n)
            def _():
                stream.start(k - 1, 1 - slot)

            @pl.when(t == 0)
            def _():
                for hh in range(2):
                    head_step(hh, slot, k, True)

            for hh in range(2):
                @pl.when((t > 0) & (k >= k0[hh]))
                def _():
                    head_step(hh, slot, k, False)
            return carry

        lax.fori_loop(0, n, loop, 0)

        @pl.when(qi + 1 < nb)
        def _():
            stream.start(qi + 1, 0)

        lane = lax.broadcasted_iota(jnp.int32, (tb, 128), 1)
        dq_ref[...] = (_pair_lanes(lane, acc_s[0], acc_s[1]) * FOX_SCALE).astype(dq_ref.dtype)
        if comm is not None:
            @pl.when((pair == FOX_PAIRS - 1) & (qi == nb - 1))
            def _():
                comm.finish(c_in, c_out, c_sems)

    qs = pl.BlockSpec((2, tb, 128), lambda p, i: (p, i, 0))
    outs = pl.pallas_call(
        body, name=name, grid=(FOX_PAIRS, nb),
        in_specs=[qs, qs, ANY, ANY, SMEM, SMEM] + [ANY] * nc,
        out_specs=[pl.BlockSpec((tb, 128), lambda p, i: (i, p)),
                   pl.BlockSpec((1, 1, 2, S), lambda p, i: (p, i, 0, 0))] + [ANY] * nc,
        out_shape=[jax.ShapeDtypeStruct((S, FOX_HEADS * FOX_DH), BF16),
                   jax.ShapeDtypeStruct((FOX_PAIRS, nb, 2, S), F32)] + (comm.out_shapes if comm is not None else []),
        scratch_shapes=[pltpu.VMEM((2, 2, tb, 128), BF16), pltpu.VMEM((2, 2, tb, 128), BF16),
                        pltpu.SemaphoreType.DMA((2, 2)), pltpu.VMEM((2, tb, 128), F32)]
        + (comm.scratch if comm is not None else []),
        compiler_params=_cparams(("arbitrary", "arbitrary")),
    )(qb, dob, ka, va, bounds, lse_min, *(comm.inputs if comm is not None else []))
    return (outs[0], outs[1]) if comm is None else (outs[0], outs[1], outs[2:])


def _fox_bwd_dkv(qb, ka, va, dob, bounds, lse_min, *, name):
    S = qb.shape[1]
    tb = min(FOX_TB, S)
    half = tb // 2
    nb = S // tb

    def body(ka_ref, va_ref, qb_hbm, dob_hbm, st_ref, lm_ref, dk_ref, dv_ref, qbuf, dbuf, sems, dk_s, dv_s):
        pair, ki = pl.program_id(0), pl.program_id(1)
        stream = _BlockStream((qb_hbm, dob_hbm), (qbuf, dbuf), sems, pair, tb)
        i1 = [_last_live_query(st_ref, lm_ref, 2 * pair + hh, nb, ki) for hh in range(2)]
        n = jnp.maximum(i1[0], i1[1]) - ki + 1

        @pl.when(ki == 0)
        def _():
            stream.start(ki, 0)

        dk_s[...] = jnp.zeros_like(dk_s)
        dv_s[...] = jnp.zeros_like(dv_s)

        def keys_step(hh, slot, keys, q0, k0):
            qq, dd = qbuf[slot, hh, q0:tb, :], dbuf[slot, hh, q0:tb, :]
            st = _dot(ka_ref[hh, keys, :], qq, _DIMS["nt"])
            if k0 is not None:
                row = k0 + lax.broadcasted_iota(jnp.int32, st.shape, 0)
                col = q0 + lax.broadcasted_iota(jnp.int32, st.shape, 1)
                st = jnp.where(row <= col, st, NEG)
            pt = jnp.exp(st)
            dst = pt * _dot(va_ref[hh, keys, :], dd, _DIMS["nt"])
            dv_s[hh, keys, :] += _dot(pt.astype(BF16), dd, _DIMS["nn"])
            dk_s[hh, keys, :] += _dot(dst.astype(BF16), qq, _DIMS["nn"])

        def head_step(hh, slot, masked):
            if masked:
                keys_step(hh, slot, pl.ds(0, half), 0, 0)
                keys_step(hh, slot, pl.ds(half, half), half, half)
            else:
                keys_step(hh, slot, pl.ds(0, tb), 0, None)

        def loop(t, carry):
            i = ki + t
            slot = t % 2
            stream.wait(i, slot)

            @pl.when(t + 1 < n)
            def _():
                stream.start(i + 1, 1 - slot)

            @pl.when(t == 0)
            def _():
                for hh in range(2):
                    head_step(hh, slot, True)

            for hh in range(2):
                @pl.when((t > 0) & (i <= i1[hh]))
                def _():
                    head_step(hh, slot, False)
            return carry

        lax.fori_loop(0, n, loop, 0)

        @pl.when(ki + 1 < nb)
        def _():
            stream.start(ki + 1, 0)

        lane = lax.broadcasted_iota(jnp.int32, (tb, 128), 1)
        dk_ref[...] = _pair_lanes(lane, dk_s[0], dk_s[1]).astype(dk_ref.dtype)
        dv_ref[...] = _pair_lanes(lane, dv_s[0], dv_s[1]).astype(dv_ref.dtype)

    ks = pl.BlockSpec((2, tb, 128), lambda p, j: (p, j, 0))
    tok = pl.BlockSpec((tb, 128), lambda p, j: (j, p))
    big = jax.ShapeDtypeStruct((S, FOX_HEADS * FOX_DH), BF16)
    return pl.pallas_call(
        body, name=name, grid=(FOX_PAIRS, nb),
        in_specs=[ks, ks, ANY, ANY, SMEM, SMEM], out_specs=[tok, tok], out_shape=[big, big],
        scratch_shapes=[pltpu.VMEM((2, 2, tb, 128), BF16), pltpu.VMEM((2, 2, tb, 128), BF16),
                        pltpu.SemaphoreType.DMA((2, 2)), pltpu.VMEM((2, tb, 128), F32), pltpu.VMEM((2, tb, 128), F32)],
        compiler_params=_cparams(("arbitrary", "arbitrary")),
    )(ka, va, qb, dob, bounds, lse_min)


def _merge_fwd(proj, pa, pb, *, name, T=512):
    S, D = pa.shape
    T = min(T, S)

    def body(ga_ref, gb_ref, pa_ref, pb_ref, m_ref):
        m_ref[...] = (_sigmoid(ga_ref[...]) * pa_ref[...] + _sigmoid(gb_ref[...]) * pb_ref[...]).astype(m_ref.dtype)

    tok = pl.BlockSpec((T, D), lambda i: (i, 0))
    return pl.pallas_call(
        body, name=name, grid=(S // T,),
        in_specs=[pl.BlockSpec((T, D), lambda i: (i, 7)), pl.BlockSpec((T, D), lambda i: (i, 8)), tok, tok],
        out_specs=tok, out_shape=jax.ShapeDtypeStruct((S, D), BF16),
        compiler_params=_cparams(("parallel",)),
    )(proj, proj, pa, pb)


def _merge_bwd(proj, pa, pb, dm, *, name, T=512):
    S, D = pa.shape
    T = min(T, S)

    def body(ga_ref, gb_ref, pa_ref, pb_ref, dm_ref, dpa_ref, dpb_ref, dga_ref, dgb_ref):
        dm_ = dm_ref[...]
        sa, sb = _sigmoid(ga_ref[...]), _sigmoid(gb_ref[...])
        dpa_ref[...] = (dm_ * sa).astype(BF16)
        dpb_ref[...] = (dm_ * sb).astype(BF16)
        dga_ref[...] = (dm_ * pa_ref[...] * sa * (1.0 - sa)).astype(BF16)
        dgb_ref[...] = (dm_ * pb_ref[...] * sb * (1.0 - sb)).astype(BF16)

    tok = pl.BlockSpec((T, D), lambda i: (i, 0))
    big = jax.ShapeDtypeStruct((S, D), BF16)
    return pl.pallas_call(
        body, name=name, grid=(S // T,),
        in_specs=[pl.BlockSpec((T, D), lambda i: (i, 7)), pl.BlockSpec((T, D), lambda i: (i, 8)), tok, tok, tok],
        out_specs=[tok, tok, tok, tok], out_shape=[big, big, big, big],
        compiler_params=_cparams(("parallel",)),
    )(proj, proj, pa, pb, dm)


INV_SQRT2 = 0.7071067811865476
INV_SQRT2PI = 0.3989422804014327


def _shifted(u, prev, rid):
    m1 = jnp.where(rid == 0, prev[7:8, :], pltpu.roll(u, 1, 0))
    m2 = jnp.where(rid == 0, prev[6:7, :], jnp.where(rid == 1, prev[7:8, :], pltpu.roll(u, 2, 0)))
    return m1, m2


def _conv_acc(u, prev, w_ref, b_ref, rid):
    m1, m2 = _shifted(u, prev, rid)
    return b_ref[...] + w_ref[0:1, :] * m2 + w_ref[1:2, :] * m1 + w_ref[2:3, :] * u, m1, m2


def _convglu_fwd(ug, uv, wg, wv, bg, bv, *, name, T=512, tc=256):
    S, F = ug.shape
    T = min(T, S)

    def body(ug_ref, uv_ref, wg_ref, wv_ref, bg_ref, bv_ref, a_ref, pg, pv):
        @pl.when(pl.program_id(1) == 0)
        def _():
            pg[...] = jnp.zeros_like(pg)
            pv[...] = jnp.zeros_like(pv)

        rid = lax.broadcasted_iota(jnp.int32, (T, tc), 0)
        g_, v_ = ug_ref[...], uv_ref[...]
        accg, _, _ = _conv_acc(g_, pg[...], wg_ref, bg_ref, rid)
        accv, _, _ = _conv_acc(v_, pv[...], wv_ref, bv_ref, rid)
        gel = 0.5 * accg * (1.0 + lax.erf(accg * INV_SQRT2))
        a_ref[...] = (gel * accv).astype(a_ref.dtype)
        pg[...] = g_[T - 8:T, :]
        pv[...] = v_[T - 8:T, :]

    tok = pl.BlockSpec((T, tc), lambda j, t: (t, j))
    w3 = pl.BlockSpec((3, tc), lambda j, t: (0, j))
    b1 = pl.BlockSpec((1, tc), lambda j, t: (0, j))
    return pl.pallas_call(
        body, name=name, grid=(F // tc, S // T),
        in_specs=[tok, tok, w3, w3, b1, b1], out_specs=tok,
        out_shape=jax.ShapeDtypeStruct((S, F), BF16),
        scratch_shapes=[pltpu.VMEM((8, tc), F32), pltpu.VMEM((8, tc), F32)],
        compiler_params=_cparams(("parallel", "arbitrary")),
    )(ug, uv, wg, wv, bg, bv)


def _convglu_bwd(ug, uv, wg, wv, bg, bv, da, *, name, T=512, tc=256):
    S, F = ug.shape
    T = min(T, S)
    nT = S // T
    halo_blocks = T // 8

    def up_shift(d, nx, rid):
        p1 = jnp.where(rid == T - 1, nx[0:1, :], pltpu.roll(d, T - 1, 0))
        p2 = jnp.where(rid == T - 1, nx[1:2, :], jnp.where(rid == T - 2, nx[0:1, :], pltpu.roll(d, T - 2, 0)))
        return p1, p2

    def body(ug_ref, uv_ref, hg_ref, hv_ref, wg_ref, wv_ref, bg_ref, bv_ref, da_ref,
             dug_ref, duv_ref, dwg_ref, dwv_ref, dbg_ref, dbv_ref, ng, nv):
        @pl.when(pl.program_id(1) == 0)
        def _():
            ng[...] = jnp.zeros_like(ng)
            nv[...] = jnp.zeros_like(nv)
            for r in (dwg_ref, dwv_ref, dbg_ref, dbv_ref):
                r[...] = jnp.zeros_like(r)

        first_block = pl.program_id(1) == nT - 1
        rid = lax.broadcasted_iota(jnp.int32, (T, tc), 0)
        g_, v_ = ug_ref[...], uv_ref[...]
        pg = jnp.where(first_block, 0.0, hg_ref[...])
        pv = jnp.where(first_block, 0.0, hv_ref[...])
        accg, g1, g2 = _conv_acc(g_, pg, wg_ref, bg_ref, rid)
        accv, v1, v2 = _conv_acc(v_, pv, wv_ref, bv_ref, rid)
        cdf = 0.5 * (1.0 + lax.erf(accg * INV_SQRT2))
        pdf = INV_SQRT2PI * jnp.exp(-0.5 * accg * accg)
        da_ = da_ref[...].astype(F32)
        dgate = da_ * accv * (cdf + accg * pdf)
        dval = da_ * (accg * cdf)
        dbg_ref[...] += jnp.sum(dgate, axis=0, keepdims=True)
        dbv_ref[...] += jnp.sum(dval, axis=0, keepdims=True)
        for j, (sg_, sv_) in enumerate(((g2, v2), (g1, v1), (g_, v_))):
            dwg_ref[j:j + 1, :] += jnp.sum(dgate * sg_, axis=0, keepdims=True)
            dwv_ref[j:j + 1, :] += jnp.sum(dval * sv_, axis=0, keepdims=True)
        for d, w_ref, nx, out_ref in ((dgate, wg_ref, ng, dug_ref), (dval, wv_ref, nv, duv_ref)):
            p1, p2 = up_shift(d, nx[...], rid)
            out_ref[...] = (w_ref[2:3, :] * d + w_ref[1:2, :] * p1 + w_ref[0:1, :] * p2).astype(out_ref.dtype)
            nx[...] = d[0:8, :]

    tok = pl.BlockSpec((T, tc), lambda j, t: (nT - 1 - t, j))
    halo = pl.BlockSpec((8, tc), lambda j, t: (jnp.maximum((nT - 1 - t) * halo_blocks - 1, 0), j))
    w3 = pl.BlockSpec((3, tc), lambda j, t: (0, j))
    b1 = pl.BlockSpec((1, tc), lambda j, t: (0, j))
    big = jax.ShapeDtypeStruct((S, F), BF16)
    return pl.pallas_call(
        body, name=name, grid=(F // tc, nT),
        in_specs=[tok, tok, halo, halo, w3, w3, b1, b1, tok], out_specs=[tok, tok, w3, w3, b1, b1],
        out_shape=[big, big, jax.ShapeDtypeStruct((3, F), F32), jax.ShapeDtypeStruct((3, F), F32),
                   jax.ShapeDtypeStruct((1, F), F32), jax.ShapeDtypeStruct((1, F), F32)],
        scratch_shapes=[pltpu.VMEM((8, tc), F32), pltpu.VMEM((8, tc), F32)],
        compiler_params=_cparams(("parallel", "arbitrary")),
    )(ug, uv, ug, uv, wg, wv, bg, bv, da)


FF_LO = 7168
IN_SHARD = 1154
FF_DEV, FF_OFF = FF_LO // IN_SHARD, FF_LO % IN_SHARD


def _col_blocks(a, width):
    return jnp.stack([a[:, d * width:(d + 1) * width] for d in range(N_DEV)])


def _w_in_blocks(d_wm, d_wff):
    def block(d):
        lo = d * IN_SHARD
        if d < FF_DEV:
            return d_wm[:, lo:lo + IN_SHARD]
        if d > FF_DEV:
            return d_wm[:, lo - FOX_HEADS:lo - FOX_HEADS + IN_SHARD]
        return jnp.concatenate([d_wm[:, lo:FF_LO], d_wff[:, :FOX_HEADS], d_wm[:, FF_LO:lo + IN_SHARD - FOX_HEADS]], axis=1)

    return jnp.stack([block(d) for d in range(N_DEV)])


def _late_weights(g_a, g_b, g_o, g_up, g_cw, g_d):
    wup = jnp.concatenate([g_up[d] for d in range(N_DEV)], axis=1)
    cw = jnp.concatenate([g_cw[d] for d in range(N_DEV)], axis=1)
    return dict(wa=g_a.reshape(D_MODEL, D_MODEL), wb=g_b.reshape(D_MODEL, D_MODEL), wo=g_o.reshape(D_MODEL, D_MODEL),
                wug=wup[:, :D_FF], wuv=wup[:, D_FF:], cwg=cw[:, :D_FF], cwv=cw[:, D_FF:], wd=g_d.reshape(D_FF, D_MODEL))


def _early_grad_blocks(d_wa, d_wb, d_wo, d_wug, d_wuv, d_wd):
    up = jnp.stack([d_wug[:, d * 704:(d + 1) * 704] for d in range(4)]
                   + [d_wuv[:, d * 704:(d + 1) * 704] for d in range(4)])
    return [d_wa.reshape(N_DEV, 128, D_MODEL), d_wb.reshape(N_DEV, 128, D_MODEL), d_wo.reshape(N_DEV, 128, D_MODEL),
            up, d_wd.reshape(N_DEV, 352, D_MODEL)]


def _local_step(x, tgt, w, p, late=None, exchange=False):
    S = x.shape[0]
    mm = _matmul
    n1 = _rms_fwd(x, p["norm_mix"], name="rms1_fwd")
    if late is None:
        proj = mm(n1, w["wm"], "nn", name="proj_main")
    else:
        proj, gathered = mm(n1, w["wm"], "nn", comm=late, name="proj_main")
        w = {**w, **_late_weights(*gathered)}
    ff = mm(n1, w["wff"], "nn", name="proj_ff")
    lb = _lb_fwd(p["hg_lb_logits"], name="lb_fwd")
    gnorm = p["hg_norm"].reshape(1, HG_DV)
    o_hg, oa, states = _hgrn_fwd(proj, lb, gnorm, name="hgrn_fwd")
    bias = jnp.pad(p["fox_f_bias"].reshape(1, FOX_HEADS), ((0, 0), (0, 128 - FOX_HEADS)))
    c = _fox_gate_fwd(ff, bias, name="fox_gate_fwd")
    qa, ka, va, fox_stats = _fox_prep(proj, c, name="fox_prep")
    bounds = fox_stats[:, :, 0, :N_STAT].reshape(-1)
    ob, qb, lse_stats = _fox_fwd(qa, ka, va, bounds, name="fox_fwd")
    lse_min = lse_stats[:, :, 0, 0].reshape(-1)
    pa = mm(oa, w["wa"], "nn", name="branch_a")
    pb = mm(ob, w["wb"], "nn", name="branch_b")
    merged = _merge_fwd(proj, pa, pb, name="merge_fwd")
    h1 = mm(merged, w["wo"], "nn", addend=x, name="mix_out")
    n2 = _rms_fwd(h1, p["norm_ffn"], name="rms2_fwd")
    ug = mm(n2, w["wug"], "nn", name="up_gate")
    uv = mm(n2, w["wuv"], "nn", name="up_val")
    a = _convglu_fwd(ug, uv, w["cwg"], w["cwv"], p["cbg"], p["cbv"], name="convglu_fwd")
    h2 = mm(a, w["wd"], "nn", addend=h1, name="ffn_down")
    loss, dh2, d_norm_final = _loss_head(h2, p["norm_final"], tgt, name="loss_head")
    da = mm(dh2, w["wd"], "nt", out_dtype=BF16, name="d_act")
    d_wd = mm(a, dh2, "tn", out_dtype=BF16, name="dw_down")
    dug, duv, d_cwg, d_cwv, d_cbg, d_cbv = _convglu_bwd(
        ug, uv, w["cwg"], w["cwv"], p["cbg"], p["cbv"], da, name="convglu_bwd")
    dn2 = mm(dug, w["wug"], "nt", name="dn2_gate")
    dn2 = mm(duv, w["wuv"], "nt", addend=dn2, name="dn2_val")
    d_wug = mm(n2, dug, "tn", out_dtype=BF16, name="dw_up_gate")
    d_wuv = mm(n2, duv, "tn", out_dtype=BF16, name="dw_up_val")
    dh1, d_norm_ffn = _rms_bwd(h1, p["norm_ffn"], dn2, dh2, name="rms2_bwd")
    dmerged = mm(dh1, w["wo"], "nt", name="d_merged")
    d_wo = mm(merged, dh1, "tn", out_dtype=BF16, name="dw_out")
    dpa, dpb, dga, dgb = _merge_bwd(proj, pa, pb, dmerged, name="merge_bwd")
    doa = mm(dpa, w["wa"], "nt", name="d_oa")
    dob = mm(dpb, w["wb"], "nt", out_dtype=BF16, name="d_ob")
    d_wa = mm(oa, dpa, "tn", out_dtype=BF16, name="dw_branch_a")
    d_wb = mm(ob, dpb, "tn", out_dtype=BF16, name="dw_branch_b")
    dhq, dhf, dhi, dhg, dlb, dgn8 = _hgrn_bwd(proj, lb, gnorm, o_hg, states, doa, name="hgrn_bwd")
    d_logits = _lb_bwd(p["hg_lb_logits"], dlb, name="lb_bwd")
    dob_hm = _fox_bwd_prep(ob, dob, name="fox_bwd_prep")
    early_parts = None
    if exchange:
        comm = _ExchangeComm(_early_grad_blocks(d_wa, d_wb, d_wo, d_wug, d_wuv, d_wd))
        dq, dcsp, early_parts = _fox_bwd_dq(qb, ka, va, dob_hm, bounds, lse_min, comm=comm, name="fox_bwd_dq")
    else:
        dq, dcsp = _fox_bwd_dq(qb, ka, va, dob_hm, bounds, lse_min, name="fox_bwd_dq")
    dk, dv = _fox_bwd_dkv(qb, ka, va, dob_hm, bounds, lse_min, name="fox_bwd_dkv")
    dcs = jnp.sum(dcsp, axis=1)
    dcs_tok = jnp.pad(dcs.reshape(FOX_HEADS, S).T, ((0, 0), (0, 128 - FOX_HEADS)))
    dff, dbias = _fox_gate_bwd(ff, bias, dcs_tok, name="fox_gate_bwd")
    dproj = jnp.concatenate([dhq, dhf, dhi, dhg, dq, dk, dv, dga, dgb], axis=1)
    d_wm = mm(n1, dproj, "tn", out_dtype=BF16, name="dw_in_main")
    d_wff = mm(n1, dff, "tn", out_dtype=BF16, name="dw_in_ff")
    dn1 = mm(dff, w["wff"], "nt", name="dn1_ff")
    late_parts = None
    if exchange:
        d_cw = jnp.concatenate([d_cwg, d_cwv], axis=1)
        comm = _ExchangeComm([_w_in_blocks(d_wm, d_wff), _col_blocks(d_cw, 704)])
        dn1, late_parts = mm(dproj, w["wm"], "nt", addend=dn1, comm=comm, name="dn1_main")
    else:
        dn1 = mm(dproj, w["wm"], "nt", addend=dn1, name="dn1_main")
    dx, d_norm_mix = _rms_bwd(x, p["norm_mix"], dn1, dh1, name="rms1_bwd")
    grads = dict(
        wm=d_wm, wff=d_wff, wa=d_wa, wb=d_wb, wo=d_wo, wug=d_wug, wuv=d_wuv, cwg=d_cwg, cwv=d_cwv, wd=d_wd,
        norm_mix=d_norm_mix.reshape(-1), fox_f_bias=dbias[0, :FOX_HEADS], hg_lb_logits=d_logits,
        hg_norm=jnp.sum(dgn8, axis=0).reshape(-1), norm_ffn=d_norm_ffn.reshape(-1), cbg=d_cbg, cbv=d_cbv,
        norm_final=d_norm_final.reshape(-1), early_parts=early_parts, late_parts=late_parts)
    return loss, dx, grads


SMALL = [("norm_mix", (1, D_MODEL)), ("fox_f_bias", (1, FOX_HEADS)), ("hg_lb_logits", (2, HG_HEADS * HG_DK)),
         ("hg_norm", (1, HG_DV)), ("norm_ffn", (1, D_MODEL)), ("conv_b", (1, 2 * D_FF)), ("norm_final", (D_MODEL,))]
SMALL_ROWS = 88
SHARDED = [("w_in", (D_MODEL, 1154), 256), ("w_branch_a", (128, D_MODEL), 128), ("w_branch_b", (128, D_MODEL), 128),
           ("w_out", (128, D_MODEL), 128), ("w_up", (D_MODEL, 704), 256), ("conv_w", (3, 704), 3),
           ("w_down", (352, D_MODEL), 352)]
NAMES = ["norm_mix", "w_in", "fox_f_bias", "hg_lb_logits", "hg_norm", "w_branch_a", "w_branch_b", "w_out",
         "norm_ffn", "w_up", "conv_w", "conv_b", "w_down", "norm_final"]


def _size(shape):
    n = 1
    for s in shape:
        n *= s
    return n


def _adamw(parts, w, m, v, *, name, T):
    R, C = w.shape
    c1 = 1.0 / (1.0 - ADAM_B1 ** ADAM_STEP)
    c2 = 1.0 / (1.0 - ADAM_B2 ** ADAM_STEP)

    def body(p_ref, w_ref, m_ref, v_ref, g_ref, d_ref, nm_ref, nv_ref):
        g = p_ref[0].astype(F32)
        for s in range(1, N_DEV):
            g = g + p_ref[s].astype(F32)
        g_ref[...] = g
        nm = ADAM_B1 * m_ref[...] + (1.0 - ADAM_B1) * g
        nv = ADAM_B2 * v_ref[...] + (1.0 - ADAM_B2) * (g * g)
        nm_ref[...] = nm
        nv_ref[...] = nv
        d_ref[...] = -ADAM_LR * ((nm * c1) / (jnp.sqrt(nv * c2) + ADAM_EPS) + ADAM_WD * w_ref[...])

    blk = pl.BlockSpec((T, C), lambda i: (i, 0))
    out = jax.ShapeDtypeStruct((R, C), F32)
    return pl.pallas_call(
        body, name=name, grid=(R // T,),
        in_specs=[pl.BlockSpec((N_DEV, T, C), lambda i: (0, i, 0)), blk, blk, blk],
        out_specs=[blk, blk, blk, blk], out_shape=[out, out, out, out],
        compiler_params=_cparams(("parallel",)),
    )(parts, w, m, v)


def _pack_small(vals):
    flat = jnp.concatenate([vals[n].reshape(-1).astype(F32) for n, _ in SMALL])
    return jnp.pad(flat, (0, SMALL_ROWS * 128 - flat.shape[0])).reshape(SMALL_ROWS, 128)


def _unpack_small(buf):
    flat, out, off = buf.reshape(-1), {}, 0
    for n, shape in SMALL:
        out[n] = flat[off:off + _size(shape)].reshape(shape)
        off += _size(shape)
    return out


def kernel(x, norm_mix, w_in, fox_f_bias, hg_lb_logits, hg_norm, w_branch_a, w_branch_b, w_out, norm_ffn, w_up, conv_w, conv_b, w_down, norm_final, loss_target, m_norm_mix, m_w_in, m_fox_f_bias, m_hg_lb_logits, m_hg_norm, m_w_branch_a, m_w_branch_b, m_w_out, m_norm_ffn, m_w_up, m_conv_w, m_conv_b, m_w_down, m_norm_final, v_norm_mix, v_w_in, v_fox_f_bias, v_hg_lb_logits, v_hg_norm, v_w_branch_a, v_w_branch_b, v_w_out, v_norm_ffn, v_w_up, v_conv_w, v_conv_b, v_w_down, v_norm_final):
    wv = dict(norm_mix=norm_mix, w_in=w_in, fox_f_bias=fox_f_bias, hg_lb_logits=hg_lb_logits, hg_norm=hg_norm,
              w_branch_a=w_branch_a, w_branch_b=w_branch_b, w_out=w_out, norm_ffn=norm_ffn, w_up=w_up, conv_w=conv_w,
              conv_b=conv_b, w_down=w_down, norm_final=norm_final)
    mv = dict(norm_mix=m_norm_mix, w_in=m_w_in, fox_f_bias=m_fox_f_bias, hg_lb_logits=m_hg_lb_logits, hg_norm=m_hg_norm,
              w_branch_a=m_w_branch_a, w_branch_b=m_w_branch_b, w_out=m_w_out, norm_ffn=m_norm_ffn, w_up=m_w_up,
              conv_w=m_conv_w, conv_b=m_conv_b, w_down=m_w_down, norm_final=m_norm_final)
    vv = dict(norm_mix=v_norm_mix, w_in=v_w_in, fox_f_bias=v_fox_f_bias, hg_lb_logits=v_hg_lb_logits, hg_norm=v_hg_norm,
              w_branch_a=v_w_branch_a, w_branch_b=v_w_branch_b, w_out=v_w_out, norm_ffn=v_norm_ffn, w_up=v_w_up,
              conv_w=v_conv_w, conv_b=v_conv_b, w_down=v_w_down, norm_final=v_norm_final)

    (g_in,) = _comm_call(_GatherComm([w_in[0].astype(BF16)]), name="gather_w_in")
    w = dict(wm=jnp.concatenate([g_in[d] for d in range(FF_DEV)]
                                + [g_in[FF_DEV][:, :FF_OFF], g_in[FF_DEV][:, FF_OFF + FOX_HEADS:]]
                                + [g_in[d] for d in range(FF_DEV + 1, N_DEV)], axis=1),
             wff=jnp.pad(g_in[FF_DEV][:, FF_OFF:FF_OFF + FOX_HEADS], ((0, 0), (0, 128 - FOX_HEADS))))
    late = _GatherComm([w_branch_a[0].astype(BF16), w_branch_b[0].astype(BF16), w_out[0].astype(BF16),
                        w_up[0].astype(BF16), conv_w[0], w_down[0].astype(BF16)])
    p = dict(norm_mix=norm_mix[0], fox_f_bias=fox_f_bias[0], hg_lb_logits=hg_lb_logits, hg_norm=hg_norm[0],
             norm_ffn=norm_ffn[0], cbg=conv_b[:, :D_FF], cbv=conv_b[:, D_FF:], norm_final=norm_final)
    loss, dx, grads = _local_step(x[0], loss_target[0], w, p, late=late, exchange=True)
    loss = lax.psum(loss[0, 0], ("x", "y", "c"))

    small = _pack_small(dict(
        norm_mix=grads["norm_mix"], fox_f_bias=grads["fox_f_bias"], hg_lb_logits=grads["hg_lb_logits"],
        hg_norm=grads["hg_norm"], norm_ffn=grads["norm_ffn"], conv_b=jnp.concatenate([grads["cbg"], grads["cbv"]], axis=1),
        norm_final=grads["norm_final"]))
    (small_parts,) = _comm_call(_ExchangeComm([jnp.broadcast_to(small[None], (N_DEV, SMALL_ROWS, 128))]),
                                name="exchange_small")
    ea, eb, eo, eup, ed = grads["early_parts"]
    p_in, p_cw = grads["late_parts"]
    parts = [p_in, ea, eb, eo, eup, p_cw, ed, small_parts]
    res = {}
    for (n, shape, tile), part in zip(SHARDED, parts):
        outs = _adamw(part, wv[n].reshape(shape), mv[n].reshape(shape), vv[n].reshape(shape), name="adamw_" + n, T=tile)
        res[n] = [o.reshape(wv[n].shape) for o in outs]
    outs = _adamw(parts[-1], _pack_small(wv), _pack_small(mv), _pack_small(vv), name="adamw_small", T=SMALL_ROWS)
    small = [_unpack_small(o) for o in outs]
    for n, _ in SMALL:
        res[n] = [s[n] for s in small]
    return (loss, dx[None], *[res[n][0] for n in NAMES], *[res[n][1] for n in NAMES],
            *[res[n][2] for n in NAMES], *[res[n][3] for n in NAMES])
```

```python
import jax
import jax.numpy as jnp
from jax import lax
from jax.experimental import pallas as pl
from jax.experimental.pallas import tpu as pltpu

F32 = jnp.float32
BF16 = jnp.bfloat16

D_MODEL = 1024
HG_HEADS = 8
HG_DK = 128
HG_DV = 128
HG_CHUNK = 64
FOX_HEADS = 16
FOX_DH = 64
D_FF = 2816
EPS = 1e-6
N_DEV = 8

ADAM_LR = 0.001
ADAM_B1 = 0.9
ADAM_B2 = 0.999
ADAM_EPS = 1e-08
ADAM_WD = 0.01
ADAM_STEP = 10

VMEM_LIMIT = 56 * 1024 * 1024


def _cparams(sem):
    return pltpu.CompilerParams(dimension_semantics=sem, vmem_limit_bytes=VMEM_LIMIT)


MESH = pl.DeviceIdType.MESH
ANY = pl.BlockSpec(memory_space=pl.ANY)
SMEM = pl.BlockSpec(memory_space=pltpu.SMEM)


class _GatherComm:
    def __init__(self, shards):
        self.inputs = list(shards)
        n = self.n = len(shards)
        self.out_shapes = [jax.ShapeDtypeStruct((N_DEV,) + s.shape, s.dtype) for s in shards]
        self.scratch = [pltpu.SemaphoreType.DMA((n, 7)), pltpu.SemaphoreType.DMA((n, 7)), pltpu.SemaphoreType.DMA((n,))]

    def _parts(self, x_refs, out_refs, sems):
        send_sems, recv_sems, local_sems = sems
        x, y, c = lax.axis_index("x"), lax.axis_index("y"), lax.axis_index("c")
        me, sibling = (x, y, c), (x, y, 1 - c)
        chips = [(1 - x, y), (x, 1 - y), (1 - x, 1 - y)]

        def copy(t, k, block, to, src=None):
            slot = out_refs[t].at[4 * block[0] + 2 * block[1] + block[2]]
            return pltpu.make_async_remote_copy(
                src_ref=slot if src is None else src, dst_ref=slot,
                send_sem=send_sems.at[t, k], recv_sem=recv_sems.at[t, k], device_id=to, device_id_type=MESH)

        mine = [pltpu.make_async_copy(x_refs[t], out_refs[t].at[4 * x + 2 * y + c], local_sems.at[t])
                for t in range(self.n)]
        first = []
        for t in range(self.n):
            first.append(copy(t, 0, me, sibling, src=x_refs[t]))
            first += [copy(t, 1 + j, me, (*chip, c), src=x_refs[t]) for j, chip in enumerate(chips)]
        return c, me, sibling, chips, copy, mine, first

    def start(self, x_refs, out_refs, sems):
        _, _, _, _, _, mine, first = self._parts(x_refs, out_refs, sems)
        for cp in mine + first:
            cp.start()

    def finish(self, x_refs, out_refs, sems):
        c, me, sibling, chips, copy, mine, first = self._parts(x_refs, out_refs, sems)
        passed = []
        for j, chip in enumerate(chips):
            for t in range(self.n):
                copy(t, 1 + j, (*chip, c), me).wait_recv()
                passed.append(copy(t, 4 + j, (*chip, c), sibling))
                passed[-1].start()
        for t in range(self.n):
            copy(t, 0, sibling, me).wait_recv()
            for j, chip in enumerate(chips):
                copy(t, 4 + j, (*chip, 1 - c), me).wait_recv()
        for cp in first + passed:
            cp.wait_send()
        for cp in mine:
            cp.wait()


class _ExchangeComm:
    def __init__(self, blocks):
        self.inputs = list(blocks)
        n = self.n = len(blocks)
        self.out_shapes = [jax.ShapeDtypeStruct(b.shape, b.dtype) for b in blocks]
        self.scratch = [pltpu.SemaphoreType.DMA((n, 7)), pltpu.SemaphoreType.DMA((n, 7)), pltpu.SemaphoreType.DMA((n,))]

    def _parts(self, g_refs, out_refs, sems):
        send_sems, recv_sems, local_sems = sems
        x, y, c = lax.axis_index("x"), lax.axis_index("y"), lax.axis_index("c")
        me = 4 * x + 2 * y + c
        mine = [pltpu.make_async_copy(g_refs[t].at[me], out_refs[t].at[me], local_sems.at[t]) for t in range(self.n)]
        sends, recvs = [], []
        for k in range(1, N_DEV):
            px = 1 - x if k & 4 else x
            py = 1 - y if k & 2 else y
            pc = 1 - c if k & 1 else c
            p = 4 * px + 2 * py + pc
            for t in range(self.n):
                sends.append(pltpu.make_async_remote_copy(
                    src_ref=g_refs[t].at[p], dst_ref=out_refs[t].at[me], send_sem=send_sems.at[t, k - 1],
                    recv_sem=recv_sems.at[t, k - 1], device_id=(px, py, pc), device_id_type=MESH))
                recvs.append(pltpu.make_async_remote_copy(
                    src_ref=g_refs[t].at[p], dst_ref=out_refs[t].at[p], send_sem=send_sems.at[t, k - 1],
                    recv_sem=recv_sems.at[t, k - 1], device_id=(px, py, pc), device_id_type=MESH))
        return mine, sends, recvs

    def start(self, g_refs, out_refs, sems):
        mine, sends, _ = self._parts(g_refs, out_refs, sems)
        for cp in mine + sends:
            cp.start()

    def finish(self, g_refs, out_refs, sems):
        mine, sends, recvs = self._parts(g_refs, out_refs, sems)
        for cp in recvs:
            cp.wait_recv()
        for cp in sends:
            cp.wait_send()
        for cp in mine:
            cp.wait()


def _comm_call(comm, *, name):
    n = comm.n

    def body(*refs):
        comm.start(refs[:n], refs[n:2 * n], refs[2 * n:])
        comm.finish(refs[:n], refs[n:2 * n], refs[2 * n:])

    return pl.pallas_call(body, name=name, in_specs=[ANY] * n, out_specs=[ANY] * n, out_shape=comm.out_shapes,
                          scratch_shapes=comm.scratch)(*comm.inputs)


_DIMS = {
    "nn": (((1,), (0,)), ((), ())),
    "nt": (((1,), (1,)), ((), ())),
    "tn": (((0,), (0,)), ((), ())),
}

MATMUL_VMEM_BUDGET = 36 * 1024 * 1024
MAX_TILE = 1536


def _pick(n, prefs):
    for p in prefs:
        if n % p == 0:
            return p
    return n


def _tile_options(n):
    return [d for d in range(128, min(n, MAX_TILE) + 1, 128) if n % d == 0] or [n]


def _pick_tiles(M, N, tk, nk, sa, sb, so, has_addend, tm, tn):
    best = None
    for cm in ([tm] if tm else _tile_options(M)):
        for cn in ([tn] if tn else _tile_options(N)):
            need = 2 * (cm * tk * sa + tk * cn * sb + cm * cn * so + (cm * cn * 4 if has_addend else 0))
            need += cm * cn * 4 if nk > 1 else 0
            if need <= MATMUL_VMEM_BUDGET and (best is None or cm * cn > best[0] * best[1]
                                               or (cm * cn == best[0] * best[1] and cn > best[1])):
                best = (cm, cn)
    assert best is not None, (M, N, tk)
    return best


class _RowTail:
    def __init__(self, fn, rows, vecs, row_dtypes, n_vec, n_scalar):
        self.fn, self.rows, self.vecs = fn, list(rows), list(vecs)
        self.row_dtypes, self.n_vec, self.n_scalar = list(row_dtypes), n_vec, n_scalar


def _matmul(a, b, form, *, out_dtype=F32, addend=None, tm=None, tn=None, tk=None, comm=None, tail=None, name):
    if form == "nn":
        (M, K), (K2, N) = a.shape, b.shape
    elif form == "nt":
        (M, K), (N, K2) = a.shape, b.shape
    else:
        (K, M), (K2, N) = a.shape, b.shape
    assert K == K2, (a.shape, b.shape, form)
    tk = tk or (K if K <= 2816 else _pick(K, (1024, 512, 256, 128)))
    nk = K // tk
    if tail is not None:
        tm, tn = tm or min(512, M), N
    if tm is None or tn is None:
        tm, tn = _pick_tiles(M, N, tk, nk, a.dtype.itemsize, b.dtype.itemsize, jnp.dtype(out_dtype).itemsize,
                             addend is not None, tm, tn)
    assert M % tm == 0 and N % tn == 0 and K % tk == 0, (M, N, K, tm, tn, tk)
    dims = _DIMS[form]
    nc = comm.n if comm is not None else 0
    grid = (M // tm, N // tn, nk)
    n_rows, n_vecs = (len(tail.rows), len(tail.vecs)) if tail is not None else (0, 0)
    n_out = len(tail.row_dtypes) + tail.n_vec + tail.n_scalar if tail is not None else 1

    def body(*refs):
        a_ref, b_ref = refs[:2]
        pos = 2
        add_ref = refs[pos] if addend is not None else None
        pos += addend is not None
        t_rows, t_vecs = refs[pos:pos + n_rows], refs[pos + n_rows:pos + n_rows + n_vecs]
        pos += n_rows + n_vecs
        c_in, o_refs, c_out = refs[pos:pos + nc], refs[pos + nc:pos + nc + n_out], refs[pos + nc + n_out:pos + 2 * nc + n_out]
        pos += 2 * nc + n_out
        acc_ref = refs[pos] if nk > 1 else None
        c_sems = refs[pos + (nk > 1):]
        ids = [pl.program_id(d) for d in range(3)]
        if comm is not None:
            @pl.when((ids[0] == 0) & (ids[1] == 0) & (ids[2] == 0))
            def _():
                comm.start(c_in, c_out, c_sems)

        def finish(r):
            if add_ref is not None:
                r = r + add_ref[...].astype(F32)
            if tail is None:
                o_refs[0][...] = r.astype(o_refs[0].dtype)
                return
            row_res, vec_parts, scalar_parts = tail.fn(r, [t[...] for t in t_rows], [t[...] for t in t_vecs])
            sums = o_refs[len(row_res):]

            @pl.when(ids[0] == 0)
            def _():
                for s_ref in sums:
                    s_ref[...] = jnp.zeros_like(s_ref)

            for o_ref, val in zip(o_refs, row_res):
                o_ref[...] = val.astype(o_ref.dtype)
            for s_ref, val in zip(sums, list(vec_parts) + list(scalar_parts)):
                s_ref[...] += val

        part = lax.dot_general(a_ref[...].astype(BF16), b_ref[...].astype(BF16), dims, preferred_element_type=F32)
        if nk == 1:
            finish(part)
        else:
            k = pl.program_id(2)

            @pl.when(k == 0)
            def _():
                acc_ref[...] = part

            @pl.when(k > 0)
            def _():
                acc_ref[...] += part

            @pl.when(k == nk - 1)
            def _():
                finish(acc_ref[...])

        if comm is not None:
            @pl.when((ids[0] == grid[0] - 1) & (ids[1] == grid[1] - 1) & (ids[2] == grid[2] - 1))
            def _():
                comm.finish(c_in, c_out, c_sems)

    if form == "nn":
        a_spec = pl.BlockSpec((tm, tk), lambda i, j, k: (i, k))
        b_spec = pl.BlockSpec((tk, tn), lambda i, j, k: (k, j))
    elif form == "nt":
        a_spec = pl.BlockSpec((tm, tk), lambda i, j, k: (i, k))
        b_spec = pl.BlockSpec((tn, tk), lambda i, j, k: (j, k))
    else:
        a_spec = pl.BlockSpec((tk, tm), lambda i, j, k: (k, i))
        b_spec = pl.BlockSpec((tk, tn), lambda i, j, k: (k, j))
    o_spec = pl.BlockSpec((tm, tn), lambda i, j, k: (i, j))
    vec_spec = pl.BlockSpec((1, N), lambda i, j, k: (0, 0))
    one_spec = pl.BlockSpec((1, 1), lambda i, j, k: (0, 0))
    in_specs = [a_spec, b_spec] + ([o_spec] if addend is not None else [])
    args = (a, b) + ((addend,) if addend is not None else ())
    scratch = [pltpu.VMEM((tm, tn), F32)] if nk > 1 else []
    if tail is None:
        out_specs, out_shape = [o_spec], [jax.ShapeDtypeStruct((M, N), out_dtype)]
    else:
        in_specs += [o_spec] * n_rows + [vec_spec] * n_vecs
        args += tuple(tail.rows) + tuple(tail.vecs)
        out_specs = [o_spec] * len(tail.row_dtypes) + [vec_spec] * tail.n_vec + [one_spec] * tail.n_scalar
        out_shape = ([jax.ShapeDtypeStruct((M, N), d) for d in tail.row_dtypes]
                     + [jax.ShapeDtypeStruct((1, N), F32)] * tail.n_vec + [jax.ShapeDtypeStruct((1, 1), F32)] * tail.n_scalar)
    sequential = comm is not None or tail is not None
    outs = pl.pallas_call(
        body, name=name, grid=grid, in_specs=in_specs + [ANY] * nc, out_specs=out_specs + [ANY] * nc,
        out_shape=out_shape + (comm.out_shapes if comm is not None else []),
        scratch_shapes=scratch + (comm.scratch if comm is not None else []),
        compiler_params=_cparams(("arbitrary",) * 3 if sequential else ("parallel", "parallel", "arbitrary")),
    )(*args, *(comm.inputs if comm is not None else []))
    res = outs[0] if tail is None else list(outs[:n_out])
    return res if comm is None else (res, outs[n_out:])


def _rms_fwd(x, g, *, name, tm=512):
    M, D = x.shape
    tm = min(tm, M)

    def body(x_ref, g_ref, n_ref):
        xf = x_ref[...]
        r = lax.rsqrt(jnp.mean(xf * xf, axis=-1, keepdims=True) + EPS)
        n_ref[...] = (xf * r * g_ref[...]).astype(n_ref.dtype)

    return pl.pallas_call(
        body, name=name, grid=(M // tm,),
        in_specs=[pl.BlockSpec((tm, D), lambda i: (i, 0)), pl.BlockSpec((1, D), lambda i: (0, 0))],
        out_specs=pl.BlockSpec((tm, D), lambda i: (i, 0)),
        out_shape=jax.ShapeDtypeStruct((M, D), BF16),
        compiler_params=_cparams(("parallel",)),
    )(x, g.reshape(1, D))


def _rms_bwd_tail(x, g, dres):
    def fn(dn, rows, vecs):
        xf, dres_ = rows
        r = lax.rsqrt(jnp.mean(xf * xf, axis=-1, keepdims=True) + EPS)
        xh = xf * r
        dxh = dn * vecs[0]
        dx = dres_ + r * (dxh - xh * jnp.mean(dxh * xh, axis=-1, keepdims=True))
        return [dx], [jnp.sum(dn * xh, axis=0, keepdims=True)], []

    return _RowTail(fn, [x, dres], [g.reshape(1, -1)], [F32], 1, 0)


def _loss_tail(g, tgt):
    def fn(h, rows, vecs):
        d = h.shape[-1]
        r = lax.rsqrt(jnp.mean(h * h, axis=-1, keepdims=True) + EPS)
        xh = h * r
        err = xh * vecs[0] - rows[0]
        loss = 0.5 * jnp.sum(jnp.mean(err * err, axis=-1, keepdims=True), axis=0, keepdims=True)
        dy = err * (1.0 / d)
        dxh = dy * vecs[0]
        dh = r * (dxh - xh * jnp.mean(dxh * xh, axis=-1, keepdims=True))
        return [dh], [jnp.sum(dy * xh, axis=0, keepdims=True)], [loss]

    return _RowTail(fn, [tgt], [g.reshape(1, -1)], [F32], 1, 1)


HG_MID = HG_CHUNK // 2 - 1
EXP_CAP = 80.0


def _sigmoid(x):
    return 1.0 / (1.0 + jnp.exp(-x))


def _dot(a, b, dims, precision=None):
    return lax.dot_general(a, b, dims, preferred_element_type=F32, precision=precision)


def _bdot(a, b, form):
    return _dot(a.astype(BF16), b.astype(BF16), _DIMS[form])


def _split2(x):
    hi = x.astype(BF16)
    return hi, (x - hi.astype(F32)).astype(BF16)


def _dot3(a, b, form):
    d = _DIMS[form]
    return _dot(a[0], b[0], d) + (_dot(a[0], b[1], d) + _dot(a[1], b[0], d))


def _hgrn_chunk_common(hq, hf, lbv, tril, rid):
    sq = _sigmoid(hq)
    q = hq * sq
    sg = _sigmoid(hf)
    f = lbv + (1.0 - lbv) * sg
    k = (1.0 - lbv) * (1.0 - sg)
    g = jnp.log(f)
    b = _dot(tril, g, _DIMS["nn"], precision=lax.Precision.HIGHEST)
    bref = jnp.sum(jnp.where(rid == HG_MID, b, 0.0), axis=0, keepdims=True)
    bend = jnp.sum(jnp.where(rid == HG_CHUNK - 1, b, 0.0), axis=0, keepdims=True)
    eb = jnp.exp(b)
    e1 = jnp.exp(jnp.minimum(b - bref, EXP_CAP))
    e2 = jnp.exp(jnp.minimum(bref - b, EXP_CAP))
    e3 = jnp.exp(bend - b)
    return sq, q, sg, f, k, bend, eb, e1, e2, e3


def _hgrn_fwd(proj, lb, gnorm, *, name, T=1024):
    S = proj.shape[0]
    T = min(T, S)
    nch = T // HG_CHUNK
    C = HG_CHUNK

    def body(hq_ref, hf_ref, hi_ref, hg_ref, lb_ref, gn_ref, o_ref, oa_ref, st_ref, state):
        @pl.when(pl.program_id(1) == 0)
        def _():
            state[...] = jnp.zeros_like(state)

        lbv = lb_ref[...]
        gn = gn_ref[...]
        row = lax.broadcasted_iota(jnp.int32, (C, C), 0)
        col = lax.broadcasted_iota(jnp.int32, (C, C), 1)
        causal = row >= col
        tril = causal.astype(F32)
        rid = lax.broadcasted_iota(jnp.int32, (C, HG_DK), 0)
        sls = [pl.ds(c * C, C) for c in range(nch)]
        pre = [_hgrn_chunk_common(hq_ref[sl, :], hf_ref[sl, :], lbv, tril, rid) for sl in sls]
        v_l = [hi_ref[sl, :].astype(BF16) for sl in sls]
        a_l, u_l = [], []
        for c in range(nch):
            _, q, _, _, k, _, _, e1, e2, e3 = pre[c]
            a_l.append(jnp.where(causal, _bdot(q * e1, k * e2, "nt"), 0.0))
            u_l.append(_bdot(v_l[c], k * e3, "tn"))
        o_l = [_bdot(a_l[c], v_l[c], "nn") for c in range(nch)]
        st = state[...]
        st_l = []
        for c in range(nch):
            st_l.append(st)
            st = st * jnp.exp(pre[c][5]) + u_l[c]
        state[...] = st
        for c in range(nch):
            st_ref[0, c] = st_l[c]
            o_l[c] = o_l[c] + _bdot(pre[c][1] * pre[c][6], st_l[c], "nt")
        for c in range(nch):
            o, hg = o_l[c], hg_ref[sls[c], :]
            o_ref[sls[c], :] = o
            r = lax.rsqrt(jnp.mean(o * o, axis=-1, keepdims=True) + EPS)
            oa_ref[sls[c], :] = (o * r * gn * (hg * _sigmoid(hg))).astype(oa_ref.dtype)

    def grp(gidx):
        return pl.BlockSpec((T, 128), lambda h, t: (t, gidx * 8 + h))

    return pl.pallas_call(
        body, name=name, grid=(HG_HEADS, S // T),
        in_specs=[grp(0), grp(1), grp(2), grp(3),
                  pl.BlockSpec((1, 128), lambda h, t: (0, h)), pl.BlockSpec((1, 128), lambda h, t: (0, 0))],
        out_specs=[pl.BlockSpec((T, 128), lambda h, t: (t, h)), pl.BlockSpec((T, 128), lambda h, t: (t, h)),
                   pl.BlockSpec((1, nch, HG_DV, HG_DK), lambda h, t: (h, t, 0, 0))],
        out_shape=[jax.ShapeDtypeStruct((S, HG_HEADS * HG_DV), F32), jax.ShapeDtypeStruct((S, HG_HEADS * HG_DV), BF16),
                   jax.ShapeDtypeStruct((HG_HEADS, S // C, HG_DV, HG_DK), F32)],
        scratch_shapes=[pltpu.VMEM((HG_DV, HG_DK), F32)],
        compiler_params=_cparams(("parallel", "arbitrary")),
    )(proj, proj, proj, proj, lb, gnorm)


def _hgrn_bwd(proj, lb, gnorm, o, states, doa, *, name, T=1024):
    S = proj.shape[0]
    T = min(T, S)
    nch = T // HG_CHUNK
    C = HG_CHUNK
    nT = S // T

    def body(hq_ref, hf_ref, hi_ref, hg_ref, lb_ref, gn_ref, o_ref, st_ref, doa_ref,
             dhq_ref, dhf_ref, dhi_ref, dhg_ref, dlb_ref, dgn_ref, dstate):
        @pl.when(pl.program_id(1) == 0)
        def _():
            dstate[...] = jnp.zeros_like(dstate)
            dlb_ref[...] = jnp.zeros_like(dlb_ref)
            dgn_ref[...] = jnp.zeros_like(dgn_ref)

        lbv = lb_ref[...]
        gn = gn_ref[...]
        row = lax.broadcasted_iota(jnp.int32, (C, C), 0)
        col = lax.broadcasted_iota(jnp.int32, (C, C), 1)
        causal = row >= col
        tril = causal.astype(F32)
        triu = (row <= col).astype(F32)
        rid = lax.broadcasted_iota(jnp.int32, (C, HG_DK), 0)
        rng = range(nch)
        sls = [pl.ds(c * C, C) for c in rng]
        pre = [_hgrn_chunk_common(hq_ref[sl, :], hf_ref[sl, :], lbv, tril, rid) for sl in sls]
        do2, dgn_acc = [], jnp.zeros((1, HG_DV), F32)
        for c in rng:
            hg, ov = hg_ref[sls[c], :], o_ref[sls[c], :]
            r = lax.rsqrt(jnp.mean(ov * ov, axis=-1, keepdims=True) + EPS)
            xh = ov * r
            sgg = _sigmoid(hg)
            d_oa = doa_ref[sls[c], :].astype(F32)
            dz = d_oa * (hg * sgg)
            dhg_ref[sls[c], :] = (d_oa * (xh * gn) * (sgg * (1.0 + hg * (1.0 - sgg)))).astype(dhg_ref.dtype)
            dgn_acc = dgn_acc + jnp.sum(dz * xh, axis=0, keepdims=True)
            dxh = dz * gn
            do2.append(_split2(r * (dxh - xh * jnp.mean(dxh * xh, axis=-1, keepdims=True))))
        dgn_ref[0] += dgn_acc
        qi = [pre[c][1] * pre[c][6] for c in rng]
        qp = [pre[c][1] * pre[c][7] for c in rng]
        kp = [pre[c][4] * pre[c][8] for c in rng]
        kend = [pre[c][4] * pre[c][9] for c in rng]
        qi2, qp2, kp2, kend2 = ([_split2(t) for t in lst] for lst in (qi, qp, kp, kend))
        v2 = [_split2(hi_ref[sl, :]) for sl in sls]
        st0 = [st_ref[0, c] for c in rng]
        a_l = [jnp.where(causal, _dot(qp2[c][0], kp2[c][0], _DIMS["nt"]), 0.0).astype(BF16) for c in rng]
        da2 = [_split2(jnp.where(causal, _dot3(do2[c], v2[c], "nt"), 0.0)) for c in rng]
        dqi = [_dot3(do2[c], _split2(st0[c]), "nn") for c in rng]
        w_l = [_dot3(do2[c], qi2[c], "tn") for c in rng]
        ds = dstate[...]
        ds1 = [None] * nch
        for c in reversed(rng):
            ds1[c] = ds
            ds = ds * jnp.exp(pre[c][5]) + w_l[c]
        dstate[...] = ds
        ds12 = [_split2(t) for t in ds1]
        dqp = [_dot3(da2[c], kp2[c], "nn") for c in rng]
        dkp = [_dot3(da2[c], qp2[c], "tn") for c in rng]
        dv = [_dot(a_l[c], do2[c][0], _DIMS["tn"]) + _dot(kend2[c][0], ds12[c][0], _DIMS["nt"]) for c in rng]
        dkend = [_dot3(v2[c], ds12[c], "nn") for c in rng]
        dq_l, dk_l, db_l = [], [], []
        for c in rng:
            _, _, _, _, _, bend, eb, e1, e2, e3 = pre[c]
            dq_l.append(dqi[c] * eb + dqp[c] * e1)
            dk_l.append(dkp[c] * e2 + dkend[c] * e3)
            db = dqi[c] * qi[c] + dqp[c] * qp[c] - dkp[c] * kp[c] - dkend[c] * kend[c]
            dbend = (jnp.sum(dkend[c] * kend[c], axis=0, keepdims=True)
                     + jnp.exp(bend) * jnp.sum(ds1[c] * st0[c], axis=0, keepdims=True))
            db_l.append(db + jnp.where(rid == C - 1, dbend, 0.0))
        dg = [_dot(triu, db_l[c], _DIMS["nn"], precision=lax.Precision.HIGHEST) for c in rng]
        dlb_acc = jnp.zeros((1, HG_DK), F32)
        for c in rng:
            sq, _, sg, f, _, _, _, _, _, _ = pre[c]
            hq = hq_ref[sls[c], :]
            df = dg[c] / f - dk_l[c]
            dlb_acc = dlb_acc + jnp.sum(df * (1.0 - sg), axis=0, keepdims=True)
            dhf_ref[sls[c], :] = (df * (1.0 - lbv) * sg * (1.0 - sg)).astype(dhf_ref.dtype)
            dhq_ref[sls[c], :] = (dq_l[c] * (sq * (1.0 + hq * (1.0 - sq)))).astype(dhq_ref.dtype)
            dhi_ref[sls[c], :] = dv[c].astype(dhi_ref.dtype)
        dlb_ref[...] += dlb_acc

    def grp(gidx):
        return pl.BlockSpec((T, 128), lambda h, t: (nT - 1 - t, gidx * 8 + h))

    tok = pl.BlockSpec((T, 128), lambda h, t: (nT - 1 - t, h))
    big = jax.ShapeDtypeStruct((S, HG_HEADS * HG_DV), BF16)
    return pl.pallas_call(
        body, name=name, grid=(HG_HEADS, nT),
        in_specs=[grp(0), grp(1), grp(2), grp(3),
                  pl.BlockSpec((1, 128), lambda h, t: (0, h)), pl.BlockSpec((1, 128), lambda h, t: (0, 0)),
                  tok, pl.BlockSpec((1, nch, HG_DV, HG_DK), lambda h, t: (h, nT - 1 - t, 0, 0)), tok],
        out_specs=[tok, tok, tok, tok, pl.BlockSpec((1, 128), lambda h, t: (0, h)),
                   pl.BlockSpec((1, 1, 128), lambda h, t: (h, 0, 0))],
        out_shape=[big, big, big, big, jax.ShapeDtypeStruct((1, HG_HEADS * HG_DK), F32),
                   jax.ShapeDtypeStruct((HG_HEADS, 1, HG_DV), F32)],
        scratch_shapes=[pltpu.VMEM((HG_DV, HG_DK), F32)],
        compiler_params=_cparams(("parallel", "arbitrary")),
    )(proj, proj, proj, proj, lb, gnorm, o, states, doa)


def _lb_fwd(logits, *, name):
    def body(l_ref, lb_ref):
        lb_ref[...] = _sigmoid(l_ref[0:1, :] - l_ref[1:2, :])

    return pl.pallas_call(body, name=name, out_shape=jax.ShapeDtypeStruct((1, logits.shape[1]), F32))(logits)


def _lb_bwd(logits, dlb, *, name):
    def body(l_ref, d_ref, o_ref):
        lbv = _sigmoid(l_ref[0:1, :] - l_ref[1:2, :])
        t = d_ref[...] * lbv * (1.0 - lbv)
        o_ref[0:1, :] = t
        o_ref[1:2, :] = -t

    return pl.pallas_call(body, name=name, out_shape=jax.ShapeDtypeStruct(logits.shape, F32))(logits, dlb)


NEG = -1e30
FOX_SCALE = FOX_DH ** -0.5
FOX_PAIRS = FOX_HEADS // 2


def _fox_gate_fwd(ff, bias, *, name, T=512):
    S = ff.shape[0]
    T = min(T, S)

    def body(ff_ref, b_ref, c_ref, carry):
        @pl.when(pl.program_id(0) == 0)
        def _():
            carry[...] = jnp.zeros_like(carry)

        z = ff_ref[...] + b_ref[...]
        logf = jnp.minimum(z, 0.0) - jnp.log(1.0 + jnp.exp(-jnp.abs(z)))
        row = lax.broadcasted_iota(jnp.int32, (T, T), 0)
        col = lax.broadcasted_iota(jnp.int32, (T, T), 1)
        c = _dot((row >= col).astype(F32), logf, _DIMS["nn"], precision=lax.Precision.HIGHEST) + carry[...]
        c_ref[...] = c
        carry[...] = c[T - 1:T, :]

    return pl.pallas_call(
        body, name=name, grid=(S // T,),
        in_specs=[pl.BlockSpec((T, 128), lambda i: (i, 0)), pl.BlockSpec((1, 128), lambda i: (0, 0))],
        out_specs=pl.BlockSpec((T, 128), lambda i: (i, 0)),
        out_shape=jax.ShapeDtypeStruct((S, 128), F32),
        scratch_shapes=[pltpu.VMEM((1, 128), F32)],
        compiler_params=_cparams(("arbitrary",)),
    )(ff, bias)


def _fox_gate_bwd(ff, bias, dcs, *, name, T=512):
    S = ff.shape[0]
    T = min(T, S)
    nT = S // T

    def body(ff_ref, b_ref, d_ref, dff_ref, db_ref, carry):
        @pl.when(pl.program_id(0) == 0)
        def _():
            carry[...] = jnp.zeros_like(carry)
            db_ref[...] = jnp.zeros_like(db_ref)

        row = lax.broadcasted_iota(jnp.int32, (T, T), 0)
        col = lax.broadcasted_iota(jnp.int32, (T, T), 1)
        dlogf = carry[...] - _dot((row <= col).astype(F32), d_ref[...], _DIMS["nn"], precision=lax.Precision.HIGHEST)
        carry[...] = dlogf[0:1, :]
        dff = dlogf * (1.0 - _sigmoid(ff_ref[...] + b_ref[...]))
        dff_ref[...] = dff.astype(dff_ref.dtype)
        db_ref[...] += jnp.sum(dff, axis=0, keepdims=True)

    rev = pl.BlockSpec((T, 128), lambda i: (nT - 1 - i, 0))
    vec = pl.BlockSpec((1, 128), lambda i: (0, 0))
    return pl.pallas_call(
        body, name=name, grid=(nT,),
        in_specs=[rev, vec, rev], out_specs=[rev, vec],
        out_shape=[jax.ShapeDtypeStruct((S, 128), BF16), jax.ShapeDtypeStruct((1, 128), F32)],
        scratch_shapes=[pltpu.VMEM((1, 128), F32)],
        compiler_params=_cparams(("arbitrary",)),
    )(ff, bias, dcs)


AUG = FOX_DH


def _bias_lane(hh):
    return AUG * (1 - hh)


def _data_lanes(lane, hh):
    return (lane < AUG) if hh == 0 else (lane >= AUG)


def _split3(x):
    a = x.astype(BF16).astype(F32)
    r = x - a
    b = r.astype(BF16).astype(F32)
    return a, b, r - b


def _lane_fill(lane, base, pieces, start):
    for i, pc in enumerate(pieces):
        base = jnp.where(lane == start + i, pc, base)
    return base


FOX_TB = 512
FOX_SKIP = 40.0
N_STAT = 4


def _fox_prep(proj, c_tok, *, name):
    S = proj.shape[0]
    T = min(FOX_TB, S)

    def body(q_ref, k_ref, v_ref, c_ref, qa_ref, ka_ref, va_ref, st_ref):
        pair = pl.program_id(0)
        lane = lax.broadcasted_iota(jnp.int32, (T, 128), 1)
        lane1 = lax.broadcasted_iota(jnp.int32, (1, 128), 1)
        c = c_ref[...]
        q, k, v = q_ref[...], k_ref[...], v_ref[...]
        for hh in range(2):
            data, b0 = _data_lanes(lane, hh), _bias_lane(hh)
            ones3 = jnp.where((lane >= b0) & (lane < b0 + 3), 1.0, 0.0)

            def max_norm(t):
                tr = jnp.where(data, t.astype(BF16).astype(F32), 0.0)
                return jnp.sqrt(jnp.max(jnp.sum(tr * tr, axis=-1, keepdims=True), axis=0, keepdims=True))

            ch = jnp.sum(jnp.where(lane == 2 * pair + hh, c, 0.0), axis=-1, keepdims=True)
            c1, c2, c3 = _split3(ch)
            aug_q = _lane_fill(lane, jnp.where((lane >= b0 + 3) & (lane < b0 + 6), 1.0, 0.0), (c1, c2, c3), b0)
            aug_k = _lane_fill(lane, ones3, (-c1, -c2, -c3), b0 + 3)
            qa_ref[hh] = jnp.where(data, q * FOX_SCALE, aug_q).astype(BF16)
            ka_ref[hh] = jnp.where(data, k, aug_k).astype(BF16)
            va_ref[hh] = jnp.where(data, v, ones3).astype(BF16)
            stats = (max_norm(q * FOX_SCALE), jnp.max(ch, axis=0, keepdims=True), max_norm(k),
                     jnp.min(ch, axis=0, keepdims=True))
            st_ref[hh, 0] = _lane_fill(lane1, jnp.zeros((1, 128), F32), stats, 0)

    def grp(g):
        return pl.BlockSpec((T, 128), lambda p, t: (t, g * 8 + p))

    hm = pl.BlockSpec((2, T, 128), lambda p, t: (p, t, 0))
    out = jax.ShapeDtypeStruct((FOX_HEADS, S, 128), BF16)
    return pl.pallas_call(
        body, name=name, grid=(FOX_PAIRS, S // T),
        in_specs=[grp(4), grp(5), grp(6), pl.BlockSpec((T, 128), lambda p, t: (t, 0))],
        out_specs=[hm, hm, hm, pl.BlockSpec((2, 1, 1, 128), lambda p, t: (p, t, 0, 0))],
        out_shape=[out, out, out, jax.ShapeDtypeStruct((FOX_HEADS, S // T, 1, 128), F32)],
        compiler_params=_cparams(("parallel", "parallel")),
    )(proj, proj, proj, c_tok)


def _fox_bound(st_ref, head, nb, qi, ki):
    qb_, kb_ = (head * nb + qi) * N_STAT, (head * nb + ki) * N_STAT
    return st_ref[qb_] * st_ref[kb_ + 2] + st_ref[qb_ + 1] - st_ref[kb_ + 3] + 0.01


def _pair_lanes(lane, a0, a1):
    return jnp.where(lane < AUG, a0, a1)


def _first_live_key(st_ref, head, nb, qi, newest, thr):
    def body(t, k0):
        k = newest - t
        return jnp.where(_fox_bound(st_ref, head, nb, qi, k) > thr, k, k0)

    return lax.fori_loop(0, newest + 1, body, newest + 1)


def _last_live_query(st_ref, lm_ref, head, nb, ki):
    def body(t, i1):
        i = ki + 1 + t
        live = _fox_bound(st_ref, head, nb, i, ki) > lm_ref[head * nb + i] - FOX_SKIP
        return jnp.where(live, i, i1)

    return lax.fori_loop(0, nb - 1 - ki, body, ki)


class _BlockStream:
    def __init__(self, hbm_refs, bufs, sems, pair, tb):
        self.hbm, self.bufs, self.sems, self.pair, self.tb = hbm_refs, bufs, sems, pair, tb

    def _copies(self, blk, slot):
        rows = pl.ds(pl.multiple_of(blk * self.tb, self.tb), self.tb)
        return [pltpu.make_async_copy(h.at[pl.ds(2 * self.pair, 2), rows, :], b.at[slot], self.sems.at[n, slot])
                for n, (h, b) in enumerate(zip(self.hbm, self.bufs))]

    def start(self, blk, slot):
        for cp in self._copies(blk, slot):
            cp.start()

    def wait(self, blk, slot):
        for cp in self._copies(blk, slot):
            cp.wait()


def _fox_fwd(qa, ka, va, bounds, *, name):
    S = qa.shape[1]
    tb = min(FOX_TB, S)
    nb = S // tb

    def body(qa_ref, ka_hbm, va_hbm, st_ref, o_ref, qb_ref, lse_ref, kbuf, vbuf, sems, m_s, acc_s, m_min):
        pair, qi = pl.program_id(0), pl.program_id(1)
        stream = _BlockStream((ka_hbm, va_hbm), (kbuf, vbuf), sems, pair, tb)

        def head_step(hh, slot, masked):
            s = _dot(qa_ref[hh], kbuf[slot, hh], _DIMS["nt"])
            if masked:
                row = lax.broadcasted_iota(jnp.int32, (tb, tb), 0)
                col = lax.broadcasted_iota(jnp.int32, (tb, tb), 1)
                s = jnp.where(col <= row, s, NEG)
            m_old = m_s[hh]
            m_new = jnp.maximum(m_old, jnp.max(s, axis=-1, keepdims=True))
            p = jnp.exp(s - m_new)
            p_hi = p.astype(BF16)
            p_lo = (p - p_hi.astype(F32)).astype(BF16)
            vv = vbuf[slot, hh]
            acc_s[hh] = (jnp.exp(m_old - m_new) * acc_s[hh]
                         + _dot(p_hi, vv, _DIMS["nn"]) + _dot(p_lo, vv, _DIMS["nn"]))
            m_s[hh] = m_new
            m_min[hh] = jnp.min(m_new)

        @pl.when(qi == 0)
        def _():
            stream.start(qi, 0)

        @pl.when(qi > 0)
        def _():
            stream.start(qi - 1, 1)

        m_s[...] = jnp.full_like(m_s, NEG)
        acc_s[...] = jnp.zeros_like(acc_s)
        stream.wait(qi, 0)
        for hh in range(2):
            head_step(hh, 0, True)

        @pl.when(qi > 1)
        def _():
            stream.start(qi - 2, 0)

        @pl.when(qi > 0)
        def _():
            stream.wait(qi - 1, 1)
            for hh in range(2):
                head_step(hh, 1, False)

        k0 = [_first_live_key(st_ref, 2 * pair + hh, nb, qi, qi - 2, m_min[hh] - FOX_SKIP) for hh in range(2)]
        n = qi - 1 - jnp.minimum(k0[0], k0[1])

        @pl.when((qi > 1) & (n == 0))
        def _():
            stream.wait(qi - 2, 0)

        def loop(t, carry):
            k = qi - 2 - t
            slot = t % 2
            stream.wait(k, slot)

            @pl.when(t + 1 < n)
            def _():
                stream.start(k - 1, 1 - slot)

            for hh in range(2):
                @pl.when(k >= k0[hh])
                def _():
                    head_step(hh, slot, False)
            return carry

        lax.fori_loop(0, n, loop, 0)

        @pl.when(qi + 1 < nb)
        def _():
            stream.start(qi + 1, 0)

        lane = lax.broadcasted_iota(jnp.int32, (tb, 128), 1)
        outs = []
        for hh in range(2):
            acc = acc_s[hh]
            b0 = _bias_lane(hh)
            l = acc[:, b0:b0 + 1]
            outs.append(acc / l)
            lse = m_s[hh] + jnp.log(l)
            lse_ref[hh, 0] = jnp.broadcast_to(jnp.min(lse, axis=0, keepdims=True), (1, 128))
            qf = qa_ref[hh].astype(F32)
            cb = qf[:, b0:b0 + 1] + qf[:, b0 + 1:b0 + 2] + qf[:, b0 + 2:b0 + 3] - lse
            qb_ref[hh] = _lane_fill(lane, qf, _split3(cb), b0).astype(BF16)
        o_ref[...] = _pair_lanes(lane, outs[0], outs[1])

    qs = pl.BlockSpec((2, tb, 128), lambda p, i: (p, i, 0))
    return pl.pallas_call(
        body, name=name, grid=(FOX_PAIRS, nb),
        in_specs=[qs, ANY, ANY, SMEM],
        out_specs=[pl.BlockSpec((tb, 128), lambda p, i: (i, p)), qs,
                   pl.BlockSpec((2, 1, 1, 128), lambda p, i: (p, i, 0, 0))],
        out_shape=[jax.ShapeDtypeStruct((S, FOX_HEADS * FOX_DH), F32), jax.ShapeDtypeStruct((FOX_HEADS, S, 128), BF16),
                   jax.ShapeDtypeStruct((FOX_HEADS, nb, 1, 128), F32)],
        scratch_shapes=[pltpu.VMEM((2, 2, tb, 128), BF16), pltpu.VMEM((2, 2, tb, 128), BF16),
                        pltpu.SemaphoreType.DMA((2, 2)), pltpu.VMEM((2, tb, 1), F32), pltpu.VMEM((2, tb, 128), F32),
                        pltpu.SMEM((2,), F32)],
        compiler_params=_cparams(("arbitrary", "arbitrary")),
    )(qa, ka, va, bounds)


def _fox_bwd_prep(o, do, *, name, T=512):
    S = o.shape[0]
    T = min(T, S)

    def body(o_ref, do_ref, dob_ref):
        lane = lax.broadcasted_iota(jnp.int32, (T, 128), 1)
        d = do_ref[...].astype(F32)
        prod = d * o_ref[...]
        for hh in range(2):
            mine = _data_lanes(lane, hh)
            delta = jnp.sum(jnp.where(mine, prod, 0.0), axis=-1, keepdims=True)
            dob_ref[hh] = _lane_fill(lane, jnp.where(mine, d, 0.0), _split3(-delta), _bias_lane(hh)).astype(BF16)

    tok = pl.BlockSpec((T, 128), lambda p, t: (t, p))
    return pl.pallas_call(
        body, name=name, grid=(FOX_PAIRS, S // T),
        in_specs=[tok, tok], out_specs=pl.BlockSpec((2, T, 128), lambda p, t: (p, t, 0)),
        out_shape=jax.ShapeDtypeStruct((FOX_HEADS, S, 128), BF16),
        compiler_params=_cparams(("parallel", "parallel")),
    )(o, do)


def _fox_bwd_dq(qb, ka, va, dob, bounds, lse_min, *, name, comm=None):
    S = qb.shape[1]
    tb = min(FOX_TB, S)
    nb = S // tb
    nc = comm.n if comm is not None else 0

    def body(qb_ref, dob_ref, ka_hbm, va_hbm, st_ref, lm_ref, *rest):
        c_in, (dq_ref, dcs_ref), c_out = rest[:nc], rest[nc:nc + 2], rest[nc + 2:2 * nc + 2]
        kbuf, vbuf, sems, acc_s = rest[2 * nc + 2:2 * nc + 6]
        c_sems = rest[2 * nc + 6:]
        pair, qi = pl.program_id(0), pl.program_id(1)
        if comm is not None:
            @pl.when((pair == 0) & (qi == 0))
            def _():
                comm.start(c_in, c_out, c_sems)

        stream = _BlockStream((ka_hbm, va_hbm), (kbuf, vbuf), sems, pair, tb)
        k0 = [_first_live_key(st_ref, 2 * pair + hh, nb, qi, qi - 1, lm_ref[(2 * pair + hh) * nb + qi] - FOX_SKIP)
              for hh in range(2)]
        n = qi - jnp.minimum(k0[0], k0[1]) + 1

        @pl.when(qi == 0)
        def _():
            stream.start(qi, 0)

        acc_s[...] = jnp.zeros_like(acc_s)
        dcs_ref[...] = jnp.zeros_like(dcs_ref)

        def head_step(hh, slot, k, masked):
            s = _dot(qb_ref[hh], kbuf[slot, hh], _DIMS["nt"])
            if masked:
                row = lax.broadcasted_iota(jnp.int32, (tb, tb), 0)
                col = lax.broadcasted_iota(jnp.int32, (tb, tb), 1)
                s = jnp.where(col <= row, s, NEG)
            ds = jnp.exp(s) * _dot(dob_ref[hh], vbuf[slot, hh], _DIMS["nt"])
            dcs_ref[0, 0, hh:hh + 1, pl.ds(pl.multiple_of(k * tb, tb), tb)] = jnp.sum(ds, axis=0, keepdims=True)
            acc_s[hh] += _dot(ds.astype(BF16), kbuf[slot, hh], _DIMS["nn"])

        def loop(t, carry):
            k = qi - t
            slot = t % 2
            stream.wait(k, slot)

            @pl.when(t + 1 < n)
            def _():
                stream.start(k - 1, 1 - slot)

            @pl.when(t == 0)
            def _():
                for hh in range(2):
                    head_step(hh, slot, k, True)

            for hh in range(2):
                @pl.when((t > 0) & (k >= k0[hh]))
                def _():
                    head_step(hh, slot, k, False)
            return carry

        lax.fori_loop(0, n, loop, 0)

        @pl.when(qi + 1 < nb)
        def _():
            stream.start(qi + 1, 0)

        lane = lax.broadcasted_iota(jnp.int32, (tb, 128), 1)
        dq_ref[...] = (_pair_lanes(lane, acc_s[0], acc_s[1]) * FOX_SCALE).astype(dq_ref.dtype)
        if comm is not None:
            @pl.when((pair == FOX_PAIRS - 1) & (qi == nb - 1))
            def _():
                comm.finish(c_in, c_out, c_sems)

    qs = pl.BlockSpec((2, tb, 128), lambda p, i: (p, i, 0))
    outs = pl.pallas_call(
        body, name=name, grid=(FOX_PAIRS, nb),
        in_specs=[qs, qs, ANY, ANY, SMEM, SMEM] + [ANY] * nc,
        out_specs=[pl.BlockSpec((tb, 128), lambda p, i: (i, p)),
                   pl.BlockSpec((1, 1, 2, S), lambda p, i: (p, i, 0, 0))] + [ANY] * nc,
        out_shape=[jax.ShapeDtypeStruct((S, FOX_HEADS * FOX_DH), BF16),
                   jax.ShapeDtypeStruct((FOX_PAIRS, nb, 2, S), F32)] + (comm.out_shapes if comm is not None else []),
        scratch_shapes=[pltpu.VMEM((2, 2, tb, 128), BF16), pltpu.VMEM((2, 2, tb, 128), BF16),
                        pltpu.SemaphoreType.DMA((2, 2)), pltpu.VMEM((2, tb, 128), F32)]
        + (comm.scratch if comm is not None else []),
        compiler_params=_cparams(("arbitrary", "arbitrary")),
    )(qb, dob, ka, va, bounds, lse_min, *(comm.inputs if comm is not None else []))
    return (outs[0], outs[1]) if comm is None else (outs[0], outs[1], outs[2:])


def _fox_bwd_dkv(qb, ka, va, dob, bounds, lse_min, *, name):
    S = qb.shape[1]
    tb = min(FOX_TB, S)
    nb = S // tb

    def body(ka_ref, va_ref, qb_hbm, dob_hbm, st_ref, lm_ref, dk_ref, dv_ref, qbuf, dbuf, sems, dk_s, dv_s):
        pair, ki = pl.program_id(0), pl.program_id(1)
        stream = _BlockStream((qb_hbm, dob_hbm), (qbuf, dbuf), sems, pair, tb)
        i1 = [_last_live_query(st_ref, lm_ref, 2 * pair + hh, nb, ki) for hh in range(2)]
        n = jnp.maximum(i1[0], i1[1]) - ki + 1

        @pl.when(ki == 0)
        def _():
            stream.start(ki, 0)

        dk_s[...] = jnp.zeros_like(dk_s)
        dv_s[...] = jnp.zeros_like(dv_s)

        def head_step(hh, slot, masked):
            st = _dot(ka_ref[hh], qbuf[slot, hh], _DIMS["nt"])
            if masked:
                row = lax.broadcasted_iota(jnp.int32, (tb, tb), 0)
                col = lax.broadcasted_iota(jnp.int32, (tb, tb), 1)
                st = jnp.where(row <= col, st, NEG)
            pt = jnp.exp(st)
            dst = pt * _dot(va_ref[hh], dbuf[slot, hh], _DIMS["nt"])
            dv_s[hh] += _dot(pt.astype(BF16), dbuf[slot, hh], _DIMS["nn"])
            dk_s[hh] += _dot(dst.astype(BF16), qbuf[slot, hh], _DIMS["nn"])

        def loop(t, carry):
            i = ki + t
            slot = t % 2
            stream.wait(i, slot)

            @pl.when(t + 1 < n)
            def _():
                stream.start(i + 1, 1 - slot)

            @pl.when(t == 0)
            def _():
                for hh in range(2):
                    head_step(hh, slot, True)

            for hh in range(2):
                @pl.when((t > 0) & (i <= i1[hh]))
                def _():
                    head_step(hh, slot, False)
            return carry

        lax.fori_loop(0, n, loop, 0)

        @pl.when(ki + 1 < nb)
        def _():
            stream.start(ki + 1, 0)

        lane = lax.broadcasted_iota(jnp.int32, (tb, 128), 1)
        dk_ref[...] = _pair_lanes(lane, dk_s[0], dk_s[1]).astype(dk_ref.dtype)
        dv_ref[...] = _pair_lanes(lane, dv_s[0], dv_s[1]).astype(dv_ref.dtype)

    ks = pl.BlockSpec((2, tb, 128), lambda p, j: (p, j, 0))
    tok = pl.BlockSpec((tb, 128), lambda p, j: (j, p))
    big = jax.ShapeDtypeStruct((S, FOX_HEADS * FOX_DH), BF16)
    return pl.pallas_call(
        body, name=name, grid=(FOX_PAIRS, nb),
        in_specs=[ks, ks, ANY, ANY, SMEM, SMEM], out_specs=[tok, tok], out_shape=[big, big],
        scratch_shapes=[pltpu.VMEM((2, 2, tb, 128), BF16), pltpu.VMEM((2, 2, tb, 128), BF16),
                        pltpu.SemaphoreType.DMA((2, 2)), pltpu.VMEM((2, tb, 128), F32), pltpu.VMEM((2, tb, 128), F32)],
        compiler_params=_cparams(("arbitrary", "arbitrary")),
    )(ka, va, qb, dob, bounds, lse_min)


def _merge_fwd(proj, pa, pb, *, name, T=512):
    S, D = pa.shape
    T = min(T, S)

    def body(ga_ref, gb_ref, pa_ref, pb_ref, m_ref):
        m_ref[...] = (_sigmoid(ga_ref[...]) * pa_ref[...] + _sigmoid(gb_ref[...]) * pb_ref[...]).astype(m_ref.dtype)

    tok = pl.BlockSpec((T, D), lambda i: (i, 0))
    return pl.pallas_call(
        body, name=name, grid=(S // T,),
        in_specs=[pl.BlockSpec((T, D), lambda i: (i, 7)), pl.BlockSpec((T, D), lambda i: (i, 8)), tok, tok],
        out_specs=tok, out_shape=jax.ShapeDtypeStruct((S, D), BF16),
        compiler_params=_cparams(("parallel",)),
    )(proj, proj, pa, pb)


def _merge_bwd(proj, pa, pb, dm, *, name, T=512):
    S, D = pa.shape
    T = min(T, S)

    def body(ga_ref, gb_ref, pa_ref, pb_ref, dm_ref, dpa_ref, dpb_ref, dga_ref, dgb_ref):
        dm_ = dm_ref[...]
        sa, sb = _sigmoid(ga_ref[...]), _sigmoid(gb_ref[...])
        dpa_ref[...] = (dm_ * sa).astype(BF16)
        dpb_ref[...] = (dm_ * sb).astype(BF16)
        dga_ref[...] = (dm_ * pa_ref[...] * sa * (1.0 - sa)).astype(BF16)
        dgb_ref[...] = (dm_ * pb_ref[...] * sb * (1.0 - sb)).astype(BF16)

    tok = pl.BlockSpec((T, D), lambda i: (i, 0))
    big = jax.ShapeDtypeStruct((S, D), BF16)
    return pl.pallas_call(
        body, name=name, grid=(S // T,),
        in_specs=[pl.BlockSpec((T, D), lambda i: (i, 7)), pl.BlockSpec((T, D), lambda i: (i, 8)), tok, tok, tok],
        out_specs=[tok, tok, tok, tok], out_shape=[big, big, big, big],
        compiler_params=_cparams(("parallel",)),
    )(proj, proj, pa, pb, dm)


INV_SQRT2 = 0.7071067811865476
INV_SQRT2PI = 0.3989422804014327


def _shifted(u, prev, rid):
    m1 = jnp.where(rid == 0, prev[7:8, :], pltpu.roll(u, 1, 0))
    m2 = jnp.where(rid == 0, prev[6:7, :], jnp.where(rid == 1, prev[7:8, :], pltpu.roll(u, 2, 0)))
    return m1, m2


def _conv_acc(u, prev, w_ref, b_ref, rid):
    m1, m2 = _shifted(u, prev, rid)
    return b_ref[...] + w_ref[0:1, :] * m2 + w_ref[1:2, :] * m1 + w_ref[2:3, :] * u, m1, m2


def _convglu_fwd(ug, uv, wg, wv, bg, bv, *, name, T=512, tc=256):
    S, F = ug.shape
    T = min(T, S)

    def body(ug_ref, uv_ref, wg_ref, wv_ref, bg_ref, bv_ref, a_ref, pg, pv):
        @pl.when(pl.program_id(1) == 0)
        def _():
            pg[...] = jnp.zeros_like(pg)
            pv[...] = jnp.zeros_like(pv)

        rid = lax.broadcasted_iota(jnp.int32, (T, tc), 0)
        g_, v_ = ug_ref[...], uv_ref[...]
        accg, _, _ = _conv_acc(g_, pg[...], wg_ref, bg_ref, rid)
        accv, _, _ = _conv_acc(v_, pv[...], wv_ref, bv_ref, rid)
        gel = 0.5 * accg * (1.0 + lax.erf(accg * INV_SQRT2))
        a_ref[...] = (gel * accv).astype(a_ref.dtype)
        pg[...] = g_[T - 8:T, :]
        pv[...] = v_[T - 8:T, :]

    tok = pl.BlockSpec((T, tc), lambda j, t: (t, j))
    w3 = pl.BlockSpec((3, tc), lambda j, t: (0, j))
    b1 = pl.BlockSpec((1, tc), lambda j, t: (0, j))
    return pl.pallas_call(
        body, name=name, grid=(F // tc, S // T),
        in_specs=[tok, tok, w3, w3, b1, b1], out_specs=tok,
        out_shape=jax.ShapeDtypeStruct((S, F), BF16),
        scratch_shapes=[pltpu.VMEM((8, tc), F32), pltpu.VMEM((8, tc), F32)],
        compiler_params=_cparams(("parallel", "arbitrary")),
    )(ug, uv, wg, wv, bg, bv)


def _convglu_bwd(ug, uv, wg, wv, bg, bv, da, *, name, T=512, tc=256):
    S, F = ug.shape
    T = min(T, S)
    nT = S // T
    halo_blocks = T // 8

    def up_shift(d, nx, rid):
        p1 = jnp.where(rid == T - 1, nx[0:1, :], pltpu.roll(d, T - 1, 0))
        p2 = jnp.where(rid == T - 1, nx[1:2, :], jnp.where(rid == T - 2, nx[0:1, :], pltpu.roll(d, T - 2, 0)))
        return p1, p2

    def body(ug_ref, uv_ref, hg_ref, hv_ref, wg_ref, wv_ref, bg_ref, bv_ref, da_ref,
             dug_ref, duv_ref, dwg_ref, dwv_ref, dbg_ref, dbv_ref, ng, nv):
        @pl.when(pl.program_id(1) == 0)
        def _():
            ng[...] = jnp.zeros_like(ng)
            nv[...] = jnp.zeros_like(nv)
            for r in (dwg_ref, dwv_ref, dbg_ref, dbv_ref):
                r[...] = jnp.zeros_like(r)

        first_block = pl.program_id(1) == nT - 1
        rid = lax.broadcasted_iota(jnp.int32, (T, tc), 0)
        g_, v_ = ug_ref[...], uv_ref[...]
        pg = jnp.where(first_block, 0.0, hg_ref[...])
        pv = jnp.where(first_block, 0.0, hv_ref[...])
        accg, g1, g2 = _conv_acc(g_, pg, wg_ref, bg_ref, rid)
        accv, v1, v2 = _conv_acc(v_, pv, wv_ref, bv_ref, rid)
        cdf = 0.5 * (1.0 + lax.erf(accg * INV_SQRT2))
        pdf = INV_SQRT2PI * jnp.exp(-0.5 * accg * accg)
        da_ = da_ref[...].astype(F32)
        dgate = da_ * accv * (cdf + accg * pdf)
        dval = da_ * (accg * cdf)
        dbg_ref[...] += jnp.sum(dgate, axis=0, keepdims=True)
        dbv_ref[...] += jnp.sum(dval, axis=0, keepdims=True)
        for j, (sg_, sv_) in enumerate(((g2, v2), (g1, v1), (g_, v_))):
            dwg_ref[j:j + 1, :] += jnp.sum(dgate * sg_, axis=0, keepdims=True)
            dwv_ref[j:j + 1, :] += jnp.sum(dval * sv_, axis=0, keepdims=True)
        for d, w_ref, nx, out_ref in ((dgate, wg_ref, ng, dug_ref), (dval, wv_ref, nv, duv_ref)):
            p1, p2 = up_shift(d, nx[...], rid)
            out_ref[...] = (w_ref[2:3, :] * d + w_ref[1:2, :] * p1 + w_ref[0:1, :] * p2).astype(out_ref.dtype)
            nx[...] = d[0:8, :]

    tok = pl.BlockSpec((T, tc), lambda j, t: (nT - 1 - t, j))
    halo = pl.BlockSpec((8, tc), lambda j, t: (jnp.maximum((nT - 1 - t) * halo_blocks - 1, 0), j))
    w3 = pl.BlockSpec((3, tc), lambda j, t: (0, j))
    b1 = pl.BlockSpec((1, tc), lambda j, t: (0, j))
    big = jax.ShapeDtypeStruct((S, F), BF16)
    return pl.pallas_call(
        body, name=name, grid=(F // tc, nT),
        in_specs=[tok, tok, halo, halo, w3, w3, b1, b1, tok], out_specs=[tok, tok, w3, w3, b1, b1],
        out_shape=[big, big, jax.ShapeDtypeStruct((3, F), F32), jax.ShapeDtypeStruct((3, F), F32),
                   jax.ShapeDtypeStruct((1, F), F32), jax.ShapeDtypeStruct((1, F), F32)],
        scratch_shapes=[pltpu.VMEM((8, tc), F32), pltpu.VMEM((8, tc), F32)],
        compiler_params=_cparams(("parallel", "arbitrary")),
    )(ug, uv, ug, uv, wg, wv, bg, bv, da)


FF_LO = 7168
IN_SHARD = 1154
FF_DEV, FF_OFF = FF_LO // IN_SHARD, FF_LO % IN_SHARD


def _col_blocks(a, width):
    return jnp.stack([a[:, d * width:(d + 1) * width] for d in range(N_DEV)])


def _w_in_blocks(d_wm, d_wff):
    def block(d):
        lo = d * IN_SHARD
        if d < FF_DEV:
            return d_wm[:, lo:lo + IN_SHARD]
        if d > FF_DEV:
            return d_wm[:, lo - FOX_HEADS:lo - FOX_HEADS + IN_SHARD]
        return jnp.concatenate([d_wm[:, lo:FF_LO], d_wff[:, :FOX_HEADS], d_wm[:, FF_LO:lo + IN_SHARD - FOX_HEADS]], axis=1)

    return jnp.stack([block(d) for d in range(N_DEV)])


def _late_weights(g_a, g_b, g_o, g_up, g_cw, g_d):
    wup = jnp.concatenate([g_up[d] for d in range(N_DEV)], axis=1)
    cw = jnp.concatenate([g_cw[d] for d in range(N_DEV)], axis=1)
    return dict(wa=g_a.reshape(D_MODEL, D_MODEL), wb=g_b.reshape(D_MODEL, D_MODEL), wo=g_o.reshape(D_MODEL, D_MODEL),
                wug=wup[:, :D_FF], wuv=wup[:, D_FF:], cwg=cw[:, :D_FF], cwv=cw[:, D_FF:], wd=g_d.reshape(D_FF, D_MODEL))


def _early_grad_blocks(d_wa, d_wb, d_wo, d_wug, d_wuv, d_wd):
    up = jnp.stack([d_wug[:, d * 704:(d + 1) * 704] for d in range(4)]
                   + [d_wuv[:, d * 704:(d + 1) * 704] for d in range(4)])
    return [d_wa.reshape(N_DEV, 128, D_MODEL), d_wb.reshape(N_DEV, 128, D_MODEL), d_wo.reshape(N_DEV, 128, D_MODEL),
            up, d_wd.reshape(N_DEV, 352, D_MODEL)]


def _local_step(x, tgt, w, p, late=None, exchange=False):
    S = x.shape[0]
    mm = _matmul
    n1 = _rms_fwd(x, p["norm_mix"], name="rms1_fwd")
    if late is None:
        proj = mm(n1, w["wm"], "nn", name="proj_main")
    else:
        proj, gathered = mm(n1, w["wm"], "nn", comm=late, name="proj_main")
        w = {**w, **_late_weights(*gathered)}
    ff = mm(n1, w["wff"], "nn", name="proj_ff")
    lb = _lb_fwd(p["hg_lb_logits"], name="lb_fwd")
    gnorm = p["hg_norm"].reshape(1, HG_DV)
    o_hg, oa, states = _hgrn_fwd(proj, lb, gnorm, name="hgrn_fwd")
    bias = jnp.pad(p["fox_f_bias"].reshape(1, FOX_HEADS), ((0, 0), (0, 128 - FOX_HEADS)))
    c = _fox_gate_fwd(ff, bias, name="fox_gate_fwd")
    qa, ka, va, fox_stats = _fox_prep(proj, c, name="fox_prep")
    bounds = fox_stats[:, :, 0, :N_STAT].reshape(-1)
    ob, qb, lse_stats = _fox_fwd(qa, ka, va, bounds, name="fox_fwd")
    lse_min = lse_stats[:, :, 0, 0].reshape(-1)
    pa = mm(oa, w["wa"], "nn", name="branch_a")
    pb = mm(ob, w["wb"], "nn", name="branch_b")
    merged = _merge_fwd(proj, pa, pb, name="merge_fwd")
    h1 = mm(merged, w["wo"], "nn", addend=x, name="mix_out")
    n2 = _rms_fwd(h1, p["norm_ffn"], name="rms2_fwd")
    ug = mm(n2, w["wug"], "nn", name="up_gate")
    uv = mm(n2, w["wuv"], "nn", name="up_val")
    a = _convglu_fwd(ug, uv, w["cwg"], w["cwv"], p["cbg"], p["cbv"], name="convglu_fwd")
    dh2, d_norm_final, loss = mm(a, w["wd"], "nn", addend=h1, tail=_loss_tail(p["norm_final"], tgt), name="ffn_down")
    da = mm(dh2, w["wd"], "nt", out_dtype=BF16, name="d_act")
    d_wd = mm(a, dh2, "tn", out_dtype=BF16, name="dw_down")
    dug, duv, d_cwg, d_cwv, d_cbg, d_cbv = _convglu_bwd(
        ug, uv, w["cwg"], w["cwv"], p["cbg"], p["cbv"], da, name="convglu_bwd")
    dn2 = mm(dug, w["wug"], "nt", name="dn2_gate")
    dh1, d_norm_ffn = mm(duv, w["wuv"], "nt", addend=dn2, tail=_rms_bwd_tail(h1, p["norm_ffn"], dh2), name="dn2_val")
    d_wug = mm(n2, dug, "tn", out_dtype=BF16, name="dw_up_gate")
    d_wuv = mm(n2, duv, "tn", out_dtype=BF16, name="dw_up_val")
    dmerged = mm(dh1, w["wo"], "nt", name="d_merged")
    d_wo = mm(merged, dh1, "tn", out_dtype=BF16, name="dw_out")
    dpa, dpb, dga, dgb = _merge_bwd(proj, pa, pb, dmerged, name="merge_bwd")
    doa = mm(dpa, w["wa"], "nt", name="d_oa")
    dob = mm(dpb, w["wb"], "nt", out_dtype=BF16, name="d_ob")
    d_wa = mm(oa, dpa, "tn", out_dtype=BF16, name="dw_branch_a")
    d_wb = mm(ob, dpb, "tn", out_dtype=BF16, name="dw_branch_b")
    dhq, dhf, dhi, dhg, dlb, dgn8 = _hgrn_bwd(proj, lb, gnorm, o_hg, states, doa, name="hgrn_bwd")
    d_logits = _lb_bwd(p["hg_lb_logits"], dlb, name="lb_bwd")
    dob_hm = _fox_bwd_prep(ob, dob, name="fox_bwd_prep")
    early_parts = None
    if exchange:
        comm = _ExchangeComm(_early_grad_blocks(d_wa, d_wb, d_wo, d_wug, d_wuv, d_wd))
        dq, dcsp, early_parts = _fox_bwd_dq(qb, ka, va, dob_hm, bounds, lse_min, comm=comm, name="fox_bwd_dq")
    else:
        dq, dcsp = _fox_bwd_dq(qb, ka, va, dob_hm, bounds, lse_min, name="fox_bwd_dq")
    dk, dv = _fox_bwd_dkv(qb, ka, va, dob_hm, bounds, lse_min, name="fox_bwd_dkv")
    dcs = jnp.sum(dcsp, axis=1)
    dcs_tok = jnp.pad(dcs.reshape(FOX_HEADS, S).T, ((0, 0), (0, 128 - FOX_HEADS)))
    dff, dbias = _fox_gate_bwd(ff, bias, dcs_tok, name="fox_gate_bwd")
    dproj = jnp.concatenate([dhq, dhf, dhi, dhg, dq, dk, dv, dga, dgb], axis=1)
    d_wm = mm(n1, dproj, "tn", out_dtype=BF16, name="dw_in_main")
    d_wff = mm(n1, dff, "tn", out_dtype=BF16, name="dw_in_ff")
    dn1 = mm(dff, w["wff"], "nt", name="dn1_ff")
    late_parts = None
    if exchange:
        d_cw = jnp.concatenate([d_cwg, d_cwv], axis=1)
        comm = _ExchangeComm([_w_in_blocks(d_wm, d_wff), _col_blocks(d_cw, 704)])
        (dx, d_norm_mix), late_parts = mm(dproj, w["wm"], "nt", addend=dn1, comm=comm,
                                          tail=_rms_bwd_tail(x, p["norm_mix"], dh1), name="dn1_main")
    else:
        dx, d_norm_mix = mm(dproj, w["wm"], "nt", addend=dn1, tail=_rms_bwd_tail(x, p["norm_mix"], dh1), name="dn1_main")
    grads = dict(
        wm=d_wm, wff=d_wff, wa=d_wa, wb=d_wb, wo=d_wo, wug=d_wug, wuv=d_wuv, cwg=d_cwg, cwv=d_cwv, wd=d_wd,
        norm_mix=d_norm_mix.reshape(-1), fox_f_bias=dbias[0, :FOX_HEADS], hg_lb_logits=d_logits,
        hg_norm=jnp.sum(dgn8, axis=0).reshape(-1), norm_ffn=d_norm_ffn.reshape(-1), cbg=d_cbg, cbv=d_cbv,
        norm_final=d_norm_final.reshape(-1), early_parts=early_parts, late_parts=late_parts)
    return loss, dx, grads


SMALL = [("norm_mix", (1, D_MODEL)), ("fox_f_bias", (1, FOX_HEADS)), ("hg_lb_logits", (2, HG_HEADS * HG_DK)),
         ("hg_norm", (1, HG_DV)), ("norm_ffn", (1, D_MODEL)), ("conv_b", (1, 2 * D_FF)), ("norm_final", (D_MODEL,))]
SMALL_ROWS = 88
SHARDED = [("w_in", (D_MODEL, 1154), 256), ("w_branch_a", (128, D_MODEL), 128), ("w_branch_b", (128, D_MODEL), 128),
           ("w_out", (128, D_MODEL), 128), ("w_up", (D_MODEL, 704), 256), ("conv_w", (3, 704), 3),
           ("w_down", (352, D_MODEL), 352)]
NAMES = ["norm_mix", "w_in", "fox_f_bias", "hg_lb_logits", "hg_norm", "w_branch_a", "w_branch_b", "w_out",
         "norm_ffn", "w_up", "conv_w", "conv_b", "w_down", "norm_final"]


def _size(shape):
    n = 1
    for s in shape:
        n *= s
    return n


def _adamw(parts, w, m, v, *, name, T):
    R, C = w.shape
    c1 = 1.0 / (1.0 - ADAM_B1 ** ADAM_STEP)
    c2 = 1.0 / (1.0 - ADAM_B2 ** ADAM_STEP)

    def body(p_ref, w_ref, m_ref, v_ref, g_ref, d_ref, nm_ref, nv_ref):
        g = p_ref[0].astype(F32)
        for s in range(1, N_DEV):
            g = g + p_ref[s].astype(F32)
        g_ref[...] = g
        nm = ADAM_B1 * m_ref[...] + (1.0 - ADAM_B1) * g
        nv = ADAM_B2 * v_ref[...] + (1.0 - ADAM_B2) * (g * g)
        nm_ref[...] = nm
        nv_ref[...] = nv
        d_ref[...] = -ADAM_LR * ((nm * c1) / (jnp.sqrt(nv * c2) + ADAM_EPS) + ADAM_WD * w_ref[...])

    blk = pl.BlockSpec((T, C), lambda i: (i, 0))
    out = jax.ShapeDtypeStruct((R, C), F32)
    return pl.pallas_call(
        body, name=name, grid=(R // T,),
        in_specs=[pl.BlockSpec((N_DEV, T, C), lambda i: (0, i, 0)), blk, blk, blk],
        out_specs=[blk, blk, blk, blk], out_shape=[out, out, out, out],
        compiler_params=_cparams(("parallel",)),
    )(parts, w, m, v)


def _pack_small(vals):
    flat = jnp.concatenate([vals[n].reshape(-1).astype(F32) for n, _ in SMALL])
    return jnp.pad(flat, (0, SMALL_ROWS * 128 - flat.shape[0])).reshape(SMALL_ROWS, 128)


def _unpack_small(buf):
    flat, out, off = buf.reshape(-1), {}, 0
    for n, shape in SMALL:
        out[n] = flat[off:off + _size(shape)].reshape(shape)
        off += _size(shape)
    return out


def kernel(x, norm_mix, w_in, fox_f_bias, hg_lb_logits, hg_norm, w_branch_a, w_branch_b, w_out, norm_ffn, w_up, conv_w, conv_b, w_down, norm_final, loss_target, m_norm_mix, m_w_in, m_fox_f_bias, m_hg_lb_logits, m_hg_norm, m_w_branch_a, m_w_branch_b, m_w_out, m_norm_ffn, m_w_up, m_conv_w, m_conv_b, m_w_down, m_norm_final, v_norm_mix, v_w_in, v_fox_f_bias, v_hg_lb_logits, v_hg_norm, v_w_branch_a, v_w_branch_b, v_w_out, v_norm_ffn, v_w_up, v_conv_w, v_conv_b, v_w_down, v_norm_final):
    wv = dict(norm_mix=norm_mix, w_in=w_in, fox_f_bias=fox_f_bias, hg_lb_logits=hg_lb_logits, hg_norm=hg_norm,
              w_branch_a=w_branch_a, w_branch_b=w_branch_b, w_out=w_out, norm_ffn=norm_ffn, w_up=w_up, conv_w=conv_w,
              conv_b=conv_b, w_down=w_down, norm_final=norm_final)
    mv = dict(norm_mix=m_norm_mix, w_in=m_w_in, fox_f_bias=m_fox_f_bias, hg_lb_logits=m_hg_lb_logits, hg_norm=m_hg_norm,
              w_branch_a=m_w_branch_a, w_branch_b=m_w_branch_b, w_out=m_w_out, norm_ffn=m_norm_ffn, w_up=m_w_up,
              conv_w=m_conv_w, conv_b=m_conv_b, w_down=m_w_down, norm_final=m_norm_final)
    vv = dict(norm_mix=v_norm_mix, w_in=v_w_in, fox_f_bias=v_fox_f_bias, hg_lb_logits=v_hg_lb_logits, hg_norm=v_hg_norm,
              w_branch_a=v_w_branch_a, w_branch_b=v_w_branch_b, w_out=v_w_out, norm_ffn=v_norm_ffn, w_up=v_w_up,
              conv_w=v_conv_w, conv_b=v_conv_b, w_down=v_w_down, norm_final=v_norm_final)

    (g_in,) = _comm_call(_GatherComm([w_in[0].astype(BF16)]), name="gather_w_in")
    w = dict(wm=jnp.concatenate([g_in[d] for d in range(FF_DEV)]
                                + [g_in[FF_DEV][:, :FF_OFF], g_in[FF_DEV][:, FF_OFF + FOX_HEADS:]]
                                + [g_in[d] for d in range(FF_DEV + 1, N_DEV)], axis=1),
             wff=jnp.pad(g_in[FF_DEV][:, FF_OFF:FF_OFF + FOX_HEADS], ((0, 0), (0, 128 - FOX_HEADS))))
    late = _GatherComm([w_branch_a[0].astype(BF16), w_branch_b[0].astype(BF16), w_out[0].astype(BF16),
                        w_up[0].astype(BF16), conv_w[0], w_down[0].astype(BF16)])
    p = dict(norm_mix=norm_mix[0], fox_f_bias=fox_f_bias[0], hg_lb_logits=hg_lb_logits, hg_norm=hg_norm[0],
             norm_ffn=norm_ffn[0], cbg=conv_b[:, :D_FF], cbv=conv_b[:, D_FF:], norm_final=norm_final)
    loss, dx, grads = _local_step(x[0], loss_target[0], w, p, late=late, exchange=True)
    loss = lax.psum(loss[0, 0], ("x", "y", "c"))

    small = _pack_small(dict(
        norm_mix=grads["norm_mix"], fox_f_bias=grads["fox_f_bias"], hg_lb_logits=grads["hg_lb_logits"],
        hg_norm=grads["hg_norm"], norm_ffn=grads["norm_ffn"], conv_b=jnp.concatenate([grads["cbg"], grads["cbv"]], axis=1),
        norm_final=grads["norm_final"]))
    (small_parts,) = _comm_call(_ExchangeComm([jnp.broadcast_to(small[None], (N_DEV, SMALL_ROWS, 128))]),
                                name="exchange_small")
    ea, eb, eo, eup, ed = grads["early_parts"]
    p_in, p_cw = grads["late_parts"]
    parts = [p_in, ea, eb, eo, eup, p_cw, ed, small_parts]
    res = {}
    for (n, shape, tile), part in zip(SHARDED, parts):
        outs = _adamw(part, wv[n].reshape(shape), mv[n].reshape(shape), vv[n].reshape(shape), name="adamw_" + n, T=tile)
        res[n] = [o.reshape(wv[n].shape) for o in outs]
    outs = _adamw(parts[-1], _pack_small(wv), _pack_small(mv), _pack_small(vv), name="adamw_small", T=SMALL_ROWS)
    small = [_unpack_small(o) for o in outs]
    for n, _ in SMALL:
        res[n] = [s[n] for s in small]
    return (loss, dx[None], *[res[n][0] for n in NAMES], *[res[n][1] for n in NAMES],
            *[res[n][2] for n in NAMES], *[res[n][3] for n in NAMES])
```

```python
import jax
import jax.numpy as jnp
from jax import lax
from jax.experimental import pallas as pl
from jax.experimental.pallas import tpu as pltpu

F32 = jnp.float32
BF16 = jnp.bfloat16

D_MODEL = 1024
HG_HEADS = 8
HG_DK = 128
HG_DV = 128
HG_CHUNK = 64
FOX_HEADS = 16
FOX_DH = 64
D_FF = 2816
EPS = 1e-6
N_DEV = 8

ADAM_LR = 0.001
ADAM_B1 = 0.9
ADAM_B2 = 0.999
ADAM_EPS = 1e-08
ADAM_WD = 0.01
ADAM_STEP = 10

VMEM_LIMIT = 56 * 1024 * 1024


def _cparams(sem):
    return pltpu.CompilerParams(dimension_semantics=sem, vmem_limit_bytes=VMEM_LIMIT)


MESH = pl.DeviceIdType.MESH
ANY = pl.BlockSpec(memory_space=pl.ANY)
SMEM = pl.BlockSpec(memory_space=pltpu.SMEM)


class _GatherComm:
    def __init__(self, shards):
        self.inputs = list(shards)
        n = self.n = len(shards)
        self.out_shapes = [jax.ShapeDtypeStruct((N_DEV,) + s.shape, s.dtype) for s in shards]
        self.scratch = [pltpu.SemaphoreType.DMA((n, 7)), pltpu.SemaphoreType.DMA((n, 7)), pltpu.SemaphoreType.DMA((n,))]

    def _parts(self, x_refs, out_refs, sems):
        send_sems, recv_sems, local_sems = sems
        x, y, c = lax.axis_index("x"), lax.axis_index("y"), lax.axis_index("c")
        me, sibling = (x, y, c), (x, y, 1 - c)
        chips = [(1 - x, y), (x, 1 - y), (1 - x, 1 - y)]

        def copy(t, k, block, to, src=None):
            slot = out_refs[t].at[4 * block[0] + 2 * block[1] + block[2]]
            return pltpu.make_async_remote_copy(
                src_ref=slot if src is None else src, dst_ref=slot,
                send_sem=send_sems.at[t, k], recv_sem=recv_sems.at[t, k], device_id=to, device_id_type=MESH)

        mine = [pltpu.make_async_copy(x_refs[t], out_refs[t].at[4 * x + 2 * y + c], local_sems.at[t])
                for t in range(self.n)]
        first = []
        for t in range(self.n):
            first.append(copy(t, 0, me, sibling, src=x_refs[t]))
            first += [copy(t, 1 + j, me, (*chip, c), src=x_refs[t]) for j, chip in enumerate(chips)]
        return c, me, sibling, chips, copy, mine, first

    def start(self, x_refs, out_refs, sems):
        _, _, _, _, _, mine, first = self._parts(x_refs, out_refs, sems)
        for cp in mine + first:
            cp.start()

    def finish(self, x_refs, out_refs, sems):
        c, me, sibling, chips, copy, mine, first = self._parts(x_refs, out_refs, sems)
        passed = []
        for j, chip in enumerate(chips):
            for t in range(self.n):
                copy(t, 1 + j, (*chip, c), me).wait_recv()
                passed.append(copy(t, 4 + j, (*chip, c), sibling))
                passed[-1].start()
        for t in range(self.n):
            copy(t, 0, sibling, me).wait_recv()
            for j, chip in enumerate(chips):
                copy(t, 4 + j, (*chip, 1 - c), me).wait_recv()
        for cp in first + passed:
            cp.wait_send()
        for cp in mine:
            cp.wait()


class _ExchangeComm:
    def __init__(self, blocks):
        self.inputs = list(blocks)
        n = self.n = len(blocks)
        self.out_shapes = [jax.ShapeDtypeStruct(b.shape, b.dtype) for b in blocks]
        self.scratch = [pltpu.SemaphoreType.DMA((n, 7)), pltpu.SemaphoreType.DMA((n, 7)), pltpu.SemaphoreType.DMA((n,))]

    def _parts(self, g_refs, out_refs, sems):
        send_sems, recv_sems, local_sems = sems
        x, y, c = lax.axis_index("x"), lax.axis_index("y"), lax.axis_index("c")
        me = 4 * x + 2 * y + c
        mine = [pltpu.make_async_copy(g_refs[t].at[me], out_refs[t].at[me], local_sems.at[t]) for t in range(self.n)]
        sends, recvs = [], []
        for k in range(1, N_DEV):
            px = 1 - x if k & 4 else x
            py = 1 - y if k & 2 else y
            pc = 1 - c if k & 1 else c
            p = 4 * px + 2 * py + pc
            for t in range(self.n):
                sends.append(pltpu.make_async_remote_copy(
                    src_ref=g_refs[t].at[p], dst_ref=out_refs[t].at[me], send_sem=send_sems.at[t, k - 1],
                    recv_sem=recv_sems.at[t, k - 1], device_id=(px, py, pc), device_id_type=MESH))
                recvs.append(pltpu.make_async_remote_copy(
                    src_ref=g_refs[t].at[p], dst_ref=out_refs[t].at[p], send_sem=send_sems.at[t, k - 1],
                    recv_sem=recv_sems.at[t, k - 1], device_id=(px, py, pc), device_id_type=MESH))
        return mine, sends, recvs

    def start(self, g_refs, out_refs, sems):
        mine, sends, _ = self._parts(g_refs, out_refs, sems)
        for cp in mine + sends:
            cp.start()

    def finish(self, g_refs, out_refs, sems):
        mine, sends, recvs = self._parts(g_refs, out_refs, sems)
        for cp in recvs:
            cp.wait_recv()
        for cp in sends:
            cp.wait_send()
        for cp in mine:
            cp.wait()


def _comm_call(comm, *, name):
    n = comm.n

    def body(*refs):
        comm.start(refs[:n], refs[n:2 * n], refs[2 * n:])
        comm.finish(refs[:n], refs[n:2 * n], refs[2 * n:])

    return pl.pallas_call(body, name=name, in_specs=[ANY] * n, out_specs=[ANY] * n, out_shape=comm.out_shapes,
                          scratch_shapes=comm.scratch)(*comm.inputs)


_DIMS = {
    "nn": (((1,), (0,)), ((), ())),
    "nt": (((1,), (1,)), ((), ())),
    "tn": (((0,), (0,)), ((), ())),
}

MATMUL_VMEM_BUDGET = 36 * 1024 * 1024
MAX_TILE = 1536


def _pick(n, prefs):
    for p in prefs:
        if n % p == 0:
            return p
    return n


def _tile_options(n):
    return [d for d in range(128, min(n, MAX_TILE) + 1, 128) if n % d == 0] or [n]


def _pick_tiles(M, N, tk, nk, sa, sb, so, has_addend, tm, tn):
    best = None
    for cm in ([tm] if tm else _tile_options(M)):
        for cn in ([tn] if tn else _tile_options(N)):
            need = 2 * (cm * tk * sa + tk * cn * sb + cm * cn * so + (cm * cn * 4 if has_addend else 0))
            need += cm * cn * 4 if nk > 1 else 0
            if need <= MATMUL_VMEM_BUDGET and (best is None or cm * cn > best[0] * best[1]
                                               or (cm * cn == best[0] * best[1] and cn > best[1])):
                best = (cm, cn)
    assert best is not None, (M, N, tk)
    return best


def _matmul(a, b, form, *, out_dtype=F32, addend=None, tm=None, tn=None, tk=None, comm=None, name):
    if form == "nn":
        (M, K), (K2, N) = a.shape, b.shape
    elif form == "nt":
        (M, K), (N, K2) = a.shape, b.shape
    else:
        (K, M), (K2, N) = a.shape, b.shape
    assert K == K2, (a.shape, b.shape, form)
    tk = tk or (K if K <= 2816 else _pick(K, (1024, 512, 256, 128)))
    nk = K // tk
    if tm is None or tn is None:
        tm, tn = _pick_tiles(M, N, tk, nk, a.dtype.itemsize, b.dtype.itemsize, jnp.dtype(out_dtype).itemsize,
                             addend is not None, tm, tn)
    assert M % tm == 0 and N % tn == 0 and K % tk == 0, (M, N, K, tm, tn, tk)
    dims = _DIMS[form]
    nc = comm.n if comm is not None else 0
    grid = (M // tm, N // tn, nk)

    def body(*refs):
        a_ref, b_ref = refs[:2]
        pos = 2
        add_ref = refs[pos] if addend is not None else None
        pos += addend is not None
        c_in, o_ref, c_out = refs[pos:pos + nc], refs[pos + nc], refs[pos + nc + 1:pos + 2 * nc + 1]
        pos += 2 * nc + 1
        acc_ref = refs[pos] if nk > 1 else None
        c_sems = refs[pos + (nk > 1):]
        if comm is not None:
            ids = [pl.program_id(d) for d in range(3)]

            @pl.when((ids[0] == 0) & (ids[1] == 0) & (ids[2] == 0))
            def _():
                comm.start(c_in, c_out, c_sems)

        def finish(r):
            if add_ref is not None:
                r = r + add_ref[...].astype(F32)
            o_ref[...] = r.astype(o_ref.dtype)

        part = lax.dot_general(a_ref[...].astype(BF16), b_ref[...].astype(BF16), dims, preferred_element_type=F32)
        if nk == 1:
            finish(part)
        else:
            k = pl.program_id(2)

            @pl.when(k == 0)
            def _():
                acc_ref[...] = part

            @pl.when(k > 0)
            def _():
                acc_ref[...] += part

            @pl.when(k == nk - 1)
            def _():
                finish(acc_ref[...])

        if comm is not None:
            @pl.when((ids[0] == grid[0] - 1) & (ids[1] == grid[1] - 1) & (ids[2] == grid[2] - 1))
            def _():
                comm.finish(c_in, c_out, c_sems)

    if form == "nn":
        a_spec = pl.BlockSpec((tm, tk), lambda i, j, k: (i, k))
        b_spec = pl.BlockSpec((tk, tn), lambda i, j, k: (k, j))
    elif form == "nt":
        a_spec = pl.BlockSpec((tm, tk), lambda i, j, k: (i, k))
        b_spec = pl.BlockSpec((tn, tk), lambda i, j, k: (j, k))
    else:
        a_spec = pl.BlockSpec((tk, tm), lambda i, j, k: (k, i))
        b_spec = pl.BlockSpec((tk, tn), lambda i, j, k: (k, j))
    o_spec = pl.BlockSpec((tm, tn), lambda i, j, k: (i, j))
    in_specs = [a_spec, b_spec] + ([o_spec] if addend is not None else [])
    args = (a, b) + ((addend,) if addend is not None else ())
    out_shape = jax.ShapeDtypeStruct((M, N), out_dtype)
    scratch = [pltpu.VMEM((tm, tn), F32)] if nk > 1 else []
    if comm is None:
        return pl.pallas_call(
            body, name=name, grid=grid, in_specs=in_specs, out_specs=o_spec, out_shape=out_shape,
            scratch_shapes=scratch, compiler_params=_cparams(("parallel", "parallel", "arbitrary")),
        )(*args)
    outs = pl.pallas_call(
        body, name=name, grid=grid, in_specs=in_specs + [ANY] * nc, out_specs=[o_spec] + [ANY] * nc,
        out_shape=[out_shape] + comm.out_shapes, scratch_shapes=scratch + comm.scratch,
        compiler_params=_cparams(("arbitrary", "arbitrary", "arbitrary")),
    )(*args, *comm.inputs)
    return outs[0], outs[1:]


def _rms_fwd(x, g, *, name, tm=512):
    M, D = x.shape
    tm = min(tm, M)

    def body(x_ref, g_ref, n_ref):
        xf = x_ref[...]
        r = lax.rsqrt(jnp.mean(xf * xf, axis=-1, keepdims=True) + EPS)
        n_ref[...] = (xf * r * g_ref[...]).astype(n_ref.dtype)

    return pl.pallas_call(
        body, name=name, grid=(M // tm,),
        in_specs=[pl.BlockSpec((tm, D), lambda i: (i, 0)), pl.BlockSpec((1, D), lambda i: (0, 0))],
        out_specs=pl.BlockSpec((tm, D), lambda i: (i, 0)),
        out_shape=jax.ShapeDtypeStruct((M, D), BF16),
        compiler_params=_cparams(("parallel",)),
    )(x, g.reshape(1, D))


def _rms_bwd(x, g, dn, dres, *, name, tm=512):
    M, D = x.shape
    tm = min(tm, M)

    def body(x_ref, g_ref, dn_ref, dres_ref, dx_ref, dg_ref):
        @pl.when(pl.program_id(0) == 0)
        def _():
            dg_ref[...] = jnp.zeros_like(dg_ref)

        xf = x_ref[...]
        r = lax.rsqrt(jnp.mean(xf * xf, axis=-1, keepdims=True) + EPS)
        xh = xf * r
        dn_ = dn_ref[...].astype(F32)
        dg_ref[...] += jnp.sum(dn_ * xh, axis=0, keepdims=True)
        dxh = dn_ * g_ref[...]
        dx = r * (dxh - xh * jnp.mean(dxh * xh, axis=-1, keepdims=True))
        dx_ref[...] = dres_ref[...] + dx

    row = pl.BlockSpec((tm, D), lambda i: (i, 0))
    vec = pl.BlockSpec((1, D), lambda i: (0, 0))
    return pl.pallas_call(
        body, name=name, grid=(M // tm,),
        in_specs=[row, vec, row, row], out_specs=[row, vec],
        out_shape=[jax.ShapeDtypeStruct((M, D), F32), jax.ShapeDtypeStruct((1, D), F32)],
        compiler_params=_cparams(("arbitrary",)),
    )(x, g.reshape(1, D), dn, dres)


def _loss_head(h, g, tgt, *, name, tm=512):
    M, D = h.shape
    tm = min(tm, M)

    def body(h_ref, g_ref, t_ref, loss_ref, dh_ref, dg_ref):
        @pl.when(pl.program_id(0) == 0)
        def _():
            dg_ref[...] = jnp.zeros_like(dg_ref)
            loss_ref[...] = jnp.zeros_like(loss_ref)

        xf = h_ref[...]
        r = lax.rsqrt(jnp.mean(xf * xf, axis=-1, keepdims=True) + EPS)
        xh = xf * r
        err = xh * g_ref[...] - t_ref[...]
        part = jnp.sum(jnp.mean(err * err, axis=-1, keepdims=True), axis=0, keepdims=True)
        loss_ref[...] += 0.5 * part
        dy = err * (1.0 / D)
        dg_ref[...] += jnp.sum(dy * xh, axis=0, keepdims=True)
        dxh = dy * g_ref[...]
        dh_ref[...] = r * (dxh - xh * jnp.mean(dxh * xh, axis=-1, keepdims=True))

    row = pl.BlockSpec((tm, D), lambda i: (i, 0))
    vec = pl.BlockSpec((1, D), lambda i: (0, 0))
    one = pl.BlockSpec((1, 1), lambda i: (0, 0))
    return pl.pallas_call(
        body, name=name, grid=(M // tm,),
        in_specs=[row, vec, row], out_specs=[one, row, vec],
        out_shape=[jax.ShapeDtypeStruct((1, 1), F32), jax.ShapeDtypeStruct((M, D), F32),
                   jax.ShapeDtypeStruct((1, D), F32)],
        compiler_params=_cparams(("arbitrary",)),
    )(h, g.reshape(1, D), tgt)


HG_MID = HG_CHUNK // 2 - 1
EXP_CAP = 80.0


def _sigmoid(x):
    return 1.0 / (1.0 + jnp.exp(-x))


def _dot(a, b, dims, precision=None):
    return lax.dot_general(a, b, dims, preferred_element_type=F32, precision=precision)


def _bdot(a, b, form):
    return _dot(a.astype(BF16), b.astype(BF16), _DIMS[form])


def _split2(x):
    hi = x.astype(BF16)
    return hi, (x - hi.astype(F32)).astype(BF16)


def _dot3(a, b, form):
    d = _DIMS[form]
    return _dot(a[0], b[0], d) + (_dot(a[0], b[1], d) + _dot(a[1], b[0], d))


def _hgrn_chunk_common(hq, hf, lbv, tril, rid):
    sq = _sigmoid(hq)
    q = hq * sq
    sg = _sigmoid(hf)
    f = lbv + (1.0 - lbv) * sg
    k = (1.0 - lbv) * (1.0 - sg)
    g = jnp.log(f)
    b = _dot(tril, g, _DIMS["nn"], precision=lax.Precision.HIGHEST)
    bref = jnp.sum(jnp.where(rid == HG_MID, b, 0.0), axis=0, keepdims=True)
    bend = jnp.sum(jnp.where(rid == HG_CHUNK - 1, b, 0.0), axis=0, keepdims=True)
    eb = jnp.exp(b)
    e1 = jnp.exp(jnp.minimum(b - bref, EXP_CAP))
    e2 = jnp.exp(jnp.minimum(bref - b, EXP_CAP))
    e3 = jnp.exp(bend - b)
    return sq, q, sg, f, k, bend, eb, e1, e2, e3


def _hgrn_fwd(proj, lb, gnorm, *, name, T=1024):
    S = proj.shape[0]
    T = min(T, S)
    nch = T // HG_CHUNK
    C = HG_CHUNK

    def body(hq_ref, hf_ref, hi_ref, hg_ref, lb_ref, gn_ref, o_ref, oa_ref, st_ref, state):
        @pl.when(pl.program_id(1) == 0)
        def _():
            state[...] = jnp.zeros_like(state)

        lbv = lb_ref[...]
        gn = gn_ref[...]
        row = lax.broadcasted_iota(jnp.int32, (C, C), 0)
        col = lax.broadcasted_iota(jnp.int32, (C, C), 1)
        causal = row >= col
        tril = causal.astype(F32)
        rid = lax.broadcasted_iota(jnp.int32, (C, HG_DK), 0)
        sls = [pl.ds(c * C, C) for c in range(nch)]
        pre = [_hgrn_chunk_common(hq_ref[sl, :], hf_ref[sl, :], lbv, tril, rid) for sl in sls]
        v_l = [hi_ref[sl, :].astype(BF16) for sl in sls]
        a_l, u_l = [], []
        for c in range(nch):
            _, q, _, _, k, _, _, e1, e2, e3 = pre[c]
            a_l.append(jnp.where(causal, _bdot(q * e1, k * e2, "nt"), 0.0))
            u_l.append(_bdot(v_l[c], k * e3, "tn"))
        o_l = [_bdot(a_l[c], v_l[c], "nn") for c in range(nch)]
        st = state[...]
        st_l = []
        for c in range(nch):
            st_l.append(st)
            st = st * jnp.exp(pre[c][5]) + u_l[c]
        state[...] = st
        for c in range(nch):
            st_ref[0, c] = st_l[c]
            o_l[c] = o_l[c] + _bdot(pre[c][1] * pre[c][6], st_l[c], "nt")
        for c in range(nch):
            o, hg = o_l[c], hg_ref[sls[c], :]
            o_ref[sls[c], :] = o
            r = lax.rsqrt(jnp.mean(o * o, axis=-1, keepdims=True) + EPS)
            oa_ref[sls[c], :] = (o * r * gn * (hg * _sigmoid(hg))).astype(oa_ref.dtype)

    def grp(gidx):
        return pl.BlockSpec((T, 128), lambda h, t: (t, gidx * 8 + h))

    return pl.pallas_call(
        body, name=name, grid=(HG_HEADS, S // T),
        in_specs=[grp(0), grp(1), grp(2), grp(3),
                  pl.BlockSpec((1, 128), lambda h, t: (0, h)), pl.BlockSpec((1, 128), lambda h, t: (0, 0))],
        out_specs=[pl.BlockSpec((T, 128), lambda h, t: (t, h)), pl.BlockSpec((T, 128), lambda h, t: (t, h)),
                   pl.BlockSpec((1, nch, HG_DV, HG_DK), lambda h, t: (h, t, 0, 0))],
        out_shape=[jax.ShapeDtypeStruct((S, HG_HEADS * HG_DV), F32), jax.ShapeDtypeStruct((S, HG_HEADS * HG_DV), BF16),
                   jax.ShapeDtypeStruct((HG_HEADS, S // C, HG_DV, HG_DK), F32)],
        scratch_shapes=[pltpu.VMEM((HG_DV, HG_DK), F32)],
        compiler_params=_cparams(("parallel", "arbitrary")),
    )(proj, proj, proj, proj, lb, gnorm)


def _hgrn_bwd(proj, lb, gnorm, o, states, doa, *, name, T=1024):
    S = proj.shape[0]
    T = min(T, S)
    nch = T // HG_CHUNK
    C = HG_CHUNK
    nT = S // T

    def body(hq_ref, hf_ref, hi_ref, hg_ref, lb_ref, gn_ref, o_ref, st_ref, doa_ref,
             dhq_ref, dhf_ref, dhi_ref, dhg_ref, dlb_ref, dgn_ref, dstate):
        @pl.when(pl.program_id(1) == 0)
        def _():
            dstate[...] = jnp.zeros_like(dstate)
            dlb_ref[...] = jnp.zeros_like(dlb_ref)
            dgn_ref[...] = jnp.zeros_like(dgn_ref)

        lbv = lb_ref[...]
        gn = gn_ref[...]
        row = lax.broadcasted_iota(jnp.int32, (C, C), 0)
        col = lax.broadcasted_iota(jnp.int32, (C, C), 1)
        causal = row >= col
        tril = causal.astype(F32)
        triu = (row <= col).astype(F32)
        rid = lax.broadcasted_iota(jnp.int32, (C, HG_DK), 0)
        rng = range(nch)
        sls = [pl.ds(c * C, C) for c in rng]
        pre = [_hgrn_chunk_common(hq_ref[sl, :], hf_ref[sl, :], lbv, tril, rid) for sl in sls]
        do2, dgn_acc = [], jnp.zeros((1, HG_DV), F32)
        for c in rng:
            hg, ov = hg_ref[sls[c], :], o_ref[sls[c], :]
            r = lax.rsqrt(jnp.mean(ov * ov, axis=-1, keepdims=True) + EPS)
            xh = ov * r
            sgg = _sigmoid(hg)
            d_oa = doa_ref[sls[c], :].astype(F32)
            dz = d_oa * (hg * sgg)
            dhg_ref[sls[c], :] = (d_oa * (xh * gn) * (sgg * (1.0 + hg * (1.0 - sgg)))).astype(dhg_ref.dtype)
            dgn_acc = dgn_acc + jnp.sum(dz * xh, axis=0, keepdims=True)
            dxh = dz * gn
            do2.append(_split2(r * (dxh - xh * jnp.mean(dxh * xh, axis=-1, keepdims=True))))
        dgn_ref[0] += dgn_acc
        qi = [pre[c][1] * pre[c][6] for c in rng]
        qp = [pre[c][1] * pre[c][7] for c in rng]
        kp = [pre[c][4] * pre[c][8] for c in rng]
        kend = [pre[c][4] * pre[c][9] for c in rng]
        qi2, qp2, kp2, kend2 = ([_split2(t) for t in lst] for lst in (qi, qp, kp, kend))
        v2 = [_split2(hi_ref[sl, :]) for sl in sls]
        st0 = [st_ref[0, c] for c in rng]
        a_l = [jnp.where(causal, _dot(qp2[c][0], kp2[c][0], _DIMS["nt"]), 0.0).astype(BF16) for c in rng]
        da2 = [_split2(jnp.where(causal, _dot3(do2[c], v2[c], "nt"), 0.0)) for c in rng]
        dqi = [_dot3(do2[c], _split2(st0[c]), "nn") for c in rng]
        w_l = [_dot3(do2[c], qi2[c], "tn") for c in rng]
        ds = dstate[...]
        ds1 = [None] * nch
        for c in reversed(rng):
            ds1[c] = ds
            ds = ds * jnp.exp(pre[c][5]) + w_l[c]
        dstate[...] = ds
        ds12 = [_split2(t) for t in ds1]
        dqp = [_dot3(da2[c], kp2[c], "nn") for c in rng]
        dkp = [_dot3(da2[c], qp2[c], "tn") for c in rng]
        dv = [_dot(a_l[c], do2[c][0], _DIMS["tn"]) + _dot(kend2[c][0], ds12[c][0], _DIMS["nt"]) for c in rng]
        dkend = [_dot3(v2[c], ds12[c], "nn") for c in rng]
        dq_l, dk_l, db_l = [], [], []
        for c in rng:
            _, _, _, _, _, bend, eb, e1, e2, e3 = pre[c]
            dq_l.append(dqi[c] * eb + dqp[c] * e1)
            dk_l.append(dkp[c] * e2 + dkend[c] * e3)
            db = dqi[c] * qi[c] + dqp[c] * qp[c] - dkp[c] * kp[c] - dkend[c] * kend[c]
            dbend = (jnp.sum(dkend[c] * kend[c], axis=0, keepdims=True)
                     + jnp.exp(bend) * jnp.sum(ds1[c] * st0[c], axis=0, keepdims=True))
            db_l.append(db + jnp.where(rid == C - 1, dbend, 0.0))
        dg = [_dot(triu, db_l[c], _DIMS["nn"], precision=lax.Precision.HIGHEST) for c in rng]
        dlb_acc = jnp.zeros((1, HG_DK), F32)
        for c in rng:
            sq, _, sg, f, _, _, _, _, _, _ = pre[c]
            hq = hq_ref[sls[c], :]
            df = dg[c] / f - dk_l[c]
            dlb_acc = dlb_acc + jnp.sum(df * (1.0 - sg), axis=0, keepdims=True)
            dhf_ref[sls[c], :] = (df * (1.0 - lbv) * sg * (1.0 - sg)).astype(dhf_ref.dtype)
            dhq_ref[sls[c], :] = (dq_l[c] * (sq * (1.0 + hq * (1.0 - sq)))).astype(dhq_ref.dtype)
            dhi_ref[sls[c], :] = dv[c].astype(dhi_ref.dtype)
        dlb_ref[...] += dlb_acc

    def grp(gidx):
        return pl.BlockSpec((T, 128), lambda h, t: (nT - 1 - t, gidx * 8 + h))

    tok = pl.BlockSpec((T, 128), lambda h, t: (nT - 1 - t, h))
    big = jax.ShapeDtypeStruct((S, HG_HEADS * HG_DV), BF16)
    return pl.pallas_call(
        body, name=name, grid=(HG_HEADS, nT),
        in_specs=[grp(0), grp(1), grp(2), grp(3),
                  pl.BlockSpec((1, 128), lambda h, t: (0, h)), pl.BlockSpec((1, 128), lambda h, t: (0, 0)),
                  tok, pl.BlockSpec((1, nch, HG_DV, HG_DK), lambda h, t: (h, nT - 1 - t, 0, 0)), tok],
        out_specs=[tok, tok, tok, tok, pl.BlockSpec((1, 128), lambda h, t: (0, h)),
                   pl.BlockSpec((1, 1, 128), lambda h, t: (h, 0, 0))],
        out_shape=[big, big, big, big, jax.ShapeDtypeStruct((1, HG_HEADS * HG_DK), F32),
                   jax.ShapeDtypeStruct((HG_HEADS, 1, HG_DV), F32)],
        scratch_shapes=[pltpu.VMEM((HG_DV, HG_DK), F32)],
        compiler_params=_cparams(("parallel", "arbitrary")),
    )(proj, proj, proj, proj, lb, gnorm, o, states, doa)


def _lb_fwd(logits, *, name):
    def body(l_ref, lb_ref):
        lb_ref[...] = _sigmoid(l_ref[0:1, :] - l_ref[1:2, :])

    return pl.pallas_call(body, name=name, out_shape=jax.ShapeDtypeStruct((1, logits.shape[1]), F32))(logits)


def _lb_bwd(logits, dlb, *, name):
    def body(l_ref, d_ref, o_ref):
        lbv = _sigmoid(l_ref[0:1, :] - l_ref[1:2, :])
        t = d_ref[...] * lbv * (1.0 - lbv)
        o_ref[0:1, :] = t
        o_ref[1:2, :] = -t

    return pl.pallas_call(body, name=name, out_shape=jax.ShapeDtypeStruct(logits.shape, F32))(logits, dlb)


NEG = -1e30
FOX_SCALE = FOX_DH ** -0.5
FOX_PAIRS = FOX_HEADS // 2


def _fox_gate_fwd(ff, bias, *, name, T=512):
    S = ff.shape[0]
    T = min(T, S)

    def body(ff_ref, b_ref, c_ref, carry):
        @pl.when(pl.program_id(0) == 0)
        def _():
            carry[...] = jnp.zeros_like(carry)

        z = ff_ref[...] + b_ref[...]
        logf = jnp.minimum(z, 0.0) - jnp.log(1.0 + jnp.exp(-jnp.abs(z)))
        row = lax.broadcasted_iota(jnp.int32, (T, T), 0)
        col = lax.broadcasted_iota(jnp.int32, (T, T), 1)
        c = _dot((row >= col).astype(F32), logf, _DIMS["nn"], precision=lax.Precision.HIGHEST) + carry[...]
        c_ref[...] = c
        carry[...] = c[T - 1:T, :]

    return pl.pallas_call(
        body, name=name, grid=(S // T,),
        in_specs=[pl.BlockSpec((T, 128), lambda i: (i, 0)), pl.BlockSpec((1, 128), lambda i: (0, 0))],
        out_specs=pl.BlockSpec((T, 128), lambda i: (i, 0)),
        out_shape=jax.ShapeDtypeStruct((S, 128), F32),
        scratch_shapes=[pltpu.VMEM((1, 128), F32)],
        compiler_params=_cparams(("arbitrary",)),
    )(ff, bias)


def _fox_gate_bwd(ff, bias, dcs, *, name, T=512):
    S = ff.shape[0]
    T = min(T, S)
    nT = S // T

    def body(ff_ref, b_ref, d_ref, dff_ref, db_ref, carry):
        @pl.when(pl.program_id(0) == 0)
        def _():
            carry[...] = jnp.zeros_like(carry)
            db_ref[...] = jnp.zeros_like(db_ref)

        row = lax.broadcasted_iota(jnp.int32, (T, T), 0)
        col = lax.broadcasted_iota(jnp.int32, (T, T), 1)
        dlogf = carry[...] - _dot((row <= col).astype(F32), d_ref[...], _DIMS["nn"], precision=lax.Precision.HIGHEST)
        carry[...] = dlogf[0:1, :]
        dff = dlogf * (1.0 - _sigmoid(ff_ref[...] + b_ref[...]))
        dff_ref[...] = dff.astype(dff_ref.dtype)
        db_ref[...] += jnp.sum(dff, axis=0, keepdims=True)

    rev = pl.BlockSpec((T, 128), lambda i: (nT - 1 - i, 0))
    vec = pl.BlockSpec((1, 128), lambda i: (0, 0))
    return pl.pallas_call(
        body, name=name, grid=(nT,),
        in_specs=[rev, vec, rev], out_specs=[rev, vec],
        out_shape=[jax.ShapeDtypeStruct((S, 128), BF16), jax.ShapeDtypeStruct((1, 128), F32)],
        scratch_shapes=[pltpu.VMEM((1, 128), F32)],
        compiler_params=_cparams(("arbitrary",)),
    )(ff, bias, dcs)


AUG = FOX_DH
RSUM_LANE = 6


def _bias_lane(hh):
    return AUG * (1 - hh)


def _data_lanes(lane, hh):
    return (lane < AUG) if hh == 0 else (lane >= AUG)


def _split3(x):
    a = x.astype(BF16).astype(F32)
    r = x - a
    b = r.astype(BF16).astype(F32)
    return a, b, r - b


def _lane_fill(lane, base, pieces, start):
    for i, pc in enumerate(pieces):
        base = jnp.where(lane == start + i, pc, base)
    return base


FOX_TB = 512
FOX_SKIP = 40.0
N_STAT = 4


def _fox_prep(proj, c_tok, *, name):
    S = proj.shape[0]
    T = min(FOX_TB, S)

    def body(q_ref, k_ref, v_ref, c_ref, qa_ref, ka_ref, va_ref, st_ref):
        pair = pl.program_id(0)
        lane = lax.broadcasted_iota(jnp.int32, (T, 128), 1)
        lane1 = lax.broadcasted_iota(jnp.int32, (1, 128), 1)
        c = c_ref[...]
        q, k, v = q_ref[...], k_ref[...], v_ref[...]
        for hh in range(2):
            data, b0 = _data_lanes(lane, hh), _bias_lane(hh)
            ones3 = jnp.where((lane >= b0) & (lane < b0 + 3), 1.0, 0.0)

            def max_norm(t):
                tr = jnp.where(data, t.astype(BF16).astype(F32), 0.0)
                return jnp.sqrt(jnp.max(jnp.sum(tr * tr, axis=-1, keepdims=True), axis=0, keepdims=True))

            ch = jnp.sum(jnp.where(lane == 2 * pair + hh, c, 0.0), axis=-1, keepdims=True)
            c1, c2, c3 = _split3(ch)
            aug_q = _lane_fill(lane, jnp.where((lane >= b0 + 3) & (lane < b0 + 6), 1.0, 0.0), (c1, c2, c3), b0)
            aug_k = _lane_fill(lane, ones3, (-c1, -c2, -c3), b0 + 3)
            qa_ref[hh] = jnp.where(data, q * FOX_SCALE, aug_q).astype(BF16)
            ka_ref[hh] = jnp.where(data, k, aug_k).astype(BF16)
            va_ref[hh] = jnp.where(data, v, ones3).astype(BF16)
            stats = (max_norm(q * FOX_SCALE), jnp.max(ch, axis=0, keepdims=True), max_norm(k),
                     jnp.min(ch, axis=0, keepdims=True))
            st_ref[hh, 0] = _lane_fill(lane1, jnp.zeros((1, 128), F32), stats, 0)

    def grp(g):
        return pl.BlockSpec((T, 128), lambda p, t: (t, g * 8 + p))

    hm = pl.BlockSpec((2, T, 128), lambda p, t: (p, t, 0))
    out = jax.ShapeDtypeStruct((FOX_HEADS, S, 128), BF16)
    return pl.pallas_call(
        body, name=name, grid=(FOX_PAIRS, S // T),
        in_specs=[grp(4), grp(5), grp(6), pl.BlockSpec((T, 128), lambda p, t: (t, 0))],
        out_specs=[hm, hm, hm, pl.BlockSpec((2, 1, 1, 128), lambda p, t: (p, t, 0, 0))],
        out_shape=[out, out, out, jax.ShapeDtypeStruct((FOX_HEADS, S // T, 1, 128), F32)],
        compiler_params=_cparams(("parallel", "parallel")),
    )(proj, proj, proj, c_tok)


def _fox_bound(st_ref, head, nb, qi, ki):
    qb_, kb_ = (head * nb + qi) * N_STAT, (head * nb + ki) * N_STAT
    return st_ref[qb_] * st_ref[kb_ + 2] + st_ref[qb_ + 1] - st_ref[kb_ + 3] + 0.01


def _pair_lanes(lane, a0, a1):
    return jnp.where(lane < AUG, a0, a1)


def _first_live_key(st_ref, head, nb, qi, newest, thr):
    def body(t, k0):
        k = newest - t
        return jnp.where(_fox_bound(st_ref, head, nb, qi, k) > thr, k, k0)

    return lax.fori_loop(0, newest + 1, body, newest + 1)


def _last_live_query(st_ref, lm_ref, head, nb, ki):
    def body(t, i1):
        i = ki + 1 + t
        live = _fox_bound(st_ref, head, nb, i, ki) > lm_ref[head * nb + i] - FOX_SKIP
        return jnp.where(live, i, i1)

    return lax.fori_loop(0, nb - 1 - ki, body, ki)


class _BlockStream:
    def __init__(self, hbm_refs, bufs, sems, pair, tb):
        self.hbm, self.bufs, self.sems, self.pair, self.tb = hbm_refs, bufs, sems, pair, tb

    def _copies(self, blk, slot):
        rows = pl.ds(pl.multiple_of(blk * self.tb, self.tb), self.tb)
        return [pltpu.make_async_copy(h.at[pl.ds(2 * self.pair, 2), rows, :], b.at[slot], self.sems.at[n, slot])
                for n, (h, b) in enumerate(zip(self.hbm, self.bufs))]

    def start(self, blk, slot):
        for cp in self._copies(blk, slot):
            cp.start()

    def wait(self, blk, slot):
        for cp in self._copies(blk, slot):
            cp.wait()


def _fox_fwd(qa, ka, va, bounds, *, name):
    S = qa.shape[1]
    tb = min(FOX_TB, S)
    nb = S // tb

    def body(qa_ref, ka_hbm, va_hbm, st_ref, o_ref, qb_ref, lse_ref, kbuf, vbuf, sems, m_s, acc_s, m_min):
        pair, qi = pl.program_id(0), pl.program_id(1)
        stream = _BlockStream((ka_hbm, va_hbm), (kbuf, vbuf), sems, pair, tb)

        def head_step(hh, slot, masked):
            s = _dot(qa_ref[hh], kbuf[slot, hh], _DIMS["nt"])
            if masked:
                row = lax.broadcasted_iota(jnp.int32, (tb, tb), 0)
                col = lax.broadcasted_iota(jnp.int32, (tb, tb), 1)
                s = jnp.where(col <= row, s, NEG)
            m_old = m_s[hh]
            m_new = jnp.maximum(m_old, jnp.max(s, axis=-1, keepdims=True))
            p = jnp.exp(s - m_new)
            acc_s[hh] = jnp.exp(m_old - m_new) * acc_s[hh] + _dot(p.astype(BF16), vbuf[slot, hh], _DIMS["nn"])
            m_s[hh] = m_new
            m_min[hh] = jnp.min(m_new)

        @pl.when(qi == 0)
        def _():
            stream.start(qi, 0)

        @pl.when(qi > 0)
        def _():
            stream.start(qi - 1, 1)

        m_s[...] = jnp.full_like(m_s, NEG)
        acc_s[...] = jnp.zeros_like(acc_s)
        stream.wait(qi, 0)
        for hh in range(2):
            head_step(hh, 0, True)

        @pl.when(qi > 1)
        def _():
            stream.start(qi - 2, 0)

        @pl.when(qi > 0)
        def _():
            stream.wait(qi - 1, 1)
            for hh in range(2):
                head_step(hh, 1, False)

        k0 = [_first_live_key(st_ref, 2 * pair + hh, nb, qi, qi - 2, m_min[hh] - FOX_SKIP) for hh in range(2)]
        n = qi - 1 - jnp.minimum(k0[0], k0[1])

        @pl.when((qi > 1) & (n == 0))
        def _():
            stream.wait(qi - 2, 0)

        def loop(t, carry):
            k = qi - 2 - t
            slot = t % 2
            stream.wait(k, slot)

            @pl.when(t + 1 < n)
            def _():
                stream.start(k - 1, 1 - slot)

            for hh in range(2):
                @pl.when(k >= k0[hh])
                def _():
                    head_step(hh, slot, False)
            return carry

        lax.fori_loop(0, n, loop, 0)

        @pl.when(qi + 1 < nb)
        def _():
            stream.start(qi + 1, 0)

        lane = lax.broadcasted_iota(jnp.int32, (tb, 128), 1)
        outs = []
        for hh in range(2):
            acc = acc_s[hh]
            b0 = _bias_lane(hh)
            l = acc[:, b0:b0 + 1]
            outs.append(acc / l)
            lse = m_s[hh] + jnp.log(l)
            lse_ref[hh, 0] = jnp.broadcast_to(jnp.min(lse, axis=0, keepdims=True), (1, 128))
            qf = qa_ref[hh].astype(F32)
            cb = qf[:, b0:b0 + 1] + qf[:, b0 + 1:b0 + 2] + qf[:, b0 + 2:b0 + 3] - lse
            qb_ref[hh] = _lane_fill(lane, qf, _split3(cb), b0).astype(BF16)
        o_ref[...] = _pair_lanes(lane, outs[0], outs[1])

    qs = pl.BlockSpec((2, tb, 128), lambda p, i: (p, i, 0))
    return pl.pallas_call(
        body, name=name, grid=(FOX_PAIRS, nb),
        in_specs=[qs, ANY, ANY, SMEM],
        out_specs=[pl.BlockSpec((tb, 128), lambda p, i: (i, p)), qs,
                   pl.BlockSpec((2, 1, 1, 128), lambda p, i: (p, i, 0, 0))],
        out_shape=[jax.ShapeDtypeStruct((S, FOX_HEADS * FOX_DH), F32), jax.ShapeDtypeStruct((FOX_HEADS, S, 128), BF16),
                   jax.ShapeDtypeStruct((FOX_HEADS, nb, 1, 128), F32)],
        scratch_shapes=[pltpu.VMEM((2, 2, tb, 128), BF16), pltpu.VMEM((2, 2, tb, 128), BF16),
                        pltpu.SemaphoreType.DMA((2, 2)), pltpu.VMEM((2, tb, 1), F32), pltpu.VMEM((2, tb, 128), F32),
                        pltpu.SMEM((2,), F32)],
        compiler_params=_cparams(("arbitrary", "arbitrary")),
    )(qa, ka, va, bounds)


def _fox_bwd_prep(o, do, *, name, T=512):
    S = o.shape[0]
    T = min(T, S)

    def body(o_ref, do_ref, dob_ref):
        lane = lax.broadcasted_iota(jnp.int32, (T, 128), 1)
        d = do_ref[...].astype(F32)
        prod = d * o_ref[...]
        for hh in range(2):
            mine = _data_lanes(lane, hh)
            delta = jnp.sum(jnp.where(mine, prod, 0.0), axis=-1, keepdims=True)
            dob_ref[hh] = _lane_fill(lane, jnp.where(mine, d, 0.0), _split3(-delta), _bias_lane(hh)).astype(BF16)

    tok = pl.BlockSpec((T, 128), lambda p, t: (t, p))
    return pl.pallas_call(
        body, name=name, grid=(FOX_PAIRS, S // T),
        in_specs=[tok, tok], out_specs=pl.BlockSpec((2, T, 128), lambda p, t: (p, t, 0)),
        out_shape=jax.ShapeDtypeStruct((FOX_HEADS, S, 128), BF16),
        compiler_params=_cparams(("parallel", "parallel")),
    )(o, do)


def _fox_bwd_dq(qb, ka, va, dob, bounds, lse_min, *, name, comm=None):
    S = qb.shape[1]
    tb = min(FOX_TB, S)
    nb = S // tb
    nc = comm.n if comm is not None else 0

    def body(qb_ref, dob_ref, ka_hbm, va_hbm, st_ref, lm_ref, *rest):
        c_in, (dq_ref, dob2_ref), c_out = rest[:nc], rest[nc:nc + 2], rest[nc + 2:2 * nc + 2]
        kbuf, vbuf, sems, acc_s = rest[2 * nc + 2:2 * nc + 6]
        c_sems = rest[2 * nc + 6:]
        pair, qi = pl.program_id(0), pl.program_id(1)
        if comm is not None:
            @pl.when((pair == 0) & (qi == 0))
            def _():
                comm.start(c_in, c_out, c_sems)

        stream = _BlockStream((ka_hbm, va_hbm), (kbuf, vbuf), sems, pair, tb)
        k0 = [_first_live_key(st_ref, 2 * pair + hh, nb, qi, qi - 1, lm_ref[(2 * pair + hh) * nb + qi] - FOX_SKIP)
              for hh in range(2)]
        n = qi - jnp.minimum(k0[0], k0[1]) + 1

        @pl.when(qi == 0)
        def _():
            stream.start(qi, 0)

        acc_s[...] = jnp.zeros_like(acc_s)

        def head_step(hh, slot, k, masked):
            s = _dot(qb_ref[hh], kbuf[slot, hh], _DIMS["nt"])
            if masked:
                row = lax.broadcasted_iota(jnp.int32, (tb, tb), 0)
                col = lax.broadcasted_iota(jnp.int32, (tb, tb), 1)
                s = jnp.where(col <= row, s, NEG)
            ds = jnp.exp(s) * _dot(dob_ref[hh], vbuf[slot, hh], _DIMS["nt"])
            acc_s[hh] += _dot(ds.astype(BF16), kbuf[slot, hh], _DIMS["nn"])

        def loop(t, carry):
            k = qi - t
            slot = t % 2
            stream.wait(k, slot)

            @pl.when(t + 1 < n)
            def _():
                stream.start(k - 1, 1 - slot)

            @pl.when(t == 0)
            def _():
                for hh in range(2):
                    head_step(hh, slot, k, True)

            for hh in range(2):
                @pl.when((t > 0) & (k >= k0[hh]))
                def _():
                    head_step(hh, slot, k, False)
            return carry

        lax.fori_loop(0, n, loop, 0)

        @pl.when(qi + 1 < nb)
        def _():
            stream.start(qi + 1, 0)

        lane = lax.broadcasted_iota(jnp.int32, (tb, 128), 1)
        dq_ref[...] = (_pair_lanes(lane, acc_s[0], acc_s[1]) * FOX_SCALE).astype(dq_ref.dtype)
        for hh in range(2):
            b0 = _bias_lane(hh)
            r = acc_s[hh][:, b0:b0 + 1]
            dob2_ref[hh] = _lane_fill(lane, dob_ref[hh].astype(F32), _split3(r), b0 + RSUM_LANE).astype(BF16)
        if comm is not None:
            @pl.when((pair == FOX_PAIRS - 1) & (qi == nb - 1))
            def _():
                comm.finish(c_in, c_out, c_sems)

    qs = pl.BlockSpec((2, tb, 128), lambda p, i: (p, i, 0))
    outs = pl.pallas_call(
        body, name=name, grid=(FOX_PAIRS, nb),
        in_specs=[qs, qs, ANY, ANY, SMEM, SMEM] + [ANY] * nc,
        out_specs=[pl.BlockSpec((tb, 128), lambda p, i: (i, p)), qs] + [ANY] * nc,
        out_shape=[jax.ShapeDtypeStruct((S, FOX_HEADS * FOX_DH), BF16),
                   jax.ShapeDtypeStruct((FOX_HEADS, S, 128), BF16)] + (comm.out_shapes if comm is not None else []),
        scratch_shapes=[pltpu.VMEM((2, 2, tb, 128), BF16), pltpu.VMEM((2, 2, tb, 128), BF16),
                        pltpu.SemaphoreType.DMA((2, 2)), pltpu.VMEM((2, tb, 128), F32)]
        + (comm.scratch if comm is not None else []),
        compiler_params=_cparams(("arbitrary", "arbitrary")),
    )(qb, dob, ka, va, bounds, lse_min, *(comm.inputs if comm is not None else []))
    return (outs[0], outs[1]) if comm is None else (outs[0], outs[1], outs[2:])


def _fox_bwd_dkv(qb, ka, va, dob, bounds, lse_min, *, name):
    S = qb.shape[1]
    tb = min(FOX_TB, S)
    nb = S // tb

    def body(ka_ref, va_ref, qb_hbm, dob_hbm, st_ref, lm_ref, dk_ref, dv_ref, dcs_ref, qbuf, dbuf, sems, dk_s, dv_s):
        pair, ki = pl.program_id(0), pl.program_id(1)
        stream = _BlockStream((qb_hbm, dob_hbm), (qbuf, dbuf), sems, pair, tb)
        i1 = [_last_live_query(st_ref, lm_ref, 2 * pair + hh, nb, ki) for hh in range(2)]
        n = jnp.maximum(i1[0], i1[1]) - ki + 1

        @pl.when(ki == 0)
        def _():
            stream.start(ki, 0)

        dk_s[...] = jnp.zeros_like(dk_s)
        dv_s[...] = jnp.zeros_like(dv_s)

        def head_step(hh, slot, masked):
            st = _dot(ka_ref[hh], qbuf[slot, hh], _DIMS["nt"])
            if masked:
                row = lax.broadcasted_iota(jnp.int32, (tb, tb), 0)
                col = lax.broadcasted_iota(jnp.int32, (tb, tb), 1)
                st = jnp.where(row <= col, st, NEG)
            pt = jnp.exp(st)
            dst = pt * _dot(va_ref[hh], dbuf[slot, hh], _DIMS["nt"])
            dv_s[hh] += _dot(pt.astype(BF16), dbuf[slot, hh], _DIMS["nn"])
            dk_s[hh] += _dot(dst.astype(BF16), qbuf[slot, hh], _DIMS["nn"])

        def loop(t, carry):
            i = ki + t
            slot = t % 2
            stream.wait(i, slot)

            @pl.when(t + 1 < n)
            def _():
                stream.start(i + 1, 1 - slot)

            @pl.when(t == 0)
            def _():
                for hh in range(2):
                    head_step(hh, slot, True)

            for hh in range(2):
                @pl.when((t > 0) & (i <= i1[hh]))
                def _():
                    head_step(hh, slot, False)
            return carry

        lax.fori_loop(0, n, loop, 0)

        @pl.when(ki + 1 < nb)
        def _():
            stream.start(ki + 1, 0)

        lane = lax.broadcasted_iota(jnp.int32, (tb, 128), 1)
        dk_ref[...] = _pair_lanes(lane, dk_s[0], dk_s[1]).astype(dk_ref.dtype)
        dv_ref[...] = _pair_lanes(lane, dv_s[0], dv_s[1]).astype(dv_ref.dtype)
        for hh in range(2):
            b0 = _bias_lane(hh)
            dk_a, dv_a = dk_s[hh], dv_s[hh]
            off = dv_a[:, b0 + RSUM_LANE:b0 + RSUM_LANE + 1] + dv_a[:, b0 + RSUM_LANE + 1:b0 + RSUM_LANE + 2] \
                + dv_a[:, b0 + RSUM_LANE + 2:b0 + RSUM_LANE + 3]
            dcs_ref[0, :, hh:hh + 1] = dk_a[:, b0 + 3:b0 + 4] - off

    ks = pl.BlockSpec((2, tb, 128), lambda p, j: (p, j, 0))
    tok = pl.BlockSpec((tb, 128), lambda p, j: (j, p))
    big = jax.ShapeDtypeStruct((S, FOX_HEADS * FOX_DH), BF16)
    return pl.pallas_call(
        body, name=name, grid=(FOX_PAIRS, nb),
        in_specs=[ks, ks, ANY, ANY, SMEM, SMEM],
        out_specs=[tok, tok, pl.BlockSpec((1, tb, 2), lambda p, j: (p, j, 0))],
        out_shape=[big, big, jax.ShapeDtypeStruct((FOX_PAIRS, S, 2), F32)],
        scratch_shapes=[pltpu.VMEM((2, 2, tb, 128), BF16), pltpu.VMEM((2, 2, tb, 128), BF16),
                        pltpu.SemaphoreType.DMA((2, 2)), pltpu.VMEM((2, tb, 128), F32), pltpu.VMEM((2, tb, 128), F32)],
        compiler_params=_cparams(("arbitrary", "arbitrary")),
    )(ka, va, qb, dob, bounds, lse_min)


def _merge_fwd(proj, pa, pb, *, name, T=512):
    S, D = pa.shape
    T = min(T, S)

    def body(ga_ref, gb_ref, pa_ref, pb_ref, m_ref):
        m_ref[...] = (_sigmoid(ga_ref[...]) * pa_ref[...] + _sigmoid(gb_ref[...]) * pb_ref[...]).astype(m_ref.dtype)

    tok = pl.BlockSpec((T, D), lambda i: (i, 0))
    return pl.pallas_call(
        body, name=name, grid=(S // T,),
        in_specs=[pl.BlockSpec((T, D), lambda i: (i, 7)), pl.BlockSpec((T, D), lambda i: (i, 8)), tok, tok],
        out_specs=tok, out_shape=jax.ShapeDtypeStruct((S, D), BF16),
        compiler_params=_cparams(("parallel",)),
    )(proj, proj, pa, pb)


def _merge_bwd(proj, pa, pb, dm, *, name, T=512):
    S, D = pa.shape
    T = min(T, S)

    def body(ga_ref, gb_ref, pa_ref, pb_ref, dm_ref, dpa_ref, dpb_ref, dga_ref, dgb_ref):
        dm_ = dm_ref[...]
        sa, sb = _sigmoid(ga_ref[...]), _sigmoid(gb_ref[...])
        dpa_ref[...] = (dm_ * sa).astype(BF16)
        dpb_ref[...] = (dm_ * sb).astype(BF16)
        dga_ref[...] = (dm_ * pa_ref[...] * sa * (1.0 - sa)).astype(BF16)
        dgb_ref[...] = (dm_ * pb_ref[...] * sb * (1.0 - sb)).astype(BF16)

    tok = pl.BlockSpec((T, D), lambda i: (i, 0))
    big = jax.ShapeDtypeStruct((S, D), BF16)
    return pl.pallas_call(
        body, name=name, grid=(S // T,),
        in_specs=[pl.BlockSpec((T, D), lambda i: (i, 7)), pl.BlockSpec((T, D), lambda i: (i, 8)), tok, tok, tok],
        out_specs=[tok, tok, tok, tok], out_shape=[big, big, big, big],
        compiler_params=_cparams(("parallel",)),
    )(proj, proj, pa, pb, dm)


INV_SQRT2 = 0.7071067811865476
INV_SQRT2PI = 0.3989422804014327


def _shifted(u, prev, rid):
    m1 = jnp.where(rid == 0, prev[7:8, :], pltpu.roll(u, 1, 0))
    m2 = jnp.where(rid == 0, prev[6:7, :], jnp.where(rid == 1, prev[7:8, :], pltpu.roll(u, 2, 0)))
    return m1, m2


def _conv_acc(u, prev, w_ref, b_ref, rid):
    m1, m2 = _shifted(u, prev, rid)
    return b_ref[...] + w_ref[0:1, :] * m2 + w_ref[1:2, :] * m1 + w_ref[2:3, :] * u, m1, m2


def _convglu_fwd(ug, uv, wg, wv, bg, bv, *, name, T=512, tc=256):
    S, F = ug.shape
    T = min(T, S)

    def body(ug_ref, uv_ref, wg_ref, wv_ref, bg_ref, bv_ref, a_ref, pg, pv):
        @pl.when(pl.program_id(1) == 0)
        def _():
            pg[...] = jnp.zeros_like(pg)
            pv[...] = jnp.zeros_like(pv)

        rid = lax.broadcasted_iota(jnp.int32, (T, tc), 0)
        g_, v_ = ug_ref[...], uv_ref[...]
        accg, _, _ = _conv_acc(g_, pg[...], wg_ref, bg_ref, rid)
        accv, _, _ = _conv_acc(v_, pv[...], wv_ref, bv_ref, rid)
        gel = 0.5 * accg * (1.0 + lax.erf(accg * INV_SQRT2))
        a_ref[...] = (gel * accv).astype(a_ref.dtype)
        pg[...] = g_[T - 8:T, :]
        pv[...] = v_[T - 8:T, :]

    tok = pl.BlockSpec((T, tc), lambda j, t: (t, j))
    w3 = pl.BlockSpec((3, tc), lambda j, t: (0, j))
    b1 = pl.BlockSpec((1, tc), lambda j, t: (0, j))
    return pl.pallas_call(
        body, name=name, grid=(F // tc, S // T),
        in_specs=[tok, tok, w3, w3, b1, b1], out_specs=tok,
        out_shape=jax.ShapeDtypeStruct((S, F), BF16),
        scratch_shapes=[pltpu.VMEM((8, tc), F32), pltpu.VMEM((8, tc), F32)],
        compiler_params=_cparams(("parallel", "arbitrary")),
    )(ug, uv, wg, wv, bg, bv)


def _convglu_bwd(ug, uv, wg, wv, bg, bv, da, *, name, T=512, tc=256):
    S, F = ug.shape
    T = min(T, S)
    nT = S // T
    halo_blocks = T // 8

    def up_shift(d, nx, rid):
        p1 = jnp.where(rid == T - 1, nx[0:1, :], pltpu.roll(d, T - 1, 0))
        p2 = jnp.where(rid == T - 1, nx[1:2, :], jnp.where(rid == T - 2, nx[0:1, :], pltpu.roll(d, T - 2, 0)))
        return p1, p2

    def body(ug_ref, uv_ref, hg_ref, hv_ref, wg_ref, wv_ref, bg_ref, bv_ref, da_ref,
             dug_ref, duv_ref, dwg_ref, dwv_ref, dbg_ref, dbv_ref, ng, nv):
        @pl.when(pl.program_id(1) == 0)
        def _():
            ng[...] = jnp.zeros_like(ng)
            nv[...] = jnp.zeros_like(nv)
            for r in (dwg_ref, dwv_ref, dbg_ref, dbv_ref):
                r[...] = jnp.zeros_like(r)

        first_block = pl.program_id(1) == nT - 1
        rid = lax.broadcasted_iota(jnp.int32, (T, tc), 0)
        g_, v_ = ug_ref[...], uv_ref[...]
        pg = jnp.where(first_block, 0.0, hg_ref[...])
        pv = jnp.where(first_block, 0.0, hv_ref[...])
        accg, g1, g2 = _conv_acc(g_, pg, wg_ref, bg_ref, rid)
        accv, v1, v2 = _conv_acc(v_, pv, wv_ref, bv_ref, rid)
        cdf = 0.5 * (1.0 + lax.erf(accg * INV_SQRT2))
        pdf = INV_SQRT2PI * jnp.exp(-0.5 * accg * accg)
        da_ = da_ref[...].astype(F32)
        dgate = da_ * accv * (cdf + accg * pdf)
        dval = da_ * (accg * cdf)
        dbg_ref[...] += jnp.sum(dgate, axis=0, keepdims=True)
        dbv_ref[...] += jnp.sum(dval, axis=0, keepdims=True)
        for j, (sg_, sv_) in enumerate(((g2, v2), (g1, v1), (g_, v_))):
            dwg_ref[j:j + 1, :] += jnp.sum(dgate * sg_, axis=0, keepdims=True)
            dwv_ref[j:j + 1, :] += jnp.sum(dval * sv_, axis=0, keepdims=True)
        for d, w_ref, nx, out_ref in ((dgate, wg_ref, ng, dug_ref), (dval, wv_ref, nv, duv_ref)):
            p1, p2 = up_shift(d, nx[...], rid)
            out_ref[...] = (w_ref[2:3, :] * d + w_ref[1:2, :] * p1 + w_ref[0:1, :] * p2).astype(out_ref.dtype)
            nx[...] = d[0:8, :]

    tok = pl.BlockSpec((T, tc), lambda j, t: (nT - 1 - t, j))
    halo = pl.BlockSpec((8, tc), lambda j, t: (jnp.maximum((nT - 1 - t) * halo_blocks - 1, 0), j))
    w3 = pl.BlockSpec((3, tc), lambda j, t: (0, j))
    b1 = pl.BlockSpec((1, tc), lambda j, t: (0, j))
    big = jax.ShapeDtypeStruct((S, F), BF16)
    return pl.pallas_call(
        body, name=name, grid=(F // tc, nT),
        in_specs=[tok, tok, halo, halo, w3, w3, b1, b1, tok], out_specs=[tok, tok, w3, w3, b1, b1],
        out_shape=[big, big, jax.ShapeDtypeStruct((3, F), F32), jax.ShapeDtypeStruct((3, F), F32),
                   jax.ShapeDtypeStruct((1, F), F32), jax.ShapeDtypeStruct((1, F), F32)],
        scratch_shapes=[pltpu.VMEM((8, tc), F32), pltpu.VMEM((8, tc), F32)],
        compiler_params=_cparams(("parallel", "arbitrary")),
    )(ug, uv, ug, uv, wg, wv, bg, bv, da)


FF_LO = 7168
IN_SHARD = 1154
FF_DEV, FF_OFF = FF_LO // IN_SHARD, FF_LO % IN_SHARD


def _col_blocks(a, width):
    return jnp.stack([a[:, d * width:(d + 1) * width] for d in range(N_DEV)])


def _w_in_blocks(d_wm, d_wff):
    def block(d):
        lo = d * IN_SHARD
        if d < FF_DEV:
            return d_wm[:, lo:lo + IN_SHARD]
        if d > FF_DEV:
            return d_wm[:, lo - FOX_HEADS:lo - FOX_HEADS + IN_SHARD]
        return jnp.concatenate([d_wm[:, lo:FF_LO], d_wff[:, :FOX_HEADS], d_wm[:, FF_LO:lo + IN_SHARD - FOX_HEADS]], axis=1)

    return jnp.stack([block(d) for d in range(N_DEV)])


def _late_weights(g_a, g_b, g_o, g_up, g_cw, g_d):
    wup = jnp.concatenate([g_up[d] for d in range(N_DEV)], axis=1)
    cw = jnp.concatenate([g_cw[d] for d in range(N_DEV)], axis=1)
    return dict(wa=g_a.reshape(D_MODEL, D_MODEL), wb=g_b.reshape(D_MODEL, D_MODEL), wo=g_o.reshape(D_MODEL, D_MODEL),
                wug=wup[:, :D_FF], wuv=wup[:, D_FF:], cwg=cw[:, :D_FF], cwv=cw[:, D_FF:], wd=g_d.reshape(D_FF, D_MODEL))


def _early_grad_blocks(d_wa, d_wb, d_wo, d_wug, d_wuv, d_wd):
    up = jnp.stack([d_wug[:, d * 704:(d + 1) * 704] for d in range(4)]
                   + [d_wuv[:, d * 704:(d + 1) * 704] for d in range(4)])
    return [d_wa.reshape(N_DEV, 128, D_MODEL), d_wb.reshape(N_DEV, 128, D_MODEL), d_wo.reshape(N_DEV, 128, D_MODEL),
            up, d_wd.reshape(N_DEV, 352, D_MODEL)]


def _local_step(x, tgt, w, p, late=None, exchange=False):
    S = x.shape[0]
    mm = _matmul
    n1 = _rms_fwd(x, p["norm_mix"], name="rms1_fwd")
    if late is None:
        proj = mm(n1, w["wm"], "nn", name="proj_main")
    else:
        proj, gathered = mm(n1, w["wm"], "nn", comm=late, name="proj_main")
        w = {**w, **_late_weights(*gathered)}
    ff = mm(n1, w["wff"], "nn", name="proj_ff")
    lb = _lb_fwd(p["hg_lb_logits"], name="lb_fwd")
    gnorm = p["hg_norm"].reshape(1, HG_DV)
    o_hg, oa, states = _hgrn_fwd(proj, lb, gnorm, name="hgrn_fwd")
    bias = jnp.pad(p["fox_f_bias"].reshape(1, FOX_HEADS), ((0, 0), (0, 128 - FOX_HEADS)))
    c = _fox_gate_fwd(ff, bias, name="fox_gate_fwd")
    qa, ka, va, fox_stats = _fox_prep(proj, c, name="fox_prep")
    bounds = fox_stats[:, :, 0, :N_STAT].reshape(-1)
    ob, qb, lse_stats = _fox_fwd(qa, ka, va, bounds, name="fox_fwd")
    lse_min = lse_stats[:, :, 0, 0].reshape(-1)
    pa = mm(oa, w["wa"], "nn", name="branch_a")
    pb = mm(ob, w["wb"], "nn", name="branch_b")
    merged = _merge_fwd(proj, pa, pb, name="merge_fwd")
    h1 = mm(merged, w["wo"], "nn", addend=x, name="mix_out")
    n2 = _rms_fwd(h1, p["norm_ffn"], name="rms2_fwd")
    ug = mm(n2, w["wug"], "nn", name="up_gate")
    uv = mm(n2, w["wuv"], "nn", name="up_val")
    a = _convglu_fwd(ug, uv, w["cwg"], w["cwv"], p["cbg"], p["cbv"], name="convglu_fwd")
    h2 = mm(a, w["wd"], "nn", addend=h1, name="ffn_down")
    loss, dh2, d_norm_final = _loss_head(h2, p["norm_final"], tgt, name="loss_head")
    da = mm(dh2, w["wd"], "nt", out_dtype=BF16, name="d_act")
    d_wd = mm(a, dh2, "tn", out_dtype=BF16, name="dw_down")
    dug, duv, d_cwg, d_cwv, d_cbg, d_cbv = _convglu_bwd(
        ug, uv, w["cwg"], w["cwv"], p["cbg"], p["cbv"], da, name="convglu_bwd")
    dn2 = mm(dug, w["wug"], "nt", name="dn2_gate")
    dn2 = mm(duv, w["wuv"], "nt", addend=dn2, name="dn2_val")
    d_wug = mm(n2, dug, "tn", out_dtype=BF16, name="dw_up_gate")
    d_wuv = mm(n2, duv, "tn", out_dtype=BF16, name="dw_up_val")
    dh1, d_norm_ffn = _rms_bwd(h1, p["norm_ffn"], dn2, dh2, name="rms2_bwd")
    dmerged = mm(dh1, w["wo"], "nt", name="d_merged")
    d_wo = mm(merged, dh1, "tn", out_dtype=BF16, name="dw_out")
    dpa, dpb, dga, dgb = _merge_bwd(proj, pa, pb, dmerged, name="merge_bwd")
    doa = mm(dpa, w["wa"], "nt", name="d_oa")
    dob = mm(dpb, w["wb"], "nt", out_dtype=BF16, name="d_ob")
    d_wa = mm(oa, dpa, "tn", out_dtype=BF16, name="dw_branch_a")
    d_wb = mm(ob, dpb, "tn", out_dtype=BF16, name="dw_branch_b")
    dhq, dhf, dhi, dhg, dlb, dgn8 = _hgrn_bwd(proj, lb, gnorm, o_hg, states, doa, name="hgrn_bwd")
    d_logits = _lb_bwd(p["hg_lb_logits"], dlb, name="lb_bwd")
    dob_hm = _fox_bwd_prep(ob, dob, name="fox_bwd_prep")
    early_parts = None
    if exchange:
        comm = _ExchangeComm(_early_grad_blocks(d_wa, d_wb, d_wo, d_wug, d_wuv, d_wd))
        dq, dob2, early_parts = _fox_bwd_dq(qb, ka, va, dob_hm, bounds, lse_min, comm=comm, name="fox_bwd_dq")
    else:
        dq, dob2 = _fox_bwd_dq(qb, ka, va, dob_hm, bounds, lse_min, name="fox_bwd_dq")
    dk, dv, dcs = _fox_bwd_dkv(qb, ka, va, dob2, bounds, lse_min, name="fox_bwd_dkv")
    dcs_tok = jnp.pad(dcs.transpose(1, 0, 2).reshape(S, FOX_HEADS), ((0, 0), (0, 128 - FOX_HEADS)))
    dff, dbias = _fox_gate_bwd(ff, bias, dcs_tok, name="fox_gate_bwd")
    dproj = jnp.concatenate([dhq, dhf, dhi, dhg, dq, dk, dv, dga, dgb], axis=1)
    d_wm = mm(n1, dproj, "tn", out_dtype=BF16, name="dw_in_main")
    d_wff = mm(n1, dff, "tn", out_dtype=BF16, name="dw_in_ff")
    dn1 = mm(dff, w["wff"], "nt", name="dn1_ff")
    late_parts = None
    if exchange:
        d_cw = jnp.concatenate([d_cwg, d_cwv], axis=1)
        comm = _ExchangeComm([_w_in_blocks(d_wm, d_wff), _col_blocks(d_cw, 704)])
        dn1, late_parts = mm(dproj, w["wm"], "nt", addend=dn1, comm=comm, name="dn1_main")
    else:
        dn1 = mm(dproj, w["wm"], "nt", addend=dn1, name="dn1_main")
    dx, d_norm_mix = _rms_bwd(x, p["norm_mix"], dn1, dh1, name="rms1_bwd")
    grads = dict(
        wm=d_wm, wff=d_wff, wa=d_wa, wb=d_wb, wo=d_wo, wug=d_wug, wuv=d_wuv, cwg=d_cwg, cwv=d_cwv, wd=d_wd,
        norm_mix=d_norm_mix.reshape(-1), fox_f_bias=dbias[0, :FOX_HEADS], hg_lb_logits=d_logits,
        hg_norm=jnp.sum(dgn8, axis=0).reshape(-1), norm_ffn=d_norm_ffn.reshape(-1), cbg=d_cbg, cbv=d_cbv,
        norm_final=d_norm_final.reshape(-1), early_parts=early_parts, late_parts=late_parts)
    return loss, dx, grads


SMALL = [("norm_mix", (1, D_MODEL)), ("fox_f_bias", (1, FOX_HEADS)), ("hg_lb_logits", (2, HG_HEADS * HG_DK)),
         ("hg_norm", (1, HG_DV)), ("norm_ffn", (1, D_MODEL)), ("conv_b", (1, 2 * D_FF)), ("norm_final", (D_MODEL,))]
SMALL_ROWS = 88
SHARDED = [("w_in", (D_MODEL, 1154), 256), ("w_branch_a", (128, D_MODEL), 128), ("w_branch_b", (128, D_MODEL), 128),
           ("w_out", (128, D_MODEL), 128), ("w_up", (D_MODEL, 704), 256), ("conv_w", (3, 704), 3),
           ("w_down", (352, D_MODEL), 352)]
NAMES = ["norm_mix", "w_in", "fox_f_bias", "hg_lb_logits", "hg_norm", "w_branch_a", "w_branch_b", "w_out",
         "norm_ffn", "w_up", "conv_w", "conv_b", "w_down", "norm_final"]


def _size(shape):
    n = 1
    for s in shape:
        n *= s
    return n


def _adamw(parts, w, m, v, *, name, T):
    R, C = w.shape
    c1 = 1.0 / (1.0 - ADAM_B1 ** ADAM_STEP)
    c2 = 1.0 / (1.0 - ADAM_B2 ** ADAM_STEP)

    def body(p_ref, w_ref, m_ref, v_ref, g_ref, d_ref, nm_ref, nv_ref):
        g = p_ref[0].astype(F32)
        for s in range(1, N_DEV):
            g = g + p_ref[s].astype(F32)
        g_ref[...] = g
        nm = ADAM_B1 * m_ref[...] + (1.0 - ADAM_B1) * g
        nv = ADAM_B2 * v_ref[...] + (1.0 - ADAM_B2) * (g * g)
        nm_ref[...] = nm
        nv_ref[...] = nv
        d_ref[...] = -ADAM_LR * ((nm * c1) / (jnp.sqrt(nv * c2) + ADAM_EPS) + ADAM_WD * w_ref[...])

    blk = pl.BlockSpec((T, C), lambda i: (i, 0))
    out = jax.ShapeDtypeStruct((R, C), F32)
    return pl.pallas_call(
        body, name=name, grid=(R // T,),
        in_specs=[pl.BlockSpec((N_DEV, T, C), lambda i: (0, i, 0)), blk, blk, blk],
        out_specs=[blk, blk, blk, blk], out_shape=[out, out, out, out],
        compiler_params=_cparams(("parallel",)),
    )(parts, w, m, v)


def _pack_small(vals):
    flat = jnp.concatenate([vals[n].reshape(-1).astype(F32) for n, _ in SMALL])
    return jnp.pad(flat, (0, SMALL_ROWS * 128 - flat.shape[0])).reshape(SMALL_ROWS, 128)


def _unpack_small(buf):
    flat, out, off = buf.reshape(-1), {}, 0
    for n, shape in SMALL:
        out[n] = flat[off:off + _size(shape)].reshape(shape)
        off += _size(shape)
    return out


def kernel(x, norm_mix, w_in, fox_f_bias, hg_lb_logits, hg_norm, w_branch_a, w_branch_b, w_out, norm_ffn, w_up, conv_w, conv_b, w_down, norm_final, loss_target, m_norm_mix, m_w_in, m_fox_f_bias, m_hg_lb_logits, m_hg_norm, m_w_branch_a, m_w_branch_b, m_w_out, m_norm_ffn, m_w_up, m_conv_w, m_conv_b, m_w_down, m_norm_final, v_norm_mix, v_w_in, v_fox_f_bias, v_hg_lb_logits, v_hg_norm, v_w_branch_a, v_w_branch_b, v_w_out, v_norm_ffn, v_w_up, v_conv_w, v_conv_b, v_w_down, v_norm_final):
    wv = dict(norm_mix=norm_mix, w_in=w_in, fox_f_bias=fox_f_bias, hg_lb_logits=hg_lb_logits, hg_norm=hg_norm,
              w_branch_a=w_branch_a, w_branch_b=w_branch_b, w_out=w_out, norm_ffn=norm_ffn, w_up=w_up, conv_w=conv_w,
              conv_b=conv_b, w_down=w_down, norm_final=norm_final)
    mv = dict(norm_mix=m_norm_mix, w_in=m_w_in, fox_f_bias=m_fox_f_bias, hg_lb_logits=m_hg_lb_logits, hg_norm=m_hg_norm,
              w_branch_a=m_w_branch_a, w_branch_b=m_w_branch_b, w_out=m_w_out, norm_ffn=m_norm_ffn, w_up=m_w_up,
              conv_w=m_conv_w, conv_b=m_conv_b, w_down=m_w_down, norm_final=m_norm_final)
    vv = dict(norm_mix=v_norm_mix, w_in=v_w_in, fox_f_bias=v_fox_f_bias, hg_lb_logits=v_hg_lb_logits, hg_norm=v_hg_norm,
              w_branch_a=v_w_branch_a, w_branch_b=v_w_branch_b, w_out=v_w_out, norm_ffn=v_norm_ffn, w_up=v_w_up,
              conv_w=v_conv_w, conv_b=v_conv_b, w_down=v_w_down, norm_final=v_norm_final)

    (g_in,) = _comm_call(_GatherComm([w_in[0].astype(BF16)]), name="gather_w_in")
    w = dict(wm=jnp.concatenate([g_in[d] for d in range(FF_DEV)]
                                + [g_in[FF_DEV][:, :FF_OFF], g_in[FF_DEV][:, FF_OFF + FOX_HEADS:]]
                                + [g_in[d] for d in range(FF_DEV + 1, N_DEV)], axis=1),
             wff=jnp.pad(g_in[FF_DEV][:, FF_OFF:FF_OFF + FOX_HEADS], ((0, 0), (0, 128 - FOX_HEADS))))
    late = _GatherComm([w_branch_a[0].astype(BF16), w_branch_b[0].astype(BF16), w_out[0].astype(BF16),
                        w_up[0].astype(BF16), conv_w[0], w_down[0].astype(BF16)])
    p = dict(norm_mix=norm_mix[0], fox_f_bias=fox_f_bias[0], hg_lb_logits=hg_lb_logits, hg_norm=hg_norm[0],
             norm_ffn=norm_ffn[0], cbg=conv_b[:, :D_FF], cbv=conv_b[:, D_FF:], norm_final=norm_final)
    loss, dx, grads = _local_step(x[0], loss_target[0], w, p, late=late, exchange=True)
    loss = lax.psum(loss[0, 0], ("x", "y", "c"))

    small = _pack_small(dict(
        norm_mix=grads["norm_mix"], fox_f_bias=grads["fox_f_bias"], hg_lb_logits=grads["hg_lb_logits"],
        hg_norm=grads["hg_norm"], norm_ffn=grads["norm_ffn"], conv_b=jnp.concatenate([grads["cbg"], grads["cbv"]], axis=1),
        norm_final=grads["norm_final"]))
    (small_parts,) = _comm_call(_ExchangeComm([jnp.broadcast_to(small[None], (N_DEV, SMALL_ROWS, 128))]),
                                name="exchange_small")
    ea, eb, eo, eup, ed = grads["early_parts"]
    p_in, p_cw = grads["late_parts"]
    parts = [p_in, ea, eb, eo, eup, p_cw, ed, small_parts]
    res = {}
    for (n, shape, tile), part in zip(SHARDED, parts):
        outs = _adamw(part, wv[n].reshape(shape), mv[n].reshape(shape), vv[n].reshape(shape), name="adamw_" + n, T=tile)
        res[n] = [o.reshape(wv[n].shape) for o in outs]
    outs = _adamw(parts[-1], _pack_small(wv), _pack_small(mv), _pack_small(vv), name="adamw_small", T=SMALL_ROWS)
    small = [_unpack_small(o) for o in outs]
    for n, _ in SMALL:
        res[n] = [s[n] for s in small]
    return (loss, dx[None], *[res[n][0] for n in NAMES], *[res[n][1] for n in NAMES],
            *[res[n][2] for n in NAMES], *[res[n][3] for n in NAMES])
```

```python
import jax
import jax.numpy as jnp
from jax import lax
from jax.experimental import pallas as pl
from jax.experimental.pallas import tpu as pltpu

F32 = jnp.float32
BF16 = jnp.bfloat16

D_MODEL = 1024
HG_HEADS = 8
HG_DK = 128
HG_DV = 128
HG_CHUNK = 64
FOX_HEADS = 16
FOX_DH = 64
D_FF = 2816
EPS = 1e-6
N_DEV = 8

ADAM_LR = 0.001
ADAM_B1 = 0.9
ADAM_B2 = 0.999
ADAM_EPS = 1e-08
ADAM_WD = 0.01
ADAM_STEP = 10

VMEM_LIMIT = 56 * 1024 * 1024


def _cparams(sem):
    return pltpu.CompilerParams(dimension_semantics=sem, vmem_limit_bytes=VMEM_LIMIT)


MESH = pl.DeviceIdType.MESH
ANY = pl.BlockSpec(memory_space=pl.ANY)
SMEM = pl.BlockSpec(memory_space=pltpu.SMEM)


class _GatherComm:
    def __init__(self, shards):
        self.inputs = list(shards)
        n = self.n = len(shards)
        self.out_shapes = [jax.ShapeDtypeStruct((N_DEV,) + s.shape, s.dtype) for s in shards]
        self.scratch = [pltpu.SemaphoreType.DMA((n, 7)), pltpu.SemaphoreType.DMA((n, 7)), pltpu.SemaphoreType.DMA((n,))]

    def _parts(self, x_refs, out_refs, sems):
        send_sems, recv_sems, local_sems = sems
        x, y, c = lax.axis_index("x"), lax.axis_index("y"), lax.axis_index("c")
        me, sibling = (x, y, c), (x, y, 1 - c)
        chips = [(1 - x, y), (x, 1 - y), (1 - x, 1 - y)]

        def copy(t, k, block, to, src=None):
            slot = out_refs[t].at[4 * block[0] + 2 * block[1] + block[2]]
            return pltpu.make_async_remote_copy(
                src_ref=slot if src is None else src, dst_ref=slot,
                send_sem=send_sems.at[t, k], recv_sem=recv_sems.at[t, k], device_id=to, device_id_type=MESH)

        mine = [pltpu.make_async_copy(x_refs[t], out_refs[t].at[4 * x + 2 * y + c], local_sems.at[t])
                for t in range(self.n)]
        first = []
        for t in range(self.n):
            first.append(copy(t, 0, me, sibling, src=x_refs[t]))
            first += [copy(t, 1 + j, me, (*chip, c), src=x_refs[t]) for j, chip in enumerate(chips)]
        return c, me, sibling, chips, copy, mine, first

    def start(self, x_refs, out_refs, sems):
        _, _, _, _, _, mine, first = self._parts(x_refs, out_refs, sems)
        for cp in mine + first:
            cp.start()

    def finish(self, x_refs, out_refs, sems):
        c, me, sibling, chips, copy, mine, first = self._parts(x_refs, out_refs, sems)
        passed = []
        for j, chip in enumerate(chips):
            for t in range(self.n):
                copy(t, 1 + j, (*chip, c), me).wait_recv()
                passed.append(copy(t, 4 + j, (*chip, c), sibling))
                passed[-1].start()
        for t in range(self.n):
            copy(t, 0, sibling, me).wait_recv()
            for j, chip in enumerate(chips):
                copy(t, 4 + j, (*chip, 1 - c), me).wait_recv()
        for cp in first + passed:
            cp.wait_send()
        for cp in mine:
            cp.wait()


class _ExchangeComm:
    def __init__(self, blocks):
        self.inputs = list(blocks)
        n = self.n = len(blocks)
        self.out_shapes = [jax.ShapeDtypeStruct(b.shape, b.dtype) for b in blocks]
        self.scratch = [pltpu.SemaphoreType.DMA((n, 7)), pltpu.SemaphoreType.DMA((n, 7)), pltpu.SemaphoreType.DMA((n,))]

    def _parts(self, g_refs, out_refs, sems):
        send_sems, recv_sems, local_sems = sems
        x, y, c = lax.axis_index("x"), lax.axis_index("y"), lax.axis_index("c")
        me = 4 * x + 2 * y + c
        mine = [pltpu.make_async_copy(g_refs[t].at[me], out_refs[t].at[me], local_sems.at[t]) for t in range(self.n)]
        sends, recvs = [], []
        for k in range(1, N_DEV):
            px = 1 - x if k & 4 else x
            py = 1 - y if k & 2 else y
            pc = 1 - c if k & 1 else c
            p = 4 * px + 2 * py + pc
            for t in range(self.n):
                sends.append(pltpu.make_async_remote_copy(
                    src_ref=g_refs[t].at[p], dst_ref=out_refs[t].at[me], send_sem=send_sems.at[t, k - 1],
                    recv_sem=recv_sems.at[t, k - 1], device_id=(px, py, pc), device_id_type=MESH))
                recvs.append(pltpu.make_async_remote_copy(
                    src_ref=g_refs[t].at[p], dst_ref=out_refs[t].at[p], send_sem=send_sems.at[t, k - 1],
                    recv_sem=recv_sems.at[t, k - 1], device_id=(px, py, pc), device_id_type=MESH))
        return mine, sends, recvs

    def start(self, g_refs, out_refs, sems):
        mine, sends, _ = self._parts(g_refs, out_refs, sems)
        for cp in mine + sends:
            cp.start()

    def finish(self, g_refs, out_refs, sems):
        mine, sends, recvs = self._parts(g_refs, out_refs, sems)
        for cp in recvs:
            cp.wait_recv()
        for cp in sends:
            cp.wait_send()
        for cp in mine:
            cp.wait()


def _comm_call(comm, *, name):
    n = comm.n

    def body(*refs):
        comm.start(refs[:n], refs[n:2 * n], refs[2 * n:])
        comm.finish(refs[:n], refs[n:2 * n], refs[2 * n:])

    return pl.pallas_call(body, name=name, in_specs=[ANY] * n, out_specs=[ANY] * n, out_shape=comm.out_shapes,
                          scratch_shapes=comm.scratch)(*comm.inputs)


_DIMS = {
    "nn": (((1,), (0,)), ((), ())),
    "nt": (((1,), (1,)), ((), ())),
    "tn": (((0,), (0,)), ((), ())),
}

MATMUL_VMEM_BUDGET = 36 * 1024 * 1024
MAX_TILE = 1536


def _pick(n, prefs):
    for p in prefs:
        if n % p == 0:
            return p
    return n


def _tile_options(n):
    return [d for d in range(128, min(n, MAX_TILE) + 1, 128) if n % d == 0] or [n]


def _pick_tiles(M, N, tk, nk, sa, sb, so, has_addend, tm, tn):
    best = None
    for cm in ([tm] if tm else _tile_options(M)):
        for cn in ([tn] if tn else _tile_options(N)):
            need = 2 * (cm * tk * sa + tk * cn * sb + cm * cn * so + (cm * cn * 4 if has_addend else 0))
            need += cm * cn * 4 if nk > 1 else 0
            if need <= MATMUL_VMEM_BUDGET and (best is None or cm * cn > best[0] * best[1]
                                               or (cm * cn == best[0] * best[1] and cn > best[1])):
                best = (cm, cn)
    assert best is not None, (M, N, tk)
    return best


def _matmul(a, b, form, *, out_dtype=F32, addend=None, tm=None, tn=None, tk=None, comm=None, name):
    if form == "nn":
        (M, K), (K2, N) = a.shape, b.shape
    elif form == "nt":
        (M, K), (N, K2) = a.shape, b.shape
    else:
        (K, M), (K2, N) = a.shape, b.shape
    assert K == K2, (a.shape, b.shape, form)
    tk = tk or (K if K <= 2816 else _pick(K, (1024, 512, 256, 128)))
    nk = K // tk
    if tm is None or tn is None:
        tm, tn = _pick_tiles(M, N, tk, nk, a.dtype.itemsize, b.dtype.itemsize, jnp.dtype(out_dtype).itemsize,
                             addend is not None, tm, tn)
    assert M % tm == 0 and N % tn == 0 and K % tk == 0, (M, N, K, tm, tn, tk)
    dims = _DIMS[form]
    nc = comm.n if comm is not None else 0
    grid = (M // tm, N // tn, nk)

    def body(*refs):
        a_ref, b_ref = refs[:2]
        pos = 2
        add_ref = refs[pos] if addend is not None else None
        pos += addend is not None
        c_in, o_ref, c_out = refs[pos:pos + nc], refs[pos + nc], refs[pos + nc + 1:pos + 2 * nc + 1]
        pos += 2 * nc + 1
        acc_ref = refs[pos] if nk > 1 else None
        c_sems = refs[pos + (nk > 1):]
        if comm is not None:
            ids = [pl.program_id(d) for d in range(3)]

            @pl.when((ids[0] == 0) & (ids[1] == 0) & (ids[2] == 0))
            def _():
                comm.start(c_in, c_out, c_sems)

        def finish(r):
            if add_ref is not None:
                r = r + add_ref[...].astype(F32)
            o_ref[...] = r.astype(o_ref.dtype)

        part = lax.dot_general(a_ref[...].astype(BF16), b_ref[...].astype(BF16), dims, preferred_element_type=F32)
        if nk == 1:
            finish(part)
        else:
            k = pl.program_id(2)

            @pl.when(k == 0)
            def _():
                acc_ref[...] = part

            @pl.when(k > 0)
            def _():
                acc_ref[...] += part

            @pl.when(k == nk - 1)
            def _():
                finish(acc_ref[...])

        if comm is not None:
            @pl.when((ids[0] == grid[0] - 1) & (ids[1] == grid[1] - 1) & (ids[2] == grid[2] - 1))
            def _():
                comm.finish(c_in, c_out, c_sems)

    if form == "nn":
        a_spec = pl.BlockSpec((tm, tk), lambda i, j, k: (i, k))
        b_spec = pl.BlockSpec((tk, tn), lambda i, j, k: (k, j))
    elif form == "nt":
        a_spec = pl.BlockSpec((tm, tk), lambda i, j, k: (i, k))
        b_spec = pl.BlockSpec((tn, tk), lambda i, j, k: (j, k))
    else:
        a_spec = pl.BlockSpec((tk, tm), lambda i, j, k: (k, i))
        b_spec = pl.BlockSpec((tk, tn), lambda i, j, k: (k, j))
    o_spec = pl.BlockSpec((tm, tn), lambda i, j, k: (i, j))
    in_specs = [a_spec, b_spec] + ([o_spec] if addend is not None else [])
    args = (a, b) + ((addend,) if addend is not None else ())
    out_shape = jax.ShapeDtypeStruct((M, N), out_dtype)
    scratch = [pltpu.VMEM((tm, tn), F32)] if nk > 1 else []
    if comm is None:
        return pl.pallas_call(
            body, name=name, grid=grid, in_specs=in_specs, out_specs=o_spec, out_shape=out_shape,
            scratch_shapes=scratch, compiler_params=_cparams(("parallel", "parallel", "arbitrary")),
        )(*args)
    outs = pl.pallas_call(
        body, name=name, grid=grid, in_specs=in_specs + [ANY] * nc, out_specs=[o_spec] + [ANY] * nc,
        out_shape=[out_shape] + comm.out_shapes, scratch_shapes=scratch + comm.scratch,
        compiler_params=_cparams(("arbitrary", "arbitrary", "arbitrary")),
    )(*args, *comm.inputs)
    return outs[0], outs[1:]


def _rms_fwd(x, g, *, name, tm=512):
    M, D = x.shape
    tm = min(tm, M)

    def body(x_ref, g_ref, n_ref):
        xf = x_ref[...]
        r = lax.rsqrt(jnp.mean(xf * xf, axis=-1, keepdims=True) + EPS)
        n_ref[...] = (xf * r * g_ref[...]).astype(n_ref.dtype)

    return pl.pallas_call(
        body, name=name, grid=(M // tm,),
        in_specs=[pl.BlockSpec((tm, D), lambda i: (i, 0)), pl.BlockSpec((1, D), lambda i: (0, 0))],
        out_specs=pl.BlockSpec((tm, D), lambda i: (i, 0)),
        out_shape=jax.ShapeDtypeStruct((M, D), BF16),
        compiler_params=_cparams(("parallel",)),
    )(x, g.reshape(1, D))


def _rms_bwd(x, g, dn, dres, *, name, tm=512):
    M, D = x.shape
    tm = min(tm, M)

    def body(x_ref, g_ref, dn_ref, dres_ref, dx_ref, dg_ref):
        @pl.when(pl.program_id(0) == 0)
        def _():
            dg_ref[...] = jnp.zeros_like(dg_ref)

        xf = x_ref[...]
        r = lax.rsqrt(jnp.mean(xf * xf, axis=-1, keepdims=True) + EPS)
        xh = xf * r
        dn_ = dn_ref[...].astype(F32)
        dg_ref[...] += jnp.sum(dn_ * xh, axis=0, keepdims=True)
        dxh = dn_ * g_ref[...]
        dx = r * (dxh - xh * jnp.mean(dxh * xh, axis=-1, keepdims=True))
        dx_ref[...] = dres_ref[...] + dx

    row = pl.BlockSpec((tm, D), lambda i: (i, 0))
    vec = pl.BlockSpec((1, D), lambda i: (0, 0))
    return pl.pallas_call(
        body, name=name, grid=(M // tm,),
        in_specs=[row, vec, row, row], out_specs=[row, vec],
        out_shape=[jax.ShapeDtypeStruct((M, D), F32), jax.ShapeDtypeStruct((1, D), F32)],
        compiler_params=_cparams(("arbitrary",)),
    )(x, g.reshape(1, D), dn, dres)


def _loss_head(h, g, tgt, *, name, tm=512):
    M, D = h.shape
    tm = min(tm, M)

    def body(h_ref, g_ref, t_ref, loss_ref, dh_ref, dg_ref):
        @pl.when(pl.program_id(0) == 0)
        def _():
            dg_ref[...] = jnp.zeros_like(dg_ref)
            loss_ref[...] = jnp.zeros_like(loss_ref)

        xf = h_ref[...]
        r = lax.rsqrt(jnp.mean(xf * xf, axis=-1, keepdims=True) + EPS)
        xh = xf * r
        err = xh * g_ref[...] - t_ref[...]
        part = jnp.sum(jnp.mean(err * err, axis=-1, keepdims=True), axis=0, keepdims=True)
        loss_ref[...] += 0.5 * part
        dy = err * (1.0 / D)
        dg_ref[...] += jnp.sum(dy * xh, axis=0, keepdims=True)
        dxh = dy * g_ref[...]
        dh_ref[...] = r * (dxh - xh * jnp.mean(dxh * xh, axis=-1, keepdims=True))

    row = pl.BlockSpec((tm, D), lambda i: (i, 0))
    vec = pl.BlockSpec((1, D), lambda i: (0, 0))
    one = pl.BlockSpec((1, 1), lambda i: (0, 0))
    return pl.pallas_call(
        body, name=name, grid=(M // tm,),
        in_specs=[row, vec, row], out_specs=[one, row, vec],
        out_shape=[jax.ShapeDtypeStruct((1, 1), F32), jax.ShapeDtypeStruct((M, D), F32),
                   jax.ShapeDtypeStruct((1, D), F32)],
        compiler_params=_cparams(("arbitrary",)),
    )(h, g.reshape(1, D), tgt)


HG_MID = HG_CHUNK // 2 - 1
EXP_CAP = 80.0


def _sigmoid(x):
    return 1.0 / (1.0 + jnp.exp(-x))


def _dot(a, b, dims, precision=None):
    return lax.dot_general(a, b, dims, preferred_element_type=F32, precision=precision)


def _bdot(a, b, form):
    return _dot(a.astype(BF16), b.astype(BF16), _DIMS[form])


def _split2(x):
    hi = x.astype(BF16)
    return hi, (x - hi.astype(F32)).astype(BF16)


def _dot3(a, b, form):
    d = _DIMS[form]
    return _dot(a[0], b[0], d) + (_dot(a[0], b[1], d) + _dot(a[1], b[0], d))


def _hgrn_chunk_common(hq, hf, lbv, tril, rid):
    sq = _sigmoid(hq)
    q = hq * sq
    sg = _sigmoid(hf)
    f = lbv + (1.0 - lbv) * sg
    k = (1.0 - lbv) * (1.0 - sg)
    g = jnp.log(f)
    b = _dot(tril, g, _DIMS["nn"], precision=lax.Precision.HIGHEST)
    bref = jnp.sum(jnp.where(rid == HG_MID, b, 0.0), axis=0, keepdims=True)
    bend = jnp.sum(jnp.where(rid == HG_CHUNK - 1, b, 0.0), axis=0, keepdims=True)
    eb = jnp.exp(b)
    e1 = jnp.exp(jnp.minimum(b - bref, EXP_CAP))
    e2 = jnp.exp(jnp.minimum(bref - b, EXP_CAP))
    e3 = jnp.exp(bend - b)
    return sq, q, sg, f, k, bend, eb, e1, e2, e3


def _hgrn_fwd(proj, lb, gnorm, *, name, T=1024):
    S = proj.shape[0]
    T = min(T, S)
    nch = T // HG_CHUNK
    C = HG_CHUNK

    def body(hq_ref, hf_ref, hi_ref, hg_ref, lb_ref, gn_ref, o_ref, oa_ref, st_ref, state):
        @pl.when(pl.program_id(1) == 0)
        def _():
            state[...] = jnp.zeros_like(state)

        lbv = lb_ref[...]
        gn = gn_ref[...]
        row = lax.broadcasted_iota(jnp.int32, (C, C), 0)
        col = lax.broadcasted_iota(jnp.int32, (C, C), 1)
        causal = row >= col
        tril = causal.astype(F32)
        rid = lax.broadcasted_iota(jnp.int32, (C, HG_DK), 0)
        sls = [pl.ds(c * C, C) for c in range(nch)]
        pre = [_hgrn_chunk_common(hq_ref[sl, :], hf_ref[sl, :], lbv, tril, rid) for sl in sls]
        v_l = [hi_ref[sl, :].astype(BF16) for sl in sls]
        a_l, u_l = [], []
        for c in range(nch):
            _, q, _, _, k, _, _, e1, e2, e3 = pre[c]
            a_l.append(jnp.where(causal, _bdot(q * e1, k * e2, "nt"), 0.0))
            u_l.append(_bdot(v_l[c], k * e3, "tn"))
        o_l = [_bdot(a_l[c], v_l[c], "nn") for c in range(nch)]
        st = state[...]
        st_l = []
        for c in range(nch):
            st_l.append(st)
            st = st * jnp.exp(pre[c][5]) + u_l[c]
        state[...] = st
        for c in range(nch):
            st_ref[0, c] = st_l[c]
            o_l[c] = o_l[c] + _bdot(pre[c][1] * pre[c][6], st_l[c], "nt")
        for c in range(nch):
            o, hg = o_l[c], hg_ref[sls[c], :]
            o_ref[sls[c], :] = o
            r = lax.rsqrt(jnp.mean(o * o, axis=-1, keepdims=True) + EPS)
            oa_ref[sls[c], :] = (o * r * gn * (hg * _sigmoid(hg))).astype(oa_ref.dtype)

    def grp(gidx):
        return pl.BlockSpec((T, 128), lambda h, t: (t, gidx * 8 + h))

    return pl.pallas_call(
        body, name=name, grid=(HG_HEADS, S // T),
        in_specs=[grp(0), grp(1), grp(2), grp(3),
                  pl.BlockSpec((1, 128), lambda h, t: (0, h)), pl.BlockSpec((1, 128), lambda h, t: (0, 0))],
        out_specs=[pl.BlockSpec((T, 128), lambda h, t: (t, h)), pl.BlockSpec((T, 128), lambda h, t: (t, h)),
                   pl.BlockSpec((1, nch, HG_DV, HG_DK), lambda h, t: (h, t, 0, 0))],
        out_shape=[jax.ShapeDtypeStruct((S, HG_HEADS * HG_DV), F32), jax.ShapeDtypeStruct((S, HG_HEADS * HG_DV), BF16),
                   jax.ShapeDtypeStruct((HG_HEADS, S // C, HG_DV, HG_DK), F32)],
        scratch_shapes=[pltpu.VMEM((HG_DV, HG_DK), F32)],
        compiler_params=_cparams(("parallel", "arbitrary")),
    )(proj, proj, proj, proj, lb, gnorm)


def _hgrn_bwd(proj, lb, gnorm, o, states, doa, *, name, T=1024):
    S = proj.shape[0]
    T = min(T, S)
    nch = T // HG_CHUNK
    C = HG_CHUNK
    nT = S // T

    def body(hq_ref, hf_ref, hi_ref, hg_ref, lb_ref, gn_ref, o_ref, st_ref, doa_ref,
             dhq_ref, dhf_ref, dhi_ref, dhg_ref, dlb_ref, dgn_ref, dstate):
        @pl.when(pl.program_id(1) == 0)
        def _():
            dstate[...] = jnp.zeros_like(dstate)
            dlb_ref[...] = jnp.zeros_like(dlb_ref)
            dgn_ref[...] = jnp.zeros_like(dgn_ref)

        lbv = lb_ref[...]
        gn = gn_ref[...]
        row = lax.broadcasted_iota(jnp.int32, (C, C), 0)
        col = lax.broadcasted_iota(jnp.int32, (C, C), 1)
        causal = row >= col
        tril = causal.astype(F32)
        triu = (row <= col).astype(F32)
        rid = lax.broadcasted_iota(jnp.int32, (C, HG_DK), 0)
        rng = range(nch)
        sls = [pl.ds(c * C, C) for c in rng]
        pre = [_hgrn_chunk_common(hq_ref[sl, :], hf_ref[sl, :], lbv, tril, rid) for sl in sls]
        do2, dgn_acc = [], jnp.zeros((1, HG_DV), F32)
        for c in rng:
            hg, ov = hg_ref[sls[c], :], o_ref[sls[c], :]
            r = lax.rsqrt(jnp.mean(ov * ov, axis=-1, keepdims=True) + EPS)
            xh = ov * r
            sgg = _sigmoid(hg)
            d_oa = doa_ref[sls[c], :].astype(F32)
            dz = d_oa * (hg * sgg)
            dhg_ref[sls[c], :] = (d_oa * (xh * gn) * (sgg * (1.0 + hg * (1.0 - sgg)))).astype(dhg_ref.dtype)
            dgn_acc = dgn_acc + jnp.sum(dz * xh, axis=0, keepdims=True)
            dxh = dz * gn
            do2.append(_split2(r * (dxh - xh * jnp.mean(dxh * xh, axis=-1, keepdims=True))))
        dgn_ref[0] += dgn_acc
        qi = [pre[c][1] * pre[c][6] for c in rng]
        qp = [pre[c][1] * pre[c][7] for c in rng]
        kp = [pre[c][4] * pre[c][8] for c in rng]
        kend = [pre[c][4] * pre[c][9] for c in rng]
        qi2, qp2, kp2, kend2 = ([_split2(t) for t in lst] for lst in (qi, qp, kp, kend))
        v2 = [_split2(hi_ref[sl, :]) for sl in sls]
        st0 = [st_ref[0, c] for c in rng]
        a_l = [jnp.where(causal, _dot(qp2[c][0], kp2[c][0], _DIMS["nt"]), 0.0).astype(BF16) for c in rng]
        da2 = [_split2(jnp.where(causal, _dot3(do2[c], v2[c], "nt"), 0.0)) for c in rng]
        dqi = [_dot3(do2[c], _split2(st0[c]), "nn") for c in rng]
        w_l = [_dot3(do2[c], qi2[c], "tn") for c in rng]
        ds = dstate[...]
        ds1 = [None] * nch
        for c in reversed(rng):
            ds1[c] = ds
            ds = ds * jnp.exp(pre[c][5]) + w_l[c]
        dstate[...] = ds
        ds12 = [_split2(t) for t in ds1]
        dqp = [_dot3(da2[c], kp2[c], "nn") for c in rng]
        dkp = [_dot3(da2[c], qp2[c], "tn") for c in rng]
        dv = [_dot(a_l[c], do2[c][0], _DIMS["tn"]) + _dot(kend2[c][0], ds12[c][0], _DIMS["nt"]) for c in rng]
        dkend = [_dot3(v2[c], ds12[c], "nn") for c in rng]
        dq_l, dk_l, db_l = [], [], []
        for c in rng:
            _, _, _, _, _, bend, eb, e1, e2, e3 = pre[c]
            dq_l.append(dqi[c] * eb + dqp[c] * e1)
            dk_l.append(dkp[c] * e2 + dkend[c] * e3)
            db = dqi[c] * qi[c] + dqp[c] * qp[c] - dkp[c] * kp[c] - dkend[c] * kend[c]
            dbend = (jnp.sum(dkend[c] * kend[c], axis=0, keepdims=True)
                     + jnp.exp(bend) * jnp.sum(ds1[c] * st0[c], axis=0, keepdims=True))
            db_l.append(db + jnp.where(rid == C - 1, dbend, 0.0))
        dg = [_dot(triu, db_l[c], _DIMS["nn"], precision=lax.Precision.HIGHEST) for c in rng]
        dlb_acc = jnp.zeros((1, HG_DK), F32)
        for c in rng:
            sq, _, sg, f, _, _, _, _, _, _ = pre[c]
            hq = hq_ref[sls[c], :]
            df = dg[c] / f - dk_l[c]
            dlb_acc = dlb_acc + jnp.sum(df * (1.0 - sg), axis=0, keepdims=True)
            dhf_ref[sls[c], :] = (df * (1.0 - lbv) * sg * (1.0 - sg)).astype(dhf_ref.dtype)
            dhq_ref[sls[c], :] = (dq_l[c] * (sq * (1.0 + hq * (1.0 - sq)))).astype(dhq_ref.dtype)
            dhi_ref[sls[c], :] = dv[c].astype(dhi_ref.dtype)
        dlb_ref[...] += dlb_acc

    def grp(gidx):
        return pl.BlockSpec((T, 128), lambda h, t: (nT - 1 - t, gidx * 8 + h))

    tok = pl.BlockSpec((T, 128), lambda h, t: (nT - 1 - t, h))
    big = jax.ShapeDtypeStruct((S, HG_HEADS * HG_DV), BF16)
    return pl.pallas_call(
        body, name=name, grid=(HG_HEADS, nT),
        in_specs=[grp(0), grp(1), grp(2), grp(3),
                  pl.BlockSpec((1, 128), lambda h, t: (0, h)), pl.BlockSpec((1, 128), lambda h, t: (0, 0)),
                  tok, pl.BlockSpec((1, nch, HG_DV, HG_DK), lambda h, t: (h, nT - 1 - t, 0, 0)), tok],
        out_specs=[tok, tok, tok, tok, pl.BlockSpec((1, 128), lambda h, t: (0, h)),
                   pl.BlockSpec((1, 1, 128), lambda h, t: (h, 0, 0))],
        out_shape=[big, big, big, big, jax.ShapeDtypeStruct((1, HG_HEADS * HG_DK), F32),
                   jax.ShapeDtypeStruct((HG_HEADS, 1, HG_DV), F32)],
        scratch_shapes=[pltpu.VMEM((HG_DV, HG_DK), F32)],
        compiler_params=_cparams(("parallel", "arbitrary")),
    )(proj, proj, proj, proj, lb, gnorm, o, states, doa)


def _lb_fwd(logits, *, name):
    def body(l_ref, lb_ref):
        lb_ref[...] = _sigmoid(l_ref[0:1, :] - l_ref[1:2, :])

    return pl.pallas_call(body, name=name, out_shape=jax.ShapeDtypeStruct((1, logits.shape[1]), F32))(logits)


def _lb_bwd(logits, dlb, *, name):
    def body(l_ref, d_ref, o_ref):
        lbv = _sigmoid(l_ref[0:1, :] - l_ref[1:2, :])
        t = d_ref[...] * lbv * (1.0 - lbv)
        o_ref[0:1, :] = t
        o_ref[1:2, :] = -t

    return pl.pallas_call(body, name=name, out_shape=jax.ShapeDtypeStruct(logits.shape, F32))(logits, dlb)


NEG = -1e30
FOX_SCALE = FOX_DH ** -0.5
FOX_PAIRS = FOX_HEADS // 2


def _fox_gate_fwd(ff, bias, *, name, T=512):
    S = ff.shape[0]
    T = min(T, S)

    def body(ff_ref, b_ref, c_ref, carry):
        @pl.when(pl.program_id(0) == 0)
        def _():
            carry[...] = jnp.zeros_like(carry)

        z = ff_ref[...] + b_ref[...]
        logf = jnp.minimum(z, 0.0) - jnp.log(1.0 + jnp.exp(-jnp.abs(z)))
        row = lax.broadcasted_iota(jnp.int32, (T, T), 0)
        col = lax.broadcasted_iota(jnp.int32, (T, T), 1)
        c = _dot((row >= col).astype(F32), logf, _DIMS["nn"], precision=lax.Precision.HIGHEST) + carry[...]
        c_ref[...] = c
        carry[...] = c[T - 1:T, :]

    return pl.pallas_call(
        body, name=name, grid=(S // T,),
        in_specs=[pl.BlockSpec((T, 128), lambda i: (i, 0)), pl.BlockSpec((1, 128), lambda i: (0, 0))],
        out_specs=pl.BlockSpec((T, 128), lambda i: (i, 0)),
        out_shape=jax.ShapeDtypeStruct((S, 128), F32),
        scratch_shapes=[pltpu.VMEM((1, 128), F32)],
        compiler_params=_cparams(("arbitrary",)),
    )(ff, bias)


def _fox_gate_bwd(ff, bias, dcs, *, name, T=512):
    S = ff.shape[0]
    T = min(T, S)
    nT = S // T

    def body(ff_ref, b_ref, d_ref, dff_ref, db_ref, carry):
        @pl.when(pl.program_id(0) == 0)
        def _():
            carry[...] = jnp.zeros_like(carry)
            db_ref[...] = jnp.zeros_like(db_ref)

        row = lax.broadcasted_iota(jnp.int32, (T, T), 0)
        col = lax.broadcasted_iota(jnp.int32, (T, T), 1)
        dlogf = carry[...] - _dot((row <= col).astype(F32), d_ref[...], _DIMS["nn"], precision=lax.Precision.HIGHEST)
        carry[...] = dlogf[0:1, :]
        dff = dlogf * (1.0 - _sigmoid(ff_ref[...] + b_ref[...]))
        dff_ref[...] = dff.astype(dff_ref.dtype)
        db_ref[...] += jnp.sum(dff, axis=0, keepdims=True)

    rev = pl.BlockSpec((T, 128), lambda i: (nT - 1 - i, 0))
    vec = pl.BlockSpec((1, 128), lambda i: (0, 0))
    return pl.pallas_call(
        body, name=name, grid=(nT,),
        in_specs=[rev, vec, rev], out_specs=[rev, vec],
        out_shape=[jax.ShapeDtypeStruct((S, 128), BF16), jax.ShapeDtypeStruct((1, 128), F32)],
        scratch_shapes=[pltpu.VMEM((1, 128), F32)],
        compiler_params=_cparams(("arbitrary",)),
    )(ff, bias, dcs)


AUG = FOX_DH
RSUM_LANE = 6


def _bias_lane(hh):
    return AUG * (1 - hh)


def _data_lanes(lane, hh):
    return (lane < AUG) if hh == 0 else (lane >= AUG)


def _split3(x):
    a = x.astype(BF16).astype(F32)
    r = x - a
    b = r.astype(BF16).astype(F32)
    return a, b, r - b


def _lane_fill(lane, base, pieces, start):
    for i, pc in enumerate(pieces):
        base = jnp.where(lane == start + i, pc, base)
    return base


FOX_TB = 512
FOX_SKIP = 40.0
N_STAT = 4


def _fox_prep(proj, c_tok, *, name):
    S = proj.shape[0]
    T = min(FOX_TB, S)

    def body(q_ref, k_ref, v_ref, c_ref, qa_ref, ka_ref, va_ref, st_ref):
        pair = pl.program_id(0)
        lane = lax.broadcasted_iota(jnp.int32, (T, 128), 1)
        lane1 = lax.broadcasted_iota(jnp.int32, (1, 128), 1)
        c = c_ref[...]
        q, k, v = q_ref[...], k_ref[...], v_ref[...]
        for hh in range(2):
            data, b0 = _data_lanes(lane, hh), _bias_lane(hh)
            ones3 = jnp.where((lane >= b0) & (lane < b0 + 3), 1.0, 0.0)

            def max_norm(t):
                tr = jnp.where(data, t.astype(BF16).astype(F32), 0.0)
                return jnp.sqrt(jnp.max(jnp.sum(tr * tr, axis=-1, keepdims=True), axis=0, keepdims=True))

            ch = jnp.sum(jnp.where(lane == 2 * pair + hh, c, 0.0), axis=-1, keepdims=True)
            c1, c2, c3 = _split3(ch)
            aug_q = _lane_fill(lane, jnp.where((lane >= b0 + 3) & (lane < b0 + 6), 1.0, 0.0), (c1, c2, c3), b0)
            aug_k = _lane_fill(lane, ones3, (-c1, -c2, -c3), b0 + 3)
            qa_ref[hh] = jnp.where(data, q * FOX_SCALE, aug_q).astype(BF16)
            ka_ref[hh] = jnp.where(data, k, aug_k).astype(BF16)
            va_ref[hh] = jnp.where(data, v, ones3).astype(BF16)
            stats = (max_norm(q * FOX_SCALE), jnp.max(ch, axis=0, keepdims=True), max_norm(k),
                     jnp.min(ch, axis=0, keepdims=True))
            st_ref[hh, 0] = _lane_fill(lane1, jnp.zeros((1, 128), F32), stats, 0)

    def grp(g):
        return pl.BlockSpec((T, 128), lambda p, t: (t, g * 8 + p))

    hm = pl.BlockSpec((2, T, 128), lambda p, t: (p, t, 0))
    out = jax.ShapeDtypeStruct((FOX_HEADS, S, 128), BF16)
    return pl.pallas_call(
        body, name=name, grid=(FOX_PAIRS, S // T),
        in_specs=[grp(4), grp(5), grp(6), pl.BlockSpec((T, 128), lambda p, t: (t, 0))],
        out_specs=[hm, hm, hm, pl.BlockSpec((2, 1, 1, 128), lambda p, t: (p, t, 0, 0))],
        out_shape=[out, out, out, jax.ShapeDtypeStruct((FOX_HEADS, S // T, 1, 128), F32)],
        compiler_params=_cparams(("parallel", "parallel")),
    )(proj, proj, proj, c_tok)


def _fox_bound(st_ref, head, nb, qi, ki):
    qb_, kb_ = (head * nb + qi) * N_STAT, (head * nb + ki) * N_STAT
    return st_ref[qb_] * st_ref[kb_ + 2] + st_ref[qb_ + 1] - st_ref[kb_ + 3] + 0.01


def _pair_lanes(lane, a0, a1):
    return jnp.where(lane < AUG, a0, a1)


def _first_live_key(st_ref, head, nb, qi, newest, thr):
    def body(t, k0):
        k = newest - t
        return jnp.where(_fox_bound(st_ref, head, nb, qi, k) > thr, k, k0)

    return lax.fori_loop(0, newest + 1, body, newest + 1)


def _last_live_query(st_ref, lm_ref, head, nb, ki):
    def body(t, i1):
        i = ki + 1 + t
        live = _fox_bound(st_ref, head, nb, i, ki) > lm_ref[head * nb + i] - FOX_SKIP
        return jnp.where(live, i, i1)

    return lax.fori_loop(0, nb - 1 - ki, body, ki)


class _BlockStream:
    def __init__(self, hbm_refs, bufs, sems, pair, tb):
        self.hbm, self.bufs, self.sems, self.pair, self.tb = hbm_refs, bufs, sems, pair, tb

    def _copies(self, blk, slot):
        rows = pl.ds(pl.multiple_of(blk * self.tb, self.tb), self.tb)
        return [pltpu.make_async_copy(h.at[pl.ds(2 * self.pair, 2), rows, :], b.at[slot], self.sems.at[n, slot])
                for n, (h, b) in enumerate(zip(self.hbm, self.bufs))]

    def start(self, blk, slot):
        for cp in self._copies(blk, slot):
            cp.start()

    def wait(self, blk, slot):
        for cp in self._copies(blk, slot):
            cp.wait()


def _fox_fwd(qa, ka, va, bounds, *, name):
    S = qa.shape[1]
    tb = min(FOX_TB, S)
    nb = S // tb

    def body(qa_ref, ka_hbm, va_hbm, st_ref, o_ref, qb_ref, lse_ref, kbuf, vbuf, sems, m_s, acc_s, m_min):
        pair, qi = pl.program_id(0), pl.program_id(1)
        stream = _BlockStream((ka_hbm, va_hbm), (kbuf, vbuf), sems, pair, tb)

        def head_step(hh, slot, masked):
            s = _dot(qa_ref[hh], kbuf[slot, hh], _DIMS["nt"])
            if masked:
                row = lax.broadcasted_iota(jnp.int32, (tb, tb), 0)
                col = lax.broadcasted_iota(jnp.int32, (tb, tb), 1)
                s = jnp.where(col <= row, s, NEG)
            m_old = m_s[hh]
            m_new = jnp.maximum(m_old, jnp.max(s, axis=-1, keepdims=True))
            p = jnp.exp(s - m_new)
            p_hi = p.astype(BF16)
            p_lo = (p - p_hi.astype(F32)).astype(BF16)
            vv = vbuf[slot, hh]
            acc_s[hh] = (jnp.exp(m_old - m_new) * acc_s[hh]
                         + _dot(p_hi, vv, _DIMS["nn"]) + _dot(p_lo, vv, _DIMS["nn"]))
            m_s[hh] = m_new
            m_min[hh] = jnp.min(m_new)

        @pl.when(qi == 0)
        def _():
            stream.start(qi, 0)

        @pl.when(qi > 0)
        def _():
            stream.start(qi - 1, 1)

        m_s[...] = jnp.full_like(m_s, NEG)
        acc_s[...] = jnp.zeros_like(acc_s)
        stream.wait(qi, 0)
        for hh in range(2):
            head_step(hh, 0, True)

        @pl.when(qi > 1)
        def _():
            stream.start(qi - 2, 0)

        @pl.when(qi > 0)
        def _():
            stream.wait(qi - 1, 1)
            for hh in range(2):
                head_step(hh, 1, False)

        k0 = [_first_live_key(st_ref, 2 * pair + hh, nb, qi, qi - 2, m_min[hh] - FOX_SKIP) for hh in range(2)]
        n = qi - 1 - jnp.minimum(k0[0], k0[1])

        @pl.when((qi > 1) & (n == 0))
        def _():
            stream.wait(qi - 2, 0)

        def loop(t, carry):
            k = qi - 2 - t
            slot = t % 2
            stream.wait(k, slot)

            @pl.when(t + 1 < n)
            def _():
                stream.start(k - 1, 1 - slot)

            live = [k >= k0[hh] for hh in range(2)]

            @pl.when(live[0] & live[1])
            def _():
                for hh in range(2):
                    head_step(hh, slot, False)

            for hh in range(2):
                @pl.when(live[hh] & jnp.logical_not(live[1 - hh]))
                def _():
                    head_step(hh, slot, False)
            return carry

        lax.fori_loop(0, n, loop, 0)

        @pl.when(qi + 1 < nb)
        def _():
            stream.start(qi + 1, 0)

        lane = lax.broadcasted_iota(jnp.int32, (tb, 128), 1)
        outs = []
        for hh in range(2):
            acc = acc_s[hh]
            b0 = _bias_lane(hh)
            l = acc[:, b0:b0 + 1]
            outs.append(acc / l)
            lse = m_s[hh] + jnp.log(l)
            lse_ref[hh, 0] = jnp.broadcast_to(jnp.min(lse, axis=0, keepdims=True), (1, 128))
            qf = qa_ref[hh].astype(F32)
            cb = qf[:, b0:b0 + 1] + qf[:, b0 + 1:b0 + 2] + qf[:, b0 + 2:b0 + 3] - lse
            qb_ref[hh] = _lane_fill(lane, qf, _split3(cb), b0).astype(BF16)
        o_ref[...] = _pair_lanes(lane, outs[0], outs[1])

    qs = pl.BlockSpec((2, tb, 128), lambda p, i: (p, i, 0))
    return pl.pallas_call(
        body, name=name, grid=(FOX_PAIRS, nb),
        in_specs=[qs, ANY, ANY, SMEM],
        out_specs=[pl.BlockSpec((tb, 128), lambda p, i: (i, p)), qs,
                   pl.BlockSpec((2, 1, 1, 128), lambda p, i: (p, i, 0, 0))],
        out_shape=[jax.ShapeDtypeStruct((S, FOX_HEADS * FOX_DH), F32), jax.ShapeDtypeStruct((FOX_HEADS, S, 128), BF16),
                   jax.ShapeDtypeStruct((FOX_HEADS, nb, 1, 128), F32)],
        scratch_shapes=[pltpu.VMEM((2, 2, tb, 128), BF16), pltpu.VMEM((2, 2, tb, 128), BF16),
                        pltpu.SemaphoreType.DMA((2, 2)), pltpu.VMEM((2, tb, 1), F32), pltpu.VMEM((2, tb, 128), F32),
                        pltpu.SMEM((2,), F32)],
        compiler_params=_cparams(("arbitrary", "arbitrary")),
    )(qa, ka, va, bounds)


def _fox_bwd_prep(o, do, *, name, T=512):
    S = o.shape[0]
    T = min(T, S)

    def body(o_ref, do_ref, dob_ref):
        lane = lax.broadcasted_iota(jnp.int32, (T, 128), 1)
        d = do_ref[...].astype(F32)
        prod = d * o_ref[...]
        for hh in range(2):
            mine = _data_lanes(lane, hh)
            delta = jnp.sum(jnp.where(mine, prod, 0.0), axis=-1, keepdims=True)
            dob_ref[hh] = _lane_fill(lane, jnp.where(mine, d, 0.0), _split3(-delta), _bias_lane(hh)).astype(BF16)

    tok = pl.BlockSpec((T, 128), lambda p, t: (t, p))
    return pl.pallas_call(
        body, name=name, grid=(FOX_PAIRS, S // T),
        in_specs=[tok, tok], out_specs=pl.BlockSpec((2, T, 128), lambda p, t: (p, t, 0)),
        out_shape=jax.ShapeDtypeStruct((FOX_HEADS, S, 128), BF16),
        compiler_params=_cparams(("parallel", "parallel")),
    )(o, do)


def _fox_bwd_dq(qb, ka, va, dob, bounds, lse_min, *, name, comm=None):
    S = qb.shape[1]
    tb = min(FOX_TB, S)
    nb = S // tb
    nc = comm.n if comm is not None else 0

    def body(qb_ref, dob_ref, ka_hbm, va_hbm, st_ref, lm_ref, *rest):
        c_in, (dq_ref, dob2_ref), c_out = rest[:nc], rest[nc:nc + 2], rest[nc + 2:2 * nc + 2]
        kbuf, vbuf, sems, acc_s = rest[2 * nc + 2:2 * nc + 6]
        c_sems = rest[2 * nc + 6:]
        pair, qi = pl.program_id(0), pl.program_id(1)
        if comm is not None:
            @pl.when((pair == 0) & (qi == 0))
            def _():
                comm.start(c_in, c_out, c_sems)

        stream = _BlockStream((ka_hbm, va_hbm), (kbuf, vbuf), sems, pair, tb)
        k0 = [_first_live_key(st_ref, 2 * pair + hh, nb, qi, qi - 1, lm_ref[(2 * pair + hh) * nb + qi] - FOX_SKIP)
              for hh in range(2)]
        n = qi - jnp.minimum(k0[0], k0[1]) + 1

        @pl.when(qi == 0)
        def _():
            stream.start(qi, 0)

        acc_s[...] = jnp.zeros_like(acc_s)

        def head_step(hh, slot, k, masked):
            s = _dot(qb_ref[hh], kbuf[slot, hh], _DIMS["nt"])
            if masked:
                row = lax.broadcasted_iota(jnp.int32, (tb, tb), 0)
                col = lax.broadcasted_iota(jnp.int32, (tb, tb), 1)
                s = jnp.where(col <= row, s, NEG)
            ds = jnp.exp(s) * _dot(dob_ref[hh], vbuf[slot, hh], _DIMS["nt"])
            acc_s[hh] += _dot(ds.astype(BF16), kbuf[slot, hh], _DIMS["nn"])

        def loop(t, carry):
            k = qi - t
            slot = t % 2
            stream.wait(k, slot)

            @pl.when(t + 1 < n)
            def _():
                stream.start(k - 1, 1 - slot)

            @pl.when(t == 0)
            def _():
                for hh in range(2):
                    head_step(hh, slot, k, True)

            for hh in range(2):
                @pl.when((t > 0) & (k >= k0[hh]))
                def _():
                    head_step(hh, slot, k, False)
            return carry

        lax.fori_loop(0, n, loop, 0)

        @pl.when(qi + 1 < nb)
        def _():
            stream.start(qi + 1, 0)

        lane = lax.broadcasted_iota(jnp.int32, (tb, 128), 1)
        dq_ref[...] = (_pair_lanes(lane, acc_s[0], acc_s[1]) * FOX_SCALE).astype(dq_ref.dtype)
        for hh in range(2):
            b0 = _bias_lane(hh)
            r = acc_s[hh][:, b0:b0 + 1]
            dob2_ref[hh] = _lane_fill(lane, dob_ref[hh].astype(F32), _split3(r), b0 + RSUM_LANE).astype(BF16)
        if comm is not None:
            @pl.when((pair == FOX_PAIRS - 1) & (qi == nb - 1))
            def _():
                comm.finish(c_in, c_out, c_sems)

    qs = pl.BlockSpec((2, tb, 128), lambda p, i: (p, i, 0))
    outs = pl.pallas_call(
        body, name=name, grid=(FOX_PAIRS, nb),
        in_specs=[qs, qs, ANY, ANY, SMEM, SMEM] + [ANY] * nc,
        out_specs=[pl.BlockSpec((tb, 128), lambda p, i: (i, p)), qs] + [ANY] * nc,
        out_shape=[jax.ShapeDtypeStruct((S, FOX_HEADS * FOX_DH), BF16),
                   jax.ShapeDtypeStruct((FOX_HEADS, S, 128), BF16)] + (comm.out_shapes if comm is not None else []),
        scratch_shapes=[pltpu.VMEM((2, 2, tb, 128), BF16), pltpu.VMEM((2, 2, tb, 128), BF16),
                        pltpu.SemaphoreType.DMA((2, 2)), pltpu.VMEM((2, tb, 128), F32)]
        + (comm.scratch if comm is not None else []),
        compiler_params=_cparams(("arbitrary", "arbitrary")),
    )(qb, dob, ka, va, bounds, lse_min, *(comm.inputs if comm is not None else []))
    return (outs[0], outs[1]) if comm is None else (outs[0], outs[1], outs[2:])


def _fox_bwd_dkv(qb, ka, va, dob, bounds, lse_min, *, name):
    S = qb.shape[1]
    tb = min(FOX_TB, S)
    nb = S // tb

    def body(ka_ref, va_ref, qb_hbm, dob_hbm, st_ref, lm_ref, dk_ref, dv_ref, dcs_ref, qbuf, dbuf, sems, dk_s, dv_s):
        pair, ki = pl.program_id(0), pl.program_id(1)
        stream = _BlockStream((qb_hbm, dob_hbm), (qbuf, dbuf), sems, pair, tb)
        i1 = [_last_live_query(st_ref, lm_ref, 2 * pair + hh, nb, ki) for hh in range(2)]
        n = jnp.maximum(i1[0], i1[1]) - ki + 1

        @pl.when(ki == 0)
        def _():
            stream.start(ki, 0)

        dk_s[...] = jnp.zeros_like(dk_s)
        dv_s[...] = jnp.zeros_like(dv_s)

        def head_step(hh, slot, masked):
            st = _dot(ka_ref[hh], qbuf[slot, hh], _DIMS["nt"])
            if masked:
                row = lax.broadcasted_iota(jnp.int32, (tb, tb), 0)
                col = lax.broadcasted_iota(jnp.int32, (tb, tb), 1)
                st = jnp.where(row <= col, st, NEG)
            pt = jnp.exp(st)
            dst = pt * _dot(va_ref[hh], dbuf[slot, hh], _DIMS["nt"])
            dv_s[hh] += _dot(pt.astype(BF16), dbuf[slot, hh], _DIMS["nn"])
            dk_s[hh] += _dot(dst.astype(BF16), qbuf[slot, hh], _DIMS["nn"])

        def loop(t, carry):
            i = ki + t
            slot = t % 2
            stream.wait(i, slot)

            @pl.when(t + 1 < n)
            def _():
                stream.start(i + 1, 1 - slot)

            @pl.when(t == 0)
            def _():
                for hh in range(2):
                    head_step(hh, slot, True)

            for hh in range(2):
                @pl.when((t > 0) & (i <= i1[hh]))
                def _():
                    head_step(hh, slot, False)
            return carry

        lax.fori_loop(0, n, loop, 0)

        @pl.when(ki + 1 < nb)
        def _():
            stream.start(ki + 1, 0)

        lane = lax.broadcasted_iota(jnp.int32, (tb, 128), 1)
        dk_ref[...] = _pair_lanes(lane, dk_s[0], dk_s[1]).astype(dk_ref.dtype)
        dv_ref[...] = _pair_lanes(lane, dv_s[0], dv_s[1]).astype(dv_ref.dtype)
        for hh in range(2):
            b0 = _bias_lane(hh)
            dk_a, dv_a = dk_s[hh], dv_s[hh]
            off = dv_a[:, b0 + RSUM_LANE:b0 + RSUM_LANE + 1] + dv_a[:, b0 + RSUM_LANE + 1:b0 + RSUM_LANE + 2] \
                + dv_a[:, b0 + RSUM_LANE + 2:b0 + RSUM_LANE + 3]
            dcs_ref[0, :, hh:hh + 1] = dk_a[:, b0 + 3:b0 + 4] - off

    ks = pl.BlockSpec((2, tb, 128), lambda p, j: (p, j, 0))
    tok = pl.BlockSpec((tb, 128), lambda p, j: (j, p))
    big = jax.ShapeDtypeStruct((S, FOX_HEADS * FOX_DH), BF16)
    return pl.pallas_call(
        body, name=name, grid=(FOX_PAIRS, nb),
        in_specs=[ks, ks, ANY, ANY, SMEM, SMEM],
        out_specs=[tok, tok, pl.BlockSpec((1, tb, 2), lambda p, j: (p, j, 0))],
        out_shape=[big, big, jax.ShapeDtypeStruct((FOX_PAIRS, S, 2), F32)],
        scratch_shapes=[pltpu.VMEM((2, 2, tb, 128), BF16), pltpu.VMEM((2, 2, tb, 128), BF16),
                        pltpu.SemaphoreType.DMA((2, 2)), pltpu.VMEM((2, tb, 128), F32), pltpu.VMEM((2, tb, 128), F32)],
        compiler_params=_cparams(("arbitrary", "arbitrary")),
    )(ka, va, qb, dob, bounds, lse_min)


def _merge_fwd(proj, pa, pb, *, name, T=512):
    S, D = pa.shape
    T = min(T, S)

    def body(ga_ref, gb_ref, pa_ref, pb_ref, m_ref):
        m_ref[...] = (_sigmoid(ga_ref[...]) * pa_ref[...] + _sigmoid(gb_ref[...]) * pb_ref[...]).astype(m_ref.dtype)

    tok = pl.BlockSpec((T, D), lambda i: (i, 0))
    return pl.pallas_call(
        body, name=name, grid=(S // T,),
        in_specs=[pl.BlockSpec((T, D), lambda i: (i, 7)), pl.BlockSpec((T, D), lambda i: (i, 8)), tok, tok],
        out_specs=tok, out_shape=jax.ShapeDtypeStruct((S, D), BF16),
        compiler_params=_cparams(("parallel",)),
    )(proj, proj, pa, pb)


def _merge_bwd(proj, pa, pb, dm, *, name, T=512):
    S, D = pa.shape
    T = min(T, S)

    def body(ga_ref, gb_ref, pa_ref, pb_ref, dm_ref, dpa_ref, dpb_ref, dga_ref, dgb_ref):
        dm_ = dm_ref[...]
        sa, sb = _sigmoid(ga_ref[...]), _sigmoid(gb_ref[...])
        dpa_ref[...] = (dm_ * sa).astype(BF16)
        dpb_ref[...] = (dm_ * sb).astype(BF16)
        dga_ref[...] = (dm_ * pa_ref[...] * sa * (1.0 - sa)).astype(BF16)
        dgb_ref[...] = (dm_ * pb_ref[...] * sb * (1.0 - sb)).astype(BF16)

    tok = pl.BlockSpec((T, D), lambda i: (i, 0))
    big = jax.ShapeDtypeStruct((S, D), BF16)
    return pl.pallas_call(
        body, name=name, grid=(S // T,),
        in_specs=[pl.BlockSpec((T, D), lambda i: (i, 7)), pl.BlockSpec((T, D), lambda i: (i, 8)), tok, tok, tok],
        out_specs=[tok, tok, tok, tok], out_shape=[big, big, big, big],
        compiler_params=_cparams(("parallel",)),
    )(proj, proj, pa, pb, dm)


INV_SQRT2 = 0.7071067811865476
INV_SQRT2PI = 0.3989422804014327


def _shifted(u, prev, rid):
    m1 = jnp.where(rid == 0, prev[7:8, :], pltpu.roll(u, 1, 0))
    m2 = jnp.where(rid == 0, prev[6:7, :], jnp.where(rid == 1, prev[7:8, :], pltpu.roll(u, 2, 0)))
    return m1, m2


def _conv_acc(u, prev, w_ref, b_ref, rid):
    m1, m2 = _shifted(u, prev, rid)
    return b_ref[...] + w_ref[0:1, :] * m2 + w_ref[1:2, :] * m1 + w_ref[2:3, :] * u, m1, m2


def _convglu_fwd(ug, uv, wg, wv, bg, bv, *, name, T=512, tc=256):
    S, F = ug.shape
    T = min(T, S)

    def body(ug_ref, uv_ref, wg_ref, wv_ref, bg_ref, bv_ref, a_ref, pg, pv):
        @pl.when(pl.program_id(1) == 0)
        def _():
            pg[...] = jnp.zeros_like(pg)
            pv[...] = jnp.zeros_like(pv)

        rid = lax.broadcasted_iota(jnp.int32, (T, tc), 0)
        g_, v_ = ug_ref[...], uv_ref[...]
        accg, _, _ = _conv_acc(g_, pg[...], wg_ref, bg_ref, rid)
        accv, _, _ = _conv_acc(v_, pv[...], wv_ref, bv_ref, rid)
        gel = 0.5 * accg * (1.0 + lax.erf(accg * INV_SQRT2))
        a_ref[...] = (gel * accv).astype(a_ref.dtype)
        pg[...] = g_[T - 8:T, :]
        pv[...] = v_[T - 8:T, :]

    tok = pl.BlockSpec((T, tc), lambda j, t: (t, j))
    w3 = pl.BlockSpec((3, tc), lambda j, t: (0, j))
    b1 = pl.BlockSpec((1, tc), lambda j, t: (0, j))
    return pl.pallas_call(
        body, name=name, grid=(F // tc, S // T),
        in_specs=[tok, tok, w3, w3, b1, b1], out_specs=tok,
        out_shape=jax.ShapeDtypeStruct((S, F), BF16),
        scratch_shapes=[pltpu.VMEM((8, tc), F32), pltpu.VMEM((8, tc), F32)],
        compiler_params=_cparams(("parallel", "arbitrary")),
    )(ug, uv, wg, wv, bg, bv)


def _convglu_bwd(ug, uv, wg, wv, bg, bv, da, *, name, T=512, tc=256):
    S, F = ug.shape
    T = min(T, S)
    nT = S // T
    halo_blocks = T // 8

    def up_shift(d, nx, rid):
        p1 = jnp.where(rid == T - 1, nx[0:1, :], pltpu.roll(d, T - 1, 0))
        p2 = jnp.where(rid == T - 1, nx[1:2, :], jnp.where(rid == T - 2, nx[0:1, :], pltpu.roll(d, T - 2, 0)))
        return p1, p2

    def body(ug_ref, uv_ref, hg_ref, hv_ref, wg_ref, wv_ref, bg_ref, bv_ref, da_ref,
             dug_ref, duv_ref, dwg_ref, dwv_ref, dbg_ref, dbv_ref, ng, nv):
        @pl.when(pl.program_id(1) == 0)
        def _():
            ng[...] = jnp.zeros_like(ng)
            nv[...] = jnp.zeros_like(nv)
            for r in (dwg_ref, dwv_ref, dbg_ref, dbv_ref):
                r[...] = jnp.zeros_like(r)

        first_block = pl.program_id(1) == nT - 1
        rid = lax.broadcasted_iota(jnp.int32, (T, tc), 0)
        g_, v_ = ug_ref[...], uv_ref[...]
        pg = jnp.where(first_block, 0.0, hg_ref[...])
        pv = jnp.where(first_block, 0.0, hv_ref[...])
        accg, g1, g2 = _conv_acc(g_, pg, wg_ref, bg_ref, rid)
        accv, v1, v2 = _conv_acc(v_, pv, wv_ref, bv_ref, rid)
        cdf = 0.5 * (1.0 + lax.erf(accg * INV_SQRT2))
        pdf = INV_SQRT2PI * jnp.exp(-0.5 * accg * accg)
        da_ = da_ref[...].astype(F32)
        dgate = da_ * accv * (cdf + accg * pdf)
        dval = da_ * (accg * cdf)
        dbg_ref[...] += jnp.sum(dgate, axis=0, keepdims=True)
        dbv_ref[...] += jnp.sum(dval, axis=0, keepdims=True)
        for j, (sg_, sv_) in enumerate(((g2, v2), (g1, v1), (g_, v_))):
            dwg_ref[j:j + 1, :] += jnp.sum(dgate * sg_, axis=0, keepdims=True)
            dwv_ref[j:j + 1, :] += jnp.sum(dval * sv_, axis=0, keepdims=True)
        for d, w_ref, nx, out_ref in ((dgate, wg_ref, ng, dug_ref), (dval, wv_ref, nv, duv_ref)):
            p1, p2 = up_shift(d, nx[...], rid)
            out_ref[...] = (w_ref[2:3, :] * d + w_ref[1:2, :] * p1 + w_ref[0:1, :] * p2).astype(out_ref.dtype)
            nx[...] = d[0:8, :]

    tok = pl.BlockSpec((T, tc), lambda j, t: (nT - 1 - t, j))
    halo = pl.BlockSpec((8, tc), lambda j, t: (jnp.maximum((nT - 1 - t) * halo_blocks - 1, 0), j))
    w3 = pl.BlockSpec((3, tc), lambda j, t: (0, j))
    b1 = pl.BlockSpec((1, tc), lambda j, t: (0, j))
    big = jax.ShapeDtypeStruct((S, F), BF16)
    return pl.pallas_call(
        body, name=name, grid=(F // tc, nT),
        in_specs=[tok, tok, halo, halo, w3, w3, b1, b1, tok], out_specs=[tok, tok, w3, w3, b1, b1],
        out_shape=[big, big, jax.ShapeDtypeStruct((3, F), F32), jax.ShapeDtypeStruct((3, F), F32),
                   jax.ShapeDtypeStruct((1, F), F32), jax.ShapeDtypeStruct((1, F), F32)],
        scratch_shapes=[pltpu.VMEM((8, tc), F32), pltpu.VMEM((8, tc), F32)],
        compiler_params=_cparams(("parallel", "arbitrary")),
    )(ug, uv, ug, uv, wg, wv, bg, bv, da)


FF_LO = 7168
IN_SHARD = 1154
FF_DEV, FF_OFF = FF_LO // IN_SHARD, FF_LO % IN_SHARD


def _col_blocks(a, width):
    return jnp.stack([a[:, d * width:(d + 1) * width] for d in range(N_DEV)])


def _w_in_blocks(d_wm, d_wff):
    def block(d):
        lo = d * IN_SHARD
        if d < FF_DEV:
            return d_wm[:, lo:lo + IN_SHARD]
        if d > FF_DEV:
            return d_wm[:, lo - FOX_HEADS:lo - FOX_HEADS + IN_SHARD]
        return jnp.concatenate([d_wm[:, lo:FF_LO], d_wff[:, :FOX_HEADS], d_wm[:, FF_LO:lo + IN_SHARD - FOX_HEADS]], axis=1)

    return jnp.stack([block(d) for d in range(N_DEV)])


def _late_weights(g_a, g_b, g_o, g_up, g_cw, g_d):
    wup = jnp.concatenate([g_up[d] for d in range(N_DEV)], axis=1)
    cw = jnp.concatenate([g_cw[d] for d in range(N_DEV)], axis=1)
    return dict(wa=g_a.reshape(D_MODEL, D_MODEL), wb=g_b.reshape(D_MODEL, D_MODEL), wo=g_o.reshape(D_MODEL, D_MODEL),
                wug=wup[:, :D_FF], wuv=wup[:, D_FF:], cwg=cw[:, :D_FF], cwv=cw[:, D_FF:], wd=g_d.reshape(D_FF, D_MODEL))


def _early_grad_blocks(d_wa, d_wb, d_wo, d_wug, d_wuv, d_wd):
    up = jnp.stack([d_wug[:, d * 704:(d + 1) * 704] for d in range(4)]
                   + [d_wuv[:, d * 704:(d + 1) * 704] for d in range(4)])
    return [d_wa.reshape(N_DEV, 128, D_MODEL), d_wb.reshape(N_DEV, 128, D_MODEL), d_wo.reshape(N_DEV, 128, D_MODEL),
            up, d_wd.reshape(N_DEV, 352, D_MODEL)]


def _local_step(x, tgt, w, p, late=None, exchange=False):
    S = x.shape[0]
    mm = _matmul
    n1 = _rms_fwd(x, p["norm_mix"], name="rms1_fwd")
    if late is None:
        proj = mm(n1, w["wm"], "nn", name="proj_main")
    else:
        proj, gathered = mm(n1, w["wm"], "nn", comm=late, name="proj_main")
        w = {**w, **_late_weights(*gathered)}
    ff = mm(n1, w["wff"], "nn", name="proj_ff")
    lb = _lb_fwd(p["hg_lb_logits"], name="lb_fwd")
    gnorm = p["hg_norm"].reshape(1, HG_DV)
    o_hg, oa, states = _hgrn_fwd(proj, lb, gnorm, name="hgrn_fwd")
    bias = jnp.pad(p["fox_f_bias"].reshape(1, FOX_HEADS), ((0, 0), (0, 128 - FOX_HEADS)))
    c = _fox_gate_fwd(ff, bias, name="fox_gate_fwd")
    qa, ka, va, fox_stats = _fox_prep(proj, c, name="fox_prep")
    bounds = fox_stats[:, :, 0, :N_STAT].reshape(-1)
    ob, qb, lse_stats = _fox_fwd(qa, ka, va, bounds, name="fox_fwd")
    lse_min = lse_stats[:, :, 0, 0].reshape(-1)
    pa = mm(oa, w["wa"], "nn", name="branch_a")
    pb = mm(ob, w["wb"], "nn", name="branch_b")
    merged = _merge_fwd(proj, pa, pb, name="merge_fwd")
    h1 = mm(merged, w["wo"], "nn", addend=x, name="mix_out")
    n2 = _rms_fwd(h1, p["norm_ffn"], name="rms2_fwd")
    ug = mm(n2, w["wug"], "nn", name="up_gate")
    uv = mm(n2, w["wuv"], "nn", name="up_val")
    a = _convglu_fwd(ug, uv, w["cwg"], w["cwv"], p["cbg"], p["cbv"], name="convglu_fwd")
    h2 = mm(a, w["wd"], "nn", addend=h1, name="ffn_down")
    loss, dh2, d_norm_final = _loss_head(h2, p["norm_final"], tgt, name="loss_head")
    da = mm(dh2, w["wd"], "nt", out_dtype=BF16, name="d_act")
    d_wd = mm(a, dh2, "tn", out_dtype=BF16, name="dw_down")
    dug, duv, d_cwg, d_cwv, d_cbg, d_cbv = _convglu_bwd(
        ug, uv, w["cwg"], w["cwv"], p["cbg"], p["cbv"], da, name="convglu_bwd")
    dn2 = mm(dug, w["wug"], "nt", name="dn2_gate")
    dn2 = mm(duv, w["wuv"], "nt", addend=dn2, name="dn2_val")
    d_wug = mm(n2, dug, "tn", out_dtype=BF16, name="dw_up_gate")
    d_wuv = mm(n2, duv, "tn", out_dtype=BF16, name="dw_up_val")
    dh1, d_norm_ffn = _rms_bwd(h1, p["norm_ffn"], dn2, dh2, name="rms2_bwd")
    dmerged = mm(dh1, w["wo"], "nt", name="d_merged")
    d_wo = mm(merged, dh1, "tn", out_dtype=BF16, name="dw_out")
    dpa, dpb, dga, dgb = _merge_bwd(proj, pa, pb, dmerged, name="merge_bwd")
    doa = mm(dpa, w["wa"], "nt", name="d_oa")
    dob = mm(dpb, w["wb"], "nt", out_dtype=BF16, name="d_ob")
    d_wa = mm(oa, dpa, "tn", out_dtype=BF16, name="dw_branch_a")
    d_wb = mm(ob, dpb, "tn", out_dtype=BF16, name="dw_branch_b")
    dhq, dhf, dhi, dhg, dlb, dgn8 = _hgrn_bwd(proj, lb, gnorm, o_hg, states, doa, name="hgrn_bwd")
    d_logits = _lb_bwd(p["hg_lb_logits"], dlb, name="lb_bwd")
    dob_hm = _fox_bwd_prep(ob, dob, name="fox_bwd_prep")
    early_parts = None
    if exchange:
        comm = _ExchangeComm(_early_grad_blocks(d_wa, d_wb, d_wo, d_wug, d_wuv, d_wd))
        dq, dob2, early_parts = _fox_bwd_dq(qb, ka, va, dob_hm, bounds, lse_min, comm=comm, name="fox_bwd_dq")
    else:
        dq, dob2 = _fox_bwd_dq(qb, ka, va, dob_hm, bounds, lse_min, name="fox_bwd_dq")
    dk, dv, dcs = _fox_bwd_dkv(qb, ka, va, dob2, bounds, lse_min, name="fox_bwd_dkv")
    dcs_tok = jnp.pad(dcs.transpose(1, 0, 2).reshape(S, FOX_HEADS), ((0, 0), (0, 128 - FOX_HEADS)))
    dff, dbias = _fox_gate_bwd(ff, bias, dcs_tok, name="fox_gate_bwd")
    dproj = jnp.concatenate([dhq, dhf, dhi, dhg, dq, dk, dv, dga, dgb], axis=1)
    d_wm = mm(n1, dproj, "tn", out_dtype=BF16, name="dw_in_main")
    d_wff = mm(n1, dff, "tn", out_dtype=BF16, name="dw_in_ff")
    dn1 = mm(dff, w["wff"], "nt", name="dn1_ff")
    late_parts = None
    if exchange:
        d_cw = jnp.concatenate([d_cwg, d_cwv], axis=1)
        comm = _ExchangeComm([_w_in_blocks(d_wm, d_wff), _col_blocks(d_cw, 704)])
        dn1, late_parts = mm(dproj, w["wm"], "nt", addend=dn1, comm=comm, name="dn1_main")
    else:
        dn1 = mm(dproj, w["wm"], "nt", addend=dn1, name="dn1_main")
    dx, d_norm_mix = _rms_bwd(x, p["norm_mix"], dn1, dh1, name="rms1_bwd")
    grads = dict(
        wm=d_wm, wff=d_wff, wa=d_wa, wb=d_wb, wo=d_wo, wug=d_wug, wuv=d_wuv, cwg=d_cwg, cwv=d_cwv, wd=d_wd,
        norm_mix=d_norm_mix.reshape(-1), fox_f_bias=dbias[0, :FOX_HEADS], hg_lb_logits=d_logits,
        hg_norm=jnp.sum(dgn8, axis=0).reshape(-1), norm_ffn=d_norm_ffn.reshape(-1), cbg=d_cbg, cbv=d_cbv,
        norm_final=d_norm_final.reshape(-1), early_parts=early_parts, late_parts=late_parts)
    return loss, dx, grads


SMALL = [("norm_mix", (1, D_MODEL)), ("fox_f_bias", (1, FOX_HEADS)), ("hg_lb_logits", (2, HG_HEADS * HG_DK)),
         ("hg_norm", (1, HG_DV)), ("norm_ffn", (1, D_MODEL)), ("conv_b", (1, 2 * D_FF)), ("norm_final", (D_MODEL,))]
SMALL_ROWS = 88
SHARDED = [("w_in", (D_MODEL, 1154), 256), ("w_branch_a", (128, D_MODEL), 128), ("w_branch_b", (128, D_MODEL), 128),
           ("w_out", (128, D_MODEL), 128), ("w_up", (D_MODEL, 704), 256), ("conv_w", (3, 704), 3),
           ("w_down", (352, D_MODEL), 352)]
NAMES = ["norm_mix", "w_in", "fox_f_bias", "hg_lb_logits", "hg_norm", "w_branch_a", "w_branch_b", "w_out",
         "norm_ffn", "w_up", "conv_w", "conv_b", "w_down", "norm_final"]


def _size(shape):
    n = 1
    for s in shape:
        n *= s
    return n


def _adamw(parts, w, m, v, *, name, T):
    R, C = w.shape
    c1 = 1.0 / (1.0 - ADAM_B1 ** ADAM_STEP)
    c2 = 1.0 / (1.0 - ADAM_B2 ** ADAM_STEP)

    def body(p_ref, w_ref, m_ref, v_ref, g_ref, d_ref, nm_ref, nv_ref):
        g = p_ref[0].astype(F32)
        for s in range(1, N_DEV):
            g = g + p_ref[s].astype(F32)
        g_ref[...] = g
        nm = ADAM_B1 * m_ref[...] + (1.0 - ADAM_B1) * g
        nv = ADAM_B2 * v_ref[...] + (1.0 - ADAM_B2) * (g * g)
        nm_ref[...] = nm
        nv_ref[...] = nv
        d_ref[...] = -ADAM_LR * ((nm * c1) / (jnp.sqrt(nv * c2) + ADAM_EPS) + ADAM_WD * w_ref[...])

    blk = pl.BlockSpec((T, C), lambda i: (i, 0))
    out = jax.ShapeDtypeStruct((R, C), F32)
    return pl.pallas_call(
        body, name=name, grid=(R // T,),
        in_specs=[pl.BlockSpec((N_DEV, T, C), lambda i: (0, i, 0)), blk, blk, blk],
        out_specs=[blk, blk, blk, blk], out_shape=[out, out, out, out],
        compiler_params=_cparams(("parallel",)),
    )(parts, w, m, v)


def _pack_small(vals):
    flat = jnp.concatenate([vals[n].reshape(-1).astype(F32) for n, _ in SMALL])
    return jnp.pad(flat, (0, SMALL_ROWS * 128 - flat.shape[0])).reshape(SMALL_ROWS, 128)


def _unpack_small(buf):
    flat, out, off = buf.reshape(-1), {}, 0
    for n, shape in SMALL:
        out[n] = flat[off:off + _size(shape)].reshape(shape)
        off += _size(shape)
    return out


def kernel(x, norm_mix, w_in, fox_f_bias, hg_lb_logits, hg_norm, w_branch_a, w_branch_b, w_out, norm_ffn, w_up, conv_w, conv_b, w_down, norm_final, loss_target, m_norm_mix, m_w_in, m_fox_f_bias, m_hg_lb_logits, m_hg_norm, m_w_branch_a, m_w_branch_b, m_w_out, m_norm_ffn, m_w_up, m_conv_w, m_conv_b, m_w_down, m_norm_final, v_norm_mix, v_w_in, v_fox_f_bias, v_hg_lb_logits, v_hg_norm, v_w_branch_a, v_w_branch_b, v_w_out, v_norm_ffn, v_w_up, v_conv_w, v_conv_b, v_w_down, v_norm_final):
    wv = dict(norm_mix=norm_mix, w_in=w_in, fox_f_bias=fox_f_bias, hg_lb_logits=hg_lb_logits, hg_norm=hg_norm,
              w_branch_a=w_branch_a, w_branch_b=w_branch_b, w_out=w_out, norm_ffn=norm_ffn, w_up=w_up, conv_w=conv_w,
              conv_b=conv_b, w_down=w_down, norm_final=norm_final)
    mv = dict(norm_mix=m_norm_mix, w_in=m_w_in, fox_f_bias=m_fox_f_bias, hg_lb_logits=m_hg_lb_logits, hg_norm=m_hg_norm,
              w_branch_a=m_w_branch_a, w_branch_b=m_w_branch_b, w_out=m_w_out, norm_ffn=m_norm_ffn, w_up=m_w_up,
              conv_w=m_conv_w, conv_b=m_conv_b, w_down=m_w_down, norm_final=m_norm_final)
    vv = dict(norm_mix=v_norm_mix, w_in=v_w_in, fox_f_bias=v_fox_f_bias, hg_lb_logits=v_hg_lb_logits, hg_norm=v_hg_norm,
              w_branch_a=v_w_branch_a, w_branch_b=v_w_branch_b, w_out=v_w_out, norm_ffn=v_norm_ffn, w_up=v_w_up,
              conv_w=v_conv_w, conv_b=v_conv_b, w_down=v_w_down, norm_final=v_norm_final)

    (g_in,) = _comm_call(_GatherComm([w_in[0].astype(BF16)]), name="gather_w_in")
    w = dict(wm=jnp.concatenate([g_in[d] for d in range(FF_DEV)]
                                + [g_in[FF_DEV][:, :FF_OFF], g_in[FF_DEV][:, FF_OFF + FOX_HEADS:]]
                                + [g_in[d] for d in range(FF_DEV + 1, N_DEV)], axis=1),
             wff=jnp.pad(g_in[FF_DEV][:, FF_OFF:FF_OFF + FOX_HEADS], ((0, 0), (0, 128 - FOX_HEADS))))
    late = _GatherComm([w_branch_a[0].astype(BF16), w_branch_b[0].astype(BF16), w_out[0].astype(BF16),
                        w_up[0].astype(BF16), conv_w[0], w_down[0].astype(BF16)])
    p = dict(norm_mix=norm_mix[0], fox_f_bias=fox_f_bias[0], hg_lb_logits=hg_lb_logits, hg_norm=hg_norm[0],
             norm_ffn=norm_ffn[0], cbg=conv_b[:, :D_FF], cbv=conv_b[:, D_FF:], norm_final=norm_final)
    loss, dx, grads = _local_step(x[0], loss_target[0], w, p, late=late, exchange=True)
    loss = lax.psum(loss[0, 0], ("x", "y", "c"))

    small = _pack_small(dict(
        norm_mix=grads["norm_mix"], fox_f_bias=grads["fox_f_bias"], hg_lb_logits=grads["hg_lb_logits"],
        hg_norm=grads["hg_norm"], norm_ffn=grads["norm_ffn"], conv_b=jnp.concatenate([grads["cbg"], grads["cbv"]], axis=1),
        norm_final=grads["norm_final"]))
    (small_parts,) = _comm_call(_ExchangeComm([jnp.broadcast_to(small[None], (N_DEV, SMALL_ROWS, 128))]),
                                name="exchange_small")
    ea, eb, eo, eup, ed = grads["early_parts"]
    p_in, p_cw = grads["late_parts"]
    parts = [p_in, ea, eb, eo, eup, p_cw, ed, small_parts]
    res = {}
    for (n, shape, tile), part in zip(SHARDED, parts):
        outs = _adamw(part, wv[n].reshape(shape), mv[n].reshape(shape), vv[n].reshape(shape), name="adamw_" + n, T=tile)
        res[n] = [o.reshape(wv[n].shape) for o in outs]
    outs = _adamw(parts[-1], _pack_small(wv), _pack_small(mv), _pack_small(vv), name="adamw_small", T=SMALL_ROWS)
    small = [_unpack_small(o) for o in outs]
    for n, _ in SMALL:
        res[n] = [s[n] for s in small]
    return (loss, dx[None], *[res[n][0] for n in NAMES], *[res[n][1] for n in NAMES],
            *[res[n][2] for n in NAMES], *[res[n][3] for n in NAMES])
```

```python
import jax
import jax.numpy as jnp
from jax import lax
from jax.experimental import pallas as pl
from jax.experimental.pallas import tpu as pltpu

F32 = jnp.float32
BF16 = jnp.bfloat16

D_MODEL = 1024
HG_HEADS = 8
HG_DK = 128
HG_DV = 128
HG_CHUNK = 64
FOX_HEADS = 16
FOX_DH = 64
D_FF = 2816
EPS = 1e-6
N_DEV = 8

ADAM_LR = 0.001
ADAM_B1 = 0.9
ADAM_B2 = 0.999
ADAM_EPS = 1e-08
ADAM_WD = 0.01
ADAM_STEP = 10

VMEM_LIMIT = 56 * 1024 * 1024


def _cparams(sem):
    return pltpu.CompilerParams(dimension_semantics=sem, vmem_limit_bytes=VMEM_LIMIT)


MESH = pl.DeviceIdType.MESH
ANY = pl.BlockSpec(memory_space=pl.ANY)
SMEM = pl.BlockSpec(memory_space=pltpu.SMEM)


class _GatherComm:
    def __init__(self, shards):
        self.inputs = list(shards)
        n = self.n = len(shards)
        self.out_shapes = [jax.ShapeDtypeStruct((N_DEV,) + s.shape, s.dtype) for s in shards]
        self.scratch = [pltpu.SemaphoreType.DMA((n, 7)), pltpu.SemaphoreType.DMA((n, 7)), pltpu.SemaphoreType.DMA((n,))]

    def _parts(self, x_refs, out_refs, sems):
        send_sems, recv_sems, local_sems = sems
        x, y, c = lax.axis_index("x"), lax.axis_index("y"), lax.axis_index("c")
        me, sibling = (x, y, c), (x, y, 1 - c)
        chips = [(1 - x, y), (x, 1 - y), (1 - x, 1 - y)]

        def copy(t, k, block, to, src=None):
            slot = out_refs[t].at[4 * block[0] + 2 * block[1] + block[2]]
            return pltpu.make_async_remote_copy(
                src_ref=slot if src is None else src, dst_ref=slot,
                send_sem=send_sems.at[t, k], recv_sem=recv_sems.at[t, k], device_id=to, device_id_type=MESH)

        mine = [pltpu.make_async_copy(x_refs[t], out_refs[t].at[4 * x + 2 * y + c], local_sems.at[t])
                for t in range(self.n)]
        first = []
        for t in range(self.n):
            first.append(copy(t, 0, me, sibling, src=x_refs[t]))
            first += [copy(t, 1 + j, me, (*chip, c), src=x_refs[t]) for j, chip in enumerate(chips)]
        return c, me, sibling, chips, copy, mine, first

    def start(self, x_refs, out_refs, sems):
        _, _, _, _, _, mine, first = self._parts(x_refs, out_refs, sems)
        for cp in mine + first:
            cp.start()

    def finish(self, x_refs, out_refs, sems):
        c, me, sibling, chips, copy, mine, first = self._parts(x_refs, out_refs, sems)
        passed = []
        for j, chip in enumerate(chips):
            for t in range(self.n):
                copy(t, 1 + j, (*chip, c), me).wait_recv()
                passed.append(copy(t, 4 + j, (*chip, c), sibling))
                passed[-1].start()
        for t in range(self.n):
            copy(t, 0, sibling, me).wait_recv()
            for j, chip in enumerate(chips):
                copy(t, 4 + j, (*chip, 1 - c), me).wait_recv()
        for cp in first + passed:
            cp.wait_send()
        for cp in mine:
            cp.wait()


class _ExchangeComm:
    def __init__(self, blocks):
        self.inputs = list(blocks)
        n = self.n = len(blocks)
        self.out_shapes = [jax.ShapeDtypeStruct(b.shape, b.dtype) for b in blocks]
        self.scratch = [pltpu.SemaphoreType.DMA((n, 7)), pltpu.SemaphoreType.DMA((n, 7)), pltpu.SemaphoreType.DMA((n,))]

    def _parts(self, g_refs, out_refs, sems):
        send_sems, recv_sems, local_sems = sems
        x, y, c = lax.axis_index("x"), lax.axis_index("y"), lax.axis_index("c")
        me = 4 * x + 2 * y + c
        mine = [pltpu.make_async_copy(g_refs[t].at[me], out_refs[t].at[me], local_sems.at[t]) for t in range(self.n)]
        sends, recvs = [], []
        for k in range(1, N_DEV):
            px = 1 - x if k & 4 else x
            py = 1 - y if k & 2 else y
            pc = 1 - c if k & 1 else c
            p = 4 * px + 2 * py + pc
            for t in range(self.n):
                sends.append(pltpu.make_async_remote_copy(
                    src_ref=g_refs[t].at[p], dst_ref=out_refs[t].at[me], send_sem=send_sems.at[t, k - 1],
                    recv_sem=recv_sems.at[t, k - 1], device_id=(px, py, pc), device_id_type=MESH))
                recvs.append(pltpu.make_async_remote_copy(
                    src_ref=g_refs[t].at[p], dst_ref=out_refs[t].at[p], send_sem=send_sems.at[t, k - 1],
                    recv_sem=recv_sems.at[t, k - 1], device_id=(px, py, pc), device_id_type=MESH))
        return mine, sends, recvs

    def start(self, g_refs, out_refs, sems):
        mine, sends, _ = self._parts(g_refs, out_refs, sems)
        for cp in mine + sends:
            cp.start()

    def finish(self, g_refs, out_refs, sems):
        mine, sends, recvs = self._parts(g_refs, out_refs, sems)
        for cp in recvs:
            cp.wait_recv()
        for cp in sends:
            cp.wait_send()
        for cp in mine:
            cp.wait()


def _comm_call(comm, *, name):
    n = comm.n

    def body(*refs):
        comm.start(refs[:n], refs[n:2 * n], refs[2 * n:])
        comm.finish(refs[:n], refs[n:2 * n], refs[2 * n:])

    return pl.pallas_call(body, name=name, in_specs=[ANY] * n, out_specs=[ANY] * n, out_shape=comm.out_shapes,
                          scratch_shapes=comm.scratch)(*comm.inputs)


_DIMS = {
    "nn": (((1,), (0,)), ((), ())),
    "nt": (((1,), (1,)), ((), ())),
    "tn": (((0,), (0,)), ((), ())),
}

MATMUL_VMEM_BUDGET = 36 * 1024 * 1024
MAX_TILE = 1536


def _pick(n, prefs):
    for p in prefs:
        if n % p == 0:
            return p
    return n


def _tile_options(n):
    return [d for d in range(128, min(n, MAX_TILE) + 1, 128) if n % d == 0] or [n]


def _pick_tiles(M, N, tk, nk, sa, sb, so, has_addend, tm, tn):
    best = None
    for cm in ([tm] if tm else _tile_options(M)):
        for cn in ([tn] if tn else _tile_options(N)):
            need = 2 * (cm * tk * sa + tk * cn * sb + cm * cn * so + (cm * cn * 4 if has_addend else 0))
            need += cm * cn * 4 if nk > 1 else 0
            if need <= MATMUL_VMEM_BUDGET and (best is None or cm * cn > best[0] * best[1]
                                               or (cm * cn == best[0] * best[1] and cn > best[1])):
                best = (cm, cn)
    assert best is not None, (M, N, tk)
    return best


def _matmul(a, b, form, *, out_dtype=F32, addend=None, tm=None, tn=None, tk=None, comm=None, name):
    if form == "nn":
        (M, K), (K2, N) = a.shape, b.shape
    elif form == "nt":
        (M, K), (N, K2) = a.shape, b.shape
    else:
        (K, M), (K2, N) = a.shape, b.shape
    assert K == K2, (a.shape, b.shape, form)
    tk = tk or (K if K <= 2816 else _pick(K, (1024, 512, 256, 128)))
    nk = K // tk
    if tm is None or tn is None:
        tm, tn = _pick_tiles(M, N, tk, nk, a.dtype.itemsize, b.dtype.itemsize, jnp.dtype(out_dtype).itemsize,
                             addend is not None, tm, tn)
    assert M % tm == 0 and N % tn == 0 and K % tk == 0, (M, N, K, tm, tn, tk)
    dims = _DIMS[form]
    nc = comm.n if comm is not None else 0
    grid = (M // tm, N // tn, nk)

    def body(*refs):
        a_ref, b_ref = refs[:2]
        pos = 2
        add_ref = refs[pos] if addend is not None else None
        pos += addend is not None
        c_in, o_ref, c_out = refs[pos:pos + nc], refs[pos + nc], refs[pos + nc + 1:pos + 2 * nc + 1]
        pos += 2 * nc + 1
        acc_ref = refs[pos] if nk > 1 else None
        c_sems = refs[pos + (nk > 1):]
        if comm is not None:
            ids = [pl.program_id(d) for d in range(3)]

            @pl.when((ids[0] == 0) & (ids[1] == 0) & (ids[2] == 0))
            def _():
                comm.start(c_in, c_out, c_sems)

        def finish(r):
            if add_ref is not None:
                r = r + add_ref[...].astype(F32)
            o_ref[...] = r.astype(o_ref.dtype)

        part = lax.dot_general(a_ref[...].astype(BF16), b_ref[...].astype(BF16), dims, preferred_element_type=F32)
        if nk == 1:
            finish(part)
        else:
            k = pl.program_id(2)

            @pl.when(k == 0)
            def _():
                acc_ref[...] = part

            @pl.when(k > 0)
            def _():
                acc_ref[...] += part

            @pl.when(k == nk - 1)
            def _():
                finish(acc_ref[...])

        if comm is not None:
            @pl.when((ids[0] == grid[0] - 1) & (ids[1] == grid[1] - 1) & (ids[2] == grid[2] - 1))
            def _():
                comm.finish(c_in, c_out, c_sems)

    if form == "nn":
        a_spec = pl.BlockSpec((tm, tk), lambda i, j, k: (i, k))
        b_spec = pl.BlockSpec((tk, tn), lambda i, j, k: (k, j))
    elif form == "nt":
        a_spec = pl.BlockSpec((tm, tk), lambda i, j, k: (i, k))
        b_spec = pl.BlockSpec((tn, tk), lambda i, j, k: (j, k))
    else:
        a_spec = pl.BlockSpec((tk, tm), lambda i, j, k: (k, i))
        b_spec = pl.BlockSpec((tk, tn), lambda i, j, k: (k, j))
    o_spec = pl.BlockSpec((tm, tn), lambda i, j, k: (i, j))
    in_specs = [a_spec, b_spec] + ([o_spec] if addend is not None else [])
    args = (a, b) + ((addend,) if addend is not None else ())
    out_shape = jax.ShapeDtypeStruct((M, N), out_dtype)
    scratch = [pltpu.VMEM((tm, tn), F32)] if nk > 1 else []
    if comm is None:
        return pl.pallas_call(
            body, name=name, grid=grid, in_specs=in_specs, out_specs=o_spec, out_shape=out_shape,
            scratch_shapes=scratch, compiler_params=_cparams(("parallel", "parallel", "arbitrary")),
        )(*args)
    outs = pl.pallas_call(
        body, name=name, grid=grid, in_specs=in_specs + [ANY] * nc, out_specs=[o_spec] + [ANY] * nc,
        out_shape=[out_shape] + comm.out_shapes, scratch_shapes=scratch + comm.scratch,
        compiler_params=_cparams(("arbitrary", "arbitrary", "arbitrary")),
    )(*args, *comm.inputs)
    return outs[0], outs[1:]


def _rms_fwd(x, g, *, name, tm=512):
    M, D = x.shape
    tm = min(tm, M)

    def body(x_ref, g_ref, n_ref):
        xf = x_ref[...]
        r = lax.rsqrt(jnp.mean(xf * xf, axis=-1, keepdims=True) + EPS)
        n_ref[...] = (xf * r * g_ref[...]).astype(n_ref.dtype)

    return pl.pallas_call(
        body, name=name, grid=(M // tm,),
        in_specs=[pl.BlockSpec((tm, D), lambda i: (i, 0)), pl.BlockSpec((1, D), lambda i: (0, 0))],
        out_specs=pl.BlockSpec((tm, D), lambda i: (i, 0)),
        out_shape=jax.ShapeDtypeStruct((M, D), BF16),
        compiler_params=_cparams(("parallel",)),
    )(x, g.reshape(1, D))


def _rms_bwd(x, g, dn, dres, *, name, tm=512):
    M, D = x.shape
    tm = min(tm, M)

    def body(x_ref, g_ref, dn_ref, dres_ref, dx_ref, dg_ref):
        @pl.when(pl.program_id(0) == 0)
        def _():
            dg_ref[...] = jnp.zeros_like(dg_ref)

        xf = x_ref[...]
        r = lax.rsqrt(jnp.mean(xf * xf, axis=-1, keepdims=True) + EPS)
        xh = xf * r
        dn_ = dn_ref[...].astype(F32)
        dg_ref[...] += jnp.sum(dn_ * xh, axis=0, keepdims=True)
        dxh = dn_ * g_ref[...]
        dx = r * (dxh - xh * jnp.mean(dxh * xh, axis=-1, keepdims=True))
        dx_ref[...] = dres_ref[...] + dx

    row = pl.BlockSpec((tm, D), lambda i: (i, 0))
    vec = pl.BlockSpec((1, D), lambda i: (0, 0))
    return pl.pallas_call(
        body, name=name, grid=(M // tm,),
        in_specs=[row, vec, row, row], out_specs=[row, vec],
        out_shape=[jax.ShapeDtypeStruct((M, D), F32), jax.ShapeDtypeStruct((1, D), F32)],
        compiler_params=_cparams(("arbitrary",)),
    )(x, g.reshape(1, D), dn, dres)


def _loss_head(h, g, tgt, *, name, tm=512):
    M, D = h.shape
    tm = min(tm, M)

    def body(h_ref, g_ref, t_ref, loss_ref, dh_ref, dg_ref):
        @pl.when(pl.program_id(0) == 0)
        def _():
            dg_ref[...] = jnp.zeros_like(dg_ref)
            loss_ref[...] = jnp.zeros_like(loss_ref)

        xf = h_ref[...]
        r = lax.rsqrt(jnp.mean(xf * xf, axis=-1, keepdims=True) + EPS)
        xh = xf * r
        err = xh * g_ref[...] - t_ref[...]
        part = jnp.sum(jnp.mean(err * err, axis=-1, keepdims=True), axis=0, keepdims=True)
        loss_ref[...] += 0.5 * part
        dy = err * (1.0 / D)
        dg_ref[...] += jnp.sum(dy * xh, axis=0, keepdims=True)
        dxh = dy * g_ref[...]
        dh_ref[...] = r * (dxh - xh * jnp.mean(dxh * xh, axis=-1, keepdims=True))

    row = pl.BlockSpec((tm, D), lambda i: (i, 0))
    vec = pl.BlockSpec((1, D), lambda i: (0, 0))
    one = pl.BlockSpec((1, 1), lambda i: (0, 0))
    return pl.pallas_call(
        body, name=name, grid=(M // tm,),
        in_specs=[row, vec, row], out_specs=[one, row, vec],
        out_shape=[jax.ShapeDtypeStruct((1, 1), F32), jax.ShapeDtypeStruct((M, D), F32),
                   jax.ShapeDtypeStruct((1, D), F32)],
        compiler_params=_cparams(("arbitrary",)),
    )(h, g.reshape(1, D), tgt)


HG_MID = HG_CHUNK // 2 - 1
EXP_CAP = 80.0


def _sigmoid(x):
    return 1.0 / (1.0 + jnp.exp(-x))


def _dot(a, b, dims, precision=None):
    return lax.dot_general(a, b, dims, preferred_element_type=F32, precision=precision)


def _bdot(a, b, form):
    return _dot(a.astype(BF16), b.astype(BF16), _DIMS[form])


def _split2(x):
    hi = x.astype(BF16)
    return hi, (x - hi.astype(F32)).astype(BF16)


def _dot3(a, b, form):
    d = _DIMS[form]
    return _dot(a[0], b[0], d) + (_dot(a[0], b[1], d) + _dot(a[1], b[0], d))


def _hgrn_chunk_common(hq, hf, lbv, tril, rid):
    sq = _sigmoid(hq)
    q = hq * sq
    sg = _sigmoid(hf)
    f = lbv + (1.0 - lbv) * sg
    k = (1.0 - lbv) * (1.0 - sg)
    g = jnp.log(f)
    b = _dot(tril, g, _DIMS["nn"], precision=lax.Precision.HIGHEST)
    bref = jnp.sum(jnp.where(rid == HG_MID, b, 0.0), axis=0, keepdims=True)
    bend = jnp.sum(jnp.where(rid == HG_CHUNK - 1, b, 0.0), axis=0, keepdims=True)
    eb = jnp.exp(b)
    e1 = jnp.exp(jnp.minimum(b - bref, EXP_CAP))
    e2 = jnp.exp(jnp.minimum(bref - b, EXP_CAP))
    e3 = jnp.exp(bend - b)
    return sq, q, sg, f, k, bend, eb, e1, e2, e3


def _hgrn_fwd(proj, lb, gnorm, *, name, T=1024):
    S = proj.shape[0]
    T = min(T, S)
    nch = T // HG_CHUNK
    C = HG_CHUNK

    def body(hq_ref, hf_ref, hi_ref, hg_ref, lb_ref, gn_ref, o_ref, oa_ref, st_ref, state):
        @pl.when(pl.program_id(1) == 0)
        def _():
            state[...] = jnp.zeros_like(state)

        lbv = lb_ref[...]
        gn = gn_ref[...]
        row = lax.broadcasted_iota(jnp.int32, (C, C), 0)
        col = lax.broadcasted_iota(jnp.int32, (C, C), 1)
        causal = row >= col
        tril = causal.astype(F32)
        rid = lax.broadcasted_iota(jnp.int32, (C, HG_DK), 0)
        sls = [pl.ds(c * C, C) for c in range(nch)]
        pre = [_hgrn_chunk_common(hq_ref[sl, :], hf_ref[sl, :], lbv, tril, rid) for sl in sls]
        v_l = [hi_ref[sl, :].astype(BF16) for sl in sls]
        a_l, u_l = [], []
        for c in range(nch):
            _, q, _, _, k, _, _, e1, e2, e3 = pre[c]
            a_l.append(jnp.where(causal, _bdot(q * e1, k * e2, "nt"), 0.0))
            u_l.append(_bdot(v_l[c], k * e3, "tn"))
        o_l = [_bdot(a_l[c], v_l[c], "nn") for c in range(nch)]
        st = state[...]
        st_l = []
        for c in range(nch):
            st_l.append(st)
            st = st * jnp.exp(pre[c][5]) + u_l[c]
        state[...] = st
        for c in range(nch):
            st_ref[0, c] = st_l[c]
            o_l[c] = o_l[c] + _bdot(pre[c][1] * pre[c][6], st_l[c], "nt")
        for c in range(nch):
            o, hg = o_l[c], hg_ref[sls[c], :]
            o_ref[sls[c], :] = o
            r = lax.rsqrt(jnp.mean(o * o, axis=-1, keepdims=True) + EPS)
            oa_ref[sls[c], :] = (o * r * gn * (hg * _sigmoid(hg))).astype(oa_ref.dtype)

    def grp(gidx):
        return pl.BlockSpec((T, 128), lambda h, t: (t, gidx * 8 + h))

    return pl.pallas_call(
        body, name=name, grid=(HG_HEADS, S // T),
        in_specs=[grp(0), grp(1), grp(2), grp(3),
                  pl.BlockSpec((1, 128), lambda h, t: (0, h)), pl.BlockSpec((1, 128), lambda h, t: (0, 0))],
        out_specs=[pl.BlockSpec((T, 128), lambda h, t: (t, h)), pl.BlockSpec((T, 128), lambda h, t: (t, h)),
                   pl.BlockSpec((1, nch, HG_DV, HG_DK), lambda h, t: (h, t, 0, 0))],
        out_shape=[jax.ShapeDtypeStruct((S, HG_HEADS * HG_DV), F32), jax.ShapeDtypeStruct((S, HG_HEADS * HG_DV), BF16),
                   jax.ShapeDtypeStruct((HG_HEADS, S // C, HG_DV, HG_DK), F32)],
        scratch_shapes=[pltpu.VMEM((HG_DV, HG_DK), F32)],
        compiler_params=_cparams(("parallel", "arbitrary")),
    )(proj, proj, proj, proj, lb, gnorm)


def _hgrn_bwd(proj, lb, gnorm, o, states, doa, *, name, T=1024):
    S = proj.shape[0]
    T = min(T, S)
    nch = T // HG_CHUNK
    C = HG_CHUNK
    nT = S // T

    def body(hq_ref, hf_ref, hi_ref, hg_ref, lb_ref, gn_ref, o_ref, st_ref, doa_ref,
             dhq_ref, dhf_ref, dhi_ref, dhg_ref, dlb_ref, dgn_ref, dstate):
        @pl.when(pl.program_id(1) == 0)
        def _():
            dstate[...] = jnp.zeros_like(dstate)
            dlb_ref[...] = jnp.zeros_like(dlb_ref)
            dgn_ref[...] = jnp.zeros_like(dgn_ref)

        lbv = lb_ref[...]
        gn = gn_ref[...]
        row = lax.broadcasted_iota(jnp.int32, (C, C), 0)
        col = lax.broadcasted_iota(jnp.int32, (C, C), 1)
        causal = row >= col
        tril = causal.astype(F32)
        triu = (row <= col).astype(F32)
        rid = lax.broadcasted_iota(jnp.int32, (C, HG_DK), 0)
        rng = range(nch)
        sls = [pl.ds(c * C, C) for c in rng]
        pre = [_hgrn_chunk_common(hq_ref[sl, :], hf_ref[sl, :], lbv, tril, rid) for sl in sls]
        do2, dgn_acc = [], jnp.zeros((1, HG_DV), F32)
        for c in rng:
            hg, ov = hg_ref[sls[c], :], o_ref[sls[c], :]
            r = lax.rsqrt(jnp.mean(ov * ov, axis=-1, keepdims=True) + EPS)
            xh = ov * r
            sgg = _sigmoid(hg)
            d_oa = doa_ref[sls[c], :].astype(F32)
            dz = d_oa * (hg * sgg)
            dhg_ref[sls[c], :] = (d_oa * (xh * gn) * (sgg * (1.0 + hg * (1.0 - sgg)))).astype(dhg_ref.dtype)
            dgn_acc = dgn_acc + jnp.sum(dz * xh, axis=0, keepdims=True)
            dxh = dz * gn
            do2.append(_split2(r * (dxh - xh * jnp.mean(dxh * xh, axis=-1, keepdims=True))))
        dgn_ref[0] += dgn_acc
        qi = [pre[c][1] * pre[c][6] for c in rng]
        qp = [pre[c][1] * pre[c][7] for c in rng]
        kp = [pre[c][4] * pre[c][8] for c in rng]
        kend = [pre[c][4] * pre[c][9] for c in rng]
        qi2, qp2, kp2, kend2 = ([_split2(t) for t in lst] for lst in (qi, qp, kp, kend))
        v2 = [_split2(hi_ref[sl, :]) for sl in sls]
        st0 = [st_ref[0, c] for c in rng]
        a_l = [jnp.where(causal, _dot(qp2[c][0], kp2[c][0], _DIMS["nt"]), 0.0).astype(BF16) for c in rng]
        da2 = [_split2(jnp.where(causal, _dot3(do2[c], v2[c], "nt"), 0.0)) for c in rng]
        dqi = [_dot3(do2[c], _split2(st0[c]), "nn") for c in rng]
        w_l = [_dot3(do2[c], qi2[c], "tn") for c in rng]
        ds = dstate[...]
        ds1 = [None] * nch
        for c in reversed(rng):
            ds1[c] = ds
            ds = ds * jnp.exp(pre[c][5]) + w_l[c]
        dstate[...] = ds
        ds12 = [_split2(t) for t in ds1]
        dqp = [_dot3(da2[c], kp2[c], "nn") for c in rng]
        dkp = [_dot3(da2[c], qp2[c], "tn") for c in rng]
        dv = [_dot(a_l[c], do2[c][0], _DIMS["tn"]) + _dot(kend2[c][0], ds12[c][0], _DIMS["nt"]) for c in rng]
        dkend = [_dot3(v2[c], ds12[c], "nn") for c in rng]
        dq_l, dk_l, db_l = [], [], []
        for c in rng:
            _, _, _, _, _, bend, eb, e1, e2, e3 = pre[c]
            dq_l.append(dqi[c] * eb + dqp[c] * e1)
            dk_l.append(dkp[c] * e2 + dkend[c] * e3)
            db = dqi[c] * qi[c] + dqp[c] * qp[c] - dkp[c] * kp[c] - dkend[c] * kend[c]
            dbend = (jnp.sum(dkend[c] * kend[c], axis=0, keepdims=True)
                     + jnp.exp(bend) * jnp.sum(ds1[c] * st0[c], axis=0, keepdims=True))
            db_l.append(db + jnp.where(rid == C - 1, dbend, 0.0))
        dg = [_dot(triu, db_l[c], _DIMS["nn"], precision=lax.Precision.HIGHEST) for c in rng]
        dlb_acc = jnp.zeros((1, HG_DK), F32)
        for c in rng:
            sq, _, sg, f, _, _, _, _, _, _ = pre[c]
            hq = hq_ref[sls[c], :]
            df = dg[c] / f - dk_l[c]
            dlb_acc = dlb_acc + jnp.sum(df * (1.0 - sg), axis=0, keepdims=True)
            dhf_ref[sls[c], :] = (df * (1.0 - lbv) * sg * (1.0 - sg)).astype(dhf_ref.dtype)
            dhq_ref[sls[c], :] = (dq_l[c] * (sq * (1.0 + hq * (1.0 - sq)))).astype(dhq_ref.dtype)
            dhi_ref[sls[c], :] = dv[c].astype(dhi_ref.dtype)
        dlb_ref[...] += dlb_acc

    def grp(gidx):
        return pl.BlockSpec((T, 128), lambda h, t: (nT - 1 - t, gidx * 8 + h))

    tok = pl.BlockSpec((T, 128), lambda h, t: (nT - 1 - t, h))
    big = jax.ShapeDtypeStruct((S, HG_HEADS * HG_DV), BF16)
    return pl.pallas_call(
        body, name=name, grid=(HG_HEADS, nT),
        in_specs=[grp(0), grp(1), grp(2), grp(3),
                  pl.BlockSpec((1, 128), lambda h, t: (0, h)), pl.BlockSpec((1, 128), lambda h, t: (0, 0)),
                  tok, pl.BlockSpec((1, nch, HG_DV, HG_DK), lambda h, t: (h, nT - 1 - t, 0, 0)), tok],
        out_specs=[tok, tok, tok, tok, pl.BlockSpec((1, 128), lambda h, t: (0, h)),
                   pl.BlockSpec((1, 1, 128), lambda h, t: (h, 0, 0))],
        out_shape=[big, big, big, big, jax.ShapeDtypeStruct((1, HG_HEADS * HG_DK), F32),
                   jax.ShapeDtypeStruct((HG_HEADS, 1, HG_DV), F32)],
        scratch_shapes=[pltpu.VMEM((HG_DV, HG_DK), F32)],
        compiler_params=_cparams(("parallel", "arbitrary")),
    )(proj, proj, proj, proj, lb, gnorm, o, states, doa)


def _lb_fwd(logits, *, name):
    def body(l_ref, lb_ref):
        lb_ref[...] = _sigmoid(l_ref[0:1, :] - l_ref[1:2, :])

    return pl.pallas_call(body, name=name, out_shape=jax.ShapeDtypeStruct((1, logits.shape[1]), F32))(logits)


def _lb_bwd(logits, dlb, *, name):
    def body(l_ref, d_ref, o_ref):
        lbv = _sigmoid(l_ref[0:1, :] - l_ref[1:2, :])
        t = d_ref[...] * lbv * (1.0 - lbv)
        o_ref[0:1, :] = t
        o_ref[1:2, :] = -t

    return pl.pallas_call(body, name=name, out_shape=jax.ShapeDtypeStruct(logits.shape, F32))(logits, dlb)


NEG = -1e30
FOX_SCALE = FOX_DH ** -0.5
FOX_PAIRS = FOX_HEADS // 2


def _fox_gate_fwd(ff, bias, *, name, T=512):
    S = ff.shape[0]
    T = min(T, S)

    def body(ff_ref, b_ref, c_ref, carry):
        @pl.when(pl.program_id(0) == 0)
        def _():
            carry[...] = jnp.zeros_like(carry)

        z = ff_ref[...] + b_ref[...]
        logf = jnp.minimum(z, 0.0) - jnp.log(1.0 + jnp.exp(-jnp.abs(z)))
        row = lax.broadcasted_iota(jnp.int32, (T, T), 0)
        col = lax.broadcasted_iota(jnp.int32, (T, T), 1)
        c = _dot((row >= col).astype(F32), logf, _DIMS["nn"], precision=lax.Precision.HIGHEST) + carry[...]
        c_ref[...] = c
        carry[...] = c[T - 1:T, :]

    return pl.pallas_call(
        body, name=name, grid=(S // T,),
        in_specs=[pl.BlockSpec((T, 128), lambda i: (i, 0)), pl.BlockSpec((1, 128), lambda i: (0, 0))],
        out_specs=pl.BlockSpec((T, 128), lambda i: (i, 0)),
        out_shape=jax.ShapeDtypeStruct((S, 128), F32),
        scratch_shapes=[pltpu.VMEM((1, 128), F32)],
        compiler_params=_cparams(("arbitrary",)),
    )(ff, bias)


def _fox_gate_bwd(ff, bias, dcs, *, name, T=512):
    S = ff.shape[0]
    T = min(T, S)
    nT = S // T

    def body(ff_ref, b_ref, d_ref, dff_ref, db_ref, carry):
        @pl.when(pl.program_id(0) == 0)
        def _():
            carry[...] = jnp.zeros_like(carry)
            db_ref[...] = jnp.zeros_like(db_ref)

        row = lax.broadcasted_iota(jnp.int32, (T, T), 0)
        col = lax.broadcasted_iota(jnp.int32, (T, T), 1)
        dlogf = carry[...] - _dot((row <= col).astype(F32), d_ref[...], _DIMS["nn"], precision=lax.Precision.HIGHEST)
        carry[...] = dlogf[0:1, :]
        dff = dlogf * (1.0 - _sigmoid(ff_ref[...] + b_ref[...]))
        dff_ref[...] = dff.astype(dff_ref.dtype)
        db_ref[...] += jnp.sum(dff, axis=0, keepdims=True)

    rev = pl.BlockSpec((T, 128), lambda i: (nT - 1 - i, 0))
    vec = pl.BlockSpec((1, 128), lambda i: (0, 0))
    return pl.pallas_call(
        body, name=name, grid=(nT,),
        in_specs=[rev, vec, rev], out_specs=[rev, vec],
        out_shape=[jax.ShapeDtypeStruct((S, 128), BF16), jax.ShapeDtypeStruct((1, 128), F32)],
        scratch_shapes=[pltpu.VMEM((1, 128), F32)],
        compiler_params=_cparams(("arbitrary",)),
    )(ff, bias, dcs)


AUG = FOX_DH
RSUM_LANE = 6


def _bias_lane(hh):
    return AUG * (1 - hh)


def _data_lanes(lane, hh):
    return (lane < AUG) if hh == 0 else (lane >= AUG)


def _split3(x):
    a = x.astype(BF16).astype(F32)
    r = x - a
    b = r.astype(BF16).astype(F32)
    return a, b, r - b


def _lane_fill(lane, base, pieces, start):
    for i, pc in enumerate(pieces):
        base = jnp.where(lane == start + i, pc, base)
    return base


FOX_TB = 512
FOX_SKIP = 40.0
N_STAT = 4


def _fox_prep(proj, c_tok, *, name):
    S = proj.shape[0]
    T = min(FOX_TB, S)

    def body(q_ref, k_ref, v_ref, c_ref, qa_ref, ka_ref, va_ref, st_ref):
        pair = pl.program_id(0)
        lane = lax.broadcasted_iota(jnp.int32, (T, 128), 1)
        lane1 = lax.broadcasted_iota(jnp.int32, (1, 128), 1)
        c = c_ref[...]
        q, k, v = q_ref[...], k_ref[...], v_ref[...]
        for hh in range(2):
            data, b0 = _data_lanes(lane, hh), _bias_lane(hh)
            ones3 = jnp.where((lane >= b0) & (lane < b0 + 3), 1.0, 0.0)

            def max_norm(t):
                tr = jnp.where(data, t.astype(BF16).astype(F32), 0.0)
                return jnp.sqrt(jnp.max(jnp.sum(tr * tr, axis=-1, keepdims=True), axis=0, keepdims=True))

            ch = jnp.sum(jnp.where(lane == 2 * pair + hh, c, 0.0), axis=-1, keepdims=True)
            c1, c2, c3 = _split3(ch)
            aug_q = _lane_fill(lane, jnp.where((lane >= b0 + 3) & (lane < b0 + 6), 1.0, 0.0), (c1, c2, c3), b0)
            aug_k = _lane_fill(lane, ones3, (-c1, -c2, -c3), b0 + 3)
            qa_ref[hh] = jnp.where(data, q * FOX_SCALE, aug_q).astype(BF16)
            ka_ref[hh] = jnp.where(data, k, aug_k).astype(BF16)
            va_ref[hh] = jnp.where(data, v, ones3).astype(BF16)
            stats = (max_norm(q * FOX_SCALE), jnp.max(ch, axis=0, keepdims=True), max_norm(k),
                     jnp.min(ch, axis=0, keepdims=True))
            st_ref[hh, 0] = _lane_fill(lane1, jnp.zeros((1, 128), F32), stats, 0)

    def grp(g):
        return pl.BlockSpec((T, 128), lambda p, t: (t, g * 8 + p))

    hm = pl.BlockSpec((2, T, 128), lambda p, t: (p, t, 0))
    out = jax.ShapeDtypeStruct((FOX_HEADS, S, 128), BF16)
    return pl.pallas_call(
        body, name=name, grid=(FOX_PAIRS, S // T),
        in_specs=[grp(4), grp(5), grp(6), pl.BlockSpec((T, 128), lambda p, t: (t, 0))],
        out_specs=[hm, hm, hm, pl.BlockSpec((2, 1, 1, 128), lambda p, t: (p, t, 0, 0))],
        out_shape=[out, out, out, jax.ShapeDtypeStruct((FOX_HEADS, S // T, 1, 128), F32)],
        compiler_params=_cparams(("parallel", "parallel")),
    )(proj, proj, proj, c_tok)


def _fox_bound(st_ref, head, nb, qi, ki):
    qb_, kb_ = (head * nb + qi) * N_STAT, (head * nb + ki) * N_STAT
    return st_ref[qb_] * st_ref[kb_ + 2] + st_ref[qb_ + 1] - st_ref[kb_ + 3] + 0.01


def _pair_lanes(lane, a0, a1):
    return jnp.where(lane < AUG, a0, a1)


def _first_live_key(st_ref, head, nb, qi, newest, thr):
    def body(t, k0):
        k = newest - t
        return jnp.where(_fox_bound(st_ref, head, nb, qi, k) > thr, k, k0)

    return lax.fori_loop(0, newest + 1, body, newest + 1)


def _last_live_query(st_ref, lm_ref, head, nb, ki):
    def body(t, i1):
        i = ki + 1 + t
        live = _fox_bound(st_ref, head, nb, i, ki) > lm_ref[head * nb + i] - FOX_SKIP
        return jnp.where(live, i, i1)

    return lax.fori_loop(0, nb - 1 - ki, body, ki)


class _BlockStream:
    def __init__(self, hbm_refs, bufs, sems, pair, tb):
        self.hbm, self.bufs, self.sems, self.pair, self.tb = hbm_refs, bufs, sems, pair, tb

    def _copies(self, blk, slot):
        rows = pl.ds(pl.multiple_of(blk * self.tb, self.tb), self.tb)
        return [pltpu.make_async_copy(h.at[pl.ds(2 * self.pair, 2), rows, :], b.at[slot], self.sems.at[n, slot])
                for n, (h, b) in enumerate(zip(self.hbm, self.bufs))]

    def start(self, blk, slot):
        for cp in self._copies(blk, slot):
            cp.start()

    def wait(self, blk, slot):
        for cp in self._copies(blk, slot):
            cp.wait()


def _fox_fwd(qa, ka, va, bounds, *, name):
    S = qa.shape[1]
    tb = min(FOX_TB, S)
    nb = S // tb

    def body(qa_ref, ka_hbm, va_hbm, st_ref, o_ref, qb_ref, lse_ref, kbuf, vbuf, sems, m_s, acc_s, m_min):
        pair, qi = pl.program_id(0), pl.program_id(1)
        stream = _BlockStream((ka_hbm, va_hbm), (kbuf, vbuf), sems, pair, tb)

        def head_step(hh, slot, masked, paired=True):
            s = _dot(qa_ref[hh], kbuf[slot, hh], _DIMS["nt"])
            if masked:
                row = lax.broadcasted_iota(jnp.int32, (tb, tb), 0)
                col = lax.broadcasted_iota(jnp.int32, (tb, tb), 1)
                s = jnp.where(col <= row, s, NEG)
            m_old = m_s[hh]
            m_new = jnp.maximum(m_old, jnp.max(s, axis=-1, keepdims=True))
            p = jnp.exp(s - m_new)
            p_hi = p.astype(BF16)
            vv = vbuf[slot, hh]
            if paired:
                p_lo = (p - p_hi.astype(F32)).astype(BF16)
                acc_s[hh] = (jnp.exp(m_old - m_new) * acc_s[hh]
                             + _dot(p_hi, vv, _DIMS["nn"]) + _dot(p_lo, vv, _DIMS["nn"]))
            else:
                acc_s[hh] = jnp.exp(m_old - m_new) * acc_s[hh] + _dot(p_hi, vv, _DIMS["nn"])
            m_s[hh] = m_new
            m_min[hh] = jnp.min(m_new)

        @pl.when(qi == 0)
        def _():
            stream.start(qi, 0)

        @pl.when(qi > 0)
        def _():
            stream.start(qi - 1, 1)

        m_s[...] = jnp.full_like(m_s, NEG)
        acc_s[...] = jnp.zeros_like(acc_s)
        stream.wait(qi, 0)
        for hh in range(2):
            head_step(hh, 0, True)

        @pl.when(qi > 1)
        def _():
            stream.start(qi - 2, 0)

        @pl.when(qi > 0)
        def _():
            stream.wait(qi - 1, 1)
            for hh in range(2):
                head_step(hh, 1, False)

        k0 = [_first_live_key(st_ref, 2 * pair + hh, nb, qi, qi - 2, m_min[hh] - FOX_SKIP) for hh in range(2)]
        n = qi - 1 - jnp.minimum(k0[0], k0[1])

        @pl.when((qi > 1) & (n == 0))
        def _():
            stream.wait(qi - 2, 0)

        def loop(t, carry):
            k = qi - 2 - t
            slot = t % 2
            stream.wait(k, slot)

            @pl.when(t + 1 < n)
            def _():
                stream.start(k - 1, 1 - slot)

            live = [k >= k0[hh] for hh in range(2)]

            @pl.when(live[0] & live[1])
            def _():
                for hh in range(2):
                    head_step(hh, slot, False)

            for hh in range(2):
                @pl.when(live[hh] & jnp.logical_not(live[1 - hh]))
                def _():
                    head_step(hh, slot, False, paired=False)
            return carry

        lax.fori_loop(0, n, loop, 0)

        @pl.when(qi + 1 < nb)
        def _():
            stream.start(qi + 1, 0)

        lane = lax.broadcasted_iota(jnp.int32, (tb, 128), 1)
        outs = []
        for hh in range(2):
            acc = acc_s[hh]
            b0 = _bias_lane(hh)
            l = acc[:, b0:b0 + 1]
            outs.append(acc / l)
            lse = m_s[hh] + jnp.log(l)
            lse_ref[hh, 0] = jnp.broadcast_to(jnp.min(lse, axis=0, keepdims=True), (1, 128))
            qf = qa_ref[hh].astype(F32)
            cb = qf[:, b0:b0 + 1] + qf[:, b0 + 1:b0 + 2] + qf[:, b0 + 2:b0 + 3] - lse
            qb_ref[hh] = _lane_fill(lane, qf, _split3(cb), b0).astype(BF16)
        o_ref[...] = _pair_lanes(lane, outs[0], outs[1])

    qs = pl.BlockSpec((2, tb, 128), lambda p, i: (p, i, 0))
    return pl.pallas_call(
        body, name=name, grid=(FOX_PAIRS, nb),
        in_specs=[qs, ANY, ANY, SMEM],
        out_specs=[pl.BlockSpec((tb, 128), lambda p, i: (i, p)), qs,
                   pl.BlockSpec((2, 1, 1, 128), lambda p, i: (p, i, 0, 0))],
        out_shape=[jax.ShapeDtypeStruct((S, FOX_HEADS * FOX_DH), F32), jax.ShapeDtypeStruct((FOX_HEADS, S, 128), BF16),
                   jax.ShapeDtypeStruct((FOX_HEADS, nb, 1, 128), F32)],
        scratch_shapes=[pltpu.VMEM((2, 2, tb, 128), BF16), pltpu.VMEM((2, 2, tb, 128), BF16),
                        pltpu.SemaphoreType.DMA((2, 2)), pltpu.VMEM((2, tb, 1), F32), pltpu.VMEM((2, tb, 128), F32),
                        pltpu.SMEM((2,), F32)],
        compiler_params=_cparams(("arbitrary", "arbitrary")),
    )(qa, ka, va, bounds)


def _fox_bwd_prep(o, do, *, name, T=512):
    S = o.shape[0]
    T = min(T, S)

    def body(o_ref, do_ref, dob_ref):
        lane = lax.broadcasted_iota(jnp.int32, (T, 128), 1)
        d = do_ref[...].astype(F32)
        prod = d * o_ref[...]
        for hh in range(2):
            mine = _data_lanes(lane, hh)
            delta = jnp.sum(jnp.where(mine, prod, 0.0), axis=-1, keepdims=True)
            dob_ref[hh] = _lane_fill(lane, jnp.where(mine, d, 0.0), _split3(-delta), _bias_lane(hh)).astype(BF16)

    tok = pl.BlockSpec((T, 128), lambda p, t: (t, p))
    return pl.pallas_call(
        body, name=name, grid=(FOX_PAIRS, S // T),
        in_specs=[tok, tok], out_specs=pl.BlockSpec((2, T, 128), lambda p, t: (p, t, 0)),
        out_shape=jax.ShapeDtypeStruct((FOX_HEADS, S, 128), BF16),
        compiler_params=_cparams(("parallel", "parallel")),
    )(o, do)


def _fox_bwd_dq(qb, ka, va, dob, bounds, lse_min, *, name, comm=None):
    S = qb.shape[1]
    tb = min(FOX_TB, S)
    nb = S // tb
    nc = comm.n if comm is not None else 0

    def body(qb_ref, dob_ref, ka_hbm, va_hbm, st_ref, lm_ref, *rest):
        c_in, (dq_ref, dob2_ref), c_out = rest[:nc], rest[nc:nc + 2], rest[nc + 2:2 * nc + 2]
        kbuf, vbuf, sems, acc_s = rest[2 * nc + 2:2 * nc + 6]
        c_sems = rest[2 * nc + 6:]
        pair, qi = pl.program_id(0), pl.program_id(1)
        if comm is not None:
            @pl.when((pair == 0) & (qi == 0))
            def _():
                comm.start(c_in, c_out, c_sems)

        stream = _BlockStream((ka_hbm, va_hbm), (kbuf, vbuf), sems, pair, tb)
        k0 = [_first_live_key(st_ref, 2 * pair + hh, nb, qi, qi - 1, lm_ref[(2 * pair + hh) * nb + qi] - FOX_SKIP)
              for hh in range(2)]
        n = qi - jnp.minimum(k0[0], k0[1]) + 1

        @pl.when(qi == 0)
        def _():
            stream.start(qi, 0)

        acc_s[...] = jnp.zeros_like(acc_s)

        def head_step(hh, slot, k, masked):
            s = _dot(qb_ref[hh], kbuf[slot, hh], _DIMS["nt"])
            if masked:
                row = lax.broadcasted_iota(jnp.int32, (tb, tb), 0)
                col = lax.broadcasted_iota(jnp.int32, (tb, tb), 1)
                s = jnp.where(col <= row, s, NEG)
            ds = jnp.exp(s) * _dot(dob_ref[hh], vbuf[slot, hh], _DIMS["nt"])
            acc_s[hh] += _dot(ds.astype(BF16), kbuf[slot, hh], _DIMS["nn"])

        def loop(t, carry):
            k = qi - t
            slot = t % 2
            stream.wait(k, slot)

            @pl.when(t + 1 < n)
            def _():
                stream.start(k - 1, 1 - slot)

            @pl.when(t == 0)
            def _():
                for hh in range(2):
                    head_step(hh, slot, k, True)

            for hh in range(2):
                @pl.when((t > 0) & (k >= k0[hh]))
                def _():
                    head_step(hh, slot, k, False)
            return carry

        lax.fori_loop(0, n, loop, 0)

        @pl.when(qi + 1 < nb)
        def _():
            stream.start(qi + 1, 0)

        lane = lax.broadcasted_iota(jnp.int32, (tb, 128), 1)
        dq_ref[...] = (_pair_lanes(lane, acc_s[0], acc_s[1]) * FOX_SCALE).astype(dq_ref.dtype)
        for hh in range(2):
            b0 = _bias_lane(hh)
            r = acc_s[hh][:, b0:b0 + 1]
            dob2_ref[hh] = _lane_fill(lane, dob_ref[hh].astype(F32), _split3(r), b0 + RSUM_LANE).astype(BF16)
        if comm is not None:
            @pl.when((pair == FOX_PAIRS - 1) & (qi == nb - 1))
            def _():
                comm.finish(c_in, c_out, c_sems)

    qs = pl.BlockSpec((2, tb, 128), lambda p, i: (p, i, 0))
    outs = pl.pallas_call(
        body, name=name, grid=(FOX_PAIRS, nb),
        in_specs=[qs, qs, ANY, ANY, SMEM, SMEM] + [ANY] * nc,
        out_specs=[pl.BlockSpec((tb, 128), lambda p, i: (i, p)), qs] + [ANY] * nc,
        out_shape=[jax.ShapeDtypeStruct((S, FOX_HEADS * FOX_DH), BF16),
                   jax.ShapeDtypeStruct((FOX_HEADS, S, 128), BF16)] + (comm.out_shapes if comm is not None else []),
        scratch_shapes=[pltpu.VMEM((2, 2, tb, 128), BF16), pltpu.VMEM((2, 2, tb, 128), BF16),
                        pltpu.SemaphoreType.DMA((2, 2)), pltpu.VMEM((2, tb, 128), F32)]
        + (comm.scratch if comm is not None else []),
        compiler_params=_cparams(("arbitrary", "arbitrary")),
    )(qb, dob, ka, va, bounds, lse_min, *(comm.inputs if comm is not None else []))
    return (outs[0], outs[1]) if comm is None else (outs[0], outs[1], outs[2:])


def _fox_bwd_dkv(qb, ka, va, dob, bounds, lse_min, *, name):
    S = qb.shape[1]
    tb = min(FOX_TB, S)
    nb = S // tb

    def body(ka_ref, va_ref, qb_hbm, dob_hbm, st_ref, lm_ref, dk_ref, dv_ref, dcs_ref, qbuf, dbuf, sems, dk_s, dv_s):
        pair, ki = pl.program_id(0), pl.program_id(1)
        stream = _BlockStream((qb_hbm, dob_hbm), (qbuf, dbuf), sems, pair, tb)
        i1 = [_last_live_query(st_ref, lm_ref, 2 * pair + hh, nb, ki) for hh in range(2)]
        n = jnp.maximum(i1[0], i1[1]) - ki + 1

        @pl.when(ki == 0)
        def _():
            stream.start(ki, 0)

        dk_s[...] = jnp.zeros_like(dk_s)
        dv_s[...] = jnp.zeros_like(dv_s)

        def head_step(hh, slot, masked):
            st = _dot(ka_ref[hh], qbuf[slot, hh], _DIMS["nt"])
            if masked:
                row = lax.broadcasted_iota(jnp.int32, (tb, tb), 0)
                col = lax.broadcasted_iota(jnp.int32, (tb, tb), 1)
                st = jnp.where(row <= col, st, NEG)
            pt = jnp.exp(st)
            dst = pt * _dot(va_ref[hh], dbuf[slot, hh], _DIMS["nt"])
            dv_s[hh] += _dot(pt.astype(BF16), dbuf[slot, hh], _DIMS["nn"])
            dk_s[hh] += _dot(dst.astype(BF16), qbuf[slot, hh], _DIMS["nn"])

        def loop(t, carry):
            i = ki + t
            slot = t % 2
            stream.wait(i, slot)

            @pl.when(t + 1 < n)
            def _():
                stream.start(i + 1, 1 - slot)

            @pl.when(t == 0)
            def _():
                for hh in range(2):
                    head_step(hh, slot, True)

            for hh in range(2):
                @pl.when((t > 0) & (i <= i1[hh]))
                def _():
                    head_step(hh, slot, False)
            return carry

        lax.fori_loop(0, n, loop, 0)

        @pl.when(ki + 1 < nb)
        def _():
            stream.start(ki + 1, 0)

        lane = lax.broadcasted_iota(jnp.int32, (tb, 128), 1)
        dk_ref[...] = _pair_lanes(lane, dk_s[0], dk_s[1]).astype(dk_ref.dtype)
        dv_ref[...] = _pair_lanes(lane, dv_s[0], dv_s[1]).astype(dv_ref.dtype)
        for hh in range(2):
            b0 = _bias_lane(hh)
            dk_a, dv_a = dk_s[hh], dv_s[hh]
            off = dv_a[:, b0 + RSUM_LANE:b0 + RSUM_LANE + 1] + dv_a[:, b0 + RSUM_LANE + 1:b0 + RSUM_LANE + 2] \
                + dv_a[:, b0 + RSUM_LANE + 2:b0 + RSUM_LANE + 3]
            dcs_ref[0, :, hh:hh + 1] = dk_a[:, b0 + 3:b0 + 4] - off

    ks = pl.BlockSpec((2, tb, 128), lambda p, j: (p, j, 0))
    tok = pl.BlockSpec((tb, 128), lambda p, j: (j, p))
    big = jax.ShapeDtypeStruct((S, FOX_HEADS * FOX_DH), BF16)
    return pl.pallas_call(
        body, name=name, grid=(FOX_PAIRS, nb),
        in_specs=[ks, ks, ANY, ANY, SMEM, SMEM],
        out_specs=[tok, tok, pl.BlockSpec((1, tb, 2), lambda p, j: (p, j, 0))],
        out_shape=[big, big, jax.ShapeDtypeStruct((FOX_PAIRS, S, 2), F32)],
        scratch_shapes=[pltpu.VMEM((2, 2, tb, 128), BF16), pltpu.VMEM((2, 2, tb, 128), BF16),
                        pltpu.SemaphoreType.DMA((2, 2)), pltpu.VMEM((2, tb, 128), F32), pltpu.VMEM((2, tb, 128), F32)],
        compiler_params=_cparams(("arbitrary", "arbitrary")),
    )(ka, va, qb, dob, bounds, lse_min)


def _merge_fwd(proj, pa, pb, *, name, T=512):
    S, D = pa.shape
    T = min(T, S)

    def body(ga_ref, gb_ref, pa_ref, pb_ref, m_ref):
        m_ref[...] = (_sigmoid(ga_ref[...]) * pa_ref[...] + _sigmoid(gb_ref[...]) * pb_ref[...]).astype(m_ref.dtype)

    tok = pl.BlockSpec((T, D), lambda i: (i, 0))
    return pl.pallas_call(
        body, name=name, grid=(S // T,),
        in_specs=[pl.BlockSpec((T, D), lambda i: (i, 7)), pl.BlockSpec((T, D), lambda i: (i, 8)), tok, tok],
        out_specs=tok, out_shape=jax.ShapeDtypeStruct((S, D), BF16),
        compiler_params=_cparams(("parallel",)),
    )(proj, proj, pa, pb)


def _merge_bwd(proj, pa, pb, dm, *, name, T=512):
    S, D = pa.shape
    T = min(T, S)

    def body(ga_ref, gb_ref, pa_ref, pb_ref, dm_ref, dpa_ref, dpb_ref, dga_ref, dgb_ref):
        dm_ = dm_ref[...]
        sa, sb = _sigmoid(ga_ref[...]), _sigmoid(gb_ref[...])
        dpa_ref[...] = (dm_ * sa).astype(BF16)
        dpb_ref[...] = (dm_ * sb).astype(BF16)
        dga_ref[...] = (dm_ * pa_ref[...] * sa * (1.0 - sa)).astype(BF16)
        dgb_ref[...] = (dm_ * pb_ref[...] * sb * (1.0 - sb)).astype(BF16)

    tok = pl.BlockSpec((T, D), lambda i: (i, 0))
    big = jax.ShapeDtypeStruct((S, D), BF16)
    return pl.pallas_call(
        body, name=name, grid=(S // T,),
        in_specs=[pl.BlockSpec((T, D), lambda i: (i, 7)), pl.BlockSpec((T, D), lambda i: (i, 8)), tok, tok, tok],
        out_specs=[tok, tok, tok, tok], out_shape=[big, big, big, big],
        compiler_params=_cparams(("parallel",)),
    )(proj, proj, pa, pb, dm)


INV_SQRT2 = 0.7071067811865476
INV_SQRT2PI = 0.3989422804014327


def _shifted(u, prev, rid):
    m1 = jnp.where(rid == 0, prev[7:8, :], pltpu.roll(u, 1, 0))
    m2 = jnp.where(rid == 0, prev[6:7, :], jnp.where(rid == 1, prev[7:8, :], pltpu.roll(u, 2, 0)))
    return m1, m2


def _conv_acc(u, prev, w_ref, b_ref, rid):
    m1, m2 = _shifted(u, prev, rid)
    return b_ref[...] + w_ref[0:1, :] * m2 + w_ref[1:2, :] * m1 + w_ref[2:3, :] * u, m1, m2


def _convglu_fwd(ug, uv, wg, wv, bg, bv, *, name, T=512, tc=256):
    S, F = ug.shape
    T = min(T, S)

    def body(ug_ref, uv_ref, wg_ref, wv_ref, bg_ref, bv_ref, a_ref, pg, pv):
        @pl.when(pl.program_id(1) == 0)
        def _():
            pg[...] = jnp.zeros_like(pg)
            pv[...] = jnp.zeros_like(pv)

        rid = lax.broadcasted_iota(jnp.int32, (T, tc), 0)
        g_, v_ = ug_ref[...], uv_ref[...]
        accg, _, _ = _conv_acc(g_, pg[...], wg_ref, bg_ref, rid)
        accv, _, _ = _conv_acc(v_, pv[...], wv_ref, bv_ref, rid)
        gel = 0.5 * accg * (1.0 + lax.erf(accg * INV_SQRT2))
        a_ref[...] = (gel * accv).astype(a_ref.dtype)
        pg[...] = g_[T - 8:T, :]
        pv[...] = v_[T - 8:T, :]

    tok = pl.BlockSpec((T, tc), lambda j, t: (t, j))
    w3 = pl.BlockSpec((3, tc), lambda j, t: (0, j))
    b1 = pl.BlockSpec((1, tc), lambda j, t: (0, j))
    return pl.pallas_call(
        body, name=name, grid=(F // tc, S // T),
        in_specs=[tok, tok, w3, w3, b1, b1], out_specs=tok,
        out_shape=jax.ShapeDtypeStruct((S, F), BF16),
        scratch_shapes=[pltpu.VMEM((8, tc), F32), pltpu.VMEM((8, tc), F32)],
        compiler_params=_cparams(("parallel", "arbitrary")),
    )(ug, uv, wg, wv, bg, bv)


def _convglu_bwd(ug, uv, wg, wv, bg, bv, da, *, name, T=512, tc=256):
    S, F = ug.shape
    T = min(T, S)
    nT = S // T
    halo_blocks = T // 8

    def up_shift(d, nx, rid):
        p1 = jnp.where(rid == T - 1, nx[0:1, :], pltpu.roll(d, T - 1, 0))
        p2 = jnp.where(rid == T - 1, nx[1:2, :], jnp.where(rid == T - 2, nx[0:1, :], pltpu.roll(d, T - 2, 0)))
        return p1, p2

    def body(ug_ref, uv_ref, hg_ref, hv_ref, wg_ref, wv_ref, bg_ref, bv_ref, da_ref,
             dug_ref, duv_ref, dwg_ref, dwv_ref, dbg_ref, dbv_ref, ng, nv):
        @pl.when(pl.program_id(1) == 0)
        def _():
            ng[...] = jnp.zeros_like(ng)
            nv[...] = jnp.zeros_like(nv)
            for r in (dwg_ref, dwv_ref, dbg_ref, dbv_ref):
                r[...] = jnp.zeros_like(r)

        first_block = pl.program_id(1) == nT - 1
        rid = lax.broadcasted_iota(jnp.int32, (T, tc), 0)
        g_, v_ = ug_ref[...], uv_ref[...]
        pg = jnp.where(first_block, 0.0, hg_ref[...])
        pv = jnp.where(first_block, 0.0, hv_ref[...])
        accg, g1, g2 = _conv_acc(g_, pg, wg_ref, bg_ref, rid)
        accv, v1, v2 = _conv_acc(v_, pv, wv_ref, bv_ref, rid)
        cdf = 0.5 * (1.0 + lax.erf(accg * INV_SQRT2))
        pdf = INV_SQRT2PI * jnp.exp(-0.5 * accg * accg)
        da_ = da_ref[...].astype(F32)
        dgate = da_ * accv * (cdf + accg * pdf)
        dval = da_ * (accg * cdf)
        dbg_ref[...] += jnp.sum(dgate, axis=0, keepdims=True)
        dbv_ref[...] += jnp.sum(dval, axis=0, keepdims=True)
        for j, (sg_, sv_) in enumerate(((g2, v2), (g1, v1), (g_, v_))):
            dwg_ref[j:j + 1, :] += jnp.sum(dgate * sg_, axis=0, keepdims=True)
            dwv_ref[j:j + 1, :] += jnp.sum(dval * sv_, axis=0, keepdims=True)
        for d, w_ref, nx, out_ref in ((dgate, wg_ref, ng, dug_ref), (dval, wv_ref, nv, duv_ref)):
            p1, p2 = up_shift(d, nx[...], rid)
            out_ref[...] = (w_ref[2:3, :] * d + w_ref[1:2, :] * p1 + w_ref[0:1, :] * p2).astype(out_ref.dtype)
            nx[...] = d[0:8, :]

    tok = pl.BlockSpec((T, tc), lambda j, t: (nT - 1 - t, j))
    halo = pl.BlockSpec((8, tc), lambda j, t: (jnp.maximum((nT - 1 - t) * halo_blocks - 1, 0), j))
    w3 = pl.BlockSpec((3, tc), lambda j, t: (0, j))
    b1 = pl.BlockSpec((1, tc), lambda j, t: (0, j))
    big = jax.ShapeDtypeStruct((S, F), BF16)
    return pl.pallas_call(
        body, name=name, grid=(F // tc, nT),
        in_specs=[tok, tok, halo, halo, w3, w3, b1, b1, tok], out_specs=[tok, tok, w3, w3, b1, b1],
        out_shape=[big, big, jax.ShapeDtypeStruct((3, F), F32), jax.ShapeDtypeStruct((3, F), F32),
                   jax.ShapeDtypeStruct((1, F), F32), jax.ShapeDtypeStruct((1, F), F32)],
        scratch_shapes=[pltpu.VMEM((8, tc), F32), pltpu.VMEM((8, tc), F32)],
        compiler_params=_cparams(("parallel", "arbitrary")),
    )(ug, uv, ug, uv, wg, wv, bg, bv, da)


FF_LO = 7168
IN_SHARD = 1154
FF_DEV, FF_OFF = FF_LO // IN_SHARD, FF_LO % IN_SHARD


def _col_blocks(a, width):
    return jnp.stack([a[:, d * width:(d + 1) * width] for d in range(N_DEV)])


def _w_in_blocks(d_wm, d_wff):
    def block(d):
        lo = d * IN_SHARD
        if d < FF_DEV:
            return d_wm[:, lo:lo + IN_SHARD]
        if d > FF_DEV:
            return d_wm[:, lo - FOX_HEADS:lo - FOX_HEADS + IN_SHARD]
        return jnp.concatenate([d_wm[:, lo:FF_LO], d_wff[:, :FOX_HEADS], d_wm[:, FF_LO:lo + IN_SHARD - FOX_HEADS]], axis=1)

    return jnp.stack([block(d) for d in range(N_DEV)])


def _late_weights(g_a, g_b, g_o, g_up, g_cw, g_d):
    wup = jnp.concatenate([g_up[d] for d in range(N_DEV)], axis=1)
    cw = jnp.concatenate([g_cw[d] for d in range(N_DEV)], axis=1)
    return dict(wa=g_a.reshape(D_MODEL, D_MODEL), wb=g_b.reshape(D_MODEL, D_MODEL), wo=g_o.reshape(D_MODEL, D_MODEL),
                wug=wup[:, :D_FF], wuv=wup[:, D_FF:], cwg=cw[:, :D_FF], cwv=cw[:, D_FF:], wd=g_d.reshape(D_FF, D_MODEL))


def _early_grad_blocks(d_wa, d_wb, d_wo, d_wug, d_wuv, d_wd):
    up = jnp.stack([d_wug[:, d * 704:(d + 1) * 704] for d in range(4)]
                   + [d_wuv[:, d * 704:(d + 1) * 704] for d in range(4)])
    return [d_wa.reshape(N_DEV, 128, D_MODEL), d_wb.reshape(N_DEV, 128, D_MODEL), d_wo.reshape(N_DEV, 128, D_MODEL),
            up, d_wd.reshape(N_DEV, 352, D_MODEL)]


def _local_step(x, tgt, w, p, late=None, exchange=False):
    S = x.shape[0]
    mm = _matmul
    n1 = _rms_fwd(x, p["norm_mix"], name="rms1_fwd")
    if late is None:
        proj = mm(n1, w["wm"], "nn", name="proj_main")
    else:
        proj, gathered = mm(n1, w["wm"], "nn", comm=late, name="proj_main")
        w = {**w, **_late_weights(*gathered)}
    ff = mm(n1, w["wff"], "nn", name="proj_ff")
    lb = _lb_fwd(p["hg_lb_logits"], name="lb_fwd")
    gnorm = p["hg_norm"].reshape(1, HG_DV)
    o_hg, oa, states = _hgrn_fwd(proj, lb, gnorm, name="hgrn_fwd")
    bias = jnp.pad(p["fox_f_bias"].reshape(1, FOX_HEADS), ((0, 0), (0, 128 - FOX_HEADS)))
    c = _fox_gate_fwd(ff, bias, name="fox_gate_fwd")
    qa, ka, va, fox_stats = _fox_prep(proj, c, name="fox_prep")
    bounds = fox_stats[:, :, 0, :N_STAT].reshape(-1)
    ob, qb, lse_stats = _fox_fwd(qa, ka, va, bounds, name="fox_fwd")
    lse_min = lse_stats[:, :, 0, 0].reshape(-1)
    pa = mm(oa, w["wa"], "nn", name="branch_a")
    pb = mm(ob, w["wb"], "nn", name="branch_b")
    merged = _merge_fwd(proj, pa, pb, name="merge_fwd")
    h1 = mm(merged, w["wo"], "nn", addend=x, name="mix_out")
    n2 = _rms_fwd(h1, p["norm_ffn"], name="rms2_fwd")
    ug = mm(n2, w["wug"], "nn", name="up_gate")
    uv = mm(n2, w["wuv"], "nn", name="up_val")
    a = _convglu_fwd(ug, uv, w["cwg"], w["cwv"], p["cbg"], p["cbv"], name="convglu_fwd")
    h2 = mm(a, w["wd"], "nn", addend=h1, name="ffn_down")
    loss, dh2, d_norm_final = _loss_head(h2, p["norm_final"], tgt, name="loss_head")
    da = mm(dh2, w["wd"], "nt", out_dtype=BF16, name="d_act")
    d_wd = mm(a, dh2, "tn", out_dtype=BF16, name="dw_down")
    dug, duv, d_cwg, d_cwv, d_cbg, d_cbv = _convglu_bwd(
        ug, uv, w["cwg"], w["cwv"], p["cbg"], p["cbv"], da, name="convglu_bwd")
    dn2 = mm(dug, w["wug"], "nt", name="dn2_gate")
    dn2 = mm(duv, w["wuv"], "nt", addend=dn2, name="dn2_val")
    d_wug = mm(n2, dug, "tn", out_dtype=BF16, name="dw_up_gate")
    d_wuv = mm(n2, duv, "tn", out_dtype=BF16, name="dw_up_val")
    dh1, d_norm_ffn = _rms_bwd(h1, p["norm_ffn"], dn2, dh2, name="rms2_bwd")
    dmerged = mm(dh1, w["wo"], "nt", name="d_merged")
    d_wo = mm(merged, dh1, "tn", out_dtype=BF16, name="dw_out")
    dpa, dpb, dga, dgb = _merge_bwd(proj, pa, pb, dmerged, name="merge_bwd")
    doa = mm(dpa, w["wa"], "nt", name="d_oa")
    dob = mm(dpb, w["wb"], "nt", out_dtype=BF16, name="d_ob")
    d_wa = mm(oa, dpa, "tn", out_dtype=BF16, name="dw_branch_a")
    d_wb = mm(ob, dpb, "tn", out_dtype=BF16, name="dw_branch_b")
    dhq, dhf, dhi, dhg, dlb, dgn8 = _hgrn_bwd(proj, lb, gnorm, o_hg, states, doa, name="hgrn_bwd")
    d_logits = _lb_bwd(p["hg_lb_logits"], dlb, name="lb_bwd")
    dob_hm = _fox_bwd_prep(ob, dob, name="fox_bwd_prep")
    early_parts = None
    if exchange:
        comm = _ExchangeComm(_early_grad_blocks(d_wa, d_wb, d_wo, d_wug, d_wuv, d_wd))
        dq, dob2, early_parts = _fox_bwd_dq(qb, ka, va, dob_hm, bounds, lse_min, comm=comm, name="fox_bwd_dq")
    else:
        dq, dob2 = _fox_bwd_dq(qb, ka, va, dob_hm, bounds, lse_min, name="fox_bwd_dq")
    dk, dv, dcs = _fox_bwd_dkv(qb, ka, va, dob2, bounds, lse_min, name="fox_bwd_dkv")
    dcs_tok = jnp.pad(dcs.transpose(1, 0, 2).reshape(S, FOX_HEADS), ((0, 0), (0, 128 - FOX_HEADS)))
    dff, dbias = _fox_gate_bwd(ff, bias, dcs_tok, name="fox_gate_bwd")
    dproj = jnp.concatenate([dhq, dhf, dhi, dhg, dq, dk, dv, dga, dgb], axis=1)
    d_wm = mm(n1, dproj, "tn", out_dtype=BF16, name="dw_in_main")
    d_wff = mm(n1, dff, "tn", out_dtype=BF16, name="dw_in_ff")
    dn1 = mm(dff, w["wff"], "nt", name="dn1_ff")
    late_parts = None
    if exchange:
        d_cw = jnp.concatenate([d_cwg, d_cwv], axis=1)
        comm = _ExchangeComm([_w_in_blocks(d_wm, d_wff), _col_blocks(d_cw, 704)])
        dn1, late_parts = mm(dproj, w["wm"], "nt", addend=dn1, comm=comm, name="dn1_main")
    else:
        dn1 = mm(dproj, w["wm"], "nt", addend=dn1, name="dn1_main")
    dx, d_norm_mix = _rms_bwd(x, p["norm_mix"], dn1, dh1, name="rms1_bwd")
    grads = dict(
        wm=d_wm, wff=d_wff, wa=d_wa, wb=d_wb, wo=d_wo, wug=d_wug, wuv=d_wuv, cwg=d_cwg, cwv=d_cwv, wd=d_wd,
        norm_mix=d_norm_mix.reshape(-1), fox_f_bias=dbias[0, :FOX_HEADS], hg_lb_logits=d_logits,
        hg_norm=jnp.sum(dgn8, axis=0).reshape(-1), norm_ffn=d_norm_ffn.reshape(-1), cbg=d_cbg, cbv=d_cbv,
        norm_final=d_norm_final.reshape(-1), early_parts=early_parts, late_parts=late_parts)
    return loss, dx, grads


SMALL = [("norm_mix", (1, D_MODEL)), ("fox_f_bias", (1, FOX_HEADS)), ("hg_lb_logits", (2, HG_HEADS * HG_DK)),
         ("hg_norm", (1, HG_DV)), ("norm_ffn", (1, D_MODEL)), ("conv_b", (1, 2 * D_FF)), ("norm_final", (D_MODEL,))]
SMALL_ROWS = 88
SHARDED = [("w_in", (D_MODEL, 1154), 256), ("w_branch_a", (128, D_MODEL), 128), ("w_branch_b", (128, D_MODEL), 128),
           ("w_out", (128, D_MODEL), 128), ("w_up", (D_MODEL, 704), 256), ("conv_w", (3, 704), 3),
           ("w_down", (352, D_MODEL), 352)]
NAMES = ["norm_mix", "w_in", "fox_f_bias", "hg_lb_logits", "hg_norm", "w_branch_a", "w_branch_b", "w_out",
         "norm_ffn", "w_up", "conv_w", "conv_b", "w_down", "norm_final"]


def _size(shape):
    n = 1
    for s in shape:
        n *= s
    return n


def _adamw(parts, w, m, v, *, name, T):
    R, C = w.shape
    c1 = 1.0 / (1.0 - ADAM_B1 ** ADAM_STEP)
    c2 = 1.0 / (1.0 - ADAM_B2 ** ADAM_STEP)

    def body(p_ref, w_ref, m_ref, v_ref, g_ref, d_ref, nm_ref, nv_ref):
        g = p_ref[0].astype(F32)
        for s in range(1, N_DEV):
            g = g + p_ref[s].astype(F32)
        g_ref[...] = g
        nm = ADAM_B1 * m_ref[...] + (1.0 - ADAM_B1) * g
        nv = ADAM_B2 * v_ref[...] + (1.0 - ADAM_B2) * (g * g)
        nm_ref[...] = nm
        nv_ref[...] = nv
        d_ref[...] = -ADAM_LR * ((nm * c1) / (jnp.sqrt(nv * c2) + ADAM_EPS) + ADAM_WD * w_ref[...])

    blk = pl.BlockSpec((T, C), lambda i: (i, 0))
    out = jax.ShapeDtypeStruct((R, C), F32)
    return pl.pallas_call(
        body, name=name, grid=(R // T,),
        in_specs=[pl.BlockSpec((N_DEV, T, C), lambda i: (0, i, 0)), blk, blk, blk],
        out_specs=[blk, blk, blk, blk], out_shape=[out, out, out, out],
        compiler_params=_cparams(("parallel",)),
    )(parts, w, m, v)


def _pack_small(vals):
    flat = jnp.concatenate([vals[n].reshape(-1).astype(F32) for n, _ in SMALL])
    return jnp.pad(flat, (0, SMALL_ROWS * 128 - flat.shape[0])).reshape(SMALL_ROWS, 128)


def _unpack_small(buf):
    flat, out, off = buf.reshape(-1), {}, 0
    for n, shape in SMALL:
        out[n] = flat[off:off + _size(shape)].reshape(shape)
        off += _size(shape)
    return out


def kernel(x, norm_mix, w_in, fox_f_bias, hg_lb_logits, hg_norm, w_branch_a, w_branch_b, w_out, norm_ffn, w_up, conv_w, conv_b, w_down, norm_final, loss_target, m_norm_mix, m_w_in, m_fox_f_bias, m_hg_lb_logits, m_hg_norm, m_w_branch_a, m_w_branch_b, m_w_out, m_norm_ffn, m_w_up, m_conv_w, m_conv_b, m_w_down, m_norm_final, v_norm_mix, v_w_in, v_fox_f_bias, v_hg_lb_logits, v_hg_norm, v_w_branch_a, v_w_branch_b, v_w_out, v_norm_ffn, v_w_up, v_conv_w, v_conv_b, v_w_down, v_norm_final):
    wv = dict(norm_mix=norm_mix, w_in=w_in, fox_f_bias=fox_f_bias, hg_lb_logits=hg_lb_logits, hg_norm=hg_norm,
              w_branch_a=w_branch_a, w_branch_b=w_branch_b, w_out=w_out, norm_ffn=norm_ffn, w_up=w_up, conv_w=conv_w,
              conv_b=conv_b, w_down=w_down, norm_final=norm_final)
    mv = dict(norm_mix=m_norm_mix, w_in=m_w_in, fox_f_bias=m_fox_f_bias, hg_lb_logits=m_hg_lb_logits, hg_norm=m_hg_norm,
              w_branch_a=m_w_branch_a, w_branch_b=m_w_branch_b, w_out=m_w_out, norm_ffn=m_norm_ffn, w_up=m_w_up,
              conv_w=m_conv_w, conv_b=m_conv_b, w_down=m_w_down, norm_final=m_norm_final)
    vv = dict(norm_mix=v_norm_mix, w_in=v_w_in, fox_f_bias=v_fox_f_bias, hg_lb_logits=v_hg_lb_logits, hg_norm=v_hg_norm,
              w_branch_a=v_w_branch_a, w_branch_b=v_w_branch_b, w_out=v_w_out, norm_ffn=v_norm_ffn, w_up=v_w_up,
              conv_w=v_conv_w, conv_b=v_conv_b, w_down=v_w_down, norm_final=v_norm_final)

    (g_in,) = _comm_call(_GatherComm([w_in[0].astype(BF16)]), name="gather_w_in")
    w = dict(wm=jnp.concatenate([g_in[d] for d in range(FF_DEV)]
                                + [g_in[FF_DEV][:, :FF_OFF], g_in[FF_DEV][:, FF_OFF + FOX_HEADS:]]
                                + [g_in[d] for d in range(FF_DEV + 1, N_DEV)], axis=1),
             wff=jnp.pad(g_in[FF_DEV][:, FF_OFF:FF_OFF + FOX_HEADS], ((0, 0), (0, 128 - FOX_HEADS))))
    late = _GatherComm([w_branch_a[0].astype(BF16), w_branch_b[0].astype(BF16), w_out[0].astype(BF16),
                        w_up[0].astype(BF16), conv_w[0], w_down[0].astype(BF16)])
    p = dict(norm_mix=norm_mix[0], fox_f_bias=fox_f_bias[0], hg_lb_logits=hg_lb_logits, hg_norm=hg_norm[0],
             norm_ffn=norm_ffn[0], cbg=conv_b[:, :D_FF], cbv=conv_b[:, D_FF:], norm_final=norm_final)
    loss, dx, grads = _local_step(x[0], loss_target[0], w, p, late=late, exchange=True)
    loss = lax.psum(loss[0, 0], ("x", "y", "c"))

    small = _pack_small(dict(
        norm_mix=grads["norm_mix"], fox_f_bias=grads["fox_f_bias"], hg_lb_logits=grads["hg_lb_logits"],
        hg_norm=grads["hg_norm"], norm_ffn=grads["norm_ffn"], conv_b=jnp.concatenate([grads["cbg"], grads["cbv"]], axis=1),
        norm_final=grads["norm_final"]))
    (small_parts,) = _comm_call(_ExchangeComm([jnp.broadcast_to(small[None], (N_DEV, SMALL_ROWS, 128))]),
                                name="exchange_small")
    ea, eb, eo, eup, ed = grads["early_parts"]
    p_in, p_cw = grads["late_parts"]
    parts = [p_in, ea, eb, eo, eup, p_cw, ed, small_parts]
    res = {}
    for (n, shape, tile), part in zip(SHARDED, parts):
        outs = _adamw(part, wv[n].reshape(shape), mv[n].reshape(shape), vv[n].reshape(shape), name="adamw_" + n, T=tile)
        res[n] = [o.reshape(wv[n].shape) for o in outs]
    outs = _adamw(parts[-1], _pack_small(wv), _pack_small(mv), _pack_small(vv), name="adamw_small", T=SMALL_ROWS)
    small = [_unpack_small(o) for o in outs]
    for n, _ in SMALL:
        res[n] = [s[n] for s in small]
    return (loss, dx[None], *[res[n][0] for n in NAMES], *[res[n][1] for n in NAMES],
            *[res[n][2] for n in NAMES], *[res[n][3] for n in NAMES])
```

```python
import jax
import jax.numpy as jnp
from jax import lax
from jax.experimental import pallas as pl
from jax.experimental.pallas import tpu as pltpu

F32 = jnp.float32
BF16 = jnp.bfloat16

D_MODEL = 1024
HG_HEADS = 8
HG_DK = 128
HG_DV = 128
HG_CHUNK = 64
FOX_HEADS = 16
FOX_DH = 64
D_FF = 2816
EPS = 1e-6
N_DEV = 8

ADAM_LR = 0.001
ADAM_B1 = 0.9
ADAM_B2 = 0.999
ADAM_EPS = 1e-08
ADAM_WD = 0.01
ADAM_STEP = 10

VMEM_LIMIT = 56 * 1024 * 1024


def _cparams(sem):
    return pltpu.CompilerParams(dimension_semantics=sem, vmem_limit_bytes=VMEM_LIMIT)


MESH = pl.DeviceIdType.MESH
ANY = pl.BlockSpec(memory_space=pl.ANY)
SMEM = pl.BlockSpec(memory_space=pltpu.SMEM)


class _GatherComm:
    def __init__(self, shards):
        self.inputs = list(shards)
        n = self.n = len(shards)
        self.out_shapes = [jax.ShapeDtypeStruct((N_DEV,) + s.shape, s.dtype) for s in shards]
        self.scratch = [pltpu.SemaphoreType.DMA((n, 7)), pltpu.SemaphoreType.DMA((n, 7)), pltpu.SemaphoreType.DMA((n,))]

    def _parts(self, x_refs, out_refs, sems):
        send_sems, recv_sems, local_sems = sems
        x, y, c = lax.axis_index("x"), lax.axis_index("y"), lax.axis_index("c")
        me, sibling = (x, y, c), (x, y, 1 - c)
        chips = [(1 - x, y), (x, 1 - y), (1 - x, 1 - y)]

        def copy(t, k, block, to, src=None):
            slot = out_refs[t].at[4 * block[0] + 2 * block[1] + block[2]]
            return pltpu.make_async_remote_copy(
                src_ref=slot if src is None else src, dst_ref=slot,
                send_sem=send_sems.at[t, k], recv_sem=recv_sems.at[t, k], device_id=to, device_id_type=MESH)

        mine = [pltpu.make_async_copy(x_refs[t], out_refs[t].at[4 * x + 2 * y + c], local_sems.at[t])
                for t in range(self.n)]
        first = []
        for t in range(self.n):
            first.append(copy(t, 0, me, sibling, src=x_refs[t]))
            first += [copy(t, 1 + j, me, (*chip, c), src=x_refs[t]) for j, chip in enumerate(chips)]
        return c, me, sibling, chips, copy, mine, first

    def start(self, x_refs, out_refs, sems):
        _, _, _, _, _, mine, first = self._parts(x_refs, out_refs, sems)
        for cp in mine + first:
            cp.start()

    def finish(self, x_refs, out_refs, sems):
        c, me, sibling, chips, copy, mine, first = self._parts(x_refs, out_refs, sems)
        passed = []
        for j, chip in enumerate(chips):
            for t in range(self.n):
                copy(t, 1 + j, (*chip, c), me).wait_recv()
                passed.append(copy(t, 4 + j, (*chip, c), sibling))
                passed[-1].start()
        for t in range(self.n):
            copy(t, 0, sibling, me).wait_recv()
            for j, chip in enumerate(chips):
                copy(t, 4 + j, (*chip, 1 - c), me).wait_recv()
        for cp in first + passed:
            cp.wait_send()
        for cp in mine:
            cp.wait()


class _ExchangeComm:
    def __init__(self, blocks):
        self.inputs = list(blocks)
        n = self.n = len(blocks)
        self.out_shapes = [jax.ShapeDtypeStruct(b.shape, b.dtype) for b in blocks]
        self.scratch = [pltpu.SemaphoreType.DMA((n, 7)), pltpu.SemaphoreType.DMA((n, 7)), pltpu.SemaphoreType.DMA((n,))]

    def _parts(self, g_refs, out_refs, sems):
        send_sems, recv_sems, local_sems = sems
        x, y, c = lax.axis_index("x"), lax.axis_index("y"), lax.axis_index("c")
        me = 4 * x + 2 * y + c
        mine = [pltpu.make_async_copy(g_refs[t].at[me], out_refs[t].at[me], local_sems.at[t]) for t in range(self.n)]
        sends, recvs = [], []
        for k in range(1, N_DEV):
            px = 1 - x if k & 4 else x
            py = 1 - y if k & 2 else y
            pc = 1 - c if k & 1 else c
            p = 4 * px + 2 * py + pc
            for t in range(self.n):
                sends.append(pltpu.make_async_remote_copy(
                    src_ref=g_refs[t].at[p], dst_ref=out_refs[t].at[me], send_sem=send_sems.at[t, k - 1],
                    recv_sem=recv_sems.at[t, k - 1], device_id=(px, py, pc), device_id_type=MESH))
                recvs.append(pltpu.make_async_remote_copy(
                    src_ref=g_refs[t].at[p], dst_ref=out_refs[t].at[p], send_sem=send_sems.at[t, k - 1],
                    recv_sem=recv_sems.at[t, k - 1], device_id=(px, py, pc), device_id_type=MESH))
        return mine, sends, recvs

    def start(self, g_refs, out_refs, sems):
        mine, sends, _ = self._parts(g_refs, out_refs, sems)
        for cp in mine + sends:
            cp.start()

    def finish(self, g_refs, out_refs, sems):
        mine, sends, recvs = self._parts(g_refs, out_refs, sems)
        for cp in recvs:
            cp.wait_recv()
        for cp in sends:
            cp.wait_send()
        for cp in mine:
            cp.wait()


def _comm_call(comm, *, name):
    n = comm.n

    def body(*refs):
        comm.start(refs[:n], refs[n:2 * n], refs[2 * n:])
        comm.finish(refs[:n], refs[n:2 * n], refs[2 * n:])

    return pl.pallas_call(body, name=name, in_specs=[ANY] * n, out_specs=[ANY] * n, out_shape=comm.out_shapes,
                          scratch_shapes=comm.scratch)(*comm.inputs)


_DIMS = {
    "nn": (((1,), (0,)), ((), ())),
    "nt": (((1,), (1,)), ((), ())),
    "tn": (((0,), (0,)), ((), ())),
}

MATMUL_VMEM_BUDGET = 36 * 1024 * 1024
MAX_TILE = 1536


def _pick(n, prefs):
    for p in prefs:
        if n % p == 0:
            return p
    return n


def _tile_options(n):
    return [d for d in range(128, min(n, MAX_TILE) + 1, 128) if n % d == 0] or [n]


def _pick_tiles(M, N, tk, nk, sa, sb, so, has_addend, tm, tn):
    best = None
    for cm in ([tm] if tm else _tile_options(M)):
        for cn in ([tn] if tn else _tile_options(N)):
            need = 2 * (cm * tk * sa + tk * cn * sb + cm * cn * so + (cm * cn * 4 if has_addend else 0))
            need += cm * cn * 4 if nk > 1 else 0
            if need <= MATMUL_VMEM_BUDGET and (best is None or cm * cn > best[0] * best[1]
                                               or (cm * cn == best[0] * best[1] and cn > best[1])):
                best = (cm, cn)
    assert best is not None, (M, N, tk)
    return best


def _matmul(a, b, form, *, out_dtype=F32, addend=None, tm=None, tn=None, tk=None, comm=None, name):
    if form == "nn":
        (M, K), (K2, N) = a.shape, b.shape
    elif form == "nt":
        (M, K), (N, K2) = a.shape, b.shape
    else:
        (K, M), (K2, N) = a.shape, b.shape
    assert K == K2, (a.shape, b.shape, form)
    tk = tk or (K if K <= 2816 else _pick(K, (1024, 512, 256, 128)))
    nk = K // tk
    if tm is None or tn is None:
        tm, tn = _pick_tiles(M, N, tk, nk, a.dtype.itemsize, b.dtype.itemsize, jnp.dtype(out_dtype).itemsize,
                             addend is not None, tm, tn)
    assert M % tm == 0 and N % tn == 0 and K % tk == 0, (M, N, K, tm, tn, tk)
    dims = _DIMS[form]
    nc = comm.n if comm is not None else 0
    grid = (M // tm, N // tn, nk)

    def body(*refs):
        a_ref, b_ref = refs[:2]
        pos = 2
        add_ref = refs[pos] if addend is not None else None
        pos += addend is not None
        c_in, o_ref, c_out = refs[pos:pos + nc], refs[pos + nc], refs[pos + nc + 1:pos + 2 * nc + 1]
        pos += 2 * nc + 1
        acc_ref = refs[pos] if nk > 1 else None
        c_sems = refs[pos + (nk > 1):]
        if comm is not None:
            ids = [pl.program_id(d) for d in range(3)]

            @pl.when((ids[0] == 0) & (ids[1] == 0) & (ids[2] == 0))
            def _():
                comm.start(c_in, c_out, c_sems)

        def finish(r):
            if add_ref is not None:
                r = r + add_ref[...].astype(F32)
            o_ref[...] = r.astype(o_ref.dtype)

        part = lax.dot_general(a_ref[...].astype(BF16), b_ref[...].astype(BF16), dims, preferred_element_type=F32)
        if nk == 1:
            finish(part)
        else:
            k = pl.program_id(2)

            @pl.when(k == 0)
            def _():
                acc_ref[...] = part

            @pl.when(k > 0)
            def _():
                acc_ref[...] += part

            @pl.when(k == nk - 1)
            def _():
                finish(acc_ref[...])

        if comm is not None:
            @pl.when((ids[0] == grid[0] - 1) & (ids[1] == grid[1] - 1) & (ids[2] == grid[2] - 1))
            def _():
                comm.finish(c_in, c_out, c_sems)

    if form == "nn":
        a_spec = pl.BlockSpec((tm, tk), lambda i, j, k: (i, k))
        b_spec = pl.BlockSpec((tk, tn), lambda i, j, k: (k, j))
    elif form == "nt":
        a_spec = pl.BlockSpec((tm, tk), lambda i, j, k: (i, k))
        b_spec = pl.BlockSpec((tn, tk), lambda i, j, k: (j, k))
    else:
        a_spec = pl.BlockSpec((tk, tm), lambda i, j, k: (k, i))
        b_spec = pl.BlockSpec((tk, tn), lambda i, j, k: (k, j))
    o_spec = pl.BlockSpec((tm, tn), lambda i, j, k: (i, j))
    in_specs = [a_spec, b_spec] + ([o_spec] if addend is not None else [])
    args = (a, b) + ((addend,) if addend is not None else ())
    out_shape = jax.ShapeDtypeStruct((M, N), out_dtype)
    scratch = [pltpu.VMEM((tm, tn), F32)] if nk > 1 else []
    if comm is None:
        return pl.pallas_call(
            body, name=name, grid=grid, in_specs=in_specs, out_specs=o_spec, out_shape=out_shape,
            scratch_shapes=scratch, compiler_params=_cparams(("parallel", "parallel", "arbitrary")),
        )(*args)
    outs = pl.pallas_call(
        body, name=name, grid=grid, in_specs=in_specs + [ANY] * nc, out_specs=[o_spec] + [ANY] * nc,
        out_shape=[out_shape] + comm.out_shapes, scratch_shapes=scratch + comm.scratch,
        compiler_params=_cparams(("arbitrary", "arbitrary", "arbitrary")),
    )(*args, *comm.inputs)
    return outs[0], outs[1:]


def _rms_fwd(x, g, *, name, tm=512):
    M, D = x.shape
    tm = min(tm, M)

    def body(x_ref, g_ref, n_ref):
        xf = x_ref[...]
        r = lax.rsqrt(jnp.mean(xf * xf, axis=-1, keepdims=True) + EPS)
        n_ref[...] = (xf * r * g_ref[...]).astype(n_ref.dtype)

    return pl.pallas_call(
        body, name=name, grid=(M // tm,),
        in_specs=[pl.BlockSpec((tm, D), lambda i: (i, 0)), pl.BlockSpec((1, D), lambda i: (0, 0))],
        out_specs=pl.BlockSpec((tm, D), lambda i: (i, 0)),
        out_shape=jax.ShapeDtypeStruct((M, D), BF16),
        compiler_params=_cparams(("parallel",)),
    )(x, g.reshape(1, D))


def _rms_bwd(x, g, dn, dres, *, name, tm=512):
    M, D = x.shape
    tm = min(tm, M)

    def body(x_ref, g_ref, dn_ref, dres_ref, dx_ref, dg_ref):
        @pl.when(pl.program_id(0) == 0)
        def _():
            dg_ref[...] = jnp.zeros_like(dg_ref)

        xf = x_ref[...]
        r = lax.rsqrt(jnp.mean(xf * xf, axis=-1, keepdims=True) + EPS)
        xh = xf * r
        dn_ = dn_ref[...].astype(F32)
        dg_ref[...] += jnp.sum(dn_ * xh, axis=0, keepdims=True)
        dxh = dn_ * g_ref[...]
        dx = r * (dxh - xh * jnp.mean(dxh * xh, axis=-1, keepdims=True))
        dx_ref[...] = dres_ref[...] + dx

    row = pl.BlockSpec((tm, D), lambda i: (i, 0))
    vec = pl.BlockSpec((1, D), lambda i: (0, 0))
    return pl.pallas_call(
        body, name=name, grid=(M // tm,),
        in_specs=[row, vec, row, row], out_specs=[row, vec],
        out_shape=[jax.ShapeDtypeStruct((M, D), F32), jax.ShapeDtypeStruct((1, D), F32)],
        compiler_params=_cparams(("arbitrary",)),
    )(x, g.reshape(1, D), dn, dres)


def _loss_head(h, g, tgt, *, name, tm=512):
    M, D = h.shape
    tm = min(tm, M)

    def body(h_ref, g_ref, t_ref, loss_ref, dh_ref, dg_ref):
        @pl.when(pl.program_id(0) == 0)
        def _():
            dg_ref[...] = jnp.zeros_like(dg_ref)
            loss_ref[...] = jnp.zeros_like(loss_ref)

        xf = h_ref[...]
        r = lax.rsqrt(jnp.mean(xf * xf, axis=-1, keepdims=True) + EPS)
        xh = xf * r
        err = xh * g_ref[...] - t_ref[...]
        part = jnp.sum(jnp.mean(err * err, axis=-1, keepdims=True), axis=0, keepdims=True)
        loss_ref[...] += 0.5 * part
        dy = err * (1.0 / D)
        dg_ref[...] += jnp.sum(dy * xh, axis=0, keepdims=True)
        dxh = dy * g_ref[...]
        dh_ref[...] = r * (dxh - xh * jnp.mean(dxh * xh, axis=-1, keepdims=True))

    row = pl.BlockSpec((tm, D), lambda i: (i, 0))
    vec = pl.BlockSpec((1, D), lambda i: (0, 0))
    one = pl.BlockSpec((1, 1), lambda i: (0, 0))
    return pl.pallas_call(
        body, name=name, grid=(M // tm,),
        in_specs=[row, vec, row], out_specs=[one, row, vec],
        out_shape=[jax.ShapeDtypeStruct((1, 1), F32), jax.ShapeDtypeStruct((M, D), F32),
                   jax.ShapeDtypeStruct((1, D), F32)],
        compiler_params=_cparams(("arbitrary",)),
    )(h, g.reshape(1, D), tgt)


HG_MID = HG_CHUNK // 2 - 1
EXP_CAP = 80.0


def _sigmoid(x):
    return 1.0 / (1.0 + jnp.exp(-x))


def _dot(a, b, dims, precision=None):
    return lax.dot_general(a, b, dims, preferred_element_type=F32, precision=precision)


def _bdot(a, b, form):
    return _dot(a.astype(BF16), b.astype(BF16), _DIMS[form])


def _split2(x):
    hi = x.astype(BF16)
    return hi, (x - hi.astype(F32)).astype(BF16)


def _dot3(a, b, form):
    d = _DIMS[form]
    return _dot(a[0], b[0], d) + (_dot(a[0], b[1], d) + _dot(a[1], b[0], d))


def _hgrn_chunk_common(hq, hf, lbv, tril, rid):
    sq = _sigmoid(hq)
    q = hq * sq
    sg = _sigmoid(hf)
    f = lbv + (1.0 - lbv) * sg
    k = (1.0 - lbv) * (1.0 - sg)
    g = jnp.log(f)
    b = _dot(tril, g, _DIMS["nn"], precision=lax.Precision.HIGHEST)
    bref = jnp.sum(jnp.where(rid == HG_MID, b, 0.0), axis=0, keepdims=True)
    bend = jnp.sum(jnp.where(rid == HG_CHUNK - 1, b, 0.0), axis=0, keepdims=True)
    eb = jnp.exp(b)
    e1 = jnp.exp(jnp.minimum(b - bref, EXP_CAP))
    e2 = jnp.exp(jnp.minimum(bref - b, EXP_CAP))
    e3 = jnp.exp(bend - b)
    return sq, q, sg, f, k, bend, eb, e1, e2, e3


def _hgrn_fwd(proj, lb, gnorm, *, name, T=1024):
    S = proj.shape[0]
    T = min(T, S)
    nch = T // HG_CHUNK
    C = HG_CHUNK

    def body(hq_ref, hf_ref, hi_ref, hg_ref, lb_ref, gn_ref, o_ref, oa_ref, st_ref, state):
        @pl.when(pl.program_id(1) == 0)
        def _():
            state[...] = jnp.zeros_like(state)

        lbv = lb_ref[...]
        gn = gn_ref[...]
        row = lax.broadcasted_iota(jnp.int32, (C, C), 0)
        col = lax.broadcasted_iota(jnp.int32, (C, C), 1)
        causal = row >= col
        tril = causal.astype(F32)
        rid = lax.broadcasted_iota(jnp.int32, (C, HG_DK), 0)
        sls = [pl.ds(c * C, C) for c in range(nch)]
        pre = [_hgrn_chunk_common(hq_ref[sl, :], hf_ref[sl, :], lbv, tril, rid) for sl in sls]
        v_l = [hi_ref[sl, :].astype(BF16) for sl in sls]
        a_l, u_l = [], []
        for c in range(nch):
            _, q, _, _, k, _, _, e1, e2, e3 = pre[c]
            a_l.append(jnp.where(causal, _bdot(q * e1, k * e2, "nt"), 0.0))
            u_l.append(_bdot(v_l[c], k * e3, "tn"))
        o_l = [_bdot(a_l[c], v_l[c], "nn") for c in range(nch)]
        st = state[...]
        st_l = []
        for c in range(nch):
            st_l.append(st)
            st = st * jnp.exp(pre[c][5]) + u_l[c]
        state[...] = st
        for c in range(nch):
            st_ref[0, c] = st_l[c]
            o_l[c] = o_l[c] + _bdot(pre[c][1] * pre[c][6], st_l[c], "nt")
        for c in range(nch):
            o, hg = o_l[c], hg_ref[sls[c], :]
            o_ref[sls[c], :] = o
            r = lax.rsqrt(jnp.mean(o * o, axis=-1, keepdims=True) + EPS)
            oa_ref[sls[c], :] = (o * r * gn * (hg * _sigmoid(hg))).astype(oa_ref.dtype)

    def grp(gidx):
        return pl.BlockSpec((T, 128), lambda h, t: (t, gidx * 8 + h))

    return pl.pallas_call(
        body, name=name, grid=(HG_HEADS, S // T),
        in_specs=[grp(0), grp(1), grp(2), grp(3),
                  pl.BlockSpec((1, 128), lambda h, t: (0, h)), pl.BlockSpec((1, 128), lambda h, t: (0, 0))],
        out_specs=[pl.BlockSpec((T, 128), lambda h, t: (t, h)), pl.BlockSpec((T, 128), lambda h, t: (t, h)),
                   pl.BlockSpec((1, nch, HG_DV, HG_DK), lambda h, t: (h, t, 0, 0))],
        out_shape=[jax.ShapeDtypeStruct((S, HG_HEADS * HG_DV), F32), jax.ShapeDtypeStruct((S, HG_HEADS * HG_DV), BF16),
                   jax.ShapeDtypeStruct((HG_HEADS, S // C, HG_DV, HG_DK), F32)],
        scratch_shapes=[pltpu.VMEM((HG_DV, HG_DK), F32)],
        compiler_params=_cparams(("parallel", "arbitrary")),
    )(proj, proj, proj, proj, lb, gnorm)


def _hgrn_bwd(proj, lb, gnorm, o, states, doa, *, name, T=1024):
    S = proj.shape[0]
    T = min(T, S)
    nch = T // HG_CHUNK
    C = HG_CHUNK
    nT = S // T

    def body(hq_ref, hf_ref, hi_ref, hg_ref, lb_ref, gn_ref, o_ref, st_ref, doa_ref,
             dhq_ref, dhf_ref, dhi_ref, dhg_ref, dlb_ref, dgn_ref, dstate):
        @pl.when(pl.program_id(1) == 0)
        def _():
            dstate[...] = jnp.zeros_like(dstate)
            dlb_ref[...] = jnp.zeros_like(dlb_ref)
            dgn_ref[...] = jnp.zeros_like(dgn_ref)

        lbv = lb_ref[...]
        gn = gn_ref[...]
        row = lax.broadcasted_iota(jnp.int32, (C, C), 0)
        col = lax.broadcasted_iota(jnp.int32, (C, C), 1)
        causal = row >= col
        tril = causal.astype(F32)
        triu = (row <= col).astype(F32)
        rid = lax.broadcasted_iota(jnp.int32, (C, HG_DK), 0)
        rng = range(nch)
        sls = [pl.ds(c * C, C) for c in rng]
        pre = [_hgrn_chunk_common(hq_ref[sl, :], hf_ref[sl, :], lbv, tril, rid) for sl in sls]
        do2, dgn_acc = [], jnp.zeros((1, HG_DV), F32)
        for c in rng:
            hg, ov = hg_ref[sls[c], :], o_ref[sls[c], :]
            r = lax.rsqrt(jnp.mean(ov * ov, axis=-1, keepdims=True) + EPS)
            xh = ov * r
            sgg = _sigmoid(hg)
            d_oa = doa_ref[sls[c], :].astype(F32)
            dz = d_oa * (hg * sgg)
            dhg_ref[sls[c], :] = (d_oa * (xh * gn) * (sgg * (1.0 + hg * (1.0 - sgg)))).astype(dhg_ref.dtype)
            dgn_acc = dgn_acc + jnp.sum(dz * xh, axis=0, keepdims=True)
            dxh = dz * gn
            do2.append(_split2(r * (dxh - xh * jnp.mean(dxh * xh, axis=-1, keepdims=True))))
        dgn_ref[0] += dgn_acc
        qi = [pre[c][1] * pre[c][6] for c in rng]
        qp = [pre[c][1] * pre[c][7] for c in rng]
        kp = [pre[c][4] * pre[c][8] for c in rng]
        kend = [pre[c][4] * pre[c][9] for c in rng]
        qi2, qp2, kp2, kend2 = ([_split2(t) for t in lst] for lst in (qi, qp, kp, kend))
        v2 = [_split2(hi_ref[sl, :]) for sl in sls]
        st0 = [st_ref[0, c] for c in rng]
        a_l = [jnp.where(causal, _dot(qp2[c][0], kp2[c][0], _DIMS["nt"]), 0.0).astype(BF16) for c in rng]
        da2 = [_split2(jnp.where(causal, _dot3(do2[c], v2[c], "nt"), 0.0)) for c in rng]
        dqi = [_dot3(do2[c], _split2(st0[c]), "nn") for c in rng]
        w_l = [_dot3(do2[c], qi2[c], "tn") for c in rng]
        ds = dstate[...]
        ds1 = [None] * nch
        for c in reversed(rng):
            ds1[c] = ds
            ds = ds * jnp.exp(pre[c][5]) + w_l[c]
        dstate[...] = ds
        ds12 = [_split2(t) for t in ds1]
        dqp = [_dot3(da2[c], kp2[c], "nn") for c in rng]
        dkp = [_dot3(da2[c], qp2[c], "tn") for c in rng]
        dv = [_dot(a_l[c], do2[c][0], _DIMS["tn"]) + _dot(kend2[c][0], ds12[c][0], _DIMS["nt"]) for c in rng]
        dkend = [_dot3(v2[c], ds12[c], "nn") for c in rng]
        dq_l, dk_l, db_l = [], [], []
        for c in rng:
            _, _, _, _, _, bend, eb, e1, e2, e3 = pre[c]
            dq_l.append(dqi[c] * eb + dqp[c] * e1)
            dk_l.append(dkp[c] * e2 + dkend[c] * e3)
            db = dqi[c] * qi[c] + dqp[c] * qp[c] - dkp[c] * kp[c] - dkend[c] * kend[c]
            dbend = (jnp.sum(dkend[c] * kend[c], axis=0, keepdims=True)
                     + jnp.exp(bend) * jnp.sum(ds1[c] * st0[c], axis=0, keepdims=True))
            db_l.append(db + jnp.where(rid == C - 1, dbend, 0.0))
        dg = [_dot(triu, db_l[c], _DIMS["nn"], precision=lax.Precision.HIGHEST) for c in rng]
        dlb_acc = jnp.zeros((1, HG_DK), F32)
        for c in rng:
            sq, _, sg, f, _, _, _, _, _, _ = pre[c]
            hq = hq_ref[sls[c], :]
            df = dg[c] / f - dk_l[c]
            dlb_acc = dlb_acc + jnp.sum(df * (1.0 - sg), axis=0, keepdims=True)
            dhf_ref[sls[c], :] = (df * (1.0 - lbv) * sg * (1.0 - sg)).astype(dhf_ref.dtype)
            dhq_ref[sls[c], :] = (dq_l[c] * (sq * (1.0 + hq * (1.0 - sq)))).astype(dhq_ref.dtype)
            dhi_ref[sls[c], :] = dv[c].astype(dhi_ref.dtype)
        dlb_ref[...] += dlb_acc

    def grp(gidx):
        return pl.BlockSpec((T, 128), lambda h, t: (nT - 1 - t, gidx * 8 + h))

    tok = pl.BlockSpec((T, 128), lambda h, t: (nT - 1 - t, h))
    big = jax.ShapeDtypeStruct((S, HG_HEADS * HG_DV), BF16)
    return pl.pallas_call(
        body, name=name, grid=(HG_HEADS, nT),
        in_specs=[grp(0), grp(1), grp(2), grp(3),
                  pl.BlockSpec((1, 128), lambda h, t: (0, h)), pl.BlockSpec((1, 128), lambda h, t: (0, 0)),
                  tok, pl.BlockSpec((1, nch, HG_DV, HG_DK), lambda h, t: (h, nT - 1 - t, 0, 0)), tok],
        out_specs=[tok, tok, tok, tok, pl.BlockSpec((1, 128), lambda h, t: (0, h)),
                   pl.BlockSpec((1, 1, 128), lambda h, t: (h, 0, 0))],
        out_shape=[big, big, big, big, jax.ShapeDtypeStruct((1, HG_HEADS * HG_DK), F32),
                   jax.ShapeDtypeStruct((HG_HEADS, 1, HG_DV), F32)],
        scratch_shapes=[pltpu.VMEM((HG_DV, HG_DK), F32)],
        compiler_params=_cparams(("parallel", "arbitrary")),
    )(proj, proj, proj, proj, lb, gnorm, o, states, doa)


def _lb_fwd(logits, *, name):
    def body(l_ref, lb_ref):
        lb_ref[...] = _sigmoid(l_ref[0:1, :] - l_ref[1:2, :])

    return pl.pallas_call(body, name=name, out_shape=jax.ShapeDtypeStruct((1, logits.shape[1]), F32))(logits)


def _lb_bwd(logits, dlb, *, name):
    def body(l_ref, d_ref, o_ref):
        lbv = _sigmoid(l_ref[0:1, :] - l_ref[1:2, :])
        t = d_ref[...] * lbv * (1.0 - lbv)
        o_ref[0:1, :] = t
        o_ref[1:2, :] = -t

    return pl.pallas_call(body, name=name, out_shape=jax.ShapeDtypeStruct(logits.shape, F32))(logits, dlb)


NEG = -1e30
FOX_SCALE = FOX_DH ** -0.5
FOX_PAIRS = FOX_HEADS // 2


def _fox_gate_fwd(ff, bias, *, name, T=512):
    S = ff.shape[0]
    T = min(T, S)

    def body(ff_ref, b_ref, c_ref, carry):
        @pl.when(pl.program_id(0) == 0)
        def _():
            carry[...] = jnp.zeros_like(carry)

        z = ff_ref[...] + b_ref[...]
        logf = jnp.minimum(z, 0.0) - jnp.log(1.0 + jnp.exp(-jnp.abs(z)))
        row = lax.broadcasted_iota(jnp.int32, (T, T), 0)
        col = lax.broadcasted_iota(jnp.int32, (T, T), 1)
        c = _dot((row >= col).astype(F32), logf, _DIMS["nn"], precision=lax.Precision.HIGHEST) + carry[...]
        c_ref[...] = c
        carry[...] = c[T - 1:T, :]

    return pl.pallas_call(
        body, name=name, grid=(S // T,),
        in_specs=[pl.BlockSpec((T, 128), lambda i: (i, 0)), pl.BlockSpec((1, 128), lambda i: (0, 0))],
        out_specs=pl.BlockSpec((T, 128), lambda i: (i, 0)),
        out_shape=jax.ShapeDtypeStruct((S, 128), F32),
        scratch_shapes=[pltpu.VMEM((1, 128), F32)],
        compiler_params=_cparams(("arbitrary",)),
    )(ff, bias)


def _fox_gate_bwd(ff, bias, dcs, *, name, T=512):
    S = ff.shape[0]
    T = min(T, S)
    nT = S // T

    def body(ff_ref, b_ref, d_ref, dff_ref, db_ref, carry):
        @pl.when(pl.program_id(0) == 0)
        def _():
            carry[...] = jnp.zeros_like(carry)
            db_ref[...] = jnp.zeros_like(db_ref)

        row = lax.broadcasted_iota(jnp.int32, (T, T), 0)
        col = lax.broadcasted_iota(jnp.int32, (T, T), 1)
        dlogf = carry[...] - _dot((row <= col).astype(F32), d_ref[...], _DIMS["nn"], precision=lax.Precision.HIGHEST)
        carry[...] = dlogf[0:1, :]
        dff = dlogf * (1.0 - _sigmoid(ff_ref[...] + b_ref[...]))
        dff_ref[...] = dff.astype(dff_ref.dtype)
        db_ref[...] += jnp.sum(dff, axis=0, keepdims=True)

    rev = pl.BlockSpec((T, 128), lambda i: (nT - 1 - i, 0))
    vec = pl.BlockSpec((1, 128), lambda i: (0, 0))
    return pl.pallas_call(
        body, name=name, grid=(nT,),
        in_specs=[rev, vec, rev], out_specs=[rev, vec],
        out_shape=[jax.ShapeDtypeStruct((S, 128), BF16), jax.ShapeDtypeStruct((1, 128), F32)],
        scratch_shapes=[pltpu.VMEM((1, 128), F32)],
        compiler_params=_cparams(("arbitrary",)),
    )(ff, bias, dcs)


AUG = FOX_DH
RSUM_LANE = 6


def _bias_lane(hh):
    return AUG * (1 - hh)


def _data_lanes(lane, hh):
    return (lane < AUG) if hh == 0 else (lane >= AUG)


def _split3(x):
    a = x.astype(BF16).astype(F32)
    r = x - a
    b = r.astype(BF16).astype(F32)
    return a, b, r - b


def _lane_fill(lane, base, pieces, start):
    for i, pc in enumerate(pieces):
        base = jnp.where(lane == start + i, pc, base)
    return base


FOX_TB = 512
FOX_SKIP = 32.0
N_STAT = 4


def _fox_prep(proj, c_tok, *, name):
    S = proj.shape[0]
    T = min(FOX_TB, S)

    def body(q_ref, k_ref, v_ref, c_ref, qa_ref, ka_ref, va_ref, st_ref):
        pair = pl.program_id(0)
        lane = lax.broadcasted_iota(jnp.int32, (T, 128), 1)
        lane1 = lax.broadcasted_iota(jnp.int32, (1, 128), 1)
        c = c_ref[...]
        q, k, v = q_ref[...], k_ref[...], v_ref[...]
        for hh in range(2):
            data, b0 = _data_lanes(lane, hh), _bias_lane(hh)
            ones3 = jnp.where((lane >= b0) & (lane < b0 + 3), 1.0, 0.0)

            def max_norm(t):
                tr = jnp.where(data, t.astype(BF16).astype(F32), 0.0)
                return jnp.sqrt(jnp.max(jnp.sum(tr * tr, axis=-1, keepdims=True), axis=0, keepdims=True))

            ch = jnp.sum(jnp.where(lane == 2 * pair + hh, c, 0.0), axis=-1, keepdims=True)
            c1, c2, c3 = _split3(ch)
            aug_q = _lane_fill(lane, jnp.where((lane >= b0 + 3) & (lane < b0 + 6), 1.0, 0.0), (c1, c2, c3), b0)
            aug_k = _lane_fill(lane, ones3, (-c1, -c2, -c3), b0 + 3)
            qa_ref[hh] = jnp.where(data, q * FOX_SCALE, aug_q).astype(BF16)
            ka_ref[hh] = jnp.where(data, k, aug_k).astype(BF16)
            va_ref[hh] = jnp.where(data, v, ones3).astype(BF16)
            stats = (max_norm(q * FOX_SCALE), jnp.max(ch, axis=0, keepdims=True), max_norm(k),
                     jnp.min(ch, axis=0, keepdims=True))
            st_ref[hh, 0] = _lane_fill(lane1, jnp.zeros((1, 128), F32), stats, 0)

    def grp(g):
        return pl.BlockSpec((T, 128), lambda p, t: (t, g * 8 + p))

    hm = pl.BlockSpec((2, T, 128), lambda p, t: (p, t, 0))
    out = jax.ShapeDtypeStruct((FOX_HEADS, S, 128), BF16)
    return pl.pallas_call(
        body, name=name, grid=(FOX_PAIRS, S // T),
        in_specs=[grp(4), grp(5), grp(6), pl.BlockSpec((T, 128), lambda p, t: (t, 0))],
        out_specs=[hm, hm, hm, pl.BlockSpec((2, 1, 1, 128), lambda p, t: (p, t, 0, 0))],
        out_shape=[out, out, out, jax.ShapeDtypeStruct((FOX_HEADS, S // T, 1, 128), F32)],
        compiler_params=_cparams(("parallel", "parallel")),
    )(proj, proj, proj, c_tok)


def _fox_bound(st_ref, head, nb, qi, ki):
    qb_, kb_ = (head * nb + qi) * N_STAT, (head * nb + ki) * N_STAT
    return st_ref[qb_] * st_ref[kb_ + 2] + st_ref[qb_ + 1] - st_ref[kb_ + 3] + 0.01


def _pair_lanes(lane, a0, a1):
    return jnp.where(lane < AUG, a0, a1)


def _first_live_key(st_ref, head, nb, qi, newest, thr):
    def body(t, k0):
        k = newest - t
        return jnp.where(_fox_bound(st_ref, head, nb, qi, k) > thr, k, k0)

    return lax.fori_loop(0, newest + 1, body, newest + 1)


def _last_live_query(st_ref, lm_ref, head, nb, ki):
    def body(t, i1):
        i = ki + 1 + t
        live = _fox_bound(st_ref, head, nb, i, ki) > lm_ref[head * nb + i] - FOX_SKIP
        return jnp.where(live, i, i1)

    return lax.fori_loop(0, nb - 1 - ki, body, ki)


class _BlockStream:
    def __init__(self, hbm_refs, bufs, sems, pair, tb):
        self.hbm, self.bufs, self.sems, self.pair, self.tb = hbm_refs, bufs, sems, pair, tb

    def _copies(self, blk, slot):
        rows = pl.ds(pl.multiple_of(blk * self.tb, self.tb), self.tb)
        return [pltpu.make_async_copy(h.at[pl.ds(2 * self.pair, 2), rows, :], b.at[slot], self.sems.at[n, slot])
                for n, (h, b) in enumerate(zip(self.hbm, self.bufs))]

    def start(self, blk, slot):
        for cp in self._copies(blk, slot):
            cp.start()

    def wait(self, blk, slot):
        for cp in self._copies(blk, slot):
            cp.wait()


def _fox_fwd(qa, ka, va, bounds, *, name):
    S = qa.shape[1]
    tb = min(FOX_TB, S)
    nb = S // tb

    def body(qa_ref, ka_hbm, va_hbm, st_ref, o_ref, qb_ref, lse_ref, kbuf, vbuf, sems, m_s, acc_s, m_min):
        pair, qi = pl.program_id(0), pl.program_id(1)
        stream = _BlockStream((ka_hbm, va_hbm), (kbuf, vbuf), sems, pair, tb)

        def head_step(hh, slot, masked, paired=True):
            s = _dot(qa_ref[hh], kbuf[slot, hh], _DIMS["nt"])
            if masked:
                row = lax.broadcasted_iota(jnp.int32, (tb, tb), 0)
                col = lax.broadcasted_iota(jnp.int32, (tb, tb), 1)
                s = jnp.where(col <= row, s, NEG)
            m_old = m_s[hh]
            m_new = jnp.maximum(m_old, jnp.max(s, axis=-1, keepdims=True))
            p = jnp.exp(s - m_new)
            p_hi = p.astype(BF16)
            vv = vbuf[slot, hh]
            if paired:
                p_lo = (p - p_hi.astype(F32)).astype(BF16)
                acc_s[hh] = (jnp.exp(m_old - m_new) * acc_s[hh]
                             + _dot(p_hi, vv, _DIMS["nn"]) + _dot(p_lo, vv, _DIMS["nn"]))
            else:
                acc_s[hh] = jnp.exp(m_old - m_new) * acc_s[hh] + _dot(p_hi, vv, _DIMS["nn"])
            m_s[hh] = m_new
            m_min[hh] = jnp.min(m_new)

        @pl.when(qi == 0)
        def _():
            stream.start(qi, 0)

        @pl.when(qi > 0)
        def _():
            stream.start(qi - 1, 1)

        m_s[...] = jnp.full_like(m_s, NEG)
        acc_s[...] = jnp.zeros_like(acc_s)
        stream.wait(qi, 0)
        for hh in range(2):
            head_step(hh, 0, True)

        @pl.when(qi > 1)
        def _():
            stream.start(qi - 2, 0)

        @pl.when(qi > 0)
        def _():
            stream.wait(qi - 1, 1)
            for hh in range(2):
                head_step(hh, 1, False)

        k0 = [_first_live_key(st_ref, 2 * pair + hh, nb, qi, qi - 2, m_min[hh] - FOX_SKIP) for hh in range(2)]
        n = qi - 1 - jnp.minimum(k0[0], k0[1])

        @pl.when((qi > 1) & (n == 0))
        def _():
            stream.wait(qi - 2, 0)

        def loop(t, carry):
            k = qi - 2 - t
            slot = t % 2
            stream.wait(k, slot)

            @pl.when(t + 1 < n)
            def _():
                stream.start(k - 1, 1 - slot)

            live = [k >= k0[hh] for hh in range(2)]

            @pl.when(live[0] & live[1])
            def _():
                for hh in range(2):
                    head_step(hh, slot, False)

            for hh in range(2):
                @pl.when(live[hh] & jnp.logical_not(live[1 - hh]))
                def _():
                    head_step(hh, slot, False, paired=False)
            return carry

        lax.fori_loop(0, n, loop, 0)

        @pl.when(qi + 1 < nb)
        def _():
            stream.start(qi + 1, 0)

        lane = lax.broadcasted_iota(jnp.int32, (tb, 128), 1)
        outs = []
        for hh in range(2):
            acc = acc_s[hh]
            b0 = _bias_lane(hh)
            l = acc[:, b0:b0 + 1]
            outs.append(acc / l)
            lse = m_s[hh] + jnp.log(l)
            lse_ref[hh, 0] = jnp.broadcast_to(jnp.min(lse, axis=0, keepdims=True), (1, 128))
            qf = qa_ref[hh].astype(F32)
            cb = qf[:, b0:b0 + 1] + qf[:, b0 + 1:b0 + 2] + qf[:, b0 + 2:b0 + 3] - lse
            qb_ref[hh] = _lane_fill(lane, qf, _split3(cb), b0).astype(BF16)
        o_ref[...] = _pair_lanes(lane, outs[0], outs[1])

    qs = pl.BlockSpec((2, tb, 128), lambda p, i: (p, i, 0))
    return pl.pallas_call(
        body, name=name, grid=(FOX_PAIRS, nb),
        in_specs=[qs, ANY, ANY, SMEM],
        out_specs=[pl.BlockSpec((tb, 128), lambda p, i: (i, p)), qs,
                   pl.BlockSpec((2, 1, 1, 128), lambda p, i: (p, i, 0, 0))],
        out_shape=[jax.ShapeDtypeStruct((S, FOX_HEADS * FOX_DH), F32), jax.ShapeDtypeStruct((FOX_HEADS, S, 128), BF16),
                   jax.ShapeDtypeStruct((FOX_HEADS, nb, 1, 128), F32)],
        scratch_shapes=[pltpu.VMEM((2, 2, tb, 128), BF16), pltpu.VMEM((2, 2, tb, 128), BF16),
                        pltpu.SemaphoreType.DMA((2, 2)), pltpu.VMEM((2, tb, 1), F32), pltpu.VMEM((2, tb, 128), F32),
                        pltpu.SMEM((2,), F32)],
        compiler_params=_cparams(("arbitrary", "arbitrary")),
    )(qa, ka, va, bounds)


def _fox_bwd_prep(o, do, *, name, T=512):
    S = o.shape[0]
    T = min(T, S)

    def body(o_ref, do_ref, dob_ref):
        lane = lax.broadcasted_iota(jnp.int32, (T, 128), 1)
        d = do_ref[...].astype(F32)
        prod = d * o_ref[...]
        for hh in range(2):
            mine = _data_lanes(lane, hh)
            delta = jnp.sum(jnp.where(mine, prod, 0.0), axis=-1, keepdims=True)
            dob_ref[hh] = _lane_fill(lane, jnp.where(mine, d, 0.0), _split3(-delta), _bias_lane(hh)).astype(BF16)

    tok = pl.BlockSpec((T, 128), lambda p, t: (t, p))
    return pl.pallas_call(
        body, name=name, grid=(FOX_PAIRS, S // T),
        in_specs=[tok, tok], out_specs=pl.BlockSpec((2, T, 128), lambda p, t: (p, t, 0)),
        out_shape=jax.ShapeDtypeStruct((FOX_HEADS, S, 128), BF16),
        compiler_params=_cparams(("parallel", "parallel")),
    )(o, do)


def _fox_bwd_dq(qb, ka, va, dob, bounds, lse_min, *, name, comm=None):
    S = qb.shape[1]
    tb = min(FOX_TB, S)
    nb = S // tb
    nc = comm.n if comm is not None else 0

    def body(qb_ref, dob_ref, ka_hbm, va_hbm, st_ref, lm_ref, *rest):
        c_in, (dq_ref, dob2_ref), c_out = rest[:nc], rest[nc:nc + 2], rest[nc + 2:2 * nc + 2]
        kbuf, vbuf, sems, acc_s = rest[2 * nc + 2:2 * nc + 6]
        c_sems = rest[2 * nc + 6:]
        pair, qi = pl.program_id(0), pl.program_id(1)
        if comm is not None:
            @pl.when((pair == 0) & (qi == 0))
            def _():
                comm.start(c_in, c_out, c_sems)

        stream = _BlockStream((ka_hbm, va_hbm), (kbuf, vbuf), sems, pair, tb)
        k0 = [_first_live_key(st_ref, 2 * pair + hh, nb, qi, qi - 1, lm_ref[(2 * pair + hh) * nb + qi] - FOX_SKIP)
              for hh in range(2)]
        n = qi - jnp.minimum(k0[0], k0[1]) + 1

        @pl.when(qi == 0)
        def _():
            stream.start(qi, 0)

        acc_s[...] = jnp.zeros_like(acc_s)

        def head_step(hh, slot, k, masked):
            s = _dot(qb_ref[hh], kbuf[slot, hh], _DIMS["nt"])
            if masked:
                row = lax.broadcasted_iota(jnp.int32, (tb, tb), 0)
                col = lax.broadcasted_iota(jnp.int32, (tb, tb), 1)
                s = jnp.where(col <= row, s, NEG)
            ds = jnp.exp(s) * _dot(dob_ref[hh], vbuf[slot, hh], _DIMS["nt"])
            acc_s[hh] += _dot(ds.astype(BF16), kbuf[slot, hh], _DIMS["nn"])

        def loop(t, carry):
            k = qi - t
            slot = t % 2
            stream.wait(k, slot)

            @pl.when(t + 1 < n)
            def _():
                stream.start(k - 1, 1 - slot)

            @pl.when(t == 0)
            def _():
                for hh in range(2):
                    head_step(hh, slot, k, True)

            for hh in range(2):
                @pl.when((t > 0) & (k >= k0[hh]))
                def _():
                    head_step(hh, slot, k, False)
            return carry

        lax.fori_loop(0, n, loop, 0)

        @pl.when(qi + 1 < nb)
        def _():
            stream.start(qi + 1, 0)

        lane = lax.broadcasted_iota(jnp.int32, (tb, 128), 1)
        dq_ref[...] = (_pair_lanes(lane, acc_s[0], acc_s[1]) * FOX_SCALE).astype(dq_ref.dtype)
        for hh in range(2):
            b0 = _bias_lane(hh)
            r = acc_s[hh][:, b0:b0 + 1]
            dob2_ref[hh] = _lane_fill(lane, dob_ref[hh].astype(F32), _split3(r), b0 + RSUM_LANE).astype(BF16)
        if comm is not None:
            @pl.when((pair == FOX_PAIRS - 1) & (qi == nb - 1))
            def _():
                comm.finish(c_in, c_out, c_sems)

    qs = pl.BlockSpec((2, tb, 128), lambda p, i: (p, i, 0))
    outs = pl.pallas_call(
        body, name=name, grid=(FOX_PAIRS, nb),
        in_specs=[qs, qs, ANY, ANY, SMEM, SMEM] + [ANY] * nc,
        out_specs=[pl.BlockSpec((tb, 128), lambda p, i: (i, p)), qs] + [ANY] * nc,
        out_shape=[jax.ShapeDtypeStruct((S, FOX_HEADS * FOX_DH), BF16),
                   jax.ShapeDtypeStruct((FOX_HEADS, S, 128), BF16)] + (comm.out_shapes if comm is not None else []),
        scratch_shapes=[pltpu.VMEM((2, 2, tb, 128), BF16), pltpu.VMEM((2, 2, tb, 128), BF16),
                        pltpu.SemaphoreType.DMA((2, 2)), pltpu.VMEM((2, tb, 128), F32)]
        + (comm.scratch if comm is not None else []),
        compiler_params=_cparams(("arbitrary", "arbitrary")),
    )(qb, dob, ka, va, bounds, lse_min, *(comm.inputs if comm is not None else []))
    return (outs[0], outs[1]) if comm is None else (outs[0], outs[1], outs[2:])


def _fox_bwd_dkv(qb, ka, va, dob, bounds, lse_min, *, name):
    S = qb.shape[1]
    tb = min(FOX_TB, S)
    nb = S // tb

    def body(ka_ref, va_ref, qb_hbm, dob_hbm, st_ref, lm_ref, dk_ref, dv_ref, dcs_ref, qbuf, dbuf, sems, dk_s, dv_s):
        pair, ki = pl.program_id(0), pl.program_id(1)
        stream = _BlockStream((qb_hbm, dob_hbm), (qbuf, dbuf), sems, pair, tb)
        i1 = [_last_live_query(st_ref, lm_ref, 2 * pair + hh, nb, ki) for hh in range(2)]
        n = jnp.maximum(i1[0], i1[1]) - ki + 1

        @pl.when(ki == 0)
        def _():
            stream.start(ki, 0)

        dk_s[...] = jnp.zeros_like(dk_s)
        dv_s[...] = jnp.zeros_like(dv_s)

        def head_step(hh, slot, masked):
            st = _dot(ka_ref[hh], qbuf[slot, hh], _DIMS["nt"])
            if masked:
                row = lax.broadcasted_iota(jnp.int32, (tb, tb), 0)
                col = lax.broadcasted_iota(jnp.int32, (tb, tb), 1)
                st = jnp.where(row <= col, st, NEG)
            pt = jnp.exp(st)
            dst = pt * _dot(va_ref[hh], dbuf[slot, hh], _DIMS["nt"])
            dv_s[hh] += _dot(pt.astype(BF16), dbuf[slot, hh], _DIMS["nn"])
            dk_s[hh] += _dot(dst.astype(BF16), qbuf[slot, hh], _DIMS["nn"])

        def loop(t, carry):
            i = ki + t
            slot = t % 2
            stream.wait(i, slot)

            @pl.when(t + 1 < n)
            def _():
                stream.start(i + 1, 1 - slot)

            @pl.when(t == 0)
            def _():
                for hh in range(2):
                    head_step(hh, slot, True)

            for hh in range(2):
                @pl.when((t > 0) & (i <= i1[hh]))
                def _():
                    head_step(hh, slot, False)
            return carry

        lax.fori_loop(0, n, loop, 0)

        @pl.when(ki + 1 < nb)
        def _():
            stream.start(ki + 1, 0)

        lane = lax.broadcasted_iota(jnp.int32, (tb, 128), 1)
        dk_ref[...] = _pair_lanes(lane, dk_s[0], dk_s[1]).astype(dk_ref.dtype)
        dv_ref[...] = _pair_lanes(lane, dv_s[0], dv_s[1]).astype(dv_ref.dtype)
        for hh in range(2):
            b0 = _bias_lane(hh)
            dk_a, dv_a = dk_s[hh], dv_s[hh]
            off = dv_a[:, b0 + RSUM_LANE:b0 + RSUM_LANE + 1] + dv_a[:, b0 + RSUM_LANE + 1:b0 + RSUM_LANE + 2] \
                + dv_a[:, b0 + RSUM_LANE + 2:b0 + RSUM_LANE + 3]
            dcs_ref[0, :, hh:hh + 1] = dk_a[:, b0 + 3:b0 + 4] - off

    ks = pl.BlockSpec((2, tb, 128), lambda p, j: (p, j, 0))
    tok = pl.BlockSpec((tb, 128), lambda p, j: (j, p))
    big = jax.ShapeDtypeStruct((S, FOX_HEADS * FOX_DH), BF16)
    return pl.pallas_call(
        body, name=name, grid=(FOX_PAIRS, nb),
        in_specs=[ks, ks, ANY, ANY, SMEM, SMEM],
        out_specs=[tok, tok, pl.BlockSpec((1, tb, 2), lambda p, j: (p, j, 0))],
        out_shape=[big, big, jax.ShapeDtypeStruct((FOX_PAIRS, S, 2), F32)],
        scratch_shapes=[pltpu.VMEM((2, 2, tb, 128), BF16), pltpu.VMEM((2, 2, tb, 128), BF16),
                        pltpu.SemaphoreType.DMA((2, 2)), pltpu.VMEM((2, tb, 128), F32), pltpu.VMEM((2, tb, 128), F32)],
        compiler_params=_cparams(("arbitrary", "arbitrary")),
    )(ka, va, qb, dob, bounds, lse_min)


def _merge_fwd(proj, pa, pb, *, name, T=512):
    S, D = pa.shape
    T = min(T, S)

    def body(ga_ref, gb_ref, pa_ref, pb_ref, m_ref):
        m_ref[...] = (_sigmoid(ga_ref[...]) * pa_ref[...] + _sigmoid(gb_ref[...]) * pb_ref[...]).astype(m_ref.dtype)

    tok = pl.BlockSpec((T, D), lambda i: (i, 0))
    return pl.pallas_call(
        body, name=name, grid=(S // T,),
        in_specs=[pl.BlockSpec((T, D), lambda i: (i, 7)), pl.BlockSpec((T, D), lambda i: (i, 8)), tok, tok],
        out_specs=tok, out_shape=jax.ShapeDtypeStruct((S, D), BF16),
        compiler_params=_cparams(("parallel",)),
    )(proj, proj, pa, pb)


def _merge_bwd(proj, pa, pb, dm, *, name, T=512):
    S, D = pa.shape
    T = min(T, S)

    def body(ga_ref, gb_ref, pa_ref, pb_ref, dm_ref, dpa_ref, dpb_ref, dga_ref, dgb_ref):
        dm_ = dm_ref[...]
        sa, sb = _sigmoid(ga_ref[...]), _sigmoid(gb_ref[...])
        dpa_ref[...] = (dm_ * sa).astype(BF16)
        dpb_ref[...] = (dm_ * sb).astype(BF16)
        dga_ref[...] = (dm_ * pa_ref[...] * sa * (1.0 - sa)).astype(BF16)
        dgb_ref[...] = (dm_ * pb_ref[...] * sb * (1.0 - sb)).astype(BF16)

    tok = pl.BlockSpec((T, D), lambda i: (i, 0))
    big = jax.ShapeDtypeStruct((S, D), BF16)
    return pl.pallas_call(
        body, name=name, grid=(S // T,),
        in_specs=[pl.BlockSpec((T, D), lambda i: (i, 7)), pl.BlockSpec((T, D), lambda i: (i, 8)), tok, tok, tok],
        out_specs=[tok, tok, tok, tok], out_shape=[big, big, big, big],
        compiler_params=_cparams(("parallel",)),
    )(proj, proj, pa, pb, dm)


INV_SQRT2 = 0.7071067811865476
INV_SQRT2PI = 0.3989422804014327


def _shifted(u, prev, rid):
    m1 = jnp.where(rid == 0, prev[7:8, :], pltpu.roll(u, 1, 0))
    m2 = jnp.where(rid == 0, prev[6:7, :], jnp.where(rid == 1, prev[7:8, :], pltpu.roll(u, 2, 0)))
    return m1, m2


def _conv_acc(u, prev, w_ref, b_ref, rid):
    m1, m2 = _shifted(u, prev, rid)
    return b_ref[...] + w_ref[0:1, :] * m2 + w_ref[1:2, :] * m1 + w_ref[2:3, :] * u, m1, m2


def _convglu_fwd(ug, uv, wg, wv, bg, bv, *, name, T=512, tc=256):
    S, F = ug.shape
    T = min(T, S)

    def body(ug_ref, uv_ref, wg_ref, wv_ref, bg_ref, bv_ref, a_ref, pg, pv):
        @pl.when(pl.program_id(1) == 0)
        def _():
            pg[...] = jnp.zeros_like(pg)
            pv[...] = jnp.zeros_like(pv)

        rid = lax.broadcasted_iota(jnp.int32, (T, tc), 0)
        g_, v_ = ug_ref[...], uv_ref[...]
        accg, _, _ = _conv_acc(g_, pg[...], wg_ref, bg_ref, rid)
        accv, _, _ = _conv_acc(v_, pv[...], wv_ref, bv_ref, rid)
        gel = 0.5 * accg * (1.0 + lax.erf(accg * INV_SQRT2))
        a_ref[...] = (gel * accv).astype(a_ref.dtype)
        pg[...] = g_[T - 8:T, :]
        pv[...] = v_[T - 8:T, :]

    tok = pl.BlockSpec((T, tc), lambda j, t: (t, j))
    w3 = pl.BlockSpec((3, tc), lambda j, t: (0, j))
    b1 = pl.BlockSpec((1, tc), lambda j, t: (0, j))
    return pl.pallas_call(
        body, name=name, grid=(F // tc, S // T),
        in_specs=[tok, tok, w3, w3, b1, b1], out_specs=tok,
        out_shape=jax.ShapeDtypeStruct((S, F), BF16),
        scratch_shapes=[pltpu.VMEM((8, tc), F32), pltpu.VMEM((8, tc), F32)],
        compiler_params=_cparams(("parallel", "arbitrary")),
    )(ug, uv, wg, wv, bg, bv)


def _convglu_bwd(ug, uv, wg, wv, bg, bv, da, *, name, T=512, tc=256):
    S, F = ug.shape
    T = min(T, S)
    nT = S // T
    halo_blocks = T // 8

    def up_shift(d, nx, rid):
        p1 = jnp.where(rid == T - 1, nx[0:1, :], pltpu.roll(d, T - 1, 0))
        p2 = jnp.where(rid == T - 1, nx[1:2, :], jnp.where(rid == T - 2, nx[0:1, :], pltpu.roll(d, T - 2, 0)))
        return p1, p2

    def body(ug_ref, uv_ref, hg_ref, hv_ref, wg_ref, wv_ref, bg_ref, bv_ref, da_ref,
             dug_ref, duv_ref, dwg_ref, dwv_ref, dbg_ref, dbv_ref, ng, nv):
        @pl.when(pl.program_id(1) == 0)
        def _():
            ng[...] = jnp.zeros_like(ng)
            nv[...] = jnp.zeros_like(nv)
            for r in (dwg_ref, dwv_ref, dbg_ref, dbv_ref):
                r[...] = jnp.zeros_like(r)

        first_block = pl.program_id(1) == nT - 1
        rid = lax.broadcasted_iota(jnp.int32, (T, tc), 0)
        g_, v_ = ug_ref[...], uv_ref[...]
        pg = jnp.where(first_block, 0.0, hg_ref[...])
        pv = jnp.where(first_block, 0.0, hv_ref[...])
        accg, g1, g2 = _conv_acc(g_, pg, wg_ref, bg_ref, rid)
        accv, v1, v2 = _conv_acc(v_, pv, wv_ref, bv_ref, rid)
        cdf = 0.5 * (1.0 + lax.erf(accg * INV_SQRT2))
        pdf = INV_SQRT2PI * jnp.exp(-0.5 * accg * accg)
        da_ = da_ref[...].astype(F32)
        dgate = da_ * accv * (cdf + accg * pdf)
        dval = da_ * (accg * cdf)
        dbg_ref[...] += jnp.sum(dgate, axis=0, keepdims=True)
        dbv_ref[...] += jnp.sum(dval, axis=0, keepdims=True)
        for j, (sg_, sv_) in enumerate(((g2, v2), (g1, v1), (g_, v_))):
            dwg_ref[j:j + 1, :] += jnp.sum(dgate * sg_, axis=0, keepdims=True)
            dwv_ref[j:j + 1, :] += jnp.sum(dval * sv_, axis=0, keepdims=True)
        for d, w_ref, nx, out_ref in ((dgate, wg_ref, ng, dug_ref), (dval, wv_ref, nv, duv_ref)):
            p1, p2 = up_shift(d, nx[...], rid)
            out_ref[...] = (w_ref[2:3, :] * d + w_ref[1:2, :] * p1 + w_ref[0:1, :] * p2).astype(out_ref.dtype)
            nx[...] = d[0:8, :]

    tok = pl.BlockSpec((T, tc), lambda j, t: (nT - 1 - t, j))
    halo = pl.BlockSpec((8, tc), lambda j, t: (jnp.maximum((nT - 1 - t) * halo_blocks - 1, 0), j))
    w3 = pl.BlockSpec((3, tc), lambda j, t: (0, j))
    b1 = pl.BlockSpec((1, tc), lambda j, t: (0, j))
    big = jax.ShapeDtypeStruct((S, F), BF16)
    return pl.pallas_call(
        body, name=name, grid=(F // tc, nT),
        in_specs=[tok, tok, halo, halo, w3, w3, b1, b1, tok], out_specs=[tok, tok, w3, w3, b1, b1],
        out_shape=[big, big, jax.ShapeDtypeStruct((3, F), F32), jax.ShapeDtypeStruct((3, F), F32),
                   jax.ShapeDtypeStruct((1, F), F32), jax.ShapeDtypeStruct((1, F), F32)],
        scratch_shapes=[pltpu.VMEM((8, tc), F32), pltpu.VMEM((8, tc), F32)],
        compiler_params=_cparams(("parallel", "arbitrary")),
    )(ug, uv, ug, uv, wg, wv, bg, bv, da)


FF_LO = 7168
IN_SHARD = 1154
FF_DEV, FF_OFF = FF_LO // IN_SHARD, FF_LO % IN_SHARD


def _col_blocks(a, width):
    return jnp.stack([a[:, d * width:(d + 1) * width] for d in range(N_DEV)])


def _w_in_blocks(d_wm, d_wff):
    def block(d):
        lo = d * IN_SHARD
        if d < FF_DEV:
            return d_wm[:, lo:lo + IN_SHARD]
        if d > FF_DEV:
            return d_wm[:, lo - FOX_HEADS:lo - FOX_HEADS + IN_SHARD]
        return jnp.concatenate([d_wm[:, lo:FF_LO], d_wff[:, :FOX_HEADS], d_wm[:, FF_LO:lo + IN_SHARD - FOX_HEADS]], axis=1)

    return jnp.stack([block(d) for d in range(N_DEV)])


def _late_weights(g_a, g_b, g_o, g_up, g_cw, g_d):
    wup = jnp.concatenate([g_up[d] for d in range(N_DEV)], axis=1)
    cw = jnp.concatenate([g_cw[d] for d in range(N_DEV)], axis=1)
    return dict(wa=g_a.reshape(D_MODEL, D_MODEL), wb=g_b.reshape(D_MODEL, D_MODEL), wo=g_o.reshape(D_MODEL, D_MODEL),
                wug=wup[:, :D_FF], wuv=wup[:, D_FF:], cwg=cw[:, :D_FF], cwv=cw[:, D_FF:], wd=g_d.reshape(D_FF, D_MODEL))


def _early_grad_blocks(d_wa, d_wb, d_wo, d_wug, d_wuv, d_wd):
    up = jnp.stack([d_wug[:, d * 704:(d + 1) * 704] for d in range(4)]
                   + [d_wuv[:, d * 704:(d + 1) * 704] for d in range(4)])
    return [d_wa.reshape(N_DEV, 128, D_MODEL), d_wb.reshape(N_DEV, 128, D_MODEL), d_wo.reshape(N_DEV, 128, D_MODEL),
            up, d_wd.reshape(N_DEV, 352, D_MODEL)]


def _local_step(x, tgt, w, p, late=None, exchange=False):
    S = x.shape[0]
    mm = _matmul
    n1 = _rms_fwd(x, p["norm_mix"], name="rms1_fwd")
    if late is None:
        proj = mm(n1, w["wm"], "nn", name="proj_main")
    else:
        proj, gathered = mm(n1, w["wm"], "nn", comm=late, name="proj_main")
        w = {**w, **_late_weights(*gathered)}
    ff = mm(n1, w["wff"], "nn", name="proj_ff")
    lb = _lb_fwd(p["hg_lb_logits"], name="lb_fwd")
    gnorm = p["hg_norm"].reshape(1, HG_DV)
    o_hg, oa, states = _hgrn_fwd(proj, lb, gnorm, name="hgrn_fwd")
    bias = jnp.pad(p["fox_f_bias"].reshape(1, FOX_HEADS), ((0, 0), (0, 128 - FOX_HEADS)))
    c = _fox_gate_fwd(ff, bias, name="fox_gate_fwd")
    qa, ka, va, fox_stats = _fox_prep(proj, c, name="fox_prep")
    bounds = fox_stats[:, :, 0, :N_STAT].reshape(-1)
    ob, qb, lse_stats = _fox_fwd(qa, ka, va, bounds, name="fox_fwd")
    lse_min = lse_stats[:, :, 0, 0].reshape(-1)
    pa = mm(oa, w["wa"], "nn", name="branch_a")
    pb = mm(ob, w["wb"], "nn", name="branch_b")
    merged = _merge_fwd(proj, pa, pb, name="merge_fwd")
    h1 = mm(merged, w["wo"], "nn", addend=x, name="mix_out")
    n2 = _rms_fwd(h1, p["norm_ffn"], name="rms2_fwd")
    ug = mm(n2, w["wug"], "nn", name="up_gate")
    uv = mm(n2, w["wuv"], "nn", name="up_val")
    a = _convglu_fwd(ug, uv, w["cwg"], w["cwv"], p["cbg"], p["cbv"], name="convglu_fwd")
    h2 = mm(a, w["wd"], "nn", addend=h1, name="ffn_down")
    loss, dh2, d_norm_final = _loss_head(h2, p["norm_final"], tgt, name="loss_head")
    da = mm(dh2, w["wd"], "nt", out_dtype=BF16, name="d_act")
    d_wd = mm(a, dh2, "tn", out_dtype=BF16, name="dw_down")
    dug, duv, d_cwg, d_cwv, d_cbg, d_cbv = _convglu_bwd(
        ug, uv, w["cwg"], w["cwv"], p["cbg"], p["cbv"], da, name="convglu_bwd")
    dn2 = mm(dug, w["wug"], "nt", name="dn2_gate")
    dn2 = mm(duv, w["wuv"], "nt", addend=dn2, name="dn2_val")
    d_wug = mm(n2, dug, "tn", out_dtype=BF16, name="dw_up_gate")
    d_wuv = mm(n2, duv, "tn", out_dtype=BF16, name="dw_up_val")
    dh1, d_norm_ffn = _rms_bwd(h1, p["norm_ffn"], dn2, dh2, name="rms2_bwd")
    dmerged = mm(dh1, w["wo"], "nt", name="d_merged")
    d_wo = mm(merged, dh1, "tn", out_dtype=BF16, name="dw_out")
    dpa, dpb, dga, dgb = _merge_bwd(proj, pa, pb, dmerged, name="merge_bwd")
    doa = mm(dpa, w["wa"], "nt", name="d_oa")
    dob = mm(dpb, w["wb"], "nt", out_dtype=BF16, name="d_ob")
    d_wa = mm(oa, dpa, "tn", out_dtype=BF16, name="dw_branch_a")
    d_wb = mm(ob, dpb, "tn", out_dtype=BF16, name="dw_branch_b")
    dhq, dhf, dhi, dhg, dlb, dgn8 = _hgrn_bwd(proj, lb, gnorm, o_hg, states, doa, name="hgrn_bwd")
    d_logits = _lb_bwd(p["hg_lb_logits"], dlb, name="lb_bwd")
    dob_hm = _fox_bwd_prep(ob, dob, name="fox_bwd_prep")
    early_parts = None
    if exchange:
        comm = _ExchangeComm(_early_grad_blocks(d_wa, d_wb, d_wo, d_wug, d_wuv, d_wd))
        dq, dob2, early_parts = _fox_bwd_dq(qb, ka, va, dob_hm, bounds, lse_min, comm=comm, name="fox_bwd_dq")
    else:
        dq, dob2 = _fox_bwd_dq(qb, ka, va, dob_hm, bounds, lse_min, name="fox_bwd_dq")
    dk, dv, dcs = _fox_bwd_dkv(qb, ka, va, dob2, bounds, lse_min, name="fox_bwd_dkv")
    dcs_tok = jnp.pad(dcs.transpose(1, 0, 2).reshape(S, FOX_HEADS), ((0, 0), (0, 128 - FOX_HEADS)))
    dff, dbias = _fox_gate_bwd(ff, bias, dcs_tok, name="fox_gate_bwd")
    dproj = jnp.concatenate([dhq, dhf, dhi, dhg, dq, dk, dv, dga, dgb], axis=1)
    d_wm = mm(n1, dproj, "tn", out_dtype=BF16, name="dw_in_main")
    d_wff = mm(n1, dff, "tn", out_dtype=BF16, name="dw_in_ff")
    dn1 = mm(dff, w["wff"], "nt", name="dn1_ff")
    late_parts = None
    if exchange:
        d_cw = jnp.concatenate([d_cwg, d_cwv], axis=1)
        comm = _ExchangeComm([_w_in_blocks(d_wm, d_wff), _col_blocks(d_cw, 704)])
        dn1, late_parts = mm(dproj, w["wm"], "nt", addend=dn1, comm=comm, name="dn1_main")
    else:
        dn1 = mm(dproj, w["wm"], "nt", addend=dn1, name="dn1_main")
    dx, d_norm_mix = _rms_bwd(x, p["norm_mix"], dn1, dh1, name="rms1_bwd")
    grads = dict(
        wm=d_wm, wff=d_wff, wa=d_wa, wb=d_wb, wo=d_wo, wug=d_wug, wuv=d_wuv, cwg=d_cwg, cwv=d_cwv, wd=d_wd,
        norm_mix=d_norm_mix.reshape(-1), fox_f_bias=dbias[0, :FOX_HEADS], hg_lb_logits=d_logits,
        hg_norm=jnp.sum(dgn8, axis=0).reshape(-1), norm_ffn=d_norm_ffn.reshape(-1), cbg=d_cbg, cbv=d_cbv,
        norm_final=d_norm_final.reshape(-1), early_parts=early_parts, late_parts=late_parts)
    return loss, dx, grads


SMALL = [("norm_mix", (1, D_MODEL)), ("fox_f_bias", (1, FOX_HEADS)), ("hg_lb_logits", (2, HG_HEADS * HG_DK)),
         ("hg_norm", (1, HG_DV)), ("norm_ffn", (1, D_MODEL)), ("conv_b", (1, 2 * D_FF)), ("norm_final", (D_MODEL,))]
SMALL_ROWS = 88
SHARDED = [("w_in", (D_MODEL, 1154), 256), ("w_branch_a", (128, D_MODEL), 128), ("w_branch_b", (128, D_MODEL), 128),
           ("w_out", (128, D_MODEL), 128), ("w_up", (D_MODEL, 704), 256), ("conv_w", (3, 704), 3),
           ("w_down", (352, D_MODEL), 352)]
NAMES = ["norm_mix", "w_in", "fox_f_bias", "hg_lb_logits", "hg_norm", "w_branch_a", "w_branch_b", "w_out",
         "norm_ffn", "w_up", "conv_w", "conv_b", "w_down", "norm_final"]


def _size(shape):
    n = 1
    for s in shape:
        n *= s
    return n


def _adamw(parts, w, m, v, *, name, T):
    R, C = w.shape
    c1 = 1.0 / (1.0 - ADAM_B1 ** ADAM_STEP)
    c2 = 1.0 / (1.0 - ADAM_B2 ** ADAM_STEP)

    def body(p_ref, w_ref, m_ref, v_ref, g_ref, d_ref, nm_ref, nv_ref):
        g = p_ref[0].astype(F32)
        for s in range(1, N_DEV):
            g = g + p_ref[s].astype(F32)
        g_ref[...] = g
        nm = ADAM_B1 * m_ref[...] + (1.0 - ADAM_B1) * g
        nv = ADAM_B2 * v_ref[...] + (1.0 - ADAM_B2) * (g * g)
        nm_ref[...] = nm
        nv_ref[...] = nv
        d_ref[...] = -ADAM_LR * ((nm * c1) / (jnp.sqrt(nv * c2) + ADAM_EPS) + ADAM_WD * w_ref[...])

    blk = pl.BlockSpec((T, C), lambda i: (i, 0))
    out = jax.ShapeDtypeStruct((R, C), F32)
    return pl.pallas_call(
        body, name=name, grid=(R // T,),
        in_specs=[pl.BlockSpec((N_DEV, T, C), lambda i: (0, i, 0)), blk, blk, blk],
        out_specs=[blk, blk, blk, blk], out_shape=[out, out, out, out],
        compiler_params=_cparams(("parallel",)),
    )(parts, w, m, v)


def _pack_small(vals):
    flat = jnp.concatenate([vals[n].reshape(-1).astype(F32) for n, _ in SMALL])
    return jnp.pad(flat, (0, SMALL_ROWS * 128 - flat.shape[0])).reshape(SMALL_ROWS, 128)


def _unpack_small(buf):
    flat, out, off = buf.reshape(-1), {}, 0
    for n, shape in SMALL:
        out[n] = flat[off:off + _size(shape)].reshape(shape)
        off += _size(shape)
    return out


def kernel(x, norm_mix, w_in, fox_f_bias, hg_lb_logits, hg_norm, w_branch_a, w_branch_b, w_out, norm_ffn, w_up, conv_w, conv_b, w_down, norm_final, loss_target, m_norm_mix, m_w_in, m_fox_f_bias, m_hg_lb_logits, m_hg_norm, m_w_branch_a, m_w_branch_b, m_w_out, m_norm_ffn, m_w_up, m_conv_w, m_conv_b, m_w_down, m_norm_final, v_norm_mix, v_w_in, v_fox_f_bias, v_hg_lb_logits, v_hg_norm, v_w_branch_a, v_w_branch_b, v_w_out, v_norm_ffn, v_w_up, v_conv_w, v_conv_b, v_w_down, v_norm_final):
    wv = dict(norm_mix=norm_mix, w_in=w_in, fox_f_bias=fox_f_bias, hg_lb_logits=hg_lb_logits, hg_norm=hg_norm,
              w_branch_a=w_branch_a, w_branch_b=w_branch_b, w_out=w_out, norm_ffn=norm_ffn, w_up=w_up, conv_w=conv_w,
              conv_b=conv_b, w_down=w_down, norm_final=norm_final)
    mv = dict(norm_mix=m_norm_mix, w_in=m_w_in, fox_f_bias=m_fox_f_bias, hg_lb_logits=m_hg_lb_logits, hg_norm=m_hg_norm,
              w_branch_a=m_w_branch_a, w_branch_b=m_w_branch_b, w_out=m_w_out, norm_ffn=m_norm_ffn, w_up=m_w_up,
              conv_w=m_conv_w, conv_b=m_conv_b, w_down=m_w_down, norm_final=m_norm_final)
    vv = dict(norm_mix=v_norm_mix, w_in=v_w_in, fox_f_bias=v_fox_f_bias, hg_lb_logits=v_hg_lb_logits, hg_norm=v_hg_norm,
              w_branch_a=v_w_branch_a, w_branch_b=v_w_branch_b, w_out=v_w_out, norm_ffn=v_norm_ffn, w_up=v_w_up,
              conv_w=v_conv_w, conv_b=v_conv_b, w_down=v_w_down, norm_final=v_norm_final)

    (g_in,) = _comm_call(_GatherComm([w_in[0].astype(BF16)]), name="gather_w_in")
    w = dict(wm=jnp.concatenate([g_in[d] for d in range(FF_DEV)]
                                + [g_in[FF_DEV][:, :FF_OFF], g_in[FF_DEV][:, FF_OFF + FOX_HEADS:]]
                                + [g_in[d] for d in range(FF_DEV + 1, N_DEV)], axis=1),
             wff=jnp.pad(g_in[FF_DEV][:, FF_OFF:FF_OFF + FOX_HEADS], ((0, 0), (0, 128 - FOX_HEADS))))
    late = _GatherComm([w_branch_a[0].astype(BF16), w_branch_b[0].astype(BF16), w_out[0].astype(BF16),
                        w_up[0].astype(BF16), conv_w[0], w_down[0].astype(BF16)])
    p = dict(norm_mix=norm_mix[0], fox_f_bias=fox_f_bias[0], hg_lb_logits=hg_lb_logits, hg_norm=hg_norm[0],
             norm_ffn=norm_ffn[0], cbg=conv_b[:, :D_FF], cbv=conv_b[:, D_FF:], norm_final=norm_final)
    loss, dx, grads = _local_step(x[0], loss_target[0], w, p, late=late, exchange=True)
    loss = lax.psum(loss[0, 0], ("x", "y", "c"))

    small = _pack_small(dict(
        norm_mix=grads["norm_mix"], fox_f_bias=grads["fox_f_bias"], hg_lb_logits=grads["hg_lb_logits"],
        hg_norm=grads["hg_norm"], norm_ffn=grads["norm_ffn"], conv_b=jnp.concatenate([grads["cbg"], grads["cbv"]], axis=1),
        norm_final=grads["norm_final"]))
    (small_parts,) = _comm_call(_ExchangeComm([jnp.broadcast_to(small[None], (N_DEV, SMALL_ROWS, 128))]),
                                name="exchange_small")
    ea, eb, eo, eup, ed = grads["early_parts"]
    p_in, p_cw = grads["late_parts"]
    parts = [p_in, ea, eb, eo, eup, p_cw, ed, small_parts]
    res = {}
    for (n, shape, tile), part in zip(SHARDED, parts):
        outs = _adamw(part, wv[n].reshape(shape), mv[n].reshape(shape), vv[n].reshape(shape), name="adamw_" + n, T=tile)
        res[n] = [o.reshape(wv[n].shape) for o in outs]
    outs = _adamw(parts[-1], _pack_small(wv), _pack_small(mv), _pack_small(vv), name="adamw_small", T=SMALL_ROWS)
    small = [_unpack_small(o) for o in outs]
    for n, _ in SMALL:
        res[n] = [s[n] for s in small]
    return (loss, dx[None], *[res[n][0] for n in NAMES], *[res[n][1] for n in NAMES],
            *[res[n][2] for n in NAMES], *[res[n][3] for n in NAMES])
```

```python
import jax
import jax.numpy as jnp
from jax import lax
from jax.experimental import pallas as pl
from jax.experimental.pallas import tpu as pltpu

F32 = jnp.float32
BF16 = jnp.bfloat16

D_MODEL = 1024
HG_HEADS = 8
HG_DK = 128
HG_DV = 128
HG_CHUNK = 64
FOX_HEADS = 16
FOX_DH = 64
D_FF = 2816
EPS = 1e-6
N_DEV = 8

ADAM_LR = 0.001
ADAM_B1 = 0.9
ADAM_B2 = 0.999
ADAM_EPS = 1e-08
ADAM_WD = 0.01
ADAM_STEP = 10

VMEM_LIMIT = 56 * 1024 * 1024


def _cparams(sem):
    return pltpu.CompilerParams(dimension_semantics=sem, vmem_limit_bytes=VMEM_LIMIT)


MESH = pl.DeviceIdType.MESH
ANY = pl.BlockSpec(memory_space=pl.ANY)
SMEM = pl.BlockSpec(memory_space=pltpu.SMEM)


class _GatherComm:
    def __init__(self, shards):
        self.inputs = list(shards)
        n = self.n = len(shards)
        self.out_shapes = [jax.ShapeDtypeStruct((N_DEV,) + s.shape, s.dtype) for s in shards]
        self.scratch = [pltpu.SemaphoreType.DMA((n, 7)), pltpu.SemaphoreType.DMA((n, 7)), pltpu.SemaphoreType.DMA((n,))]

    def _parts(self, x_refs, out_refs, sems):
        send_sems, recv_sems, local_sems = sems
        x, y, c = lax.axis_index("x"), lax.axis_index("y"), lax.axis_index("c")
        me, sibling = (x, y, c), (x, y, 1 - c)
        chips = [(1 - x, y), (x, 1 - y), (1 - x, 1 - y)]

        def copy(t, k, block, to, src=None):
            slot = out_refs[t].at[4 * block[0] + 2 * block[1] + block[2]]
            return pltpu.make_async_remote_copy(
                src_ref=slot if src is None else src, dst_ref=slot,
                send_sem=send_sems.at[t, k], recv_sem=recv_sems.at[t, k], device_id=to, device_id_type=MESH)

        mine = [pltpu.make_async_copy(x_refs[t], out_refs[t].at[4 * x + 2 * y + c], local_sems.at[t])
                for t in range(self.n)]
        first = []
        for t in range(self.n):
            first.append(copy(t, 0, me, sibling, src=x_refs[t]))
            first += [copy(t, 1 + j, me, (*chip, c), src=x_refs[t]) for j, chip in enumerate(chips)]
        return c, me, sibling, chips, copy, mine, first

    def start(self, x_refs, out_refs, sems):
        _, _, _, _, _, mine, first = self._parts(x_refs, out_refs, sems)
        for cp in mine + first:
            cp.start()

    def finish(self, x_refs, out_refs, sems):
        c, me, sibling, chips, copy, mine, first = self._parts(x_refs, out_refs, sems)
        passed = []
        for j, chip in enumerate(chips):
            for t in range(self.n):
                copy(t, 1 + j, (*chip, c), me).wait_recv()
                passed.append(copy(t, 4 + j, (*chip, c), sibling))
                passed[-1].start()
        for t in range(self.n):
            copy(t, 0, sibling, me).wait_recv()
            for j, chip in enumerate(chips):
                copy(t, 4 + j, (*chip, 1 - c), me).wait_recv()
        for cp in first + passed:
            cp.wait_send()
        for cp in mine:
            cp.wait()


class _ExchangeComm:
    def __init__(self, blocks):
        self.inputs = list(blocks)
        n = self.n = len(blocks)
        self.out_shapes = [jax.ShapeDtypeStruct(b.shape, b.dtype) for b in blocks]
        self.scratch = [pltpu.SemaphoreType.DMA((n, 7)), pltpu.SemaphoreType.DMA((n, 7)), pltpu.SemaphoreType.DMA((n,))]

    def _parts(self, g_refs, out_refs, sems):
        send_sems, recv_sems, local_sems = sems
        x, y, c = lax.axis_index("x"), lax.axis_index("y"), lax.axis_index("c")
        me = 4 * x + 2 * y + c
        mine = [pltpu.make_async_copy(g_refs[t].at[me], out_refs[t].at[me], local_sems.at[t]) for t in range(self.n)]
        sends, recvs = [], []
        for k in range(1, N_DEV):
            px = 1 - x if k & 4 else x
            py = 1 - y if k & 2 else y
            pc = 1 - c if k & 1 else c
            p = 4 * px + 2 * py + pc
            for t in range(self.n):
                sends.append(pltpu.make_async_remote_copy(
                    src_ref=g_refs[t].at[p], dst_ref=out_refs[t].at[me], send_sem=send_sems.at[t, k - 1],
                    recv_sem=recv_sems.at[t, k - 1], device_id=(px, py, pc), device_id_type=MESH))
                recvs.append(pltpu.make_async_remote_copy(
                    src_ref=g_refs[t].at[p], dst_ref=out_refs[t].at[p], send_sem=send_sems.at[t, k - 1],
                    recv_sem=recv_sems.at[t, k - 1], device_id=(px, py, pc), device_id_type=MESH))
        return mine, sends, recvs

    def start(self, g_refs, out_refs, sems):
        mine, sends, _ = self._parts(g_refs, out_refs, sems)
        for cp in mine + sends:
            cp.start()

    def finish(self, g_refs, out_refs, sems):
        mine, sends, recvs = self._parts(g_refs, out_refs, sems)
        for cp in recvs:
            cp.wait_recv()
        for cp in sends:
            cp.wait_send()
        for cp in mine:
            cp.wait()


def _comm_call(comm, *, name):
    n = comm.n

    def body(*refs):
        comm.start(refs[:n], refs[n:2 * n], refs[2 * n:])
        comm.finish(refs[:n], refs[n:2 * n], refs[2 * n:])

    return pl.pallas_call(body, name=name, in_specs=[ANY] * n, out_specs=[ANY] * n, out_shape=comm.out_shapes,
                          scratch_shapes=comm.scratch)(*comm.inputs)


_DIMS = {
    "nn": (((1,), (0,)), ((), ())),
    "nt": (((1,), (1,)), ((), ())),
    "tn": (((0,), (0,)), ((), ())),
}

MATMUL_VMEM_BUDGET = 36 * 1024 * 1024
MAX_TILE = 1536


def _pick(n, prefs):
    for p in prefs:
        if n % p == 0:
            return p
    return n


def _tile_options(n):
    return [d for d in range(128, min(n, MAX_TILE) + 1, 128) if n % d == 0] or [n]


def _pick_tiles(M, N, tk, nk, sa, sb, so, has_addend, tm, tn):
    best = None
    for cm in ([tm] if tm else _tile_options(M)):
        for cn in ([tn] if tn else _tile_options(N)):
            need = 2 * (cm * tk * sa + tk * cn * sb + cm * cn * so + (cm * cn * 4 if has_addend else 0))
            need += cm * cn * 4 if nk > 1 else 0
            if need <= MATMUL_VMEM_BUDGET and (best is None or cm * cn > best[0] * best[1]
                                               or (cm * cn == best[0] * best[1] and cn > best[1])):
                best = (cm, cn)
    assert best is not None, (M, N, tk)
    return best


def _matmul(a, b, form, *, out_dtype=F32, addend=None, tm=None, tn=None, tk=None, comm=None, name):
    if form == "nn":
        (M, K), (K2, N) = a.shape, b.shape
    elif form == "nt":
        (M, K), (N, K2) = a.shape, b.shape
    else:
        (K, M), (K2, N) = a.shape, b.shape
    assert K == K2, (a.shape, b.shape, form)
    tk = tk or (K if K <= 2816 else _pick(K, (1024, 512, 256, 128)))
    nk = K // tk
    if tm is None or tn is None:
        tm, tn = _pick_tiles(M, N, tk, nk, a.dtype.itemsize, b.dtype.itemsize, jnp.dtype(out_dtype).itemsize,
                             addend is not None, tm, tn)
    assert M % tm == 0 and N % tn == 0 and K % tk == 0, (M, N, K, tm, tn, tk)
    dims = _DIMS[form]
    nc = comm.n if comm is not None else 0
    grid = (M // tm, N // tn, nk)

    def body(*refs):
        a_ref, b_ref = refs[:2]
        pos = 2
        add_ref = refs[pos] if addend is not None else None
        pos += addend is not None
        c_in, o_ref, c_out = refs[pos:pos + nc], refs[pos + nc], refs[pos + nc + 1:pos + 2 * nc + 1]
        pos += 2 * nc + 1
        acc_ref = refs[pos] if nk > 1 else None
        c_sems = refs[pos + (nk > 1):]
        if comm is not None:
            ids = [pl.program_id(d) for d in range(3)]

            @pl.when((ids[0] == 0) & (ids[1] == 0) & (ids[2] == 0))
            def _():
                comm.start(c_in, c_out, c_sems)

        def finish(r):
            if add_ref is not None:
                r = r + add_ref[...].astype(F32)
            o_ref[...] = r.astype(o_ref.dtype)

        part = lax.dot_general(a_ref[...].astype(BF16), b_ref[...].astype(BF16), dims, preferred_element_type=F32)
        if nk == 1:
            finish(part)
        else:
            k = pl.program_id(2)

            @pl.when(k == 0)
            def _():
                acc_ref[...] = part

            @pl.when(k > 0)
            def _():
                acc_ref[...] += part

            @pl.when(k == nk - 1)
            def _():
                finish(acc_ref[...])

        if comm is not None:
            @pl.when((ids[0] == grid[0] - 1) & (ids[1] == grid[1] - 1) & (ids[2] == grid[2] - 1))
            def _():
                comm.finish(c_in, c_out, c_sems)

    if form == "nn":
        a_spec = pl.BlockSpec((tm, tk), lambda i, j, k: (i, k))
        b_spec = pl.BlockSpec((tk, tn), lambda i, j, k: (k, j))
    elif form == "nt":
        a_spec = pl.BlockSpec((tm, tk), lambda i, j, k: (i, k))
        b_spec = pl.BlockSpec((tn, tk), lambda i, j, k: (j, k))
    else:
        a_spec = pl.BlockSpec((tk, tm), lambda i, j, k: (k, i))
        b_spec = pl.BlockSpec((tk, tn), lambda i, j, k: (k, j))
    o_spec = pl.BlockSpec((tm, tn), lambda i, j, k: (i, j))
    in_specs = [a_spec, b_spec] + ([o_spec] if addend is not None else [])
    args = (a, b) + ((addend,) if addend is not None else ())
    out_shape = jax.ShapeDtypeStruct((M, N), out_dtype)
    scratch = [pltpu.VMEM((tm, tn), F32)] if nk > 1 else []
    if comm is None:
        return pl.pallas_call(
            body, name=name, grid=grid, in_specs=in_specs, out_specs=o_spec, out_shape=out_shape,
            scratch_shapes=scratch, compiler_params=_cparams(("parallel", "parallel", "arbitrary")),
        )(*args)
    outs = pl.pallas_call(
        body, name=name, grid=grid, in_specs=in_specs + [ANY] * nc, out_specs=[o_spec] + [ANY] * nc,
        out_shape=[out_shape] + comm.out_shapes, scratch_shapes=scratch + comm.scratch,
        compiler_params=_cparams(("arbitrary", "arbitrary", "arbitrary")),
    )(*args, *comm.inputs)
    return outs[0], outs[1:]


def _rms_fwd(x, g, *, name, tm=512):
    M, D = x.shape
    tm = min(tm, M)

    def body(x_ref, g_ref, n_ref):
        xf = x_ref[...]
        r = lax.rsqrt(jnp.mean(xf * xf, axis=-1, keepdims=True) + EPS)
        n_ref[...] = (xf * r * g_ref[...]).astype(n_ref.dtype)

    return pl.pallas_call(
        body, name=name, grid=(M // tm,),
        in_specs=[pl.BlockSpec((tm, D), lambda i: (i, 0)), pl.BlockSpec((1, D), lambda i: (0, 0))],
        out_specs=pl.BlockSpec((tm, D), lambda i: (i, 0)),
        out_shape=jax.ShapeDtypeStruct((M, D), BF16),
        compiler_params=_cparams(("parallel",)),
    )(x, g.reshape(1, D))


def _rms_bwd(x, g, dn, dres, *, name, tm=512):
    M, D = x.shape
    tm = min(tm, M)

    def body(x_ref, g_ref, dn_ref, dres_ref, dx_ref, dg_ref):
        @pl.when(pl.program_id(0) == 0)
        def _():
            dg_ref[...] = jnp.zeros_like(dg_ref)

        xf = x_ref[...]
        r = lax.rsqrt(jnp.mean(xf * xf, axis=-1, keepdims=True) + EPS)
        xh = xf * r
        dn_ = dn_ref[...].astype(F32)
        dg_ref[...] += jnp.sum(dn_ * xh, axis=0, keepdims=True)
        dxh = dn_ * g_ref[...]
        dx = r * (dxh - xh * jnp.mean(dxh * xh, axis=-1, keepdims=True))
        dx_ref[...] = dres_ref[...] + dx

    row = pl.BlockSpec((tm, D), lambda i: (i, 0))
    vec = pl.BlockSpec((1, D), lambda i: (0, 0))
    return pl.pallas_call(
        body, name=name, grid=(M // tm,),
        in_specs=[row, vec, row, row], out_specs=[row, vec],
        out_shape=[jax.ShapeDtypeStruct((M, D), F32), jax.ShapeDtypeStruct((1, D), F32)],
        compiler_params=_cparams(("arbitrary",)),
    )(x, g.reshape(1, D), dn, dres)


def _loss_head(h, g, tgt, *, name, tm=512):
    M, D = h.shape
    tm = min(tm, M)

    def body(h_ref, g_ref, t_ref, loss_ref, dh_ref, dg_ref):
        @pl.when(pl.program_id(0) == 0)
        def _():
            dg_ref[...] = jnp.zeros_like(dg_ref)
            loss_ref[...] = jnp.zeros_like(loss_ref)

        xf = h_ref[...]
        r = lax.rsqrt(jnp.mean(xf * xf, axis=-1, keepdims=True) + EPS)
        xh = xf * r
        err = xh * g_ref[...] - t_ref[...]
        part = jnp.sum(jnp.mean(err * err, axis=-1, keepdims=True), axis=0, keepdims=True)
        loss_ref[...] += 0.5 * part
        dy = err * (1.0 / D)
        dg_ref[...] += jnp.sum(dy * xh, axis=0, keepdims=True)
        dxh = dy * g_ref[...]
        dh_ref[...] = r * (dxh - xh * jnp.mean(dxh * xh, axis=-1, keepdims=True))

    row = pl.BlockSpec((tm, D), lambda i: (i, 0))
    vec = pl.BlockSpec((1, D), lambda i: (0, 0))
    one = pl.BlockSpec((1, 1), lambda i: (0, 0))
    return pl.pallas_call(
        body, name=name, grid=(M // tm,),
        in_specs=[row, vec, row], out_specs=[one, row, vec],
        out_shape=[jax.ShapeDtypeStruct((1, 1), F32), jax.ShapeDtypeStruct((M, D), F32),
                   jax.ShapeDtypeStruct((1, D), F32)],
        compiler_params=_cparams(("arbitrary",)),
    )(h, g.reshape(1, D), tgt)


HG_MID = HG_CHUNK // 2 - 1
EXP_CAP = 80.0


def _sigmoid(x):
    return 1.0 / (1.0 + jnp.exp(-x))


def _dot(a, b, dims, precision=None):
    return lax.dot_general(a, b, dims, preferred_element_type=F32, precision=precision)


def _bdot(a, b, form):
    return _dot(a.astype(BF16), b.astype(BF16), _DIMS[form])


def _split2(x):
    hi = x.astype(BF16)
    return hi, (x - hi.astype(F32)).astype(BF16)


def _dot3(a, b, form):
    d = _DIMS[form]
    return _dot(a[0], b[0], d) + (_dot(a[0], b[1], d) + _dot(a[1], b[0], d))


def _hgrn_chunk_common(hq, hf, lbv, tril, rid):
    sq = _sigmoid(hq)
    q = hq * sq
    sg = _sigmoid(hf)
    f = lbv + (1.0 - lbv) * sg
    k = (1.0 - lbv) * (1.0 - sg)
    g = jnp.log(f)
    b = _dot(tril, g, _DIMS["nn"], precision=lax.Precision.HIGHEST)
    bref = jnp.sum(jnp.where(rid == HG_MID, b, 0.0), axis=0, keepdims=True)
    bend = jnp.sum(jnp.where(rid == HG_CHUNK - 1, b, 0.0), axis=0, keepdims=True)
    eb = jnp.exp(b)
    e1 = jnp.exp(jnp.minimum(b - bref, EXP_CAP))
    e2 = jnp.exp(jnp.minimum(bref - b, EXP_CAP))
    e3 = jnp.exp(bend - b)
    return sq, q, sg, f, k, bend, eb, e1, e2, e3


def _hgrn_fwd(proj, lb, gnorm, *, name, T=1024):
    S = proj.shape[0]
    T = min(T, S)
    nch = T // HG_CHUNK
    C = HG_CHUNK

    def body(hq_ref, hf_ref, hi_ref, hg_ref, lb_ref, gn_ref, o_ref, oa_ref, st_ref, state):
        @pl.when(pl.program_id(1) == 0)
        def _():
            state[...] = jnp.zeros_like(state)

        lbv = lb_ref[...]
        gn = gn_ref[...]
        row = lax.broadcasted_iota(jnp.int32, (C, C), 0)
        col = lax.broadcasted_iota(jnp.int32, (C, C), 1)
        causal = row >= col
        tril = causal.astype(F32)
        rid = lax.broadcasted_iota(jnp.int32, (C, HG_DK), 0)
        sls = [pl.ds(c * C, C) for c in range(nch)]
        pre = [_hgrn_chunk_common(hq_ref[sl, :], hf_ref[sl, :], lbv, tril, rid) for sl in sls]
        v_l = [hi_ref[sl, :].astype(BF16) for sl in sls]
        a_l, u_l = [], []
        for c in range(nch):
            _, q, _, _, k, _, _, e1, e2, e3 = pre[c]
            a_l.append(jnp.where(causal, _bdot(q * e1, k * e2, "nt"), 0.0))
            u_l.append(_bdot(v_l[c], k * e3, "tn"))
        o_l = [_bdot(a_l[c], v_l[c], "nn") for c in range(nch)]
        st = state[...]
        st_l = []
        for c in range(nch):
            st_l.append(st)
            st = st * jnp.exp(pre[c][5]) + u_l[c]
        state[...] = st
        for c in range(nch):
            st_ref[0, c] = st_l[c]
            o_l[c] = o_l[c] + _bdot(pre[c][1] * pre[c][6], st_l[c], "nt")
        for c in range(nch):
            o, hg = o_l[c], hg_ref[sls[c], :]
            o_ref[sls[c], :] = o
            r = lax.rsqrt(jnp.mean(o * o, axis=-1, keepdims=True) + EPS)
            oa_ref[sls[c], :] = (o * r * gn * (hg * _sigmoid(hg))).astype(oa_ref.dtype)

    def grp(gidx):
        return pl.BlockSpec((T, 128), lambda h, t: (t, gidx * 8 + h))

    return pl.pallas_call(
        body, name=name, grid=(HG_HEADS, S // T),
        in_specs=[grp(0), grp(1), grp(2), grp(3),
                  pl.BlockSpec((1, 128), lambda h, t: (0, h)), pl.BlockSpec((1, 128), lambda h, t: (0, 0))],
        out_specs=[pl.BlockSpec((T, 128), lambda h, t: (t, h)), pl.BlockSpec((T, 128), lambda h, t: (t, h)),
                   pl.BlockSpec((1, nch, HG_DV, HG_DK), lambda h, t: (h, t, 0, 0))],
        out_shape=[jax.ShapeDtypeStruct((S, HG_HEADS * HG_DV), F32), jax.ShapeDtypeStruct((S, HG_HEADS * HG_DV), BF16),
                   jax.ShapeDtypeStruct((HG_HEADS, S // C, HG_DV, HG_DK), F32)],
        scratch_shapes=[pltpu.VMEM((HG_DV, HG_DK), F32)],
        compiler_params=_cparams(("parallel", "arbitrary")),
    )(proj, proj, proj, proj, lb, gnorm)


def _hgrn_bwd(proj, lb, gnorm, o, states, doa, *, name, T=1024):
    S = proj.shape[0]
    T = min(T, S)
    nch = T // HG_CHUNK
    C = HG_CHUNK
    nT = S // T

    def body(hq_ref, hf_ref, hi_ref, hg_ref, lb_ref, gn_ref, o_ref, st_ref, doa_ref,
             dhq_ref, dhf_ref, dhi_ref, dhg_ref, dlb_ref, dgn_ref, dstate):
        @pl.when(pl.program_id(1) == 0)
        def _():
            dstate[...] = jnp.zeros_like(dstate)
            dlb_ref[...] = jnp.zeros_like(dlb_ref)
            dgn_ref[...] = jnp.zeros_like(dgn_ref)

        lbv = lb_ref[...]
        gn = gn_ref[...]
        row = lax.broadcasted_iota(jnp.int32, (C, C), 0)
        col = lax.broadcasted_iota(jnp.int32, (C, C), 1)
        causal = row >= col
        tril = causal.astype(F32)
        triu = (row <= col).astype(F32)
        rid = lax.broadcasted_iota(jnp.int32, (C, HG_DK), 0)
        rng = range(nch)
        sls = [pl.ds(c * C, C) for c in rng]
        pre = [_hgrn_chunk_common(hq_ref[sl, :], hf_ref[sl, :], lbv, tril, rid) for sl in sls]
        do2, dgn_acc = [], jnp.zeros((1, HG_DV), F32)
        for c in rng:
            hg, ov = hg_ref[sls[c], :], o_ref[sls[c], :]
            r = lax.rsqrt(jnp.mean(ov * ov, axis=-1, keepdims=True) + EPS)
            xh = ov * r
            sgg = _sigmoid(hg)
            d_oa = doa_ref[sls[c], :].astype(F32)
            dz = d_oa * (hg * sgg)
            dhg_ref[sls[c], :] = (d_oa * (xh * gn) * (sgg * (1.0 + hg * (1.0 - sgg)))).astype(dhg_ref.dtype)
            dgn_acc = dgn_acc + jnp.sum(dz * xh, axis=0, keepdims=True)
            dxh = dz * gn
            do2.append(_split2(r * (dxh - xh * jnp.mean(dxh * xh, axis=-1, keepdims=True))))
        dgn_ref[0] += dgn_acc
        qi = [pre[c][1] * pre[c][6] for c in rng]
        qp = [pre[c][1] * pre[c][7] for c in rng]
        kp = [pre[c][4] * pre[c][8] for c in rng]
        kend = [pre[c][4] * pre[c][9] for c in rng]
        qi2, qp2, kp2, kend2 = ([_split2(t) for t in lst] for lst in (qi, qp, kp, kend))
        v2 = [_split2(hi_ref[sl, :]) for sl in sls]
        st0 = [st_ref[0, c] for c in rng]
        a_l = [jnp.where(causal, _dot(qp2[c][0], kp2[c][0], _DIMS["nt"]), 0.0).astype(BF16) for c in rng]
        da2 = [_split2(jnp.where(causal, _dot3(do2[c], v2[c], "nt"), 0.0)) for c in rng]
        dqi = [_dot3(do2[c], _split2(st0[c]), "nn") for c in rng]
        w_l = [_dot3(do2[c], qi2[c], "tn") for c in rng]
        ds = dstate[...]
        ds1 = [None] * nch
        for c in reversed(rng):
            ds1[c] = ds
            ds = ds * jnp.exp(pre[c][5]) + w_l[c]
        dstate[...] = ds
        ds12 = [_split2(t) for t in ds1]
        dqp = [_dot3(da2[c], kp2[c], "nn") for c in rng]
        dkp = [_dot3(da2[c], qp2[c], "tn") for c in rng]
        dv = [_dot(a_l[c], do2[c][0], _DIMS["tn"]) + _dot(kend2[c][0], ds12[c][0], _DIMS["nt"]) for c in rng]
        dkend = [_dot3(v2[c], ds12[c], "nn") for c in rng]
        dq_l, dk_l, db_l = [], [], []
        for c in rng:
            _, _, _, _, _, bend, eb, e1, e2, e3 = pre[c]
            dq_l.append(dqi[c] * eb + dqp[c] * e1)
            dk_l.append(dkp[c] * e2 + dkend[c] * e3)
            db = dqi[c] * qi[c] + dqp[c] * qp[c] - dkp[c] * kp[c] - dkend[c] * kend[c]
            dbend = (jnp.sum(dkend[c] * kend[c], axis=0, keepdims=True)
                     + jnp.exp(bend) * jnp.sum(ds1[c] * st0[c], axis=0, keepdims=True))
            db_l.append(db + jnp.where(rid == C - 1, dbend, 0.0))
        dg = [_dot(triu, db_l[c], _DIMS["nn"], precision=lax.Precision.HIGHEST) for c in rng]
        dlb_acc = jnp.zeros((1, HG_DK), F32)
        for c in rng:
            sq, _, sg, f, _, _, _, _, _, _ = pre[c]
            hq = hq_ref[sls[c], :]
            df = dg[c] / f - dk_l[c]
            dlb_acc = dlb_acc + jnp.sum(df * (1.0 - sg), axis=0, keepdims=True)
            dhf_ref[sls[c], :] = (df * (1.0 - lbv) * sg * (1.0 - sg)).astype(dhf_ref.dtype)
            dhq_ref[sls[c], :] = (dq_l[c] * (sq * (1.0 + hq * (1.0 - sq)))).astype(dhq_ref.dtype)
            dhi_ref[sls[c], :] = dv[c].astype(dhi_ref.dtype)
        dlb_ref[...] += dlb_acc

    def grp(gidx):
        return pl.BlockSpec((T, 128), lambda h, t: (nT - 1 - t, gidx * 8 + h))

    tok = pl.BlockSpec((T, 128), lambda h, t: (nT - 1 - t, h))
    big = jax.ShapeDtypeStruct((S, HG_HEADS * HG_DV), BF16)
    return pl.pallas_call(
        body, name=name, grid=(HG_HEADS, nT),
        in_specs=[grp(0), grp(1), grp(2), grp(3),
                  pl.BlockSpec((1, 128), lambda h, t: (0, h)), pl.BlockSpec((1, 128), lambda h, t: (0, 0)),
                  tok, pl.BlockSpec((1, nch, HG_DV, HG_DK), lambda h, t: (h, nT - 1 - t, 0, 0)), tok],
        out_specs=[tok, tok, tok, tok, pl.BlockSpec((1, 128), lambda h, t: (0, h)),
                   pl.BlockSpec((1, 1, 128), lambda h, t: (h, 0, 0))],
        out_shape=[big, big, big, big, jax.ShapeDtypeStruct((1, HG_HEADS * HG_DK), F32),
                   jax.ShapeDtypeStruct((HG_HEADS, 1, HG_DV), F32)],
        scratch_shapes=[pltpu.VMEM((HG_DV, HG_DK), F32)],
        compiler_params=_cparams(("parallel", "arbitrary")),
    )(proj, proj, proj, proj, lb, gnorm, o, states, doa)


def _lb_fwd(logits, *, name):
    def body(l_ref, lb_ref):
        lb_ref[...] = _sigmoid(l_ref[0:1, :] - l_ref[1:2, :])

    return pl.pallas_call(body, name=name, out_shape=jax.ShapeDtypeStruct((1, logits.shape[1]), F32))(logits)


def _lb_bwd(logits, dlb, *, name):
    def body(l_ref, d_ref, o_ref):
        lbv = _sigmoid(l_ref[0:1, :] - l_ref[1:2, :])
        t = d_ref[...] * lbv * (1.0 - lbv)
        o_ref[0:1, :] = t
        o_ref[1:2, :] = -t

    return pl.pallas_call(body, name=name, out_shape=jax.ShapeDtypeStruct(logits.shape, F32))(logits, dlb)


NEG = -1e30
FOX_SCALE = FOX_DH ** -0.5
FOX_PAIRS = FOX_HEADS // 2


def _fox_gate_fwd(ff, bias, *, name, T=512):
    S = ff.shape[0]
    T = min(T, S)

    def body(ff_ref, b_ref, c_ref, carry):
        @pl.when(pl.program_id(0) == 0)
        def _():
            carry[...] = jnp.zeros_like(carry)

        z = ff_ref[...] + b_ref[...]
        logf = jnp.minimum(z, 0.0) - jnp.log(1.0 + jnp.exp(-jnp.abs(z)))
        row = lax.broadcasted_iota(jnp.int32, (T, T), 0)
        col = lax.broadcasted_iota(jnp.int32, (T, T), 1)
        c = _dot((row >= col).astype(F32), logf, _DIMS["nn"], precision=lax.Precision.HIGHEST) + carry[...]
        c_ref[...] = c
        carry[...] = c[T - 1:T, :]

    return pl.pallas_call(
        body, name=name, grid=(S // T,),
        in_specs=[pl.BlockSpec((T, 128), lambda i: (i, 0)), pl.BlockSpec((1, 128), lambda i: (0, 0))],
        out_specs=pl.BlockSpec((T, 128), lambda i: (i, 0)),
        out_shape=jax.ShapeDtypeStruct((S, 128), F32),
        scratch_shapes=[pltpu.VMEM((1, 128), F32)],
        compiler_params=_cparams(("arbitrary",)),
    )(ff, bias)


def _fox_gate_bwd(ff, bias, dcs, *, name, T=512):
    S = ff.shape[0]
    T = min(T, S)
    nT = S // T

    def body(ff_ref, b_ref, d_ref, dff_ref, db_ref, carry):
        @pl.when(pl.program_id(0) == 0)
        def _():
            carry[...] = jnp.zeros_like(carry)
            db_ref[...] = jnp.zeros_like(db_ref)

        row = lax.broadcasted_iota(jnp.int32, (T, T), 0)
        col = lax.broadcasted_iota(jnp.int32, (T, T), 1)
        dlogf = carry[...] - _dot((row <= col).astype(F32), d_ref[...], _DIMS["nn"], precision=lax.Precision.HIGHEST)
        carry[...] = dlogf[0:1, :]
        dff = dlogf * (1.0 - _sigmoid(ff_ref[...] + b_ref[...]))
        dff_ref[...] = dff.astype(dff_ref.dtype)
        db_ref[...] += jnp.sum(dff, axis=0, keepdims=True)

    rev = pl.BlockSpec((T, 128), lambda i: (nT - 1 - i, 0))
    vec = pl.BlockSpec((1, 128), lambda i: (0, 0))
    return pl.pallas_call(
        body, name=name, grid=(nT,),
        in_specs=[rev, vec, rev], out_specs=[rev, vec],
        out_shape=[jax.ShapeDtypeStruct((S, 128), BF16), jax.ShapeDtypeStruct((1, 128), F32)],
        scratch_shapes=[pltpu.VMEM((1, 128), F32)],
        compiler_params=_cparams(("arbitrary",)),
    )(ff, bias, dcs)


AUG = FOX_DH
RSUM_LANE = 6


def _bias_lane(hh):
    return AUG * (1 - hh)


def _data_lanes(lane, hh):
    return (lane < AUG) if hh == 0 else (lane >= AUG)


def _split3(x):
    a = x.astype(BF16).astype(F32)
    r = x - a
    b = r.astype(BF16).astype(F32)
    return a, b, r - b


def _lane_fill(lane, base, pieces, start):
    for i, pc in enumerate(pieces):
        base = jnp.where(lane == start + i, pc, base)
    return base


FOX_TB = 512
FOX_SKIP = 32.0
N_STAT = 4


def _fox_prep(proj, c_tok, *, name):
    S = proj.shape[0]
    T = min(FOX_TB, S)

    def body(q_ref, k_ref, v_ref, c_ref, qa_ref, ka_ref, va_ref, st_ref):
        pair = pl.program_id(0)
        lane = lax.broadcasted_iota(jnp.int32, (T, 128), 1)
        lane1 = lax.broadcasted_iota(jnp.int32, (1, 128), 1)
        c = c_ref[...]
        q, k, v = q_ref[...], k_ref[...], v_ref[...]
        for hh in range(2):
            data, b0 = _data_lanes(lane, hh), _bias_lane(hh)
            ones3 = jnp.where((lane >= b0) & (lane < b0 + 3), 1.0, 0.0)

            def max_norm(t):
                tr = jnp.where(data, t.astype(BF16).astype(F32), 0.0)
                return jnp.sqrt(jnp.max(jnp.sum(tr * tr, axis=-1, keepdims=True), axis=0, keepdims=True))

            ch = jnp.sum(jnp.where(lane == 2 * pair + hh, c, 0.0), axis=-1, keepdims=True)
            c1, c2, c3 = _split3(ch)
            aug_q = _lane_fill(lane, jnp.where((lane >= b0 + 3) & (lane < b0 + 6), 1.0, 0.0), (c1, c2, c3), b0)
            aug_k = _lane_fill(lane, ones3, (-c1, -c2, -c3), b0 + 3)
            qa_ref[hh] = jnp.where(data, q * FOX_SCALE, aug_q).astype(BF16)
            ka_ref[hh] = jnp.where(data, k, aug_k).astype(BF16)
            va_ref[hh] = jnp.where(data, v, ones3).astype(BF16)
            stats = (max_norm(q * FOX_SCALE), jnp.max(ch, axis=0, keepdims=True), max_norm(k),
                     jnp.min(ch, axis=0, keepdims=True))
            st_ref[hh, 0] = _lane_fill(lane1, jnp.zeros((1, 128), F32), stats, 0)

    def grp(g):
        return pl.BlockSpec((T, 128), lambda p, t: (t, g * 8 + p))

    hm = pl.BlockSpec((2, T, 128), lambda p, t: (p, t, 0))
    out = jax.ShapeDtypeStruct((FOX_HEADS, S, 128), BF16)
    return pl.pallas_call(
        body, name=name, grid=(FOX_PAIRS, S // T),
        in_specs=[grp(4), grp(5), grp(6), pl.BlockSpec((T, 128), lambda p, t: (t, 0))],
        out_specs=[hm, hm, hm, pl.BlockSpec((2, 1, 1, 128), lambda p, t: (p, t, 0, 0))],
        out_shape=[out, out, out, jax.ShapeDtypeStruct((FOX_HEADS, S // T, 1, 128), F32)],
        compiler_params=_cparams(("parallel", "parallel")),
    )(proj, proj, proj, c_tok)


def _fox_bound(st_ref, head, nb, qi, ki):
    qb_, kb_ = (head * nb + qi) * N_STAT, (head * nb + ki) * N_STAT
    return st_ref[qb_] * st_ref[kb_ + 2] + st_ref[qb_ + 1] - st_ref[kb_ + 3] + 0.01


def _pair_lanes(lane, a0, a1):
    return jnp.where(lane < AUG, a0, a1)


def _first_live_key(st_ref, head, nb, qi, newest, thr):
    def body(t, k0):
        k = newest - t
        return jnp.where(_fox_bound(st_ref, head, nb, qi, k) > thr, k, k0)

    return lax.fori_loop(0, newest + 1, body, newest + 1)


def _last_live_query(st_ref, lm_ref, head, nb, ki):
    def body(t, i1):
        i = ki + 1 + t
        live = _fox_bound(st_ref, head, nb, i, ki) > lm_ref[head * nb + i] - FOX_SKIP
        return jnp.where(live, i, i1)

    return lax.fori_loop(0, nb - 1 - ki, body, ki)


class _BlockStream:
    def __init__(self, hbm_refs, bufs, sems, pair, tb):
        self.hbm, self.bufs, self.sems, self.pair, self.tb = hbm_refs, bufs, sems, pair, tb

    def _copies(self, blk, slot):
        rows = pl.ds(pl.multiple_of(blk * self.tb, self.tb), self.tb)
        return [pltpu.make_async_copy(h.at[pl.ds(2 * self.pair, 2), rows, :], b.at[slot], self.sems.at[n, slot])
                for n, (h, b) in enumerate(zip(self.hbm, self.bufs))]

    def start(self, blk, slot):
        for cp in self._copies(blk, slot):
            cp.start()

    def wait(self, blk, slot):
        for cp in self._copies(blk, slot):
            cp.wait()


def _fox_fwd(qa, ka, va, bounds, *, name):
    S = qa.shape[1]
    tb = min(FOX_TB, S)
    nb = S // tb

    def body(qa_ref, ka_hbm, va_hbm, st_ref, o_ref, qb_ref, lse_ref, kbuf, vbuf, sems, m_s, acc_s, m_min):
        pair, qi = pl.program_id(0), pl.program_id(1)
        stream = _BlockStream((ka_hbm, va_hbm), (kbuf, vbuf), sems, pair, tb)

        def head_step(hh, slot, masked, paired=True):
            s = _dot(qa_ref[hh], kbuf[slot, hh], _DIMS["nt"])
            if masked:
                row = lax.broadcasted_iota(jnp.int32, (tb, tb), 0)
                col = lax.broadcasted_iota(jnp.int32, (tb, tb), 1)
                s = jnp.where(col <= row, s, NEG)
            m_old = m_s[hh]
            m_new = jnp.maximum(m_old, jnp.max(s, axis=-1, keepdims=True))
            p = jnp.exp(s - m_new)
            p_hi = p.astype(BF16)
            vv = vbuf[slot, hh]
            if paired:
                p_lo = (p - p_hi.astype(F32)).astype(BF16)
                acc_s[hh] = (jnp.exp(m_old - m_new) * acc_s[hh]
                             + _dot(p_hi, vv, _DIMS["nn"]) + _dot(p_lo, vv, _DIMS["nn"]))
            else:
                acc_s[hh] = jnp.exp(m_old - m_new) * acc_s[hh] + _dot(p_hi, vv, _DIMS["nn"])
            m_s[hh] = m_new
            m_min[hh] = jnp.min(m_new)

        @pl.when(qi == 0)
        def _():
            stream.start(qi, 0)

        @pl.when(qi > 0)
        def _():
            stream.start(qi - 1, 1)

        m_s[...] = jnp.full_like(m_s, NEG)
        acc_s[...] = jnp.zeros_like(acc_s)
        stream.wait(qi, 0)
        for hh in range(2):
            head_step(hh, 0, True)

        @pl.when(qi > 1)
        def _():
            stream.start(qi - 2, 0)

        @pl.when(qi > 0)
        def _():
            stream.wait(qi - 1, 1)
            for hh in range(2):
                head_step(hh, 1, False)

        k0 = [_first_live_key(st_ref, 2 * pair + hh, nb, qi, qi - 2, m_min[hh] - FOX_SKIP) for hh in range(2)]
        n = qi - 1 - jnp.minimum(k0[0], k0[1])

        @pl.when((qi > 1) & (n == 0))
        def _():
            stream.wait(qi - 2, 0)

        def loop(t, carry):
            k = qi - 2 - t
            slot = t % 2
            stream.wait(k, slot)

            @pl.when(t + 1 < n)
            def _():
                stream.start(k - 1, 1 - slot)

            live = [k >= k0[hh] for hh in range(2)]

            @pl.when(live[0] & live[1])
            def _():
                for hh in range(2):
                    head_step(hh, slot, False)

            for hh in range(2):
                @pl.when(live[hh] & jnp.logical_not(live[1 - hh]))
                def _():
                    head_step(hh, slot, False, paired=False)
            return carry

        lax.fori_loop(0, n, loop, 0)

        @pl.when(qi + 1 < nb)
        def _():
            stream.start(qi + 1, 0)

        lane = lax.broadcasted_iota(jnp.int32, (tb, 128), 1)
        outs = []
        for hh in range(2):
            acc = acc_s[hh]
            b0 = _bias_lane(hh)
            l = acc[:, b0:b0 + 1]
            outs.append(acc / l)
            lse = m_s[hh] + jnp.log(l)
            lse_ref[hh, 0] = jnp.broadcast_to(jnp.min(lse, axis=0, keepdims=True), (1, 128))
            qf = qa_ref[hh].astype(F32)
            cb = qf[:, b0:b0 + 1] + qf[:, b0 + 1:b0 + 2] + qf[:, b0 + 2:b0 + 3] - lse
            qb_ref[hh] = _lane_fill(lane, qf, _split3(cb), b0).astype(BF16)
        o_ref[...] = _pair_lanes(lane, outs[0], outs[1])

    qs = pl.BlockSpec((2, tb, 128), lambda p, i: (p, i, 0))
    return pl.pallas_call(
        body, name=name, grid=(FOX_PAIRS, nb),
        in_specs=[qs, ANY, ANY, SMEM],
        out_specs=[pl.BlockSpec((tb, 128), lambda p, i: (i, p)), qs,
                   pl.BlockSpec((2, 1, 1, 128), lambda p, i: (p, i, 0, 0))],
        out_shape=[jax.ShapeDtypeStruct((S, FOX_HEADS * FOX_DH), F32), jax.ShapeDtypeStruct((FOX_HEADS, S, 128), BF16),
                   jax.ShapeDtypeStruct((FOX_HEADS, nb, 1, 128), F32)],
        scratch_shapes=[pltpu.VMEM((2, 2, tb, 128), BF16), pltpu.VMEM((2, 2, tb, 128), BF16),
                        pltpu.SemaphoreType.DMA((2, 2)), pltpu.VMEM((2, tb, 1), F32), pltpu.VMEM((2, tb, 128), F32),
                        pltpu.SMEM((2,), F32)],
        compiler_params=_cparams(("arbitrary", "arbitrary")),
    )(qa, ka, va, bounds)


def _fox_bwd_prep(o, do, *, name, T=512):
    S = o.shape[0]
    T = min(T, S)

    def body(o_ref, do_ref, dob_ref):
        lane = lax.broadcasted_iota(jnp.int32, (T, 128), 1)
        d = do_ref[...].astype(F32)
        prod = d * o_ref[...]
        for hh in range(2):
            mine = _data_lanes(lane, hh)
            delta = jnp.sum(jnp.where(mine, prod, 0.0), axis=-1, keepdims=True)
            dob_ref[hh] = _lane_fill(lane, jnp.where(mine, d, 0.0), _split3(-delta), _bias_lane(hh)).astype(BF16)

    tok = pl.BlockSpec((T, 128), lambda p, t: (t, p))
    return pl.pallas_call(
        body, name=name, grid=(FOX_PAIRS, S // T),
        in_specs=[tok, tok], out_specs=pl.BlockSpec((2, T, 128), lambda p, t: (p, t, 0)),
        out_shape=jax.ShapeDtypeStruct((FOX_HEADS, S, 128), BF16),
        compiler_params=_cparams(("parallel", "parallel")),
    )(o, do)


def _fox_bwd_dq(qb, ka, va, dob, bounds, lse_min, *, name, comm=None):
    S = qb.shape[1]
    tb = min(FOX_TB, S)
    nb = S // tb
    nc = comm.n if comm is not None else 0

    def body(qb_ref, dob_ref, ka_hbm, va_hbm, st_ref, lm_ref, *rest):
        c_in, (dq_ref, dob2_ref), c_out = rest[:nc], rest[nc:nc + 2], rest[nc + 2:2 * nc + 2]
        kbuf, vbuf, sems, acc_s = rest[2 * nc + 2:2 * nc + 6]
        c_sems = rest[2 * nc + 6:]
        pair, qi = pl.program_id(0), pl.program_id(1)
        if comm is not None:
            @pl.when((pair == 0) & (qi == 0))
            def _():
                comm.start(c_in, c_out, c_sems)

        stream = _BlockStream((ka_hbm, va_hbm), (kbuf, vbuf), sems, pair, tb)
        k0 = [_first_live_key(st_ref, 2 * pair + hh, nb, qi, qi - 1, lm_ref[(2 * pair + hh) * nb + qi] - FOX_SKIP)
              for hh in range(2)]
        n = qi - jnp.minimum(k0[0], k0[1]) + 1

        @pl.when(qi == 0)
        def _():
            stream.start(qi, 0)

        acc_s[...] = jnp.zeros_like(acc_s)

        def head_step(hh, slot, k, masked):
            s = _dot(qb_ref[hh], kbuf[slot, hh], _DIMS["nt"])
            if masked:
                row = lax.broadcasted_iota(jnp.int32, (tb, tb), 0)
                col = lax.broadcasted_iota(jnp.int32, (tb, tb), 1)
                s = jnp.where(col <= row, s, NEG)
            ds = jnp.exp(s) * _dot(dob_ref[hh], vbuf[slot, hh], _DIMS["nt"])
            acc_s[hh] += _dot(ds.astype(BF16), kbuf[slot, hh], _DIMS["nn"])

        def loop(t, carry):
            k = qi - t
            slot = t % 2
            stream.wait(k, slot)

            @pl.when(t + 1 < n)
            def _():
                stream.start(k - 1, 1 - slot)

            @pl.when(t == 0)
            def _():
                for hh in range(2):
                    head_step(hh, slot, k, True)

            live = [(t > 0) & (k >= k0[hh]) for hh in range(2)]

            @pl.when(live[0] & live[1])
            def _():
                for hh in range(2):
                    head_step(hh, slot, k, False)

            for hh in range(2):
                @pl.when(live[hh] & jnp.logical_not(live[1 - hh]))
                def _():
                    head_step(hh, slot, k, False)
            return carry

        lax.fori_loop(0, n, loop, 0)

        @pl.when(qi + 1 < nb)
        def _():
            stream.start(qi + 1, 0)

        lane = lax.broadcasted_iota(jnp.int32, (tb, 128), 1)
        dq_ref[...] = (_pair_lanes(lane, acc_s[0], acc_s[1]) * FOX_SCALE).astype(dq_ref.dtype)
        for hh in range(2):
            b0 = _bias_lane(hh)
            r = acc_s[hh][:, b0:b0 + 1]
            dob2_ref[hh] = _lane_fill(lane, dob_ref[hh].astype(F32), _split3(r), b0 + RSUM_LANE).astype(BF16)
        if comm is not None:
            @pl.when((pair == FOX_PAIRS - 1) & (qi == nb - 1))
            def _():
                comm.finish(c_in, c_out, c_sems)

    qs = pl.BlockSpec((2, tb, 128), lambda p, i: (p, i, 0))
    outs = pl.pallas_call(
        body, name=name, grid=(FOX_PAIRS, nb),
        in_specs=[qs, qs, ANY, ANY, SMEM, SMEM] + [ANY] * nc,
        out_specs=[pl.BlockSpec((tb, 128), lambda p, i: (i, p)), qs] + [ANY] * nc,
        out_shape=[jax.ShapeDtypeStruct((S, FOX_HEADS * FOX_DH), BF16),
                   jax.ShapeDtypeStruct((FOX_HEADS, S, 128), BF16)] + (comm.out_shapes if comm is not None else []),
        scratch_shapes=[pltpu.VMEM((2, 2, tb, 128), BF16), pltpu.VMEM((2, 2, tb, 128), BF16),
                        pltpu.SemaphoreType.DMA((2, 2)), pltpu.VMEM((2, tb, 128), F32)]
        + (comm.scratch if comm is not None else []),
        compiler_params=_cparams(("arbitrary", "arbitrary")),
    )(qb, dob, ka, va, bounds, lse_min, *(comm.inputs if comm is not None else []))
    return (outs[0], outs[1]) if comm is None else (outs[0], outs[1], outs[2:])


def _fox_bwd_dkv(qb, ka, va, dob, bounds, lse_min, *, name):
    S = qb.shape[1]
    tb = min(FOX_TB, S)
    nb = S // tb

    def body(ka_ref, va_ref, qb_hbm, dob_hbm, st_ref, lm_ref, dk_ref, dv_ref, dcs_ref, qbuf, dbuf, sems, dk_s, dv_s):
        pair, ki = pl.program_id(0), pl.program_id(1)
        stream = _BlockStream((qb_hbm, dob_hbm), (qbuf, dbuf), sems, pair, tb)
        i1 = [_last_live_query(st_ref, lm_ref, 2 * pair + hh, nb, ki) for hh in range(2)]
        n = jnp.maximum(i1[0], i1[1]) - ki + 1

        @pl.when(ki == 0)
        def _():
            stream.start(ki, 0)

        dk_s[...] = jnp.zeros_like(dk_s)
        dv_s[...] = jnp.zeros_like(dv_s)

        def head_step(hh, slot, masked):
            st = _dot(ka_ref[hh], qbuf[slot, hh], _DIMS["nt"])
            if masked:
                row = lax.broadcasted_iota(jnp.int32, (tb, tb), 0)
                col = lax.broadcasted_iota(jnp.int32, (tb, tb), 1)
                st = jnp.where(row <= col, st, NEG)
            pt = jnp.exp(st)
            dst = pt * _dot(va_ref[hh], dbuf[slot, hh], _DIMS["nt"])
            dv_s[hh] += _dot(pt.astype(BF16), dbuf[slot, hh], _DIMS["nn"])
            dk_s[hh] += _dot(dst.astype(BF16), qbuf[slot, hh], _DIMS["nn"])

        def loop(t, carry):
            i = ki + t
            slot = t % 2
            stream.wait(i, slot)

            @pl.when(t + 1 < n)
            def _():
                stream.start(i + 1, 1 - slot)

            @pl.when(t == 0)
            def _():
                for hh in range(2):
                    head_step(hh, slot, True)

            live = [(t > 0) & (i <= i1[hh]) for hh in range(2)]

            @pl.when(live[0] & live[1])
            def _():
                for hh in range(2):
                    head_step(hh, slot, False)

            for hh in range(2):
                @pl.when(live[hh] & jnp.logical_not(live[1 - hh]))
                def _():
                    head_step(hh, slot, False)
            return carry

        lax.fori_loop(0, n, loop, 0)

        @pl.when(ki + 1 < nb)
        def _():
            stream.start(ki + 1, 0)

        lane = lax.broadcasted_iota(jnp.int32, (tb, 128), 1)
        dk_ref[...] = _pair_lanes(lane, dk_s[0], dk_s[1]).astype(dk_ref.dtype)
        dv_ref[...] = _pair_lanes(lane, dv_s[0], dv_s[1]).astype(dv_ref.dtype)
        for hh in range(2):
            b0 = _bias_lane(hh)
            dk_a, dv_a = dk_s[hh], dv_s[hh]
            off = dv_a[:, b0 + RSUM_LANE:b0 + RSUM_LANE + 1] + dv_a[:, b0 + RSUM_LANE + 1:b0 + RSUM_LANE + 2] \
                + dv_a[:, b0 + RSUM_LANE + 2:b0 + RSUM_LANE + 3]
            dcs_ref[0, :, hh:hh + 1] = dk_a[:, b0 + 3:b0 + 4] - off

    ks = pl.BlockSpec((2, tb, 128), lambda p, j: (p, j, 0))
    tok = pl.BlockSpec((tb, 128), lambda p, j: (j, p))
    big = jax.ShapeDtypeStruct((S, FOX_HEADS * FOX_DH), BF16)
    return pl.pallas_call(
        body, name=name, grid=(FOX_PAIRS, nb),
        in_specs=[ks, ks, ANY, ANY, SMEM, SMEM],
        out_specs=[tok, tok, pl.BlockSpec((1, tb, 2), lambda p, j: (p, j, 0))],
        out_shape=[big, big, jax.ShapeDtypeStruct((FOX_PAIRS, S, 2), F32)],
        scratch_shapes=[pltpu.VMEM((2, 2, tb, 128), BF16), pltpu.VMEM((2, 2, tb, 128), BF16),
                        pltpu.SemaphoreType.DMA((2, 2)), pltpu.VMEM((2, tb, 128), F32), pltpu.VMEM((2, tb, 128), F32)],
        compiler_params=_cparams(("arbitrary", "arbitrary")),
    )(ka, va, qb, dob, bounds, lse_min)


def _merge_fwd(proj, pa, pb, *, name, T=512):
    S, D = pa.shape
    T = min(T, S)

    def body(ga_ref, gb_ref, pa_ref, pb_ref, m_ref):
        m_ref[...] = (_sigmoid(ga_ref[...]) * pa_ref[...] + _sigmoid(gb_ref[...]) * pb_ref[...]).astype(m_ref.dtype)

    tok = pl.BlockSpec((T, D), lambda i: (i, 0))
    return pl.pallas_call(
        body, name=name, grid=(S // T,),
        in_specs=[pl.BlockSpec((T, D), lambda i: (i, 7)), pl.BlockSpec((T, D), lambda i: (i, 8)), tok, tok],
        out_specs=tok, out_shape=jax.ShapeDtypeStruct((S, D), BF16),
        compiler_params=_cparams(("parallel",)),
    )(proj, proj, pa, pb)


def _merge_bwd(proj, pa, pb, dm, *, name, T=512):
    S, D = pa.shape
    T = min(T, S)

    def body(ga_ref, gb_ref, pa_ref, pb_ref, dm_ref, dpa_ref, dpb_ref, dga_ref, dgb_ref):
        dm_ = dm_ref[...]
        sa, sb = _sigmoid(ga_ref[...]), _sigmoid(gb_ref[...])
        dpa_ref[...] = (dm_ * sa).astype(BF16)
        dpb_ref[...] = (dm_ * sb).astype(BF16)
        dga_ref[...] = (dm_ * pa_ref[...] * sa * (1.0 - sa)).astype(BF16)
        dgb_ref[...] = (dm_ * pb_ref[...] * sb * (1.0 - sb)).astype(BF16)

    tok = pl.BlockSpec((T, D), lambda i: (i, 0))
    big = jax.ShapeDtypeStruct((S, D), BF16)
    return pl.pallas_call(
        body, name=name, grid=(S // T,),
        in_specs=[pl.BlockSpec((T, D), lambda i: (i, 7)), pl.BlockSpec((T, D), lambda i: (i, 8)), tok, tok, tok],
        out_specs=[tok, tok, tok, tok], out_shape=[big, big, big, big],
        compiler_params=_cparams(("parallel",)),
    )(proj, proj, pa, pb, dm)


INV_SQRT2 = 0.7071067811865476
INV_SQRT2PI = 0.3989422804014327


def _shifted(u, prev, rid):
    m1 = jnp.where(rid == 0, prev[7:8, :], pltpu.roll(u, 1, 0))
    m2 = jnp.where(rid == 0, prev[6:7, :], jnp.where(rid == 1, prev[7:8, :], pltpu.roll(u, 2, 0)))
    return m1, m2


def _conv_acc(u, prev, w_ref, b_ref, rid):
    m1, m2 = _shifted(u, prev, rid)
    return b_ref[...] + w_ref[0:1, :] * m2 + w_ref[1:2, :] * m1 + w_ref[2:3, :] * u, m1, m2


def _convglu_fwd(ug, uv, wg, wv, bg, bv, *, name, T=512, tc=256):
    S, F = ug.shape
    T = min(T, S)

    def body(ug_ref, uv_ref, wg_ref, wv_ref, bg_ref, bv_ref, a_ref, pg, pv):
        @pl.when(pl.program_id(1) == 0)
        def _():
            pg[...] = jnp.zeros_like(pg)
            pv[...] = jnp.zeros_like(pv)

        rid = lax.broadcasted_iota(jnp.int32, (T, tc), 0)
        g_, v_ = ug_ref[...], uv_ref[...]
        accg, _, _ = _conv_acc(g_, pg[...], wg_ref, bg_ref, rid)
        accv, _, _ = _conv_acc(v_, pv[...], wv_ref, bv_ref, rid)
        gel = 0.5 * accg * (1.0 + lax.erf(accg * INV_SQRT2))
        a_ref[...] = (gel * accv).astype(a_ref.dtype)
        pg[...] = g_[T - 8:T, :]
        pv[...] = v_[T - 8:T, :]

    tok = pl.BlockSpec((T, tc), lambda j, t: (t, j))
    w3 = pl.BlockSpec((3, tc), lambda j, t: (0, j))
    b1 = pl.BlockSpec((1, tc), lambda j, t: (0, j))
    return pl.pallas_call(
        body, name=name, grid=(F // tc, S // T),
        in_specs=[tok, tok, w3, w3, b1, b1], out_specs=tok,
        out_shape=jax.ShapeDtypeStruct((S, F), BF16),
        scratch_shapes=[pltpu.VMEM((8, tc), F32), pltpu.VMEM((8, tc), F32)],
        compiler_params=_cparams(("parallel", "arbitrary")),
    )(ug, uv, wg, wv, bg, bv)


def _convglu_bwd(ug, uv, wg, wv, bg, bv, da, *, name, T=512, tc=256):
    S, F = ug.shape
    T = min(T, S)
    nT = S // T
    halo_blocks = T // 8

    def up_shift(d, nx, rid):
        p1 = jnp.where(rid == T - 1, nx[0:1, :], pltpu.roll(d, T - 1, 0))
        p2 = jnp.where(rid == T - 1, nx[1:2, :], jnp.where(rid == T - 2, nx[0:1, :], pltpu.roll(d, T - 2, 0)))
        return p1, p2

    def body(ug_ref, uv_ref, hg_ref, hv_ref, wg_ref, wv_ref, bg_ref, bv_ref, da_ref,
             dug_ref, duv_ref, dwg_ref, dwv_ref, dbg_ref, dbv_ref, ng, nv):
        @pl.when(pl.program_id(1) == 0)
        def _():
            ng[...] = jnp.zeros_like(ng)
            nv[...] = jnp.zeros_like(nv)
            for r in (dwg_ref, dwv_ref, dbg_ref, dbv_ref):
                r[...] = jnp.zeros_like(r)

        first_block = pl.program_id(1) == nT - 1
        rid = lax.broadcasted_iota(jnp.int32, (T, tc), 0)
        g_, v_ = ug_ref[...], uv_ref[...]
        pg = jnp.where(first_block, 0.0, hg_ref[...])
        pv = jnp.where(first_block, 0.0, hv_ref[...])
        accg, g1, g2 = _conv_acc(g_, pg, wg_ref, bg_ref, rid)
        accv, v1, v2 = _conv_acc(v_, pv, wv_ref, bv_ref, rid)
        cdf = 0.5 * (1.0 + lax.erf(accg * INV_SQRT2))
        pdf = INV_SQRT2PI * jnp.exp(-0.5 * accg * accg)
        da_ = da_ref[...].astype(F32)
        dgate = da_ * accv * (cdf + accg * pdf)
        dval = da_ * (accg * cdf)
        dbg_ref[...] += jnp.sum(dgate, axis=0, keepdims=True)
        dbv_ref[...] += jnp.sum(dval, axis=0, keepdims=True)
        for j, (sg_, sv_) in enumerate(((g2, v2), (g1, v1), (g_, v_))):
            dwg_ref[j:j + 1, :] += jnp.sum(dgate * sg_, axis=0, keepdims=True)
            dwv_ref[j:j + 1, :] += jnp.sum(dval * sv_, axis=0, keepdims=True)
        for d, w_ref, nx, out_ref in ((dgate, wg_ref, ng, dug_ref), (dval, wv_ref, nv, duv_ref)):
            p1, p2 = up_shift(d, nx[...], rid)
            out_ref[...] = (w_ref[2:3, :] * d + w_ref[1:2, :] * p1 + w_ref[0:1, :] * p2).astype(out_ref.dtype)
            nx[...] = d[0:8, :]

    tok = pl.BlockSpec((T, tc), lambda j, t: (nT - 1 - t, j))
    halo = pl.BlockSpec((8, tc), lambda j, t: (jnp.maximum((nT - 1 - t) * halo_blocks - 1, 0), j))
    w3 = pl.BlockSpec((3, tc), lambda j, t: (0, j))
    b1 = pl.BlockSpec((1, tc), lambda j, t: (0, j))
    big = jax.ShapeDtypeStruct((S, F), BF16)
    return pl.pallas_call(
        body, name=name, grid=(F // tc, nT),
        in_specs=[tok, tok, halo, halo, w3, w3, b1, b1, tok], out_specs=[tok, tok, w3, w3, b1, b1],
        out_shape=[big, big, jax.ShapeDtypeStruct((3, F), F32), jax.ShapeDtypeStruct((3, F), F32),
                   jax.ShapeDtypeStruct((1, F), F32), jax.ShapeDtypeStruct((1, F), F32)],
        scratch_shapes=[pltpu.VMEM((8, tc), F32), pltpu.VMEM((8, tc), F32)],
        compiler_params=_cparams(("parallel", "arbitrary")),
    )(ug, uv, ug, uv, wg, wv, bg, bv, da)


FF_LO = 7168
IN_SHARD = 1154
FF_DEV, FF_OFF = FF_LO // IN_SHARD, FF_LO % IN_SHARD


def _col_blocks(a, width):
    return jnp.stack([a[:, d * width:(d + 1) * width] for d in range(N_DEV)])


def _w_in_blocks(d_wm, d_wff):
    def block(d):
        lo = d * IN_SHARD
        if d < FF_DEV:
            return d_wm[:, lo:lo + IN_SHARD]
        if d > FF_DEV:
            return d_wm[:, lo - FOX_HEADS:lo - FOX_HEADS + IN_SHARD]
        return jnp.concatenate([d_wm[:, lo:FF_LO], d_wff[:, :FOX_HEADS], d_wm[:, FF_LO:lo + IN_SHARD - FOX_HEADS]], axis=1)

    return jnp.stack([block(d) for d in range(N_DEV)])


def _late_weights(g_a, g_b, g_o, g_up, g_cw, g_d):
    wup = jnp.concatenate([g_up[d] for d in range(N_DEV)], axis=1)
    cw = jnp.concatenate([g_cw[d] for d in range(N_DEV)], axis=1)
    return dict(wa=g_a.reshape(D_MODEL, D_MODEL), wb=g_b.reshape(D_MODEL, D_MODEL), wo=g_o.reshape(D_MODEL, D_MODEL),
                wug=wup[:, :D_FF], wuv=wup[:, D_FF:], cwg=cw[:, :D_FF], cwv=cw[:, D_FF:], wd=g_d.reshape(D_FF, D_MODEL))


def _early_grad_blocks(d_wa, d_wb, d_wo, d_wug, d_wuv, d_wd):
    up = jnp.stack([d_wug[:, d * 704:(d + 1) * 704] for d in range(4)]
                   + [d_wuv[:, d * 704:(d + 1) * 704] for d in range(4)])
    return [d_wa.reshape(N_DEV, 128, D_MODEL), d_wb.reshape(N_DEV, 128, D_MODEL), d_wo.reshape(N_DEV, 128, D_MODEL),
            up, d_wd.reshape(N_DEV, 352, D_MODEL)]


def _local_step(x, tgt, w, p, late=None, exchange=False):
    S = x.shape[0]
    mm = _matmul
    n1 = _rms_fwd(x, p["norm_mix"], name="rms1_fwd")
    if late is None:
        proj = mm(n1, w["wm"], "nn", name="proj_main")
    else:
        proj, gathered = mm(n1, w["wm"], "nn", comm=late, name="proj_main")
        w = {**w, **_late_weights(*gathered)}
    ff = mm(n1, w["wff"], "nn", name="proj_ff")
    lb = _lb_fwd(p["hg_lb_logits"], name="lb_fwd")
    gnorm = p["hg_norm"].reshape(1, HG_DV)
    o_hg, oa, states = _hgrn_fwd(proj, lb, gnorm, name="hgrn_fwd")
    bias = jnp.pad(p["fox_f_bias"].reshape(1, FOX_HEADS), ((0, 0), (0, 128 - FOX_HEADS)))
    c = _fox_gate_fwd(ff, bias, name="fox_gate_fwd")
    qa, ka, va, fox_stats = _fox_prep(proj, c, name="fox_prep")
    bounds = fox_stats[:, :, 0, :N_STAT].reshape(-1)
    ob, qb, lse_stats = _fox_fwd(qa, ka, va, bounds, name="fox_fwd")
    lse_min = lse_stats[:, :, 0, 0].reshape(-1)
    pa = mm(oa, w["wa"], "nn", name="branch_a")
    pb = mm(ob, w["wb"], "nn", name="branch_b")
    merged = _merge_fwd(proj, pa, pb, name="merge_fwd")
    h1 = mm(merged, w["wo"], "nn", addend=x, name="mix_out")
    n2 = _rms_fwd(h1, p["norm_ffn"], name="rms2_fwd")
    ug = mm(n2, w["wug"], "nn", name="up_gate")
    uv = mm(n2, w["wuv"], "nn", name="up_val")
    a = _convglu_fwd(ug, uv, w["cwg"], w["cwv"], p["cbg"], p["cbv"], name="convglu_fwd")
    h2 = mm(a, w["wd"], "nn", addend=h1, name="ffn_down")
    loss, dh2, d_norm_final = _loss_head(h2, p["norm_final"], tgt, name="loss_head")
    da = mm(dh2, w["wd"], "nt", out_dtype=BF16, name="d_act")
    d_wd = mm(a, dh2, "tn", out_dtype=BF16, name="dw_down")
    dug, duv, d_cwg, d_cwv, d_cbg, d_cbv = _convglu_bwd(
        ug, uv, w["cwg"], w["cwv"], p["cbg"], p["cbv"], da, name="convglu_bwd")
    dn2 = mm(dug, w["wug"], "nt", name="dn2_gate")
    dn2 = mm(duv, w["wuv"], "nt", addend=dn2, name="dn2_val")
    d_wug = mm(n2, dug, "tn", out_dtype=BF16, name="dw_up_gate")
    d_wuv = mm(n2, duv, "tn", out_dtype=BF16, name="dw_up_val")
    dh1, d_norm_ffn = _rms_bwd(h1, p["norm_ffn"], dn2, dh2, name="rms2_bwd")
    dmerged = mm(dh1, w["wo"], "nt", name="d_merged")
    d_wo = mm(merged, dh1, "tn", out_dtype=BF16, name="dw_out")
    dpa, dpb, dga, dgb = _merge_bwd(proj, pa, pb, dmerged, name="merge_bwd")
    doa = mm(dpa, w["wa"], "nt", name="d_oa")
    dob = mm(dpb, w["wb"], "nt", out_dtype=BF16, name="d_ob")
    d_wa = mm(oa, dpa, "tn", out_dtype=BF16, name="dw_branch_a")
    d_wb = mm(ob, dpb, "tn", out_dtype=BF16, name="dw_branch_b")
    dhq, dhf, dhi, dhg, dlb, dgn8 = _hgrn_bwd(proj, lb, gnorm, o_hg, states, doa, name="hgrn_bwd")
    d_logits = _lb_bwd(p["hg_lb_logits"], dlb, name="lb_bwd")
    dob_hm = _fox_bwd_prep(ob, dob, name="fox_bwd_prep")
    early_parts = None
    if exchange:
        comm = _ExchangeComm(_early_grad_blocks(d_wa, d_wb, d_wo, d_wug, d_wuv, d_wd))
        dq, dob2, early_parts = _fox_bwd_dq(qb, ka, va, dob_hm, bounds, lse_min, comm=comm, name="fox_bwd_dq")
    else:
        dq, dob2 = _fox_bwd_dq(qb, ka, va, dob_hm, bounds, lse_min, name="fox_bwd_dq")
    dk, dv, dcs = _fox_bwd_dkv(qb, ka, va, dob2, bounds, lse_min, name="fox_bwd_dkv")
    dcs_tok = jnp.pad(dcs.transpose(1, 0, 2).reshape(S, FOX_HEADS), ((0, 0), (0, 128 - FOX_HEADS)))
    dff, dbias = _fox_gate_bwd(ff, bias, dcs_tok, name="fox_gate_bwd")
    dproj = jnp.concatenate([dhq, dhf, dhi, dhg, dq, dk, dv, dga, dgb], axis=1)
    d_wm = mm(n1, dproj, "tn", out_dtype=BF16, name="dw_in_main")
    d_wff = mm(n1, dff, "tn", out_dtype=BF16, name="dw_in_ff")
    dn1 = mm(dff, w["wff"], "nt", name="dn1_ff")
    late_parts = None
    if exchange:
        d_cw = jnp.concatenate([d_cwg, d_cwv], axis=1)
        comm = _ExchangeComm([_w_in_blocks(d_wm, d_wff), _col_blocks(d_cw, 704)])
        dn1, late_parts = mm(dproj, w["wm"], "nt", addend=dn1, comm=comm, name="dn1_main")
    else:
        dn1 = mm(dproj, w["wm"], "nt", addend=dn1, name="dn1_main")
    dx, d_norm_mix = _rms_bwd(x, p["norm_mix"], dn1, dh1, name="rms1_bwd")
    grads = dict(
        wm=d_wm, wff=d_wff, wa=d_wa, wb=d_wb, wo=d_wo, wug=d_wug, wuv=d_wuv, cwg=d_cwg, cwv=d_cwv, wd=d_wd,
        norm_mix=d_norm_mix.reshape(-1), fox_f_bias=dbias[0, :FOX_HEADS], hg_lb_logits=d_logits,
        hg_norm=jnp.sum(dgn8, axis=0).reshape(-1), norm_ffn=d_norm_ffn.reshape(-1), cbg=d_cbg, cbv=d_cbv,
        norm_final=d_norm_final.reshape(-1), early_parts=early_parts, late_parts=late_parts)
    return loss, dx, grads


SMALL = [("norm_mix", (1, D_MODEL)), ("fox_f_bias", (1, FOX_HEADS)), ("hg_lb_logits", (2, HG_HEADS * HG_DK)),
         ("hg_norm", (1, HG_DV)), ("norm_ffn", (1, D_MODEL)), ("conv_b", (1, 2 * D_FF)), ("norm_final", (D_MODEL,))]
SMALL_ROWS = 88
SHARDED = [("w_in", (D_MODEL, 1154), 256), ("w_branch_a", (128, D_MODEL), 128), ("w_branch_b", (128, D_MODEL), 128),
           ("w_out", (128, D_MODEL), 128), ("w_up", (D_MODEL, 704), 256), ("conv_w", (3, 704), 3),
           ("w_down", (352, D_MODEL), 352)]
NAMES = ["norm_mix", "w_in", "fox_f_bias", "hg_lb_logits", "hg_norm", "w_branch_a", "w_branch_b", "w_out",
         "norm_ffn", "w_up", "conv_w", "conv_b", "w_down", "norm_final"]


def _size(shape):
    n = 1
    for s in shape:
        n *= s
    return n


def _adamw(parts, w, m, v, *, name, T):
    R, C = w.shape
    c1 = 1.0 / (1.0 - ADAM_B1 ** ADAM_STEP)
    c2 = 1.0 / (1.0 - ADAM_B2 ** ADAM_STEP)

    def body(p_ref, w_ref, m_ref, v_ref, g_ref, d_ref, nm_ref, nv_ref):
        g = p_ref[0].astype(F32)
        for s in range(1, N_DEV):
            g = g + p_ref[s].astype(F32)
        g_ref[...] = g
        nm = ADAM_B1 * m_ref[...] + (1.0 - ADAM_B1) * g
        nv = ADAM_B2 * v_ref[...] + (1.0 - ADAM_B2) * (g * g)
        nm_ref[...] = nm
        nv_ref[...] = nv
        d_ref[...] = -ADAM_LR * ((nm * c1) / (jnp.sqrt(nv * c2) + ADAM_EPS) + ADAM_WD * w_ref[...])

    blk = pl.BlockSpec((T, C), lambda i: (i, 0))
    out = jax.ShapeDtypeStruct((R, C), F32)
    return pl.pallas_call(
        body, name=name, grid=(R // T,),
        in_specs=[pl.BlockSpec((N_DEV, T, C), lambda i: (0, i, 0)), blk, blk, blk],
        out_specs=[blk, blk, blk, blk], out_shape=[out, out, out, out],
        compiler_params=_cparams(("parallel",)),
    )(parts, w, m, v)


def _pack_small(vals):
    flat = jnp.concatenate([vals[n].reshape(-1).astype(F32) for n, _ in SMALL])
    return jnp.pad(flat, (0, SMALL_ROWS * 128 - flat.shape[0])).reshape(SMALL_ROWS, 128)


def _unpack_small(buf):
    flat, out, off = buf.reshape(-1), {}, 0
    for n, shape in SMALL:
        out[n] = flat[off:off + _size(shape)].reshape(shape)
        off += _size(shape)
    return out


def kernel(x, norm_mix, w_in, fox_f_bias, hg_lb_logits, hg_norm, w_branch_a, w_branch_b, w_out, norm_ffn, w_up, conv_w, conv_b, w_down, norm_final, loss_target, m_norm_mix, m_w_in, m_fox_f_bias, m_hg_lb_logits, m_hg_norm, m_w_branch_a, m_w_branch_b, m_w_out, m_norm_ffn, m_w_up, m_conv_w, m_conv_b, m_w_down, m_norm_final, v_norm_mix, v_w_in, v_fox_f_bias, v_hg_lb_logits, v_hg_norm, v_w_branch_a, v_w_branch_b, v_w_out, v_norm_ffn, v_w_up, v_conv_w, v_conv_b, v_w_down, v_norm_final):
    wv = dict(norm_mix=norm_mix, w_in=w_in, fox_f_bias=fox_f_bias, hg_lb_logits=hg_lb_logits, hg_norm=hg_norm,
              w_branch_a=w_branch_a, w_branch_b=w_branch_b, w_out=w_out, norm_ffn=norm_ffn, w_up=w_up, conv_w=conv_w,
              conv_b=conv_b, w_down=w_down, norm_final=norm_final)
    mv = dict(norm_mix=m_norm_mix, w_in=m_w_in, fox_f_bias=m_fox_f_bias, hg_lb_logits=m_hg_lb_logits, hg_norm=m_hg_norm,
              w_branch_a=m_w_branch_a, w_branch_b=m_w_branch_b, w_out=m_w_out, norm_ffn=m_norm_ffn, w_up=m_w_up,
              conv_w=m_conv_w, conv_b=m_conv_b, w_down=m_w_down, norm_final=m_norm_final)
    vv = dict(norm_mix=v_norm_mix, w_in=v_w_in, fox_f_bias=v_fox_f_bias, hg_lb_logits=v_hg_lb_logits, hg_norm=v_hg_norm,
              w_branch_a=v_w_branch_a, w_branch_b=v_w_branch_b, w_out=v_w_out, norm_ffn=v_norm_ffn, w_up=v_w_up,
              conv_w=v_conv_w, conv_b=v_conv_b, w_down=v_w_down, norm_final=v_norm_final)

    (g_in,) = _comm_call(_GatherComm([w_in[0].astype(BF16)]), name="gather_w_in")
    w = dict(wm=jnp.concatenate([g_in[d] for d in range(FF_DEV)]
                                + [g_in[FF_DEV][:, :FF_OFF], g_in[FF_DEV][:, FF_OFF + FOX_HEADS:]]
                                + [g_in[d] for d in range(FF_DEV + 1, N_DEV)], axis=1),
             wff=jnp.pad(g_in[FF_DEV][:, FF_OFF:FF_OFF + FOX_HEADS], ((0, 0), (0, 128 - FOX_HEADS))))
    late = _GatherComm([w_branch_a[0].astype(BF16), w_branch_b[0].astype(BF16), w_out[0].astype(BF16),
                        w_up[0].astype(BF16), conv_w[0], w_down[0].astype(BF16)])
    p = dict(norm_mix=norm_mix[0], fox_f_bias=fox_f_bias[0], hg_lb_logits=hg_lb_logits, hg_norm=hg_norm[0],
             norm_ffn=norm_ffn[0], cbg=conv_b[:, :D_FF], cbv=conv_b[:, D_FF:], norm_final=norm_final)
    loss, dx, grads = _local_step(x[0], loss_target[0], w, p, late=late, exchange=True)
    loss = lax.psum(loss[0, 0], ("x", "y", "c"))

    small = _pack_small(dict(
        norm_mix=grads["norm_mix"], fox_f_bias=grads["fox_f_bias"], hg_lb_logits=grads["hg_lb_logits"],
        hg_norm=grads["hg_norm"], norm_ffn=grads["norm_ffn"], conv_b=jnp.concatenate([grads["cbg"], grads["cbv"]], axis=1),
        norm_final=grads["norm_final"]))
    (small_parts,) = _comm_call(_ExchangeComm([jnp.broadcast_to(small[None], (N_DEV, SMALL_ROWS, 128))]),
                                name="exchange_small")
    ea, eb, eo, eup, ed = grads["early_parts"]
    p_in, p_cw = grads["late_parts"]
    parts = [p_in, ea, eb, eo, eup, p_cw, ed, small_parts]
    res = {}
    for (n, shape, tile), part in zip(SHARDED, parts):
        outs = _adamw(part, wv[n].reshape(shape), mv[n].reshape(shape), vv[n].reshape(shape), name="adamw_" + n, T=tile)
        res[n] = [o.reshape(wv[n].shape) for o in outs]
    outs = _adamw(parts[-1], _pack_small(wv), _pack_small(mv), _pack_small(vv), name="adamw_small", T=SMALL_ROWS)
    small = [_unpack_small(o) for o in outs]
    for n, _ in SMALL:
        res[n] = [s[n] for s in small]
    return (loss, dx[None], *[res[n][0] for n in NAMES], *[res[n][1] for n in NAMES],
            *[res[n][2] for n in NAMES], *[res[n][3] for n in NAMES])
```

```python
import jax
import jax.numpy as jnp
from jax import lax
from jax.experimental import pallas as pl
from jax.experimental.pallas import tpu as pltpu

F32 = jnp.float32
BF16 = jnp.bfloat16

D_MODEL = 1024
HG_HEADS = 8
HG_DK = 128
HG_DV = 128
HG_CHUNK = 64
FOX_HEADS = 16
FOX_DH = 64
D_FF = 2816
EPS = 1e-6
N_DEV = 8

ADAM_LR = 0.001
ADAM_B1 = 0.9
ADAM_B2 = 0.999
ADAM_EPS = 1e-08
ADAM_WD = 0.01
ADAM_STEP = 10

VMEM_LIMIT = 56 * 1024 * 1024


def _cparams(sem):
    return pltpu.CompilerParams(dimension_semantics=sem, vmem_limit_bytes=VMEM_LIMIT)


MESH = pl.DeviceIdType.MESH
ANY = pl.BlockSpec(memory_space=pl.ANY)
SMEM = pl.BlockSpec(memory_space=pltpu.SMEM)


class _GatherComm:
    def __init__(self, shards):
        self.inputs = list(shards)
        n = self.n = len(shards)
        self.out_shapes = [jax.ShapeDtypeStruct((N_DEV,) + s.shape, s.dtype) for s in shards]
        self.scratch = [pltpu.SemaphoreType.DMA((n, 7)), pltpu.SemaphoreType.DMA((n, 7)), pltpu.SemaphoreType.DMA((n,))]

    def _parts(self, x_refs, out_refs, sems):
        send_sems, recv_sems, local_sems = sems
        x, y, c = lax.axis_index("x"), lax.axis_index("y"), lax.axis_index("c")
        me, sibling = (x, y, c), (x, y, 1 - c)
        chips = [(1 - x, y), (x, 1 - y), (1 - x, 1 - y)]

        def copy(t, k, block, to, src=None):
            slot = out_refs[t].at[4 * block[0] + 2 * block[1] + block[2]]
            return pltpu.make_async_remote_copy(
                src_ref=slot if src is None else src, dst_ref=slot,
                send_sem=send_sems.at[t, k], recv_sem=recv_sems.at[t, k], device_id=to, device_id_type=MESH)

        mine = [pltpu.make_async_copy(x_refs[t], out_refs[t].at[4 * x + 2 * y + c], local_sems.at[t])
                for t in range(self.n)]
        first = []
        for t in range(self.n):
            first.append(copy(t, 0, me, sibling, src=x_refs[t]))
            first += [copy(t, 1 + j, me, (*chip, c), src=x_refs[t]) for j, chip in enumerate(chips)]
        return c, me, sibling, chips, copy, mine, first

    def start(self, x_refs, out_refs, sems):
        _, _, _, _, _, mine, first = self._parts(x_refs, out_refs, sems)
        for cp in mine + first:
            cp.start()

    def finish(self, x_refs, out_refs, sems):
        c, me, sibling, chips, copy, mine, first = self._parts(x_refs, out_refs, sems)
        passed = []
        for j, chip in enumerate(chips):
            for t in range(self.n):
                copy(t, 1 + j, (*chip, c), me).wait_recv()
                passed.append(copy(t, 4 + j, (*chip, c), sibling))
                passed[-1].start()
        for t in range(self.n):
            copy(t, 0, sibling, me).wait_recv()
            for j, chip in enumerate(chips):
                copy(t, 4 + j, (*chip, 1 - c), me).wait_recv()
        for cp in first + passed:
            cp.wait_send()
        for cp in mine:
            cp.wait()


class _ExchangeComm:
    def __init__(self, blocks):
        self.inputs = list(blocks)
        n = self.n = len(blocks)
        self.out_shapes = [jax.ShapeDtypeStruct(b.shape, b.dtype) for b in blocks]
        self.scratch = [pltpu.SemaphoreType.DMA((n, 7)), pltpu.SemaphoreType.DMA((n, 7)), pltpu.SemaphoreType.DMA((n,))]

    def _parts(self, g_refs, out_refs, sems):
        send_sems, recv_sems, local_sems = sems
        x, y, c = lax.axis_index("x"), lax.axis_index("y"), lax.axis_index("c")
        me = 4 * x + 2 * y + c
        mine = [pltpu.make_async_copy(g_refs[t].at[me], out_refs[t].at[me], local_sems.at[t]) for t in range(self.n)]
        sends, recvs = [], []
        for k in range(1, N_DEV):
            px = 1 - x if k & 4 else x
            py = 1 - y if k & 2 else y
            pc = 1 - c if k & 1 else c
            p = 4 * px + 2 * py + pc
            for t in range(self.n):
                sends.append(pltpu.make_async_remote_copy(
                    src_ref=g_refs[t].at[p], dst_ref=out_refs[t].at[me], send_sem=send_sems.at[t, k - 1],
                    recv_sem=recv_sems.at[t, k - 1], device_id=(px, py, pc), device_id_type=MESH))
                recvs.append(pltpu.make_async_remote_copy(
                    src_ref=g_refs[t].at[p], dst_ref=out_refs[t].at[p], send_sem=send_sems.at[t, k - 1],
                    recv_sem=recv_sems.at[t, k - 1], device_id=(px, py, pc), device_id_type=MESH))
        return mine, sends, recvs

    def start(self, g_refs, out_refs, sems):
        mine, sends, _ = self._parts(g_refs, out_refs, sems)
        for cp in mine + sends:
            cp.start()

    def finish(self, g_refs, out_refs, sems):
        mine, sends, recvs = self._parts(g_refs, out_refs, sems)
        for cp in recvs:
            cp.wait_recv()
        for cp in sends:
            cp.wait_send()
        for cp in mine:
            cp.wait()


def _comm_call(comm, *, name):
    n = comm.n

    def body(*refs):
        comm.start(refs[:n], refs[n:2 * n], refs[2 * n:])
        comm.finish(refs[:n], refs[n:2 * n], refs[2 * n:])

    return pl.pallas_call(body, name=name, in_specs=[ANY] * n, out_specs=[ANY] * n, out_shape=comm.out_shapes,
                          scratch_shapes=comm.scratch)(*comm.inputs)


_DIMS = {
    "nn": (((1,), (0,)), ((), ())),
    "nt": (((1,), (1,)), ((), ())),
    "tn": (((0,), (0,)), ((), ())),
}

MATMUL_VMEM_BUDGET = 36 * 1024 * 1024
MAX_TILE = 1536


def _pick(n, prefs):
    for p in prefs:
        if n % p == 0:
            return p
    return n


def _tile_options(n):
    return [d for d in range(128, min(n, MAX_TILE) + 1, 128) if n % d == 0] or [n]


def _pick_tiles(M, N, tk, nk, sa, sb, so, has_addend, tm, tn):
    best = None
    for cm in ([tm] if tm else _tile_options(M)):
        for cn in ([tn] if tn else _tile_options(N)):
            need = 2 * (cm * tk * sa + tk * cn * sb + cm * cn * so + (cm * cn * 4 if has_addend else 0))
            need += cm * cn * 4 if nk > 1 else 0
            if need <= MATMUL_VMEM_BUDGET and (best is None or cm * cn > best[0] * best[1]
                                               or (cm * cn == best[0] * best[1] and cn > best[1])):
                best = (cm, cn)
    assert best is not None, (M, N, tk)
    return best


def _matmul(a, b, form, *, out_dtype=F32, addend=None, tm=None, tn=None, tk=None, comm=None, name):
    if form == "nn":
        (M, K), (K2, N) = a.shape, b.shape
    elif form == "nt":
        (M, K), (N, K2) = a.shape, b.shape
    else:
        (K, M), (K2, N) = a.shape, b.shape
    assert K == K2, (a.shape, b.shape, form)
    tk = tk or (K if K <= 2816 else _pick(K, (1024, 512, 256, 128)))
    nk = K // tk
    if tm is None or tn is None:
        tm, tn = _pick_tiles(M, N, tk, nk, a.dtype.itemsize, b.dtype.itemsize, jnp.dtype(out_dtype).itemsize,
                             addend is not None, tm, tn)
    assert M % tm == 0 and N % tn == 0 and K % tk == 0, (M, N, K, tm, tn, tk)
    dims = _DIMS[form]
    nc = comm.n if comm is not None else 0
    grid = (M // tm, N // tn, nk)

    def body(*refs):
        a_ref, b_ref = refs[:2]
        pos = 2
        add_ref = refs[pos] if addend is not None else None
        pos += addend is not None
        c_in, o_ref, c_out = refs[pos:pos + nc], refs[pos + nc], refs[pos + nc + 1:pos + 2 * nc + 1]
        pos += 2 * nc + 1
        acc_ref = refs[pos] if nk > 1 else None
        c_sems = refs[pos + (nk > 1):]
        if comm is not None:
            ids = [pl.program_id(d) for d in range(3)]

            @pl.when((ids[0] == 0) & (ids[1] == 0) & (ids[2] == 0))
            def _():
                comm.start(c_in, c_out, c_sems)

        def finish(r):
            if add_ref is not None:
                r = r + add_ref[...].astype(F32)
            o_ref[...] = r.astype(o_ref.dtype)

        part = lax.dot_general(a_ref[...].astype(BF16), b_ref[...].astype(BF16), dims, preferred_element_type=F32)
        if nk == 1:
            finish(part)
        else:
            k = pl.program_id(2)

            @pl.when(k == 0)
            def _():
                acc_ref[...] = part

            @pl.when(k > 0)
            def _():
                acc_ref[...] += part

            @pl.when(k == nk - 1)
            def _():
                finish(acc_ref[...])

        if comm is not None:
            @pl.when((ids[0] == grid[0] - 1) & (ids[1] == grid[1] - 1) & (ids[2] == grid[2] - 1))
            def _():
                comm.finish(c_in, c_out, c_sems)

    if form == "nn":
        a_spec = pl.BlockSpec((tm, tk), lambda i, j, k: (i, k))
        b_spec = pl.BlockSpec((tk, tn), lambda i, j, k: (k, j))
    elif form == "nt":
        a_spec = pl.BlockSpec((tm, tk), lambda i, j, k: (i, k))
        b_spec = pl.BlockSpec((tn, tk), lambda i, j, k: (j, k))
    else:
        a_spec = pl.BlockSpec((tk, tm), lambda i, j, k: (k, i))
        b_spec = pl.BlockSpec((tk, tn), lambda i, j, k: (k, j))
    o_spec = pl.BlockSpec((tm, tn), lambda i, j, k: (i, j))
    in_specs = [a_spec, b_spec] + ([o_spec] if addend is not None else [])
    args = (a, b) + ((addend,) if addend is not None else ())
    out_shape = jax.ShapeDtypeStruct((M, N), out_dtype)
    scratch = [pltpu.VMEM((tm, tn), F32)] if nk > 1 else []
    if comm is None:
        return pl.pallas_call(
            body, name=name, grid=grid, in_specs=in_specs, out_specs=o_spec, out_shape=out_shape,
            scratch_shapes=scratch, compiler_params=_cparams(("parallel", "parallel", "arbitrary")),
        )(*args)
    outs = pl.pallas_call(
        body, name=name, grid=grid, in_specs=in_specs + [ANY] * nc, out_specs=[o_spec] + [ANY] * nc,
        out_shape=[out_shape] + comm.out_shapes, scratch_shapes=scratch + comm.scratch,
        compiler_params=_cparams(("arbitrary", "arbitrary", "arbitrary")),
    )(*args, *comm.inputs)
    return outs[0], outs[1:]


def _rms_fwd(x, g, *, name, tm=512):
    M, D = x.shape
    tm = min(tm, M)

    def body(x_ref, g_ref, n_ref):
        xf = x_ref[...]
        r = lax.rsqrt(jnp.mean(xf * xf, axis=-1, keepdims=True) + EPS)
        n_ref[...] = (xf * r * g_ref[...]).astype(n_ref.dtype)

    return pl.pallas_call(
        body, name=name, grid=(M // tm,),
        in_specs=[pl.BlockSpec((tm, D), lambda i: (i, 0)), pl.BlockSpec((1, D), lambda i: (0, 0))],
        out_specs=pl.BlockSpec((tm, D), lambda i: (i, 0)),
        out_shape=jax.ShapeDtypeStruct((M, D), BF16),
        compiler_params=_cparams(("parallel",)),
    )(x, g.reshape(1, D))


def _rms_bwd(x, g, dn, dres, *, name, tm=512):
    M, D = x.shape
    tm = min(tm, M)

    def body(x_ref, g_ref, dn_ref, dres_ref, dx_ref, dg_ref):
        @pl.when(pl.program_id(0) == 0)
        def _():
            dg_ref[...] = jnp.zeros_like(dg_ref)

        xf = x_ref[...]
        r = lax.rsqrt(jnp.mean(xf * xf, axis=-1, keepdims=True) + EPS)
        xh = xf * r
        dn_ = dn_ref[...].astype(F32)
        dg_ref[...] += jnp.sum(dn_ * xh, axis=0, keepdims=True)
        dxh = dn_ * g_ref[...]
        dx = r * (dxh - xh * jnp.mean(dxh * xh, axis=-1, keepdims=True))
        dx_ref[...] = dres_ref[...] + dx

    row = pl.BlockSpec((tm, D), lambda i: (i, 0))
    vec = pl.BlockSpec((1, D), lambda i: (0, 0))
    return pl.pallas_call(
        body, name=name, grid=(M // tm,),
        in_specs=[row, vec, row, row], out_specs=[row, vec],
        out_shape=[jax.ShapeDtypeStruct((M, D), F32), jax.ShapeDtypeStruct((1, D), F32)],
        compiler_params=_cparams(("arbitrary",)),
    )(x, g.reshape(1, D), dn, dres)


def _loss_head(h, g, tgt, *, name, tm=512):
    M, D = h.shape
    tm = min(tm, M)

    def body(h_ref, g_ref, t_ref, loss_ref, dh_ref, dg_ref):
        @pl.when(pl.program_id(0) == 0)
        def _():
            dg_ref[...] = jnp.zeros_like(dg_ref)
            loss_ref[...] = jnp.zeros_like(loss_ref)

        xf = h_ref[...]
        r = lax.rsqrt(jnp.mean(xf * xf, axis=-1, keepdims=True) + EPS)
        xh = xf * r
        err = xh * g_ref[...] - t_ref[...]
        part = jnp.sum(jnp.mean(err * err, axis=-1, keepdims=True), axis=0, keepdims=True)
        loss_ref[...] += 0.5 * part
        dy = err * (1.0 / D)
        dg_ref[...] += jnp.sum(dy * xh, axis=0, keepdims=True)
        dxh = dy * g_ref[...]
        dh_ref[...] = r * (dxh - xh * jnp.mean(dxh * xh, axis=-1, keepdims=True))

    row = pl.BlockSpec((tm, D), lambda i: (i, 0))
    vec = pl.BlockSpec((1, D), lambda i: (0, 0))
    one = pl.BlockSpec((1, 1), lambda i: (0, 0))
    return pl.pallas_call(
        body, name=name, grid=(M // tm,),
        in_specs=[row, vec, row], out_specs=[one, row, vec],
        out_shape=[jax.ShapeDtypeStruct((1, 1), F32), jax.ShapeDtypeStruct((M, D), F32),
                   jax.ShapeDtypeStruct((1, D), F32)],
        compiler_params=_cparams(("arbitrary",)),
    )(h, g.reshape(1, D), tgt)


HG_MID = HG_CHUNK // 2 - 1
EXP_CAP = 80.0


def _sigmoid(x):
    return 1.0 / (1.0 + jnp.exp(-x))


def _dot(a, b, dims, precision=None):
    return lax.dot_general(a, b, dims, preferred_element_type=F32, precision=precision)


def _bdot(a, b, form):
    return _dot(a.astype(BF16), b.astype(BF16), _DIMS[form])


def _split2(x):
    hi = x.astype(BF16)
    return hi, (x - hi.astype(F32)).astype(BF16)


def _dot3(a, b, form):
    d = _DIMS[form]
    return _dot(a[0], b[0], d) + (_dot(a[0], b[1], d) + _dot(a[1], b[0], d))


def _hgrn_chunk_common(hq, hf, lbv, tril, rid):
    sq = _sigmoid(hq)
    q = hq * sq
    sg = _sigmoid(hf)
    f = lbv + (1.0 - lbv) * sg
    k = (1.0 - lbv) * (1.0 - sg)
    g = jnp.log(f)
    b = _dot(tril, g, _DIMS["nn"], precision=lax.Precision.HIGHEST)
    bref = jnp.sum(jnp.where(rid == HG_MID, b, 0.0), axis=0, keepdims=True)
    bend = jnp.sum(jnp.where(rid == HG_CHUNK - 1, b, 0.0), axis=0, keepdims=True)
    eb = jnp.exp(b)
    e1 = jnp.exp(jnp.minimum(b - bref, EXP_CAP))
    e2 = jnp.exp(jnp.minimum(bref - b, EXP_CAP))
    e3 = jnp.exp(bend - b)
    return sq, q, sg, f, k, bend, eb, e1, e2, e3


def _hgrn_fwd(proj, lb, gnorm, *, name, T=1024):
    S = proj.shape[0]
    T = min(T, S)
    nch = T // HG_CHUNK
    C = HG_CHUNK

    def body(hq_ref, hf_ref, hi_ref, hg_ref, lb_ref, gn_ref, o_ref, oa_ref, st_ref, state):
        @pl.when(pl.program_id(1) == 0)
        def _():
            state[...] = jnp.zeros_like(state)

        lbv = lb_ref[...]
        gn = gn_ref[...]
        row = lax.broadcasted_iota(jnp.int32, (C, C), 0)
        col = lax.broadcasted_iota(jnp.int32, (C, C), 1)
        causal = row >= col
        tril = causal.astype(F32)
        rid = lax.broadcasted_iota(jnp.int32, (C, HG_DK), 0)
        sls = [pl.ds(c * C, C) for c in range(nch)]
        pre = [_hgrn_chunk_common(hq_ref[sl, :], hf_ref[sl, :], lbv, tril, rid) for sl in sls]
        v_l = [hi_ref[sl, :].astype(BF16) for sl in sls]
        a_l, u_l = [], []
        for c in range(nch):
            _, q, _, _, k, _, _, e1, e2, e3 = pre[c]
            a_l.append(jnp.where(causal, _bdot(q * e1, k * e2, "nt"), 0.0))
            u_l.append(_bdot(v_l[c], k * e3, "tn"))
        o_l = [_bdot(a_l[c], v_l[c], "nn") for c in range(nch)]
        st = state[...]
        st_l = []
        for c in range(nch):
            st_l.append(st)
            st = st * jnp.exp(pre[c][5]) + u_l[c]
        state[...] = st
        for c in range(nch):
            st_ref[0, c] = st_l[c]
            o_l[c] = o_l[c] + _bdot(pre[c][1] * pre[c][6], st_l[c], "nt")
        for c in range(nch):
            o, hg = o_l[c], hg_ref[sls[c], :]
            o_ref[sls[c], :] = o
            r = lax.rsqrt(jnp.mean(o * o, axis=-1, keepdims=True) + EPS)
            oa_ref[sls[c], :] = (o * r * gn * (hg * _sigmoid(hg))).astype(oa_ref.dtype)

    def grp(gidx):
        return pl.BlockSpec((T, 128), lambda h, t: (t, gidx * 8 + h))

    return pl.pallas_call(
        body, name=name, grid=(HG_HEADS, S // T),
        in_specs=[grp(0), grp(1), grp(2), grp(3),
                  pl.BlockSpec((1, 128), lambda h, t: (0, h)), pl.BlockSpec((1, 128), lambda h, t: (0, 0))],
        out_specs=[pl.BlockSpec((T, 128), lambda h, t: (t, h)), pl.BlockSpec((T, 128), lambda h, t: (t, h)),
                   pl.BlockSpec((1, nch, HG_DV, HG_DK), lambda h, t: (h, t, 0, 0))],
        out_shape=[jax.ShapeDtypeStruct((S, HG_HEADS * HG_DV), F32), jax.ShapeDtypeStruct((S, HG_HEADS * HG_DV), BF16),
                   jax.ShapeDtypeStruct((HG_HEADS, S // C, HG_DV, HG_DK), F32)],
        scratch_shapes=[pltpu.VMEM((HG_DV, HG_DK), F32)],
        compiler_params=_cparams(("parallel", "arbitrary")),
    )(proj, proj, proj, proj, lb, gnorm)


def _hgrn_bwd(proj, lb, gnorm, o, states, doa, *, name, T=1024):
    S = proj.shape[0]
    T = min(T, S)
    nch = T // HG_CHUNK
    C = HG_CHUNK
    nT = S // T

    def body(hq_ref, hf_ref, hi_ref, hg_ref, lb_ref, gn_ref, o_ref, st_ref, doa_ref,
             dhq_ref, dhf_ref, dhi_ref, dhg_ref, dlb_ref, dgn_ref, dstate):
        @pl.when(pl.program_id(1) == 0)
        def _():
            dstate[...] = jnp.zeros_like(dstate)
            dlb_ref[...] = jnp.zeros_like(dlb_ref)
            dgn_ref[...] = jnp.zeros_like(dgn_ref)

        lbv = lb_ref[...]
        gn = gn_ref[...]
        row = lax.broadcasted_iota(jnp.int32, (C, C), 0)
        col = lax.broadcasted_iota(jnp.int32, (C, C), 1)
        causal = row >= col
        tril = causal.astype(F32)
        triu = (row <= col).astype(F32)
        rid = lax.broadcasted_iota(jnp.int32, (C, HG_DK), 0)
        rng = range(nch)
        sls = [pl.ds(c * C, C) for c in rng]
        pre = [_hgrn_chunk_common(hq_ref[sl, :], hf_ref[sl, :], lbv, tril, rid) for sl in sls]
        do2, dgn_acc = [], jnp.zeros((1, HG_DV), F32)
        for c in rng:
            hg, ov = hg_ref[sls[c], :], o_ref[sls[c], :]
            r = lax.rsqrt(jnp.mean(ov * ov, axis=-1, keepdims=True) + EPS)
            xh = ov * r
            sgg = _sigmoid(hg)
            d_oa = doa_ref[sls[c], :].astype(F32)
            dz = d_oa * (hg * sgg)
            dhg_ref[sls[c], :] = (d_oa * (xh * gn) * (sgg * (1.0 + hg * (1.0 - sgg)))).astype(dhg_ref.dtype)
            dgn_acc = dgn_acc + jnp.sum(dz * xh, axis=0, keepdims=True)
            dxh = dz * gn
            do2.append(_split2(r * (dxh - xh * jnp.mean(dxh * xh, axis=-1, keepdims=True))))
        dgn_ref[0] += dgn_acc
        qi = [pre[c][1] * pre[c][6] for c in rng]
        qp = [pre[c][1] * pre[c][7] for c in rng]
        kp = [pre[c][4] * pre[c][8] for c in rng]
        kend = [pre[c][4] * pre[c][9] for c in rng]
        qi2, qp2, kp2, kend2 = ([_split2(t) for t in lst] for lst in (qi, qp, kp, kend))
        v2 = [_split2(hi_ref[sl, :]) for sl in sls]
        st0 = [st_ref[0, c] for c in rng]
        a_l = [jnp.where(causal, _dot(qp2[c][0], kp2[c][0], _DIMS["nt"]), 0.0).astype(BF16) for c in rng]
        da2 = [_split2(jnp.where(causal, _dot3(do2[c], v2[c], "nt"), 0.0)) for c in rng]
        dqi = [_dot3(do2[c], _split2(st0[c]), "nn") for c in rng]
        w_l = [_dot3(do2[c], qi2[c], "tn") for c in rng]
        ds = dstate[...]
        ds1 = [None] * nch
        for c in reversed(rng):
            ds1[c] = ds
            ds = ds * jnp.exp(pre[c][5]) + w_l[c]
        dstate[...] = ds
        ds12 = [_split2(t) for t in ds1]
        dqp = [_dot3(da2[c], kp2[c], "nn") for c in rng]
        dkp = [_dot3(da2[c], qp2[c], "tn") for c in rng]
        dv = [_dot(a_l[c], do2[c][0], _DIMS["tn"]) + _dot(kend2[c][0], ds12[c][0], _DIMS["nt"]) for c in rng]
        dkend = [_dot3(v2[c], ds12[c], "nn") for c in rng]
        dq_l, dk_l, db_l = [], [], []
        for c in rng:
            _, _, _, _, _, bend, eb, e1, e2, e3 = pre[c]
            dq_l.append(dqi[c] * eb + dqp[c] * e1)
            dk_l.append(dkp[c] * e2 + dkend[c] * e3)
            db = dqi[c] * qi[c] + dqp[c] * qp[c] - dkp[c] * kp[c] - dkend[c] * kend[c]
            dbend = (jnp.sum(dkend[c] * kend[c], axis=0, keepdims=True)
                     + jnp.exp(bend) * jnp.sum(ds1[c] * st0[c], axis=0, keepdims=True))
            db_l.append(db + jnp.where(rid == C - 1, dbend, 0.0))
        dg = [_dot(triu, db_l[c], _DIMS["nn"], precision=lax.Precision.HIGHEST) for c in rng]
        dlb_acc = jnp.zeros((1, HG_DK), F32)
        for c in rng:
            sq, _, sg, f, _, _, _, _, _, _ = pre[c]
            hq = hq_ref[sls[c], :]
            df = dg[c] / f - dk_l[c]
            dlb_acc = dlb_acc + jnp.sum(df * (1.0 - sg), axis=0, keepdims=True)
            dhf_ref[sls[c], :] = (df * (1.0 - lbv) * sg * (1.0 - sg)).astype(dhf_ref.dtype)
            dhq_ref[sls[c], :] = (dq_l[c] * (sq * (1.0 + hq * (1.0 - sq)))).astype(dhq_ref.dtype)
            dhi_ref[sls[c], :] = dv[c].astype(dhi_ref.dtype)
        dlb_ref[...] += dlb_acc

    def grp(gidx):
        return pl.BlockSpec((T, 128), lambda h, t: (nT - 1 - t, gidx * 8 + h))

    tok = pl.BlockSpec((T, 128), lambda h, t: (nT - 1 - t, h))
    big = jax.ShapeDtypeStruct((S, HG_HEADS * HG_DV), BF16)
    return pl.pallas_call(
        body, name=name, grid=(HG_HEADS, nT),
        in_specs=[grp(0), grp(1), grp(2), grp(3),
                  pl.BlockSpec((1, 128), lambda h, t: (0, h)), pl.BlockSpec((1, 128), lambda h, t: (0, 0)),
                  tok, pl.BlockSpec((1, nch, HG_DV, HG_DK), lambda h, t: (h, nT - 1 - t, 0, 0)), tok],
        out_specs=[tok, tok, tok, tok, pl.BlockSpec((1, 128), lambda h, t: (0, h)),
                   pl.BlockSpec((1, 1, 128), lambda h, t: (h, 0, 0))],
        out_shape=[big, big, big, big, jax.ShapeDtypeStruct((1, HG_HEADS * HG_DK), F32),
                   jax.ShapeDtypeStruct((HG_HEADS, 1, HG_DV), F32)],
        scratch_shapes=[pltpu.VMEM((HG_DV, HG_DK), F32)],
        compiler_params=_cparams(("parallel", "arbitrary")),
    )(proj, proj, proj, proj, lb, gnorm, o, states, doa)


def _lb_fwd(logits, *, name):
    def body(l_ref, lb_ref):
        lb_ref[...] = _sigmoid(l_ref[0:1, :] - l_ref[1:2, :])

    return pl.pallas_call(body, name=name, out_shape=jax.ShapeDtypeStruct((1, logits.shape[1]), F32))(logits)


def _lb_bwd(logits, dlb, *, name):
    def body(l_ref, d_ref, o_ref):
        lbv = _sigmoid(l_ref[0:1, :] - l_ref[1:2, :])
        t = d_ref[...] * lbv * (1.0 - lbv)
        o_ref[0:1, :] = t
        o_ref[1:2, :] = -t

    return pl.pallas_call(body, name=name, out_shape=jax.ShapeDtypeStruct(logits.shape, F32))(logits, dlb)


NEG = -1e30
FOX_SCALE = FOX_DH ** -0.5
FOX_PAIRS = FOX_HEADS // 2


def _fox_gate_fwd(ff, bias, *, name, T=512):
    S = ff.shape[0]
    T = min(T, S)

    def body(ff_ref, b_ref, c_ref, carry):
        @pl.when(pl.program_id(0) == 0)
        def _():
            carry[...] = jnp.zeros_like(carry)

        z = ff_ref[...] + b_ref[...]
        logf = jnp.minimum(z, 0.0) - jnp.log(1.0 + jnp.exp(-jnp.abs(z)))
        row = lax.broadcasted_iota(jnp.int32, (T, T), 0)
        col = lax.broadcasted_iota(jnp.int32, (T, T), 1)
        c = _dot((row >= col).astype(F32), logf, _DIMS["nn"], precision=lax.Precision.HIGHEST) + carry[...]
        c_ref[...] = c
        carry[...] = c[T - 1:T, :]

    return pl.pallas_call(
        body, name=name, grid=(S // T,),
        in_specs=[pl.BlockSpec((T, 128), lambda i: (i, 0)), pl.BlockSpec((1, 128), lambda i: (0, 0))],
        out_specs=pl.BlockSpec((T, 128), lambda i: (i, 0)),
        out_shape=jax.ShapeDtypeStruct((S, 128), F32),
        scratch_shapes=[pltpu.VMEM((1, 128), F32)],
        compiler_params=_cparams(("arbitrary",)),
    )(ff, bias)


def _fox_gate_bwd(ff, bias, dcs, *, name, T=512):
    S = ff.shape[0]
    T = min(T, S)
    nT = S // T

    def body(ff_ref, b_ref, d_ref, dff_ref, db_ref, carry):
        @pl.when(pl.program_id(0) == 0)
        def _():
            carry[...] = jnp.zeros_like(carry)
            db_ref[...] = jnp.zeros_like(db_ref)

        row = lax.broadcasted_iota(jnp.int32, (T, T), 0)
        col = lax.broadcasted_iota(jnp.int32, (T, T), 1)
        dlogf = carry[...] - _dot((row <= col).astype(F32), d_ref[...], _DIMS["nn"], precision=lax.Precision.HIGHEST)
        carry[...] = dlogf[0:1, :]
        dff = dlogf * (1.0 - _sigmoid(ff_ref[...] + b_ref[...]))
        dff_ref[...] = dff.astype(dff_ref.dtype)
        db_ref[...] += jnp.sum(dff, axis=0, keepdims=True)

    rev = pl.BlockSpec((T, 128), lambda i: (nT - 1 - i, 0))
    vec = pl.BlockSpec((1, 128), lambda i: (0, 0))
    return pl.pallas_call(
        body, name=name, grid=(nT,),
        in_specs=[rev, vec, rev], out_specs=[rev, vec],
        out_shape=[jax.ShapeDtypeStruct((S, 128), BF16), jax.ShapeDtypeStruct((1, 128), F32)],
        scratch_shapes=[pltpu.VMEM((1, 128), F32)],
        compiler_params=_cparams(("arbitrary",)),
    )(ff, bias, dcs)


AUG = FOX_DH
RSUM_LANE = 6


def _bias_lane(hh):
    return AUG * (1 - hh)


def _data_lanes(lane, hh):
    return (lane < AUG) if hh == 0 else (lane >= AUG)


def _split3(x):
    a = x.astype(BF16).astype(F32)
    r = x - a
    b = r.astype(BF16).astype(F32)
    return a, b, r - b


def _lane_fill(lane, base, pieces, start):
    for i, pc in enumerate(pieces):
        base = jnp.where(lane == start + i, pc, base)
    return base


FOX_TB = 512
FOX_SKIP = 32.0
N_STAT = 4


def _fox_prep(proj, c_tok, *, name):
    S = proj.shape[0]
    T = min(FOX_TB, S)

    def body(q_ref, k_ref, v_ref, c_ref, qa_ref, ka_ref, va_ref, st_ref):
        pair = pl.program_id(0)
        lane = lax.broadcasted_iota(jnp.int32, (T, 128), 1)
        lane1 = lax.broadcasted_iota(jnp.int32, (1, 128), 1)
        c = c_ref[...]
        q, k, v = q_ref[...], k_ref[...], v_ref[...]
        for hh in range(2):
            data, b0 = _data_lanes(lane, hh), _bias_lane(hh)
            ones3 = jnp.where((lane >= b0) & (lane < b0 + 3), 1.0, 0.0)

            def max_norm(t):
                tr = jnp.where(data, t.astype(BF16).astype(F32), 0.0)
                return jnp.sqrt(jnp.max(jnp.sum(tr * tr, axis=-1, keepdims=True), axis=0, keepdims=True))

            ch = jnp.sum(jnp.where(lane == 2 * pair + hh, c, 0.0), axis=-1, keepdims=True)
            c1, c2, c3 = _split3(ch)
            aug_q = _lane_fill(lane, jnp.where((lane >= b0 + 3) & (lane < b0 + 6), 1.0, 0.0), (c1, c2, c3), b0)
            aug_k = _lane_fill(lane, ones3, (-c1, -c2, -c3), b0 + 3)
            qa_ref[hh] = jnp.where(data, q * FOX_SCALE, aug_q).astype(BF16)
            ka_ref[hh] = jnp.where(data, k, aug_k).astype(BF16)
            va_ref[hh] = jnp.where(data, v, ones3).astype(BF16)
            stats = (max_norm(q * FOX_SCALE), jnp.max(ch, axis=0, keepdims=True), max_norm(k),
                     jnp.min(ch, axis=0, keepdims=True))
            st_ref[hh, 0] = _lane_fill(lane1, jnp.zeros((1, 128), F32), stats, 0)

    def grp(g):
        return pl.BlockSpec((T, 128), lambda p, t: (t, g * 8 + p))

    hm = pl.BlockSpec((2, T, 128), lambda p, t: (p, t, 0))
    out = jax.ShapeDtypeStruct((FOX_HEADS, S, 128), BF16)
    return pl.pallas_call(
        body, name=name, grid=(FOX_PAIRS, S // T),
        in_specs=[grp(4), grp(5), grp(6), pl.BlockSpec((T, 128), lambda p, t: (t, 0))],
        out_specs=[hm, hm, hm, pl.BlockSpec((2, 1, 1, 128), lambda p, t: (p, t, 0, 0))],
        out_shape=[out, out, out, jax.ShapeDtypeStruct((FOX_HEADS, S // T, 1, 128), F32)],
        compiler_params=_cparams(("parallel", "parallel")),
    )(proj, proj, proj, c_tok)


def _fox_bound(st_ref, head, nb, qi, ki):
    qb_, kb_ = (head * nb + qi) * N_STAT, (head * nb + ki) * N_STAT
    return st_ref[qb_] * st_ref[kb_ + 2] + st_ref[qb_ + 1] - st_ref[kb_ + 3] + 0.01


def _pair_lanes(lane, a0, a1):
    return jnp.where(lane < AUG, a0, a1)


def _first_live_key(st_ref, head, nb, qi, newest, thr):
    def body(t, k0):
        k = newest - t
        return jnp.where(_fox_bound(st_ref, head, nb, qi, k) > thr, k, k0)

    return lax.fori_loop(0, newest + 1, body, newest + 1)


def _last_live_query(st_ref, lm_ref, head, nb, ki):
    def body(t, i1):
        i = ki + 1 + t
        live = _fox_bound(st_ref, head, nb, i, ki) > lm_ref[head * nb + i] - FOX_SKIP
        return jnp.where(live, i, i1)

    return lax.fori_loop(0, nb - 1 - ki, body, ki)


class _BlockStream:
    def __init__(self, hbm_refs, bufs, sems, pair, tb):
        self.hbm, self.bufs, self.sems, self.pair, self.tb = hbm_refs, bufs, sems, pair, tb

    def _copies(self, blk, slot):
        rows = pl.ds(pl.multiple_of(blk * self.tb, self.tb), self.tb)
        return [pltpu.make_async_copy(h.at[pl.ds(2 * self.pair, 2), rows, :], b.at[slot], self.sems.at[n, slot])
                for n, (h, b) in enumerate(zip(self.hbm, self.bufs))]

    def start(self, blk, slot):
        for cp in self._copies(blk, slot):
            cp.start()

    def wait(self, blk, slot):
        for cp in self._copies(blk, slot):
            cp.wait()


def _fox_fwd(qa, ka, va, bounds, *, name):
    S = qa.shape[1]
    tb = min(FOX_TB, S)
    nb = S // tb

    def body(qa_ref, ka_hbm, va_hbm, st_ref, o_ref, qb_ref, lse_ref, kbuf, vbuf, sems, m_s, acc_s, m_min):
        pair, qi = pl.program_id(0), pl.program_id(1)
        stream = _BlockStream((ka_hbm, va_hbm), (kbuf, vbuf), sems, pair, tb)

        def head_step(hh, slot, masked, paired=True):
            s = _dot(qa_ref[hh], kbuf[slot, hh], _DIMS["nt"])
            if masked:
                row = lax.broadcasted_iota(jnp.int32, (tb, tb), 0)
                col = lax.broadcasted_iota(jnp.int32, (tb, tb), 1)
                s = jnp.where(col <= row, s, NEG)
            m_old = m_s[hh]
            m_new = jnp.maximum(m_old, jnp.max(s, axis=-1, keepdims=True))
            p = jnp.exp(s - m_new)
            p_hi = p.astype(BF16)
            vv = vbuf[slot, hh]
            if paired:
                p_lo = (p - p_hi.astype(F32)).astype(BF16)
                acc_s[hh] = (jnp.exp(m_old - m_new) * acc_s[hh]
                             + _dot(p_hi, vv, _DIMS["nn"]) + _dot(p_lo, vv, _DIMS["nn"]))
            else:
                acc_s[hh] = jnp.exp(m_old - m_new) * acc_s[hh] + _dot(p_hi, vv, _DIMS["nn"])
            m_s[hh] = m_new
            m_min[hh] = jnp.min(m_new)

        @pl.when(qi == 0)
        def _():
            stream.start(qi, 0)

        @pl.when(qi > 0)
        def _():
            stream.start(qi - 1, 1)

        m_s[...] = jnp.full_like(m_s, NEG)
        acc_s[...] = jnp.zeros_like(acc_s)
        stream.wait(qi, 0)
        for hh in range(2):
            head_step(hh, 0, True)

        @pl.when(qi > 1)
        def _():
            stream.start(qi - 2, 0)

        @pl.when(qi > 0)
        def _():
            stream.wait(qi - 1, 1)
            for hh in range(2):
                head_step(hh, 1, False)

        k0 = [_first_live_key(st_ref, 2 * pair + hh, nb, qi, qi - 2, m_min[hh] - FOX_SKIP) for hh in range(2)]
        n = qi - 1 - jnp.minimum(k0[0], k0[1])

        @pl.when((qi > 1) & (n == 0))
        def _():
            stream.wait(qi - 2, 0)

        def loop(t, carry):
            k = qi - 2 - t
            slot = t % 2
            stream.wait(k, slot)

            @pl.when(t + 1 < n)
            def _():
                stream.start(k - 1, 1 - slot)

            live = [k >= k0[hh] for hh in range(2)]

            @pl.when(live[0] & live[1])
            def _():
                for hh in range(2):
                    head_step(hh, slot, False)

            for hh in range(2):
                @pl.when(live[hh] & jnp.logical_not(live[1 - hh]))
                def _():
                    head_step(hh, slot, False, paired=False)
            return carry

        lax.fori_loop(0, n, loop, 0)

        @pl.when(qi + 1 < nb)
        def _():
            stream.start(qi + 1, 0)

        lane = lax.broadcasted_iota(jnp.int32, (tb, 128), 1)
        outs = []
        for hh in range(2):
            acc = acc_s[hh]
            b0 = _bias_lane(hh)
            l = acc[:, b0:b0 + 1]
            outs.append(acc / l)
            lse = m_s[hh] + jnp.log(l)
            lse_ref[hh, 0] = jnp.broadcast_to(jnp.min(lse, axis=0, keepdims=True), (1, 128))
            qf = qa_ref[hh].astype(F32)
            c_t = jnp.sum(jnp.where((lane >= b0) & (lane < b0 + 3), qf, 0.0), axis=-1, keepdims=True)
            cb = jnp.broadcast_to(c_t - lse, (tb, 128))
            qb_ref[hh] = _lane_fill(lane, qf, _split3(cb), b0).astype(BF16)
        o_ref[...] = _pair_lanes(lane, outs[0], outs[1])

    qs = pl.BlockSpec((2, tb, 128), lambda p, i: (p, i, 0))
    return pl.pallas_call(
        body, name=name, grid=(FOX_PAIRS, nb),
        in_specs=[qs, ANY, ANY, SMEM],
        out_specs=[pl.BlockSpec((tb, 128), lambda p, i: (i, p)), qs,
                   pl.BlockSpec((2, 1, 1, 128), lambda p, i: (p, i, 0, 0))],
        out_shape=[jax.ShapeDtypeStruct((S, FOX_HEADS * FOX_DH), F32), jax.ShapeDtypeStruct((FOX_HEADS, S, 128), BF16),
                   jax.ShapeDtypeStruct((FOX_HEADS, nb, 1, 128), F32)],
        scratch_shapes=[pltpu.VMEM((2, 2, tb, 128), BF16), pltpu.VMEM((2, 2, tb, 128), BF16),
                        pltpu.SemaphoreType.DMA((2, 2)), pltpu.VMEM((2, tb, 1), F32), pltpu.VMEM((2, tb, 128), F32),
                        pltpu.SMEM((2,), F32)],
        compiler_params=_cparams(("arbitrary", "arbitrary")),
    )(qa, ka, va, bounds)


def _fox_bwd_prep(o, do, *, name, T=512):
    S = o.shape[0]
    T = min(T, S)

    def body(o_ref, do_ref, dob_ref):
        lane = lax.broadcasted_iota(jnp.int32, (T, 128), 1)
        d = do_ref[...].astype(F32)
        prod = d * o_ref[...]
        for hh in range(2):
            mine = _data_lanes(lane, hh)
            delta = jnp.sum(jnp.where(mine, prod, 0.0), axis=-1, keepdims=True)
            dob_ref[hh] = _lane_fill(lane, jnp.where(mine, d, 0.0), _split3(-delta), _bias_lane(hh)).astype(BF16)

    tok = pl.BlockSpec((T, 128), lambda p, t: (t, p))
    return pl.pallas_call(
        body, name=name, grid=(FOX_PAIRS, S // T),
        in_specs=[tok, tok], out_specs=pl.BlockSpec((2, T, 128), lambda p, t: (p, t, 0)),
        out_shape=jax.ShapeDtypeStruct((FOX_HEADS, S, 128), BF16),
        compiler_params=_cparams(("parallel", "parallel")),
    )(o, do)


def _fox_bwd_dq(qb, ka, va, dob, bounds, lse_min, *, name, comm=None):
    S = qb.shape[1]
    tb = min(FOX_TB, S)
    nb = S // tb
    nc = comm.n if comm is not None else 0

    def body(qb_ref, dob_ref, ka_hbm, va_hbm, st_ref, lm_ref, *rest):
        c_in, (dq_ref, dob2_ref), c_out = rest[:nc], rest[nc:nc + 2], rest[nc + 2:2 * nc + 2]
        kbuf, vbuf, sems, acc_s = rest[2 * nc + 2:2 * nc + 6]
        c_sems = rest[2 * nc + 6:]
        pair, qi = pl.program_id(0), pl.program_id(1)
        if comm is not None:
            @pl.when((pair == 0) & (qi == 0))
            def _():
                comm.start(c_in, c_out, c_sems)

        stream = _BlockStream((ka_hbm, va_hbm), (kbuf, vbuf), sems, pair, tb)
        k0 = [_first_live_key(st_ref, 2 * pair + hh, nb, qi, qi - 1, lm_ref[(2 * pair + hh) * nb + qi] - FOX_SKIP)
              for hh in range(2)]
        n = qi - jnp.minimum(k0[0], k0[1]) + 1

        @pl.when(qi == 0)
        def _():
            stream.start(qi, 0)

        acc_s[...] = jnp.zeros_like(acc_s)

        def head_step(hh, slot, k, masked):
            s = _dot(qb_ref[hh], kbuf[slot, hh], _DIMS["nt"])
            if masked:
                row = lax.broadcasted_iota(jnp.int32, (tb, tb), 0)
                col = lax.broadcasted_iota(jnp.int32, (tb, tb), 1)
                s = jnp.where(col <= row, s, NEG)
            ds = jnp.exp(s) * _dot(dob_ref[hh], vbuf[slot, hh], _DIMS["nt"])
            acc_s[hh] += _dot(ds.astype(BF16), kbuf[slot, hh], _DIMS["nn"])

        def loop(t, carry):
            k = qi - t
            slot = t % 2
            stream.wait(k, slot)

            @pl.when(t + 1 < n)
            def _():
                stream.start(k - 1, 1 - slot)

            @pl.when(t == 0)
            def _():
                for hh in range(2):
                    head_step(hh, slot, k, True)

            live = [(t > 0) & (k >= k0[hh]) for hh in range(2)]

            @pl.when(live[0] & live[1])
            def _():
                for hh in range(2):
                    head_step(hh, slot, k, False)

            for hh in range(2):
                @pl.when(live[hh] & jnp.logical_not(live[1 - hh]))
                def _():
                    head_step(hh, slot, k, False)
            return carry

        lax.fori_loop(0, n, loop, 0)

        @pl.when(qi + 1 < nb)
        def _():
            stream.start(qi + 1, 0)

        lane = lax.broadcasted_iota(jnp.int32, (tb, 128), 1)
        dq_ref[...] = (_pair_lanes(lane, acc_s[0], acc_s[1]) * FOX_SCALE).astype(dq_ref.dtype)
        for hh in range(2):
            b0 = _bias_lane(hh)
            r = jnp.broadcast_to(acc_s[hh][:, b0:b0 + 1], (tb, 128))
            dob2_ref[hh] = _lane_fill(lane, dob_ref[hh].astype(F32), _split3(r), b0 + RSUM_LANE).astype(BF16)
        if comm is not None:
            @pl.when((pair == FOX_PAIRS - 1) & (qi == nb - 1))
            def _():
                comm.finish(c_in, c_out, c_sems)

    qs = pl.BlockSpec((2, tb, 128), lambda p, i: (p, i, 0))
    outs = pl.pallas_call(
        body, name=name, grid=(FOX_PAIRS, nb),
        in_specs=[qs, qs, ANY, ANY, SMEM, SMEM] + [ANY] * nc,
        out_specs=[pl.BlockSpec((tb, 128), lambda p, i: (i, p)), qs] + [ANY] * nc,
        out_shape=[jax.ShapeDtypeStruct((S, FOX_HEADS * FOX_DH), BF16),
                   jax.ShapeDtypeStruct((FOX_HEADS, S, 128), BF16)] + (comm.out_shapes if comm is not None else []),
        scratch_shapes=[pltpu.VMEM((2, 2, tb, 128), BF16), pltpu.VMEM((2, 2, tb, 128), BF16),
                        pltpu.SemaphoreType.DMA((2, 2)), pltpu.VMEM((2, tb, 128), F32)]
        + (comm.scratch if comm is not None else []),
        compiler_params=_cparams(("arbitrary", "arbitrary")),
    )(qb, dob, ka, va, bounds, lse_min, *(comm.inputs if comm is not None else []))
    return (outs[0], outs[1]) if comm is None else (outs[0], outs[1], outs[2:])


def _fox_bwd_dkv(qb, ka, va, dob, bounds, lse_min, *, name):
    S = qb.shape[1]
    tb = min(FOX_TB, S)
    nb = S // tb

    def body(ka_ref, va_ref, qb_hbm, dob_hbm, st_ref, lm_ref, dk_ref, dv_ref, dcs_ref, qbuf, dbuf, sems, dk_s, dv_s):
        pair, ki = pl.program_id(0), pl.program_id(1)
        stream = _BlockStream((qb_hbm, dob_hbm), (qbuf, dbuf), sems, pair, tb)
        i1 = [_last_live_query(st_ref, lm_ref, 2 * pair + hh, nb, ki) for hh in range(2)]
        n = jnp.maximum(i1[0], i1[1]) - ki + 1

        @pl.when(ki == 0)
        def _():
            stream.start(ki, 0)

        dk_s[...] = jnp.zeros_like(dk_s)
        dv_s[...] = jnp.zeros_like(dv_s)

        def head_step(hh, slot, masked):
            st = _dot(ka_ref[hh], qbuf[slot, hh], _DIMS["nt"])
            if masked:
                row = lax.broadcasted_iota(jnp.int32, (tb, tb), 0)
                col = lax.broadcasted_iota(jnp.int32, (tb, tb), 1)
                st = jnp.where(row <= col, st, NEG)
            pt = jnp.exp(st)
            dst = pt * _dot(va_ref[hh], dbuf[slot, hh], _DIMS["nt"])
            dv_s[hh] += _dot(pt.astype(BF16), dbuf[slot, hh], _DIMS["nn"])
            dk_s[hh] += _dot(dst.astype(BF16), qbuf[slot, hh], _DIMS["nn"])

        def loop(t, carry):
            i = ki + t
            slot = t % 2
            stream.wait(i, slot)

            @pl.when(t + 1 < n)
            def _():
                stream.start(i + 1, 1 - slot)

            @pl.when(t == 0)
            def _():
                for hh in range(2):
                    head_step(hh, slot, True)

            live = [(t > 0) & (i <= i1[hh]) for hh in range(2)]

            @pl.when(live[0] & live[1])
            def _():
                for hh in range(2):
                    head_step(hh, slot, False)

            for hh in range(2):
                @pl.when(live[hh] & jnp.logical_not(live[1 - hh]))
                def _():
                    head_step(hh, slot, False)
            return carry

        lax.fori_loop(0, n, loop, 0)

        @pl.when(ki + 1 < nb)
        def _():
            stream.start(ki + 1, 0)

        lane = lax.broadcasted_iota(jnp.int32, (tb, 128), 1)
        dk_ref[...] = _pair_lanes(lane, dk_s[0], dk_s[1]).astype(dk_ref.dtype)
        dv_ref[...] = _pair_lanes(lane, dv_s[0], dv_s[1]).astype(dv_ref.dtype)
        for hh in range(2):
            b0 = _bias_lane(hh)
            dk_a, dv_a = dk_s[hh], dv_s[hh]
            off = dv_a[:, b0 + RSUM_LANE:b0 + RSUM_LANE + 1] + dv_a[:, b0 + RSUM_LANE + 1:b0 + RSUM_LANE + 2] \
                + dv_a[:, b0 + RSUM_LANE + 2:b0 + RSUM_LANE + 3]
            dcs_ref[0, :, hh:hh + 1] = dk_a[:, b0 + 3:b0 + 4] - off

    ks = pl.BlockSpec((2, tb, 128), lambda p, j: (p, j, 0))
    tok = pl.BlockSpec((tb, 128), lambda p, j: (j, p))
    big = jax.ShapeDtypeStruct((S, FOX_HEADS * FOX_DH), BF16)
    return pl.pallas_call(
        body, name=name, grid=(FOX_PAIRS, nb),
        in_specs=[ks, ks, ANY, ANY, SMEM, SMEM],
        out_specs=[tok, tok, pl.BlockSpec((1, tb, 2), lambda p, j: (p, j, 0))],
        out_shape=[big, big, jax.ShapeDtypeStruct((FOX_PAIRS, S, 2), F32)],
        scratch_shapes=[pltpu.VMEM((2, 2, tb, 128), BF16), pltpu.VMEM((2, 2, tb, 128), BF16),
                        pltpu.SemaphoreType.DMA((2, 2)), pltpu.VMEM((2, tb, 128), F32), pltpu.VMEM((2, tb, 128), F32)],
        compiler_params=_cparams(("arbitrary", "arbitrary")),
    )(ka, va, qb, dob, bounds, lse_min)


def _merge_fwd(proj, pa, pb, *, name, T=512):
    S, D = pa.shape
    T = min(T, S)

    def body(ga_ref, gb_ref, pa_ref, pb_ref, m_ref):
        m_ref[...] = (_sigmoid(ga_ref[...]) * pa_ref[...] + _sigmoid(gb_ref[...]) * pb_ref[...]).astype(m_ref.dtype)

    tok = pl.BlockSpec((T, D), lambda i: (i, 0))
    return pl.pallas_call(
        body, name=name, grid=(S // T,),
        in_specs=[pl.BlockSpec((T, D), lambda i: (i, 7)), pl.BlockSpec((T, D), lambda i: (i, 8)), tok, tok],
        out_specs=tok, out_shape=jax.ShapeDtypeStruct((S, D), BF16),
        compiler_params=_cparams(("parallel",)),
    )(proj, proj, pa, pb)


def _merge_bwd(proj, pa, pb, dm, *, name, T=512):
    S, D = pa.shape
    T = min(T, S)

    def body(ga_ref, gb_ref, pa_ref, pb_ref, dm_ref, dpa_ref, dpb_ref, dga_ref, dgb_ref):
        dm_ = dm_ref[...]
        sa, sb = _sigmoid(ga_ref[...]), _sigmoid(gb_ref[...])
        dpa_ref[...] = (dm_ * sa).astype(BF16)
        dpb_ref[...] = (dm_ * sb).astype(BF16)
        dga_ref[...] = (dm_ * pa_ref[...] * sa * (1.0 - sa)).astype(BF16)
        dgb_ref[...] = (dm_ * pb_ref[...] * sb * (1.0 - sb)).astype(BF16)

    tok = pl.BlockSpec((T, D), lambda i: (i, 0))
    big = jax.ShapeDtypeStruct((S, D), BF16)
    return pl.pallas_call(
        body, name=name, grid=(S // T,),
        in_specs=[pl.BlockSpec((T, D), lambda i: (i, 7)), pl.BlockSpec((T, D), lambda i: (i, 8)), tok, tok, tok],
        out_specs=[tok, tok, tok, tok], out_shape=[big, big, big, big],
        compiler_params=_cparams(("parallel",)),
    )(proj, proj, pa, pb, dm)


INV_SQRT2 = 0.7071067811865476
INV_SQRT2PI = 0.3989422804014327


def _shifted(u, prev, rid):
    m1 = jnp.where(rid == 0, prev[7:8, :], pltpu.roll(u, 1, 0))
    m2 = jnp.where(rid == 0, prev[6:7, :], jnp.where(rid == 1, prev[7:8, :], pltpu.roll(u, 2, 0)))
    return m1, m2


def _conv_acc(u, prev, w_ref, b_ref, rid):
    m1, m2 = _shifted(u, prev, rid)
    return b_ref[...] + w_ref[0:1, :] * m2 + w_ref[1:2, :] * m1 + w_ref[2:3, :] * u, m1, m2


def _convglu_fwd(ug, uv, wg, wv, bg, bv, *, name, T=512, tc=256):
    S, F = ug.shape
    T = min(T, S)

    def body(ug_ref, uv_ref, wg_ref, wv_ref, bg_ref, bv_ref, a_ref, pg, pv):
        @pl.when(pl.program_id(1) == 0)
        def _():
            pg[...] = jnp.zeros_like(pg)
            pv[...] = jnp.zeros_like(pv)

        rid = lax.broadcasted_iota(jnp.int32, (T, tc), 0)
        g_, v_ = ug_ref[...], uv_ref[...]
        accg, _, _ = _conv_acc(g_, pg[...], wg_ref, bg_ref, rid)
        accv, _, _ = _conv_acc(v_, pv[...], wv_ref, bv_ref, rid)
        gel = 0.5 * accg * (1.0 + lax.erf(accg * INV_SQRT2))
        a_ref[...] = (gel * accv).astype(a_ref.dtype)
        pg[...] = g_[T - 8:T, :]
        pv[...] = v_[T - 8:T, :]

    tok = pl.BlockSpec((T, tc), lambda j, t: (t, j))
    w3 = pl.BlockSpec((3, tc), lambda j, t: (0, j))
    b1 = pl.BlockSpec((1, tc), lambda j, t: (0, j))
    return pl.pallas_call(
        body, name=name, grid=(F // tc, S // T),
        in_specs=[tok, tok, w3, w3, b1, b1], out_specs=tok,
        out_shape=jax.ShapeDtypeStruct((S, F), BF16),
        scratch_shapes=[pltpu.VMEM((8, tc), F32), pltpu.VMEM((8, tc), F32)],
        compiler_params=_cparams(("parallel", "arbitrary")),
    )(ug, uv, wg, wv, bg, bv)


def _convglu_bwd(ug, uv, wg, wv, bg, bv, da, *, name, T=512, tc=256):
    S, F = ug.shape
    T = min(T, S)
    nT = S // T
    halo_blocks = T // 8

    def up_shift(d, nx, rid):
        p1 = jnp.where(rid == T - 1, nx[0:1, :], pltpu.roll(d, T - 1, 0))
        p2 = jnp.where(rid == T - 1, nx[1:2, :], jnp.where(rid == T - 2, nx[0:1, :], pltpu.roll(d, T - 2, 0)))
        return p1, p2

    def body(ug_ref, uv_ref, hg_ref, hv_ref, wg_ref, wv_ref, bg_ref, bv_ref, da_ref,
             dug_ref, duv_ref, dwg_ref, dwv_ref, dbg_ref, dbv_ref, ng, nv):
        @pl.when(pl.program_id(1) == 0)
        def _():
            ng[...] = jnp.zeros_like(ng)
            nv[...] = jnp.zeros_like(nv)
            for r in (dwg_ref, dwv_ref, dbg_ref, dbv_ref):
                r[...] = jnp.zeros_like(r)

        first_block = pl.program_id(1) == nT - 1
        rid = lax.broadcasted_iota(jnp.int32, (T, tc), 0)
        g_, v_ = ug_ref[...], uv_ref[...]
        pg = jnp.where(first_block, 0.0, hg_ref[...])
        pv = jnp.where(first_block, 0.0, hv_ref[...])
        accg, g1, g2 = _conv_acc(g_, pg, wg_ref, bg_ref, rid)
        accv, v1, v2 = _conv_acc(v_, pv, wv_ref, bv_ref, rid)
        cdf = 0.5 * (1.0 + lax.erf(accg * INV_SQRT2))
        pdf = INV_SQRT2PI * jnp.exp(-0.5 * accg * accg)
        da_ = da_ref[...].astype(F32)
        dgate = da_ * accv * (cdf + accg * pdf)
        dval = da_ * (accg * cdf)
        dbg_ref[...] += jnp.sum(dgate, axis=0, keepdims=True)
        dbv_ref[...] += jnp.sum(dval, axis=0, keepdims=True)
        for j, (sg_, sv_) in enumerate(((g2, v2), (g1, v1), (g_, v_))):
            dwg_ref[j:j + 1, :] += jnp.sum(dgate * sg_, axis=0, keepdims=True)
            dwv_ref[j:j + 1, :] += jnp.sum(dval * sv_, axis=0, keepdims=True)
        for d, w_ref, nx, out_ref in ((dgate, wg_ref, ng, dug_ref), (dval, wv_ref, nv, duv_ref)):
            p1, p2 = up_shift(d, nx[...], rid)
            out_ref[...] = (w_ref[2:3, :] * d + w_ref[1:2, :] * p1 + w_ref[0:1, :] * p2).astype(out_ref.dtype)
            nx[...] = d[0:8, :]

    tok = pl.BlockSpec((T, tc), lambda j, t: (nT - 1 - t, j))
    halo = pl.BlockSpec((8, tc), lambda j, t: (jnp.maximum((nT - 1 - t) * halo_blocks - 1, 0), j))
    w3 = pl.BlockSpec((3, tc), lambda j, t: (0, j))
    b1 = pl.BlockSpec((1, tc), lambda j, t: (0, j))
    big = jax.ShapeDtypeStruct((S, F), BF16)
    return pl.pallas_call(
        body, name=name, grid=(F // tc, nT),
        in_specs=[tok, tok, halo, halo, w3, w3, b1, b1, tok], out_specs=[tok, tok, w3, w3, b1, b1],
        out_shape=[big, big, jax.ShapeDtypeStruct((3, F), F32), jax.ShapeDtypeStruct((3, F), F32),
                   jax.ShapeDtypeStruct((1, F), F32), jax.ShapeDtypeStruct((1, F), F32)],
        scratch_shapes=[pltpu.VMEM((8, tc), F32), pltpu.VMEM((8, tc), F32)],
        compiler_params=_cparams(("parallel", "arbitrary")),
    )(ug, uv, ug, uv, wg, wv, bg, bv, da)


FF_LO = 7168
IN_SHARD = 1154
FF_DEV, FF_OFF = FF_LO // IN_SHARD, FF_LO % IN_SHARD


def _col_blocks(a, width):
    return jnp.stack([a[:, d * width:(d + 1) * width] for d in range(N_DEV)])


def _w_in_blocks(d_wm, d_wff):
    def block(d):
        lo = d * IN_SHARD
        if d < FF_DEV:
            return d_wm[:, lo:lo + IN_SHARD]
        if d > FF_DEV:
            return d_wm[:, lo - FOX_HEADS:lo - FOX_HEADS + IN_SHARD]
        return jnp.concatenate([d_wm[:, lo:FF_LO], d_wff[:, :FOX_HEADS], d_wm[:, FF_LO:lo + IN_SHARD - FOX_HEADS]], axis=1)

    return jnp.stack([block(d) for d in range(N_DEV)])


def _late_weights(g_a, g_b, g_o, g_up, g_cw, g_d):
    wup = jnp.concatenate([g_up[d] for d in range(N_DEV)], axis=1)
    cw = jnp.concatenate([g_cw[d] for d in range(N_DEV)], axis=1)
    return dict(wa=g_a.reshape(D_MODEL, D_MODEL), wb=g_b.reshape(D_MODEL, D_MODEL), wo=g_o.reshape(D_MODEL, D_MODEL),
                wug=wup[:, :D_FF], wuv=wup[:, D_FF:], cwg=cw[:, :D_FF], cwv=cw[:, D_FF:], wd=g_d.reshape(D_FF, D_MODEL))


def _early_grad_blocks(d_wa, d_wb, d_wo, d_wug, d_wuv, d_wd):
    up = jnp.stack([d_wug[:, d * 704:(d + 1) * 704] for d in range(4)]
                   + [d_wuv[:, d * 704:(d + 1) * 704] for d in range(4)])
    return [d_wa.reshape(N_DEV, 128, D_MODEL), d_wb.reshape(N_DEV, 128, D_MODEL), d_wo.reshape(N_DEV, 128, D_MODEL),
            up, d_wd.reshape(N_DEV, 352, D_MODEL)]


def _local_step(x, tgt, w, p, late=None, exchange=False):
    S = x.shape[0]
    mm = _matmul
    n1 = _rms_fwd(x, p["norm_mix"], name="rms1_fwd")
    if late is None:
        proj = mm(n1, w["wm"], "nn", name="proj_main")
    else:
        proj, gathered = mm(n1, w["wm"], "nn", comm=late, name="proj_main")
        w = {**w, **_late_weights(*gathered)}
    ff = mm(n1, w["wff"], "nn", name="proj_ff")
    lb = _lb_fwd(p["hg_lb_logits"], name="lb_fwd")
    gnorm = p["hg_norm"].reshape(1, HG_DV)
    o_hg, oa, states = _hgrn_fwd(proj, lb, gnorm, name="hgrn_fwd")
    bias = jnp.pad(p["fox_f_bias"].reshape(1, FOX_HEADS), ((0, 0), (0, 128 - FOX_HEADS)))
    c = _fox_gate_fwd(ff, bias, name="fox_gate_fwd")
    qa, ka, va, fox_stats = _fox_prep(proj, c, name="fox_prep")
    bounds = fox_stats[:, :, 0, :N_STAT].reshape(-1)
    ob, qb, lse_stats = _fox_fwd(qa, ka, va, bounds, name="fox_fwd")
    lse_min = lse_stats[:, :, 0, 0].reshape(-1)
    pa = mm(oa, w["wa"], "nn", name="branch_a")
    pb = mm(ob, w["wb"], "nn", name="branch_b")
    merged = _merge_fwd(proj, pa, pb, name="merge_fwd")
    h1 = mm(merged, w["wo"], "nn", addend=x, name="mix_out")
    n2 = _rms_fwd(h1, p["norm_ffn"], name="rms2_fwd")
    ug = mm(n2, w["wug"], "nn", name="up_gate")
    uv = mm(n2, w["wuv"], "nn", name="up_val")
    a = _convglu_fwd(ug, uv, w["cwg"], w["cwv"], p["cbg"], p["cbv"], name="convglu_fwd")
    h2 = mm(a, w["wd"], "nn", addend=h1, name="ffn_down")
    loss, dh2, d_norm_final = _loss_head(h2, p["norm_final"], tgt, name="loss_head")
    da = mm(dh2, w["wd"], "nt", out_dtype=BF16, name="d_act")
    d_wd = mm(a, dh2, "tn", out_dtype=BF16, name="dw_down")
    dug, duv, d_cwg, d_cwv, d_cbg, d_cbv = _convglu_bwd(
        ug, uv, w["cwg"], w["cwv"], p["cbg"], p["cbv"], da, name="convglu_bwd")
    dn2 = mm(dug, w["wug"], "nt", name="dn2_gate")
    dn2 = mm(duv, w["wuv"], "nt", addend=dn2, name="dn2_val")
    d_wug = mm(n2, dug, "tn", out_dtype=BF16, name="dw_up_gate")
    d_wuv = mm(n2, duv, "tn", out_dtype=BF16, name="dw_up_val")
    dh1, d_norm_ffn = _rms_bwd(h1, p["norm_ffn"], dn2, dh2, name="rms2_bwd")
    dmerged = mm(dh1, w["wo"], "nt", name="d_merged")
    d_wo = mm(merged, dh1, "tn", out_dtype=BF16, name="dw_out")
    dpa, dpb, dga, dgb = _merge_bwd(proj, pa, pb, dmerged, name="merge_bwd")
    doa = mm(dpa, w["wa"], "nt", name="d_oa")
    dob = mm(dpb, w["wb"], "nt", out_dtype=BF16, name="d_ob")
    d_wa = mm(oa, dpa, "tn", out_dtype=BF16, name="dw_branch_a")
    d_wb = mm(ob, dpb, "tn", out_dtype=BF16, name="dw_branch_b")
    dhq, dhf, dhi, dhg, dlb, dgn8 = _hgrn_bwd(proj, lb, gnorm, o_hg, states, doa, name="hgrn_bwd")
    d_logits = _lb_bwd(p["hg_lb_logits"], dlb, name="lb_bwd")
    dob_hm = _fox_bwd_prep(ob, dob, name="fox_bwd_prep")
    early_parts = None
    if exchange:
        comm = _ExchangeComm(_early_grad_blocks(d_wa, d_wb, d_wo, d_wug, d_wuv, d_wd))
        dq, dob2, early_parts = _fox_bwd_dq(qb, ka, va, dob_hm, bounds, lse_min, comm=comm, name="fox_bwd_dq")
    else:
        dq, dob2 = _fox_bwd_dq(qb, ka, va, dob_hm, bounds, lse_min, name="fox_bwd_dq")
    dk, dv, dcs = _fox_bwd_dkv(qb, ka, va, dob2, bounds, lse_min, name="fox_bwd_dkv")
    dcs_tok = jnp.pad(dcs.transpose(1, 0, 2).reshape(S, FOX_HEADS), ((0, 0), (0, 128 - FOX_HEADS)))
    dff, dbias = _fox_gate_bwd(ff, bias, dcs_tok, name="fox_gate_bwd")
    dproj = jnp.concatenate([dhq, dhf, dhi, dhg, dq, dk, dv, dga, dgb], axis=1)
    d_wm = mm(n1, dproj, "tn", out_dtype=BF16, name="dw_in_main")
    d_wff = mm(n1, dff, "tn", out_dtype=BF16, name="dw_in_ff")
    dn1 = mm(dff, w["wff"], "nt", name="dn1_ff")
    late_parts = None
    if exchange:
        d_cw = jnp.concatenate([d_cwg, d_cwv], axis=1)
        comm = _ExchangeComm([_w_in_blocks(d_wm, d_wff), _col_blocks(d_cw, 704)])
        dn1, late_parts = mm(dproj, w["wm"], "nt", addend=dn1, comm=comm, name="dn1_main")
    else:
        dn1 = mm(dproj, w["wm"], "nt", addend=dn1, name="dn1_main")
    dx, d_norm_mix = _rms_bwd(x, p["norm_mix"], dn1, dh1, name="rms1_bwd")
    grads = dict(
        wm=d_wm, wff=d_wff, wa=d_wa, wb=d_wb, wo=d_wo, wug=d_wug, wuv=d_wuv, cwg=d_cwg, cwv=d_cwv, wd=d_wd,
        norm_mix=d_norm_mix.reshape(-1), fox_f_bias=dbias[0, :FOX_HEADS], hg_lb_logits=d_logits,
        hg_norm=jnp.sum(dgn8, axis=0).reshape(-1), norm_ffn=d_norm_ffn.reshape(-1), cbg=d_cbg, cbv=d_cbv,
        norm_final=d_norm_final.reshape(-1), early_parts=early_parts, late_parts=late_parts)
    return loss, dx, grads


SMALL = [("norm_mix", (1, D_MODEL)), ("fox_f_bias", (1, FOX_HEADS)), ("hg_lb_logits", (2, HG_HEADS * HG_DK)),
         ("hg_norm", (1, HG_DV)), ("norm_ffn", (1, D_MODEL)), ("conv_b", (1, 2 * D_FF)), ("norm_final", (D_MODEL,))]
SMALL_ROWS = 88
SHARDED = [("w_in", (D_MODEL, 1154), 256), ("w_branch_a", (128, D_MODEL), 128), ("w_branch_b", (128, D_MODEL), 128),
           ("w_out", (128, D_MODEL), 128), ("w_up", (D_MODEL, 704), 256), ("conv_w", (3, 704), 3),
           ("w_down", (352, D_MODEL), 352)]
NAMES = ["norm_mix", "w_in", "fox_f_bias", "hg_lb_logits", "hg_norm", "w_branch_a", "w_branch_b", "w_out",
         "norm_ffn", "w_up", "conv_w", "conv_b", "w_down", "norm_final"]


def _size(shape):
    n = 1
    for s in shape:
        n *= s
    return n


def _adamw(parts, w, m, v, *, name, T):
    R, C = w.shape
    c1 = 1.0 / (1.0 - ADAM_B1 ** ADAM_STEP)
    c2 = 1.0 / (1.0 - ADAM_B2 ** ADAM_STEP)

    def body(p_ref, w_ref, m_ref, v_ref, g_ref, d_ref, nm_ref, nv_ref):
        g = p_ref[0].astype(F32)
        for s in range(1, N_DEV):
            g = g + p_ref[s].astype(F32)
        g_ref[...] = g
        nm = ADAM_B1 * m_ref[...] + (1.0 - ADAM_B1) * g
        nv = ADAM_B2 * v_ref[...] + (1.0 - ADAM_B2) * (g * g)
        nm_ref[...] = nm
        nv_ref[...] = nv
        d_ref[...] = -ADAM_LR * ((nm * c1) / (jnp.sqrt(nv * c2) + ADAM_EPS) + ADAM_WD * w_ref[...])

    blk = pl.BlockSpec((T, C), lambda i: (i, 0))
    out = jax.ShapeDtypeStruct((R, C), F32)
    return pl.pallas_call(
        body, name=name, grid=(R // T,),
        in_specs=[pl.BlockSpec((N_DEV, T, C), lambda i: (0, i, 0)), blk, blk, blk],
        out_specs=[blk, blk, blk, blk], out_shape=[out, out, out, out],
        compiler_params=_cparams(("parallel",)),
    )(parts, w, m, v)


def _pack_small(vals):
    flat = jnp.concatenate([vals[n].reshape(-1).astype(F32) for n, _ in SMALL])
    return jnp.pad(flat, (0, SMALL_ROWS * 128 - flat.shape[0])).reshape(SMALL_ROWS, 128)


def _unpack_small(buf):
    flat, out, off = buf.reshape(-1), {}, 0
    for n, shape in SMALL:
        out[n] = flat[off:off + _size(shape)].reshape(shape)
        off += _size(shape)
    return out


def kernel(x, norm_mix, w_in, fox_f_bias, hg_lb_logits, hg_norm, w_branch_a, w_branch_b, w_out, norm_ffn, w_up, conv_w, conv_b, w_down, norm_final, loss_target, m_norm_mix, m_w_in, m_fox_f_bias, m_hg_lb_logits, m_hg_norm, m_w_branch_a, m_w_branch_b, m_w_out, m_norm_ffn, m_w_up, m_conv_w, m_conv_b, m_w_down, m_norm_final, v_norm_mix, v_w_in, v_fox_f_bias, v_hg_lb_logits, v_hg_norm, v_w_branch_a, v_w_branch_b, v_w_out, v_norm_ffn, v_w_up, v_conv_w, v_conv_b, v_w_down, v_norm_final):
    wv = dict(norm_mix=norm_mix, w_in=w_in, fox_f_bias=fox_f_bias, hg_lb_logits=hg_lb_logits, hg_norm=hg_norm,
              w_branch_a=w_branch_a, w_branch_b=w_branch_b, w_out=w_out, norm_ffn=norm_ffn, w_up=w_up, conv_w=conv_w,
              conv_b=conv_b, w_down=w_down, norm_final=norm_final)
    mv = dict(norm_mix=m_norm_mix, w_in=m_w_in, fox_f_bias=m_fox_f_bias, hg_lb_logits=m_hg_lb_logits, hg_norm=m_hg_norm,
              w_branch_a=m_w_branch_a, w_branch_b=m_w_branch_b, w_out=m_w_out, norm_ffn=m_norm_ffn, w_up=m_w_up,
              conv_w=m_conv_w, conv_b=m_conv_b, w_down=m_w_down, norm_final=m_norm_final)
    vv = dict(norm_mix=v_norm_mix, w_in=v_w_in, fox_f_bias=v_fox_f_bias, hg_lb_logits=v_hg_lb_logits, hg_norm=v_hg_norm,
              w_branch_a=v_w_branch_a, w_branch_b=v_w_branch_b, w_out=v_w_out, norm_ffn=v_norm_ffn, w_up=v_w_up,
              conv_w=v_conv_w, conv_b=v_conv_b, w_down=v_w_down, norm_final=v_norm_final)

    (g_in,) = _comm_call(_GatherComm([w_in[0].astype(BF16)]), name="gather_w_in")
    w = dict(wm=jnp.concatenate([g_in[d] for d in range(FF_DEV)]
                                + [g_in[FF_DEV][:, :FF_OFF], g_in[FF_DEV][:, FF_OFF + FOX_HEADS:]]
                                + [g_in[d] for d in range(FF_DEV + 1, N_DEV)], axis=1),
             wff=jnp.pad(g_in[FF_DEV][:, FF_OFF:FF_OFF + FOX_HEADS], ((0, 0), (0, 128 - FOX_HEADS))))
    late = _GatherComm([w_branch_a[0].astype(BF16), w_branch_b[0].astype(BF16), w_out[0].astype(BF16),
                        w_up[0].astype(BF16), conv_w[0], w_down[0].astype(BF16)])
    p = dict(norm_mix=norm_mix[0], fox_f_bias=fox_f_bias[0], hg_lb_logits=hg_lb_logits, hg_norm=hg_norm[0],
             norm_ffn=norm_ffn[0], cbg=conv_b[:, :D_FF], cbv=conv_b[:, D_FF:], norm_final=norm_final)
    loss, dx, grads = _local_step(x[0], loss_target[0], w, p, late=late, exchange=True)
    loss = lax.psum(loss[0, 0], ("x", "y", "c"))

    small = _pack_small(dict(
        norm_mix=grads["norm_mix"], fox_f_bias=grads["fox_f_bias"], hg_lb_logits=grads["hg_lb_logits"],
        hg_norm=grads["hg_norm"], norm_ffn=grads["norm_ffn"], conv_b=jnp.concatenate([grads["cbg"], grads["cbv"]], axis=1),
        norm_final=grads["norm_final"]))
    (small_parts,) = _comm_call(_ExchangeComm([jnp.broadcast_to(small[None], (N_DEV, SMALL_ROWS, 128))]),
                                name="exchange_small")
    ea, eb, eo, eup, ed = grads["early_parts"]
    p_in, p_cw = grads["late_parts"]
    parts = [p_in, ea, eb, eo, eup, p_cw, ed, small_parts]
    res = {}
    for (n, shape, tile), part in zip(SHARDED, parts):
        outs = _adamw(part, wv[n].reshape(shape), mv[n].reshape(shape), vv[n].reshape(shape), name="adamw_" + n, T=tile)
        res[n] = [o.reshape(wv[n].shape) for o in outs]
    outs = _adamw(parts[-1], _pack_small(wv), _pack_small(mv), _pack_small(vv), name="adamw_small", T=SMALL_ROWS)
    small = [_unpack_small(o) for o in outs]
    for n, _ in SMALL:
        res[n] = [s[n] for s in small]
    return (loss, dx[None], *[res[n][0] for n in NAMES], *[res[n][1] for n in NAMES],
            *[res[n][2] for n in NAMES], *[res[n][3] for n in NAMES])
```

```python
import jax
import jax.numpy as jnp
from jax import lax
from jax.experimental import pallas as pl
from jax.experimental.pallas import tpu as pltpu

F32 = jnp.float32
BF16 = jnp.bfloat16

D_MODEL = 1024
HG_HEADS = 8
HG_DK = 128
HG_DV = 128
HG_CHUNK = 64
FOX_HEADS = 16
FOX_DH = 64
D_FF = 2816
EPS = 1e-6
N_DEV = 8

ADAM_LR = 0.001
ADAM_B1 = 0.9
ADAM_B2 = 0.999
ADAM_EPS = 1e-08
ADAM_WD = 0.01
ADAM_STEP = 10

VMEM_LIMIT = 56 * 1024 * 1024


def _cparams(sem):
    return pltpu.CompilerParams(dimension_semantics=sem, vmem_limit_bytes=VMEM_LIMIT)


MESH = pl.DeviceIdType.MESH
ANY = pl.BlockSpec(memory_space=pl.ANY)
SMEM = pl.BlockSpec(memory_space=pltpu.SMEM)


class _GatherComm:
    def __init__(self, shards):
        self.inputs = list(shards)
        n = self.n = len(shards)
        self.out_shapes = [jax.ShapeDtypeStruct((N_DEV,) + s.shape, s.dtype) for s in shards]
        self.scratch = [pltpu.SemaphoreType.DMA((n, 7)), pltpu.SemaphoreType.DMA((n, 7)), pltpu.SemaphoreType.DMA((n,))]

    def _parts(self, x_refs, out_refs, sems):
        send_sems, recv_sems, local_sems = sems
        x, y, c = lax.axis_index("x"), lax.axis_index("y"), lax.axis_index("c")
        me, sibling = (x, y, c), (x, y, 1 - c)
        chips = [(1 - x, y), (x, 1 - y), (1 - x, 1 - y)]

        def copy(t, k, block, to, src=None):
            slot = out_refs[t].at[4 * block[0] + 2 * block[1] + block[2]]
            return pltpu.make_async_remote_copy(
                src_ref=slot if src is None else src, dst_ref=slot,
                send_sem=send_sems.at[t, k], recv_sem=recv_sems.at[t, k], device_id=to, device_id_type=MESH)

        mine = [pltpu.make_async_copy(x_refs[t], out_refs[t].at[4 * x + 2 * y + c], local_sems.at[t])
                for t in range(self.n)]
        first = []
        for t in range(self.n):
            first.append(copy(t, 0, me, sibling, src=x_refs[t]))
            first += [copy(t, 1 + j, me, (*chip, c), src=x_refs[t]) for j, chip in enumerate(chips)]
        return c, me, sibling, chips, copy, mine, first

    def start(self, x_refs, out_refs, sems):
        _, _, _, _, _, mine, first = self._parts(x_refs, out_refs, sems)
        for cp in mine + first:
            cp.start()

    def finish(self, x_refs, out_refs, sems):
        c, me, sibling, chips, copy, mine, first = self._parts(x_refs, out_refs, sems)
        passed = []
        for j, chip in enumerate(chips):
            for t in range(self.n):
                copy(t, 1 + j, (*chip, c), me).wait_recv()
                passed.append(copy(t, 4 + j, (*chip, c), sibling))
                passed[-1].start()
        for t in range(self.n):
            copy(t, 0, sibling, me).wait_recv()
            for j, chip in enumerate(chips):
                copy(t, 4 + j, (*chip, 1 - c), me).wait_recv()
        for cp in first + passed:
            cp.wait_send()
        for cp in mine:
            cp.wait()


class _ExchangeComm:
    def __init__(self, blocks):
        self.inputs = list(blocks)
        n = self.n = len(blocks)
        self.out_shapes = [jax.ShapeDtypeStruct(b.shape, b.dtype) for b in blocks]
        self.scratch = [pltpu.SemaphoreType.DMA((n, 7)), pltpu.SemaphoreType.DMA((n, 7)), pltpu.SemaphoreType.DMA((n,))]

    def _parts(self, g_refs, out_refs, sems):
        send_sems, recv_sems, local_sems = sems
        x, y, c = lax.axis_index("x"), lax.axis_index("y"), lax.axis_index("c")
        me = 4 * x + 2 * y + c
        mine = [pltpu.make_async_copy(g_refs[t].at[me], out_refs[t].at[me], local_sems.at[t]) for t in range(self.n)]
        sends, recvs = [], []
        for k in range(1, N_DEV):
            px = 1 - x if k & 4 else x
            py = 1 - y if k & 2 else y
            pc = 1 - c if k & 1 else c
            p = 4 * px + 2 * py + pc
            for t in range(self.n):
                sends.append(pltpu.make_async_remote_copy(
                    src_ref=g_refs[t].at[p], dst_ref=out_refs[t].at[me], send_sem=send_sems.at[t, k - 1],
                    recv_sem=recv_sems.at[t, k - 1], device_id=(px, py, pc), device_id_type=MESH))
                recvs.append(pltpu.make_async_remote_copy(
                    src_ref=g_refs[t].at[p], dst_ref=out_refs[t].at[p], send_sem=send_sems.at[t, k - 1],
                    recv_sem=recv_sems.at[t, k - 1], device_id=(px, py, pc), device_id_type=MESH))
        return mine, sends, recvs

    def start(self, g_refs, out_refs, sems):
        mine, sends, _ = self._parts(g_refs, out_refs, sems)
        for cp in mine + sends:
            cp.start()

    def finish(self, g_refs, out_refs, sems):
        mine, sends, recvs = self._parts(g_refs, out_refs, sems)
        for cp in recvs:
            cp.wait_recv()
        for cp in sends:
            cp.wait_send()
        for cp in mine:
            cp.wait()


def _comm_call(comm, *, name):
    n = comm.n

    def body(*refs):
        comm.start(refs[:n], refs[n:2 * n], refs[2 * n:])
        comm.finish(refs[:n], refs[n:2 * n], refs[2 * n:])

    return pl.pallas_call(body, name=name, in_specs=[ANY] * n, out_specs=[ANY] * n, out_shape=comm.out_shapes,
                          scratch_shapes=comm.scratch)(*comm.inputs)


_DIMS = {
    "nn": (((1,), (0,)), ((), ())),
    "nt": (((1,), (1,)), ((), ())),
    "tn": (((0,), (0,)), ((), ())),
}

MATMUL_VMEM_BUDGET = 36 * 1024 * 1024
MAX_TILE = 1536


def _pick(n, prefs):
    for p in prefs:
        if n % p == 0:
            return p
    return n


def _tile_options(n):
    return [d for d in range(128, min(n, MAX_TILE) + 1, 128) if n % d == 0] or [n]


def _pick_tiles(M, N, tk, nk, sa, sb, so, has_addend, tm, tn):
    best = None
    for cm in ([tm] if tm else _tile_options(M)):
        for cn in ([tn] if tn else _tile_options(N)):
            need = 2 * (cm * tk * sa + tk * cn * sb + cm * cn * so + (cm * cn * 4 if has_addend else 0))
            need += cm * cn * 4 if nk > 1 else 0
            if need <= MATMUL_VMEM_BUDGET and (best is None or cm * cn > best[0] * best[1]
                                               or (cm * cn == best[0] * best[1] and cn > best[1])):
                best = (cm, cn)
    assert best is not None, (M, N, tk)
    return best


def _matmul(a, b, form, *, out_dtype=F32, addend=None, tm=None, tn=None, tk=None, comm=None, name):
    if form == "nn":
        (M, K), (K2, N) = a.shape, b.shape
    elif form == "nt":
        (M, K), (N, K2) = a.shape, b.shape
    else:
        (K, M), (K2, N) = a.shape, b.shape
    assert K == K2, (a.shape, b.shape, form)
    tk = tk or (K if K <= 2816 else _pick(K, (1024, 512, 256, 128)))
    nk = K // tk
    if tm is None or tn is None:
        tm, tn = _pick_tiles(M, N, tk, nk, a.dtype.itemsize, b.dtype.itemsize, jnp.dtype(out_dtype).itemsize,
                             addend is not None, tm, tn)
    assert M % tm == 0 and N % tn == 0 and K % tk == 0, (M, N, K, tm, tn, tk)
    dims = _DIMS[form]
    nc = comm.n if comm is not None else 0
    grid = (M // tm, N // tn, nk)

    def body(*refs):
        a_ref, b_ref = refs[:2]
        pos = 2
        add_ref = refs[pos] if addend is not None else None
        pos += addend is not None
        c_in, o_ref, c_out = refs[pos:pos + nc], refs[pos + nc], refs[pos + nc + 1:pos + 2 * nc + 1]
        pos += 2 * nc + 1
        acc_ref = refs[pos] if nk > 1 else None
        c_sems = refs[pos + (nk > 1):]
        if comm is not None:
            ids = [pl.program_id(d) for d in range(3)]

            @pl.when((ids[0] == 0) & (ids[1] == 0) & (ids[2] == 0))
            def _():
                comm.start(c_in, c_out, c_sems)

        def finish(r):
            if add_ref is not None:
                r = r + add_ref[...].astype(F32)
            o_ref[...] = r.astype(o_ref.dtype)

        part = lax.dot_general(a_ref[...].astype(BF16), b_ref[...].astype(BF16), dims, preferred_element_type=F32)
        if nk == 1:
            finish(part)
        else:
            k = pl.program_id(2)

            @pl.when(k == 0)
            def _():
                acc_ref[...] = part

            @pl.when(k > 0)
            def _():
                acc_ref[...] += part

            @pl.when(k == nk - 1)
            def _():
                finish(acc_ref[...])

        if comm is not None:
            @pl.when((ids[0] == grid[0] - 1) & (ids[1] == grid[1] - 1) & (ids[2] == grid[2] - 1))
            def _():
                comm.finish(c_in, c_out, c_sems)

    if form == "nn":
        a_spec = pl.BlockSpec((tm, tk), lambda i, j, k: (i, k))
        b_spec = pl.BlockSpec((tk, tn), lambda i, j, k: (k, j))
    elif form == "nt":
        a_spec = pl.BlockSpec((tm, tk), lambda i, j, k: (i, k))
        b_spec = pl.BlockSpec((tn, tk), lambda i, j, k: (j, k))
    else:
        a_spec = pl.BlockSpec((tk, tm), lambda i, j, k: (k, i))
        b_spec = pl.BlockSpec((tk, tn), lambda i, j, k: (k, j))
    o_spec = pl.BlockSpec((tm, tn), lambda i, j, k: (i, j))
    in_specs = [a_spec, b_spec] + ([o_spec] if addend is not None else [])
    args = (a, b) + ((addend,) if addend is not None else ())
    out_shape = jax.ShapeDtypeStruct((M, N), out_dtype)
    scratch = [pltpu.VMEM((tm, tn), F32)] if nk > 1 else []
    if comm is None:
        return pl.pallas_call(
            body, name=name, grid=grid, in_specs=in_specs, out_specs=o_spec, out_shape=out_shape,
            scratch_shapes=scratch, compiler_params=_cparams(("parallel", "parallel", "arbitrary")),
        )(*args)
    outs = pl.pallas_call(
        body, name=name, grid=grid, in_specs=in_specs + [ANY] * nc, out_specs=[o_spec] + [ANY] * nc,
        out_shape=[out_shape] + comm.out_shapes, scratch_shapes=scratch + comm.scratch,
        compiler_params=_cparams(("arbitrary", "arbitrary", "arbitrary")),
    )(*args, *comm.inputs)
    return outs[0], outs[1:]


def _rms_fwd(x, g, *, name, tm=512):
    M, D = x.shape
    tm = min(tm, M)

    def body(x_ref, g_ref, n_ref):
        xf = x_ref[...]
        r = lax.rsqrt(jnp.mean(xf * xf, axis=-1, keepdims=True) + EPS)
        n_ref[...] = (xf * r * g_ref[...]).astype(n_ref.dtype)

    return pl.pallas_call(
        body, name=name, grid=(M // tm,),
        in_specs=[pl.BlockSpec((tm, D), lambda i: (i, 0)), pl.BlockSpec((1, D), lambda i: (0, 0))],
        out_specs=pl.BlockSpec((tm, D), lambda i: (i, 0)),
        out_shape=jax.ShapeDtypeStruct((M, D), BF16),
        compiler_params=_cparams(("parallel",)),
    )(x, g.reshape(1, D))


def _rms_bwd(x, g, dn, dres, *, name, tm=512):
    M, D = x.shape
    tm = min(tm, M)

    def body(x_ref, g_ref, dn_ref, dres_ref, dx_ref, dg_ref):
        @pl.when(pl.program_id(0) == 0)
        def _():
            dg_ref[...] = jnp.zeros_like(dg_ref)

        xf = x_ref[...]
        r = lax.rsqrt(jnp.mean(xf * xf, axis=-1, keepdims=True) + EPS)
        xh = xf * r
        dn_ = dn_ref[...].astype(F32)
        dg_ref[...] += jnp.sum(dn_ * xh, axis=0, keepdims=True)
        dxh = dn_ * g_ref[...]
        dx = r * (dxh - xh * jnp.mean(dxh * xh, axis=-1, keepdims=True))
        dx_ref[...] = dres_ref[...] + dx

    row = pl.BlockSpec((tm, D), lambda i: (i, 0))
    vec = pl.BlockSpec((1, D), lambda i: (0, 0))
    return pl.pallas_call(
        body, name=name, grid=(M // tm,),
        in_specs=[row, vec, row, row], out_specs=[row, vec],
        out_shape=[jax.ShapeDtypeStruct((M, D), F32), jax.ShapeDtypeStruct((1, D), F32)],
        compiler_params=_cparams(("arbitrary",)),
    )(x, g.reshape(1, D), dn, dres)


def _loss_head(h, g, tgt, *, name, tm=512):
    M, D = h.shape
    tm = min(tm, M)

    def body(h_ref, g_ref, t_ref, loss_ref, dh_ref, dg_ref):
        @pl.when(pl.program_id(0) == 0)
        def _():
            dg_ref[...] = jnp.zeros_like(dg_ref)
            loss_ref[...] = jnp.zeros_like(loss_ref)

        xf = h_ref[...]
        r = lax.rsqrt(jnp.mean(xf * xf, axis=-1, keepdims=True) + EPS)
        xh = xf * r
        err = xh * g_ref[...] - t_ref[...]
        part = jnp.sum(jnp.mean(err * err, axis=-1, keepdims=True), axis=0, keepdims=True)
        loss_ref[...] += 0.5 * part
        dy = err * (1.0 / D)
        dg_ref[...] += jnp.sum(dy * xh, axis=0, keepdims=True)
        dxh = dy * g_ref[...]
        dh_ref[...] = r * (dxh - xh * jnp.mean(dxh * xh, axis=-1, keepdims=True))

    row = pl.BlockSpec((tm, D), lambda i: (i, 0))
    vec = pl.BlockSpec((1, D), lambda i: (0, 0))
    one = pl.BlockSpec((1, 1), lambda i: (0, 0))
    return pl.pallas_call(
        body, name=name, grid=(M // tm,),
        in_specs=[row, vec, row], out_specs=[one, row, vec],
        out_shape=[jax.ShapeDtypeStruct((1, 1), F32), jax.ShapeDtypeStruct((M, D), F32),
                   jax.ShapeDtypeStruct((1, D), F32)],
        compiler_params=_cparams(("arbitrary",)),
    )(h, g.reshape(1, D), tgt)


HG_MID = HG_CHUNK // 2 - 1
EXP_CAP = 80.0


def _sigmoid(x):
    return 1.0 / (1.0 + jnp.exp(-x))


def _dot(a, b, dims, precision=None):
    return lax.dot_general(a, b, dims, preferred_element_type=F32, precision=precision)


def _bdot(a, b, form):
    return _dot(a.astype(BF16), b.astype(BF16), _DIMS[form])


def _split2(x):
    hi = x.astype(BF16)
    return hi, (x - hi.astype(F32)).astype(BF16)


def _dot3(a, b, form):
    d = _DIMS[form]
    return _dot(a[0], b[0], d) + (_dot(a[0], b[1], d) + _dot(a[1], b[0], d))


def _hgrn_chunk_common(hq, hf, lbv, tril, rid):
    sq = _sigmoid(hq)
    q = hq * sq
    sg = _sigmoid(hf)
    f = lbv + (1.0 - lbv) * sg
    k = (1.0 - lbv) * (1.0 - sg)
    g = jnp.log(f)
    b = _dot(tril, g, _DIMS["nn"], precision=lax.Precision.HIGHEST)
    bref = jnp.sum(jnp.where(rid == HG_MID, b, 0.0), axis=0, keepdims=True)
    bend = jnp.sum(jnp.where(rid == HG_CHUNK - 1, b, 0.0), axis=0, keepdims=True)
    eb = jnp.exp(b)
    e1 = jnp.exp(jnp.minimum(b - bref, EXP_CAP))
    e2 = jnp.exp(jnp.minimum(bref - b, EXP_CAP))
    e3 = jnp.exp(bend - b)
    return sq, q, sg, f, k, bend, eb, e1, e2, e3


def _hgrn_fwd(proj, lb, gnorm, *, name, T=1024):
    S = proj.shape[0]
    T = min(T, S)
    nch = T // HG_CHUNK
    C = HG_CHUNK

    def body(hq_ref, hf_ref, hi_ref, hg_ref, lb_ref, gn_ref, o_ref, oa_ref, st_ref, state):
        @pl.when(pl.program_id(1) == 0)
        def _():
            state[...] = jnp.zeros_like(state)

        lbv = lb_ref[...]
        gn = gn_ref[...]
        row = lax.broadcasted_iota(jnp.int32, (C, C), 0)
        col = lax.broadcasted_iota(jnp.int32, (C, C), 1)
        causal = row >= col
        tril = causal.astype(F32)
        rid = lax.broadcasted_iota(jnp.int32, (C, HG_DK), 0)
        sls = [pl.ds(c * C, C) for c in range(nch)]
        pre = [_hgrn_chunk_common(hq_ref[sl, :], hf_ref[sl, :], lbv, tril, rid) for sl in sls]
        v_l = [hi_ref[sl, :].astype(BF16) for sl in sls]
        a_l, u_l = [], []
        for c in range(nch):
            _, q, _, _, k, _, _, e1, e2, e3 = pre[c]
            a_l.append(jnp.where(causal, _bdot(q * e1, k * e2, "nt"), 0.0))
            u_l.append(_bdot(v_l[c], k * e3, "tn"))
        o_l = [_bdot(a_l[c], v_l[c], "nn") for c in range(nch)]
        st = state[...]
        st_l = []
        for c in range(nch):
            st_l.append(st)
            st = st * jnp.exp(pre[c][5]) + u_l[c]
        state[...] = st
        for c in range(nch):
            st_ref[0, c] = st_l[c]
            o_l[c] = o_l[c] + _bdot(pre[c][1] * pre[c][6], st_l[c], "nt")
        for c in range(nch):
            o, hg = o_l[c], hg_ref[sls[c], :]
            o_ref[sls[c], :] = o
            r = lax.rsqrt(jnp.mean(o * o, axis=-1, keepdims=True) + EPS)
            oa_ref[sls[c], :] = (o * r * gn * (hg * _sigmoid(hg))).astype(oa_ref.dtype)

    def grp(gidx):
        return pl.BlockSpec((T, 128), lambda h, t: (t, gidx * 8 + h))

    return pl.pallas_call(
        body, name=name, grid=(HG_HEADS, S // T),
        in_specs=[grp(0), grp(1), grp(2), grp(3),
                  pl.BlockSpec((1, 128), lambda h, t: (0, h)), pl.BlockSpec((1, 128), lambda h, t: (0, 0))],
        out_specs=[pl.BlockSpec((T, 128), lambda h, t: (t, h)), pl.BlockSpec((T, 128), lambda h, t: (t, h)),
                   pl.BlockSpec((1, nch, HG_DV, HG_DK), lambda h, t: (h, t, 0, 0))],
        out_shape=[jax.ShapeDtypeStruct((S, HG_HEADS * HG_DV), F32), jax.ShapeDtypeStruct((S, HG_HEADS * HG_DV), BF16),
                   jax.ShapeDtypeStruct((HG_HEADS, S // C, HG_DV, HG_DK), F32)],
        scratch_shapes=[pltpu.VMEM((HG_DV, HG_DK), F32)],
        compiler_params=_cparams(("parallel", "arbitrary")),
    )(proj, proj, proj, proj, lb, gnorm)


def _hgrn_bwd(proj, lb, gnorm, o, states, doa, *, name, T=1024):
    S = proj.shape[0]
    T = min(T, S)
    nch = T // HG_CHUNK
    C = HG_CHUNK
    nT = S // T

    def body(hq_ref, hf_ref, hi_ref, hg_ref, lb_ref, gn_ref, o_ref, st_ref, doa_ref,
             dhq_ref, dhf_ref, dhi_ref, dhg_ref, dlb_ref, dgn_ref, dstate):
        @pl.when(pl.program_id(1) == 0)
        def _():
            dstate[...] = jnp.zeros_like(dstate)
            dlb_ref[...] = jnp.zeros_like(dlb_ref)
            dgn_ref[...] = jnp.zeros_like(dgn_ref)

        lbv = lb_ref[...]
        gn = gn_ref[...]
        row = lax.broadcasted_iota(jnp.int32, (C, C), 0)
        col = lax.broadcasted_iota(jnp.int32, (C, C), 1)
        causal = row >= col
        tril = causal.astype(F32)
        triu = (row <= col).astype(F32)
        rid = lax.broadcasted_iota(jnp.int32, (C, HG_DK), 0)
        rng = range(nch)
        sls = [pl.ds(c * C, C) for c in rng]
        pre = [_hgrn_chunk_common(hq_ref[sl, :], hf_ref[sl, :], lbv, tril, rid) for sl in sls]
        do2, dgn_acc = [], jnp.zeros((1, HG_DV), F32)
        for c in rng:
            hg, ov = hg_ref[sls[c], :], o_ref[sls[c], :]
            r = lax.rsqrt(jnp.mean(ov * ov, axis=-1, keepdims=True) + EPS)
            xh = ov * r
            sgg = _sigmoid(hg)
            d_oa = doa_ref[sls[c], :].astype(F32)
            dz = d_oa * (hg * sgg)
            dhg_ref[sls[c], :] = (d_oa * (xh * gn) * (sgg * (1.0 + hg * (1.0 - sgg)))).astype(dhg_ref.dtype)
            dgn_acc = dgn_acc + jnp.sum(dz * xh, axis=0, keepdims=True)
            dxh = dz * gn
            do2.append(_split2(r * (dxh - xh * jnp.mean(dxh * xh, axis=-1, keepdims=True))))
        dgn_ref[0] += dgn_acc
        qi = [pre[c][1] * pre[c][6] for c in rng]
        qp = [pre[c][1] * pre[c][7] for c in rng]
        kp = [pre[c][4] * pre[c][8] for c in rng]
        kend = [pre[c][4] * pre[c][9] for c in rng]
        qi2, qp2, kp2, kend2 = ([_split2(t) for t in lst] for lst in (qi, qp, kp, kend))
        v2 = [_split2(hi_ref[sl, :]) for sl in sls]
        st0 = [st_ref[0, c] for c in rng]
        a_l = [jnp.where(causal, _dot(qp2[c][0], kp2[c][0], _DIMS["nt"]), 0.0).astype(BF16) for c in rng]
        da2 = [_split2(jnp.where(causal, _dot3(do2[c], v2[c], "nt"), 0.0)) for c in rng]
        dqi = [_dot3(do2[c], _split2(st0[c]), "nn") for c in rng]
        w_l = [_dot3(do2[c], qi2[c], "tn") for c in rng]
        ds = dstate[...]
        ds1 = [None] * nch
        for c in reversed(rng):
            ds1[c] = ds
            ds = ds * jnp.exp(pre[c][5]) + w_l[c]
        dstate[...] = ds
        ds12 = [_split2(t) for t in ds1]
        dqp = [_dot3(da2[c], kp2[c], "nn") for c in rng]
        dkp = [_dot3(da2[c], qp2[c], "tn") for c in rng]
        dv = [_dot(a_l[c], do2[c][0], _DIMS["tn"]) + _dot(kend2[c][0], ds12[c][0], _DIMS["nt"]) for c in rng]
        dkend = [_dot3(v2[c], ds12[c], "nn") for c in rng]
        dq_l, dk_l, db_l = [], [], []
        for c in rng:
            _, _, _, _, _, bend, eb, e1, e2, e3 = pre[c]
            dq_l.append(dqi[c] * eb + dqp[c] * e1)
            dk_l.append(dkp[c] * e2 + dkend[c] * e3)
            db = dqi[c] * qi[c] + dqp[c] * qp[c] - dkp[c] * kp[c] - dkend[c] * kend[c]
            dbend = (jnp.sum(dkend[c] * kend[c], axis=0, keepdims=True)
                     + jnp.exp(bend) * jnp.sum(ds1[c] * st0[c], axis=0, keepdims=True))
            db_l.append(db + jnp.where(rid == C - 1, dbend, 0.0))
        dg = [_dot(triu, db_l[c], _DIMS["nn"], precision=lax.Precision.HIGHEST) for c in rng]
        dlb_acc = jnp.zeros((1, HG_DK), F32)
        for c in rng:
            sq, _, sg, f, _, _, _, _, _, _ = pre[c]
            hq = hq_ref[sls[c], :]
            df = dg[c] / f - dk_l[c]
            dlb_acc = dlb_acc + jnp.sum(df * (1.0 - sg), axis=0, keepdims=True)
            dhf_ref[sls[c], :] = (df * (1.0 - lbv) * sg * (1.0 - sg)).astype(dhf_ref.dtype)
            dhq_ref[sls[c], :] = (dq_l[c] * (sq * (1.0 + hq * (1.0 - sq)))).astype(dhq_ref.dtype)
            dhi_ref[sls[c], :] = dv[c].astype(dhi_ref.dtype)
        dlb_ref[...] += dlb_acc

    def grp(gidx):
        return pl.BlockSpec((T, 128), lambda h, t: (nT - 1 - t, gidx * 8 + h))

    tok = pl.BlockSpec((T, 128), lambda h, t: (nT - 1 - t, h))
    big = jax.ShapeDtypeStruct((S, HG_HEADS * HG_DV), BF16)
    return pl.pallas_call(
        body, name=name, grid=(HG_HEADS, nT),
        in_specs=[grp(0), grp(1), grp(2), grp(3),
                  pl.BlockSpec((1, 128), lambda h, t: (0, h)), pl.BlockSpec((1, 128), lambda h, t: (0, 0)),
                  tok, pl.BlockSpec((1, nch, HG_DV, HG_DK), lambda h, t: (h, nT - 1 - t, 0, 0)), tok],
        out_specs=[tok, tok, tok, tok, pl.BlockSpec((1, 128), lambda h, t: (0, h)),
                   pl.BlockSpec((1, 1, 128), lambda h, t: (h, 0, 0))],
        out_shape=[big, big, big, big, jax.ShapeDtypeStruct((1, HG_HEADS * HG_DK), F32),
                   jax.ShapeDtypeStruct((HG_HEADS, 1, HG_DV), F32)],
        scratch_shapes=[pltpu.VMEM((HG_DV, HG_DK), F32)],
        compiler_params=_cparams(("parallel", "arbitrary")),
    )(proj, proj, proj, proj, lb, gnorm, o, states, doa)


def _lb_fwd(logits, *, name):
    def body(l_ref, lb_ref):
        lb_ref[...] = _sigmoid(l_ref[0:1, :] - l_ref[1:2, :])

    return pl.pallas_call(body, name=name, out_shape=jax.ShapeDtypeStruct((1, logits.shape[1]), F32))(logits)


def _lb_bwd(logits, dlb, *, name):
    def body(l_ref, d_ref, o_ref):
        lbv = _sigmoid(l_ref[0:1, :] - l_ref[1:2, :])
        t = d_ref[...] * lbv * (1.0 - lbv)
        o_ref[0:1, :] = t
        o_ref[1:2, :] = -t

    return pl.pallas_call(body, name=name, out_shape=jax.ShapeDtypeStruct(logits.shape, F32))(logits, dlb)


NEG = -1e30
FOX_SCALE = FOX_DH ** -0.5
FOX_PAIRS = FOX_HEADS // 2


def _fox_gate_fwd(ff, bias, *, name, T=512):
    S = ff.shape[0]
    T = min(T, S)

    def body(ff_ref, b_ref, c_ref, carry):
        @pl.when(pl.program_id(0) == 0)
        def _():
            carry[...] = jnp.zeros_like(carry)

        z = ff_ref[...] + b_ref[...]
        logf = jnp.minimum(z, 0.0) - jnp.log(1.0 + jnp.exp(-jnp.abs(z)))
        row = lax.broadcasted_iota(jnp.int32, (T, T), 0)
        col = lax.broadcasted_iota(jnp.int32, (T, T), 1)
        c = _dot((row >= col).astype(F32), logf, _DIMS["nn"], precision=lax.Precision.HIGHEST) + carry[...]
        c_ref[...] = c
        carry[...] = c[T - 1:T, :]

    return pl.pallas_call(
        body, name=name, grid=(S // T,),
        in_specs=[pl.BlockSpec((T, 128), lambda i: (i, 0)), pl.BlockSpec((1, 128), lambda i: (0, 0))],
        out_specs=pl.BlockSpec((T, 128), lambda i: (i, 0)),
        out_shape=jax.ShapeDtypeStruct((S, 128), F32),
        scratch_shapes=[pltpu.VMEM((1, 128), F32)],
        compiler_params=_cparams(("arbitrary",)),
    )(ff, bias)


def _fox_gate_bwd(ff, bias, dcs, *, name, T=512):
    S = ff.shape[0]
    T = min(T, S)
    nT = S // T

    def body(ff_ref, b_ref, d_ref, dff_ref, db_ref, carry):
        @pl.when(pl.program_id(0) == 0)
        def _():
            carry[...] = jnp.zeros_like(carry)
            db_ref[...] = jnp.zeros_like(db_ref)

        row = lax.broadcasted_iota(jnp.int32, (T, T), 0)
        col = lax.broadcasted_iota(jnp.int32, (T, T), 1)
        dlogf = carry[...] - _dot((row <= col).astype(F32), d_ref[...], _DIMS["nn"], precision=lax.Precision.HIGHEST)
        carry[...] = dlogf[0:1, :]
        dff = dlogf * (1.0 - _sigmoid(ff_ref[...] + b_ref[...]))
        dff_ref[...] = dff.astype(dff_ref.dtype)
        db_ref[...] += jnp.sum(dff, axis=0, keepdims=True)

    rev = pl.BlockSpec((T, 128), lambda i: (nT - 1 - i, 0))
    vec = pl.BlockSpec((1, 128), lambda i: (0, 0))
    return pl.pallas_call(
        body, name=name, grid=(nT,),
        in_specs=[rev, vec, rev], out_specs=[rev, vec],
        out_shape=[jax.ShapeDtypeStruct((S, 128), BF16), jax.ShapeDtypeStruct((1, 128), F32)],
        scratch_shapes=[pltpu.VMEM((1, 128), F32)],
        compiler_params=_cparams(("arbitrary",)),
    )(ff, bias, dcs)


AUG = FOX_DH
RSUM_LANE = 6


def _bias_lane(hh):
    return AUG * (1 - hh)


def _data_lanes(lane, hh):
    return (lane < AUG) if hh == 0 else (lane >= AUG)


def _split3(x):
    a = x.astype(BF16).astype(F32)
    r = x - a
    b = r.astype(BF16).astype(F32)
    return a, b, r - b


def _lane_fill(lane, base, pieces, start):
    for i, pc in enumerate(pieces):
        base = jnp.where(lane == start + i, pc, base)
    return base


FOX_TB = 512
FOX_SKIP = 32.0
N_STAT = 4


def _fox_prep(proj, c_tok, *, name):
    S = proj.shape[0]
    T = min(FOX_TB, S)

    def body(q_ref, k_ref, v_ref, c_ref, qa_ref, ka_ref, va_ref, st_ref):
        pair = pl.program_id(0)
        lane = lax.broadcasted_iota(jnp.int32, (T, 128), 1)
        lane1 = lax.broadcasted_iota(jnp.int32, (1, 128), 1)
        c = c_ref[...]
        q, k, v = q_ref[...], k_ref[...], v_ref[...]
        for hh in range(2):
            data, b0 = _data_lanes(lane, hh), _bias_lane(hh)
            ones3 = jnp.where((lane >= b0) & (lane < b0 + 3), 1.0, 0.0)

            def max_norm(t):
                tr = jnp.where(data, t.astype(BF16).astype(F32), 0.0)
                return jnp.sqrt(jnp.max(jnp.sum(tr * tr, axis=-1, keepdims=True), axis=0, keepdims=True))

            ch = jnp.sum(jnp.where(lane == 2 * pair + hh, c, 0.0), axis=-1, keepdims=True)
            c1, c2, c3 = _split3(ch)
            aug_q = _lane_fill(lane, jnp.where((lane >= b0 + 3) & (lane < b0 + 6), 1.0, 0.0), (c1, c2, c3), b0)
            aug_k = _lane_fill(lane, ones3, (-c1, -c2, -c3), b0 + 3)
            qa_ref[hh] = jnp.where(data, q * FOX_SCALE, aug_q).astype(BF16)
            ka_ref[hh] = jnp.where(data, k, aug_k).astype(BF16)
            va_ref[hh] = jnp.where(data, v, ones3).astype(BF16)
            stats = (max_norm(q * FOX_SCALE), jnp.max(ch, axis=0, keepdims=True), max_norm(k),
                     jnp.min(ch, axis=0, keepdims=True))
            st_ref[hh, 0] = _lane_fill(lane1, jnp.zeros((1, 128), F32), stats, 0)

    def grp(g):
        return pl.BlockSpec((T, 128), lambda p, t: (t, g * 8 + p))

    hm = pl.BlockSpec((2, T, 128), lambda p, t: (p, t, 0))
    out = jax.ShapeDtypeStruct((FOX_HEADS, S, 128), BF16)
    return pl.pallas_call(
        body, name=name, grid=(FOX_PAIRS, S // T),
        in_specs=[grp(4), grp(5), grp(6), pl.BlockSpec((T, 128), lambda p, t: (t, 0))],
        out_specs=[hm, hm, hm, pl.BlockSpec((2, 1, 1, 128), lambda p, t: (p, t, 0, 0))],
        out_shape=[out, out, out, jax.ShapeDtypeStruct((FOX_HEADS, S // T, 1, 128), F32)],
        compiler_params=_cparams(("parallel", "parallel")),
    )(proj, proj, proj, c_tok)


def _fox_bound(st_ref, head, nb, qi, ki):
    qb_, kb_ = (head * nb + qi) * N_STAT, (head * nb + ki) * N_STAT
    return st_ref[qb_] * st_ref[kb_ + 2] + st_ref[qb_ + 1] - st_ref[kb_ + 3] + 0.01


def _pair_lanes(lane, a0, a1):
    return jnp.where(lane < AUG, a0, a1)


def _first_live_key(st_ref, head, nb, qi, newest, thr):
    def body(t, k0):
        k = newest - t
        return jnp.where(_fox_bound(st_ref, head, nb, qi, k) > thr, k, k0)

    return lax.fori_loop(0, newest + 1, body, newest + 1)


def _last_live_query(st_ref, lm_ref, head, nb, ki):
    def body(t, i1):
        i = ki + 1 + t
        live = _fox_bound(st_ref, head, nb, i, ki) > lm_ref[head * nb + i] - FOX_SKIP
        return jnp.where(live, i, i1)

    return lax.fori_loop(0, nb - 1 - ki, body, ki)


class _BlockStream:
    def __init__(self, hbm_refs, bufs, sems, pair, tb):
        self.hbm, self.bufs, self.sems, self.pair, self.tb = hbm_refs, bufs, sems, pair, tb

    def _copies(self, blk, slot):
        rows = pl.ds(pl.multiple_of(blk * self.tb, self.tb), self.tb)
        return [pltpu.make_async_copy(h.at[pl.ds(2 * self.pair, 2), rows, :], b.at[slot], self.sems.at[n, slot])
                for n, (h, b) in enumerate(zip(self.hbm, self.bufs))]

    def start(self, blk, slot):
        for cp in self._copies(blk, slot):
            cp.start()

    def wait(self, blk, slot):
        for cp in self._copies(blk, slot):
            cp.wait()


def _fox_fwd(qa, ka, va, bounds, *, name):
    S = qa.shape[1]
    tb = min(FOX_TB, S)
    nb = S // tb

    def body(qa_ref, ka_hbm, va_hbm, st_ref, o_ref, qb_ref, lse_ref, kbuf, vbuf, sems, m_s, acc_s, m_min):
        pair, qi = pl.program_id(0), pl.program_id(1)
        stream = _BlockStream((ka_hbm, va_hbm), (kbuf, vbuf), sems, pair, tb)

        def head_step(hh, slot, masked, paired=True):
            s = _dot(qa_ref[hh], kbuf[slot, hh], _DIMS["nt"])
            if masked:
                row = lax.broadcasted_iota(jnp.int32, (tb, tb), 0)
                col = lax.broadcasted_iota(jnp.int32, (tb, tb), 1)
                s = jnp.where(col <= row, s, NEG)
            m_old = m_s[hh]
            m_new = jnp.maximum(m_old, jnp.broadcast_to(jnp.max(s, axis=-1, keepdims=True), (tb, 128)))
            p = jnp.exp(s - jnp.concatenate([m_new] * (tb // 128), axis=1))
            p_hi = p.astype(BF16)
            vv = vbuf[slot, hh]
            if paired:
                p_lo = (p - p_hi.astype(F32)).astype(BF16)
                acc_s[hh] = (jnp.exp(m_old - m_new) * acc_s[hh]
                             + _dot(p_hi, vv, _DIMS["nn"]) + _dot(p_lo, vv, _DIMS["nn"]))
            else:
                acc_s[hh] = jnp.exp(m_old - m_new) * acc_s[hh] + _dot(p_hi, vv, _DIMS["nn"])
            m_s[hh] = m_new
            m_min[hh] = jnp.min(m_new)

        @pl.when(qi == 0)
        def _():
            stream.start(qi, 0)

        @pl.when(qi > 0)
        def _():
            stream.start(qi - 1, 1)

        m_s[...] = jnp.full_like(m_s, NEG)
        acc_s[...] = jnp.zeros_like(acc_s)
        stream.wait(qi, 0)
        for hh in range(2):
            head_step(hh, 0, True)

        @pl.when(qi > 1)
        def _():
            stream.start(qi - 2, 0)

        @pl.when(qi > 0)
        def _():
            stream.wait(qi - 1, 1)
            for hh in range(2):
                head_step(hh, 1, False)

        k0 = [_first_live_key(st_ref, 2 * pair + hh, nb, qi, qi - 2, m_min[hh] - FOX_SKIP) for hh in range(2)]
        n = qi - 1 - jnp.minimum(k0[0], k0[1])

        @pl.when((qi > 1) & (n == 0))
        def _():
            stream.wait(qi - 2, 0)

        def loop(t, carry):
            k = qi - 2 - t
            slot = t % 2
            stream.wait(k, slot)

            @pl.when(t + 1 < n)
            def _():
                stream.start(k - 1, 1 - slot)

            live = [k >= k0[hh] for hh in range(2)]

            @pl.when(live[0] & live[1])
            def _():
                for hh in range(2):
                    head_step(hh, slot, False)

            for hh in range(2):
                @pl.when(live[hh] & jnp.logical_not(live[1 - hh]))
                def _():
                    head_step(hh, slot, False, paired=False)
            return carry

        lax.fori_loop(0, n, loop, 0)

        @pl.when(qi + 1 < nb)
        def _():
            stream.start(qi + 1, 0)

        lane = lax.broadcasted_iota(jnp.int32, (tb, 128), 1)
        outs = []
        for hh in range(2):
            acc = acc_s[hh]
            b0 = _bias_lane(hh)
            l = jnp.broadcast_to(acc[:, b0:b0 + 1], (tb, 128))
            outs.append(acc / l)
            lse = m_s[hh] + jnp.log(l)
            lse_ref[hh, 0] = jnp.min(lse, axis=0, keepdims=True)
            qf = qa_ref[hh].astype(F32)
            c_t = jnp.sum(jnp.where((lane >= b0) & (lane < b0 + 3), qf, 0.0), axis=-1, keepdims=True)
            cb = jnp.broadcast_to(c_t, (tb, 128)) - lse
            qb_ref[hh] = _lane_fill(lane, qf, _split3(cb), b0).astype(BF16)
        o_ref[...] = _pair_lanes(lane, outs[0], outs[1])

    qs = pl.BlockSpec((2, tb, 128), lambda p, i: (p, i, 0))
    return pl.pallas_call(
        body, name=name, grid=(FOX_PAIRS, nb),
        in_specs=[qs, ANY, ANY, SMEM],
        out_specs=[pl.BlockSpec((tb, 128), lambda p, i: (i, p)), qs,
                   pl.BlockSpec((2, 1, 1, 128), lambda p, i: (p, i, 0, 0))],
        out_shape=[jax.ShapeDtypeStruct((S, FOX_HEADS * FOX_DH), F32), jax.ShapeDtypeStruct((FOX_HEADS, S, 128), BF16),
                   jax.ShapeDtypeStruct((FOX_HEADS, nb, 1, 128), F32)],
        scratch_shapes=[pltpu.VMEM((2, 2, tb, 128), BF16), pltpu.VMEM((2, 2, tb, 128), BF16),
                        pltpu.SemaphoreType.DMA((2, 2)), pltpu.VMEM((2, tb, 128), F32), pltpu.VMEM((2, tb, 128), F32),
                        pltpu.SMEM((2,), F32)],
        compiler_params=_cparams(("arbitrary", "arbitrary")),
    )(qa, ka, va, bounds)


def _fox_bwd_prep(o, do, *, name, T=512):
    S = o.shape[0]
    T = min(T, S)

    def body(o_ref, do_ref, dob_ref):
        lane = lax.broadcasted_iota(jnp.int32, (T, 128), 1)
        d = do_ref[...].astype(F32)
        prod = d * o_ref[...]
        for hh in range(2):
            mine = _data_lanes(lane, hh)
            delta = jnp.sum(jnp.where(mine, prod, 0.0), axis=-1, keepdims=True)
            dob_ref[hh] = _lane_fill(lane, jnp.where(mine, d, 0.0), _split3(-delta), _bias_lane(hh)).astype(BF16)

    tok = pl.BlockSpec((T, 128), lambda p, t: (t, p))
    return pl.pallas_call(
        body, name=name, grid=(FOX_PAIRS, S // T),
        in_specs=[tok, tok], out_specs=pl.BlockSpec((2, T, 128), lambda p, t: (p, t, 0)),
        out_shape=jax.ShapeDtypeStruct((FOX_HEADS, S, 128), BF16),
        compiler_params=_cparams(("parallel", "parallel")),
    )(o, do)


def _fox_bwd_dq(qb, ka, va, dob, bounds, lse_min, *, name, comm=None):
    S = qb.shape[1]
    tb = min(FOX_TB, S)
    nb = S // tb
    nc = comm.n if comm is not None else 0

    def body(qb_ref, dob_ref, ka_hbm, va_hbm, st_ref, lm_ref, *rest):
        c_in, (dq_ref, dob2_ref), c_out = rest[:nc], rest[nc:nc + 2], rest[nc + 2:2 * nc + 2]
        kbuf, vbuf, sems, acc_s = rest[2 * nc + 2:2 * nc + 6]
        c_sems = rest[2 * nc + 6:]
        pair, qi = pl.program_id(0), pl.program_id(1)
        if comm is not None:
            @pl.when((pair == 0) & (qi == 0))
            def _():
                comm.start(c_in, c_out, c_sems)

        stream = _BlockStream((ka_hbm, va_hbm), (kbuf, vbuf), sems, pair, tb)
        k0 = [_first_live_key(st_ref, 2 * pair + hh, nb, qi, qi - 1, lm_ref[(2 * pair + hh) * nb + qi] - FOX_SKIP)
              for hh in range(2)]
        n = qi - jnp.minimum(k0[0], k0[1]) + 1

        @pl.when(qi == 0)
        def _():
            stream.start(qi, 0)

        acc_s[...] = jnp.zeros_like(acc_s)

        def head_step(hh, slot, k, masked):
            s = _dot(qb_ref[hh], kbuf[slot, hh], _DIMS["nt"])
            if masked:
                row = lax.broadcasted_iota(jnp.int32, (tb, tb), 0)
                col = lax.broadcasted_iota(jnp.int32, (tb, tb), 1)
                s = jnp.where(col <= row, s, NEG)
            ds = jnp.exp(s) * _dot(dob_ref[hh], vbuf[slot, hh], _DIMS["nt"])
            acc_s[hh] += _dot(ds.astype(BF16), kbuf[slot, hh], _DIMS["nn"])

        def loop(t, carry):
            k = qi - t
            slot = t % 2
            stream.wait(k, slot)

            @pl.when(t + 1 < n)
            def _():
                stream.start(k - 1, 1 - slot)

            @pl.when(t == 0)
            def _():
                for hh in range(2):
                    head_step(hh, slot, k, True)

            live = [(t > 0) & (k >= k0[hh]) for hh in range(2)]

            @pl.when(live[0] & live[1])
            def _():
                for hh in range(2):
                    head_step(hh, slot, k, False)

            for hh in range(2):
                @pl.when(live[hh] & jnp.logical_not(live[1 - hh]))
                def _():
                    head_step(hh, slot, k, False)
            return carry

        lax.fori_loop(0, n, loop, 0)

        @pl.when(qi + 1 < nb)
        def _():
            stream.start(qi + 1, 0)

        lane = lax.broadcasted_iota(jnp.int32, (tb, 128), 1)
        dq_ref[...] = (_pair_lanes(lane, acc_s[0], acc_s[1]) * FOX_SCALE).astype(dq_ref.dtype)
        for hh in range(2):
            b0 = _bias_lane(hh)
            r = jnp.broadcast_to(acc_s[hh][:, b0:b0 + 1], (tb, 128))
            dob2_ref[hh] = _lane_fill(lane, dob_ref[hh].astype(F32), _split3(r), b0 + RSUM_LANE).astype(BF16)
        if comm is not None:
            @pl.when((pair == FOX_PAIRS - 1) & (qi == nb - 1))
            def _():
                comm.finish(c_in, c_out, c_sems)

    qs = pl.BlockSpec((2, tb, 128), lambda p, i: (p, i, 0))
    outs = pl.pallas_call(
        body, name=name, grid=(FOX_PAIRS, nb),
        in_specs=[qs, qs, ANY, ANY, SMEM, SMEM] + [ANY] * nc,
        out_specs=[pl.BlockSpec((tb, 128), lambda p, i: (i, p)), qs] + [ANY] * nc,
        out_shape=[jax.ShapeDtypeStruct((S, FOX_HEADS * FOX_DH), BF16),
                   jax.ShapeDtypeStruct((FOX_HEADS, S, 128), BF16)] + (comm.out_shapes if comm is not None else []),
        scratch_shapes=[pltpu.VMEM((2, 2, tb, 128), BF16), pltpu.VMEM((2, 2, tb, 128), BF16),
                        pltpu.SemaphoreType.DMA((2, 2)), pltpu.VMEM((2, tb, 128), F32)]
        + (comm.scratch if comm is not None else []),
        compiler_params=_cparams(("arbitrary", "arbitrary")),
    )(qb, dob, ka, va, bounds, lse_min, *(comm.inputs if comm is not None else []))
    return (outs[0], outs[1]) if comm is None else (outs[0], outs[1], outs[2:])


def _fox_bwd_dkv(qb, ka, va, dob, bounds, lse_min, *, name):
    S = qb.shape[1]
    tb = min(FOX_TB, S)
    nb = S // tb

    def body(ka_ref, va_ref, qb_hbm, dob_hbm, st_ref, lm_ref, dk_ref, dv_ref, dcs_ref, qbuf, dbuf, sems, dk_s, dv_s):
        pair, ki = pl.program_id(0), pl.program_id(1)
        stream = _BlockStream((qb_hbm, dob_hbm), (qbuf, dbuf), sems, pair, tb)
        i1 = [_last_live_query(st_ref, lm_ref, 2 * pair + hh, nb, ki) for hh in range(2)]
        n = jnp.maximum(i1[0], i1[1]) - ki + 1

        @pl.when(ki == 0)
        def _():
            stream.start(ki, 0)

        dk_s[...] = jnp.zeros_like(dk_s)
        dv_s[...] = jnp.zeros_like(dv_s)

        def head_step(hh, slot, masked):
            st = _dot(ka_ref[hh], qbuf[slot, hh], _DIMS["nt"])
            if masked:
                row = lax.broadcasted_iota(jnp.int32, (tb, tb), 0)
                col = lax.broadcasted_iota(jnp.int32, (tb, tb), 1)
                st = jnp.where(row <= col, st, NEG)
            pt = jnp.exp(st)
            dst = pt * _dot(va_ref[hh], dbuf[slot, hh], _DIMS["nt"])
            dv_s[hh] += _dot(pt.astype(BF16), dbuf[slot, hh], _DIMS["nn"])
            dk_s[hh] += _dot(dst.astype(BF16), qbuf[slot, hh], _DIMS["nn"])

        def loop(t, carry):
            i = ki + t
            slot = t % 2
            stream.wait(i, slot)

            @pl.when(t + 1 < n)
            def _():
                stream.start(i + 1, 1 - slot)

            @pl.when(t == 0)
            def _():
                for hh in range(2):
                    head_step(hh, slot, True)

            live = [(t > 0) & (i <= i1[hh]) for hh in range(2)]

            @pl.when(live[0] & live[1])
            def _():
                for hh in range(2):
                    head_step(hh, slot, False)

            for hh in range(2):
                @pl.when(live[hh] & jnp.logical_not(live[1 - hh]))
                def _():
                    head_step(hh, slot, False)
            return carry

        lax.fori_loop(0, n, loop, 0)

        @pl.when(ki + 1 < nb)
        def _():
            stream.start(ki + 1, 0)

        lane = lax.broadcasted_iota(jnp.int32, (tb, 128), 1)
        dk_ref[...] = _pair_lanes(lane, dk_s[0], dk_s[1]).astype(dk_ref.dtype)
        dv_ref[...] = _pair_lanes(lane, dv_s[0], dv_s[1]).astype(dv_ref.dtype)
        for hh in range(2):
            b0 = _bias_lane(hh)
            dk_a, dv_a = dk_s[hh], dv_s[hh]
            off = dv_a[:, b0 + RSUM_LANE:b0 + RSUM_LANE + 1] + dv_a[:, b0 + RSUM_LANE + 1:b0 + RSUM_LANE + 2] \
                + dv_a[:, b0 + RSUM_LANE + 2:b0 + RSUM_LANE + 3]
            dcs_ref[0, :, hh:hh + 1] = dk_a[:, b0 + 3:b0 + 4] - off

    ks = pl.BlockSpec((2, tb, 128), lambda p, j: (p, j, 0))
    tok = pl.BlockSpec((tb, 128), lambda p, j: (j, p))
    big = jax.ShapeDtypeStruct((S, FOX_HEADS * FOX_DH), BF16)
    return pl.pallas_call(
        body, name=name, grid=(FOX_PAIRS, nb),
        in_specs=[ks, ks, ANY, ANY, SMEM, SMEM],
        out_specs=[tok, tok, pl.BlockSpec((1, tb, 2), lambda p, j: (p, j, 0))],
        out_shape=[big, big, jax.ShapeDtypeStruct((FOX_PAIRS, S, 2), F32)],
        scratch_shapes=[pltpu.VMEM((2, 2, tb, 128), BF16), pltpu.VMEM((2, 2, tb, 128), BF16),
                        pltpu.SemaphoreType.DMA((2, 2)), pltpu.VMEM((2, tb, 128), F32), pltpu.VMEM((2, tb, 128), F32)],
        compiler_params=_cparams(("arbitrary", "arbitrary")),
    )(ka, va, qb, dob, bounds, lse_min)


def _merge_fwd(proj, pa, pb, *, name, T=512):
    S, D = pa.shape
    T = min(T, S)

    def body(ga_ref, gb_ref, pa_ref, pb_ref, m_ref):
        m_ref[...] = (_sigmoid(ga_ref[...]) * pa_ref[...] + _sigmoid(gb_ref[...]) * pb_ref[...]).astype(m_ref.dtype)

    tok = pl.BlockSpec((T, D), lambda i: (i, 0))
    return pl.pallas_call(
        body, name=name, grid=(S // T,),
        in_specs=[pl.BlockSpec((T, D), lambda i: (i, 7)), pl.BlockSpec((T, D), lambda i: (i, 8)), tok, tok],
        out_specs=tok, out_shape=jax.ShapeDtypeStruct((S, D), BF16),
        compiler_params=_cparams(("parallel",)),
    )(proj, proj, pa, pb)


def _merge_bwd(proj, pa, pb, dm, *, name, T=512):
    S, D = pa.shape
    T = min(T, S)

    def body(ga_ref, gb_ref, pa_ref, pb_ref, dm_ref, dpa_ref, dpb_ref, dga_ref, dgb_ref):
        dm_ = dm_ref[...]
        sa, sb = _sigmoid(ga_ref[...]), _sigmoid(gb_ref[...])
        dpa_ref[...] = (dm_ * sa).astype(BF16)
        dpb_ref[...] = (dm_ * sb).astype(BF16)
        dga_ref[...] = (dm_ * pa_ref[...] * sa * (1.0 - sa)).astype(BF16)
        dgb_ref[...] = (dm_ * pb_ref[...] * sb * (1.0 - sb)).astype(BF16)

    tok = pl.BlockSpec((T, D), lambda i: (i, 0))
    big = jax.ShapeDtypeStruct((S, D), BF16)
    return pl.pallas_call(
        body, name=name, grid=(S // T,),
        in_specs=[pl.BlockSpec((T, D), lambda i: (i, 7)), pl.BlockSpec((T, D), lambda i: (i, 8)), tok, tok, tok],
        out_specs=[tok, tok, tok, tok], out_shape=[big, big, big, big],
        compiler_params=_cparams(("parallel",)),
    )(proj, proj, pa, pb, dm)


INV_SQRT2 = 0.7071067811865476
INV_SQRT2PI = 0.3989422804014327


def _shifted(u, prev, rid):
    m1 = jnp.where(rid == 0, prev[7:8, :], pltpu.roll(u, 1, 0))
    m2 = jnp.where(rid == 0, prev[6:7, :], jnp.where(rid == 1, prev[7:8, :], pltpu.roll(u, 2, 0)))
    return m1, m2


def _conv_acc(u, prev, w_ref, b_ref, rid):
    m1, m2 = _shifted(u, prev, rid)
    return b_ref[...] + w_ref[0:1, :] * m2 + w_ref[1:2, :] * m1 + w_ref[2:3, :] * u, m1, m2


def _convglu_fwd(ug, uv, wg, wv, bg, bv, *, name, T=512, tc=256):
    S, F = ug.shape
    T = min(T, S)

    def body(ug_ref, uv_ref, wg_ref, wv_ref, bg_ref, bv_ref, a_ref, pg, pv):
        @pl.when(pl.program_id(1) == 0)
        def _():
            pg[...] = jnp.zeros_like(pg)
            pv[...] = jnp.zeros_like(pv)

        rid = lax.broadcasted_iota(jnp.int32, (T, tc), 0)
        g_, v_ = ug_ref[...], uv_ref[...]
        accg, _, _ = _conv_acc(g_, pg[...], wg_ref, bg_ref, rid)
        accv, _, _ = _conv_acc(v_, pv[...], wv_ref, bv_ref, rid)
        gel = 0.5 * accg * (1.0 + lax.erf(accg * INV_SQRT2))
        a_ref[...] = (gel * accv).astype(a_ref.dtype)
        pg[...] = g_[T - 8:T, :]
        pv[...] = v_[T - 8:T, :]

    tok = pl.BlockSpec((T, tc), lambda j, t: (t, j))
    w3 = pl.BlockSpec((3, tc), lambda j, t: (0, j))
    b1 = pl.BlockSpec((1, tc), lambda j, t: (0, j))
    return pl.pallas_call(
        body, name=name, grid=(F // tc, S // T),
        in_specs=[tok, tok, w3, w3, b1, b1], out_specs=tok,
        out_shape=jax.ShapeDtypeStruct((S, F), BF16),
        scratch_shapes=[pltpu.VMEM((8, tc), F32), pltpu.VMEM((8, tc), F32)],
        compiler_params=_cparams(("parallel", "arbitrary")),
    )(ug, uv, wg, wv, bg, bv)


def _convglu_bwd(ug, uv, wg, wv, bg, bv, da, *, name, T=512, tc=256):
    S, F = ug.shape
    T = min(T, S)
    nT = S // T
    halo_blocks = T // 8

    def up_shift(d, nx, rid):
        p1 = jnp.where(rid == T - 1, nx[0:1, :], pltpu.roll(d, T - 1, 0))
        p2 = jnp.where(rid == T - 1, nx[1:2, :], jnp.where(rid == T - 2, nx[0:1, :], pltpu.roll(d, T - 2, 0)))
        return p1, p2

    def body(ug_ref, uv_ref, hg_ref, hv_ref, wg_ref, wv_ref, bg_ref, bv_ref, da_ref,
             dug_ref, duv_ref, dwg_ref, dwv_ref, dbg_ref, dbv_ref, ng, nv):
        @pl.when(pl.program_id(1) == 0)
        def _():
            ng[...] = jnp.zeros_like(ng)
            nv[...] = jnp.zeros_like(nv)
            for r in (dwg_ref, dwv_ref, dbg_ref, dbv_ref):
                r[...] = jnp.zeros_like(r)

        first_block = pl.program_id(1) == nT - 1
        rid = lax.broadcasted_iota(jnp.int32, (T, tc), 0)
        g_, v_ = ug_ref[...], uv_ref[...]
        pg = jnp.where(first_block, 0.0, hg_ref[...])
        pv = jnp.where(first_block, 0.0, hv_ref[...])
        accg, g1, g2 = _conv_acc(g_, pg, wg_ref, bg_ref, rid)
        accv, v1, v2 = _conv_acc(v_, pv, wv_ref, bv_ref, rid)
        cdf = 0.5 * (1.0 + lax.erf(accg * INV_SQRT2))
        pdf = INV_SQRT2PI * jnp.exp(-0.5 * accg * accg)
        da_ = da_ref[...].astype(F32)
        dgate = da_ * accv * (cdf + accg * pdf)
        dval = da_ * (accg * cdf)
        dbg_ref[...] += jnp.sum(dgate, axis=0, keepdims=True)
        dbv_ref[...] += jnp.sum(dval, axis=0, keepdims=True)
        for j, (sg_, sv_) in enumerate(((g2, v2), (g1, v1), (g_, v_))):
            dwg_ref[j:j + 1, :] += jnp.sum(dgate * sg_, axis=0, keepdims=True)
            dwv_ref[j:j + 1, :] += jnp.sum(dval * sv_, axis=0, keepdims=True)
        for d, w_ref, nx, out_ref in ((dgate, wg_ref, ng, dug_ref), (dval, wv_ref, nv, duv_ref)):
            p1, p2 = up_shift(d, nx[...], rid)
            out_ref[...] = (w_ref[2:3, :] * d + w_ref[1:2, :] * p1 + w_ref[0:1, :] * p2).astype(out_ref.dtype)
            nx[...] = d[0:8, :]

    tok = pl.BlockSpec((T, tc), lambda j, t: (nT - 1 - t, j))
    halo = pl.BlockSpec((8, tc), lambda j, t: (jnp.maximum((nT - 1 - t) * halo_blocks - 1, 0), j))
    w3 = pl.BlockSpec((3, tc), lambda j, t: (0, j))
    b1 = pl.BlockSpec((1, tc), lambda j, t: (0, j))
    big = jax.ShapeDtypeStruct((S, F), BF16)
    return pl.pallas_call(
        body, name=name, grid=(F // tc, nT),
        in_specs=[tok, tok, halo, halo, w3, w3, b1, b1, tok], out_specs=[tok, tok, w3, w3, b1, b1],
        out_shape=[big, big, jax.ShapeDtypeStruct((3, F), F32), jax.ShapeDtypeStruct((3, F), F32),
                   jax.ShapeDtypeStruct((1, F), F32), jax.ShapeDtypeStruct((1, F), F32)],
        scratch_shapes=[pltpu.VMEM((8, tc), F32), pltpu.VMEM((8, tc), F32)],
        compiler_params=_cparams(("parallel", "arbitrary")),
    )(ug, uv, ug, uv, wg, wv, bg, bv, da)


FF_LO = 7168
IN_SHARD = 1154
FF_DEV, FF_OFF = FF_LO // IN_SHARD, FF_LO % IN_SHARD


def _col_blocks(a, width):
    return jnp.stack([a[:, d * width:(d + 1) * width] for d in range(N_DEV)])


def _w_in_blocks(d_wm, d_wff):
    def block(d):
        lo = d * IN_SHARD
        if d < FF_DEV:
            return d_wm[:, lo:lo + IN_SHARD]
        if d > FF_DEV:
            return d_wm[:, lo - FOX_HEADS:lo - FOX_HEADS + IN_SHARD]
        return jnp.concatenate([d_wm[:, lo:FF_LO], d_wff[:, :FOX_HEADS], d_wm[:, FF_LO:lo + IN_SHARD - FOX_HEADS]], axis=1)

    return jnp.stack([block(d) for d in range(N_DEV)])


def _late_weights(g_a, g_b, g_o, g_up, g_cw, g_d):
    wup = jnp.concatenate([g_up[d] for d in range(N_DEV)], axis=1)
    cw = jnp.concatenate([g_cw[d] for d in range(N_DEV)], axis=1)
    return dict(wa=g_a.reshape(D_MODEL, D_MODEL), wb=g_b.reshape(D_MODEL, D_MODEL), wo=g_o.reshape(D_MODEL, D_MODEL),
                wug=wup[:, :D_FF], wuv=wup[:, D_FF:], cwg=cw[:, :D_FF], cwv=cw[:, D_FF:], wd=g_d.reshape(D_FF, D_MODEL))


def _early_grad_blocks(d_wa, d_wb, d_wo, d_wug, d_wuv, d_wd):
    up = jnp.stack([d_wug[:, d * 704:(d + 1) * 704] for d in range(4)]
                   + [d_wuv[:, d * 704:(d + 1) * 704] for d in range(4)])
    return [d_wa.reshape(N_DEV, 128, D_MODEL), d_wb.reshape(N_DEV, 128, D_MODEL), d_wo.reshape(N_DEV, 128, D_MODEL),
            up, d_wd.reshape(N_DEV, 352, D_MODEL)]


def _local_step(x, tgt, w, p, late=None, exchange=False):
    S = x.shape[0]
    mm = _matmul
    n1 = _rms_fwd(x, p["norm_mix"], name="rms1_fwd")
    if late is None:
        proj = mm(n1, w["wm"], "nn", name="proj_main")
    else:
        proj, gathered = mm(n1, w["wm"], "nn", comm=late, name="proj_main")
        w = {**w, **_late_weights(*gathered)}
    ff = mm(n1, w["wff"], "nn", name="proj_ff")
    lb = _lb_fwd(p["hg_lb_logits"], name="lb_fwd")
    gnorm = p["hg_norm"].reshape(1, HG_DV)
    o_hg, oa, states = _hgrn_fwd(proj, lb, gnorm, name="hgrn_fwd")
    bias = jnp.pad(p["fox_f_bias"].reshape(1, FOX_HEADS), ((0, 0), (0, 128 - FOX_HEADS)))
    c = _fox_gate_fwd(ff, bias, name="fox_gate_fwd")
    qa, ka, va, fox_stats = _fox_prep(proj, c, name="fox_prep")
    bounds = fox_stats[:, :, 0, :N_STAT].reshape(-1)
    ob, qb, lse_stats = _fox_fwd(qa, ka, va, bounds, name="fox_fwd")
    lse_min = lse_stats[:, :, 0, 0].reshape(-1)
    pa = mm(oa, w["wa"], "nn", name="branch_a")
    pb = mm(ob, w["wb"], "nn", name="branch_b")
    merged = _merge_fwd(proj, pa, pb, name="merge_fwd")
    h1 = mm(merged, w["wo"], "nn", addend=x, name="mix_out")
    n2 = _rms_fwd(h1, p["norm_ffn"], name="rms2_fwd")
    ug = mm(n2, w["wug"], "nn", name="up_gate")
    uv = mm(n2, w["wuv"], "nn", name="up_val")
    a = _convglu_fwd(ug, uv, w["cwg"], w["cwv"], p["cbg"], p["cbv"], name="convglu_fwd")
    h2 = mm(a, w["wd"], "nn", addend=h1, name="ffn_down")
    loss, dh2, d_norm_final = _loss_head(h2, p["norm_final"], tgt, name="loss_head")
    da = mm(dh2, w["wd"], "nt", out_dtype=BF16, name="d_act")
    d_wd = mm(a, dh2, "tn", out_dtype=BF16, name="dw_down")
    dug, duv, d_cwg, d_cwv, d_cbg, d_cbv = _convglu_bwd(
        ug, uv, w["cwg"], w["cwv"], p["cbg"], p["cbv"], da, name="convglu_bwd")
    dn2 = mm(dug, w["wug"], "nt", name="dn2_gate")
    dn2 = mm(duv, w["wuv"], "nt", addend=dn2, name="dn2_val")
    d_wug = mm(n2, dug, "tn", out_dtype=BF16, name="dw_up_gate")
    d_wuv = mm(n2, duv, "tn", out_dtype=BF16, name="dw_up_val")
    dh1, d_norm_ffn = _rms_bwd(h1, p["norm_ffn"], dn2, dh2, name="rms2_bwd")
    dmerged = mm(dh1, w["wo"], "nt", name="d_merged")
    d_wo = mm(merged, dh1, "tn", out_dtype=BF16, name="dw_out")
    dpa, dpb, dga, dgb = _merge_bwd(proj, pa, pb, dmerged, name="merge_bwd")
    doa = mm(dpa, w["wa"], "nt", name="d_oa")
    dob = mm(dpb, w["wb"], "nt", out_dtype=BF16, name="d_ob")
    d_wa = mm(oa, dpa, "tn", out_dtype=BF16, name="dw_branch_a")
    d_wb = mm(ob, dpb, "tn", out_dtype=BF16, name="dw_branch_b")
    dhq, dhf, dhi, dhg, dlb, dgn8 = _hgrn_bwd(proj, lb, gnorm, o_hg, states, doa, name="hgrn_bwd")
    d_logits = _lb_bwd(p["hg_lb_logits"], dlb, name="lb_bwd")
    dob_hm = _fox_bwd_prep(ob, dob, name="fox_bwd_prep")
    early_parts = None
    if exchange:
        comm = _ExchangeComm(_early_grad_blocks(d_wa, d_wb, d_wo, d_wug, d_wuv, d_wd))
        dq, dob2, early_parts = _fox_bwd_dq(qb, ka, va, dob_hm, bounds, lse_min, comm=comm, name="fox_bwd_dq")
    else:
        dq, dob2 = _fox_bwd_dq(qb, ka, va, dob_hm, bounds, lse_min, name="fox_bwd_dq")
    dk, dv, dcs = _fox_bwd_dkv(qb, ka, va, dob2, bounds, lse_min, name="fox_bwd_dkv")
    dcs_tok = jnp.pad(dcs.transpose(1, 0, 2).reshape(S, FOX_HEADS), ((0, 0), (0, 128 - FOX_HEADS)))
    dff, dbias = _fox_gate_bwd(ff, bias, dcs_tok, name="fox_gate_bwd")
    dproj = jnp.concatenate([dhq, dhf, dhi, dhg, dq, dk, dv, dga, dgb], axis=1)
    d_wm = mm(n1, dproj, "tn", out_dtype=BF16, name="dw_in_main")
    d_wff = mm(n1, dff, "tn", out_dtype=BF16, name="dw_in_ff")
    dn1 = mm(dff, w["wff"], "nt", name="dn1_ff")
    late_parts = None
    if exchange:
        d_cw = jnp.concatenate([d_cwg, d_cwv], axis=1)
        comm = _ExchangeComm([_w_in_blocks(d_wm, d_wff), _col_blocks(d_cw, 704)])
        dn1, late_parts = mm(dproj, w["wm"], "nt", addend=dn1, comm=comm, name="dn1_main")
    else:
        dn1 = mm(dproj, w["wm"], "nt", addend=dn1, name="dn1_main")
    dx, d_norm_mix = _rms_bwd(x, p["norm_mix"], dn1, dh1, name="rms1_bwd")
    grads = dict(
        wm=d_wm, wff=d_wff, wa=d_wa, wb=d_wb, wo=d_wo, wug=d_wug, wuv=d_wuv, cwg=d_cwg, cwv=d_cwv, wd=d_wd,
        norm_mix=d_norm_mix.reshape(-1), fox_f_bias=dbias[0, :FOX_HEADS], hg_lb_logits=d_logits,
        hg_norm=jnp.sum(dgn8, axis=0).reshape(-1), norm_ffn=d_norm_ffn.reshape(-1), cbg=d_cbg, cbv=d_cbv,
        norm_final=d_norm_final.reshape(-1), early_parts=early_parts, late_parts=late_parts)
    return loss, dx, grads


SMALL = [("norm_mix", (1, D_MODEL)), ("fox_f_bias", (1, FOX_HEADS)), ("hg_lb_logits", (2, HG_HEADS * HG_DK)),
         ("hg_norm", (1, HG_DV)), ("norm_ffn", (1, D_MODEL)), ("conv_b", (1, 2 * D_FF)), ("norm_final", (D_MODEL,))]
SMALL_ROWS = 88
SHARDED = [("w_in", (D_MODEL, 1154), 256), ("w_branch_a", (128, D_MODEL), 128), ("w_branch_b", (128, D_MODEL), 128),
           ("w_out", (128, D_MODEL), 128), ("w_up", (D_MODEL, 704), 256), ("conv_w", (3, 704), 3),
           ("w_down", (352, D_MODEL), 352)]
NAMES = ["norm_mix", "w_in", "fox_f_bias", "hg_lb_logits", "hg_norm", "w_branch_a", "w_branch_b", "w_out",
         "norm_ffn", "w_up", "conv_w", "conv_b", "w_down", "norm_final"]


def _size(shape):
    n = 1
    for s in shape:
        n *= s
    return n


def _adamw(parts, w, m, v, *, name, T):
    R, C = w.shape
    c1 = 1.0 / (1.0 - ADAM_B1 ** ADAM_STEP)
    c2 = 1.0 / (1.0 - ADAM_B2 ** ADAM_STEP)

    def body(p_ref, w_ref, m_ref, v_ref, g_ref, d_ref, nm_ref, nv_ref):
        g = p_ref[0].astype(F32)
        for s in range(1, N_DEV):
            g = g + p_ref[s].astype(F32)
        g_ref[...] = g
        nm = ADAM_B1 * m_ref[...] + (1.0 - ADAM_B1) * g
        nv = ADAM_B2 * v_ref[...] + (1.0 - ADAM_B2) * (g * g)
        nm_ref[...] = nm
        nv_ref[...] = nv
        d_ref[...] = -ADAM_LR * ((nm * c1) / (jnp.sqrt(nv * c2) + ADAM_EPS) + ADAM_WD * w_ref[...])

    blk = pl.BlockSpec((T, C), lambda i: (i, 0))
    out = jax.ShapeDtypeStruct((R, C), F32)
    return pl.pallas_call(
        body, name=name, grid=(R // T,),
        in_specs=[pl.BlockSpec((N_DEV, T, C), lambda i: (0, i, 0)), blk, blk, blk],
        out_specs=[blk, blk, blk, blk], out_shape=[out, out, out, out],
        compiler_params=_cparams(("parallel",)),
    )(parts, w, m, v)


def _pack_small(vals):
    flat = jnp.concatenate([vals[n].reshape(-1).astype(F32) for n, _ in SMALL])
    return jnp.pad(flat, (0, SMALL_ROWS * 128 - flat.shape[0])).reshape(SMALL_ROWS, 128)


def _unpack_small(buf):
    flat, out, off = buf.reshape(-1), {}, 0
    for n, shape in SMALL:
        out[n] = flat[off:off + _size(shape)].reshape(shape)
        off += _size(shape)
    return out


def kernel(x, norm_mix, w_in, fox_f_bias, hg_lb_logits, hg_norm, w_branch_a, w_branch_b, w_out, norm_ffn, w_up, conv_w, conv_b, w_down, norm_final, loss_target, m_norm_mix, m_w_in, m_fox_f_bias, m_hg_lb_logits, m_hg_norm, m_w_branch_a, m_w_branch_b, m_w_out, m_norm_ffn, m_w_up, m_conv_w, m_conv_b, m_w_down, m_norm_final, v_norm_mix, v_w_in, v_fox_f_bias, v_hg_lb_logits, v_hg_norm, v_w_branch_a, v_w_branch_b, v_w_out, v_norm_ffn, v_w_up, v_conv_w, v_conv_b, v_w_down, v_norm_final):
    wv = dict(norm_mix=norm_mix, w_in=w_in, fox_f_bias=fox_f_bias, hg_lb_logits=hg_lb_logits, hg_norm=hg_norm,
              w_branch_a=w_branch_a, w_branch_b=w_branch_b, w_out=w_out, norm_ffn=norm_ffn, w_up=w_up, conv_w=conv_w,
              conv_b=conv_b, w_down=w_down, norm_final=norm_final)
    mv = dict(norm_mix=m_norm_mix, w_in=m_w_in, fox_f_bias=m_fox_f_bias, hg_lb_logits=m_hg_lb_logits, hg_norm=m_hg_norm,
              w_branch_a=m_w_branch_a, w_branch_b=m_w_branch_b, w_out=m_w_out, norm_ffn=m_norm_ffn, w_up=m_w_up,
              conv_w=m_conv_w, conv_b=m_conv_b, w_down=m_w_down, norm_final=m_norm_final)
    vv = dict(norm_mix=v_norm_mix, w_in=v_w_in, fox_f_bias=v_fox_f_bias, hg_lb_logits=v_hg_lb_logits, hg_norm=v_hg_norm,
              w_branch_a=v_w_branch_a, w_branch_b=v_w_branch_b, w_out=v_w_out, norm_ffn=v_norm_ffn, w_up=v_w_up,
              conv_w=v_conv_w, conv_b=v_conv_b, w_down=v_w_down, norm_final=v_norm_final)

    (g_in,) = _comm_call(_GatherComm([w_in[0].astype(BF16)]), name="gather_w_in")
    w = dict(wm=jnp.concatenate([g_in[d] for d in range(FF_DEV)]
                                + [g_in[FF_DEV][:, :FF_OFF], g_in[FF_DEV][:, FF_OFF + FOX_HEADS:]]
                                + [g_in[d] for d in range(FF_DEV + 1, N_DEV)], axis=1),
             wff=jnp.pad(g_in[FF_DEV][:, FF_OFF:FF_OFF + FOX_HEADS], ((0, 0), (0, 128 - FOX_HEADS))))
    late = _GatherComm([w_branch_a[0].astype(BF16), w_branch_b[0].astype(BF16), w_out[0].astype(BF16),
                        w_up[0].astype(BF16), conv_w[0], w_down[0].astype(BF16)])
    p = dict(norm_mix=norm_mix[0], fox_f_bias=fox_f_bias[0], hg_lb_logits=hg_lb_logits, hg_norm=hg_norm[0],
             norm_ffn=norm_ffn[0], cbg=conv_b[:, :D_FF], cbv=conv_b[:, D_FF:], norm_final=norm_final)
    loss, dx, grads = _local_step(x[0], loss_target[0], w, p, late=late, exchange=True)
    loss = lax.psum(loss[0, 0], ("x", "y", "c"))

    small = _pack_small(dict(
        norm_mix=grads["norm_mix"], fox_f_bias=grads["fox_f_bias"], hg_lb_logits=grads["hg_lb_logits"],
        hg_norm=grads["hg_norm"], norm_ffn=grads["norm_ffn"], conv_b=jnp.concatenate([grads["cbg"], grads["cbv"]], axis=1),
        norm_final=grads["norm_final"]))
    (small_parts,) = _comm_call(_ExchangeComm([jnp.broadcast_to(small[None], (N_DEV, SMALL_ROWS, 128))]),
                                name="exchange_small")
    ea, eb, eo, eup, ed = grads["early_parts"]
    p_in, p_cw = grads["late_parts"]
    parts = [p_in, ea, eb, eo, eup, p_cw, ed, small_parts]
    res = {}
    for (n, shape, tile), part in zip(SHARDED, parts):
        outs = _adamw(part, wv[n].reshape(shape), mv[n].reshape(shape), vv[n].reshape(shape), name="adamw_" + n, T=tile)
        res[n] = [o.reshape(wv[n].shape) for o in outs]
    outs = _adamw(parts[-1], _pack_small(wv), _pack_small(mv), _pack_small(vv), name="adamw_small", T=SMALL_ROWS)
    small = [_unpack_small(o) for o in outs]
    for n, _ in SMALL:
        res[n] = [s[n] for s in small]
    return (loss, dx[None], *[res[n][0] for n in NAMES], *[res[n][1] for n in NAMES],
            *[res[n][2] for n in NAMES], *[res[n][3] for n in NAMES])
```

```python
import jax
import jax.numpy as jnp
from jax import lax
from jax.experimental import pallas as pl
from jax.experimental.pallas import tpu as pltpu

F32 = jnp.float32
BF16 = jnp.bfloat16

D_MODEL = 1024
HG_HEADS = 8
HG_DK = 128
HG_DV = 128
HG_CHUNK = 64
FOX_HEADS = 16
FOX_DH = 64
D_FF = 2816
EPS = 1e-6
N_DEV = 8

ADAM_LR = 0.001
ADAM_B1 = 0.9
ADAM_B2 = 0.999
ADAM_EPS = 1e-08
ADAM_WD = 0.01
ADAM_STEP = 10

VMEM_LIMIT = 56 * 1024 * 1024


def _cparams(sem):
    return pltpu.CompilerParams(dimension_semantics=sem, vmem_limit_bytes=VMEM_LIMIT)


MESH = pl.DeviceIdType.MESH
ANY = pl.BlockSpec(memory_space=pl.ANY)
SMEM = pl.BlockSpec(memory_space=pltpu.SMEM)


class _GatherComm:
    def __init__(self, shards):
        self.inputs = list(shards)
        n = self.n = len(shards)
        self.out_shapes = [jax.ShapeDtypeStruct((N_DEV,) + s.shape, s.dtype) for s in shards]
        self.scratch = [pltpu.SemaphoreType.DMA((n, 7)), pltpu.SemaphoreType.DMA((n, 7)), pltpu.SemaphoreType.DMA((n,))]

    def _parts(self, x_refs, out_refs, sems):
        send_sems, recv_sems, local_sems = sems
        x, y, c = lax.axis_index("x"), lax.axis_index("y"), lax.axis_index("c")
        me, sibling = (x, y, c), (x, y, 1 - c)
        chips = [(1 - x, y), (x, 1 - y), (1 - x, 1 - y)]

        def copy(t, k, block, to, src=None):
            slot = out_refs[t].at[4 * block[0] + 2 * block[1] + block[2]]
            return pltpu.make_async_remote_copy(
                src_ref=slot if src is None else src, dst_ref=slot,
                send_sem=send_sems.at[t, k], recv_sem=recv_sems.at[t, k], device_id=to, device_id_type=MESH)

        mine = [pltpu.make_async_copy(x_refs[t], out_refs[t].at[4 * x + 2 * y + c], local_sems.at[t])
                for t in range(self.n)]
        first = []
        for t in range(self.n):
            first.append(copy(t, 0, me, sibling, src=x_refs[t]))
            first += [copy(t, 1 + j, me, (*chip, c), src=x_refs[t]) for j, chip in enumerate(chips)]
        return c, me, sibling, chips, copy, mine, first

    def start(self, x_refs, out_refs, sems):
        _, _, _, _, _, mine, first = self._parts(x_refs, out_refs, sems)
        for cp in mine + first:
            cp.start()

    def finish(self, x_refs, out_refs, sems):
        c, me, sibling, chips, copy, mine, first = self._parts(x_refs, out_refs, sems)
        passed = []
        for j, chip in enumerate(chips):
            for t in range(self.n):
                copy(t, 1 + j, (*chip, c), me).wait_recv()
                passed.append(copy(t, 4 + j, (*chip, c), sibling))
                passed[-1].start()
        for t in range(self.n):
            copy(t, 0, sibling, me).wait_recv()
            for j, chip in enumerate(chips):
                copy(t, 4 + j, (*chip, 1 - c), me).wait_recv()
        for cp in first + passed:
            cp.wait_send()
        for cp in mine:
            cp.wait()


class _ExchangeComm:
    def __init__(self, blocks):
        self.inputs = list(blocks)
        n = self.n = len(blocks)
        self.out_shapes = [jax.ShapeDtypeStruct(b.shape, b.dtype) for b in blocks]
        self.scratch = [pltpu.SemaphoreType.DMA((n, 7)), pltpu.SemaphoreType.DMA((n, 7)), pltpu.SemaphoreType.DMA((n,))]

    def _parts(self, g_refs, out_refs, sems):
        send_sems, recv_sems, local_sems = sems
        x, y, c = lax.axis_index("x"), lax.axis_index("y"), lax.axis_index("c")
        me = 4 * x + 2 * y + c
        mine = [pltpu.make_async_copy(g_refs[t].at[me], out_refs[t].at[me], local_sems.at[t]) for t in range(self.n)]
        sends, recvs = [], []
        for k in range(1, N_DEV):
            px = 1 - x if k & 4 else x
            py = 1 - y if k & 2 else y
            pc = 1 - c if k & 1 else c
            p = 4 * px + 2 * py + pc
            for t in range(self.n):
                sends.append(pltpu.make_async_remote_copy(
                    src_ref=g_refs[t].at[p], dst_ref=out_refs[t].at[me], send_sem=send_sems.at[t, k - 1],
                    recv_sem=recv_sems.at[t, k - 1], device_id=(px, py, pc), device_id_type=MESH))
                recvs.append(pltpu.make_async_remote_copy(
                    src_ref=g_refs[t].at[p], dst_ref=out_refs[t].at[p], send_sem=send_sems.at[t, k - 1],
                    recv_sem=recv_sems.at[t, k - 1], device_id=(px, py, pc), device_id_type=MESH))
        return mine, sends, recvs

    def start(self, g_refs, out_refs, sems):
        mine, sends, _ = self._parts(g_refs, out_refs, sems)
        for cp in mine + sends:
            cp.start()

    def finish(self, g_refs, out_refs, sems):
        mine, sends, recvs = self._parts(g_refs, out_refs, sems)
        for cp in recvs:
            cp.wait_recv()
        for cp in sends:
            cp.wait_send()
        for cp in mine:
            cp.wait()


def _comm_call(comm, *, name):
    n = comm.n

    def body(*refs):
        comm.start(refs[:n], refs[n:2 * n], refs[2 * n:])
        comm.finish(refs[:n], refs[n:2 * n], refs[2 * n:])

    return pl.pallas_call(body, name=name, in_specs=[ANY] * n, out_specs=[ANY] * n, out_shape=comm.out_shapes,
                          scratch_shapes=comm.scratch)(*comm.inputs)


_DIMS = {
    "nn": (((1,), (0,)), ((), ())),
    "nt": (((1,), (1,)), ((), ())),
    "tn": (((0,), (0,)), ((), ())),
}

MATMUL_VMEM_BUDGET = 36 * 1024 * 1024
MAX_TILE = 1536


def _pick(n, prefs):
    for p in prefs:
        if n % p == 0:
            return p
    return n


def _tile_options(n):
    return [d for d in range(128, min(n, MAX_TILE) + 1, 128) if n % d == 0] or [n]


def _pick_tiles(M, N, tk, nk, sa, sb, so, has_addend, tm, tn):
    best = None
    for cm in ([tm] if tm else _tile_options(M)):
        for cn in ([tn] if tn else _tile_options(N)):
            need = 2 * (cm * tk * sa + tk * cn * sb + cm * cn * so + (cm * cn * 4 if has_addend else 0))
            need += cm * cn * 4 if nk > 1 else 0
            if need <= MATMUL_VMEM_BUDGET and (best is None or cm * cn > best[0] * best[1]
                                               or (cm * cn == best[0] * best[1] and cn > best[1])):
                best = (cm, cn)
    assert best is not None, (M, N, tk)
    return best


def _matmul(a, b, form, *, out_dtype=F32, addend=None, tm=None, tn=None, tk=None, comm=None, name):
    if form == "nn":
        (M, K), (K2, N) = a.shape, b.shape
    elif form == "nt":
        (M, K), (N, K2) = a.shape, b.shape
    else:
        (K, M), (K2, N) = a.shape, b.shape
    assert K == K2, (a.shape, b.shape, form)
    tk = tk or (K if K <= 2816 else _pick(K, (1024, 512, 256, 128)))
    nk = K // tk
    if tm is None or tn is None:
        tm, tn = _pick_tiles(M, N, tk, nk, a.dtype.itemsize, b.dtype.itemsize, jnp.dtype(out_dtype).itemsize,
                             addend is not None, tm, tn)
    assert M % tm == 0 and N % tn == 0 and K % tk == 0, (M, N, K, tm, tn, tk)
    dims = _DIMS[form]
    nc = comm.n if comm is not None else 0
    grid = (M // tm, N // tn, nk)

    def body(*refs):
        a_ref, b_ref = refs[:2]
        pos = 2
        add_ref = refs[pos] if addend is not None else None
        pos += addend is not None
        c_in, o_ref, c_out = refs[pos:pos + nc], refs[pos + nc], refs[pos + nc + 1:pos + 2 * nc + 1]
        pos += 2 * nc + 1
        acc_ref = refs[pos] if nk > 1 else None
        c_sems = refs[pos + (nk > 1):]
        if comm is not None:
            ids = [pl.program_id(d) for d in range(3)]

            @pl.when((ids[0] == 0) & (ids[1] == 0) & (ids[2] == 0))
            def _():
                comm.start(c_in, c_out, c_sems)

        def finish(r):
            if add_ref is not None:
                r = r + add_ref[...].astype(F32)
            o_ref[...] = r.astype(o_ref.dtype)

        part = lax.dot_general(a_ref[...].astype(BF16), b_ref[...].astype(BF16), dims, preferred_element_type=F32)
        if nk == 1:
            finish(part)
        else:
            k = pl.program_id(2)

            @pl.when(k == 0)
            def _():
                acc_ref[...] = part

            @pl.when(k > 0)
            def _():
                acc_ref[...] += part

            @pl.when(k == nk - 1)
            def _():
                finish(acc_ref[...])

        if comm is not None:
            @pl.when((ids[0] == grid[0] - 1) & (ids[1] == grid[1] - 1) & (ids[2] == grid[2] - 1))
            def _():
                comm.finish(c_in, c_out, c_sems)

    if form == "nn":
        a_spec = pl.BlockSpec((tm, tk), lambda i, j, k: (i, k))
        b_spec = pl.BlockSpec((tk, tn), lambda i, j, k: (k, j))
    elif form == "nt":
        a_spec = pl.BlockSpec((tm, tk), lambda i, j, k: (i, k))
        b_spec = pl.BlockSpec((tn, tk), lambda i, j, k: (j, k))
    else:
        a_spec = pl.BlockSpec((tk, tm), lambda i, j, k: (k, i))
        b_spec = pl.BlockSpec((tk, tn), lambda i, j, k: (k, j))
    o_spec = pl.BlockSpec((tm, tn), lambda i, j, k: (i, j))
    in_specs = [a_spec, b_spec] + ([o_spec] if addend is not None else [])
    args = (a, b) + ((addend,) if addend is not None else ())
    out_shape = jax.ShapeDtypeStruct((M, N), out_dtype)
    scratch = [pltpu.VMEM((tm, tn), F32)] if nk > 1 else []
    if comm is None:
        return pl.pallas_call(
            body, name=name, grid=grid, in_specs=in_specs, out_specs=o_spec, out_shape=out_shape,
            scratch_shapes=scratch, compiler_params=_cparams(("parallel", "parallel", "arbitrary")),
        )(*args)
    outs = pl.pallas_call(
        body, name=name, grid=grid, in_specs=in_specs + [ANY] * nc, out_specs=[o_spec] + [ANY] * nc,
        out_shape=[out_shape] + comm.out_shapes, scratch_shapes=scratch + comm.scratch,
        compiler_params=_cparams(("arbitrary", "arbitrary", "arbitrary")),
    )(*args, *comm.inputs)
    return outs[0], outs[1:]


def _rms_fwd(x, g, *, name, tm=512):
    M, D = x.shape
    tm = min(tm, M)

    def body(x_ref, g_ref, n_ref):
        xf = x_ref[...]
        r = lax.rsqrt(jnp.mean(xf * xf, axis=-1, keepdims=True) + EPS)
        n_ref[...] = (xf * r * g_ref[...]).astype(n_ref.dtype)

    return pl.pallas_call(
        body, name=name, grid=(M // tm,),
        in_specs=[pl.BlockSpec((tm, D), lambda i: (i, 0)), pl.BlockSpec((1, D), lambda i: (0, 0))],
        out_specs=pl.BlockSpec((tm, D), lambda i: (i, 0)),
        out_shape=jax.ShapeDtypeStruct((M, D), BF16),
        compiler_params=_cparams(("parallel",)),
    )(x, g.reshape(1, D))


def _rms_bwd(x, g, dn, dres, *, name, tm=512):
    M, D = x.shape
    tm = min(tm, M)

    def body(x_ref, g_ref, dn_ref, dres_ref, dx_ref, dg_ref):
        @pl.when(pl.program_id(0) == 0)
        def _():
            dg_ref[...] = jnp.zeros_like(dg_ref)

        xf = x_ref[...]
        r = lax.rsqrt(jnp.mean(xf * xf, axis=-1, keepdims=True) + EPS)
        xh = xf * r
        dn_ = dn_ref[...].astype(F32)
        dg_ref[...] += jnp.sum(dn_ * xh, axis=0, keepdims=True)
        dxh = dn_ * g_ref[...]
        dx = r * (dxh - xh * jnp.mean(dxh * xh, axis=-1, keepdims=True))
        dx_ref[...] = dres_ref[...] + dx

    row = pl.BlockSpec((tm, D), lambda i: (i, 0))
    vec = pl.BlockSpec((1, D), lambda i: (0, 0))
    return pl.pallas_call(
        body, name=name, grid=(M // tm,),
        in_specs=[row, vec, row, row], out_specs=[row, vec],
        out_shape=[jax.ShapeDtypeStruct((M, D), F32), jax.ShapeDtypeStruct((1, D), F32)],
        compiler_params=_cparams(("arbitrary",)),
    )(x, g.reshape(1, D), dn, dres)


def _loss_head(h, g, tgt, *, name, tm=512):
    M, D = h.shape
    tm = min(tm, M)

    def body(h_ref, g_ref, t_ref, loss_ref, dh_ref, dg_ref):
        @pl.when(pl.program_id(0) == 0)
        def _():
            dg_ref[...] = jnp.zeros_like(dg_ref)
            loss_ref[...] = jnp.zeros_like(loss_ref)

        xf = h_ref[...]
        r = lax.rsqrt(jnp.mean(xf * xf, axis=-1, keepdims=True) + EPS)
        xh = xf * r
        err = xh * g_ref[...] - t_ref[...]
        part = jnp.sum(jnp.mean(err * err, axis=-1, keepdims=True), axis=0, keepdims=True)
        loss_ref[...] += 0.5 * part
        dy = err * (1.0 / D)
        dg_ref[...] += jnp.sum(dy * xh, axis=0, keepdims=True)
        dxh = dy * g_ref[...]
        dh_ref[...] = r * (dxh - xh * jnp.mean(dxh * xh, axis=-1, keepdims=True))

    row = pl.BlockSpec((tm, D), lambda i: (i, 0))
    vec = pl.BlockSpec((1, D), lambda i: (0, 0))
    one = pl.BlockSpec((1, 1), lambda i: (0, 0))
    return pl.pallas_call(
        body, name=name, grid=(M // tm,),
        in_specs=[row, vec, row], out_specs=[one, row, vec],
        out_shape=[jax.ShapeDtypeStruct((1, 1), F32), jax.ShapeDtypeStruct((M, D), F32),
                   jax.ShapeDtypeStruct((1, D), F32)],
        compiler_params=_cparams(("arbitrary",)),
    )(h, g.reshape(1, D), tgt)


HG_MID = HG_CHUNK // 2 - 1
EXP_CAP = 80.0


def _sigmoid(x):
    return 1.0 / (1.0 + jnp.exp(-x))


def _dot(a, b, dims, precision=None):
    return lax.dot_general(a, b, dims, preferred_element_type=F32, precision=precision)


def _bdot(a, b, form):
    return _dot(a.astype(BF16), b.astype(BF16), _DIMS[form])


def _split2(x):
    hi = x.astype(BF16)
    return hi, (x - hi.astype(F32)).astype(BF16)


def _dot3(a, b, form):
    d = _DIMS[form]
    return _dot(a[0], b[0], d) + (_dot(a[0], b[1], d) + _dot(a[1], b[0], d))


def _hgrn_chunk_common(hq, hf, lbv, tril, rid):
    sq = _sigmoid(hq)
    q = hq * sq
    sg = _sigmoid(hf)
    f = lbv + (1.0 - lbv) * sg
    k = (1.0 - lbv) * (1.0 - sg)
    g = jnp.log(f)
    b = _dot(tril, g, _DIMS["nn"], precision=lax.Precision.HIGHEST)
    bref = jnp.sum(jnp.where(rid == HG_MID, b, 0.0), axis=0, keepdims=True)
    bend = jnp.sum(jnp.where(rid == HG_CHUNK - 1, b, 0.0), axis=0, keepdims=True)
    eb = jnp.exp(b)
    e1 = jnp.exp(jnp.minimum(b - bref, EXP_CAP))
    e2 = jnp.exp(jnp.minimum(bref - b, EXP_CAP))
    e3 = jnp.exp(bend - b)
    return sq, q, sg, f, k, bend, eb, e1, e2, e3


def _hgrn_fwd(proj, lb, gnorm, *, name, T=1024):
    S = proj.shape[0]
    T = min(T, S)
    nch = T // HG_CHUNK
    C = HG_CHUNK

    def body(hq_ref, hf_ref, hi_ref, hg_ref, lb_ref, gn_ref, o_ref, oa_ref, st_ref, state):
        @pl.when(pl.program_id(1) == 0)
        def _():
            state[...] = jnp.zeros_like(state)

        lbv = lb_ref[...]
        gn = gn_ref[...]
        row = lax.broadcasted_iota(jnp.int32, (C, C), 0)
        col = lax.broadcasted_iota(jnp.int32, (C, C), 1)
        causal = row >= col
        tril = causal.astype(F32)
        rid = lax.broadcasted_iota(jnp.int32, (C, HG_DK), 0)
        sls = [pl.ds(c * C, C) for c in range(nch)]
        pre = [_hgrn_chunk_common(hq_ref[sl, :], hf_ref[sl, :], lbv, tril, rid) for sl in sls]
        v_l = [hi_ref[sl, :].astype(BF16) for sl in sls]
        a_l, u_l = [], []
        for c in range(nch):
            _, q, _, _, k, _, _, e1, e2, e3 = pre[c]
            a_l.append(jnp.where(causal, _bdot(q * e1, k * e2, "nt"), 0.0))
            u_l.append(_bdot(v_l[c], k * e3, "tn"))
        o_l = [_bdot(a_l[c], v_l[c], "nn") for c in range(nch)]
        st = state[...]
        st_l = []
        for c in range(nch):
            st_l.append(st)
            st = st * jnp.exp(pre[c][5]) + u_l[c]
        state[...] = st
        for c in range(nch):
            st_ref[0, c] = st_l[c]
            o_l[c] = o_l[c] + _bdot(pre[c][1] * pre[c][6], st_l[c], "nt")
        for c in range(nch):
            o, hg = o_l[c], hg_ref[sls[c], :]
            o_ref[sls[c], :] = o
            r = lax.rsqrt(jnp.mean(o * o, axis=-1, keepdims=True) + EPS)
            oa_ref[sls[c], :] = (o * r * gn * (hg * _sigmoid(hg))).astype(oa_ref.dtype)

    def grp(gidx):
        return pl.BlockSpec((T, 128), lambda h, t: (t, gidx * 8 + h))

    return pl.pallas_call(
        body, name=name, grid=(HG_HEADS, S // T),
        in_specs=[grp(0), grp(1), grp(2), grp(3),
                  pl.BlockSpec((1, 128), lambda h, t: (0, h)), pl.BlockSpec((1, 128), lambda h, t: (0, 0))],
        out_specs=[pl.BlockSpec((T, 128), lambda h, t: (t, h)), pl.BlockSpec((T, 128), lambda h, t: (t, h)),
                   pl.BlockSpec((1, nch, HG_DV, HG_DK), lambda h, t: (h, t, 0, 0))],
        out_shape=[jax.ShapeDtypeStruct((S, HG_HEADS * HG_DV), F32), jax.ShapeDtypeStruct((S, HG_HEADS * HG_DV), BF16),
                   jax.ShapeDtypeStruct((HG_HEADS, S // C, HG_DV, HG_DK), F32)],
        scratch_shapes=[pltpu.VMEM((HG_DV, HG_DK), F32)],
        compiler_params=_cparams(("parallel", "arbitrary")),
    )(proj, proj, proj, proj, lb, gnorm)


def _hgrn_bwd(proj, lb, gnorm, o, states, doa, *, name, T=1024):
    S = proj.shape[0]
    T = min(T, S)
    nch = T // HG_CHUNK
    C = HG_CHUNK
    nT = S // T

    def body(hq_ref, hf_ref, hi_ref, hg_ref, lb_ref, gn_ref, o_ref, st_ref, doa_ref,
             dhq_ref, dhf_ref, dhi_ref, dhg_ref, dlb_ref, dgn_ref, dstate):
        @pl.when(pl.program_id(1) == 0)
        def _():
            dstate[...] = jnp.zeros_like(dstate)
            dlb_ref[...] = jnp.zeros_like(dlb_ref)
            dgn_ref[...] = jnp.zeros_like(dgn_ref)

        lbv = lb_ref[...]
        gn = gn_ref[...]
        row = lax.broadcasted_iota(jnp.int32, (C, C), 0)
        col = lax.broadcasted_iota(jnp.int32, (C, C), 1)
        causal = row >= col
        tril = causal.astype(F32)
        triu = (row <= col).astype(F32)
        rid = lax.broadcasted_iota(jnp.int32, (C, HG_DK), 0)
        rng = range(nch)
        sls = [pl.ds(c * C, C) for c in rng]
        pre = [_hgrn_chunk_common(hq_ref[sl, :], hf_ref[sl, :], lbv, tril, rid) for sl in sls]
        do2, dgn_acc = [], jnp.zeros((1, HG_DV), F32)
        for c in rng:
            hg, ov = hg_ref[sls[c], :], o_ref[sls[c], :]
            r = lax.rsqrt(jnp.mean(ov * ov, axis=-1, keepdims=True) + EPS)
            xh = ov * r
            sgg = _sigmoid(hg)
            d_oa = doa_ref[sls[c], :].astype(F32)
            dz = d_oa * (hg * sgg)
            dhg_ref[sls[c], :] = (d_oa * (xh * gn) * (sgg * (1.0 + hg * (1.0 - sgg)))).astype(dhg_ref.dtype)
            dgn_acc = dgn_acc + jnp.sum(dz * xh, axis=0, keepdims=True)
            dxh = dz * gn
            do2.append(_split2(r * (dxh - xh * jnp.mean(dxh * xh, axis=-1, keepdims=True))))
        dgn_ref[0] += dgn_acc
        qi = [pre[c][1] * pre[c][6] for c in rng]
        qp = [pre[c][1] * pre[c][7] for c in rng]
        kp = [pre[c][4] * pre[c][8] for c in rng]
        kend = [pre[c][4] * pre[c][9] for c in rng]
        qi2, qp2, kp2, kend2 = ([_split2(t) for t in lst] for lst in (qi, qp, kp, kend))
        v2 = [_split2(hi_ref[sl, :]) for sl in sls]
        st0 = [st_ref[0, c] for c in rng]
        a_l = [jnp.where(causal, _dot(qp2[c][0], kp2[c][0], _DIMS["nt"]), 0.0).astype(BF16) for c in rng]
        da2 = [_split2(jnp.where(causal, _dot3(do2[c], v2[c], "nt"), 0.0)) for c in rng]
        dqi = [_dot3(do2[c], _split2(st0[c]), "nn") for c in rng]
        w_l = [_dot3(do2[c], qi2[c], "tn") for c in rng]
        ds = dstate[...]
        ds1 = [None] * nch
        for c in reversed(rng):
            ds1[c] = ds
            ds = ds * jnp.exp(pre[c][5]) + w_l[c]
        dstate[...] = ds
        ds12 = [_split2(t) for t in ds1]
        dqp = [_dot3(da2[c], kp2[c], "nn") for c in rng]
        dkp = [_dot3(da2[c], qp2[c], "tn") for c in rng]
        dv = [_dot(a_l[c], do2[c][0], _DIMS["tn"]) + _dot(kend2[c][0], ds12[c][0], _DIMS["nt"]) for c in rng]
        dkend = [_dot3(v2[c], ds12[c], "nn") for c in rng]
        dq_l, dk_l, db_l = [], [], []
        for c in rng:
            _, _, _, _, _, bend, eb, e1, e2, e3 = pre[c]
            dq_l.append(dqi[c] * eb + dqp[c] * e1)
            dk_l.append(dkp[c] * e2 + dkend[c] * e3)
            db = dqi[c] * qi[c] + dqp[c] * qp[c] - dkp[c] * kp[c] - dkend[c] * kend[c]
            dbend = (jnp.sum(dkend[c] * kend[c], axis=0, keepdims=True)
                     + jnp.exp(bend) * jnp.sum(ds1[c] * st0[c], axis=0, keepdims=True))
            db_l.append(db + jnp.where(rid == C - 1, dbend, 0.0))
        dg = [_dot(triu, db_l[c], _DIMS["nn"], precision=lax.Precision.HIGHEST) for c in rng]
        dlb_acc = jnp.zeros((1, HG_DK), F32)
        for c in rng:
            sq, _, sg, f, _, _, _, _, _, _ = pre[c]
            hq = hq_ref[sls[c], :]
            df = dg[c] / f - dk_l[c]
            dlb_acc = dlb_acc + jnp.sum(df * (1.0 - sg), axis=0, keepdims=True)
            dhf_ref[sls[c], :] = (df * (1.0 - lbv) * sg * (1.0 - sg)).astype(dhf_ref.dtype)
            dhq_ref[sls[c], :] = (dq_l[c] * (sq * (1.0 + hq * (1.0 - sq)))).astype(dhq_ref.dtype)
            dhi_ref[sls[c], :] = dv[c].astype(dhi_ref.dtype)
        dlb_ref[...] += dlb_acc

    def grp(gidx):
        return pl.BlockSpec((T, 128), lambda h, t: (nT - 1 - t, gidx * 8 + h))

    tok = pl.BlockSpec((T, 128), lambda h, t: (nT - 1 - t, h))
    big = jax.ShapeDtypeStruct((S, HG_HEADS * HG_DV), BF16)
    return pl.pallas_call(
        body, name=name, grid=(HG_HEADS, nT),
        in_specs=[grp(0), grp(1), grp(2), grp(3),
                  pl.BlockSpec((1, 128), lambda h, t: (0, h)), pl.BlockSpec((1, 128), lambda h, t: (0, 0)),
                  tok, pl.BlockSpec((1, nch, HG_DV, HG_DK), lambda h, t: (h, nT - 1 - t, 0, 0)), tok],
        out_specs=[tok, tok, tok, tok, pl.BlockSpec((1, 128), lambda h, t: (0, h)),
                   pl.BlockSpec((1, 1, 128), lambda h, t: (h, 0, 0))],
        out_shape=[big, big, big, big, jax.ShapeDtypeStruct((1, HG_HEADS * HG_DK), F32),
                   jax.ShapeDtypeStruct((HG_HEADS, 1, HG_DV), F32)],
        scratch_shapes=[pltpu.VMEM((HG_DV, HG_DK), F32)],
        compiler_params=_cparams(("parallel", "arbitrary")),
    )(proj, proj, proj, proj, lb, gnorm, o, states, doa)


def _lb_fwd(logits, *, name):
    def body(l_ref, lb_ref):
        lb_ref[...] = _sigmoid(l_ref[0:1, :] - l_ref[1:2, :])

    return pl.pallas_call(body, name=name, out_shape=jax.ShapeDtypeStruct((1, logits.shape[1]), F32))(logits)


def _lb_bwd(logits, dlb, *, name):
    def body(l_ref, d_ref, o_ref):
        lbv = _sigmoid(l_ref[0:1, :] - l_ref[1:2, :])
        t = d_ref[...] * lbv * (1.0 - lbv)
        o_ref[0:1, :] = t
        o_ref[1:2, :] = -t

    return pl.pallas_call(body, name=name, out_shape=jax.ShapeDtypeStruct(logits.shape, F32))(logits, dlb)


NEG = -1e30
FOX_SCALE = FOX_DH ** -0.5
FOX_PAIRS = FOX_HEADS // 2


def _fox_gate_fwd(ff, bias, *, name, T=512):
    S = ff.shape[0]
    T = min(T, S)

    def body(ff_ref, b_ref, c_ref, carry):
        @pl.when(pl.program_id(0) == 0)
        def _():
            carry[...] = jnp.zeros_like(carry)

        z = ff_ref[...] + b_ref[...]
        logf = jnp.minimum(z, 0.0) - jnp.log(1.0 + jnp.exp(-jnp.abs(z)))
        row = lax.broadcasted_iota(jnp.int32, (T, T), 0)
        col = lax.broadcasted_iota(jnp.int32, (T, T), 1)
        c = _dot((row >= col).astype(F32), logf, _DIMS["nn"], precision=lax.Precision.HIGHEST) + carry[...]
        c_ref[...] = c
        carry[...] = c[T - 1:T, :]

    return pl.pallas_call(
        body, name=name, grid=(S // T,),
        in_specs=[pl.BlockSpec((T, 128), lambda i: (i, 0)), pl.BlockSpec((1, 128), lambda i: (0, 0))],
        out_specs=pl.BlockSpec((T, 128), lambda i: (i, 0)),
        out_shape=jax.ShapeDtypeStruct((S, 128), F32),
        scratch_shapes=[pltpu.VMEM((1, 128), F32)],
        compiler_params=_cparams(("arbitrary",)),
    )(ff, bias)


def _fox_gate_bwd(ff, bias, dcs, *, name, T=512):
    S = ff.shape[0]
    T = min(T, S)
    nT = S // T

    def body(ff_ref, b_ref, d_ref, dff_ref, db_ref, carry):
        @pl.when(pl.program_id(0) == 0)
        def _():
            carry[...] = jnp.zeros_like(carry)
            db_ref[...] = jnp.zeros_like(db_ref)

        row = lax.broadcasted_iota(jnp.int32, (T, T), 0)
        col = lax.broadcasted_iota(jnp.int32, (T, T), 1)
        dlogf = carry[...] - _dot((row <= col).astype(F32), d_ref[...], _DIMS["nn"], precision=lax.Precision.HIGHEST)
        carry[...] = dlogf[0:1, :]
        dff = dlogf * (1.0 - _sigmoid(ff_ref[...] + b_ref[...]))
        dff_ref[...] = dff.astype(dff_ref.dtype)
        db_ref[...] += jnp.sum(dff, axis=0, keepdims=True)

    rev = pl.BlockSpec((T, 128), lambda i: (nT - 1 - i, 0))
    vec = pl.BlockSpec((1, 128), lambda i: (0, 0))
    return pl.pallas_call(
        body, name=name, grid=(nT,),
        in_specs=[rev, vec, rev], out_specs=[rev, vec],
        out_shape=[jax.ShapeDtypeStruct((S, 128), BF16), jax.ShapeDtypeStruct((1, 128), F32)],
        scratch_shapes=[pltpu.VMEM((1, 128), F32)],
        compiler_params=_cparams(("arbitrary",)),
    )(ff, bias, dcs)


AUG = FOX_DH
RSUM_LANE = 6


def _bias_lane(hh):
    return AUG * (1 - hh)


def _data_lanes(lane, hh):
    return (lane < AUG) if hh == 0 else (lane >= AUG)


def _split3(x):
    a = x.astype(BF16).astype(F32)
    r = x - a
    b = r.astype(BF16).astype(F32)
    return a, b, r - b


def _lane_fill(lane, base, pieces, start):
    for i, pc in enumerate(pieces):
        base = jnp.where(lane == start + i, pc, base)
    return base


FOX_TB = 512
FOX_SKIP = 32.0
N_STAT = 4


def _fox_prep(proj, c_tok, *, name):
    S = proj.shape[0]
    T = min(FOX_TB, S)

    def body(q_ref, k_ref, v_ref, c_ref, qa_ref, ka_ref, va_ref, st_ref):
        pair = pl.program_id(0)
        lane = lax.broadcasted_iota(jnp.int32, (T, 128), 1)
        lane1 = lax.broadcasted_iota(jnp.int32, (1, 128), 1)
        c = c_ref[...]
        q, k, v = q_ref[...], k_ref[...], v_ref[...]
        for hh in range(2):
            data, b0 = _data_lanes(lane, hh), _bias_lane(hh)
            ones3 = jnp.where((lane >= b0) & (lane < b0 + 3), 1.0, 0.0)

            def max_norm(t):
                tr = jnp.where(data, t.astype(BF16).astype(F32), 0.0)
                return jnp.sqrt(jnp.max(jnp.sum(tr * tr, axis=-1, keepdims=True), axis=0, keepdims=True))

            ch = jnp.sum(jnp.where(lane == 2 * pair + hh, c, 0.0), axis=-1, keepdims=True)
            c1, c2, c3 = _split3(ch)
            aug_q = _lane_fill(lane, jnp.where((lane >= b0 + 3) & (lane < b0 + 6), 1.0, 0.0), (c1, c2, c3), b0)
            aug_k = _lane_fill(lane, ones3, (-c1, -c2, -c3), b0 + 3)
            qa_ref[hh] = jnp.where(data, q * FOX_SCALE, aug_q).astype(BF16)
            ka_ref[hh] = jnp.where(data, k, aug_k).astype(BF16)
            va_ref[hh] = jnp.where(data, v, ones3).astype(BF16)
            stats = (max_norm(q * FOX_SCALE), jnp.max(ch, axis=0, keepdims=True), max_norm(k),
                     jnp.min(ch, axis=0, keepdims=True))
            st_ref[hh, 0] = _lane_fill(lane1, jnp.zeros((1, 128), F32), stats, 0)

    def grp(g):
        return pl.BlockSpec((T, 128), lambda p, t: (t, g * 8 + p))

    hm = pl.BlockSpec((2, T, 128), lambda p, t: (p, t, 0))
    out = jax.ShapeDtypeStruct((FOX_HEADS, S, 128), BF16)
    return pl.pallas_call(
        body, name=name, grid=(FOX_PAIRS, S // T),
        in_specs=[grp(4), grp(5), grp(6), pl.BlockSpec((T, 128), lambda p, t: (t, 0))],
        out_specs=[hm, hm, hm, pl.BlockSpec((2, 1, 1, 128), lambda p, t: (p, t, 0, 0))],
        out_shape=[out, out, out, jax.ShapeDtypeStruct((FOX_HEADS, S // T, 1, 128), F32)],
        compiler_params=_cparams(("parallel", "parallel")),
    )(proj, proj, proj, c_tok)


def _fox_bound(st_ref, head, nb, qi, ki):
    qb_, kb_ = (head * nb + qi) * N_STAT, (head * nb + ki) * N_STAT
    return st_ref[qb_] * st_ref[kb_ + 2] + st_ref[qb_ + 1] - st_ref[kb_ + 3] + 0.01


def _pair_lanes(lane, a0, a1):
    return jnp.where(lane < AUG, a0, a1)


def _first_live_key(st_ref, head, nb, qi, newest, thr):
    def body(t, k0):
        k = newest - t
        return jnp.where(_fox_bound(st_ref, head, nb, qi, k) > thr, k, k0)

    return lax.fori_loop(0, newest + 1, body, newest + 1)


def _last_live_query(st_ref, lm_ref, head, nb, ki):
    def body(t, i1):
        i = ki + 1 + t
        live = _fox_bound(st_ref, head, nb, i, ki) > lm_ref[head * nb + i] - FOX_SKIP
        return jnp.where(live, i, i1)

    return lax.fori_loop(0, nb - 1 - ki, body, ki)


class _BlockStream:
    def __init__(self, hbm_refs, bufs, sems, pair, tb):
        self.hbm, self.bufs, self.sems, self.pair, self.tb = hbm_refs, bufs, sems, pair, tb

    def _copies(self, blk, slot):
        rows = pl.ds(pl.multiple_of(blk * self.tb, self.tb), self.tb)
        return [pltpu.make_async_copy(h.at[pl.ds(2 * self.pair, 2), rows, :], b.at[slot], self.sems.at[n, slot])
                for n, (h, b) in enumerate(zip(self.hbm, self.bufs))]

    def start(self, blk, slot):
        for cp in self._copies(blk, slot):
            cp.start()

    def wait(self, blk, slot):
        for cp in self._copies(blk, slot):
            cp.wait()


def _fox_fwd(qa, ka, va, bounds, *, name):
    S = qa.shape[1]
    tb = min(FOX_TB, S)
    nb = S // tb

    def body(qa_ref, ka_hbm, va_hbm, st_ref, o_ref, qb_ref, lse_ref, kbuf, vbuf, sems, m_s, acc_s, m_min):
        pair, qi = pl.program_id(0), pl.program_id(1)
        stream = _BlockStream((ka_hbm, va_hbm), (kbuf, vbuf), sems, pair, tb)

        def head_step(hh, slot, masked):
            s = _dot(qa_ref[hh], kbuf[slot, hh], _DIMS["nt"])
            if masked:
                row = lax.broadcasted_iota(jnp.int32, (tb, tb), 0)
                col = lax.broadcasted_iota(jnp.int32, (tb, tb), 1)
                s = jnp.where(col <= row, s, NEG)
            m_old = m_s[hh]
            m_new = jnp.maximum(m_old, jnp.broadcast_to(jnp.max(s, axis=-1, keepdims=True), (tb, 128)))
            p = jnp.exp(s - jnp.concatenate([m_new] * (tb // 128), axis=1))
            acc_s[hh] = jnp.exp(m_old - m_new) * acc_s[hh] + _dot(p.astype(BF16), vbuf[slot, hh], _DIMS["nn"])
            m_s[hh] = m_new
            m_min[hh] = jnp.min(m_new)

        @pl.when(qi == 0)
        def _():
            stream.start(qi, 0)

        @pl.when(qi > 0)
        def _():
            stream.start(qi - 1, 1)

        m_s[...] = jnp.full_like(m_s, NEG)
        acc_s[...] = jnp.zeros_like(acc_s)
        stream.wait(qi, 0)
        for hh in range(2):
            head_step(hh, 0, True)

        @pl.when(qi > 1)
        def _():
            stream.start(qi - 2, 0)

        @pl.when(qi > 0)
        def _():
            stream.wait(qi - 1, 1)
            for hh in range(2):
                head_step(hh, 1, False)

        k0 = [_first_live_key(st_ref, 2 * pair + hh, nb, qi, qi - 2, m_min[hh] - FOX_SKIP) for hh in range(2)]
        n = qi - 1 - jnp.minimum(k0[0], k0[1])

        @pl.when((qi > 1) & (n == 0))
        def _():
            stream.wait(qi - 2, 0)

        def loop(t, carry):
            k = qi - 2 - t
            slot = t % 2
            stream.wait(k, slot)

            @pl.when(t + 1 < n)
            def _():
                stream.start(k - 1, 1 - slot)

            live = [k >= k0[hh] for hh in range(2)]

            @pl.when(live[0] & live[1])
            def _():
                for hh in range(2):
                    head_step(hh, slot, False)

            for hh in range(2):
                @pl.when(live[hh] & jnp.logical_not(live[1 - hh]))
                def _():
                    head_step(hh, slot, False)
            return carry

        lax.fori_loop(0, n, loop, 0)

        @pl.when(qi + 1 < nb)
        def _():
            stream.start(qi + 1, 0)

        lane = lax.broadcasted_iota(jnp.int32, (tb, 128), 1)
        outs = []
        for hh in range(2):
            acc = acc_s[hh]
            b0 = _bias_lane(hh)
            l = jnp.broadcast_to(acc[:, b0:b0 + 1], (tb, 128))
            outs.append(acc / l)
            lse = m_s[hh] + jnp.log(l)
            lse_ref[hh, 0] = jnp.min(lse, axis=0, keepdims=True)
            qf = qa_ref[hh].astype(F32)
            c_t = jnp.sum(jnp.where((lane >= b0) & (lane < b0 + 3), qf, 0.0), axis=-1, keepdims=True)
            cb = jnp.broadcast_to(c_t, (tb, 128)) - lse
            qb_ref[hh] = _lane_fill(lane, qf, _split3(cb), b0).astype(BF16)
        o_ref[...] = _pair_lanes(lane, outs[0], outs[1])

    qs = pl.BlockSpec((2, tb, 128), lambda p, i: (p, i, 0))
    return pl.pallas_call(
        body, name=name, grid=(FOX_PAIRS, nb),
        in_specs=[qs, ANY, ANY, SMEM],
        out_specs=[pl.BlockSpec((tb, 128), lambda p, i: (i, p)), qs,
                   pl.BlockSpec((2, 1, 1, 128), lambda p, i: (p, i, 0, 0))],
        out_shape=[jax.ShapeDtypeStruct((S, FOX_HEADS * FOX_DH), F32), jax.ShapeDtypeStruct((FOX_HEADS, S, 128), BF16),
                   jax.ShapeDtypeStruct((FOX_HEADS, nb, 1, 128), F32)],
        scratch_shapes=[pltpu.VMEM((2, 2, tb, 128), BF16), pltpu.VMEM((2, 2, tb, 128), BF16),
                        pltpu.SemaphoreType.DMA((2, 2)), pltpu.VMEM((2, tb, 128), F32), pltpu.VMEM((2, tb, 128), F32),
                        pltpu.SMEM((2,), F32)],
        compiler_params=_cparams(("arbitrary", "arbitrary")),
    )(qa, ka, va, bounds)


def _fox_bwd_prep(o, do, *, name, T=512):
    S = o.shape[0]
    T = min(T, S)

    def body(o_ref, do_ref, dob_ref):
        lane = lax.broadcasted_iota(jnp.int32, (T, 128), 1)
        d = do_ref[...].astype(F32)
        prod = d * o_ref[...]
        for hh in range(2):
            mine = _data_lanes(lane, hh)
            delta = jnp.sum(jnp.where(mine, prod, 0.0), axis=-1, keepdims=True)
            dob_ref[hh] = _lane_fill(lane, jnp.where(mine, d, 0.0), _split3(-delta), _bias_lane(hh)).astype(BF16)

    tok = pl.BlockSpec((T, 128), lambda p, t: (t, p))
    return pl.pallas_call(
        body, name=name, grid=(FOX_PAIRS, S // T),
        in_specs=[tok, tok], out_specs=pl.BlockSpec((2, T, 128), lambda p, t: (p, t, 0)),
        out_shape=jax.ShapeDtypeStruct((FOX_HEADS, S, 128), BF16),
        compiler_params=_cparams(("parallel", "parallel")),
    )(o, do)


def _fox_bwd_dq(qb, ka, va, dob, bounds, lse_min, *, name, comm=None):
    S = qb.shape[1]
    tb = min(FOX_TB, S)
    nb = S // tb
    nc = comm.n if comm is not None else 0

    def body(qb_ref, dob_ref, ka_hbm, va_hbm, st_ref, lm_ref, *rest):
        c_in, (dq_ref, dob2_ref), c_out = rest[:nc], rest[nc:nc + 2], rest[nc + 2:2 * nc + 2]
        kbuf, vbuf, sems, acc_s = rest[2 * nc + 2:2 * nc + 6]
        c_sems = rest[2 * nc + 6:]
        pair, qi = pl.program_id(0), pl.program_id(1)
        if comm is not None:
            @pl.when((pair == 0) & (qi == 0))
            def _():
                comm.start(c_in, c_out, c_sems)

        stream = _BlockStream((ka_hbm, va_hbm), (kbuf, vbuf), sems, pair, tb)
        k0 = [_first_live_key(st_ref, 2 * pair + hh, nb, qi, qi - 1, lm_ref[(2 * pair + hh) * nb + qi] - FOX_SKIP)
              for hh in range(2)]
        n = qi - jnp.minimum(k0[0], k0[1]) + 1

        @pl.when(qi == 0)
        def _():
            stream.start(qi, 0)

        acc_s[...] = jnp.zeros_like(acc_s)

        def head_step(hh, slot, k, masked):
            s = _dot(qb_ref[hh], kbuf[slot, hh], _DIMS["nt"])
            if masked:
                row = lax.broadcasted_iota(jnp.int32, (tb, tb), 0)
                col = lax.broadcasted_iota(jnp.int32, (tb, tb), 1)
                s = jnp.where(col <= row, s, NEG)
            ds = jnp.exp(s) * _dot(dob_ref[hh], vbuf[slot, hh], _DIMS["nt"])
            acc_s[hh] += _dot(ds.astype(BF16), kbuf[slot, hh], _DIMS["nn"])

        def loop(t, carry):
            k = qi - t
            slot = t % 2
            stream.wait(k, slot)

            @pl.when(t + 1 < n)
            def _():
                stream.start(k - 1, 1 - slot)

            @pl.when(t == 0)
            def _():
                for hh in range(2):
                    head_step(hh, slot, k, True)

            live = [(t > 0) & (k >= k0[hh]) for hh in range(2)]

            @pl.when(live[0] & live[1])
            def _():
                for hh in range(2):
                    head_step(hh, slot, k, False)

            for hh in range(2):
                @pl.when(live[hh] & jnp.logical_not(live[1 - hh]))
                def _():
                    head_step(hh, slot, k, False)
            return carry

        lax.fori_loop(0, n, loop, 0)

        @pl.when(qi + 1 < nb)
        def _():
            stream.start(qi + 1, 0)

        lane = lax.broadcasted_iota(jnp.int32, (tb, 128), 1)
        dq_ref[...] = (_pair_lanes(lane, acc_s[0], acc_s[1]) * FOX_SCALE).astype(dq_ref.dtype)
        for hh in range(2):
            b0 = _bias_lane(hh)
            r = jnp.broadcast_to(acc_s[hh][:, b0:b0 + 1], (tb, 128))
            dob2_ref[hh] = _lane_fill(lane, dob_ref[hh].astype(F32), _split3(r), b0 + RSUM_LANE).astype(BF16)
        if comm is not None:
            @pl.when((pair == FOX_PAIRS - 1) & (qi == nb - 1))
            def _():
                comm.finish(c_in, c_out, c_sems)

    qs = pl.BlockSpec((2, tb, 128), lambda p, i: (p, i, 0))
    outs = pl.pallas_call(
        body, name=name, grid=(FOX_PAIRS, nb),
        in_specs=[qs, qs, ANY, ANY, SMEM, SMEM] + [ANY] * nc,
        out_specs=[pl.BlockSpec((tb, 128), lambda p, i: (i, p)), qs] + [ANY] * nc,
        out_shape=[jax.ShapeDtypeStruct((S, FOX_HEADS * FOX_DH), BF16),
                   jax.ShapeDtypeStruct((FOX_HEADS, S, 128), BF16)] + (comm.out_shapes if comm is not None else []),
        scratch_shapes=[pltpu.VMEM((2, 2, tb, 128), BF16), pltpu.VMEM((2, 2, tb, 128), BF16),
                        pltpu.SemaphoreType.DMA((2, 2)), pltpu.VMEM((2, tb, 128), F32)]
        + (comm.scratch if comm is not None else []),
        compiler_params=_cparams(("arbitrary", "arbitrary")),
    )(qb, dob, ka, va, bounds, lse_min, *(comm.inputs if comm is not None else []))
    return (outs[0], outs[1]) if comm is None else (outs[0], outs[1], outs[2:])


def _fox_bwd_dkv(qb, ka, va, dob, bounds, lse_min, *, name):
    S = qb.shape[1]
    tb = min(FOX_TB, S)
    nb = S // tb

    def body(ka_ref, va_ref, qb_hbm, dob_hbm, st_ref, lm_ref, dk_ref, dv_ref, dcs_ref, qbuf, dbuf, sems, dk_s, dv_s):
        pair, ki = pl.program_id(0), pl.program_id(1)
        stream = _BlockStream((qb_hbm, dob_hbm), (qbuf, dbuf), sems, pair, tb)
        i1 = [_last_live_query(st_ref, lm_ref, 2 * pair + hh, nb, ki) for hh in range(2)]
        n = jnp.maximum(i1[0], i1[1]) - ki + 1

        @pl.when(ki == 0)
        def _():
            stream.start(ki, 0)

        dk_s[...] = jnp.zeros_like(dk_s)
        dv_s[...] = jnp.zeros_like(dv_s)

        def head_step(hh, slot, masked):
            st = _dot(ka_ref[hh], qbuf[slot, hh], _DIMS["nt"])
            if masked:
                row = lax.broadcasted_iota(jnp.int32, (tb, tb), 0)
                col = lax.broadcasted_iota(jnp.int32, (tb, tb), 1)
                st = jnp.where(row <= col, st, NEG)
            pt = jnp.exp(st)
            dst = pt * _dot(va_ref[hh], dbuf[slot, hh], _DIMS["nt"])
            dv_s[hh] += _dot(pt.astype(BF16), dbuf[slot, hh], _DIMS["nn"])
            dk_s[hh] += _dot(dst.astype(BF16), qbuf[slot, hh], _DIMS["nn"])

        def loop(t, carry):
            i = ki + t
            slot = t % 2
            stream.wait(i, slot)

            @pl.when(t + 1 < n)
            def _():
                stream.start(i + 1, 1 - slot)

            @pl.when(t == 0)
            def _():
                for hh in range(2):
                    head_step(hh, slot, True)

            live = [(t > 0) & (i <= i1[hh]) for hh in range(2)]

            @pl.when(live[0] & live[1])
            def _():
                for hh in range(2):
                    head_step(hh, slot, False)

            for hh in range(2):
                @pl.when(live[hh] & jnp.logical_not(live[1 - hh]))
                def _():
                    head_step(hh, slot, False)
            return carry

        lax.fori_loop(0, n, loop, 0)

        @pl.when(ki + 1 < nb)
        def _():
            stream.start(ki + 1, 0)

        lane = lax.broadcasted_iota(jnp.int32, (tb, 128), 1)
        dk_ref[...] = _pair_lanes(lane, dk_s[0], dk_s[1]).astype(dk_ref.dtype)
        dv_ref[...] = _pair_lanes(lane, dv_s[0], dv_s[1]).astype(dv_ref.dtype)
        for hh in range(2):
            b0 = _bias_lane(hh)
            dk_a, dv_a = dk_s[hh], dv_s[hh]
            off = dv_a[:, b0 + RSUM_LANE:b0 + RSUM_LANE + 1] + dv_a[:, b0 + RSUM_LANE + 1:b0 + RSUM_LANE + 2] \
                + dv_a[:, b0 + RSUM_LANE + 2:b0 + RSUM_LANE + 3]
            dcs_ref[0, :, hh:hh + 1] = dk_a[:, b0 + 3:b0 + 4] - off

    ks = pl.BlockSpec((2, tb, 128), lambda p, j: (p, j, 0))
    tok = pl.BlockSpec((tb, 128), lambda p, j: (j, p))
    big = jax.ShapeDtypeStruct((S, FOX_HEADS * FOX_DH), BF16)
    return pl.pallas_call(
        body, name=name, grid=(FOX_PAIRS, nb),
        in_specs=[ks, ks, ANY, ANY, SMEM, SMEM],
        out_specs=[tok, tok, pl.BlockSpec((1, tb, 2), lambda p, j: (p, j, 0))],
        out_shape=[big, big, jax.ShapeDtypeStruct((FOX_PAIRS, S, 2), F32)],
        scratch_shapes=[pltpu.VMEM((2, 2, tb, 128), BF16), pltpu.VMEM((2, 2, tb, 128), BF16),
                        pltpu.SemaphoreType.DMA((2, 2)), pltpu.VMEM((2, tb, 128), F32), pltpu.VMEM((2, tb, 128), F32)],
        compiler_params=_cparams(("arbitrary", "arbitrary")),
    )(ka, va, qb, dob, bounds, lse_min)


def _merge_fwd(proj, pa, pb, *, name, T=512):
    S, D = pa.shape
    T = min(T, S)

    def body(ga_ref, gb_ref, pa_ref, pb_ref, m_ref):
        m_ref[...] = (_sigmoid(ga_ref[...]) * pa_ref[...] + _sigmoid(gb_ref[...]) * pb_ref[...]).astype(m_ref.dtype)

    tok = pl.BlockSpec((T, D), lambda i: (i, 0))
    return pl.pallas_call(
        body, name=name, grid=(S // T,),
        in_specs=[pl.BlockSpec((T, D), lambda i: (i, 7)), pl.BlockSpec((T, D), lambda i: (i, 8)), tok, tok],
        out_specs=tok, out_shape=jax.ShapeDtypeStruct((S, D), BF16),
        compiler_params=_cparams(("parallel",)),
    )(proj, proj, pa, pb)


def _merge_bwd(proj, pa, pb, dm, *, name, T=512):
    S, D = pa.shape
    T = min(T, S)

    def body(ga_ref, gb_ref, pa_ref, pb_ref, dm_ref, dpa_ref, dpb_ref, dga_ref, dgb_ref):
        dm_ = dm_ref[...]
        sa, sb = _sigmoid(ga_ref[...]), _sigmoid(gb_ref[...])
        dpa_ref[...] = (dm_ * sa).astype(BF16)
        dpb_ref[...] = (dm_ * sb).astype(BF16)
        dga_ref[...] = (dm_ * pa_ref[...] * sa * (1.0 - sa)).astype(BF16)
        dgb_ref[...] = (dm_ * pb_ref[...] * sb * (1.0 - sb)).astype(BF16)

    tok = pl.BlockSpec((T, D), lambda i: (i, 0))
    big = jax.ShapeDtypeStruct((S, D), BF16)
    return pl.pallas_call(
        body, name=name, grid=(S // T,),
        in_specs=[pl.BlockSpec((T, D), lambda i: (i, 7)), pl.BlockSpec((T, D), lambda i: (i, 8)), tok, tok, tok],
        out_specs=[tok, tok, tok, tok], out_shape=[big, big, big, big],
        compiler_params=_cparams(("parallel",)),
    )(proj, proj, pa, pb, dm)


INV_SQRT2 = 0.7071067811865476
INV_SQRT2PI = 0.3989422804014327


def _shifted(u, prev, rid):
    m1 = jnp.where(rid == 0, prev[7:8, :], pltpu.roll(u, 1, 0))
    m2 = jnp.where(rid == 0, prev[6:7, :], jnp.where(rid == 1, prev[7:8, :], pltpu.roll(u, 2, 0)))
    return m1, m2


def _conv_acc(u, prev, w_ref, b_ref, rid):
    m1, m2 = _shifted(u, prev, rid)
    return b_ref[...] + w_ref[0:1, :] * m2 + w_ref[1:2, :] * m1 + w_ref[2:3, :] * u, m1, m2


def _convglu_fwd(ug, uv, wg, wv, bg, bv, *, name, T=512, tc=256):
    S, F = ug.shape
    T = min(T, S)

    def body(ug_ref, uv_ref, wg_ref, wv_ref, bg_ref, bv_ref, a_ref, pg, pv):
        @pl.when(pl.program_id(1) == 0)
        def _():
            pg[...] = jnp.zeros_like(pg)
            pv[...] = jnp.zeros_like(pv)

        rid = lax.broadcasted_iota(jnp.int32, (T, tc), 0)
        g_, v_ = ug_ref[...], uv_ref[...]
        accg, _, _ = _conv_acc(g_, pg[...], wg_ref, bg_ref, rid)
        accv, _, _ = _conv_acc(v_, pv[...], wv_ref, bv_ref, rid)
        gel = 0.5 * accg * (1.0 + lax.erf(accg * INV_SQRT2))
        a_ref[...] = (gel * accv).astype(a_ref.dtype)
        pg[...] = g_[T - 8:T, :]
        pv[...] = v_[T - 8:T, :]

    tok = pl.BlockSpec((T, tc), lambda j, t: (t, j))
    w3 = pl.BlockSpec((3, tc), lambda j, t: (0, j))
    b1 = pl.BlockSpec((1, tc), lambda j, t: (0, j))
    return pl.pallas_call(
        body, name=name, grid=(F // tc, S // T),
        in_specs=[tok, tok, w3, w3, b1, b1], out_specs=tok,
        out_shape=jax.ShapeDtypeStruct((S, F), BF16),
        scratch_shapes=[pltpu.VMEM((8, tc), F32), pltpu.VMEM((8, tc), F32)],
        compiler_params=_cparams(("parallel", "arbitrary")),
    )(ug, uv, wg, wv, bg, bv)


def _convglu_bwd(ug, uv, wg, wv, bg, bv, da, *, name, T=512, tc=256):
    S, F = ug.shape
    T = min(T, S)
    nT = S // T
    halo_blocks = T // 8

    def up_shift(d, nx, rid):
        p1 = jnp.where(rid == T - 1, nx[0:1, :], pltpu.roll(d, T - 1, 0))
        p2 = jnp.where(rid == T - 1, nx[1:2, :], jnp.where(rid == T - 2, nx[0:1, :], pltpu.roll(d, T - 2, 0)))
        return p1, p2

    def body(ug_ref, uv_ref, hg_ref, hv_ref, wg_ref, wv_ref, bg_ref, bv_ref, da_ref,
             dug_ref, duv_ref, dwg_ref, dwv_ref, dbg_ref, dbv_ref, ng, nv):
        @pl.when(pl.program_id(1) == 0)
        def _():
            ng[...] = jnp.zeros_like(ng)
            nv[...] = jnp.zeros_like(nv)
            for r in (dwg_ref, dwv_ref, dbg_ref, dbv_ref):
                r[...] = jnp.zeros_like(r)

        first_block = pl.program_id(1) == nT - 1
        rid = lax.broadcasted_iota(jnp.int32, (T, tc), 0)
        g_, v_ = ug_ref[...], uv_ref[...]
        pg = jnp.where(first_block, 0.0, hg_ref[...])
        pv = jnp.where(first_block, 0.0, hv_ref[...])
        accg, g1, g2 = _conv_acc(g_, pg, wg_ref, bg_ref, rid)
        accv, v1, v2 = _conv_acc(v_, pv, wv_ref, bv_ref, rid)
        cdf = 0.5 * (1.0 + lax.erf(accg * INV_SQRT2))
        pdf = INV_SQRT2PI * jnp.exp(-0.5 * accg * accg)
        da_ = da_ref[...].astype(F32)
        dgate = da_ * accv * (cdf + accg * pdf)
        dval = da_ * (accg * cdf)
        dbg_ref[...] += jnp.sum(dgate, axis=0, keepdims=True)
        dbv_ref[...] += jnp.sum(dval, axis=0, keepdims=True)
        for j, (sg_, sv_) in enumerate(((g2, v2), (g1, v1), (g_, v_))):
            dwg_ref[j:j + 1, :] += jnp.sum(dgate * sg_, axis=0, keepdims=True)
            dwv_ref[j:j + 1, :] += jnp.sum(dval * sv_, axis=0, keepdims=True)
        for d, w_ref, nx, out_ref in ((dgate, wg_ref, ng, dug_ref), (dval, wv_ref, nv, duv_ref)):
            p1, p2 = up_shift(d, nx[...], rid)
            out_ref[...] = (w_ref[2:3, :] * d + w_ref[1:2, :] * p1 + w_ref[0:1, :] * p2).astype(out_ref.dtype)
            nx[...] = d[0:8, :]

    tok = pl.BlockSpec((T, tc), lambda j, t: (nT - 1 - t, j))
    halo = pl.BlockSpec((8, tc), lambda j, t: (jnp.maximum((nT - 1 - t) * halo_blocks - 1, 0), j))
    w3 = pl.BlockSpec((3, tc), lambda j, t: (0, j))
    b1 = pl.BlockSpec((1, tc), lambda j, t: (0, j))
    big = jax.ShapeDtypeStruct((S, F), BF16)
    return pl.pallas_call(
        body, name=name, grid=(F // tc, nT),
        in_specs=[tok, tok, halo, halo, w3, w3, b1, b1, tok], out_specs=[tok, tok, w3, w3, b1, b1],
        out_shape=[big, big, jax.ShapeDtypeStruct((3, F), F32), jax.ShapeDtypeStruct((3, F), F32),
                   jax.ShapeDtypeStruct((1, F), F32), jax.ShapeDtypeStruct((1, F), F32)],
        scratch_shapes=[pltpu.VMEM((8, tc), F32), pltpu.VMEM((8, tc), F32)],
        compiler_params=_cparams(("parallel", "arbitrary")),
    )(ug, uv, ug, uv, wg, wv, bg, bv, da)


FF_LO = 7168
IN_SHARD = 1154
FF_DEV, FF_OFF = FF_LO // IN_SHARD, FF_LO % IN_SHARD


def _col_blocks(a, width):
    return jnp.stack([a[:, d * width:(d + 1) * width] for d in range(N_DEV)])


def _w_in_blocks(d_wm, d_wff):
    def block(d):
        lo = d * IN_SHARD
        if d < FF_DEV:
            return d_wm[:, lo:lo + IN_SHARD]
        if d > FF_DEV:
            return d_wm[:, lo - FOX_HEADS:lo - FOX_HEADS + IN_SHARD]
        return jnp.concatenate([d_wm[:, lo:FF_LO], d_wff[:, :FOX_HEADS], d_wm[:, FF_LO:lo + IN_SHARD - FOX_HEADS]], axis=1)

    return jnp.stack([block(d) for d in range(N_DEV)])


def _late_weights(g_a, g_b, g_o, g_up, g_cw, g_d):
    wup = jnp.concatenate([g_up[d] for d in range(N_DEV)], axis=1)
    cw = jnp.concatenate([g_cw[d] for d in range(N_DEV)], axis=1)
    return dict(wa=g_a.reshape(D_MODEL, D_MODEL), wb=g_b.reshape(D_MODEL, D_MODEL), wo=g_o.reshape(D_MODEL, D_MODEL),
                wug=wup[:, :D_FF], wuv=wup[:, D_FF:], cwg=cw[:, :D_FF], cwv=cw[:, D_FF:], wd=g_d.reshape(D_FF, D_MODEL))


def _early_grad_blocks(d_wa, d_wb, d_wo, d_wug, d_wuv, d_wd):
    up = jnp.stack([d_wug[:, d * 704:(d + 1) * 704] for d in range(4)]
                   + [d_wuv[:, d * 704:(d + 1) * 704] for d in range(4)])
    return [d_wa.reshape(N_DEV, 128, D_MODEL), d_wb.reshape(N_DEV, 128, D_MODEL), d_wo.reshape(N_DEV, 128, D_MODEL),
            up, d_wd.reshape(N_DEV, 352, D_MODEL)]


def _local_step(x, tgt, w, p, late=None, exchange=False):
    S = x.shape[0]
    mm = _matmul
    n1 = _rms_fwd(x, p["norm_mix"], name="rms1_fwd")
    if late is None:
        proj = mm(n1, w["wm"], "nn", name="proj_main")
    else:
        proj, gathered = mm(n1, w["wm"], "nn", comm=late, name="proj_main")
        w = {**w, **_late_weights(*gathered)}
    ff = mm(n1, w["wff"], "nn", name="proj_ff")
    lb = _lb_fwd(p["hg_lb_logits"], name="lb_fwd")
    gnorm = p["hg_norm"].reshape(1, HG_DV)
    o_hg, oa, states = _hgrn_fwd(proj, lb, gnorm, name="hgrn_fwd")
    bias = jnp.pad(p["fox_f_bias"].reshape(1, FOX_HEADS), ((0, 0), (0, 128 - FOX_HEADS)))
    c = _fox_gate_fwd(ff, bias, name="fox_gate_fwd")
    qa, ka, va, fox_stats = _fox_prep(proj, c, name="fox_prep")
    bounds = fox_stats[:, :, 0, :N_STAT].reshape(-1)
    ob, qb, lse_stats = _fox_fwd(qa, ka, va, bounds, name="fox_fwd")
    lse_min = lse_stats[:, :, 0, 0].reshape(-1)
    pa = mm(oa, w["wa"], "nn", name="branch_a")
    pb = mm(ob, w["wb"], "nn", name="branch_b")
    merged = _merge_fwd(proj, pa, pb, name="merge_fwd")
    h1 = mm(merged, w["wo"], "nn", addend=x, name="mix_out")
    n2 = _rms_fwd(h1, p["norm_ffn"], name="rms2_fwd")
    ug = mm(n2, w["wug"], "nn", name="up_gate")
    uv = mm(n2, w["wuv"], "nn", name="up_val")
    a = _convglu_fwd(ug, uv, w["cwg"], w["cwv"], p["cbg"], p["cbv"], name="convglu_fwd")
    h2 = mm(a, w["wd"], "nn", addend=h1, name="ffn_down")
    loss, dh2, d_norm_final = _loss_head(h2, p["norm_final"], tgt, name="loss_head")
    da = mm(dh2, w["wd"], "nt", out_dtype=BF16, name="d_act")
    d_wd = mm(a, dh2, "tn", out_dtype=BF16, name="dw_down")
    dug, duv, d_cwg, d_cwv, d_cbg, d_cbv = _convglu_bwd(
        ug, uv, w["cwg"], w["cwv"], p["cbg"], p["cbv"], da, name="convglu_bwd")
    dn2 = mm(dug, w["wug"], "nt", name="dn2_gate")
    dn2 = mm(duv, w["wuv"], "nt", addend=dn2, name="dn2_val")
    d_wug = mm(n2, dug, "tn", out_dtype=BF16, name="dw_up_gate")
    d_wuv = mm(n2, duv, "tn", out_dtype=BF16, name="dw_up_val")
    dh1, d_norm_ffn = _rms_bwd(h1, p["norm_ffn"], dn2, dh2, name="rms2_bwd")
    dmerged = mm(dh1, w["wo"], "nt", name="d_merged")
    d_wo = mm(merged, dh1, "tn", out_dtype=BF16, name="dw_out")
    dpa, dpb, dga, dgb = _merge_bwd(proj, pa, pb, dmerged, name="merge_bwd")
    doa = mm(dpa, w["wa"], "nt", name="d_oa")
    dob = mm(dpb, w["wb"], "nt", out_dtype=BF16, name="d_ob")
    d_wa = mm(oa, dpa, "tn", out_dtype=BF16, name="dw_branch_a")
    d_wb = mm(ob, dpb, "tn", out_dtype=BF16, name="dw_branch_b")
    dhq, dhf, dhi, dhg, dlb, dgn8 = _hgrn_bwd(proj, lb, gnorm, o_hg, states, doa, name="hgrn_bwd")
    d_logits = _lb_bwd(p["hg_lb_logits"], dlb, name="lb_bwd")
    dob_hm = _fox_bwd_prep(ob, dob, name="fox_bwd_prep")
    early_parts = None
    if exchange:
        comm = _ExchangeComm(_early_grad_blocks(d_wa, d_wb, d_wo, d_wug, d_wuv, d_wd))
        dq, dob2, early_parts = _fox_bwd_dq(qb, ka, va, dob_hm, bounds, lse_min, comm=comm, name="fox_bwd_dq")
    else:
        dq, dob2 = _fox_bwd_dq(qb, ka, va, dob_hm, bounds, lse_min, name="fox_bwd_dq")
    dk, dv, dcs = _fox_bwd_dkv(qb, ka, va, dob2, bounds, lse_min, name="fox_bwd_dkv")
    dcs_tok = jnp.pad(dcs.transpose(1, 0, 2).reshape(S, FOX_HEADS), ((0, 0), (0, 128 - FOX_HEADS)))
    dff, dbias = _fox_gate_bwd(ff, bias, dcs_tok, name="fox_gate_bwd")
    dproj = jnp.concatenate([dhq, dhf, dhi, dhg, dq, dk, dv, dga, dgb], axis=1)
    d_wm = mm(n1, dproj, "tn", out_dtype=BF16, name="dw_in_main")
    d_wff = mm(n1, dff, "tn", out_dtype=BF16, name="dw_in_ff")
    dn1 = mm(dff, w["wff"], "nt", name="dn1_ff")
    late_parts = None
    if exchange:
        d_cw = jnp.concatenate([d_cwg, d_cwv], axis=1)
        comm = _ExchangeComm([_w_in_blocks(d_wm, d_wff), _col_blocks(d_cw, 704)])
        dn1, late_parts = mm(dproj, w["wm"], "nt", addend=dn1, comm=comm, name="dn1_main")
    else:
        dn1 = mm(dproj, w["wm"], "nt", addend=dn1, name="dn1_main")
    dx, d_norm_mix = _rms_bwd(x, p["norm_mix"], dn1, dh1, name="rms1_bwd")
    grads = dict(
        wm=d_wm, wff=d_wff, wa=d_wa, wb=d_wb, wo=d_wo, wug=d_wug, wuv=d_wuv, cwg=d_cwg, cwv=d_cwv, wd=d_wd,
        norm_mix=d_norm_mix.reshape(-1), fox_f_bias=dbias[0, :FOX_HEADS], hg_lb_logits=d_logits,
        hg_norm=jnp.sum(dgn8, axis=0).reshape(-1), norm_ffn=d_norm_ffn.reshape(-1), cbg=d_cbg, cbv=d_cbv,
        norm_final=d_norm_final.reshape(-1), early_parts=early_parts, late_parts=late_parts)
    return loss, dx, grads


SMALL = [("norm_mix", (1, D_MODEL)), ("fox_f_bias", (1, FOX_HEADS)), ("hg_lb_logits", (2, HG_HEADS * HG_DK)),
         ("hg_norm", (1, HG_DV)), ("norm_ffn", (1, D_MODEL)), ("conv_b", (1, 2 * D_FF)), ("norm_final", (D_MODEL,))]
SMALL_ROWS = 88
SHARDED = [("w_in", (D_MODEL, 1154), 256), ("w_branch_a", (128, D_MODEL), 128), ("w_branch_b", (128, D_MODEL), 128),
           ("w_out", (128, D_MODEL), 128), ("w_up", (D_MODEL, 704), 256), ("conv_w", (3, 704), 3),
           ("w_down", (352, D_MODEL), 352)]
NAMES = ["norm_mix", "w_in", "fox_f_bias", "hg_lb_logits", "hg_norm", "w_branch_a", "w_branch_b", "w_out",
         "norm_ffn", "w_up", "conv_w", "conv_b", "w_down", "norm_final"]


def _size(shape):
    n = 1
    for s in shape:
        n *= s
    return n


def _adamw(parts, w, m, v, *, name, T):
    R, C = w.shape
    c1 = 1.0 / (1.0 - ADAM_B1 ** ADAM_STEP)
    c2 = 1.0 / (1.0 - ADAM_B2 ** ADAM_STEP)

    def body(p_ref, w_ref, m_ref, v_ref, g_ref, d_ref, nm_ref, nv_ref):
        g = p_ref[0].astype(F32)
        for s in range(1, N_DEV):
            g = g + p_ref[s].astype(F32)
        g_ref[...] = g
        nm = ADAM_B1 * m_ref[...] + (1.0 - ADAM_B1) * g
        nv = ADAM_B2 * v_ref[...] + (1.0 - ADAM_B2) * (g * g)
        nm_ref[...] = nm
        nv_ref[...] = nv
        d_ref[...] = -ADAM_LR * ((nm * c1) / (jnp.sqrt(nv * c2) + ADAM_EPS) + ADAM_WD * w_ref[...])

    blk = pl.BlockSpec((T, C), lambda i: (i, 0))
    out = jax.ShapeDtypeStruct((R, C), F32)
    return pl.pallas_call(
        body, name=name, grid=(R // T,),
        in_specs=[pl.BlockSpec((N_DEV, T, C), lambda i: (0, i, 0)), blk, blk, blk],
        out_specs=[blk, blk, blk, blk], out_shape=[out, out, out, out],
        compiler_params=_cparams(("parallel",)),
    )(parts, w, m, v)


def _pack_small(vals):
    flat = jnp.concatenate([vals[n].reshape(-1).astype(F32) for n, _ in SMALL])
    return jnp.pad(flat, (0, SMALL_ROWS * 128 - flat.shape[0])).reshape(SMALL_ROWS, 128)


def _unpack_small(buf):
    flat, out, off = buf.reshape(-1), {}, 0
    for n, shape in SMALL:
        out[n] = flat[off:off + _size(shape)].reshape(shape)
        off += _size(shape)
    return out


def kernel(x, norm_mix, w_in, fox_f_bias, hg_lb_logits, hg_norm, w_branch_a, w_branch_b, w_out, norm_ffn, w_up, conv_w, conv_b, w_down, norm_final, loss_target, m_norm_mix, m_w_in, m_fox_f_bias, m_hg_lb_logits, m_hg_norm, m_w_branch_a, m_w_branch_b, m_w_out, m_norm_ffn, m_w_up, m_conv_w, m_conv_b, m_w_down, m_norm_final, v_norm_mix, v_w_in, v_fox_f_bias, v_hg_lb_logits, v_hg_norm, v_w_branch_a, v_w_branch_b, v_w_out, v_norm_ffn, v_w_up, v_conv_w, v_conv_b, v_w_down, v_norm_final):
    wv = dict(norm_mix=norm_mix, w_in=w_in, fox_f_bias=fox_f_bias, hg_lb_logits=hg_lb_logits, hg_norm=hg_norm,
              w_branch_a=w_branch_a, w_branch_b=w_branch_b, w_out=w_out, norm_ffn=norm_ffn, w_up=w_up, conv_w=conv_w,
              conv_b=conv_b, w_down=w_down, norm_final=norm_final)
    mv = dict(norm_mix=m_norm_mix, w_in=m_w_in, fox_f_bias=m_fox_f_bias, hg_lb_logits=m_hg_lb_logits, hg_norm=m_hg_norm,
              w_branch_a=m_w_branch_a, w_branch_b=m_w_branch_b, w_out=m_w_out, norm_ffn=m_norm_ffn, w_up=m_w_up,
              conv_w=m_conv_w, conv_b=m_conv_b, w_down=m_w_down, norm_final=m_norm_final)
    vv = dict(norm_mix=v_norm_mix, w_in=v_w_in, fox_f_bias=v_fox_f_bias, hg_lb_logits=v_hg_lb_logits, hg_norm=v_hg_norm,
              w_branch_a=v_w_branch_a, w_branch_b=v_w_branch_b, w_out=v_w_out, norm_ffn=v_norm_ffn, w_up=v_w_up,
              conv_w=v_conv_w, conv_b=v_conv_b, w_down=v_w_down, norm_final=v_norm_final)

    (g_in,) = _comm_call(_GatherComm([w_in[0].astype(BF16)]), name="gather_w_in")
    w = dict(wm=jnp.concatenate([g_in[d] for d in range(FF_DEV)]
                                + [g_in[FF_DEV][:, :FF_OFF], g_in[FF_DEV][:, FF_OFF + FOX_HEADS:]]
                                + [g_in[d] for d in range(FF_DEV + 1, N_DEV)], axis=1),
             wff=jnp.pad(g_in[FF_DEV][:, FF_OFF:FF_OFF + FOX_HEADS], ((0, 0), (0, 128 - FOX_HEADS))))
    late = _GatherComm([w_branch_a[0].astype(BF16), w_branch_b[0].astype(BF16), w_out[0].astype(BF16),
                        w_up[0].astype(BF16), conv_w[0], w_down[0].astype(BF16)])
    p = dict(norm_mix=norm_mix[0], fox_f_bias=fox_f_bias[0], hg_lb_logits=hg_lb_logits, hg_norm=hg_norm[0],
             norm_ffn=norm_ffn[0], cbg=conv_b[:, :D_FF], cbv=conv_b[:, D_FF:], norm_final=norm_final)
    loss, dx, grads = _local_step(x[0], loss_target[0], w, p, late=late, exchange=True)
    loss = lax.psum(loss[0, 0], ("x", "y", "c"))

    small = _pack_small(dict(
        norm_mix=grads["norm_mix"], fox_f_bias=grads["fox_f_bias"], hg_lb_logits=grads["hg_lb_logits"],
        hg_norm=grads["hg_norm"], norm_ffn=grads["norm_ffn"], conv_b=jnp.concatenate([grads["cbg"], grads["cbv"]], axis=1),
        norm_final=grads["norm_final"]))
    (small_parts,) = _comm_call(_ExchangeComm([jnp.broadcast_to(small[None], (N_DEV, SMALL_ROWS, 128))]),
                                name="exchange_small")
    ea, eb, eo, eup, ed = grads["early_parts"]
    p_in, p_cw = grads["late_parts"]
    parts = [p_in, ea, eb, eo, eup, p_cw, ed, small_parts]
    res = {}
    for (n, shape, tile), part in zip(SHARDED, parts):
        outs = _adamw(part, wv[n].reshape(shape), mv[n].reshape(shape), vv[n].reshape(shape), name="adamw_" + n, T=tile)
        res[n] = [o.reshape(wv[n].shape) for o in outs]
    outs = _adamw(parts[-1], _pack_small(wv), _pack_small(mv), _pack_small(vv), name="adamw_small", T=SMALL_ROWS)
    small = [_unpack_small(o) for o in outs]
    for n, _ in SMALL:
        res[n] = [s[n] for s in small]
    return (loss, dx[None], *[res[n][0] for n in NAMES], *[res[n][1] for n in NAMES],
            *[res[n][2] for n in NAMES], *[res[n][3] for n in NAMES])
```

```python
import jax
import jax.numpy as jnp
from jax import lax
from jax.experimental import pallas as pl
from jax.experimental.pallas import tpu as pltpu

F32 = jnp.float32
BF16 = jnp.bfloat16

D_MODEL = 1024
HG_HEADS = 8
HG_DK = 128
HG_DV = 128
HG_CHUNK = 64
FOX_HEADS = 16
FOX_DH = 64
D_FF = 2816
EPS = 1e-6
N_DEV = 8

ADAM_LR = 0.001
ADAM_B1 = 0.9
ADAM_B2 = 0.999
ADAM_EPS = 1e-08
ADAM_WD = 0.01
ADAM_STEP = 10

VMEM_LIMIT = 56 * 1024 * 1024


def _cparams(sem):
    return pltpu.CompilerParams(dimension_semantics=sem, vmem_limit_bytes=VMEM_LIMIT)


MESH = pl.DeviceIdType.MESH
ANY = pl.BlockSpec(memory_space=pl.ANY)
SMEM = pl.BlockSpec(memory_space=pltpu.SMEM)


class _GatherComm:
    def __init__(self, shards):
        self.inputs = list(shards)
        n = self.n = len(shards)
        self.out_shapes = [jax.ShapeDtypeStruct((N_DEV,) + s.shape, s.dtype) for s in shards]
        self.scratch = [pltpu.SemaphoreType.DMA((n, 7)), pltpu.SemaphoreType.DMA((n, 7)), pltpu.SemaphoreType.DMA((n,))]

    def _parts(self, x_refs, out_refs, sems):
        send_sems, recv_sems, local_sems = sems
        x, y, c = lax.axis_index("x"), lax.axis_index("y"), lax.axis_index("c")
        me, sibling = (x, y, c), (x, y, 1 - c)
        chips = [(1 - x, y), (x, 1 - y), (1 - x, 1 - y)]

        def copy(t, k, block, to, src=None):
            slot = out_refs[t].at[4 * block[0] + 2 * block[1] + block[2]]
            return pltpu.make_async_remote_copy(
                src_ref=slot if src is None else src, dst_ref=slot,
                send_sem=send_sems.at[t, k], recv_sem=recv_sems.at[t, k], device_id=to, device_id_type=MESH)

        mine = [pltpu.make_async_copy(x_refs[t], out_refs[t].at[4 * x + 2 * y + c], local_sems.at[t])
                for t in range(self.n)]
        first = []
        for t in range(self.n):
            first.append(copy(t, 0, me, sibling, src=x_refs[t]))
            first += [copy(t, 1 + j, me, (*chip, c), src=x_refs[t]) for j, chip in enumerate(chips)]
        return c, me, sibling, chips, copy, mine, first

    def start(self, x_refs, out_refs, sems):
        _, _, _, _, _, mine, first = self._parts(x_refs, out_refs, sems)
        for cp in mine + first:
            cp.start()

    def finish(self, x_refs, out_refs, sems):
        c, me, sibling, chips, copy, mine, first = self._parts(x_refs, out_refs, sems)
        passed = []
        for j, chip in enumerate(chips):
            for t in range(self.n):
                copy(t, 1 + j, (*chip, c), me).wait_recv()
                passed.append(copy(t, 4 + j, (*chip, c), sibling))
                passed[-1].start()
        for t in range(self.n):
            copy(t, 0, sibling, me).wait_recv()
            for j, chip in enumerate(chips):
                copy(t, 4 + j, (*chip, 1 - c), me).wait_recv()
        for cp in first + passed:
            cp.wait_send()
        for cp in mine:
            cp.wait()


class _ExchangeComm:
    def __init__(self, blocks):
        self.inputs = list(blocks)
        n = self.n = len(blocks)
        self.out_shapes = [jax.ShapeDtypeStruct(b.shape, b.dtype) for b in blocks]
        self.scratch = [pltpu.SemaphoreType.DMA((n, 7)), pltpu.SemaphoreType.DMA((n, 7)), pltpu.SemaphoreType.DMA((n,))]

    def _parts(self, g_refs, out_refs, sems):
        send_sems, recv_sems, local_sems = sems
        x, y, c = lax.axis_index("x"), lax.axis_index("y"), lax.axis_index("c")
        me = 4 * x + 2 * y + c
        mine = [pltpu.make_async_copy(g_refs[t].at[me], out_refs[t].at[me], local_sems.at[t]) for t in range(self.n)]
        sends, recvs = [], []
        for k in range(1, N_DEV):
            px = 1 - x if k & 4 else x
            py = 1 - y if k & 2 else y
            pc = 1 - c if k & 1 else c
            p = 4 * px + 2 * py + pc
            for t in range(self.n):
                sends.append(pltpu.make_async_remote_copy(
                    src_ref=g_refs[t].at[p], dst_ref=out_refs[t].at[me], send_sem=send_sems.at[t, k - 1],
                    recv_sem=recv_sems.at[t, k - 1], device_id=(px, py, pc), device_id_type=MESH))
                recvs.append(pltpu.make_async_remote_copy(
                    src_ref=g_refs[t].at[p], dst_ref=out_refs[t].at[p], send_sem=send_sems.at[t, k - 1],
                    recv_sem=recv_sems.at[t, k - 1], device_id=(px, py, pc), device_id_type=MESH))
        return mine, sends, recvs

    def start(self, g_refs, out_refs, sems):
        mine, sends, _ = self._parts(g_refs, out_refs, sems)
        for cp in mine + sends:
            cp.start()

    def finish(self, g_refs, out_refs, sems):
        mine, sends, recvs = self._parts(g_refs, out_refs, sems)
        for cp in recvs:
            cp.wait_recv()
        for cp in sends:
            cp.wait_send()
        for cp in mine:
            cp.wait()


def _comm_call(comm, *, name):
    n = comm.n

    def body(*refs):
        comm.start(refs[:n], refs[n:2 * n], refs[2 * n:])
        comm.finish(refs[:n], refs[n:2 * n], refs[2 * n:])

    return pl.pallas_call(body, name=name, in_specs=[ANY] * n, out_specs=[ANY] * n, out_shape=comm.out_shapes,
                          scratch_shapes=comm.scratch)(*comm.inputs)


_DIMS = {
    "nn": (((1,), (0,)), ((), ())),
    "nt": (((1,), (1,)), ((), ())),
    "tn": (((0,), (0,)), ((), ())),
}

MATMUL_VMEM_BUDGET = 36 * 1024 * 1024
MAX_TILE = 1536


def _pick(n, prefs):
    for p in prefs:
        if n % p == 0:
            return p
    return n


def _tile_options(n):
    return [d for d in range(128, min(n, MAX_TILE) + 1, 128) if n % d == 0] or [n]


def _pick_tiles(M, N, tk, nk, sa, sb, so, has_addend, tm, tn):
    best = None
    for cm in ([tm] if tm else _tile_options(M)):
        for cn in ([tn] if tn else _tile_options(N)):
            need = 2 * (cm * tk * sa + tk * cn * sb + cm * cn * so + (cm * cn * 4 if has_addend else 0))
            need += cm * cn * 4 if nk > 1 else 0
            if need <= MATMUL_VMEM_BUDGET and (best is None or cm * cn > best[0] * best[1]
                                               or (cm * cn == best[0] * best[1] and cn > best[1])):
                best = (cm, cn)
    assert best is not None, (M, N, tk)
    return best


def _matmul(a, b, form, *, out_dtype=F32, addend=None, tm=None, tn=None, tk=None, comm=None, name):
    if form == "nn":
        (M, K), (K2, N) = a.shape, b.shape
    elif form == "nt":
        (M, K), (N, K2) = a.shape, b.shape
    else:
        (K, M), (K2, N) = a.shape, b.shape
    assert K == K2, (a.shape, b.shape, form)
    tk = tk or (K if K <= 2816 else _pick(K, (1024, 512, 256, 128)))
    nk = K // tk
    if tm is None or tn is None:
        tm, tn = _pick_tiles(M, N, tk, nk, a.dtype.itemsize, b.dtype.itemsize, jnp.dtype(out_dtype).itemsize,
                             addend is not None, tm, tn)
    assert M % tm == 0 and N % tn == 0 and K % tk == 0, (M, N, K, tm, tn, tk)
    dims = _DIMS[form]
    nc = comm.n if comm is not None else 0
    grid = (M // tm, N // tn, nk)

    def body(*refs):
        a_ref, b_ref = refs[:2]
        pos = 2
        add_ref = refs[pos] if addend is not None else None
        pos += addend is not None
        c_in, o_ref, c_out = refs[pos:pos + nc], refs[pos + nc], refs[pos + nc + 1:pos + 2 * nc + 1]
        pos += 2 * nc + 1
        acc_ref = refs[pos] if nk > 1 else None
        c_sems = refs[pos + (nk > 1):]
        if comm is not None:
            ids = [pl.program_id(d) for d in range(3)]

            @pl.when((ids[0] == 0) & (ids[1] == 0) & (ids[2] == 0))
            def _():
                comm.start(c_in, c_out, c_sems)

        def finish(r):
            if add_ref is not None:
                r = r + add_ref[...].astype(F32)
            o_ref[...] = r.astype(o_ref.dtype)

        part = lax.dot_general(a_ref[...].astype(BF16), b_ref[...].astype(BF16), dims, preferred_element_type=F32)
        if nk == 1:
            finish(part)
        else:
            k = pl.program_id(2)

            @pl.when(k == 0)
            def _():
                acc_ref[...] = part

            @pl.when(k > 0)
            def _():
                acc_ref[...] += part

            @pl.when(k == nk - 1)
            def _():
                finish(acc_ref[...])

        if comm is not None:
            @pl.when((ids[0] == grid[0] - 1) & (ids[1] == grid[1] - 1) & (ids[2] == grid[2] - 1))
            def _():
                comm.finish(c_in, c_out, c_sems)

    if form == "nn":
        a_spec = pl.BlockSpec((tm, tk), lambda i, j, k: (i, k))
        b_spec = pl.BlockSpec((tk, tn), lambda i, j, k: (k, j))
    elif form == "nt":
        a_spec = pl.BlockSpec((tm, tk), lambda i, j, k: (i, k))
        b_spec = pl.BlockSpec((tn, tk), lambda i, j, k: (j, k))
    else:
        a_spec = pl.BlockSpec((tk, tm), lambda i, j, k: (k, i))
        b_spec = pl.BlockSpec((tk, tn), lambda i, j, k: (k, j))
    o_spec = pl.BlockSpec((tm, tn), lambda i, j, k: (i, j))
    in_specs = [a_spec, b_spec] + ([o_spec] if addend is not None else [])
    args = (a, b) + ((addend,) if addend is not None else ())
    out_shape = jax.ShapeDtypeStruct((M, N), out_dtype)
    scratch = [pltpu.VMEM((tm, tn), F32)] if nk > 1 else []
    if comm is None:
        return pl.pallas_call(
            body, name=name, grid=grid, in_specs=in_specs, out_specs=o_spec, out_shape=out_shape,
            scratch_shapes=scratch, compiler_params=_cparams(("parallel", "parallel", "arbitrary")),
        )(*args)
    outs = pl.pallas_call(
        body, name=name, grid=grid, in_specs=in_specs + [ANY] * nc, out_specs=[o_spec] + [ANY] * nc,
        out_shape=[out_shape] + comm.out_shapes, scratch_shapes=scratch + comm.scratch,
        compiler_params=_cparams(("arbitrary", "arbitrary", "arbitrary")),
    )(*args, *comm.inputs)
    return outs[0], outs[1:]


def _rms_fwd(x, g, *, name, tm=512):
    M, D = x.shape
    tm = min(tm, M)

    def body(x_ref, g_ref, n_ref):
        xf = x_ref[...]
        r = lax.rsqrt(jnp.mean(xf * xf, axis=-1, keepdims=True) + EPS)
        n_ref[...] = (xf * r * g_ref[...]).astype(n_ref.dtype)

    return pl.pallas_call(
        body, name=name, grid=(M // tm,),
        in_specs=[pl.BlockSpec((tm, D), lambda i: (i, 0)), pl.BlockSpec((1, D), lambda i: (0, 0))],
        out_specs=pl.BlockSpec((tm, D), lambda i: (i, 0)),
        out_shape=jax.ShapeDtypeStruct((M, D), BF16),
        compiler_params=_cparams(("parallel",)),
    )(x, g.reshape(1, D))


def _rms_bwd(x, g, dn, dres, *, name, tm=512):
    M, D = x.shape
    tm = min(tm, M)

    def body(x_ref, g_ref, dn_ref, dres_ref, dx_ref, dg_ref):
        @pl.when(pl.program_id(0) == 0)
        def _():
            dg_ref[...] = jnp.zeros_like(dg_ref)

        xf = x_ref[...]
        r = lax.rsqrt(jnp.mean(xf * xf, axis=-1, keepdims=True) + EPS)
        xh = xf * r
        dn_ = dn_ref[...].astype(F32)
        dg_ref[...] += jnp.sum(dn_ * xh, axis=0, keepdims=True)
        dxh = dn_ * g_ref[...]
        dx = r * (dxh - xh * jnp.mean(dxh * xh, axis=-1, keepdims=True))
        dx_ref[...] = dres_ref[...] + dx

    row = pl.BlockSpec((tm, D), lambda i: (i, 0))
    vec = pl.BlockSpec((1, D), lambda i: (0, 0))
    return pl.pallas_call(
        body, name=name, grid=(M // tm,),
        in_specs=[row, vec, row, row], out_specs=[row, vec],
        out_shape=[jax.ShapeDtypeStruct((M, D), F32), jax.ShapeDtypeStruct((1, D), F32)],
        compiler_params=_cparams(("arbitrary",)),
    )(x, g.reshape(1, D), dn, dres)


def _loss_head(h, g, tgt, *, name, tm=512):
    M, D = h.shape
    tm = min(tm, M)

    def body(h_ref, g_ref, t_ref, loss_ref, dh_ref, dg_ref):
        @pl.when(pl.program_id(0) == 0)
        def _():
            dg_ref[...] = jnp.zeros_like(dg_ref)
            loss_ref[...] = jnp.zeros_like(loss_ref)

        xf = h_ref[...]
        r = lax.rsqrt(jnp.mean(xf * xf, axis=-1, keepdims=True) + EPS)
        xh = xf * r
        err = xh * g_ref[...] - t_ref[...]
        part = jnp.sum(jnp.mean(err * err, axis=-1, keepdims=True), axis=0, keepdims=True)
        loss_ref[...] += 0.5 * part
        dy = err * (1.0 / D)
        dg_ref[...] += jnp.sum(dy * xh, axis=0, keepdims=True)
        dxh = dy * g_ref[...]
        dh_ref[...] = r * (dxh - xh * jnp.mean(dxh * xh, axis=-1, keepdims=True))

    row = pl.BlockSpec((tm, D), lambda i: (i, 0))
    vec = pl.BlockSpec((1, D), lambda i: (0, 0))
    one = pl.BlockSpec((1, 1), lambda i: (0, 0))
    return pl.pallas_call(
        body, name=name, grid=(M // tm,),
        in_specs=[row, vec, row], out_specs=[one, row, vec],
        out_shape=[jax.ShapeDtypeStruct((1, 1), F32), jax.ShapeDtypeStruct((M, D), F32),
                   jax.ShapeDtypeStruct((1, D), F32)],
        compiler_params=_cparams(("arbitrary",)),
    )(h, g.reshape(1, D), tgt)


HG_MID = HG_CHUNK // 2 - 1
EXP_CAP = 80.0


def _sigmoid(x):
    return 1.0 / (1.0 + jnp.exp(-x))


def _dot(a, b, dims, precision=None):
    return lax.dot_general(a, b, dims, preferred_element_type=F32, precision=precision)


def _bdot(a, b, form):
    return _dot(a.astype(BF16), b.astype(BF16), _DIMS[form])


def _split2(x):
    hi = x.astype(BF16)
    return hi, (x - hi.astype(F32)).astype(BF16)


def _dot3(a, b, form):
    d = _DIMS[form]
    return _dot(a[0], b[0], d) + (_dot(a[0], b[1], d) + _dot(a[1], b[0], d))


def _hgrn_chunk_common(hq, hf, lbv, tril, rid):
    sq = _sigmoid(hq)
    q = hq * sq
    sg = _sigmoid(hf)
    f = lbv + (1.0 - lbv) * sg
    k = (1.0 - lbv) * (1.0 - sg)
    g = jnp.log(f)
    b = _dot(tril, g, _DIMS["nn"], precision=lax.Precision.HIGHEST)
    bref = jnp.sum(jnp.where(rid == HG_MID, b, 0.0), axis=0, keepdims=True)
    bend = jnp.sum(jnp.where(rid == HG_CHUNK - 1, b, 0.0), axis=0, keepdims=True)
    eb = jnp.exp(b)
    e1 = jnp.exp(jnp.minimum(b - bref, EXP_CAP))
    e2 = jnp.exp(jnp.minimum(bref - b, EXP_CAP))
    e3 = jnp.exp(bend - b)
    return sq, q, sg, f, k, bend, eb, e1, e2, e3


def _hgrn_fwd(proj, lb, gnorm, *, name, T=1024):
    S = proj.shape[0]
    T = min(T, S)
    nch = T // HG_CHUNK
    C = HG_CHUNK

    def body(hq_ref, hf_ref, hi_ref, hg_ref, lb_ref, gn_ref, o_ref, oa_ref, st_ref, state):
        @pl.when(pl.program_id(1) == 0)
        def _():
            state[...] = jnp.zeros_like(state)

        lbv = lb_ref[...]
        gn = gn_ref[...]
        row = lax.broadcasted_iota(jnp.int32, (C, C), 0)
        col = lax.broadcasted_iota(jnp.int32, (C, C), 1)
        causal = row >= col
        tril = causal.astype(F32)
        rid = lax.broadcasted_iota(jnp.int32, (C, HG_DK), 0)
        sls = [pl.ds(c * C, C) for c in range(nch)]
        pre = [_hgrn_chunk_common(hq_ref[sl, :], hf_ref[sl, :], lbv, tril, rid) for sl in sls]
        v_l = [hi_ref[sl, :].astype(BF16) for sl in sls]
        a_l, u_l = [], []
        for c in range(nch):
            _, q, _, _, k, _, _, e1, e2, e3 = pre[c]
            a_l.append(jnp.where(causal, _bdot(q * e1, k * e2, "nt"), 0.0))
            u_l.append(_bdot(v_l[c], k * e3, "tn"))
        o_l = [_bdot(a_l[c], v_l[c], "nn") for c in range(nch)]
        st = state[...]
        st_l = []
        for c in range(nch):
            st_l.append(st)
            st = st * jnp.exp(pre[c][5]) + u_l[c]
        state[...] = st
        for c in range(nch):
            st_ref[0, c] = st_l[c]
            o_l[c] = o_l[c] + _bdot(pre[c][1] * pre[c][6], st_l[c], "nt")
        for c in range(nch):
            o, hg = o_l[c], hg_ref[sls[c], :]
            o_ref[sls[c], :] = o
            r = lax.rsqrt(jnp.mean(o * o, axis=-1, keepdims=True) + EPS)
            oa_ref[sls[c], :] = (o * r * gn * (hg * _sigmoid(hg))).astype(oa_ref.dtype)

    def grp(gidx):
        return pl.BlockSpec((T, 128), lambda h, t: (t, gidx * 8 + h))

    return pl.pallas_call(
        body, name=name, grid=(HG_HEADS, S // T),
        in_specs=[grp(0), grp(1), grp(2), grp(3),
                  pl.BlockSpec((1, 128), lambda h, t: (0, h)), pl.BlockSpec((1, 128), lambda h, t: (0, 0))],
        out_specs=[pl.BlockSpec((T, 128), lambda h, t: (t, h)), pl.BlockSpec((T, 128), lambda h, t: (t, h)),
                   pl.BlockSpec((1, nch, HG_DV, HG_DK), lambda h, t: (h, t, 0, 0))],
        out_shape=[jax.ShapeDtypeStruct((S, HG_HEADS * HG_DV), F32), jax.ShapeDtypeStruct((S, HG_HEADS * HG_DV), BF16),
                   jax.ShapeDtypeStruct((HG_HEADS, S // C, HG_DV, HG_DK), F32)],
        scratch_shapes=[pltpu.VMEM((HG_DV, HG_DK), F32)],
        compiler_params=_cparams(("parallel", "arbitrary")),
    )(proj, proj, proj, proj, lb, gnorm)


def _hgrn_bwd(proj, lb, gnorm, o, states, doa, *, name, T=1024):
    S = proj.shape[0]
    T = min(T, S)
    nch = T // HG_CHUNK
    C = HG_CHUNK
    nT = S // T

    def body(hq_ref, hf_ref, hi_ref, hg_ref, lb_ref, gn_ref, o_ref, st_ref, doa_ref,
             dhq_ref, dhf_ref, dhi_ref, dhg_ref, dlb_ref, dgn_ref, dstate):
        @pl.when(pl.program_id(1) == 0)
        def _():
            dstate[...] = jnp.zeros_like(dstate)
            dlb_ref[...] = jnp.zeros_like(dlb_ref)
            dgn_ref[...] = jnp.zeros_like(dgn_ref)

        lbv = lb_ref[...]
        gn = gn_ref[...]
        row = lax.broadcasted_iota(jnp.int32, (C, C), 0)
        col = lax.broadcasted_iota(jnp.int32, (C, C), 1)
        causal = row >= col
        tril = causal.astype(F32)
        triu = (row <= col).astype(F32)
        rid = lax.broadcasted_iota(jnp.int32, (C, HG_DK), 0)
        rng = range(nch)
        sls = [pl.ds(c * C, C) for c in rng]
        pre = [_hgrn_chunk_common(hq_ref[sl, :], hf_ref[sl, :], lbv, tril, rid) for sl in sls]
        do2, dgn_acc = [], jnp.zeros((1, HG_DV), F32)
        for c in rng:
            hg, ov = hg_ref[sls[c], :], o_ref[sls[c], :]
            r = lax.rsqrt(jnp.mean(ov * ov, axis=-1, keepdims=True) + EPS)
            xh = ov * r
            sgg = _sigmoid(hg)
            d_oa = doa_ref[sls[c], :].astype(F32)
            dz = d_oa * (hg * sgg)
            dhg_ref[sls[c], :] = (d_oa * (xh * gn) * (sgg * (1.0 + hg * (1.0 - sgg)))).astype(dhg_ref.dtype)
            dgn_acc = dgn_acc + jnp.sum(dz * xh, axis=0, keepdims=True)
            dxh = dz * gn
            do2.append(_split2(r * (dxh - xh * jnp.mean(dxh * xh, axis=-1, keepdims=True))))
        dgn_ref[0] += dgn_acc
        qi = [pre[c][1] * pre[c][6] for c in rng]
        qp = [pre[c][1] * pre[c][7] for c in rng]
        kp = [pre[c][4] * pre[c][8] for c in rng]
        kend = [pre[c][4] * pre[c][9] for c in rng]
        qi2, qp2, kp2, kend2 = ([_split2(t) for t in lst] for lst in (qi, qp, kp, kend))
        v2 = [_split2(hi_ref[sl, :]) for sl in sls]
        st0 = [st_ref[0, c] for c in rng]
        a_l = [jnp.where(causal, _dot(qp2[c][0], kp2[c][0], _DIMS["nt"]), 0.0).astype(BF16) for c in rng]
        da2 = [_split2(jnp.where(causal, _dot3(do2[c], v2[c], "nt"), 0.0)) for c in rng]
        dqi = [_dot3(do2[c], _split2(st0[c]), "nn") for c in rng]
        w_l = [_dot3(do2[c], qi2[c], "tn") for c in rng]
        ds = dstate[...]
        ds1 = [None] * nch
        for c in reversed(rng):
            ds1[c] = ds
            ds = ds * jnp.exp(pre[c][5]) + w_l[c]
        dstate[...] = ds
        ds12 = [_split2(t) for t in ds1]
        dqp = [_dot3(da2[c], kp2[c], "nn") for c in rng]
        dkp = [_dot3(da2[c], qp2[c], "tn") for c in rng]
        dv = [_dot(a_l[c], do2[c][0], _DIMS["tn"]) + _dot(kend2[c][0], ds12[c][0], _DIMS["nt"]) for c in rng]
        dkend = [_dot3(v2[c], ds12[c], "nn") for c in rng]
        dq_l, dk_l, db_l = [], [], []
        for c in rng:
            _, _, _, _, _, bend, eb, e1, e2, e3 = pre[c]
            dq_l.append(dqi[c] * eb + dqp[c] * e1)
            dk_l.append(dkp[c] * e2 + dkend[c] * e3)
            db = dqi[c] * qi[c] + dqp[c] * qp[c] - dkp[c] * kp[c] - dkend[c] * kend[c]
            dbend = (jnp.sum(dkend[c] * kend[c], axis=0, keepdims=True)
                     + jnp.exp(bend) * jnp.sum(ds1[c] * st0[c], axis=0, keepdims=True))
            db_l.append(db + jnp.where(rid == C - 1, dbend, 0.0))
        dg = [_dot(triu, db_l[c], _DIMS["nn"], precision=lax.Precision.HIGHEST) for c in rng]
        dlb_acc = jnp.zeros((1, HG_DK), F32)
        for c in rng:
            sq, _, sg, f, _, _, _, _, _, _ = pre[c]
            hq = hq_ref[sls[c], :]
            df = dg[c] / f - dk_l[c]
            dlb_acc = dlb_acc + jnp.sum(df * (1.0 - sg), axis=0, keepdims=True)
            dhf_ref[sls[c], :] = (df * (1.0 - lbv) * sg * (1.0 - sg)).astype(dhf_ref.dtype)
            dhq_ref[sls[c], :] = (dq_l[c] * (sq * (1.0 + hq * (1.0 - sq)))).astype(dhq_ref.dtype)
            dhi_ref[sls[c], :] = dv[c].astype(dhi_ref.dtype)
        dlb_ref[...] += dlb_acc

    def grp(gidx):
        return pl.BlockSpec((T, 128), lambda h, t: (nT - 1 - t, gidx * 8 + h))

    tok = pl.BlockSpec((T, 128), lambda h, t: (nT - 1 - t, h))
    big = jax.ShapeDtypeStruct((S, HG_HEADS * HG_DV), BF16)
    return pl.pallas_call(
        body, name=name, grid=(HG_HEADS, nT),
        in_specs=[grp(0), grp(1), grp(2), grp(3),
                  pl.BlockSpec((1, 128), lambda h, t: (0, h)), pl.BlockSpec((1, 128), lambda h, t: (0, 0)),
                  tok, pl.BlockSpec((1, nch, HG_DV, HG_DK), lambda h, t: (h, nT - 1 - t, 0, 0)), tok],
        out_specs=[tok, tok, tok, tok, pl.BlockSpec((1, 128), lambda h, t: (0, h)),
                   pl.BlockSpec((1, 1, 128), lambda h, t: (h, 0, 0))],
        out_shape=[big, big, big, big, jax.ShapeDtypeStruct((1, HG_HEADS * HG_DK), F32),
                   jax.ShapeDtypeStruct((HG_HEADS, 1, HG_DV), F32)],
        scratch_shapes=[pltpu.VMEM((HG_DV, HG_DK), F32)],
        compiler_params=_cparams(("parallel", "arbitrary")),
    )(proj, proj, proj, proj, lb, gnorm, o, states, doa)


def _lb_fwd(logits, *, name):
    def body(l_ref, lb_ref):
        lb_ref[...] = _sigmoid(l_ref[0:1, :] - l_ref[1:2, :])

    return pl.pallas_call(body, name=name, out_shape=jax.ShapeDtypeStruct((1, logits.shape[1]), F32))(logits)


def _lb_bwd(logits, dlb, *, name):
    def body(l_ref, d_ref, o_ref):
        lbv = _sigmoid(l_ref[0:1, :] - l_ref[1:2, :])
        t = d_ref[...] * lbv * (1.0 - lbv)
        o_ref[0:1, :] = t
        o_ref[1:2, :] = -t

    return pl.pallas_call(body, name=name, out_shape=jax.ShapeDtypeStruct(logits.shape, F32))(logits, dlb)


NEG = -1e30
FOX_SCALE = FOX_DH ** -0.5
FOX_PAIRS = FOX_HEADS // 2


def _fox_gate_fwd(ff, bias, *, name, T=512):
    S = ff.shape[0]
    T = min(T, S)

    def body(ff_ref, b_ref, c_ref, carry):
        @pl.when(pl.program_id(0) == 0)
        def _():
            carry[...] = jnp.zeros_like(carry)

        z = ff_ref[...] + b_ref[...]
        logf = jnp.minimum(z, 0.0) - jnp.log(1.0 + jnp.exp(-jnp.abs(z)))
        row = lax.broadcasted_iota(jnp.int32, (T, T), 0)
        col = lax.broadcasted_iota(jnp.int32, (T, T), 1)
        c = _dot((row >= col).astype(F32), logf, _DIMS["nn"], precision=lax.Precision.HIGHEST) + carry[...]
        c_ref[...] = c
        carry[...] = c[T - 1:T, :]

    return pl.pallas_call(
        body, name=name, grid=(S // T,),
        in_specs=[pl.BlockSpec((T, 128), lambda i: (i, 0)), pl.BlockSpec((1, 128), lambda i: (0, 0))],
        out_specs=pl.BlockSpec((T, 128), lambda i: (i, 0)),
        out_shape=jax.ShapeDtypeStruct((S, 128), F32),
        scratch_shapes=[pltpu.VMEM((1, 128), F32)],
        compiler_params=_cparams(("arbitrary",)),
    )(ff, bias)


def _fox_gate_bwd(ff, bias, dcs, *, name, T=512):
    S = ff.shape[0]
    T = min(T, S)
    nT = S // T

    def body(ff_ref, b_ref, d_ref, dff_ref, db_ref, carry):
        @pl.when(pl.program_id(0) == 0)
        def _():
            carry[...] = jnp.zeros_like(carry)
            db_ref[...] = jnp.zeros_like(db_ref)

        row = lax.broadcasted_iota(jnp.int32, (T, T), 0)
        col = lax.broadcasted_iota(jnp.int32, (T, T), 1)
        dlogf = carry[...] - _dot((row <= col).astype(F32), d_ref[...], _DIMS["nn"], precision=lax.Precision.HIGHEST)
        carry[...] = dlogf[0:1, :]
        dff = dlogf * (1.0 - _sigmoid(ff_ref[...] + b_ref[...]))
        dff_ref[...] = dff.astype(dff_ref.dtype)
        db_ref[...] += jnp.sum(dff, axis=0, keepdims=True)

    rev = pl.BlockSpec((T, 128), lambda i: (nT - 1 - i, 0))
    vec = pl.BlockSpec((1, 128), lambda i: (0, 0))
    return pl.pallas_call(
        body, name=name, grid=(nT,),
        in_specs=[rev, vec, rev], out_specs=[rev, vec],
        out_shape=[jax.ShapeDtypeStruct((S, 128), BF16), jax.ShapeDtypeStruct((1, 128), F32)],
        scratch_shapes=[pltpu.VMEM((1, 128), F32)],
        compiler_params=_cparams(("arbitrary",)),
    )(ff, bias, dcs)


AUG = FOX_DH
RSUM_LANE = 6


def _bias_lane(hh):
    return AUG * (1 - hh)


def _data_lanes(lane, hh):
    return (lane < AUG) if hh == 0 else (lane >= AUG)


def _split3(x):
    a = x.astype(BF16).astype(F32)
    r = x - a
    b = r.astype(BF16).astype(F32)
    return a, b, r - b


def _lane_fill(lane, base, pieces, start):
    for i, pc in enumerate(pieces):
        base = jnp.where(lane == start + i, pc, base)
    return base


FOX_TB = 512
FOX_SKIP = 32.0
N_STAT = 4


def _fox_prep(proj, c_tok, *, name):
    S = proj.shape[0]
    T = min(FOX_TB, S)

    def body(q_ref, k_ref, v_ref, c_ref, qa_ref, ka_ref, va_ref, st_ref):
        pair = pl.program_id(0)
        lane = lax.broadcasted_iota(jnp.int32, (T, 128), 1)
        lane1 = lax.broadcasted_iota(jnp.int32, (1, 128), 1)
        c = c_ref[...]
        q, k, v = q_ref[...], k_ref[...], v_ref[...]
        for hh in range(2):
            data, b0 = _data_lanes(lane, hh), _bias_lane(hh)
            ones3 = jnp.where((lane >= b0) & (lane < b0 + 3), 1.0, 0.0)

            def max_norm(t):
                tr = jnp.where(data, t.astype(BF16).astype(F32), 0.0)
                return jnp.sqrt(jnp.max(jnp.sum(tr * tr, axis=-1, keepdims=True), axis=0, keepdims=True))

            ch = jnp.sum(jnp.where(lane == 2 * pair + hh, c, 0.0), axis=-1, keepdims=True)
            c1, c2, c3 = _split3(ch)
            aug_q = _lane_fill(lane, jnp.where((lane >= b0 + 3) & (lane < b0 + 6), 1.0, 0.0), (c1, c2, c3), b0)
            aug_k = _lane_fill(lane, ones3, (-c1, -c2, -c3), b0 + 3)
            qa_ref[hh] = jnp.where(data, q * FOX_SCALE, aug_q).astype(BF16)
            ka_ref[hh] = jnp.where(data, k, aug_k).astype(BF16)
            va_ref[hh] = jnp.where(data, v, ones3).astype(BF16)
            stats = (max_norm(q * FOX_SCALE), jnp.max(ch, axis=0, keepdims=True), max_norm(k),
                     jnp.min(ch, axis=0, keepdims=True))
            st_ref[hh, 0] = _lane_fill(lane1, jnp.zeros((1, 128), F32), stats, 0)

    def grp(g):
        return pl.BlockSpec((T, 128), lambda p, t: (t, g * 8 + p))

    hm = pl.BlockSpec((2, T, 128), lambda p, t: (p, t, 0))
    out = jax.ShapeDtypeStruct((FOX_HEADS, S, 128), BF16)
    return pl.pallas_call(
        body, name=name, grid=(FOX_PAIRS, S // T),
        in_specs=[grp(4), grp(5), grp(6), pl.BlockSpec((T, 128), lambda p, t: (t, 0))],
        out_specs=[hm, hm, hm, pl.BlockSpec((2, 1, 1, 128), lambda p, t: (p, t, 0, 0))],
        out_shape=[out, out, out, jax.ShapeDtypeStruct((FOX_HEADS, S // T, 1, 128), F32)],
        compiler_params=_cparams(("parallel", "parallel")),
    )(proj, proj, proj, c_tok)


def _fox_bound(st_ref, head, nb, qi, ki):
    qb_, kb_ = (head * nb + qi) * N_STAT, (head * nb + ki) * N_STAT
    return st_ref[qb_] * st_ref[kb_ + 2] + st_ref[qb_ + 1] - st_ref[kb_ + 3] + 0.01


def _pair_lanes(lane, a0, a1):
    return jnp.where(lane < AUG, a0, a1)


def _first_live_key(st_ref, head, nb, qi, newest, thr):
    def body(t, k0):
        k = newest - t
        return jnp.where(_fox_bound(st_ref, head, nb, qi, k) > thr, k, k0)

    return lax.fori_loop(0, newest + 1, body, newest + 1)


def _last_live_query(st_ref, lm_ref, head, nb, ki):
    def body(t, i1):
        i = ki + 1 + t
        live = _fox_bound(st_ref, head, nb, i, ki) > lm_ref[head * nb + i] - FOX_SKIP
        return jnp.where(live, i, i1)

    return lax.fori_loop(0, nb - 1 - ki, body, ki)


class _BlockStream:
    def __init__(self, hbm_refs, bufs, sems, pair, tb):
        self.hbm, self.bufs, self.sems, self.pair, self.tb = hbm_refs, bufs, sems, pair, tb

    def _copies(self, blk, slot):
        rows = pl.ds(pl.multiple_of(blk * self.tb, self.tb), self.tb)
        return [pltpu.make_async_copy(h.at[pl.ds(2 * self.pair, 2), rows, :], b.at[slot], self.sems.at[n, slot])
                for n, (h, b) in enumerate(zip(self.hbm, self.bufs))]

    def start(self, blk, slot):
        for cp in self._copies(blk, slot):
            cp.start()

    def wait(self, blk, slot):
        for cp in self._copies(blk, slot):
            cp.wait()


def _fox_fwd(qa, ka, va, bounds, *, name):
    S = qa.shape[1]
    tb = min(FOX_TB, S)
    nb = S // tb

    def body(qa_ref, ka_hbm, va_hbm, st_ref, o_ref, qb_ref, lse_ref, kbuf, vbuf, sems, m_s, acc_s, m_min):
        pair, qi = pl.program_id(0), pl.program_id(1)
        stream = _BlockStream((ka_hbm, va_hbm), (kbuf, vbuf), sems, pair, tb)

        def head_step(hh, slot, masked):
            s = _dot(qa_ref[hh], kbuf[slot, hh], _DIMS["nt"])
            if masked:
                row = lax.broadcasted_iota(jnp.int32, (tb, tb), 0)
                col = lax.broadcasted_iota(jnp.int32, (tb, tb), 1)
                s = jnp.where(col <= row, s, NEG)
            m_old = m_s[hh]
            m_new = jnp.maximum(m_old, jnp.broadcast_to(jnp.max(s, axis=-1, keepdims=True), (tb, 128)))
            p = jnp.exp(s - jnp.concatenate([m_new] * (tb // 128), axis=1))
            acc_s[hh] = jnp.exp(m_old - m_new) * acc_s[hh] + _dot(p.astype(BF16), vbuf[slot, hh], _DIMS["nn"])
            m_s[hh] = m_new
            m_min[hh] = jnp.min(m_new)

        @pl.when(qi == 0)
        def _():
            stream.start(qi, 0)

        @pl.when(qi > 0)
        def _():
            stream.start(qi - 1, 1)

        m_s[...] = jnp.full_like(m_s, NEG)
        acc_s[...] = jnp.zeros_like(acc_s)
        stream.wait(qi, 0)
        for hh in range(2):
            head_step(hh, 0, True)

        @pl.when(qi > 1)
        def _():
            stream.start(qi - 2, 0)

        @pl.when(qi > 0)
        def _():
            stream.wait(qi - 1, 1)
            for hh in range(2):
                head_step(hh, 1, False)

        k0 = [_first_live_key(st_ref, 2 * pair + hh, nb, qi, qi - 2, m_min[hh] - FOX_SKIP) for hh in range(2)]
        n = qi - 1 - jnp.minimum(k0[0], k0[1])

        @pl.when((qi > 1) & (n == 0))
        def _():
            stream.wait(qi - 2, 0)

        def loop(t, carry):
            k = qi - 2 - t
            slot = t % 2
            stream.wait(k, slot)

            @pl.when(t + 1 < n)
            def _():
                stream.start(k - 1, 1 - slot)

            live = [k >= k0[hh] for hh in range(2)]

            @pl.when(live[0] & live[1])
            def _():
                for hh in range(2):
                    head_step(hh, slot, False)

            for hh in range(2):
                @pl.when(live[hh] & jnp.logical_not(live[1 - hh]))
                def _():
                    head_step(hh, slot, False)
            return carry

        lax.fori_loop(0, n, loop, 0)

        @pl.when(qi + 1 < nb)
        def _():
            stream.start(qi + 1, 0)

        lane = lax.broadcasted_iota(jnp.int32, (tb, 128), 1)
        outs = []
        for hh in range(2):
            acc = acc_s[hh]
            b0 = _bias_lane(hh)
            l = jnp.broadcast_to(acc[:, b0:b0 + 1], (tb, 128))
            outs.append(acc / l)
            lse = m_s[hh] + jnp.log(l)
            lse_ref[hh, 0] = jnp.min(lse, axis=0, keepdims=True)
            qf = qa_ref[hh].astype(F32)
            c_t = jnp.sum(jnp.where((lane >= b0) & (lane < b0 + 3), qf, 0.0), axis=-1, keepdims=True)
            cb = jnp.broadcast_to(c_t, (tb, 128)) - lse
            qb_ref[hh] = _lane_fill(lane, qf, _split3(cb), b0).astype(BF16)
        o_ref[...] = _pair_lanes(lane, outs[0], outs[1])

    qs = pl.BlockSpec((2, tb, 128), lambda p, i: (p, i, 0))
    return pl.pallas_call(
        body, name=name, grid=(FOX_PAIRS, nb),
        in_specs=[qs, ANY, ANY, SMEM],
        out_specs=[pl.BlockSpec((tb, 128), lambda p, i: (i, p)), qs,
                   pl.BlockSpec((2, 1, 1, 128), lambda p, i: (p, i, 0, 0))],
        out_shape=[jax.ShapeDtypeStruct((S, FOX_HEADS * FOX_DH), F32), jax.ShapeDtypeStruct((FOX_HEADS, S, 128), BF16),
                   jax.ShapeDtypeStruct((FOX_HEADS, nb, 1, 128), F32)],
        scratch_shapes=[pltpu.VMEM((2, 2, tb, 128), BF16), pltpu.VMEM((2, 2, tb, 128), BF16),
                        pltpu.SemaphoreType.DMA((2, 2)), pltpu.VMEM((2, tb, 128), F32), pltpu.VMEM((2, tb, 128), F32),
                        pltpu.SMEM((2,), F32)],
        compiler_params=_cparams(("arbitrary", "arbitrary")),
    )(qa, ka, va, bounds)


def _fox_bwd_prep(o, do, *, name, T=512):
    S = o.shape[0]
    T = min(T, S)

    def body(o_ref, do_ref, dob_ref):
        lane = lax.broadcasted_iota(jnp.int32, (T, 128), 1)
        d = do_ref[...].astype(F32)
        prod = d * o_ref[...]
        for hh in range(2):
            mine = _data_lanes(lane, hh)
            delta = jnp.sum(jnp.where(mine, prod, 0.0), axis=-1, keepdims=True)
            dob_ref[hh] = _lane_fill(lane, jnp.where(mine, d, 0.0), _split3(-delta), _bias_lane(hh)).astype(BF16)

    tok = pl.BlockSpec((T, 128), lambda p, t: (t, p))
    return pl.pallas_call(
        body, name=name, grid=(FOX_PAIRS, S // T),
        in_specs=[tok, tok], out_specs=pl.BlockSpec((2, T, 128), lambda p, t: (p, t, 0)),
        out_shape=jax.ShapeDtypeStruct((FOX_HEADS, S, 128), BF16),
        compiler_params=_cparams(("parallel", "parallel")),
    )(o, do)


def _fox_bwd_dq(qb, ka, va, dob, bounds, lse_min, *, name, comm=None):
    S = qb.shape[1]
    tb = min(FOX_TB, S)
    nb = S // tb
    nc = comm.n if comm is not None else 0

    def body(qb_ref, dob_ref, ka_hbm, va_hbm, st_ref, lm_ref, *rest):
        c_in, (dq_ref, dob2_ref), c_out = rest[:nc], rest[nc:nc + 2], rest[nc + 2:2 * nc + 2]
        kbuf, vbuf, sems, acc_s = rest[2 * nc + 2:2 * nc + 6]
        c_sems = rest[2 * nc + 6:]
        pair, qi = pl.program_id(0), pl.program_id(1)
        if comm is not None:
            @pl.when((pair == 0) & (qi == 0))
            def _():
                comm.start(c_in, c_out, c_sems)

        stream = _BlockStream((ka_hbm, va_hbm), (kbuf, vbuf), sems, pair, tb)
        k0 = [_first_live_key(st_ref, 2 * pair + hh, nb, qi, qi - 1, lm_ref[(2 * pair + hh) * nb + qi] - FOX_SKIP)
              for hh in range(2)]
        n = qi - jnp.minimum(k0[0], k0[1]) + 1

        @pl.when(qi == 0)
        def _():
            stream.start(qi, 0)

        acc_s[...] = jnp.zeros_like(acc_s)

        def head_step(hh, slot, k, masked):
            s = _dot(qb_ref[hh], kbuf[slot, hh], _DIMS["nt"])
            if masked:
                row = lax.broadcasted_iota(jnp.int32, (tb, tb), 0)
                col = lax.broadcasted_iota(jnp.int32, (tb, tb), 1)
                s = jnp.where(col <= row, s, NEG)
            ds = jnp.exp(s) * _dot(dob_ref[hh], vbuf[slot, hh], _DIMS["nt"])
            acc_s[hh] += _dot(ds.astype(BF16), kbuf[slot, hh], _DIMS["nn"])

        def loop(t, carry):
            k = qi - t
            slot = t % 2
            stream.wait(k, slot)

            @pl.when(t + 1 < n)
            def _():
                stream.start(k - 1, 1 - slot)

            @pl.when(t == 0)
            def _():
                for hh in range(2):
                    head_step(hh, slot, k, True)

            live = [(t > 0) & (k >= k0[hh]) for hh in range(2)]

            @pl.when(live[0] & live[1])
            def _():
                for hh in range(2):
                    head_step(hh, slot, k, False)

            for hh in range(2):
                @pl.when(live[hh] & jnp.logical_not(live[1 - hh]))
                def _():
                    head_step(hh, slot, k, False)
            return carry

        lax.fori_loop(0, n, loop, 0)

        @pl.when(qi + 1 < nb)
        def _():
            stream.start(qi + 1, 0)

        lane = lax.broadcasted_iota(jnp.int32, (tb, 128), 1)
        dq_ref[...] = (_pair_lanes(lane, acc_s[0], acc_s[1]) * FOX_SCALE).astype(dq_ref.dtype)
        for hh in range(2):
            b0 = _bias_lane(hh)
            r = jnp.broadcast_to(acc_s[hh][:, b0:b0 + 1], (tb, 128))
            dob2_ref[hh] = _lane_fill(lane, dob_ref[hh].astype(F32), _split3(r), b0 + RSUM_LANE).astype(BF16)
        if comm is not None:
            @pl.when((pair == FOX_PAIRS - 1) & (qi == nb - 1))
            def _():
                comm.finish(c_in, c_out, c_sems)

    qs = pl.BlockSpec((2, tb, 128), lambda p, i: (p, i, 0))
    outs = pl.pallas_call(
        body, name=name, grid=(FOX_PAIRS, nb),
        in_specs=[qs, qs, ANY, ANY, SMEM, SMEM] + [ANY] * nc,
        out_specs=[pl.BlockSpec((tb, 128), lambda p, i: (i, p)), qs] + [ANY] * nc,
        out_shape=[jax.ShapeDtypeStruct((S, FOX_HEADS * FOX_DH), BF16),
                   jax.ShapeDtypeStruct((FOX_HEADS, S, 128), BF16)] + (comm.out_shapes if comm is not None else []),
        scratch_shapes=[pltpu.VMEM((2, 2, tb, 128), BF16), pltpu.VMEM((2, 2, tb, 128), BF16),
                        pltpu.SemaphoreType.DMA((2, 2)), pltpu.VMEM((2, tb, 128), F32)]
        + (comm.scratch if comm is not None else []),
        compiler_params=_cparams(("arbitrary", "arbitrary")),
    )(qb, dob, ka, va, bounds, lse_min, *(comm.inputs if comm is not None else []))
    return (outs[0], outs[1]) if comm is None else (outs[0], outs[1], outs[2:])


def _fox_bwd_dkv(qb, ka, va, dob, bounds, lse_min, *, name):
    S = qb.shape[1]
    tb = min(FOX_TB, S)
    nb = S // tb

    def body(ka_ref, va_ref, qb_hbm, dob_hbm, st_ref, lm_ref, dk_ref, dv_ref, dcs_ref, qbuf, dbuf, sems, dk_s, dv_s):
        pair, ki = pl.program_id(0), pl.program_id(1)
        stream = _BlockStream((qb_hbm, dob_hbm), (qbuf, dbuf), sems, pair, tb)
        i1 = [_last_live_query(st_ref, lm_ref, 2 * pair + hh, nb, ki) for hh in range(2)]
        n = jnp.maximum(i1[0], i1[1]) - ki + 1

        @pl.when(ki == 0)
        def _():
            stream.start(ki, 0)

        dk_s[...] = jnp.zeros_like(dk_s)
        dv_s[...] = jnp.zeros_like(dv_s)

        def head_step(hh, slot, masked):
            st = _dot(ka_ref[hh], qbuf[slot, hh], _DIMS["nt"])
            if masked:
                row = lax.broadcasted_iota(jnp.int32, (tb, tb), 0)
                col = lax.broadcasted_iota(jnp.int32, (tb, tb), 1)
                st = jnp.where(row <= col, st, NEG)
            pt = jnp.exp(st)
            dst = pt * _dot(va_ref[hh], dbuf[slot, hh], _DIMS["nt"])
            dv_s[hh] += _dot(pt.astype(BF16), dbuf[slot, hh], _DIMS["nn"])
            dk_s[hh] += _dot(dst.astype(BF16), qbuf[slot, hh], _DIMS["nn"])

        def loop(t, carry):
            i = ki + t
            slot = t % 2
            stream.wait(i, slot)

            @pl.when(t + 1 < n)
            def _():
                stream.start(i + 1, 1 - slot)

            @pl.when(t == 0)
            def _():
                for hh in range(2):
                    head_step(hh, slot, True)

            live = [(t > 0) & (i <= i1[hh]) for hh in range(2)]

            @pl.when(live[0] & live[1])
            def _():
                for hh in range(2):
                    head_step(hh, slot, False)

            for hh in range(2):
                @pl.when(live[hh] & jnp.logical_not(live[1 - hh]))
                def _():
                    head_step(hh, slot, False)
            return carry

        lax.fori_loop(0, n, loop, 0)

        @pl.when(ki + 1 < nb)
        def _():
            stream.start(ki + 1, 0)

        lane = lax.broadcasted_iota(jnp.int32, (tb, 128), 1)
        dk_ref[...] = _pair_lanes(lane, dk_s[0], dk_s[1]).astype(dk_ref.dtype)
        dv_ref[...] = _pair_lanes(lane, dv_s[0], dv_s[1]).astype(dv_ref.dtype)
        for hh in range(2):
            b0 = _bias_lane(hh)
            rsum = (lane >= b0 + RSUM_LANE) & (lane < b0 + RSUM_LANE + 3)
            parts = jnp.where(lane == b0 + 3, dk_s[hh], jnp.where(rsum, -dv_s[hh], 0.0))
            dcs_ref[0, :, hh:hh + 1] = jnp.sum(parts, axis=-1, keepdims=True)

    ks = pl.BlockSpec((2, tb, 128), lambda p, j: (p, j, 0))
    tok = pl.BlockSpec((tb, 128), lambda p, j: (j, p))
    big = jax.ShapeDtypeStruct((S, FOX_HEADS * FOX_DH), BF16)
    return pl.pallas_call(
        body, name=name, grid=(FOX_PAIRS, nb),
        in_specs=[ks, ks, ANY, ANY, SMEM, SMEM],
        out_specs=[tok, tok, pl.BlockSpec((1, tb, 2), lambda p, j: (p, j, 0))],
        out_shape=[big, big, jax.ShapeDtypeStruct((FOX_PAIRS, S, 2), F32)],
        scratch_shapes=[pltpu.VMEM((2, 2, tb, 128), BF16), pltpu.VMEM((2, 2, tb, 128), BF16),
                        pltpu.SemaphoreType.DMA((2, 2)), pltpu.VMEM((2, tb, 128), F32), pltpu.VMEM((2, tb, 128), F32)],
        compiler_params=_cparams(("arbitrary", "arbitrary")),
    )(ka, va, qb, dob, bounds, lse_min)


def _merge_fwd(proj, pa, pb, *, name, T=512):
    S, D = pa.shape
    T = min(T, S)

    def body(ga_ref, gb_ref, pa_ref, pb_ref, m_ref):
        m_ref[...] = (_sigmoid(ga_ref[...]) * pa_ref[...] + _sigmoid(gb_ref[...]) * pb_ref[...]).astype(m_ref.dtype)

    tok = pl.BlockSpec((T, D), lambda i: (i, 0))
    return pl.pallas_call(
        body, name=name, grid=(S // T,),
        in_specs=[pl.BlockSpec((T, D), lambda i: (i, 7)), pl.BlockSpec((T, D), lambda i: (i, 8)), tok, tok],
        out_specs=tok, out_shape=jax.ShapeDtypeStruct((S, D), BF16),
        compiler_params=_cparams(("parallel",)),
    )(proj, proj, pa, pb)


def _merge_bwd(proj, pa, pb, dm, *, name, T=512):
    S, D = pa.shape
    T = min(T, S)

    def body(ga_ref, gb_ref, pa_ref, pb_ref, dm_ref, dpa_ref, dpb_ref, dga_ref, dgb_ref):
        dm_ = dm_ref[...]
        sa, sb = _sigmoid(ga_ref[...]), _sigmoid(gb_ref[...])
        dpa_ref[...] = (dm_ * sa).astype(BF16)
        dpb_ref[...] = (dm_ * sb).astype(BF16)
        dga_ref[...] = (dm_ * pa_ref[...] * sa * (1.0 - sa)).astype(BF16)
        dgb_ref[...] = (dm_ * pb_ref[...] * sb * (1.0 - sb)).astype(BF16)

    tok = pl.BlockSpec((T, D), lambda i: (i, 0))
    big = jax.ShapeDtypeStruct((S, D), BF16)
    return pl.pallas_call(
        body, name=name, grid=(S // T,),
        in_specs=[pl.BlockSpec((T, D), lambda i: (i, 7)), pl.BlockSpec((T, D), lambda i: (i, 8)), tok, tok, tok],
        out_specs=[tok, tok, tok, tok], out_shape=[big, big, big, big],
        compiler_params=_cparams(("parallel",)),
    )(proj, proj, pa, pb, dm)


INV_SQRT2 = 0.7071067811865476
INV_SQRT2PI = 0.3989422804014327


def _shifted(u, prev, rid):
    m1 = jnp.where(rid == 0, prev[7:8, :], pltpu.roll(u, 1, 0))
    m2 = jnp.where(rid == 0, prev[6:7, :], jnp.where(rid == 1, prev[7:8, :], pltpu.roll(u, 2, 0)))
    return m1, m2


def _conv_acc(u, prev, w_ref, b_ref, rid):
    m1, m2 = _shifted(u, prev, rid)
    return b_ref[...] + w_ref[0:1, :] * m2 + w_ref[1:2, :] * m1 + w_ref[2:3, :] * u, m1, m2


def _convglu_fwd(ug, uv, wg, wv, bg, bv, *, name, T=512, tc=256):
    S, F = ug.shape
    T = min(T, S)

    def body(ug_ref, uv_ref, wg_ref, wv_ref, bg_ref, bv_ref, a_ref, pg, pv):
        @pl.when(pl.program_id(1) == 0)
        def _():
            pg[...] = jnp.zeros_like(pg)
            pv[...] = jnp.zeros_like(pv)

        rid = lax.broadcasted_iota(jnp.int32, (T, tc), 0)
        g_, v_ = ug_ref[...], uv_ref[...]
        accg, _, _ = _conv_acc(g_, pg[...], wg_ref, bg_ref, rid)
        accv, _, _ = _conv_acc(v_, pv[...], wv_ref, bv_ref, rid)
        gel = 0.5 * accg * (1.0 + lax.erf(accg * INV_SQRT2))
        a_ref[...] = (gel * accv).astype(a_ref.dtype)
        pg[...] = g_[T - 8:T, :]
        pv[...] = v_[T - 8:T, :]

    tok = pl.BlockSpec((T, tc), lambda j, t: (t, j))
    w3 = pl.BlockSpec((3, tc), lambda j, t: (0, j))
    b1 = pl.BlockSpec((1, tc), lambda j, t: (0, j))
    return pl.pallas_call(
        body, name=name, grid=(F // tc, S // T),
        in_specs=[tok, tok, w3, w3, b1, b1], out_specs=tok,
        out_shape=jax.ShapeDtypeStruct((S, F), BF16),
        scratch_shapes=[pltpu.VMEM((8, tc), F32), pltpu.VMEM((8, tc), F32)],
        compiler_params=_cparams(("parallel", "arbitrary")),
    )(ug, uv, wg, wv, bg, bv)


def _convglu_bwd(ug, uv, wg, wv, bg, bv, da, *, name, T=512, tc=256):
    S, F = ug.shape
    T = min(T, S)
    nT = S // T
    halo_blocks = T // 8

    def up_shift(d, nx, rid):
        p1 = jnp.where(rid == T - 1, nx[0:1, :], pltpu.roll(d, T - 1, 0))
        p2 = jnp.where(rid == T - 1, nx[1:2, :], jnp.where(rid == T - 2, nx[0:1, :], pltpu.roll(d, T - 2, 0)))
        return p1, p2

    def body(ug_ref, uv_ref, hg_ref, hv_ref, wg_ref, wv_ref, bg_ref, bv_ref, da_ref,
             dug_ref, duv_ref, dwg_ref, dwv_ref, dbg_ref, dbv_ref, ng, nv):
        @pl.when(pl.program_id(1) == 0)
        def _():
            ng[...] = jnp.zeros_like(ng)
            nv[...] = jnp.zeros_like(nv)
            for r in (dwg_ref, dwv_ref, dbg_ref, dbv_ref):
                r[...] = jnp.zeros_like(r)

        first_block = pl.program_id(1) == nT - 1
        rid = lax.broadcasted_iota(jnp.int32, (T, tc), 0)
        g_, v_ = ug_ref[...], uv_ref[...]
        pg = jnp.where(first_block, 0.0, hg_ref[...])
        pv = jnp.where(first_block, 0.0, hv_ref[...])
        accg, g1, g2 = _conv_acc(g_, pg, wg_ref, bg_ref, rid)
        accv, v1, v2 = _conv_acc(v_, pv, wv_ref, bv_ref, rid)
        cdf = 0.5 * (1.0 + lax.erf(accg * INV_SQRT2))
        pdf = INV_SQRT2PI * jnp.exp(-0.5 * accg * accg)
        da_ = da_ref[...].astype(F32)
        dgate = da_ * accv * (cdf + accg * pdf)
        dval = da_ * (accg * cdf)
        dbg_ref[...] += jnp.sum(dgate, axis=0, keepdims=True)
        dbv_ref[...] += jnp.sum(dval, axis=0, keepdims=True)
        for j, (sg_, sv_) in enumerate(((g2, v2), (g1, v1), (g_, v_))):
            dwg_ref[j:j + 1, :] += jnp.sum(dgate * sg_, axis=0, keepdims=True)
            dwv_ref[j:j + 1, :] += jnp.sum(dval * sv_, axis=0, keepdims=True)
        for d, w_ref, nx, out_ref in ((dgate, wg_ref, ng, dug_ref), (dval, wv_ref, nv, duv_ref)):
            p1, p2 = up_shift(d, nx[...], rid)
            out_ref[...] = (w_ref[2:3, :] * d + w_ref[1:2, :] * p1 + w_ref[0:1, :] * p2).astype(out_ref.dtype)
            nx[...] = d[0:8, :]

    tok = pl.BlockSpec((T, tc), lambda j, t: (nT - 1 - t, j))
    halo = pl.BlockSpec((8, tc), lambda j, t: (jnp.maximum((nT - 1 - t) * halo_blocks - 1, 0), j))
    w3 = pl.BlockSpec((3, tc), lambda j, t: (0, j))
    b1 = pl.BlockSpec((1, tc), lambda j, t: (0, j))
    big = jax.ShapeDtypeStruct((S, F), BF16)
    return pl.pallas_call(
        body, name=name, grid=(F // tc, nT),
        in_specs=[tok, tok, halo, halo, w3, w3, b1, b1, tok], out_specs=[tok, tok, w3, w3, b1, b1],
        out_shape=[big, big, jax.ShapeDtypeStruct((3, F), F32), jax.ShapeDtypeStruct((3, F), F32),
                   jax.ShapeDtypeStruct((1, F), F32), jax.ShapeDtypeStruct((1, F), F32)],
        scratch_shapes=[pltpu.VMEM((8, tc), F32), pltpu.VMEM((8, tc), F32)],
        compiler_params=_cparams(("parallel", "arbitrary")),
    )(ug, uv, ug, uv, wg, wv, bg, bv, da)


FF_LO = 7168
IN_SHARD = 1154
FF_DEV, FF_OFF = FF_LO // IN_SHARD, FF_LO % IN_SHARD


def _col_blocks(a, width):
    return jnp.stack([a[:, d * width:(d + 1) * width] for d in range(N_DEV)])


def _w_in_blocks(d_wm, d_wff):
    def block(d):
        lo = d * IN_SHARD
        if d < FF_DEV:
            return d_wm[:, lo:lo + IN_SHARD]
        if d > FF_DEV:
            return d_wm[:, lo - FOX_HEADS:lo - FOX_HEADS + IN_SHARD]
        return jnp.concatenate([d_wm[:, lo:FF_LO], d_wff[:, :FOX_HEADS], d_wm[:, FF_LO:lo + IN_SHARD - FOX_HEADS]], axis=1)

    return jnp.stack([block(d) for d in range(N_DEV)])


def _late_weights(g_a, g_b, g_o, g_up, g_cw, g_d):
    wup = jnp.concatenate([g_up[d] for d in range(N_DEV)], axis=1)
    cw = jnp.concatenate([g_cw[d] for d in range(N_DEV)], axis=1)
    return dict(wa=g_a.reshape(D_MODEL, D_MODEL), wb=g_b.reshape(D_MODEL, D_MODEL), wo=g_o.reshape(D_MODEL, D_MODEL),
                wug=wup[:, :D_FF], wuv=wup[:, D_FF:], cwg=cw[:, :D_FF], cwv=cw[:, D_FF:], wd=g_d.reshape(D_FF, D_MODEL))


def _early_grad_blocks(d_wa, d_wb, d_wo, d_wug, d_wuv, d_wd):
    up = jnp.stack([d_wug[:, d * 704:(d + 1) * 704] for d in range(4)]
                   + [d_wuv[:, d * 704:(d + 1) * 704] for d in range(4)])
    return [d_wa.reshape(N_DEV, 128, D_MODEL), d_wb.reshape(N_DEV, 128, D_MODEL), d_wo.reshape(N_DEV, 128, D_MODEL),
            up, d_wd.reshape(N_DEV, 352, D_MODEL)]


def _local_step(x, tgt, w, p, late=None, exchange=False):
    S = x.shape[0]
    mm = _matmul
    n1 = _rms_fwd(x, p["norm_mix"], name="rms1_fwd")
    if late is None:
        proj = mm(n1, w["wm"], "nn", name="proj_main")
    else:
        proj, gathered = mm(n1, w["wm"], "nn", comm=late, name="proj_main")
        w = {**w, **_late_weights(*gathered)}
    ff = mm(n1, w["wff"], "nn", name="proj_ff")
    lb = _lb_fwd(p["hg_lb_logits"], name="lb_fwd")
    gnorm = p["hg_norm"].reshape(1, HG_DV)
    o_hg, oa, states = _hgrn_fwd(proj, lb, gnorm, name="hgrn_fwd")
    bias = jnp.pad(p["fox_f_bias"].reshape(1, FOX_HEADS), ((0, 0), (0, 128 - FOX_HEADS)))
    c = _fox_gate_fwd(ff, bias, name="fox_gate_fwd")
    qa, ka, va, fox_stats = _fox_prep(proj, c, name="fox_prep")
    bounds = fox_stats[:, :, 0, :N_STAT].reshape(-1)
    ob, qb, lse_stats = _fox_fwd(qa, ka, va, bounds, name="fox_fwd")
    lse_min = lse_stats[:, :, 0, 0].reshape(-1)
    pa = mm(oa, w["wa"], "nn", name="branch_a")
    pb = mm(ob, w["wb"], "nn", name="branch_b")
    merged = _merge_fwd(proj, pa, pb, name="merge_fwd")
    h1 = mm(merged, w["wo"], "nn", addend=x, name="mix_out")
    n2 = _rms_fwd(h1, p["norm_ffn"], name="rms2_fwd")
    ug = mm(n2, w["wug"], "nn", name="up_gate")
    uv = mm(n2, w["wuv"], "nn", name="up_val")
    a = _convglu_fwd(ug, uv, w["cwg"], w["cwv"], p["cbg"], p["cbv"], name="convglu_fwd")
    h2 = mm(a, w["wd"], "nn", addend=h1, name="ffn_down")
    loss, dh2, d_norm_final = _loss_head(h2, p["norm_final"], tgt, name="loss_head")
    da = mm(dh2, w["wd"], "nt", out_dtype=BF16, name="d_act")
    d_wd = mm(a, dh2, "tn", out_dtype=BF16, name="dw_down")
    dug, duv, d_cwg, d_cwv, d_cbg, d_cbv = _convglu_bwd(
        ug, uv, w["cwg"], w["cwv"], p["cbg"], p["cbv"], da, name="convglu_bwd")
    dn2 = mm(dug, w["wug"], "nt", name="dn2_gate")
    dn2 = mm(duv, w["wuv"], "nt", addend=dn2, name="dn2_val")
    d_wug = mm(n2, dug, "tn", out_dtype=BF16, name="dw_up_gate")
    d_wuv = mm(n2, duv, "tn", out_dtype=BF16, name="dw_up_val")
    dh1, d_norm_ffn = _rms_bwd(h1, p["norm_ffn"], dn2, dh2, name="rms2_bwd")
    dmerged = mm(dh1, w["wo"], "nt", name="d_merged")
    d_wo = mm(merged, dh1, "tn", out_dtype=BF16, name="dw_out")
    dpa, dpb, dga, dgb = _merge_bwd(proj, pa, pb, dmerged, name="merge_bwd")
    doa = mm(dpa, w["wa"], "nt", name="d_oa")
    dob = mm(dpb, w["wb"], "nt", out_dtype=BF16, name="d_ob")
    d_wa = mm(oa, dpa, "tn", out_dtype=BF16, name="dw_branch_a")
    d_wb = mm(ob, dpb, "tn", out_dtype=BF16, name="dw_branch_b")
    dhq, dhf, dhi, dhg, dlb, dgn8 = _hgrn_bwd(proj, lb, gnorm, o_hg, states, doa, name="hgrn_bwd")
    d_logits = _lb_bwd(p["hg_lb_logits"], dlb, name="lb_bwd")
    dob_hm = _fox_bwd_prep(ob, dob, name="fox_bwd_prep")
    early_parts = None
    if exchange:
        comm = _ExchangeComm(_early_grad_blocks(d_wa, d_wb, d_wo, d_wug, d_wuv, d_wd))
        dq, dob2, early_parts = _fox_bwd_dq(qb, ka, va, dob_hm, bounds, lse_min, comm=comm, name="fox_bwd_dq")
    else:
        dq, dob2 = _fox_bwd_dq(qb, ka, va, dob_hm, bounds, lse_min, name="fox_bwd_dq")
    dk, dv, dcs = _fox_bwd_dkv(qb, ka, va, dob2, bounds, lse_min, name="fox_bwd_dkv")
    dcs_tok = jnp.pad(dcs.transpose(1, 0, 2).reshape(S, FOX_HEADS), ((0, 0), (0, 128 - FOX_HEADS)))
    dff, dbias = _fox_gate_bwd(ff, bias, dcs_tok, name="fox_gate_bwd")
    dproj = jnp.concatenate([dhq, dhf, dhi, dhg, dq, dk, dv, dga, dgb], axis=1)
    d_wm = mm(n1, dproj, "tn", out_dtype=BF16, name="dw_in_main")
    d_wff = mm(n1, dff, "tn", out_dtype=BF16, name="dw_in_ff")
    dn1 = mm(dff, w["wff"], "nt", name="dn1_ff")
    late_parts = None
    if exchange:
        d_cw = jnp.concatenate([d_cwg, d_cwv], axis=1)
        comm = _ExchangeComm([_w_in_blocks(d_wm, d_wff), _col_blocks(d_cw, 704)])
        dn1, late_parts = mm(dproj, w["wm"], "nt", addend=dn1, comm=comm, name="dn1_main")
    else:
        dn1 = mm(dproj, w["wm"], "nt", addend=dn1, name="dn1_main")
    dx, d_norm_mix = _rms_bwd(x, p["norm_mix"], dn1, dh1, name="rms1_bwd")
    grads = dict(
        wm=d_wm, wff=d_wff, wa=d_wa, wb=d_wb, wo=d_wo, wug=d_wug, wuv=d_wuv, cwg=d_cwg, cwv=d_cwv, wd=d_wd,
        norm_mix=d_norm_mix.reshape(-1), fox_f_bias=dbias[0, :FOX_HEADS], hg_lb_logits=d_logits,
        hg_norm=jnp.sum(dgn8, axis=0).reshape(-1), norm_ffn=d_norm_ffn.reshape(-1), cbg=d_cbg, cbv=d_cbv,
        norm_final=d_norm_final.reshape(-1), early_parts=early_parts, late_parts=late_parts)
    return loss, dx, grads


SMALL = [("norm_mix", (1, D_MODEL)), ("fox_f_bias", (1, FOX_HEADS)), ("hg_lb_logits", (2, HG_HEADS * HG_DK)),
         ("hg_norm", (1, HG_DV)), ("norm_ffn", (1, D_MODEL)), ("conv_b", (1, 2 * D_FF)), ("norm_final", (D_MODEL,))]
SMALL_ROWS = 88
SHARDED = [("w_in", (D_MODEL, 1154), 256), ("w_branch_a", (128, D_MODEL), 128), ("w_branch_b", (128, D_MODEL), 128),
           ("w_out", (128, D_MODEL), 128), ("w_up", (D_MODEL, 704), 256), ("conv_w", (3, 704), 3),
           ("w_down", (352, D_MODEL), 352)]
NAMES = ["norm_mix", "w_in", "fox_f_bias", "hg_lb_logits", "hg_norm", "w_branch_a", "w_branch_b", "w_out",
         "norm_ffn", "w_up", "conv_w", "conv_b", "w_down", "norm_final"]


def _size(shape):
    n = 1
    for s in shape:
        n *= s
    return n


def _adamw(parts, w, m, v, *, name, T):
    R, C = w.shape
    c1 = 1.0 / (1.0 - ADAM_B1 ** ADAM_STEP)
    c2 = 1.0 / (1.0 - ADAM_B2 ** ADAM_STEP)

    def body(p_ref, w_ref, m_ref, v_ref, g_ref, d_ref, nm_ref, nv_ref):
        g = p_ref[0].astype(F32)
        for s in range(1, N_DEV):
            g = g + p_ref[s].astype(F32)
        g_ref[...] = g
        nm = ADAM_B1 * m_ref[...] + (1.0 - ADAM_B1) * g
        nv = ADAM_B2 * v_ref[...] + (1.0 - ADAM_B2) * (g * g)
        nm_ref[...] = nm
        nv_ref[...] = nv
        d_ref[...] = -ADAM_LR * ((nm * c1) / (jnp.sqrt(nv * c2) + ADAM_EPS) + ADAM_WD * w_ref[...])

    blk = pl.BlockSpec((T, C), lambda i: (i, 0))
    out = jax.ShapeDtypeStruct((R, C), F32)
    return pl.pallas_call(
        body, name=name, grid=(R // T,),
        in_specs=[pl.BlockSpec((N_DEV, T, C), lambda i: (0, i, 0)), blk, blk, blk],
        out_specs=[blk, blk, blk, blk], out_shape=[out, out, out, out],
        compiler_params=_cparams(("parallel",)),
    )(parts, w, m, v)


def _pack_small(vals):
    flat = jnp.concatenate([vals[n].reshape(-1).astype(F32) for n, _ in SMALL])
    return jnp.pad(flat, (0, SMALL_ROWS * 128 - flat.shape[0])).reshape(SMALL_ROWS, 128)


def _unpack_small(buf):
    flat, out, off = buf.reshape(-1), {}, 0
    for n, shape in SMALL:
        out[n] = flat[off:off + _size(shape)].reshape(shape)
        off += _size(shape)
    return out


def kernel(x, norm_mix, w_in, fox_f_bias, hg_lb_logits, hg_norm, w_branch_a, w_branch_b, w_out, norm_ffn, w_up, conv_w, conv_b, w_down, norm_final, loss_target, m_norm_mix, m_w_in, m_fox_f_bias, m_hg_lb_logits, m_hg_norm, m_w_branch_a, m_w_branch_b, m_w_out, m_norm_ffn, m_w_up, m_conv_w, m_conv_b, m_w_down, m_norm_final, v_norm_mix, v_w_in, v_fox_f_bias, v_hg_lb_logits, v_hg_norm, v_w_branch_a, v_w_branch_b, v_w_out, v_norm_ffn, v_w_up, v_conv_w, v_conv_b, v_w_down, v_norm_final):
    wv = dict(norm_mix=norm_mix, w_in=w_in, fox_f_bias=fox_f_bias, hg_lb_logits=hg_lb_logits, hg_norm=hg_norm,
              w_branch_a=w_branch_a, w_branch_b=w_branch_b, w_out=w_out, norm_ffn=norm_ffn, w_up=w_up, conv_w=conv_w,
              conv_b=conv_b, w_down=w_down, norm_final=norm_final)
    mv = dict(norm_mix=m_norm_mix, w_in=m_w_in, fox_f_bias=m_fox_f_bias, hg_lb_logits=m_hg_lb_logits, hg_norm=m_hg_norm,
              w_branch_a=m_w_branch_a, w_branch_b=m_w_branch_b, w_out=m_w_out, norm_ffn=m_norm_ffn, w_up=m_w_up,
              conv_w=m_conv_w, conv_b=m_conv_b, w_down=m_w_down, norm_final=m_norm_final)
    vv = dict(norm_mix=v_norm_mix, w_in=v_w_in, fox_f_bias=v_fox_f_bias, hg_lb_logits=v_hg_lb_logits, hg_norm=v_hg_norm,
              w_branch_a=v_w_branch_a, w_branch_b=v_w_branch_b, w_out=v_w_out, norm_ffn=v_norm_ffn, w_up=v_w_up,
              conv_w=v_conv_w, conv_b=v_conv_b, w_down=v_w_down, norm_final=v_norm_final)

    (g_in,) = _comm_call(_GatherComm([w_in[0].astype(BF16)]), name="gather_w_in")
    w = dict(wm=jnp.concatenate([g_in[d] for d in range(FF_DEV)]
                                + [g_in[FF_DEV][:, :FF_OFF], g_in[FF_DEV][:, FF_OFF + FOX_HEADS:]]
                                + [g_in[d] for d in range(FF_DEV + 1, N_DEV)], axis=1),
             wff=jnp.pad(g_in[FF_DEV][:, FF_OFF:FF_OFF + FOX_HEADS], ((0, 0), (0, 128 - FOX_HEADS))))
    late = _GatherComm([w_branch_a[0].astype(BF16), w_branch_b[0].astype(BF16), w_out[0].astype(BF16),
                        w_up[0].astype(BF16), conv_w[0], w_down[0].astype(BF16)])
    p = dict(norm_mix=norm_mix[0], fox_f_bias=fox_f_bias[0], hg_lb_logits=hg_lb_logits, hg_norm=hg_norm[0],
             norm_ffn=norm_ffn[0], cbg=conv_b[:, :D_FF], cbv=conv_b[:, D_FF:], norm_final=norm_final)
    loss, dx, grads = _local_step(x[0], loss_target[0], w, p, late=late, exchange=True)
    loss = lax.psum(loss[0, 0], ("x", "y", "c"))

    small = _pack_small(dict(
        norm_mix=grads["norm_mix"], fox_f_bias=grads["fox_f_bias"], hg_lb_logits=grads["hg_lb_logits"],
        hg_norm=grads["hg_norm"], norm_ffn=grads["norm_ffn"], conv_b=jnp.concatenate([grads["cbg"], grads["cbv"]], axis=1),
        norm_final=grads["norm_final"]))
    (small_parts,) = _comm_call(_ExchangeComm([jnp.broadcast_to(small[None], (N_DEV, SMALL_ROWS, 128))]),
                                name="exchange_small")
    ea, eb, eo, eup, ed = grads["early_parts"]
    p_in, p_cw = grads["late_parts"]
    parts = [p_in, ea, eb, eo, eup, p_cw, ed, small_parts]
    res = {}
    for (n, shape, tile), part in zip(SHARDED, parts):
        outs = _adamw(part, wv[n].reshape(shape), mv[n].reshape(shape), vv[n].reshape(shape), name="adamw_" + n, T=tile)
        res[n] = [o.reshape(wv[n].shape) for o in outs]
    outs = _adamw(parts[-1], _pack_small(wv), _pack_small(mv), _pack_small(vv), name="adamw_small", T=SMALL_ROWS)
    small = [_unpack_small(o) for o in outs]
    for n, _ in SMALL:
        res[n] = [s[n] for s in small]
    return (loss, dx[None], *[res[n][0] for n in NAMES], *[res[n][1] for n in NAMES],
            *[res[n][2] for n in NAMES], *[res[n][3] for n in NAMES])
```
